```python
import math
import jax, jax.numpy as jnp
from jax import lax
import numpy as np

D_MODEL = 1024
BATCH = 16
SEQ = 4096
DEPTH = 1

EXPAND = 2
D_MIX = EXPAND * D_MODEL
D_SSD = D_MIX // 2
D_MLA = D_MIX - D_SSD
SSD_HEAD_DIM = 64
SSD_HEADS = D_SSD // SSD_HEAD_DIM
SSD_GROUPS = 2
SSD_HPG = SSD_HEADS // SSD_GROUPS
SSD_STATE = 128
CONV_WIDTH = 4
CHUNK = 128
MLA_HEADS = 8
QK_NOPE = 64
QK_ROPE = 32
QK_DIM = QK_NOPE + QK_ROPE
V_HEAD = D_MLA // MLA_HEADS
Q_LORA = 384
KV_LORA = 256
ROPE_THETA = 10000.0
Q_BLOCK = 128
D_FF = 2816
N_MOD = 9
EPS = 1e-6
D_CONV = D_SSD + 2 * SSD_GROUPS * SSD_STATE
IN_WIDTHS = (D_SSD, D_SSD, SSD_GROUPS * SSD_STATE, SSD_GROUPS * SSD_STATE,
             SSD_HEADS, Q_LORA, KV_LORA, QK_ROPE)
D_IN_PROJ = sum(IN_WIDTHS)
IN_SPLITS = tuple(int(v) for v in np.cumsum(IN_WIDTHS)[:-1])

kernel_name = "hymba_ssd_mla_macaron_adaln"


def rmsnorm(x, w):
    xf = x.astype(jnp.float32)
    y = xf * lax.rsqrt(jnp.mean(xf * xf, axis=-1, keepdims=True) + EPS)
    return (y * w.astype(jnp.float32)).astype(x.dtype)


def modulate(h, shift, scale):
    return h * (1.0 + scale[:, None, :]) + shift[:, None, :]


def swiglu(h, w_gate, w_up, w_down):
    return (jax.nn.silu(h @ w_gate) * (h @ w_up)) @ w_down


def apply_rope(u, cos, sin):
    u1, u2 = jnp.split(u, 2, axis=-1)
    return jnp.concatenate([u1 * cos - u2 * sin, u2 * cos + u1 * sin], axis=-1)


def causal_depthwise_conv(u, w, b):
    out = lax.conv_general_dilated(
        u, w[:, None, :].astype(u.dtype), window_strides=(1,),
        padding=[(CONV_WIDTH - 1, 0)],
        dimension_numbers=('NWC', 'WIO', 'NWC'),
        feature_group_count=u.shape[-1])
    return out + b.astype(u.dtype)


def ssd_chunked(xh, dt, A, Bm, Cm):
    b, S = xh.shape[0], xh.shape[1]
    nc = S // CHUNK
    dtype = xh.dtype
    xdt = (xh * dt[..., None].astype(dtype)).reshape(b, nc, CHUNK, SSD_GROUPS, SSD_HPG, SSD_HEAD_DIM)
    Bc = Bm.reshape(b, nc, CHUNK, SSD_GROUPS, SSD_STATE)
    Cc = Cm.reshape(b, nc, CHUNK, SSD_GROUPS, SSD_STATE)
    a_cum = jnp.cumsum((dt * A).reshape(b, nc, CHUNK, SSD_GROUPS, SSD_HPG), axis=2)
    seg = a_cum[:, :, :, None] - a_cum[:, :, None, :]
    causal = jnp.tril(jnp.ones((CHUNK, CHUNK), dtype=bool))[None, None, :, :, None, None]
    Lmat = jnp.exp(jnp.where(causal, seg, -jnp.inf)).astype(dtype)
    cb = jnp.einsum('bclgn,bcsgn->bclsg', Cc, Bc)
    y_diag = jnp.einsum('bclsg,bclsgr,bcsgrp->bclgrp', cb, Lmat, xdt)
    decay_states = jnp.exp(a_cum[:, :, -1:] - a_cum).astype(dtype)
    states = jnp.einsum('bclgn,bclgr,bclgrp->bcgrpn', Bc, decay_states, xdt)
    chunk_decay = jnp.exp(a_cum[:, :, -1]).astype(dtype)

    def step(h, inp):
        s_c, d_c = inp
        return d_c[..., None, None] * h + s_c, h

    h0 = jnp.zeros_like(states[:, 0])
    _, prev = lax.scan(step, h0, (jnp.moveaxis(states, 1, 0), jnp.moveaxis(chunk_decay, 1, 0)))
    prev = jnp.moveaxis(prev, 0, 1)
    y_off = jnp.einsum('bclgn,bcgrpn,bclgr->bclgrp', Cc, prev, jnp.exp(a_cum).astype(dtype))
    return (y_diag + y_off).reshape(b, S, SSD_GROUPS, SSD_HPG, SSD_HEAD_DIM)


def mla_causal_attention(q_nope, q_rope, k_nope, k_rope, v):
    b, S = q_nope.shape[0], q_nope.shape[1]
    nblk = S // Q_BLOCK
    scale = 1.0 / math.sqrt(QK_DIM)
    qn_b = q_nope.reshape(b, nblk, Q_BLOCK, MLA_HEADS, QK_NOPE).swapaxes(0, 1)
    qr_b = q_rope.reshape(b, nblk, Q_BLOCK, MLA_HEADS, QK_ROPE).swapaxes(0, 1)
    starts = jnp.arange(nblk, dtype=jnp.int32) * Q_BLOCK
    k_idx = jnp.arange(S, dtype=jnp.int32)

    def one_block(args):
        qn, qr, start = args
        s = (jnp.einsum('bqhd,bkhd->bhqk', qn, k_nope)
             + jnp.einsum('bqhr,bkr->bhqk', qr, k_rope)).astype(jnp.float32) * scale
        q_idx = start + jnp.arange(Q_BLOCK, dtype=jnp.int32)
        mask = k_idx[None, :] <= q_idx[:, None]
        s = jnp.where(mask[None, None], s, -jnp.inf)
        p = jax.nn.softmax(s, axis=-1).astype(v.dtype)
        return jnp.einsum('bhqk,bkhv->bqhv', p, v)

    out = lax.map(one_block, (qn_b, qr_b, starts))
    return out.swapaxes(0, 1).reshape(b, S, MLA_HEADS * V_HEAD)


def hybrid_mixer(h, positions, w_in, conv_w, conv_b, dt_bias, a_log, d_skip, ssd_norm_w,
                 q_norm_w, w_uq, kv_norm_w, w_ukv, mla_norm_w, w_out):
    b, S = h.shape[0], h.shape[1]
    proj = h @ w_in
    z, xs, Bm, Cm, dt_raw, cq, ckv, k_rope_raw = jnp.split(proj, IN_SPLITS, axis=-1)

    xBC = jax.nn.silu(causal_depthwise_conv(jnp.concatenate([xs, Bm, Cm], axis=-1), conv_w, conv_b))
    xs, Bm, Cm = jnp.split(xBC, [D_SSD, D_SSD + SSD_GROUPS * SSD_STATE], axis=-1)
    dt = jax.nn.softplus(dt_raw.astype(jnp.float32) + dt_bias.astype(jnp.float32))
    A = -jnp.exp(a_log.astype(jnp.float32))
    xh = xs.reshape(b, S, SSD_GROUPS, SSD_HPG, SSD_HEAD_DIM)
    y = ssd_chunked(xh, dt.reshape(b, S, SSD_GROUPS, SSD_HPG), A.reshape(SSD_GROUPS, SSD_HPG),
                    Bm.reshape(b, S, SSD_GROUPS, SSD_STATE), Cm.reshape(b, S, SSD_GROUPS, SSD_STATE))
    y = y + d_skip.reshape(SSD_GROUPS, SSD_HPG)[:, :, None].astype(y.dtype) * xh
    yg = (y.reshape(b, S, D_SSD) * jax.nn.silu(z)).reshape(b, S, SSD_GROUPS, D_SSD // SSD_GROUPS)
    y_ssd = rmsnorm(yg, ssd_norm_w.reshape(SSD_GROUPS, D_SSD // SSD_GROUPS)).reshape(b, S, D_SSD)

    q = (rmsnorm(cq, q_norm_w) @ w_uq).reshape(b, S, MLA_HEADS, QK_DIM)
    q_nope, q_rope = jnp.split(q, [QK_NOPE], axis=-1)
    kv = (rmsnorm(ckv, kv_norm_w) @ w_ukv).reshape(b, S, MLA_HEADS, QK_NOPE + V_HEAD)
    k_nope, v = jnp.split(kv, [QK_NOPE], axis=-1)
    inv_freq = ROPE_THETA ** (-jnp.arange(0, QK_ROPE, 2, dtype=jnp.float32) / QK_ROPE)
    ang = positions[..., None].astype(jnp.float32) * inv_freq
    cos, sin = jnp.cos(ang).astype(h.dtype), jnp.sin(ang).astype(h.dtype)
    q_rope = apply_rope(q_rope, cos[:, :, None], sin[:, :, None])
    k_rope = apply_rope(k_rope_raw, cos, sin)
    attn = mla_causal_attention(q_nope, q_rope, k_nope, k_rope, v)
    y_mla = rmsnorm(attn, mla_norm_w)

    return jnp.concatenate([y_ssd, y_mla], axis=-1) @ w_out


def _fwd_setup_inputs(seed: int = 0) -> dict:
    key = jax.random.key(seed)
    ks = iter(jax.random.split(key, 40))

    def dense(shape, fan_in):
        return jax.random.normal(next(ks), shape, jnp.float32) * fan_in ** -0.5

    def gain(shape):
        return 1.0 + 0.05 * jax.random.normal(next(ks), shape, jnp.float32)

    def small(shape, s=0.02):
        return s * jax.random.normal(next(ks), shape, jnp.float32)

    L = DEPTH
    x = jax.random.normal(next(ks), (BATCH, SEQ, D_MODEL), jnp.float32)
    c = jax.random.normal(next(ks), (BATCH, D_MODEL), jnp.float32)
    offsets = jax.random.randint(next(ks), (BATCH, 1), 0, 1024, dtype=jnp.int32)
    positions = offsets + jnp.arange(SEQ, dtype=jnp.int32)[None, :]
    dt0 = jnp.exp(jax.random.uniform(next(ks), (L, SSD_HEADS), jnp.float32,
                                     math.log(1e-3), math.log(1e-1)))
    dt_bias = dt0 + jnp.log(-jnp.expm1(-dt0))
    a_log = jnp.log(jax.random.uniform(next(ks), (L, SSD_HEADS), jnp.float32, 1.0, 16.0))
    return {
        "x": x,
        "c": c,
        "positions": positions,
        "w_ada": dense((L, D_MODEL, N_MOD * D_MODEL), D_MODEL),
        "b_ada": small((L, N_MOD * D_MODEL)),
        "norm_ffn1": gain((L, D_MODEL)),
        "ffn1_w_gate": dense((L, D_MODEL, D_FF), D_MODEL),
        "ffn1_w_up": dense((L, D_MODEL, D_FF), D_MODEL),
        "ffn1_w_down": dense((L, D_FF, D_MODEL), D_FF),
        "norm_mix": gain((L, D_MODEL)),
        "w_in": dense((L, D_MODEL, D_IN_PROJ), D_MODEL),
        "conv_w": dense((L, CONV_WIDTH, D_CONV), CONV_WIDTH),
        "conv_b": small((L, D_CONV)),
        "dt_bias": dt_bias,
        "a_log": a_log,
        "d_skip": gain((L, SSD_HEADS)),
        "ssd_norm_w": gain((L, D_SSD)),
        "q_norm_w": gain((L, Q_LORA)),
        "w_uq": dense((L, Q_LORA, MLA_HEADS * QK_DIM), Q_LORA),
        "kv_norm_w": gain((L, KV_LORA)),
        "w_ukv": dense((L, KV_LORA, MLA_HEADS * (QK_NOPE + V_HEAD)), KV_LORA),
        "mla_norm_w": gain((L, D_MLA)),
        "w_out": dense((L, D_MIX, D_MODEL), D_MIX),
        "norm_ffn2": gain((L, D_MODEL)),
        "ffn2_w_gate": dense((L, D_MODEL, D_FF), D_MODEL),
        "ffn2_w_up": dense((L, D_MODEL, D_FF), D_MODEL),
        "ffn2_w_down": dense((L, D_FF, D_MODEL), D_FF),
        "norm_final": gain((D_MODEL,)),
    }


def _fwd_reference(x, c, positions, w_ada, b_ada, norm_ffn1, ffn1_w_gate, ffn1_w_up, ffn1_w_down,
              norm_mix, w_in, conv_w, conv_b, dt_bias, a_log, d_skip, ssd_norm_w,
              q_norm_w, w_uq, kv_norm_w, w_ukv, mla_norm_w, w_out,
              norm_ffn2, ffn2_w_gate, ffn2_w_up, ffn2_w_down, norm_final):
    c_act = jax.nn.silu(c)
    for l in range(DEPTH):
        mod = c_act @ w_ada[l] + b_ada[l]
        (sh1, sc1, g1, sh2, sc2, g2, sh3, sc3, g3) = jnp.split(mod, N_MOD, axis=-1)
        h = modulate(rmsnorm(x, norm_ffn1[l]), sh1, sc1)
        x = x + 0.5 * g1[:, None, :] * swiglu(h, ffn1_w_gate[l], ffn1_w_up[l], ffn1_w_down[l])
        h = modulate(rmsnorm(x, norm_mix[l]), sh2, sc2)
        x = x + g2[:, None, :] * hybrid_mixer(
            h, positions, w_in[l], conv_w[l], conv_b[l], dt_bias[l], a_log[l], d_skip[l],
            ssd_norm_w[l], q_norm_w[l], w_uq[l], kv_norm_w[l], w_ukv[l], mla_norm_w[l], w_out[l])
        h = modulate(rmsnorm(x, norm_ffn2[l]), sh3, sc3)
        x = x + 0.5 * g3[:, None, :] * swiglu(h, ffn2_w_gate[l], ffn2_w_up[l], ffn2_w_down[l])
    return rmsnorm(x, norm_final)


import jax as _jax
import jax.numpy as _jnp

TWIN_FORMAT = 'train_step'
FWD_PARAMS = ['x', 'c', 'positions', 'w_ada', 'b_ada', 'norm_ffn1', 'ffn1_w_gate', 'ffn1_w_up', 'ffn1_w_down', 'norm_mix', 'w_in', 'conv_w', 'conv_b', 'dt_bias', 'a_log', 'd_skip', 'ssd_norm_w', 'q_norm_w', 'w_uq', 'kv_norm_w', 'w_ukv', 'mla_norm_w', 'w_out', 'norm_ffn2', 'ffn2_w_gate', 'ffn2_w_up', 'ffn2_w_down', 'norm_final']
TWIN_WEIGHTS = ['w_ada', 'b_ada', 'norm_ffn1', 'ffn1_w_gate', 'ffn1_w_up', 'ffn1_w_down', 'norm_mix', 'w_in', 'conv_w', 'conv_b', 'dt_bias', 'a_log', 'd_skip', 'ssd_norm_w', 'q_norm_w', 'w_uq', 'kv_norm_w', 'w_ukv', 'mla_norm_w', 'w_out', 'norm_ffn2', 'ffn2_w_gate', 'ffn2_w_up', 'ffn2_w_down', 'norm_final']
TWIN_DIFF_INPUT = 'x'
TWIN_INPUTS = ['x', 'c', 'positions', 'w_ada', 'b_ada', 'norm_ffn1', 'ffn1_w_gate', 'ffn1_w_up', 'ffn1_w_down', 'norm_mix', 'w_in', 'conv_w', 'conv_b', 'dt_bias', 'a_log', 'd_skip', 'ssd_norm_w', 'q_norm_w', 'w_uq', 'kv_norm_w', 'w_ukv', 'mla_norm_w', 'w_out', 'norm_ffn2', 'ffn2_w_gate', 'ffn2_w_up', 'ffn2_w_down', 'norm_final', 'loss_target', 'm_w_ada', 'm_b_ada', 'm_norm_ffn1', 'm_ffn1_w_gate', 'm_ffn1_w_up', 'm_ffn1_w_down', 'm_norm_mix', 'm_w_in', 'm_conv_w', 'm_conv_b', 'm_dt_bias', 'm_a_log', 'm_d_skip', 'm_ssd_norm_w', 'm_q_norm_w', 'm_w_uq', 'm_kv_norm_w', 'm_w_ukv', 'm_mla_norm_w', 'm_w_out', 'm_norm_ffn2', 'm_ffn2_w_gate', 'm_ffn2_w_up', 'm_ffn2_w_down', 'm_norm_final', 'v_w_ada', 'v_b_ada', 'v_norm_ffn1', 'v_ffn1_w_gate', 'v_ffn1_w_up', 'v_ffn1_w_down', 'v_norm_mix', 'v_w_in', 'v_conv_w', 'v_conv_b', 'v_dt_bias', 'v_a_log', 'v_d_skip', 'v_ssd_norm_w', 'v_q_norm_w', 'v_w_uq', 'v_kv_norm_w', 'v_w_ukv', 'v_mla_norm_w', 'v_w_out', 'v_norm_ffn2', 'v_ffn2_w_gate', 'v_ffn2_w_up', 'v_ffn2_w_down', 'v_norm_final']
TWIN_OUTPUTS = ['loss', 'grad_x', 'grad_w_ada', 'grad_b_ada', 'grad_norm_ffn1', 'grad_ffn1_w_gate', 'grad_ffn1_w_up', 'grad_ffn1_w_down', 'grad_norm_mix', 'grad_w_in', 'grad_conv_w', 'grad_conv_b', 'grad_dt_bias', 'grad_a_log', 'grad_d_skip', 'grad_ssd_norm_w', 'grad_q_norm_w', 'grad_w_uq', 'grad_kv_norm_w', 'grad_w_ukv', 'grad_mla_norm_w', 'grad_w_out', 'grad_norm_ffn2', 'grad_ffn2_w_gate', 'grad_ffn2_w_up', 'grad_ffn2_w_down', 'grad_norm_final', 'delta_w_ada', 'delta_b_ada', 'delta_norm_ffn1', 'delta_ffn1_w_gate', 'delta_ffn1_w_up', 'delta_ffn1_w_down', 'delta_norm_mix', 'delta_w_in', 'delta_conv_w', 'delta_conv_b', 'delta_dt_bias', 'delta_a_log', 'delta_d_skip', 'delta_ssd_norm_w', 'delta_q_norm_w', 'delta_w_uq', 'delta_kv_norm_w', 'delta_w_ukv', 'delta_mla_norm_w', 'delta_w_out', 'delta_norm_ffn2', 'delta_ffn2_w_gate', 'delta_ffn2_w_up', 'delta_ffn2_w_down', 'delta_norm_final', 'new_m_w_ada', 'new_m_b_ada', 'new_m_norm_ffn1', 'new_m_ffn1_w_gate', 'new_m_ffn1_w_up', 'new_m_ffn1_w_down', 'new_m_norm_mix', 'new_m_w_in', 'new_m_conv_w', 'new_m_conv_b', 'new_m_dt_bias', 'new_m_a_log', 'new_m_d_skip', 'new_m_ssd_norm_w', 'new_m_q_norm_w', 'new_m_w_uq', 'new_m_kv_norm_w', 'new_m_w_ukv', 'new_m_mla_norm_w', 'new_m_w_out', 'new_m_norm_ffn2', 'new_m_ffn2_w_gate', 'new_m_ffn2_w_up', 'new_m_ffn2_w_down', 'new_m_norm_final', 'new_v_w_ada', 'new_v_b_ada', 'new_v_norm_ffn1', 'new_v_ffn1_w_gate', 'new_v_ffn1_w_up', 'new_v_ffn1_w_down', 'new_v_norm_mix', 'new_v_w_in', 'new_v_conv_w', 'new_v_conv_b', 'new_v_dt_bias', 'new_v_a_log', 'new_v_d_skip', 'new_v_ssd_norm_w', 'new_v_q_norm_w', 'new_v_w_uq', 'new_v_kv_norm_w', 'new_v_w_ukv', 'new_v_mla_norm_w', 'new_v_w_out', 'new_v_norm_ffn2', 'new_v_ffn2_w_gate', 'new_v_ffn2_w_up', 'new_v_ffn2_w_down', 'new_v_norm_final']
TWIN_LEAF_KINDS = {'loss': 'loss', 'grad_x': 'grad_x', 'grad_w_ada': 'grad_w', 'grad_b_ada': 'grad_w', 'grad_norm_ffn1': 'grad_w', 'grad_ffn1_w_gate': 'grad_w', 'grad_ffn1_w_up': 'grad_w', 'grad_ffn1_w_down': 'grad_w', 'grad_norm_mix': 'grad_w', 'grad_w_in': 'grad_w', 'grad_conv_w': 'grad_w', 'grad_conv_b': 'grad_w', 'grad_dt_bias': 'grad_w', 'grad_a_log': 'grad_w', 'grad_d_skip': 'grad_w', 'grad_ssd_norm_w': 'grad_w', 'grad_q_norm_w': 'grad_w', 'grad_w_uq': 'grad_w', 'grad_kv_norm_w': 'grad_w', 'grad_w_ukv': 'grad_w', 'grad_mla_norm_w': 'grad_w', 'grad_w_out': 'grad_w', 'grad_norm_ffn2': 'grad_w', 'grad_ffn2_w_gate': 'grad_w', 'grad_ffn2_w_up': 'grad_w', 'grad_ffn2_w_down': 'grad_w', 'grad_norm_final': 'grad_w', 'delta_w_ada': 'delta_w', 'delta_b_ada': 'delta_w', 'delta_norm_ffn1': 'delta_w', 'delta_ffn1_w_gate': 'delta_w', 'delta_ffn1_w_up': 'delta_w', 'delta_ffn1_w_down': 'delta_w', 'delta_norm_mix': 'delta_w', 'delta_w_in': 'delta_w', 'delta_conv_w': 'delta_w', 'delta_conv_b': 'delta_w', 'delta_dt_bias': 'delta_w', 'delta_a_log': 'delta_w', 'delta_d_skip': 'delta_w', 'delta_ssd_norm_w': 'delta_w', 'delta_q_norm_w': 'delta_w', 'delta_w_uq': 'delta_w', 'delta_kv_norm_w': 'delta_w', 'delta_w_ukv': 'delta_w', 'delta_mla_norm_w': 'delta_w', 'delta_w_out': 'delta_w', 'delta_norm_ffn2': 'delta_w', 'delta_ffn2_w_gate': 'delta_w', 'delta_ffn2_w_up': 'delta_w', 'delta_ffn2_w_down': 'delta_w', 'delta_norm_final': 'delta_w', 'new_m_w_ada': 'new_m', 'new_m_b_ada': 'new_m', 'new_m_norm_ffn1': 'new_m', 'new_m_ffn1_w_gate': 'new_m', 'new_m_ffn1_w_up': 'new_m', 'new_m_ffn1_w_down': 'new_m', 'new_m_norm_mix': 'new_m', 'new_m_w_in': 'new_m', 'new_m_conv_w': 'new_m', 'new_m_conv_b': 'new_m', 'new_m_dt_bias': 'new_m', 'new_m_a_log': 'new_m', 'new_m_d_skip': 'new_m', 'new_m_ssd_norm_w': 'new_m', 'new_m_q_norm_w': 'new_m', 'new_m_w_uq': 'new_m', 'new_m_kv_norm_w': 'new_m', 'new_m_w_ukv': 'new_m', 'new_m_mla_norm_w': 'new_m', 'new_m_w_out': 'new_m', 'new_m_norm_ffn2': 'new_m', 'new_m_ffn2_w_gate': 'new_m', 'new_m_ffn2_w_up': 'new_m', 'new_m_ffn2_w_down': 'new_m', 'new_m_norm_final': 'new_m', 'new_v_w_ada': 'new_v', 'new_v_b_ada': 'new_v', 'new_v_norm_ffn1': 'new_v', 'new_v_ffn1_w_gate': 'new_v', 'new_v_ffn1_w_up': 'new_v', 'new_v_ffn1_w_down': 'new_v', 'new_v_norm_mix': 'new_v', 'new_v_w_in': 'new_v', 'new_v_conv_w': 'new_v', 'new_v_conv_b': 'new_v', 'new_v_dt_bias': 'new_v', 'new_v_a_log': 'new_v', 'new_v_d_skip': 'new_v', 'new_v_ssd_norm_w': 'new_v', 'new_v_q_norm_w': 'new_v', 'new_v_w_uq': 'new_v', 'new_v_kv_norm_w': 'new_v', 'new_v_w_ukv': 'new_v', 'new_v_mla_norm_w': 'new_v', 'new_v_w_out': 'new_v', 'new_v_norm_ffn2': 'new_v', 'new_v_ffn2_w_gate': 'new_v', 'new_v_ffn2_w_up': 'new_v', 'new_v_ffn2_w_down': 'new_v', 'new_v_norm_final': 'new_v'}


def _forward(args):
    return _fwd_reference(*[args[k] for k in FWD_PARAMS])


def _output_shape():
    out = _jax.eval_shape(lambda: _forward(_fwd_setup_inputs(0)))
    return out.shape, out.dtype

N_MICROBATCH = 1
ADAM_LR = 0.001
ADAM_B1 = 0.9
ADAM_B2 = 0.999
ADAM_EPS = 1e-08
ADAM_WD = 0.01
ADAM_STEP = 10
PER_EXAMPLE_BATCH_AXIS = {'x': 0, 'c': 0, 'positions': 0, 'loss_target': 0}
SHARED_INPUTS = []
_WEIGHT_DTYPES = {'w_ada': _jnp.float32, 'b_ada': _jnp.float32, 'norm_ffn1': _jnp.float32, 'ffn1_w_gate': _jnp.float32, 'ffn1_w_up': _jnp.float32, 'ffn1_w_down': _jnp.float32, 'norm_mix': _jnp.float32, 'w_in': _jnp.float32, 'conv_w': _jnp.float32, 'conv_b': _jnp.float32, 'dt_bias': _jnp.float32, 'a_log': _jnp.float32, 'd_skip': _jnp.float32, 'ssd_norm_w': _jnp.float32, 'q_norm_w': _jnp.float32, 'w_uq': _jnp.float32, 'kv_norm_w': _jnp.float32, 'w_ukv': _jnp.float32, 'mla_norm_w': _jnp.float32, 'w_out': _jnp.float32, 'norm_ffn2': _jnp.float32, 'ffn2_w_gate': _jnp.float32, 'ffn2_w_up': _jnp.float32, 'ffn2_w_down': _jnp.float32, 'norm_final': _jnp.float32}
MOMENT_SCALE = {'w_ada': 2.581025e-01, 'b_ada': 4.458916e-01, 'norm_ffn1': 1.135611e-01, 'ffn1_w_gate': 5.741060e-02, 'ffn1_w_up': 6.292316e-02, 'ffn1_w_down': 1.041439e-01, 'norm_mix': 1.413220e-01, 'w_in': 2.733578e-01, 'conv_w': 1.231981e-01, 'conv_b': 1.797434e-01, 'dt_bias': 8.029165e-01, 'a_log': 8.833745e-01, 'd_skip': 3.417333e-01, 'ssd_norm_w': 1.413848e-01, 'q_norm_w': 5.078902e-02, 'w_uq': 3.554593e-02, 'kv_norm_w': 1.062628e+00, 'w_ukv': 4.069641e-01, 'mla_norm_w': 4.981139e-01, 'w_out': 4.748968e-01, 'norm_ffn2': 1.021277e-01, 'ffn2_w_gate': 4.618573e-02, 'ffn2_w_up': 4.956720e-02, 'ffn2_w_down': 7.998671e-02, 'norm_final': 6.654506e+01}


def _to_microbatches(a, axis):
    t = _jnp.moveaxis(a, axis, 0)
    t = t.reshape((N_MICROBATCH, t.shape[0] // N_MICROBATCH) + t.shape[1:])
    return _jnp.moveaxis(t, 1, axis + 1)


def setup_inputs(seed: int = 0) -> dict:
    inp = _fwd_setup_inputs(seed)
    key = _jax.random.fold_in(_jax.random.key(seed), 7919)
    shape, _ = _output_shape()
    out = dict(inp)
    out["loss_target"] = _jax.random.normal(_jax.random.fold_in(key, 0), shape, _jnp.float32)
    for i, name in enumerate(TWIN_WEIGHTS):
        w = inp[name].astype(_jnp.float32)
        if MOMENT_SCALE is None:
            s = _jnp.sqrt(_jnp.mean(_jnp.square(w)) + 1e-30)
        else:
            s = MOMENT_SCALE[name]
        km, kv = _jax.random.split(_jax.random.fold_in(key, i + 1))
        out[name] = w
        out["m_" + name] = s * _jax.random.normal(km, w.shape, _jnp.float32)
        out["v_" + name] = (s * s) * _jax.random.uniform(kv, w.shape, _jnp.float32, 0.5, 1.5)
    if N_MICROBATCH > 1:
        for name, axis in PER_EXAMPLE_BATCH_AXIS.items():
            out[name] = _to_microbatches(out[name], axis)
    return {'x': out['x'], 'c': out['c'], 'positions': out['positions'], 'w_ada': out['w_ada'], 'b_ada': out['b_ada'], 'norm_ffn1': out['norm_ffn1'], 'ffn1_w_gate': out['ffn1_w_gate'], 'ffn1_w_up': out['ffn1_w_up'], 'ffn1_w_down': out['ffn1_w_down'], 'norm_mix': out['norm_mix'], 'w_in': out['w_in'], 'conv_w': out['conv_w'], 'conv_b': out['conv_b'], 'dt_bias': out['dt_bias'], 'a_log': out['a_log'], 'd_skip': out['d_skip'], 'ssd_norm_w': out['ssd_norm_w'], 'q_norm_w': out['q_norm_w'], 'w_uq': out['w_uq'], 'kv_norm_w': out['kv_norm_w'], 'w_ukv': out['w_ukv'], 'mla_norm_w': out['mla_norm_w'], 'w_out': out['w_out'], 'norm_ffn2': out['norm_ffn2'], 'ffn2_w_gate': out['ffn2_w_gate'], 'ffn2_w_up': out['ffn2_w_up'], 'ffn2_w_down': out['ffn2_w_down'], 'norm_final': out['norm_final'], 'loss_target': out['loss_target'], 'm_w_ada': out['m_w_ada'], 'm_b_ada': out['m_b_ada'], 'm_norm_ffn1': out['m_norm_ffn1'], 'm_ffn1_w_gate': out['m_ffn1_w_gate'], 'm_ffn1_w_up': out['m_ffn1_w_up'], 'm_ffn1_w_down': out['m_ffn1_w_down'], 'm_norm_mix': out['m_norm_mix'], 'm_w_in': out['m_w_in'], 'm_conv_w': out['m_conv_w'], 'm_conv_b': out['m_conv_b'], 'm_dt_bias': out['m_dt_bias'], 'm_a_log': out['m_a_log'], 'm_d_skip': out['m_d_skip'], 'm_ssd_norm_w': out['m_ssd_norm_w'], 'm_q_norm_w': out['m_q_norm_w'], 'm_w_uq': out['m_w_uq'], 'm_kv_norm_w': out['m_kv_norm_w'], 'm_w_ukv': out['m_w_ukv'], 'm_mla_norm_w': out['m_mla_norm_w'], 'm_w_out': out['m_w_out'], 'm_norm_ffn2': out['m_norm_ffn2'], 'm_ffn2_w_gate': out['m_ffn2_w_gate'], 'm_ffn2_w_up': out['m_ffn2_w_up'], 'm_ffn2_w_down': out['m_ffn2_w_down'], 'm_norm_final': out['m_norm_final'], 'v_w_ada': out['v_w_ada'], 'v_b_ada': out['v_b_ada'], 'v_norm_ffn1': out['v_norm_ffn1'], 'v_ffn1_w_gate': out['v_ffn1_w_gate'], 'v_ffn1_w_up': out['v_ffn1_w_up'], 'v_ffn1_w_down': out['v_ffn1_w_down'], 'v_norm_mix': out['v_norm_mix'], 'v_w_in': out['v_w_in'], 'v_conv_w': out['v_conv_w'], 'v_conv_b': out['v_conv_b'], 'v_dt_bias': out['v_dt_bias'], 'v_a_log': out['v_a_log'], 'v_d_skip': out['v_d_skip'], 'v_ssd_norm_w': out['v_ssd_norm_w'], 'v_q_norm_w': out['v_q_norm_w'], 'v_w_uq': out['v_w_uq'], 'v_kv_norm_w': out['v_kv_norm_w'], 'v_w_ukv': out['v_w_ukv'], 'v_mla_norm_w': out['v_mla_norm_w'], 'v_w_out': out['v_w_out'], 'v_norm_ffn2': out['v_norm_ffn2'], 'v_ffn2_w_gate': out['v_ffn2_w_gate'], 'v_ffn2_w_up': out['v_ffn2_w_up'], 'v_ffn2_w_down': out['v_ffn2_w_down'], 'v_norm_final': out['v_norm_final']}


def _loss(weights, diff, rest, loss_target):
    with _jax.named_scope("forward"):
        args = {**rest, TWIN_DIFF_INPUT: diff, **{k: w.astype(_WEIGHT_DTYPES[k]) for k, w in weights.items()}}
        y = _forward(args)
    with _jax.named_scope("loss_head"):
        err = _jnp.square(y.astype(_jnp.float32) - loss_target)
        return 0.5 * _jnp.sum(_jnp.mean(err, axis=-1)) if err.ndim else 0.5 * err


def _adamw(w, g, m, v):
    m = ADAM_B1 * m + (1.0 - ADAM_B1) * g
    v = ADAM_B2 * v + (1.0 - ADAM_B2) * _jnp.square(g)
    m_hat = m / (1.0 - ADAM_B1 ** ADAM_STEP)
    v_hat = v / (1.0 - ADAM_B2 ** ADAM_STEP)
    delta = -ADAM_LR * (m_hat / (_jnp.sqrt(v_hat) + ADAM_EPS) + ADAM_WD * w)
    return delta, m, v


def reference(x, c, positions, w_ada, b_ada, norm_ffn1, ffn1_w_gate, ffn1_w_up, ffn1_w_down, norm_mix, w_in, conv_w, conv_b, dt_bias, a_log, d_skip, ssd_norm_w, q_norm_w, w_uq, kv_norm_w, w_ukv, mla_norm_w, w_out, norm_ffn2, ffn2_w_gate, ffn2_w_up, ffn2_w_down, norm_final, loss_target, m_w_ada, m_b_ada, m_norm_ffn1, m_ffn1_w_gate, m_ffn1_w_up, m_ffn1_w_down, m_norm_mix, m_w_in, m_conv_w, m_conv_b, m_dt_bias, m_a_log, m_d_skip, m_ssd_norm_w, m_q_norm_w, m_w_uq, m_kv_norm_w, m_w_ukv, m_mla_norm_w, m_w_out, m_norm_ffn2, m_ffn2_w_gate, m_ffn2_w_up, m_ffn2_w_down, m_norm_final, v_w_ada, v_b_ada, v_norm_ffn1, v_ffn1_w_gate, v_ffn1_w_up, v_ffn1_w_down, v_norm_mix, v_w_in, v_conv_w, v_conv_b, v_dt_bias, v_a_log, v_d_skip, v_ssd_norm_w, v_q_norm_w, v_w_uq, v_kv_norm_w, v_w_ukv, v_mla_norm_w, v_w_out, v_norm_ffn2, v_ffn2_w_gate, v_ffn2_w_up, v_ffn2_w_down, v_norm_final):
    given = dict(x=x, c=c, positions=positions, w_ada=w_ada, b_ada=b_ada, norm_ffn1=norm_ffn1, ffn1_w_gate=ffn1_w_gate, ffn1_w_up=ffn1_w_up, ffn1_w_down=ffn1_w_down, norm_mix=norm_mix, w_in=w_in, conv_w=conv_w, conv_b=conv_b, dt_bias=dt_bias, a_log=a_log, d_skip=d_skip, ssd_norm_w=ssd_norm_w, q_norm_w=q_norm_w, w_uq=w_uq, kv_norm_w=kv_norm_w, w_ukv=w_ukv, mla_norm_w=mla_norm_w, w_out=w_out, norm_ffn2=norm_ffn2, ffn2_w_gate=ffn2_w_gate, ffn2_w_up=ffn2_w_up, ffn2_w_down=ffn2_w_down, norm_final=norm_final, loss_target=loss_target, m_w_ada=m_w_ada, m_b_ada=m_b_ada, m_norm_ffn1=m_norm_ffn1, m_ffn1_w_gate=m_ffn1_w_gate, m_ffn1_w_up=m_ffn1_w_up, m_ffn1_w_down=m_ffn1_w_down, m_norm_mix=m_norm_mix, m_w_in=m_w_in, m_conv_w=m_conv_w, m_conv_b=m_conv_b, m_dt_bias=m_dt_bias, m_a_log=m_a_log, m_d_skip=m_d_skip, m_ssd_norm_w=m_ssd_norm_w, m_q_norm_w=m_q_norm_w, m_w_uq=m_w_uq, m_kv_norm_w=m_kv_norm_w, m_w_ukv=m_w_ukv, m_mla_norm_w=m_mla_norm_w, m_w_out=m_w_out, m_norm_ffn2=m_norm_ffn2, m_ffn2_w_gate=m_ffn2_w_gate, m_ffn2_w_up=m_ffn2_w_up, m_ffn2_w_down=m_ffn2_w_down, m_norm_final=m_norm_final, v_w_ada=v_w_ada, v_b_ada=v_b_ada, v_norm_ffn1=v_norm_ffn1, v_ffn1_w_gate=v_ffn1_w_gate, v_ffn1_w_up=v_ffn1_w_up, v_ffn1_w_down=v_ffn1_w_down, v_norm_mix=v_norm_mix, v_w_in=v_w_in, v_conv_w=v_conv_w, v_conv_b=v_conv_b, v_dt_bias=v_dt_bias, v_a_log=v_a_log, v_d_skip=v_d_skip, v_ssd_norm_w=v_ssd_norm_w, v_q_norm_w=v_q_norm_w, v_w_uq=v_w_uq, v_kv_norm_w=v_kv_norm_w, v_w_ukv=v_w_ukv, v_mla_norm_w=v_mla_norm_w, v_w_out=v_w_out, v_norm_ffn2=v_norm_ffn2, v_ffn2_w_gate=v_ffn2_w_gate, v_ffn2_w_up=v_ffn2_w_up, v_ffn2_w_down=v_ffn2_w_down, v_norm_final=v_norm_final)
    weights = {n: given[n] for n in TWIN_WEIGHTS}
    shared = {n: given[n] for n in SHARED_INPUTS}
    per_example = {n: given[n] for n in ['x', 'c', 'positions']}
    grad_fn = _jax.value_and_grad(_loss, argnums=(0, 1))

    def one_microbatch(ex, loss_target):
        ex = dict(ex)
        diff = ex.pop(TWIN_DIFF_INPUT)
        return grad_fn(weights, diff, {**shared, **ex}, loss_target)

    if N_MICROBATCH == 1:
        loss, (grad_w, grad_x) = one_microbatch(per_example, given["loss_target"])
    else:
        def body(carry, xs):
            loss_sum, grad_sum = carry
            l_k, (gw_k, gx_k) = one_microbatch(xs[0], xs[1])
            with _jax.named_scope("update"):
                return (loss_sum + l_k, _jax.tree.map(_jnp.add, grad_sum, gw_k)), gx_k

        init = (_jnp.zeros((), _jnp.float32), _jax.tree.map(_jnp.zeros_like, weights))
        (loss, grad_w), grad_x = _jax.lax.scan(body, init, (per_example, given["loss_target"]))
    with _jax.named_scope("update"):
        delta_w, new_m, new_v = {}, {}, {}
        for n in TWIN_WEIGHTS:
            delta_w[n], new_m[n], new_v[n] = _adamw(weights[n], grad_w[n], given["m_" + n], given["v_" + n])
    return (loss, grad_x, *[grad_w[n] for n in TWIN_WEIGHTS], *[delta_w[n] for n in TWIN_WEIGHTS],
            *[new_m[n] for n in TWIN_WEIGHTS], *[new_v[n] for n in TWIN_WEIGHTS])
```

```python
import functools
import math

import jax
import jax.numpy as jnp
import numpy as np
from jax import lax
from jax.experimental import pallas as pl
from jax.experimental.pallas import tpu as pltpu

F32 = jnp.float32
BF16 = jnp.bfloat16
HIGHEST = lax.Precision.HIGHEST

D_MODEL = 1024
D_FF = 2816
D_SSD = 1024
D_MLA = 1024
SSD_HEADS = 16
SSD_HEAD_DIM = 64
SSD_GROUPS = 2
SSD_STATE = 128
CONV_WIDTH = 4
CHUNK = 128
MLA_HEADS = 8
QK_NOPE = 64
QK_ROPE = 32
QK_DIM = QK_NOPE + QK_ROPE
V_HEAD = 128
Q_LORA = 384
KV_LORA = 256
ROPE_THETA = 10000.0
N_MOD = 9
EPS = 1e-6
D_CONV = D_SSD + 2 * SSD_GROUPS * SSD_STATE
D_PROJ = 3328
HEAD_LANES = 128
ADAM_LR = 0.001
ADAM_B1 = 0.9
ADAM_B2 = 0.999
ADAM_EPS = 1e-08
ADAM_WD = 0.01
ADAM_STEP = 10

LANES = 128
VMEM_LIMIT = 56 * 1024 * 1024
TOKEN_TILE = 512
ATTN_TILE = 512
N_CHIPS = 4
N_DEV = 8
FLAT_ROWS = 46080

MESH = pl.DeviceIdType.MESH


def _dot(a, b, precision=None):
    return jnp.dot(a, b, preferred_element_type=F32, precision=precision)


def _dot_nt(a, b, precision=None):
    return lax.dot_general(a, b, (((1,), (1,)), ((), ())), preferred_element_type=F32, precision=precision)


def _dot_tn(a, b, precision=None):
    return lax.dot_general(a, b, (((0,), (0,)), ((), ())), preferred_element_type=F32, precision=precision)


def _cparams(semantics):
    return pltpu.CompilerParams(dimension_semantics=semantics, vmem_limit_bytes=VMEM_LIMIT)


def _resident(shape):
    zeros = (0,) * len(shape)
    return pl.BlockSpec(shape, lambda *_: zeros, pipeline_mode=pl.Buffered(1))


def _sigmoid(x):
    return jax.nn.sigmoid(x)


def _rms_stats(x):
    r = lax.rsqrt(jnp.mean(x * x, axis=-1, keepdims=True) + EPS)
    return x * r, r


def _rms_bwd(dn, xh, r, w):
    dxh = dn * w
    dx = r * (dxh - xh * jnp.mean(dxh * xh, axis=-1, keepdims=True))
    return dx, dn * xh


def _colsum(v):
    return jnp.sum(v, axis=0, keepdims=True)


def _ffn_fwd(x, nw, sh, sc, g, wg, wu, wd, seq, name):
    T, D = x.shape
    F = wg.shape[1]
    tm = min(TOKEN_TILE, seq)
    tps = seq // tm

    def body(x_ref, nw_ref, sh_ref, sc_ref, g_ref, wg_ref, wu_ref, wd_ref, xo_ref, a_ref, u_ref, f_ref):
        xv = x_ref[...]
        xh, _ = _rms_stats(xv)
        h = (xh * nw_ref[...]) * (1.0 + sc_ref[0]) + sh_ref[0]
        hb = h.astype(BF16)
        a = _dot(hb, wg_ref[...])
        u = _dot(hb, wu_ref[...])
        s = a * _sigmoid(a) * u
        f = _dot(s.astype(BF16), wd_ref[...])
        xo_ref[...] = xv + 0.5 * g_ref[0] * f
        a_ref[...] = a.astype(BF16)
        u_ref[...] = u.astype(BF16)
        f_ref[...] = f.astype(BF16)

    rows = lambda n: pl.BlockSpec((tm, n), lambda i: (i, 0))
    perb = pl.BlockSpec((1, 1, D), lambda i: (i // tps, 0, 0))
    return pl.pallas_call(
        body, grid=(T // tm,), name=name,
        in_specs=[rows(D), _resident((1, D)), perb, perb, perb, _resident((D, F)), _resident((D, F)), _resident((F, D))],
        out_specs=[rows(D), rows(F), rows(F), rows(D)],
        out_shape=[jax.ShapeDtypeStruct((T, D), F32), jax.ShapeDtypeStruct((T, F), BF16),
                   jax.ShapeDtypeStruct((T, F), BF16), jax.ShapeDtypeStruct((T, D), BF16)],
        compiler_params=_cparams(("arbitrary",)),
    )(x, nw, sh, sc, g, wg, wu, wd)


def _ffn_bwd(dxo, x, nw, sh, sc, g, a, u, f, wg, wu, wd, seq, name):
    T, D = x.shape
    F = wg.shape[1]
    B = T // seq
    tm = min(TOKEN_TILE // 2, seq)
    tps = seq // tm

    def body(dxo_ref, x_ref, nw_ref, sh_ref, sc_ref, g_ref, a_ref, u_ref, f_ref, wg_ref, wu_ref, wd_ref,
             dx_ref, h_ref, s_ref, df_ref, da_ref, du_ref, dsh_ref, dsc_ref, dg_ref, dnw_ref):
        i = pl.program_id(0)

        @pl.when(i % tps == 0)
        def _():
            dsh_ref[...] = jnp.zeros_like(dsh_ref)
            dsc_ref[...] = jnp.zeros_like(dsc_ref)
            dg_ref[...] = jnp.zeros_like(dg_ref)

        @pl.when(i == 0)
        def _():
            dnw_ref[...] = jnp.zeros_like(dnw_ref)

        dxo_v = dxo_ref[...]
        gate = g_ref[0]
        df = 0.5 * gate * dxo_v
        dfb = df.astype(BF16)
        dg_ref[0] += _colsum(0.5 * dxo_v * f_ref[...].astype(F32))
        ds = _dot_nt(dfb, wd_ref[...])
        av = a_ref[...].astype(F32)
        uv = u_ref[...].astype(F32)
        sig = _sigmoid(av)
        sil = av * sig
        du = ds * sil
        da = ds * uv * (sig * (1.0 + av * (1.0 - sig)))
        dab = da.astype(BF16)
        dub = du.astype(BF16)
        dh = _dot_nt(dab, wg_ref[...]) + _dot_nt(dub, wu_ref[...])
        xv = x_ref[...]
        xh, r = _rms_stats(xv)
        nwv = nw_ref[...]
        n = xh * nwv
        scale1 = 1.0 + sc_ref[0]
        dsc_ref[0] += _colsum(dh * n)
        dsh_ref[0] += _colsum(dh)
        dx, dw_rows = _rms_bwd(dh * scale1, xh, r, nwv)
        dnw_ref[...] += _colsum(dw_rows)
        dx_ref[...] = dxo_v + dx
        h_ref[...] = (n * scale1 + sh_ref[0]).astype(BF16)
        s_ref[...] = (sil * uv).astype(BF16)
        df_ref[...] = dfb
        da_ref[...] = dab
        du_ref[...] = dub

    rows = lambda n: pl.BlockSpec((tm, n), lambda i: (i, 0))
    perb = pl.BlockSpec((1, 1, D), lambda i: (i // tps, 0, 0))
    sd = jax.ShapeDtypeStruct
    return pl.pallas_call(
        body, grid=(T // tm,), name=name,
        in_specs=[rows(D), rows(D), _resident((1, D)), perb, perb, perb, rows(F), rows(F), rows(D),
                  _resident((D, F)), _resident((D, F)), _resident((F, D))],
        out_specs=[rows(D), rows(D), rows(F), rows(D), rows(F), rows(F), perb, perb, perb,
                   pl.BlockSpec((1, D), lambda i: (0, 0))],
        out_shape=[sd((T, D), F32), sd((T, D), BF16), sd((T, F), BF16), sd((T, D), BF16), sd((T, F), BF16),
                   sd((T, F), BF16), sd((B, 1, D), F32), sd((B, 1, D), F32), sd((B, 1, D), F32), sd((1, D), F32)],
        compiler_params=_cparams(("arbitrary",)),
    )(dxo, x, nw, sh, sc, g, a, u, f, wg, wu, wd)


def _mm_tn(xa, ya, tn, name):
    T, K = xa.shape
    N = ya.shape[1]
    tt = min(TOKEN_TILE, T)

    def body(x_ref, y_ref, o_ref):
        @pl.when(pl.program_id(1) == 0)
        def _():
            o_ref[...] = jnp.zeros_like(o_ref)

        o_ref[...] += _dot_tn(x_ref[...], y_ref[...])

    return pl.pallas_call(
        body, grid=(N // tn, T // tt), name=name,
        in_specs=[pl.BlockSpec((tt, K), lambda j, t: (t, 0)), pl.BlockSpec((tt, tn), lambda j, t: (t, j))],
        out_specs=pl.BlockSpec((K, tn), lambda j, t: (0, j)),
        out_shape=jax.ShapeDtypeStruct((K, N), F32),
        compiler_params=_cparams(("arbitrary", "arbitrary")),
    )(xa, ya)


def _final_loss(x, nw, tgt):
    T, D = x.shape
    tm = min(TOKEN_TILE, T)

    def body(x_ref, nw_ref, t_ref, dx_ref, loss_ref, dnw_ref):
        @pl.when(pl.program_id(0) == 0)
        def _():
            loss_ref[...] = jnp.zeros_like(loss_ref)
            dnw_ref[...] = jnp.zeros_like(dnw_ref)

        xv = x_ref[...]
        xh, r = _rms_stats(xv)
        nwv = nw_ref[...]
        err = xh * nwv - t_ref[...]
        loss_ref[...] += (0.5 / D) * jnp.sum(err * err)
        dx, dw_rows = _rms_bwd(err * (1.0 / D), xh, r, nwv)
        dx_ref[...] = dx
        dnw_ref[...] += _colsum(dw_rows)

    rows = pl.BlockSpec((tm, D), lambda i: (i, 0))
    return pl.pallas_call(
        body, grid=(T // tm,), name="final_loss",
        in_specs=[rows, _resident((1, D)), rows],
        out_specs=[rows, pl.BlockSpec((8, LANES), lambda i: (0, 0)), pl.BlockSpec((1, D), lambda i: (0, 0))],
        out_shape=[jax.ShapeDtypeStruct((T, D), F32), jax.ShapeDtypeStruct((8, LANES), F32),
                   jax.ShapeDtypeStruct((1, D), F32)],
        compiler_params=_cparams(("arbitrary",)),
    )(x, nw, tgt)


_PROJ_SPLITS = (0, 1024, 2560, 2944, 3200, 3328)


def _inproj_fwd(x, nw, sh, sc, win, seq):
    T, D = x.shape
    tm = min(TOKEN_TILE, seq)
    tps = seq // tm
    widths = [b - a for a, b in zip(_PROJ_SPLITS[:-1], _PROJ_SPLITS[1:])]
    dtypes = [BF16, BF16, F32, F32, F32]

    def body(x_ref, nw_ref, sh_ref, sc_ref, w_ref, *outs):
        xh, _ = _rms_stats(x_ref[...])
        h = (xh * nw_ref[...]) * (1.0 + sc_ref[0]) + sh_ref[0]
        proj = _dot(h.astype(BF16), w_ref[...])
        for o, lo, hi in zip(outs, _PROJ_SPLITS[:-1], _PROJ_SPLITS[1:]):
            o[...] = proj[:, lo:hi].astype(o.dtype)

    rows = lambda n: pl.BlockSpec((tm, n), lambda i: (i, 0))
    perb = pl.BlockSpec((1, 1, D), lambda i: (i // tps, 0, 0))
    return pl.pallas_call(
        body, grid=(T // tm,), name="inproj_fwd",
        in_specs=[rows(D), _resident((1, D)), perb, perb, _resident((D, D_PROJ))],
        out_specs=[rows(w) for w in widths],
        out_shape=[jax.ShapeDtypeStruct((T, w), dt) for w, dt in zip(widths, dtypes)],
        compiler_params=_cparams(("arbitrary",)),
    )(x, nw, sh, sc, win)


def _inproj_bwd(dx2, x, nw, sh, sc, win, dz, dxbc, dcq, dckv, ddtk_a, ddtk_b, seq):
    T, D = x.shape
    B = T // seq
    tm = min(TOKEN_TILE, seq)
    tps = seq // tm

    def body(dx2_ref, x_ref, nw_ref, sh_ref, sc_ref, w_ref, dz_ref, dxbc_ref, dcq_ref, dckv_ref, da_ref, db_ref,
             dx_ref, h_ref, dp_ref, dsh_ref, dsc_ref, dnw_ref):
        i = pl.program_id(0)

        @pl.when(i % tps == 0)
        def _():
            dsh_ref[...] = jnp.zeros_like(dsh_ref)
            dsc_ref[...] = jnp.zeros_like(dsc_ref)

        @pl.when(i == 0)
        def _():
            dnw_ref[...] = jnp.zeros_like(dnw_ref)

        dproj = jnp.concatenate(
            [dz_ref[...], dxbc_ref[...], dcq_ref[...].astype(BF16), dckv_ref[...].astype(BF16),
             (da_ref[...] + db_ref[...]).astype(BF16)], axis=1)
        dp_ref[...] = dproj
        dh = _dot_nt(dproj, w_ref[...])
        xh, r = _rms_stats(x_ref[...])
        nwv = nw_ref[...]
        n = xh * nwv
        scale1 = 1.0 + sc_ref[0]
        dsc_ref[0] += _colsum(dh * n)
        dsh_ref[0] += _colsum(dh)
        dx, dw_rows = _rms_bwd(dh * scale1, xh, r, nwv)
        dnw_ref[...] += _colsum(dw_rows)
        dx_ref[...] = dx2_ref[...] + dx
        h_ref[...] = (n * scale1 + sh_ref[0]).astype(BF16)

    rows = lambda n: pl.BlockSpec((tm, n), lambda i: (i, 0))
    perb = pl.BlockSpec((1, 1, D), lambda i: (i // tps, 0, 0))
    sd = jax.ShapeDtypeStruct
    return pl.pallas_call(
        body, grid=(T // tm,), name="inproj_bwd",
        in_specs=[rows(D), rows(D), _resident((1, D)), perb, perb, _resident((D, D_PROJ)),
                  rows(1024), rows(D_CONV), rows(Q_LORA), rows(KV_LORA), rows(LANES), rows(LANES)],
        out_specs=[rows(D), rows(D), rows(D_PROJ), perb, perb, pl.BlockSpec((1, D), lambda i: (0, 0))],
        out_shape=[sd((T, D), F32), sd((T, D), BF16), sd((T, D_PROJ), BF16), sd((B, 1, D), F32), sd((B, 1, D), F32),
                   sd((1, D), F32)],
        compiler_params=_cparams(("arbitrary",)),
    )(dx2, x, nw, sh, sc, win, dz, dxbc, dcq, dckv, ddtk_a, ddtk_b)


def _shift_down(v, k, row):
    return jnp.where(row < k, 0.0, pltpu.roll(v, k, 0))


def _shift_up(v, k, row, n):
    return jnp.where(row >= n - k, 0.0, pltpu.roll(v, n - k, 0))


def _conv_pre(xv, w_ref, b_ref, row):
    pre = b_ref[...] + w_ref[CONV_WIDTH - 1:CONV_WIDTH, :] * xv
    for k in range(1, CONV_WIDTH):
        pre = pre + w_ref[CONV_WIDTH - 1 - k:CONV_WIDTH - k, :] * _shift_down(xv, k, row)
    return pre


def _conv_fwd(xraw, cw, cb):
    B, S, C = xraw.shape

    def body(x_ref, w_ref, b_ref, o_ref):
        xv = x_ref[0].astype(F32)
        row = lax.broadcasted_iota(jnp.int32, xv.shape, 0)
        pre = _conv_pre(xv, w_ref, b_ref, row)
        o_ref[0] = (pre * _sigmoid(pre)).astype(BF16)

    blk = pl.BlockSpec((1, S, LANES), lambda b, j: (b, 0, j))
    return pl.pallas_call(
        body, grid=(B, C // LANES), name="conv_fwd",
        in_specs=[blk, pl.BlockSpec((CONV_WIDTH, LANES), lambda b, j: (0, j)), pl.BlockSpec((1, LANES), lambda b, j: (0, j))],
        out_specs=blk, out_shape=jax.ShapeDtypeStruct((B, S, C), BF16),
        compiler_params=_cparams(("arbitrary", "arbitrary")),
    )(xraw, cw, cb)


def _conv_bwd(dout, xraw, cw, cb):
    B, S, C = xraw.shape

    def body(d_ref, x_ref, w_ref, b_ref, dx_ref, dw_ref, db_ref):
        @pl.when(pl.program_id(1) == 0)
        def _():
            dw_ref[...] = jnp.zeros_like(dw_ref)
            db_ref[...] = jnp.zeros_like(db_ref)

        xv = x_ref[0].astype(F32)
        row = lax.broadcasted_iota(jnp.int32, xv.shape, 0)
        pre = _conv_pre(xv, w_ref, b_ref, row)
        sig = _sigmoid(pre)
        dpre = d_ref[0].astype(F32) * (sig * (1.0 + pre * (1.0 - sig)))
        dx = w_ref[CONV_WIDTH - 1:CONV_WIDTH, :] * dpre
        for k in range(1, CONV_WIDTH):
            dx = dx + w_ref[CONV_WIDTH - 1 - k:CONV_WIDTH - k, :] * _shift_up(dpre, k, row, S)
        dx_ref[0] = dx.astype(BF16)
        db_ref[...] += _colsum(dpre)
        dws = [_colsum(dpre * (xv if k == 0 else _shift_down(xv, k, row))) for k in range(CONV_WIDTH - 1, -1, -1)]
        dw_ref[...] += jnp.concatenate(dws, axis=0)

    blk = pl.BlockSpec((1, S, LANES), lambda j, b: (b, 0, j))
    wspec = pl.BlockSpec((CONV_WIDTH, LANES), lambda j, b: (0, j))
    bspec = pl.BlockSpec((1, LANES), lambda j, b: (0, j))
    return pl.pallas_call(
        body, grid=(C // LANES, B), name="conv_bwd",
        in_specs=[blk, blk, wspec, bspec], out_specs=[blk, wspec, bspec],
        out_shape=[jax.ShapeDtypeStruct((B, S, C), BF16), jax.ShapeDtypeStruct((CONV_WIDTH, C), F32),
                   jax.ShapeDtypeStruct((1, C), F32)],
        compiler_params=_cparams(("arbitrary", "arbitrary")),
    )(dout, xraw, cw, cb)


def _softplus(x):
    return jnp.maximum(x, 0.0) + jnp.log(1.0 + jnp.exp(-jnp.abs(x)))


def _ssd_common(xbc_ref, dtk_ref, dtb_ref, alog_ref, e_ref):
    L = CHUNK
    xbc = xbc_ref[0]
    xs = xbc[:, :D_SSD].astype(F32)
    bm = xbc[:, D_SSD:D_SSD + 256]
    cm = xbc[:, D_SSD + 256:D_SSD + 512]
    head = lax.broadcasted_iota(jnp.int32, (1, LANES), 1) < SSD_HEADS
    a128 = jnp.where(head, -jnp.exp(alog_ref[...]), 0.0)
    pre = dtk_ref[0] + dtb_ref[...]
    dt = _softplus(pre)
    dA = dt * a128
    row = lax.broadcasted_iota(jnp.int32, (L, L), 0)
    col = lax.broadcasted_iota(jnp.int32, (L, L), 1)
    causal = col <= row
    tri = causal.astype(F32)
    triT = (row <= col).astype(F32)
    acum = _dot(tri, dA, HIGHEST)
    acumT = _dot_tn(dA, triT, HIGHEST)
    E = e_ref[...]
    acum_f = _dot(acum, E, HIGHEST)
    dt_f = _dot(dt, E, HIGHEST)
    e_f = jnp.exp(acum_f)
    w_f = jnp.exp(acum_f[L - 1:L, :] - acum_f)
    xt = xs * dt_f
    alast_rows = _dot_tn(E, jnp.broadcast_to(acumT[:, L - 1:L], (LANES, LANES)), HIGHEST)
    decay_rows = jnp.exp(alast_rows)
    return dict(xs=xs, bm=bm, cm=cm, a128=a128, pre=pre, dt=dt, causal=causal, tri=tri, triT=triT, acum=acum,
                acumT=acumT, E=E, dt_f=dt_f, e_f=e_f, w_f=w_f, xt=xt, decay_rows=decay_rows, head=head)


def _head_mask(k):
    lane = lax.broadcasted_iota(jnp.int32, (CHUNK, LANES), 1)
    return (lane >= SSD_HEAD_DIM) if k == 1 else (lane < SSD_HEAD_DIM)


def _decay_matrix(q, h):
    seg = q["acum"][:, h:h + 1] - q["acumT"][h:h + 1, :]
    return jnp.exp(jnp.where(q["causal"], seg, -1e30))


def _gated_norm(y, zz, nw):
    sig = _sigmoid(zz)
    sil = zz * sig
    yg = y * sil
    half = D_SSD // SSD_GROUPS
    parts = []
    for g in range(SSD_GROUPS):
        xh, r = _rms_stats(yg[:, g * half:(g + 1) * half])
        parts.append((xh, r))
    return sig, sil, parts


def _ssd_fwd(xbc, dtk, z, dtb, alog, dsk, nw, expand):
    B, S, _ = xbc.shape
    L = CHUNK
    nc = S // L

    def body(xbc_ref, dtk_ref, z_ref, dtb_ref, alog_ref, dsk_ref, nw_ref, e_ref, y_ref, ys_ref, prev_ref, st_ref):
        @pl.when(pl.program_id(1) == 0)
        def _():
            st_ref[...] = jnp.zeros_like(st_ref)

        q = _ssd_common(xbc_ref, dtk_ref, dtb_ref, alog_ref, e_ref)
        xtb = q["xt"].astype(BF16)
        xwb = (q["xt"] * q["w_f"]).astype(BF16)
        ys = []
        for g in range(SSD_GROUPS):
            bg = q["bm"][:, g * 128:(g + 1) * 128]
            cg = q["cm"][:, g * 128:(g + 1) * 128]
            G = _dot_nt(cg, bg)
            for pr in range(SSD_HEADS // SSD_GROUPS // 2):
                h0 = g * 8 + 2 * pr
                lo = h0 * SSD_HEAD_DIM
                xt_p = xtb[:, lo:lo + 128]
                ydiag = jnp.zeros((L, LANES), F32)
                for k in range(2):
                    M = (G * _decay_matrix(q, h0 + k)).astype(BF16)
                    ydiag = ydiag + _dot(M, jnp.where(_head_mask(k), xt_p, jnp.zeros_like(xt_p)))
                hp = st_ref[lo:lo + 128, :]
                prev_ref[0, 0, lo:lo + 128, :] = hp
                zoff = _dot_nt(cg, hp.astype(BF16))
                ys.append(ydiag + zoff * q["e_f"][:, lo:lo + 128])
                st_ref[lo:lo + 128, :] = q["decay_rows"][lo:lo + 128, :] * hp + _dot_tn(xwb[:, lo:lo + 128], bg)
        dsk_f = _dot(jnp.broadcast_to(dsk_ref[...], (8, LANES)), q["E"], HIGHEST)[0:1, :]
        y = jnp.concatenate(ys, axis=1) + dsk_f * q["xs"]
        y_ref[0] = y.astype(BF16)
        _, _, parts = _gated_norm(y, z_ref[0].astype(F32), nw_ref[...])
        half = D_SSD // SSD_GROUPS
        ys_ref[0] = jnp.concatenate(
            [xh * nw_ref[:, g * half:(g + 1) * half] for g, (xh, _) in enumerate(parts)], axis=1).astype(BF16)

    chunk = lambda n: pl.BlockSpec((1, L, n), lambda b, c: (b, c, 0))
    vec = pl.BlockSpec((1, LANES), lambda b, c: (0, 0))
    return pl.pallas_call(
        body, grid=(B, nc), name="ssd_fwd",
        in_specs=[chunk(D_CONV), chunk(LANES), chunk(D_SSD), vec, vec, vec,
                  pl.BlockSpec((1, D_SSD), lambda b, c: (0, 0)), pl.BlockSpec((LANES, D_SSD), lambda b, c: (0, 0))],
        out_specs=[chunk(D_SSD), chunk(D_SSD), pl.BlockSpec((1, 1, D_SSD, SSD_STATE), lambda b, c: (b, c, 0, 0))],
        out_shape=[jax.ShapeDtypeStruct((B, S, D_SSD), BF16), jax.ShapeDtypeStruct((B, S, D_SSD), BF16),
                   jax.ShapeDtypeStruct((B, nc, D_SSD, SSD_STATE), F32)],
        scratch_shapes=[pltpu.VMEM((D_SSD, SSD_STATE), F32)],
        compiler_params=_cparams(("arbitrary", "arbitrary")),
    )(xbc, dtk, z, dtb, alog, dsk, nw, expand)


def _ssd_bwd(xbc, dtk, z, y, prev, dys, dtb, alog, dsk, nw, expand):
    B, S, _ = xbc.shape
    L = CHUNK
    nc = S // L
    half = D_SSD // SSD_GROUPS

    def body(xbc_ref, dtk_ref, z_ref, y_ref, prev_ref, dys_ref, dtb_ref, alog_ref, dsk_ref, nw_ref, e_ref,
             dxbc_ref, ddtk_ref, dz_ref, dnw_ref, dvec_ref, dh_ref):
        @pl.when((pl.program_id(0) == 0) & (pl.program_id(1) == 0))
        def _():
            dnw_ref[...] = jnp.zeros_like(dnw_ref)
            dvec_ref[...] = jnp.zeros_like(dvec_ref)

        @pl.when(pl.program_id(1) == 0)
        def _():
            dh_ref[...] = jnp.zeros_like(dh_ref)

        q = _ssd_common(xbc_ref, dtk_ref, dtb_ref, alog_ref, e_ref)
        E = q["E"]
        xs = q["xs"]
        yv = y_ref[0].astype(F32)
        zz = z_ref[0].astype(F32)
        sig, sil, parts = _gated_norm(yv, zz, nw_ref[...])
        dn = dys_ref[0].astype(F32)
        dyg, dnw_rows = [], []
        for g, (xh, r) in enumerate(parts):
            dpart, dw_rows = _rms_bwd(dn[:, g * half:(g + 1) * half], xh, r, nw_ref[:, g * half:(g + 1) * half])
            dyg.append(dpart)
            dnw_rows.append(dw_rows)
        dyg = jnp.concatenate(dyg, axis=1)
        dnw_ref[...] += _colsum(jnp.concatenate(dnw_rows, axis=1))
        dY = dyg * sil
        dz_ref[0] = (dyg * yv * (sig * (1.0 + zz * (1.0 - sig)))).astype(BF16)
        dsk_f = _dot(jnp.broadcast_to(dsk_ref[...], (8, LANES)), E, HIGHEST)[0:1, :]
        ddsk = _dot_nt(jnp.broadcast_to(_colsum(dY * xs), (8, D_SSD)), E, HIGHEST)[0:1, :]
        dYb = dY.astype(BF16)
        xtb = q["xt"].astype(BF16)
        xwb = (q["xt"] * q["w_f"]).astype(BF16)
        lane_id = lax.broadcasted_iota(jnp.int32, (L, LANES), 1)
        da_rows = jnp.zeros((L, LANES), F32)
        daT = jnp.zeros((LANES, L), F32)
        dxt, prod_off, prod_st, dbs, dcs = [], [], [], [], []
        hsum = jnp.zeros((LANES, LANES), F32)
        for g in range(SSD_GROUPS):
            bg = q["bm"][:, g * 128:(g + 1) * 128]
            cg = q["cm"][:, g * 128:(g + 1) * 128]
            G = _dot_nt(cg, bg)
            dG = jnp.zeros((L, L), F32)
            dcg = jnp.zeros((L, SSD_STATE), F32)
            dbg = jnp.zeros((L, SSD_STATE), F32)
            for pr in range(SSD_HEADS // SSD_GROUPS // 2):
                h0 = g * 8 + 2 * pr
                lo = h0 * SSD_HEAD_DIM
                cols = slice(lo, lo + 128)
                dY_p = dYb[:, cols]
                xt_p = xtb[:, cols]
                dxt_p = jnp.zeros((L, LANES), F32)
                for k in range(2):
                    h = h0 + k
                    Lm = _decay_matrix(q, h)
                    Mf = G * Lm
                    dYk = jnp.where(_head_mask(k), dY_p, jnp.zeros_like(dY_p))
                    dM = _dot_nt(dYk, xt_p)
                    dxt_p = dxt_p + _dot_tn(Mf.astype(BF16), dYk)
                    dG = dG + dM * Lm
                    Q = dM * Mf
                    onehot = (lane_id == h).astype(F32)
                    da_rows = da_rows + _dot(Q, onehot, HIGHEST)
                    daT = daT + _dot_tn(onehot, Q, HIGHEST)
                hp = prev_ref[0, 0, lo:lo + 128, :]
                hpb = hp.astype(BF16)
                zoff = _dot_nt(cg, hpb)
                e_p = q["e_f"][:, cols]
                dY_pf = dY[:, cols]
                dZb = (dY_pf * e_p).astype(BF16)
                dcg = dcg + _dot(dZb, hpb)
                dhp_off = _dot_tn(dZb, cg)
                prod_off.append(dY_pf * zoff * e_p)
                dS = dh_ref[lo:lo + 128, :]
                dSb = dS.astype(BF16)
                U = _dot_nt(bg, dSb)
                dxt_p = dxt_p + U * q["w_f"][:, cols]
                dbg = dbg + _dot(xwb[:, cols], dSb)
                prod_st.append(q["xt"][:, cols] * U)
                decay_p = q["decay_rows"][lo:lo + 128, :]
                dh_ref[lo:lo + 128, :] = decay_p * dS + dhp_off
                hsum = hsum + _dot(E[:, cols], dS * hp, HIGHEST)
                dxt.append(dxt_p)
            dGb = dG.astype(BF16)
            dcs.append(dcg + _dot(dGb, bg))
            dbs.append(dbg + _dot_tn(dGb, cg))
        dxt = jnp.concatenate(dxt, axis=1)
        da_rows = da_rows + _dot_nt(jnp.concatenate(prod_off, axis=1), E, HIGHEST)
        dw = _dot_nt(jnp.concatenate(prod_st, axis=1), E, HIGHEST)
        acum = q["acum"]
        alast = acum[L - 1:L, :]
        dww = dw * jnp.exp(alast - acum)
        da_rows = da_rows - dww
        hsum_row = _colsum(hsum.T)
        dlast = _colsum(dww) + jnp.exp(alast) * hsum_row
        ddA = _dot(q["triT"], da_rows, HIGHEST) - _dot_nt(q["triT"], daT, HIGHEST) + dlast
        ddA = jnp.where(q["head"], ddA, 0.0)
        ddt = ddA * q["a128"] + _dot_nt(dxt * xs, E, HIGHEST)
        ddt_raw = jnp.where(q["head"], ddt * _sigmoid(q["pre"]), 0.0)
        ddtk_ref[0] = ddt_raw
        dxs = dxt * q["dt_f"] + dsk_f * dY
        dxbc_ref[0] = jnp.concatenate([dxs] + dbs + dcs, axis=1).astype(BF16)
        dvec_ref[0:1, :] += _colsum(ddt_raw)
        dvec_ref[1:2, :] += _colsum(ddA * q["dt"]) * q["a128"]
        dvec_ref[2:3, :] += ddsk

    rev = lambda n: pl.BlockSpec((1, L, n), lambda b, c: (b, nc - 1 - c, 0))
    vec = pl.BlockSpec((1, LANES), lambda b, c: (0, 0))
    sd = jax.ShapeDtypeStruct
    return pl.pallas_call(
        body, grid=(B, nc), name="ssd_bwd",
        in_specs=[rev(D_CONV), rev(LANES), rev(D_SSD), rev(D_SSD),
                  pl.BlockSpec((1, 1, D_SSD, SSD_STATE), lambda b, c: (b, nc - 1 - c, 0, 0)), rev(D_SSD), vec, vec, vec,
                  pl.BlockSpec((1, D_SSD), lambda b, c: (0, 0)), pl.BlockSpec((LANES, D_SSD), lambda b, c: (0, 0))],
        out_specs=[rev(D_CONV), rev(LANES), rev(D_SSD), pl.BlockSpec((1, D_SSD), lambda b, c: (0, 0)),
                   pl.BlockSpec((8, LANES), lambda b, c: (0, 0))],
        out_shape=[sd((B, S, D_CONV), BF16), sd((B, S, LANES), F32), sd((B, S, D_SSD), BF16), sd((1, D_SSD), F32),
                   sd((8, LANES), F32)],
        scratch_shapes=[pltpu.VMEM((D_SSD, SSD_STATE), F32)],
        compiler_params=_cparams(("arbitrary", "arbitrary")),
    )(xbc, dtk, z, y, prev, dys, dtb, alog, dsk, nw, expand)


def _rope_tables(pos_ref, invf_ref):
    ang = pos_ref[...].astype(F32) * invf_ref[...]
    return jnp.cos(ang), jnp.sin(ang)


def _rot(u):
    lane = lax.broadcasted_iota(jnp.int32, u.shape, 1)
    first = (lane >= QK_NOPE) & (lane < QK_NOPE + QK_ROPE // 2)
    second = (lane >= QK_NOPE + QK_ROPE // 2) & (lane < QK_DIM)
    return jnp.where(first, -pltpu.roll(u, LANES - QK_ROPE // 2, 1), jnp.where(second, pltpu.roll(u, QK_ROPE // 2, 1), 0.0))


def _rope_lanes(shape):
    lane = lax.broadcasted_iota(jnp.int32, shape, 1)
    return (lane >= QK_NOPE) & (lane < QK_DIM)


def _mla_prep(cq, ckv, dtk, pos, qw, kvw, wuq, wukv, invf):
    T = cq.shape[0]
    tm = min(TOKEN_TILE, T)
    scale = 1.0 / math.sqrt(QK_DIM)
    HW = MLA_HEADS * HEAD_LANES

    def body(cq_ref, ckv_ref, dtk_ref, pos_ref, qw_ref, kvw_ref, wuq_ref, wukv_ref, invf_ref, q_ref, k_ref, v_ref):
        xh, _ = _rms_stats(cq_ref[...])
        qv = _dot((xh * qw_ref[...]).astype(BF16), wuq_ref[...])
        xh, _ = _rms_stats(ckv_ref[...])
        kv = _dot((xh * kvw_ref[...]).astype(BF16), wukv_ref[...])
        cosf, sinf = _rope_tables(pos_ref, invf_ref)
        rope = lambda u: u * cosf + _rot(u) * sinf
        dtkv = dtk_ref[...]
        kr = rope(jnp.where(_rope_lanes(dtkv.shape), dtkv, 0.0))
        for h in range(MLA_HEADS):
            cols = slice(h * HEAD_LANES, (h + 1) * HEAD_LANES)
            q_ref[:, cols] = (rope(qv[:, cols]) * scale).astype(BF16)
            k_ref[:, cols] = (kv[:, cols] + kr).astype(BF16)
        v_ref[...] = kv[:, HW:].astype(BF16)

    rows = lambda n: pl.BlockSpec((tm, n), lambda i: (i, 0))
    return pl.pallas_call(
        body, grid=(T // tm,), name="mla_prep",
        in_specs=[rows(Q_LORA), rows(KV_LORA), rows(LANES), rows(1), _resident((1, Q_LORA)), _resident((1, KV_LORA)),
                  _resident((Q_LORA, HW)), _resident((KV_LORA, 2 * HW)), _resident((1, LANES))],
        out_specs=[rows(HW), rows(HW), rows(HW)],
        out_shape=[jax.ShapeDtypeStruct((T, HW), BF16)] * 3,
        compiler_params=_cparams(("arbitrary",)),
    )(cq, ckv, dtk, pos, qw, kvw, wuq, wukv, invf)


def _mla_prep_bwd(dq, dk, dv, cq, ckv, pos, qw, kvw, wuq, wukv, invf):
    T = cq.shape[0]
    tm = min(TOKEN_TILE, T)
    scale = 1.0 / math.sqrt(QK_DIM)
    HW = MLA_HEADS * HEAD_LANES

    def body(dq_ref, dk_ref, dv_ref, cq_ref, ckv_ref, pos_ref, qw_ref, kvw_ref, wuq_ref, wukv_ref, invf_ref,
             dcq_ref, dckv_ref, ddtk_ref, qn_ref, kvn_ref, dqo_ref, dkvo_ref, dqw_ref, dkvw_ref):
        @pl.when(pl.program_id(0) == 0)
        def _():
            dqw_ref[...] = jnp.zeros_like(dqw_ref)
            dkvw_ref[...] = jnp.zeros_like(dkvw_ref)

        cosf, sinf = _rope_tables(pos_ref, invf_ref)
        unrope = lambda d: d * cosf - _rot(d * sinf)
        dkr = jnp.zeros((tm, LANES), F32)
        nope = lax.broadcasted_iota(jnp.int32, (tm, LANES), 1) < QK_NOPE
        for h in range(MLA_HEADS):
            cols = slice(h * HEAD_LANES, (h + 1) * HEAD_LANES)
            dqo_ref[:, cols] = unrope(dq_ref[:, cols] * scale).astype(BF16)
            dkh = dk_ref[:, cols]
            dkr = dkr + jnp.where(_rope_lanes(dkh.shape), dkh, 0.0)
            dkvo_ref[:, cols] = jnp.where(nope, dkh, 0.0).astype(BF16)
        dkvo_ref[:, HW:] = dv_ref[...].astype(BF16)
        ddtk_ref[...] = unrope(dkr)
        xh, r = _rms_stats(cq_ref[...])
        qn_ref[...] = (xh * qw_ref[...]).astype(BF16)
        dx, dw_rows = _rms_bwd(_dot_nt(dqo_ref[...], wuq_ref[...]), xh, r, qw_ref[...])
        dcq_ref[...] = dx
        dqw_ref[...] += _colsum(dw_rows)
        xh, r = _rms_stats(ckv_ref[...])
        kvn_ref[...] = (xh * kvw_ref[...]).astype(BF16)
        dx, dw_rows = _rms_bwd(_dot_nt(dkvo_ref[...], wukv_ref[...]), xh, r, kvw_ref[...])
        dckv_ref[...] = dx
        dkvw_ref[...] += _colsum(dw_rows)

    rows = lambda n: pl.BlockSpec((tm, n), lambda i: (i, 0))
    sd = jax.ShapeDtypeStruct
    return pl.pallas_call(
        body, grid=(T // tm,), name="mla_prep_bwd",
        in_specs=[rows(HW), rows(HW), rows(HW), rows(Q_LORA), rows(KV_LORA), rows(1), _resident((1, Q_LORA)),
                  _resident((1, KV_LORA)), _resident((Q_LORA, HW)), _resident((KV_LORA, 2 * HW)), _resident((1, LANES))],
        out_specs=[rows(Q_LORA), rows(KV_LORA), rows(LANES), rows(Q_LORA), rows(KV_LORA), rows(HW), rows(2 * HW),
                   pl.BlockSpec((1, Q_LORA), lambda i: (0, 0)), pl.BlockSpec((1, KV_LORA), lambda i: (0, 0))],
        out_shape=[sd((T, Q_LORA), F32), sd((T, KV_LORA), F32), sd((T, LANES), F32), sd((T, Q_LORA), BF16),
                   sd((T, KV_LORA), BF16), sd((T, HW), BF16), sd((T, 2 * HW), BF16), sd((1, Q_LORA), F32),
                   sd((1, KV_LORA), F32)],
        compiler_params=_cparams(("arbitrary",)),
    )(dq, dk, dv, cq, ckv, pos, qw, kvw, wuq, wukv, invf)


def _causal_mask(t):
    row = lax.broadcasted_iota(jnp.int32, (t, t), 0)
    col = lax.broadcasted_iota(jnp.int32, (t, t), 1)
    return col <= row


def _attn_fwd(q, k, v):
    B, S, HW = q.shape
    H = HW // HEAD_LANES
    t = min(ATTN_TILE, S)
    nq = S // t

    def body(q_ref, k_ref, v_ref, o_ref, lse_ref):
        qi = pl.program_id(2)
        qv = q_ref[0]

        def step(j, carry, masked):
            m, l, acc = carry
            sl = pl.ds(pl.multiple_of(j * t, t), t)
            s = _dot_nt(qv, k_ref[0, sl, :])
            if masked:
                s = jnp.where(_causal_mask(t), s, -1e30)
            m_new = jnp.maximum(m, jnp.max(s, axis=-1, keepdims=True))
            alpha = jnp.exp(m - m_new)
            p = jnp.exp(s - m_new)
            l = alpha * l + jnp.sum(p, axis=-1, keepdims=True)
            acc = alpha * acc + _dot(p.astype(BF16), v_ref[0, sl, :])
            return m_new, l, acc

        init = (jnp.full((t, 1), -1e30, F32), jnp.zeros((t, 1), F32), jnp.zeros((t, HEAD_LANES), F32))
        carry = lax.fori_loop(0, qi, lambda j, c: step(j, c, False), init)
        m, l, acc = step(qi, carry, True)
        o_ref[0] = (acc / l).astype(BF16)
        lse_ref[0, 0] = m + jnp.log(l)

    return pl.pallas_call(
        body, grid=(B, H, nq), name="attn_fwd",
        in_specs=[pl.BlockSpec((1, t, HEAD_LANES), lambda b, h, i: (b, i, h)),
                  pl.BlockSpec((1, S, HEAD_LANES), lambda b, h, i: (b, 0, h)),
                  pl.BlockSpec((1, S, HEAD_LANES), lambda b, h, i: (b, 0, h))],
        out_specs=[pl.BlockSpec((1, t, HEAD_LANES), lambda b, h, i: (b, i, h)),
                   pl.BlockSpec((1, 1, t, 1), lambda b, h, i: (b, h, i, 0))],
        out_shape=[jax.ShapeDtypeStruct((B, S, HW), BF16), jax.ShapeDtypeStruct((B, H, S, 1), F32)],
        compiler_params=_cparams(("arbitrary", "arbitrary", "arbitrary")),
    )(q, k, v)


def _attn_bwd(q, k, v, o, do, lse):
    B, S, HW = q.shape
    H = HW // HEAD_LANES
    t = min(ATTN_TILE, S)
    nq = S // t

    def body(q_ref, k_ref, v_ref, o_ref, do_ref, lse_ref, dq_ref, dk_ref, dv_ref):
        j = pl.program_id(2)

        @pl.when(j == 0)
        def _():
            dq_ref[...] = jnp.zeros_like(dq_ref)

        kj = k_ref[0]
        vj = v_ref[0]

        def step(i, carry, masked):
            dk, dv = carry
            sl = pl.ds(pl.multiple_of(i * t, t), t)
            qi = q_ref[0, sl, :]
            doi = do_ref[0, sl, :]
            s = _dot_nt(qi, kj)
            if masked:
                s = jnp.where(_causal_mask(t), s, -1e30)
            p = jnp.exp(s - lse_ref[0, 0, sl, :])
            dv = dv + _dot_tn(p.astype(BF16), doi)
            dp = _dot_nt(doi, vj)
            delta = jnp.sum(doi.astype(F32) * o_ref[0, sl, :].astype(F32), axis=-1, keepdims=True)
            dsb = (p * (dp - delta)).astype(BF16)
            dk = dk + _dot_tn(dsb, qi)
            dq_ref[0, sl, :] += _dot(dsb, kj)
            return dk, dv

        zero = jnp.zeros((t, HEAD_LANES), F32)
        carry = step(j, (zero, zero), True)
        dk, dv = lax.fori_loop(j + 1, nq, lambda i, c: step(i, c, False), carry)
        dk_ref[0] = dk
        dv_ref[0] = dv

    full = pl.BlockSpec((1, S, HEAD_LANES), lambda b, h, j: (b, 0, h))
    tile = pl.BlockSpec((1, t, HEAD_LANES), lambda b, h, j: (b, j, h))
    sd = jax.ShapeDtypeStruct
    return pl.pallas_call(
        body, grid=(B, H, nq), name="attn_bwd",
        in_specs=[full, tile, tile, full, full, pl.BlockSpec((1, 1, S, 1), lambda b, h, j: (b, h, 0, 0))],
        out_specs=[full, tile, tile],
        out_shape=[sd((B, S, HW), F32), sd((B, S, HW), F32), sd((B, S, HW), F32)],
        compiler_params=_cparams(("arbitrary", "arbitrary", "arbitrary")),
    )(q, k, v, o, do, lse)


def _mix_out(x1, yssd, o, mw, wout, g, seq):
    T, D = x1.shape
    tm = min(TOKEN_TILE, seq)
    tps = seq // tm

    def body(x_ref, ys_ref, o_ref, mw_ref, w_ref, g_ref, xo_ref, m_ref, ym_ref):
        xh, _ = _rms_stats(o_ref[...].astype(F32))
        ym = (xh * mw_ref[...]).astype(BF16)
        m = _dot(ys_ref[...], w_ref[:D_SSD, :]) + _dot(ym, w_ref[D_SSD:, :])
        xo_ref[...] = x_ref[...] + g_ref[0] * m
        m_ref[...] = m.astype(BF16)
        ym_ref[...] = ym

    rows = lambda n: pl.BlockSpec((tm, n), lambda i: (i, 0))
    perb = pl.BlockSpec((1, 1, D), lambda i: (i // tps, 0, 0))
    sd = jax.ShapeDtypeStruct
    return pl.pallas_call(
        body, grid=(T // tm,), name="mix_out",
        in_specs=[rows(D), rows(D_SSD), rows(D_MLA), _resident((1, D_MLA)), _resident((D_SSD + D_MLA, D)), perb],
        out_specs=[rows(D), rows(D), rows(D_MLA)],
        out_shape=[sd((T, D), F32), sd((T, D), BF16), sd((T, D_MLA), BF16)],
        compiler_params=_cparams(("arbitrary",)),
    )(x1, yssd, o, mw, wout, g)


def _mix_out_bwd(dx2, m, o, mw, wout, g, seq):
    T, D = dx2.shape
    B = T // seq
    tm = min(TOKEN_TILE, seq)
    tps = seq // tm

    def body(dx_ref, m_ref, o_ref, mw_ref, w_ref, g_ref, dys_ref, do_ref, dm_ref, dg_ref, dmw_ref):
        i = pl.program_id(0)

        @pl.when(i % tps == 0)
        def _():
            dg_ref[...] = jnp.zeros_like(dg_ref)

        @pl.when(i == 0)
        def _():
            dmw_ref[...] = jnp.zeros_like(dmw_ref)

        dxv = dx_ref[...]
        dg_ref[0] += _colsum(dxv * m_ref[...].astype(F32))
        dmb = (g_ref[0] * dxv).astype(BF16)
        dm_ref[...] = dmb
        dycat = _dot_nt(dmb, w_ref[...])
        dys_ref[...] = dycat[:, :D_SSD].astype(BF16)
        xh, r = _rms_stats(o_ref[...].astype(F32))
        dx, dw_rows = _rms_bwd(dycat[:, D_SSD:], xh, r, mw_ref[...])
        do_ref[...] = dx.astype(BF16)
        dmw_ref[...] += _colsum(dw_rows)

    rows = lambda n: pl.BlockSpec((tm, n), lambda i: (i, 0))
    perb = pl.BlockSpec((1, 1, D), lambda i: (i // tps, 0, 0))
    sd = jax.ShapeDtypeStruct
    return pl.pallas_call(
        body, grid=(T // tm,), name="mix_out_bwd",
        in_specs=[rows(D), rows(D), rows(D_MLA), _resident((1, D_MLA)), _resident((D_SSD + D_MLA, D)), perb],
        out_specs=[rows(D_SSD), rows(D_MLA), rows(D), perb, pl.BlockSpec((1, D_MLA), lambda i: (0, 0))],
        out_shape=[sd((T, D_SSD), BF16), sd((T, D_MLA), BF16), sd((T, D), BF16), sd((B, 1, D), F32), sd((1, D_MLA), F32)],
        compiler_params=_cparams(("arbitrary",)),
    )(dx2, m, o, mw, wout, g)


def _win_to_kernel(w):
    z0 = jnp.zeros((w.shape[0], 48), w.dtype)
    z1 = jnp.zeros((w.shape[0], 32), w.dtype)
    return jnp.concatenate([w[:, :2560], w[:, 2576:3216], w[:, 2560:2576], z0, w[:, 3216:3248], z1], axis=1)


def _win_from_kernel(g):
    return jnp.concatenate([g[:, :2560], g[:, 3200:3216], g[:, 2560:3200], g[:, 3264:3296]], axis=1)


def _wuq_to_kernel(w):
    w = w.reshape(Q_LORA, MLA_HEADS, QK_DIM)
    return jnp.pad(w, ((0, 0), (0, 0), (0, HEAD_LANES - QK_DIM))).reshape(Q_LORA, MLA_HEADS * HEAD_LANES)


def _wuq_from_kernel(g):
    return g.reshape(Q_LORA, MLA_HEADS, HEAD_LANES)[:, :, :QK_DIM].reshape(Q_LORA, MLA_HEADS * QK_DIM)


def _wukv_to_kernel(w):
    w = w.reshape(KV_LORA, MLA_HEADS, QK_NOPE + V_HEAD)
    kp = jnp.pad(w[:, :, :QK_NOPE], ((0, 0), (0, 0), (0, HEAD_LANES - QK_NOPE)))
    return jnp.concatenate([kp.reshape(KV_LORA, -1), w[:, :, QK_NOPE:].reshape(KV_LORA, -1)], axis=1)


def _wukv_from_kernel(g):
    hw = MLA_HEADS * HEAD_LANES
    kp = g[:, :hw].reshape(KV_LORA, MLA_HEADS, HEAD_LANES)[:, :, :QK_NOPE]
    vp = g[:, hw:].reshape(KV_LORA, MLA_HEADS, V_HEAD)
    return jnp.concatenate([kp, vp], axis=2).reshape(KV_LORA, MLA_HEADS * (QK_NOPE + V_HEAD))


def _lanes16(v):
    return jnp.pad(v.reshape(1, SSD_HEADS), ((0, 0), (0, LANES - SSD_HEADS)))


def _constants():
    e = np.zeros((LANES, D_SSD), np.float32)
    for h in range(SSD_HEADS):
        e[h, h * SSD_HEAD_DIM:(h + 1) * SSD_HEAD_DIM] = 1.0
    inv_freq = ROPE_THETA ** (-jnp.arange(0, QK_ROPE, 2, dtype=F32) / QK_ROPE)
    half = QK_ROPE // 2
    invf = jnp.zeros((1, LANES), F32).at[0, QK_NOPE:QK_NOPE + half].set(inv_freq).at[0, QK_NOPE + half:QK_DIM].set(inv_freq)
    return jnp.asarray(e), invf


def _local_step(x, positions, mod, w, small, tgt):
    B, S, D = x.shape
    T = B * S
    expand, invf = _constants()
    x0 = x.reshape(T, D)
    pos = positions.reshape(T, 1)
    mods = [mod[:, i * D:(i + 1) * D].reshape(B, 1, D) for i in range(N_MOD)]
    sh1, sc1, g1, sh2, sc2, g2, sh3, sc3, g3 = mods
    dtb, alog, dsk = _lanes16(small["dt_bias"]), _lanes16(small["a_log"]), _lanes16(small["d_skip"])

    x1, a1, u1, f1 = _ffn_fwd(x0, small["norm_ffn1"], sh1, sc1, g1, w["ffn1_w_gate"], w["ffn1_w_up"], w["ffn1_w_down"], S, "ffn1_fwd")
    z, xraw, cq, ckv, dtk = _inproj_fwd(x1, small["norm_mix"], sh2, sc2, w["w_in"], S)
    xraw3 = xraw.reshape(B, S, D_CONV)
    xbc = _conv_fwd(xraw3, small["conv_w"], small["conv_b"])
    dtk3, z3 = dtk.reshape(B, S, LANES), z.reshape(B, S, D_SSD)
    y, yssd, prev = _ssd_fwd(xbc, dtk3, z3, dtb, alog, dsk, small["ssd_norm_w"], expand)
    q, k, v = _mla_prep(cq, ckv, dtk, pos, small["q_norm_w"], small["kv_norm_w"], w["w_uq"], w["w_ukv"], invf)
    hw = MLA_HEADS * HEAD_LANES
    q3, k3, v3 = q.reshape(B, S, hw), k.reshape(B, S, hw), v.reshape(B, S, hw)
    o3, lse = _attn_fwd(q3, k3, v3)
    o = o3.reshape(T, hw)
    x2, m, ym = _mix_out(x1, yssd.reshape(T, D_SSD), o, small["mla_norm_w"], w["w_out"], g2, S)
    x3, a2, u2, f2 = _ffn_fwd(x2, small["norm_ffn2"], sh3, sc3, g3, w["ffn2_w_gate"], w["ffn2_w_up"], w["ffn2_w_down"], S, "ffn2_fwd")
    dx3, loss, d_norm_final = _final_loss(x3, small["norm_final"].reshape(1, D), tgt.reshape(T, D))

    gw, gs = {}, {}
    dx2, h3, s3, df3, da3, du3, dsh3, dsc3, dg3, gs["norm_ffn2"] = _ffn_bwd(
        dx3, x2, small["norm_ffn2"], sh3, sc3, g3, a2, u2, f2, w["ffn2_w_gate"], w["ffn2_w_up"], w["ffn2_w_down"], S, "ffn2_bwd")
    gw["ffn2_w_gate"] = _mm_tn(h3, da3, 1408, "ffn2_dgate")
    gw["ffn2_w_up"] = _mm_tn(h3, du3, 1408, "ffn2_dup")
    gw["ffn2_w_down"] = _mm_tn(s3, df3, 512, "ffn2_ddown")

    dys, do, dm, dg2, gs["mla_norm_w"] = _mix_out_bwd(dx2, m, o, small["mla_norm_w"], w["w_out"], g2, S)
    gw["w_out"] = jnp.concatenate([_mm_tn(yssd.reshape(T, D_SSD), dm, 512, "dwout_ssd"), _mm_tn(ym, dm, 512, "dwout_mla")], axis=0)

    dq3, dk3, dv3 = _attn_bwd(q3, k3, v3, o3, do.reshape(B, S, hw), lse)
    dcq, dckv, ddtk_b, qn, kvn, dqb, dkvb, gs["q_norm_w"], gs["kv_norm_w"] = _mla_prep_bwd(
        dq3.reshape(T, hw), dk3.reshape(T, hw), dv3.reshape(T, hw), cq, ckv, pos, small["q_norm_w"], small["kv_norm_w"],
        w["w_uq"], w["w_ukv"], invf)
    gw["w_uq"] = _mm_tn(qn, dqb, 512, "dwuq")
    gw["w_ukv"] = _mm_tn(kvn, dkvb, 1024, "dwukv")

    dxbc, ddtk_a, dz, gs["ssd_norm_w"], dvec = _ssd_bwd(
        xbc, dtk3, z3, y, prev, dys.reshape(B, S, D_SSD), dtb, alog, dsk, small["ssd_norm_w"], expand)
    gs["dt_bias"], gs["a_log"], gs["d_skip"] = dvec[0:1, :SSD_HEADS], dvec[1:2, :SSD_HEADS], dvec[2:3, :SSD_HEADS]
    dxraw, gs["conv_w"], gs["conv_b"] = _conv_bwd(dxbc, xraw3, small["conv_w"], small["conv_b"])
    dx1, h2, dproj, dsh2, dsc2, gs["norm_mix"] = _inproj_bwd(
        dx2, x1, small["norm_mix"], sh2, sc2, w["w_in"], dz.reshape(T, D_SSD), dxraw.reshape(T, D_CONV), dcq, dckv,
        ddtk_a.reshape(T, LANES), ddtk_b, S)
    gw["w_in"] = _mm_tn(h2, dproj, 1664, "dwin")

    dx0, h1, s1, df1, da1, du1, dsh1, dsc1, dg1, gs["norm_ffn1"] = _ffn_bwd(
        dx1, x0, small["norm_ffn1"], sh1, sc1, g1, a1, u1, f1, w["ffn1_w_gate"], w["ffn1_w_up"], w["ffn1_w_down"], S, "ffn1_bwd")
    gw["ffn1_w_gate"] = _mm_tn(h1, da1, 1408, "ffn1_dgate")
    gw["ffn1_w_up"] = _mm_tn(h1, du1, 1408, "ffn1_dup")
    gw["ffn1_w_down"] = _mm_tn(s1, df1, 512, "ffn1_ddown")
    gs["norm_final"] = d_norm_final
    dmod = jnp.concatenate([t.reshape(B, D) for t in (dsh1, dsc1, dg1, dsh2, dsc2, dg2, dsh3, dsc3, dg3)], axis=1)
    return loss, dx0.reshape(B, S, D), gw, dmod, gs


HBM_SPEC = pl.BlockSpec(memory_space=pltpu.HBM)
VMEM_SPEC = pl.BlockSpec(memory_space=pltpu.VMEM)


def _place():
    return lax.axis_index("x"), lax.axis_index("y"), lax.axis_index("c")


def _other_chips(mx, my):
    return [(1 - mx, my), (mx, 1 - my), (1 - mx, 1 - my)]


def _remote(src, dst, send_sem, recv_sem, to):
    return pltpu.make_async_remote_copy(src_ref=src, dst_ref=dst, send_sem=send_sem, recv_sem=recv_sem,
                                        device_id=to, device_id_type=MESH)


def _all_gather_small(xa, name):
    r, n = xa.shape

    def body(x_ref, o_ref, send_sems, recv_sems):
        mx, my, mc = _place()
        me = 4 * mx + 2 * my + mc
        o_ref[pl.ds(me, 1)] = x_ref[...][None]
        sends = []
        for k in range(1, N_DEV):
            peer = (mx ^ (k >> 2), my ^ ((k >> 1) & 1), mc ^ (k & 1))
            cp = _remote(x_ref, o_ref.at[me], send_sems.at[k - 1], recv_sems.at[k - 1], peer)
            cp.start()
            sends.append(cp)
        for k in range(1, N_DEV):
            peer = (mx ^ (k >> 2), my ^ ((k >> 1) & 1), mc ^ (k & 1))
            slot = 4 * peer[0] + 2 * peer[1] + peer[2]
            _remote(x_ref, o_ref.at[slot], send_sems.at[k - 1], recv_sems.at[k - 1], peer).wait_recv()
        for cp in sends:
            cp.wait_send()

    return pl.pallas_call(
        body, name=name, in_specs=[VMEM_SPEC], out_specs=VMEM_SPEC,
        out_shape=jax.ShapeDtypeStruct((N_DEV, r, n), xa.dtype),
        scratch_shapes=[pltpu.SemaphoreType.DMA((N_DEV - 1,)), pltpu.SemaphoreType.DMA((N_DEV - 1,))],
        compiler_params=pltpu.CompilerParams(vmem_limit_bytes=VMEM_LIMIT),
    )(xa)


def _gather_weights(flat):
    R = flat.shape[0]
    half = R // 2

    def body(w_ref, o_ref, send_sems, recv_sems, local_sem):
        mx, my, mc = _place()
        chip = 2 * mx + my
        others = _other_chips(mx, my)
        sibling = (mx, my, 1 - mc)

        def blk(cx, cy, hc):
            return o_ref.at[2 * cx + cy, pl.ds(pl.multiple_of(hc * half, 16), half), :]

        mine = pltpu.make_async_copy(w_ref, o_ref.at[chip], local_sem)
        mine.start()
        my_half = w_ref.at[pl.ds(pl.multiple_of(mc * half, 16), half), :]
        first = [_remote(my_half, blk(mx, my, mc), send_sems.at[k], recv_sems.at[k], (cx, cy, mc))
                 for k, (cx, cy) in enumerate(others)]
        for cp in first:
            cp.start()
        passed = [_remote(blk(cx, cy, mc), blk(cx, cy, mc), send_sems.at[3 + k], recv_sems.at[3 + k], sibling)
                  for k, (cx, cy) in enumerate(others)]
        for k, (cx, cy) in enumerate(others):
            _remote(my_half, blk(cx, cy, mc), send_sems.at[k], recv_sems.at[k], (cx, cy, mc)).wait_recv()
            passed[k].start()
        for k, (cx, cy) in enumerate(others):
            _remote(my_half, blk(cx, cy, 1 - mc), send_sems.at[3 + k], recv_sems.at[3 + k], sibling).wait_recv()
        for cp in first + passed:
            cp.wait_send()
        mine.wait()

    return pl.pallas_call(
        body, name="gather_weights", in_specs=[HBM_SPEC], out_specs=HBM_SPEC,
        out_shape=jax.ShapeDtypeStruct((N_CHIPS, R, LANES), flat.dtype),
        scratch_shapes=[pltpu.SemaphoreType.DMA((6,)), pltpu.SemaphoreType.DMA((6,)), pltpu.SemaphoreType.DMA],
    )(flat)


def _swap_halves(g):
    _, R, _ = g.shape
    half = R // 2

    def body(g_ref, o_ref, send_sem, recv_sem):
        mx, my, mc = _place()
        src = g_ref.at[:, pl.ds(pl.multiple_of((1 - mc) * half, 16), half), :]
        cp = _remote(src, o_ref, send_sem, recv_sem, (mx, my, 1 - mc))
        cp.start()
        cp.wait()

    return pl.pallas_call(
        body, name="swap_halves", in_specs=[HBM_SPEC], out_specs=HBM_SPEC,
        out_shape=jax.ShapeDtypeStruct((N_CHIPS, half, LANES), g.dtype),
        scratch_shapes=[pltpu.SemaphoreType.DMA, pltpu.SemaphoreType.DMA],
    )(g)


def _row_block(rows, cap):
    best = 16
    for b in range(16, cap + 1, 16):
        if rows % b == 0:
            best = b
    return best


def _pair_sum(g, got, core):
    _, R, _ = g.shape
    half = R // 2
    rb = _row_block(half, 2880)
    nb = half // rb

    def body(core_ref, g_ref, got_ref, o_ref):
        o_ref[...] = (g_ref[...] + got_ref[...]).astype(BF16)

    return pl.pallas_call(
        body, name="pair_sum",
        grid_spec=pltpu.PrefetchScalarGridSpec(
            num_scalar_prefetch=1, grid=(N_CHIPS, nb),
            in_specs=[pl.BlockSpec((1, rb, LANES), lambda j, r, core_ref: (j, core_ref[0] * nb + r, 0)),
                      pl.BlockSpec((1, rb, LANES), lambda j, r, core_ref: (j, r, 0))],
            out_specs=pl.BlockSpec((1, rb, LANES), lambda j, r, core_ref: (j, r, 0))),
        out_shape=jax.ShapeDtypeStruct((N_CHIPS, half, LANES), BF16),
        compiler_params=_cparams(("arbitrary", "arbitrary")),
    )(core, g, got)


def _scatter_chips(s):
    def body(s_ref, o_ref, send_sems, recv_sems, local_sem):
        mx, my, mc = _place()
        chip = 2 * mx + my
        others = _other_chips(mx, my)
        mine = pltpu.make_async_copy(s_ref.at[chip], o_ref.at[chip], local_sem)
        mine.start()
        sends = [_remote(s_ref.at[2 * cx + cy], o_ref.at[chip], send_sems.at[k], recv_sems.at[k], (cx, cy, mc))
                 for k, (cx, cy) in enumerate(others)]
        for cp in sends:
            cp.start()
        for k, (cx, cy) in enumerate(others):
            _remote(s_ref.at[chip], o_ref.at[2 * cx + cy], send_sems.at[k], recv_sems.at[k], (cx, cy, mc)).wait_recv()
        for cp in sends:
            cp.wait_send()
        mine.wait()

    return pl.pallas_call(
        body, name="scatter_chips", in_specs=[HBM_SPEC], out_specs=HBM_SPEC,
        out_shape=jax.ShapeDtypeStruct(s.shape, s.dtype),
        scratch_shapes=[pltpu.SemaphoreType.DMA((3,)), pltpu.SemaphoreType.DMA((3,)), pltpu.SemaphoreType.DMA],
    )(s)


def _chip_sum(parts):
    _, h, _ = parts.shape
    rb = _row_block(h, 2880)

    def body(p_ref, o_ref):
        acc = p_ref[0].astype(F32)
        for j in range(1, N_CHIPS):
            acc = acc + p_ref[j].astype(F32)
        o_ref[...] = acc

    return pl.pallas_call(
        body, name="chip_sum", grid=(h // rb,),
        in_specs=[pl.BlockSpec((N_CHIPS, rb, LANES), lambda r: (0, r, 0))],
        out_specs=pl.BlockSpec((rb, LANES), lambda r: (r, 0)),
        out_shape=jax.ShapeDtypeStruct((h, LANES), F32),
        compiler_params=_cparams(("arbitrary",)),
    )(parts)


def _join_halves(mine):
    h = mine.shape[0]

    def body(m_ref, o_ref, send_sem, recv_sem, local_sem):
        mx, my, mc = _place()
        rows = lambda hc: o_ref.at[pl.ds(pl.multiple_of(hc * h, 16), h), :]
        own = pltpu.make_async_copy(m_ref, rows(mc), local_sem)
        own.start()
        cp = _remote(m_ref, rows(mc), send_sem, recv_sem, (mx, my, 1 - mc))
        cp.start()
        _remote(m_ref, rows(1 - mc), send_sem, recv_sem, (mx, my, 1 - mc)).wait_recv()
        cp.wait_send()
        own.wait()

    return pl.pallas_call(
        body, name="join_halves", in_specs=[HBM_SPEC], out_specs=HBM_SPEC,
        out_shape=jax.ShapeDtypeStruct((2 * h, LANES), mine.dtype),
        scratch_shapes=[pltpu.SemaphoreType.DMA, pltpu.SemaphoreType.DMA, pltpu.SemaphoreType.DMA],
    )(mine)


def _adam_math(w, g, m, v):
    m2 = ADAM_B1 * m + (1.0 - ADAM_B1) * g
    v2 = ADAM_B2 * v + (1.0 - ADAM_B2) * (g * g)
    m_hat = m2 * (1.0 / (1.0 - ADAM_B1 ** ADAM_STEP))
    v_hat = v2 * (1.0 / (1.0 - ADAM_B2 ** ADAM_STEP))
    delta = -ADAM_LR * (m_hat / (jnp.sqrt(v_hat) + ADAM_EPS) + ADAM_WD * w)
    return delta, m2, v2


def _adam(w, g, m, v, name):
    rows = w.shape[0]
    rb = _row_block(rows, 2880)

    def body(w_ref, g_ref, m_ref, v_ref, d_ref, m2_ref, v2_ref):
        d_ref[...], m2_ref[...], v2_ref[...] = _adam_math(w_ref[...], g_ref[...], m_ref[...], v_ref[...])

    blk = pl.BlockSpec((rb, LANES), lambda r: (r, 0))
    return pl.pallas_call(
        body, name=name, grid=(rows // rb,), in_specs=[blk] * 4, out_specs=[blk] * 3,
        out_shape=[jax.ShapeDtypeStruct(w.shape, F32)] * 3, compiler_params=_cparams(("arbitrary",)),
    )(w, g, m, v)


ADA_COLS = N_MOD * D_MODEL // N_CHIPS


def _ada_fwd(c_all, w_ada, b_cols):
    def body(c_ref, w_ref, b_ref, o_ref):
        cv = c_ref[...]
        act = (cv * _sigmoid(cv)).astype(BF16)
        o_ref[...] = _dot(act, w_ref[...].astype(BF16)) + b_ref[...]

    return pl.pallas_call(
        body, name="ada_fwd", out_shape=jax.ShapeDtypeStruct((c_all.shape[0], ADA_COLS), F32),
        compiler_params=pltpu.CompilerParams(vmem_limit_bytes=VMEM_LIMIT),
    )(c_all, w_ada, b_cols)


def _ada_bwd(c_all, dmod_cols, w, m, v):
    nb = c_all.shape[0]
    tn = 384

    def body(c_ref, d_ref, w_ref, m_ref, v_ref, g_ref, dl_ref, m2_ref, v2_ref):
        cv = c_ref[...]
        act = (cv * _sigmoid(cv)).astype(BF16)
        g = _dot_tn(act, d_ref[...].astype(BF16))
        g_ref[...] = g
        dl_ref[...], m2_ref[...], v2_ref[...] = _adam_math(w_ref[...], g, m_ref[...], v_ref[...])

    blk = pl.BlockSpec((D_MODEL, tn), lambda j: (0, j))
    return pl.pallas_call(
        body, name="ada_bwd", grid=(ADA_COLS // tn,),
        in_specs=[pl.BlockSpec((nb, D_MODEL), lambda j: (0, 0)), pl.BlockSpec((nb, tn), lambda j: (0, j)), blk, blk, blk],
        out_specs=[blk] * 4, out_shape=[jax.ShapeDtypeStruct((D_MODEL, ADA_COLS), F32)] * 4,
        compiler_params=_cparams(("arbitrary",)),
    )(c_all, dmod_cols, w, m, v)


SMALL_NAMES = ("norm_ffn1", "norm_mix", "conv_w", "conv_b", "ssd_norm_w", "q_norm_w", "kv_norm_w", "mla_norm_w",
               "norm_ffn2", "norm_final", "dt_bias", "a_log", "d_skip")
SMALL_SIZES = (1024, 1024, CONV_WIDTH * D_CONV, D_CONV, 1024, Q_LORA, KV_LORA, 1024, 1024, 1024, 16, 16, 16)
SMALL_ROWS = 16
MOD_ROWS = 2 * N_MOD
SEND_ROWS = 40


def _pack_small(parts):
    flat = jnp.concatenate([parts[n].reshape(-1) for n in SMALL_NAMES])
    return jnp.pad(flat, (0, SMALL_ROWS * D_MODEL - flat.shape[0]))


def _unpack_small(flat):
    out, off = {}, 0
    for n, size in zip(SMALL_NAMES, SMALL_SIZES):
        out[n] = flat[off:off + size]
        off += size
    return out


def _small_sum(got):
    def body(g_ref, o_ref):
        bsum = jnp.zeros((N_MOD, D_MODEL), F32)
        ssum = jnp.zeros((SMALL_ROWS, D_MODEL), F32)
        for d in range(N_DEV):
            bsum = bsum + g_ref[d, 0:N_MOD, :] + g_ref[d, N_MOD:MOD_ROWS, :]
            ssum = ssum + g_ref[d, MOD_ROWS:MOD_ROWS + SMALL_ROWS, :]
        o_ref[...] = jnp.concatenate([bsum, ssum, jnp.zeros((32 - N_MOD - SMALL_ROWS, D_MODEL), F32)], axis=0)

    return pl.pallas_call(body, name="small_sum", out_shape=jax.ShapeDtypeStruct((32, D_MODEL), F32))(got)


_BIG = (("ffn1_w_gate", "col"), ("ffn1_w_up", "col"), ("ffn1_w_down", "row"), ("w_in", "col"), ("w_uq", "col"),
        ("w_ukv", "col"), ("w_out", "row"), ("ffn2_w_gate", "col"), ("ffn2_w_up", "col"), ("ffn2_w_down", "row"))


def _flatten_shards(shards, dtype):
    flat = jnp.concatenate([shards[n].astype(dtype).reshape(-1) for n, _ in _BIG])
    return jnp.pad(flat, (0, FLAT_ROWS * LANES - flat.shape[0])).reshape(FLAT_ROWS, LANES)


def _whole_from_gathered(gathered, shapes):
    flat = gathered.reshape(N_CHIPS, FLAT_ROWS * LANES)
    out, off = {}, 0
    for n, kind in _BIG:
        r, ccols = shapes[n]
        part = flat[:, off:off + r * ccols].reshape(N_CHIPS, r, ccols)
        off += r * ccols
        out[n] = part.reshape(N_CHIPS * r, ccols) if kind == "row" else part.transpose(1, 0, 2).reshape(r, N_CHIPS * ccols)
    return out


def _shard_major(grads, shapes):
    parts = []
    for n, kind in _BIG:
        r, ccols = shapes[n]
        g = grads[n]
        g = g.reshape(N_CHIPS, r * ccols) if kind == "row" else g.reshape(r, N_CHIPS, ccols).transpose(1, 0, 2).reshape(N_CHIPS, r * ccols)
        parts.append(g)
    flat = jnp.concatenate(parts, axis=1)
    return jnp.pad(flat, ((0, 0), (0, FLAT_ROWS * LANES - flat.shape[1]))).reshape(N_CHIPS, FLAT_ROWS, LANES)


def _split_flat(flat, shapes):
    flat = flat.reshape(-1)
    out, off = {}, 0
    for n, _ in _BIG:
        r, ccols = shapes[n]
        out[n] = flat[off:off + r * ccols].reshape(1, r, ccols)
        off += r * ccols
    return out


def kernel(x, c, positions, w_ada, b_ada, norm_ffn1, ffn1_w_gate, ffn1_w_up, ffn1_w_down, norm_mix, w_in, conv_w, conv_b, dt_bias, a_log, d_skip, ssd_norm_w, q_norm_w, w_uq, kv_norm_w, w_ukv, mla_norm_w, w_out, norm_ffn2, ffn2_w_gate, ffn2_w_up, ffn2_w_down, norm_final, loss_target, m_w_ada, m_b_ada, m_norm_ffn1, m_ffn1_w_gate, m_ffn1_w_up, m_ffn1_w_down, m_norm_mix, m_w_in, m_conv_w, m_conv_b, m_dt_bias, m_a_log, m_d_skip, m_ssd_norm_w, m_q_norm_w, m_w_uq, m_kv_norm_w, m_w_ukv, m_mla_norm_w, m_w_out, m_norm_ffn2, m_ffn2_w_gate, m_ffn2_w_up, m_ffn2_w_down, m_norm_final, v_w_ada, v_b_ada, v_norm_ffn1, v_ffn1_w_gate, v_ffn1_w_up, v_ffn1_w_down, v_norm_mix, v_w_in, v_conv_w, v_conv_b, v_dt_bias, v_a_log, v_d_skip, v_ssd_norm_w, v_q_norm_w, v_w_uq, v_kv_norm_w, v_w_ukv, v_mla_norm_w, v_w_out, v_norm_ffn2, v_ffn2_w_gate, v_ffn2_w_up, v_ffn2_w_down, v_norm_final):
    a = dict(locals())
    B, S, D = x.shape
    mx, my, mc = _place()
    chip = 2 * mx + my
    dev = 2 * chip + mc
    big = [n for n, _ in _BIG]
    shapes = {n: a[n].shape[1:] for n in big}

    gathered = _gather_weights(_flatten_shards({n: a[n] for n in big}, BF16))
    w = _whole_from_gathered(gathered, shapes)
    w["w_in"], w["w_uq"], w["w_ukv"] = _win_to_kernel(w["w_in"]), _wuq_to_kernel(w["w_uq"]), _wukv_to_kernel(w["w_ukv"])

    cw_rows = jnp.pad(conv_w[0], ((0, 0), (0, D - conv_w.shape[2])))
    got = _all_gather_small(jnp.concatenate([c, cw_rows, jnp.zeros((8 - B - CONV_WIDTH, D), F32)], axis=0), "gather_c")
    c_all = got[:, :B, :].reshape(N_DEV * B, D)
    conv_full = got[::2, B:B + CONV_WIDTH, :conv_w.shape[2]].transpose(1, 0, 2).reshape(CONV_WIDTH, D_CONV)

    b_cols = lax.dynamic_slice(b_ada, (0, chip * ADA_COLS), (1, ADA_COLS))
    mod_all = _all_gather_small(_ada_fwd(c_all, w_ada[0], b_cols), "gather_mod")
    mod = lax.dynamic_slice(mod_all, (0, B * dev, 0), (N_DEV, B, ADA_COLS))[::2].transpose(1, 0, 2).reshape(B, N_MOD * D)

    small = {n: a[n].reshape(1, -1) for n in SMALL_NAMES if n not in ("conv_w", "norm_final")}
    small["conv_w"], small["norm_final"] = conv_full, norm_final
    loss_blk, grad_x, gw, dmod, gs = _local_step(x, positions, mod, w, small, loss_target)

    gw["w_in"], gw["w_uq"], gw["w_ukv"] = _win_from_kernel(gw["w_in"]), _wuq_from_kernel(gw["w_uq"]), _wukv_from_kernel(gw["w_ukv"])
    gflat = _shard_major(gw, shapes)
    core = mc.astype(jnp.int32).reshape(1)
    chip_part = _pair_sum(gflat, _swap_halves(gflat), core)
    g_shard = _join_halves(_chip_sum(_scatter_chips(chip_part)))
    w_f = _flatten_shards({n: a[n] for n in big}, F32)
    m_f = _flatten_shards({n: a["m_" + n] for n in big}, F32)
    v_f = _flatten_shards({n: a["v_" + n] for n in big}, F32)
    d_f, m2_f, v2_f = _adam(w_f, g_shard, m_f, v_f, "adam_large")
    grads, deltas, new_m, new_v = (_split_flat(t, shapes) for t in (g_shard, d_f, m2_f, v2_f))

    small_flat = _pack_small(gs).at[-1].set(loss_blk[0, 0])
    send = jnp.concatenate([dmod.reshape(MOD_ROWS, D), small_flat.reshape(SMALL_ROWS, D),
                            jnp.zeros((SEND_ROWS - MOD_ROWS - SMALL_ROWS, D), F32)], axis=0)
    got = _all_gather_small(send, "gather_small")
    summed = _small_sum(got)
    sums = summed[N_MOD:N_MOD + SMALL_ROWS].reshape(-1)
    loss = sums[-1]
    gsmall = _unpack_small(sums)
    gsmall["conv_w"] = lax.dynamic_slice(gsmall["conv_w"].reshape(CONV_WIDTH, D_CONV), (0, chip * conv_w.shape[2]),
                                         (CONV_WIDTH, conv_w.shape[2]))
    gsmall["b_ada"] = summed[:N_MOD]
    names = ("b_ada",) + SMALL_NAMES
    rows = 208

    def pack(parts):
        flat = jnp.concatenate([parts[n].reshape(-1) for n in names])
        return jnp.pad(flat, (0, rows * LANES - flat.shape[0])).reshape(rows, LANES)

    packed = [pack({n: a[p + n] for n in names}) for p in ("", "m_", "v_")]
    g_p = pack(gsmall)
    outs = (g_p,) + tuple(_adam(packed[0], g_p, packed[1], packed[2], "adam_small"))
    for dst, flat in zip((grads, deltas, new_m, new_v), outs):
        flat, off = flat.reshape(-1), 0
        for n in names:
            dst[n] = flat[off:off + a[n].size].reshape(a[n].shape)
            off += a[n].size

    dmod_all = got[:, :MOD_ROWS, :].reshape(N_DEV * B, N_MOD * D)
    dmod_cols = lax.dynamic_slice(dmod_all, (0, chip * ADA_COLS), (N_DEV * B, ADA_COLS))
    ada = _ada_bwd(c_all, dmod_cols, w_ada[0], m_w_ada[0], v_w_ada[0])
    for dst, t in zip((grads, deltas, new_m, new_v), ada):
        dst["w_ada"] = t[None]

    order = ("w_ada", "b_ada", "norm_ffn1", "ffn1_w_gate", "ffn1_w_up", "ffn1_w_down", "norm_mix", "w_in", "conv_w", "conv_b",
             "dt_bias", "a_log", "d_skip", "ssd_norm_w", "q_norm_w", "w_uq", "kv_norm_w", "w_ukv", "mla_norm_w", "w_out",
             "norm_ffn2", "ffn2_w_gate", "ffn2_w_up", "ffn2_w_down", "norm_final")
    return (loss, grad_x, *[grads[n] for n in order], *[deltas[n] for n in order], *[new_m[n] for n in order],
            *[new_v[n] for n in order])
```

```python
import functools
import math

import jax
import jax.numpy as jnp
import numpy as np
from jax import lax
from jax.experimental import pallas as pl
from jax.experimental.pallas import tpu as pltpu

F32 = jnp.float32
BF16 = jnp.bfloat16
HIGHEST = lax.Precision.HIGHEST

D_MODEL = 1024
D_FF = 2816
D_SSD = 1024
D_MLA = 1024
SSD_HEADS = 16
SSD_HEAD_DIM = 64
SSD_GROUPS = 2
SSD_STATE = 128
CONV_WIDTH = 4
CHUNK = 128
MLA_HEADS = 8
QK_NOPE = 64
QK_ROPE = 32
QK_DIM = QK_NOPE + QK_ROPE
V_HEAD = 128
Q_LORA = 384
KV_LORA = 256
ROPE_THETA = 10000.0
N_MOD = 9
EPS = 1e-6
D_CONV = D_SSD + 2 * SSD_GROUPS * SSD_STATE
D_PROJ = 3328
HEAD_LANES = 128
ADAM_LR = 0.001
ADAM_B1 = 0.9
ADAM_B2 = 0.999
ADAM_EPS = 1e-08
ADAM_WD = 0.01
ADAM_STEP = 10

LANES = 128
VMEM_LIMIT = 56 * 1024 * 1024
TOKEN_TILE = 512
ATTN_TILE = 512
N_CHIPS = 4
N_DEV = 8

MESH = pl.DeviceIdType.MESH


def _dot(a, b, precision=None):
    return jnp.dot(a, b, preferred_element_type=F32, precision=precision)


def _dot_nt(a, b, precision=None):
    return lax.dot_general(a, b, (((1,), (1,)), ((), ())), preferred_element_type=F32, precision=precision)


def _dot_tn(a, b, precision=None):
    return lax.dot_general(a, b, (((0,), (0,)), ((), ())), preferred_element_type=F32, precision=precision)


def _cparams(semantics):
    return pltpu.CompilerParams(dimension_semantics=semantics, vmem_limit_bytes=VMEM_LIMIT)


def _resident(shape):
    zeros = (0,) * len(shape)
    return pl.BlockSpec(shape, lambda *_: zeros, pipeline_mode=pl.Buffered(1))


def _sigmoid(x):
    return jax.nn.sigmoid(x)


def _rms_stats(x):
    r = lax.rsqrt(jnp.mean(x * x, axis=-1, keepdims=True) + EPS)
    return x * r, r


def _rms_bwd(dn, xh, r, w):
    dxh = dn * w
    dx = r * (dxh - xh * jnp.mean(dxh * xh, axis=-1, keepdims=True))
    return dx, dn * xh


def _colsum(v):
    return jnp.sum(v, axis=0, keepdims=True)


def _ffn_fwd(x, nw, sh, sc, g, wg, wu, wd, seq, name):
    T, D = x.shape
    fs = wg.shape[2]
    tm = min(TOKEN_TILE, seq)
    tps = seq // tm

    def body(x_ref, nw_ref, sh_ref, sc_ref, g_ref, wg_ref, wu_ref, wd_ref, xo_ref, a_ref, u_ref, f_ref):
        xv = x_ref[...]
        xh, _ = _rms_stats(xv)
        h = (xh * nw_ref[...]) * (1.0 + sc_ref[0]) + sh_ref[0]
        hb = h.astype(BF16)
        f = jnp.zeros((tm, D), F32)
        for j in range(N_CHIPS):
            a = _dot(hb, wg_ref[j])
            u = _dot(hb, wu_ref[j])
            a_ref[j] = a.astype(BF16)
            u_ref[j] = u.astype(BF16)
            f = f + _dot((a * _sigmoid(a) * u).astype(BF16), wd_ref[j])
        xo_ref[...] = xv + 0.5 * g_ref[0] * f
        f_ref[...] = f.astype(BF16)

    rows = lambda n: pl.BlockSpec((tm, n), lambda i: (i, 0))
    act = pl.BlockSpec((N_CHIPS, tm, fs), lambda i: (0, i, 0))
    perb = pl.BlockSpec((1, 1, D), lambda i: (i // tps, 0, 0))
    return pl.pallas_call(
        body, grid=(T // tm,), name=name,
        in_specs=[rows(D), _resident((1, D)), perb, perb, perb, _resident((N_CHIPS, D, fs)), _resident((N_CHIPS, D, fs)),
                  _resident((N_CHIPS, fs, D))],
        out_specs=[rows(D), act, act, rows(D)],
        out_shape=[jax.ShapeDtypeStruct((T, D), F32), jax.ShapeDtypeStruct((N_CHIPS, T, fs), BF16),
                   jax.ShapeDtypeStruct((N_CHIPS, T, fs), BF16), jax.ShapeDtypeStruct((T, D), BF16)],
        compiler_params=_cparams(("arbitrary",)),
    )(x, nw, sh, sc, g, wg, wu, wd)


def _ffn_bwd(dxo, x, nw, sh, sc, g, a, u, f, wg, wu, wd, seq, name):
    T, D = x.shape
    fs = wg.shape[2]
    B = T // seq
    tm = min(TOKEN_TILE // 2, seq)
    tps = seq // tm

    def body(dxo_ref, x_ref, nw_ref, sh_ref, sc_ref, g_ref, a_ref, u_ref, f_ref, wg_ref, wu_ref, wd_ref,
             dx_ref, h_ref, s_ref, df_ref, da_ref, du_ref, dsh_ref, dsc_ref, dg_ref, dnw_ref):
        i = pl.program_id(0)

        @pl.when(i % tps == 0)
        def _():
            dsh_ref[...] = jnp.zeros_like(dsh_ref)
            dsc_ref[...] = jnp.zeros_like(dsc_ref)
            dg_ref[...] = jnp.zeros_like(dg_ref)

        @pl.when(i == 0)
        def _():
            dnw_ref[...] = jnp.zeros_like(dnw_ref)

        dxo_v = dxo_ref[...]
        dfb = (0.5 * g_ref[0] * dxo_v).astype(BF16)
        dg_ref[0] += _colsum(0.5 * dxo_v * f_ref[...].astype(F32))
        dh = jnp.zeros((tm, D), F32)
        for j in range(N_CHIPS):
            ds = _dot_nt(dfb, wd_ref[j])
            av = a_ref[j].astype(F32)
            uv = u_ref[j].astype(F32)
            sig = _sigmoid(av)
            sil = av * sig
            dab = (ds * uv * (sig * (1.0 + av * (1.0 - sig)))).astype(BF16)
            dub = (ds * sil).astype(BF16)
            dh = dh + _dot_nt(dab, wg_ref[j]) + _dot_nt(dub, wu_ref[j])
            s_ref[j] = (sil * uv).astype(BF16)
            da_ref[j] = dab
            du_ref[j] = dub
        xv = x_ref[...]
        xh, r = _rms_stats(xv)
        nwv = nw_ref[...]
        n = xh * nwv
        scale1 = 1.0 + sc_ref[0]
        dsc_ref[0] += _colsum(dh * n)
        dsh_ref[0] += _colsum(dh)
        dx, dw_rows = _rms_bwd(dh * scale1, xh, r, nwv)
        dnw_ref[...] += _colsum(dw_rows)
        dx_ref[...] = dxo_v + dx
        h_ref[...] = (n * scale1 + sh_ref[0]).astype(BF16)
        df_ref[...] = dfb

    rows = lambda n: pl.BlockSpec((tm, n), lambda i: (i, 0))
    act = pl.BlockSpec((N_CHIPS, tm, fs), lambda i: (0, i, 0))
    perb = pl.BlockSpec((1, 1, D), lambda i: (i // tps, 0, 0))
    sd = jax.ShapeDtypeStruct
    return pl.pallas_call(
        body, grid=(T // tm,), name=name,
        in_specs=[rows(D), rows(D), _resident((1, D)), perb, perb, perb, act, act, rows(D),
                  _resident((N_CHIPS, D, fs)), _resident((N_CHIPS, D, fs)), _resident((N_CHIPS, fs, D))],
        out_specs=[rows(D), rows(D), act, rows(D), act, act, perb, perb, perb, pl.BlockSpec((1, D), lambda i: (0, 0))],
        out_shape=[sd((T, D), F32), sd((T, D), BF16), sd((N_CHIPS, T, fs), BF16), sd((T, D), BF16),
                   sd((N_CHIPS, T, fs), BF16), sd((N_CHIPS, T, fs), BF16), sd((B, 1, D), F32), sd((B, 1, D), F32),
                   sd((B, 1, D), F32), sd((1, D), F32)],
        compiler_params=_cparams(("arbitrary",)),
    )(dxo, x, nw, sh, sc, g, a, u, f, wg, wu, wd)


def _ffn_wgrad(h, s, df, da, du, name):
    T, D = h.shape
    fs = s.shape[2]
    tt = min(TOKEN_TILE, T)

    def body(h_ref, s_ref, df_ref, da_ref, du_ref, dgate_ref, dup_ref, ddown_ref):
        @pl.when(pl.program_id(1) == 0)
        def _():
            dgate_ref[...] = jnp.zeros_like(dgate_ref)
            dup_ref[...] = jnp.zeros_like(dup_ref)
            ddown_ref[...] = jnp.zeros_like(ddown_ref)

        hv = h_ref[...]
        dgate_ref[0] += _dot_tn(hv, da_ref[0])
        dup_ref[0] += _dot_tn(hv, du_ref[0])
        ddown_ref[0] += _dot_tn(s_ref[0], df_ref[...])

    rows = pl.BlockSpec((tt, D), lambda j, t: (t, 0))
    act = pl.BlockSpec((1, tt, fs), lambda j, t: (j, t, 0))
    wcol = pl.BlockSpec((1, D, fs), lambda j, t: (j, 0, 0))
    return pl.pallas_call(
        body, grid=(N_CHIPS, T // tt), name=name,
        in_specs=[rows, act, rows, act, act],
        out_specs=[wcol, wcol, pl.BlockSpec((1, fs, D), lambda j, t: (j, 0, 0))],
        out_shape=[jax.ShapeDtypeStruct((N_CHIPS, D, fs), F32), jax.ShapeDtypeStruct((N_CHIPS, D, fs), F32),
                   jax.ShapeDtypeStruct((N_CHIPS, fs, D), F32)],
        compiler_params=_cparams(("arbitrary", "arbitrary")),
    )(h, s, df, da, du)


def _mm_tn(xa, ya, tn, name):
    T, K = xa.shape
    N = ya.shape[1]
    tt = min(TOKEN_TILE, T)

    def body(x_ref, y_ref, o_ref):
        @pl.when(pl.program_id(1) == 0)
        def _():
            o_ref[...] = jnp.zeros_like(o_ref)

        o_ref[...] += _dot_tn(x_ref[...], y_ref[...])

    return pl.pallas_call(
        body, grid=(N // tn, T // tt), name=name,
        in_specs=[pl.BlockSpec((tt, K), lambda j, t: (t, 0)), pl.BlockSpec((tt, tn), lambda j, t: (t, j))],
        out_specs=pl.BlockSpec((K, tn), lambda j, t: (0, j)),
        out_shape=jax.ShapeDtypeStruct((K, N), F32),
        compiler_params=_cparams(("arbitrary", "arbitrary")),
    )(xa, ya)


def _final_loss(x, nw, tgt):
    T, D = x.shape
    tm = min(TOKEN_TILE, T)

    def body(x_ref, nw_ref, t_ref, dx_ref, loss_ref, dnw_ref):
        @pl.when(pl.program_id(0) == 0)
        def _():
            loss_ref[...] = jnp.zeros_like(loss_ref)
            dnw_ref[...] = jnp.zeros_like(dnw_ref)

        xv = x_ref[...]
        xh, r = _rms_stats(xv)
        nwv = nw_ref[...]
        err = xh * nwv - t_ref[...]
        loss_ref[...] += (0.5 / D) * jnp.sum(err * err)
        dx, dw_rows = _rms_bwd(err * (1.0 / D), xh, r, nwv)
        dx_ref[...] = dx
        dnw_ref[...] += _colsum(dw_rows)

    rows = pl.BlockSpec((tm, D), lambda i: (i, 0))
    return pl.pallas_call(
        body, grid=(T // tm,), name="final_loss",
        in_specs=[rows, _resident((1, D)), rows],
        out_specs=[rows, pl.BlockSpec((8, LANES), lambda i: (0, 0)), pl.BlockSpec((1, D), lambda i: (0, 0))],
        out_shape=[jax.ShapeDtypeStruct((T, D), F32), jax.ShapeDtypeStruct((8, LANES), F32),
                   jax.ShapeDtypeStruct((1, D), F32)],
        compiler_params=_cparams(("arbitrary",)),
    )(x, nw, tgt)


_PROJ_SPLITS = (0, 1024, 2560, 2944, 3200, 3328)


def _inproj_fwd(x, nw, sh, sc, win, seq):
    T, D = x.shape
    tm = min(TOKEN_TILE, seq)
    tps = seq // tm
    widths = [b - a for a, b in zip(_PROJ_SPLITS[:-1], _PROJ_SPLITS[1:])]
    dtypes = [BF16, BF16, F32, F32, F32]

    def body(x_ref, nw_ref, sh_ref, sc_ref, w_ref, *outs):
        xh, _ = _rms_stats(x_ref[...])
        h = (xh * nw_ref[...]) * (1.0 + sc_ref[0]) + sh_ref[0]
        proj = _dot(h.astype(BF16), w_ref[...])
        for o, lo, hi in zip(outs, _PROJ_SPLITS[:-1], _PROJ_SPLITS[1:]):
            o[...] = proj[:, lo:hi].astype(o.dtype)

    rows = lambda n: pl.BlockSpec((tm, n), lambda i: (i, 0))
    perb = pl.BlockSpec((1, 1, D), lambda i: (i // tps, 0, 0))
    return pl.pallas_call(
        body, grid=(T // tm,), name="inproj_fwd",
        in_specs=[rows(D), _resident((1, D)), perb, perb, _resident((D, D_PROJ))],
        out_specs=[rows(w) for w in widths],
        out_shape=[jax.ShapeDtypeStruct((T, w), dt) for w, dt in zip(widths, dtypes)],
        compiler_params=_cparams(("arbitrary",)),
    )(x, nw, sh, sc, win)


def _inproj_bwd(dx2, x, nw, sh, sc, win, dz, dxbc, dcq, dckv, ddtk_a, ddtk_b, seq):
    T, D = x.shape
    B = T // seq
    tm = min(TOKEN_TILE, seq)
    tps = seq // tm

    def body(dx2_ref, x_ref, nw_ref, sh_ref, sc_ref, w_ref, dz_ref, dxbc_ref, dcq_ref, dckv_ref, da_ref, db_ref,
             dx_ref, h_ref, dp_ref, dsh_ref, dsc_ref, dnw_ref):
        i = pl.program_id(0)

        @pl.when(i % tps == 0)
        def _():
            dsh_ref[...] = jnp.zeros_like(dsh_ref)
            dsc_ref[...] = jnp.zeros_like(dsc_ref)

        @pl.when(i == 0)
        def _():
            dnw_ref[...] = jnp.zeros_like(dnw_ref)

        dproj = jnp.concatenate(
            [dz_ref[...], dxbc_ref[...], dcq_ref[...].astype(BF16), dckv_ref[...].astype(BF16),
             (da_ref[...] + db_ref[...]).astype(BF16)], axis=1)
        dp_ref[...] = dproj
        dh = _dot_nt(dproj, w_ref[...])
        xh, r = _rms_stats(x_ref[...])
        nwv = nw_ref[...]
        n = xh * nwv
        scale1 = 1.0 + sc_ref[0]
        dsc_ref[0] += _colsum(dh * n)
        dsh_ref[0] += _colsum(dh)
        dx, dw_rows = _rms_bwd(dh * scale1, xh, r, nwv)
        dnw_ref[...] += _colsum(dw_rows)
        dx_ref[...] = dx2_ref[...] + dx
        h_ref[...] = (n * scale1 + sh_ref[0]).astype(BF16)

    rows = lambda n: pl.BlockSpec((tm, n), lambda i: (i, 0))
    perb = pl.BlockSpec((1, 1, D), lambda i: (i // tps, 0, 0))
    sd = jax.ShapeDtypeStruct
    return pl.pallas_call(
        body, grid=(T // tm,), name="inproj_bwd",
        in_specs=[rows(D), rows(D), _resident((1, D)), perb, perb, _resident((D, D_PROJ)),
                  rows(1024), rows(D_CONV), rows(Q_LORA), rows(KV_LORA), rows(LANES), rows(LANES)],
        out_specs=[rows(D), rows(D), rows(D_PROJ), perb, perb, pl.BlockSpec((1, D), lambda i: (0, 0))],
        out_shape=[sd((T, D), F32), sd((T, D), BF16), sd((T, D_PROJ), BF16), sd((B, 1, D), F32), sd((B, 1, D), F32),
                   sd((1, D), F32)],
        compiler_params=_cparams(("arbitrary",)),
    )(dx2, x, nw, sh, sc, win, dz, dxbc, dcq, dckv, ddtk_a, ddtk_b)


def _shift_down(v, k, row):
    return jnp.where(row < k, 0.0, pltpu.roll(v, k, 0))


def _shift_up(v, k, row, n):
    return jnp.where(row >= n - k, 0.0, pltpu.roll(v, n - k, 0))


def _conv_pre(xv, w_ref, b_ref, row):
    pre = b_ref[...] + w_ref[CONV_WIDTH - 1:CONV_WIDTH, :] * xv
    for k in range(1, CONV_WIDTH):
        pre = pre + w_ref[CONV_WIDTH - 1 - k:CONV_WIDTH - k, :] * _shift_down(xv, k, row)
    return pre


def _conv_fwd(xraw, cw, cb):
    B, S, C = xraw.shape

    def body(x_ref, w_ref, b_ref, o_ref):
        xv = x_ref[0].astype(F32)
        row = lax.broadcasted_iota(jnp.int32, xv.shape, 0)
        pre = _conv_pre(xv, w_ref, b_ref, row)
        o_ref[0] = (pre * _sigmoid(pre)).astype(BF16)

    blk = pl.BlockSpec((1, S, LANES), lambda b, j: (b, 0, j))
    return pl.pallas_call(
        body, grid=(B, C // LANES), name="conv_fwd",
        in_specs=[blk, pl.BlockSpec((CONV_WIDTH, LANES), lambda b, j: (0, j)), pl.BlockSpec((1, LANES), lambda b, j: (0, j))],
        out_specs=blk, out_shape=jax.ShapeDtypeStruct((B, S, C), BF16),
        compiler_params=_cparams(("arbitrary", "arbitrary")),
    )(xraw, cw, cb)


def _conv_bwd(dout, xraw, cw, cb):
    B, S, C = xraw.shape

    def body(d_ref, x_ref, w_ref, b_ref, dx_ref, dw_ref, db_ref):
        @pl.when(pl.program_id(1) == 0)
        def _():
            dw_ref[...] = jnp.zeros_like(dw_ref)
            db_ref[...] = jnp.zeros_like(db_ref)

        xv = x_ref[0].astype(F32)
        row = lax.broadcasted_iota(jnp.int32, xv.shape, 0)
        pre = _conv_pre(xv, w_ref, b_ref, row)
        sig = _sigmoid(pre)
        dpre = d_ref[0].astype(F32) * (sig * (1.0 + pre * (1.0 - sig)))
        dx = w_ref[CONV_WIDTH - 1:CONV_WIDTH, :] * dpre
        for k in range(1, CONV_WIDTH):
            dx = dx + w_ref[CONV_WIDTH - 1 - k:CONV_WIDTH - k, :] * _shift_up(dpre, k, row, S)
        dx_ref[0] = dx.astype(BF16)
        db_ref[...] += _colsum(dpre)
        dws = [_colsum(dpre * (xv if k == 0 else _shift_down(xv, k, row))) for k in range(CONV_WIDTH - 1, -1, -1)]
        dw_ref[...] += jnp.concatenate(dws, axis=0)

    blk = pl.BlockSpec((1, S, LANES), lambda j, b: (b, 0, j))
    wspec = pl.BlockSpec((CONV_WIDTH, LANES), lambda j, b: (0, j))
    bspec = pl.BlockSpec((1, LANES), lambda j, b: (0, j))
    return pl.pallas_call(
        body, grid=(C // LANES, B), name="conv_bwd",
        in_specs=[blk, blk, wspec, bspec], out_specs=[blk, wspec, bspec],
        out_shape=[jax.ShapeDtypeStruct((B, S, C), BF16), jax.ShapeDtypeStruct((CONV_WIDTH, C), F32),
                   jax.ShapeDtypeStruct((1, C), F32)],
        compiler_params=_cparams(("arbitrary", "arbitrary")),
    )(dout, xraw, cw, cb)


def _softplus(x):
    return jnp.maximum(x, 0.0) + jnp.log(1.0 + jnp.exp(-jnp.abs(x)))


def _ssd_common(xbc_ref, dtk_ref, dtb_ref, alog_ref, e_ref):
    L = CHUNK
    xbc = xbc_ref[0]
    xs = xbc[:, :D_SSD].astype(F32)
    bm = xbc[:, D_SSD:D_SSD + 256]
    cm = xbc[:, D_SSD + 256:D_SSD + 512]
    head = lax.broadcasted_iota(jnp.int32, (1, LANES), 1) < SSD_HEADS
    a128 = jnp.where(head, -jnp.exp(alog_ref[...]), 0.0)
    pre = dtk_ref[0] + dtb_ref[...]
    dt = _softplus(pre)
    dA = dt * a128
    row = lax.broadcasted_iota(jnp.int32, (L, L), 0)
    col = lax.broadcasted_iota(jnp.int32, (L, L), 1)
    causal = col <= row
    tri = causal.astype(F32)
    triT = (row <= col).astype(F32)
    acum = _dot(tri, dA, HIGHEST)
    acumT = _dot_tn(dA, triT, HIGHEST)
    E = e_ref[...]
    acum_f = _dot(acum, E, HIGHEST)
    dt_f = _dot(dt, E, HIGHEST)
    e_f = jnp.exp(acum_f)
    w_f = jnp.exp(acum_f[L - 1:L, :] - acum_f)
    xt = xs * dt_f
    alast_rows = _dot_tn(E, jnp.broadcast_to(acumT[:, L - 1:L], (LANES, LANES)), HIGHEST)
    decay_rows = jnp.exp(alast_rows)
    return dict(xs=xs, bm=bm, cm=cm, a128=a128, pre=pre, dt=dt, causal=causal, tri=tri, triT=triT, acum=acum,
                acumT=acumT, E=E, dt_f=dt_f, e_f=e_f, w_f=w_f, xt=xt, decay_rows=decay_rows, head=head)


def _head_mask(k):
    lane = lax.broadcasted_iota(jnp.int32, (CHUNK, LANES), 1)
    return (lane >= SSD_HEAD_DIM) if k == 1 else (lane < SSD_HEAD_DIM)


def _decay_matrix(q, h):
    seg = q["acum"][:, h:h + 1] - q["acumT"][h:h + 1, :]
    return jnp.exp(jnp.where(q["causal"], seg, -1e30))


def _gated_norm(y, zz, nw):
    sig = _sigmoid(zz)
    sil = zz * sig
    yg = y * sil
    half = D_SSD // SSD_GROUPS
    parts = []
    for g in range(SSD_GROUPS):
        xh, r = _rms_stats(yg[:, g * half:(g + 1) * half])
        parts.append((xh, r))
    return sig, sil, parts


def _ssd_fwd(xbc, dtk, z, dtb, alog, dsk, nw, expand):
    B, S, _ = xbc.shape
    L = CHUNK
    nc = S // L

    def body(xbc_ref, dtk_ref, z_ref, dtb_ref, alog_ref, dsk_ref, nw_ref, e_ref, y_ref, ys_ref, prev_ref, st_ref):
        @pl.when(pl.program_id(1) == 0)
        def _():
            st_ref[...] = jnp.zeros_like(st_ref)

        q = _ssd_common(xbc_ref, dtk_ref, dtb_ref, alog_ref, e_ref)
        xtb = q["xt"].astype(BF16)
        xwb = (q["xt"] * q["w_f"]).astype(BF16)
        ys = []
        for g in range(SSD_GROUPS):
            bg = q["bm"][:, g * 128:(g + 1) * 128]
            cg = q["cm"][:, g * 128:(g + 1) * 128]
            G = _dot_nt(cg, bg)
            for pr in range(SSD_HEADS // SSD_GROUPS // 2):
                h0 = g * 8 + 2 * pr
                lo = h0 * SSD_HEAD_DIM
                xt_p = xtb[:, lo:lo + 128]
                ydiag = jnp.zeros((L, LANES), F32)
                for k in range(2):
                    M = (G * _decay_matrix(q, h0 + k)).astype(BF16)
                    ydiag = ydiag + _dot(M, jnp.where(_head_mask(k), xt_p, jnp.zeros_like(xt_p)))
                hp = st_ref[lo:lo + 128, :]
                prev_ref[0, 0, lo:lo + 128, :] = hp
                zoff = _dot_nt(cg, hp.astype(BF16))
                ys.append(ydiag + zoff * q["e_f"][:, lo:lo + 128])
                st_ref[lo:lo + 128, :] = q["decay_rows"][lo:lo + 128, :] * hp + _dot_tn(xwb[:, lo:lo + 128], bg)
        dsk_f = _dot(jnp.broadcast_to(dsk_ref[...], (8, LANES)), q["E"], HIGHEST)[0:1, :]
        y = jnp.concatenate(ys, axis=1) + dsk_f * q["xs"]
        y_ref[0] = y.astype(BF16)
        _, _, parts = _gated_norm(y, z_ref[0].astype(F32), nw_ref[...])
        half = D_SSD // SSD_GROUPS
        ys_ref[0] = jnp.concatenate(
            [xh * nw_ref[:, g * half:(g + 1) * half] for g, (xh, _) in enumerate(parts)], axis=1).astype(BF16)

    chunk = lambda n: pl.BlockSpec((1, L, n), lambda b, c: (b, c, 0))
    vec = pl.BlockSpec((1, LANES), lambda b, c: (0, 0))
    return pl.pallas_call(
        body, grid=(B, nc), name="ssd_fwd",
        in_specs=[chunk(D_CONV), chunk(LANES), chunk(D_SSD), vec, vec, vec,
                  pl.BlockSpec((1, D_SSD), lambda b, c: (0, 0)), pl.BlockSpec((LANES, D_SSD), lambda b, c: (0, 0))],
        out_specs=[chunk(D_SSD), chunk(D_SSD), pl.BlockSpec((1, 1, D_SSD, SSD_STATE), lambda b, c: (b, c, 0, 0))],
        out_shape=[jax.ShapeDtypeStruct((B, S, D_SSD), BF16), jax.ShapeDtypeStruct((B, S, D_SSD), BF16),
                   jax.ShapeDtypeStruct((B, nc, D_SSD, SSD_STATE), F32)],
        scratch_shapes=[pltpu.VMEM((D_SSD, SSD_STATE), F32)],
        compiler_params=_cparams(("arbitrary", "arbitrary")),
    )(xbc, dtk, z, dtb, alog, dsk, nw, expand)


def _ssd_bwd(xbc, dtk, z, y, prev, dys, dtb, alog, dsk, nw, expand):
    B, S, _ = xbc.shape
    L = CHUNK
    nc = S // L
    half = D_SSD // SSD_GROUPS

    def body(xbc_ref, dtk_ref, z_ref, y_ref, prev_ref, dys_ref, dtb_ref, alog_ref, dsk_ref, nw_ref, e_ref,
             dxbc_ref, ddtk_ref, dz_ref, dnw_ref, dvec_ref, dh_ref):
        @pl.when((pl.program_id(0) == 0) & (pl.program_id(1) == 0))
        def _():
            dnw_ref[...] = jnp.zeros_like(dnw_ref)
            dvec_ref[...] = jnp.zeros_like(dvec_ref)

        @pl.when(pl.program_id(1) == 0)
        def _():
            dh_ref[...] = jnp.zeros_like(dh_ref)

        q = _ssd_common(xbc_ref, dtk_ref, dtb_ref, alog_ref, e_ref)
        E = q["E"]
        xs = q["xs"]
        yv = y_ref[0].astype(F32)
        zz = z_ref[0].astype(F32)
        sig, sil, parts = _gated_norm(yv, zz, nw_ref[...])
        dn = dys_ref[0].astype(F32)
        dyg, dnw_rows = [], []
        for g, (xh, r) in enumerate(parts):
            dpart, dw_rows = _rms_bwd(dn[:, g * half:(g + 1) * half], xh, r, nw_ref[:, g * half:(g + 1) * half])
            dyg.append(dpart)
            dnw_rows.append(dw_rows)
        dyg = jnp.concatenate(dyg, axis=1)
        dnw_ref[...] += _colsum(jnp.concatenate(dnw_rows, axis=1))
        dY = dyg * sil
        dz_ref[0] = (dyg * yv * (sig * (1.0 + zz * (1.0 - sig)))).astype(BF16)
        dsk_f = _dot(jnp.broadcast_to(dsk_ref[...], (8, LANES)), E, HIGHEST)[0:1, :]
        ddsk = _dot_nt(jnp.broadcast_to(_colsum(dY * xs), (8, D_SSD)), E, HIGHEST)[0:1, :]
        dYb = dY.astype(BF16)
        xtb = q["xt"].astype(BF16)
        xwb = (q["xt"] * q["w_f"]).astype(BF16)
        lane_id = lax.broadcasted_iota(jnp.int32, (L, LANES), 1)
        da_rows = jnp.zeros((L, LANES), F32)
        daT = jnp.zeros((LANES, L), F32)
        dxt, prod_off, prod_st, dbs, dcs = [], [], [], [], []
        hsum = jnp.zeros((LANES, LANES), F32)
        for g in range(SSD_GROUPS):
            bg = q["bm"][:, g * 128:(g + 1) * 128]
            cg = q["cm"][:, g * 128:(g + 1) * 128]
            G = _dot_nt(cg, bg)
            dG = jnp.zeros((L, L), F32)
            dcg = jnp.zeros((L, SSD_STATE), F32)
            dbg = jnp.zeros((L, SSD_STATE), F32)
            for pr in range(SSD_HEADS // SSD_GROUPS // 2):
                h0 = g * 8 + 2 * pr
                lo = h0 * SSD_HEAD_DIM
                cols = slice(lo, lo + 128)
                dY_p = dYb[:, cols]
                xt_p = xtb[:, cols]
                dxt_p = jnp.zeros((L, LANES), F32)
                for k in range(2):
                    h = h0 + k
                    Lm = _decay_matrix(q, h)
                    Mf = G * Lm
                    dYk = jnp.where(_head_mask(k), dY_p, jnp.zeros_like(dY_p))
                    dM = _dot_nt(dYk, xt_p)
                    dxt_p = dxt_p + _dot_tn(Mf.astype(BF16), dYk)
                    dG = dG + dM * Lm
                    Q = dM * Mf
                    onehot = (lane_id == h).astype(F32)
                    da_rows = da_rows + _dot(Q, onehot, HIGHEST)
                    daT = daT + _dot_tn(onehot, Q, HIGHEST)
                hp = prev_ref[0, 0, lo:lo + 128, :]
                hpb = hp.astype(BF16)
                zoff = _dot_nt(cg, hpb)
                e_p = q["e_f"][:, cols]
                dY_pf = dY[:, cols]
                dZb = (dY_pf * e_p).astype(BF16)
                dcg = dcg + _dot(dZb, hpb)
                dhp_off = _dot_tn(dZb, cg)
                prod_off.append(dY_pf * zoff * e_p)
                dS = dh_ref[lo:lo + 128, :]
                dSb = dS.astype(BF16)
                U = _dot_nt(bg, dSb)
                dxt_p = dxt_p + U * q["w_f"][:, cols]
                dbg = dbg + _dot(xwb[:, cols], dSb)
                prod_st.append(q["xt"][:, cols] * U)
                decay_p = q["decay_rows"][lo:lo + 128, :]
                dh_ref[lo:lo + 128, :] = decay_p * dS + dhp_off
                hsum = hsum + _dot(E[:, cols], dS * hp, HIGHEST)
                dxt.append(dxt_p)
            dGb = dG.astype(BF16)
            dcs.append(dcg + _dot(dGb, bg))
            dbs.append(dbg + _dot_tn(dGb, cg))
        dxt = jnp.concatenate(dxt, axis=1)
        da_rows = da_rows + _dot_nt(jnp.concatenate(prod_off, axis=1), E, HIGHEST)
        dw = _dot_nt(jnp.concatenate(prod_st, axis=1), E, HIGHEST)
        acum = q["acum"]
        alast = acum[L - 1:L, :]
        dww = dw * jnp.exp(alast - acum)
        da_rows = da_rows - dww
        hsum_row = _colsum(hsum.T)
        dlast = _colsum(dww) + jnp.exp(alast) * hsum_row
        ddA = _dot(q["triT"], da_rows, HIGHEST) - _dot_nt(q["triT"], daT, HIGHEST) + dlast
        ddA = jnp.where(q["head"], ddA, 0.0)
        ddt = ddA * q["a128"] + _dot_nt(dxt * xs, E, HIGHEST)
        ddt_raw = jnp.where(q["head"], ddt * _sigmoid(q["pre"]), 0.0)
        ddtk_ref[0] = ddt_raw
        dxs = dxt * q["dt_f"] + dsk_f * dY
        dxbc_ref[0] = jnp.concatenate([dxs] + dbs + dcs, axis=1).astype(BF16)
        dvec_ref[0:1, :] += _colsum(ddt_raw)
        dvec_ref[1:2, :] += _colsum(ddA * q["dt"]) * q["a128"]
        dvec_ref[2:3, :] += ddsk

    rev = lambda n: pl.BlockSpec((1, L, n), lambda b, c: (b, nc - 1 - c, 0))
    vec = pl.BlockSpec((1, LANES), lambda b, c: (0, 0))
    sd = jax.ShapeDtypeStruct
    return pl.pallas_call(
        body, grid=(B, nc), name="ssd_bwd",
        in_specs=[rev(D_CONV), rev(LANES), rev(D_SSD), rev(D_SSD),
                  pl.BlockSpec((1, 1, D_SSD, SSD_STATE), lambda b, c: (b, nc - 1 - c, 0, 0)), rev(D_SSD), vec, vec, vec,
                  pl.BlockSpec((1, D_SSD), lambda b, c: (0, 0)), pl.BlockSpec((LANES, D_SSD), lambda b, c: (0, 0))],
        out_specs=[rev(D_CONV), rev(LANES), rev(D_SSD), pl.BlockSpec((1, D_SSD), lambda b, c: (0, 0)),
                   pl.BlockSpec((8, LANES), lambda b, c: (0, 0))],
        out_shape=[sd((B, S, D_CONV), BF16), sd((B, S, LANES), F32), sd((B, S, D_SSD), BF16), sd((1, D_SSD), F32),
                   sd((8, LANES), F32)],
        scratch_shapes=[pltpu.VMEM((D_SSD, SSD_STATE), F32)],
        compiler_params=_cparams(("arbitrary", "arbitrary")),
    )(xbc, dtk, z, y, prev, dys, dtb, alog, dsk, nw, expand)


def _rope_tables(pos_ref, invf_ref):
    ang = pos_ref[...].astype(F32) * invf_ref[...]
    return jnp.cos(ang), jnp.sin(ang)


def _rot(u):
    lane = lax.broadcasted_iota(jnp.int32, u.shape, 1)
    first = (lane >= QK_NOPE) & (lane < QK_NOPE + QK_ROPE // 2)
    second = (lane >= QK_NOPE + QK_ROPE // 2) & (lane < QK_DIM)
    return jnp.where(first, -pltpu.roll(u, LANES - QK_ROPE // 2, 1), jnp.where(second, pltpu.roll(u, QK_ROPE // 2, 1), 0.0))


def _rope_lanes(shape):
    lane = lax.broadcasted_iota(jnp.int32, shape, 1)
    return (lane >= QK_NOPE) & (lane < QK_DIM)


def _mla_prep(cq, ckv, dtk, pos, qw, kvw, wuq, wukv, invf):
    T = cq.shape[0]
    tm = min(TOKEN_TILE, T)
    scale = 1.0 / math.sqrt(QK_DIM)
    HW = MLA_HEADS * HEAD_LANES

    def body(cq_ref, ckv_ref, dtk_ref, pos_ref, qw_ref, kvw_ref, wuq_ref, wukv_ref, invf_ref, q_ref, k_ref, v_ref):
        xh, _ = _rms_stats(cq_ref[...])
        qv = _dot((xh * qw_ref[...]).astype(BF16), wuq_ref[...])
        xh, _ = _rms_stats(ckv_ref[...])
        kv = _dot((xh * kvw_ref[...]).astype(BF16), wukv_ref[...])
        cosf, sinf = _rope_tables(pos_ref, invf_ref)
        rope = lambda u: u * cosf + _rot(u) * sinf
        dtkv = dtk_ref[...]
        kr = rope(jnp.where(_rope_lanes(dtkv.shape), dtkv, 0.0))
        for h in range(MLA_HEADS):
            cols = slice(h * HEAD_LANES, (h + 1) * HEAD_LANES)
            q_ref[:, cols] = (rope(qv[:, cols]) * scale).astype(BF16)
            k_ref[:, cols] = (kv[:, cols] + kr).astype(BF16)
        v_ref[...] = kv[:, HW:].astype(BF16)

    rows = lambda n: pl.BlockSpec((tm, n), lambda i: (i, 0))
    return pl.pallas_call(
        body, grid=(T // tm,), name="mla_prep",
        in_specs=[rows(Q_LORA), rows(KV_LORA), rows(LANES), rows(1), _resident((1, Q_LORA)), _resident((1, KV_LORA)),
                  _resident((Q_LORA, HW)), _resident((KV_LORA, 2 * HW)), _resident((1, LANES))],
        out_specs=[rows(HW), rows(HW), rows(HW)],
        out_shape=[jax.ShapeDtypeStruct((T, HW), BF16)] * 3,
        compiler_params=_cparams(("arbitrary",)),
    )(cq, ckv, dtk, pos, qw, kvw, wuq, wukv, invf)


def _mla_prep_bwd(dq, dk, dv, cq, ckv, pos, qw, kvw, wuq, wukv, invf):
    T = cq.shape[0]
    tm = min(TOKEN_TILE, T)
    scale = 1.0 / math.sqrt(QK_DIM)
    HW = MLA_HEADS * HEAD_LANES

    def body(dq_ref, dk_ref, dv_ref, cq_ref, ckv_ref, pos_ref, qw_ref, kvw_ref, wuq_ref, wukv_ref, invf_ref,
             dcq_ref, dckv_ref, ddtk_ref, qn_ref, kvn_ref, dqo_ref, dkvo_ref, dqw_ref, dkvw_ref):
        @pl.when(pl.program_id(0) == 0)
        def _():
            dqw_ref[...] = jnp.zeros_like(dqw_ref)
            dkvw_ref[...] = jnp.zeros_like(dkvw_ref)

        cosf, sinf = _rope_tables(pos_ref, invf_ref)
        unrope = lambda d: d * cosf - _rot(d * sinf)
        dkr = jnp.zeros((tm, LANES), F32)
        nope = lax.broadcasted_iota(jnp.int32, (tm, LANES), 1) < QK_NOPE
        for h in range(MLA_HEADS):
            cols = slice(h * HEAD_LANES, (h + 1) * HEAD_LANES)
            dqo_ref[:, cols] = unrope(dq_ref[:, cols] * scale).astype(BF16)
            dkh = dk_ref[:, cols]
            dkr = dkr + jnp.where(_rope_lanes(dkh.shape), dkh, 0.0)
            dkvo_ref[:, cols] = jnp.where(nope, dkh, 0.0).astype(BF16)
        dkvo_ref[:, HW:] = dv_ref[...].astype(BF16)
        ddtk_ref[...] = unrope(dkr)
        xh, r = _rms_stats(cq_ref[...])
        qn_ref[...] = (xh * qw_ref[...]).astype(BF16)
        dx, dw_rows = _rms_bwd(_dot_nt(dqo_ref[...], wuq_ref[...]), xh, r, qw_ref[...])
        dcq_ref[...] = dx
        dqw_ref[...] += _colsum(dw_rows)
        xh, r = _rms_stats(ckv_ref[...])
        kvn_ref[...] = (xh * kvw_ref[...]).astype(BF16)
        dx, dw_rows = _rms_bwd(_dot_nt(dkvo_ref[...], wukv_ref[...]), xh, r, kvw_ref[...])
        dckv_ref[...] = dx
        dkvw_ref[...] += _colsum(dw_rows)

    rows = lambda n: pl.BlockSpec((tm, n), lambda i: (i, 0))
    sd = jax.ShapeDtypeStruct
    return pl.pallas_call(
        body, grid=(T // tm,), name="mla_prep_bwd",
        in_specs=[rows(HW), rows(HW), rows(HW), rows(Q_LORA), rows(KV_LORA), rows(1), _resident((1, Q_LORA)),
                  _resident((1, KV_LORA)), _resident((Q_LORA, HW)), _resident((KV_LORA, 2 * HW)), _resident((1, LANES))],
        out_specs=[rows(Q_LORA), rows(KV_LORA), rows(LANES), rows(Q_LORA), rows(KV_LORA), rows(HW), rows(2 * HW),
                   pl.BlockSpec((1, Q_LORA), lambda i: (0, 0)), pl.BlockSpec((1, KV_LORA), lambda i: (0, 0))],
        out_shape=[sd((T, Q_LORA), F32), sd((T, KV_LORA), F32), sd((T, LANES), F32), sd((T, Q_LORA), BF16),
                   sd((T, KV_LORA), BF16), sd((T, HW), BF16), sd((T, 2 * HW), BF16), sd((1, Q_LORA), F32),
                   sd((1, KV_LORA), F32)],
        compiler_params=_cparams(("arbitrary",)),
    )(dq, dk, dv, cq, ckv, pos, qw, kvw, wuq, wukv, invf)


def _causal_mask(t):
    row = lax.broadcasted_iota(jnp.int32, (t, t), 0)
    col = lax.broadcasted_iota(jnp.int32, (t, t), 1)
    return col <= row


def _attn_fwd(q, k, v):
    B, S, HW = q.shape
    H = HW // HEAD_LANES
    t = min(ATTN_TILE, S)
    nq = S // t

    def body(q_ref, k_ref, v_ref, o_ref, lse_ref):
        qi = pl.program_id(2)
        qv = q_ref[0]

        def step(j, carry, masked):
            m, l, acc = carry
            sl = pl.ds(pl.multiple_of(j * t, t), t)
            s = _dot_nt(qv, k_ref[0, sl, :])
            if masked:
                s = jnp.where(_causal_mask(t), s, -1e30)
            m_new = jnp.maximum(m, jnp.max(s, axis=-1, keepdims=True))
            alpha = jnp.exp(m - m_new)
            p = jnp.exp(s - m_new)
            l = alpha * l + jnp.sum(p, axis=-1, keepdims=True)
            acc = alpha * acc + _dot(p.astype(BF16), v_ref[0, sl, :])
            return m_new, l, acc

        init = (jnp.full((t, 1), -1e30, F32), jnp.zeros((t, 1), F32), jnp.zeros((t, HEAD_LANES), F32))
        carry = lax.fori_loop(0, qi, lambda j, c: step(j, c, False), init)
        m, l, acc = step(qi, carry, True)
        o_ref[0] = (acc / l).astype(BF16)
        lse_ref[0, 0] = m + jnp.log(l)

    return pl.pallas_call(
        body, grid=(B, H, nq), name="attn_fwd",
        in_specs=[pl.BlockSpec((1, t, HEAD_LANES), lambda b, h, i: (b, i, h)),
                  pl.BlockSpec((1, S, HEAD_LANES), lambda b, h, i: (b, 0, h)),
                  pl.BlockSpec((1, S, HEAD_LANES), lambda b, h, i: (b, 0, h))],
        out_specs=[pl.BlockSpec((1, t, HEAD_LANES), lambda b, h, i: (b, i, h)),
                   pl.BlockSpec((1, 1, t, 1), lambda b, h, i: (b, h, i, 0))],
        out_shape=[jax.ShapeDtypeStruct((B, S, HW), BF16), jax.ShapeDtypeStruct((B, H, S, 1), F32)],
        compiler_params=_cparams(("arbitrary", "arbitrary", "arbitrary")),
    )(q, k, v)


def _attn_bwd(q, k, v, o, do, lse):
    B, S, HW = q.shape
    H = HW // HEAD_LANES
    t = min(ATTN_TILE, S)
    nq = S // t

    def body(q_ref, k_ref, v_ref, o_ref, do_ref, lse_ref, dq_ref, dk_ref, dv_ref):
        j = pl.program_id(2)

        @pl.when(j == 0)
        def _():
            dq_ref[...] = jnp.zeros_like(dq_ref)

        kj = k_ref[0]
        vj = v_ref[0]

        def step(i, carry, masked):
            dk, dv = carry
            sl = pl.ds(pl.multiple_of(i * t, t), t)
            qi = q_ref[0, sl, :]
            doi = do_ref[0, sl, :]
            s = _dot_nt(qi, kj)
            if masked:
                s = jnp.where(_causal_mask(t), s, -1e30)
            p = jnp.exp(s - lse_ref[0, 0, sl, :])
            dv = dv + _dot_tn(p.astype(BF16), doi)
            dp = _dot_nt(doi, vj)
            delta = jnp.sum(doi.astype(F32) * o_ref[0, sl, :].astype(F32), axis=-1, keepdims=True)
            dsb = (p * (dp - delta)).astype(BF16)
            dk = dk + _dot_tn(dsb, qi)
            dq_ref[0, sl, :] += _dot(dsb, kj)
            return dk, dv

        zero = jnp.zeros((t, HEAD_LANES), F32)
        carry = step(j, (zero, zero), True)
        dk, dv = lax.fori_loop(j + 1, nq, lambda i, c: step(i, c, False), carry)
        dk_ref[0] = dk
        dv_ref[0] = dv

    full = pl.BlockSpec((1, S, HEAD_LANES), lambda b, h, j: (b, 0, h))
    tile = pl.BlockSpec((1, t, HEAD_LANES), lambda b, h, j: (b, j, h))
    sd = jax.ShapeDtypeStruct
    return pl.pallas_call(
        body, grid=(B, H, nq), name="attn_bwd",
        in_specs=[full, tile, tile, full, full, pl.BlockSpec((1, 1, S, 1), lambda b, h, j: (b, h, 0, 0))],
        out_specs=[full, tile, tile],
        out_shape=[sd((B, S, HW), F32), sd((B, S, HW), F32), sd((B, S, HW), F32)],
        compiler_params=_cparams(("arbitrary", "arbitrary", "arbitrary")),
    )(q, k, v, o, do, lse)


def _mix_out(x1, yssd, o, mw, wout, g, seq):
    T, D = x1.shape
    tm = min(TOKEN_TILE, seq)
    tps = seq // tm

    def body(x_ref, ys_ref, o_ref, mw_ref, w_ref, g_ref, xo_ref, m_ref, yc_ref):
        xh, _ = _rms_stats(o_ref[...].astype(F32))
        ycat = jnp.concatenate([ys_ref[...], (xh * mw_ref[...]).astype(BF16)], axis=1)
        m = _dot(ycat, w_ref[...])
        xo_ref[...] = x_ref[...] + g_ref[0] * m
        m_ref[...] = m.astype(BF16)
        yc_ref[...] = ycat

    rows = lambda n: pl.BlockSpec((tm, n), lambda i: (i, 0))
    perb = pl.BlockSpec((1, 1, D), lambda i: (i // tps, 0, 0))
    sd = jax.ShapeDtypeStruct
    return pl.pallas_call(
        body, grid=(T // tm,), name="mix_out",
        in_specs=[rows(D), rows(D_SSD), rows(D_MLA), _resident((1, D_MLA)), _resident((D_SSD + D_MLA, D)), perb],
        out_specs=[rows(D), rows(D), rows(D_SSD + D_MLA)],
        out_shape=[sd((T, D), F32), sd((T, D), BF16), sd((T, D_SSD + D_MLA), BF16)],
        compiler_params=_cparams(("arbitrary",)),
    )(x1, yssd, o, mw, wout, g)


def _mix_out_bwd(dx2, m, o, mw, wout, g, seq):
    T, D = dx2.shape
    B = T // seq
    tm = min(TOKEN_TILE, seq)
    tps = seq // tm

    def body(dx_ref, m_ref, o_ref, mw_ref, w_ref, g_ref, dys_ref, do_ref, dm_ref, dg_ref, dmw_ref):
        i = pl.program_id(0)

        @pl.when(i % tps == 0)
        def _():
            dg_ref[...] = jnp.zeros_like(dg_ref)

        @pl.when(i == 0)
        def _():
            dmw_ref[...] = jnp.zeros_like(dmw_ref)

        dxv = dx_ref[...]
        dg_ref[0] += _colsum(dxv * m_ref[...].astype(F32))
        dmb = (g_ref[0] * dxv).astype(BF16)
        dm_ref[...] = dmb
        dycat = _dot_nt(dmb, w_ref[...])
        dys_ref[...] = dycat[:, :D_SSD].astype(BF16)
        xh, r = _rms_stats(o_ref[...].astype(F32))
        dx, dw_rows = _rms_bwd(dycat[:, D_SSD:], xh, r, mw_ref[...])
        do_ref[...] = dx.astype(BF16)
        dmw_ref[...] += _colsum(dw_rows)

    rows = lambda n: pl.BlockSpec((tm, n), lambda i: (i, 0))
    perb = pl.BlockSpec((1, 1, D), lambda i: (i // tps, 0, 0))
    sd = jax.ShapeDtypeStruct
    return pl.pallas_call(
        body, grid=(T // tm,), name="mix_out_bwd",
        in_specs=[rows(D), rows(D), rows(D_MLA), _resident((1, D_MLA)), _resident((D_SSD + D_MLA, D)), perb],
        out_specs=[rows(D_SSD), rows(D_MLA), rows(D), perb, pl.BlockSpec((1, D_MLA), lambda i: (0, 0))],
        out_shape=[sd((T, D_SSD), BF16), sd((T, D_MLA), BF16), sd((T, D), BF16), sd((B, 1, D), F32), sd((1, D_MLA), F32)],
        compiler_params=_cparams(("arbitrary",)),
    )(dx2, m, o, mw, wout, g)


def _win_to_kernel(w):
    z0 = jnp.zeros((w.shape[0], 48), w.dtype)
    z1 = jnp.zeros((w.shape[0], 32), w.dtype)
    return jnp.concatenate([w[:, :2560], w[:, 2576:3216], w[:, 2560:2576], z0, w[:, 3216:3248], z1], axis=1)


def _win_from_kernel(g):
    return jnp.concatenate([g[:, :2560], g[:, 3200:3216], g[:, 2560:3200], g[:, 3264:3296]], axis=1)


def _wuq_to_kernel(w):
    w = w.reshape(Q_LORA, MLA_HEADS, QK_DIM)
    return jnp.pad(w, ((0, 0), (0, 0), (0, HEAD_LANES - QK_DIM))).reshape(Q_LORA, MLA_HEADS * HEAD_LANES)


def _wuq_from_kernel(g):
    return g.reshape(Q_LORA, MLA_HEADS, HEAD_LANES)[:, :, :QK_DIM].reshape(Q_LORA, MLA_HEADS * QK_DIM)


def _wukv_to_kernel(w):
    w = w.reshape(KV_LORA, MLA_HEADS, QK_NOPE + V_HEAD)
    kp = jnp.pad(w[:, :, :QK_NOPE], ((0, 0), (0, 0), (0, HEAD_LANES - QK_NOPE)))
    return jnp.concatenate([kp.reshape(KV_LORA, -1), w[:, :, QK_NOPE:].reshape(KV_LORA, -1)], axis=1)


def _wukv_from_kernel(g):
    hw = MLA_HEADS * HEAD_LANES
    kp = g[:, :hw].reshape(KV_LORA, MLA_HEADS, HEAD_LANES)[:, :, :QK_NOPE]
    vp = g[:, hw:].reshape(KV_LORA, MLA_HEADS, V_HEAD)
    return jnp.concatenate([kp, vp], axis=2).reshape(KV_LORA, MLA_HEADS * (QK_NOPE + V_HEAD))


def _lanes16(v):
    return jnp.pad(v.reshape(1, SSD_HEADS), ((0, 0), (0, LANES - SSD_HEADS)))


def _constants():
    e = np.zeros((LANES, D_SSD), np.float32)
    for h in range(SSD_HEADS):
        e[h, h * SSD_HEAD_DIM:(h + 1) * SSD_HEAD_DIM] = 1.0
    inv_freq = ROPE_THETA ** (-jnp.arange(0, QK_ROPE, 2, dtype=F32) / QK_ROPE)
    half = QK_ROPE // 2
    invf = jnp.zeros((1, LANES), F32).at[0, QK_NOPE:QK_NOPE + half].set(inv_freq).at[0, QK_NOPE + half:QK_DIM].set(inv_freq)
    return jnp.asarray(e), invf


def _local_step(x, positions, mod, w, small, tgt):
    B, S, D = x.shape
    T = B * S
    expand, invf = _constants()
    x0 = x.reshape(T, D)
    pos = positions.reshape(T, 1)
    mods = [mod[:, i * D:(i + 1) * D].reshape(B, 1, D) for i in range(N_MOD)]
    sh1, sc1, g1, sh2, sc2, g2, sh3, sc3, g3 = mods
    dtb, alog, dsk = _lanes16(small["dt_bias"]), _lanes16(small["a_log"]), _lanes16(small["d_skip"])

    x1, a1, u1, f1 = _ffn_fwd(x0, small["norm_ffn1"], sh1, sc1, g1, w["ffn1_w_gate"], w["ffn1_w_up"], w["ffn1_w_down"], S, "ffn1_fwd")
    z, xraw, cq, ckv, dtk = _inproj_fwd(x1, small["norm_mix"], sh2, sc2, w["w_in"], S)
    xraw3 = xraw.reshape(B, S, D_CONV)
    xbc = _conv_fwd(xraw3, small["conv_w"], small["conv_b"])
    dtk3, z3 = dtk.reshape(B, S, LANES), z.reshape(B, S, D_SSD)
    y, yssd, prev = _ssd_fwd(xbc, dtk3, z3, dtb, alog, dsk, small["ssd_norm_w"], expand)
    q, k, v = _mla_prep(cq, ckv, dtk, pos, small["q_norm_w"], small["kv_norm_w"], w["w_uq"], w["w_ukv"], invf)
    hw = MLA_HEADS * HEAD_LANES
    q3, k3, v3 = q.reshape(B, S, hw), k.reshape(B, S, hw), v.reshape(B, S, hw)
    o3, lse = _attn_fwd(q3, k3, v3)
    o = o3.reshape(T, hw)
    x2, m, ycat = _mix_out(x1, yssd.reshape(T, D_SSD), o, small["mla_norm_w"], w["w_out"], g2, S)
    x3, a2, u2, f2 = _ffn_fwd(x2, small["norm_ffn2"], sh3, sc3, g3, w["ffn2_w_gate"], w["ffn2_w_up"], w["ffn2_w_down"], S, "ffn2_fwd")
    dx3, loss, d_norm_final = _final_loss(x3, small["norm_final"].reshape(1, D), tgt.reshape(T, D))

    gw, gs = {}, {}
    dx2, h3, s3, df3, da3, du3, dsh3, dsc3, dg3, gs["norm_ffn2"] = _ffn_bwd(
        dx3, x2, small["norm_ffn2"], sh3, sc3, g3, a2, u2, f2, w["ffn2_w_gate"], w["ffn2_w_up"], w["ffn2_w_down"], S, "ffn2_bwd")
    gw["ffn2_w_gate"], gw["ffn2_w_up"], gw["ffn2_w_down"] = _ffn_wgrad(h3, s3, df3, da3, du3, "ffn2_wgrad")

    dys, do, dm, dg2, gs["mla_norm_w"] = _mix_out_bwd(dx2, m, o, small["mla_norm_w"], w["w_out"], g2, S)
    gw["w_out"] = _mm_tn(ycat, dm, 512, "dwout")

    dq3, dk3, dv3 = _attn_bwd(q3, k3, v3, o3, do.reshape(B, S, hw), lse)
    dcq, dckv, ddtk_b, qn, kvn, dqb, dkvb, gs["q_norm_w"], gs["kv_norm_w"] = _mla_prep_bwd(
        dq3.reshape(T, hw), dk3.reshape(T, hw), dv3.reshape(T, hw), cq, ckv, pos, small["q_norm_w"], small["kv_norm_w"],
        w["w_uq"], w["w_ukv"], invf)
    gw["w_uq"] = _mm_tn(qn, dqb, 512, "dwuq")
    gw["w_ukv"] = _mm_tn(kvn, dkvb, 1024, "dwukv")

    dxbc, ddtk_a, dz, gs["ssd_norm_w"], dvec = _ssd_bwd(
        xbc, dtk3, z3, y, prev, dys.reshape(B, S, D_SSD), dtb, alog, dsk, small["ssd_norm_w"], expand)
    gs["dt_bias"], gs["a_log"], gs["d_skip"] = dvec[0:1, :SSD_HEADS], dvec[1:2, :SSD_HEADS], dvec[2:3, :SSD_HEADS]
    dxraw, gs["conv_w"], gs["conv_b"] = _conv_bwd(dxbc, xraw3, small["conv_w"], small["conv_b"])
    dx1, h2, dproj, dsh2, dsc2, gs["norm_mix"] = _inproj_bwd(
        dx2, x1, small["norm_mix"], sh2, sc2, w["w_in"], dz.reshape(T, D_SSD), dxraw.reshape(T, D_CONV), dcq, dckv,
        ddtk_a.reshape(T, LANES), ddtk_b, S)
    gw["w_in"] = _mm_tn(h2, dproj, 1664, "dwin")

    dx0, h1, s1, df1, da1, du1, dsh1, dsc1, dg1, gs["norm_ffn1"] = _ffn_bwd(
        dx1, x0, small["norm_ffn1"], sh1, sc1, g1, a1, u1, f1, w["ffn1_w_gate"], w["ffn1_w_up"], w["ffn1_w_down"], S, "ffn1_bwd")
    gw["ffn1_w_gate"], gw["ffn1_w_up"], gw["ffn1_w_down"] = _ffn_wgrad(h1, s1, df1, da1, du1, "ffn1_wgrad")
    gs["norm_final"] = d_norm_final
    dmod = jnp.concatenate([t.reshape(B, D) for t in (dsh1, dsc1, dg1, dsh2, dsc2, dg2, dsh3, dsc3, dg3)], axis=1)
    return loss, dx0.reshape(B, S, D), gw, dmod, gs


HBM_SPEC = pl.BlockSpec(memory_space=pltpu.HBM)
VMEM_SPEC = pl.BlockSpec(memory_space=pltpu.VMEM)


def _place():
    return lax.axis_index("x"), lax.axis_index("y"), lax.axis_index("c")


def _other_chips(mx, my):
    return [(1 - mx, my), (mx, 1 - my), (1 - mx, 1 - my)]


def _remote(src, dst, send_sem, recv_sem, to):
    return pltpu.make_async_remote_copy(src_ref=src, dst_ref=dst, send_sem=send_sem, recv_sem=recv_sem,
                                        device_id=to, device_id_type=MESH)


def _all_gather_small(xa, name):
    r, n = xa.shape

    def body(x_ref, o_ref, send_sems, recv_sems):
        mx, my, mc = _place()
        me = 4 * mx + 2 * my + mc
        o_ref[pl.ds(me, 1)] = x_ref[...][None]
        sends = []
        for k in range(1, N_DEV):
            peer = (mx ^ (k >> 2), my ^ ((k >> 1) & 1), mc ^ (k & 1))
            cp = _remote(x_ref, o_ref.at[me], send_sems.at[k - 1], recv_sems.at[k - 1], peer)
            cp.start()
            sends.append(cp)
        for k in range(1, N_DEV):
            peer = (mx ^ (k >> 2), my ^ ((k >> 1) & 1), mc ^ (k & 1))
            slot = 4 * peer[0] + 2 * peer[1] + peer[2]
            _remote(x_ref, o_ref.at[slot], send_sems.at[k - 1], recv_sems.at[k - 1], peer).wait_recv()
        for cp in sends:
            cp.wait_send()

    return pl.pallas_call(
        body, name=name, in_specs=[VMEM_SPEC], out_specs=VMEM_SPEC,
        out_shape=jax.ShapeDtypeStruct((N_DEV, r, n), xa.dtype),
        scratch_shapes=[pltpu.SemaphoreType.DMA((N_DEV - 1,)), pltpu.SemaphoreType.DMA((N_DEV - 1,))],
        compiler_params=pltpu.CompilerParams(vmem_limit_bytes=VMEM_LIMIT),
    )(xa)


def _half_rows(ref, hc, rh, lead=None):
    rows = pl.ds(pl.multiple_of(hc * rh, 8), rh)
    return ref.at[rows, :] if lead is None else ref.at[lead, rows, :]


def _gather_weights(shards):
    n = len(shards)

    def body(*refs):
        w_refs, o_refs = refs[:n], refs[n:2 * n]
        send_sems, recv_sems, stage_sems = refs[2 * n:2 * n + 3]
        stages = refs[2 * n + 3:]
        mx, my, mc = _place()
        chip = 2 * mx + my
        others = _other_chips(mx, my)
        sibling = (mx, my, 1 - mc)
        stage_in = [pltpu.make_async_copy(w, st, stage_sems.at[0, i]) for i, (w, st) in enumerate(zip(w_refs, stages))]
        for cp in stage_in:
            cp.start()
        first = []
        for i, (w, o) in enumerate(zip(w_refs, o_refs)):
            rh = w.shape[0] // 2
            for k, (cx, cy) in enumerate(others):
                first.append(_remote(_half_rows(w, mc, rh), _half_rows(o, mc, rh, chip), send_sems.at[i, k],
                                     recv_sems.at[i, k], (cx, cy, mc)))
                first[-1].start()
        stage_out = []
        for i, (st, o) in enumerate(zip(stages, o_refs)):
            stage_in[i].wait()
            stage_out.append(pltpu.make_async_copy(st, o.at[chip], stage_sems.at[1, i]))
            stage_out[-1].start()
        passed = []
        for i, (w, o) in enumerate(zip(w_refs, o_refs)):
            rh = w.shape[0] // 2
            for k, (cx, cy) in enumerate(others):
                landed = _half_rows(o, mc, rh, 2 * cx + cy)
                _remote(landed, landed, send_sems.at[i, k], recv_sems.at[i, k], (cx, cy, mc)).wait_recv()
                passed.append(_remote(landed, landed, send_sems.at[i, 3 + k], recv_sems.at[i, 3 + k], sibling))
                passed[-1].start()
        for i, (w, o) in enumerate(zip(w_refs, o_refs)):
            rh = w.shape[0] // 2
            for k, (cx, cy) in enumerate(others):
                there = _half_rows(o, 1 - mc, rh, 2 * cx + cy)
                _remote(there, there, send_sems.at[i, 3 + k], recv_sems.at[i, 3 + k], sibling).wait_recv()
        for cp in first + passed:
            cp.wait_send()
        for cp in stage_out:
            cp.wait()

    return pl.pallas_call(
        body, name="gather_weights", in_specs=[HBM_SPEC] * n, out_specs=[HBM_SPEC] * n,
        out_shape=[jax.ShapeDtypeStruct((N_CHIPS,) + s.shape, s.dtype) for s in shards],
        scratch_shapes=[pltpu.SemaphoreType.DMA((n, 6)), pltpu.SemaphoreType.DMA((n, 6)), pltpu.SemaphoreType.DMA((2, n))]
        + [pltpu.VMEM(s.shape, s.dtype) for s in shards],
        compiler_params=pltpu.CompilerParams(vmem_limit_bytes=VMEM_LIMIT),
    )(*shards)


def _swap_halves(gs):
    n = len(gs)

    def body(*refs):
        g_refs, o_refs, send_sems, recv_sems = refs[:n], refs[n:2 * n], refs[2 * n], refs[2 * n + 1]
        mx, my, mc = _place()
        copies = []
        for i, (g, o) in enumerate(zip(g_refs, o_refs)):
            rh = g.shape[1] // 2
            src = g.at[:, pl.ds(pl.multiple_of((1 - mc) * rh, 8), rh), :]
            copies.append(_remote(src, o, send_sems.at[i], recv_sems.at[i], (mx, my, 1 - mc)))
            copies[-1].start()
        for cp in copies:
            cp.wait()

    return pl.pallas_call(
        body, name="swap_halves", in_specs=[HBM_SPEC] * n, out_specs=[HBM_SPEC] * n,
        out_shape=[jax.ShapeDtypeStruct((N_CHIPS, g.shape[1] // 2, g.shape[2]), g.dtype) for g in gs],
        scratch_shapes=[pltpu.SemaphoreType.DMA((n,)), pltpu.SemaphoreType.DMA((n,))],
    )(*gs)


def _pair_sum(g, got, core, name):
    _, r, c = g.shape
    rh = r // 2

    def body(core_ref, g_ref, got_ref, o_ref):
        o_ref[...] = (g_ref[...] + got_ref[...]).astype(BF16)

    return pl.pallas_call(
        body, name=name,
        grid_spec=pltpu.PrefetchScalarGridSpec(
            num_scalar_prefetch=1, grid=(N_CHIPS,),
            in_specs=[pl.BlockSpec((1, rh, c), lambda j, core_ref: (j, core_ref[0], 0)),
                      pl.BlockSpec((1, rh, c), lambda j, core_ref: (j, 0, 0))],
            out_specs=pl.BlockSpec((1, rh, c), lambda j, core_ref: (j, 0, 0))),
        out_shape=jax.ShapeDtypeStruct((N_CHIPS, rh, c), BF16),
        compiler_params=_cparams(("arbitrary",)),
    )(core, g, got)


def _scatter_chips(ss):
    n = len(ss)

    def body(*refs):
        s_refs, o_refs, send_sems, recv_sems = refs[:n], refs[n:2 * n], refs[2 * n], refs[2 * n + 1]
        mx, my, mc = _place()
        chip = 2 * mx + my
        others = _other_chips(mx, my)
        sends = []
        for i, (s, o) in enumerate(zip(s_refs, o_refs)):
            for k, (cx, cy) in enumerate(others):
                sends.append(_remote(s.at[2 * cx + cy], o.at[chip], send_sems.at[i, k], recv_sems.at[i, k], (cx, cy, mc)))
                sends[-1].start()
        for i, (s, o) in enumerate(zip(s_refs, o_refs)):
            for k, (cx, cy) in enumerate(others):
                slot = o.at[2 * cx + cy]
                _remote(slot, slot, send_sems.at[i, k], recv_sems.at[i, k], (cx, cy, mc)).wait_recv()
        for cp in sends:
            cp.wait_send()

    return pl.pallas_call(
        body, name="scatter_chips", in_specs=[HBM_SPEC] * n, out_specs=[HBM_SPEC] * n,
        out_shape=[jax.ShapeDtypeStruct(s.shape, s.dtype) for s in ss],
        scratch_shapes=[pltpu.SemaphoreType.DMA((n, 3)), pltpu.SemaphoreType.DMA((n, 3))],
    )(*ss)


def _chip_sum(own, got, chip, name):
    _, h, c = own.shape

    def body(chip_ref, a_ref, b_ref, c_ref, d_ref, o_ref):
        o_ref[...] = ((a_ref[0].astype(F32) + b_ref[0].astype(F32)) + c_ref[0].astype(F32)) + d_ref[0].astype(F32)

    slot = lambda flip: pl.BlockSpec((1, h, c), lambda i, chip_ref: (chip_ref[0] ^ flip, 0, 0))
    return pl.pallas_call(
        body, name=name,
        grid_spec=pltpu.PrefetchScalarGridSpec(
            num_scalar_prefetch=1, grid=(1,), in_specs=[slot(0), slot(1), slot(2), slot(3)],
            out_specs=pl.BlockSpec((h, c), lambda i, chip_ref: (0, 0))),
        out_shape=jax.ShapeDtypeStruct((h, c), F32),
        compiler_params=_cparams(("arbitrary",)),
    )(chip, own, got, got, got)


def _join_halves(mine):
    n = len(mine)

    def body(*refs):
        m_refs, o_refs, send_sems, recv_sems = refs[:n], refs[n:2 * n], refs[2 * n], refs[2 * n + 1]
        mx, my, mc = _place()
        copies = []
        for i, (m, o) in enumerate(zip(m_refs, o_refs)):
            copies.append(_remote(m, o, send_sems.at[i], recv_sems.at[i], (mx, my, 1 - mc)))
            copies[-1].start()
        for cp in copies:
            cp.wait()

    return pl.pallas_call(
        body, name="join_halves", in_specs=[HBM_SPEC] * n, out_specs=[HBM_SPEC] * n,
        out_shape=[jax.ShapeDtypeStruct(m.shape, m.dtype) for m in mine],
        scratch_shapes=[pltpu.SemaphoreType.DMA((n,)), pltpu.SemaphoreType.DMA((n,))],
    )(*mine)


def _adam_math(w, g, m, v):
    m2 = ADAM_B1 * m + (1.0 - ADAM_B1) * g
    v2 = ADAM_B2 * v + (1.0 - ADAM_B2) * (g * g)
    m_hat = m2 * (1.0 / (1.0 - ADAM_B1 ** ADAM_STEP))
    v_hat = v2 * (1.0 / (1.0 - ADAM_B2 ** ADAM_STEP))
    delta = -ADAM_LR * (m_hat / (jnp.sqrt(v_hat) + ADAM_EPS) + ADAM_WD * w)
    return delta, m2, v2


def _adam(w, g, m, v, name):
    def body(w_ref, g_ref, m_ref, v_ref, d_ref, m2_ref, v2_ref):
        d_ref[...], m2_ref[...], v2_ref[...] = _adam_math(w_ref[...], g_ref[...], m_ref[...], v_ref[...])

    return pl.pallas_call(body, name=name, out_shape=[jax.ShapeDtypeStruct(w.shape, F32)] * 3)(w, g, m, v)


def _adam_halves(w, m, v, mine, theirs, core, name):
    _, r, c = w.shape
    rh = r // 2

    def body(core_ref, w_ref, m_ref, v_ref, mine_ref, theirs_ref, g_ref, d_ref, m2_ref, v2_ref):
        g = jnp.where(pl.program_id(0) == core_ref[0], mine_ref[...], theirs_ref[...])
        g_ref[0] = g
        d_ref[0], m2_ref[0], v2_ref[0] = _adam_math(w_ref[0], g, m_ref[0], v_ref[0])

    half = pl.BlockSpec((1, rh, c), lambda hc, core_ref: (0, hc, 0))
    whole = pl.BlockSpec((rh, c), lambda hc, core_ref: (0, 0))
    return pl.pallas_call(
        body, name=name,
        grid_spec=pltpu.PrefetchScalarGridSpec(
            num_scalar_prefetch=1, grid=(2,), in_specs=[half, half, half, whole, whole], out_specs=[half] * 4),
        out_shape=[jax.ShapeDtypeStruct(w.shape, F32)] * 4,
        compiler_params=_cparams(("arbitrary",)),
    )(core, w, m, v, mine, theirs)


ADA_COLS = N_MOD * D_MODEL // N_CHIPS


def _ada_fwd(c_all, w_ada, b_cols):
    def body(c_ref, w_ref, b_ref, o_ref):
        cv = c_ref[...]
        act = (cv * _sigmoid(cv)).astype(BF16)
        o_ref[...] = _dot(act, w_ref[...].astype(BF16)) + b_ref[...]

    return pl.pallas_call(
        body, name="ada_fwd", out_shape=jax.ShapeDtypeStruct((c_all.shape[0], ADA_COLS), F32),
        compiler_params=pltpu.CompilerParams(vmem_limit_bytes=VMEM_LIMIT),
    )(c_all, w_ada, b_cols)


def _ada_bwd(c_all, dmod_cols, w, m, v):
    nb = c_all.shape[0]
    tn = 384

    def body(c_ref, d_ref, w_ref, m_ref, v_ref, g_ref, dl_ref, m2_ref, v2_ref):
        cv = c_ref[...]
        act = (cv * _sigmoid(cv)).astype(BF16)
        g = _dot_tn(act, d_ref[...].astype(BF16))
        g_ref[...] = g
        dl_ref[...], m2_ref[...], v2_ref[...] = _adam_math(w_ref[...], g, m_ref[...], v_ref[...])

    blk = pl.BlockSpec((D_MODEL, tn), lambda j: (0, j))
    return pl.pallas_call(
        body, name="ada_bwd", grid=(ADA_COLS // tn,),
        in_specs=[pl.BlockSpec((nb, D_MODEL), lambda j: (0, 0)), pl.BlockSpec((nb, tn), lambda j: (0, j)), blk, blk, blk],
        out_specs=[blk] * 4, out_shape=[jax.ShapeDtypeStruct((D_MODEL, ADA_COLS), F32)] * 4,
        compiler_params=_cparams(("arbitrary",)),
    )(c_all, dmod_cols, w, m, v)


SMALL_NAMES = ("norm_ffn1", "norm_mix", "conv_w", "conv_b", "ssd_norm_w", "q_norm_w", "kv_norm_w", "mla_norm_w",
               "norm_ffn2", "norm_final", "dt_bias", "a_log", "d_skip")
SMALL_SIZES = (1024, 1024, CONV_WIDTH * D_CONV, D_CONV, 1024, Q_LORA, KV_LORA, 1024, 1024, 1024, 16, 16, 16)
SMALL_ROWS = 16
MOD_ROWS = 2 * N_MOD
SEND_ROWS = 40


def _pack_small(parts):
    flat = jnp.concatenate([parts[n].reshape(-1) for n in SMALL_NAMES])
    return jnp.pad(flat, (0, SMALL_ROWS * D_MODEL - flat.shape[0]))


def _unpack_small(flat):
    out, off = {}, 0
    for n, size in zip(SMALL_NAMES, SMALL_SIZES):
        out[n] = flat[off:off + size]
        off += size
    return out


def _small_sum(got):
    def body(g_ref, o_ref):
        bsum = jnp.zeros((N_MOD, D_MODEL), F32)
        ssum = jnp.zeros((SMALL_ROWS, D_MODEL), F32)
        for d in range(N_DEV):
            bsum = bsum + g_ref[d, 0:N_MOD, :] + g_ref[d, N_MOD:MOD_ROWS, :]
            ssum = ssum + g_ref[d, MOD_ROWS:MOD_ROWS + SMALL_ROWS, :]
        o_ref[...] = jnp.concatenate([bsum, ssum, jnp.zeros((32 - N_MOD - SMALL_ROWS, D_MODEL), F32)], axis=0)

    return pl.pallas_call(body, name="small_sum", out_shape=jax.ShapeDtypeStruct((32, D_MODEL), F32))(got)


BIG_NAMES = ("ffn1_w_gate", "ffn1_w_up", "ffn1_w_down", "w_in", "w_uq", "w_ukv", "w_out", "ffn2_w_gate", "ffn2_w_up",
             "ffn2_w_down")
_TO_KERNEL = {"w_in": _win_to_kernel, "w_uq": _wuq_to_kernel, "w_ukv": _wukv_to_kernel}
_FROM_KERNEL = {"w_in": _win_from_kernel, "w_uq": _wuq_from_kernel, "w_ukv": _wukv_from_kernel}


def _columns_joined(w4):
    n, r, c = w4.shape
    return w4.transpose(1, 0, 2).reshape(r, n * c)


def _columns_split(g):
    r, cols = g.shape
    return g.reshape(r, N_CHIPS, cols // N_CHIPS).transpose(1, 0, 2)


def kernel(x, c, positions, w_ada, b_ada, norm_ffn1, ffn1_w_gate, ffn1_w_up, ffn1_w_down, norm_mix, w_in, conv_w, conv_b, dt_bias, a_log, d_skip, ssd_norm_w, q_norm_w, w_uq, kv_norm_w, w_ukv, mla_norm_w, w_out, norm_ffn2, ffn2_w_gate, ffn2_w_up, ffn2_w_down, norm_final, loss_target, m_w_ada, m_b_ada, m_norm_ffn1, m_ffn1_w_gate, m_ffn1_w_up, m_ffn1_w_down, m_norm_mix, m_w_in, m_conv_w, m_conv_b, m_dt_bias, m_a_log, m_d_skip, m_ssd_norm_w, m_q_norm_w, m_w_uq, m_kv_norm_w, m_w_ukv, m_mla_norm_w, m_w_out, m_norm_ffn2, m_ffn2_w_gate, m_ffn2_w_up, m_ffn2_w_down, m_norm_final, v_w_ada, v_b_ada, v_norm_ffn1, v_ffn1_w_gate, v_ffn1_w_up, v_ffn1_w_down, v_norm_mix, v_w_in, v_conv_w, v_conv_b, v_dt_bias, v_a_log, v_d_skip, v_ssd_norm_w, v_q_norm_w, v_w_uq, v_kv_norm_w, v_w_ukv, v_mla_norm_w, v_w_out, v_norm_ffn2, v_ffn2_w_gate, v_ffn2_w_up, v_ffn2_w_down, v_norm_final):
    a = dict(locals())
    B, S, D = x.shape
    mx, my, mc = _place()
    chip = 2 * mx + my
    dev = 2 * chip + mc
    core = mc.astype(jnp.int32).reshape(1)
    chip_id = chip.astype(jnp.int32).reshape(1)

    w = dict(zip(BIG_NAMES, _gather_weights([a[n][0].astype(BF16) for n in BIG_NAMES])))
    for n, to_kernel in _TO_KERNEL.items():
        w[n] = to_kernel(_columns_joined(w[n]))
    w["w_out"] = w["w_out"].reshape(D_SSD + D_MLA, D)

    cw_rows = jnp.pad(conv_w[0], ((0, 0), (0, D - conv_w.shape[2])))
    got = _all_gather_small(jnp.concatenate([c, cw_rows, jnp.zeros((8 - B - CONV_WIDTH, D), F32)], axis=0), "gather_c")
    c_all = got[:, :B, :].reshape(N_DEV * B, D)
    conv_full = got[::2, B:B + CONV_WIDTH, :conv_w.shape[2]].transpose(1, 0, 2).reshape(CONV_WIDTH, D_CONV)

    b_cols = lax.dynamic_slice(b_ada, (0, chip * ADA_COLS), (1, ADA_COLS))
    mod_all = _all_gather_small(_ada_fwd(c_all, w_ada[0], b_cols), "gather_mod")
    mod = lax.dynamic_slice(mod_all, (0, B * dev, 0), (N_DEV, B, ADA_COLS))[::2].transpose(1, 0, 2).reshape(B, N_MOD * D)

    small = {n: a[n].reshape(1, -1) for n in SMALL_NAMES if n not in ("conv_w", "norm_final")}
    small["conv_w"], small["norm_final"] = conv_full, norm_final
    loss_blk, grad_x, gw, dmod, gs = _local_step(x, positions, mod, w, small, loss_target)

    for n, from_kernel in _FROM_KERNEL.items():
        gw[n] = _columns_split(from_kernel(gw[n]))
    gw["w_out"] = gw["w_out"].reshape(N_CHIPS, (D_SSD + D_MLA) // N_CHIPS, D)
    g4 = [gw[n] for n in BIG_NAMES]
    pair = [_pair_sum(g, got, core, "pair_sum_" + n) for n, g, got in zip(BIG_NAMES, g4, _swap_halves(g4))]
    mine = [_chip_sum(own, got, chip_id, "chip_sum_" + n) for n, own, got in zip(BIG_NAMES, pair, _scatter_chips(pair))]
    grads, deltas, new_m, new_v = {}, {}, {}, {}
    for n, own, other in zip(BIG_NAMES, mine, _join_halves(mine)):
        grads[n], deltas[n], new_m[n], new_v[n] = _adam_halves(a[n], a["m_" + n], a["v_" + n], own, other, core, "adam_" + n)

    small_flat = _pack_small(gs).at[-1].set(loss_blk[0, 0])
    send = jnp.concatenate([dmod.reshape(MOD_ROWS, D), small_flat.reshape(SMALL_ROWS, D),
                            jnp.zeros((SEND_ROWS - MOD_ROWS - SMALL_ROWS, D), F32)], axis=0)
    got = _all_gather_small(send, "gather_small")
    summed = _small_sum(got)
    sums = summed[N_MOD:N_MOD + SMALL_ROWS].reshape(-1)
    loss = sums[-1]
    gsmall = _unpack_small(sums)
    gsmall["conv_w"] = lax.dynamic_slice(gsmall["conv_w"].reshape(CONV_WIDTH, D_CONV), (0, chip * conv_w.shape[2]),
                                         (CONV_WIDTH, conv_w.shape[2]))
    gsmall["b_ada"] = summed[:N_MOD]
    names = ("b_ada",) + SMALL_NAMES
    rows = 208

    def pack(parts):
        flat = jnp.concatenate([parts[n].reshape(-1) for n in names])
        return jnp.pad(flat, (0, rows * LANES - flat.shape[0])).reshape(rows, LANES)

    packed = [pack({n: a[p + n] for n in names}) for p in ("", "m_", "v_")]
    g_p = pack(gsmall)
    outs = (g_p,) + tuple(_adam(packed[0], g_p, packed[1], packed[2], "adam_small"))
    for dst, flat in zip((grads, deltas, new_m, new_v), outs):
        flat, off = flat.reshape(-1), 0
        for n in names:
            dst[n] = flat[off:off + a[n].size].reshape(a[n].shape)
            off += a[n].size

    dmod_all = got[:, :MOD_ROWS, :].reshape(N_DEV * B, N_MOD * D)
    dmod_cols = lax.dynamic_slice(dmod_all, (0, chip * ADA_COLS), (N_DEV * B, ADA_COLS))
    ada = _ada_bwd(c_all, dmod_cols, w_ada[0], m_w_ada[0], v_w_ada[0])
    for dst, t in zip((grads, deltas, new_m, new_v), ada):
        dst["w_ada"] = t[None]

    order = ("w_ada", "b_ada", "norm_ffn1", "ffn1_w_gate", "ffn1_w_up", "ffn1_w_down", "norm_mix", "w_in", "conv_w", "conv_b",
             "dt_bias", "a_log", "d_skip", "ssd_norm_w", "q_norm_w", "w_uq", "kv_norm_w", "w_ukv", "mla_norm_w", "w_out",
             "norm_ffn2", "ffn2_w_gate", "ffn2_w_up", "ffn2_w_down", "norm_final")
    return (loss, grad_x, *[grads[n] for n in order], *[deltas[n] for n in order], *[new_m[n] for n in order],
            *[new_v[n] for n in order])
```

```python
import functools
import math

import jax
import jax.numpy as jnp
import numpy as np
from jax import lax
from jax.experimental import pallas as pl
from jax.experimental.pallas import tpu as pltpu

F32 = jnp.float32
BF16 = jnp.bfloat16
HIGHEST = lax.Precision.HIGHEST

D_MODEL = 1024
D_FF = 2816
D_SSD = 1024
D_MLA = 1024
SSD_HEADS = 16
SSD_HEAD_DIM = 64
SSD_GROUPS = 2
SSD_STATE = 128
CONV_WIDTH = 4
CHUNK = 128
MLA_HEADS = 8
QK_NOPE = 64
QK_ROPE = 32
QK_DIM = QK_NOPE + QK_ROPE
V_HEAD = 128
Q_LORA = 384
KV_LORA = 256
ROPE_THETA = 10000.0
N_MOD = 9
EPS = 1e-6
D_CONV = D_SSD + 2 * SSD_GROUPS * SSD_STATE
D_PROJ = 3328
HEAD_LANES = 128
ADAM_LR = 0.001
ADAM_B1 = 0.9
ADAM_B2 = 0.999
ADAM_EPS = 1e-08
ADAM_WD = 0.01
ADAM_STEP = 10

LANES = 128
VMEM_LIMIT = 56 * 1024 * 1024
TOKEN_TILE = 512
ATTN_TILE = 512
N_CHIPS = 4
N_DEV = 8

MESH = pl.DeviceIdType.MESH


def _dot(a, b, precision=None):
    return jnp.dot(a, b, preferred_element_type=F32, precision=precision)


def _dot_nt(a, b, precision=None):
    return lax.dot_general(a, b, (((1,), (1,)), ((), ())), preferred_element_type=F32, precision=precision)


def _dot_tn(a, b, precision=None):
    return lax.dot_general(a, b, (((0,), (0,)), ((), ())), preferred_element_type=F32, precision=precision)


def _cparams(semantics):
    return pltpu.CompilerParams(dimension_semantics=semantics, vmem_limit_bytes=VMEM_LIMIT)


def _resident(shape):
    zeros = (0,) * len(shape)
    return pl.BlockSpec(shape, lambda *_: zeros, pipeline_mode=pl.Buffered(1))


def _sigmoid(x):
    return jax.nn.sigmoid(x)


def _rms_stats(x):
    r = lax.rsqrt(jnp.mean(x * x, axis=-1, keepdims=True) + EPS)
    return x * r, r


def _rms_bwd(dn, xh, r, w):
    dxh = dn * w
    dx = r * (dxh - xh * jnp.mean(dxh * xh, axis=-1, keepdims=True))
    return dx, dn * xh


def _colsum(v):
    return jnp.sum(v, axis=0, keepdims=True)


def _ffn_fwd(x, nw, sh, sc, g, wg, wu, wd, seq, name):
    T, D = x.shape
    fs = wg.shape[2]
    tm = min(TOKEN_TILE, seq)
    tps = seq // tm

    def body(x_ref, nw_ref, sh_ref, sc_ref, g_ref, wg_ref, wu_ref, wd_ref, xo_ref, a_ref, u_ref, f_ref):
        xv = x_ref[...]
        xh, _ = _rms_stats(xv)
        h = (xh * nw_ref[...]) * (1.0 + sc_ref[0]) + sh_ref[0]
        hb = h.astype(BF16)
        f = jnp.zeros((tm, D), F32)
        for j in range(N_CHIPS):
            a = _dot(hb, wg_ref[j])
            u = _dot(hb, wu_ref[j])
            a_ref[j] = a.astype(BF16)
            u_ref[j] = u.astype(BF16)
            f = f + _dot((a * _sigmoid(a) * u).astype(BF16), wd_ref[j])
        xo_ref[...] = xv + 0.5 * g_ref[0] * f
        f_ref[...] = f.astype(BF16)

    rows = lambda n: pl.BlockSpec((tm, n), lambda i: (i, 0))
    act = pl.BlockSpec((N_CHIPS, tm, fs), lambda i: (0, i, 0))
    perb = pl.BlockSpec((1, 1, D), lambda i: (i // tps, 0, 0))
    return pl.pallas_call(
        body, grid=(T // tm,), name=name,
        in_specs=[rows(D), _resident((1, D)), perb, perb, perb, _resident((N_CHIPS, D, fs)), _resident((N_CHIPS, D, fs)),
                  _resident((N_CHIPS, fs, D))],
        out_specs=[rows(D), act, act, rows(D)],
        out_shape=[jax.ShapeDtypeStruct((T, D), F32), jax.ShapeDtypeStruct((N_CHIPS, T, fs), BF16),
                   jax.ShapeDtypeStruct((N_CHIPS, T, fs), BF16), jax.ShapeDtypeStruct((T, D), BF16)],
        compiler_params=_cparams(("arbitrary",)),
    )(x, nw, sh, sc, g, wg, wu, wd)


def _ffn_bwd(dxo, x, nw, sh, sc, g, a, u, f, wg, wu, wd, seq, name):
    T, D = x.shape
    fs = wg.shape[2]
    B = T // seq
    tm = min(TOKEN_TILE // 2, seq)
    tps = seq // tm

    def body(dxo_ref, x_ref, nw_ref, sh_ref, sc_ref, g_ref, a_ref, u_ref, f_ref, wg_ref, wu_ref, wd_ref,
             dx_ref, h_ref, s_ref, df_ref, da_ref, du_ref, dsh_ref, dsc_ref, dg_ref, dnw_ref):
        i = pl.program_id(0)

        @pl.when(i % tps == 0)
        def _():
            dsh_ref[...] = jnp.zeros_like(dsh_ref)
            dsc_ref[...] = jnp.zeros_like(dsc_ref)
            dg_ref[...] = jnp.zeros_like(dg_ref)

        @pl.when(i == 0)
        def _():
            dnw_ref[...] = jnp.zeros_like(dnw_ref)

        dxo_v = dxo_ref[...]
        dfb = (0.5 * g_ref[0] * dxo_v).astype(BF16)
        dg_ref[0] += _colsum(0.5 * dxo_v * f_ref[...].astype(F32))
        dh = jnp.zeros((tm, D), F32)
        for j in range(N_CHIPS):
            ds = _dot_nt(dfb, wd_ref[j])
            av = a_ref[j].astype(F32)
            uv = u_ref[j].astype(F32)
            sig = _sigmoid(av)
            sil = av * sig
            dab = (ds * uv * (sig * (1.0 + av * (1.0 - sig)))).astype(BF16)
            dub = (ds * sil).astype(BF16)
            dh = dh + _dot_nt(dab, wg_ref[j]) + _dot_nt(dub, wu_ref[j])
            s_ref[j] = (sil * uv).astype(BF16)
            da_ref[j] = dab
            du_ref[j] = dub
        xv = x_ref[...]
        xh, r = _rms_stats(xv)
        nwv = nw_ref[...]
        n = xh * nwv
        scale1 = 1.0 + sc_ref[0]
        dsc_ref[0] += _colsum(dh * n)
        dsh_ref[0] += _colsum(dh)
        dx, dw_rows = _rms_bwd(dh * scale1, xh, r, nwv)
        dnw_ref[...] += _colsum(dw_rows)
        dx_ref[...] = dxo_v + dx
        h_ref[...] = (n * scale1 + sh_ref[0]).astype(BF16)
        df_ref[...] = dfb

    rows = lambda n: pl.BlockSpec((tm, n), lambda i: (i, 0))
    act = pl.BlockSpec((N_CHIPS, tm, fs), lambda i: (0, i, 0))
    perb = pl.BlockSpec((1, 1, D), lambda i: (i // tps, 0, 0))
    sd = jax.ShapeDtypeStruct
    return pl.pallas_call(
        body, grid=(T // tm,), name=name,
        in_specs=[rows(D), rows(D), _resident((1, D)), perb, perb, perb, act, act, rows(D),
                  _resident((N_CHIPS, D, fs)), _resident((N_CHIPS, D, fs)), _resident((N_CHIPS, fs, D))],
        out_specs=[rows(D), rows(D), act, rows(D), act, act, perb, perb, perb, pl.BlockSpec((1, D), lambda i: (0, 0))],
        out_shape=[sd((T, D), F32), sd((T, D), BF16), sd((N_CHIPS, T, fs), BF16), sd((T, D), BF16),
                   sd((N_CHIPS, T, fs), BF16), sd((N_CHIPS, T, fs), BF16), sd((B, 1, D), F32), sd((B, 1, D), F32),
                   sd((B, 1, D), F32), sd((1, D), F32)],
        compiler_params=_cparams(("arbitrary",)),
    )(dxo, x, nw, sh, sc, g, a, u, f, wg, wu, wd)


def _ffn_wgrad(h, s, df, da, du, name):
    T, D = h.shape
    fs = s.shape[2]
    tt = min(TOKEN_TILE, T)

    def body(h_ref, s_ref, df_ref, da_ref, du_ref, dgate_ref, dup_ref, ddown_ref):
        @pl.when(pl.program_id(1) == 0)
        def _():
            dgate_ref[...] = jnp.zeros_like(dgate_ref)
            dup_ref[...] = jnp.zeros_like(dup_ref)
            ddown_ref[...] = jnp.zeros_like(ddown_ref)

        hv = h_ref[...]
        dgate_ref[0] += _dot_tn(hv, da_ref[0])
        dup_ref[0] += _dot_tn(hv, du_ref[0])
        ddown_ref[0] += _dot_tn(s_ref[0], df_ref[...])

    rows = pl.BlockSpec((tt, D), lambda j, t: (t, 0))
    act = pl.BlockSpec((1, tt, fs), lambda j, t: (j, t, 0))
    wcol = pl.BlockSpec((1, D, fs), lambda j, t: (j, 0, 0))
    return pl.pallas_call(
        body, grid=(N_CHIPS, T // tt), name=name,
        in_specs=[rows, act, rows, act, act],
        out_specs=[wcol, wcol, pl.BlockSpec((1, fs, D), lambda j, t: (j, 0, 0))],
        out_shape=[jax.ShapeDtypeStruct((N_CHIPS, D, fs), F32), jax.ShapeDtypeStruct((N_CHIPS, D, fs), F32),
                   jax.ShapeDtypeStruct((N_CHIPS, fs, D), F32)],
        compiler_params=_cparams(("arbitrary", "arbitrary")),
    )(h, s, df, da, du)


def _mm_tn(xa, ya, tn, name):
    T, K = xa.shape
    N = ya.shape[1]
    tt = min(TOKEN_TILE, T)

    def body(x_ref, y_ref, o_ref):
        @pl.when(pl.program_id(1) == 0)
        def _():
            o_ref[...] = jnp.zeros_like(o_ref)

        o_ref[...] += _dot_tn(x_ref[...], y_ref[...])

    return pl.pallas_call(
        body, grid=(N // tn, T // tt), name=name,
        in_specs=[pl.BlockSpec((tt, K), lambda j, t: (t, 0)), pl.BlockSpec((tt, tn), lambda j, t: (t, j))],
        out_specs=pl.BlockSpec((K, tn), lambda j, t: (0, j)),
        out_shape=jax.ShapeDtypeStruct((K, N), F32),
        compiler_params=_cparams(("arbitrary", "arbitrary")),
    )(xa, ya)


def _final_loss(x, nw, tgt):
    T, D = x.shape
    tm = min(TOKEN_TILE, T)

    def body(x_ref, nw_ref, t_ref, dx_ref, loss_ref, dnw_ref):
        @pl.when(pl.program_id(0) == 0)
        def _():
            loss_ref[...] = jnp.zeros_like(loss_ref)
            dnw_ref[...] = jnp.zeros_like(dnw_ref)

        xv = x_ref[...]
        xh, r = _rms_stats(xv)
        nwv = nw_ref[...]
        err = xh * nwv - t_ref[...]
        loss_ref[...] += (0.5 / D) * jnp.sum(err * err)
        dx, dw_rows = _rms_bwd(err * (1.0 / D), xh, r, nwv)
        dx_ref[...] = dx
        dnw_ref[...] += _colsum(dw_rows)

    rows = pl.BlockSpec((tm, D), lambda i: (i, 0))
    return pl.pallas_call(
        body, grid=(T // tm,), name="final_loss",
        in_specs=[rows, _resident((1, D)), rows],
        out_specs=[rows, pl.BlockSpec((8, LANES), lambda i: (0, 0)), pl.BlockSpec((1, D), lambda i: (0, 0))],
        out_shape=[jax.ShapeDtypeStruct((T, D), F32), jax.ShapeDtypeStruct((8, LANES), F32),
                   jax.ShapeDtypeStruct((1, D), F32)],
        compiler_params=_cparams(("arbitrary",)),
    )(x, nw, tgt)


_PROJ_SPLITS = (0, 1024, 2560, 2944, 3200, 3328)


def _inproj_fwd(x, nw, sh, sc, win, seq):
    T, D = x.shape
    tm = min(TOKEN_TILE, seq)
    tps = seq // tm
    widths = [b - a for a, b in zip(_PROJ_SPLITS[:-1], _PROJ_SPLITS[1:])]
    dtypes = [BF16, BF16, F32, F32, F32]

    def body(x_ref, nw_ref, sh_ref, sc_ref, w_ref, *outs):
        xh, _ = _rms_stats(x_ref[...])
        h = (xh * nw_ref[...]) * (1.0 + sc_ref[0]) + sh_ref[0]
        proj = _dot(h.astype(BF16), w_ref[...])
        for o, lo, hi in zip(outs, _PROJ_SPLITS[:-1], _PROJ_SPLITS[1:]):
            o[...] = proj[:, lo:hi].astype(o.dtype)

    rows = lambda n: pl.BlockSpec((tm, n), lambda i: (i, 0))
    perb = pl.BlockSpec((1, 1, D), lambda i: (i // tps, 0, 0))
    return pl.pallas_call(
        body, grid=(T // tm,), name="inproj_fwd",
        in_specs=[rows(D), _resident((1, D)), perb, perb, _resident((D, D_PROJ))],
        out_specs=[rows(w) for w in widths],
        out_shape=[jax.ShapeDtypeStruct((T, w), dt) for w, dt in zip(widths, dtypes)],
        compiler_params=_cparams(("arbitrary",)),
    )(x, nw, sh, sc, win)


def _inproj_bwd(dx2, x, nw, sh, sc, win, dz, dxbc, dcq, dckv, ddtk_a, ddtk_b, seq):
    T, D = x.shape
    B = T // seq
    tm = min(TOKEN_TILE, seq)
    tps = seq // tm

    def body(dx2_ref, x_ref, nw_ref, sh_ref, sc_ref, w_ref, dz_ref, dxbc_ref, dcq_ref, dckv_ref, da_ref, db_ref,
             dx_ref, h_ref, dp_ref, dsh_ref, dsc_ref, dnw_ref):
        i = pl.program_id(0)

        @pl.when(i % tps == 0)
        def _():
            dsh_ref[...] = jnp.zeros_like(dsh_ref)
            dsc_ref[...] = jnp.zeros_like(dsc_ref)

        @pl.when(i == 0)
        def _():
            dnw_ref[...] = jnp.zeros_like(dnw_ref)

        dproj = jnp.concatenate(
            [dz_ref[...], dxbc_ref[...], dcq_ref[...].astype(BF16), dckv_ref[...].astype(BF16),
             (da_ref[...] + db_ref[...]).astype(BF16)], axis=1)
        dp_ref[...] = dproj
        dh = _dot_nt(dproj, w_ref[...])
        xh, r = _rms_stats(x_ref[...])
        nwv = nw_ref[...]
        n = xh * nwv
        scale1 = 1.0 + sc_ref[0]
        dsc_ref[0] += _colsum(dh * n)
        dsh_ref[0] += _colsum(dh)
        dx, dw_rows = _rms_bwd(dh * scale1, xh, r, nwv)
        dnw_ref[...] += _colsum(dw_rows)
        dx_ref[...] = dx2_ref[...] + dx
        h_ref[...] = (n * scale1 + sh_ref[0]).astype(BF16)

    rows = lambda n: pl.BlockSpec((tm, n), lambda i: (i, 0))
    perb = pl.BlockSpec((1, 1, D), lambda i: (i // tps, 0, 0))
    sd = jax.ShapeDtypeStruct
    return pl.pallas_call(
        body, grid=(T // tm,), name="inproj_bwd",
        in_specs=[rows(D), rows(D), _resident((1, D)), perb, perb, _resident((D, D_PROJ)),
                  rows(1024), rows(D_CONV), rows(Q_LORA), rows(KV_LORA), rows(LANES), rows(LANES)],
        out_specs=[rows(D), rows(D), rows(D_PROJ), perb, perb, pl.BlockSpec((1, D), lambda i: (0, 0))],
        out_shape=[sd((T, D), F32), sd((T, D), BF16), sd((T, D_PROJ), BF16), sd((B, 1, D), F32), sd((B, 1, D), F32),
                   sd((1, D), F32)],
        compiler_params=_cparams(("arbitrary",)),
    )(dx2, x, nw, sh, sc, win, dz, dxbc, dcq, dckv, ddtk_a, ddtk_b)


def _shift_down(v, k, row):
    return jnp.where(row < k, 0.0, pltpu.roll(v, k, 0))


def _shift_up(v, k, row, n):
    return jnp.where(row >= n - k, 0.0, pltpu.roll(v, n - k, 0))


def _conv_pre(xv, w_ref, b_ref, row):
    pre = b_ref[...] + w_ref[CONV_WIDTH - 1:CONV_WIDTH, :] * xv
    for k in range(1, CONV_WIDTH):
        pre = pre + w_ref[CONV_WIDTH - 1 - k:CONV_WIDTH - k, :] * _shift_down(xv, k, row)
    return pre


def _conv_fwd(xraw, cw, cb):
    B, S, C = xraw.shape

    def body(x_ref, w_ref, b_ref, o_ref):
        xv = x_ref[0].astype(F32)
        row = lax.broadcasted_iota(jnp.int32, xv.shape, 0)
        pre = _conv_pre(xv, w_ref, b_ref, row)
        o_ref[0] = (pre * _sigmoid(pre)).astype(BF16)

    blk = pl.BlockSpec((1, S, LANES), lambda b, j: (b, 0, j))
    return pl.pallas_call(
        body, grid=(B, C // LANES), name="conv_fwd",
        in_specs=[blk, pl.BlockSpec((CONV_WIDTH, LANES), lambda b, j: (0, j)), pl.BlockSpec((1, LANES), lambda b, j: (0, j))],
        out_specs=blk, out_shape=jax.ShapeDtypeStruct((B, S, C), BF16),
        compiler_params=_cparams(("arbitrary", "arbitrary")),
    )(xraw, cw, cb)


def _conv_bwd(dout, xraw, cw, cb):
    B, S, C = xraw.shape

    def body(d_ref, x_ref, w_ref, b_ref, dx_ref, dw_ref, db_ref):
        @pl.when(pl.program_id(1) == 0)
        def _():
            dw_ref[...] = jnp.zeros_like(dw_ref)
            db_ref[...] = jnp.zeros_like(db_ref)

        xv = x_ref[0].astype(F32)
        row = lax.broadcasted_iota(jnp.int32, xv.shape, 0)
        pre = _conv_pre(xv, w_ref, b_ref, row)
        sig = _sigmoid(pre)
        dpre = d_ref[0].astype(F32) * (sig * (1.0 + pre * (1.0 - sig)))
        dx = w_ref[CONV_WIDTH - 1:CONV_WIDTH, :] * dpre
        for k in range(1, CONV_WIDTH):
            dx = dx + w_ref[CONV_WIDTH - 1 - k:CONV_WIDTH - k, :] * _shift_up(dpre, k, row, S)
        dx_ref[0] = dx.astype(BF16)
        db_ref[...] += _colsum(dpre)
        dws = [_colsum(dpre * (xv if k == 0 else _shift_down(xv, k, row))) for k in range(CONV_WIDTH - 1, -1, -1)]
        dw_ref[...] += jnp.concatenate(dws, axis=0)

    blk = pl.BlockSpec((1, S, LANES), lambda j, b: (b, 0, j))
    wspec = pl.BlockSpec((CONV_WIDTH, LANES), lambda j, b: (0, j))
    bspec = pl.BlockSpec((1, LANES), lambda j, b: (0, j))
    return pl.pallas_call(
        body, grid=(C // LANES, B), name="conv_bwd",
        in_specs=[blk, blk, wspec, bspec], out_specs=[blk, wspec, bspec],
        out_shape=[jax.ShapeDtypeStruct((B, S, C), BF16), jax.ShapeDtypeStruct((CONV_WIDTH, C), F32),
                   jax.ShapeDtypeStruct((1, C), F32)],
        compiler_params=_cparams(("arbitrary", "arbitrary")),
    )(dout, xraw, cw, cb)


def _softplus(x):
    return jnp.maximum(x, 0.0) + jnp.log(1.0 + jnp.exp(-jnp.abs(x)))


def _ssd_common(xbc_ref, dtk_ref, dtb_ref, alog_ref, e_ref):
    L = CHUNK
    xbc = xbc_ref[0]
    xs = xbc[:, :D_SSD].astype(F32)
    bm = xbc[:, D_SSD:D_SSD + 256]
    cm = xbc[:, D_SSD + 256:D_SSD + 512]
    head = lax.broadcasted_iota(jnp.int32, (1, LANES), 1) < SSD_HEADS
    a128 = jnp.where(head, -jnp.exp(alog_ref[...]), 0.0)
    pre = dtk_ref[0] + dtb_ref[...]
    dt = _softplus(pre)
    dA = dt * a128
    row = lax.broadcasted_iota(jnp.int32, (L, L), 0)
    col = lax.broadcasted_iota(jnp.int32, (L, L), 1)
    causal = col <= row
    tri = causal.astype(F32)
    triT = (row <= col).astype(F32)
    tri = causal.astype(BF16)
    triT = (row <= col).astype(BF16)
    dA3 = _split3(dA)
    acum = _sum3(lambda part: _dot(tri, part), dA3)
    acumT = _sum3(lambda part: _dot_tn(part, triT), dA3)
    E = e_ref[...]
    acum_f = _spread(acum, E)
    dt_f = _spread(dt, E)
    e_f = jnp.exp(acum_f)
    w_f = jnp.exp(acum_f[L - 1:L, :] - acum_f)
    xt = xs * dt_f
    return dict(xs=xs, bm=bm, cm=cm, a128=a128, pre=pre, dt=dt, causal=causal, tri=tri, triT=triT, acum=acum,
                acumT=acumT, E=E, dt_f=dt_f, e_f=e_f, w_f=w_f, xt=xt, head=head)


def _split3(x):
    p1 = x.astype(BF16)
    r1 = x - p1.astype(F32)
    p2 = r1.astype(BF16)
    return p1, p2, (r1 - p2.astype(F32)).astype(BF16)


def _sum3(mm, parts):
    return (mm(parts[0]) + mm(parts[1])) + mm(parts[2])


def _spread(v, e):
    return _sum3(lambda part: _dot(part, e), _split3(v))


def _gather_heads(v, e):
    return _sum3(lambda part: _dot_nt(part, e), _split3(v))


def _head_mask(k):
    lane = lax.broadcasted_iota(jnp.int32, (CHUNK, LANES), 1)
    return (lane >= SSD_HEAD_DIM) if k == 1 else (lane < SSD_HEAD_DIM)


def _pair_decay(alast, h0):
    row = lax.broadcasted_iota(jnp.int32, (2 * SSD_HEAD_DIM, SSD_STATE), 0)
    return jnp.exp(jnp.where(row < SSD_HEAD_DIM, alast[:, h0:h0 + 1], alast[:, h0 + 1:h0 + 2]))


def _decay_matrix(q, h):
    seg = q["acum"][:, h:h + 1] - q["acumT"][h:h + 1, :]
    return jnp.exp(jnp.where(q["causal"], seg, -1e30))


def _gated_norm(y, zz, nw):
    sig = _sigmoid(zz)
    sil = zz * sig
    yg = y * sil
    half = D_SSD // SSD_GROUPS
    parts = []
    for g in range(SSD_GROUPS):
        xh, r = _rms_stats(yg[:, g * half:(g + 1) * half])
        parts.append((xh, r))
    return sig, sil, parts


def _ssd_fwd(xbc, dtk, z, dtb, alog, dsk, nw, expand):
    B, S, _ = xbc.shape
    L = CHUNK
    nc = S // L

    def body(xbc_ref, dtk_ref, z_ref, dtb_ref, alog_ref, dsk_ref, nw_ref, e_ref, y_ref, ys_ref, prev_ref, st_ref):
        @pl.when(pl.program_id(1) == 0)
        def _():
            st_ref[...] = jnp.zeros_like(st_ref)

        q = _ssd_common(xbc_ref, dtk_ref, dtb_ref, alog_ref, e_ref)
        xtb = q["xt"].astype(BF16)
        xwb = (q["xt"] * q["w_f"]).astype(BF16)
        alast = q["acum"][L - 1:L, :]
        ys = []
        for g in range(SSD_GROUPS):
            bg = q["bm"][:, g * 128:(g + 1) * 128]
            cg = q["cm"][:, g * 128:(g + 1) * 128]
            G = _dot_nt(cg, bg)
            for pr in range(SSD_HEADS // SSD_GROUPS // 2):
                h0 = g * 8 + 2 * pr
                lo = h0 * SSD_HEAD_DIM
                xt_p = xtb[:, lo:lo + 128]
                ydiag = jnp.zeros((L, LANES), F32)
                for k in range(2):
                    M = (G * _decay_matrix(q, h0 + k)).astype(BF16)
                    ydiag = ydiag + _dot(M, jnp.where(_head_mask(k), xt_p, jnp.zeros_like(xt_p)))
                hp = st_ref[lo:lo + 128, :]
                prev_ref[0, 0, lo:lo + 128, :] = hp
                zoff = _dot_nt(cg, hp.astype(BF16))
                ys.append(ydiag + zoff * q["e_f"][:, lo:lo + 128])
                st_ref[lo:lo + 128, :] = _pair_decay(alast, h0) * hp + _dot_tn(xwb[:, lo:lo + 128], bg)
        y = jnp.concatenate(ys, axis=1) + dsk_ref[...] * q["xs"]
        y_ref[0] = y.astype(BF16)
        _, _, parts = _gated_norm(y, z_ref[0].astype(F32), nw_ref[...])
        half = D_SSD // SSD_GROUPS
        ys_ref[0] = jnp.concatenate(
            [xh * nw_ref[:, g * half:(g + 1) * half] for g, (xh, _) in enumerate(parts)], axis=1).astype(BF16)

    chunk = lambda n: pl.BlockSpec((1, L, n), lambda b, c: (b, c, 0))
    vec = pl.BlockSpec((1, LANES), lambda b, c: (0, 0))
    return pl.pallas_call(
        body, grid=(B, nc), name="ssd_fwd",
        in_specs=[chunk(D_CONV), chunk(LANES), chunk(D_SSD), vec, vec, pl.BlockSpec((1, D_SSD), lambda b, c: (0, 0)),
                  pl.BlockSpec((1, D_SSD), lambda b, c: (0, 0)), pl.BlockSpec((LANES, D_SSD), lambda b, c: (0, 0))],
        out_specs=[chunk(D_SSD), chunk(D_SSD), pl.BlockSpec((1, 1, D_SSD, SSD_STATE), lambda b, c: (b, c, 0, 0))],
        out_shape=[jax.ShapeDtypeStruct((B, S, D_SSD), BF16), jax.ShapeDtypeStruct((B, S, D_SSD), BF16),
                   jax.ShapeDtypeStruct((B, nc, D_SSD, SSD_STATE), F32)],
        scratch_shapes=[pltpu.VMEM((D_SSD, SSD_STATE), F32)],
        compiler_params=_cparams(("arbitrary", "arbitrary")),
    )(xbc, dtk, z, dtb, alog, dsk, nw, expand)


def _ssd_bwd(xbc, dtk, z, y, prev, dys, dtb, alog, dsk, nw, expand):
    B, S, _ = xbc.shape
    L = CHUNK
    nc = S // L
    half = D_SSD // SSD_GROUPS

    def body(xbc_ref, dtk_ref, z_ref, y_ref, prev_ref, dys_ref, dtb_ref, alog_ref, dsk_ref, nw_ref, e_ref,
             dxbc_ref, ddtk_ref, dz_ref, dnw_ref, dvec_ref, dh_ref, dskc_ref):
        @pl.when((pl.program_id(0) == 0) & (pl.program_id(1) == 0))
        def _():
            dnw_ref[...] = jnp.zeros_like(dnw_ref)
            dvec_ref[...] = jnp.zeros_like(dvec_ref)
            dskc_ref[...] = jnp.zeros_like(dskc_ref)

        @pl.when(pl.program_id(1) == 0)
        def _():
            dh_ref[...] = jnp.zeros_like(dh_ref)

        q = _ssd_common(xbc_ref, dtk_ref, dtb_ref, alog_ref, e_ref)
        E = q["E"]
        xs = q["xs"]
        yv = y_ref[0].astype(F32)
        zz = z_ref[0].astype(F32)
        sig, sil, parts = _gated_norm(yv, zz, nw_ref[...])
        dn = dys_ref[0].astype(F32)
        dyg, dnw_rows = [], []
        for g, (xh, r) in enumerate(parts):
            dpart, dw_rows = _rms_bwd(dn[:, g * half:(g + 1) * half], xh, r, nw_ref[:, g * half:(g + 1) * half])
            dyg.append(dpart)
            dnw_rows.append(dw_rows)
        dyg = jnp.concatenate(dyg, axis=1)
        dnw_ref[...] += _colsum(jnp.concatenate(dnw_rows, axis=1))
        dY = dyg * sil
        dz_ref[0] = (dyg * yv * (sig * (1.0 + zz * (1.0 - sig)))).astype(BF16)
        dsk_f = dsk_ref[...]
        dskc_ref[...] += _colsum(dY * xs)
        dYb = dY.astype(BF16)
        xtb = q["xt"].astype(BF16)
        xwb = (q["xt"] * q["w_f"]).astype(BF16)
        acum = q["acum"]
        alast = acum[L - 1:L, :]
        lane_id = lax.broadcasted_iota(jnp.int32, (L, LANES), 1)
        sub_id = lax.broadcasted_iota(jnp.int32, (LANES, L), 0)
        lane_row = lax.broadcasted_iota(jnp.int32, (1, LANES), 1)
        da_rows = jnp.zeros((L, LANES), F32)
        daT = jnp.zeros((LANES, L), F32)
        dxt, prod_off, prod_st, dbs, dcs = [], [], [], [], []
        hsum_row = jnp.zeros((1, LANES), F32)
        for g in range(SSD_GROUPS):
            bg = q["bm"][:, g * 128:(g + 1) * 128]
            cg = q["cm"][:, g * 128:(g + 1) * 128]
            G = _dot_nt(cg, bg)
            dG = jnp.zeros((L, L), F32)
            dcg = jnp.zeros((L, SSD_STATE), F32)
            dbg = jnp.zeros((L, SSD_STATE), F32)
            for pr in range(SSD_HEADS // SSD_GROUPS // 2):
                h0 = g * 8 + 2 * pr
                lo = h0 * SSD_HEAD_DIM
                cols = slice(lo, lo + 128)
                dY_p = dYb[:, cols]
                xt_p = xtb[:, cols]
                dxt_p = jnp.zeros((L, LANES), F32)
                for k in range(2):
                    h = h0 + k
                    Lm = _decay_matrix(q, h)
                    Mf = G * Lm
                    dYk = jnp.where(_head_mask(k), dY_p, jnp.zeros_like(dY_p))
                    dM = _dot_nt(dYk, xt_p)
                    dxt_p = dxt_p + _dot_tn(Mf.astype(BF16), dYk)
                    dG = dG + dM * Lm
                    Q = dM * Mf
                    da_rows = da_rows + jnp.where(lane_id == h, jnp.sum(Q, axis=1, keepdims=True), 0.0)
                    daT = daT + jnp.where(sub_id == h, jnp.sum(Q, axis=0, keepdims=True), 0.0)
                hp = prev_ref[0, 0, lo:lo + 128, :]
                hpb = hp.astype(BF16)
                zoff = _dot_nt(cg, hpb)
                e_p = q["e_f"][:, cols]
                dY_pf = dY[:, cols]
                dZb = (dY_pf * e_p).astype(BF16)
                dcg = dcg + _dot(dZb, hpb)
                dhp_off = _dot_tn(dZb, cg)
                prod_off.append(dY_pf * zoff * e_p)
                dS = dh_ref[lo:lo + 128, :]
                dSb = dS.astype(BF16)
                U = _dot_nt(bg, dSb)
                dxt_p = dxt_p + U * q["w_f"][:, cols]
                dbg = dbg + _dot(xwb[:, cols], dSb)
                prod_st.append(q["xt"][:, cols] * U)
                dh_ref[lo:lo + 128, :] = _pair_decay(alast, h0) * dS + dhp_off
                dsh = dS * hp
                for k in range(2):
                    total = jnp.sum(dsh[k * SSD_HEAD_DIM:(k + 1) * SSD_HEAD_DIM, :], axis=(0, 1), keepdims=True)
                    hsum_row = hsum_row + jnp.where(lane_row == h0 + k, total, 0.0)
                dxt.append(dxt_p)
            dGb = dG.astype(BF16)
            dcs.append(dcg + _dot(dGb, bg))
            dbs.append(dbg + _dot_tn(dGb, cg))
        dxt = jnp.concatenate(dxt, axis=1)
        da_rows = da_rows + _gather_heads(jnp.concatenate(prod_off, axis=1), E)
        dww = _gather_heads(jnp.concatenate(prod_st, axis=1), E) * jnp.exp(alast - acum)
        da_rows = da_rows - dww
        dlast = _colsum(dww) + jnp.exp(alast) * hsum_row
        triT = q["triT"]
        ddA = (_sum3(lambda part: _dot(triT, part), _split3(da_rows))
               - _sum3(lambda part: _dot_nt(triT, part), _split3(daT)) + dlast)
        ddA = jnp.where(q["head"], ddA, 0.0)
        ddt = ddA * q["a128"] + _gather_heads(dxt * xs, E)
        ddt_raw = jnp.where(q["head"], ddt * _sigmoid(q["pre"]), 0.0)
        ddtk_ref[0] = ddt_raw
        dxs = dxt * q["dt_f"] + dsk_f * dY
        dxbc_ref[0] = jnp.concatenate([dxs] + dbs + dcs, axis=1).astype(BF16)
        dvec_ref[0:1, :] += _colsum(ddt_raw)
        dvec_ref[1:2, :] += _colsum(ddA * q["dt"]) * q["a128"]

        @pl.when((pl.program_id(0) == B - 1) & (pl.program_id(1) == nc - 1))
        def _():
            dvec_ref[2:3, :] = _gather_heads(jnp.broadcast_to(dskc_ref[...], (8, D_SSD)), E)[0:1, :]

    rev = lambda n: pl.BlockSpec((1, L, n), lambda b, c: (b, nc - 1 - c, 0))
    vec = pl.BlockSpec((1, LANES), lambda b, c: (0, 0))
    sd = jax.ShapeDtypeStruct
    return pl.pallas_call(
        body, grid=(B, nc), name="ssd_bwd",
        in_specs=[rev(D_CONV), rev(LANES), rev(D_SSD), rev(D_SSD),
                  pl.BlockSpec((1, 1, D_SSD, SSD_STATE), lambda b, c: (b, nc - 1 - c, 0, 0)), rev(D_SSD), vec, vec,
                  pl.BlockSpec((1, D_SSD), lambda b, c: (0, 0)),
                  pl.BlockSpec((1, D_SSD), lambda b, c: (0, 0)), pl.BlockSpec((LANES, D_SSD), lambda b, c: (0, 0))],
        out_specs=[rev(D_CONV), rev(LANES), rev(D_SSD), pl.BlockSpec((1, D_SSD), lambda b, c: (0, 0)),
                   pl.BlockSpec((8, LANES), lambda b, c: (0, 0))],
        out_shape=[sd((B, S, D_CONV), BF16), sd((B, S, LANES), F32), sd((B, S, D_SSD), BF16), sd((1, D_SSD), F32),
                   sd((8, LANES), F32)],
        scratch_shapes=[pltpu.VMEM((D_SSD, SSD_STATE), F32), pltpu.VMEM((1, D_SSD), F32)],
        compiler_params=_cparams(("arbitrary", "arbitrary")),
    )(xbc, dtk, z, y, prev, dys, dtb, alog, dsk, nw, expand)


def _rope_tables(pos_ref, invf_ref):
    ang = pos_ref[...].astype(F32) * invf_ref[...]
    return jnp.cos(ang), jnp.sin(ang)


def _rot(u):
    lane = lax.broadcasted_iota(jnp.int32, u.shape, 1)
    first = (lane >= QK_NOPE) & (lane < QK_NOPE + QK_ROPE // 2)
    second = (lane >= QK_NOPE + QK_ROPE // 2) & (lane < QK_DIM)
    return jnp.where(first, -pltpu.roll(u, LANES - QK_ROPE // 2, 1), jnp.where(second, pltpu.roll(u, QK_ROPE // 2, 1), 0.0))


def _rope_lanes(shape):
    lane = lax.broadcasted_iota(jnp.int32, shape, 1)
    return (lane >= QK_NOPE) & (lane < QK_DIM)


def _mla_prep(cq, ckv, dtk, pos, qw, kvw, wuq, wukv, invf):
    T = cq.shape[0]
    tm = min(TOKEN_TILE, T)
    scale = 1.0 / math.sqrt(QK_DIM)
    HW = MLA_HEADS * HEAD_LANES

    def body(cq_ref, ckv_ref, dtk_ref, pos_ref, qw_ref, kvw_ref, wuq_ref, wukv_ref, invf_ref, q_ref, k_ref, v_ref):
        xh, _ = _rms_stats(cq_ref[...])
        qv = _dot((xh * qw_ref[...]).astype(BF16), wuq_ref[...])
        xh, _ = _rms_stats(ckv_ref[...])
        kv = _dot((xh * kvw_ref[...]).astype(BF16), wukv_ref[...])
        cosf, sinf = _rope_tables(pos_ref, invf_ref)
        rope = lambda u: u * cosf + _rot(u) * sinf
        dtkv = dtk_ref[...]
        kr = rope(jnp.where(_rope_lanes(dtkv.shape), dtkv, 0.0))
        for h in range(MLA_HEADS):
            cols = slice(h * HEAD_LANES, (h + 1) * HEAD_LANES)
            q_ref[:, cols] = (rope(qv[:, cols]) * scale).astype(BF16)
            k_ref[:, cols] = (kv[:, cols] + kr).astype(BF16)
        v_ref[...] = kv[:, HW:].astype(BF16)

    rows = lambda n: pl.BlockSpec((tm, n), lambda i: (i, 0))
    return pl.pallas_call(
        body, grid=(T // tm,), name="mla_prep",
        in_specs=[rows(Q_LORA), rows(KV_LORA), rows(LANES), rows(1), _resident((1, Q_LORA)), _resident((1, KV_LORA)),
                  _resident((Q_LORA, HW)), _resident((KV_LORA, 2 * HW)), _resident((1, LANES))],
        out_specs=[rows(HW), rows(HW), rows(HW)],
        out_shape=[jax.ShapeDtypeStruct((T, HW), BF16)] * 3,
        compiler_params=_cparams(("arbitrary",)),
    )(cq, ckv, dtk, pos, qw, kvw, wuq, wukv, invf)


def _mla_prep_bwd(dq, dk, dv, cq, ckv, pos, qw, kvw, wuq, wukv, invf):
    T = cq.shape[0]
    tm = min(TOKEN_TILE, T)
    scale = 1.0 / math.sqrt(QK_DIM)
    HW = MLA_HEADS * HEAD_LANES

    def body(dq_ref, dk_ref, dv_ref, cq_ref, ckv_ref, pos_ref, qw_ref, kvw_ref, wuq_ref, wukv_ref, invf_ref,
             dcq_ref, dckv_ref, ddtk_ref, qn_ref, kvn_ref, dqo_ref, dkvo_ref, dqw_ref, dkvw_ref):
        @pl.when(pl.program_id(0) == 0)
        def _():
            dqw_ref[...] = jnp.zeros_like(dqw_ref)
            dkvw_ref[...] = jnp.zeros_like(dkvw_ref)

        cosf, sinf = _rope_tables(pos_ref, invf_ref)
        unrope = lambda d: d * cosf - _rot(d * sinf)
        dkr = jnp.zeros((tm, LANES), F32)
        nope = lax.broadcasted_iota(jnp.int32, (tm, LANES), 1) < QK_NOPE
        for h in range(MLA_HEADS):
            cols = slice(h * HEAD_LANES, (h + 1) * HEAD_LANES)
            dqo_ref[:, cols] = unrope(dq_ref[:, cols] * scale).astype(BF16)
            dkh = dk_ref[:, cols]
            dkr = dkr + jnp.where(_rope_lanes(dkh.shape), dkh, 0.0)
            dkvo_ref[:, cols] = jnp.where(nope, dkh, 0.0).astype(BF16)
        dkvo_ref[:, HW:] = dv_ref[...].astype(BF16)
        ddtk_ref[...] = unrope(dkr)
        xh, r = _rms_stats(cq_ref[...])
        qn_ref[...] = (xh * qw_ref[...]).astype(BF16)
        dx, dw_rows = _rms_bwd(_dot_nt(dqo_ref[...], wuq_ref[...]), xh, r, qw_ref[...])
        dcq_ref[...] = dx
        dqw_ref[...] += _colsum(dw_rows)
        xh, r = _rms_stats(ckv_ref[...])
        kvn_ref[...] = (xh * kvw_ref[...]).astype(BF16)
        dx, dw_rows = _rms_bwd(_dot_nt(dkvo_ref[...], wukv_ref[...]), xh, r, kvw_ref[...])
        dckv_ref[...] = dx
        dkvw_ref[...] += _colsum(dw_rows)

    rows = lambda n: pl.BlockSpec((tm, n), lambda i: (i, 0))
    sd = jax.ShapeDtypeStruct
    return pl.pallas_call(
        body, grid=(T // tm,), name="mla_prep_bwd",
        in_specs=[rows(HW), rows(HW), rows(HW), rows(Q_LORA), rows(KV_LORA), rows(1), _resident((1, Q_LORA)),
                  _resident((1, KV_LORA)), _resident((Q_LORA, HW)), _resident((KV_LORA, 2 * HW)), _resident((1, LANES))],
        out_specs=[rows(Q_LORA), rows(KV_LORA), rows(LANES), rows(Q_LORA), rows(KV_LORA), rows(HW), rows(2 * HW),
                   pl.BlockSpec((1, Q_LORA), lambda i: (0, 0)), pl.BlockSpec((1, KV_LORA), lambda i: (0, 0))],
        out_shape=[sd((T, Q_LORA), F32), sd((T, KV_LORA), F32), sd((T, LANES), F32), sd((T, Q_LORA), BF16),
                   sd((T, KV_LORA), BF16), sd((T, HW), BF16), sd((T, 2 * HW), BF16), sd((1, Q_LORA), F32),
                   sd((1, KV_LORA), F32)],
        compiler_params=_cparams(("arbitrary",)),
    )(dq, dk, dv, cq, ckv, pos, qw, kvw, wuq, wukv, invf)


def _causal_mask(t):
    row = lax.broadcasted_iota(jnp.int32, (t, t), 0)
    col = lax.broadcasted_iota(jnp.int32, (t, t), 1)
    return col <= row


def _attn_fwd(q, k, v):
    B, S, HW = q.shape
    H = HW // HEAD_LANES
    t = min(ATTN_TILE, S)
    nq = S // t

    def body(q_ref, k_ref, v_ref, o_ref, lse_ref):
        qi = pl.program_id(2)
        qv = q_ref[0]

        def step(j, carry, masked):
            m, l, acc = carry
            sl = pl.ds(pl.multiple_of(j * t, t), t)
            s = _dot_nt(qv, k_ref[0, sl, :])
            if masked:
                s = jnp.where(_causal_mask(t), s, -1e30)
            m_new = jnp.maximum(m, jnp.max(s, axis=-1, keepdims=True))
            alpha = jnp.exp(m - m_new)
            p = jnp.exp(s - m_new)
            l = alpha * l + jnp.sum(p, axis=-1, keepdims=True)
            acc = alpha * acc + _dot(p.astype(BF16), v_ref[0, sl, :])
            return m_new, l, acc

        init = (jnp.full((t, 1), -1e30, F32), jnp.zeros((t, 1), F32), jnp.zeros((t, HEAD_LANES), F32))
        carry = lax.fori_loop(0, qi, lambda j, c: step(j, c, False), init)
        m, l, acc = step(qi, carry, True)
        o_ref[0] = (acc / l).astype(BF16)
        lse_ref[0, 0] = m + jnp.log(l)

    return pl.pallas_call(
        body, grid=(B, H, nq), name="attn_fwd",
        in_specs=[pl.BlockSpec((1, t, HEAD_LANES), lambda b, h, i: (b, i, h)),
                  pl.BlockSpec((1, S, HEAD_LANES), lambda b, h, i: (b, 0, h)),
                  pl.BlockSpec((1, S, HEAD_LANES), lambda b, h, i: (b, 0, h))],
        out_specs=[pl.BlockSpec((1, t, HEAD_LANES), lambda b, h, i: (b, i, h)),
                   pl.BlockSpec((1, 1, t, 1), lambda b, h, i: (b, h, i, 0))],
        out_shape=[jax.ShapeDtypeStruct((B, S, HW), BF16), jax.ShapeDtypeStruct((B, H, S, 1), F32)],
        compiler_params=_cparams(("arbitrary", "arbitrary", "arbitrary")),
    )(q, k, v)


def _attn_bwd(q, k, v, o, do, lse):
    B, S, HW = q.shape
    H = HW // HEAD_LANES
    t = min(ATTN_TILE, S)
    nq = S // t

    def body(q_ref, k_ref, v_ref, o_ref, do_ref, lse_ref, dq_ref, dk_ref, dv_ref):
        j = pl.program_id(2)

        @pl.when(j == 0)
        def _():
            dq_ref[...] = jnp.zeros_like(dq_ref)

        kj = k_ref[0]
        vj = v_ref[0]

        def step(i, carry, masked):
            dk, dv = carry
            sl = pl.ds(pl.multiple_of(i * t, t), t)
            qi = q_ref[0, sl, :]
            doi = do_ref[0, sl, :]
            s = _dot_nt(qi, kj)
            if masked:
                s = jnp.where(_causal_mask(t), s, -1e30)
            p = jnp.exp(s - lse_ref[0, 0, sl, :])
            dv = dv + _dot_tn(p.astype(BF16), doi)
            dp = _dot_nt(doi, vj)
            delta = jnp.sum(doi.astype(F32) * o_ref[0, sl, :].astype(F32), axis=-1, keepdims=True)
            dsb = (p * (dp - delta)).astype(BF16)
            dk = dk + _dot_tn(dsb, qi)
            dq_ref[0, sl, :] += _dot(dsb, kj)
            return dk, dv

        zero = jnp.zeros((t, HEAD_LANES), F32)
        carry = step(j, (zero, zero), True)
        dk, dv = lax.fori_loop(j + 1, nq, lambda i, c: step(i, c, False), carry)
        dk_ref[0] = dk
        dv_ref[0] = dv

    full = pl.BlockSpec((1, S, HEAD_LANES), lambda b, h, j: (b, 0, h))
    tile = pl.BlockSpec((1, t, HEAD_LANES), lambda b, h, j: (b, j, h))
    sd = jax.ShapeDtypeStruct
    return pl.pallas_call(
        body, grid=(B, H, nq), name="attn_bwd",
        in_specs=[full, tile, tile, full, full, pl.BlockSpec((1, 1, S, 1), lambda b, h, j: (b, h, 0, 0))],
        out_specs=[full, tile, tile],
        out_shape=[sd((B, S, HW), F32), sd((B, S, HW), F32), sd((B, S, HW), F32)],
        compiler_params=_cparams(("arbitrary", "arbitrary", "arbitrary")),
    )(q, k, v, o, do, lse)


def _mix_out(x1, yssd, o, mw, wout, g, seq):
    T, D = x1.shape
    tm = min(TOKEN_TILE, seq)
    tps = seq // tm

    def body(x_ref, ys_ref, o_ref, mw_ref, w_ref, g_ref, xo_ref, m_ref, yc_ref):
        xh, _ = _rms_stats(o_ref[...].astype(F32))
        ycat = jnp.concatenate([ys_ref[...], (xh * mw_ref[...]).astype(BF16)], axis=1)
        m = _dot(ycat, w_ref[...])
        xo_ref[...] = x_ref[...] + g_ref[0] * m
        m_ref[...] = m.astype(BF16)
        yc_ref[...] = ycat

    rows = lambda n: pl.BlockSpec((tm, n), lambda i: (i, 0))
    perb = pl.BlockSpec((1, 1, D), lambda i: (i // tps, 0, 0))
    sd = jax.ShapeDtypeStruct
    return pl.pallas_call(
        body, grid=(T // tm,), name="mix_out",
        in_specs=[rows(D), rows(D_SSD), rows(D_MLA), _resident((1, D_MLA)), _resident((D_SSD + D_MLA, D)), perb],
        out_specs=[rows(D), rows(D), rows(D_SSD + D_MLA)],
        out_shape=[sd((T, D), F32), sd((T, D), BF16), sd((T, D_SSD + D_MLA), BF16)],
        compiler_params=_cparams(("arbitrary",)),
    )(x1, yssd, o, mw, wout, g)


def _mix_out_bwd(dx2, m, o, mw, wout, g, seq):
    T, D = dx2.shape
    B = T // seq
    tm = min(TOKEN_TILE, seq)
    tps = seq // tm

    def body(dx_ref, m_ref, o_ref, mw_ref, w_ref, g_ref, dys_ref, do_ref, dm_ref, dg_ref, dmw_ref):
        i = pl.program_id(0)

        @pl.when(i % tps == 0)
        def _():
            dg_ref[...] = jnp.zeros_like(dg_ref)

        @pl.when(i == 0)
        def _():
            dmw_ref[...] = jnp.zeros_like(dmw_ref)

        dxv = dx_ref[...]
        dg_ref[0] += _colsum(dxv * m_ref[...].astype(F32))
        dmb = (g_ref[0] * dxv).astype(BF16)
        dm_ref[...] = dmb
        dycat = _dot_nt(dmb, w_ref[...])
        dys_ref[...] = dycat[:, :D_SSD].astype(BF16)
        xh, r = _rms_stats(o_ref[...].astype(F32))
        dx, dw_rows = _rms_bwd(dycat[:, D_SSD:], xh, r, mw_ref[...])
        do_ref[...] = dx.astype(BF16)
        dmw_ref[...] += _colsum(dw_rows)

    rows = lambda n: pl.BlockSpec((tm, n), lambda i: (i, 0))
    perb = pl.BlockSpec((1, 1, D), lambda i: (i // tps, 0, 0))
    sd = jax.ShapeDtypeStruct
    return pl.pallas_call(
        body, grid=(T // tm,), name="mix_out_bwd",
        in_specs=[rows(D), rows(D), rows(D_MLA), _resident((1, D_MLA)), _resident((D_SSD + D_MLA, D)), perb],
        out_specs=[rows(D_SSD), rows(D_MLA), rows(D), perb, pl.BlockSpec((1, D_MLA), lambda i: (0, 0))],
        out_shape=[sd((T, D_SSD), BF16), sd((T, D_MLA), BF16), sd((T, D), BF16), sd((B, 1, D), F32), sd((1, D_MLA), F32)],
        compiler_params=_cparams(("arbitrary",)),
    )(dx2, m, o, mw, wout, g)


def _win_to_kernel(w):
    z0 = jnp.zeros((w.shape[0], 48), w.dtype)
    z1 = jnp.zeros((w.shape[0], 32), w.dtype)
    return jnp.concatenate([w[:, :2560], w[:, 2576:3216], w[:, 2560:2576], z0, w[:, 3216:3248], z1], axis=1)


def _win_from_kernel(g):
    return jnp.concatenate([g[:, :2560], g[:, 3200:3216], g[:, 2560:3200], g[:, 3264:3296]], axis=1)


def _wuq_to_kernel(w):
    w = w.reshape(Q_LORA, MLA_HEADS, QK_DIM)
    return jnp.pad(w, ((0, 0), (0, 0), (0, HEAD_LANES - QK_DIM))).reshape(Q_LORA, MLA_HEADS * HEAD_LANES)


def _wuq_from_kernel(g):
    return g.reshape(Q_LORA, MLA_HEADS, HEAD_LANES)[:, :, :QK_DIM].reshape(Q_LORA, MLA_HEADS * QK_DIM)


def _wukv_to_kernel(w):
    w = w.reshape(KV_LORA, MLA_HEADS, QK_NOPE + V_HEAD)
    kp = jnp.pad(w[:, :, :QK_NOPE], ((0, 0), (0, 0), (0, HEAD_LANES - QK_NOPE)))
    return jnp.concatenate([kp.reshape(KV_LORA, -1), w[:, :, QK_NOPE:].reshape(KV_LORA, -1)], axis=1)


def _wukv_from_kernel(g):
    hw = MLA_HEADS * HEAD_LANES
    kp = g[:, :hw].reshape(KV_LORA, MLA_HEADS, HEAD_LANES)[:, :, :QK_NOPE]
    vp = g[:, hw:].reshape(KV_LORA, MLA_HEADS, V_HEAD)
    return jnp.concatenate([kp, vp], axis=2).reshape(KV_LORA, MLA_HEADS * (QK_NOPE + V_HEAD))


def _lanes16(v):
    return jnp.pad(v.reshape(1, SSD_HEADS), ((0, 0), (0, LANES - SSD_HEADS)))


def _constants():
    e = np.zeros((LANES, D_SSD), np.float32)
    for h in range(SSD_HEADS):
        e[h, h * SSD_HEAD_DIM:(h + 1) * SSD_HEAD_DIM] = 1.0
    inv_freq = ROPE_THETA ** (-jnp.arange(0, QK_ROPE, 2, dtype=F32) / QK_ROPE)
    half = QK_ROPE // 2
    invf = jnp.zeros((1, LANES), F32).at[0, QK_NOPE:QK_NOPE + half].set(inv_freq).at[0, QK_NOPE + half:QK_DIM].set(inv_freq)
    return jnp.asarray(e, BF16), invf


def _local_step(x, positions, mod, w, small, tgt):
    B, S, D = x.shape
    T = B * S
    expand, invf = _constants()
    x0 = x.reshape(T, D)
    pos = positions.reshape(T, 1)
    mods = [mod[:, i * D:(i + 1) * D].reshape(B, 1, D) for i in range(N_MOD)]
    sh1, sc1, g1, sh2, sc2, g2, sh3, sc3, g3 = mods
    dtb, alog = _lanes16(small["dt_bias"]), _lanes16(small["a_log"])
    dsk = jnp.repeat(small["d_skip"].reshape(1, SSD_HEADS), SSD_HEAD_DIM, axis=1)

    x1, a1, u1, f1 = _ffn_fwd(x0, small["norm_ffn1"], sh1, sc1, g1, w["ffn1_w_gate"], w["ffn1_w_up"], w["ffn1_w_down"], S, "ffn1_fwd")
    z, xraw, cq, ckv, dtk = _inproj_fwd(x1, small["norm_mix"], sh2, sc2, w["w_in"], S)
    xraw3 = xraw.reshape(B, S, D_CONV)
    xbc = _conv_fwd(xraw3, small["conv_w"], small["conv_b"])
    dtk3, z3 = dtk.reshape(B, S, LANES), z.reshape(B, S, D_SSD)
    y, yssd, prev = _ssd_fwd(xbc, dtk3, z3, dtb, alog, dsk, small["ssd_norm_w"], expand)
    q, k, v = _mla_prep(cq, ckv, dtk, pos, small["q_norm_w"], small["kv_norm_w"], w["w_uq"], w["w_ukv"], invf)
    hw = MLA_HEADS * HEAD_LANES
    q3, k3, v3 = q.reshape(B, S, hw), k.reshape(B, S, hw), v.reshape(B, S, hw)
    o3, lse = _attn_fwd(q3, k3, v3)
    o = o3.reshape(T, hw)
    x2, m, ycat = _mix_out(x1, yssd.reshape(T, D_SSD), o, small["mla_norm_w"], w["w_out"], g2, S)
    x3, a2, u2, f2 = _ffn_fwd(x2, small["norm_ffn2"], sh3, sc3, g3, w["ffn2_w_gate"], w["ffn2_w_up"], w["ffn2_w_down"], S, "ffn2_fwd")
    dx3, loss, d_norm_final = _final_loss(x3, small["norm_final"].reshape(1, D), tgt.reshape(T, D))

    gw, gs = {}, {}
    dx2, h3, s3, df3, da3, du3, dsh3, dsc3, dg3, gs["norm_ffn2"] = _ffn_bwd(
        dx3, x2, small["norm_ffn2"], sh3, sc3, g3, a2, u2, f2, w["ffn2_w_gate"], w["ffn2_w_up"], w["ffn2_w_down"], S, "ffn2_bwd")
    gw["ffn2_w_gate"], gw["ffn2_w_up"], gw["ffn2_w_down"] = _ffn_wgrad(h3, s3, df3, da3, du3, "ffn2_wgrad")

    dys, do, dm, dg2, gs["mla_norm_w"] = _mix_out_bwd(dx2, m, o, small["mla_norm_w"], w["w_out"], g2, S)
    gw["w_out"] = _mm_tn(ycat, dm, 512, "dwout")

    dq3, dk3, dv3 = _attn_bwd(q3, k3, v3, o3, do.reshape(B, S, hw), lse)
    dcq, dckv, ddtk_b, qn, kvn, dqb, dkvb, gs["q_norm_w"], gs["kv_norm_w"] = _mla_prep_bwd(
        dq3.reshape(T, hw), dk3.reshape(T, hw), dv3.reshape(T, hw), cq, ckv, pos, small["q_norm_w"], small["kv_norm_w"],
        w["w_uq"], w["w_ukv"], invf)
    gw["w_uq"] = _mm_tn(qn, dqb, 512, "dwuq")
    gw["w_ukv"] = _mm_tn(kvn, dkvb, 1024, "dwukv")

    dxbc, ddtk_a, dz, gs["ssd_norm_w"], dvec = _ssd_bwd(
        xbc, dtk3, z3, y, prev, dys.reshape(B, S, D_SSD), dtb, alog, dsk, small["ssd_norm_w"], expand)
    gs["dt_bias"], gs["a_log"], gs["d_skip"] = dvec[0:1, :SSD_HEADS], dvec[1:2, :SSD_HEADS], dvec[2:3, :SSD_HEADS]
    dxraw, gs["conv_w"], gs["conv_b"] = _conv_bwd(dxbc, xraw3, small["conv_w"], small["conv_b"])
    dx1, h2, dproj, dsh2, dsc2, gs["norm_mix"] = _inproj_bwd(
        dx2, x1, small["norm_mix"], sh2, sc2, w["w_in"], dz.reshape(T, D_SSD), dxraw.reshape(T, D_CONV), dcq, dckv,
        ddtk_a.reshape(T, LANES), ddtk_b, S)
    gw["w_in"] = _mm_tn(h2, dproj, 1664, "dwin")

    dx0, h1, s1, df1, da1, du1, dsh1, dsc1, dg1, gs["norm_ffn1"] = _ffn_bwd(
        dx1, x0, small["norm_ffn1"], sh1, sc1, g1, a1, u1, f1, w["ffn1_w_gate"], w["ffn1_w_up"], w["ffn1_w_down"], S, "ffn1_bwd")
    gw["ffn1_w_gate"], gw["ffn1_w_up"], gw["ffn1_w_down"] = _ffn_wgrad(h1, s1, df1, da1, du1, "ffn1_wgrad")
    gs["norm_final"] = d_norm_final
    dmod = jnp.concatenate([t.reshape(B, D) for t in (dsh1, dsc1, dg1, dsh2, dsc2, dg2, dsh3, dsc3, dg3)], axis=1)
    return loss, dx0.reshape(B, S, D), gw, dmod, gs


HBM_SPEC = pl.BlockSpec(memory_space=pltpu.HBM)
VMEM_SPEC = pl.BlockSpec(memory_space=pltpu.VMEM)


def _place():
    return lax.axis_index("x"), lax.axis_index("y"), lax.axis_index("c")


def _other_chips(mx, my):
    return [(1 - mx, my), (mx, 1 - my), (1 - mx, 1 - my)]


def _remote(src, dst, send_sem, recv_sem, to):
    return pltpu.make_async_remote_copy(src_ref=src, dst_ref=dst, send_sem=send_sem, recv_sem=recv_sem,
                                        device_id=to, device_id_type=MESH)


def _all_gather_small(xa, name):
    r, n = xa.shape

    def body(x_ref, o_ref, send_sems, recv_sems):
        mx, my, mc = _place()
        me = 4 * mx + 2 * my + mc
        o_ref[pl.ds(me, 1)] = x_ref[...][None]
        sends = []
        for k in range(1, N_DEV):
            peer = (mx ^ (k >> 2), my ^ ((k >> 1) & 1), mc ^ (k & 1))
            cp = _remote(x_ref, o_ref.at[me], send_sems.at[k - 1], recv_sems.at[k - 1], peer)
            cp.start()
            sends.append(cp)
        for k in range(1, N_DEV):
            peer = (mx ^ (k >> 2), my ^ ((k >> 1) & 1), mc ^ (k & 1))
            slot = 4 * peer[0] + 2 * peer[1] + peer[2]
            _remote(x_ref, o_ref.at[slot], send_sems.at[k - 1], recv_sems.at[k - 1], peer).wait_recv()
        for cp in sends:
            cp.wait_send()

    return pl.pallas_call(
        body, name=name, in_specs=[VMEM_SPEC], out_specs=VMEM_SPEC,
        out_shape=jax.ShapeDtypeStruct((N_DEV, r, n), xa.dtype),
        scratch_shapes=[pltpu.SemaphoreType.DMA((N_DEV - 1,)), pltpu.SemaphoreType.DMA((N_DEV - 1,))],
        compiler_params=pltpu.CompilerParams(vmem_limit_bytes=VMEM_LIMIT),
    )(xa)


def _half_rows(ref, hc, rh, lead=None):
    rows = pl.ds(pl.multiple_of(hc * rh, 8), rh)
    return ref.at[rows, :] if lead is None else ref.at[lead, rows, :]


def _gather_weights(shards):
    n = len(shards)

    def body(*refs):
        w_refs, o_refs = refs[:n], refs[n:2 * n]
        send_sems, recv_sems, stage_sems = refs[2 * n:2 * n + 3]
        stages = refs[2 * n + 3:]
        mx, my, mc = _place()
        chip = 2 * mx + my
        others = _other_chips(mx, my)
        sibling = (mx, my, 1 - mc)
        stage_in = [pltpu.make_async_copy(w, st, stage_sems.at[0, i]) for i, (w, st) in enumerate(zip(w_refs, stages))]
        for cp in stage_in:
            cp.start()
        first = []
        for i, (w, o) in enumerate(zip(w_refs, o_refs)):
            rh = w.shape[0] // 2
            for k, (cx, cy) in enumerate(others):
                first.append(_remote(_half_rows(w, mc, rh), _half_rows(o, mc, rh, chip), send_sems.at[i, k],
                                     recv_sems.at[i, k], (cx, cy, mc)))
                first[-1].start()
        stage_out = []
        for i, (st, o) in enumerate(zip(stages, o_refs)):
            stage_in[i].wait()
            stage_out.append(pltpu.make_async_copy(st, o.at[chip], stage_sems.at[1, i]))
            stage_out[-1].start()
        passed = []
        for i, (w, o) in enumerate(zip(w_refs, o_refs)):
            rh = w.shape[0] // 2
            for k, (cx, cy) in enumerate(others):
                landed = _half_rows(o, mc, rh, 2 * cx + cy)
                _remote(landed, landed, send_sems.at[i, k], recv_sems.at[i, k], (cx, cy, mc)).wait_recv()
                passed.append(_remote(landed, landed, send_sems.at[i, 3 + k], recv_sems.at[i, 3 + k], sibling))
                passed[-1].start()
        for i, (w, o) in enumerate(zip(w_refs, o_refs)):
            rh = w.shape[0] // 2
            for k, (cx, cy) in enumerate(others):
                there = _half_rows(o, 1 - mc, rh, 2 * cx + cy)
                _remote(there, there, send_sems.at[i, 3 + k], recv_sems.at[i, 3 + k], sibling).wait_recv()
        for cp in first + passed:
            cp.wait_send()
        for cp in stage_out:
            cp.wait()

    return pl.pallas_call(
        body, name="gather_weights", in_specs=[HBM_SPEC] * n, out_specs=[HBM_SPEC] * n,
        out_shape=[jax.ShapeDtypeStruct((N_CHIPS,) + s.shape, s.dtype) for s in shards],
        scratch_shapes=[pltpu.SemaphoreType.DMA((n, 6)), pltpu.SemaphoreType.DMA((n, 6)), pltpu.SemaphoreType.DMA((2, n))]
        + [pltpu.VMEM(s.shape, s.dtype) for s in shards],
        compiler_params=pltpu.CompilerParams(vmem_limit_bytes=VMEM_LIMIT),
    )(*shards)


def _swap_halves(gs):
    n = len(gs)

    def body(*refs):
        g_refs, o_refs, send_sems, recv_sems = refs[:n], refs[n:2 * n], refs[2 * n], refs[2 * n + 1]
        mx, my, mc = _place()
        copies = []
        for i, (g, o) in enumerate(zip(g_refs, o_refs)):
            rh = g.shape[1] // 2
            src = g.at[:, pl.ds(pl.multiple_of((1 - mc) * rh, 8), rh), :]
            copies.append(_remote(src, o, send_sems.at[i], recv_sems.at[i], (mx, my, 1 - mc)))
            copies[-1].start()
        for cp in copies:
            cp.wait()

    return pl.pallas_call(
        body, name="swap_halves", in_specs=[HBM_SPEC] * n, out_specs=[HBM_SPEC] * n,
        out_shape=[jax.ShapeDtypeStruct((N_CHIPS, g.shape[1] // 2, g.shape[2]), g.dtype) for g in gs],
        scratch_shapes=[pltpu.SemaphoreType.DMA((n,)), pltpu.SemaphoreType.DMA((n,))],
    )(*gs)


def _pair_sum(g, got, core, name):
    _, r, c = g.shape
    rh = r // 2

    def body(core_ref, g_ref, got_ref, o_ref):
        o_ref[...] = (g_ref[...] + got_ref[...]).astype(BF16)

    return pl.pallas_call(
        body, name=name,
        grid_spec=pltpu.PrefetchScalarGridSpec(
            num_scalar_prefetch=1, grid=(N_CHIPS,),
            in_specs=[pl.BlockSpec((1, rh, c), lambda j, core_ref: (j, core_ref[0], 0)),
                      pl.BlockSpec((1, rh, c), lambda j, core_ref: (j, 0, 0))],
            out_specs=pl.BlockSpec((1, rh, c), lambda j, core_ref: (j, 0, 0))),
        out_shape=jax.ShapeDtypeStruct((N_CHIPS, rh, c), BF16),
        compiler_params=_cparams(("arbitrary",)),
    )(core, g, got)


def _scatter_chips(ss):
    n = len(ss)

    def body(*refs):
        s_refs, o_refs, send_sems, recv_sems = refs[:n], refs[n:2 * n], refs[2 * n], refs[2 * n + 1]
        mx, my, mc = _place()
        chip = 2 * mx + my
        others = _other_chips(mx, my)
        sends = []
        for i, (s, o) in enumerate(zip(s_refs, o_refs)):
            for k, (cx, cy) in enumerate(others):
                sends.append(_remote(s.at[2 * cx + cy], o.at[chip], send_sems.at[i, k], recv_sems.at[i, k], (cx, cy, mc)))
                sends[-1].start()
        for i, (s, o) in enumerate(zip(s_refs, o_refs)):
            for k, (cx, cy) in enumerate(others):
                slot = o.at[2 * cx + cy]
                _remote(slot, slot, send_sems.at[i, k], recv_sems.at[i, k], (cx, cy, mc)).wait_recv()
        for cp in sends:
            cp.wait_send()

    return pl.pallas_call(
        body, name="scatter_chips", in_specs=[HBM_SPEC] * n, out_specs=[HBM_SPEC] * n,
        out_shape=[jax.ShapeDtypeStruct(s.shape, s.dtype) for s in ss],
        scratch_shapes=[pltpu.SemaphoreType.DMA((n, 3)), pltpu.SemaphoreType.DMA((n, 3))],
    )(*ss)


def _chip_sum(own, got, chip, name):
    _, h, c = own.shape

    def body(chip_ref, a_ref, b_ref, c_ref, d_ref, o_ref):
        o_ref[...] = ((a_ref[0].astype(F32) + b_ref[0].astype(F32)) + c_ref[0].astype(F32)) + d_ref[0].astype(F32)

    slot = lambda flip: pl.BlockSpec((1, h, c), lambda i, chip_ref: (chip_ref[0] ^ flip, 0, 0))
    return pl.pallas_call(
        body, name=name,
        grid_spec=pltpu.PrefetchScalarGridSpec(
            num_scalar_prefetch=1, grid=(1,), in_specs=[slot(0), slot(1), slot(2), slot(3)],
            out_specs=pl.BlockSpec((h, c), lambda i, chip_ref: (0, 0))),
        out_shape=jax.ShapeDtypeStruct((h, c), F32),
        compiler_params=_cparams(("arbitrary",)),
    )(chip, own, got, got, got)


def _join_halves(mine):
    n = len(mine)

    def body(*refs):
        m_refs, o_refs, send_sems, recv_sems = refs[:n], refs[n:2 * n], refs[2 * n], refs[2 * n + 1]
        mx, my, mc = _place()
        copies = []
        for i, (m, o) in enumerate(zip(m_refs, o_refs)):
            copies.append(_remote(m, o, send_sems.at[i], recv_sems.at[i], (mx, my, 1 - mc)))
            copies[-1].start()
        for cp in copies:
            cp.wait()

    return pl.pallas_call(
        body, name="join_halves", in_specs=[HBM_SPEC] * n, out_specs=[HBM_SPEC] * n,
        out_shape=[jax.ShapeDtypeStruct(m.shape, m.dtype) for m in mine],
        scratch_shapes=[pltpu.SemaphoreType.DMA((n,)), pltpu.SemaphoreType.DMA((n,))],
    )(*mine)


def _adam_math(w, g, m, v):
    m2 = ADAM_B1 * m + (1.0 - ADAM_B1) * g
    v2 = ADAM_B2 * v + (1.0 - ADAM_B2) * (g * g)
    m_hat = m2 * (1.0 / (1.0 - ADAM_B1 ** ADAM_STEP))
    v_hat = v2 * (1.0 / (1.0 - ADAM_B2 ** ADAM_STEP))
    delta = -ADAM_LR * (m_hat / (jnp.sqrt(v_hat) + ADAM_EPS) + ADAM_WD * w)
    return delta, m2, v2


def _adam(w, g, m, v, name):
    def body(w_ref, g_ref, m_ref, v_ref, d_ref, m2_ref, v2_ref):
        d_ref[...], m2_ref[...], v2_ref[...] = _adam_math(w_ref[...], g_ref[...], m_ref[...], v_ref[...])

    return pl.pallas_call(body, name=name, out_shape=[jax.ShapeDtypeStruct(w.shape, F32)] * 3)(w, g, m, v)


def _adam_halves(w, m, v, mine, theirs, core, name):
    _, r, c = w.shape
    rh = r // 2

    def body(core_ref, w_ref, m_ref, v_ref, mine_ref, theirs_ref, g_ref, d_ref, m2_ref, v2_ref):
        g = jnp.where(pl.program_id(0) == core_ref[0], mine_ref[...], theirs_ref[...])
        g_ref[0] = g
        d_ref[0], m2_ref[0], v2_ref[0] = _adam_math(w_ref[0], g, m_ref[0], v_ref[0])

    half = pl.BlockSpec((1, rh, c), lambda hc, core_ref: (0, hc, 0))
    whole = pl.BlockSpec((rh, c), lambda hc, core_ref: (0, 0))
    return pl.pallas_call(
        body, name=name,
        grid_spec=pltpu.PrefetchScalarGridSpec(
            num_scalar_prefetch=1, grid=(2,), in_specs=[half, half, half, whole, whole], out_specs=[half] * 4),
        out_shape=[jax.ShapeDtypeStruct(w.shape, F32)] * 4,
        compiler_params=_cparams(("arbitrary",)),
    )(core, w, m, v, mine, theirs)


ADA_COLS = N_MOD * D_MODEL // N_CHIPS


def _ada_fwd(c_all, w_ada, b_cols):
    def body(c_ref, w_ref, b_ref, o_ref):
        cv = c_ref[...]
        act = (cv * _sigmoid(cv)).astype(BF16)
        o_ref[...] = _dot(act, w_ref[...].astype(BF16)) + b_ref[...]

    return pl.pallas_call(
        body, name="ada_fwd", out_shape=jax.ShapeDtypeStruct((c_all.shape[0], ADA_COLS), F32),
        compiler_params=pltpu.CompilerParams(vmem_limit_bytes=VMEM_LIMIT),
    )(c_all, w_ada, b_cols)


def _ada_bwd(c_all, dmod_cols, w, m, v):
    nb = c_all.shape[0]
    tn = 384

    def body(c_ref, d_ref, w_ref, m_ref, v_ref, g_ref, dl_ref, m2_ref, v2_ref):
        cv = c_ref[...]
        act = (cv * _sigmoid(cv)).astype(BF16)
        g = _dot_tn(act, d_ref[...].astype(BF16))
        g_ref[...] = g
        dl_ref[...], m2_ref[...], v2_ref[...] = _adam_math(w_ref[...], g, m_ref[...], v_ref[...])

    blk = pl.BlockSpec((D_MODEL, tn), lambda j: (0, j))
    return pl.pallas_call(
        body, name="ada_bwd", grid=(ADA_COLS // tn,),
        in_specs=[pl.BlockSpec((nb, D_MODEL), lambda j: (0, 0)), pl.BlockSpec((nb, tn), lambda j: (0, j)), blk, blk, blk],
        out_specs=[blk] * 4, out_shape=[jax.ShapeDtypeStruct((D_MODEL, ADA_COLS), F32)] * 4,
        compiler_params=_cparams(("arbitrary",)),
    )(c_all, dmod_cols, w, m, v)


SMALL_NAMES = ("norm_ffn1", "norm_mix", "conv_w", "conv_b", "ssd_norm_w", "q_norm_w", "kv_norm_w", "mla_norm_w",
               "norm_ffn2", "norm_final", "dt_bias", "a_log", "d_skip")
SMALL_SIZES = (1024, 1024, CONV_WIDTH * D_CONV, D_CONV, 1024, Q_LORA, KV_LORA, 1024, 1024, 1024, 16, 16, 16)
SMALL_ROWS = 16
MOD_ROWS = 2 * N_MOD
SEND_ROWS = 40


def _pack_small(parts):
    flat = jnp.concatenate([parts[n].reshape(-1) for n in SMALL_NAMES])
    return jnp.pad(flat, (0, SMALL_ROWS * D_MODEL - flat.shape[0]))


def _unpack_small(flat):
    out, off = {}, 0
    for n, size in zip(SMALL_NAMES, SMALL_SIZES):
        out[n] = flat[off:off + size]
        off += size
    return out


def _small_sum(got):
    def body(g_ref, o_ref):
        bsum = jnp.zeros((N_MOD, D_MODEL), F32)
        ssum = jnp.zeros((SMALL_ROWS, D_MODEL), F32)
        for d in range(N_DEV):
            bsum = bsum + g_ref[d, 0:N_MOD, :] + g_ref[d, N_MOD:MOD_ROWS, :]
            ssum = ssum + g_ref[d, MOD_ROWS:MOD_ROWS + SMALL_ROWS, :]
        o_ref[...] = jnp.concatenate([bsum, ssum, jnp.zeros((32 - N_MOD - SMALL_ROWS, D_MODEL), F32)], axis=0)

    return pl.pallas_call(body, name="small_sum", out_shape=jax.ShapeDtypeStruct((32, D_MODEL), F32))(got)


BIG_NAMES = ("ffn1_w_gate", "ffn1_w_up", "ffn1_w_down", "w_in", "w_uq", "w_ukv", "w_out", "ffn2_w_gate", "ffn2_w_up",
             "ffn2_w_down")
_TO_KERNEL = {"w_in": _win_to_kernel, "w_uq": _wuq_to_kernel, "w_ukv": _wukv_to_kernel}
_FROM_KERNEL = {"w_in": _win_from_kernel, "w_uq": _wuq_from_kernel, "w_ukv": _wukv_from_kernel}


def _columns_joined(w4):
    n, r, c = w4.shape
    return w4.transpose(1, 0, 2).reshape(r, n * c)


def _columns_split(g):
    r, cols = g.shape
    return g.reshape(r, N_CHIPS, cols // N_CHIPS).transpose(1, 0, 2)


def kernel(x, c, positions, w_ada, b_ada, norm_ffn1, ffn1_w_gate, ffn1_w_up, ffn1_w_down, norm_mix, w_in, conv_w, conv_b, dt_bias, a_log, d_skip, ssd_norm_w, q_norm_w, w_uq, kv_norm_w, w_ukv, mla_norm_w, w_out, norm_ffn2, ffn2_w_gate, ffn2_w_up, ffn2_w_down, norm_final, loss_target, m_w_ada, m_b_ada, m_norm_ffn1, m_ffn1_w_gate, m_ffn1_w_up, m_ffn1_w_down, m_norm_mix, m_w_in, m_conv_w, m_conv_b, m_dt_bias, m_a_log, m_d_skip, m_ssd_norm_w, m_q_norm_w, m_w_uq, m_kv_norm_w, m_w_ukv, m_mla_norm_w, m_w_out, m_norm_ffn2, m_ffn2_w_gate, m_ffn2_w_up, m_ffn2_w_down, m_norm_final, v_w_ada, v_b_ada, v_norm_ffn1, v_ffn1_w_gate, v_ffn1_w_up, v_ffn1_w_down, v_norm_mix, v_w_in, v_conv_w, v_conv_b, v_dt_bias, v_a_log, v_d_skip, v_ssd_norm_w, v_q_norm_w, v_w_uq, v_kv_norm_w, v_w_ukv, v_mla_norm_w, v_w_out, v_norm_ffn2, v_ffn2_w_gate, v_ffn2_w_up, v_ffn2_w_down, v_norm_final):
    a = dict(locals())
    B, S, D = x.shape
    mx, my, mc = _place()
    chip = 2 * mx + my
    dev = 2 * chip + mc
    core = mc.astype(jnp.int32).reshape(1)
    chip_id = chip.astype(jnp.int32).reshape(1)

    w = dict(zip(BIG_NAMES, _gather_weights([a[n][0].astype(BF16) for n in BIG_NAMES])))
    for n, to_kernel in _TO_KERNEL.items():
        w[n] = to_kernel(_columns_joined(w[n]))
    w["w_out"] = w["w_out"].reshape(D_SSD + D_MLA, D)

    cw_rows = jnp.pad(conv_w[0], ((0, 0), (0, D - conv_w.shape[2])))
    got = _all_gather_small(jnp.concatenate([c, cw_rows, jnp.zeros((8 - B - CONV_WIDTH, D), F32)], axis=0), "gather_c")
    c_all = got[:, :B, :].reshape(N_DEV * B, D)
    conv_full = got[::2, B:B + CONV_WIDTH, :conv_w.shape[2]].transpose(1, 0, 2).reshape(CONV_WIDTH, D_CONV)

    b_cols = lax.dynamic_slice(b_ada, (0, chip * ADA_COLS), (1, ADA_COLS))
    mod_all = _all_gather_small(_ada_fwd(c_all, w_ada[0], b_cols), "gather_mod")
    mod = lax.dynamic_slice(mod_all, (0, B * dev, 0), (N_DEV, B, ADA_COLS))[::2].transpose(1, 0, 2).reshape(B, N_MOD * D)

    small = {n: a[n].reshape(1, -1) for n in SMALL_NAMES if n not in ("conv_w", "norm_final")}
    small["conv_w"], small["norm_final"] = conv_full, norm_final
    loss_blk, grad_x, gw, dmod, gs = _local_step(x, positions, mod, w, small, loss_target)

    for n, from_kernel in _FROM_KERNEL.items():
        gw[n] = _columns_split(from_kernel(gw[n]))
    gw["w_out"] = gw["w_out"].reshape(N_CHIPS, (D_SSD + D_MLA) // N_CHIPS, D)
    g4 = [gw[n] for n in BIG_NAMES]
    pair = [_pair_sum(g, got, core, "pair_sum_" + n) for n, g, got in zip(BIG_NAMES, g4, _swap_halves(g4))]
    mine = [_chip_sum(own, got, chip_id, "chip_sum_" + n) for n, own, got in zip(BIG_NAMES, pair, _scatter_chips(pair))]
    grads, deltas, new_m, new_v = {}, {}, {}, {}
    for n, own, other in zip(BIG_NAMES, mine, _join_halves(mine)):
        grads[n], deltas[n], new_m[n], new_v[n] = _adam_halves(a[n], a["m_" + n], a["v_" + n], own, other, core, "adam_" + n)

    small_flat = _pack_small(gs).at[-1].set(loss_blk[0, 0])
    send = jnp.concatenate([dmod.reshape(MOD_ROWS, D), small_flat.reshape(SMALL_ROWS, D),
                            jnp.zeros((SEND_ROWS - MOD_ROWS - SMALL_ROWS, D), F32)], axis=0)
    got = _all_gather_small(send, "gather_small")
    summed = _small_sum(got)
    sums = summed[N_MOD:N_MOD + SMALL_ROWS].reshape(-1)
    loss = sums[-1]
    gsmall = _unpack_small(sums)
    gsmall["conv_w"] = lax.dynamic_slice(gsmall["conv_w"].reshape(CONV_WIDTH, D_CONV), (0, chip * conv_w.shape[2]),
                                         (CONV_WIDTH, conv_w.shape[2]))
    gsmall["b_ada"] = summed[:N_MOD]
    names = ("b_ada",) + SMALL_NAMES
    rows = 208

    def pack(parts):
        flat = jnp.concatenate([parts[n].reshape(-1) for n in names])
        return jnp.pad(flat, (0, rows * LANES - flat.shape[0])).reshape(rows, LANES)

    packed = [pack({n: a[p + n] for n in names}) for p in ("", "m_", "v_")]
    g_p = pack(gsmall)
    outs = (g_p,) + tuple(_adam(packed[0], g_p, packed[1], packed[2], "adam_small"))
    for dst, flat in zip((grads, deltas, new_m, new_v), outs):
        flat, off = flat.reshape(-1), 0
        for n in names:
            dst[n] = flat[off:off + a[n].size].reshape(a[n].shape)
            off += a[n].size

    dmod_all = got[:, :MOD_ROWS, :].reshape(N_DEV * B, N_MOD * D)
    dmod_cols = lax.dynamic_slice(dmod_all, (0, chip * ADA_COLS), (N_DEV * B, ADA_COLS))
    ada = _ada_bwd(c_all, dmod_cols, w_ada[0], m_w_ada[0], v_w_ada[0])
    for dst, t in zip((grads, deltas, new_m, new_v), ada):
        dst["w_ada"] = t[None]

    order = ("w_ada", "b_ada", "norm_ffn1", "ffn1_w_gate", "ffn1_w_up", "ffn1_w_down", "norm_mix", "w_in", "conv_w", "conv_b",
             "dt_bias", "a_log", "d_skip", "ssd_norm_w", "q_norm_w", "w_uq", "kv_norm_w", "w_ukv", "mla_norm_w", "w_out",
             "norm_ffn2", "ffn2_w_gate", "ffn2_w_up", "ffn2_w_down", "norm_final")
    return (loss, grad_x, *[grads[n] for n in order], *[deltas[n] for n in order], *[new_m[n] for n in order],
            *[new_v[n] for n in order])
```

```python
import functools
import math

import jax
import jax.numpy as jnp
import numpy as np
from jax import lax
from jax.experimental import pallas as pl
from jax.experimental.pallas import tpu as pltpu

F32 = jnp.float32
BF16 = jnp.bfloat16
HIGHEST = lax.Precision.HIGHEST

D_MODEL = 1024
D_FF = 2816
D_SSD = 1024
D_MLA = 1024
SSD_HEADS = 16
SSD_HEAD_DIM = 64
SSD_GROUPS = 2
SSD_STATE = 128
CONV_WIDTH = 4
CHUNK = 128
MLA_HEADS = 8
QK_NOPE = 64
QK_ROPE = 32
QK_DIM = QK_NOPE + QK_ROPE
V_HEAD = 128
Q_LORA = 384
KV_LORA = 256
ROPE_THETA = 10000.0
N_MOD = 9
EPS = 1e-6
D_CONV = D_SSD + 2 * SSD_GROUPS * SSD_STATE
D_PROJ = 3328
HEAD_LANES = 128
ADAM_LR = 0.001
ADAM_B1 = 0.9
ADAM_B2 = 0.999
ADAM_EPS = 1e-08
ADAM_WD = 0.01
ADAM_STEP = 10

LANES = 128
VMEM_LIMIT = 56 * 1024 * 1024
TOKEN_TILE = 512
ATTN_TILE = 512
N_CHIPS = 4
N_DEV = 8

MESH = pl.DeviceIdType.MESH


def _dot(a, b, precision=None):
    return jnp.dot(a, b, preferred_element_type=F32, precision=precision)


def _dot_nt(a, b, precision=None):
    return lax.dot_general(a, b, (((1,), (1,)), ((), ())), preferred_element_type=F32, precision=precision)


def _dot_tn(a, b, precision=None):
    return lax.dot_general(a, b, (((0,), (0,)), ((), ())), preferred_element_type=F32, precision=precision)


def _cparams(semantics):
    return pltpu.CompilerParams(dimension_semantics=semantics, vmem_limit_bytes=VMEM_LIMIT)


def _resident(shape):
    zeros = (0,) * len(shape)
    return pl.BlockSpec(shape, lambda *_: zeros, pipeline_mode=pl.Buffered(1))


def _sigmoid(x):
    return jax.nn.sigmoid(x)


def _rms_stats(x):
    r = lax.rsqrt(jnp.mean(x * x, axis=-1, keepdims=True) + EPS)
    return x * r, r


def _rms_bwd(dn, xh, r, w):
    dxh = dn * w
    dx = r * (dxh - xh * jnp.mean(dxh * xh, axis=-1, keepdims=True))
    return dx, dn * xh


def _colsum(v):
    return jnp.sum(v, axis=0, keepdims=True)


def _ffn_fwd(x, nw, sh, sc, g, wg, wu, wd, seq, name):
    T, D = x.shape
    fs = wg.shape[2]
    tm = min(TOKEN_TILE, seq)
    tps = seq // tm

    def body(x_ref, nw_ref, sh_ref, sc_ref, g_ref, wg_ref, wu_ref, wd_ref, xo_ref, a_ref, u_ref, f_ref):
        xv = x_ref[...]
        xh, _ = _rms_stats(xv)
        h = (xh * nw_ref[...]) * (1.0 + sc_ref[0]) + sh_ref[0]
        hb = h.astype(BF16)
        f = jnp.zeros((tm, D), F32)
        for j in range(N_CHIPS):
            a = _dot(hb, wg_ref[j])
            u = _dot(hb, wu_ref[j])
            a_ref[j] = a.astype(BF16)
            u_ref[j] = u.astype(BF16)
            f = f + _dot((a * _sigmoid(a) * u).astype(BF16), wd_ref[j])
        xo_ref[...] = xv + 0.5 * g_ref[0] * f
        f_ref[...] = f.astype(BF16)

    rows = lambda n: pl.BlockSpec((tm, n), lambda i: (i, 0))
    act = pl.BlockSpec((N_CHIPS, tm, fs), lambda i: (0, i, 0))
    perb = pl.BlockSpec((1, 1, D), lambda i: (i // tps, 0, 0))
    return pl.pallas_call(
        body, grid=(T // tm,), name=name,
        in_specs=[rows(D), _resident((1, D)), perb, perb, perb, _resident((N_CHIPS, D, fs)), _resident((N_CHIPS, D, fs)),
                  _resident((N_CHIPS, fs, D))],
        out_specs=[rows(D), act, act, rows(D)],
        out_shape=[jax.ShapeDtypeStruct((T, D), F32), jax.ShapeDtypeStruct((N_CHIPS, T, fs), BF16),
                   jax.ShapeDtypeStruct((N_CHIPS, T, fs), BF16), jax.ShapeDtypeStruct((T, D), BF16)],
        compiler_params=_cparams(("arbitrary",)),
    )(x, nw, sh, sc, g, wg, wu, wd)


def _ffn_bwd(dxo, x, nw, sh, sc, g, a, u, f, wg, wu, wd, seq, name):
    T, D = x.shape
    fs = wg.shape[2]
    B = T // seq
    tm = min(TOKEN_TILE // 2, seq)
    tps = seq // tm

    def body(dxo_ref, x_ref, nw_ref, sh_ref, sc_ref, g_ref, a_ref, u_ref, f_ref, wg_ref, wu_ref, wd_ref,
             dx_ref, h_ref, s_ref, df_ref, da_ref, du_ref, dsh_ref, dsc_ref, dg_ref, dnw_ref):
        i = pl.program_id(0)

        @pl.when(i % tps == 0)
        def _():
            dsh_ref[...] = jnp.zeros_like(dsh_ref)
            dsc_ref[...] = jnp.zeros_like(dsc_ref)
            dg_ref[...] = jnp.zeros_like(dg_ref)

        @pl.when(i == 0)
        def _():
            dnw_ref[...] = jnp.zeros_like(dnw_ref)

        dxo_v = dxo_ref[...]
        dfb = (0.5 * g_ref[0] * dxo_v).astype(BF16)
        dg_ref[0] += _colsum(0.5 * dxo_v * f_ref[...].astype(F32))
        dh = jnp.zeros((tm, D), F32)
        for j in range(N_CHIPS):
            ds = _dot_nt(dfb, wd_ref[j])
            av = a_ref[j].astype(F32)
            uv = u_ref[j].astype(F32)
            sig = _sigmoid(av)
            sil = av * sig
            dab = (ds * uv * (sig * (1.0 + av * (1.0 - sig)))).astype(BF16)
            dub = (ds * sil).astype(BF16)
            dh = dh + _dot_nt(dab, wg_ref[j]) + _dot_nt(dub, wu_ref[j])
            s_ref[j] = (sil * uv).astype(BF16)
            da_ref[j] = dab
            du_ref[j] = dub
        xv = x_ref[...]
        xh, r = _rms_stats(xv)
        nwv = nw_ref[...]
        n = xh * nwv
        scale1 = 1.0 + sc_ref[0]
        dsc_ref[0] += _colsum(dh * n)
        dsh_ref[0] += _colsum(dh)
        dx, dw_rows = _rms_bwd(dh * scale1, xh, r, nwv)
        dnw_ref[...] += _colsum(dw_rows)
        dx_ref[...] = dxo_v + dx
        h_ref[...] = (n * scale1 + sh_ref[0]).astype(BF16)
        df_ref[...] = dfb

    rows = lambda n: pl.BlockSpec((tm, n), lambda i: (i, 0))
    act = pl.BlockSpec((N_CHIPS, tm, fs), lambda i: (0, i, 0))
    perb = pl.BlockSpec((1, 1, D), lambda i: (i // tps, 0, 0))
    sd = jax.ShapeDtypeStruct
    return pl.pallas_call(
        body, grid=(T // tm,), name=name,
        in_specs=[rows(D), rows(D), _resident((1, D)), perb, perb, perb, act, act, rows(D),
                  _resident((N_CHIPS, D, fs)), _resident((N_CHIPS, D, fs)), _resident((N_CHIPS, fs, D))],
        out_specs=[rows(D), rows(D), act, rows(D), act, act, perb, perb, perb, pl.BlockSpec((1, D), lambda i: (0, 0))],
        out_shape=[sd((T, D), F32), sd((T, D), BF16), sd((N_CHIPS, T, fs), BF16), sd((T, D), BF16),
                   sd((N_CHIPS, T, fs), BF16), sd((N_CHIPS, T, fs), BF16), sd((B, 1, D), F32), sd((B, 1, D), F32),
                   sd((B, 1, D), F32), sd((1, D), F32)],
        compiler_params=_cparams(("arbitrary",)),
    )(dxo, x, nw, sh, sc, g, a, u, f, wg, wu, wd)


def _ffn_wgrad(h, s, df, da, du, name):
    T, D = h.shape
    fs = s.shape[2]
    tt = min(TOKEN_TILE, T)

    def body(h_ref, s_ref, df_ref, da_ref, du_ref, dgate_ref, dup_ref, ddown_ref):
        @pl.when(pl.program_id(1) == 0)
        def _():
            dgate_ref[...] = jnp.zeros_like(dgate_ref)
            dup_ref[...] = jnp.zeros_like(dup_ref)
            ddown_ref[...] = jnp.zeros_like(ddown_ref)

        hv = h_ref[...]
        dgate_ref[0] += _dot_tn(hv, da_ref[0])
        dup_ref[0] += _dot_tn(hv, du_ref[0])
        ddown_ref[0] += _dot_tn(s_ref[0], df_ref[...])

    rows = pl.BlockSpec((tt, D), lambda j, t: (t, 0))
    act = pl.BlockSpec((1, tt, fs), lambda j, t: (j, t, 0))
    wcol = pl.BlockSpec((1, D, fs), lambda j, t: (j, 0, 0))
    return pl.pallas_call(
        body, grid=(N_CHIPS, T // tt), name=name,
        in_specs=[rows, act, rows, act, act],
        out_specs=[wcol, wcol, pl.BlockSpec((1, fs, D), lambda j, t: (j, 0, 0))],
        out_shape=[jax.ShapeDtypeStruct((N_CHIPS, D, fs), F32), jax.ShapeDtypeStruct((N_CHIPS, D, fs), F32),
                   jax.ShapeDtypeStruct((N_CHIPS, fs, D), F32)],
        compiler_params=_cparams(("arbitrary", "arbitrary")),
    )(h, s, df, da, du)


def _mm_tn(xa, ya, tn, name):
    T, K = xa.shape
    N = ya.shape[1]
    tt = min(TOKEN_TILE, T)

    def body(x_ref, y_ref, o_ref):
        @pl.when(pl.program_id(1) == 0)
        def _():
            o_ref[...] = jnp.zeros_like(o_ref)

        o_ref[...] += _dot_tn(x_ref[...], y_ref[...])

    return pl.pallas_call(
        body, grid=(N // tn, T // tt), name=name,
        in_specs=[pl.BlockSpec((tt, K), lambda j, t: (t, 0)), pl.BlockSpec((tt, tn), lambda j, t: (t, j))],
        out_specs=pl.BlockSpec((K, tn), lambda j, t: (0, j)),
        out_shape=jax.ShapeDtypeStruct((K, N), F32),
        compiler_params=_cparams(("arbitrary", "arbitrary")),
    )(xa, ya)


def _final_loss(x, nw, tgt):
    T, D = x.shape
    tm = min(TOKEN_TILE, T)

    def body(x_ref, nw_ref, t_ref, dx_ref, loss_ref, dnw_ref):
        @pl.when(pl.program_id(0) == 0)
        def _():
            loss_ref[...] = jnp.zeros_like(loss_ref)
            dnw_ref[...] = jnp.zeros_like(dnw_ref)

        xv = x_ref[...]
        xh, r = _rms_stats(xv)
        nwv = nw_ref[...]
        err = xh * nwv - t_ref[...]
        loss_ref[...] += (0.5 / D) * jnp.sum(err * err)
        dx, dw_rows = _rms_bwd(err * (1.0 / D), xh, r, nwv)
        dx_ref[...] = dx
        dnw_ref[...] += _colsum(dw_rows)

    rows = pl.BlockSpec((tm, D), lambda i: (i, 0))
    return pl.pallas_call(
        body, grid=(T // tm,), name="final_loss",
        in_specs=[rows, _resident((1, D)), rows],
        out_specs=[rows, pl.BlockSpec((8, LANES), lambda i: (0, 0)), pl.BlockSpec((1, D), lambda i: (0, 0))],
        out_shape=[jax.ShapeDtypeStruct((T, D), F32), jax.ShapeDtypeStruct((8, LANES), F32),
                   jax.ShapeDtypeStruct((1, D), F32)],
        compiler_params=_cparams(("arbitrary",)),
    )(x, nw, tgt)


_PROJ_SPLITS = (0, 1024, 2560, 2944, 3200, 3328)


def _inproj_fwd(x, nw, sh, sc, win, seq):
    T, D = x.shape
    tm = min(TOKEN_TILE, seq)
    tps = seq // tm
    widths = [b - a for a, b in zip(_PROJ_SPLITS[:-1], _PROJ_SPLITS[1:])]
    dtypes = [BF16, BF16, F32, F32, F32]

    def body(x_ref, nw_ref, sh_ref, sc_ref, w_ref, *outs):
        xh, _ = _rms_stats(x_ref[...])
        h = (xh * nw_ref[...]) * (1.0 + sc_ref[0]) + sh_ref[0]
        proj = _dot(h.astype(BF16), w_ref[...])
        for o, lo, hi in zip(outs, _PROJ_SPLITS[:-1], _PROJ_SPLITS[1:]):
            o[...] = proj[:, lo:hi].astype(o.dtype)

    rows = lambda n: pl.BlockSpec((tm, n), lambda i: (i, 0))
    perb = pl.BlockSpec((1, 1, D), lambda i: (i // tps, 0, 0))
    return pl.pallas_call(
        body, grid=(T // tm,), name="inproj_fwd",
        in_specs=[rows(D), _resident((1, D)), perb, perb, _resident((D, D_PROJ))],
        out_specs=[rows(w) for w in widths],
        out_shape=[jax.ShapeDtypeStruct((T, w), dt) for w, dt in zip(widths, dtypes)],
        compiler_params=_cparams(("arbitrary",)),
    )(x, nw, sh, sc, win)


def _inproj_bwd(dx2, x, nw, sh, sc, win, dz, dxbc, dcq, dckv, ddtk_a, ddtk_b, seq):
    T, D = x.shape
    B = T // seq
    tm = min(TOKEN_TILE, seq)
    tps = seq // tm

    def body(dx2_ref, x_ref, nw_ref, sh_ref, sc_ref, w_ref, dz_ref, dxbc_ref, dcq_ref, dckv_ref, da_ref, db_ref,
             dx_ref, h_ref, dp_ref, dsh_ref, dsc_ref, dnw_ref):
        i = pl.program_id(0)

        @pl.when(i % tps == 0)
        def _():
            dsh_ref[...] = jnp.zeros_like(dsh_ref)
            dsc_ref[...] = jnp.zeros_like(dsc_ref)

        @pl.when(i == 0)
        def _():
            dnw_ref[...] = jnp.zeros_like(dnw_ref)

        dproj = jnp.concatenate(
            [dz_ref[...], dxbc_ref[...], dcq_ref[...].astype(BF16), dckv_ref[...].astype(BF16),
             (da_ref[...] + db_ref[...]).astype(BF16)], axis=1)
        dp_ref[...] = dproj
        dh = _dot_nt(dproj, w_ref[...])
        xh, r = _rms_stats(x_ref[...])
        nwv = nw_ref[...]
        n = xh * nwv
        scale1 = 1.0 + sc_ref[0]
        dsc_ref[0] += _colsum(dh * n)
        dsh_ref[0] += _colsum(dh)
        dx, dw_rows = _rms_bwd(dh * scale1, xh, r, nwv)
        dnw_ref[...] += _colsum(dw_rows)
        dx_ref[...] = dx2_ref[...] + dx
        h_ref[...] = (n * scale1 + sh_ref[0]).astype(BF16)

    rows = lambda n: pl.BlockSpec((tm, n), lambda i: (i, 0))
    perb = pl.BlockSpec((1, 1, D), lambda i: (i // tps, 0, 0))
    sd = jax.ShapeDtypeStruct
    return pl.pallas_call(
        body, grid=(T // tm,), name="inproj_bwd",
        in_specs=[rows(D), rows(D), _resident((1, D)), perb, perb, _resident((D, D_PROJ)),
                  rows(1024), rows(D_CONV), rows(Q_LORA), rows(KV_LORA), rows(LANES), rows(LANES)],
        out_specs=[rows(D), rows(D), rows(D_PROJ), perb, perb, pl.BlockSpec((1, D), lambda i: (0, 0))],
        out_shape=[sd((T, D), F32), sd((T, D), BF16), sd((T, D_PROJ), BF16), sd((B, 1, D), F32), sd((B, 1, D), F32),
                   sd((1, D), F32)],
        compiler_params=_cparams(("arbitrary",)),
    )(dx2, x, nw, sh, sc, win, dz, dxbc, dcq, dckv, ddtk_a, ddtk_b)


def _shift_down(v, k, row):
    return jnp.where(row < k, 0.0, pltpu.roll(v, k, 0))


def _shift_up(v, k, row, n):
    return jnp.where(row >= n - k, 0.0, pltpu.roll(v, n - k, 0))


def _conv_pre(xv, w_ref, b_ref, row):
    pre = b_ref[...] + w_ref[CONV_WIDTH - 1:CONV_WIDTH, :] * xv
    for k in range(1, CONV_WIDTH):
        pre = pre + w_ref[CONV_WIDTH - 1 - k:CONV_WIDTH - k, :] * _shift_down(xv, k, row)
    return pre


def _conv_fwd(xraw, cw, cb):
    B, S, C = xraw.shape

    def body(x_ref, w_ref, b_ref, o_ref):
        xv = x_ref[0].astype(F32)
        row = lax.broadcasted_iota(jnp.int32, xv.shape, 0)
        pre = _conv_pre(xv, w_ref, b_ref, row)
        o_ref[0] = (pre * _sigmoid(pre)).astype(BF16)

    blk = pl.BlockSpec((1, S, LANES), lambda b, j: (b, 0, j))
    return pl.pallas_call(
        body, grid=(B, C // LANES), name="conv_fwd",
        in_specs=[blk, pl.BlockSpec((CONV_WIDTH, LANES), lambda b, j: (0, j)), pl.BlockSpec((1, LANES), lambda b, j: (0, j))],
        out_specs=blk, out_shape=jax.ShapeDtypeStruct((B, S, C), BF16),
        compiler_params=_cparams(("arbitrary", "arbitrary")),
    )(xraw, cw, cb)


def _conv_bwd(dout, xraw, cw, cb):
    B, S, C = xraw.shape

    def body(d_ref, x_ref, w_ref, b_ref, dx_ref, dw_ref, db_ref):
        @pl.when(pl.program_id(1) == 0)
        def _():
            dw_ref[...] = jnp.zeros_like(dw_ref)
            db_ref[...] = jnp.zeros_like(db_ref)

        xv = x_ref[0].astype(F32)
        row = lax.broadcasted_iota(jnp.int32, xv.shape, 0)
        pre = _conv_pre(xv, w_ref, b_ref, row)
        sig = _sigmoid(pre)
        dpre = d_ref[0].astype(F32) * (sig * (1.0 + pre * (1.0 - sig)))
        dx = w_ref[CONV_WIDTH - 1:CONV_WIDTH, :] * dpre
        for k in range(1, CONV_WIDTH):
            dx = dx + w_ref[CONV_WIDTH - 1 - k:CONV_WIDTH - k, :] * _shift_up(dpre, k, row, S)
        dx_ref[0] = dx.astype(BF16)
        db_ref[...] += _colsum(dpre)
        dws = [_colsum(dpre * (xv if k == 0 else _shift_down(xv, k, row))) for k in range(CONV_WIDTH - 1, -1, -1)]
        dw_ref[...] += jnp.concatenate(dws, axis=0)

    blk = pl.BlockSpec((1, S, LANES), lambda j, b: (b, 0, j))
    wspec = pl.BlockSpec((CONV_WIDTH, LANES), lambda j, b: (0, j))
    bspec = pl.BlockSpec((1, LANES), lambda j, b: (0, j))
    return pl.pallas_call(
        body, grid=(C // LANES, B), name="conv_bwd",
        in_specs=[blk, blk, wspec, bspec], out_specs=[blk, wspec, bspec],
        out_shape=[jax.ShapeDtypeStruct((B, S, C), BF16), jax.ShapeDtypeStruct((CONV_WIDTH, C), F32),
                   jax.ShapeDtypeStruct((1, C), F32)],
        compiler_params=_cparams(("arbitrary", "arbitrary")),
    )(dout, xraw, cw, cb)


def _softplus(x):
    return jnp.maximum(x, 0.0) + jnp.log(1.0 + jnp.exp(-jnp.abs(x)))


def _ssd_common(xbc_ref, dtk_ref, dtb_ref, alog_ref, e_ref):
    L = CHUNK
    xbc = xbc_ref[0]
    xs = xbc[:, :D_SSD].astype(F32)
    bm = xbc[:, D_SSD:D_SSD + 256]
    cm = xbc[:, D_SSD + 256:D_SSD + 512]
    head = lax.broadcasted_iota(jnp.int32, (1, LANES), 1) < SSD_HEADS
    a128 = jnp.where(head, -jnp.exp(alog_ref[...]), 0.0)
    pre = dtk_ref[0] + dtb_ref[...]
    dt = _softplus(pre)
    dA = dt * a128
    row = lax.broadcasted_iota(jnp.int32, (L, L), 0)
    col = lax.broadcasted_iota(jnp.int32, (L, L), 1)
    causal = col <= row
    tri = causal.astype(F32)
    triT = (row <= col).astype(F32)
    tri = causal.astype(BF16)
    triT = (row <= col).astype(BF16)
    dA3 = _split3(dA)
    acum = _sum3(lambda part: _dot(tri, part), dA3)
    acumT = _sum3(lambda part: _dot_tn(part, triT), dA3)
    E = e_ref[...]
    acum_f = _spread(acum, E)
    dt_f = _spread(dt, E)
    e_f = jnp.exp(acum_f)
    w_f = jnp.exp(acum_f[L - 1:L, :] - acum_f)
    xt = xs * dt_f
    return dict(xs=xs, bm=bm, cm=cm, a128=a128, pre=pre, dt=dt, causal=causal, tri=tri, triT=triT, acum=acum,
                acumT=acumT, E=E, dt_f=dt_f, e_f=e_f, w_f=w_f, xt=xt, head=head)


def _split3(x):
    p1 = x.astype(BF16)
    r1 = x - p1.astype(F32)
    p2 = r1.astype(BF16)
    return p1, p2, (r1 - p2.astype(F32)).astype(BF16)


def _sum3(mm, parts):
    return (mm(parts[0]) + mm(parts[1])) + mm(parts[2])


def _spread(v, e):
    return _sum3(lambda part: _dot(part, e), _split3(v))


def _gather_heads(v, e):
    return _sum3(lambda part: _dot_nt(part, e), _split3(v))


def _head_mask(k):
    lane = lax.broadcasted_iota(jnp.int32, (CHUNK, LANES), 1)
    return (lane >= SSD_HEAD_DIM) if k == 1 else (lane < SSD_HEAD_DIM)


def _pair_decay(alast, h0):
    row = lax.broadcasted_iota(jnp.int32, (2 * SSD_HEAD_DIM, SSD_STATE), 0)
    return jnp.exp(jnp.where(row < SSD_HEAD_DIM, alast[:, h0:h0 + 1], alast[:, h0 + 1:h0 + 2]))


def _decay_matrix(q, h):
    seg = q["acum"][:, h:h + 1] - q["acumT"][h:h + 1, :]
    return jnp.exp(jnp.where(q["causal"], seg, -1e30))


def _gated_norm(y, zz, nw):
    sig = _sigmoid(zz)
    sil = zz * sig
    yg = y * sil
    half = D_SSD // SSD_GROUPS
    parts = []
    for g in range(SSD_GROUPS):
        xh, r = _rms_stats(yg[:, g * half:(g + 1) * half])
        parts.append((xh, r))
    return sig, sil, parts


def _ssd_fwd(xbc, dtk, z, dtb, alog, dsk, nw, expand):
    B, S, _ = xbc.shape
    L = CHUNK
    nc = S // L

    def body(xbc_ref, dtk_ref, z_ref, dtb_ref, alog_ref, dsk_ref, nw_ref, e_ref, y_ref, ys_ref, prev_ref, st_ref):
        @pl.when(pl.program_id(1) == 0)
        def _():
            st_ref[...] = jnp.zeros_like(st_ref)

        q = _ssd_common(xbc_ref, dtk_ref, dtb_ref, alog_ref, e_ref)
        xtb = q["xt"].astype(BF16)
        xwb = (q["xt"] * q["w_f"]).astype(BF16)
        alast = q["acum"][L - 1:L, :]
        ys = []
        for g in range(SSD_GROUPS):
            bg = q["bm"][:, g * 128:(g + 1) * 128]
            cg = q["cm"][:, g * 128:(g + 1) * 128]
            G = _dot_nt(cg, bg)
            for pr in range(SSD_HEADS // SSD_GROUPS // 2):
                h0 = g * 8 + 2 * pr
                lo = h0 * SSD_HEAD_DIM
                xt_p = xtb[:, lo:lo + 128]
                ydiag = jnp.zeros((L, LANES), F32)
                for k in range(2):
                    M = (G * _decay_matrix(q, h0 + k)).astype(BF16)
                    ydiag = ydiag + _dot(M, jnp.where(_head_mask(k), xt_p, jnp.zeros_like(xt_p)))
                hp = st_ref[lo:lo + 128, :]
                prev_ref[0, 0, lo:lo + 128, :] = hp
                zoff = _dot_nt(cg, hp.astype(BF16))
                ys.append(ydiag + zoff * q["e_f"][:, lo:lo + 128])
                st_ref[lo:lo + 128, :] = _pair_decay(alast, h0) * hp + _dot_tn(xwb[:, lo:lo + 128], bg)
        y = jnp.concatenate(ys, axis=1) + dsk_ref[...] * q["xs"]
        y_ref[0] = y.astype(BF16)
        _, _, parts = _gated_norm(y, z_ref[0].astype(F32), nw_ref[...])
        half = D_SSD // SSD_GROUPS
        ys_ref[0] = jnp.concatenate(
            [xh * nw_ref[:, g * half:(g + 1) * half] for g, (xh, _) in enumerate(parts)], axis=1).astype(BF16)

    chunk = lambda n: pl.BlockSpec((1, L, n), lambda b, c: (b, c, 0))
    vec = pl.BlockSpec((1, LANES), lambda b, c: (0, 0))
    return pl.pallas_call(
        body, grid=(B, nc), name="ssd_fwd",
        in_specs=[chunk(D_CONV), chunk(LANES), chunk(D_SSD), vec, vec, pl.BlockSpec((1, D_SSD), lambda b, c: (0, 0)),
                  pl.BlockSpec((1, D_SSD), lambda b, c: (0, 0)), pl.BlockSpec((LANES, D_SSD), lambda b, c: (0, 0))],
        out_specs=[chunk(D_SSD), chunk(D_SSD), pl.BlockSpec((1, 1, D_SSD, SSD_STATE), lambda b, c: (b, c, 0, 0))],
        out_shape=[jax.ShapeDtypeStruct((B, S, D_SSD), BF16), jax.ShapeDtypeStruct((B, S, D_SSD), BF16),
                   jax.ShapeDtypeStruct((B, nc, D_SSD, SSD_STATE), F32)],
        scratch_shapes=[pltpu.VMEM((D_SSD, SSD_STATE), F32)],
        compiler_params=_cparams(("arbitrary", "arbitrary")),
    )(xbc, dtk, z, dtb, alog, dsk, nw, expand)


def _ssd_bwd(xbc, dtk, z, y, prev, dys, dtb, alog, dsk, nw, expand):
    B, S, _ = xbc.shape
    L = CHUNK
    nc = S // L
    half = D_SSD // SSD_GROUPS

    def body(xbc_ref, dtk_ref, z_ref, y_ref, prev_ref, dys_ref, dtb_ref, alog_ref, dsk_ref, nw_ref, e_ref,
             dxbc_ref, ddtk_ref, dz_ref, dnw_ref, dvec_ref, dh_ref, dskc_ref):
        @pl.when((pl.program_id(0) == 0) & (pl.program_id(1) == 0))
        def _():
            dnw_ref[...] = jnp.zeros_like(dnw_ref)
            dvec_ref[...] = jnp.zeros_like(dvec_ref)
            dskc_ref[...] = jnp.zeros_like(dskc_ref)

        @pl.when(pl.program_id(1) == 0)
        def _():
            dh_ref[...] = jnp.zeros_like(dh_ref)

        q = _ssd_common(xbc_ref, dtk_ref, dtb_ref, alog_ref, e_ref)
        E = q["E"]
        xs = q["xs"]
        yv = y_ref[0].astype(F32)
        zz = z_ref[0].astype(F32)
        sig, sil, parts = _gated_norm(yv, zz, nw_ref[...])
        dn = dys_ref[0].astype(F32)
        dyg, dnw_rows = [], []
        for g, (xh, r) in enumerate(parts):
            dpart, dw_rows = _rms_bwd(dn[:, g * half:(g + 1) * half], xh, r, nw_ref[:, g * half:(g + 1) * half])
            dyg.append(dpart)
            dnw_rows.append(dw_rows)
        dyg = jnp.concatenate(dyg, axis=1)
        dnw_ref[...] += _colsum(jnp.concatenate(dnw_rows, axis=1))
        dY = dyg * sil
        dz_ref[0] = (dyg * yv * (sig * (1.0 + zz * (1.0 - sig)))).astype(BF16)
        dsk_f = dsk_ref[...]
        dskc_ref[...] += _colsum(dY * xs)
        dYb = dY.astype(BF16)
        xtb = q["xt"].astype(BF16)
        xwb = (q["xt"] * q["w_f"]).astype(BF16)
        acum = q["acum"]
        alast = acum[L - 1:L, :]
        lane_id = lax.broadcasted_iota(jnp.int32, (L, LANES), 1)
        sub_id = lax.broadcasted_iota(jnp.int32, (LANES, L), 0)
        lane_row = lax.broadcasted_iota(jnp.int32, (1, LANES), 1)
        da_rows = jnp.zeros((L, LANES), F32)
        daT = jnp.zeros((LANES, L), F32)
        dxt, prod_off, prod_st, dbs, dcs = [], [], [], [], []
        hsum_row = jnp.zeros((1, LANES), F32)
        for g in range(SSD_GROUPS):
            bg = q["bm"][:, g * 128:(g + 1) * 128]
            cg = q["cm"][:, g * 128:(g + 1) * 128]
            G = _dot_nt(cg, bg)
            dG = jnp.zeros((L, L), F32)
            dcg = jnp.zeros((L, SSD_STATE), F32)
            dbg = jnp.zeros((L, SSD_STATE), F32)
            for pr in range(SSD_HEADS // SSD_GROUPS // 2):
                h0 = g * 8 + 2 * pr
                lo = h0 * SSD_HEAD_DIM
                cols = slice(lo, lo + 128)
                dY_p = dYb[:, cols]
                xt_p = xtb[:, cols]
                dxt_p = jnp.zeros((L, LANES), F32)
                for k in range(2):
                    h = h0 + k
                    Lm = _decay_matrix(q, h)
                    Mf = G * Lm
                    dYk = jnp.where(_head_mask(k), dY_p, jnp.zeros_like(dY_p))
                    dM = _dot_nt(dYk, xt_p)
                    dxt_p = dxt_p + _dot_tn(Mf.astype(BF16), dYk)
                    dG = dG + dM * Lm
                    Q = dM * Mf
                    da_rows = da_rows + jnp.where(lane_id == h, jnp.sum(Q, axis=1, keepdims=True), 0.0)
                    daT = daT + jnp.where(sub_id == h, jnp.sum(Q, axis=0, keepdims=True), 0.0)
                hp = prev_ref[0, 0, lo:lo + 128, :]
                hpb = hp.astype(BF16)
                zoff = _dot_nt(cg, hpb)
                e_p = q["e_f"][:, cols]
                dY_pf = dY[:, cols]
                dZb = (dY_pf * e_p).astype(BF16)
                dcg = dcg + _dot(dZb, hpb)
                dhp_off = _dot_tn(dZb, cg)
                prod_off.append(dY_pf * zoff * e_p)
                dS = dh_ref[lo:lo + 128, :]
                dSb = dS.astype(BF16)
                U = _dot_nt(bg, dSb)
                dxt_p = dxt_p + U * q["w_f"][:, cols]
                dbg = dbg + _dot(xwb[:, cols], dSb)
                prod_st.append(q["xt"][:, cols] * U)
                dh_ref[lo:lo + 128, :] = _pair_decay(alast, h0) * dS + dhp_off
                dsh = dS * hp
                for k in range(2):
                    total = jnp.sum(dsh[k * SSD_HEAD_DIM:(k + 1) * SSD_HEAD_DIM, :], axis=(0, 1), keepdims=True)
                    hsum_row = hsum_row + jnp.where(lane_row == h0 + k, total, 0.0)
                dxt.append(dxt_p)
            dGb = dG.astype(BF16)
            dcs.append(dcg + _dot(dGb, bg))
            dbs.append(dbg + _dot_tn(dGb, cg))
        dxt = jnp.concatenate(dxt, axis=1)
        da_rows = da_rows + _gather_heads(jnp.concatenate(prod_off, axis=1), E)
        dww = _gather_heads(jnp.concatenate(prod_st, axis=1), E) * jnp.exp(alast - acum)
        da_rows = da_rows - dww
        dlast = _colsum(dww) + jnp.exp(alast) * hsum_row
        triT = q["triT"]
        ddA = (_sum3(lambda part: _dot(triT, part), _split3(da_rows))
               - _sum3(lambda part: _dot_nt(triT, part), _split3(daT)) + dlast)
        ddA = jnp.where(q["head"], ddA, 0.0)
        ddt = ddA * q["a128"] + _gather_heads(dxt * xs, E)
        ddt_raw = jnp.where(q["head"], ddt * _sigmoid(q["pre"]), 0.0)
        ddtk_ref[0] = ddt_raw
        dxs = dxt * q["dt_f"] + dsk_f * dY
        dxbc_ref[0] = jnp.concatenate([dxs] + dbs + dcs, axis=1).astype(BF16)
        dvec_ref[0:1, :] += _colsum(ddt_raw)
        dvec_ref[1:2, :] += _colsum(ddA * q["dt"]) * q["a128"]

        @pl.when((pl.program_id(0) == B - 1) & (pl.program_id(1) == nc - 1))
        def _():
            dvec_ref[2:3, :] = _gather_heads(jnp.broadcast_to(dskc_ref[...], (8, D_SSD)), E)[0:1, :]

    rev = lambda n: pl.BlockSpec((1, L, n), lambda b, c: (b, nc - 1 - c, 0))
    vec = pl.BlockSpec((1, LANES), lambda b, c: (0, 0))
    sd = jax.ShapeDtypeStruct
    return pl.pallas_call(
        body, grid=(B, nc), name="ssd_bwd",
        in_specs=[rev(D_CONV), rev(LANES), rev(D_SSD), rev(D_SSD),
                  pl.BlockSpec((1, 1, D_SSD, SSD_STATE), lambda b, c: (b, nc - 1 - c, 0, 0)), rev(D_SSD), vec, vec,
                  pl.BlockSpec((1, D_SSD), lambda b, c: (0, 0)),
                  pl.BlockSpec((1, D_SSD), lambda b, c: (0, 0)), pl.BlockSpec((LANES, D_SSD), lambda b, c: (0, 0))],
        out_specs=[rev(D_CONV), rev(LANES), rev(D_SSD), pl.BlockSpec((1, D_SSD), lambda b, c: (0, 0)),
                   pl.BlockSpec((8, LANES), lambda b, c: (0, 0))],
        out_shape=[sd((B, S, D_CONV), BF16), sd((B, S, LANES), F32), sd((B, S, D_SSD), BF16), sd((1, D_SSD), F32),
                   sd((8, LANES), F32)],
        scratch_shapes=[pltpu.VMEM((D_SSD, SSD_STATE), F32), pltpu.VMEM((1, D_SSD), F32)],
        compiler_params=_cparams(("arbitrary", "arbitrary")),
    )(xbc, dtk, z, y, prev, dys, dtb, alog, dsk, nw, expand)


def _rope_tables(pos_ref, invf_ref):
    ang = pos_ref[...].astype(F32) * invf_ref[...]
    return jnp.cos(ang), jnp.sin(ang)


def _rot(u):
    lane = lax.broadcasted_iota(jnp.int32, u.shape, 1)
    first = (lane >= QK_NOPE) & (lane < QK_NOPE + QK_ROPE // 2)
    second = (lane >= QK_NOPE + QK_ROPE // 2) & (lane < QK_DIM)
    return jnp.where(first, -pltpu.roll(u, LANES - QK_ROPE // 2, 1), jnp.where(second, pltpu.roll(u, QK_ROPE // 2, 1), 0.0))


def _rope_lanes(shape):
    lane = lax.broadcasted_iota(jnp.int32, shape, 1)
    return (lane >= QK_NOPE) & (lane < QK_DIM)


def _mla_prep(cq, ckv, dtk, pos, qw, kvw, wuq, wukv, invf):
    T = cq.shape[0]
    tm = min(TOKEN_TILE, T)
    scale = 1.0 / math.sqrt(QK_DIM)
    HW = MLA_HEADS * HEAD_LANES

    def body(cq_ref, ckv_ref, dtk_ref, pos_ref, qw_ref, kvw_ref, wuq_ref, wukv_ref, invf_ref, q_ref, k_ref, v_ref):
        xh, _ = _rms_stats(cq_ref[...])
        qv = _dot((xh * qw_ref[...]).astype(BF16), wuq_ref[...])
        xh, _ = _rms_stats(ckv_ref[...])
        kv = _dot((xh * kvw_ref[...]).astype(BF16), wukv_ref[...])
        cosf, sinf = _rope_tables(pos_ref, invf_ref)
        rope = lambda u: u * cosf + _rot(u) * sinf
        dtkv = dtk_ref[...]
        kr = rope(jnp.where(_rope_lanes(dtkv.shape), dtkv, 0.0))
        for h in range(MLA_HEADS):
            cols = slice(h * HEAD_LANES, (h + 1) * HEAD_LANES)
            q_ref[:, cols] = (rope(qv[:, cols]) * scale).astype(BF16)
            k_ref[:, cols] = (kv[:, cols] + kr).astype(BF16)
        v_ref[...] = kv[:, HW:].astype(BF16)

    rows = lambda n: pl.BlockSpec((tm, n), lambda i: (i, 0))
    return pl.pallas_call(
        body, grid=(T // tm,), name="mla_prep",
        in_specs=[rows(Q_LORA), rows(KV_LORA), rows(LANES), rows(1), _resident((1, Q_LORA)), _resident((1, KV_LORA)),
                  _resident((Q_LORA, HW)), _resident((KV_LORA, 2 * HW)), _resident((1, LANES))],
        out_specs=[rows(HW), rows(HW), rows(HW)],
        out_shape=[jax.ShapeDtypeStruct((T, HW), BF16)] * 3,
        compiler_params=_cparams(("arbitrary",)),
    )(cq, ckv, dtk, pos, qw, kvw, wuq, wukv, invf)


def _mla_prep_bwd(dq, dk, dv, cq, ckv, pos, qw, kvw, wuq, wukv, invf):
    T = cq.shape[0]
    tm = min(TOKEN_TILE, T)
    scale = 1.0 / math.sqrt(QK_DIM)
    HW = MLA_HEADS * HEAD_LANES

    def body(dq_ref, dk_ref, dv_ref, cq_ref, ckv_ref, pos_ref, qw_ref, kvw_ref, wuq_ref, wukv_ref, invf_ref,
             dcq_ref, dckv_ref, ddtk_ref, qn_ref, kvn_ref, dqo_ref, dkvo_ref, dqw_ref, dkvw_ref):
        @pl.when(pl.program_id(0) == 0)
        def _():
            dqw_ref[...] = jnp.zeros_like(dqw_ref)
            dkvw_ref[...] = jnp.zeros_like(dkvw_ref)

        cosf, sinf = _rope_tables(pos_ref, invf_ref)
        unrope = lambda d: d * cosf - _rot(d * sinf)
        dkr = jnp.zeros((tm, LANES), F32)
        nope = lax.broadcasted_iota(jnp.int32, (tm, LANES), 1) < QK_NOPE
        for h in range(MLA_HEADS):
            cols = slice(h * HEAD_LANES, (h + 1) * HEAD_LANES)
            dqo_ref[:, cols] = unrope(dq_ref[:, cols] * scale).astype(BF16)
            dkh = dk_ref[:, cols]
            dkr = dkr + jnp.where(_rope_lanes(dkh.shape), dkh, 0.0)
            dkvo_ref[:, cols] = jnp.where(nope, dkh, 0.0).astype(BF16)
        dkvo_ref[:, HW:] = dv_ref[...].astype(BF16)
        ddtk_ref[...] = unrope(dkr)
        xh, r = _rms_stats(cq_ref[...])
        qn_ref[...] = (xh * qw_ref[...]).astype(BF16)
        dx, dw_rows = _rms_bwd(_dot_nt(dqo_ref[...], wuq_ref[...]), xh, r, qw_ref[...])
        dcq_ref[...] = dx
        dqw_ref[...] += _colsum(dw_rows)
        xh, r = _rms_stats(ckv_ref[...])
        kvn_ref[...] = (xh * kvw_ref[...]).astype(BF16)
        dx, dw_rows = _rms_bwd(_dot_nt(dkvo_ref[...], wukv_ref[...]), xh, r, kvw_ref[...])
        dckv_ref[...] = dx
        dkvw_ref[...] += _colsum(dw_rows)

    rows = lambda n: pl.BlockSpec((tm, n), lambda i: (i, 0))
    sd = jax.ShapeDtypeStruct
    return pl.pallas_call(
        body, grid=(T // tm,), name="mla_prep_bwd",
        in_specs=[rows(HW), rows(HW), rows(HW), rows(Q_LORA), rows(KV_LORA), rows(1), _resident((1, Q_LORA)),
                  _resident((1, KV_LORA)), _resident((Q_LORA, HW)), _resident((KV_LORA, 2 * HW)), _resident((1, LANES))],
        out_specs=[rows(Q_LORA), rows(KV_LORA), rows(LANES), rows(Q_LORA), rows(KV_LORA), rows(HW), rows(2 * HW),
                   pl.BlockSpec((1, Q_LORA), lambda i: (0, 0)), pl.BlockSpec((1, KV_LORA), lambda i: (0, 0))],
        out_shape=[sd((T, Q_LORA), F32), sd((T, KV_LORA), F32), sd((T, LANES), F32), sd((T, Q_LORA), BF16),
                   sd((T, KV_LORA), BF16), sd((T, HW), BF16), sd((T, 2 * HW), BF16), sd((1, Q_LORA), F32),
                   sd((1, KV_LORA), F32)],
        compiler_params=_cparams(("arbitrary",)),
    )(dq, dk, dv, cq, ckv, pos, qw, kvw, wuq, wukv, invf)


def _causal_mask(t):
    row = lax.broadcasted_iota(jnp.int32, (t, t), 0)
    col = lax.broadcasted_iota(jnp.int32, (t, t), 1)
    return col <= row


def _attn_fwd(q, k, v):
    B, S, HW = q.shape
    H = HW // HEAD_LANES
    t = min(ATTN_TILE, S)
    nq = S // t

    def body(q_ref, k_ref, v_ref, o_ref, lse_ref):
        qi = pl.program_id(2)
        qv = q_ref[0]

        def step(j, carry, masked):
            m, l, acc = carry
            sl = pl.ds(pl.multiple_of(j * t, t), t)
            s = _dot_nt(qv, k_ref[0, sl, :])
            if masked:
                s = jnp.where(_causal_mask(t), s, -1e30)
            m_new = jnp.maximum(m, jnp.max(s, axis=-1, keepdims=True))
            alpha = jnp.exp(m - m_new)
            p = jnp.exp(s - m_new)
            l = alpha * l + jnp.sum(p, axis=-1, keepdims=True)
            acc = alpha * acc + _dot(p.astype(BF16), v_ref[0, sl, :])
            return m_new, l, acc

        init = (jnp.full((t, 1), -1e30, F32), jnp.zeros((t, 1), F32), jnp.zeros((t, HEAD_LANES), F32))
        carry = lax.fori_loop(0, qi, lambda j, c: step(j, c, False), init)
        m, l, acc = step(qi, carry, True)
        o_ref[0] = (acc / l).astype(BF16)
        lse_ref[0, 0] = m + jnp.log(l)

    return pl.pallas_call(
        body, grid=(B, H, nq), name="attn_fwd",
        in_specs=[pl.BlockSpec((1, t, HEAD_LANES), lambda b, h, i: (b, i, h)),
                  pl.BlockSpec((1, S, HEAD_LANES), lambda b, h, i: (b, 0, h)),
                  pl.BlockSpec((1, S, HEAD_LANES), lambda b, h, i: (b, 0, h))],
        out_specs=[pl.BlockSpec((1, t, HEAD_LANES), lambda b, h, i: (b, i, h)),
                   pl.BlockSpec((1, 1, t, 1), lambda b, h, i: (b, h, i, 0))],
        out_shape=[jax.ShapeDtypeStruct((B, S, HW), BF16), jax.ShapeDtypeStruct((B, H, S, 1), F32)],
        compiler_params=_cparams(("arbitrary", "arbitrary", "arbitrary")),
    )(q, k, v)


def _attn_bwd(q, k, v, o, do, lse):
    B, S, HW = q.shape
    H = HW // HEAD_LANES
    t = min(ATTN_TILE, S)
    nq = S // t

    def body(q_ref, k_ref, v_ref, o_ref, do_ref, lse_ref, dq_ref, dk_ref, dv_ref):
        j = pl.program_id(2)

        @pl.when(j == 0)
        def _():
            dq_ref[...] = jnp.zeros_like(dq_ref)

        kj = k_ref[0]
        vj = v_ref[0]

        def step(i, carry, masked):
            dk, dv = carry
            sl = pl.ds(pl.multiple_of(i * t, t), t)
            qi = q_ref[0, sl, :]
            doi = do_ref[0, sl, :]
            s = _dot_nt(qi, kj)
            if masked:
                s = jnp.where(_causal_mask(t), s, -1e30)
            p = jnp.exp(s - lse_ref[0, 0, sl, :])
            dv = dv + _dot_tn(p.astype(BF16), doi)
            dp = _dot_nt(doi, vj)
            delta = jnp.sum(doi.astype(F32) * o_ref[0, sl, :].astype(F32), axis=-1, keepdims=True)
            dsb = (p * (dp - delta)).astype(BF16)
            dk = dk + _dot_tn(dsb, qi)
            dq_ref[0, sl, :] += _dot(dsb, kj)
            return dk, dv

        zero = jnp.zeros((t, HEAD_LANES), F32)
        carry = step(j, (zero, zero), True)
        dk, dv = lax.fori_loop(j + 1, nq, lambda i, c: step(i, c, False), carry)
        dk_ref[0] = dk
        dv_ref[0] = dv

    full = pl.BlockSpec((1, S, HEAD_LANES), lambda b, h, j: (b, 0, h))
    tile = pl.BlockSpec((1, t, HEAD_LANES), lambda b, h, j: (b, j, h))
    sd = jax.ShapeDtypeStruct
    return pl.pallas_call(
        body, grid=(B, H, nq), name="attn_bwd",
        in_specs=[full, tile, tile, full, full, pl.BlockSpec((1, 1, S, 1), lambda b, h, j: (b, h, 0, 0))],
        out_specs=[full, tile, tile],
        out_shape=[sd((B, S, HW), F32), sd((B, S, HW), F32), sd((B, S, HW), F32)],
        compiler_params=_cparams(("arbitrary", "arbitrary", "arbitrary")),
    )(q, k, v, o, do, lse)


def _mix_out(x1, yssd, o, mw, wout, g, seq):
    T, D = x1.shape
    tm = min(TOKEN_TILE, seq)
    tps = seq // tm

    def body(x_ref, ys_ref, o_ref, mw_ref, w_ref, g_ref, xo_ref, m_ref, yc_ref):
        xh, _ = _rms_stats(o_ref[...].astype(F32))
        ycat = jnp.concatenate([ys_ref[...], (xh * mw_ref[...]).astype(BF16)], axis=1)
        m = _dot(ycat, w_ref[...])
        xo_ref[...] = x_ref[...] + g_ref[0] * m
        m_ref[...] = m.astype(BF16)
        yc_ref[...] = ycat

    rows = lambda n: pl.BlockSpec((tm, n), lambda i: (i, 0))
    perb = pl.BlockSpec((1, 1, D), lambda i: (i // tps, 0, 0))
    sd = jax.ShapeDtypeStruct
    return pl.pallas_call(
        body, grid=(T // tm,), name="mix_out",
        in_specs=[rows(D), rows(D_SSD), rows(D_MLA), _resident((1, D_MLA)), _resident((D_SSD + D_MLA, D)), perb],
        out_specs=[rows(D), rows(D), rows(D_SSD + D_MLA)],
        out_shape=[sd((T, D), F32), sd((T, D), BF16), sd((T, D_SSD + D_MLA), BF16)],
        compiler_params=_cparams(("arbitrary",)),
    )(x1, yssd, o, mw, wout, g)


def _mix_out_bwd(dx2, m, o, mw, wout, g, seq):
    T, D = dx2.shape
    B = T // seq
    tm = min(TOKEN_TILE, seq)
    tps = seq // tm

    def body(dx_ref, m_ref, o_ref, mw_ref, w_ref, g_ref, dys_ref, do_ref, dm_ref, dg_ref, dmw_ref):
        i = pl.program_id(0)

        @pl.when(i % tps == 0)
        def _():
            dg_ref[...] = jnp.zeros_like(dg_ref)

        @pl.when(i == 0)
        def _():
            dmw_ref[...] = jnp.zeros_like(dmw_ref)

        dxv = dx_ref[...]
        dg_ref[0] += _colsum(dxv * m_ref[...].astype(F32))
        dmb = (g_ref[0] * dxv).astype(BF16)
        dm_ref[...] = dmb
        dycat = _dot_nt(dmb, w_ref[...])
        dys_ref[...] = dycat[:, :D_SSD].astype(BF16)
        xh, r = _rms_stats(o_ref[...].astype(F32))
        dx, dw_rows = _rms_bwd(dycat[:, D_SSD:], xh, r, mw_ref[...])
        do_ref[...] = dx.astype(BF16)
        dmw_ref[...] += _colsum(dw_rows)

    rows = lambda n: pl.BlockSpec((tm, n), lambda i: (i, 0))
    perb = pl.BlockSpec((1, 1, D), lambda i: (i // tps, 0, 0))
    sd = jax.ShapeDtypeStruct
    return pl.pallas_call(
        body, grid=(T // tm,), name="mix_out_bwd",
        in_specs=[rows(D), rows(D), rows(D_MLA), _resident((1, D_MLA)), _resident((D_SSD + D_MLA, D)), perb],
        out_specs=[rows(D_SSD), rows(D_MLA), rows(D), perb, pl.BlockSpec((1, D_MLA), lambda i: (0, 0))],
        out_shape=[sd((T, D_SSD), BF16), sd((T, D_MLA), BF16), sd((T, D), BF16), sd((B, 1, D), F32), sd((1, D_MLA), F32)],
        compiler_params=_cparams(("arbitrary",)),
    )(dx2, m, o, mw, wout, g)


def _win_to_kernel(w):
    z0 = jnp.zeros((w.shape[0], 48), w.dtype)
    z1 = jnp.zeros((w.shape[0], 32), w.dtype)
    return jnp.concatenate([w[:, :2560], w[:, 2576:3216], w[:, 2560:2576], z0, w[:, 3216:3248], z1], axis=1)


def _win_from_kernel(g):
    return jnp.concatenate([g[:, :2560], g[:, 3200:3216], g[:, 2560:3200], g[:, 3264:3296]], axis=1)


def _wuq_to_kernel(w):
    w = w.reshape(Q_LORA, MLA_HEADS, QK_DIM)
    return jnp.pad(w, ((0, 0), (0, 0), (0, HEAD_LANES - QK_DIM))).reshape(Q_LORA, MLA_HEADS * HEAD_LANES)


def _wuq_from_kernel(g):
    return g.reshape(Q_LORA, MLA_HEADS, HEAD_LANES)[:, :, :QK_DIM].reshape(Q_LORA, MLA_HEADS * QK_DIM)


def _wukv_to_kernel(w):
    w = w.reshape(KV_LORA, MLA_HEADS, QK_NOPE + V_HEAD)
    kp = jnp.pad(w[:, :, :QK_NOPE], ((0, 0), (0, 0), (0, HEAD_LANES - QK_NOPE)))
    return jnp.concatenate([kp.reshape(KV_LORA, -1), w[:, :, QK_NOPE:].reshape(KV_LORA, -1)], axis=1)


def _wukv_from_kernel(g):
    hw = MLA_HEADS * HEAD_LANES
    kp = g[:, :hw].reshape(KV_LORA, MLA_HEADS, HEAD_LANES)[:, :, :QK_NOPE]
    vp = g[:, hw:].reshape(KV_LORA, MLA_HEADS, V_HEAD)
    return jnp.concatenate([kp, vp], axis=2).reshape(KV_LORA, MLA_HEADS * (QK_NOPE + V_HEAD))


def _lanes16(v):
    return jnp.pad(v.reshape(1, SSD_HEADS), ((0, 0), (0, LANES - SSD_HEADS)))


def _constants():
    e = np.zeros((LANES, D_SSD), np.float32)
    for h in range(SSD_HEADS):
        e[h, h * SSD_HEAD_DIM:(h + 1) * SSD_HEAD_DIM] = 1.0
    inv_freq = ROPE_THETA ** (-jnp.arange(0, QK_ROPE, 2, dtype=F32) / QK_ROPE)
    half = QK_ROPE // 2
    invf = jnp.zeros((1, LANES), F32).at[0, QK_NOPE:QK_NOPE + half].set(inv_freq).at[0, QK_NOPE + half:QK_DIM].set(inv_freq)
    return jnp.asarray(e, BF16), invf


def _local_step(x, positions, mod, w, later_weights, small, tgt, on_grads):
    B, S, D = x.shape
    T = B * S
    expand, invf = _constants()
    x0 = x.reshape(T, D)
    pos = positions.reshape(T, 1)
    mods = [mod[:, i * D:(i + 1) * D].reshape(B, 1, D) for i in range(N_MOD)]
    sh1, sc1, g1, sh2, sc2, g2, sh3, sc3, g3 = mods
    dtb, alog = _lanes16(small["dt_bias"]), _lanes16(small["a_log"])
    dsk = jnp.repeat(small["d_skip"].reshape(1, SSD_HEADS), SSD_HEAD_DIM, axis=1)

    x1, a1, u1, f1 = _ffn_fwd(x0, small["norm_ffn1"], sh1, sc1, g1, w["ffn1_w_gate"], w["ffn1_w_up"], w["ffn1_w_down"], S, "ffn1_fwd")
    w = {**w, **later_weights(f1)}
    z, xraw, cq, ckv, dtk = _inproj_fwd(x1, small["norm_mix"], sh2, sc2, w["w_in"], S)
    xraw3 = xraw.reshape(B, S, D_CONV)
    xbc = _conv_fwd(xraw3, small["conv_w"], small["conv_b"])
    dtk3, z3 = dtk.reshape(B, S, LANES), z.reshape(B, S, D_SSD)
    y, yssd, prev = _ssd_fwd(xbc, dtk3, z3, dtb, alog, dsk, small["ssd_norm_w"], expand)
    q, k, v = _mla_prep(cq, ckv, dtk, pos, small["q_norm_w"], small["kv_norm_w"], w["w_uq"], w["w_ukv"], invf)
    hw = MLA_HEADS * HEAD_LANES
    q3, k3, v3 = q.reshape(B, S, hw), k.reshape(B, S, hw), v.reshape(B, S, hw)
    o3, lse = _attn_fwd(q3, k3, v3)
    o = o3.reshape(T, hw)
    x2, m, ycat = _mix_out(x1, yssd.reshape(T, D_SSD), o, small["mla_norm_w"], w["w_out"], g2, S)
    x3, a2, u2, f2 = _ffn_fwd(x2, small["norm_ffn2"], sh3, sc3, g3, w["ffn2_w_gate"], w["ffn2_w_up"], w["ffn2_w_down"], S, "ffn2_fwd")
    dx3, loss, d_norm_final = _final_loss(x3, small["norm_final"].reshape(1, D), tgt.reshape(T, D))

    gw, gs = {}, {}
    dx2, h3, s3, df3, da3, du3, dsh3, dsc3, dg3, gs["norm_ffn2"] = _ffn_bwd(
        dx3, x2, small["norm_ffn2"], sh3, sc3, g3, a2, u2, f2, w["ffn2_w_gate"], w["ffn2_w_up"], w["ffn2_w_down"], S, "ffn2_bwd")
    gw["ffn2_w_gate"], gw["ffn2_w_up"], gw["ffn2_w_down"] = _ffn_wgrad(h3, s3, df3, da3, du3, "ffn2_wgrad")
    g2 = g2 + on_grads(("ffn2_w_gate", "ffn2_w_up", "ffn2_w_down"), gw)

    dys, do, dm, dg2, gs["mla_norm_w"] = _mix_out_bwd(dx2, m, o, small["mla_norm_w"], w["w_out"], g2, S)
    gw["w_out"] = _mm_tn(ycat, dm, 512, "dwout")

    dq3, dk3, dv3 = _attn_bwd(q3, k3, v3, o3, do.reshape(B, S, hw), lse)
    dcq, dckv, ddtk_b, qn, kvn, dqb, dkvb, gs["q_norm_w"], gs["kv_norm_w"] = _mla_prep_bwd(
        dq3.reshape(T, hw), dk3.reshape(T, hw), dv3.reshape(T, hw), cq, ckv, pos, small["q_norm_w"], small["kv_norm_w"],
        w["w_uq"], w["w_ukv"], invf)
    gw["w_uq"] = _mm_tn(qn, dqb, 512, "dwuq")
    gw["w_ukv"] = _mm_tn(kvn, dkvb, 1024, "dwukv")

    dxbc, ddtk_a, dz, gs["ssd_norm_w"], dvec = _ssd_bwd(
        xbc, dtk3, z3, y, prev, dys.reshape(B, S, D_SSD), dtb, alog, dsk, small["ssd_norm_w"], expand)
    gs["dt_bias"], gs["a_log"], gs["d_skip"] = dvec[0:1, :SSD_HEADS], dvec[1:2, :SSD_HEADS], dvec[2:3, :SSD_HEADS]
    dxraw, gs["conv_w"], gs["conv_b"] = _conv_bwd(dxbc, xraw3, small["conv_w"], small["conv_b"])
    dx1, h2, dproj, dsh2, dsc2, gs["norm_mix"] = _inproj_bwd(
        dx2, x1, small["norm_mix"], sh2, sc2, w["w_in"], dz.reshape(T, D_SSD), dxraw.reshape(T, D_CONV), dcq, dckv,
        ddtk_a.reshape(T, LANES), ddtk_b, S)
    gw["w_in"] = _mm_tn(h2, dproj, 1664, "dwin")
    g1 = g1 + on_grads(("w_in", "w_uq", "w_ukv", "w_out"), gw)

    dx0, h1, s1, df1, da1, du1, dsh1, dsc1, dg1, gs["norm_ffn1"] = _ffn_bwd(
        dx1, x0, small["norm_ffn1"], sh1, sc1, g1, a1, u1, f1, w["ffn1_w_gate"], w["ffn1_w_up"], w["ffn1_w_down"], S, "ffn1_bwd")
    gw["ffn1_w_gate"], gw["ffn1_w_up"], gw["ffn1_w_down"] = _ffn_wgrad(h1, s1, df1, da1, du1, "ffn1_wgrad")
    on_grads(("ffn1_w_gate", "ffn1_w_up", "ffn1_w_down"), gw)
    gs["norm_final"] = d_norm_final
    dmod = jnp.concatenate([t.reshape(B, D) for t in (dsh1, dsc1, dg1, dsh2, dsc2, dg2, dsh3, dsc3, dg3)], axis=1)
    return loss, dx0.reshape(B, S, D), gw, dmod, gs


HBM_SPEC = pl.BlockSpec(memory_space=pltpu.HBM)
VMEM_SPEC = pl.BlockSpec(memory_space=pltpu.VMEM)


def _place():
    return lax.axis_index("x"), lax.axis_index("y"), lax.axis_index("c")


def _other_chips(mx, my):
    return [(1 - mx, my), (mx, 1 - my), (1 - mx, 1 - my)]


def _remote(src, dst, send_sem, recv_sem, to):
    return pltpu.make_async_remote_copy(src_ref=src, dst_ref=dst, send_sem=send_sem, recv_sem=recv_sem,
                                        device_id=to, device_id_type=MESH)


def _all_gather_small(xa, name):
    r, n = xa.shape

    def body(x_ref, o_ref, send_sems, recv_sems):
        mx, my, mc = _place()
        me = 4 * mx + 2 * my + mc
        o_ref[pl.ds(me, 1)] = x_ref[...][None]
        sends = []
        for k in range(1, N_DEV):
            peer = (mx ^ (k >> 2), my ^ ((k >> 1) & 1), mc ^ (k & 1))
            cp = _remote(x_ref, o_ref.at[me], send_sems.at[k - 1], recv_sems.at[k - 1], peer)
            cp.start()
            sends.append(cp)
        for k in range(1, N_DEV):
            peer = (mx ^ (k >> 2), my ^ ((k >> 1) & 1), mc ^ (k & 1))
            slot = 4 * peer[0] + 2 * peer[1] + peer[2]
            _remote(x_ref, o_ref.at[slot], send_sems.at[k - 1], recv_sems.at[k - 1], peer).wait_recv()
        for cp in sends:
            cp.wait_send()

    return pl.pallas_call(
        body, name=name, in_specs=[VMEM_SPEC], out_specs=VMEM_SPEC,
        out_shape=jax.ShapeDtypeStruct((N_DEV, r, n), xa.dtype),
        scratch_shapes=[pltpu.SemaphoreType.DMA((N_DEV - 1,)), pltpu.SemaphoreType.DMA((N_DEV - 1,))],
        compiler_params=pltpu.CompilerParams(vmem_limit_bytes=VMEM_LIMIT),
    )(xa)


def _half_rows(ref, hc, rh, lead=None):
    rows = pl.ds(pl.multiple_of(hc * rh, 8), rh)
    return ref.at[rows, :] if lead is None else ref.at[lead, rows, :]


def _gather_weights(shards):
    n = len(shards)

    def body(*refs):
        w_refs, o_refs, token = refs[:n], refs[n:2 * n], refs[2 * n]
        send_sems, recv_sems, stage_sems = refs[2 * n + 1:2 * n + 4]
        stages = refs[2 * n + 4:]
        mx, my, mc = _place()
        chip = 2 * mx + my
        others = _other_chips(mx, my)
        sibling = (mx, my, 1 - mc)
        token[...] = jnp.zeros_like(token)
        stage_in = [pltpu.make_async_copy(w, st, stage_sems.at[0, i]) for i, (w, st) in enumerate(zip(w_refs, stages))]
        for cp in stage_in:
            cp.start()
        first = []
        for i, (w, o) in enumerate(zip(w_refs, o_refs)):
            rh = w.shape[0] // 2
            for k, (cx, cy) in enumerate(others):
                first.append(_remote(_half_rows(w, mc, rh), _half_rows(o, mc, rh, chip), send_sems.at[i, k],
                                     recv_sems.at[i, k], (cx, cy, mc)))
                first[-1].start()
        stage_out = []
        for i, (st, o) in enumerate(zip(stages, o_refs)):
            stage_in[i].wait()
            stage_out.append(pltpu.make_async_copy(st, o.at[chip], stage_sems.at[1, i]))
            stage_out[-1].start()
        passed = []
        for i, (w, o) in enumerate(zip(w_refs, o_refs)):
            rh = w.shape[0] // 2
            for k, (cx, cy) in enumerate(others):
                landed = _half_rows(o, mc, rh, 2 * cx + cy)
                _remote(landed, landed, send_sems.at[i, k], recv_sems.at[i, k], (cx, cy, mc)).wait_recv()
                passed.append(_remote(landed, landed, send_sems.at[i, 3 + k], recv_sems.at[i, 3 + k], sibling))
                passed[-1].start()
        for i, (w, o) in enumerate(zip(w_refs, o_refs)):
            rh = w.shape[0] // 2
            for k, (cx, cy) in enumerate(others):
                there = _half_rows(o, 1 - mc, rh, 2 * cx + cy)
                _remote(there, there, send_sems.at[i, 3 + k], recv_sems.at[i, 3 + k], sibling).wait_recv()
        for cp in first + passed:
            cp.wait_send()
        for cp in stage_out:
            cp.wait()

    out = pl.pallas_call(
        body, name="gather_weights", in_specs=[HBM_SPEC] * n, out_specs=[HBM_SPEC] * n + [VMEM_SPEC],
        out_shape=[jax.ShapeDtypeStruct((N_CHIPS,) + s.shape, s.dtype) for s in shards] + [jax.ShapeDtypeStruct((8, LANES), F32)],
        scratch_shapes=[pltpu.SemaphoreType.DMA((n, 6)), pltpu.SemaphoreType.DMA((n, 6)), pltpu.SemaphoreType.DMA((2, n))]
        + [pltpu.VMEM(s.shape, s.dtype) for s in shards],
        compiler_params=pltpu.CompilerParams(vmem_limit_bytes=VMEM_LIMIT),
    )(*shards)
    return out[:n], out[n]


SEM_SPEC = pl.BlockSpec(memory_space=pltpu.SEMAPHORE)
ANY_SPEC = pl.BlockSpec(memory_space=pl.ANY)
DATAFLOW = pltpu.SideEffectType.DATAFLOW_SIDE_EFFECTING


def _hbm(arr):
    return pltpu.with_memory_space_constraint(arr, pltpu.HBM)


def _gather_start(shards):
    n = len(shards)

    def body(*refs):
        w_refs, land_refs, send_sems, recv_sems, token = refs[:n], refs[n:2 * n], refs[2 * n], refs[2 * n + 1], refs[-1]
        mx, my, mc = _place()
        chip = 2 * mx + my
        for i, (w, land) in enumerate(zip(w_refs, land_refs)):
            rh = w.shape[0] // 2
            for k, (cx, cy) in enumerate(_other_chips(mx, my)):
                _remote(_half_rows(w, mc, rh), _half_rows(land, mc, rh, chip), send_sems.at[3 * i + k],
                        recv_sems.at[3 * i + k], (cx, cy, mc)).start()
        token[...] = jnp.zeros_like(token)

    lands = [lax.empty((N_CHIPS,) + s.shape, s.dtype) for s in shards]
    out = pl.pallas_call(
        body, name="gather_start",
        out_shape=(pltpu.SemaphoreType.DMA((3 * n,)), pltpu.SemaphoreType.DMA((3 * n,)),
                   *[pltpu.HBM(s.shape, s.dtype) for s in shards], *[pltpu.HBM(l.shape, l.dtype) for l in lands],
                   jax.ShapeDtypeStruct((8, LANES), F32)),
        in_specs=[HBM_SPEC] * (2 * n), out_specs=(SEM_SPEC, SEM_SPEC, *[HBM_SPEC] * (2 * n), VMEM_SPEC),
        input_output_aliases={i: 2 + i for i in range(2 * n)},
        compiler_params=pltpu.CompilerParams(has_side_effects=DATAFLOW),
    )(*[_hbm(s) for s in shards], *[_hbm(l) for l in lands])
    return out[0], out[1], out[2:2 + n], out[2 + n:2 + 2 * n], out[-1]


def _gather_wait(send_sems, recv_sems, shards, lands, after):
    n = len(shards)

    def body(*refs):
        w_refs, land_refs, send_sems, recv_sems = refs[:n], refs[n:2 * n], refs[2 * n], refs[2 * n + 1]
        mx, my, mc = _place()
        for i, (w, land) in enumerate(zip(w_refs, land_refs)):
            rh = w.shape[0] // 2
            for k, (cx, cy) in enumerate(_other_chips(mx, my)):
                cp = _remote(_half_rows(w, mc, rh), _half_rows(land, mc, rh, 2 * cx + cy), send_sems.at[3 * i + k],
                             recv_sems.at[3 * i + k], (cx, cy, mc))
                cp.wait_send()
                cp.wait_recv()

    out = pl.pallas_call(
        body, name="gather_wait",
        out_shape=(*[pltpu.HBM(s.shape, s.dtype) for s in shards], *[pltpu.HBM(l.shape, l.dtype) for l in lands]),
        in_specs=[HBM_SPEC] * (2 * n) + [SEM_SPEC, SEM_SPEC, ANY_SPEC], out_specs=tuple([HBM_SPEC] * (2 * n)),
        input_output_aliases={i: i for i in range(2 * n)},
        compiler_params=pltpu.CompilerParams(has_side_effects=DATAFLOW),
    )(*shards, *lands, send_sems, recv_sems, after)
    return out[n:]


def _gather_finish(shards, lands):
    n = len(shards)

    def body(*refs):
        w_refs, land_refs, o_refs = refs[:n], refs[n:2 * n], refs[2 * n:3 * n]
        send_sems, recv_sems, stage_sems = refs[3 * n:3 * n + 3]
        stages = refs[3 * n + 3:]
        mx, my, mc = _place()
        chip = 2 * mx + my
        others = _other_chips(mx, my)
        sibling = (mx, my, 1 - mc)
        stage_in = [pltpu.make_async_copy(w, st, stage_sems.at[0, i]) for i, (w, st) in enumerate(zip(w_refs, stages))]
        for cp in stage_in:
            cp.start()
        passed = []
        for i, (w, o) in enumerate(zip(w_refs, o_refs)):
            rh = w.shape[0] // 2
            for k, (cx, cy) in enumerate(others):
                landed = _half_rows(o, mc, rh, 2 * cx + cy)
                passed.append(_remote(landed, landed, send_sems.at[i, k], recv_sems.at[i, k], sibling))
                passed[-1].start()
        stage_out = []
        for i, (st, o) in enumerate(zip(stages, o_refs)):
            stage_in[i].wait()
            stage_out.append(pltpu.make_async_copy(st, o.at[chip], stage_sems.at[1, i]))
            stage_out[-1].start()
        for i, (w, o) in enumerate(zip(w_refs, o_refs)):
            rh = w.shape[0] // 2
            for k, (cx, cy) in enumerate(others):
                there = _half_rows(o, 1 - mc, rh, 2 * cx + cy)
                _remote(there, there, send_sems.at[i, k], recv_sems.at[i, k], sibling).wait_recv()
        for cp in passed:
            cp.wait_send()
        for cp in stage_out:
            cp.wait()

    return pl.pallas_call(
        body, name="gather_finish", in_specs=[HBM_SPEC] * (2 * n), out_specs=[HBM_SPEC] * n,
        out_shape=[jax.ShapeDtypeStruct(l.shape, l.dtype) for l in lands],
        input_output_aliases={n + i: i for i in range(n)},
        scratch_shapes=[pltpu.SemaphoreType.DMA((n, 3)), pltpu.SemaphoreType.DMA((n, 3)), pltpu.SemaphoreType.DMA((2, n))]
        + [pltpu.VMEM(s.shape, s.dtype) for s in shards],
        compiler_params=pltpu.CompilerParams(vmem_limit_bytes=VMEM_LIMIT),
    )(*shards, *lands)


def _scatter_start(ss):
    n = len(ss)

    def body(*refs):
        s_refs, land_refs, send_sems, recv_sems, token = refs[:n], refs[n:2 * n], refs[2 * n], refs[2 * n + 1], refs[-1]
        mx, my, mc = _place()
        chip = 2 * mx + my
        for i, (s, land) in enumerate(zip(s_refs, land_refs)):
            for k, (cx, cy) in enumerate(_other_chips(mx, my)):
                _remote(s.at[2 * cx + cy], land.at[chip], send_sems.at[3 * i + k], recv_sems.at[3 * i + k],
                        (cx, cy, mc)).start()
        token[...] = jnp.zeros_like(token)

    lands = [lax.empty(s.shape, s.dtype) for s in ss]
    out = pl.pallas_call(
        body, name="scatter_start_" + str(n),
        out_shape=(pltpu.SemaphoreType.DMA((3 * n,)), pltpu.SemaphoreType.DMA((3 * n,)),
                   *[pltpu.HBM(s.shape, s.dtype) for s in ss], *[pltpu.HBM(l.shape, l.dtype) for l in lands],
                   jax.ShapeDtypeStruct((8, LANES), F32)),
        in_specs=[HBM_SPEC] * (2 * n), out_specs=(SEM_SPEC, SEM_SPEC, *[HBM_SPEC] * (2 * n), VMEM_SPEC),
        input_output_aliases={i: 2 + i for i in range(2 * n)},
        compiler_params=pltpu.CompilerParams(has_side_effects=DATAFLOW),
    )(*[_hbm(s) for s in ss], *[_hbm(l) for l in lands])
    return out[0], out[1], out[2:2 + n], out[2 + n:2 + 2 * n], out[-1]


def _scatter_wait(send_sems, recv_sems, ss, lands, after):
    n = len(ss)

    def body(*refs):
        s_refs, land_refs, send_sems, recv_sems = refs[:n], refs[n:2 * n], refs[2 * n], refs[2 * n + 1]
        mx, my, mc = _place()
        for i, (s, land) in enumerate(zip(s_refs, land_refs)):
            for k, (cx, cy) in enumerate(_other_chips(mx, my)):
                slot = land.at[2 * cx + cy]
                cp = _remote(s.at[2 * cx + cy], slot, send_sems.at[3 * i + k], recv_sems.at[3 * i + k], (cx, cy, mc))
                cp.wait_send()
                cp.wait_recv()

    out = pl.pallas_call(
        body, name="scatter_wait_" + str(n),
        out_shape=(*[pltpu.HBM(s.shape, s.dtype) for s in ss], *[pltpu.HBM(l.shape, l.dtype) for l in lands]),
        in_specs=[HBM_SPEC] * (2 * n) + [SEM_SPEC, SEM_SPEC, ANY_SPEC], out_specs=tuple([HBM_SPEC] * (2 * n)),
        input_output_aliases={i: i for i in range(2 * n)},
        compiler_params=pltpu.CompilerParams(has_side_effects=DATAFLOW),
    )(*ss, *lands, send_sems, recv_sems, after)
    return out[:n], out[n:]


def _swap_halves(gs, name):
    n = len(gs)

    def body(*refs):
        g_refs, o_refs, send_sems, recv_sems = refs[:n], refs[n:2 * n], refs[2 * n], refs[2 * n + 1]
        mx, my, mc = _place()
        copies = []
        for i, (g, o) in enumerate(zip(g_refs, o_refs)):
            rh = g.shape[1] // 2
            src = g.at[:, pl.ds(pl.multiple_of((1 - mc) * rh, 8), rh), :]
            copies.append(_remote(src, o, send_sems.at[i], recv_sems.at[i], (mx, my, 1 - mc)))
            copies[-1].start()
        for cp in copies:
            cp.wait()

    return pl.pallas_call(
        body, name=name, in_specs=[HBM_SPEC] * n, out_specs=[HBM_SPEC] * n,
        out_shape=[jax.ShapeDtypeStruct((N_CHIPS, g.shape[1] // 2, g.shape[2]), g.dtype) for g in gs],
        scratch_shapes=[pltpu.SemaphoreType.DMA((n,)), pltpu.SemaphoreType.DMA((n,))],
    )(*gs)


def _pair_sum(g, got, core, name):
    _, r, c = g.shape
    rh = r // 2

    def body(core_ref, g_ref, got_ref, o_ref):
        o_ref[...] = (g_ref[...] + got_ref[...]).astype(BF16)

    return pl.pallas_call(
        body, name=name,
        grid_spec=pltpu.PrefetchScalarGridSpec(
            num_scalar_prefetch=1, grid=(N_CHIPS,),
            in_specs=[pl.BlockSpec((1, rh, c), lambda j, core_ref: (j, core_ref[0], 0)),
                      pl.BlockSpec((1, rh, c), lambda j, core_ref: (j, 0, 0))],
            out_specs=pl.BlockSpec((1, rh, c), lambda j, core_ref: (j, 0, 0))),
        out_shape=jax.ShapeDtypeStruct((N_CHIPS, rh, c), BF16),
        compiler_params=_cparams(("arbitrary",)),
    )(core, g, got)


def _scatter_chips(ss):
    n = len(ss)

    def body(*refs):
        s_refs, o_refs, send_sems, recv_sems = refs[:n], refs[n:2 * n], refs[2 * n], refs[2 * n + 1]
        mx, my, mc = _place()
        chip = 2 * mx + my
        others = _other_chips(mx, my)
        sends = []
        for i, (s, o) in enumerate(zip(s_refs, o_refs)):
            for k, (cx, cy) in enumerate(others):
                sends.append(_remote(s.at[2 * cx + cy], o.at[chip], send_sems.at[i, k], recv_sems.at[i, k], (cx, cy, mc)))
                sends[-1].start()
        for i, (s, o) in enumerate(zip(s_refs, o_refs)):
            for k, (cx, cy) in enumerate(others):
                slot = o.at[2 * cx + cy]
                _remote(slot, slot, send_sems.at[i, k], recv_sems.at[i, k], (cx, cy, mc)).wait_recv()
        for cp in sends:
            cp.wait_send()

    return pl.pallas_call(
        body, name="scatter_chips", in_specs=[HBM_SPEC] * n, out_specs=[HBM_SPEC] * n,
        out_shape=[jax.ShapeDtypeStruct(s.shape, s.dtype) for s in ss],
        scratch_shapes=[pltpu.SemaphoreType.DMA((n, 3)), pltpu.SemaphoreType.DMA((n, 3))],
    )(*ss)


def _chip_sum(own, got, chip, name):
    _, h, c = own.shape

    def body(chip_ref, a_ref, b_ref, c_ref, d_ref, o_ref):
        o_ref[...] = ((a_ref[0].astype(F32) + b_ref[0].astype(F32)) + c_ref[0].astype(F32)) + d_ref[0].astype(F32)

    slot = lambda flip: pl.BlockSpec((1, h, c), lambda i, chip_ref: (chip_ref[0] ^ flip, 0, 0))
    return pl.pallas_call(
        body, name=name,
        grid_spec=pltpu.PrefetchScalarGridSpec(
            num_scalar_prefetch=1, grid=(1,), in_specs=[slot(0), slot(1), slot(2), slot(3)],
            out_specs=pl.BlockSpec((h, c), lambda i, chip_ref: (0, 0))),
        out_shape=jax.ShapeDtypeStruct((h, c), F32),
        compiler_params=_cparams(("arbitrary",)),
    )(chip, own, got, got, got)


def _join_halves(mine):
    n = len(mine)

    def body(*refs):
        m_refs, o_refs, send_sems, recv_sems = refs[:n], refs[n:2 * n], refs[2 * n], refs[2 * n + 1]
        mx, my, mc = _place()
        copies = []
        for i, (m, o) in enumerate(zip(m_refs, o_refs)):
            copies.append(_remote(m, o, send_sems.at[i], recv_sems.at[i], (mx, my, 1 - mc)))
            copies[-1].start()
        for cp in copies:
            cp.wait()

    return pl.pallas_call(
        body, name="join_halves", in_specs=[HBM_SPEC] * n, out_specs=[HBM_SPEC] * n,
        out_shape=[jax.ShapeDtypeStruct(m.shape, m.dtype) for m in mine],
        scratch_shapes=[pltpu.SemaphoreType.DMA((n,)), pltpu.SemaphoreType.DMA((n,))],
    )(*mine)


def _adam_math(w, g, m, v):
    m2 = ADAM_B1 * m + (1.0 - ADAM_B1) * g
    v2 = ADAM_B2 * v + (1.0 - ADAM_B2) * (g * g)
    m_hat = m2 * (1.0 / (1.0 - ADAM_B1 ** ADAM_STEP))
    v_hat = v2 * (1.0 / (1.0 - ADAM_B2 ** ADAM_STEP))
    delta = -ADAM_LR * (m_hat / (jnp.sqrt(v_hat) + ADAM_EPS) + ADAM_WD * w)
    return delta, m2, v2


def _adam(w, g, m, v, name):
    def body(w_ref, g_ref, m_ref, v_ref, d_ref, m2_ref, v2_ref):
        d_ref[...], m2_ref[...], v2_ref[...] = _adam_math(w_ref[...], g_ref[...], m_ref[...], v_ref[...])

    return pl.pallas_call(body, name=name, out_shape=[jax.ShapeDtypeStruct(w.shape, F32)] * 3)(w, g, m, v)


def _adam_halves(w, m, v, mine, theirs, core, name):
    _, r, c = w.shape
    rh = r // 2

    def body(core_ref, w_ref, m_ref, v_ref, mine_ref, theirs_ref, g_ref, d_ref, m2_ref, v2_ref):
        g = jnp.where(pl.program_id(0) == core_ref[0], mine_ref[...], theirs_ref[...])
        g_ref[0] = g
        d_ref[0], m2_ref[0], v2_ref[0] = _adam_math(w_ref[0], g, m_ref[0], v_ref[0])

    half = pl.BlockSpec((1, rh, c), lambda hc, core_ref: (0, hc, 0))
    whole = pl.BlockSpec((rh, c), lambda hc, core_ref: (0, 0))
    return pl.pallas_call(
        body, name=name,
        grid_spec=pltpu.PrefetchScalarGridSpec(
            num_scalar_prefetch=1, grid=(2,), in_specs=[half, half, half, whole, whole], out_specs=[half] * 4),
        out_shape=[jax.ShapeDtypeStruct(w.shape, F32)] * 4,
        compiler_params=_cparams(("arbitrary",)),
    )(core, w, m, v, mine, theirs)


ADA_COLS = N_MOD * D_MODEL // N_CHIPS


def _ada_fwd(c_all, w_ada, b_cols):
    def body(c_ref, w_ref, b_ref, o_ref):
        cv = c_ref[...]
        act = (cv * _sigmoid(cv)).astype(BF16)
        o_ref[...] = _dot(act, w_ref[...].astype(BF16)) + b_ref[...]

    return pl.pallas_call(
        body, name="ada_fwd", out_shape=jax.ShapeDtypeStruct((c_all.shape[0], ADA_COLS), F32),
        compiler_params=pltpu.CompilerParams(vmem_limit_bytes=VMEM_LIMIT),
    )(c_all, w_ada, b_cols)


def _ada_bwd(c_all, dmod_cols, w, m, v):
    nb = c_all.shape[0]
    tn = 384

    def body(c_ref, d_ref, w_ref, m_ref, v_ref, g_ref, dl_ref, m2_ref, v2_ref):
        cv = c_ref[...]
        act = (cv * _sigmoid(cv)).astype(BF16)
        g = _dot_tn(act, d_ref[...].astype(BF16))
        g_ref[...] = g
        dl_ref[...], m2_ref[...], v2_ref[...] = _adam_math(w_ref[...], g, m_ref[...], v_ref[...])

    blk = pl.BlockSpec((D_MODEL, tn), lambda j: (0, j))
    return pl.pallas_call(
        body, name="ada_bwd", grid=(ADA_COLS // tn,),
        in_specs=[pl.BlockSpec((nb, D_MODEL), lambda j: (0, 0)), pl.BlockSpec((nb, tn), lambda j: (0, j)), blk, blk, blk],
        out_specs=[blk] * 4, out_shape=[jax.ShapeDtypeStruct((D_MODEL, ADA_COLS), F32)] * 4,
        compiler_params=_cparams(("arbitrary",)),
    )(c_all, dmod_cols, w, m, v)


SMALL_NAMES = ("norm_ffn1", "norm_mix", "conv_w", "conv_b", "ssd_norm_w", "q_norm_w", "kv_norm_w", "mla_norm_w",
               "norm_ffn2", "norm_final", "dt_bias", "a_log", "d_skip")
SMALL_SIZES = (1024, 1024, CONV_WIDTH * D_CONV, D_CONV, 1024, Q_LORA, KV_LORA, 1024, 1024, 1024, 16, 16, 16)
SMALL_ROWS = 16
MOD_ROWS = 2 * N_MOD
SEND_ROWS = 40


def _pack_small(parts):
    flat = jnp.concatenate([parts[n].reshape(-1) for n in SMALL_NAMES])
    return jnp.pad(flat, (0, SMALL_ROWS * D_MODEL - flat.shape[0]))


def _unpack_small(flat):
    out, off = {}, 0
    for n, size in zip(SMALL_NAMES, SMALL_SIZES):
        out[n] = flat[off:off + size]
        off += size
    return out


def _small_sum(got):
    def body(g_ref, o_ref):
        bsum = jnp.zeros((N_MOD, D_MODEL), F32)
        ssum = jnp.zeros((SMALL_ROWS, D_MODEL), F32)
        for d in range(N_DEV):
            bsum = bsum + g_ref[d, 0:N_MOD, :] + g_ref[d, N_MOD:MOD_ROWS, :]
            ssum = ssum + g_ref[d, MOD_ROWS:MOD_ROWS + SMALL_ROWS, :]
        o_ref[...] = jnp.concatenate([bsum, ssum, jnp.zeros((32 - N_MOD - SMALL_ROWS, D_MODEL), F32)], axis=0)

    return pl.pallas_call(body, name="small_sum", out_shape=jax.ShapeDtypeStruct((32, D_MODEL), F32))(got)


BIG_NAMES = ("ffn1_w_gate", "ffn1_w_up", "ffn1_w_down", "w_in", "w_uq", "w_ukv", "w_out", "ffn2_w_gate", "ffn2_w_up",
             "ffn2_w_down")
_TO_KERNEL = {"w_in": _win_to_kernel, "w_uq": _wuq_to_kernel, "w_ukv": _wukv_to_kernel}
_FROM_KERNEL = {"w_in": _win_from_kernel, "w_uq": _wuq_from_kernel, "w_ukv": _wukv_from_kernel}


def _columns_joined(w4):
    n, r, c = w4.shape
    return w4.transpose(1, 0, 2).reshape(r, n * c)


def _columns_split(g):
    r, cols = g.shape
    return g.reshape(r, N_CHIPS, cols // N_CHIPS).transpose(1, 0, 2)


def kernel(x, c, positions, w_ada, b_ada, norm_ffn1, ffn1_w_gate, ffn1_w_up, ffn1_w_down, norm_mix, w_in, conv_w, conv_b, dt_bias, a_log, d_skip, ssd_norm_w, q_norm_w, w_uq, kv_norm_w, w_ukv, mla_norm_w, w_out, norm_ffn2, ffn2_w_gate, ffn2_w_up, ffn2_w_down, norm_final, loss_target, m_w_ada, m_b_ada, m_norm_ffn1, m_ffn1_w_gate, m_ffn1_w_up, m_ffn1_w_down, m_norm_mix, m_w_in, m_conv_w, m_conv_b, m_dt_bias, m_a_log, m_d_skip, m_ssd_norm_w, m_q_norm_w, m_w_uq, m_kv_norm_w, m_w_ukv, m_mla_norm_w, m_w_out, m_norm_ffn2, m_ffn2_w_gate, m_ffn2_w_up, m_ffn2_w_down, m_norm_final, v_w_ada, v_b_ada, v_norm_ffn1, v_ffn1_w_gate, v_ffn1_w_up, v_ffn1_w_down, v_norm_mix, v_w_in, v_conv_w, v_conv_b, v_dt_bias, v_a_log, v_d_skip, v_ssd_norm_w, v_q_norm_w, v_w_uq, v_kv_norm_w, v_w_ukv, v_mla_norm_w, v_w_out, v_norm_ffn2, v_ffn2_w_gate, v_ffn2_w_up, v_ffn2_w_down, v_norm_final):
    a = dict(locals())
    B, S, D = x.shape
    mx, my, mc = _place()
    chip = 2 * mx + my
    dev = 2 * chip + mc
    core = mc.astype(jnp.int32).reshape(1)
    chip_id = chip.astype(jnp.int32).reshape(1)

    first = ("ffn1_w_gate", "ffn1_w_up", "ffn1_w_down")
    later = tuple(n for n in BIG_NAMES if n not in first)
    got_first, gathered = _gather_weights([a[n][0].astype(BF16) for n in first])
    w = dict(zip(first, got_first))
    in_flight = _gather_start([(a[n][0] + gathered[0, 0]).astype(BF16) for n in later])

    def later_weights(after):
        send_sems, recv_sems, shards, lands, _ = in_flight
        lands = _gather_wait(send_sems, recv_sems, shards, lands, after)
        wl = dict(zip(later, _gather_finish([a[n][0].astype(BF16) for n in later], lands)))
        for n, to_kernel in _TO_KERNEL.items():
            wl[n] = to_kernel(_columns_joined(wl[n]))
        wl["w_out"] = wl["w_out"].reshape(D_SSD + D_MLA, D)
        return wl

    cw_rows = jnp.pad(conv_w[0], ((0, 0), (0, D - conv_w.shape[2])))
    got = _all_gather_small(jnp.concatenate([c, cw_rows, jnp.zeros((8 - B - CONV_WIDTH, D), F32)], axis=0), "gather_c")
    c_all = got[:, :B, :].reshape(N_DEV * B, D)
    conv_full = got[::2, B:B + CONV_WIDTH, :conv_w.shape[2]].transpose(1, 0, 2).reshape(CONV_WIDTH, D_CONV)

    b_cols = lax.dynamic_slice(b_ada, (0, chip * ADA_COLS), (1, ADA_COLS))
    mod_all = _all_gather_small(_ada_fwd(c_all, w_ada[0], b_cols), "gather_mod")
    mod = lax.dynamic_slice(mod_all, (0, B * dev, 0), (N_DEV, B, ADA_COLS))[::2].transpose(1, 0, 2).reshape(B, N_MOD * D)

    small = {n: a[n].reshape(1, -1) for n in SMALL_NAMES if n not in ("conv_w", "norm_final")}
    small["conv_w"], small["norm_final"] = conv_full, norm_final
    pending = []

    def on_grads(names, gw):
        g4 = []
        for n in names:
            g = gw[n]
            if n in _FROM_KERNEL:
                g = _columns_split(_FROM_KERNEL[n](g))
            g4.append(g.reshape(N_CHIPS, a[n].shape[1], a[n].shape[2]))
        pair = [_pair_sum(g, got, core, "pair_sum_" + n) for n, g, got in zip(names, g4, _swap_halves(g4, "swap_" + names[0]))]
        if names[0].startswith("ffn1"):
            pending.append((names, pair, _scatter_chips(pair)))
            return 0.0
        send_sems, recv_sems, pair, lands, token = _scatter_start(pair)
        pending.append((names, send_sems, recv_sems, pair, lands))
        return token[0, 0]

    loss_blk, grad_x, gw, dmod, gs = _local_step(x, positions, mod + in_flight[4][0, 0], w, later_weights, small, loss_target,
                                                 on_grads)
    mine = {}
    for entry in pending:
        names, pair, lands = entry if len(entry) == 3 else (entry[0], *_scatter_wait(*entry[1:], grad_x))
        for n, own, got in zip(names, pair, lands):
            mine[n] = _chip_sum(own, got, chip_id, "chip_sum_" + n)
    grads, deltas, new_m, new_v = {}, {}, {}, {}
    for n, other in zip(BIG_NAMES, _join_halves([mine[n] for n in BIG_NAMES])):
        grads[n], deltas[n], new_m[n], new_v[n] = _adam_halves(a[n], a["m_" + n], a["v_" + n], mine[n], other, core, "adam_" + n)

    small_flat = _pack_small(gs).at[-1].set(loss_blk[0, 0])
    send = jnp.concatenate([dmod.reshape(MOD_ROWS, D), small_flat.reshape(SMALL_ROWS, D),
                            jnp.zeros((SEND_ROWS - MOD_ROWS - SMALL_ROWS, D), F32)], axis=0)
    got = _all_gather_small(send, "gather_small")
    summed = _small_sum(got)
    sums = summed[N_MOD:N_MOD + SMALL_ROWS].reshape(-1)
    loss = sums[-1]
    gsmall = _unpack_small(sums)
    gsmall["conv_w"] = lax.dynamic_slice(gsmall["conv_w"].reshape(CONV_WIDTH, D_CONV), (0, chip * conv_w.shape[2]),
                                         (CONV_WIDTH, conv_w.shape[2]))
    gsmall["b_ada"] = summed[:N_MOD]
    names = ("b_ada",) + SMALL_NAMES
    rows = 208

    def pack(parts):
        flat = jnp.concatenate([parts[n].reshape(-1) for n in names])
        return jnp.pad(flat, (0, rows * LANES - flat.shape[0])).reshape(rows, LANES)

    packed = [pack({n: a[p + n] for n in names}) for p in ("", "m_", "v_")]
    g_p = pack(gsmall)
    outs = (g_p,) + tuple(_adam(packed[0], g_p, packed[1], packed[2], "adam_small"))
    for dst, flat in zip((grads, deltas, new_m, new_v), outs):
        flat, off = flat.reshape(-1), 0
        for n in names:
            dst[n] = flat[off:off + a[n].size].reshape(a[n].shape)
            off += a[n].size

    dmod_all = got[:, :MOD_ROWS, :].reshape(N_DEV * B, N_MOD * D)
    dmod_cols = lax.dynamic_slice(dmod_all, (0, chip * ADA_COLS), (N_DEV * B, ADA_COLS))
    ada = _ada_bwd(c_all, dmod_cols, w_ada[0], m_w_ada[0], v_w_ada[0])
    for dst, t in zip((grads, deltas, new_m, new_v), ada):
        dst["w_ada"] = t[None]

    order = ("w_ada", "b_ada", "norm_ffn1", "ffn1_w_gate", "ffn1_w_up", "ffn1_w_down", "norm_mix", "w_in", "conv_w", "conv_b",
             "dt_bias", "a_log", "d_skip", "ssd_norm_w", "q_norm_w", "w_uq", "kv_norm_w", "w_ukv", "mla_norm_w", "w_out",
             "norm_ffn2", "ffn2_w_gate", "ffn2_w_up", "ffn2_w_down", "norm_final")
    return (loss, grad_x, *[grads[n] for n in order], *[deltas[n] for n in order], *[new_m[n] for n in order],
            *[new_v[n] for n in order])
```

```python
import functools
import math

import jax
import jax.numpy as jnp
import numpy as np
from jax import lax
from jax.experimental import pallas as pl
from jax.experimental.pallas import tpu as pltpu

F32 = jnp.float32
BF16 = jnp.bfloat16
HIGHEST = lax.Precision.HIGHEST

D_MODEL = 1024
D_FF = 2816
D_SSD = 1024
D_MLA = 1024
SSD_HEADS = 16
SSD_HEAD_DIM = 64
SSD_GROUPS = 2
SSD_STATE = 128
CONV_WIDTH = 4
CHUNK = 128
MLA_HEADS = 8
QK_NOPE = 64
QK_ROPE = 32
QK_DIM = QK_NOPE + QK_ROPE
V_HEAD = 128
Q_LORA = 384
KV_LORA = 256
ROPE_THETA = 10000.0
N_MOD = 9
EPS = 1e-6
D_CONV = D_SSD + 2 * SSD_GROUPS * SSD_STATE
D_PROJ = 3328
HEAD_LANES = 128
ADAM_LR = 0.001
ADAM_B1 = 0.9
ADAM_B2 = 0.999
ADAM_EPS = 1e-08
ADAM_WD = 0.01
ADAM_STEP = 10

LANES = 128
VMEM_LIMIT = 56 * 1024 * 1024
TOKEN_TILE = 512
ATTN_TILE = 512
N_CHIPS = 4
N_DEV = 8

MESH = pl.DeviceIdType.MESH


def _dot(a, b, precision=None):
    return jnp.dot(a, b, preferred_element_type=F32, precision=precision)


def _dot_nt(a, b, precision=None):
    return lax.dot_general(a, b, (((1,), (1,)), ((), ())), preferred_element_type=F32, precision=precision)


def _dot_tn(a, b, precision=None):
    return lax.dot_general(a, b, (((0,), (0,)), ((), ())), preferred_element_type=F32, precision=precision)


def _cparams(semantics):
    return pltpu.CompilerParams(dimension_semantics=semantics, vmem_limit_bytes=VMEM_LIMIT)


def _resident(shape):
    zeros = (0,) * len(shape)
    return pl.BlockSpec(shape, lambda *_: zeros, pipeline_mode=pl.Buffered(1))


def _sigmoid(x):
    return jax.nn.sigmoid(x)


def _rms_stats(x):
    r = lax.rsqrt(jnp.mean(x * x, axis=-1, keepdims=True) + EPS)
    return x * r, r


def _rms_bwd(dn, xh, r, w):
    dxh = dn * w
    dx = r * (dxh - xh * jnp.mean(dxh * xh, axis=-1, keepdims=True))
    return dx, dn * xh


def _colsum(v):
    return jnp.sum(v, axis=0, keepdims=True)


def _ffn_fwd(x, nw, sh, sc, g, wg, wu, wd, seq, name):
    T, D = x.shape
    fs = wg.shape[2]
    tm = min(TOKEN_TILE, seq)
    tps = seq // tm

    def body(x_ref, nw_ref, sh_ref, sc_ref, g_ref, wg_ref, wu_ref, wd_ref, xo_ref, a_ref, u_ref, f_ref):
        xv = x_ref[...]
        xh, _ = _rms_stats(xv)
        h = (xh * nw_ref[...]) * (1.0 + sc_ref[0]) + sh_ref[0]
        hb = h.astype(BF16)
        f = jnp.zeros((tm, D), F32)
        for j in range(N_CHIPS):
            a = _dot(hb, wg_ref[j])
            u = _dot(hb, wu_ref[j])
            a_ref[j] = a.astype(BF16)
            u_ref[j] = u.astype(BF16)
            f = f + _dot((a * _sigmoid(a) * u).astype(BF16), wd_ref[j])
        xo_ref[...] = xv + 0.5 * g_ref[0] * f
        f_ref[...] = f.astype(BF16)

    rows = lambda n: pl.BlockSpec((tm, n), lambda i: (i, 0))
    act = pl.BlockSpec((N_CHIPS, tm, fs), lambda i: (0, i, 0))
    perb = pl.BlockSpec((1, 1, D), lambda i: (i // tps, 0, 0))
    return pl.pallas_call(
        body, grid=(T // tm,), name=name,
        in_specs=[rows(D), _resident((1, D)), perb, perb, perb, _resident((N_CHIPS, D, fs)), _resident((N_CHIPS, D, fs)),
                  _resident((N_CHIPS, fs, D))],
        out_specs=[rows(D), act, act, rows(D)],
        out_shape=[jax.ShapeDtypeStruct((T, D), F32), jax.ShapeDtypeStruct((N_CHIPS, T, fs), BF16),
                   jax.ShapeDtypeStruct((N_CHIPS, T, fs), BF16), jax.ShapeDtypeStruct((T, D), BF16)],
        compiler_params=_cparams(("arbitrary",)),
    )(x, nw, sh, sc, g, wg, wu, wd)


def _ffn_bwd(dxo, x, nw, sh, sc, g, a, u, f, wg, wu, wd, seq, name):
    T, D = x.shape
    fs = wg.shape[2]
    B = T // seq
    tm = min(TOKEN_TILE // 2, seq)
    tps = seq // tm

    def body(dxo_ref, x_ref, nw_ref, sh_ref, sc_ref, g_ref, a_ref, u_ref, f_ref, wg_ref, wu_ref, wd_ref,
             dx_ref, h_ref, s_ref, df_ref, da_ref, du_ref, dsh_ref, dsc_ref, dg_ref, dnw_ref):
        i = pl.program_id(0)

        @pl.when(i % tps == 0)
        def _():
            dsh_ref[...] = jnp.zeros_like(dsh_ref)
            dsc_ref[...] = jnp.zeros_like(dsc_ref)
            dg_ref[...] = jnp.zeros_like(dg_ref)

        @pl.when(i == 0)
        def _():
            dnw_ref[...] = jnp.zeros_like(dnw_ref)

        dxo_v = dxo_ref[...]
        dfb = (0.5 * g_ref[0] * dxo_v).astype(BF16)
        dg_ref[0] += _colsum(0.5 * dxo_v * f_ref[...].astype(F32))
        dh = jnp.zeros((tm, D), F32)
        for j in range(N_CHIPS):
            ds = _dot_nt(dfb, wd_ref[j])
            av = a_ref[j].astype(F32)
            uv = u_ref[j].astype(F32)
            sig = _sigmoid(av)
            sil = av * sig
            dab = (ds * uv * (sig * (1.0 + av * (1.0 - sig)))).astype(BF16)
            dub = (ds * sil).astype(BF16)
            dh = dh + _dot_nt(dab, wg_ref[j]) + _dot_nt(dub, wu_ref[j])
            s_ref[j] = (sil * uv).astype(BF16)
            da_ref[j] = dab
            du_ref[j] = dub
        xv = x_ref[...]
        xh, r = _rms_stats(xv)
        nwv = nw_ref[...]
        n = xh * nwv
        scale1 = 1.0 + sc_ref[0]
        dsc_ref[0] += _colsum(dh * n)
        dsh_ref[0] += _colsum(dh)
        dx, dw_rows = _rms_bwd(dh * scale1, xh, r, nwv)
        dnw_ref[...] += _colsum(dw_rows)
        dx_ref[...] = dxo_v + dx
        h_ref[...] = (n * scale1 + sh_ref[0]).astype(BF16)
        df_ref[...] = dfb

    rows = lambda n: pl.BlockSpec((tm, n), lambda i: (i, 0))
    act = pl.BlockSpec((N_CHIPS, tm, fs), lambda i: (0, i, 0))
    perb = pl.BlockSpec((1, 1, D), lambda i: (i // tps, 0, 0))
    sd = jax.ShapeDtypeStruct
    return pl.pallas_call(
        body, grid=(T // tm,), name=name,
        in_specs=[rows(D), rows(D), _resident((1, D)), perb, perb, perb, act, act, rows(D),
                  _resident((N_CHIPS, D, fs)), _resident((N_CHIPS, D, fs)), _resident((N_CHIPS, fs, D))],
        out_specs=[rows(D), rows(D), act, rows(D), act, act, perb, perb, perb, pl.BlockSpec((1, D), lambda i: (0, 0))],
        out_shape=[sd((T, D), F32), sd((T, D), BF16), sd((N_CHIPS, T, fs), BF16), sd((T, D), BF16),
                   sd((N_CHIPS, T, fs), BF16), sd((N_CHIPS, T, fs), BF16), sd((B, 1, D), F32), sd((B, 1, D), F32),
                   sd((B, 1, D), F32), sd((1, D), F32)],
        compiler_params=_cparams(("arbitrary",)),
    )(dxo, x, nw, sh, sc, g, a, u, f, wg, wu, wd)


def _ffn_wgrad(h, s, df, da, du, name):
    T, D = h.shape
    fs = s.shape[2]
    tt = min(TOKEN_TILE, T)

    def body(h_ref, s_ref, df_ref, da_ref, du_ref, dgate_ref, dup_ref, ddown_ref):
        @pl.when(pl.program_id(1) == 0)
        def _():
            dgate_ref[...] = jnp.zeros_like(dgate_ref)
            dup_ref[...] = jnp.zeros_like(dup_ref)
            ddown_ref[...] = jnp.zeros_like(ddown_ref)

        hv = h_ref[...]
        dgate_ref[0] += _dot_tn(hv, da_ref[0])
        dup_ref[0] += _dot_tn(hv, du_ref[0])
        ddown_ref[0] += _dot_tn(s_ref[0], df_ref[...])

    rows = pl.BlockSpec((tt, D), lambda j, t: (t, 0))
    act = pl.BlockSpec((1, tt, fs), lambda j, t: (j, t, 0))
    wcol = pl.BlockSpec((1, D, fs), lambda j, t: (j, 0, 0))
    return pl.pallas_call(
        body, grid=(N_CHIPS, T // tt), name=name,
        in_specs=[rows, act, rows, act, act],
        out_specs=[wcol, wcol, pl.BlockSpec((1, fs, D), lambda j, t: (j, 0, 0))],
        out_shape=[jax.ShapeDtypeStruct((N_CHIPS, D, fs), F32), jax.ShapeDtypeStruct((N_CHIPS, D, fs), F32),
                   jax.ShapeDtypeStruct((N_CHIPS, fs, D), F32)],
        compiler_params=_cparams(("arbitrary", "arbitrary")),
    )(h, s, df, da, du)


def _mm_tn(xa, ya, tn, name):
    T, K = xa.shape
    N = ya.shape[1]
    tt = min(TOKEN_TILE, T)

    def body(x_ref, y_ref, o_ref):
        @pl.when(pl.program_id(1) == 0)
        def _():
            o_ref[...] = jnp.zeros_like(o_ref)

        o_ref[...] += _dot_tn(x_ref[...], y_ref[...])

    return pl.pallas_call(
        body, grid=(N // tn, T // tt), name=name,
        in_specs=[pl.BlockSpec((tt, K), lambda j, t: (t, 0)), pl.BlockSpec((tt, tn), lambda j, t: (t, j))],
        out_specs=pl.BlockSpec((K, tn), lambda j, t: (0, j)),
        out_shape=jax.ShapeDtypeStruct((K, N), F32),
        compiler_params=_cparams(("arbitrary", "arbitrary")),
    )(xa, ya)


def _final_loss(x, nw, tgt):
    T, D = x.shape
    tm = min(TOKEN_TILE, T)

    def body(x_ref, nw_ref, t_ref, dx_ref, loss_ref, dnw_ref):
        @pl.when(pl.program_id(0) == 0)
        def _():
            loss_ref[...] = jnp.zeros_like(loss_ref)
            dnw_ref[...] = jnp.zeros_like(dnw_ref)

        xv = x_ref[...]
        xh, r = _rms_stats(xv)
        nwv = nw_ref[...]
        err = xh * nwv - t_ref[...]
        loss_ref[...] += (0.5 / D) * jnp.sum(err * err)
        dx, dw_rows = _rms_bwd(err * (1.0 / D), xh, r, nwv)
        dx_ref[...] = dx
        dnw_ref[...] += _colsum(dw_rows)

    rows = pl.BlockSpec((tm, D), lambda i: (i, 0))
    return pl.pallas_call(
        body, grid=(T // tm,), name="final_loss",
        in_specs=[rows, _resident((1, D)), rows],
        out_specs=[rows, pl.BlockSpec((8, LANES), lambda i: (0, 0)), pl.BlockSpec((1, D), lambda i: (0, 0))],
        out_shape=[jax.ShapeDtypeStruct((T, D), F32), jax.ShapeDtypeStruct((8, LANES), F32),
                   jax.ShapeDtypeStruct((1, D), F32)],
        compiler_params=_cparams(("arbitrary",)),
    )(x, nw, tgt)


_PROJ_SPLITS = (0, 1024, 2560, 2944, 3200, 3328)


def _inproj_fwd(x, nw, sh, sc, win, seq):
    T, D = x.shape
    tm = min(TOKEN_TILE, seq)
    tps = seq // tm
    widths = [b - a for a, b in zip(_PROJ_SPLITS[:-1], _PROJ_SPLITS[1:])]
    dtypes = [BF16, BF16, F32, F32, F32]

    def body(x_ref, nw_ref, sh_ref, sc_ref, w_ref, *outs):
        xh, _ = _rms_stats(x_ref[...])
        h = (xh * nw_ref[...]) * (1.0 + sc_ref[0]) + sh_ref[0]
        proj = _dot(h.astype(BF16), w_ref[...])
        for o, lo, hi in zip(outs, _PROJ_SPLITS[:-1], _PROJ_SPLITS[1:]):
            o[...] = proj[:, lo:hi].astype(o.dtype)

    rows = lambda n: pl.BlockSpec((tm, n), lambda i: (i, 0))
    perb = pl.BlockSpec((1, 1, D), lambda i: (i // tps, 0, 0))
    return pl.pallas_call(
        body, grid=(T // tm,), name="inproj_fwd",
        in_specs=[rows(D), _resident((1, D)), perb, perb, _resident((D, D_PROJ))],
        out_specs=[rows(w) for w in widths],
        out_shape=[jax.ShapeDtypeStruct((T, w), dt) for w, dt in zip(widths, dtypes)],
        compiler_params=_cparams(("arbitrary",)),
    )(x, nw, sh, sc, win)


def _inproj_bwd(dx2, x, nw, sh, sc, win, dz, dxbc, dcq, dckv, ddtk_a, ddtk_b, seq):
    T, D = x.shape
    B = T // seq
    tm = min(TOKEN_TILE, seq)
    tps = seq // tm

    def body(dx2_ref, x_ref, nw_ref, sh_ref, sc_ref, w_ref, dz_ref, dxbc_ref, dcq_ref, dckv_ref, da_ref, db_ref,
             dx_ref, h_ref, dp_ref, dsh_ref, dsc_ref, dnw_ref):
        i = pl.program_id(0)

        @pl.when(i % tps == 0)
        def _():
            dsh_ref[...] = jnp.zeros_like(dsh_ref)
            dsc_ref[...] = jnp.zeros_like(dsc_ref)

        @pl.when(i == 0)
        def _():
            dnw_ref[...] = jnp.zeros_like(dnw_ref)

        dproj = jnp.concatenate(
            [dz_ref[...], dxbc_ref[...], dcq_ref[...].astype(BF16), dckv_ref[...].astype(BF16),
             (da_ref[...] + db_ref[...]).astype(BF16)], axis=1)
        dp_ref[...] = dproj
        dh = _dot_nt(dproj, w_ref[...])
        xh, r = _rms_stats(x_ref[...])
        nwv = nw_ref[...]
        n = xh * nwv
        scale1 = 1.0 + sc_ref[0]
        dsc_ref[0] += _colsum(dh * n)
        dsh_ref[0] += _colsum(dh)
        dx, dw_rows = _rms_bwd(dh * scale1, xh, r, nwv)
        dnw_ref[...] += _colsum(dw_rows)
        dx_ref[...] = dx2_ref[...] + dx
        h_ref[...] = (n * scale1 + sh_ref[0]).astype(BF16)

    rows = lambda n: pl.BlockSpec((tm, n), lambda i: (i, 0))
    perb = pl.BlockSpec((1, 1, D), lambda i: (i // tps, 0, 0))
    sd = jax.ShapeDtypeStruct
    return pl.pallas_call(
        body, grid=(T // tm,), name="inproj_bwd",
        in_specs=[rows(D), rows(D), _resident((1, D)), perb, perb, _resident((D, D_PROJ)),
                  rows(1024), rows(D_CONV), rows(Q_LORA), rows(KV_LORA), rows(LANES), rows(LANES)],
        out_specs=[rows(D), rows(D), rows(D_PROJ), perb, perb, pl.BlockSpec((1, D), lambda i: (0, 0))],
        out_shape=[sd((T, D), F32), sd((T, D), BF16), sd((T, D_PROJ), BF16), sd((B, 1, D), F32), sd((B, 1, D), F32),
                   sd((1, D), F32)],
        compiler_params=_cparams(("arbitrary",)),
    )(dx2, x, nw, sh, sc, win, dz, dxbc, dcq, dckv, ddtk_a, ddtk_b)


def _shift_down(v, k, row):
    return jnp.where(row < k, 0.0, pltpu.roll(v, k, 0))


def _shift_up(v, k, row, n):
    return jnp.where(row >= n - k, 0.0, pltpu.roll(v, n - k, 0))


def _conv_pre(xv, w_ref, b_ref, row):
    pre = b_ref[...] + w_ref[CONV_WIDTH - 1:CONV_WIDTH, :] * xv
    for k in range(1, CONV_WIDTH):
        pre = pre + w_ref[CONV_WIDTH - 1 - k:CONV_WIDTH - k, :] * _shift_down(xv, k, row)
    return pre


def _conv_fwd(xraw, cw, cb):
    B, S, C = xraw.shape

    def body(x_ref, w_ref, b_ref, o_ref):
        xv = x_ref[0].astype(F32)
        row = lax.broadcasted_iota(jnp.int32, xv.shape, 0)
        pre = _conv_pre(xv, w_ref, b_ref, row)
        o_ref[0] = (pre * _sigmoid(pre)).astype(BF16)

    blk = pl.BlockSpec((1, S, LANES), lambda b, j: (b, 0, j))
    return pl.pallas_call(
        body, grid=(B, C // LANES), name="conv_fwd",
        in_specs=[blk, pl.BlockSpec((CONV_WIDTH, LANES), lambda b, j: (0, j)), pl.BlockSpec((1, LANES), lambda b, j: (0, j))],
        out_specs=blk, out_shape=jax.ShapeDtypeStruct((B, S, C), BF16),
        compiler_params=_cparams(("arbitrary", "arbitrary")),
    )(xraw, cw, cb)


def _conv_bwd(dout, xraw, cw, cb):
    B, S, C = xraw.shape

    def body(d_ref, x_ref, w_ref, b_ref, dx_ref, dw_ref, db_ref):
        @pl.when(pl.program_id(1) == 0)
        def _():
            dw_ref[...] = jnp.zeros_like(dw_ref)
            db_ref[...] = jnp.zeros_like(db_ref)

        xv = x_ref[0].astype(F32)
        row = lax.broadcasted_iota(jnp.int32, xv.shape, 0)
        pre = _conv_pre(xv, w_ref, b_ref, row)
        sig = _sigmoid(pre)
        dpre = d_ref[0].astype(F32) * (sig * (1.0 + pre * (1.0 - sig)))
        dx = w_ref[CONV_WIDTH - 1:CONV_WIDTH, :] * dpre
        for k in range(1, CONV_WIDTH):
            dx = dx + w_ref[CONV_WIDTH - 1 - k:CONV_WIDTH - k, :] * _shift_up(dpre, k, row, S)
        dx_ref[0] = dx.astype(BF16)
        db_ref[...] += _colsum(dpre)
        dws = [_colsum(dpre * (xv if k == 0 else _shift_down(xv, k, row))) for k in range(CONV_WIDTH - 1, -1, -1)]
        dw_ref[...] += jnp.concatenate(dws, axis=0)

    blk = pl.BlockSpec((1, S, LANES), lambda j, b: (b, 0, j))
    wspec = pl.BlockSpec((CONV_WIDTH, LANES), lambda j, b: (0, j))
    bspec = pl.BlockSpec((1, LANES), lambda j, b: (0, j))
    return pl.pallas_call(
        body, grid=(C // LANES, B), name="conv_bwd",
        in_specs=[blk, blk, wspec, bspec], out_specs=[blk, wspec, bspec],
        out_shape=[jax.ShapeDtypeStruct((B, S, C), BF16), jax.ShapeDtypeStruct((CONV_WIDTH, C), F32),
                   jax.ShapeDtypeStruct((1, C), F32)],
        compiler_params=_cparams(("arbitrary", "arbitrary")),
    )(dout, xraw, cw, cb)


def _softplus(x):
    return jnp.maximum(x, 0.0) + jnp.log(1.0 + jnp.exp(-jnp.abs(x)))


def _ssd_common(xbc_ref, dtk_ref, dtb_ref, alog_ref, e_ref):
    L = CHUNK
    xbc = xbc_ref[0]
    xs = xbc[:, :D_SSD].astype(F32)
    bm = xbc[:, D_SSD:D_SSD + 256]
    cm = xbc[:, D_SSD + 256:D_SSD + 512]
    head = lax.broadcasted_iota(jnp.int32, (1, LANES), 1) < SSD_HEADS
    a128 = jnp.where(head, -jnp.exp(alog_ref[...]), 0.0)
    pre = dtk_ref[0] + dtb_ref[...]
    dt = _softplus(pre)
    dA = dt * a128
    row = lax.broadcasted_iota(jnp.int32, (L, L), 0)
    col = lax.broadcasted_iota(jnp.int32, (L, L), 1)
    causal = col <= row
    tri = causal.astype(F32)
    triT = (row <= col).astype(F32)
    tri = causal.astype(BF16)
    triT = (row <= col).astype(BF16)
    dA3 = _split3(dA)
    acum = _sum3(lambda part: _dot(tri, part), dA3)
    acumT = _sum3(lambda part: _dot_tn(part, triT), dA3)
    E = e_ref[...]
    acum_f = _spread(acum, E)
    dt_f = _spread(dt, E)
    e_f = jnp.exp(acum_f)
    w_f = jnp.exp(acum_f[L - 1:L, :] - acum_f)
    xt = xs * dt_f
    return dict(xs=xs, bm=bm, cm=cm, a128=a128, pre=pre, dt=dt, causal=causal, tri=tri, triT=triT, acum=acum,
                acumT=acumT, E=E, dt_f=dt_f, e_f=e_f, w_f=w_f, xt=xt, head=head)


def _split3(x):
    p1 = x.astype(BF16)
    r1 = x - p1.astype(F32)
    p2 = r1.astype(BF16)
    return p1, p2, (r1 - p2.astype(F32)).astype(BF16)


def _sum3(mm, parts):
    return (mm(parts[0]) + mm(parts[1])) + mm(parts[2])


def _spread(v, e):
    return _sum3(lambda part: _dot(part, e), _split3(v))


def _gather_heads(v, e):
    return _sum3(lambda part: _dot_nt(part, e), _split3(v))


def _head_mask(k):
    lane = lax.broadcasted_iota(jnp.int32, (CHUNK, LANES), 1)
    return (lane >= SSD_HEAD_DIM) if k == 1 else (lane < SSD_HEAD_DIM)


def _pair_decay(alast, h0):
    row = lax.broadcasted_iota(jnp.int32, (2 * SSD_HEAD_DIM, SSD_STATE), 0)
    return jnp.exp(jnp.where(row < SSD_HEAD_DIM, alast[:, h0:h0 + 1], alast[:, h0 + 1:h0 + 2]))


def _decay_matrix(q, h):
    seg = q["acum"][:, h:h + 1] - q["acumT"][h:h + 1, :]
    return jnp.exp(jnp.where(q["causal"], seg, -1e30))


def _gated_norm(y, zz, nw):
    sig = _sigmoid(zz)
    sil = zz * sig
    yg = y * sil
    half = D_SSD // SSD_GROUPS
    parts = []
    for g in range(SSD_GROUPS):
        xh, r = _rms_stats(yg[:, g * half:(g + 1) * half])
        parts.append((xh, r))
    return sig, sil, parts


def _ssd_fwd(xbc, dtk, z, dtb, alog, dsk, nw, expand):
    B, S, _ = xbc.shape
    L = CHUNK
    nc = S // L

    def body(xbc_ref, dtk_ref, z_ref, dtb_ref, alog_ref, dsk_ref, nw_ref, e_ref, y_ref, ys_ref, prev_ref, st_ref):
        @pl.when(pl.program_id(1) == 0)
        def _():
            st_ref[...] = jnp.zeros_like(st_ref)

        q = _ssd_common(xbc_ref, dtk_ref, dtb_ref, alog_ref, e_ref)
        xtb = q["xt"].astype(BF16)
        xwb = (q["xt"] * q["w_f"]).astype(BF16)
        alast = q["acum"][L - 1:L, :]
        ys = []
        for g in range(SSD_GROUPS):
            bg = q["bm"][:, g * 128:(g + 1) * 128]
            cg = q["cm"][:, g * 128:(g + 1) * 128]
            G = _dot_nt(cg, bg)
            for pr in range(SSD_HEADS // SSD_GROUPS // 2):
                h0 = g * 8 + 2 * pr
                lo = h0 * SSD_HEAD_DIM
                xt_p = xtb[:, lo:lo + 128]
                ydiag = jnp.zeros((L, LANES), F32)
                for k in range(2):
                    M = (G * _decay_matrix(q, h0 + k)).astype(BF16)
                    ydiag = ydiag + _dot(M, jnp.where(_head_mask(k), xt_p, jnp.zeros_like(xt_p)))
                hp = st_ref[lo:lo + 128, :]
                prev_ref[0, 0, lo:lo + 128, :] = hp
                zoff = _dot_nt(cg, hp.astype(BF16))
                ys.append(ydiag + zoff * q["e_f"][:, lo:lo + 128])
                st_ref[lo:lo + 128, :] = _pair_decay(alast, h0) * hp + _dot_tn(xwb[:, lo:lo + 128], bg)
        y = jnp.concatenate(ys, axis=1) + dsk_ref[...] * q["xs"]
        y_ref[0] = y.astype(BF16)
        _, _, parts = _gated_norm(y, z_ref[0].astype(F32), nw_ref[...])
        half = D_SSD // SSD_GROUPS
        ys_ref[0] = jnp.concatenate(
            [xh * nw_ref[:, g * half:(g + 1) * half] for g, (xh, _) in enumerate(parts)], axis=1).astype(BF16)

    chunk = lambda n: pl.BlockSpec((1, L, n), lambda b, c: (b, c, 0))
    vec = pl.BlockSpec((1, LANES), lambda b, c: (0, 0))
    return pl.pallas_call(
        body, grid=(B, nc), name="ssd_fwd",
        in_specs=[chunk(D_CONV), chunk(LANES), chunk(D_SSD), vec, vec, pl.BlockSpec((1, D_SSD), lambda b, c: (0, 0)),
                  pl.BlockSpec((1, D_SSD), lambda b, c: (0, 0)), pl.BlockSpec((LANES, D_SSD), lambda b, c: (0, 0))],
        out_specs=[chunk(D_SSD), chunk(D_SSD), pl.BlockSpec((1, 1, D_SSD, SSD_STATE), lambda b, c: (b, c, 0, 0))],
        out_shape=[jax.ShapeDtypeStruct((B, S, D_SSD), BF16), jax.ShapeDtypeStruct((B, S, D_SSD), BF16),
                   jax.ShapeDtypeStruct((B, nc, D_SSD, SSD_STATE), F32)],
        scratch_shapes=[pltpu.VMEM((D_SSD, SSD_STATE), F32)],
        compiler_params=_cparams(("arbitrary", "arbitrary")),
    )(xbc, dtk, z, dtb, alog, dsk, nw, expand)


def _ssd_bwd(xbc, dtk, z, y, prev, dys, dtb, alog, dsk, nw, expand):
    B, S, _ = xbc.shape
    L = CHUNK
    nc = S // L
    half = D_SSD // SSD_GROUPS

    def body(xbc_ref, dtk_ref, z_ref, y_ref, prev_ref, dys_ref, dtb_ref, alog_ref, dsk_ref, nw_ref, e_ref,
             dxbc_ref, ddtk_ref, dz_ref, dnw_ref, dvec_ref, dh_ref, dskc_ref):
        @pl.when((pl.program_id(0) == 0) & (pl.program_id(1) == 0))
        def _():
            dnw_ref[...] = jnp.zeros_like(dnw_ref)
            dvec_ref[...] = jnp.zeros_like(dvec_ref)
            dskc_ref[...] = jnp.zeros_like(dskc_ref)

        @pl.when(pl.program_id(1) == 0)
        def _():
            dh_ref[...] = jnp.zeros_like(dh_ref)

        q = _ssd_common(xbc_ref, dtk_ref, dtb_ref, alog_ref, e_ref)
        E = q["E"]
        xs = q["xs"]
        yv = y_ref[0].astype(F32)
        zz = z_ref[0].astype(F32)
        sig, sil, parts = _gated_norm(yv, zz, nw_ref[...])
        dn = dys_ref[0].astype(F32)
        dyg, dnw_rows = [], []
        for g, (xh, r) in enumerate(parts):
            dpart, dw_rows = _rms_bwd(dn[:, g * half:(g + 1) * half], xh, r, nw_ref[:, g * half:(g + 1) * half])
            dyg.append(dpart)
            dnw_rows.append(dw_rows)
        dyg = jnp.concatenate(dyg, axis=1)
        dnw_ref[...] += _colsum(jnp.concatenate(dnw_rows, axis=1))
        dY = dyg * sil
        dz_ref[0] = (dyg * yv * (sig * (1.0 + zz * (1.0 - sig)))).astype(BF16)
        dsk_f = dsk_ref[...]
        dskc_ref[...] += _colsum(dY * xs)
        dYb = dY.astype(BF16)
        xtb = q["xt"].astype(BF16)
        xwb = (q["xt"] * q["w_f"]).astype(BF16)
        acum = q["acum"]
        alast = acum[L - 1:L, :]
        lane_id = lax.broadcasted_iota(jnp.int32, (L, LANES), 1)
        sub_id = lax.broadcasted_iota(jnp.int32, (LANES, L), 0)
        lane_row = lax.broadcasted_iota(jnp.int32, (1, LANES), 1)
        da_rows = jnp.zeros((L, LANES), F32)
        daT = jnp.zeros((LANES, L), F32)
        dxt, prod_off, prod_st, dbs, dcs = [], [], [], [], []
        hsum_row = jnp.zeros((1, LANES), F32)
        for g in range(SSD_GROUPS):
            bg = q["bm"][:, g * 128:(g + 1) * 128]
            cg = q["cm"][:, g * 128:(g + 1) * 128]
            G = _dot_nt(cg, bg)
            dG = jnp.zeros((L, L), F32)
            dcg = jnp.zeros((L, SSD_STATE), F32)
            dbg = jnp.zeros((L, SSD_STATE), F32)
            for pr in range(SSD_HEADS // SSD_GROUPS // 2):
                h0 = g * 8 + 2 * pr
                lo = h0 * SSD_HEAD_DIM
                cols = slice(lo, lo + 128)
                dY_p = dYb[:, cols]
                xt_p = xtb[:, cols]
                dxt_p = jnp.zeros((L, LANES), F32)
                for k in range(2):
                    h = h0 + k
                    Lm = _decay_matrix(q, h)
                    Mf = G * Lm
                    dYk = jnp.where(_head_mask(k), dY_p, jnp.zeros_like(dY_p))
                    dM = _dot_nt(dYk, xt_p)
                    dxt_p = dxt_p + _dot_tn(Mf.astype(BF16), dYk)
                    dG = dG + dM * Lm
                    Q = dM * Mf
                    da_rows = da_rows + jnp.where(lane_id == h, jnp.sum(Q, axis=1, keepdims=True), 0.0)
                    daT = daT + jnp.where(sub_id == h, jnp.sum(Q, axis=0, keepdims=True), 0.0)
                hp = prev_ref[0, 0, lo:lo + 128, :]
                hpb = hp.astype(BF16)
                zoff = _dot_nt(cg, hpb)
                e_p = q["e_f"][:, cols]
                dY_pf = dY[:, cols]
                dZb = (dY_pf * e_p).astype(BF16)
                dcg = dcg + _dot(dZb, hpb)
                dhp_off = _dot_tn(dZb, cg)
                prod_off.append(dY_pf * zoff * e_p)
                dS = dh_ref[lo:lo + 128, :]
                dSb = dS.astype(BF16)
                U = _dot_nt(bg, dSb)
                dxt_p = dxt_p + U * q["w_f"][:, cols]
                dbg = dbg + _dot(xwb[:, cols], dSb)
                prod_st.append(q["xt"][:, cols] * U)
                dh_ref[lo:lo + 128, :] = _pair_decay(alast, h0) * dS + dhp_off
                dsh = dS * hp
                for k in range(2):
                    total = jnp.sum(dsh[k * SSD_HEAD_DIM:(k + 1) * SSD_HEAD_DIM, :], axis=(0, 1), keepdims=True)
                    hsum_row = hsum_row + jnp.where(lane_row == h0 + k, total, 0.0)
                dxt.append(dxt_p)
            dGb = dG.astype(BF16)
            dcs.append(dcg + _dot(dGb, bg))
            dbs.append(dbg + _dot_tn(dGb, cg))
        dxt = jnp.concatenate(dxt, axis=1)
        da_rows = da_rows + _gather_heads(jnp.concatenate(prod_off, axis=1), E)
        dww = _gather_heads(jnp.concatenate(prod_st, axis=1), E) * jnp.exp(alast - acum)
        da_rows = da_rows - dww
        dlast = _colsum(dww) + jnp.exp(alast) * hsum_row
        triT = q["triT"]
        ddA = (_sum3(lambda part: _dot(triT, part), _split3(da_rows))
               - _sum3(lambda part: _dot_nt(triT, part), _split3(daT)) + dlast)
        ddA = jnp.where(q["head"], ddA, 0.0)
        ddt = ddA * q["a128"] + _gather_heads(dxt * xs, E)
        ddt_raw = jnp.where(q["head"], ddt * _sigmoid(q["pre"]), 0.0)
        ddtk_ref[0] = ddt_raw
        dxs = dxt * q["dt_f"] + dsk_f * dY
        dxbc_ref[0] = jnp.concatenate([dxs] + dbs + dcs, axis=1).astype(BF16)
        dvec_ref[0:1, :] += _colsum(ddt_raw)
        dvec_ref[1:2, :] += _colsum(ddA * q["dt"]) * q["a128"]

        @pl.when((pl.program_id(0) == B - 1) & (pl.program_id(1) == nc - 1))
        def _():
            dvec_ref[2:3, :] = _gather_heads(jnp.broadcast_to(dskc_ref[...], (8, D_SSD)), E)[0:1, :]

    rev = lambda n: pl.BlockSpec((1, L, n), lambda b, c: (b, nc - 1 - c, 0))
    vec = pl.BlockSpec((1, LANES), lambda b, c: (0, 0))
    sd = jax.ShapeDtypeStruct
    return pl.pallas_call(
        body, grid=(B, nc), name="ssd_bwd",
        in_specs=[rev(D_CONV), rev(LANES), rev(D_SSD), rev(D_SSD),
                  pl.BlockSpec((1, 1, D_SSD, SSD_STATE), lambda b, c: (b, nc - 1 - c, 0, 0)), rev(D_SSD), vec, vec,
                  pl.BlockSpec((1, D_SSD), lambda b, c: (0, 0)),
                  pl.BlockSpec((1, D_SSD), lambda b, c: (0, 0)), pl.BlockSpec((LANES, D_SSD), lambda b, c: (0, 0))],
        out_specs=[rev(D_CONV), rev(LANES), rev(D_SSD), pl.BlockSpec((1, D_SSD), lambda b, c: (0, 0)),
                   pl.BlockSpec((8, LANES), lambda b, c: (0, 0))],
        out_shape=[sd((B, S, D_CONV), BF16), sd((B, S, LANES), F32), sd((B, S, D_SSD), BF16), sd((1, D_SSD), F32),
                   sd((8, LANES), F32)],
        scratch_shapes=[pltpu.VMEM((D_SSD, SSD_STATE), F32), pltpu.VMEM((1, D_SSD), F32)],
        compiler_params=_cparams(("arbitrary", "arbitrary")),
    )(xbc, dtk, z, y, prev, dys, dtb, alog, dsk, nw, expand)


def _rope_tables(pos_ref, invf_ref):
    ang = pos_ref[...].astype(F32) * invf_ref[...]
    return jnp.cos(ang), jnp.sin(ang)


def _rot(u):
    lane = lax.broadcasted_iota(jnp.int32, u.shape, 1)
    first = (lane >= QK_NOPE) & (lane < QK_NOPE + QK_ROPE // 2)
    second = (lane >= QK_NOPE + QK_ROPE // 2) & (lane < QK_DIM)
    return jnp.where(first, -pltpu.roll(u, LANES - QK_ROPE // 2, 1), jnp.where(second, pltpu.roll(u, QK_ROPE // 2, 1), 0.0))


def _rope_lanes(shape):
    lane = lax.broadcasted_iota(jnp.int32, shape, 1)
    return (lane >= QK_NOPE) & (lane < QK_DIM)


def _mla_prep(cq, ckv, dtk, pos, qw, kvw, wuq, wukv, invf):
    T = cq.shape[0]
    tm = min(TOKEN_TILE, T)
    scale = 1.0 / math.sqrt(QK_DIM)
    HW = MLA_HEADS * HEAD_LANES

    def body(cq_ref, ckv_ref, dtk_ref, pos_ref, qw_ref, kvw_ref, wuq_ref, wukv_ref, invf_ref, q_ref, k_ref, v_ref):
        xh, _ = _rms_stats(cq_ref[...])
        qv = _dot((xh * qw_ref[...]).astype(BF16), wuq_ref[...])
        xh, _ = _rms_stats(ckv_ref[...])
        kv = _dot((xh * kvw_ref[...]).astype(BF16), wukv_ref[...])
        cosf, sinf = _rope_tables(pos_ref, invf_ref)
        rope = lambda u: u * cosf + _rot(u) * sinf
        dtkv = dtk_ref[...]
        kr = rope(jnp.where(_rope_lanes(dtkv.shape), dtkv, 0.0))
        for h in range(MLA_HEADS):
            cols = slice(h * HEAD_LANES, (h + 1) * HEAD_LANES)
            q_ref[:, cols] = (rope(qv[:, cols]) * scale).astype(BF16)
            k_ref[:, cols] = (kv[:, cols] + kr).astype(BF16)
        v_ref[...] = kv[:, HW:].astype(BF16)

    rows = lambda n: pl.BlockSpec((tm, n), lambda i: (i, 0))
    return pl.pallas_call(
        body, grid=(T // tm,), name="mla_prep",
        in_specs=[rows(Q_LORA), rows(KV_LORA), rows(LANES), rows(1), _resident((1, Q_LORA)), _resident((1, KV_LORA)),
                  _resident((Q_LORA, HW)), _resident((KV_LORA, 2 * HW)), _resident((1, LANES))],
        out_specs=[rows(HW), rows(HW), rows(HW)],
        out_shape=[jax.ShapeDtypeStruct((T, HW), BF16)] * 3,
        compiler_params=_cparams(("arbitrary",)),
    )(cq, ckv, dtk, pos, qw, kvw, wuq, wukv, invf)


def _mla_prep_bwd(dq, dk, dv, cq, ckv, pos, qw, kvw, wuq, wukv, invf):
    T = cq.shape[0]
    tm = min(TOKEN_TILE, T)
    scale = 1.0 / math.sqrt(QK_DIM)
    HW = MLA_HEADS * HEAD_LANES

    def body(dq_ref, dk_ref, dv_ref, cq_ref, ckv_ref, pos_ref, qw_ref, kvw_ref, wuq_ref, wukv_ref, invf_ref,
             dcq_ref, dckv_ref, ddtk_ref, qn_ref, kvn_ref, dqo_ref, dkvo_ref, dqw_ref, dkvw_ref):
        @pl.when(pl.program_id(0) == 0)
        def _():
            dqw_ref[...] = jnp.zeros_like(dqw_ref)
            dkvw_ref[...] = jnp.zeros_like(dkvw_ref)

        cosf, sinf = _rope_tables(pos_ref, invf_ref)
        unrope = lambda d: d * cosf - _rot(d * sinf)
        dkr = jnp.zeros((tm, LANES), F32)
        nope = lax.broadcasted_iota(jnp.int32, (tm, LANES), 1) < QK_NOPE
        for h in range(MLA_HEADS):
            cols = slice(h * HEAD_LANES, (h + 1) * HEAD_LANES)
            dqo_ref[:, cols] = unrope(dq_ref[:, cols] * scale).astype(BF16)
            dkh = dk_ref[:, cols]
            dkr = dkr + jnp.where(_rope_lanes(dkh.shape), dkh, 0.0)
            dkvo_ref[:, cols] = jnp.where(nope, dkh, 0.0).astype(BF16)
        dkvo_ref[:, HW:] = dv_ref[...].astype(BF16)
        ddtk_ref[...] = unrope(dkr)
        xh, r = _rms_stats(cq_ref[...])
        qn_ref[...] = (xh * qw_ref[...]).astype(BF16)
        dx, dw_rows = _rms_bwd(_dot_nt(dqo_ref[...], wuq_ref[...]), xh, r, qw_ref[...])
        dcq_ref[...] = dx
        dqw_ref[...] += _colsum(dw_rows)
        xh, r = _rms_stats(ckv_ref[...])
        kvn_ref[...] = (xh * kvw_ref[...]).astype(BF16)
        dx, dw_rows = _rms_bwd(_dot_nt(dkvo_ref[...], wukv_ref[...]), xh, r, kvw_ref[...])
        dckv_ref[...] = dx
        dkvw_ref[...] += _colsum(dw_rows)

    rows = lambda n: pl.BlockSpec((tm, n), lambda i: (i, 0))
    sd = jax.ShapeDtypeStruct
    return pl.pallas_call(
        body, grid=(T // tm,), name="mla_prep_bwd",
        in_specs=[rows(HW), rows(HW), rows(HW), rows(Q_LORA), rows(KV_LORA), rows(1), _resident((1, Q_LORA)),
                  _resident((1, KV_LORA)), _resident((Q_LORA, HW)), _resident((KV_LORA, 2 * HW)), _resident((1, LANES))],
        out_specs=[rows(Q_LORA), rows(KV_LORA), rows(LANES), rows(Q_LORA), rows(KV_LORA), rows(HW), rows(2 * HW),
                   pl.BlockSpec((1, Q_LORA), lambda i: (0, 0)), pl.BlockSpec((1, KV_LORA), lambda i: (0, 0))],
        out_shape=[sd((T, Q_LORA), F32), sd((T, KV_LORA), F32), sd((T, LANES), F32), sd((T, Q_LORA), BF16),
                   sd((T, KV_LORA), BF16), sd((T, HW), BF16), sd((T, 2 * HW), BF16), sd((1, Q_LORA), F32),
                   sd((1, KV_LORA), F32)],
        compiler_params=_cparams(("arbitrary",)),
    )(dq, dk, dv, cq, ckv, pos, qw, kvw, wuq, wukv, invf)


def _causal_mask(t):
    row = lax.broadcasted_iota(jnp.int32, (t, t), 0)
    col = lax.broadcasted_iota(jnp.int32, (t, t), 1)
    return col <= row


def _attn_fwd(q, k, v):
    B, S, HW = q.shape
    H = HW // HEAD_LANES
    t = min(ATTN_TILE, S)
    nq = S // t

    def body(q_ref, k_ref, v_ref, o_ref, lse_ref):
        qi = pl.program_id(2)
        qv = q_ref[0]

        def step(j, carry, masked):
            m, l, acc = carry
            sl = pl.ds(pl.multiple_of(j * t, t), t)
            s = _dot_nt(qv, k_ref[0, sl, :])
            if masked:
                s = jnp.where(_causal_mask(t), s, -1e30)
            m_new = jnp.maximum(m, jnp.max(s, axis=-1, keepdims=True))
            alpha = jnp.exp(m - m_new)
            p = jnp.exp(s - m_new)
            l = alpha * l + jnp.sum(p, axis=-1, keepdims=True)
            acc = alpha * acc + _dot(p.astype(BF16), v_ref[0, sl, :])
            return m_new, l, acc

        init = (jnp.full((t, 1), -1e30, F32), jnp.zeros((t, 1), F32), jnp.zeros((t, HEAD_LANES), F32))
        carry = lax.fori_loop(0, qi, lambda j, c: step(j, c, False), init)
        m, l, acc = step(qi, carry, True)
        o_ref[0] = (acc / l).astype(BF16)
        lse_ref[0, 0] = m + jnp.log(l)

    return pl.pallas_call(
        body, grid=(B, H, nq), name="attn_fwd",
        in_specs=[pl.BlockSpec((1, t, HEAD_LANES), lambda b, h, i: (b, i, h)),
                  pl.BlockSpec((1, S, HEAD_LANES), lambda b, h, i: (b, 0, h)),
                  pl.BlockSpec((1, S, HEAD_LANES), lambda b, h, i: (b, 0, h))],
        out_specs=[pl.BlockSpec((1, t, HEAD_LANES), lambda b, h, i: (b, i, h)),
                   pl.BlockSpec((1, 1, t, 1), lambda b, h, i: (b, h, i, 0))],
        out_shape=[jax.ShapeDtypeStruct((B, S, HW), BF16), jax.ShapeDtypeStruct((B, H, S, 1), F32)],
        compiler_params=_cparams(("arbitrary", "arbitrary", "arbitrary")),
    )(q, k, v)


def _attn_bwd(q, k, v, o, do, lse):
    B, S, HW = q.shape
    H = HW // HEAD_LANES
    t = min(ATTN_TILE, S)
    nq = S // t

    def body(q_ref, k_ref, v_ref, o_ref, do_ref, lse_ref, dq_ref, dk_ref, dv_ref):
        j = pl.program_id(2)

        @pl.when(j == 0)
        def _():
            dq_ref[...] = jnp.zeros_like(dq_ref)

        kj = k_ref[0]
        vj = v_ref[0]

        def step(i, carry, masked):
            dk, dv = carry
            sl = pl.ds(pl.multiple_of(i * t, t), t)
            qi = q_ref[0, sl, :]
            doi = do_ref[0, sl, :]
            s = _dot_nt(qi, kj)
            if masked:
                s = jnp.where(_causal_mask(t), s, -1e30)
            p = jnp.exp(s - lse_ref[0, 0, sl, :])
            dv = dv + _dot_tn(p.astype(BF16), doi)
            dp = _dot_nt(doi, vj)
            delta = jnp.sum(doi.astype(F32) * o_ref[0, sl, :].astype(F32), axis=-1, keepdims=True)
            dsb = (p * (dp - delta)).astype(BF16)
            dk = dk + _dot_tn(dsb, qi)
            dq_ref[0, sl, :] += _dot(dsb, kj)
            return dk, dv

        zero = jnp.zeros((t, HEAD_LANES), F32)
        carry = step(j, (zero, zero), True)
        dk, dv = lax.fori_loop(j + 1, nq, lambda i, c: step(i, c, False), carry)
        dk_ref[0] = dk
        dv_ref[0] = dv

    full = pl.BlockSpec((1, S, HEAD_LANES), lambda b, h, j: (b, 0, h))
    tile = pl.BlockSpec((1, t, HEAD_LANES), lambda b, h, j: (b, j, h))
    sd = jax.ShapeDtypeStruct
    return pl.pallas_call(
        body, grid=(B, H, nq), name="attn_bwd",
        in_specs=[full, tile, tile, full, full, pl.BlockSpec((1, 1, S, 1), lambda b, h, j: (b, h, 0, 0))],
        out_specs=[full, tile, tile],
        out_shape=[sd((B, S, HW), F32), sd((B, S, HW), F32), sd((B, S, HW), F32)],
        compiler_params=_cparams(("arbitrary", "arbitrary", "arbitrary")),
    )(q, k, v, o, do, lse)


def _mix_out(x1, yssd, o, mw, wout, g, seq):
    T, D = x1.shape
    tm = min(TOKEN_TILE, seq)
    tps = seq // tm

    def body(x_ref, ys_ref, o_ref, mw_ref, w_ref, g_ref, xo_ref, m_ref, yc_ref):
        xh, _ = _rms_stats(o_ref[...].astype(F32))
        ycat = jnp.concatenate([ys_ref[...], (xh * mw_ref[...]).astype(BF16)], axis=1)
        m = _dot(ycat, w_ref[...])
        xo_ref[...] = x_ref[...] + g_ref[0] * m
        m_ref[...] = m.astype(BF16)
        yc_ref[...] = ycat

    rows = lambda n: pl.BlockSpec((tm, n), lambda i: (i, 0))
    perb = pl.BlockSpec((1, 1, D), lambda i: (i // tps, 0, 0))
    sd = jax.ShapeDtypeStruct
    return pl.pallas_call(
        body, grid=(T // tm,), name="mix_out",
        in_specs=[rows(D), rows(D_SSD), rows(D_MLA), _resident((1, D_MLA)), _resident((D_SSD + D_MLA, D)), perb],
        out_specs=[rows(D), rows(D), rows(D_SSD + D_MLA)],
        out_shape=[sd((T, D), F32), sd((T, D), BF16), sd((T, D_SSD + D_MLA), BF16)],
        compiler_params=_cparams(("arbitrary",)),
    )(x1, yssd, o, mw, wout, g)


def _mix_out_bwd(dx2, m, o, mw, wout, g, seq):
    T, D = dx2.shape
    B = T // seq
    tm = min(TOKEN_TILE, seq)
    tps = seq // tm

    def body(dx_ref, m_ref, o_ref, mw_ref, w_ref, g_ref, dys_ref, do_ref, dm_ref, dg_ref, dmw_ref):
        i = pl.program_id(0)

        @pl.when(i % tps == 0)
        def _():
            dg_ref[...] = jnp.zeros_like(dg_ref)

        @pl.when(i == 0)
        def _():
            dmw_ref[...] = jnp.zeros_like(dmw_ref)

        dxv = dx_ref[...]
        dg_ref[0] += _colsum(dxv * m_ref[...].astype(F32))
        dmb = (g_ref[0] * dxv).astype(BF16)
        dm_ref[...] = dmb
        dycat = _dot_nt(dmb, w_ref[...])
        dys_ref[...] = dycat[:, :D_SSD].astype(BF16)
        xh, r = _rms_stats(o_ref[...].astype(F32))
        dx, dw_rows = _rms_bwd(dycat[:, D_SSD:], xh, r, mw_ref[...])
        do_ref[...] = dx.astype(BF16)
        dmw_ref[...] += _colsum(dw_rows)

    rows = lambda n: pl.BlockSpec((tm, n), lambda i: (i, 0))
    perb = pl.BlockSpec((1, 1, D), lambda i: (i // tps, 0, 0))
    sd = jax.ShapeDtypeStruct
    return pl.pallas_call(
        body, grid=(T // tm,), name="mix_out_bwd",
        in_specs=[rows(D), rows(D), rows(D_MLA), _resident((1, D_MLA)), _resident((D_SSD + D_MLA, D)), perb],
        out_specs=[rows(D_SSD), rows(D_MLA), rows(D), perb, pl.BlockSpec((1, D_MLA), lambda i: (0, 0))],
        out_shape=[sd((T, D_SSD), BF16), sd((T, D_MLA), BF16), sd((T, D), BF16), sd((B, 1, D), F32), sd((1, D_MLA), F32)],
        compiler_params=_cparams(("arbitrary",)),
    )(dx2, m, o, mw, wout, g)


def _win_to_kernel(w):
    z0 = jnp.zeros((w.shape[0], 48), w.dtype)
    z1 = jnp.zeros((w.shape[0], 32), w.dtype)
    return jnp.concatenate([w[:, :2560], w[:, 2576:3216], w[:, 2560:2576], z0, w[:, 3216:3248], z1], axis=1)


def _win_from_kernel(g):
    return jnp.concatenate([g[:, :2560], g[:, 3200:3216], g[:, 2560:3200], g[:, 3264:3296]], axis=1)


def _wuq_to_kernel(w):
    w = w.reshape(Q_LORA, MLA_HEADS, QK_DIM)
    return jnp.pad(w, ((0, 0), (0, 0), (0, HEAD_LANES - QK_DIM))).reshape(Q_LORA, MLA_HEADS * HEAD_LANES)


def _wuq_from_kernel(g):
    return g.reshape(Q_LORA, MLA_HEADS, HEAD_LANES)[:, :, :QK_DIM].reshape(Q_LORA, MLA_HEADS * QK_DIM)


def _wukv_to_kernel(w):
    w = w.reshape(KV_LORA, MLA_HEADS, QK_NOPE + V_HEAD)
    kp = jnp.pad(w[:, :, :QK_NOPE], ((0, 0), (0, 0), (0, HEAD_LANES - QK_NOPE)))
    return jnp.concatenate([kp.reshape(KV_LORA, -1), w[:, :, QK_NOPE:].reshape(KV_LORA, -1)], axis=1)


def _wukv_from_kernel(g):
    hw = MLA_HEADS * HEAD_LANES
    kp = g[:, :hw].reshape(KV_LORA, MLA_HEADS, HEAD_LANES)[:, :, :QK_NOPE]
    vp = g[:, hw:].reshape(KV_LORA, MLA_HEADS, V_HEAD)
    return jnp.concatenate([kp, vp], axis=2).reshape(KV_LORA, MLA_HEADS * (QK_NOPE + V_HEAD))


def _lanes16(v):
    return jnp.pad(v.reshape(1, SSD_HEADS), ((0, 0), (0, LANES - SSD_HEADS)))


def _constants():
    e = np.zeros((LANES, D_SSD), np.float32)
    for h in range(SSD_HEADS):
        e[h, h * SSD_HEAD_DIM:(h + 1) * SSD_HEAD_DIM] = 1.0
    inv_freq = ROPE_THETA ** (-jnp.arange(0, QK_ROPE, 2, dtype=F32) / QK_ROPE)
    half = QK_ROPE // 2
    invf = jnp.zeros((1, LANES), F32).at[0, QK_NOPE:QK_NOPE + half].set(inv_freq).at[0, QK_NOPE + half:QK_DIM].set(inv_freq)
    return jnp.asarray(e, BF16), invf


def _local_step(x, positions, mod, w, later_weights, small, tgt, on_grads):
    B, S, D = x.shape
    T = B * S
    expand, invf = _constants()
    x0 = x.reshape(T, D)
    pos = positions.reshape(T, 1)
    mods = [mod[:, i * D:(i + 1) * D].reshape(B, 1, D) for i in range(N_MOD)]
    sh1, sc1, g1, sh2, sc2, g2, sh3, sc3, g3 = mods
    dtb, alog = _lanes16(small["dt_bias"]), _lanes16(small["a_log"])
    dsk = jnp.repeat(small["d_skip"].reshape(1, SSD_HEADS), SSD_HEAD_DIM, axis=1)

    x1, a1, u1, f1 = _ffn_fwd(x0, small["norm_ffn1"], sh1, sc1, g1, w["ffn1_w_gate"], w["ffn1_w_up"], w["ffn1_w_down"], S, "ffn1_fwd")
    w = {**w, **later_weights(f1)}
    z, xraw, cq, ckv, dtk = _inproj_fwd(x1, small["norm_mix"], sh2, sc2, w["w_in"], S)
    xraw3 = xraw.reshape(B, S, D_CONV)
    xbc = _conv_fwd(xraw3, small["conv_w"], small["conv_b"])
    dtk3, z3 = dtk.reshape(B, S, LANES), z.reshape(B, S, D_SSD)
    y, yssd, prev = _ssd_fwd(xbc, dtk3, z3, dtb, alog, dsk, small["ssd_norm_w"], expand)
    q, k, v = _mla_prep(cq, ckv, dtk, pos, small["q_norm_w"], small["kv_norm_w"], w["w_uq"], w["w_ukv"], invf)
    hw = MLA_HEADS * HEAD_LANES
    q3, k3, v3 = q.reshape(B, S, hw), k.reshape(B, S, hw), v.reshape(B, S, hw)
    o3, lse = _attn_fwd(q3, k3, v3)
    o = o3.reshape(T, hw)
    x2, m, ycat = _mix_out(x1, yssd.reshape(T, D_SSD), o, small["mla_norm_w"], w["w_out"], g2, S)
    x3, a2, u2, f2 = _ffn_fwd(x2, small["norm_ffn2"], sh3, sc3, g3, w["ffn2_w_gate"], w["ffn2_w_up"], w["ffn2_w_down"], S, "ffn2_fwd")
    dx3, loss, d_norm_final = _final_loss(x3, small["norm_final"].reshape(1, D), tgt.reshape(T, D))

    gw, gs = {}, {}
    dx2, h3, s3, df3, da3, du3, dsh3, dsc3, dg3, gs["norm_ffn2"] = _ffn_bwd(
        dx3, x2, small["norm_ffn2"], sh3, sc3, g3, a2, u2, f2, w["ffn2_w_gate"], w["ffn2_w_up"], w["ffn2_w_down"], S, "ffn2_bwd")
    gw["ffn2_w_gate"], gw["ffn2_w_up"], gw["ffn2_w_down"] = _ffn_wgrad(h3, s3, df3, da3, du3, "ffn2_wgrad")
    g2 = g2 + on_grads(("ffn2_w_gate", "ffn2_w_up", "ffn2_w_down"), gw)

    dys, do, dm, dg2, gs["mla_norm_w"] = _mix_out_bwd(dx2, m, o, small["mla_norm_w"], w["w_out"], g2, S)
    gw["w_out"] = _mm_tn(ycat, dm, 512, "dwout")

    dq3, dk3, dv3 = _attn_bwd(q3, k3, v3, o3, do.reshape(B, S, hw), lse)
    dcq, dckv, ddtk_b, qn, kvn, dqb, dkvb, gs["q_norm_w"], gs["kv_norm_w"] = _mla_prep_bwd(
        dq3.reshape(T, hw), dk3.reshape(T, hw), dv3.reshape(T, hw), cq, ckv, pos, small["q_norm_w"], small["kv_norm_w"],
        w["w_uq"], w["w_ukv"], invf)
    gw["w_uq"] = _mm_tn(qn, dqb, 512, "dwuq")
    gw["w_ukv"] = _mm_tn(kvn, dkvb, 1024, "dwukv")

    dxbc, ddtk_a, dz, gs["ssd_norm_w"], dvec = _ssd_bwd(
        xbc, dtk3, z3, y, prev, dys.reshape(B, S, D_SSD), dtb, alog, dsk, small["ssd_norm_w"], expand)
    gs["dt_bias"], gs["a_log"], gs["d_skip"] = dvec[0:1, :SSD_HEADS], dvec[1:2, :SSD_HEADS], dvec[2:3, :SSD_HEADS]
    dxraw, gs["conv_w"], gs["conv_b"] = _conv_bwd(dxbc, xraw3, small["conv_w"], small["conv_b"])
    dx1, h2, dproj, dsh2, dsc2, gs["norm_mix"] = _inproj_bwd(
        dx2, x1, small["norm_mix"], sh2, sc2, w["w_in"], dz.reshape(T, D_SSD), dxraw.reshape(T, D_CONV), dcq, dckv,
        ddtk_a.reshape(T, LANES), ddtk_b, S)
    gw["w_in"] = _mm_tn(h2, dproj, 1664, "dwin")
    g1 = g1 + on_grads(("w_in", "w_uq", "w_ukv", "w_out"), gw)

    dx0, h1, s1, df1, da1, du1, dsh1, dsc1, dg1, gs["norm_ffn1"] = _ffn_bwd(
        dx1, x0, small["norm_ffn1"], sh1, sc1, g1, a1, u1, f1, w["ffn1_w_gate"], w["ffn1_w_up"], w["ffn1_w_down"], S, "ffn1_bwd")
    gw["ffn1_w_gate"], gw["ffn1_w_up"], gw["ffn1_w_down"] = _ffn_wgrad(h1, s1, df1, da1, du1, "ffn1_wgrad")
    gs["norm_final"] = d_norm_final
    dmod = jnp.concatenate([t.reshape(B, D) for t in (dsh1, dsc1, dg1, dsh2, dsc2, dg2, dsh3, dsc3, dg3)], axis=1)
    return loss, dx0.reshape(B, S, D), gw, dmod, gs


HBM_SPEC = pl.BlockSpec(memory_space=pltpu.HBM)
VMEM_SPEC = pl.BlockSpec(memory_space=pltpu.VMEM)


def _place():
    return lax.axis_index("x"), lax.axis_index("y"), lax.axis_index("c")


def _other_chips(mx, my):
    return [(1 - mx, my), (mx, 1 - my), (1 - mx, 1 - my)]


def _remote(src, dst, send_sem, recv_sem, to):
    return pltpu.make_async_remote_copy(src_ref=src, dst_ref=dst, send_sem=send_sem, recv_sem=recv_sem,
                                        device_id=to, device_id_type=MESH)


def _all_gather_small(xa, name):
    r, n = xa.shape

    def body(x_ref, o_ref, token, send_sems, recv_sems):
        mx, my, mc = _place()
        me = 4 * mx + 2 * my + mc
        token[...] = jnp.zeros_like(token)
        o_ref[pl.ds(me, 1)] = x_ref[...][None]
        sends = []
        for k in range(1, N_DEV):
            peer = (mx ^ (k >> 2), my ^ ((k >> 1) & 1), mc ^ (k & 1))
            cp = _remote(x_ref, o_ref.at[me], send_sems.at[k - 1], recv_sems.at[k - 1], peer)
            cp.start()
            sends.append(cp)
        for k in range(1, N_DEV):
            peer = (mx ^ (k >> 2), my ^ ((k >> 1) & 1), mc ^ (k & 1))
            slot = 4 * peer[0] + 2 * peer[1] + peer[2]
            _remote(x_ref, o_ref.at[slot], send_sems.at[k - 1], recv_sems.at[k - 1], peer).wait_recv()
        for cp in sends:
            cp.wait_send()

    return pl.pallas_call(
        body, name=name, in_specs=[VMEM_SPEC], out_specs=[VMEM_SPEC, VMEM_SPEC],
        out_shape=[jax.ShapeDtypeStruct((N_DEV, r, n), xa.dtype), jax.ShapeDtypeStruct((8, LANES), F32)],
        scratch_shapes=[pltpu.SemaphoreType.DMA((N_DEV - 1,)), pltpu.SemaphoreType.DMA((N_DEV - 1,))],
        compiler_params=pltpu.CompilerParams(vmem_limit_bytes=VMEM_LIMIT),
    )(xa)


def _half_rows(ref, hc, rh, lead=None):
    rows = pl.ds(pl.multiple_of(hc * rh, 8), rh)
    return ref.at[rows, :] if lead is None else ref.at[lead, rows, :]


def _gather_weights(shards):
    n = len(shards)

    def body(*refs):
        w_refs, o_refs, token = refs[:n], refs[n:2 * n], refs[2 * n]
        send_sems, recv_sems, stage_sems = refs[2 * n + 1:2 * n + 4]
        stages = refs[2 * n + 4:]
        mx, my, mc = _place()
        chip = 2 * mx + my
        others = _other_chips(mx, my)
        sibling = (mx, my, 1 - mc)
        token[...] = jnp.zeros_like(token)
        stage_in = [pltpu.make_async_copy(w, st, stage_sems.at[0, i]) for i, (w, st) in enumerate(zip(w_refs, stages))]
        for cp in stage_in:
            cp.start()
        first = []
        for i, (w, o) in enumerate(zip(w_refs, o_refs)):
            rh = w.shape[0] // 2
            for k, (cx, cy) in enumerate(others):
                first.append(_remote(_half_rows(w, mc, rh), _half_rows(o, mc, rh, chip), send_sems.at[i, k],
                                     recv_sems.at[i, k], (cx, cy, mc)))
                first[-1].start()
        stage_out = []
        for i, (st, o) in enumerate(zip(stages, o_refs)):
            stage_in[i].wait()
            stage_out.append(pltpu.make_async_copy(st, o.at[chip], stage_sems.at[1, i]))
            stage_out[-1].start()
        passed = []
        for i, (w, o) in enumerate(zip(w_refs, o_refs)):
            rh = w.shape[0] // 2
            for k, (cx, cy) in enumerate(others):
                landed = _half_rows(o, mc, rh, 2 * cx + cy)
                _remote(landed, landed, send_sems.at[i, k], recv_sems.at[i, k], (cx, cy, mc)).wait_recv()
                passed.append(_remote(landed, landed, send_sems.at[i, 3 + k], recv_sems.at[i, 3 + k], sibling))
                passed[-1].start()
        for i, (w, o) in enumerate(zip(w_refs, o_refs)):
            rh = w.shape[0] // 2
            for k, (cx, cy) in enumerate(others):
                there = _half_rows(o, 1 - mc, rh, 2 * cx + cy)
                _remote(there, there, send_sems.at[i, 3 + k], recv_sems.at[i, 3 + k], sibling).wait_recv()
        for cp in first + passed:
            cp.wait_send()
        for cp in stage_out:
            cp.wait()

    out = pl.pallas_call(
        body, name="gather_weights", in_specs=[HBM_SPEC] * n, out_specs=[HBM_SPEC] * n + [VMEM_SPEC],
        out_shape=[jax.ShapeDtypeStruct((N_CHIPS,) + s.shape, s.dtype) for s in shards] + [jax.ShapeDtypeStruct((8, LANES), F32)],
        scratch_shapes=[pltpu.SemaphoreType.DMA((n, 6)), pltpu.SemaphoreType.DMA((n, 6)), pltpu.SemaphoreType.DMA((2, n))]
        + [pltpu.VMEM(s.shape, s.dtype) for s in shards],
        compiler_params=pltpu.CompilerParams(vmem_limit_bytes=VMEM_LIMIT),
    )(*shards)
    return out[:n], out[n]


SEM_SPEC = pl.BlockSpec(memory_space=pltpu.SEMAPHORE)
ANY_SPEC = pl.BlockSpec(memory_space=pl.ANY)
DATAFLOW = pltpu.SideEffectType.DATAFLOW_SIDE_EFFECTING


def _hbm(arr):
    return pltpu.with_memory_space_constraint(arr, pltpu.HBM)


def _gather_start(shards):
    n = len(shards)

    def body(*refs):
        w_refs, land_refs, send_sems, recv_sems, token = refs[:n], refs[n:2 * n], refs[2 * n], refs[2 * n + 1], refs[-1]
        mx, my, mc = _place()
        chip = 2 * mx + my
        for i, (w, land) in enumerate(zip(w_refs, land_refs)):
            rh = w.shape[0] // 2
            for k, (cx, cy) in enumerate(_other_chips(mx, my)):
                _remote(_half_rows(w, mc, rh), _half_rows(land, mc, rh, chip), send_sems.at[3 * i + k],
                        recv_sems.at[3 * i + k], (cx, cy, mc)).start()
        token[...] = jnp.zeros_like(token)

    lands = [lax.empty((N_CHIPS,) + s.shape, s.dtype) for s in shards]
    out = pl.pallas_call(
        body, name="gather_start",
        out_shape=(pltpu.SemaphoreType.DMA((3 * n,)), pltpu.SemaphoreType.DMA((3 * n,)),
                   *[pltpu.HBM(s.shape, s.dtype) for s in shards], *[pltpu.HBM(l.shape, l.dtype) for l in lands],
                   jax.ShapeDtypeStruct((8, LANES), F32)),
        in_specs=[HBM_SPEC] * (2 * n), out_specs=(SEM_SPEC, SEM_SPEC, *[HBM_SPEC] * (2 * n), VMEM_SPEC),
        input_output_aliases={i: 2 + i for i in range(2 * n)},
        compiler_params=pltpu.CompilerParams(has_side_effects=DATAFLOW),
    )(*[_hbm(s) for s in shards], *[_hbm(l) for l in lands])
    return out[0], out[1], out[2:2 + n], out[2 + n:2 + 2 * n], out[-1]


def _gather_wait(send_sems, recv_sems, shards, lands, after):
    n = len(shards)

    def body(*refs):
        w_refs, land_refs, send_sems, recv_sems = refs[:n], refs[n:2 * n], refs[2 * n], refs[2 * n + 1]
        mx, my, mc = _place()
        for i, (w, land) in enumerate(zip(w_refs, land_refs)):
            rh = w.shape[0] // 2
            for k, (cx, cy) in enumerate(_other_chips(mx, my)):
                cp = _remote(_half_rows(w, mc, rh), _half_rows(land, mc, rh, 2 * cx + cy), send_sems.at[3 * i + k],
                             recv_sems.at[3 * i + k], (cx, cy, mc))
                cp.wait_send()
                cp.wait_recv()

    out = pl.pallas_call(
        body, name="gather_wait",
        out_shape=(*[pltpu.HBM(s.shape, s.dtype) for s in shards], *[pltpu.HBM(l.shape, l.dtype) for l in lands]),
        in_specs=[HBM_SPEC] * (2 * n) + [SEM_SPEC, SEM_SPEC, ANY_SPEC], out_specs=tuple([HBM_SPEC] * (2 * n)),
        input_output_aliases={i: i for i in range(2 * n)},
        compiler_params=pltpu.CompilerParams(has_side_effects=DATAFLOW),
    )(*shards, *lands, send_sems, recv_sems, after)
    return out[n:]


def _gather_finish(shards, lands):
    n = len(shards)

    def body(*refs):
        w_refs, land_refs, o_refs = refs[:n], refs[n:2 * n], refs[2 * n:3 * n]
        send_sems, recv_sems, stage_sems = refs[3 * n:3 * n + 3]
        stages = refs[3 * n + 3:]
        mx, my, mc = _place()
        chip = 2 * mx + my
        others = _other_chips(mx, my)
        sibling = (mx, my, 1 - mc)
        stage_in = [pltpu.make_async_copy(w, st, stage_sems.at[0, i]) for i, (w, st) in enumerate(zip(w_refs, stages))]
        for cp in stage_in:
            cp.start()
        passed = []
        for i, (w, o) in enumerate(zip(w_refs, o_refs)):
            rh = w.shape[0] // 2
            for k, (cx, cy) in enumerate(others):
                landed = _half_rows(o, mc, rh, 2 * cx + cy)
                passed.append(_remote(landed, landed, send_sems.at[i, k], recv_sems.at[i, k], sibling))
                passed[-1].start()
        stage_out = []
        for i, (st, o) in enumerate(zip(stages, o_refs)):
            stage_in[i].wait()
            stage_out.append(pltpu.make_async_copy(st, o.at[chip], stage_sems.at[1, i]))
            stage_out[-1].start()
        for i, (w, o) in enumerate(zip(w_refs, o_refs)):
            rh = w.shape[0] // 2
            for k, (cx, cy) in enumerate(others):
                there = _half_rows(o, 1 - mc, rh, 2 * cx + cy)
                _remote(there, there, send_sems.at[i, k], recv_sems.at[i, k], sibling).wait_recv()
        for cp in passed:
            cp.wait_send()
        for cp in stage_out:
            cp.wait()

    return pl.pallas_call(
        body, name="gather_finish", in_specs=[HBM_SPEC] * (2 * n), out_specs=[HBM_SPEC] * n,
        out_shape=[jax.ShapeDtypeStruct(l.shape, l.dtype) for l in lands],
        input_output_aliases={n + i: i for i in range(n)},
        scratch_shapes=[pltpu.SemaphoreType.DMA((n, 3)), pltpu.SemaphoreType.DMA((n, 3)), pltpu.SemaphoreType.DMA((2, n))]
        + [pltpu.VMEM(s.shape, s.dtype) for s in shards],
        compiler_params=pltpu.CompilerParams(vmem_limit_bytes=VMEM_LIMIT),
    )(*shards, *lands)


def _scatter_start(ss, tag):
    n = len(ss)

    def body(*refs):
        s_refs, land_refs, send_sems, recv_sems, token = refs[:n], refs[n:2 * n], refs[2 * n], refs[2 * n + 1], refs[-1]
        mx, my, mc = _place()
        chip = 2 * mx + my
        for i, (s, land) in enumerate(zip(s_refs, land_refs)):
            for k, (cx, cy) in enumerate(_other_chips(mx, my)):
                _remote(s.at[2 * cx + cy], land.at[chip], send_sems.at[3 * i + k], recv_sems.at[3 * i + k],
                        (cx, cy, mc)).start()
        token[...] = jnp.zeros_like(token)

    lands = [lax.empty(s.shape, s.dtype) for s in ss]
    out = pl.pallas_call(
        body, name="scatter_start_" + tag,
        out_shape=(pltpu.SemaphoreType.DMA((3 * n,)), pltpu.SemaphoreType.DMA((3 * n,)),
                   *[pltpu.HBM(s.shape, s.dtype) for s in ss], *[pltpu.HBM(l.shape, l.dtype) for l in lands],
                   jax.ShapeDtypeStruct((8, LANES), F32)),
        in_specs=[HBM_SPEC] * (2 * n), out_specs=(SEM_SPEC, SEM_SPEC, *[HBM_SPEC] * (2 * n), VMEM_SPEC),
        input_output_aliases={i: 2 + i for i in range(2 * n)},
        compiler_params=pltpu.CompilerParams(has_side_effects=DATAFLOW),
    )(*[_hbm(s) for s in ss], *[_hbm(l) for l in lands])
    return out[0], out[1], out[2:2 + n], out[2 + n:2 + 2 * n], out[-1]


def _scatter_wait(send_sems, recv_sems, ss, lands, after, tag):
    n = len(ss)

    def body(*refs):
        s_refs, land_refs, send_sems, recv_sems = refs[:n], refs[n:2 * n], refs[2 * n], refs[2 * n + 1]
        mx, my, mc = _place()
        for i, (s, land) in enumerate(zip(s_refs, land_refs)):
            for k, (cx, cy) in enumerate(_other_chips(mx, my)):
                slot = land.at[2 * cx + cy]
                cp = _remote(s.at[2 * cx + cy], slot, send_sems.at[3 * i + k], recv_sems.at[3 * i + k], (cx, cy, mc))
                cp.wait_send()
                cp.wait_recv()

    out = pl.pallas_call(
        body, name="scatter_wait_" + tag,
        out_shape=(*[pltpu.HBM(s.shape, s.dtype) for s in ss], *[pltpu.HBM(l.shape, l.dtype) for l in lands]),
        in_specs=[HBM_SPEC] * (2 * n) + [SEM_SPEC, SEM_SPEC, ANY_SPEC], out_specs=tuple([HBM_SPEC] * (2 * n)),
        input_output_aliases={i: i for i in range(2 * n)},
        compiler_params=pltpu.CompilerParams(has_side_effects=DATAFLOW),
    )(*ss, *lands, send_sems, recv_sems, after)
    return out[:n], out[n:]


def _swap_halves(gs, after, name):
    n = len(gs)

    def body(*refs):
        g_refs, o_refs, send_sems, recv_sems = refs[:n], refs[n + 1:2 * n + 1], refs[2 * n + 1], refs[2 * n + 2]
        mx, my, mc = _place()
        copies = []
        for i, (g, o) in enumerate(zip(g_refs, o_refs)):
            rh = g.shape[1] // 2
            src = g.at[:, pl.ds(pl.multiple_of((1 - mc) * rh, 8), rh), :]
            copies.append(_remote(src, o, send_sems.at[i], recv_sems.at[i], (mx, my, 1 - mc)))
            copies[-1].start()
        for cp in copies:
            cp.wait()

    return pl.pallas_call(
        body, name=name, in_specs=[HBM_SPEC] * n + [ANY_SPEC], out_specs=[HBM_SPEC] * n,
        out_shape=[jax.ShapeDtypeStruct((N_CHIPS, g.shape[1] // 2, g.shape[2]), g.dtype) for g in gs],
        scratch_shapes=[pltpu.SemaphoreType.DMA((n,)), pltpu.SemaphoreType.DMA((n,))],
    )(*gs, after)


def _pair_sum(g, got, core, name):
    _, r, c = g.shape
    rh = r // 2

    def body(core_ref, g_ref, got_ref, o_ref):
        o_ref[...] = (g_ref[...] + got_ref[...]).astype(BF16)

    return pl.pallas_call(
        body, name=name,
        grid_spec=pltpu.PrefetchScalarGridSpec(
            num_scalar_prefetch=1, grid=(N_CHIPS,),
            in_specs=[pl.BlockSpec((1, rh, c), lambda j, core_ref: (j, core_ref[0], 0)),
                      pl.BlockSpec((1, rh, c), lambda j, core_ref: (j, 0, 0))],
            out_specs=pl.BlockSpec((1, rh, c), lambda j, core_ref: (j, 0, 0))),
        out_shape=jax.ShapeDtypeStruct((N_CHIPS, rh, c), BF16),
        compiler_params=_cparams(("arbitrary",)),
    )(core, g, got)


def _chip_sum(own, got, chip, name):
    _, h, c = own.shape

    def body(chip_ref, a_ref, b_ref, c_ref, d_ref, o_ref):
        o_ref[...] = ((a_ref[0].astype(F32) + b_ref[0].astype(F32)) + c_ref[0].astype(F32)) + d_ref[0].astype(F32)

    slot = lambda flip: pl.BlockSpec((1, h, c), lambda i, chip_ref: (chip_ref[0] ^ flip, 0, 0))
    return pl.pallas_call(
        body, name=name,
        grid_spec=pltpu.PrefetchScalarGridSpec(
            num_scalar_prefetch=1, grid=(1,), in_specs=[slot(0), slot(1), slot(2), slot(3)],
            out_specs=pl.BlockSpec((h, c), lambda i, chip_ref: (0, 0))),
        out_shape=jax.ShapeDtypeStruct((h, c), F32),
        compiler_params=_cparams(("arbitrary",)),
    )(chip, own, got, got, got)


def _join_halves(mine, name):
    n = len(mine)

    def body(*refs):
        m_refs, o_refs, send_sems, recv_sems = refs[:n], refs[n:2 * n], refs[2 * n], refs[2 * n + 1]
        mx, my, mc = _place()
        copies = []
        for i, (m, o) in enumerate(zip(m_refs, o_refs)):
            copies.append(_remote(m, o, send_sems.at[i], recv_sems.at[i], (mx, my, 1 - mc)))
            copies[-1].start()
        for cp in copies:
            cp.wait()

    return pl.pallas_call(
        body, name=name, in_specs=[HBM_SPEC] * n, out_specs=[HBM_SPEC] * n,
        out_shape=[jax.ShapeDtypeStruct(m.shape, m.dtype) for m in mine],
        scratch_shapes=[pltpu.SemaphoreType.DMA((n,)), pltpu.SemaphoreType.DMA((n,))],
    )(*mine)


def _adam_math(w, g, m, v):
    m2 = ADAM_B1 * m + (1.0 - ADAM_B1) * g
    v2 = ADAM_B2 * v + (1.0 - ADAM_B2) * (g * g)
    m_hat = m2 * (1.0 / (1.0 - ADAM_B1 ** ADAM_STEP))
    v_hat = v2 * (1.0 / (1.0 - ADAM_B2 ** ADAM_STEP))
    delta = -ADAM_LR * (m_hat / (jnp.sqrt(v_hat) + ADAM_EPS) + ADAM_WD * w)
    return delta, m2, v2


def _adam(w, g, m, v, name):
    def body(w_ref, g_ref, m_ref, v_ref, d_ref, m2_ref, v2_ref):
        d_ref[...], m2_ref[...], v2_ref[...] = _adam_math(w_ref[...], g_ref[...], m_ref[...], v_ref[...])

    return pl.pallas_call(body, name=name, out_shape=[jax.ShapeDtypeStruct(w.shape, F32)] * 3)(w, g, m, v)


def _adam_halves(w, m, v, mine, theirs, core, name):
    _, r, c = w.shape
    rh = r // 2

    def body(core_ref, w_ref, m_ref, v_ref, mine_ref, theirs_ref, g_ref, d_ref, m2_ref, v2_ref):
        g = jnp.where(pl.program_id(0) == core_ref[0], mine_ref[...], theirs_ref[...])
        g_ref[0] = g
        d_ref[0], m2_ref[0], v2_ref[0] = _adam_math(w_ref[0], g, m_ref[0], v_ref[0])

    half = pl.BlockSpec((1, rh, c), lambda hc, core_ref: (0, hc, 0))
    whole = pl.BlockSpec((rh, c), lambda hc, core_ref: (0, 0))
    return pl.pallas_call(
        body, name=name,
        grid_spec=pltpu.PrefetchScalarGridSpec(
            num_scalar_prefetch=1, grid=(2,), in_specs=[half, half, half, whole, whole], out_specs=[half] * 4),
        out_shape=[jax.ShapeDtypeStruct(w.shape, F32)] * 4,
        compiler_params=_cparams(("arbitrary",)),
    )(core, w, m, v, mine, theirs)


ADA_COLS = N_MOD * D_MODEL // N_CHIPS


def _ada_fwd(c_all, w_ada, b_cols):
    def body(c_ref, w_ref, b_ref, o_ref):
        cv = c_ref[...]
        act = (cv * _sigmoid(cv)).astype(BF16)
        o_ref[...] = _dot(act, w_ref[...].astype(BF16)) + b_ref[...]

    return pl.pallas_call(
        body, name="ada_fwd", out_shape=jax.ShapeDtypeStruct((c_all.shape[0], ADA_COLS), F32),
        compiler_params=pltpu.CompilerParams(vmem_limit_bytes=VMEM_LIMIT),
    )(c_all, w_ada, b_cols)


def _ada_bwd(c_all, dmod_cols, w, m, v):
    nb = c_all.shape[0]
    tn = 384

    def body(c_ref, d_ref, w_ref, m_ref, v_ref, g_ref, dl_ref, m2_ref, v2_ref):
        cv = c_ref[...]
        act = (cv * _sigmoid(cv)).astype(BF16)
        g = _dot_tn(act, d_ref[...].astype(BF16))
        g_ref[...] = g
        dl_ref[...], m2_ref[...], v2_ref[...] = _adam_math(w_ref[...], g, m_ref[...], v_ref[...])

    blk = pl.BlockSpec((D_MODEL, tn), lambda j: (0, j))
    return pl.pallas_call(
        body, name="ada_bwd", grid=(ADA_COLS // tn,),
        in_specs=[pl.BlockSpec((nb, D_MODEL), lambda j: (0, 0)), pl.BlockSpec((nb, tn), lambda j: (0, j)), blk, blk, blk],
        out_specs=[blk] * 4, out_shape=[jax.ShapeDtypeStruct((D_MODEL, ADA_COLS), F32)] * 4,
        compiler_params=_cparams(("arbitrary",)),
    )(c_all, dmod_cols, w, m, v)


SMALL_NAMES = ("norm_ffn1", "norm_mix", "conv_w", "conv_b", "ssd_norm_w", "q_norm_w", "kv_norm_w", "mla_norm_w",
               "norm_ffn2", "norm_final", "dt_bias", "a_log", "d_skip")
SMALL_SIZES = (1024, 1024, CONV_WIDTH * D_CONV, D_CONV, 1024, Q_LORA, KV_LORA, 1024, 1024, 1024, 16, 16, 16)
SMALL_ROWS = 16
MOD_ROWS = 2 * N_MOD
SEND_ROWS = 40


def _pack_small(parts):
    flat = jnp.concatenate([parts[n].reshape(-1) for n in SMALL_NAMES])
    return jnp.pad(flat, (0, SMALL_ROWS * D_MODEL - flat.shape[0]))


def _unpack_small(flat):
    out, off = {}, 0
    for n, size in zip(SMALL_NAMES, SMALL_SIZES):
        out[n] = flat[off:off + size]
        off += size
    return out


def _small_sum(got):
    def body(g_ref, o_ref):
        bsum = jnp.zeros((N_MOD, D_MODEL), F32)
        ssum = jnp.zeros((SMALL_ROWS, D_MODEL), F32)
        for d in range(N_DEV):
            bsum = bsum + g_ref[d, 0:N_MOD, :] + g_ref[d, N_MOD:MOD_ROWS, :]
            ssum = ssum + g_ref[d, MOD_ROWS:MOD_ROWS + SMALL_ROWS, :]
        o_ref[...] = jnp.concatenate([bsum, ssum, jnp.zeros((32 - N_MOD - SMALL_ROWS, D_MODEL), F32)], axis=0)

    return pl.pallas_call(body, name="small_sum", out_shape=jax.ShapeDtypeStruct((32, D_MODEL), F32))(got)


BIG_NAMES = ("ffn1_w_gate", "ffn1_w_up", "ffn1_w_down", "w_in", "w_uq", "w_ukv", "w_out", "ffn2_w_gate", "ffn2_w_up",
             "ffn2_w_down")
_TO_KERNEL = {"w_in": _win_to_kernel, "w_uq": _wuq_to_kernel, "w_ukv": _wukv_to_kernel}
_FROM_KERNEL = {"w_in": _win_from_kernel, "w_uq": _wuq_from_kernel, "w_ukv": _wukv_from_kernel}


def _columns_joined(w4):
    n, r, c = w4.shape
    return w4.transpose(1, 0, 2).reshape(r, n * c)


def _columns_split(g):
    r, cols = g.shape
    return g.reshape(r, N_CHIPS, cols // N_CHIPS).transpose(1, 0, 2)


def kernel(x, c, positions, w_ada, b_ada, norm_ffn1, ffn1_w_gate, ffn1_w_up, ffn1_w_down, norm_mix, w_in, conv_w, conv_b, dt_bias, a_log, d_skip, ssd_norm_w, q_norm_w, w_uq, kv_norm_w, w_ukv, mla_norm_w, w_out, norm_ffn2, ffn2_w_gate, ffn2_w_up, ffn2_w_down, norm_final, loss_target, m_w_ada, m_b_ada, m_norm_ffn1, m_ffn1_w_gate, m_ffn1_w_up, m_ffn1_w_down, m_norm_mix, m_w_in, m_conv_w, m_conv_b, m_dt_bias, m_a_log, m_d_skip, m_ssd_norm_w, m_q_norm_w, m_w_uq, m_kv_norm_w, m_w_ukv, m_mla_norm_w, m_w_out, m_norm_ffn2, m_ffn2_w_gate, m_ffn2_w_up, m_ffn2_w_down, m_norm_final, v_w_ada, v_b_ada, v_norm_ffn1, v_ffn1_w_gate, v_ffn1_w_up, v_ffn1_w_down, v_norm_mix, v_w_in, v_conv_w, v_conv_b, v_dt_bias, v_a_log, v_d_skip, v_ssd_norm_w, v_q_norm_w, v_w_uq, v_kv_norm_w, v_w_ukv, v_mla_norm_w, v_w_out, v_norm_ffn2, v_ffn2_w_gate, v_ffn2_w_up, v_ffn2_w_down, v_norm_final):
    a = dict(locals())
    B, S, D = x.shape
    mx, my, mc = _place()
    chip = 2 * mx + my
    dev = 2 * chip + mc
    core = mc.astype(jnp.int32).reshape(1)
    chip_id = chip.astype(jnp.int32).reshape(1)

    cw_rows = jnp.pad(conv_w[0], ((0, 0), (0, D - conv_w.shape[2])))
    got, _ = _all_gather_small(jnp.concatenate([c, cw_rows, jnp.zeros((8 - B - CONV_WIDTH, D), F32)], axis=0), "gather_c")
    c_all = got[:, :B, :].reshape(N_DEV * B, D)
    conv_full = got[::2, B:B + CONV_WIDTH, :conv_w.shape[2]].transpose(1, 0, 2).reshape(CONV_WIDTH, D_CONV)

    b_cols = lax.dynamic_slice(b_ada, (0, chip * ADA_COLS), (1, ADA_COLS))
    mod_all, mod_done = _all_gather_small(_ada_fwd(c_all, w_ada[0], b_cols), "gather_mod")
    mod = lax.dynamic_slice(mod_all, (0, B * dev, 0), (N_DEV, B, ADA_COLS))[::2].transpose(1, 0, 2).reshape(B, N_MOD * D)

    first = ("ffn1_w_gate", "ffn1_w_up", "ffn1_w_down")
    later = tuple(n for n in BIG_NAMES if n not in first)
    got_first, gathered = _gather_weights([(a[n][0] + mod_done[0, 0]).astype(BF16) for n in first])
    w = dict(zip(first, got_first))
    in_flight = _gather_start([(a[n][0] + gathered[0, 0]).astype(BF16) for n in later])

    def later_weights(after):
        send_sems, recv_sems, shards, lands, _ = in_flight
        lands = _gather_wait(send_sems, recv_sems, shards, lands, after)
        wl = dict(zip(later, _gather_finish([a[n][0].astype(BF16) for n in later], lands)))
        for n, to_kernel in _TO_KERNEL.items():
            wl[n] = to_kernel(_columns_joined(wl[n]))
        wl["w_out"] = wl["w_out"].reshape(D_SSD + D_MLA, D)
        return wl

    small = {n: a[n].reshape(1, -1) for n in SMALL_NAMES if n not in ("conv_w", "norm_final")}
    small["conv_w"], small["norm_final"] = conv_full, norm_final

    def scatter_group(names, gw, after):
        g4 = []
        for n in names:
            g = gw[n]
            if n in _FROM_KERNEL:
                g = _columns_split(_FROM_KERNEL[n](g))
            g4.append(g.reshape(N_CHIPS, a[n].shape[1], a[n].shape[2]))
        swapped = _swap_halves(g4, g4[0] if after is None else after, "swap_" + names[0])
        pair = [_pair_sum(g, got, core, "pair_sum_" + n) for n, g, got in zip(names, g4, swapped)]
        return (names,) + tuple(_scatter_start(pair, names[0]))

    grads, deltas, new_m, new_v = {}, {}, {}, {}

    def finish_group(group, after):
        names, send_sems, recv_sems, pair, lands, _ = group
        pair, lands = _scatter_wait(send_sems, recv_sems, pair, lands, after, names[0])
        mine = [_chip_sum(own, got, chip_id, "chip_sum_" + n) for n, own, got in zip(names, pair, lands)]
        for n, own, other in zip(names, mine, _join_halves(mine, "join_" + names[0])):
            grads[n], deltas[n], new_m[n], new_v[n] = _adam_halves(a[n], a["m_" + n], a["v_" + n], own, other, core, "adam_" + n)
        return deltas[names[-1]]

    groups = []

    def on_grads(names, gw):
        groups.append(scatter_group(names, gw, None))
        return groups[-1][5][0, 0]

    loss_blk, grad_x, gw, dmod, gs = _local_step(x, positions, mod + in_flight[4][0, 0], w, later_weights, small, loss_target,
                                                 on_grads)

    small_flat = _pack_small(gs).at[-1].set(loss_blk[0, 0])
    send = jnp.concatenate([dmod.reshape(MOD_ROWS, D), small_flat.reshape(SMALL_ROWS, D),
                            jnp.zeros((SEND_ROWS - MOD_ROWS - SMALL_ROWS, D), F32)], axis=0)
    got, _ = _all_gather_small(send, "gather_small")
    summed = _small_sum(got)
    sums = summed[N_MOD:N_MOD + SMALL_ROWS].reshape(-1)
    loss = sums[-1]
    gsmall = _unpack_small(sums)
    gsmall["conv_w"] = lax.dynamic_slice(gsmall["conv_w"].reshape(CONV_WIDTH, D_CONV), (0, chip * conv_w.shape[2]),
                                         (CONV_WIDTH, conv_w.shape[2]))
    gsmall["b_ada"] = summed[:N_MOD]
    names = ("b_ada",) + SMALL_NAMES
    rows = 208

    def pack(parts):
        flat = jnp.concatenate([parts[n].reshape(-1) for n in names])
        return jnp.pad(flat, (0, rows * LANES - flat.shape[0])).reshape(rows, LANES)

    packed = [pack({n: a[p + n] for n in names}) for p in ("", "m_", "v_")]
    g_p = pack(gsmall)
    outs = (g_p,) + tuple(_adam(packed[0], g_p, packed[1], packed[2], "adam_small"))
    for dst, flat in zip((grads, deltas, new_m, new_v), outs):
        flat, off = flat.reshape(-1), 0
        for n in names:
            dst[n] = flat[off:off + a[n].size].reshape(a[n].shape)
            off += a[n].size

    dmod_all = got[:, :MOD_ROWS, :].reshape(N_DEV * B, N_MOD * D)
    dmod_cols = lax.dynamic_slice(dmod_all, (0, chip * ADA_COLS), (N_DEV * B, ADA_COLS))
    ada = _ada_bwd(c_all, dmod_cols, w_ada[0], m_w_ada[0], v_w_ada[0])
    for dst, t in zip((grads, deltas, new_m, new_v), ada):
        dst["w_ada"] = t[None]

    last = scatter_group(first, gw, summed)
    after = last[5]
    for group in groups:
        after = finish_group(group, after)
    finish_group(last, after)

    order = ("w_ada", "b_ada", "norm_ffn1", "ffn1_w_gate", "ffn1_w_up", "ffn1_w_down", "norm_mix", "w_in", "conv_w", "conv_b",
             "dt_bias", "a_log", "d_skip", "ssd_norm_w", "q_norm_w", "w_uq", "kv_norm_w", "w_ukv", "mla_norm_w", "w_out",
             "norm_ffn2", "ffn2_w_gate", "ffn2_w_up", "ffn2_w_down", "norm_final")
    return (loss, grad_x, *[grads[n] for n in order], *[deltas[n] for n in order], *[new_m[n] for n in order],
            *[new_v[n] for n in order])
```

```python
import functools
import math

import jax
import jax.numpy as jnp
import numpy as np
from jax import lax
from jax.experimental import pallas as pl
from jax.experimental.pallas import tpu as pltpu

F32 = jnp.float32
BF16 = jnp.bfloat16
HIGHEST = lax.Precision.HIGHEST

D_MODEL = 1024
D_FF = 2816
D_SSD = 1024
D_MLA = 1024
SSD_HEADS = 16
SSD_HEAD_DIM = 64
SSD_GROUPS = 2
SSD_STATE = 128
CONV_WIDTH = 4
CHUNK = 128
MLA_HEADS = 8
QK_NOPE = 64
QK_ROPE = 32
QK_DIM = QK_NOPE + QK_ROPE
V_HEAD = 128
Q_LORA = 384
KV_LORA = 256
ROPE_THETA = 10000.0
N_MOD = 9
EPS = 1e-6
D_CONV = D_SSD + 2 * SSD_GROUPS * SSD_STATE
D_PROJ = 3328
HEAD_LANES = 128
ADAM_LR = 0.001
ADAM_B1 = 0.9
ADAM_B2 = 0.999
ADAM_EPS = 1e-08
ADAM_WD = 0.01
ADAM_STEP = 10

LANES = 128
VMEM_LIMIT = 56 * 1024 * 1024
TOKEN_TILE = 512
ATTN_FWD_Q_TILE = 1024
ATTN_FWD_KV_TILE = 1024
ATTN_BWD_TILE = 1024
N_CHIPS = 4
N_DEV = 8

MESH = pl.DeviceIdType.MESH


def _dot(a, b, precision=None):
    return jnp.dot(a, b, preferred_element_type=F32, precision=precision)


def _dot_nt(a, b, precision=None):
    return lax.dot_general(a, b, (((1,), (1,)), ((), ())), preferred_element_type=F32, precision=precision)


def _dot_tn(a, b, precision=None):
    return lax.dot_general(a, b, (((0,), (0,)), ((), ())), preferred_element_type=F32, precision=precision)


def _cparams(semantics):
    return pltpu.CompilerParams(dimension_semantics=semantics, vmem_limit_bytes=VMEM_LIMIT)


def _resident(shape):
    zeros = (0,) * len(shape)
    return pl.BlockSpec(shape, lambda *_: zeros, pipeline_mode=pl.Buffered(1))


def _sigmoid(x):
    return jax.nn.sigmoid(x)


def _rms_stats(x):
    r = lax.rsqrt(jnp.mean(x * x, axis=-1, keepdims=True) + EPS)
    return x * r, r


def _rms_bwd(dn, xh, r, w):
    dxh = dn * w
    dx = r * (dxh - xh * jnp.mean(dxh * xh, axis=-1, keepdims=True))
    return dx, dn * xh


def _colsum(v):
    return jnp.sum(v, axis=0, keepdims=True)


def _ffn_fwd(x, nw, sh, sc, g, wg, wu, wd, seq, name):
    T, D = x.shape
    fs = wg.shape[2]
    tm = min(TOKEN_TILE, seq)
    tps = seq // tm

    def body(x_ref, nw_ref, sh_ref, sc_ref, g_ref, wg_ref, wu_ref, wd_ref, xo_ref, a_ref, u_ref, f_ref):
        xv = x_ref[...]
        xh, _ = _rms_stats(xv)
        h = (xh * nw_ref[...]) * (1.0 + sc_ref[0]) + sh_ref[0]
        hb = h.astype(BF16)
        f = jnp.zeros((tm, D), F32)
        for j in range(N_CHIPS):
            a = _dot(hb, wg_ref[j])
            u = _dot(hb, wu_ref[j])
            a_ref[j] = a.astype(BF16)
            u_ref[j] = u.astype(BF16)
            f = f + _dot((a * _sigmoid(a) * u).astype(BF16), wd_ref[j])
        xo_ref[...] = xv + 0.5 * g_ref[0] * f
        f_ref[...] = f.astype(BF16)

    rows = lambda n: pl.BlockSpec((tm, n), lambda i: (i, 0))
    act = pl.BlockSpec((N_CHIPS, tm, fs), lambda i: (0, i, 0))
    perb = pl.BlockSpec((1, 1, D), lambda i: (i // tps, 0, 0))
    return pl.pallas_call(
        body, grid=(T // tm,), name=name,
        in_specs=[rows(D), _resident((1, D)), perb, perb, perb, _resident((N_CHIPS, D, fs)), _resident((N_CHIPS, D, fs)),
                  _resident((N_CHIPS, fs, D))],
        out_specs=[rows(D), act, act, rows(D)],
        out_shape=[jax.ShapeDtypeStruct((T, D), F32), jax.ShapeDtypeStruct((N_CHIPS, T, fs), BF16),
                   jax.ShapeDtypeStruct((N_CHIPS, T, fs), BF16), jax.ShapeDtypeStruct((T, D), BF16)],
        compiler_params=_cparams(("arbitrary",)),
    )(x, nw, sh, sc, g, wg, wu, wd)


def _ffn_bwd(dxo, x, nw, sh, sc, g, a, u, f, wg, wu, wd, seq, name):
    T, D = x.shape
    fs = wg.shape[2]
    B = T // seq
    tm = min(TOKEN_TILE // 2, seq)
    tps = seq // tm

    def body(dxo_ref, x_ref, nw_ref, sh_ref, sc_ref, g_ref, a_ref, u_ref, f_ref, wg_ref, wu_ref, wd_ref,
             dx_ref, h_ref, s_ref, df_ref, da_ref, du_ref, dsh_ref, dsc_ref, dg_ref, dnw_ref):
        i = pl.program_id(0)

        @pl.when(i % tps == 0)
        def _():
            dsh_ref[...] = jnp.zeros_like(dsh_ref)
            dsc_ref[...] = jnp.zeros_like(dsc_ref)
            dg_ref[...] = jnp.zeros_like(dg_ref)

        @pl.when(i == 0)
        def _():
            dnw_ref[...] = jnp.zeros_like(dnw_ref)

        dxo_v = dxo_ref[...]
        dfb = (0.5 * g_ref[0] * dxo_v).astype(BF16)
        dg_ref[0] += _colsum(0.5 * dxo_v * f_ref[...].astype(F32))
        dh = jnp.zeros((tm, D), F32)
        for j in range(N_CHIPS):
            ds = _dot_nt(dfb, wd_ref[j])
            av = a_ref[j].astype(F32)
            uv = u_ref[j].astype(F32)
            sig = _sigmoid(av)
            sil = av * sig
            dab = (ds * uv * (sig * (1.0 + av * (1.0 - sig)))).astype(BF16)
            dub = (ds * sil).astype(BF16)
            dh = dh + _dot_nt(dab, wg_ref[j]) + _dot_nt(dub, wu_ref[j])
            s_ref[j] = (sil * uv).astype(BF16)
            da_ref[j] = dab
            du_ref[j] = dub
        xv = x_ref[...]
        xh, r = _rms_stats(xv)
        nwv = nw_ref[...]
        n = xh * nwv
        scale1 = 1.0 + sc_ref[0]
        dsc_ref[0] += _colsum(dh * n)
        dsh_ref[0] += _colsum(dh)
        dx, dw_rows = _rms_bwd(dh * scale1, xh, r, nwv)
        dnw_ref[...] += _colsum(dw_rows)
        dx_ref[...] = dxo_v + dx
        h_ref[...] = (n * scale1 + sh_ref[0]).astype(BF16)
        df_ref[...] = dfb

    rows = lambda n: pl.BlockSpec((tm, n), lambda i: (i, 0))
    act = pl.BlockSpec((N_CHIPS, tm, fs), lambda i: (0, i, 0))
    perb = pl.BlockSpec((1, 1, D), lambda i: (i // tps, 0, 0))
    sd = jax.ShapeDtypeStruct
    return pl.pallas_call(
        body, grid=(T // tm,), name=name,
        in_specs=[rows(D), rows(D), _resident((1, D)), perb, perb, perb, act, act, rows(D),
                  _resident((N_CHIPS, D, fs)), _resident((N_CHIPS, D, fs)), _resident((N_CHIPS, fs, D))],
        out_specs=[rows(D), rows(D), act, rows(D), act, act, perb, perb, perb, pl.BlockSpec((1, D), lambda i: (0, 0))],
        out_shape=[sd((T, D), F32), sd((T, D), BF16), sd((N_CHIPS, T, fs), BF16), sd((T, D), BF16),
                   sd((N_CHIPS, T, fs), BF16), sd((N_CHIPS, T, fs), BF16), sd((B, 1, D), F32), sd((B, 1, D), F32),
                   sd((B, 1, D), F32), sd((1, D), F32)],
        compiler_params=_cparams(("arbitrary",)),
    )(dxo, x, nw, sh, sc, g, a, u, f, wg, wu, wd)


def _ffn_wgrad(h, s, df, da, du, name):
    T, D = h.shape
    fs = s.shape[2]
    tt = min(TOKEN_TILE, T)

    def body(h_ref, s_ref, df_ref, da_ref, du_ref, dgate_ref, dup_ref, ddown_ref):
        @pl.when(pl.program_id(1) == 0)
        def _():
            dgate_ref[...] = jnp.zeros_like(dgate_ref)
            dup_ref[...] = jnp.zeros_like(dup_ref)
            ddown_ref[...] = jnp.zeros_like(ddown_ref)

        hv = h_ref[...]
        dgate_ref[0] += _dot_tn(hv, da_ref[0])
        dup_ref[0] += _dot_tn(hv, du_ref[0])
        ddown_ref[0] += _dot_tn(s_ref[0], df_ref[...])

    rows = pl.BlockSpec((tt, D), lambda j, t: (t, 0))
    act = pl.BlockSpec((1, tt, fs), lambda j, t: (j, t, 0))
    wcol = pl.BlockSpec((1, D, fs), lambda j, t: (j, 0, 0))
    return pl.pallas_call(
        body, grid=(N_CHIPS, T // tt), name=name,
        in_specs=[rows, act, rows, act, act],
        out_specs=[wcol, wcol, pl.BlockSpec((1, fs, D), lambda j, t: (j, 0, 0))],
        out_shape=[jax.ShapeDtypeStruct((N_CHIPS, D, fs), F32), jax.ShapeDtypeStruct((N_CHIPS, D, fs), F32),
                   jax.ShapeDtypeStruct((N_CHIPS, fs, D), F32)],
        compiler_params=_cparams(("arbitrary", "arbitrary")),
    )(h, s, df, da, du)


def _mm_tn(xa, ya, tn, name):
    T, K = xa.shape
    N = ya.shape[1]
    tt = min(TOKEN_TILE, T)

    def body(x_ref, y_ref, o_ref):
        @pl.when(pl.program_id(1) == 0)
        def _():
            o_ref[...] = jnp.zeros_like(o_ref)

        o_ref[...] += _dot_tn(x_ref[...], y_ref[...])

    return pl.pallas_call(
        body, grid=(N // tn, T // tt), name=name,
        in_specs=[pl.BlockSpec((tt, K), lambda j, t: (t, 0)), pl.BlockSpec((tt, tn), lambda j, t: (t, j))],
        out_specs=pl.BlockSpec((K, tn), lambda j, t: (0, j)),
        out_shape=jax.ShapeDtypeStruct((K, N), F32),
        compiler_params=_cparams(("arbitrary", "arbitrary")),
    )(xa, ya)


def _final_loss(x, nw, tgt):
    T, D = x.shape
    tm = min(TOKEN_TILE, T)

    def body(x_ref, nw_ref, t_ref, dx_ref, loss_ref, dnw_ref):
        @pl.when(pl.program_id(0) == 0)
        def _():
            loss_ref[...] = jnp.zeros_like(loss_ref)
            dnw_ref[...] = jnp.zeros_like(dnw_ref)

        xv = x_ref[...]
        xh, r = _rms_stats(xv)
        nwv = nw_ref[...]
        err = xh * nwv - t_ref[...]
        loss_ref[...] += (0.5 / D) * jnp.sum(err * err)
        dx, dw_rows = _rms_bwd(err * (1.0 / D), xh, r, nwv)
        dx_ref[...] = dx
        dnw_ref[...] += _colsum(dw_rows)

    rows = pl.BlockSpec((tm, D), lambda i: (i, 0))
    return pl.pallas_call(
        body, grid=(T // tm,), name="final_loss",
        in_specs=[rows, _resident((1, D)), rows],
        out_specs=[rows, pl.BlockSpec((8, LANES), lambda i: (0, 0)), pl.BlockSpec((1, D), lambda i: (0, 0))],
        out_shape=[jax.ShapeDtypeStruct((T, D), F32), jax.ShapeDtypeStruct((8, LANES), F32),
                   jax.ShapeDtypeStruct((1, D), F32)],
        compiler_params=_cparams(("arbitrary",)),
    )(x, nw, tgt)


_PROJ_SPLITS = (0, 1024, 2560, 2944, 3200, 3328)


def _inproj_fwd(x, nw, sh, sc, win, seq):
    T, D = x.shape
    tm = min(TOKEN_TILE, seq)
    tps = seq // tm
    widths = [b - a for a, b in zip(_PROJ_SPLITS[:-1], _PROJ_SPLITS[1:])]
    dtypes = [BF16, BF16, F32, F32, F32]

    def body(x_ref, nw_ref, sh_ref, sc_ref, w_ref, *outs):
        xh, _ = _rms_stats(x_ref[...])
        h = (xh * nw_ref[...]) * (1.0 + sc_ref[0]) + sh_ref[0]
        proj = _dot(h.astype(BF16), w_ref[...])
        for o, lo, hi in zip(outs, _PROJ_SPLITS[:-1], _PROJ_SPLITS[1:]):
            o[...] = proj[:, lo:hi].astype(o.dtype)

    rows = lambda n: pl.BlockSpec((tm, n), lambda i: (i, 0))
    perb = pl.BlockSpec((1, 1, D), lambda i: (i // tps, 0, 0))
    return pl.pallas_call(
        body, grid=(T // tm,), name="inproj_fwd",
        in_specs=[rows(D), _resident((1, D)), perb, perb, _resident((D, D_PROJ))],
        out_specs=[rows(w) for w in widths],
        out_shape=[jax.ShapeDtypeStruct((T, w), dt) for w, dt in zip(widths, dtypes)],
        compiler_params=_cparams(("arbitrary",)),
    )(x, nw, sh, sc, win)


def _inproj_bwd(dx2, x, nw, sh, sc, win, dz, dxbc, dcq, dckv, ddtk_a, ddtk_b, seq):
    T, D = x.shape
    B = T // seq
    tm = min(TOKEN_TILE, seq)
    tps = seq // tm

    def body(dx2_ref, x_ref, nw_ref, sh_ref, sc_ref, w_ref, dz_ref, dxbc_ref, dcq_ref, dckv_ref, da_ref, db_ref,
             dx_ref, h_ref, dp_ref, dsh_ref, dsc_ref, dnw_ref):
        i = pl.program_id(0)

        @pl.when(i % tps == 0)
        def _():
            dsh_ref[...] = jnp.zeros_like(dsh_ref)
            dsc_ref[...] = jnp.zeros_like(dsc_ref)

        @pl.when(i == 0)
        def _():
            dnw_ref[...] = jnp.zeros_like(dnw_ref)

        dproj = jnp.concatenate(
            [dz_ref[...], dxbc_ref[...], dcq_ref[...].astype(BF16), dckv_ref[...].astype(BF16),
             (da_ref[...] + db_ref[...]).astype(BF16)], axis=1)
        dp_ref[...] = dproj
        dh = _dot_nt(dproj, w_ref[...])
        xh, r = _rms_stats(x_ref[...])
        nwv = nw_ref[...]
        n = xh * nwv
        scale1 = 1.0 + sc_ref[0]
        dsc_ref[0] += _colsum(dh * n)
        dsh_ref[0] += _colsum(dh)
        dx, dw_rows = _rms_bwd(dh * scale1, xh, r, nwv)
        dnw_ref[...] += _colsum(dw_rows)
        dx_ref[...] = dx2_ref[...] + dx
        h_ref[...] = (n * scale1 + sh_ref[0]).astype(BF16)

    rows = lambda n: pl.BlockSpec((tm, n), lambda i: (i, 0))
    perb = pl.BlockSpec((1, 1, D), lambda i: (i // tps, 0, 0))
    sd = jax.ShapeDtypeStruct
    return pl.pallas_call(
        body, grid=(T // tm,), name="inproj_bwd",
        in_specs=[rows(D), rows(D), _resident((1, D)), perb, perb, _resident((D, D_PROJ)),
                  rows(1024), rows(D_CONV), rows(Q_LORA), rows(KV_LORA), rows(LANES), rows(LANES)],
        out_specs=[rows(D), rows(D), rows(D_PROJ), perb, perb, pl.BlockSpec((1, D), lambda i: (0, 0))],
        out_shape=[sd((T, D), F32), sd((T, D), BF16), sd((T, D_PROJ), BF16), sd((B, 1, D), F32), sd((B, 1, D), F32),
                   sd((1, D), F32)],
        compiler_params=_cparams(("arbitrary",)),
    )(dx2, x, nw, sh, sc, win, dz, dxbc, dcq, dckv, ddtk_a, ddtk_b)


def _shift_down(v, k, row):
    return jnp.where(row < k, 0.0, pltpu.roll(v, k, 0))


def _shift_up(v, k, row, n):
    return jnp.where(row >= n - k, 0.0, pltpu.roll(v, n - k, 0))


def _conv_pre(xv, w_ref, b_ref, row):
    pre = b_ref[...] + w_ref[CONV_WIDTH - 1:CONV_WIDTH, :] * xv
    for k in range(1, CONV_WIDTH):
        pre = pre + w_ref[CONV_WIDTH - 1 - k:CONV_WIDTH - k, :] * _shift_down(xv, k, row)
    return pre


def _conv_fwd(xraw, cw, cb):
    B, S, C = xraw.shape

    def body(x_ref, w_ref, b_ref, o_ref):
        xv = x_ref[0].astype(F32)
        row = lax.broadcasted_iota(jnp.int32, xv.shape, 0)
        pre = _conv_pre(xv, w_ref, b_ref, row)
        o_ref[0] = (pre * _sigmoid(pre)).astype(BF16)

    blk = pl.BlockSpec((1, S, LANES), lambda b, j: (b, 0, j))
    return pl.pallas_call(
        body, grid=(B, C // LANES), name="conv_fwd",
        in_specs=[blk, pl.BlockSpec((CONV_WIDTH, LANES), lambda b, j: (0, j)), pl.BlockSpec((1, LANES), lambda b, j: (0, j))],
        out_specs=blk, out_shape=jax.ShapeDtypeStruct((B, S, C), BF16),
        compiler_params=_cparams(("arbitrary", "arbitrary")),
    )(xraw, cw, cb)


def _conv_bwd(dout, xraw, cw, cb):
    B, S, C = xraw.shape

    def body(d_ref, x_ref, w_ref, b_ref, dx_ref, dw_ref, db_ref):
        @pl.when(pl.program_id(1) == 0)
        def _():
            dw_ref[...] = jnp.zeros_like(dw_ref)
            db_ref[...] = jnp.zeros_like(db_ref)

        xv = x_ref[0].astype(F32)
        row = lax.broadcasted_iota(jnp.int32, xv.shape, 0)
        pre = _conv_pre(xv, w_ref, b_ref, row)
        sig = _sigmoid(pre)
        dpre = d_ref[0].astype(F32) * (sig * (1.0 + pre * (1.0 - sig)))
        dx = w_ref[CONV_WIDTH - 1:CONV_WIDTH, :] * dpre
        for k in range(1, CONV_WIDTH):
            dx = dx + w_ref[CONV_WIDTH - 1 - k:CONV_WIDTH - k, :] * _shift_up(dpre, k, row, S)
        dx_ref[0] = dx.astype(BF16)
        db_ref[...] += _colsum(dpre)
        dws = [_colsum(dpre * (xv if k == 0 else _shift_down(xv, k, row))) for k in range(CONV_WIDTH - 1, -1, -1)]
        dw_ref[...] += jnp.concatenate(dws, axis=0)

    blk = pl.BlockSpec((1, S, LANES), lambda j, b: (b, 0, j))
    wspec = pl.BlockSpec((CONV_WIDTH, LANES), lambda j, b: (0, j))
    bspec = pl.BlockSpec((1, LANES), lambda j, b: (0, j))
    return pl.pallas_call(
        body, grid=(C // LANES, B), name="conv_bwd",
        in_specs=[blk, blk, wspec, bspec], out_specs=[blk, wspec, bspec],
        out_shape=[jax.ShapeDtypeStruct((B, S, C), BF16), jax.ShapeDtypeStruct((CONV_WIDTH, C), F32),
                   jax.ShapeDtypeStruct((1, C), F32)],
        compiler_params=_cparams(("arbitrary", "arbitrary")),
    )(dout, xraw, cw, cb)


def _softplus(x):
    return jnp.maximum(x, 0.0) + jnp.log(1.0 + jnp.exp(-jnp.abs(x)))


def _ssd_common(xbc_ref, dtk_ref, dtb_ref, alog_ref, e_ref):
    L = CHUNK
    xbc = xbc_ref[0]
    xs = xbc[:, :D_SSD].astype(F32)
    bm = xbc[:, D_SSD:D_SSD + 256]
    cm = xbc[:, D_SSD + 256:D_SSD + 512]
    head = lax.broadcasted_iota(jnp.int32, (1, LANES), 1) < SSD_HEADS
    a128 = jnp.where(head, -jnp.exp(alog_ref[...]), 0.0)
    pre = dtk_ref[0] + dtb_ref[...]
    dt = _softplus(pre)
    dA = dt * a128
    row = lax.broadcasted_iota(jnp.int32, (L, L), 0)
    col = lax.broadcasted_iota(jnp.int32, (L, L), 1)
    causal = col <= row
    tri = causal.astype(F32)
    triT = (row <= col).astype(F32)
    tri = causal.astype(BF16)
    triT = (row <= col).astype(BF16)
    dA3 = _split3(dA)
    acum = _sum3(lambda part: _dot(tri, part), dA3)
    acumT = _sum3(lambda part: _dot_tn(part, triT), dA3)
    E = e_ref[...]
    acum_f = _spread(acum, E)
    dt_f = _spread(dt, E)
    e_f = jnp.exp(acum_f)
    w_f = jnp.exp(acum_f[L - 1:L, :] - acum_f)
    xt = xs * dt_f
    return dict(xs=xs, bm=bm, cm=cm, a128=a128, pre=pre, dt=dt, causal=causal, tri=tri, triT=triT, acum=acum,
                acumT=acumT, E=E, dt_f=dt_f, e_f=e_f, w_f=w_f, xt=xt, head=head)


def _split3(x):
    p1 = x.astype(BF16)
    r1 = x - p1.astype(F32)
    p2 = r1.astype(BF16)
    return p1, p2, (r1 - p2.astype(F32)).astype(BF16)


def _sum3(mm, parts):
    return (mm(parts[0]) + mm(parts[1])) + mm(parts[2])


def _spread(v, e):
    return _sum3(lambda part: _dot(part, e), _split3(v))


def _gather_heads(v, e):
    return _sum3(lambda part: _dot_nt(part, e), _split3(v))


def _head_mask(k):
    lane = lax.broadcasted_iota(jnp.int32, (CHUNK, LANES), 1)
    return (lane >= SSD_HEAD_DIM) if k == 1 else (lane < SSD_HEAD_DIM)


def _pair_decay(alast, h0):
    row = lax.broadcasted_iota(jnp.int32, (2 * SSD_HEAD_DIM, SSD_STATE), 0)
    return jnp.exp(jnp.where(row < SSD_HEAD_DIM, alast[:, h0:h0 + 1], alast[:, h0 + 1:h0 + 2]))


def _decay_matrix(q, h):
    seg = q["acum"][:, h:h + 1] - q["acumT"][h:h + 1, :]
    return jnp.exp(jnp.where(q["causal"], seg, -1e30))


def _gated_norm(y, zz, nw):
    sig = _sigmoid(zz)
    sil = zz * sig
    yg = y * sil
    half = D_SSD // SSD_GROUPS
    parts = []
    for g in range(SSD_GROUPS):
        xh, r = _rms_stats(yg[:, g * half:(g + 1) * half])
        parts.append((xh, r))
    return sig, sil, parts


def _ssd_fwd(xbc, dtk, z, dtb, alog, dsk, nw, expand):
    B, S, _ = xbc.shape
    L = CHUNK
    nc = S // L

    def body(xbc_ref, dtk_ref, z_ref, dtb_ref, alog_ref, dsk_ref, nw_ref, e_ref, y_ref, ys_ref, prev_ref, st_ref):
        @pl.when(pl.program_id(1) == 0)
        def _():
            st_ref[...] = jnp.zeros_like(st_ref)

        q = _ssd_common(xbc_ref, dtk_ref, dtb_ref, alog_ref, e_ref)
        xtb = q["xt"].astype(BF16)
        xwb = (q["xt"] * q["w_f"]).astype(BF16)
        alast = q["acum"][L - 1:L, :]
        ys = []
        for g in range(SSD_GROUPS):
            bg = q["bm"][:, g * 128:(g + 1) * 128]
            cg = q["cm"][:, g * 128:(g + 1) * 128]
            G = _dot_nt(cg, bg)
            for pr in range(SSD_HEADS // SSD_GROUPS // 2):
                h0 = g * 8 + 2 * pr
                lo = h0 * SSD_HEAD_DIM
                xt_p = xtb[:, lo:lo + 128]
                ydiag = jnp.zeros((L, LANES), F32)
                for k in range(2):
                    M = (G * _decay_matrix(q, h0 + k)).astype(BF16)
                    ydiag = ydiag + _dot(M, jnp.where(_head_mask(k), xt_p, jnp.zeros_like(xt_p)))
                hp = st_ref[lo:lo + 128, :]
                prev_ref[0, 0, lo:lo + 128, :] = hp
                zoff = _dot_nt(cg, hp.astype(BF16))
                ys.append(ydiag + zoff * q["e_f"][:, lo:lo + 128])
                st_ref[lo:lo + 128, :] = _pair_decay(alast, h0) * hp + _dot_tn(xwb[:, lo:lo + 128], bg)
        y = jnp.concatenate(ys, axis=1) + dsk_ref[...] * q["xs"]
        y_ref[0] = y.astype(BF16)
        _, _, parts = _gated_norm(y, z_ref[0].astype(F32), nw_ref[...])
        half = D_SSD // SSD_GROUPS
        ys_ref[0] = jnp.concatenate(
            [xh * nw_ref[:, g * half:(g + 1) * half] for g, (xh, _) in enumerate(parts)], axis=1).astype(BF16)

    chunk = lambda n: pl.BlockSpec((1, L, n), lambda b, c: (b, c, 0))
    vec = pl.BlockSpec((1, LANES), lambda b, c: (0, 0))
    return pl.pallas_call(
        body, grid=(B, nc), name="ssd_fwd",
        in_specs=[chunk(D_CONV), chunk(LANES), chunk(D_SSD), vec, vec, pl.BlockSpec((1, D_SSD), lambda b, c: (0, 0)),
                  pl.BlockSpec((1, D_SSD), lambda b, c: (0, 0)), pl.BlockSpec((LANES, D_SSD), lambda b, c: (0, 0))],
        out_specs=[chunk(D_SSD), chunk(D_SSD), pl.BlockSpec((1, 1, D_SSD, SSD_STATE), lambda b, c: (b, c, 0, 0))],
        out_shape=[jax.ShapeDtypeStruct((B, S, D_SSD), BF16), jax.ShapeDtypeStruct((B, S, D_SSD), BF16),
                   jax.ShapeDtypeStruct((B, nc, D_SSD, SSD_STATE), F32)],
        scratch_shapes=[pltpu.VMEM((D_SSD, SSD_STATE), F32)],
        compiler_params=_cparams(("arbitrary", "arbitrary")),
    )(xbc, dtk, z, dtb, alog, dsk, nw, expand)


def _ssd_bwd(xbc, dtk, z, y, prev, dys, dtb, alog, dsk, nw, expand):
    B, S, _ = xbc.shape
    L = CHUNK
    nc = S // L
    half = D_SSD // SSD_GROUPS

    def body(xbc_ref, dtk_ref, z_ref, y_ref, prev_ref, dys_ref, dtb_ref, alog_ref, dsk_ref, nw_ref, e_ref,
             dxbc_ref, ddtk_ref, dz_ref, dnw_ref, dvec_ref, dh_ref, dskc_ref):
        @pl.when((pl.program_id(0) == 0) & (pl.program_id(1) == 0))
        def _():
            dnw_ref[...] = jnp.zeros_like(dnw_ref)
            dvec_ref[...] = jnp.zeros_like(dvec_ref)
            dskc_ref[...] = jnp.zeros_like(dskc_ref)

        @pl.when(pl.program_id(1) == 0)
        def _():
            dh_ref[...] = jnp.zeros_like(dh_ref)

        q = _ssd_common(xbc_ref, dtk_ref, dtb_ref, alog_ref, e_ref)
        E = q["E"]
        xs = q["xs"]
        yv = y_ref[0].astype(F32)
        zz = z_ref[0].astype(F32)
        sig, sil, parts = _gated_norm(yv, zz, nw_ref[...])
        dn = dys_ref[0].astype(F32)
        dyg, dnw_rows = [], []
        for g, (xh, r) in enumerate(parts):
            dpart, dw_rows = _rms_bwd(dn[:, g * half:(g + 1) * half], xh, r, nw_ref[:, g * half:(g + 1) * half])
            dyg.append(dpart)
            dnw_rows.append(dw_rows)
        dyg = jnp.concatenate(dyg, axis=1)
        dnw_ref[...] += _colsum(jnp.concatenate(dnw_rows, axis=1))
        dY = dyg * sil
        dz_ref[0] = (dyg * yv * (sig * (1.0 + zz * (1.0 - sig)))).astype(BF16)
        dsk_f = dsk_ref[...]
        dskc_ref[...] += _colsum(dY * xs)
        dYb = dY.astype(BF16)
        xtb = q["xt"].astype(BF16)
        xwb = (q["xt"] * q["w_f"]).astype(BF16)
        acum = q["acum"]
        alast = acum[L - 1:L, :]
        lane_id = lax.broadcasted_iota(jnp.int32, (L, LANES), 1)
        sub_id = lax.broadcasted_iota(jnp.int32, (LANES, L), 0)
        lane_row = lax.broadcasted_iota(jnp.int32, (1, LANES), 1)
        da_rows = jnp.zeros((L, LANES), F32)
        daT = jnp.zeros((LANES, L), F32)
        dxt, prod_off, prod_st, dbs, dcs = [], [], [], [], []
        hsum_row = jnp.zeros((1, LANES), F32)
        for g in range(SSD_GROUPS):
            bg = q["bm"][:, g * 128:(g + 1) * 128]
            cg = q["cm"][:, g * 128:(g + 1) * 128]
            G = _dot_nt(cg, bg)
            dG = jnp.zeros((L, L), F32)
            dcg = jnp.zeros((L, SSD_STATE), F32)
            dbg = jnp.zeros((L, SSD_STATE), F32)
            for pr in range(SSD_HEADS // SSD_GROUPS // 2):
                h0 = g * 8 + 2 * pr
                lo = h0 * SSD_HEAD_DIM
                cols = slice(lo, lo + 128)
                dY_p = dYb[:, cols]
                xt_p = xtb[:, cols]
                dxt_p = jnp.zeros((L, LANES), F32)
                for k in range(2):
                    h = h0 + k
                    Lm = _decay_matrix(q, h)
                    Mf = G * Lm
                    dYk = jnp.where(_head_mask(k), dY_p, jnp.zeros_like(dY_p))
                    dM = _dot_nt(dYk, xt_p)
                    dxt_p = dxt_p + _dot_tn(Mf.astype(BF16), dYk)
                    dG = dG + dM * Lm
                    Q = dM * Mf
                    da_rows = da_rows + jnp.where(lane_id == h, jnp.sum(Q, axis=1, keepdims=True), 0.0)
                    daT = daT + jnp.where(sub_id == h, jnp.sum(Q, axis=0, keepdims=True), 0.0)
                hp = prev_ref[0, 0, lo:lo + 128, :]
                hpb = hp.astype(BF16)
                zoff = _dot_nt(cg, hpb)
                e_p = q["e_f"][:, cols]
                dY_pf = dY[:, cols]
                dZb = (dY_pf * e_p).astype(BF16)
                dcg = dcg + _dot(dZb, hpb)
                dhp_off = _dot_tn(dZb, cg)
                prod_off.append(dY_pf * zoff * e_p)
                dS = dh_ref[lo:lo + 128, :]
                dSb = dS.astype(BF16)
                U = _dot_nt(bg, dSb)
                dxt_p = dxt_p + U * q["w_f"][:, cols]
                dbg = dbg + _dot(xwb[:, cols], dSb)
                prod_st.append(q["xt"][:, cols] * U)
                dh_ref[lo:lo + 128, :] = _pair_decay(alast, h0) * dS + dhp_off
                dsh = dS * hp
                for k in range(2):
                    total = jnp.sum(dsh[k * SSD_HEAD_DIM:(k + 1) * SSD_HEAD_DIM, :], axis=(0, 1), keepdims=True)
                    hsum_row = hsum_row + jnp.where(lane_row == h0 + k, total, 0.0)
                dxt.append(dxt_p)
            dGb = dG.astype(BF16)
            dcs.append(dcg + _dot(dGb, bg))
            dbs.append(dbg + _dot_tn(dGb, cg))
        dxt = jnp.concatenate(dxt, axis=1)
        da_rows = da_rows + _gather_heads(jnp.concatenate(prod_off, axis=1), E)
        dww = _gather_heads(jnp.concatenate(prod_st, axis=1), E) * jnp.exp(alast - acum)
        da_rows = da_rows - dww
        dlast = _colsum(dww) + jnp.exp(alast) * hsum_row
        triT = q["triT"]
        ddA = (_sum3(lambda part: _dot(triT, part), _split3(da_rows))
               - _sum3(lambda part: _dot_nt(triT, part), _split3(daT)) + dlast)
        ddA = jnp.where(q["head"], ddA, 0.0)
        ddt = ddA * q["a128"] + _gather_heads(dxt * xs, E)
        ddt_raw = jnp.where(q["head"], ddt * _sigmoid(q["pre"]), 0.0)
        ddtk_ref[0] = ddt_raw
        dxs = dxt * q["dt_f"] + dsk_f * dY
        dxbc_ref[0] = jnp.concatenate([dxs] + dbs + dcs, axis=1).astype(BF16)
        dvec_ref[0:1, :] += _colsum(ddt_raw)
        dvec_ref[1:2, :] += _colsum(ddA * q["dt"]) * q["a128"]

        @pl.when((pl.program_id(0) == B - 1) & (pl.program_id(1) == nc - 1))
        def _():
            dvec_ref[2:3, :] = _gather_heads(jnp.broadcast_to(dskc_ref[...], (8, D_SSD)), E)[0:1, :]

    rev = lambda n: pl.BlockSpec((1, L, n), lambda b, c: (b, nc - 1 - c, 0))
    vec = pl.BlockSpec((1, LANES), lambda b, c: (0, 0))
    sd = jax.ShapeDtypeStruct
    return pl.pallas_call(
        body, grid=(B, nc), name="ssd_bwd",
        in_specs=[rev(D_CONV), rev(LANES), rev(D_SSD), rev(D_SSD),
                  pl.BlockSpec((1, 1, D_SSD, SSD_STATE), lambda b, c: (b, nc - 1 - c, 0, 0)), rev(D_SSD), vec, vec,
                  pl.BlockSpec((1, D_SSD), lambda b, c: (0, 0)),
                  pl.BlockSpec((1, D_SSD), lambda b, c: (0, 0)), pl.BlockSpec((LANES, D_SSD), lambda b, c: (0, 0))],
        out_specs=[rev(D_CONV), rev(LANES), rev(D_SSD), pl.BlockSpec((1, D_SSD), lambda b, c: (0, 0)),
                   pl.BlockSpec((8, LANES), lambda b, c: (0, 0))],
        out_shape=[sd((B, S, D_CONV), BF16), sd((B, S, LANES), F32), sd((B, S, D_SSD), BF16), sd((1, D_SSD), F32),
                   sd((8, LANES), F32)],
        scratch_shapes=[pltpu.VMEM((D_SSD, SSD_STATE), F32), pltpu.VMEM((1, D_SSD), F32)],
        compiler_params=_cparams(("arbitrary", "arbitrary")),
    )(xbc, dtk, z, y, prev, dys, dtb, alog, dsk, nw, expand)


def _rope_tables(pos_ref, invf_ref):
    ang = pos_ref[...].astype(F32) * invf_ref[...]
    return jnp.cos(ang), jnp.sin(ang)


def _rot(u):
    lane = lax.broadcasted_iota(jnp.int32, u.shape, 1)
    first = (lane >= QK_NOPE) & (lane < QK_NOPE + QK_ROPE // 2)
    second = (lane >= QK_NOPE + QK_ROPE // 2) & (lane < QK_DIM)
    return jnp.where(first, -pltpu.roll(u, LANES - QK_ROPE // 2, 1), jnp.where(second, pltpu.roll(u, QK_ROPE // 2, 1), 0.0))


def _rope_lanes(shape):
    lane = lax.broadcasted_iota(jnp.int32, shape, 1)
    return (lane >= QK_NOPE) & (lane < QK_DIM)


def _mla_prep(cq, ckv, dtk, pos, qw, kvw, wuq, wukv, invf):
    T = cq.shape[0]
    tm = min(TOKEN_TILE, T)
    scale = 1.0 / math.sqrt(QK_DIM)
    HW = MLA_HEADS * HEAD_LANES

    def body(cq_ref, ckv_ref, dtk_ref, pos_ref, qw_ref, kvw_ref, wuq_ref, wukv_ref, invf_ref, q_ref, k_ref, v_ref):
        xh, _ = _rms_stats(cq_ref[...])
        qv = _dot((xh * qw_ref[...]).astype(BF16), wuq_ref[...])
        xh, _ = _rms_stats(ckv_ref[...])
        kv = _dot((xh * kvw_ref[...]).astype(BF16), wukv_ref[...])
        cosf, sinf = _rope_tables(pos_ref, invf_ref)
        rope = lambda u: u * cosf + _rot(u) * sinf
        dtkv = dtk_ref[...]
        kr = rope(jnp.where(_rope_lanes(dtkv.shape), dtkv, 0.0))
        for h in range(MLA_HEADS):
            cols = slice(h * HEAD_LANES, (h + 1) * HEAD_LANES)
            q_ref[:, cols] = (rope(qv[:, cols]) * scale).astype(BF16)
            k_ref[:, cols] = (kv[:, cols] + kr).astype(BF16)
        v_ref[...] = kv[:, HW:].astype(BF16)

    rows = lambda n: pl.BlockSpec((tm, n), lambda i: (i, 0))
    return pl.pallas_call(
        body, grid=(T // tm,), name="mla_prep",
        in_specs=[rows(Q_LORA), rows(KV_LORA), rows(LANES), rows(1), _resident((1, Q_LORA)), _resident((1, KV_LORA)),
                  _resident((Q_LORA, HW)), _resident((KV_LORA, 2 * HW)), _resident((1, LANES))],
        out_specs=[rows(HW), rows(HW), rows(HW)],
        out_shape=[jax.ShapeDtypeStruct((T, HW), BF16)] * 3,
        compiler_params=_cparams(("arbitrary",)),
    )(cq, ckv, dtk, pos, qw, kvw, wuq, wukv, invf)


def _mla_prep_bwd(dq, dk, dv, cq, ckv, pos, qw, kvw, wuq, wukv, invf):
    T = cq.shape[0]
    tm = min(TOKEN_TILE, T)
    scale = 1.0 / math.sqrt(QK_DIM)
    HW = MLA_HEADS * HEAD_LANES

    def body(dq_ref, dk_ref, dv_ref, cq_ref, ckv_ref, pos_ref, qw_ref, kvw_ref, wuq_ref, wukv_ref, invf_ref,
             dcq_ref, dckv_ref, ddtk_ref, qn_ref, kvn_ref, dqo_ref, dkvo_ref, dqw_ref, dkvw_ref):
        @pl.when(pl.program_id(0) == 0)
        def _():
            dqw_ref[...] = jnp.zeros_like(dqw_ref)
            dkvw_ref[...] = jnp.zeros_like(dkvw_ref)

        cosf, sinf = _rope_tables(pos_ref, invf_ref)
        unrope = lambda d: d * cosf - _rot(d * sinf)
        dkr = jnp.zeros((tm, LANES), F32)
        nope = lax.broadcasted_iota(jnp.int32, (tm, LANES), 1) < QK_NOPE
        for h in range(MLA_HEADS):
            cols = slice(h * HEAD_LANES, (h + 1) * HEAD_LANES)
            dqo_ref[:, cols] = unrope(dq_ref[:, cols] * scale).astype(BF16)
            dkh = dk_ref[:, cols]
            dkr = dkr + jnp.where(_rope_lanes(dkh.shape), dkh, 0.0)
            dkvo_ref[:, cols] = jnp.where(nope, dkh, 0.0).astype(BF16)
        dkvo_ref[:, HW:] = dv_ref[...].astype(BF16)
        ddtk_ref[...] = unrope(dkr)
        xh, r = _rms_stats(cq_ref[...])
        qn_ref[...] = (xh * qw_ref[...]).astype(BF16)
        dx, dw_rows = _rms_bwd(_dot_nt(dqo_ref[...], wuq_ref[...]), xh, r, qw_ref[...])
        dcq_ref[...] = dx
        dqw_ref[...] += _colsum(dw_rows)
        xh, r = _rms_stats(ckv_ref[...])
        kvn_ref[...] = (xh * kvw_ref[...]).astype(BF16)
        dx, dw_rows = _rms_bwd(_dot_nt(dkvo_ref[...], wukv_ref[...]), xh, r, kvw_ref[...])
        dckv_ref[...] = dx
        dkvw_ref[...] += _colsum(dw_rows)

    rows = lambda n: pl.BlockSpec((tm, n), lambda i: (i, 0))
    sd = jax.ShapeDtypeStruct
    return pl.pallas_call(
        body, grid=(T // tm,), name="mla_prep_bwd",
        in_specs=[rows(HW), rows(HW), rows(HW), rows(Q_LORA), rows(KV_LORA), rows(1), _resident((1, Q_LORA)),
                  _resident((1, KV_LORA)), _resident((Q_LORA, HW)), _resident((KV_LORA, 2 * HW)), _resident((1, LANES))],
        out_specs=[rows(Q_LORA), rows(KV_LORA), rows(LANES), rows(Q_LORA), rows(KV_LORA), rows(HW), rows(2 * HW),
                   pl.BlockSpec((1, Q_LORA), lambda i: (0, 0)), pl.BlockSpec((1, KV_LORA), lambda i: (0, 0))],
        out_shape=[sd((T, Q_LORA), F32), sd((T, KV_LORA), F32), sd((T, LANES), F32), sd((T, Q_LORA), BF16),
                   sd((T, KV_LORA), BF16), sd((T, HW), BF16), sd((T, 2 * HW), BF16), sd((1, Q_LORA), F32),
                   sd((1, KV_LORA), F32)],
        compiler_params=_cparams(("arbitrary",)),
    )(dq, dk, dv, cq, ckv, pos, qw, kvw, wuq, wukv, invf)


def _causal_mask(t):
    row = lax.broadcasted_iota(jnp.int32, (t, t), 0)
    col = lax.broadcasted_iota(jnp.int32, (t, t), 1)
    return col <= row


def _attn_fwd(q, k, v):
    B, S, HW = q.shape
    H = HW // HEAD_LANES
    t = min(ATTN_FWD_Q_TILE, S)
    tk = min(ATTN_FWD_KV_TILE, t)
    nq = S // t
    per = t // tk

    pair = 2
    pw = pair * HEAD_LANES

    def body(q_ref, k_ref, v_ref, o_ref, lse_ref):
        qi = pl.program_id(2)
        lanes = [slice(hh * HEAD_LANES, (hh + 1) * HEAD_LANES) for hh in range(pair)]
        qs = [q_ref[0, :, cols] for cols in lanes]

        def step(j, carry, diag):
            sl = pl.ds(pl.multiple_of(j * tk, tk), tk)
            out = []
            for qv, cols, (m, l, acc) in zip(qs, lanes, carry):
                s = _dot_nt(qv, k_ref[0, sl, cols])
                if diag is not None:
                    row = lax.broadcasted_iota(jnp.int32, (t, tk), 0)
                    col = lax.broadcasted_iota(jnp.int32, (t, tk), 1)
                    s = jnp.where(col + diag * tk <= row, s, -1e30)
                m_new = jnp.maximum(m, jnp.max(s, axis=-1, keepdims=True))
                alpha = jnp.exp(m - m_new)
                p = jnp.exp(s - m_new)
                l = alpha * l + jnp.sum(p, axis=-1, keepdims=True)
                acc = alpha * acc + _dot(p.astype(BF16), v_ref[0, sl, cols])
                out.append((m_new, l, acc))
            return tuple(out)

        init = tuple((jnp.full((t, 1), -1e30, F32), jnp.zeros((t, 1), F32), jnp.zeros((t, HEAD_LANES), F32))
                     for _ in range(pair))
        carry = lax.fori_loop(0, qi * per, lambda j, c: step(j, c, None), init)
        for d in range(per):
            carry = step(qi * per + d, carry, d)
        for hh, (m, l, acc) in enumerate(carry):
            o_ref[0, :, lanes[hh]] = (acc / l).astype(BF16)
            lse_ref[0, hh] = m + jnp.log(l)

    return pl.pallas_call(
        body, grid=(B, H // pair, nq), name="attn_fwd",
        in_specs=[pl.BlockSpec((1, t, pw), lambda b, h, i: (b, i, h)),
                  pl.BlockSpec((1, S, pw), lambda b, h, i: (b, 0, h)),
                  pl.BlockSpec((1, S, pw), lambda b, h, i: (b, 0, h))],
        out_specs=[pl.BlockSpec((1, t, pw), lambda b, h, i: (b, i, h)),
                   pl.BlockSpec((1, pair, t, 1), lambda b, h, i: (b, h, i, 0))],
        out_shape=[jax.ShapeDtypeStruct((B, S, HW), BF16), jax.ShapeDtypeStruct((B, H, S, 1), F32)],
        compiler_params=_cparams(("arbitrary", "arbitrary", "arbitrary")),
    )(q, k, v)


def _attn_bwd(q, k, v, o, do, lse):
    B, S, HW = q.shape
    H = HW // HEAD_LANES
    t = min(ATTN_BWD_TILE, S)
    nq = S // t

    def body(q_ref, k_ref, v_ref, o_ref, do_ref, lse_ref, dq_ref, dk_ref, dv_ref):
        j = pl.program_id(2)

        @pl.when(j == 0)
        def _():
            dq_ref[...] = jnp.zeros_like(dq_ref)

        kj = k_ref[0]
        vj = v_ref[0]

        def step(i, carry, masked):
            dk, dv = carry
            sl = pl.ds(pl.multiple_of(i * t, t), t)
            qi = q_ref[0, sl, :]
            doi = do_ref[0, sl, :]
            s = _dot_nt(qi, kj)
            if masked:
                s = jnp.where(_causal_mask(t), s, -1e30)
            p = jnp.exp(s - lse_ref[0, 0, sl, :])
            dv = dv + _dot_tn(p.astype(BF16), doi)
            dp = _dot_nt(doi, vj)
            delta = jnp.sum(doi.astype(F32) * o_ref[0, sl, :].astype(F32), axis=-1, keepdims=True)
            dsb = (p * (dp - delta)).astype(BF16)
            dk = dk + _dot_tn(dsb, qi)
            dq_ref[0, sl, :] += _dot(dsb, kj)
            return dk, dv

        zero = jnp.zeros((t, HEAD_LANES), F32)
        carry = step(j, (zero, zero), True)
        dk, dv = lax.fori_loop(j + 1, nq, lambda i, c: step(i, c, False), carry)
        dk_ref[0] = dk
        dv_ref[0] = dv

    full = pl.BlockSpec((1, S, HEAD_LANES), lambda b, h, j: (b, 0, h))
    tile = pl.BlockSpec((1, t, HEAD_LANES), lambda b, h, j: (b, j, h))
    sd = jax.ShapeDtypeStruct
    return pl.pallas_call(
        body, grid=(B, H, nq), name="attn_bwd",
        in_specs=[full, tile, tile, full, full, pl.BlockSpec((1, 1, S, 1), lambda b, h, j: (b, h, 0, 0))],
        out_specs=[full, tile, tile],
        out_shape=[sd((B, S, HW), F32), sd((B, S, HW), F32), sd((B, S, HW), F32)],
        compiler_params=_cparams(("arbitrary", "arbitrary", "arbitrary")),
    )(q, k, v, o, do, lse)


def _mix_out(x1, yssd, o, mw, wout, g, seq):
    T, D = x1.shape
    tm = min(TOKEN_TILE, seq)
    tps = seq // tm

    def body(x_ref, ys_ref, o_ref, mw_ref, w_ref, g_ref, xo_ref, m_ref, yc_ref):
        xh, _ = _rms_stats(o_ref[...].astype(F32))
        ycat = jnp.concatenate([ys_ref[...], (xh * mw_ref[...]).astype(BF16)], axis=1)
        m = _dot(ycat, w_ref[...])
        xo_ref[...] = x_ref[...] + g_ref[0] * m
        m_ref[...] = m.astype(BF16)
        yc_ref[...] = ycat

    rows = lambda n: pl.BlockSpec((tm, n), lambda i: (i, 0))
    perb = pl.BlockSpec((1, 1, D), lambda i: (i // tps, 0, 0))
    sd = jax.ShapeDtypeStruct
    return pl.pallas_call(
        body, grid=(T // tm,), name="mix_out",
        in_specs=[rows(D), rows(D_SSD), rows(D_MLA), _resident((1, D_MLA)), _resident((D_SSD + D_MLA, D)), perb],
        out_specs=[rows(D), rows(D), rows(D_SSD + D_MLA)],
        out_shape=[sd((T, D), F32), sd((T, D), BF16), sd((T, D_SSD + D_MLA), BF16)],
        compiler_params=_cparams(("arbitrary",)),
    )(x1, yssd, o, mw, wout, g)


def _mix_out_bwd(dx2, m, o, mw, wout, g, seq):
    T, D = dx2.shape
    B = T // seq
    tm = min(TOKEN_TILE, seq)
    tps = seq // tm

    def body(dx_ref, m_ref, o_ref, mw_ref, w_ref, g_ref, dys_ref, do_ref, dm_ref, dg_ref, dmw_ref):
        i = pl.program_id(0)

        @pl.when(i % tps == 0)
        def _():
            dg_ref[...] = jnp.zeros_like(dg_ref)

        @pl.when(i == 0)
        def _():
            dmw_ref[...] = jnp.zeros_like(dmw_ref)

        dxv = dx_ref[...]
        dg_ref[0] += _colsum(dxv * m_ref[...].astype(F32))
        dmb = (g_ref[0] * dxv).astype(BF16)
        dm_ref[...] = dmb
        dycat = _dot_nt(dmb, w_ref[...])
        dys_ref[...] = dycat[:, :D_SSD].astype(BF16)
        xh, r = _rms_stats(o_ref[...].astype(F32))
        dx, dw_rows = _rms_bwd(dycat[:, D_SSD:], xh, r, mw_ref[...])
        do_ref[...] = dx.astype(BF16)
        dmw_ref[...] += _colsum(dw_rows)

    rows = lambda n: pl.BlockSpec((tm, n), lambda i: (i, 0))
    perb = pl.BlockSpec((1, 1, D), lambda i: (i // tps, 0, 0))
    sd = jax.ShapeDtypeStruct
    return pl.pallas_call(
        body, grid=(T // tm,), name="mix_out_bwd",
        in_specs=[rows(D), rows(D), rows(D_MLA), _resident((1, D_MLA)), _resident((D_SSD + D_MLA, D)), perb],
        out_specs=[rows(D_SSD), rows(D_MLA), rows(D), perb, pl.BlockSpec((1, D_MLA), lambda i: (0, 0))],
        out_shape=[sd((T, D_SSD), BF16), sd((T, D_MLA), BF16), sd((T, D), BF16), sd((B, 1, D), F32), sd((1, D_MLA), F32)],
        compiler_params=_cparams(("arbitrary",)),
    )(dx2, m, o, mw, wout, g)


def _win_to_kernel(w):
    z0 = jnp.zeros((w.shape[0], 48), w.dtype)
    z1 = jnp.zeros((w.shape[0], 32), w.dtype)
    return jnp.concatenate([w[:, :2560], w[:, 2576:3216], w[:, 2560:2576], z0, w[:, 3216:3248], z1], axis=1)


def _win_from_kernel(g):
    return jnp.concatenate([g[:, :2560], g[:, 3200:3216], g[:, 2560:3200], g[:, 3264:3296]], axis=1)


def _wuq_to_kernel(w):
    w = w.reshape(Q_LORA, MLA_HEADS, QK_DIM)
    return jnp.pad(w, ((0, 0), (0, 0), (0, HEAD_LANES - QK_DIM))).reshape(Q_LORA, MLA_HEADS * HEAD_LANES)


def _wuq_from_kernel(g):
    return g.reshape(Q_LORA, MLA_HEADS, HEAD_LANES)[:, :, :QK_DIM].reshape(Q_LORA, MLA_HEADS * QK_DIM)


def _wukv_to_kernel(w):
    w = w.reshape(KV_LORA, MLA_HEADS, QK_NOPE + V_HEAD)
    kp = jnp.pad(w[:, :, :QK_NOPE], ((0, 0), (0, 0), (0, HEAD_LANES - QK_NOPE)))
    return jnp.concatenate([kp.reshape(KV_LORA, -1), w[:, :, QK_NOPE:].reshape(KV_LORA, -1)], axis=1)


def _wukv_from_kernel(g):
    hw = MLA_HEADS * HEAD_LANES
    kp = g[:, :hw].reshape(KV_LORA, MLA_HEADS, HEAD_LANES)[:, :, :QK_NOPE]
    vp = g[:, hw:].reshape(KV_LORA, MLA_HEADS, V_HEAD)
    return jnp.concatenate([kp, vp], axis=2).reshape(KV_LORA, MLA_HEADS * (QK_NOPE + V_HEAD))


def _lanes16(v):
    return jnp.pad(v.reshape(1, SSD_HEADS), ((0, 0), (0, LANES - SSD_HEADS)))


def _constants():
    e = np.zeros((LANES, D_SSD), np.float32)
    for h in range(SSD_HEADS):
        e[h, h * SSD_HEAD_DIM:(h + 1) * SSD_HEAD_DIM] = 1.0
    inv_freq = ROPE_THETA ** (-jnp.arange(0, QK_ROPE, 2, dtype=F32) / QK_ROPE)
    half = QK_ROPE // 2
    invf = jnp.zeros((1, LANES), F32).at[0, QK_NOPE:QK_NOPE + half].set(inv_freq).at[0, QK_NOPE + half:QK_DIM].set(inv_freq)
    return jnp.asarray(e, BF16), invf


def _local_step(x, positions, mod, w, later_weights, small, tgt, on_grads):
    B, S, D = x.shape
    T = B * S
    expand, invf = _constants()
    x0 = x.reshape(T, D)
    pos = positions.reshape(T, 1)
    mods = [mod[:, i * D:(i + 1) * D].reshape(B, 1, D) for i in range(N_MOD)]
    sh1, sc1, g1, sh2, sc2, g2, sh3, sc3, g3 = mods
    dtb, alog = _lanes16(small["dt_bias"]), _lanes16(small["a_log"])
    dsk = jnp.repeat(small["d_skip"].reshape(1, SSD_HEADS), SSD_HEAD_DIM, axis=1)

    x1, a1, u1, f1 = _ffn_fwd(x0, small["norm_ffn1"], sh1, sc1, g1, w["ffn1_w_gate"], w["ffn1_w_up"], w["ffn1_w_down"], S, "ffn1_fwd")
    w = {**w, **later_weights(f1)}
    z, xraw, cq, ckv, dtk = _inproj_fwd(x1, small["norm_mix"], sh2, sc2, w["w_in"], S)
    xraw3 = xraw.reshape(B, S, D_CONV)
    xbc = _conv_fwd(xraw3, small["conv_w"], small["conv_b"])
    dtk3, z3 = dtk.reshape(B, S, LANES), z.reshape(B, S, D_SSD)
    y, yssd, prev = _ssd_fwd(xbc, dtk3, z3, dtb, alog, dsk, small["ssd_norm_w"], expand)
    q, k, v = _mla_prep(cq, ckv, dtk, pos, small["q_norm_w"], small["kv_norm_w"], w["w_uq"], w["w_ukv"], invf)
    hw = MLA_HEADS * HEAD_LANES
    q3, k3, v3 = q.reshape(B, S, hw), k.reshape(B, S, hw), v.reshape(B, S, hw)
    o3, lse = _attn_fwd(q3, k3, v3)
    o = o3.reshape(T, hw)
    x2, m, ycat = _mix_out(x1, yssd.reshape(T, D_SSD), o, small["mla_norm_w"], w["w_out"], g2, S)
    x3, a2, u2, f2 = _ffn_fwd(x2, small["norm_ffn2"], sh3, sc3, g3, w["ffn2_w_gate"], w["ffn2_w_up"], w["ffn2_w_down"], S, "ffn2_fwd")
    dx3, loss, d_norm_final = _final_loss(x3, small["norm_final"].reshape(1, D), tgt.reshape(T, D))

    gw, gs = {}, {}
    dx2, h3, s3, df3, da3, du3, dsh3, dsc3, dg3, gs["norm_ffn2"] = _ffn_bwd(
        dx3, x2, small["norm_ffn2"], sh3, sc3, g3, a2, u2, f2, w["ffn2_w_gate"], w["ffn2_w_up"], w["ffn2_w_down"], S, "ffn2_bwd")
    gw["ffn2_w_gate"], gw["ffn2_w_up"], gw["ffn2_w_down"] = _ffn_wgrad(h3, s3, df3, da3, du3, "ffn2_wgrad")
    g2 = g2 + on_grads(("ffn2_w_gate", "ffn2_w_up", "ffn2_w_down"), gw)

    dys, do, dm, dg2, gs["mla_norm_w"] = _mix_out_bwd(dx2, m, o, small["mla_norm_w"], w["w_out"], g2, S)
    gw["w_out"] = _mm_tn(ycat, dm, 512, "dwout")

    dq3, dk3, dv3 = _attn_bwd(q3, k3, v3, o3, do.reshape(B, S, hw), lse)
    dcq, dckv, ddtk_b, qn, kvn, dqb, dkvb, gs["q_norm_w"], gs["kv_norm_w"] = _mla_prep_bwd(
        dq3.reshape(T, hw), dk3.reshape(T, hw), dv3.reshape(T, hw), cq, ckv, pos, small["q_norm_w"], small["kv_norm_w"],
        w["w_uq"], w["w_ukv"], invf)
    gw["w_uq"] = _mm_tn(qn, dqb, 512, "dwuq")
    gw["w_ukv"] = _mm_tn(kvn, dkvb, 1024, "dwukv")

    dxbc, ddtk_a, dz, gs["ssd_norm_w"], dvec = _ssd_bwd(
        xbc, dtk3, z3, y, prev, dys.reshape(B, S, D_SSD), dtb, alog, dsk, small["ssd_norm_w"], expand)
    gs["dt_bias"], gs["a_log"], gs["d_skip"] = dvec[0:1, :SSD_HEADS], dvec[1:2, :SSD_HEADS], dvec[2:3, :SSD_HEADS]
    dxraw, gs["conv_w"], gs["conv_b"] = _conv_bwd(dxbc, xraw3, small["conv_w"], small["conv_b"])
    dx1, h2, dproj, dsh2, dsc2, gs["norm_mix"] = _inproj_bwd(
        dx2, x1, small["norm_mix"], sh2, sc2, w["w_in"], dz.reshape(T, D_SSD), dxraw.reshape(T, D_CONV), dcq, dckv,
        ddtk_a.reshape(T, LANES), ddtk_b, S)
    gw["w_in"] = _mm_tn(h2, dproj, 1664, "dwin")
    g1 = g1 + on_grads(("w_in", "w_uq", "w_ukv", "w_out"), gw)

    dx0, h1, s1, df1, da1, du1, dsh1, dsc1, dg1, gs["norm_ffn1"] = _ffn_bwd(
        dx1, x0, small["norm_ffn1"], sh1, sc1, g1, a1, u1, f1, w["ffn1_w_gate"], w["ffn1_w_up"], w["ffn1_w_down"], S, "ffn1_bwd")
    gw["ffn1_w_gate"], gw["ffn1_w_up"], gw["ffn1_w_down"] = _ffn_wgrad(h1, s1, df1, da1, du1, "ffn1_wgrad")
    gs["norm_final"] = d_norm_final
    dmod = jnp.concatenate([t.reshape(B, D) for t in (dsh1, dsc1, dg1, dsh2, dsc2, dg2, dsh3, dsc3, dg3)], axis=1)
    return loss, dx0.reshape(B, S, D), gw, dmod, gs


HBM_SPEC = pl.BlockSpec(memory_space=pltpu.HBM)
VMEM_SPEC = pl.BlockSpec(memory_space=pltpu.VMEM)


def _place():
    return lax.axis_index("x"), lax.axis_index("y"), lax.axis_index("c")


def _other_chips(mx, my):
    return [(1 - mx, my), (mx, 1 - my), (1 - mx, 1 - my)]


def _remote(src, dst, send_sem, recv_sem, to):
    return pltpu.make_async_remote_copy(src_ref=src, dst_ref=dst, send_sem=send_sem, recv_sem=recv_sem,
                                        device_id=to, device_id_type=MESH)


def _all_gather_small(xa, name):
    r, n = xa.shape

    def body(x_ref, o_ref, token, send_sems, recv_sems):
        mx, my, mc = _place()
        me = 4 * mx + 2 * my + mc
        token[...] = jnp.zeros_like(token)
        o_ref[pl.ds(me, 1)] = x_ref[...][None]
        sends = []
        for k in range(1, N_DEV):
            peer = (mx ^ (k >> 2), my ^ ((k >> 1) & 1), mc ^ (k & 1))
            cp = _remote(x_ref, o_ref.at[me], send_sems.at[k - 1], recv_sems.at[k - 1], peer)
            cp.start()
            sends.append(cp)
        for k in range(1, N_DEV):
            peer = (mx ^ (k >> 2), my ^ ((k >> 1) & 1), mc ^ (k & 1))
            slot = 4 * peer[0] + 2 * peer[1] + peer[2]
            _remote(x_ref, o_ref.at[slot], send_sems.at[k - 1], recv_sems.at[k - 1], peer).wait_recv()
        for cp in sends:
            cp.wait_send()

    return pl.pallas_call(
        body, name=name, in_specs=[VMEM_SPEC], out_specs=[VMEM_SPEC, VMEM_SPEC],
        out_shape=[jax.ShapeDtypeStruct((N_DEV, r, n), xa.dtype), jax.ShapeDtypeStruct((8, LANES), F32)],
        scratch_shapes=[pltpu.SemaphoreType.DMA((N_DEV - 1,)), pltpu.SemaphoreType.DMA((N_DEV - 1,))],
        compiler_params=pltpu.CompilerParams(vmem_limit_bytes=VMEM_LIMIT),
    )(xa)


def _half_rows(ref, hc, rh, lead=None):
    rows = pl.ds(pl.multiple_of(hc * rh, 8), rh)
    return ref.at[rows, :] if lead is None else ref.at[lead, rows, :]


def _gather_weights(shards):
    n = len(shards)

    def body(*refs):
        w_refs, o_refs, token = refs[:n], refs[n:2 * n], refs[2 * n]
        send_sems, recv_sems, stage_sems = refs[2 * n + 1:2 * n + 4]
        stages = refs[2 * n + 4:]
        mx, my, mc = _place()
        chip = 2 * mx + my
        others = _other_chips(mx, my)
        sibling = (mx, my, 1 - mc)
        token[...] = jnp.zeros_like(token)
        stage_in = [pltpu.make_async_copy(w, st, stage_sems.at[0, i]) for i, (w, st) in enumerate(zip(w_refs, stages))]
        for cp in stage_in:
            cp.start()
        first = []
        for i, (w, o) in enumerate(zip(w_refs, o_refs)):
            rh = w.shape[0] // 2
            for k, (cx, cy) in enumerate(others):
                first.append(_remote(_half_rows(w, mc, rh), _half_rows(o, mc, rh, chip), send_sems.at[i, k],
                                     recv_sems.at[i, k], (cx, cy, mc)))
                first[-1].start()
        stage_out = []
        for i, (st, o) in enumerate(zip(stages, o_refs)):
            stage_in[i].wait()
            stage_out.append(pltpu.make_async_copy(st, o.at[chip], stage_sems.at[1, i]))
            stage_out[-1].start()
        passed = []
        for i, (w, o) in enumerate(zip(w_refs, o_refs)):
            rh = w.shape[0] // 2
            for k, (cx, cy) in enumerate(others):
                landed = _half_rows(o, mc, rh, 2 * cx + cy)
                _remote(landed, landed, send_sems.at[i, k], recv_sems.at[i, k], (cx, cy, mc)).wait_recv()
                passed.append(_remote(landed, landed, send_sems.at[i, 3 + k], recv_sems.at[i, 3 + k], sibling))
                passed[-1].start()
        for i, (w, o) in enumerate(zip(w_refs, o_refs)):
            rh = w.shape[0] // 2
            for k, (cx, cy) in enumerate(others):
                there = _half_rows(o, 1 - mc, rh, 2 * cx + cy)
                _remote(there, there, send_sems.at[i, 3 + k], recv_sems.at[i, 3 + k], sibling).wait_recv()
        for cp in first + passed:
            cp.wait_send()
        for cp in stage_out:
            cp.wait()

    out = pl.pallas_call(
        body, name="gather_weights", in_specs=[HBM_SPEC] * n, out_specs=[HBM_SPEC] * n + [VMEM_SPEC],
        out_shape=[jax.ShapeDtypeStruct((N_CHIPS,) + s.shape, s.dtype) for s in shards] + [jax.ShapeDtypeStruct((8, LANES), F32)],
        scratch_shapes=[pltpu.SemaphoreType.DMA((n, 6)), pltpu.SemaphoreType.DMA((n, 6)), pltpu.SemaphoreType.DMA((2, n))]
        + [pltpu.VMEM(s.shape, s.dtype) for s in shards],
        compiler_params=pltpu.CompilerParams(vmem_limit_bytes=VMEM_LIMIT),
    )(*shards)
    return out[:n], out[n]


SEM_SPEC = pl.BlockSpec(memory_space=pltpu.SEMAPHORE)
ANY_SPEC = pl.BlockSpec(memory_space=pl.ANY)
DATAFLOW = pltpu.SideEffectType.DATAFLOW_SIDE_EFFECTING


def _hbm(arr):
    return pltpu.with_memory_space_constraint(arr, pltpu.HBM)


def _gather_start(shards):
    n = len(shards)

    def body(*refs):
        w_refs, land_refs, send_sems, recv_sems, token = refs[:n], refs[n:2 * n], refs[2 * n], refs[2 * n + 1], refs[-1]
        mx, my, mc = _place()
        chip = 2 * mx + my
        for i, (w, land) in enumerate(zip(w_refs, land_refs)):
            rh = w.shape[0] // 2
            for k, (cx, cy) in enumerate(_other_chips(mx, my)):
                _remote(_half_rows(w, mc, rh), _half_rows(land, mc, rh, chip), send_sems.at[3 * i + k],
                        recv_sems.at[3 * i + k], (cx, cy, mc)).start()
        token[...] = jnp.zeros_like(token)

    lands = [lax.empty((N_CHIPS,) + s.shape, s.dtype) for s in shards]
    out = pl.pallas_call(
        body, name="gather_start",
        out_shape=(pltpu.SemaphoreType.DMA((3 * n,)), pltpu.SemaphoreType.DMA((3 * n,)),
                   *[pltpu.HBM(s.shape, s.dtype) for s in shards], *[pltpu.HBM(l.shape, l.dtype) for l in lands],
                   jax.ShapeDtypeStruct((8, LANES), F32)),
        in_specs=[HBM_SPEC] * (2 * n), out_specs=(SEM_SPEC, SEM_SPEC, *[HBM_SPEC] * (2 * n), VMEM_SPEC),
        input_output_aliases={i: 2 + i for i in range(2 * n)},
        compiler_params=pltpu.CompilerParams(has_side_effects=DATAFLOW),
    )(*[_hbm(s) for s in shards], *[_hbm(l) for l in lands])
    return out[0], out[1], out[2:2 + n], out[2 + n:2 + 2 * n], out[-1]


def _gather_wait(send_sems, recv_sems, shards, lands, after):
    n = len(shards)

    def body(*refs):
        w_refs, land_refs, send_sems, recv_sems = refs[:n], refs[n:2 * n], refs[2 * n], refs[2 * n + 1]
        mx, my, mc = _place()
        for i, (w, land) in enumerate(zip(w_refs, land_refs)):
            rh = w.shape[0] // 2
            for k, (cx, cy) in enumerate(_other_chips(mx, my)):
                cp = _remote(_half_rows(w, mc, rh), _half_rows(land, mc, rh, 2 * cx + cy), send_sems.at[3 * i + k],
                             recv_sems.at[3 * i + k], (cx, cy, mc))
                cp.wait_send()
                cp.wait_recv()

    out = pl.pallas_call(
        body, name="gather_wait",
        out_shape=(*[pltpu.HBM(s.shape, s.dtype) for s in shards], *[pltpu.HBM(l.shape, l.dtype) for l in lands]),
        in_specs=[HBM_SPEC] * (2 * n) + [SEM_SPEC, SEM_SPEC, ANY_SPEC], out_specs=tuple([HBM_SPEC] * (2 * n)),
        input_output_aliases={i: i for i in range(2 * n)},
        compiler_params=pltpu.CompilerParams(has_side_effects=DATAFLOW),
    )(*shards, *lands, send_sems, recv_sems, after)
    return out[n:]


def _gather_finish(shards, lands):
    n = len(shards)

    def body(*refs):
        w_refs, land_refs, o_refs = refs[:n], refs[n:2 * n], refs[2 * n:3 * n]
        send_sems, recv_sems, stage_sems = refs[3 * n:3 * n + 3]
        stages = refs[3 * n + 3:]
        mx, my, mc = _place()
        chip = 2 * mx + my
        others = _other_chips(mx, my)
        sibling = (mx, my, 1 - mc)
        stage_in = [pltpu.make_async_copy(w, st, stage_sems.at[0, i]) for i, (w, st) in enumerate(zip(w_refs, stages))]
        for cp in stage_in:
            cp.start()
        passed = []
        for i, (w, o) in enumerate(zip(w_refs, o_refs)):
            rh = w.shape[0] // 2
            for k, (cx, cy) in enumerate(others):
                landed = _half_rows(o, mc, rh, 2 * cx + cy)
                passed.append(_remote(landed, landed, send_sems.at[i, k], recv_sems.at[i, k], sibling))
                passed[-1].start()
        stage_out = []
        for i, (st, o) in enumerate(zip(stages, o_refs)):
            stage_in[i].wait()
            stage_out.append(pltpu.make_async_copy(st, o.at[chip], stage_sems.at[1, i]))
            stage_out[-1].start()
        for i, (w, o) in enumerate(zip(w_refs, o_refs)):
            rh = w.shape[0] // 2
            for k, (cx, cy) in enumerate(others):
                there = _half_rows(o, 1 - mc, rh, 2 * cx + cy)
                _remote(there, there, send_sems.at[i, k], recv_sems.at[i, k], sibling).wait_recv()
        for cp in passed:
            cp.wait_send()
        for cp in stage_out:
            cp.wait()

    return pl.pallas_call(
        body, name="gather_finish", in_specs=[HBM_SPEC] * (2 * n), out_specs=[HBM_SPEC] * n,
        out_shape=[jax.ShapeDtypeStruct(l.shape, l.dtype) for l in lands],
        input_output_aliases={n + i: i for i in range(n)},
        scratch_shapes=[pltpu.SemaphoreType.DMA((n, 3)), pltpu.SemaphoreType.DMA((n, 3)), pltpu.SemaphoreType.DMA((2, n))]
        + [pltpu.VMEM(s.shape, s.dtype) for s in shards],
        compiler_params=pltpu.CompilerParams(vmem_limit_bytes=VMEM_LIMIT),
    )(*shards, *lands)


def _scatter_start(ss, tag):
    n = len(ss)

    def body(*refs):
        s_refs, land_refs, send_sems, recv_sems, token = refs[:n], refs[n:2 * n], refs[2 * n], refs[2 * n + 1], refs[-1]
        mx, my, mc = _place()
        chip = 2 * mx + my
        for i, (s, land) in enumerate(zip(s_refs, land_refs)):
            for k, (cx, cy) in enumerate(_other_chips(mx, my)):
                _remote(s.at[2 * cx + cy], land.at[chip], send_sems.at[3 * i + k], recv_sems.at[3 * i + k],
                        (cx, cy, mc)).start()
        token[...] = jnp.zeros_like(token)

    lands = [lax.empty(s.shape, s.dtype) for s in ss]
    out = pl.pallas_call(
        body, name="scatter_start_" + tag,
        out_shape=(pltpu.SemaphoreType.DMA((3 * n,)), pltpu.SemaphoreType.DMA((3 * n,)),
                   *[pltpu.HBM(s.shape, s.dtype) for s in ss], *[pltpu.HBM(l.shape, l.dtype) for l in lands],
                   jax.ShapeDtypeStruct((8, LANES), F32)),
        in_specs=[HBM_SPEC] * (2 * n), out_specs=(SEM_SPEC, SEM_SPEC, *[HBM_SPEC] * (2 * n), VMEM_SPEC),
        input_output_aliases={i: 2 + i for i in range(2 * n)},
        compiler_params=pltpu.CompilerParams(has_side_effects=DATAFLOW),
    )(*[_hbm(s) for s in ss], *[_hbm(l) for l in lands])
    return out[0], out[1], out[2:2 + n], out[2 + n:2 + 2 * n], out[-1]


def _scatter_wait(send_sems, recv_sems, ss, lands, after, tag):
    n = len(ss)

    def body(*refs):
        s_refs, land_refs, send_sems, recv_sems = refs[:n], refs[n:2 * n], refs[2 * n], refs[2 * n + 1]
        mx, my, mc = _place()
        for i, (s, land) in enumerate(zip(s_refs, land_refs)):
            for k, (cx, cy) in enumerate(_other_chips(mx, my)):
                slot = land.at[2 * cx + cy]
                cp = _remote(s.at[2 * cx + cy], slot, send_sems.at[3 * i + k], recv_sems.at[3 * i + k], (cx, cy, mc))
                cp.wait_send()
                cp.wait_recv()

    out = pl.pallas_call(
        body, name="scatter_wait_" + tag,
        out_shape=(*[pltpu.HBM(s.shape, s.dtype) for s in ss], *[pltpu.HBM(l.shape, l.dtype) for l in lands]),
        in_specs=[HBM_SPEC] * (2 * n) + [SEM_SPEC, SEM_SPEC, ANY_SPEC], out_specs=tuple([HBM_SPEC] * (2 * n)),
        input_output_aliases={i: i for i in range(2 * n)},
        compiler_params=pltpu.CompilerParams(has_side_effects=DATAFLOW),
    )(*ss, *lands, send_sems, recv_sems, after)
    return out[:n], out[n:]


def _swap_halves(gs, after, name):
    n = len(gs)

    def body(*refs):
        g_refs, o_refs, send_sems, recv_sems = refs[:n], refs[n + 1:2 * n + 1], refs[2 * n + 1], refs[2 * n + 2]
        mx, my, mc = _place()
        copies = []
        for i, (g, o) in enumerate(zip(g_refs, o_refs)):
            rh = g.shape[1] // 2
            src = g.at[:, pl.ds(pl.multiple_of((1 - mc) * rh, 8), rh), :]
            copies.append(_remote(src, o, send_sems.at[i], recv_sems.at[i], (mx, my, 1 - mc)))
            copies[-1].start()
        for cp in copies:
            cp.wait()

    return pl.pallas_call(
        body, name=name, in_specs=[HBM_SPEC] * n + [ANY_SPEC], out_specs=[HBM_SPEC] * n,
        out_shape=[jax.ShapeDtypeStruct((N_CHIPS, g.shape[1] // 2, g.shape[2]), g.dtype) for g in gs],
        scratch_shapes=[pltpu.SemaphoreType.DMA((n,)), pltpu.SemaphoreType.DMA((n,))],
    )(*gs, after)


def _pair_sum(g, got, core, name):
    _, r, c = g.shape
    rh = r // 2

    def body(core_ref, g_ref, got_ref, o_ref):
        o_ref[...] = (g_ref[...] + got_ref[...]).astype(BF16)

    return pl.pallas_call(
        body, name=name,
        grid_spec=pltpu.PrefetchScalarGridSpec(
            num_scalar_prefetch=1, grid=(N_CHIPS,),
            in_specs=[pl.BlockSpec((1, rh, c), lambda j, core_ref: (j, core_ref[0], 0)),
                      pl.BlockSpec((1, rh, c), lambda j, core_ref: (j, 0, 0))],
            out_specs=pl.BlockSpec((1, rh, c), lambda j, core_ref: (j, 0, 0))),
        out_shape=jax.ShapeDtypeStruct((N_CHIPS, rh, c), BF16),
        compiler_params=_cparams(("arbitrary",)),
    )(core, g, got)


def _chip_sum(own, got, chip, name):
    _, h, c = own.shape

    def body(chip_ref, a_ref, b_ref, c_ref, d_ref, o_ref):
        o_ref[...] = ((a_ref[0].astype(F32) + b_ref[0].astype(F32)) + c_ref[0].astype(F32)) + d_ref[0].astype(F32)

    slot = lambda flip: pl.BlockSpec((1, h, c), lambda i, chip_ref: (chip_ref[0] ^ flip, 0, 0))
    return pl.pallas_call(
        body, name=name,
        grid_spec=pltpu.PrefetchScalarGridSpec(
            num_scalar_prefetch=1, grid=(1,), in_specs=[slot(0), slot(1), slot(2), slot(3)],
            out_specs=pl.BlockSpec((h, c), lambda i, chip_ref: (0, 0))),
        out_shape=jax.ShapeDtypeStruct((h, c), F32),
        compiler_params=_cparams(("arbitrary",)),
    )(chip, own, got, got, got)


def _join_halves(mine, name):
    n = len(mine)

    def body(*refs):
        m_refs, o_refs, send_sems, recv_sems = refs[:n], refs[n:2 * n], refs[2 * n], refs[2 * n + 1]
        mx, my, mc = _place()
        copies = []
        for i, (m, o) in enumerate(zip(m_refs, o_refs)):
            copies.append(_remote(m, o, send_sems.at[i], recv_sems.at[i], (mx, my, 1 - mc)))
            copies[-1].start()
        for cp in copies:
            cp.wait()

    return pl.pallas_call(
        body, name=name, in_specs=[HBM_SPEC] * n, out_specs=[HBM_SPEC] * n,
        out_shape=[jax.ShapeDtypeStruct(m.shape, m.dtype) for m in mine],
        scratch_shapes=[pltpu.SemaphoreType.DMA((n,)), pltpu.SemaphoreType.DMA((n,))],
    )(*mine)


def _adam_math(w, g, m, v):
    m2 = ADAM_B1 * m + (1.0 - ADAM_B1) * g
    v2 = ADAM_B2 * v + (1.0 - ADAM_B2) * (g * g)
    m_hat = m2 * (1.0 / (1.0 - ADAM_B1 ** ADAM_STEP))
    v_hat = v2 * (1.0 / (1.0 - ADAM_B2 ** ADAM_STEP))
    delta = -ADAM_LR * (m_hat / (jnp.sqrt(v_hat) + ADAM_EPS) + ADAM_WD * w)
    return delta, m2, v2


def _adam(w, g, m, v, name):
    def body(w_ref, g_ref, m_ref, v_ref, d_ref, m2_ref, v2_ref):
        d_ref[...], m2_ref[...], v2_ref[...] = _adam_math(w_ref[...], g_ref[...], m_ref[...], v_ref[...])

    return pl.pallas_call(body, name=name, out_shape=[jax.ShapeDtypeStruct(w.shape, F32)] * 3)(w, g, m, v)


def _adam_halves(w, m, v, mine, theirs, core, name):
    _, r, c = w.shape
    rh = r // 2

    def body(core_ref, w_ref, m_ref, v_ref, mine_ref, theirs_ref, g_ref, d_ref, m2_ref, v2_ref):
        g = jnp.where(pl.program_id(0) == core_ref[0], mine_ref[...], theirs_ref[...])
        g_ref[0] = g
        d_ref[0], m2_ref[0], v2_ref[0] = _adam_math(w_ref[0], g, m_ref[0], v_ref[0])

    half = pl.BlockSpec((1, rh, c), lambda hc, core_ref: (0, hc, 0))
    whole = pl.BlockSpec((rh, c), lambda hc, core_ref: (0, 0))
    return pl.pallas_call(
        body, name=name,
        grid_spec=pltpu.PrefetchScalarGridSpec(
            num_scalar_prefetch=1, grid=(2,), in_specs=[half, half, half, whole, whole], out_specs=[half] * 4),
        out_shape=[jax.ShapeDtypeStruct(w.shape, F32)] * 4,
        compiler_params=_cparams(("arbitrary",)),
    )(core, w, m, v, mine, theirs)


ADA_COLS = N_MOD * D_MODEL // N_CHIPS


def _ada_fwd(c_all, w_ada, b_cols):
    def body(c_ref, w_ref, b_ref, o_ref):
        cv = c_ref[...]
        act = (cv * _sigmoid(cv)).astype(BF16)
        o_ref[...] = _dot(act, w_ref[...].astype(BF16)) + b_ref[...]

    return pl.pallas_call(
        body, name="ada_fwd", out_shape=jax.ShapeDtypeStruct((c_all.shape[0], ADA_COLS), F32),
        compiler_params=pltpu.CompilerParams(vmem_limit_bytes=VMEM_LIMIT),
    )(c_all, w_ada, b_cols)


def _ada_bwd(c_all, dmod_cols, w, m, v):
    nb = c_all.shape[0]
    tn = 384

    def body(c_ref, d_ref, w_ref, m_ref, v_ref, g_ref, dl_ref, m2_ref, v2_ref):
        cv = c_ref[...]
        act = (cv * _sigmoid(cv)).astype(BF16)
        g = _dot_tn(act, d_ref[...].astype(BF16))
        g_ref[...] = g
        dl_ref[...], m2_ref[...], v2_ref[...] = _adam_math(w_ref[...], g, m_ref[...], v_ref[...])

    blk = pl.BlockSpec((D_MODEL, tn), lambda j: (0, j))
    return pl.pallas_call(
        body, name="ada_bwd", grid=(ADA_COLS // tn,),
        in_specs=[pl.BlockSpec((nb, D_MODEL), lambda j: (0, 0)), pl.BlockSpec((nb, tn), lambda j: (0, j)), blk, blk, blk],
        out_specs=[blk] * 4, out_shape=[jax.ShapeDtypeStruct((D_MODEL, ADA_COLS), F32)] * 4,
        compiler_params=_cparams(("arbitrary",)),
    )(c_all, dmod_cols, w, m, v)


SMALL_NAMES = ("norm_ffn1", "norm_mix", "conv_w", "conv_b", "ssd_norm_w", "q_norm_w", "kv_norm_w", "mla_norm_w",
               "norm_ffn2", "norm_final", "dt_bias", "a_log", "d_skip")
SMALL_SIZES = (1024, 1024, CONV_WIDTH * D_CONV, D_CONV, 1024, Q_LORA, KV_LORA, 1024, 1024, 1024, 16, 16, 16)
SMALL_ROWS = 16
MOD_ROWS = 2 * N_MOD
SEND_ROWS = 40


def _pack_small(parts):
    flat = jnp.concatenate([parts[n].reshape(-1) for n in SMALL_NAMES])
    return jnp.pad(flat, (0, SMALL_ROWS * D_MODEL - flat.shape[0]))


def _unpack_small(flat):
    out, off = {}, 0
    for n, size in zip(SMALL_NAMES, SMALL_SIZES):
        out[n] = flat[off:off + size]
        off += size
    return out


def _small_sum(got):
    def body(g_ref, o_ref):
        bsum = jnp.zeros((N_MOD, D_MODEL), F32)
        ssum = jnp.zeros((SMALL_ROWS, D_MODEL), F32)
        for d in range(N_DEV):
            bsum = bsum + g_ref[d, 0:N_MOD, :] + g_ref[d, N_MOD:MOD_ROWS, :]
            ssum = ssum + g_ref[d, MOD_ROWS:MOD_ROWS + SMALL_ROWS, :]
        o_ref[...] = jnp.concatenate([bsum, ssum, jnp.zeros((32 - N_MOD - SMALL_ROWS, D_MODEL), F32)], axis=0)

    return pl.pallas_call(body, name="small_sum", out_shape=jax.ShapeDtypeStruct((32, D_MODEL), F32))(got)


BIG_NAMES = ("ffn1_w_gate", "ffn1_w_up", "ffn1_w_down", "w_in", "w_uq", "w_ukv", "w_out", "ffn2_w_gate", "ffn2_w_up",
             "ffn2_w_down")
_TO_KERNEL = {"w_in": _win_to_kernel, "w_uq": _wuq_to_kernel, "w_ukv": _wukv_to_kernel}
_FROM_KERNEL = {"w_in": _win_from_kernel, "w_uq": _wuq_from_kernel, "w_ukv": _wukv_from_kernel}


def _columns_joined(w4):
    n, r, c = w4.shape
    return w4.transpose(1, 0, 2).reshape(r, n * c)


def _columns_split(g):
    r, cols = g.shape
    return g.reshape(r, N_CHIPS, cols // N_CHIPS).transpose(1, 0, 2)


def kernel(x, c, positions, w_ada, b_ada, norm_ffn1, ffn1_w_gate, ffn1_w_up, ffn1_w_down, norm_mix, w_in, conv_w, conv_b, dt_bias, a_log, d_skip, ssd_norm_w, q_norm_w, w_uq, kv_norm_w, w_ukv, mla_norm_w, w_out, norm_ffn2, ffn2_w_gate, ffn2_w_up, ffn2_w_down, norm_final, loss_target, m_w_ada, m_b_ada, m_norm_ffn1, m_ffn1_w_gate, m_ffn1_w_up, m_ffn1_w_down, m_norm_mix, m_w_in, m_conv_w, m_conv_b, m_dt_bias, m_a_log, m_d_skip, m_ssd_norm_w, m_q_norm_w, m_w_uq, m_kv_norm_w, m_w_ukv, m_mla_norm_w, m_w_out, m_norm_ffn2, m_ffn2_w_gate, m_ffn2_w_up, m_ffn2_w_down, m_norm_final, v_w_ada, v_b_ada, v_norm_ffn1, v_ffn1_w_gate, v_ffn1_w_up, v_ffn1_w_down, v_norm_mix, v_w_in, v_conv_w, v_conv_b, v_dt_bias, v_a_log, v_d_skip, v_ssd_norm_w, v_q_norm_w, v_w_uq, v_kv_norm_w, v_w_ukv, v_mla_norm_w, v_w_out, v_norm_ffn2, v_ffn2_w_gate, v_ffn2_w_up, v_ffn2_w_down, v_norm_final):
    a = dict(locals())
    B, S, D = x.shape
    mx, my, mc = _place()
    chip = 2 * mx + my
    dev = 2 * chip + mc
    core = mc.astype(jnp.int32).reshape(1)
    chip_id = chip.astype(jnp.int32).reshape(1)

    cw_rows = jnp.pad(conv_w[0], ((0, 0), (0, D - conv_w.shape[2])))
    got, _ = _all_gather_small(jnp.concatenate([c, cw_rows, jnp.zeros((8 - B - CONV_WIDTH, D), F32)], axis=0), "gather_c")
    c_all = got[:, :B, :].reshape(N_DEV * B, D)
    conv_full = got[::2, B:B + CONV_WIDTH, :conv_w.shape[2]].transpose(1, 0, 2).reshape(CONV_WIDTH, D_CONV)

    b_cols = lax.dynamic_slice(b_ada, (0, chip * ADA_COLS), (1, ADA_COLS))
    mod_all, mod_done = _all_gather_small(_ada_fwd(c_all, w_ada[0], b_cols), "gather_mod")
    mod = lax.dynamic_slice(mod_all, (0, B * dev, 0), (N_DEV, B, ADA_COLS))[::2].transpose(1, 0, 2).reshape(B, N_MOD * D)

    first = ("ffn1_w_gate", "ffn1_w_up", "ffn1_w_down")
    later = tuple(n for n in BIG_NAMES if n not in first)
    got_first, gathered = _gather_weights([(a[n][0] + mod_done[0, 0]).astype(BF16) for n in first])
    w = dict(zip(first, got_first))
    in_flight = _gather_start([(a[n][0] + gathered[0, 0]).astype(BF16) for n in later])

    def later_weights(after):
        send_sems, recv_sems, shards, lands, _ = in_flight
        lands = _gather_wait(send_sems, recv_sems, shards, lands, after)
        wl = dict(zip(later, _gather_finish([a[n][0].astype(BF16) for n in later], lands)))
        for n, to_kernel in _TO_KERNEL.items():
            wl[n] = to_kernel(_columns_joined(wl[n]))
        wl["w_out"] = wl["w_out"].reshape(D_SSD + D_MLA, D)
        return wl

    small = {n: a[n].reshape(1, -1) for n in SMALL_NAMES if n not in ("conv_w", "norm_final")}
    small["conv_w"], small["norm_final"] = conv_full, norm_final

    def scatter_group(names, gw, after):
        g4 = []
        for n in names:
            g = gw[n]
            if n in _FROM_KERNEL:
                g = _columns_split(_FROM_KERNEL[n](g))
            g4.append(g.reshape(N_CHIPS, a[n].shape[1], a[n].shape[2]))
        swapped = _swap_halves(g4, g4[0] if after is None else after, "swap_" + names[0])
        pair = [_pair_sum(g, got, core, "pair_sum_" + n) for n, g, got in zip(names, g4, swapped)]
        return (names,) + tuple(_scatter_start(pair, names[0]))

    grads, deltas, new_m, new_v = {}, {}, {}, {}

    def finish_group(group, after):
        names, send_sems, recv_sems, pair, lands, _ = group
        pair, lands = _scatter_wait(send_sems, recv_sems, pair, lands, after, names[0])
        mine = [_chip_sum(own, got, chip_id, "chip_sum_" + n) for n, own, got in zip(names, pair, lands)]
        for n, own, other in zip(names, mine, _join_halves(mine, "join_" + names[0])):
            grads[n], deltas[n], new_m[n], new_v[n] = _adam_halves(a[n], a["m_" + n], a["v_" + n], own, other, core, "adam_" + n)
        return deltas[names[-1]]

    groups = []

    def on_grads(names, gw):
        groups.append(scatter_group(names, gw, None))
        return groups[-1][5][0, 0]

    loss_blk, grad_x, gw, dmod, gs = _local_step(x, positions, mod + in_flight[4][0, 0], w, later_weights, small, loss_target,
                                                 on_grads)

    small_flat = _pack_small(gs).at[-1].set(loss_blk[0, 0])
    send = jnp.concatenate([dmod.reshape(MOD_ROWS, D), small_flat.reshape(SMALL_ROWS, D),
                            jnp.zeros((SEND_ROWS - MOD_ROWS - SMALL_ROWS, D), F32)], axis=0)
    got, _ = _all_gather_small(send, "gather_small")
    summed = _small_sum(got)
    sums = summed[N_MOD:N_MOD + SMALL_ROWS].reshape(-1)
    loss = sums[-1]
    gsmall = _unpack_small(sums)
    gsmall["conv_w"] = lax.dynamic_slice(gsmall["conv_w"].reshape(CONV_WIDTH, D_CONV), (0, chip * conv_w.shape[2]),
                                         (CONV_WIDTH, conv_w.shape[2]))
    gsmall["b_ada"] = summed[:N_MOD]
    names = ("b_ada",) + SMALL_NAMES
    rows = 208

    def pack(parts):
        flat = jnp.concatenate([parts[n].reshape(-1) for n in names])
        return jnp.pad(flat, (0, rows * LANES - flat.shape[0])).reshape(rows, LANES)

    packed = [pack({n: a[p + n] for n in names}) for p in ("", "m_", "v_")]
    g_p = pack(gsmall)
    outs = (g_p,) + tuple(_adam(packed[0], g_p, packed[1], packed[2], "adam_small"))
    for dst, flat in zip((grads, deltas, new_m, new_v), outs):
        flat, off = flat.reshape(-1), 0
        for n in names:
            dst[n] = flat[off:off + a[n].size].reshape(a[n].shape)
            off += a[n].size

    dmod_all = got[:, :MOD_ROWS, :].reshape(N_DEV * B, N_MOD * D)
    dmod_cols = lax.dynamic_slice(dmod_all, (0, chip * ADA_COLS), (N_DEV * B, ADA_COLS))
    ada = _ada_bwd(c_all, dmod_cols, w_ada[0], m_w_ada[0], v_w_ada[0])
    for dst, t in zip((grads, deltas, new_m, new_v), ada):
        dst["w_ada"] = t[None]

    last = scatter_group(first, gw, summed)
    after = last[5]
    for group in groups:
        after = finish_group(group, after)
    finish_group(last, after)

    order = ("w_ada", "b_ada", "norm_ffn1", "ffn1_w_gate", "ffn1_w_up", "ffn1_w_down", "norm_mix", "w_in", "conv_w", "conv_b",
             "dt_bias", "a_log", "d_skip", "ssd_norm_w", "q_norm_w", "w_uq", "kv_norm_w", "w_ukv", "mla_norm_w", "w_out",
             "norm_ffn2", "ffn2_w_gate", "ffn2_w_up", "ffn2_w_down", "norm_final")
    return (loss, grad_x, *[grads[n] for n in order], *[deltas[n] for n in order], *[new_m[n] for n in order],
            *[new_v[n] for n in order])
```

```python
import functools
import math

import jax
import jax.numpy as jnp
import numpy as np
from jax import lax
from jax.experimental import pallas as pl
from jax.experimental.pallas import tpu as pltpu

F32 = jnp.float32
BF16 = jnp.bfloat16
HIGHEST = lax.Precision.HIGHEST

D_MODEL = 1024
D_FF = 2816
D_SSD = 1024
D_MLA = 1024
SSD_HEADS = 16
SSD_HEAD_DIM = 64
SSD_GROUPS = 2
SSD_STATE = 128
CONV_WIDTH = 4
CHUNK = 128
MLA_HEADS = 8
QK_NOPE = 64
QK_ROPE = 32
QK_DIM = QK_NOPE + QK_ROPE
V_HEAD = 128
Q_LORA = 384
KV_LORA = 256
ROPE_THETA = 10000.0
N_MOD = 9
EPS = 1e-6
D_CONV = D_SSD + 2 * SSD_GROUPS * SSD_STATE
D_PROJ = 3328
HEAD_LANES = 128
ADAM_LR = 0.001
ADAM_B1 = 0.9
ADAM_B2 = 0.999
ADAM_EPS = 1e-08
ADAM_WD = 0.01
ADAM_STEP = 10

LANES = 128
VMEM_LIMIT = 56 * 1024 * 1024
TOKEN_TILE = 512
ATTN_FWD_Q_TILE = 1024
ATTN_FWD_KV_TILE = 1024
ATTN_BWD_TILE = 1024
N_CHIPS = 4
N_DEV = 8

MESH = pl.DeviceIdType.MESH


def _dot(a, b, precision=None):
    return jnp.dot(a, b, preferred_element_type=F32, precision=precision)


def _dot_nt(a, b, precision=None):
    return lax.dot_general(a, b, (((1,), (1,)), ((), ())), preferred_element_type=F32, precision=precision)


def _dot_tn(a, b, precision=None):
    return lax.dot_general(a, b, (((0,), (0,)), ((), ())), preferred_element_type=F32, precision=precision)


def _cparams(semantics):
    return pltpu.CompilerParams(dimension_semantics=semantics, vmem_limit_bytes=VMEM_LIMIT)


def _resident(shape):
    zeros = (0,) * len(shape)
    return pl.BlockSpec(shape, lambda *_: zeros, pipeline_mode=pl.Buffered(1))


def _sigmoid(x):
    return jax.nn.sigmoid(x)


def _rms_stats(x):
    r = lax.rsqrt(jnp.mean(x * x, axis=-1, keepdims=True) + EPS)
    return x * r, r


def _rms_bwd(dn, xh, r, w):
    dxh = dn * w
    dx = r * (dxh - xh * jnp.mean(dxh * xh, axis=-1, keepdims=True))
    return dx, dn * xh


def _colsum(v):
    return jnp.sum(v, axis=0, keepdims=True)


def _ffn_fwd(x, nw, sh, sc, g, wg, wu, wd, seq, name):
    T, D = x.shape
    fs = wg.shape[1]
    tm = min(TOKEN_TILE, seq)
    tps = seq // tm

    def body(x_ref, nw_ref, sh_ref, sc_ref, g_ref, wg_ref, wu_ref, wd_ref, xo_ref, a_ref, u_ref, f_ref):
        xv = x_ref[...]
        xh, _ = _rms_stats(xv)
        h = (xh * nw_ref[...]) * (1.0 + sc_ref[0]) + sh_ref[0]
        hb = h.astype(BF16)
        f = jnp.zeros((tm, D), F32)
        for j in range(N_CHIPS):
            a = _dot_nt(hb, wg_ref[j])
            u = _dot_nt(hb, wu_ref[j])
            a_ref[j] = a.astype(BF16)
            u_ref[j] = u.astype(BF16)
            f = f + _dot((a * _sigmoid(a) * u).astype(BF16), wd_ref[j])
        xo_ref[...] = xv + 0.5 * g_ref[0] * f
        f_ref[...] = f.astype(BF16)

    rows = lambda n: pl.BlockSpec((tm, n), lambda i: (i, 0))
    act = pl.BlockSpec((N_CHIPS, tm, fs), lambda i: (0, i, 0))
    perb = pl.BlockSpec((1, 1, D), lambda i: (i // tps, 0, 0))
    return pl.pallas_call(
        body, grid=(T // tm,), name=name,
        in_specs=[rows(D), _resident((1, D)), perb, perb, perb, _resident((N_CHIPS, fs, D)), _resident((N_CHIPS, fs, D)),
                  _resident((N_CHIPS, fs, D))],
        out_specs=[rows(D), act, act, rows(D)],
        out_shape=[jax.ShapeDtypeStruct((T, D), F32), jax.ShapeDtypeStruct((N_CHIPS, T, fs), BF16),
                   jax.ShapeDtypeStruct((N_CHIPS, T, fs), BF16), jax.ShapeDtypeStruct((T, D), BF16)],
        compiler_params=_cparams(("arbitrary",)),
    )(x, nw, sh, sc, g, wg, wu, wd)


def _ffn_bwd(dxo, x, nw, sh, sc, g, a, u, f, wg, wu, wd, seq, name):
    T, D = x.shape
    fs = wg.shape[1]
    B = T // seq
    tm = min(TOKEN_TILE // 2, seq)
    tps = seq // tm

    def body(dxo_ref, x_ref, nw_ref, sh_ref, sc_ref, g_ref, a_ref, u_ref, f_ref, wg_ref, wu_ref, wd_ref,
             dx_ref, h_ref, s_ref, df_ref, da_ref, du_ref, dsh_ref, dsc_ref, dg_ref, dnw_ref):
        i = pl.program_id(0)

        @pl.when(i % tps == 0)
        def _():
            dsh_ref[...] = jnp.zeros_like(dsh_ref)
            dsc_ref[...] = jnp.zeros_like(dsc_ref)
            dg_ref[...] = jnp.zeros_like(dg_ref)

        @pl.when(i == 0)
        def _():
            dnw_ref[...] = jnp.zeros_like(dnw_ref)

        dxo_v = dxo_ref[...]
        dfb = (0.5 * g_ref[0] * dxo_v).astype(BF16)
        dg_ref[0] += _colsum(0.5 * dxo_v * f_ref[...].astype(F32))
        dh = jnp.zeros((tm, D), F32)
        for j in range(N_CHIPS):
            ds = _dot_nt(dfb, wd_ref[j])
            av = a_ref[j].astype(F32)
            uv = u_ref[j].astype(F32)
            sig = _sigmoid(av)
            sil = av * sig
            dab = (ds * uv * (sig * (1.0 + av * (1.0 - sig)))).astype(BF16)
            dub = (ds * sil).astype(BF16)
            dh = dh + _dot(dab, wg_ref[j]) + _dot(dub, wu_ref[j])
            s_ref[j] = (sil * uv).astype(BF16)
            da_ref[j] = dab
            du_ref[j] = dub
        xv = x_ref[...]
        xh, r = _rms_stats(xv)
        nwv = nw_ref[...]
        n = xh * nwv
        scale1 = 1.0 + sc_ref[0]
        dsc_ref[0] += _colsum(dh * n)
        dsh_ref[0] += _colsum(dh)
        dx, dw_rows = _rms_bwd(dh * scale1, xh, r, nwv)
        dnw_ref[...] += _colsum(dw_rows)
        dx_ref[...] = dxo_v + dx
        h_ref[...] = (n * scale1 + sh_ref[0]).astype(BF16)
        df_ref[...] = dfb

    rows = lambda n: pl.BlockSpec((tm, n), lambda i: (i, 0))
    act = pl.BlockSpec((N_CHIPS, tm, fs), lambda i: (0, i, 0))
    perb = pl.BlockSpec((1, 1, D), lambda i: (i // tps, 0, 0))
    sd = jax.ShapeDtypeStruct
    return pl.pallas_call(
        body, grid=(T // tm,), name=name,
        in_specs=[rows(D), rows(D), _resident((1, D)), perb, perb, perb, act, act, rows(D),
                  _resident((N_CHIPS, fs, D)), _resident((N_CHIPS, fs, D)), _resident((N_CHIPS, fs, D))],
        out_specs=[rows(D), rows(D), act, rows(D), act, act, perb, perb, perb, pl.BlockSpec((1, D), lambda i: (0, 0))],
        out_shape=[sd((T, D), F32), sd((T, D), BF16), sd((N_CHIPS, T, fs), BF16), sd((T, D), BF16),
                   sd((N_CHIPS, T, fs), BF16), sd((N_CHIPS, T, fs), BF16), sd((B, 1, D), F32), sd((B, 1, D), F32),
                   sd((B, 1, D), F32), sd((1, D), F32)],
        compiler_params=_cparams(("arbitrary",)),
    )(dxo, x, nw, sh, sc, g, a, u, f, wg, wu, wd)


def _ffn_wgrad(h, s, df, da, du, name):
    T, D = h.shape
    fs = s.shape[2]
    tt = min(TOKEN_TILE, T)

    def body(h_ref, s_ref, df_ref, da_ref, du_ref, dgate_ref, dup_ref, ddown_ref):
        @pl.when(pl.program_id(1) == 0)
        def _():
            dgate_ref[...] = jnp.zeros_like(dgate_ref)
            dup_ref[...] = jnp.zeros_like(dup_ref)
            ddown_ref[...] = jnp.zeros_like(ddown_ref)

        hv = h_ref[...]
        dgate_ref[0] += _dot_tn(da_ref[0], hv)
        dup_ref[0] += _dot_tn(du_ref[0], hv)
        ddown_ref[0] += _dot_tn(s_ref[0], df_ref[...])

    rows = pl.BlockSpec((tt, D), lambda j, t: (t, 0))
    act = pl.BlockSpec((1, tt, fs), lambda j, t: (j, t, 0))
    shard = pl.BlockSpec((1, fs, D), lambda j, t: (j, 0, 0))
    return pl.pallas_call(
        body, grid=(N_CHIPS, T // tt), name=name,
        in_specs=[rows, act, rows, act, act],
        out_specs=[shard] * 3, out_shape=[jax.ShapeDtypeStruct((N_CHIPS, fs, D), F32)] * 3,
        compiler_params=_cparams(("arbitrary", "arbitrary")),
    )(h, s, df, da, du)


def _mm_tn(xa, ya, tn, name):
    T, K = xa.shape
    N = ya.shape[1]
    tt = min(TOKEN_TILE, T)

    def body(x_ref, y_ref, o_ref):
        @pl.when(pl.program_id(1) == 0)
        def _():
            o_ref[...] = jnp.zeros_like(o_ref)

        o_ref[...] += _dot_tn(x_ref[...], y_ref[...])

    return pl.pallas_call(
        body, grid=(N // tn, T // tt), name=name,
        in_specs=[pl.BlockSpec((tt, K), lambda j, t: (t, 0)), pl.BlockSpec((tt, tn), lambda j, t: (t, j))],
        out_specs=pl.BlockSpec((K, tn), lambda j, t: (0, j)),
        out_shape=jax.ShapeDtypeStruct((K, N), F32),
        compiler_params=_cparams(("arbitrary", "arbitrary")),
    )(xa, ya)


def _final_loss(x, nw, tgt):
    T, D = x.shape
    tm = min(TOKEN_TILE, T)

    def body(x_ref, nw_ref, t_ref, dx_ref, loss_ref, dnw_ref):
        @pl.when(pl.program_id(0) == 0)
        def _():
            loss_ref[...] = jnp.zeros_like(loss_ref)
            dnw_ref[...] = jnp.zeros_like(dnw_ref)

        xv = x_ref[...]
        xh, r = _rms_stats(xv)
        nwv = nw_ref[...]
        err = xh * nwv - t_ref[...]
        loss_ref[...] += (0.5 / D) * jnp.sum(err * err)
        dx, dw_rows = _rms_bwd(err * (1.0 / D), xh, r, nwv)
        dx_ref[...] = dx
        dnw_ref[...] += _colsum(dw_rows)

    rows = pl.BlockSpec((tm, D), lambda i: (i, 0))
    return pl.pallas_call(
        body, grid=(T // tm,), name="final_loss",
        in_specs=[rows, _resident((1, D)), rows],
        out_specs=[rows, pl.BlockSpec((8, LANES), lambda i: (0, 0)), pl.BlockSpec((1, D), lambda i: (0, 0))],
        out_shape=[jax.ShapeDtypeStruct((T, D), F32), jax.ShapeDtypeStruct((8, LANES), F32),
                   jax.ShapeDtypeStruct((1, D), F32)],
        compiler_params=_cparams(("arbitrary",)),
    )(x, nw, tgt)


_PROJ_SPLITS = (0, 1024, 2560, 2944, 3200, 3328)


def _inproj_fwd(x, nw, sh, sc, win, seq):
    T, D = x.shape
    tm = min(TOKEN_TILE, seq)
    tps = seq // tm
    widths = [b - a for a, b in zip(_PROJ_SPLITS[:-1], _PROJ_SPLITS[1:])]
    dtypes = [BF16, BF16, F32, F32, F32]

    def body(x_ref, nw_ref, sh_ref, sc_ref, w_ref, *outs):
        xh, _ = _rms_stats(x_ref[...])
        h = (xh * nw_ref[...]) * (1.0 + sc_ref[0]) + sh_ref[0]
        proj = _dot_nt(h.astype(BF16), w_ref[...])
        for o, lo, hi in zip(outs, _PROJ_SPLITS[:-1], _PROJ_SPLITS[1:]):
            o[...] = proj[:, lo:hi].astype(o.dtype)

    rows = lambda n: pl.BlockSpec((tm, n), lambda i: (i, 0))
    perb = pl.BlockSpec((1, 1, D), lambda i: (i // tps, 0, 0))
    return pl.pallas_call(
        body, grid=(T // tm,), name="inproj_fwd",
        in_specs=[rows(D), _resident((1, D)), perb, perb, _resident((D_PROJ, D))],
        out_specs=[rows(w) for w in widths],
        out_shape=[jax.ShapeDtypeStruct((T, w), dt) for w, dt in zip(widths, dtypes)],
        compiler_params=_cparams(("arbitrary",)),
    )(x, nw, sh, sc, win)


def _inproj_bwd(dx2, x, nw, sh, sc, win, dz, dxbc, dcq, dckv, ddtk_a, ddtk_b, seq):
    T, D = x.shape
    B = T // seq
    tm = min(TOKEN_TILE, seq)
    tps = seq // tm

    def body(dx2_ref, x_ref, nw_ref, sh_ref, sc_ref, w_ref, dz_ref, dxbc_ref, dcq_ref, dckv_ref, da_ref, db_ref,
             dx_ref, h_ref, dp_ref, dsh_ref, dsc_ref, dnw_ref):
        i = pl.program_id(0)

        @pl.when(i % tps == 0)
        def _():
            dsh_ref[...] = jnp.zeros_like(dsh_ref)
            dsc_ref[...] = jnp.zeros_like(dsc_ref)

        @pl.when(i == 0)
        def _():
            dnw_ref[...] = jnp.zeros_like(dnw_ref)

        dproj = jnp.concatenate(
            [dz_ref[...], dxbc_ref[...], dcq_ref[...].astype(BF16), dckv_ref[...].astype(BF16),
             (da_ref[...] + db_ref[...]).astype(BF16)], axis=1)
        dp_ref[...] = dproj
        dh = _dot(dproj, w_ref[...])
        xh, r = _rms_stats(x_ref[...])
        nwv = nw_ref[...]
        n = xh * nwv
        scale1 = 1.0 + sc_ref[0]
        dsc_ref[0] += _colsum(dh * n)
        dsh_ref[0] += _colsum(dh)
        dx, dw_rows = _rms_bwd(dh * scale1, xh, r, nwv)
        dnw_ref[...] += _colsum(dw_rows)
        dx_ref[...] = dx2_ref[...] + dx
        h_ref[...] = (n * scale1 + sh_ref[0]).astype(BF16)

    rows = lambda n: pl.BlockSpec((tm, n), lambda i: (i, 0))
    perb = pl.BlockSpec((1, 1, D), lambda i: (i // tps, 0, 0))
    sd = jax.ShapeDtypeStruct
    return pl.pallas_call(
        body, grid=(T // tm,), name="inproj_bwd",
        in_specs=[rows(D), rows(D), _resident((1, D)), perb, perb, _resident((D_PROJ, D)),
                  rows(1024), rows(D_CONV), rows(Q_LORA), rows(KV_LORA), rows(LANES), rows(LANES)],
        out_specs=[rows(D), rows(D), rows(D_PROJ), perb, perb, pl.BlockSpec((1, D), lambda i: (0, 0))],
        out_shape=[sd((T, D), F32), sd((T, D), BF16), sd((T, D_PROJ), BF16), sd((B, 1, D), F32), sd((B, 1, D), F32),
                   sd((1, D), F32)],
        compiler_params=_cparams(("arbitrary",)),
    )(dx2, x, nw, sh, sc, win, dz, dxbc, dcq, dckv, ddtk_a, ddtk_b)


def _shift_down(v, k, row):
    return jnp.where(row < k, 0.0, pltpu.roll(v, k, 0))


def _shift_up(v, k, row, n):
    return jnp.where(row >= n - k, 0.0, pltpu.roll(v, n - k, 0))


def _conv_pre(xv, w_ref, b_ref, row):
    pre = b_ref[...] + w_ref[CONV_WIDTH - 1:CONV_WIDTH, :] * xv
    for k in range(1, CONV_WIDTH):
        pre = pre + w_ref[CONV_WIDTH - 1 - k:CONV_WIDTH - k, :] * _shift_down(xv, k, row)
    return pre


def _conv_fwd(xraw, cw, cb):
    B, S, C = xraw.shape

    def body(x_ref, w_ref, b_ref, o_ref):
        xv = x_ref[0].astype(F32)
        row = lax.broadcasted_iota(jnp.int32, xv.shape, 0)
        pre = _conv_pre(xv, w_ref, b_ref, row)
        o_ref[0] = (pre * _sigmoid(pre)).astype(BF16)

    blk = pl.BlockSpec((1, S, LANES), lambda b, j: (b, 0, j))
    return pl.pallas_call(
        body, grid=(B, C // LANES), name="conv_fwd",
        in_specs=[blk, pl.BlockSpec((CONV_WIDTH, LANES), lambda b, j: (0, j)), pl.BlockSpec((1, LANES), lambda b, j: (0, j))],
        out_specs=blk, out_shape=jax.ShapeDtypeStruct((B, S, C), BF16),
        compiler_params=_cparams(("arbitrary", "arbitrary")),
    )(xraw, cw, cb)


def _conv_bwd(dout, xraw, cw, cb):
    B, S, C = xraw.shape

    def body(d_ref, x_ref, w_ref, b_ref, dx_ref, dw_ref, db_ref):
        @pl.when(pl.program_id(1) == 0)
        def _():
            dw_ref[...] = jnp.zeros_like(dw_ref)
            db_ref[...] = jnp.zeros_like(db_ref)

        xv = x_ref[0].astype(F32)
        row = lax.broadcasted_iota(jnp.int32, xv.shape, 0)
        pre = _conv_pre(xv, w_ref, b_ref, row)
        sig = _sigmoid(pre)
        dpre = d_ref[0].astype(F32) * (sig * (1.0 + pre * (1.0 - sig)))
        dx = w_ref[CONV_WIDTH - 1:CONV_WIDTH, :] * dpre
        for k in range(1, CONV_WIDTH):
            dx = dx + w_ref[CONV_WIDTH - 1 - k:CONV_WIDTH - k, :] * _shift_up(dpre, k, row, S)
        dx_ref[0] = dx.astype(BF16)
        db_ref[...] += _colsum(dpre)
        dws = [_colsum(dpre * (xv if k == 0 else _shift_down(xv, k, row))) for k in range(CONV_WIDTH - 1, -1, -1)]
        dw_ref[...] += jnp.concatenate(dws, axis=0)

    blk = pl.BlockSpec((1, S, LANES), lambda j, b: (b, 0, j))
    wspec = pl.BlockSpec((CONV_WIDTH, LANES), lambda j, b: (0, j))
    bspec = pl.BlockSpec((1, LANES), lambda j, b: (0, j))
    return pl.pallas_call(
        body, grid=(C // LANES, B), name="conv_bwd",
        in_specs=[blk, blk, wspec, bspec], out_specs=[blk, wspec, bspec],
        out_shape=[jax.ShapeDtypeStruct((B, S, C), BF16), jax.ShapeDtypeStruct((CONV_WIDTH, C), F32),
                   jax.ShapeDtypeStruct((1, C), F32)],
        compiler_params=_cparams(("arbitrary", "arbitrary")),
    )(dout, xraw, cw, cb)


def _softplus(x):
    return jnp.maximum(x, 0.0) + jnp.log(1.0 + jnp.exp(-jnp.abs(x)))


def _ssd_common(xbc_ref, dtk_ref, dtb_ref, alog_ref, e_ref):
    L = CHUNK
    xbc = xbc_ref[0]
    xs = xbc[:, :D_SSD].astype(F32)
    bm = xbc[:, D_SSD:D_SSD + 256]
    cm = xbc[:, D_SSD + 256:D_SSD + 512]
    head = lax.broadcasted_iota(jnp.int32, (1, LANES), 1) < SSD_HEADS
    a128 = jnp.where(head, -jnp.exp(alog_ref[...]), 0.0)
    pre = dtk_ref[0] + dtb_ref[...]
    dt = _softplus(pre)
    dA = dt * a128
    row = lax.broadcasted_iota(jnp.int32, (L, L), 0)
    col = lax.broadcasted_iota(jnp.int32, (L, L), 1)
    causal = col <= row
    tri = causal.astype(F32)
    triT = (row <= col).astype(F32)
    tri = causal.astype(BF16)
    triT = (row <= col).astype(BF16)
    dA3 = _split3(dA)
    acum = _sum3(lambda part: _dot(tri, part), dA3)
    acumT = _sum3(lambda part: _dot_tn(part, triT), dA3)
    E = e_ref[...]
    acum_f = _spread(acum, E)
    dt_f = _spread(dt, E)
    e_f = jnp.exp(acum_f)
    w_f = jnp.exp(acum_f[L - 1:L, :] - acum_f)
    xt = xs * dt_f
    return dict(xs=xs, bm=bm, cm=cm, a128=a128, pre=pre, dt=dt, causal=causal, tri=tri, triT=triT, acum=acum,
                acumT=acumT, E=E, dt_f=dt_f, e_f=e_f, w_f=w_f, xt=xt, head=head)


def _split3(x):
    p1 = x.astype(BF16)
    r1 = x - p1.astype(F32)
    p2 = r1.astype(BF16)
    return p1, p2, (r1 - p2.astype(F32)).astype(BF16)


def _sum3(mm, parts):
    return (mm(parts[0]) + mm(parts[1])) + mm(parts[2])


def _spread(v, e):
    return _sum3(lambda part: _dot(part, e), _split3(v))


def _gather_heads(v, e):
    return _sum3(lambda part: _dot_nt(part, e), _split3(v))


def _head_mask(k):
    lane = lax.broadcasted_iota(jnp.int32, (CHUNK, LANES), 1)
    return (lane >= SSD_HEAD_DIM) if k == 1 else (lane < SSD_HEAD_DIM)


def _pair_decay(alast, h0):
    row = lax.broadcasted_iota(jnp.int32, (2 * SSD_HEAD_DIM, SSD_STATE), 0)
    return jnp.exp(jnp.where(row < SSD_HEAD_DIM, alast[:, h0:h0 + 1], alast[:, h0 + 1:h0 + 2]))


def _decay_matrix(q, h):
    seg = q["acum"][:, h:h + 1] - q["acumT"][h:h + 1, :]
    return jnp.exp(jnp.where(q["causal"], seg, -1e30))


def _gated_norm(y, zz, nw):
    sig = _sigmoid(zz)
    sil = zz * sig
    yg = y * sil
    half = D_SSD // SSD_GROUPS
    parts = []
    for g in range(SSD_GROUPS):
        xh, r = _rms_stats(yg[:, g * half:(g + 1) * half])
        parts.append((xh, r))
    return sig, sil, parts


def _ssd_fwd(xbc, dtk, z, dtb, alog, dsk, nw, expand):
    B, S, _ = xbc.shape
    L = CHUNK
    nc = S // L

    def body(xbc_ref, dtk_ref, z_ref, dtb_ref, alog_ref, dsk_ref, nw_ref, e_ref, y_ref, ys_ref, prev_ref, st_ref):
        @pl.when(pl.program_id(1) == 0)
        def _():
            st_ref[...] = jnp.zeros_like(st_ref)

        q = _ssd_common(xbc_ref, dtk_ref, dtb_ref, alog_ref, e_ref)
        xtb = q["xt"].astype(BF16)
        xwb = (q["xt"] * q["w_f"]).astype(BF16)
        alast = q["acum"][L - 1:L, :]
        ys = []
        for g in range(SSD_GROUPS):
            bg = q["bm"][:, g * 128:(g + 1) * 128]
            cg = q["cm"][:, g * 128:(g + 1) * 128]
            G = _dot_nt(cg, bg)
            for pr in range(SSD_HEADS // SSD_GROUPS // 2):
                h0 = g * 8 + 2 * pr
                lo = h0 * SSD_HEAD_DIM
                xt_p = xtb[:, lo:lo + 128]
                ydiag = jnp.zeros((L, LANES), F32)
                for k in range(2):
                    M = (G * _decay_matrix(q, h0 + k)).astype(BF16)
                    ydiag = ydiag + _dot(M, jnp.where(_head_mask(k), xt_p, jnp.zeros_like(xt_p)))
                hp = st_ref[lo:lo + 128, :]
                prev_ref[0, 0, lo:lo + 128, :] = hp
                zoff = _dot_nt(cg, hp.astype(BF16))
                ys.append(ydiag + zoff * q["e_f"][:, lo:lo + 128])
                st_ref[lo:lo + 128, :] = _pair_decay(alast, h0) * hp + _dot_tn(xwb[:, lo:lo + 128], bg)
        y = jnp.concatenate(ys, axis=1) + dsk_ref[...] * q["xs"]
        y_ref[0] = y.astype(BF16)
        _, _, parts = _gated_norm(y, z_ref[0].astype(F32), nw_ref[...])
        half = D_SSD // SSD_GROUPS
        ys_ref[0] = jnp.concatenate(
            [xh * nw_ref[:, g * half:(g + 1) * half] for g, (xh, _) in enumerate(parts)], axis=1).astype(BF16)

    chunk = lambda n: pl.BlockSpec((1, L, n), lambda b, c: (b, c, 0))
    vec = pl.BlockSpec((1, LANES), lambda b, c: (0, 0))
    return pl.pallas_call(
        body, grid=(B, nc), name="ssd_fwd",
        in_specs=[chunk(D_CONV), chunk(LANES), chunk(D_SSD), vec, vec, pl.BlockSpec((1, D_SSD), lambda b, c: (0, 0)),
                  pl.BlockSpec((1, D_SSD), lambda b, c: (0, 0)), pl.BlockSpec((LANES, D_SSD), lambda b, c: (0, 0))],
        out_specs=[chunk(D_SSD), chunk(D_SSD), pl.BlockSpec((1, 1, D_SSD, SSD_STATE), lambda b, c: (b, c, 0, 0))],
        out_shape=[jax.ShapeDtypeStruct((B, S, D_SSD), BF16), jax.ShapeDtypeStruct((B, S, D_SSD), BF16),
                   jax.ShapeDtypeStruct((B, nc, D_SSD, SSD_STATE), F32)],
        scratch_shapes=[pltpu.VMEM((D_SSD, SSD_STATE), F32)],
        compiler_params=_cparams(("arbitrary", "arbitrary")),
    )(xbc, dtk, z, dtb, alog, dsk, nw, expand)


def _ssd_bwd(xbc, dtk, z, y, prev, dys, dtb, alog, dsk, nw, expand):
    B, S, _ = xbc.shape
    L = CHUNK
    nc = S // L
    half = D_SSD // SSD_GROUPS

    def body(xbc_ref, dtk_ref, z_ref, y_ref, prev_ref, dys_ref, dtb_ref, alog_ref, dsk_ref, nw_ref, e_ref,
             dxbc_ref, ddtk_ref, dz_ref, dnw_ref, dvec_ref, dh_ref, dskc_ref):
        @pl.when((pl.program_id(0) == 0) & (pl.program_id(1) == 0))
        def _():
            dnw_ref[...] = jnp.zeros_like(dnw_ref)
            dvec_ref[...] = jnp.zeros_like(dvec_ref)
            dskc_ref[...] = jnp.zeros_like(dskc_ref)

        @pl.when(pl.program_id(1) == 0)
        def _():
            dh_ref[...] = jnp.zeros_like(dh_ref)

        q = _ssd_common(xbc_ref, dtk_ref, dtb_ref, alog_ref, e_ref)
        E = q["E"]
        xs = q["xs"]
        yv = y_ref[0].astype(F32)
        zz = z_ref[0].astype(F32)
        sig, sil, parts = _gated_norm(yv, zz, nw_ref[...])
        dn = dys_ref[0].astype(F32)
        dyg, dnw_rows = [], []
        for g, (xh, r) in enumerate(parts):
            dpart, dw_rows = _rms_bwd(dn[:, g * half:(g + 1) * half], xh, r, nw_ref[:, g * half:(g + 1) * half])
            dyg.append(dpart)
            dnw_rows.append(dw_rows)
        dyg = jnp.concatenate(dyg, axis=1)
        dnw_ref[...] += _colsum(jnp.concatenate(dnw_rows, axis=1))
        dY = dyg * sil
        dz_ref[0] = (dyg * yv * (sig * (1.0 + zz * (1.0 - sig)))).astype(BF16)
        dsk_f = dsk_ref[...]
        dskc_ref[...] += _colsum(dY * xs)
        dYb = dY.astype(BF16)
        xtb = q["xt"].astype(BF16)
        xwb = (q["xt"] * q["w_f"]).astype(BF16)
        acum = q["acum"]
        alast = acum[L - 1:L, :]
        lane_id = lax.broadcasted_iota(jnp.int32, (L, LANES), 1)
        sub_id = lax.broadcasted_iota(jnp.int32, (LANES, L), 0)
        lane_row = lax.broadcasted_iota(jnp.int32, (1, LANES), 1)
        da_rows = jnp.zeros((L, LANES), F32)
        daT = jnp.zeros((LANES, L), F32)
        dxt, prod_off, prod_st, dbs, dcs = [], [], [], [], []
        hsum_row = jnp.zeros((1, LANES), F32)
        for g in range(SSD_GROUPS):
            bg = q["bm"][:, g * 128:(g + 1) * 128]
            cg = q["cm"][:, g * 128:(g + 1) * 128]
            G = _dot_nt(cg, bg)
            dG = jnp.zeros((L, L), F32)
            dcg = jnp.zeros((L, SSD_STATE), F32)
            dbg = jnp.zeros((L, SSD_STATE), F32)
            for pr in range(SSD_HEADS // SSD_GROUPS // 2):
                h0 = g * 8 + 2 * pr
                lo = h0 * SSD_HEAD_DIM
                cols = slice(lo, lo + 128)
                dY_p = dYb[:, cols]
                xt_p = xtb[:, cols]
                dxt_p = jnp.zeros((L, LANES), F32)
                for k in range(2):
                    h = h0 + k
                    Lm = _decay_matrix(q, h)
                    Mf = G * Lm
                    dYk = jnp.where(_head_mask(k), dY_p, jnp.zeros_like(dY_p))
                    dM = _dot_nt(dYk, xt_p)
                    dxt_p = dxt_p + _dot_tn(Mf.astype(BF16), dYk)
                    dG = dG + dM * Lm
                    Q = dM * Mf
                    da_rows = da_rows + jnp.where(lane_id == h, jnp.sum(Q, axis=1, keepdims=True), 0.0)
                    daT = daT + jnp.where(sub_id == h, jnp.sum(Q, axis=0, keepdims=True), 0.0)
                hp = prev_ref[0, 0, lo:lo + 128, :]
                hpb = hp.astype(BF16)
                zoff = _dot_nt(cg, hpb)
                e_p = q["e_f"][:, cols]
                dY_pf = dY[:, cols]
                dZb = (dY_pf * e_p).astype(BF16)
                dcg = dcg + _dot(dZb, hpb)
                dhp_off = _dot_tn(dZb, cg)
                prod_off.append(dY_pf * zoff * e_p)
                dS = dh_ref[lo:lo + 128, :]
                dSb = dS.astype(BF16)
                U = _dot_nt(bg, dSb)
                dxt_p = dxt_p + U * q["w_f"][:, cols]
                dbg = dbg + _dot(xwb[:, cols], dSb)
                prod_st.append(q["xt"][:, cols] * U)
                dh_ref[lo:lo + 128, :] = _pair_decay(alast, h0) * dS + dhp_off
                dsh = dS * hp
                for k in range(2):
                    total = jnp.sum(dsh[k * SSD_HEAD_DIM:(k + 1) * SSD_HEAD_DIM, :], axis=(0, 1), keepdims=True)
                    hsum_row = hsum_row + jnp.where(lane_row == h0 + k, total, 0.0)
                dxt.append(dxt_p)
            dGb = dG.astype(BF16)
            dcs.append(dcg + _dot(dGb, bg))
            dbs.append(dbg + _dot_tn(dGb, cg))
        dxt = jnp.concatenate(dxt, axis=1)
        da_rows = da_rows + _gather_heads(jnp.concatenate(prod_off, axis=1), E)
        dww = _gather_heads(jnp.concatenate(prod_st, axis=1), E) * jnp.exp(alast - acum)
        da_rows = da_rows - dww
        dlast = _colsum(dww) + jnp.exp(alast) * hsum_row
        triT = q["triT"]
        ddA = (_sum3(lambda part: _dot(triT, part), _split3(da_rows))
               - _sum3(lambda part: _dot_nt(triT, part), _split3(daT)) + dlast)
        ddA = jnp.where(q["head"], ddA, 0.0)
        ddt = ddA * q["a128"] + _gather_heads(dxt * xs, E)
        ddt_raw = jnp.where(q["head"], ddt * _sigmoid(q["pre"]), 0.0)
        ddtk_ref[0] = ddt_raw
        dxs = dxt * q["dt_f"] + dsk_f * dY
        dxbc_ref[0] = jnp.concatenate([dxs] + dbs + dcs, axis=1).astype(BF16)
        dvec_ref[0:1, :] += _colsum(ddt_raw)
        dvec_ref[1:2, :] += _colsum(ddA * q["dt"]) * q["a128"]

        @pl.when((pl.program_id(0) == B - 1) & (pl.program_id(1) == nc - 1))
        def _():
            dvec_ref[2:3, :] = _gather_heads(jnp.broadcast_to(dskc_ref[...], (8, D_SSD)), E)[0:1, :]

    rev = lambda n: pl.BlockSpec((1, L, n), lambda b, c: (b, nc - 1 - c, 0))
    vec = pl.BlockSpec((1, LANES), lambda b, c: (0, 0))
    sd = jax.ShapeDtypeStruct
    return pl.pallas_call(
        body, grid=(B, nc), name="ssd_bwd",
        in_specs=[rev(D_CONV), rev(LANES), rev(D_SSD), rev(D_SSD),
                  pl.BlockSpec((1, 1, D_SSD, SSD_STATE), lambda b, c: (b, nc - 1 - c, 0, 0)), rev(D_SSD), vec, vec,
                  pl.BlockSpec((1, D_SSD), lambda b, c: (0, 0)),
                  pl.BlockSpec((1, D_SSD), lambda b, c: (0, 0)), pl.BlockSpec((LANES, D_SSD), lambda b, c: (0, 0))],
        out_specs=[rev(D_CONV), rev(LANES), rev(D_SSD), pl.BlockSpec((1, D_SSD), lambda b, c: (0, 0)),
                   pl.BlockSpec((8, LANES), lambda b, c: (0, 0))],
        out_shape=[sd((B, S, D_CONV), BF16), sd((B, S, LANES), F32), sd((B, S, D_SSD), BF16), sd((1, D_SSD), F32),
                   sd((8, LANES), F32)],
        scratch_shapes=[pltpu.VMEM((D_SSD, SSD_STATE), F32), pltpu.VMEM((1, D_SSD), F32)],
        compiler_params=_cparams(("arbitrary", "arbitrary")),
    )(xbc, dtk, z, y, prev, dys, dtb, alog, dsk, nw, expand)


def _rope_tables(pos_ref, invf_ref):
    ang = pos_ref[...].astype(F32) * invf_ref[...]
    return jnp.cos(ang), jnp.sin(ang)


def _rot(u):
    lane = lax.broadcasted_iota(jnp.int32, u.shape, 1)
    first = (lane >= QK_NOPE) & (lane < QK_NOPE + QK_ROPE // 2)
    second = (lane >= QK_NOPE + QK_ROPE // 2) & (lane < QK_DIM)
    return jnp.where(first, -pltpu.roll(u, LANES - QK_ROPE // 2, 1), jnp.where(second, pltpu.roll(u, QK_ROPE // 2, 1), 0.0))


def _rope_lanes(shape):
    lane = lax.broadcasted_iota(jnp.int32, shape, 1)
    return (lane >= QK_NOPE) & (lane < QK_DIM)


def _mla_prep(cq, ckv, dtk, pos, qw, kvw, wuq, wukv, invf):
    T = cq.shape[0]
    tm = min(TOKEN_TILE, T)
    scale = 1.0 / math.sqrt(QK_DIM)
    HW = MLA_HEADS * HEAD_LANES

    def body(cq_ref, ckv_ref, dtk_ref, pos_ref, qw_ref, kvw_ref, wuq_ref, wukv_ref, invf_ref, q_ref, k_ref, v_ref):
        xh, _ = _rms_stats(cq_ref[...])
        qv = _dot((xh * qw_ref[...]).astype(BF16), wuq_ref[...])
        xh, _ = _rms_stats(ckv_ref[...])
        kv = _dot((xh * kvw_ref[...]).astype(BF16), wukv_ref[...])
        cosf, sinf = _rope_tables(pos_ref, invf_ref)
        rope = lambda u: u * cosf + _rot(u) * sinf
        dtkv = dtk_ref[...]
        kr = rope(jnp.where(_rope_lanes(dtkv.shape), dtkv, 0.0))
        for h in range(MLA_HEADS):
            cols = slice(h * HEAD_LANES, (h + 1) * HEAD_LANES)
            q_ref[:, cols] = (rope(qv[:, cols]) * scale).astype(BF16)
            k_ref[:, cols] = (kv[:, cols] + kr).astype(BF16)
        v_ref[...] = kv[:, HW:].astype(BF16)

    rows = lambda n: pl.BlockSpec((tm, n), lambda i: (i, 0))
    return pl.pallas_call(
        body, grid=(T // tm,), name="mla_prep",
        in_specs=[rows(Q_LORA), rows(KV_LORA), rows(LANES), rows(1), _resident((1, Q_LORA)), _resident((1, KV_LORA)),
                  _resident((Q_LORA, HW)), _resident((KV_LORA, 2 * HW)), _resident((1, LANES))],
        out_specs=[rows(HW), rows(HW), rows(HW)],
        out_shape=[jax.ShapeDtypeStruct((T, HW), BF16)] * 3,
        compiler_params=_cparams(("arbitrary",)),
    )(cq, ckv, dtk, pos, qw, kvw, wuq, wukv, invf)


def _mla_prep_bwd(dq, dk, dv, cq, ckv, pos, qw, kvw, wuq, wukv, invf):
    T = cq.shape[0]
    tm = min(TOKEN_TILE, T)
    scale = 1.0 / math.sqrt(QK_DIM)
    HW = MLA_HEADS * HEAD_LANES

    def body(dq_ref, dk_ref, dv_ref, cq_ref, ckv_ref, pos_ref, qw_ref, kvw_ref, wuq_ref, wukv_ref, invf_ref,
             dcq_ref, dckv_ref, ddtk_ref, qn_ref, kvn_ref, dqo_ref, dkvo_ref, dqw_ref, dkvw_ref):
        @pl.when(pl.program_id(0) == 0)
        def _():
            dqw_ref[...] = jnp.zeros_like(dqw_ref)
            dkvw_ref[...] = jnp.zeros_like(dkvw_ref)

        cosf, sinf = _rope_tables(pos_ref, invf_ref)
        unrope = lambda d: d * cosf - _rot(d * sinf)
        dkr = jnp.zeros((tm, LANES), F32)
        nope = lax.broadcasted_iota(jnp.int32, (tm, LANES), 1) < QK_NOPE
        for h in range(MLA_HEADS):
            cols = slice(h * HEAD_LANES, (h + 1) * HEAD_LANES)
            dqo_ref[:, cols] = unrope(dq_ref[:, cols] * scale).astype(BF16)
            dkh = dk_ref[:, cols]
            dkr = dkr + jnp.where(_rope_lanes(dkh.shape), dkh, 0.0)
            dkvo_ref[:, cols] = jnp.where(nope, dkh, 0.0).astype(BF16)
        dkvo_ref[:, HW:] = dv_ref[...].astype(BF16)
        ddtk_ref[...] = unrope(dkr)
        xh, r = _rms_stats(cq_ref[...])
        qn_ref[...] = (xh * qw_ref[...]).astype(BF16)
        dx, dw_rows = _rms_bwd(_dot_nt(dqo_ref[...], wuq_ref[...]), xh, r, qw_ref[...])
        dcq_ref[...] = dx
        dqw_ref[...] += _colsum(dw_rows)
        xh, r = _rms_stats(ckv_ref[...])
        kvn_ref[...] = (xh * kvw_ref[...]).astype(BF16)
        dx, dw_rows = _rms_bwd(_dot_nt(dkvo_ref[...], wukv_ref[...]), xh, r, kvw_ref[...])
        dckv_ref[...] = dx
        dkvw_ref[...] += _colsum(dw_rows)

    rows = lambda n: pl.BlockSpec((tm, n), lambda i: (i, 0))
    sd = jax.ShapeDtypeStruct
    return pl.pallas_call(
        body, grid=(T // tm,), name="mla_prep_bwd",
        in_specs=[rows(HW), rows(HW), rows(HW), rows(Q_LORA), rows(KV_LORA), rows(1), _resident((1, Q_LORA)),
                  _resident((1, KV_LORA)), _resident((Q_LORA, HW)), _resident((KV_LORA, 2 * HW)), _resident((1, LANES))],
        out_specs=[rows(Q_LORA), rows(KV_LORA), rows(LANES), rows(Q_LORA), rows(KV_LORA), rows(HW), rows(2 * HW),
                   pl.BlockSpec((1, Q_LORA), lambda i: (0, 0)), pl.BlockSpec((1, KV_LORA), lambda i: (0, 0))],
        out_shape=[sd((T, Q_LORA), F32), sd((T, KV_LORA), F32), sd((T, LANES), F32), sd((T, Q_LORA), BF16),
                   sd((T, KV_LORA), BF16), sd((T, HW), BF16), sd((T, 2 * HW), BF16), sd((1, Q_LORA), F32),
                   sd((1, KV_LORA), F32)],
        compiler_params=_cparams(("arbitrary",)),
    )(dq, dk, dv, cq, ckv, pos, qw, kvw, wuq, wukv, invf)


def _causal_mask(t):
    row = lax.broadcasted_iota(jnp.int32, (t, t), 0)
    col = lax.broadcasted_iota(jnp.int32, (t, t), 1)
    return col <= row


def _attn_fwd(q, k, v):
    B, S, HW = q.shape
    H = HW // HEAD_LANES
    t = min(ATTN_FWD_Q_TILE, S)
    tk = min(ATTN_FWD_KV_TILE, t)
    nq = S // t
    per = t // tk

    pair = 2
    pw = pair * HEAD_LANES

    def body(q_ref, k_ref, v_ref, o_ref, lse_ref):
        qi = pl.program_id(2)
        lanes = [slice(hh * HEAD_LANES, (hh + 1) * HEAD_LANES) for hh in range(pair)]
        qs = [q_ref[0, :, cols] for cols in lanes]

        def step(j, carry, diag):
            sl = pl.ds(pl.multiple_of(j * tk, tk), tk)
            out = []
            for qv, cols, (m, l, acc) in zip(qs, lanes, carry):
                s = _dot_nt(qv, k_ref[0, sl, cols])
                if diag is not None:
                    row = lax.broadcasted_iota(jnp.int32, (t, tk), 0)
                    col = lax.broadcasted_iota(jnp.int32, (t, tk), 1)
                    s = jnp.where(col + diag * tk <= row, s, -1e30)
                m_new = jnp.maximum(m, jnp.max(s, axis=-1, keepdims=True))
                alpha = jnp.exp(m - m_new)
                p = jnp.exp(s - m_new)
                l = alpha * l + jnp.sum(p, axis=-1, keepdims=True)
                acc = alpha * acc + _dot(p.astype(BF16), v_ref[0, sl, cols])
                out.append((m_new, l, acc))
            return tuple(out)

        init = tuple((jnp.full((t, 1), -1e30, F32), jnp.zeros((t, 1), F32), jnp.zeros((t, HEAD_LANES), F32))
                     for _ in range(pair))
        carry = lax.fori_loop(0, qi * per, lambda j, c: step(j, c, None), init)
        for d in range(per):
            carry = step(qi * per + d, carry, d)
        for hh, (m, l, acc) in enumerate(carry):
            o_ref[0, :, lanes[hh]] = (acc / l).astype(BF16)
            lse_ref[0, hh] = m + jnp.log(l)

    return pl.pallas_call(
        body, grid=(B, H // pair, nq), name="attn_fwd",
        in_specs=[pl.BlockSpec((1, t, pw), lambda b, h, i: (b, i, h)),
                  pl.BlockSpec((1, S, pw), lambda b, h, i: (b, 0, h)),
                  pl.BlockSpec((1, S, pw), lambda b, h, i: (b, 0, h))],
        out_specs=[pl.BlockSpec((1, t, pw), lambda b, h, i: (b, i, h)),
                   pl.BlockSpec((1, pair, t, 1), lambda b, h, i: (b, h, i, 0))],
        out_shape=[jax.ShapeDtypeStruct((B, S, HW), BF16), jax.ShapeDtypeStruct((B, H, S, 1), F32)],
        compiler_params=_cparams(("arbitrary", "arbitrary", "arbitrary")),
    )(q, k, v)


def _attn_bwd(q, k, v, o, do, lse):
    B, S, HW = q.shape
    H = HW // HEAD_LANES
    t = min(ATTN_BWD_TILE, S)
    nq = S // t

    def body(q_ref, k_ref, v_ref, o_ref, do_ref, lse_ref, dq_ref, dk_ref, dv_ref):
        j = pl.program_id(2)

        @pl.when(j == 0)
        def _():
            dq_ref[...] = jnp.zeros_like(dq_ref)

        kj = k_ref[0]
        vj = v_ref[0]

        def step(i, carry, masked):
            dk, dv = carry
            sl = pl.ds(pl.multiple_of(i * t, t), t)
            qi = q_ref[0, sl, :]
            doi = do_ref[0, sl, :]
            s = _dot_nt(qi, kj)
            if masked:
                s = jnp.where(_causal_mask(t), s, -1e30)
            p = jnp.exp(s - lse_ref[0, 0, sl, :])
            dv = dv + _dot_tn(p.astype(BF16), doi)
            dp = _dot_nt(doi, vj)
            delta = jnp.sum(doi.astype(F32) * o_ref[0, sl, :].astype(F32), axis=-1, keepdims=True)
            dsb = (p * (dp - delta)).astype(BF16)
            dk = dk + _dot_tn(dsb, qi)
            dq_ref[0, sl, :] += _dot(dsb, kj)
            return dk, dv

        zero = jnp.zeros((t, HEAD_LANES), F32)
        carry = step(j, (zero, zero), True)
        dk, dv = lax.fori_loop(j + 1, nq, lambda i, c: step(i, c, False), carry)
        dk_ref[0] = dk
        dv_ref[0] = dv

    full = pl.BlockSpec((1, S, HEAD_LANES), lambda b, h, j: (b, 0, h))
    tile = pl.BlockSpec((1, t, HEAD_LANES), lambda b, h, j: (b, j, h))
    sd = jax.ShapeDtypeStruct
    return pl.pallas_call(
        body, grid=(B, H, nq), name="attn_bwd",
        in_specs=[full, tile, tile, full, full, pl.BlockSpec((1, 1, S, 1), lambda b, h, j: (b, h, 0, 0))],
        out_specs=[full, tile, tile],
        out_shape=[sd((B, S, HW), F32), sd((B, S, HW), F32), sd((B, S, HW), F32)],
        compiler_params=_cparams(("arbitrary", "arbitrary", "arbitrary")),
    )(q, k, v, o, do, lse)


def _mix_out(x1, yssd, o, mw, wout, g, seq):
    T, D = x1.shape
    tm = min(TOKEN_TILE, seq)
    tps = seq // tm

    def body(x_ref, ys_ref, o_ref, mw_ref, w_ref, g_ref, xo_ref, m_ref, yc_ref):
        xh, _ = _rms_stats(o_ref[...].astype(F32))
        ycat = jnp.concatenate([ys_ref[...], (xh * mw_ref[...]).astype(BF16)], axis=1)
        m = _dot(ycat, w_ref[...])
        xo_ref[...] = x_ref[...] + g_ref[0] * m
        m_ref[...] = m.astype(BF16)
        yc_ref[...] = ycat

    rows = lambda n: pl.BlockSpec((tm, n), lambda i: (i, 0))
    perb = pl.BlockSpec((1, 1, D), lambda i: (i // tps, 0, 0))
    sd = jax.ShapeDtypeStruct
    return pl.pallas_call(
        body, grid=(T // tm,), name="mix_out",
        in_specs=[rows(D), rows(D_SSD), rows(D_MLA), _resident((1, D_MLA)), _resident((D_SSD + D_MLA, D)), perb],
        out_specs=[rows(D), rows(D), rows(D_SSD + D_MLA)],
        out_shape=[sd((T, D), F32), sd((T, D), BF16), sd((T, D_SSD + D_MLA), BF16)],
        compiler_params=_cparams(("arbitrary",)),
    )(x1, yssd, o, mw, wout, g)


def _mix_out_bwd(dx2, m, o, mw, wout, g, seq):
    T, D = dx2.shape
    B = T // seq
    tm = min(TOKEN_TILE, seq)
    tps = seq // tm

    def body(dx_ref, m_ref, o_ref, mw_ref, w_ref, g_ref, dys_ref, do_ref, dm_ref, dg_ref, dmw_ref):
        i = pl.program_id(0)

        @pl.when(i % tps == 0)
        def _():
            dg_ref[...] = jnp.zeros_like(dg_ref)

        @pl.when(i == 0)
        def _():
            dmw_ref[...] = jnp.zeros_like(dmw_ref)

        dxv = dx_ref[...]
        dg_ref[0] += _colsum(dxv * m_ref[...].astype(F32))
        dmb = (g_ref[0] * dxv).astype(BF16)
        dm_ref[...] = dmb
        dycat = _dot_nt(dmb, w_ref[...])
        dys_ref[...] = dycat[:, :D_SSD].astype(BF16)
        xh, r = _rms_stats(o_ref[...].astype(F32))
        dx, dw_rows = _rms_bwd(dycat[:, D_SSD:], xh, r, mw_ref[...])
        do_ref[...] = dx.astype(BF16)
        dmw_ref[...] += _colsum(dw_rows)

    rows = lambda n: pl.BlockSpec((tm, n), lambda i: (i, 0))
    perb = pl.BlockSpec((1, 1, D), lambda i: (i // tps, 0, 0))
    sd = jax.ShapeDtypeStruct
    return pl.pallas_call(
        body, grid=(T // tm,), name="mix_out_bwd",
        in_specs=[rows(D), rows(D), rows(D_MLA), _resident((1, D_MLA)), _resident((D_SSD + D_MLA, D)), perb],
        out_specs=[rows(D_SSD), rows(D_MLA), rows(D), perb, pl.BlockSpec((1, D_MLA), lambda i: (0, 0))],
        out_shape=[sd((T, D_SSD), BF16), sd((T, D_MLA), BF16), sd((T, D), BF16), sd((B, 1, D), F32), sd((1, D_MLA), F32)],
        compiler_params=_cparams(("arbitrary",)),
    )(dx2, m, o, mw, wout, g)


def _win_to_kernel(w):
    z0 = jnp.zeros((48, w.shape[1]), w.dtype)
    z1 = jnp.zeros((32, w.shape[1]), w.dtype)
    return jnp.concatenate([w[:2560], w[2576:3216], w[2560:2576], z0, w[3216:3248], z1], axis=0)


def _win_from_kernel(g):
    return jnp.concatenate([g[:2560], g[3200:3216], g[2560:3200], g[3264:3296]], axis=0)


def _wuq_to_kernel(w):
    w = w.reshape(Q_LORA, MLA_HEADS, QK_DIM)
    return jnp.pad(w, ((0, 0), (0, 0), (0, HEAD_LANES - QK_DIM))).reshape(Q_LORA, MLA_HEADS * HEAD_LANES)


def _wuq_from_kernel(g):
    return g.reshape(Q_LORA, MLA_HEADS, HEAD_LANES)[:, :, :QK_DIM].reshape(Q_LORA, MLA_HEADS * QK_DIM)


def _wukv_to_kernel(w):
    w = w.reshape(KV_LORA, MLA_HEADS, QK_NOPE + V_HEAD)
    kp = jnp.pad(w[:, :, :QK_NOPE], ((0, 0), (0, 0), (0, HEAD_LANES - QK_NOPE)))
    return jnp.concatenate([kp.reshape(KV_LORA, -1), w[:, :, QK_NOPE:].reshape(KV_LORA, -1)], axis=1)


def _wukv_from_kernel(g):
    hw = MLA_HEADS * HEAD_LANES
    kp = g[:, :hw].reshape(KV_LORA, MLA_HEADS, HEAD_LANES)[:, :, :QK_NOPE]
    vp = g[:, hw:].reshape(KV_LORA, MLA_HEADS, V_HEAD)
    return jnp.concatenate([kp, vp], axis=2).reshape(KV_LORA, MLA_HEADS * (QK_NOPE + V_HEAD))


def _lanes16(v):
    return jnp.pad(v.reshape(1, SSD_HEADS), ((0, 0), (0, LANES - SSD_HEADS)))


def _constants():
    e = np.zeros((LANES, D_SSD), np.float32)
    for h in range(SSD_HEADS):
        e[h, h * SSD_HEAD_DIM:(h + 1) * SSD_HEAD_DIM] = 1.0
    inv_freq = ROPE_THETA ** (-jnp.arange(0, QK_ROPE, 2, dtype=F32) / QK_ROPE)
    half = QK_ROPE // 2
    invf = jnp.zeros((1, LANES), F32).at[0, QK_NOPE:QK_NOPE + half].set(inv_freq).at[0, QK_NOPE + half:QK_DIM].set(inv_freq)
    return jnp.asarray(e, BF16), invf


def _local_step(x, positions, mod, w, later_weights, small, tgt, on_grads):
    B, S, D = x.shape
    T = B * S
    expand, invf = _constants()
    x0 = x.reshape(T, D)
    pos = positions.reshape(T, 1)
    mods = [mod[:, i * D:(i + 1) * D].reshape(B, 1, D) for i in range(N_MOD)]
    sh1, sc1, g1, sh2, sc2, g2, sh3, sc3, g3 = mods
    dtb, alog = _lanes16(small["dt_bias"]), _lanes16(small["a_log"])
    dsk = jnp.repeat(small["d_skip"].reshape(1, SSD_HEADS), SSD_HEAD_DIM, axis=1)

    x1, a1, u1, f1 = _ffn_fwd(x0, small["norm_ffn1"], sh1, sc1, g1, w["ffn1_w_gate"], w["ffn1_w_up"], w["ffn1_w_down"], S, "ffn1_fwd")
    w = {**w, **later_weights(f1)}
    z, xraw, cq, ckv, dtk = _inproj_fwd(x1, small["norm_mix"], sh2, sc2, w["w_in"], S)
    xraw3 = xraw.reshape(B, S, D_CONV)
    xbc = _conv_fwd(xraw3, small["conv_w"], small["conv_b"])
    dtk3, z3 = dtk.reshape(B, S, LANES), z.reshape(B, S, D_SSD)
    y, yssd, prev = _ssd_fwd(xbc, dtk3, z3, dtb, alog, dsk, small["ssd_norm_w"], expand)
    q, k, v = _mla_prep(cq, ckv, dtk, pos, small["q_norm_w"], small["kv_norm_w"], w["w_uq"], w["w_ukv"], invf)
    hw = MLA_HEADS * HEAD_LANES
    q3, k3, v3 = q.reshape(B, S, hw), k.reshape(B, S, hw), v.reshape(B, S, hw)
    o3, lse = _attn_fwd(q3, k3, v3)
    o = o3.reshape(T, hw)
    x2, m, ycat = _mix_out(x1, yssd.reshape(T, D_SSD), o, small["mla_norm_w"], w["w_out"], g2, S)
    x3, a2, u2, f2 = _ffn_fwd(x2, small["norm_ffn2"], sh3, sc3, g3, w["ffn2_w_gate"], w["ffn2_w_up"], w["ffn2_w_down"], S, "ffn2_fwd")
    dx3, loss, d_norm_final = _final_loss(x3, small["norm_final"].reshape(1, D), tgt.reshape(T, D))

    gw, gs = {}, {}
    dx2, h3, s3, df3, da3, du3, dsh3, dsc3, dg3, gs["norm_ffn2"] = _ffn_bwd(
        dx3, x2, small["norm_ffn2"], sh3, sc3, g3, a2, u2, f2, w["ffn2_w_gate"], w["ffn2_w_up"], w["ffn2_w_down"], S, "ffn2_bwd")
    gw["ffn2_w_gate"], gw["ffn2_w_up"], gw["ffn2_w_down"] = _ffn_wgrad(h3, s3, df3, da3, du3, "ffn2_wgrad")
    g2 = g2 + on_grads(("ffn2_w_gate", "ffn2_w_up", "ffn2_w_down"), gw)

    dys, do, dm, dg2, gs["mla_norm_w"] = _mix_out_bwd(dx2, m, o, small["mla_norm_w"], w["w_out"], g2, S)
    gw["w_out"] = _mm_tn(ycat, dm, 512, "dwout")

    dq3, dk3, dv3 = _attn_bwd(q3, k3, v3, o3, do.reshape(B, S, hw), lse)
    dcq, dckv, ddtk_b, qn, kvn, dqb, dkvb, gs["q_norm_w"], gs["kv_norm_w"] = _mla_prep_bwd(
        dq3.reshape(T, hw), dk3.reshape(T, hw), dv3.reshape(T, hw), cq, ckv, pos, small["q_norm_w"], small["kv_norm_w"],
        w["w_uq"], w["w_ukv"], invf)
    gw["w_uq"] = _mm_tn(qn, dqb, 512, "dwuq")
    gw["w_ukv"] = _mm_tn(kvn, dkvb, 1024, "dwukv")

    dxbc, ddtk_a, dz, gs["ssd_norm_w"], dvec = _ssd_bwd(
        xbc, dtk3, z3, y, prev, dys.reshape(B, S, D_SSD), dtb, alog, dsk, small["ssd_norm_w"], expand)
    gs["dt_bias"], gs["a_log"], gs["d_skip"] = dvec[0:1, :SSD_HEADS], dvec[1:2, :SSD_HEADS], dvec[2:3, :SSD_HEADS]
    dxraw, gs["conv_w"], gs["conv_b"] = _conv_bwd(dxbc, xraw3, small["conv_w"], small["conv_b"])
    dx1, h2, dproj, dsh2, dsc2, gs["norm_mix"] = _inproj_bwd(
        dx2, x1, small["norm_mix"], sh2, sc2, w["w_in"], dz.reshape(T, D_SSD), dxraw.reshape(T, D_CONV), dcq, dckv,
        ddtk_a.reshape(T, LANES), ddtk_b, S)
    gw["w_in"] = _mm_tn(dproj, h2, 512, "dwin")
    g1 = g1 + on_grads(("w_in", "w_uq", "w_ukv", "w_out"), gw)

    dx0, h1, s1, df1, da1, du1, dsh1, dsc1, dg1, gs["norm_ffn1"] = _ffn_bwd(
        dx1, x0, small["norm_ffn1"], sh1, sc1, g1, a1, u1, f1, w["ffn1_w_gate"], w["ffn1_w_up"], w["ffn1_w_down"], S, "ffn1_bwd")
    gw["ffn1_w_gate"], gw["ffn1_w_up"], gw["ffn1_w_down"] = _ffn_wgrad(h1, s1, df1, da1, du1, "ffn1_wgrad")
    gs["norm_final"] = d_norm_final
    dmod = jnp.concatenate([t.reshape(B, D) for t in (dsh1, dsc1, dg1, dsh2, dsc2, dg2, dsh3, dsc3, dg3)], axis=1)
    return loss, dx0.reshape(B, S, D), gw, dmod, gs


HBM_SPEC = pl.BlockSpec(memory_space=pltpu.HBM)
VMEM_SPEC = pl.BlockSpec(memory_space=pltpu.VMEM)


def _place():
    return lax.axis_index("x"), lax.axis_index("y"), lax.axis_index("c")


def _other_chips(mx, my):
    return [(1 - mx, my), (mx, 1 - my), (1 - mx, 1 - my)]


def _remote(src, dst, send_sem, recv_sem, to):
    return pltpu.make_async_remote_copy(src_ref=src, dst_ref=dst, send_sem=send_sem, recv_sem=recv_sem,
                                        device_id=to, device_id_type=MESH)


def _all_gather_small(xa, name):
    r, n = xa.shape

    def body(x_ref, o_ref, token, send_sems, recv_sems):
        mx, my, mc = _place()
        me = 4 * mx + 2 * my + mc
        token[...] = jnp.zeros_like(token)
        o_ref[pl.ds(me, 1)] = x_ref[...][None]
        sends = []
        for k in range(1, N_DEV):
            peer = (mx ^ (k >> 2), my ^ ((k >> 1) & 1), mc ^ (k & 1))
            cp = _remote(x_ref, o_ref.at[me], send_sems.at[k - 1], recv_sems.at[k - 1], peer)
            cp.start()
            sends.append(cp)
        for k in range(1, N_DEV):
            peer = (mx ^ (k >> 2), my ^ ((k >> 1) & 1), mc ^ (k & 1))
            slot = 4 * peer[0] + 2 * peer[1] + peer[2]
            _remote(x_ref, o_ref.at[slot], send_sems.at[k - 1], recv_sems.at[k - 1], peer).wait_recv()
        for cp in sends:
            cp.wait_send()

    return pl.pallas_call(
        body, name=name, in_specs=[VMEM_SPEC], out_specs=[VMEM_SPEC, VMEM_SPEC],
        out_shape=[jax.ShapeDtypeStruct((N_DEV, r, n), xa.dtype), jax.ShapeDtypeStruct((8, LANES), F32)],
        scratch_shapes=[pltpu.SemaphoreType.DMA((N_DEV - 1,)), pltpu.SemaphoreType.DMA((N_DEV - 1,))],
        compiler_params=pltpu.CompilerParams(vmem_limit_bytes=VMEM_LIMIT),
    )(xa)


def _halves_by_rows(shape):
    return (shape[-2] // 2) % 16 == 0


def _half_shape(shape):
    r, c = shape[-2:]
    return tuple(shape[:-2]) + ((r // 2, c) if _halves_by_rows(shape) else (r, c // 2))


def _half_index(shape, hc):
    r, c = shape[-2:]
    if _halves_by_rows(shape):
        return (pl.ds(pl.multiple_of(hc * (r // 2), 16), r // 2), slice(None))
    return (slice(None), pl.ds(pl.multiple_of(hc * (c // 2), LANES), c // 2))


def _half(ref, hc, lead=None):
    idx = _half_index(ref.shape, hc)
    return ref.at[idx] if lead is None else ref.at[(lead,) + idx]


def _gather_weights(shards):
    n = len(shards)

    def body(*refs):
        w_refs, o_refs, token = refs[:n], refs[n:2 * n], refs[2 * n]
        send_sems, recv_sems, stage_sems = refs[2 * n + 1:2 * n + 4]
        stages = refs[2 * n + 4:]
        mx, my, mc = _place()
        chip = 2 * mx + my
        others = _other_chips(mx, my)
        sibling = (mx, my, 1 - mc)
        token[...] = jnp.zeros_like(token)
        stage_in = [pltpu.make_async_copy(w, st, stage_sems.at[0, i]) for i, (w, st) in enumerate(zip(w_refs, stages))]
        for cp in stage_in:
            cp.start()
        first = []
        for i, (w, o) in enumerate(zip(w_refs, o_refs)):
            for k, (cx, cy) in enumerate(others):
                first.append(_remote(_half(w, mc), _half(o, mc, chip), send_sems.at[i, k],
                                     recv_sems.at[i, k], (cx, cy, mc)))
                first[-1].start()
        stage_out = []
        for i, (st, o) in enumerate(zip(stages, o_refs)):
            stage_in[i].wait()
            stage_out.append(pltpu.make_async_copy(st, o.at[chip], stage_sems.at[1, i]))
            stage_out[-1].start()
        passed = []
        for i, (w, o) in enumerate(zip(w_refs, o_refs)):
            for k, (cx, cy) in enumerate(others):
                landed = _half(o, mc, 2 * cx + cy)
                _remote(landed, landed, send_sems.at[i, k], recv_sems.at[i, k], (cx, cy, mc)).wait_recv()
                passed.append(_remote(landed, landed, send_sems.at[i, 3 + k], recv_sems.at[i, 3 + k], sibling))
                passed[-1].start()
        for i, (w, o) in enumerate(zip(w_refs, o_refs)):
            for k, (cx, cy) in enumerate(others):
                there = _half(o, 1 - mc, 2 * cx + cy)
                _remote(there, there, send_sems.at[i, 3 + k], recv_sems.at[i, 3 + k], sibling).wait_recv()
        for cp in first + passed:
            cp.wait_send()
        for cp in stage_out:
            cp.wait()

    out = pl.pallas_call(
        body, name="gather_weights", in_specs=[HBM_SPEC] * n, out_specs=[HBM_SPEC] * n + [VMEM_SPEC],
        out_shape=[jax.ShapeDtypeStruct((N_CHIPS,) + s.shape, s.dtype) for s in shards] + [jax.ShapeDtypeStruct((8, LANES), F32)],
        scratch_shapes=[pltpu.SemaphoreType.DMA((n, 6)), pltpu.SemaphoreType.DMA((n, 6)), pltpu.SemaphoreType.DMA((2, n))]
        + [pltpu.VMEM(s.shape, s.dtype) for s in shards],
        compiler_params=pltpu.CompilerParams(vmem_limit_bytes=VMEM_LIMIT),
    )(*shards)
    return out[:n], out[n]


SEM_SPEC = pl.BlockSpec(memory_space=pltpu.SEMAPHORE)
ANY_SPEC = pl.BlockSpec(memory_space=pl.ANY)
DATAFLOW = pltpu.SideEffectType.DATAFLOW_SIDE_EFFECTING


def _hbm(arr):
    return pltpu.with_memory_space_constraint(arr, pltpu.HBM)


def _gather_start(shards):
    n = len(shards)

    def body(*refs):
        w_refs, land_refs, send_sems, recv_sems, token = refs[:n], refs[n:2 * n], refs[2 * n], refs[2 * n + 1], refs[-1]
        mx, my, mc = _place()
        chip = 2 * mx + my
        for i, (w, land) in enumerate(zip(w_refs, land_refs)):
            for k, (cx, cy) in enumerate(_other_chips(mx, my)):
                _remote(_half(w, mc), _half(land, mc, chip), send_sems.at[3 * i + k],
                        recv_sems.at[3 * i + k], (cx, cy, mc)).start()
        token[...] = jnp.zeros_like(token)

    lands = [lax.empty((N_CHIPS,) + s.shape, s.dtype) for s in shards]
    out = pl.pallas_call(
        body, name="gather_start",
        out_shape=(pltpu.SemaphoreType.DMA((3 * n,)), pltpu.SemaphoreType.DMA((3 * n,)),
                   *[pltpu.HBM(s.shape, s.dtype) for s in shards], *[pltpu.HBM(l.shape, l.dtype) for l in lands],
                   jax.ShapeDtypeStruct((8, LANES), F32)),
        in_specs=[HBM_SPEC] * (2 * n), out_specs=(SEM_SPEC, SEM_SPEC, *[HBM_SPEC] * (2 * n), VMEM_SPEC),
        input_output_aliases={i: 2 + i for i in range(2 * n)},
        compiler_params=pltpu.CompilerParams(has_side_effects=DATAFLOW),
    )(*[_hbm(s) for s in shards], *[_hbm(l) for l in lands])
    return out[0], out[1], out[2:2 + n], out[2 + n:2 + 2 * n], out[-1]


def _gather_wait(send_sems, recv_sems, shards, lands, after):
    n = len(shards)

    def body(*refs):
        w_refs, land_refs, send_sems, recv_sems = refs[:n], refs[n:2 * n], refs[2 * n], refs[2 * n + 1]
        mx, my, mc = _place()
        for i, (w, land) in enumerate(zip(w_refs, land_refs)):
            for k, (cx, cy) in enumerate(_other_chips(mx, my)):
                cp = _remote(_half(w, mc), _half(land, mc, 2 * cx + cy), send_sems.at[3 * i + k],
                             recv_sems.at[3 * i + k], (cx, cy, mc))
                cp.wait_send()
                cp.wait_recv()

    out = pl.pallas_call(
        body, name="gather_wait",
        out_shape=(*[pltpu.HBM(s.shape, s.dtype) for s in shards], *[pltpu.HBM(l.shape, l.dtype) for l in lands]),
        in_specs=[HBM_SPEC] * (2 * n) + [SEM_SPEC, SEM_SPEC, ANY_SPEC], out_specs=tuple([HBM_SPEC] * (2 * n)),
        input_output_aliases={i: i for i in range(2 * n)},
        compiler_params=pltpu.CompilerParams(has_side_effects=DATAFLOW),
    )(*shards, *lands, send_sems, recv_sems, after)
    return out[n:]


def _gather_finish(shards, lands):
    n = len(shards)

    def body(*refs):
        w_refs, land_refs, o_refs = refs[:n], refs[n:2 * n], refs[2 * n:3 * n]
        send_sems, recv_sems, stage_sems = refs[3 * n:3 * n + 3]
        stages = refs[3 * n + 3:]
        mx, my, mc = _place()
        chip = 2 * mx + my
        others = _other_chips(mx, my)
        sibling = (mx, my, 1 - mc)
        stage_in = [pltpu.make_async_copy(w, st, stage_sems.at[0, i]) for i, (w, st) in enumerate(zip(w_refs, stages))]
        for cp in stage_in:
            cp.start()
        passed = []
        for i, (w, o) in enumerate(zip(w_refs, o_refs)):
            for k, (cx, cy) in enumerate(others):
                landed = _half(o, mc, 2 * cx + cy)
                passed.append(_remote(landed, landed, send_sems.at[i, k], recv_sems.at[i, k], sibling))
                passed[-1].start()
        stage_out = []
        for i, (st, o) in enumerate(zip(stages, o_refs)):
            stage_in[i].wait()
            stage_out.append(pltpu.make_async_copy(st, o.at[chip], stage_sems.at[1, i]))
            stage_out[-1].start()
        for i, (w, o) in enumerate(zip(w_refs, o_refs)):
            for k, (cx, cy) in enumerate(others):
                there = _half(o, 1 - mc, 2 * cx + cy)
                _remote(there, there, send_sems.at[i, k], recv_sems.at[i, k], sibling).wait_recv()
        for cp in passed:
            cp.wait_send()
        for cp in stage_out:
            cp.wait()

    return pl.pallas_call(
        body, name="gather_finish", in_specs=[HBM_SPEC] * (2 * n), out_specs=[HBM_SPEC] * n,
        out_shape=[jax.ShapeDtypeStruct(l.shape, l.dtype) for l in lands],
        input_output_aliases={n + i: i for i in range(n)},
        scratch_shapes=[pltpu.SemaphoreType.DMA((n, 3)), pltpu.SemaphoreType.DMA((n, 3)), pltpu.SemaphoreType.DMA((2, n))]
        + [pltpu.VMEM(s.shape, s.dtype) for s in shards],
        compiler_params=pltpu.CompilerParams(vmem_limit_bytes=VMEM_LIMIT),
    )(*shards, *lands)


def _scatter_start(ss, tag):
    n = len(ss)

    def body(*refs):
        s_refs, land_refs, send_sems, recv_sems, token = refs[:n], refs[n:2 * n], refs[2 * n], refs[2 * n + 1], refs[-1]
        mx, my, mc = _place()
        chip = 2 * mx + my
        for i, (s, land) in enumerate(zip(s_refs, land_refs)):
            for k, (cx, cy) in enumerate(_other_chips(mx, my)):
                _remote(s.at[2 * cx + cy], land.at[chip], send_sems.at[3 * i + k], recv_sems.at[3 * i + k],
                        (cx, cy, mc)).start()
        token[...] = jnp.zeros_like(token)

    lands = [lax.empty(s.shape, s.dtype) for s in ss]
    out = pl.pallas_call(
        body, name="scatter_start_" + tag,
        out_shape=(pltpu.SemaphoreType.DMA((3 * n,)), pltpu.SemaphoreType.DMA((3 * n,)),
                   *[pltpu.HBM(s.shape, s.dtype) for s in ss], *[pltpu.HBM(l.shape, l.dtype) for l in lands],
                   jax.ShapeDtypeStruct((8, LANES), F32)),
        in_specs=[HBM_SPEC] * (2 * n), out_specs=(SEM_SPEC, SEM_SPEC, *[HBM_SPEC] * (2 * n), VMEM_SPEC),
        input_output_aliases={i: 2 + i for i in range(2 * n)},
        compiler_params=pltpu.CompilerParams(has_side_effects=DATAFLOW),
    )(*[_hbm(s) for s in ss], *[_hbm(l) for l in lands])
    return out[0], out[1], out[2:2 + n], out[2 + n:2 + 2 * n], out[-1]


def _scatter_wait(send_sems, recv_sems, ss, lands, after, tag):
    n = len(ss)

    def body(*refs):
        s_refs, land_refs, send_sems, recv_sems = refs[:n], refs[n:2 * n], refs[2 * n], refs[2 * n + 1]
        mx, my, mc = _place()
        for i, (s, land) in enumerate(zip(s_refs, land_refs)):
            for k, (cx, cy) in enumerate(_other_chips(mx, my)):
                slot = land.at[2 * cx + cy]
                cp = _remote(s.at[2 * cx + cy], slot, send_sems.at[3 * i + k], recv_sems.at[3 * i + k], (cx, cy, mc))
                cp.wait_send()
                cp.wait_recv()

    out = pl.pallas_call(
        body, name="scatter_wait_" + tag,
        out_shape=(*[pltpu.HBM(s.shape, s.dtype) for s in ss], *[pltpu.HBM(l.shape, l.dtype) for l in lands]),
        in_specs=[HBM_SPEC] * (2 * n) + [SEM_SPEC, SEM_SPEC, ANY_SPEC], out_specs=tuple([HBM_SPEC] * (2 * n)),
        input_output_aliases={i: i for i in range(2 * n)},
        compiler_params=pltpu.CompilerParams(has_side_effects=DATAFLOW),
    )(*ss, *lands, send_sems, recv_sems, after)
    return out[:n], out[n:]


def _swap_halves(gs, after, name):
    n = len(gs)

    def body(*refs):
        g_refs, o_refs, send_sems, recv_sems = refs[:n], refs[n + 1:2 * n + 1], refs[2 * n + 1], refs[2 * n + 2]
        mx, my, mc = _place()
        copies = []
        for i, (g, o) in enumerate(zip(g_refs, o_refs)):
            src = g.at[(slice(None),) + _half_index(g.shape, 1 - mc)]
            copies.append(_remote(src, o, send_sems.at[i], recv_sems.at[i], (mx, my, 1 - mc)))
            copies[-1].start()
        for cp in copies:
            cp.wait()

    return pl.pallas_call(
        body, name=name, in_specs=[HBM_SPEC] * n + [ANY_SPEC], out_specs=[HBM_SPEC] * n,
        out_shape=[jax.ShapeDtypeStruct(_half_shape(g.shape), g.dtype) for g in gs],
        scratch_shapes=[pltpu.SemaphoreType.DMA((n,)), pltpu.SemaphoreType.DMA((n,))],
    )(*gs, after)


def _pair_sum(g, got, core, name):
    hr, hc = _half_shape(g.shape)[1:]
    by_rows = _halves_by_rows(g.shape)

    def body(core_ref, g_ref, got_ref, o_ref):
        o_ref[...] = (g_ref[...] + got_ref[...]).astype(BF16)

    return pl.pallas_call(
        body, name=name,
        grid_spec=pltpu.PrefetchScalarGridSpec(
            num_scalar_prefetch=1, grid=(N_CHIPS,),
            in_specs=[pl.BlockSpec((1, hr, hc), lambda j, core_ref: (j, core_ref[0], 0) if by_rows else (j, 0, core_ref[0])),
                      pl.BlockSpec((1, hr, hc), lambda j, core_ref: (j, 0, 0))],
            out_specs=pl.BlockSpec((1, hr, hc), lambda j, core_ref: (j, 0, 0))),
        out_shape=jax.ShapeDtypeStruct((N_CHIPS, hr, hc), BF16),
        compiler_params=_cparams(("arbitrary",)),
    )(core, g, got)


def _chip_sum(own, got, chip, name):
    _, h, c = own.shape

    def body(chip_ref, a_ref, b_ref, c_ref, d_ref, o_ref):
        o_ref[...] = ((a_ref[0].astype(F32) + b_ref[0].astype(F32)) + c_ref[0].astype(F32)) + d_ref[0].astype(F32)

    slot = lambda flip: pl.BlockSpec((1, h, c), lambda i, chip_ref: (chip_ref[0] ^ flip, 0, 0))
    return pl.pallas_call(
        body, name=name,
        grid_spec=pltpu.PrefetchScalarGridSpec(
            num_scalar_prefetch=1, grid=(1,), in_specs=[slot(0), slot(1), slot(2), slot(3)],
            out_specs=pl.BlockSpec((h, c), lambda i, chip_ref: (0, 0))),
        out_shape=jax.ShapeDtypeStruct((h, c), F32),
        compiler_params=_cparams(("arbitrary",)),
    )(chip, own, got, got, got)


def _join_halves(mine, name):
    n = len(mine)

    def body(*refs):
        m_refs, o_refs, send_sems, recv_sems = refs[:n], refs[n:2 * n], refs[2 * n], refs[2 * n + 1]
        mx, my, mc = _place()
        copies = []
        for i, (m, o) in enumerate(zip(m_refs, o_refs)):
            copies.append(_remote(m, o, send_sems.at[i], recv_sems.at[i], (mx, my, 1 - mc)))
            copies[-1].start()
        for cp in copies:
            cp.wait()

    return pl.pallas_call(
        body, name=name, in_specs=[HBM_SPEC] * n, out_specs=[HBM_SPEC] * n,
        out_shape=[jax.ShapeDtypeStruct(m.shape, m.dtype) for m in mine],
        scratch_shapes=[pltpu.SemaphoreType.DMA((n,)), pltpu.SemaphoreType.DMA((n,))],
    )(*mine)


def _adam_math(w, g, m, v):
    m2 = ADAM_B1 * m + (1.0 - ADAM_B1) * g
    v2 = ADAM_B2 * v + (1.0 - ADAM_B2) * (g * g)
    m_hat = m2 * (1.0 / (1.0 - ADAM_B1 ** ADAM_STEP))
    v_hat = v2 * (1.0 / (1.0 - ADAM_B2 ** ADAM_STEP))
    delta = -ADAM_LR * (m_hat / (jnp.sqrt(v_hat) + ADAM_EPS) + ADAM_WD * w)
    return delta, m2, v2


def _adam(w, g, m, v, name):
    def body(w_ref, g_ref, m_ref, v_ref, d_ref, m2_ref, v2_ref):
        d_ref[...], m2_ref[...], v2_ref[...] = _adam_math(w_ref[...], g_ref[...], m_ref[...], v_ref[...])

    return pl.pallas_call(body, name=name, out_shape=[jax.ShapeDtypeStruct(w.shape, F32)] * 3)(w, g, m, v)


def _adam_halves(w, m, v, mine, theirs, core, name):
    hr, hcols = _half_shape(w.shape)[1:]
    by_rows = _halves_by_rows(w.shape)

    def body(core_ref, w_ref, m_ref, v_ref, mine_ref, theirs_ref, g_ref, d_ref, m2_ref, v2_ref):
        g = jnp.where(pl.program_id(0) == core_ref[0], mine_ref[...], theirs_ref[...])
        g_ref[0] = g
        d_ref[0], m2_ref[0], v2_ref[0] = _adam_math(w_ref[0], g, m_ref[0], v_ref[0])

    half = pl.BlockSpec((1, hr, hcols), lambda hc, core_ref: (0, hc, 0) if by_rows else (0, 0, hc))
    whole = pl.BlockSpec((hr, hcols), lambda hc, core_ref: (0, 0))
    return pl.pallas_call(
        body, name=name,
        grid_spec=pltpu.PrefetchScalarGridSpec(
            num_scalar_prefetch=1, grid=(2,), in_specs=[half, half, half, whole, whole], out_specs=[half] * 4),
        out_shape=[jax.ShapeDtypeStruct(w.shape, F32)] * 4,
        compiler_params=_cparams(("arbitrary",)),
    )(core, w, m, v, mine, theirs)


ADA_COLS = N_MOD * D_MODEL // N_CHIPS


def _ada_fwd(c_all, w_ada, b_cols):
    def body(c_ref, w_ref, b_ref, o_ref):
        cv = c_ref[...]
        act = (cv * _sigmoid(cv)).astype(BF16)
        o_ref[...] = _dot(act, w_ref[...].astype(BF16)) + b_ref[...]

    return pl.pallas_call(
        body, name="ada_fwd", out_shape=jax.ShapeDtypeStruct((c_all.shape[0], ADA_COLS), F32),
        compiler_params=pltpu.CompilerParams(vmem_limit_bytes=VMEM_LIMIT),
    )(c_all, w_ada, b_cols)


def _ada_bwd(c_all, dmod_cols, w, m, v):
    nb = c_all.shape[0]
    tn = 384

    def body(c_ref, d_ref, w_ref, m_ref, v_ref, g_ref, dl_ref, m2_ref, v2_ref):
        cv = c_ref[...]
        act = (cv * _sigmoid(cv)).astype(BF16)
        g = _dot_tn(act, d_ref[...].astype(BF16))
        g_ref[...] = g
        dl_ref[...], m2_ref[...], v2_ref[...] = _adam_math(w_ref[...], g, m_ref[...], v_ref[...])

    blk = pl.BlockSpec((D_MODEL, tn), lambda j: (0, j))
    return pl.pallas_call(
        body, name="ada_bwd", grid=(ADA_COLS // tn,),
        in_specs=[pl.BlockSpec((nb, D_MODEL), lambda j: (0, 0)), pl.BlockSpec((nb, tn), lambda j: (0, j)), blk, blk, blk],
        out_specs=[blk] * 4, out_shape=[jax.ShapeDtypeStruct((D_MODEL, ADA_COLS), F32)] * 4,
        compiler_params=_cparams(("arbitrary",)),
    )(c_all, dmod_cols, w, m, v)


SMALL_NAMES = ("norm_ffn1", "norm_mix", "conv_w", "conv_b", "ssd_norm_w", "q_norm_w", "kv_norm_w", "mla_norm_w",
               "norm_ffn2", "norm_final", "dt_bias", "a_log", "d_skip")
SMALL_SIZES = (1024, 1024, CONV_WIDTH * D_CONV, D_CONV, 1024, Q_LORA, KV_LORA, 1024, 1024, 1024, 16, 16, 16)
SMALL_ROWS = 16
MOD_ROWS = 2 * N_MOD
SEND_ROWS = 40


def _pack_small(parts):
    flat = jnp.concatenate([parts[n].reshape(-1) for n in SMALL_NAMES])
    return jnp.pad(flat, (0, SMALL_ROWS * D_MODEL - flat.shape[0]))


def _unpack_small(flat):
    out, off = {}, 0
    for n, size in zip(SMALL_NAMES, SMALL_SIZES):
        out[n] = flat[off:off + size]
        off += size
    return out


def _small_sum(got):
    def body(g_ref, o_ref):
        bsum = jnp.zeros((N_MOD, D_MODEL), F32)
        ssum = jnp.zeros((SMALL_ROWS, D_MODEL), F32)
        for d in range(N_DEV):
            bsum = bsum + g_ref[d, 0:N_MOD, :] + g_ref[d, N_MOD:MOD_ROWS, :]
            ssum = ssum + g_ref[d, MOD_ROWS:MOD_ROWS + SMALL_ROWS, :]
        o_ref[...] = jnp.concatenate([bsum, ssum, jnp.zeros((32 - N_MOD - SMALL_ROWS, D_MODEL), F32)], axis=0)

    return pl.pallas_call(body, name="small_sum", out_shape=jax.ShapeDtypeStruct((32, D_MODEL), F32))(got)


BIG_NAMES = ("ffn1_w_gate", "ffn1_w_up", "ffn1_w_down", "w_in", "w_uq", "w_ukv", "w_out", "ffn2_w_gate", "ffn2_w_up",
             "ffn2_w_down")
_TO_KERNEL = {"w_in": _win_to_kernel, "w_uq": _wuq_to_kernel, "w_ukv": _wukv_to_kernel}
_FROM_KERNEL = {"w_in": _win_from_kernel, "w_uq": _wuq_from_kernel, "w_ukv": _wukv_from_kernel}


def _columns_joined(w4):
    n, r, c = w4.shape
    return w4.transpose(1, 0, 2).reshape(r, n * c)


def _columns_split(g):
    r, cols = g.shape
    return g.reshape(r, N_CHIPS, cols // N_CHIPS).transpose(1, 0, 2)


def kernel(x, c, positions, w_ada, b_ada, norm_ffn1, ffn1_w_gate, ffn1_w_up, ffn1_w_down, norm_mix, w_in, conv_w, conv_b, dt_bias, a_log, d_skip, ssd_norm_w, q_norm_w, w_uq, kv_norm_w, w_ukv, mla_norm_w, w_out, norm_ffn2, ffn2_w_gate, ffn2_w_up, ffn2_w_down, norm_final, loss_target, m_w_ada, m_b_ada, m_norm_ffn1, m_ffn1_w_gate, m_ffn1_w_up, m_ffn1_w_down, m_norm_mix, m_w_in, m_conv_w, m_conv_b, m_dt_bias, m_a_log, m_d_skip, m_ssd_norm_w, m_q_norm_w, m_w_uq, m_kv_norm_w, m_w_ukv, m_mla_norm_w, m_w_out, m_norm_ffn2, m_ffn2_w_gate, m_ffn2_w_up, m_ffn2_w_down, m_norm_final, v_w_ada, v_b_ada, v_norm_ffn1, v_ffn1_w_gate, v_ffn1_w_up, v_ffn1_w_down, v_norm_mix, v_w_in, v_conv_w, v_conv_b, v_dt_bias, v_a_log, v_d_skip, v_ssd_norm_w, v_q_norm_w, v_w_uq, v_kv_norm_w, v_w_ukv, v_mla_norm_w, v_w_out, v_norm_ffn2, v_ffn2_w_gate, v_ffn2_w_up, v_ffn2_w_down, v_norm_final):
    a = dict(locals())
    held_transposed = ("ffn1_w_gate", "ffn1_w_up", "ffn2_w_gate", "ffn2_w_up", "w_in")
    for n in held_transposed:
        for p in ("", "m_", "v_"):
            a[p + n] = a[p + n].transpose(0, 2, 1)
    B, S, D = x.shape
    mx, my, mc = _place()
    chip = 2 * mx + my
    dev = 2 * chip + mc
    core = mc.astype(jnp.int32).reshape(1)
    chip_id = chip.astype(jnp.int32).reshape(1)

    cw_rows = jnp.pad(conv_w[0], ((0, 0), (0, D - conv_w.shape[2])))
    got, _ = _all_gather_small(jnp.concatenate([c, cw_rows, jnp.zeros((8 - B - CONV_WIDTH, D), F32)], axis=0), "gather_c")
    c_all = got[:, :B, :].reshape(N_DEV * B, D)
    conv_full = got[::2, B:B + CONV_WIDTH, :conv_w.shape[2]].transpose(1, 0, 2).reshape(CONV_WIDTH, D_CONV)

    b_cols = lax.dynamic_slice(b_ada, (0, chip * ADA_COLS), (1, ADA_COLS))
    mod_all, mod_done = _all_gather_small(_ada_fwd(c_all, w_ada[0], b_cols), "gather_mod")
    mod = lax.dynamic_slice(mod_all, (0, B * dev, 0), (N_DEV, B, ADA_COLS))[::2].transpose(1, 0, 2).reshape(B, N_MOD * D)

    first = ("ffn1_w_gate", "ffn1_w_up", "ffn1_w_down")
    later = tuple(n for n in BIG_NAMES if n not in first)
    got_first, gathered = _gather_weights([(a[n][0] + mod_done[0, 0]).astype(BF16) for n in first])
    w = dict(zip(first, got_first))
    in_flight = _gather_start([(a[n][0] + gathered[0, 0]).astype(BF16) for n in later])

    def later_weights(after):
        send_sems, recv_sems, shards, lands, _ = in_flight
        lands = _gather_wait(send_sems, recv_sems, shards, lands, after)
        wl = dict(zip(later, _gather_finish([a[n][0].astype(BF16) for n in later], lands)))
        for n, to_kernel in _TO_KERNEL.items():
            wl[n] = to_kernel(wl[n].reshape(-1, D) if n in held_transposed else _columns_joined(wl[n]))
        wl["w_out"] = wl["w_out"].reshape(D_SSD + D_MLA, D)
        return wl

    small = {n: a[n].reshape(1, -1) for n in SMALL_NAMES if n not in ("conv_w", "norm_final")}
    small["conv_w"], small["norm_final"] = conv_full, norm_final

    def scatter_group(names, gw, after):
        g4 = []
        for n in names:
            g = gw[n]
            if n in _FROM_KERNEL:
                g = _FROM_KERNEL[n](g) if n in held_transposed else _columns_split(_FROM_KERNEL[n](g))
            g4.append(g.reshape(N_CHIPS, a[n].shape[1], a[n].shape[2]))
        swapped = _swap_halves(g4, g4[0] if after is None else after, "swap_" + names[0])
        pair = [_pair_sum(g, got, core, "pair_sum_" + n) for n, g, got in zip(names, g4, swapped)]
        return (names,) + tuple(_scatter_start(pair, names[0]))

    grads, deltas, new_m, new_v = {}, {}, {}, {}

    def finish_group(group, after):
        names, send_sems, recv_sems, pair, lands, _ = group
        pair, lands = _scatter_wait(send_sems, recv_sems, pair, lands, after, names[0])
        mine = [_chip_sum(own, got, chip_id, "chip_sum_" + n) for n, own, got in zip(names, pair, lands)]
        for n, own, other in zip(names, mine, _join_halves(mine, "join_" + names[0])):
            grads[n], deltas[n], new_m[n], new_v[n] = _adam_halves(a[n], a["m_" + n], a["v_" + n], own, other, core, "adam_" + n)
        return deltas[names[-1]]

    groups = []

    def on_grads(names, gw):
        groups.append(scatter_group(names, gw, None))
        return groups[-1][5][0, 0]

    loss_blk, grad_x, gw, dmod, gs = _local_step(x, positions, mod + in_flight[4][0, 0], w, later_weights, small, loss_target,
                                                 on_grads)

    small_flat = _pack_small(gs).at[-1].set(loss_blk[0, 0])
    send = jnp.concatenate([dmod.reshape(MOD_ROWS, D), small_flat.reshape(SMALL_ROWS, D),
                            jnp.zeros((SEND_ROWS - MOD_ROWS - SMALL_ROWS, D), F32)], axis=0)
    got, _ = _all_gather_small(send, "gather_small")
    summed = _small_sum(got)
    sums = summed[N_MOD:N_MOD + SMALL_ROWS].reshape(-1)
    loss = sums[-1]
    gsmall = _unpack_small(sums)
    gsmall["conv_w"] = lax.dynamic_slice(gsmall["conv_w"].reshape(CONV_WIDTH, D_CONV), (0, chip * conv_w.shape[2]),
                                         (CONV_WIDTH, conv_w.shape[2]))
    gsmall["b_ada"] = summed[:N_MOD]
    names = ("b_ada",) + SMALL_NAMES
    rows = 208

    def pack(parts):
        flat = jnp.concatenate([parts[n].reshape(-1) for n in names])
        return jnp.pad(flat, (0, rows * LANES - flat.shape[0])).reshape(rows, LANES)

    packed = [pack({n: a[p + n] for n in names}) for p in ("", "m_", "v_")]
    g_p = pack(gsmall)
    outs = (g_p,) + tuple(_adam(packed[0], g_p, packed[1], packed[2], "adam_small"))
    for dst, flat in zip((grads, deltas, new_m, new_v), outs):
        flat, off = flat.reshape(-1), 0
        for n in names:
            dst[n] = flat[off:off + a[n].size].reshape(a[n].shape)
            off += a[n].size

    dmod_all = got[:, :MOD_ROWS, :].reshape(N_DEV * B, N_MOD * D)
    dmod_cols = lax.dynamic_slice(dmod_all, (0, chip * ADA_COLS), (N_DEV * B, ADA_COLS))
    ada = _ada_bwd(c_all, dmod_cols, w_ada[0], m_w_ada[0], v_w_ada[0])
    for dst, t in zip((grads, deltas, new_m, new_v), ada):
        dst["w_ada"] = t[None]

    last = scatter_group(first, gw, summed)
    after = last[5]
    for group in groups:
        after = finish_group(group, after)
    finish_group(last, after)
    for dst in (grads, deltas, new_m, new_v):
        for n in held_transposed:
            dst[n] = dst[n].transpose(0, 2, 1)

    order = ("w_ada", "b_ada", "norm_ffn1", "ffn1_w_gate", "ffn1_w_up", "ffn1_w_down", "norm_mix", "w_in", "conv_w", "conv_b",
             "dt_bias", "a_log", "d_skip", "ssd_norm_w", "q_norm_w", "w_uq", "kv_norm_w", "w_ukv", "mla_norm_w", "w_out",
             "norm_ffn2", "ffn2_w_gate", "ffn2_w_up", "ffn2_w_down", "norm_final")
    return (loss, grad_x, *[grads[n] for n in order], *[deltas[n] for n in order], *[new_m[n] for n in order],
            *[new_v[n] for n in order])
```

```python
import functools
import math

import jax
import jax.numpy as jnp
import numpy as np
from jax import lax
from jax.experimental import pallas as pl
from jax.experimental.pallas import tpu as pltpu

F32 = jnp.float32
BF16 = jnp.bfloat16
HIGHEST = lax.Precision.HIGHEST

D_MODEL = 1024
D_FF = 2816
D_SSD = 1024
D_MLA = 1024
SSD_HEADS = 16
SSD_HEAD_DIM = 64
SSD_GROUPS = 2
SSD_STATE = 128
CONV_WIDTH = 4
CHUNK = 128
MLA_HEADS = 8
QK_NOPE = 64
QK_ROPE = 32
QK_DIM = QK_NOPE + QK_ROPE
V_HEAD = 128
Q_LORA = 384
KV_LORA = 256
ROPE_THETA = 10000.0
N_MOD = 9
EPS = 1e-6
D_CONV = D_SSD + 2 * SSD_GROUPS * SSD_STATE
D_PROJ = 3328
HEAD_LANES = 128
ADAM_LR = 0.001
ADAM_B1 = 0.9
ADAM_B2 = 0.999
ADAM_EPS = 1e-08
ADAM_WD = 0.01
ADAM_STEP = 10

LANES = 128
VMEM_LIMIT = 56 * 1024 * 1024
TOKEN_TILE = 512
ATTN_FWD_Q_TILE = 1024
ATTN_FWD_KV_TILE = 1024
ATTN_BWD_TILE = 1024
N_CHIPS = 4
N_DEV = 8

MESH = pl.DeviceIdType.MESH


def _dot(a, b, precision=None):
    return jnp.dot(a, b, preferred_element_type=F32, precision=precision)


def _dot_nt(a, b, precision=None):
    return lax.dot_general(a, b, (((1,), (1,)), ((), ())), preferred_element_type=F32, precision=precision)


def _dot_tn(a, b, precision=None):
    return lax.dot_general(a, b, (((0,), (0,)), ((), ())), preferred_element_type=F32, precision=precision)


def _cparams(semantics):
    return pltpu.CompilerParams(dimension_semantics=semantics, vmem_limit_bytes=VMEM_LIMIT)


def _resident(shape):
    zeros = (0,) * len(shape)
    return pl.BlockSpec(shape, lambda *_: zeros, pipeline_mode=pl.Buffered(1))


def _sigmoid(x):
    return jax.nn.sigmoid(x)


def _rms_stats(x):
    r = lax.rsqrt(jnp.mean(x * x, axis=-1, keepdims=True) + EPS)
    return x * r, r


def _rms_bwd(dn, xh, r, w):
    dxh = dn * w
    dx = r * (dxh - xh * jnp.mean(dxh * xh, axis=-1, keepdims=True))
    return dx, dn * xh


def _colsum(v):
    return jnp.sum(v, axis=0, keepdims=True)


def _ffn_fwd(x, nw, sh, sc, g, wg, wu, wd, seq, name, head=None):
    T, D = x.shape
    fs = wg.shape[1]
    tm = min(TOKEN_TILE, seq)
    tps = seq // tm

    def body(x_ref, nw_ref, sh_ref, sc_ref, g_ref, wg_ref, wu_ref, wd_ref, *rest):
        if head is None:
            xo_ref, a_ref, u_ref, f_ref = rest
        else:
            nf_ref, t_ref, xo_ref, a_ref, u_ref, f_ref, loss_ref, dnf_ref = rest

            @pl.when(pl.program_id(0) == 0)
            def _():
                loss_ref[...] = jnp.zeros_like(loss_ref)
                dnf_ref[...] = jnp.zeros_like(dnf_ref)

        xv = x_ref[...]
        xh, _ = _rms_stats(xv)
        h = (xh * nw_ref[...]) * (1.0 + sc_ref[0]) + sh_ref[0]
        hb = h.astype(BF16)
        f = jnp.zeros((tm, D), F32)
        for j in range(N_CHIPS):
            a = _dot_nt(hb, wg_ref[j])
            u = _dot_nt(hb, wu_ref[j])
            a_ref[j] = a.astype(BF16)
            u_ref[j] = u.astype(BF16)
            f = f + _dot((a * _sigmoid(a) * u).astype(BF16), wd_ref[j])
        f_ref[...] = f.astype(BF16)
        xo = xv + 0.5 * g_ref[0] * f
        if head is None:
            xo_ref[...] = xo
        else:
            xh, r = _rms_stats(xo)
            nfv = nf_ref[...]
            err = xh * nfv - t_ref[...]
            loss_ref[...] += (0.5 / D) * jnp.sum(err * err)
            dxo, dw_rows = _rms_bwd(err * (1.0 / D), xh, r, nfv)
            xo_ref[...] = dxo
            dnf_ref[...] += _colsum(dw_rows)

    rows = lambda n: pl.BlockSpec((tm, n), lambda i: (i, 0))
    act = pl.BlockSpec((N_CHIPS, tm, fs), lambda i: (0, i, 0))
    perb = pl.BlockSpec((1, 1, D), lambda i: (i // tps, 0, 0))
    sd = jax.ShapeDtypeStruct
    in_specs = [rows(D), _resident((1, D)), perb, perb, perb, _resident((N_CHIPS, fs, D)), _resident((N_CHIPS, fs, D)),
                _resident((N_CHIPS, fs, D))]
    out_specs = [rows(D), act, act, rows(D)]
    out_shape = [sd((T, D), F32), sd((N_CHIPS, T, fs), BF16), sd((N_CHIPS, T, fs), BF16), sd((T, D), BF16)]
    if head is not None:
        in_specs += [_resident((1, D)), rows(D)]
        out_specs += [pl.BlockSpec((8, LANES), lambda i: (0, 0)), pl.BlockSpec((1, D), lambda i: (0, 0))]
        out_shape += [sd((8, LANES), F32), sd((1, D), F32)]
    return pl.pallas_call(
        body, grid=(T // tm,), name=name, in_specs=in_specs, out_specs=out_specs, out_shape=out_shape,
        compiler_params=_cparams(("arbitrary",)),
    )(x, nw, sh, sc, g, wg, wu, wd, *(head or ()))


def _ffn_bwd(dxo, x, nw, sh, sc, g, a, u, f, wg, wu, wd, seq, name):
    T, D = x.shape
    fs = wg.shape[1]
    B = T // seq
    tm = min(TOKEN_TILE // 2, seq)
    tps = seq // tm

    def body(dxo_ref, x_ref, nw_ref, sh_ref, sc_ref, g_ref, a_ref, u_ref, f_ref, wg_ref, wu_ref, wd_ref,
             dx_ref, h_ref, s_ref, df_ref, da_ref, du_ref, dsh_ref, dsc_ref, dg_ref, dnw_ref):
        i = pl.program_id(0)

        @pl.when(i % tps == 0)
        def _():
            dsh_ref[...] = jnp.zeros_like(dsh_ref)
            dsc_ref[...] = jnp.zeros_like(dsc_ref)
            dg_ref[...] = jnp.zeros_like(dg_ref)

        @pl.when(i == 0)
        def _():
            dnw_ref[...] = jnp.zeros_like(dnw_ref)

        dxo_v = dxo_ref[...]
        dfb = (0.5 * g_ref[0] * dxo_v).astype(BF16)
        dg_ref[0] += _colsum(0.5 * dxo_v * f_ref[...].astype(F32))
        dh = jnp.zeros((tm, D), F32)
        for j in range(N_CHIPS):
            ds = _dot_nt(dfb, wd_ref[j])
            av = a_ref[j].astype(F32)
            uv = u_ref[j].astype(F32)
            sig = _sigmoid(av)
            sil = av * sig
            dab = (ds * uv * (sig * (1.0 + av * (1.0 - sig)))).astype(BF16)
            dub = (ds * sil).astype(BF16)
            dh = dh + _dot(dab, wg_ref[j]) + _dot(dub, wu_ref[j])
            s_ref[j] = (sil * uv).astype(BF16)
            da_ref[j] = dab
            du_ref[j] = dub
        xv = x_ref[...]
        xh, r = _rms_stats(xv)
        nwv = nw_ref[...]
        n = xh * nwv
        scale1 = 1.0 + sc_ref[0]
        dsc_ref[0] += _colsum(dh * n)
        dsh_ref[0] += _colsum(dh)
        dx, dw_rows = _rms_bwd(dh * scale1, xh, r, nwv)
        dnw_ref[...] += _colsum(dw_rows)
        dx_ref[...] = dxo_v + dx
        h_ref[...] = (n * scale1 + sh_ref[0]).astype(BF16)
        df_ref[...] = dfb

    rows = lambda n: pl.BlockSpec((tm, n), lambda i: (i, 0))
    act = pl.BlockSpec((N_CHIPS, tm, fs), lambda i: (0, i, 0))
    perb = pl.BlockSpec((1, 1, D), lambda i: (i // tps, 0, 0))
    sd = jax.ShapeDtypeStruct
    return pl.pallas_call(
        body, grid=(T // tm,), name=name,
        in_specs=[rows(D), rows(D), _resident((1, D)), perb, perb, perb, act, act, rows(D),
                  _resident((N_CHIPS, fs, D)), _resident((N_CHIPS, fs, D)), _resident((N_CHIPS, fs, D))],
        out_specs=[rows(D), rows(D), act, rows(D), act, act, perb, perb, perb, pl.BlockSpec((1, D), lambda i: (0, 0))],
        out_shape=[sd((T, D), F32), sd((T, D), BF16), sd((N_CHIPS, T, fs), BF16), sd((T, D), BF16),
                   sd((N_CHIPS, T, fs), BF16), sd((N_CHIPS, T, fs), BF16), sd((B, 1, D), F32), sd((B, 1, D), F32),
                   sd((B, 1, D), F32), sd((1, D), F32)],
        compiler_params=_cparams(("arbitrary",)),
    )(dxo, x, nw, sh, sc, g, a, u, f, wg, wu, wd)


def _ffn_wgrad(h, s, df, da, du, name):
    T, D = h.shape
    fs = s.shape[2]
    tt = min(TOKEN_TILE, T)

    def body(h_ref, s_ref, df_ref, da_ref, du_ref, dgate_ref, dup_ref, ddown_ref):
        @pl.when(pl.program_id(1) == 0)
        def _():
            dgate_ref[...] = jnp.zeros_like(dgate_ref)
            dup_ref[...] = jnp.zeros_like(dup_ref)
            ddown_ref[...] = jnp.zeros_like(ddown_ref)

        hv = h_ref[...]
        dgate_ref[0] += _dot_tn(da_ref[0], hv)
        dup_ref[0] += _dot_tn(du_ref[0], hv)
        ddown_ref[0] += _dot_tn(s_ref[0], df_ref[...])

    rows = pl.BlockSpec((tt, D), lambda j, t: (t, 0))
    act = pl.BlockSpec((1, tt, fs), lambda j, t: (j, t, 0))
    shard = pl.BlockSpec((1, fs, D), lambda j, t: (j, 0, 0))
    return pl.pallas_call(
        body, grid=(N_CHIPS, T // tt), name=name,
        in_specs=[rows, act, rows, act, act],
        out_specs=[shard] * 3, out_shape=[jax.ShapeDtypeStruct((N_CHIPS, fs, D), F32)] * 3,
        compiler_params=_cparams(("arbitrary", "arbitrary")),
    )(h, s, df, da, du)


def _mm_tn(xa, ya, tn, name):
    T, K = xa.shape
    N = ya.shape[1]
    tt = min(TOKEN_TILE, T)

    def body(x_ref, y_ref, o_ref):
        @pl.when(pl.program_id(1) == 0)
        def _():
            o_ref[...] = jnp.zeros_like(o_ref)

        o_ref[...] += _dot_tn(x_ref[...], y_ref[...])

    return pl.pallas_call(
        body, grid=(N // tn, T // tt), name=name,
        in_specs=[pl.BlockSpec((tt, K), lambda j, t: (t, 0)), pl.BlockSpec((tt, tn), lambda j, t: (t, j))],
        out_specs=pl.BlockSpec((K, tn), lambda j, t: (0, j)),
        out_shape=jax.ShapeDtypeStruct((K, N), F32),
        compiler_params=_cparams(("arbitrary", "arbitrary")),
    )(xa, ya)


_PROJ_SPLITS = (0, 1024, 2560, 2944, 3200, 3328)


def _inproj_fwd(x, nw, sh, sc, win, seq):
    T, D = x.shape
    tm = min(TOKEN_TILE, seq)
    tps = seq // tm
    widths = [b - a for a, b in zip(_PROJ_SPLITS[:-1], _PROJ_SPLITS[1:])]
    dtypes = [BF16, BF16, F32, F32, F32]

    def body(x_ref, nw_ref, sh_ref, sc_ref, w_ref, *outs):
        xh, _ = _rms_stats(x_ref[...])
        h = (xh * nw_ref[...]) * (1.0 + sc_ref[0]) + sh_ref[0]
        proj = _dot_nt(h.astype(BF16), w_ref[...])
        for o, lo, hi in zip(outs, _PROJ_SPLITS[:-1], _PROJ_SPLITS[1:]):
            o[...] = proj[:, lo:hi].astype(o.dtype)

    rows = lambda n: pl.BlockSpec((tm, n), lambda i: (i, 0))
    perb = pl.BlockSpec((1, 1, D), lambda i: (i // tps, 0, 0))
    return pl.pallas_call(
        body, grid=(T // tm,), name="inproj_fwd",
        in_specs=[rows(D), _resident((1, D)), perb, perb, _resident((D_PROJ, D))],
        out_specs=[rows(w) for w in widths],
        out_shape=[jax.ShapeDtypeStruct((T, w), dt) for w, dt in zip(widths, dtypes)],
        compiler_params=_cparams(("arbitrary",)),
    )(x, nw, sh, sc, win)


def _inproj_bwd(dx2, x, nw, sh, sc, win, dz, dxbc, dcq, dckv, ddtk_a, ddtk_b, seq):
    T, D = x.shape
    B = T // seq
    tm = min(TOKEN_TILE, seq)
    tps = seq // tm

    def body(dx2_ref, x_ref, nw_ref, sh_ref, sc_ref, w_ref, dz_ref, dxbc_ref, dcq_ref, dckv_ref, da_ref, db_ref,
             dx_ref, h_ref, dp_ref, dsh_ref, dsc_ref, dnw_ref):
        i = pl.program_id(0)

        @pl.when(i % tps == 0)
        def _():
            dsh_ref[...] = jnp.zeros_like(dsh_ref)
            dsc_ref[...] = jnp.zeros_like(dsc_ref)

        @pl.when(i == 0)
        def _():
            dnw_ref[...] = jnp.zeros_like(dnw_ref)

        dproj = jnp.concatenate(
            [dz_ref[...], dxbc_ref[...], dcq_ref[...].astype(BF16), dckv_ref[...].astype(BF16),
             (da_ref[...] + db_ref[...]).astype(BF16)], axis=1)
        dp_ref[...] = dproj
        dh = _dot(dproj, w_ref[...])
        xh, r = _rms_stats(x_ref[...])
        nwv = nw_ref[...]
        n = xh * nwv
        scale1 = 1.0 + sc_ref[0]
        dsc_ref[0] += _colsum(dh * n)
        dsh_ref[0] += _colsum(dh)
        dx, dw_rows = _rms_bwd(dh * scale1, xh, r, nwv)
        dnw_ref[...] += _colsum(dw_rows)
        dx_ref[...] = dx2_ref[...] + dx
        h_ref[...] = (n * scale1 + sh_ref[0]).astype(BF16)

    rows = lambda n: pl.BlockSpec((tm, n), lambda i: (i, 0))
    perb = pl.BlockSpec((1, 1, D), lambda i: (i // tps, 0, 0))
    sd = jax.ShapeDtypeStruct
    return pl.pallas_call(
        body, grid=(T // tm,), name="inproj_bwd",
        in_specs=[rows(D), rows(D), _resident((1, D)), perb, perb, _resident((D_PROJ, D)),
                  rows(1024), rows(D_CONV), rows(Q_LORA), rows(KV_LORA), rows(LANES), rows(LANES)],
        out_specs=[rows(D), rows(D), rows(D_PROJ), perb, perb, pl.BlockSpec((1, D), lambda i: (0, 0))],
        out_shape=[sd((T, D), F32), sd((T, D), BF16), sd((T, D_PROJ), BF16), sd((B, 1, D), F32), sd((B, 1, D), F32),
                   sd((1, D), F32)],
        compiler_params=_cparams(("arbitrary",)),
    )(dx2, x, nw, sh, sc, win, dz, dxbc, dcq, dckv, ddtk_a, ddtk_b)


def _shift_down(v, k, row):
    return jnp.where(row < k, 0.0, pltpu.roll(v, k, 0))


def _shift_up(v, k, row, n):
    return jnp.where(row >= n - k, 0.0, pltpu.roll(v, n - k, 0))


def _conv_pre(xv, w_ref, b_ref, row):
    pre = b_ref[...] + w_ref[CONV_WIDTH - 1:CONV_WIDTH, :] * xv
    for k in range(1, CONV_WIDTH):
        pre = pre + w_ref[CONV_WIDTH - 1 - k:CONV_WIDTH - k, :] * _shift_down(xv, k, row)
    return pre


def _conv_fwd(xraw, cw, cb):
    B, S, C = xraw.shape

    def body(x_ref, w_ref, b_ref, o_ref):
        xv = x_ref[0].astype(F32)
        row = lax.broadcasted_iota(jnp.int32, xv.shape, 0)
        pre = _conv_pre(xv, w_ref, b_ref, row)
        o_ref[0] = (pre * _sigmoid(pre)).astype(BF16)

    blk = pl.BlockSpec((1, S, LANES), lambda b, j: (b, 0, j))
    return pl.pallas_call(
        body, grid=(B, C // LANES), name="conv_fwd",
        in_specs=[blk, pl.BlockSpec((CONV_WIDTH, LANES), lambda b, j: (0, j)), pl.BlockSpec((1, LANES), lambda b, j: (0, j))],
        out_specs=blk, out_shape=jax.ShapeDtypeStruct((B, S, C), BF16),
        compiler_params=_cparams(("arbitrary", "arbitrary")),
    )(xraw, cw, cb)


def _conv_bwd(dout, xraw, cw, cb):
    B, S, C = xraw.shape

    def body(d_ref, x_ref, w_ref, b_ref, dx_ref, dw_ref, db_ref):
        @pl.when(pl.program_id(1) == 0)
        def _():
            dw_ref[...] = jnp.zeros_like(dw_ref)
            db_ref[...] = jnp.zeros_like(db_ref)

        xv = x_ref[0].astype(F32)
        row = lax.broadcasted_iota(jnp.int32, xv.shape, 0)
        pre = _conv_pre(xv, w_ref, b_ref, row)
        sig = _sigmoid(pre)
        dpre = d_ref[0].astype(F32) * (sig * (1.0 + pre * (1.0 - sig)))
        dx = w_ref[CONV_WIDTH - 1:CONV_WIDTH, :] * dpre
        for k in range(1, CONV_WIDTH):
            dx = dx + w_ref[CONV_WIDTH - 1 - k:CONV_WIDTH - k, :] * _shift_up(dpre, k, row, S)
        dx_ref[0] = dx.astype(BF16)
        db_ref[...] += _colsum(dpre)
        dws = [_colsum(dpre * (xv if k == 0 else _shift_down(xv, k, row))) for k in range(CONV_WIDTH - 1, -1, -1)]
        dw_ref[...] += jnp.concatenate(dws, axis=0)

    blk = pl.BlockSpec((1, S, LANES), lambda j, b: (b, 0, j))
    wspec = pl.BlockSpec((CONV_WIDTH, LANES), lambda j, b: (0, j))
    bspec = pl.BlockSpec((1, LANES), lambda j, b: (0, j))
    return pl.pallas_call(
        body, grid=(C // LANES, B), name="conv_bwd",
        in_specs=[blk, blk, wspec, bspec], out_specs=[blk, wspec, bspec],
        out_shape=[jax.ShapeDtypeStruct((B, S, C), BF16), jax.ShapeDtypeStruct((CONV_WIDTH, C), F32),
                   jax.ShapeDtypeStruct((1, C), F32)],
        compiler_params=_cparams(("arbitrary", "arbitrary")),
    )(dout, xraw, cw, cb)


def _softplus(x):
    return jnp.maximum(x, 0.0) + jnp.log(1.0 + jnp.exp(-jnp.abs(x)))


def _ssd_common(xbc_ref, dtk_ref, dtb_ref, alog_ref, e_ref):
    L = CHUNK
    xbc = xbc_ref[0]
    xs = xbc[:, :D_SSD].astype(F32)
    bm = xbc[:, D_SSD:D_SSD + 256]
    cm = xbc[:, D_SSD + 256:D_SSD + 512]
    head = lax.broadcasted_iota(jnp.int32, (1, LANES), 1) < SSD_HEADS
    a128 = jnp.where(head, -jnp.exp(alog_ref[...]), 0.0)
    pre = dtk_ref[0] + dtb_ref[...]
    dt = _softplus(pre)
    dA = dt * a128
    row = lax.broadcasted_iota(jnp.int32, (L, L), 0)
    col = lax.broadcasted_iota(jnp.int32, (L, L), 1)
    causal = col <= row
    tri = causal.astype(F32)
    triT = (row <= col).astype(F32)
    tri = causal.astype(BF16)
    triT = (row <= col).astype(BF16)
    dA3 = _split3(dA)
    acum = _sum3(lambda part: _dot(tri, part), dA3)
    acumT = _sum3(lambda part: _dot_tn(part, triT), dA3)
    E = e_ref[...]
    acum_f = _spread(acum, E)
    dt_f = _spread(dt, E)
    e_f = jnp.exp(acum_f)
    w_f = jnp.exp(acum_f[L - 1:L, :] - acum_f)
    xt = xs * dt_f
    return dict(xs=xs, bm=bm, cm=cm, a128=a128, pre=pre, dt=dt, causal=causal, tri=tri, triT=triT, acum=acum,
                acumT=acumT, E=E, dt_f=dt_f, e_f=e_f, w_f=w_f, xt=xt, head=head)


def _split3(x):
    p1 = x.astype(BF16)
    r1 = x - p1.astype(F32)
    p2 = r1.astype(BF16)
    return p1, p2, (r1 - p2.astype(F32)).astype(BF16)


def _sum3(mm, parts):
    return (mm(parts[0]) + mm(parts[1])) + mm(parts[2])


def _spread(v, e):
    return _sum3(lambda part: _dot(part, e), _split3(v))


def _gather_heads(v, e):
    return _sum3(lambda part: _dot_nt(part, e), _split3(v))


def _head_mask(k):
    lane = lax.broadcasted_iota(jnp.int32, (CHUNK, LANES), 1)
    return (lane >= SSD_HEAD_DIM) if k == 1 else (lane < SSD_HEAD_DIM)


def _pair_decay(alast, h0):
    row = lax.broadcasted_iota(jnp.int32, (2 * SSD_HEAD_DIM, SSD_STATE), 0)
    return jnp.exp(jnp.where(row < SSD_HEAD_DIM, alast[:, h0:h0 + 1], alast[:, h0 + 1:h0 + 2]))


def _decay_matrix(q, h):
    seg = q["acum"][:, h:h + 1] - q["acumT"][h:h + 1, :]
    return jnp.exp(jnp.where(q["causal"], seg, -1e30))


def _gated_norm(y, zz, nw):
    sig = _sigmoid(zz)
    sil = zz * sig
    yg = y * sil
    half = D_SSD // SSD_GROUPS
    parts = []
    for g in range(SSD_GROUPS):
        xh, r = _rms_stats(yg[:, g * half:(g + 1) * half])
        parts.append((xh, r))
    return sig, sil, parts


def _ssd_fwd(xbc, dtk, z, dtb, alog, dsk, nw, expand):
    B, S, _ = xbc.shape
    L = CHUNK
    nc = S // L

    def body(xbc_ref, dtk_ref, z_ref, dtb_ref, alog_ref, dsk_ref, nw_ref, e_ref, y_ref, ys_ref, prev_ref, st_ref):
        @pl.when(pl.program_id(1) == 0)
        def _():
            st_ref[...] = jnp.zeros_like(st_ref)

        q = _ssd_common(xbc_ref, dtk_ref, dtb_ref, alog_ref, e_ref)
        xtb = q["xt"].astype(BF16)
        xwb = (q["xt"] * q["w_f"]).astype(BF16)
        alast = q["acum"][L - 1:L, :]
        ys = []
        for g in range(SSD_GROUPS):
            bg = q["bm"][:, g * 128:(g + 1) * 128]
            cg = q["cm"][:, g * 128:(g + 1) * 128]
            G = _dot_nt(cg, bg)
            for pr in range(SSD_HEADS // SSD_GROUPS // 2):
                h0 = g * 8 + 2 * pr
                lo = h0 * SSD_HEAD_DIM
                xt_p = xtb[:, lo:lo + 128]
                ydiag = jnp.zeros((L, LANES), F32)
                for k in range(2):
                    M = (G * _decay_matrix(q, h0 + k)).astype(BF16)
                    ydiag = ydiag + _dot(M, jnp.where(_head_mask(k), xt_p, jnp.zeros_like(xt_p)))
                hp = st_ref[lo:lo + 128, :]
                prev_ref[0, 0, lo:lo + 128, :] = hp
                zoff = _dot_nt(cg, hp.astype(BF16))
                ys.append(ydiag + zoff * q["e_f"][:, lo:lo + 128])
                st_ref[lo:lo + 128, :] = _pair_decay(alast, h0) * hp + _dot_tn(xwb[:, lo:lo + 128], bg)
        y = jnp.concatenate(ys, axis=1) + dsk_ref[...] * q["xs"]
        y_ref[0] = y.astype(BF16)
        _, _, parts = _gated_norm(y, z_ref[0].astype(F32), nw_ref[...])
        half = D_SSD // SSD_GROUPS
        ys_ref[0] = jnp.concatenate(
            [xh * nw_ref[:, g * half:(g + 1) * half] for g, (xh, _) in enumerate(parts)], axis=1).astype(BF16)

    chunk = lambda n: pl.BlockSpec((1, L, n), lambda b, c: (b, c, 0))
    vec = pl.BlockSpec((1, LANES), lambda b, c: (0, 0))
    return pl.pallas_call(
        body, grid=(B, nc), name="ssd_fwd",
        in_specs=[chunk(D_CONV), chunk(LANES), chunk(D_SSD), vec, vec, pl.BlockSpec((1, D_SSD), lambda b, c: (0, 0)),
                  pl.BlockSpec((1, D_SSD), lambda b, c: (0, 0)), pl.BlockSpec((LANES, D_SSD), lambda b, c: (0, 0))],
        out_specs=[chunk(D_SSD), chunk(D_SSD), pl.BlockSpec((1, 1, D_SSD, SSD_STATE), lambda b, c: (b, c, 0, 0))],
        out_shape=[jax.ShapeDtypeStruct((B, S, D_SSD), BF16), jax.ShapeDtypeStruct((B, S, D_SSD), BF16),
                   jax.ShapeDtypeStruct((B, nc, D_SSD, SSD_STATE), F32)],
        scratch_shapes=[pltpu.VMEM((D_SSD, SSD_STATE), F32)],
        compiler_params=_cparams(("arbitrary", "arbitrary")),
    )(xbc, dtk, z, dtb, alog, dsk, nw, expand)


def _ssd_bwd(xbc, dtk, z, y, prev, dys, dtb, alog, dsk, nw, expand):
    B, S, _ = xbc.shape
    L = CHUNK
    nc = S // L
    half = D_SSD // SSD_GROUPS

    def body(xbc_ref, dtk_ref, z_ref, y_ref, prev_ref, dys_ref, dtb_ref, alog_ref, dsk_ref, nw_ref, e_ref,
             dxbc_ref, ddtk_ref, dz_ref, dnw_ref, dvec_ref, dh_ref, dskc_ref):
        @pl.when((pl.program_id(0) == 0) & (pl.program_id(1) == 0))
        def _():
            dnw_ref[...] = jnp.zeros_like(dnw_ref)
            dvec_ref[...] = jnp.zeros_like(dvec_ref)
            dskc_ref[...] = jnp.zeros_like(dskc_ref)

        @pl.when(pl.program_id(1) == 0)
        def _():
            dh_ref[...] = jnp.zeros_like(dh_ref)

        q = _ssd_common(xbc_ref, dtk_ref, dtb_ref, alog_ref, e_ref)
        E = q["E"]
        xs = q["xs"]
        yv = y_ref[0].astype(F32)
        zz = z_ref[0].astype(F32)
        sig, sil, parts = _gated_norm(yv, zz, nw_ref[...])
        dn = dys_ref[0].astype(F32)
        dyg, dnw_rows = [], []
        for g, (xh, r) in enumerate(parts):
            dpart, dw_rows = _rms_bwd(dn[:, g * half:(g + 1) * half], xh, r, nw_ref[:, g * half:(g + 1) * half])
            dyg.append(dpart)
            dnw_rows.append(dw_rows)
        dyg = jnp.concatenate(dyg, axis=1)
        dnw_ref[...] += _colsum(jnp.concatenate(dnw_rows, axis=1))
        dY = dyg * sil
        dz_ref[0] = (dyg * yv * (sig * (1.0 + zz * (1.0 - sig)))).astype(BF16)
        dsk_f = dsk_ref[...]
        dskc_ref[...] += _colsum(dY * xs)
        dYb = dY.astype(BF16)
        xtb = q["xt"].astype(BF16)
        xwb = (q["xt"] * q["w_f"]).astype(BF16)
        acum = q["acum"]
        alast = acum[L - 1:L, :]
        lane_id = lax.broadcasted_iota(jnp.int32, (L, LANES), 1)
        sub_id = lax.broadcasted_iota(jnp.int32, (LANES, L), 0)
        lane_row = lax.broadcasted_iota(jnp.int32, (1, LANES), 1)
        da_rows = jnp.zeros((L, LANES), F32)
        daT = jnp.zeros((LANES, L), F32)
        dxt, prod_off, prod_st, dbs, dcs = [], [], [], [], []
        hsum_row = jnp.zeros((1, LANES), F32)
        for g in range(SSD_GROUPS):
            bg = q["bm"][:, g * 128:(g + 1) * 128]
            cg = q["cm"][:, g * 128:(g + 1) * 128]
            G = _dot_nt(cg, bg)
            dG = jnp.zeros((L, L), F32)
            dcg = jnp.zeros((L, SSD_STATE), F32)
            dbg = jnp.zeros((L, SSD_STATE), F32)
            for pr in range(SSD_HEADS // SSD_GROUPS // 2):
                h0 = g * 8 + 2 * pr
                lo = h0 * SSD_HEAD_DIM
                cols = slice(lo, lo + 128)
                dY_p = dYb[:, cols]
                xt_p = xtb[:, cols]
                dxt_p = jnp.zeros((L, LANES), F32)
                for k in range(2):
                    h = h0 + k
                    Lm = _decay_matrix(q, h)
                    Mf = G * Lm
                    dYk = jnp.where(_head_mask(k), dY_p, jnp.zeros_like(dY_p))
                    dM = _dot_nt(dYk, xt_p)
                    dxt_p = dxt_p + _dot_tn(Mf.astype(BF16), dYk)
                    dG = dG + dM * Lm
                    Q = dM * Mf
                    da_rows = da_rows + jnp.where(lane_id == h, jnp.sum(Q, axis=1, keepdims=True), 0.0)
                    daT = daT + jnp.where(sub_id == h, jnp.sum(Q, axis=0, keepdims=True), 0.0)
                hp = prev_ref[0, 0, lo:lo + 128, :]
                hpb = hp.astype(BF16)
                zoff = _dot_nt(cg, hpb)
                e_p = q["e_f"][:, cols]
                dY_pf = dY[:, cols]
                dZb = (dY_pf * e_p).astype(BF16)
                dcg = dcg + _dot(dZb, hpb)
                dhp_off = _dot_tn(dZb, cg)
                prod_off.append(dY_pf * zoff * e_p)
                dS = dh_ref[lo:lo + 128, :]
                dSb = dS.astype(BF16)
                U = _dot_nt(bg, dSb)
                dxt_p = dxt_p + U * q["w_f"][:, cols]
                dbg = dbg + _dot(xwb[:, cols], dSb)
                prod_st.append(q["xt"][:, cols] * U)
                dh_ref[lo:lo + 128, :] = _pair_decay(alast, h0) * dS + dhp_off
                dsh = dS * hp
                for k in range(2):
                    total = jnp.sum(dsh[k * SSD_HEAD_DIM:(k + 1) * SSD_HEAD_DIM, :], axis=(0, 1), keepdims=True)
                    hsum_row = hsum_row + jnp.where(lane_row == h0 + k, total, 0.0)
                dxt.append(dxt_p)
            dGb = dG.astype(BF16)
            dcs.append(dcg + _dot(dGb, bg))
            dbs.append(dbg + _dot_tn(dGb, cg))
        dxt = jnp.concatenate(dxt, axis=1)
        da_rows = da_rows + _gather_heads(jnp.concatenate(prod_off, axis=1), E)
        dww = _gather_heads(jnp.concatenate(prod_st, axis=1), E) * jnp.exp(alast - acum)
        da_rows = da_rows - dww
        dlast = _colsum(dww) + jnp.exp(alast) * hsum_row
        triT = q["triT"]
        ddA = (_sum3(lambda part: _dot(triT, part), _split3(da_rows))
               - _sum3(lambda part: _dot_nt(triT, part), _split3(daT)) + dlast)
        ddA = jnp.where(q["head"], ddA, 0.0)
        ddt = ddA * q["a128"] + _gather_heads(dxt * xs, E)
        ddt_raw = jnp.where(q["head"], ddt * _sigmoid(q["pre"]), 0.0)
        ddtk_ref[0] = ddt_raw
        dxs = dxt * q["dt_f"] + dsk_f * dY
        dxbc_ref[0] = jnp.concatenate([dxs] + dbs + dcs, axis=1).astype(BF16)
        dvec_ref[0:1, :] += _colsum(ddt_raw)
        dvec_ref[1:2, :] += _colsum(ddA * q["dt"]) * q["a128"]

        @pl.when((pl.program_id(0) == B - 1) & (pl.program_id(1) == nc - 1))
        def _():
            dvec_ref[2:3, :] = _gather_heads(jnp.broadcast_to(dskc_ref[...], (8, D_SSD)), E)[0:1, :]

    rev = lambda n: pl.BlockSpec((1, L, n), lambda b, c: (b, nc - 1 - c, 0))
    vec = pl.BlockSpec((1, LANES), lambda b, c: (0, 0))
    sd = jax.ShapeDtypeStruct
    return pl.pallas_call(
        body, grid=(B, nc), name="ssd_bwd",
        in_specs=[rev(D_CONV), rev(LANES), rev(D_SSD), rev(D_SSD),
                  pl.BlockSpec((1, 1, D_SSD, SSD_STATE), lambda b, c: (b, nc - 1 - c, 0, 0)), rev(D_SSD), vec, vec,
                  pl.BlockSpec((1, D_SSD), lambda b, c: (0, 0)),
                  pl.BlockSpec((1, D_SSD), lambda b, c: (0, 0)), pl.BlockSpec((LANES, D_SSD), lambda b, c: (0, 0))],
        out_specs=[rev(D_CONV), rev(LANES), rev(D_SSD), pl.BlockSpec((1, D_SSD), lambda b, c: (0, 0)),
                   pl.BlockSpec((8, LANES), lambda b, c: (0, 0))],
        out_shape=[sd((B, S, D_CONV), BF16), sd((B, S, LANES), F32), sd((B, S, D_SSD), BF16), sd((1, D_SSD), F32),
                   sd((8, LANES), F32)],
        scratch_shapes=[pltpu.VMEM((D_SSD, SSD_STATE), F32), pltpu.VMEM((1, D_SSD), F32)],
        compiler_params=_cparams(("arbitrary", "arbitrary")),
    )(xbc, dtk, z, y, prev, dys, dtb, alog, dsk, nw, expand)


def _rope_tables(pos_ref, invf_ref, place_ref):
    ang = invf_ref[...] * pos_ref[0].astype(F32)
    place = place_ref[...]
    cosf = 1.0 + _sum3(lambda part: _dot_tn(part, place), _split3(jnp.cos(ang) - 1.0))
    sinf = _sum3(lambda part: _dot_tn(part, place), _split3(jnp.sin(ang)))
    return cosf, sinf


def _rot(u):
    lane = lax.broadcasted_iota(jnp.int32, u.shape, 1)
    first = (lane >= QK_NOPE) & (lane < QK_NOPE + QK_ROPE // 2)
    second = (lane >= QK_NOPE + QK_ROPE // 2) & (lane < QK_DIM)
    return jnp.where(first, -pltpu.roll(u, LANES - QK_ROPE // 2, 1), jnp.where(second, pltpu.roll(u, QK_ROPE // 2, 1), 0.0))


def _rope_lanes(shape):
    lane = lax.broadcasted_iota(jnp.int32, shape, 1)
    return (lane >= QK_NOPE) & (lane < QK_DIM)


def _mla_prep(cq, ckv, dtk, pos, qw, kvw, wuq, wukv, invf, place):
    T = cq.shape[0]
    tm = min(TOKEN_TILE, T)
    scale = 1.0 / math.sqrt(QK_DIM)
    HW = MLA_HEADS * HEAD_LANES

    def body(cq_ref, ckv_ref, dtk_ref, pos_ref, qw_ref, kvw_ref, wuq_ref, wukv_ref, invf_ref, place_ref, q_ref, k_ref, v_ref,
             cos_ref, sin_ref):
        xh, _ = _rms_stats(cq_ref[...])
        qv = _dot((xh * qw_ref[...]).astype(BF16), wuq_ref[...])
        xh, _ = _rms_stats(ckv_ref[...])
        kv = _dot((xh * kvw_ref[...]).astype(BF16), wukv_ref[...])
        cosf, sinf = _rope_tables(pos_ref, invf_ref, place_ref)
        cos_ref[...] = cosf
        sin_ref[...] = sinf
        rope = lambda u: u * cosf + _rot(u) * sinf
        dtkv = dtk_ref[...]
        kr = rope(jnp.where(_rope_lanes(dtkv.shape), dtkv, 0.0))
        for h in range(MLA_HEADS):
            cols = slice(h * HEAD_LANES, (h + 1) * HEAD_LANES)
            q_ref[:, cols] = (rope(qv[:, cols]) * scale).astype(BF16)
            k_ref[:, cols] = (kv[:, cols] + kr).astype(BF16)
        v_ref[...] = kv[:, HW:].astype(BF16)

    rows = lambda n: pl.BlockSpec((tm, n), lambda i: (i, 0))
    return pl.pallas_call(
        body, grid=(T // tm,), name="mla_prep",
        in_specs=[rows(Q_LORA), rows(KV_LORA), rows(LANES), pl.BlockSpec((1, 1, tm), lambda i: (i, 0, 0)),
                  _resident((1, Q_LORA)), _resident((1, KV_LORA)), _resident((Q_LORA, HW)), _resident((KV_LORA, 2 * HW)),
                  _resident((QK_ROPE // 2, 1)), _resident((QK_ROPE // 2, LANES))],
        out_specs=[rows(HW), rows(HW), rows(HW), rows(LANES), rows(LANES)],
        out_shape=[jax.ShapeDtypeStruct((T, HW), BF16)] * 3 + [jax.ShapeDtypeStruct((T, LANES), F32)] * 2,
        compiler_params=_cparams(("arbitrary",)),
    )(cq, ckv, dtk, pos.reshape(T // tm, 1, tm), qw, kvw, wuq, wukv, invf, place)


def _mla_prep_bwd(dq, dk, dv, cq, ckv, cos_t, sin_t, qw, kvw, wuq, wukv):
    T = cq.shape[0]
    tm = min(TOKEN_TILE, T)
    scale = 1.0 / math.sqrt(QK_DIM)
    HW = MLA_HEADS * HEAD_LANES

    def body(dq_ref, dk_ref, dv_ref, cq_ref, ckv_ref, cos_ref, sin_ref, qw_ref, kvw_ref, wuq_ref, wukv_ref,
             dcq_ref, dckv_ref, ddtk_ref, qn_ref, kvn_ref, dqo_ref, dkvo_ref, dqw_ref, dkvw_ref):
        @pl.when(pl.program_id(0) == 0)
        def _():
            dqw_ref[...] = jnp.zeros_like(dqw_ref)
            dkvw_ref[...] = jnp.zeros_like(dkvw_ref)

        cosf, sinf = cos_ref[...], sin_ref[...]
        unrope = lambda d: d * cosf - _rot(d * sinf)
        dkr = jnp.zeros((tm, LANES), F32)
        nope = lax.broadcasted_iota(jnp.int32, (tm, LANES), 1) < QK_NOPE
        for h in range(MLA_HEADS):
            cols = slice(h * HEAD_LANES, (h + 1) * HEAD_LANES)
            dqo_ref[:, cols] = unrope(dq_ref[:, cols] * scale).astype(BF16)
            dkh = dk_ref[:, cols]
            dkr = dkr + jnp.where(_rope_lanes(dkh.shape), dkh, 0.0)
            dkvo_ref[:, cols] = jnp.where(nope, dkh, 0.0).astype(BF16)
        dkvo_ref[:, HW:] = dv_ref[...].astype(BF16)
        ddtk_ref[...] = unrope(dkr)
        xh, r = _rms_stats(cq_ref[...])
        qn_ref[...] = (xh * qw_ref[...]).astype(BF16)
        dx, dw_rows = _rms_bwd(_dot_nt(dqo_ref[...], wuq_ref[...]), xh, r, qw_ref[...])
        dcq_ref[...] = dx
        dqw_ref[...] += _colsum(dw_rows)
        xh, r = _rms_stats(ckv_ref[...])
        kvn_ref[...] = (xh * kvw_ref[...]).astype(BF16)
        dx, dw_rows = _rms_bwd(_dot_nt(dkvo_ref[...], wukv_ref[...]), xh, r, kvw_ref[...])
        dckv_ref[...] = dx
        dkvw_ref[...] += _colsum(dw_rows)

    rows = lambda n: pl.BlockSpec((tm, n), lambda i: (i, 0))
    sd = jax.ShapeDtypeStruct
    return pl.pallas_call(
        body, grid=(T // tm,), name="mla_prep_bwd",
        in_specs=[rows(HW), rows(HW), rows(HW), rows(Q_LORA), rows(KV_LORA), rows(LANES), rows(LANES), _resident((1, Q_LORA)),
                  _resident((1, KV_LORA)), _resident((Q_LORA, HW)), _resident((KV_LORA, 2 * HW))],
        out_specs=[rows(Q_LORA), rows(KV_LORA), rows(LANES), rows(Q_LORA), rows(KV_LORA), rows(HW), rows(2 * HW),
                   pl.BlockSpec((1, Q_LORA), lambda i: (0, 0)), pl.BlockSpec((1, KV_LORA), lambda i: (0, 0))],
        out_shape=[sd((T, Q_LORA), F32), sd((T, KV_LORA), F32), sd((T, LANES), F32), sd((T, Q_LORA), BF16),
                   sd((T, KV_LORA), BF16), sd((T, HW), BF16), sd((T, 2 * HW), BF16), sd((1, Q_LORA), F32),
                   sd((1, KV_LORA), F32)],
        compiler_params=_cparams(("arbitrary",)),
    )(dq, dk, dv, cq, ckv, cos_t, sin_t, qw, kvw, wuq, wukv)


def _causal_mask(t):
    row = lax.broadcasted_iota(jnp.int32, (t, t), 0)
    col = lax.broadcasted_iota(jnp.int32, (t, t), 1)
    return col <= row


def _attn_fwd(q, k, v):
    B, S, HW = q.shape
    H = HW // HEAD_LANES
    t = min(ATTN_FWD_Q_TILE, S)
    tk = min(ATTN_FWD_KV_TILE, t)
    nq = S // t
    per = t // tk

    pair = 2
    pw = pair * HEAD_LANES

    def body(q_ref, k_ref, v_ref, o_ref, lse_ref):
        qi = pl.program_id(2)
        lanes = [slice(hh * HEAD_LANES, (hh + 1) * HEAD_LANES) for hh in range(pair)]
        qs = [q_ref[0, :, cols] for cols in lanes]

        def step(j, carry, diag):
            sl = pl.ds(pl.multiple_of(j * tk, tk), tk)
            out = []
            for qv, cols, (m, l, acc) in zip(qs, lanes, carry):
                s = _dot_nt(qv, k_ref[0, sl, cols])
                if diag is not None:
                    row = lax.broadcasted_iota(jnp.int32, (t, tk), 0)
                    col = lax.broadcasted_iota(jnp.int32, (t, tk), 1)
                    s = jnp.where(col + diag * tk <= row, s, -1e30)
                m_new = jnp.maximum(m, jnp.max(s, axis=-1, keepdims=True))
                alpha = jnp.exp(m - m_new)
                p = jnp.exp(s - m_new)
                l = alpha * l + jnp.sum(p, axis=-1, keepdims=True)
                acc = alpha * acc + _dot(p.astype(BF16), v_ref[0, sl, cols])
                out.append((m_new, l, acc))
            return tuple(out)

        init = tuple((jnp.full((t, 1), -1e30, F32), jnp.zeros((t, 1), F32), jnp.zeros((t, HEAD_LANES), F32))
                     for _ in range(pair))
        carry = lax.fori_loop(0, qi * per, lambda j, c: step(j, c, None), init)
        for d in range(per):
            carry = step(qi * per + d, carry, d)
        for hh, (m, l, acc) in enumerate(carry):
            o_ref[0, :, lanes[hh]] = (acc / l).astype(BF16)
            lse_ref[0, hh] = m + jnp.log(l)

    return pl.pallas_call(
        body, grid=(B, H // pair, nq), name="attn_fwd",
        in_specs=[pl.BlockSpec((1, t, pw), lambda b, h, i: (b, i, h)),
                  pl.BlockSpec((1, S, pw), lambda b, h, i: (b, 0, h)),
                  pl.BlockSpec((1, S, pw), lambda b, h, i: (b, 0, h))],
        out_specs=[pl.BlockSpec((1, t, pw), lambda b, h, i: (b, i, h)),
                   pl.BlockSpec((1, pair, t, 1), lambda b, h, i: (b, h, i, 0))],
        out_shape=[jax.ShapeDtypeStruct((B, S, HW), BF16), jax.ShapeDtypeStruct((B, H, S, 1), F32)],
        compiler_params=_cparams(("arbitrary", "arbitrary", "arbitrary")),
    )(q, k, v)


def _attn_bwd(q, k, v, o, do, lse):
    B, S, HW = q.shape
    H = HW // HEAD_LANES
    t = min(ATTN_BWD_TILE, S)
    nq = S // t

    def body(q_ref, k_ref, v_ref, o_ref, do_ref, lse_ref, dq_ref, dk_ref, dv_ref):
        j = pl.program_id(2)

        @pl.when(j == 0)
        def _():
            dq_ref[...] = jnp.zeros_like(dq_ref)

        kj = k_ref[0]
        vj = v_ref[0]

        def step(i, carry, masked):
            dk, dv = carry
            sl = pl.ds(pl.multiple_of(i * t, t), t)
            qi = q_ref[0, sl, :]
            doi = do_ref[0, sl, :]
            s = _dot_nt(qi, kj)
            if masked:
                s = jnp.where(_causal_mask(t), s, -1e30)
            p = jnp.exp(s - lse_ref[0, 0, sl, :])
            dv = dv + _dot_tn(p.astype(BF16), doi)
            dp = _dot_nt(doi, vj)
            delta = jnp.sum(doi.astype(F32) * o_ref[0, sl, :].astype(F32), axis=-1, keepdims=True)
            dsb = (p * (dp - delta)).astype(BF16)
            dk = dk + _dot_tn(dsb, qi)
            dq_ref[0, sl, :] += _dot(dsb, kj)
            return dk, dv

        zero = jnp.zeros((t, HEAD_LANES), F32)
        carry = step(j, (zero, zero), True)
        dk, dv = lax.fori_loop(j + 1, nq, lambda i, c: step(i, c, False), carry)
        dk_ref[0] = dk
        dv_ref[0] = dv

    full = pl.BlockSpec((1, S, HEAD_LANES), lambda b, h, j: (b, 0, h))
    tile = pl.BlockSpec((1, t, HEAD_LANES), lambda b, h, j: (b, j, h))
    sd = jax.ShapeDtypeStruct
    return pl.pallas_call(
        body, grid=(B, H, nq), name="attn_bwd",
        in_specs=[full, tile, tile, full, full, pl.BlockSpec((1, 1, S, 1), lambda b, h, j: (b, h, 0, 0))],
        out_specs=[full, tile, tile],
        out_shape=[sd((B, S, HW), F32), sd((B, S, HW), F32), sd((B, S, HW), F32)],
        compiler_params=_cparams(("arbitrary", "arbitrary", "arbitrary")),
    )(q, k, v, o, do, lse)


def _mix_out(x1, yssd, o, mw, wout, g, seq):
    T, D = x1.shape
    tm = min(TOKEN_TILE, seq)
    tps = seq // tm

    def body(x_ref, ys_ref, o_ref, mw_ref, w_ref, g_ref, xo_ref, m_ref, yc_ref):
        xh, _ = _rms_stats(o_ref[...].astype(F32))
        ycat = jnp.concatenate([ys_ref[...], (xh * mw_ref[...]).astype(BF16)], axis=1)
        m = _dot(ycat, w_ref[...])
        xo_ref[...] = x_ref[...] + g_ref[0] * m
        m_ref[...] = m.astype(BF16)
        yc_ref[...] = ycat

    rows = lambda n: pl.BlockSpec((tm, n), lambda i: (i, 0))
    perb = pl.BlockSpec((1, 1, D), lambda i: (i // tps, 0, 0))
    sd = jax.ShapeDtypeStruct
    return pl.pallas_call(
        body, grid=(T // tm,), name="mix_out",
        in_specs=[rows(D), rows(D_SSD), rows(D_MLA), _resident((1, D_MLA)), _resident((D_SSD + D_MLA, D)), perb],
        out_specs=[rows(D), rows(D), rows(D_SSD + D_MLA)],
        out_shape=[sd((T, D), F32), sd((T, D), BF16), sd((T, D_SSD + D_MLA), BF16)],
        compiler_params=_cparams(("arbitrary",)),
    )(x1, yssd, o, mw, wout, g)


def _mix_out_bwd(dx2, m, o, mw, wout, g, seq):
    T, D = dx2.shape
    B = T // seq
    tm = min(TOKEN_TILE, seq)
    tps = seq // tm

    def body(dx_ref, m_ref, o_ref, mw_ref, w_ref, g_ref, dys_ref, do_ref, dm_ref, dg_ref, dmw_ref):
        i = pl.program_id(0)

        @pl.when(i % tps == 0)
        def _():
            dg_ref[...] = jnp.zeros_like(dg_ref)

        @pl.when(i == 0)
        def _():
            dmw_ref[...] = jnp.zeros_like(dmw_ref)

        dxv = dx_ref[...]
        dg_ref[0] += _colsum(dxv * m_ref[...].astype(F32))
        dmb = (g_ref[0] * dxv).astype(BF16)
        dm_ref[...] = dmb
        dycat = _dot_nt(dmb, w_ref[...])
        dys_ref[...] = dycat[:, :D_SSD].astype(BF16)
        xh, r = _rms_stats(o_ref[...].astype(F32))
        dx, dw_rows = _rms_bwd(dycat[:, D_SSD:], xh, r, mw_ref[...])
        do_ref[...] = dx.astype(BF16)
        dmw_ref[...] += _colsum(dw_rows)

    rows = lambda n: pl.BlockSpec((tm, n), lambda i: (i, 0))
    perb = pl.BlockSpec((1, 1, D), lambda i: (i // tps, 0, 0))
    sd = jax.ShapeDtypeStruct
    return pl.pallas_call(
        body, grid=(T // tm,), name="mix_out_bwd",
        in_specs=[rows(D), rows(D), rows(D_MLA), _resident((1, D_MLA)), _resident((D_SSD + D_MLA, D)), perb],
        out_specs=[rows(D_SSD), rows(D_MLA), rows(D), perb, pl.BlockSpec((1, D_MLA), lambda i: (0, 0))],
        out_shape=[sd((T, D_SSD), BF16), sd((T, D_MLA), BF16), sd((T, D), BF16), sd((B, 1, D), F32), sd((1, D_MLA), F32)],
        compiler_params=_cparams(("arbitrary",)),
    )(dx2, m, o, mw, wout, g)


def _win_to_kernel(w):
    z0 = jnp.zeros((48, w.shape[1]), w.dtype)
    z1 = jnp.zeros((32, w.shape[1]), w.dtype)
    return jnp.concatenate([w[:2560], w[2576:3216], w[2560:2576], z0, w[3216:3248], z1], axis=0)


def _win_from_kernel(g):
    return jnp.concatenate([g[:2560], g[3200:3216], g[2560:3200], g[3264:3296]], axis=0)


def _wuq_to_kernel(w):
    w = w.reshape(Q_LORA, MLA_HEADS, QK_DIM)
    return jnp.pad(w, ((0, 0), (0, 0), (0, HEAD_LANES - QK_DIM))).reshape(Q_LORA, MLA_HEADS * HEAD_LANES)


def _wuq_from_kernel(g):
    return g.reshape(Q_LORA, MLA_HEADS, HEAD_LANES)[:, :, :QK_DIM].reshape(Q_LORA, MLA_HEADS * QK_DIM)


def _wukv_to_kernel(w):
    w = w.reshape(KV_LORA, MLA_HEADS, QK_NOPE + V_HEAD)
    kp = jnp.pad(w[:, :, :QK_NOPE], ((0, 0), (0, 0), (0, HEAD_LANES - QK_NOPE)))
    return jnp.concatenate([kp.reshape(KV_LORA, -1), w[:, :, QK_NOPE:].reshape(KV_LORA, -1)], axis=1)


def _wukv_from_kernel(g):
    hw = MLA_HEADS * HEAD_LANES
    kp = g[:, :hw].reshape(KV_LORA, MLA_HEADS, HEAD_LANES)[:, :, :QK_NOPE]
    vp = g[:, hw:].reshape(KV_LORA, MLA_HEADS, V_HEAD)
    return jnp.concatenate([kp, vp], axis=2).reshape(KV_LORA, MLA_HEADS * (QK_NOPE + V_HEAD))


def _lanes16(v):
    return jnp.pad(v.reshape(1, SSD_HEADS), ((0, 0), (0, LANES - SSD_HEADS)))


def _constants():
    e = np.zeros((LANES, D_SSD), np.float32)
    for h in range(SSD_HEADS):
        e[h, h * SSD_HEAD_DIM:(h + 1) * SSD_HEAD_DIM] = 1.0
    inv_freq = ROPE_THETA ** (-jnp.arange(0, QK_ROPE, 2, dtype=F32) / QK_ROPE)
    half = QK_ROPE // 2
    place = np.zeros((half, LANES), np.float32)
    for j in range(half):
        place[j, QK_NOPE + j] = place[j, QK_NOPE + half + j] = 1.0
    return jnp.asarray(e, BF16), inv_freq.reshape(half, 1), jnp.asarray(place, BF16)


def _local_step(x, positions, mod, w, later_weights, small, tgt, on_grads):
    B, S, D = x.shape
    T = B * S
    expand, invf, place = _constants()
    x0 = x.reshape(T, D)
    pos = positions.reshape(T)
    mods = [mod[:, i * D:(i + 1) * D].reshape(B, 1, D) for i in range(N_MOD)]
    sh1, sc1, g1, sh2, sc2, g2, sh3, sc3, g3 = mods
    dtb, alog = _lanes16(small["dt_bias"]), _lanes16(small["a_log"])
    dsk = jnp.repeat(small["d_skip"].reshape(1, SSD_HEADS), SSD_HEAD_DIM, axis=1)

    x1, a1, u1, f1 = _ffn_fwd(x0, small["norm_ffn1"], sh1, sc1, g1, w["ffn1_w_gate"], w["ffn1_w_up"], w["ffn1_w_down"], S, "ffn1_fwd")
    w = {**w, **later_weights(f1)}
    z, xraw, cq, ckv, dtk = _inproj_fwd(x1, small["norm_mix"], sh2, sc2, w["w_in"], S)
    xraw3 = xraw.reshape(B, S, D_CONV)
    xbc = _conv_fwd(xraw3, small["conv_w"], small["conv_b"])
    dtk3, z3 = dtk.reshape(B, S, LANES), z.reshape(B, S, D_SSD)
    y, yssd, prev = _ssd_fwd(xbc, dtk3, z3, dtb, alog, dsk, small["ssd_norm_w"], expand)
    q, k, v, cos_t, sin_t = _mla_prep(cq, ckv, dtk, pos, small["q_norm_w"], small["kv_norm_w"], w["w_uq"], w["w_ukv"], invf,
                                      place)
    hw = MLA_HEADS * HEAD_LANES
    q3, k3, v3 = q.reshape(B, S, hw), k.reshape(B, S, hw), v.reshape(B, S, hw)
    o3, lse = _attn_fwd(q3, k3, v3)
    o = o3.reshape(T, hw)
    x2, m, ycat = _mix_out(x1, yssd.reshape(T, D_SSD), o, small["mla_norm_w"], w["w_out"], g2, S)
    dx3, a2, u2, f2, loss, d_norm_final = _ffn_fwd(
        x2, small["norm_ffn2"], sh3, sc3, g3, w["ffn2_w_gate"], w["ffn2_w_up"], w["ffn2_w_down"], S, "ffn2_fwd",
        head=(small["norm_final"].reshape(1, D), tgt.reshape(T, D)))

    gw, gs = {}, {}
    dx2, h3, s3, df3, da3, du3, dsh3, dsc3, dg3, gs["norm_ffn2"] = _ffn_bwd(
        dx3, x2, small["norm_ffn2"], sh3, sc3, g3, a2, u2, f2, w["ffn2_w_gate"], w["ffn2_w_up"], w["ffn2_w_down"], S, "ffn2_bwd")
    gw["ffn2_w_gate"], gw["ffn2_w_up"], gw["ffn2_w_down"] = _ffn_wgrad(h3, s3, df3, da3, du3, "ffn2_wgrad")
    g2 = g2 + on_grads(("ffn2_w_gate", "ffn2_w_up", "ffn2_w_down"), gw)

    dys, do, dm, dg2, gs["mla_norm_w"] = _mix_out_bwd(dx2, m, o, small["mla_norm_w"], w["w_out"], g2, S)
    gw["w_out"] = _mm_tn(ycat, dm, 512, "dwout")

    dq3, dk3, dv3 = _attn_bwd(q3, k3, v3, o3, do.reshape(B, S, hw), lse)
    dcq, dckv, ddtk_b, qn, kvn, dqb, dkvb, gs["q_norm_w"], gs["kv_norm_w"] = _mla_prep_bwd(
        dq3.reshape(T, hw), dk3.reshape(T, hw), dv3.reshape(T, hw), cq, ckv, cos_t, sin_t, small["q_norm_w"],
        small["kv_norm_w"], w["w_uq"], w["w_ukv"])
    gw["w_uq"] = _mm_tn(qn, dqb, 512, "dwuq")
    gw["w_ukv"] = _mm_tn(kvn, dkvb, 1024, "dwukv")

    dxbc, ddtk_a, dz, gs["ssd_norm_w"], dvec = _ssd_bwd(
        xbc, dtk3, z3, y, prev, dys.reshape(B, S, D_SSD), dtb, alog, dsk, small["ssd_norm_w"], expand)
    gs["dt_bias"], gs["a_log"], gs["d_skip"] = dvec[0:1, :SSD_HEADS], dvec[1:2, :SSD_HEADS], dvec[2:3, :SSD_HEADS]
    dxraw, gs["conv_w"], gs["conv_b"] = _conv_bwd(dxbc, xraw3, small["conv_w"], small["conv_b"])
    dx1, h2, dproj, dsh2, dsc2, gs["norm_mix"] = _inproj_bwd(
        dx2, x1, small["norm_mix"], sh2, sc2, w["w_in"], dz.reshape(T, D_SSD), dxraw.reshape(T, D_CONV), dcq, dckv,
        ddtk_a.reshape(T, LANES), ddtk_b, S)
    gw["w_in"] = _mm_tn(dproj, h2, 512, "dwin")
    g1 = g1 + on_grads(("w_in", "w_uq", "w_ukv", "w_out"), gw)

    dx0, h1, s1, df1, da1, du1, dsh1, dsc1, dg1, gs["norm_ffn1"] = _ffn_bwd(
        dx1, x0, small["norm_ffn1"], sh1, sc1, g1, a1, u1, f1, w["ffn1_w_gate"], w["ffn1_w_up"], w["ffn1_w_down"], S, "ffn1_bwd")
    gw["ffn1_w_gate"], gw["ffn1_w_up"], gw["ffn1_w_down"] = _ffn_wgrad(h1, s1, df1, da1, du1, "ffn1_wgrad")
    gs["norm_final"] = d_norm_final
    dmod = jnp.concatenate([t.reshape(B, D) for t in (dsh1, dsc1, dg1, dsh2, dsc2, dg2, dsh3, dsc3, dg3)], axis=1)
    return loss, dx0.reshape(B, S, D), gw, dmod, gs


HBM_SPEC = pl.BlockSpec(memory_space=pltpu.HBM)
VMEM_SPEC = pl.BlockSpec(memory_space=pltpu.VMEM)


def _place():
    return lax.axis_index("x"), lax.axis_index("y"), lax.axis_index("c")


def _other_chips(mx, my):
    return [(1 - mx, my), (mx, 1 - my), (1 - mx, 1 - my)]


def _remote(src, dst, send_sem, recv_sem, to):
    return pltpu.make_async_remote_copy(src_ref=src, dst_ref=dst, send_sem=send_sem, recv_sem=recv_sem,
                                        device_id=to, device_id_type=MESH)


def _all_gather_small(xa, name):
    r, n = xa.shape

    def body(x_ref, o_ref, token, send_sems, recv_sems):
        mx, my, mc = _place()
        me = 4 * mx + 2 * my + mc
        token[...] = jnp.zeros_like(token)
        o_ref[pl.ds(me, 1)] = x_ref[...][None]
        sends = []
        for k in range(1, N_DEV):
            peer = (mx ^ (k >> 2), my ^ ((k >> 1) & 1), mc ^ (k & 1))
            cp = _remote(x_ref, o_ref.at[me], send_sems.at[k - 1], recv_sems.at[k - 1], peer)
            cp.start()
            sends.append(cp)
        for k in range(1, N_DEV):
            peer = (mx ^ (k >> 2), my ^ ((k >> 1) & 1), mc ^ (k & 1))
            slot = 4 * peer[0] + 2 * peer[1] + peer[2]
            _remote(x_ref, o_ref.at[slot], send_sems.at[k - 1], recv_sems.at[k - 1], peer).wait_recv()
        for cp in sends:
            cp.wait_send()

    return pl.pallas_call(
        body, name=name, in_specs=[VMEM_SPEC], out_specs=[VMEM_SPEC, VMEM_SPEC],
        out_shape=[jax.ShapeDtypeStruct((N_DEV, r, n), xa.dtype), jax.ShapeDtypeStruct((8, LANES), F32)],
        scratch_shapes=[pltpu.SemaphoreType.DMA((N_DEV - 1,)), pltpu.SemaphoreType.DMA((N_DEV - 1,))],
        compiler_params=pltpu.CompilerParams(vmem_limit_bytes=VMEM_LIMIT),
    )(xa)


def _halves_by_rows(shape):
    return (shape[-2] // 2) % 16 == 0


def _half_shape(shape):
    r, c = shape[-2:]
    return tuple(shape[:-2]) + ((r // 2, c) if _halves_by_rows(shape) else (r, c // 2))


def _half_index(shape, hc):
    r, c = shape[-2:]
    if _halves_by_rows(shape):
        return (pl.ds(pl.multiple_of(hc * (r // 2), 16), r // 2), slice(None))
    return (slice(None), pl.ds(pl.multiple_of(hc * (c // 2), LANES), c // 2))


def _half(ref, hc, lead=None):
    idx = _half_index(ref.shape, hc)
    return ref.at[idx] if lead is None else ref.at[(lead,) + idx]


def _gather_weights(shards):
    n = len(shards)

    def body(*refs):
        w_refs, o_refs, token = refs[:n], refs[n:2 * n], refs[2 * n]
        send_sems, recv_sems, stage_sems = refs[2 * n + 1:2 * n + 4]
        stages = refs[2 * n + 4:]
        mx, my, mc = _place()
        chip = 2 * mx + my
        others = _other_chips(mx, my)
        sibling = (mx, my, 1 - mc)
        token[...] = jnp.zeros_like(token)
        stage_in = [pltpu.make_async_copy(w, st, stage_sems.at[0, i]) for i, (w, st) in enumerate(zip(w_refs, stages))]
        for cp in stage_in:
            cp.start()
        first = []
        for i, (w, o) in enumerate(zip(w_refs, o_refs)):
            for k, (cx, cy) in enumerate(others):
                first.append(_remote(_half(w, mc), _half(o, mc, chip), send_sems.at[i, k],
                                     recv_sems.at[i, k], (cx, cy, mc)))
                first[-1].start()
        stage_out = []
        for i, (st, o) in enumerate(zip(stages, o_refs)):
            stage_in[i].wait()
            stage_out.append(pltpu.make_async_copy(st, o.at[chip], stage_sems.at[1, i]))
            stage_out[-1].start()
        passed = []
        for i, (w, o) in enumerate(zip(w_refs, o_refs)):
            for k, (cx, cy) in enumerate(others):
                landed = _half(o, mc, 2 * cx + cy)
                _remote(landed, landed, send_sems.at[i, k], recv_sems.at[i, k], (cx, cy, mc)).wait_recv()
                passed.append(_remote(landed, landed, send_sems.at[i, 3 + k], recv_sems.at[i, 3 + k], sibling))
                passed[-1].start()
        for i, (w, o) in enumerate(zip(w_refs, o_refs)):
            for k, (cx, cy) in enumerate(others):
                there = _half(o, 1 - mc, 2 * cx + cy)
                _remote(there, there, send_sems.at[i, 3 + k], recv_sems.at[i, 3 + k], sibling).wait_recv()
        for cp in first + passed:
            cp.wait_send()
        for cp in stage_out:
            cp.wait()

    out = pl.pallas_call(
        body, name="gather_weights", in_specs=[HBM_SPEC] * n, out_specs=[HBM_SPEC] * n + [VMEM_SPEC],
        out_shape=[jax.ShapeDtypeStruct((N_CHIPS,) + s.shape, s.dtype) for s in shards] + [jax.ShapeDtypeStruct((8, LANES), F32)],
        scratch_shapes=[pltpu.SemaphoreType.DMA((n, 6)), pltpu.SemaphoreType.DMA((n, 6)), pltpu.SemaphoreType.DMA((2, n))]
        + [pltpu.VMEM(s.shape, s.dtype) for s in shards],
        compiler_params=pltpu.CompilerParams(vmem_limit_bytes=VMEM_LIMIT),
    )(*shards)
    return out[:n], out[n]


SEM_SPEC = pl.BlockSpec(memory_space=pltpu.SEMAPHORE)
ANY_SPEC = pl.BlockSpec(memory_space=pl.ANY)
DATAFLOW = pltpu.SideEffectType.DATAFLOW_SIDE_EFFECTING


def _hbm(arr):
    return pltpu.with_memory_space_constraint(arr, pltpu.HBM)


def _gather_start(shards):
    n = len(shards)

    def body(*refs):
        w_refs, land_refs, send_sems, recv_sems, token = refs[:n], refs[n:2 * n], refs[2 * n], refs[2 * n + 1], refs[-1]
        mx, my, mc = _place()
        chip = 2 * mx + my
        for i, (w, land) in enumerate(zip(w_refs, land_refs)):
            for k, (cx, cy) in enumerate(_other_chips(mx, my)):
                _remote(_half(w, mc), _half(land, mc, chip), send_sems.at[3 * i + k],
                        recv_sems.at[3 * i + k], (cx, cy, mc)).start()
        token[...] = jnp.zeros_like(token)

    lands = [lax.empty((N_CHIPS,) + s.shape, s.dtype) for s in shards]
    out = pl.pallas_call(
        body, name="gather_start",
        out_shape=(pltpu.SemaphoreType.DMA((3 * n,)), pltpu.SemaphoreType.DMA((3 * n,)),
                   *[pltpu.HBM(s.shape, s.dtype) for s in shards], *[pltpu.HBM(l.shape, l.dtype) for l in lands],
                   jax.ShapeDtypeStruct((8, LANES), F32)),
        in_specs=[HBM_SPEC] * (2 * n), out_specs=(SEM_SPEC, SEM_SPEC, *[HBM_SPEC] * (2 * n), VMEM_SPEC),
        input_output_aliases={i: 2 + i for i in range(2 * n)},
        compiler_params=pltpu.CompilerParams(has_side_effects=DATAFLOW),
    )(*[_hbm(s) for s in shards], *[_hbm(l) for l in lands])
    return out[0], out[1], out[2:2 + n], out[2 + n:2 + 2 * n], out[-1]


def _gather_wait(send_sems, recv_sems, shards, lands, after):
    n = len(shards)

    def body(*refs):
        w_refs, land_refs, send_sems, recv_sems = refs[:n], refs[n:2 * n], refs[2 * n], refs[2 * n + 1]
        mx, my, mc = _place()
        for i, (w, land) in enumerate(zip(w_refs, land_refs)):
            for k, (cx, cy) in enumerate(_other_chips(mx, my)):
                cp = _remote(_half(w, mc), _half(land, mc, 2 * cx + cy), send_sems.at[3 * i + k],
                             recv_sems.at[3 * i + k], (cx, cy, mc))
                cp.wait_send()
                cp.wait_recv()

    out = pl.pallas_call(
        body, name="gather_wait",
        out_shape=(*[pltpu.HBM(s.shape, s.dtype) for s in shards], *[pltpu.HBM(l.shape, l.dtype) for l in lands]),
        in_specs=[HBM_SPEC] * (2 * n) + [SEM_SPEC, SEM_SPEC, ANY_SPEC], out_specs=tuple([HBM_SPEC] * (2 * n)),
        input_output_aliases={i: i for i in range(2 * n)},
        compiler_params=pltpu.CompilerParams(has_side_effects=DATAFLOW),
    )(*shards, *lands, send_sems, recv_sems, after)
    return out[n:]


def _gather_finish(shards, lands):
    n = len(shards)

    def body(*refs):
        w_refs, land_refs, o_refs = refs[:n], refs[n:2 * n], refs[2 * n:3 * n]
        send_sems, recv_sems, stage_sems = refs[3 * n:3 * n + 3]
        stages = refs[3 * n + 3:]
        mx, my, mc = _place()
        chip = 2 * mx + my
        others = _other_chips(mx, my)
        sibling = (mx, my, 1 - mc)
        stage_in = [pltpu.make_async_copy(w, st, stage_sems.at[0, i]) for i, (w, st) in enumerate(zip(w_refs, stages))]
        for cp in stage_in:
            cp.start()
        passed = []
        for i, (w, o) in enumerate(zip(w_refs, o_refs)):
            for k, (cx, cy) in enumerate(others):
                landed = _half(o, mc, 2 * cx + cy)
                passed.append(_remote(landed, landed, send_sems.at[i, k], recv_sems.at[i, k], sibling))
                passed[-1].start()
        stage_out = []
        for i, (st, o) in enumerate(zip(stages, o_refs)):
            stage_in[i].wait()
            stage_out.append(pltpu.make_async_copy(st, o.at[chip], stage_sems.at[1, i]))
            stage_out[-1].start()
        for i, (w, o) in enumerate(zip(w_refs, o_refs)):
            for k, (cx, cy) in enumerate(others):
                there = _half(o, 1 - mc, 2 * cx + cy)
                _remote(there, there, send_sems.at[i, k], recv_sems.at[i, k], sibling).wait_recv()
        for cp in passed:
            cp.wait_send()
        for cp in stage_out:
            cp.wait()

    return pl.pallas_call(
        body, name="gather_finish", in_specs=[HBM_SPEC] * (2 * n), out_specs=[HBM_SPEC] * n,
        out_shape=[jax.ShapeDtypeStruct(l.shape, l.dtype) for l in lands],
        input_output_aliases={n + i: i for i in range(n)},
        scratch_shapes=[pltpu.SemaphoreType.DMA((n, 3)), pltpu.SemaphoreType.DMA((n, 3)), pltpu.SemaphoreType.DMA((2, n))]
        + [pltpu.VMEM(s.shape, s.dtype) for s in shards],
        compiler_params=pltpu.CompilerParams(vmem_limit_bytes=VMEM_LIMIT),
    )(*shards, *lands)


def _scatter_start(ss, tag):
    n = len(ss)

    def body(*refs):
        s_refs, land_refs, send_sems, recv_sems, token = refs[:n], refs[n:2 * n], refs[2 * n], refs[2 * n + 1], refs[-1]
        mx, my, mc = _place()
        chip = 2 * mx + my
        for i, (s, land) in enumerate(zip(s_refs, land_refs)):
            for k, (cx, cy) in enumerate(_other_chips(mx, my)):
                _remote(s.at[2 * cx + cy], land.at[chip], send_sems.at[3 * i + k], recv_sems.at[3 * i + k],
                        (cx, cy, mc)).start()
        token[...] = jnp.zeros_like(token)

    lands = [lax.empty(s.shape, s.dtype) for s in ss]
    out = pl.pallas_call(
        body, name="scatter_start_" + tag,
        out_shape=(pltpu.SemaphoreType.DMA((3 * n,)), pltpu.SemaphoreType.DMA((3 * n,)),
                   *[pltpu.HBM(s.shape, s.dtype) for s in ss], *[pltpu.HBM(l.shape, l.dtype) for l in lands],
                   jax.ShapeDtypeStruct((8, LANES), F32)),
        in_specs=[HBM_SPEC] * (2 * n), out_specs=(SEM_SPEC, SEM_SPEC, *[HBM_SPEC] * (2 * n), VMEM_SPEC),
        input_output_aliases={i: 2 + i for i in range(2 * n)},
        compiler_params=pltpu.CompilerParams(has_side_effects=DATAFLOW),
    )(*[_hbm(s) for s in ss], *[_hbm(l) for l in lands])
    return out[0], out[1], out[2:2 + n], out[2 + n:2 + 2 * n], out[-1]


def _scatter_wait(send_sems, recv_sems, ss, lands, after, tag):
    n = len(ss)

    def body(*refs):
        s_refs, land_refs, send_sems, recv_sems = refs[:n], refs[n:2 * n], refs[2 * n], refs[2 * n + 1]
        mx, my, mc = _place()
        for i, (s, land) in enumerate(zip(s_refs, land_refs)):
            for k, (cx, cy) in enumerate(_other_chips(mx, my)):
                slot = land.at[2 * cx + cy]
                cp = _remote(s.at[2 * cx + cy], slot, send_sems.at[3 * i + k], recv_sems.at[3 * i + k], (cx, cy, mc))
                cp.wait_send()
                cp.wait_recv()

    out = pl.pallas_call(
        body, name="scatter_wait_" + tag,
        out_shape=(*[pltpu.HBM(s.shape, s.dtype) for s in ss], *[pltpu.HBM(l.shape, l.dtype) for l in lands]),
        in_specs=[HBM_SPEC] * (2 * n) + [SEM_SPEC, SEM_SPEC, ANY_SPEC], out_specs=tuple([HBM_SPEC] * (2 * n)),
        input_output_aliases={i: i for i in range(2 * n)},
        compiler_params=pltpu.CompilerParams(has_side_effects=DATAFLOW),
    )(*ss, *lands, send_sems, recv_sems, after)
    return out[:n], out[n:]


def _swap_halves(gs, after, name):
    n = len(gs)

    def body(*refs):
        g_refs, o_refs, send_sems, recv_sems = refs[:n], refs[n + 1:2 * n + 1], refs[2 * n + 1], refs[2 * n + 2]
        mx, my, mc = _place()
        copies = []
        for i, (g, o) in enumerate(zip(g_refs, o_refs)):
            src = g.at[(slice(None),) + _half_index(g.shape, 1 - mc)]
            copies.append(_remote(src, o, send_sems.at[i], recv_sems.at[i], (mx, my, 1 - mc)))
            copies[-1].start()
        for cp in copies:
            cp.wait()

    return pl.pallas_call(
        body, name=name, in_specs=[HBM_SPEC] * n + [ANY_SPEC], out_specs=[HBM_SPEC] * n,
        out_shape=[jax.ShapeDtypeStruct(_half_shape(g.shape), g.dtype) for g in gs],
        scratch_shapes=[pltpu.SemaphoreType.DMA((n,)), pltpu.SemaphoreType.DMA((n,))],
    )(*gs, after)


def _pair_sum(g, got, core, name):
    hr, hc = _half_shape(g.shape)[1:]
    by_rows = _halves_by_rows(g.shape)

    def body(core_ref, g_ref, got_ref, o_ref):
        o_ref[...] = (g_ref[...] + got_ref[...]).astype(BF16)

    return pl.pallas_call(
        body, name=name,
        grid_spec=pltpu.PrefetchScalarGridSpec(
            num_scalar_prefetch=1, grid=(N_CHIPS,),
            in_specs=[pl.BlockSpec((1, hr, hc), lambda j, core_ref: (j, core_ref[0], 0) if by_rows else (j, 0, core_ref[0])),
                      pl.BlockSpec((1, hr, hc), lambda j, core_ref: (j, 0, 0))],
            out_specs=pl.BlockSpec((1, hr, hc), lambda j, core_ref: (j, 0, 0))),
        out_shape=jax.ShapeDtypeStruct((N_CHIPS, hr, hc), BF16),
        compiler_params=_cparams(("arbitrary",)),
    )(core, g, got)


def _chip_sum(own, got, chip, name):
    _, h, c = own.shape

    def body(chip_ref, a_ref, b_ref, c_ref, d_ref, o_ref):
        o_ref[...] = ((a_ref[0].astype(F32) + b_ref[0].astype(F32)) + c_ref[0].astype(F32)) + d_ref[0].astype(F32)

    slot = lambda flip: pl.BlockSpec((1, h, c), lambda i, chip_ref: (chip_ref[0] ^ flip, 0, 0))
    return pl.pallas_call(
        body, name=name,
        grid_spec=pltpu.PrefetchScalarGridSpec(
            num_scalar_prefetch=1, grid=(1,), in_specs=[slot(0), slot(1), slot(2), slot(3)],
            out_specs=pl.BlockSpec((h, c), lambda i, chip_ref: (0, 0))),
        out_shape=jax.ShapeDtypeStruct((h, c), F32),
        compiler_params=_cparams(("arbitrary",)),
    )(chip, own, got, got, got)


def _join_halves(mine, name):
    n = len(mine)

    def body(*refs):
        m_refs, o_refs, send_sems, recv_sems = refs[:n], refs[n:2 * n], refs[2 * n], refs[2 * n + 1]
        mx, my, mc = _place()
        copies = []
        for i, (m, o) in enumerate(zip(m_refs, o_refs)):
            copies.append(_remote(m, o, send_sems.at[i], recv_sems.at[i], (mx, my, 1 - mc)))
            copies[-1].start()
        for cp in copies:
            cp.wait()

    return pl.pallas_call(
        body, name=name, in_specs=[HBM_SPEC] * n, out_specs=[HBM_SPEC] * n,
        out_shape=[jax.ShapeDtypeStruct(m.shape, m.dtype) for m in mine],
        scratch_shapes=[pltpu.SemaphoreType.DMA((n,)), pltpu.SemaphoreType.DMA((n,))],
    )(*mine)


def _adam_math(w, g, m, v):
    m2 = ADAM_B1 * m + (1.0 - ADAM_B1) * g
    v2 = ADAM_B2 * v + (1.0 - ADAM_B2) * (g * g)
    m_hat = m2 * (1.0 / (1.0 - ADAM_B1 ** ADAM_STEP))
    v_hat = v2 * (1.0 / (1.0 - ADAM_B2 ** ADAM_STEP))
    delta = -ADAM_LR * (m_hat / (jnp.sqrt(v_hat) + ADAM_EPS) + ADAM_WD * w)
    return delta, m2, v2


def _adam(w, g, m, v, name):
    def body(w_ref, g_ref, m_ref, v_ref, d_ref, m2_ref, v2_ref):
        d_ref[...], m2_ref[...], v2_ref[...] = _adam_math(w_ref[...], g_ref[...], m_ref[...], v_ref[...])

    return pl.pallas_call(body, name=name, out_shape=[jax.ShapeDtypeStruct(w.shape, F32)] * 3)(w, g, m, v)


def _adam_halves(w, m, v, mine, theirs, core, name):
    hr, hcols = _half_shape(w.shape)[1:]
    by_rows = _halves_by_rows(w.shape)

    def body(core_ref, w_ref, m_ref, v_ref, mine_ref, theirs_ref, g_ref, d_ref, m2_ref, v2_ref):
        g = jnp.where(pl.program_id(0) == core_ref[0], mine_ref[...], theirs_ref[...])
        g_ref[0] = g
        d_ref[0], m2_ref[0], v2_ref[0] = _adam_math(w_ref[0], g, m_ref[0], v_ref[0])

    half = pl.BlockSpec((1, hr, hcols), lambda hc, core_ref: (0, hc, 0) if by_rows else (0, 0, hc))
    whole = pl.BlockSpec((hr, hcols), lambda hc, core_ref: (0, 0))
    return pl.pallas_call(
        body, name=name,
        grid_spec=pltpu.PrefetchScalarGridSpec(
            num_scalar_prefetch=1, grid=(2,), in_specs=[half, half, half, whole, whole], out_specs=[half] * 4),
        out_shape=[jax.ShapeDtypeStruct(w.shape, F32)] * 4,
        compiler_params=_cparams(("arbitrary",)),
    )(core, w, m, v, mine, theirs)


ADA_COLS = N_MOD * D_MODEL // N_CHIPS


def _ada_fwd(c_all, w_ada, b_cols):
    def body(c_ref, w_ref, b_ref, o_ref):
        cv = c_ref[...]
        act = (cv * _sigmoid(cv)).astype(BF16)
        o_ref[...] = _dot(act, w_ref[...].astype(BF16)) + b_ref[...]

    return pl.pallas_call(
        body, name="ada_fwd", out_shape=jax.ShapeDtypeStruct((c_all.shape[0], ADA_COLS), F32),
        compiler_params=pltpu.CompilerParams(vmem_limit_bytes=VMEM_LIMIT),
    )(c_all, w_ada, b_cols)


def _ada_bwd(c_all, dmod_cols, w, m, v):
    nb = c_all.shape[0]
    tn = 384

    def body(c_ref, d_ref, w_ref, m_ref, v_ref, g_ref, dl_ref, m2_ref, v2_ref):
        cv = c_ref[...]
        act = (cv * _sigmoid(cv)).astype(BF16)
        g = _dot_tn(act, d_ref[...].astype(BF16))
        g_ref[...] = g
        dl_ref[...], m2_ref[...], v2_ref[...] = _adam_math(w_ref[...], g, m_ref[...], v_ref[...])

    blk = pl.BlockSpec((D_MODEL, tn), lambda j: (0, j))
    return pl.pallas_call(
        body, name="ada_bwd", grid=(ADA_COLS // tn,),
        in_specs=[pl.BlockSpec((nb, D_MODEL), lambda j: (0, 0)), pl.BlockSpec((nb, tn), lambda j: (0, j)), blk, blk, blk],
        out_specs=[blk] * 4, out_shape=[jax.ShapeDtypeStruct((D_MODEL, ADA_COLS), F32)] * 4,
        compiler_params=_cparams(("arbitrary",)),
    )(c_all, dmod_cols, w, m, v)


SMALL_NAMES = ("norm_ffn1", "norm_mix", "conv_w", "conv_b", "ssd_norm_w", "q_norm_w", "kv_norm_w", "mla_norm_w",
               "norm_ffn2", "norm_final", "dt_bias", "a_log", "d_skip")
SMALL_SIZES = (1024, 1024, CONV_WIDTH * D_CONV, D_CONV, 1024, Q_LORA, KV_LORA, 1024, 1024, 1024, 16, 16, 16)
SMALL_ROWS = 16
MOD_ROWS = 2 * N_MOD
SEND_ROWS = 40


def _pack_small(parts):
    flat = jnp.concatenate([parts[n].reshape(-1) for n in SMALL_NAMES])
    return jnp.pad(flat, (0, SMALL_ROWS * D_MODEL - flat.shape[0]))


def _unpack_small(flat):
    out, off = {}, 0
    for n, size in zip(SMALL_NAMES, SMALL_SIZES):
        out[n] = flat[off:off + size]
        off += size
    return out


def _small_sum(got):
    def body(g_ref, o_ref):
        bsum = jnp.zeros((N_MOD, D_MODEL), F32)
        ssum = jnp.zeros((SMALL_ROWS, D_MODEL), F32)
        for d in range(N_DEV):
            bsum = bsum + g_ref[d, 0:N_MOD, :] + g_ref[d, N_MOD:MOD_ROWS, :]
            ssum = ssum + g_ref[d, MOD_ROWS:MOD_ROWS + SMALL_ROWS, :]
        o_ref[...] = jnp.concatenate([bsum, ssum, jnp.zeros((32 - N_MOD - SMALL_ROWS, D_MODEL), F32)], axis=0)

    return pl.pallas_call(body, name="small_sum", out_shape=jax.ShapeDtypeStruct((32, D_MODEL), F32))(got)


BIG_NAMES = ("ffn1_w_gate", "ffn1_w_up", "ffn1_w_down", "w_in", "w_uq", "w_ukv", "w_out", "ffn2_w_gate", "ffn2_w_up",
             "ffn2_w_down")
_TO_KERNEL = {"w_in": _win_to_kernel, "w_uq": _wuq_to_kernel, "w_ukv": _wukv_to_kernel}
_FROM_KERNEL = {"w_in": _win_from_kernel, "w_uq": _wuq_from_kernel, "w_ukv": _wukv_from_kernel}


def _columns_joined(w4):
    n, r, c = w4.shape
    return w4.transpose(1, 0, 2).reshape(r, n * c)


def _columns_split(g):
    r, cols = g.shape
    return g.reshape(r, N_CHIPS, cols // N_CHIPS).transpose(1, 0, 2)


def kernel(x, c, positions, w_ada, b_ada, norm_ffn1, ffn1_w_gate, ffn1_w_up, ffn1_w_down, norm_mix, w_in, conv_w, conv_b, dt_bias, a_log, d_skip, ssd_norm_w, q_norm_w, w_uq, kv_norm_w, w_ukv, mla_norm_w, w_out, norm_ffn2, ffn2_w_gate, ffn2_w_up, ffn2_w_down, norm_final, loss_target, m_w_ada, m_b_ada, m_norm_ffn1, m_ffn1_w_gate, m_ffn1_w_up, m_ffn1_w_down, m_norm_mix, m_w_in, m_conv_w, m_conv_b, m_dt_bias, m_a_log, m_d_skip, m_ssd_norm_w, m_q_norm_w, m_w_uq, m_kv_norm_w, m_w_ukv, m_mla_norm_w, m_w_out, m_norm_ffn2, m_ffn2_w_gate, m_ffn2_w_up, m_ffn2_w_down, m_norm_final, v_w_ada, v_b_ada, v_norm_ffn1, v_ffn1_w_gate, v_ffn1_w_up, v_ffn1_w_down, v_norm_mix, v_w_in, v_conv_w, v_conv_b, v_dt_bias, v_a_log, v_d_skip, v_ssd_norm_w, v_q_norm_w, v_w_uq, v_kv_norm_w, v_w_ukv, v_mla_norm_w, v_w_out, v_norm_ffn2, v_ffn2_w_gate, v_ffn2_w_up, v_ffn2_w_down, v_norm_final):
    a = dict(locals())
    held_transposed = ("ffn1_w_gate", "ffn1_w_up", "ffn2_w_gate", "ffn2_w_up", "w_in")
    for n in held_transposed:
        for p in ("", "m_", "v_"):
            a[p + n] = a[p + n].transpose(0, 2, 1)
    B, S, D = x.shape
    mx, my, mc = _place()
    chip = 2 * mx + my
    dev = 2 * chip + mc
    core = mc.astype(jnp.int32).reshape(1)
    chip_id = chip.astype(jnp.int32).reshape(1)

    cw_rows = jnp.pad(conv_w[0], ((0, 0), (0, D - conv_w.shape[2])))
    got, _ = _all_gather_small(jnp.concatenate([c, cw_rows, jnp.zeros((8 - B - CONV_WIDTH, D), F32)], axis=0), "gather_c")
    c_all = got[:, :B, :].reshape(N_DEV * B, D)
    conv_full = got[::2, B:B + CONV_WIDTH, :conv_w.shape[2]].transpose(1, 0, 2).reshape(CONV_WIDTH, D_CONV)

    b_cols = lax.dynamic_slice(b_ada, (0, chip * ADA_COLS), (1, ADA_COLS))
    mod_all, mod_done = _all_gather_small(_ada_fwd(c_all, w_ada[0], b_cols), "gather_mod")
    mod = lax.dynamic_slice(mod_all, (0, B * dev, 0), (N_DEV, B, ADA_COLS))[::2].transpose(1, 0, 2).reshape(B, N_MOD * D)

    first = ("ffn1_w_gate", "ffn1_w_up", "ffn1_w_down")
    later = tuple(n for n in BIG_NAMES if n not in first)
    got_first, gathered = _gather_weights([(a[n][0] + mod_done[0, 0]).astype(BF16) for n in first])
    w = dict(zip(first, got_first))
    in_flight = _gather_start([(a[n][0] + gathered[0, 0]).astype(BF16) for n in later])

    def later_weights(after):
        send_sems, recv_sems, shards, lands, _ = in_flight
        lands = _gather_wait(send_sems, recv_sems, shards, lands, after)
        wl = dict(zip(later, _gather_finish([a[n][0].astype(BF16) for n in later], lands)))
        for n, to_kernel in _TO_KERNEL.items():
            wl[n] = to_kernel(wl[n].reshape(-1, D) if n in held_transposed else _columns_joined(wl[n]))
        wl["w_out"] = wl["w_out"].reshape(D_SSD + D_MLA, D)
        return wl

    small = {n: a[n].reshape(1, -1) for n in SMALL_NAMES if n not in ("conv_w", "norm_final")}
    small["conv_w"], small["norm_final"] = conv_full, norm_final

    def scatter_group(names, gw, after):
        g4 = []
        for n in names:
            g = gw[n]
            if n in _FROM_KERNEL:
                g = _FROM_KERNEL[n](g) if n in held_transposed else _columns_split(_FROM_KERNEL[n](g))
            g4.append(g.reshape(N_CHIPS, a[n].shape[1], a[n].shape[2]))
        swapped = _swap_halves(g4, g4[0] if after is None else after, "swap_" + names[0])
        pair = [_pair_sum(g, got, core, "pair_sum_" + n) for n, g, got in zip(names, g4, swapped)]
        return (names,) + tuple(_scatter_start(pair, names[0]))

    grads, deltas, new_m, new_v = {}, {}, {}, {}

    def finish_group(group, after):
        names, send_sems, recv_sems, pair, lands, _ = group
        pair, lands = _scatter_wait(send_sems, recv_sems, pair, lands, after, names[0])
        mine = [_chip_sum(own, got, chip_id, "chip_sum_" + n) for n, own, got in zip(names, pair, lands)]
        for n, own, other in zip(names, mine, _join_halves(mine, "join_" + names[0])):
            grads[n], deltas[n], new_m[n], new_v[n] = _adam_halves(a[n], a["m_" + n], a["v_" + n], own, other, core, "adam_" + n)
        return deltas[names[-1]]

    groups = []

    def on_grads(names, gw):
        groups.append(scatter_group(names, gw, None))
        return groups[-1][5][0, 0]

    loss_blk, grad_x, gw, dmod, gs = _local_step(x, positions, mod + in_flight[4][0, 0], w, later_weights, small, loss_target,
                                                 on_grads)

    small_flat = _pack_small(gs).at[-1].set(loss_blk[0, 0])
    send = jnp.concatenate([dmod.reshape(MOD_ROWS, D), small_flat.reshape(SMALL_ROWS, D),
                            jnp.zeros((SEND_ROWS - MOD_ROWS - SMALL_ROWS, D), F32)], axis=0)
    got, _ = _all_gather_small(send, "gather_small")
    summed = _small_sum(got)
    sums = summed[N_MOD:N_MOD + SMALL_ROWS].reshape(-1)
    loss = sums[-1]
    gsmall = _unpack_small(sums)
    gsmall["conv_w"] = lax.dynamic_slice(gsmall["conv_w"].reshape(CONV_WIDTH, D_CONV), (0, chip * conv_w.shape[2]),
                                         (CONV_WIDTH, conv_w.shape[2]))
    gsmall["b_ada"] = summed[:N_MOD]
    names = ("b_ada",) + SMALL_NAMES
    rows = 208

    def pack(parts):
        flat = jnp.concatenate([parts[n].reshape(-1) for n in names])
        return jnp.pad(flat, (0, rows * LANES - flat.shape[0])).reshape(rows, LANES)

    packed = [pack({n: a[p + n] for n in names}) for p in ("", "m_", "v_")]
    g_p = pack(gsmall)
    outs = (g_p,) + tuple(_adam(packed[0], g_p, packed[1], packed[2], "adam_small"))
    for dst, flat in zip((grads, deltas, new_m, new_v), outs):
        flat, off = flat.reshape(-1), 0
        for n in names:
            dst[n] = flat[off:off + a[n].size].reshape(a[n].shape)
            off += a[n].size

    dmod_all = got[:, :MOD_ROWS, :].reshape(N_DEV * B, N_MOD * D)
    dmod_cols = lax.dynamic_slice(dmod_all, (0, chip * ADA_COLS), (N_DEV * B, ADA_COLS))
    ada = _ada_bwd(c_all, dmod_cols, w_ada[0], m_w_ada[0], v_w_ada[0])
    for dst, t in zip((grads, deltas, new_m, new_v), ada):
        dst["w_ada"] = t[None]

    last = scatter_group(first, gw, summed)
    after = last[5]
    for group in groups:
        after = finish_group(group, after)
    finish_group(last, after)
    for dst in (grads, deltas, new_m, new_v):
        for n in held_transposed:
            dst[n] = dst[n].transpose(0, 2, 1)

    order = ("w_ada", "b_ada", "norm_ffn1", "ffn1_w_gate", "ffn1_w_up", "ffn1_w_down", "norm_mix", "w_in", "conv_w", "conv_b",
             "dt_bias", "a_log", "d_skip", "ssd_norm_w", "q_norm_w", "w_uq", "kv_norm_w", "w_ukv", "mla_norm_w", "w_out",
             "norm_ffn2", "ffn2_w_gate", "ffn2_w_up", "ffn2_w_down", "norm_final")
    return (loss, grad_x, *[grads[n] for n in order], *[deltas[n] for n in order], *[new_m[n] for n in order],
            *[new_v[n] for n in order])
```

```python
import functools
import math

import jax
import jax.numpy as jnp
import numpy as np
from jax import lax
from jax.experimental import pallas as pl
from jax.experimental.pallas import tpu as pltpu

F32 = jnp.float32
BF16 = jnp.bfloat16
HIGHEST = lax.Precision.HIGHEST

D_MODEL = 1024
D_FF = 2816
D_SSD = 1024
D_MLA = 1024
SSD_HEADS = 16
SSD_HEAD_DIM = 64
SSD_GROUPS = 2
SSD_STATE = 128
CONV_WIDTH = 4
CHUNK = 128
MLA_HEADS = 8
QK_NOPE = 64
QK_ROPE = 32
QK_DIM = QK_NOPE + QK_ROPE
V_HEAD = 128
Q_LORA = 384
KV_LORA = 256
ROPE_THETA = 10000.0
N_MOD = 9
EPS = 1e-6
D_CONV = D_SSD + 2 * SSD_GROUPS * SSD_STATE
D_PROJ = 3328
HEAD_LANES = 128
ADAM_LR = 0.001
ADAM_B1 = 0.9
ADAM_B2 = 0.999
ADAM_EPS = 1e-08
ADAM_WD = 0.01
ADAM_STEP = 10

LANES = 128
VMEM_LIMIT = 56 * 1024 * 1024
TOKEN_TILE = 512
ATTN_FWD_Q_TILE = 1024
ATTN_FWD_KV_TILE = 1024
ATTN_BWD_TILE = 1024
N_CHIPS = 4
N_DEV = 8

MESH = pl.DeviceIdType.MESH


def _dot(a, b, precision=None):
    return jnp.dot(a, b, preferred_element_type=F32, precision=precision)


def _dot_nt(a, b, precision=None):
    return lax.dot_general(a, b, (((1,), (1,)), ((), ())), preferred_element_type=F32, precision=precision)


def _dot_tn(a, b, precision=None):
    return lax.dot_general(a, b, (((0,), (0,)), ((), ())), preferred_element_type=F32, precision=precision)


def _cparams(semantics):
    return pltpu.CompilerParams(dimension_semantics=semantics, vmem_limit_bytes=VMEM_LIMIT)


def _resident(shape):
    zeros = (0,) * len(shape)
    return pl.BlockSpec(shape, lambda *_: zeros, pipeline_mode=pl.Buffered(1))


def _sigmoid(x):
    return jax.nn.sigmoid(x)


def _rms_stats(x):
    r = lax.rsqrt(jnp.mean(x * x, axis=-1, keepdims=True) + EPS)
    return x * r, r


def _rms_bwd(dn, xh, r, w):
    dxh = dn * w
    dx = r * (dxh - xh * jnp.mean(dxh * xh, axis=-1, keepdims=True))
    return dx, dn * xh


def _colsum(v):
    return jnp.sum(v, axis=0, keepdims=True)


def _ffn_fwd(x, nw, sh, sc, g, wg, wu, wd, seq, name, head=None):
    T, D = x.shape
    fs = wg.shape[1]
    tm = min(TOKEN_TILE, seq)
    tps = seq // tm

    def body(x_ref, nw_ref, sh_ref, sc_ref, g_ref, wg_ref, wu_ref, wd_ref, *rest):
        if head is None:
            xo_ref, a_ref, u_ref, f_ref = rest
        else:
            nf_ref, t_ref, xo_ref, a_ref, u_ref, f_ref, loss_ref, dnf_ref = rest

            @pl.when(pl.program_id(0) == 0)
            def _():
                loss_ref[...] = jnp.zeros_like(loss_ref)
                dnf_ref[...] = jnp.zeros_like(dnf_ref)

        xv = x_ref[...]
        xh, _ = _rms_stats(xv)
        h = (xh * nw_ref[...]) * (1.0 + sc_ref[0]) + sh_ref[0]
        hb = h.astype(BF16)
        f = jnp.zeros((tm, D), F32)
        for j in range(N_CHIPS):
            a = _dot_nt(hb, wg_ref[j])
            u = _dot_nt(hb, wu_ref[j])
            a_ref[j] = a.astype(BF16)
            u_ref[j] = u.astype(BF16)
            f = f + _dot((a * _sigmoid(a) * u).astype(BF16), wd_ref[j])
        f_ref[...] = f.astype(BF16)
        xo = xv + 0.5 * g_ref[0] * f
        if head is None:
            xo_ref[...] = xo
        else:
            xh, r = _rms_stats(xo)
            nfv = nf_ref[...]
            err = xh * nfv - t_ref[...]
            loss_ref[...] += (0.5 / D) * jnp.sum(err * err)
            dxo, dw_rows = _rms_bwd(err * (1.0 / D), xh, r, nfv)
            xo_ref[...] = dxo
            dnf_ref[...] += _colsum(dw_rows)

    rows = lambda n: pl.BlockSpec((tm, n), lambda i: (i, 0))
    act = pl.BlockSpec((N_CHIPS, tm, fs), lambda i: (0, i, 0))
    perb = pl.BlockSpec((1, 1, D), lambda i: (i // tps, 0, 0))
    sd = jax.ShapeDtypeStruct
    in_specs = [rows(D), _resident((1, D)), perb, perb, perb, _resident((N_CHIPS, fs, D)), _resident((N_CHIPS, fs, D)),
                _resident((N_CHIPS, fs, D))]
    out_specs = [rows(D), act, act, rows(D)]
    out_shape = [sd((T, D), F32), sd((N_CHIPS, T, fs), BF16), sd((N_CHIPS, T, fs), BF16), sd((T, D), BF16)]
    if head is not None:
        in_specs += [_resident((1, D)), rows(D)]
        out_specs += [pl.BlockSpec((8, LANES), lambda i: (0, 0)), pl.BlockSpec((1, D), lambda i: (0, 0))]
        out_shape += [sd((8, LANES), F32), sd((1, D), F32)]
    return pl.pallas_call(
        body, grid=(T // tm,), name=name, in_specs=in_specs, out_specs=out_specs, out_shape=out_shape,
        compiler_params=_cparams(("arbitrary",)),
    )(x, nw, sh, sc, g, wg, wu, wd, *(head or ()))


def _ffn_bwd(dxo, x, nw, sh, sc, g, a, u, f, wg, wu, wd, seq, name):
    T, D = x.shape
    fs = wg.shape[1]
    B = T // seq
    tm = min(TOKEN_TILE // 2, seq)
    tps = seq // tm

    def body(dxo_ref, x_ref, nw_ref, sh_ref, sc_ref, g_ref, a_ref, u_ref, f_ref, wg_ref, wu_ref, wd_ref,
             dx_ref, h_ref, s_ref, df_ref, da_ref, du_ref, dsh_ref, dsc_ref, dg_ref, dnw_ref):
        i = pl.program_id(0)

        @pl.when(i % tps == 0)
        def _():
            dsh_ref[...] = jnp.zeros_like(dsh_ref)
            dsc_ref[...] = jnp.zeros_like(dsc_ref)
            dg_ref[...] = jnp.zeros_like(dg_ref)

        @pl.when(i == 0)
        def _():
            dnw_ref[...] = jnp.zeros_like(dnw_ref)

        dxo_v = dxo_ref[...]
        dfb = (0.5 * g_ref[0] * dxo_v).astype(BF16)
        dg_ref[0] += _colsum(0.5 * dxo_v * f_ref[...].astype(F32))
        dh = jnp.zeros((tm, D), F32)
        for j in range(N_CHIPS):
            ds = _dot_nt(dfb, wd_ref[j])
            av = a_ref[j].astype(F32)
            uv = u_ref[j].astype(F32)
            sig = _sigmoid(av)
            sil = av * sig
            dab = (ds * uv * (sig * (1.0 + av * (1.0 - sig)))).astype(BF16)
            dub = (ds * sil).astype(BF16)
            dh = dh + _dot(dab, wg_ref[j]) + _dot(dub, wu_ref[j])
            s_ref[j] = (sil * uv).astype(BF16)
            da_ref[j] = dab
            du_ref[j] = dub
        xv = x_ref[...]
        xh, r = _rms_stats(xv)
        nwv = nw_ref[...]
        n = xh * nwv
        scale1 = 1.0 + sc_ref[0]
        dsc_ref[0] += _colsum(dh * n)
        dsh_ref[0] += _colsum(dh)
        dx, dw_rows = _rms_bwd(dh * scale1, xh, r, nwv)
        dnw_ref[...] += _colsum(dw_rows)
        dx_ref[...] = dxo_v + dx
        h_ref[...] = (n * scale1 + sh_ref[0]).astype(BF16)
        df_ref[...] = dfb

    rows = lambda n: pl.BlockSpec((tm, n), lambda i: (i, 0))
    act = pl.BlockSpec((N_CHIPS, tm, fs), lambda i: (0, i, 0))
    perb = pl.BlockSpec((1, 1, D), lambda i: (i // tps, 0, 0))
    sd = jax.ShapeDtypeStruct
    return pl.pallas_call(
        body, grid=(T // tm,), name=name,
        in_specs=[rows(D), rows(D), _resident((1, D)), perb, perb, perb, act, act, rows(D),
                  _resident((N_CHIPS, fs, D)), _resident((N_CHIPS, fs, D)), _resident((N_CHIPS, fs, D))],
        out_specs=[rows(D), rows(D), act, rows(D), act, act, perb, perb, perb, pl.BlockSpec((1, D), lambda i: (0, 0))],
        out_shape=[sd((T, D), F32), sd((T, D), BF16), sd((N_CHIPS, T, fs), BF16), sd((T, D), BF16),
                   sd((N_CHIPS, T, fs), BF16), sd((N_CHIPS, T, fs), BF16), sd((B, 1, D), F32), sd((B, 1, D), F32),
                   sd((B, 1, D), F32), sd((1, D), F32)],
        compiler_params=_cparams(("arbitrary",)),
    )(dxo, x, nw, sh, sc, g, a, u, f, wg, wu, wd)


def _ffn_wgrad(h, s, df, da, du, name):
    T, D = h.shape
    fs = s.shape[2]
    tt = min(TOKEN_TILE, T)
    nt = T // tt

    def body(h_ref, s_ref, df_ref, da_ref, du_ref, dgate_ref, dup_ref, ddown_ref, gate_acc, up_acc, down_acc):
        @pl.when(pl.program_id(1) == 0)
        def _():
            gate_acc[...] = jnp.zeros_like(gate_acc)
            up_acc[...] = jnp.zeros_like(up_acc)
            down_acc[...] = jnp.zeros_like(down_acc)

        hv = h_ref[...]
        gate_acc[...] += _dot_tn(da_ref[0], hv)
        up_acc[...] += _dot_tn(du_ref[0], hv)
        down_acc[...] += _dot_tn(s_ref[0], df_ref[...])

        @pl.when(pl.program_id(1) == nt - 1)
        def _():
            dgate_ref[0] = gate_acc[...].astype(BF16)
            dup_ref[0] = up_acc[...].astype(BF16)
            ddown_ref[0] = down_acc[...].astype(BF16)

    rows = pl.BlockSpec((tt, D), lambda j, t: (t, 0))
    act = pl.BlockSpec((1, tt, fs), lambda j, t: (j, t, 0))
    shard = pl.BlockSpec((1, fs, D), lambda j, t: (j, 0, 0))
    return pl.pallas_call(
        body, grid=(N_CHIPS, nt), name=name,
        in_specs=[rows, act, rows, act, act],
        out_specs=[shard] * 3, out_shape=[jax.ShapeDtypeStruct((N_CHIPS, fs, D), BF16)] * 3,
        scratch_shapes=[pltpu.VMEM((fs, D), F32)] * 3,
        compiler_params=_cparams(("arbitrary", "arbitrary")),
    )(h, s, df, da, du)


def _mm_tn(xa, ya, tn, name):
    T, K = xa.shape
    N = ya.shape[1]
    tt = min(TOKEN_TILE, T)

    def body(x_ref, y_ref, o_ref):
        @pl.when(pl.program_id(1) == 0)
        def _():
            o_ref[...] = jnp.zeros_like(o_ref)

        o_ref[...] += _dot_tn(x_ref[...], y_ref[...])

    return pl.pallas_call(
        body, grid=(N // tn, T // tt), name=name,
        in_specs=[pl.BlockSpec((tt, K), lambda j, t: (t, 0)), pl.BlockSpec((tt, tn), lambda j, t: (t, j))],
        out_specs=pl.BlockSpec((K, tn), lambda j, t: (0, j)),
        out_shape=jax.ShapeDtypeStruct((K, N), F32),
        compiler_params=_cparams(("arbitrary", "arbitrary")),
    )(xa, ya)


_PROJ_SPLITS = (0, 1024, 2560, 2944, 3200, 3328)


def _inproj_fwd(x, nw, sh, sc, win, seq):
    T, D = x.shape
    tm = min(TOKEN_TILE, seq)
    tps = seq // tm
    widths = [b - a for a, b in zip(_PROJ_SPLITS[:-1], _PROJ_SPLITS[1:])]
    dtypes = [BF16, BF16, F32, F32, F32]

    def body(x_ref, nw_ref, sh_ref, sc_ref, w_ref, *outs):
        xh, _ = _rms_stats(x_ref[...])
        h = (xh * nw_ref[...]) * (1.0 + sc_ref[0]) + sh_ref[0]
        proj = _dot_nt(h.astype(BF16), w_ref[...])
        for o, lo, hi in zip(outs, _PROJ_SPLITS[:-1], _PROJ_SPLITS[1:]):
            o[...] = proj[:, lo:hi].astype(o.dtype)

    rows = lambda n: pl.BlockSpec((tm, n), lambda i: (i, 0))
    perb = pl.BlockSpec((1, 1, D), lambda i: (i // tps, 0, 0))
    return pl.pallas_call(
        body, grid=(T // tm,), name="inproj_fwd",
        in_specs=[rows(D), _resident((1, D)), perb, perb, _resident((D_PROJ, D))],
        out_specs=[rows(w) for w in widths],
        out_shape=[jax.ShapeDtypeStruct((T, w), dt) for w, dt in zip(widths, dtypes)],
        compiler_params=_cparams(("arbitrary",)),
    )(x, nw, sh, sc, win)


def _inproj_bwd(dx2, x, nw, sh, sc, win, dz, dxbc, dcq, dckv, ddtk_a, ddtk_b, seq):
    T, D = x.shape
    B = T // seq
    tm = min(TOKEN_TILE, seq)
    tps = seq // tm

    def body(dx2_ref, x_ref, nw_ref, sh_ref, sc_ref, w_ref, dz_ref, dxbc_ref, dcq_ref, dckv_ref, da_ref, db_ref,
             dx_ref, h_ref, dp_ref, dsh_ref, dsc_ref, dnw_ref):
        i = pl.program_id(0)

        @pl.when(i % tps == 0)
        def _():
            dsh_ref[...] = jnp.zeros_like(dsh_ref)
            dsc_ref[...] = jnp.zeros_like(dsc_ref)

        @pl.when(i == 0)
        def _():
            dnw_ref[...] = jnp.zeros_like(dnw_ref)

        dproj = jnp.concatenate(
            [dz_ref[...], dxbc_ref[...], dcq_ref[...].astype(BF16), dckv_ref[...].astype(BF16),
             (da_ref[...] + db_ref[...]).astype(BF16)], axis=1)
        dp_ref[...] = dproj
        dh = _dot(dproj, w_ref[...])
        xh, r = _rms_stats(x_ref[...])
        nwv = nw_ref[...]
        n = xh * nwv
        scale1 = 1.0 + sc_ref[0]
        dsc_ref[0] += _colsum(dh * n)
        dsh_ref[0] += _colsum(dh)
        dx, dw_rows = _rms_bwd(dh * scale1, xh, r, nwv)
        dnw_ref[...] += _colsum(dw_rows)
        dx_ref[...] = dx2_ref[...] + dx
        h_ref[...] = (n * scale1 + sh_ref[0]).astype(BF16)

    rows = lambda n: pl.BlockSpec((tm, n), lambda i: (i, 0))
    perb = pl.BlockSpec((1, 1, D), lambda i: (i // tps, 0, 0))
    sd = jax.ShapeDtypeStruct
    return pl.pallas_call(
        body, grid=(T // tm,), name="inproj_bwd",
        in_specs=[rows(D), rows(D), _resident((1, D)), perb, perb, _resident((D_PROJ, D)),
                  rows(1024), rows(D_CONV), rows(Q_LORA), rows(KV_LORA), rows(LANES), rows(LANES)],
        out_specs=[rows(D), rows(D), rows(D_PROJ), perb, perb, pl.BlockSpec((1, D), lambda i: (0, 0))],
        out_shape=[sd((T, D), F32), sd((T, D), BF16), sd((T, D_PROJ), BF16), sd((B, 1, D), F32), sd((B, 1, D), F32),
                   sd((1, D), F32)],
        compiler_params=_cparams(("arbitrary",)),
    )(dx2, x, nw, sh, sc, win, dz, dxbc, dcq, dckv, ddtk_a, ddtk_b)


def _shift_down(v, k, row):
    return jnp.where(row < k, 0.0, pltpu.roll(v, k, 0))


def _shift_up(v, k, row, n):
    return jnp.where(row >= n - k, 0.0, pltpu.roll(v, n - k, 0))


def _conv_pre(xv, w_ref, b_ref, row):
    pre = b_ref[...] + w_ref[CONV_WIDTH - 1:CONV_WIDTH, :] * xv
    for k in range(1, CONV_WIDTH):
        pre = pre + w_ref[CONV_WIDTH - 1 - k:CONV_WIDTH - k, :] * _shift_down(xv, k, row)
    return pre


def _conv_fwd(xraw, cw, cb):
    B, S, C = xraw.shape

    def body(x_ref, w_ref, b_ref, o_ref):
        xv = x_ref[0].astype(F32)
        row = lax.broadcasted_iota(jnp.int32, xv.shape, 0)
        pre = _conv_pre(xv, w_ref, b_ref, row)
        o_ref[0] = (pre * _sigmoid(pre)).astype(BF16)

    blk = pl.BlockSpec((1, S, LANES), lambda b, j: (b, 0, j))
    return pl.pallas_call(
        body, grid=(B, C // LANES), name="conv_fwd",
        in_specs=[blk, pl.BlockSpec((CONV_WIDTH, LANES), lambda b, j: (0, j)), pl.BlockSpec((1, LANES), lambda b, j: (0, j))],
        out_specs=blk, out_shape=jax.ShapeDtypeStruct((B, S, C), BF16),
        compiler_params=_cparams(("arbitrary", "arbitrary")),
    )(xraw, cw, cb)


def _conv_bwd(dout, xraw, cw, cb):
    B, S, C = xraw.shape

    def body(d_ref, x_ref, w_ref, b_ref, dx_ref, dw_ref, db_ref):
        @pl.when(pl.program_id(1) == 0)
        def _():
            dw_ref[...] = jnp.zeros_like(dw_ref)
            db_ref[...] = jnp.zeros_like(db_ref)

        xv = x_ref[0].astype(F32)
        row = lax.broadcasted_iota(jnp.int32, xv.shape, 0)
        pre = _conv_pre(xv, w_ref, b_ref, row)
        sig = _sigmoid(pre)
        dpre = d_ref[0].astype(F32) * (sig * (1.0 + pre * (1.0 - sig)))
        dx = w_ref[CONV_WIDTH - 1:CONV_WIDTH, :] * dpre
        for k in range(1, CONV_WIDTH):
            dx = dx + w_ref[CONV_WIDTH - 1 - k:CONV_WIDTH - k, :] * _shift_up(dpre, k, row, S)
        dx_ref[0] = dx.astype(BF16)
        db_ref[...] += _colsum(dpre)
        dws = [_colsum(dpre * (xv if k == 0 else _shift_down(xv, k, row))) for k in range(CONV_WIDTH - 1, -1, -1)]
        dw_ref[...] += jnp.concatenate(dws, axis=0)

    blk = pl.BlockSpec((1, S, LANES), lambda j, b: (b, 0, j))
    wspec = pl.BlockSpec((CONV_WIDTH, LANES), lambda j, b: (0, j))
    bspec = pl.BlockSpec((1, LANES), lambda j, b: (0, j))
    return pl.pallas_call(
        body, grid=(C // LANES, B), name="conv_bwd",
        in_specs=[blk, blk, wspec, bspec], out_specs=[blk, wspec, bspec],
        out_shape=[jax.ShapeDtypeStruct((B, S, C), BF16), jax.ShapeDtypeStruct((CONV_WIDTH, C), F32),
                   jax.ShapeDtypeStruct((1, C), F32)],
        compiler_params=_cparams(("arbitrary", "arbitrary")),
    )(dout, xraw, cw, cb)


def _softplus(x):
    return jnp.maximum(x, 0.0) + jnp.log(1.0 + jnp.exp(-jnp.abs(x)))


def _ssd_common(xbc_ref, dtk_ref, dtb_ref, alog_ref, e_ref):
    L = CHUNK
    xbc = xbc_ref[0]
    xs = xbc[:, :D_SSD].astype(F32)
    bm = xbc[:, D_SSD:D_SSD + 256]
    cm = xbc[:, D_SSD + 256:D_SSD + 512]
    head = lax.broadcasted_iota(jnp.int32, (1, LANES), 1) < SSD_HEADS
    a128 = jnp.where(head, -jnp.exp(alog_ref[...]), 0.0)
    pre = dtk_ref[0] + dtb_ref[...]
    dt = _softplus(pre)
    dA = dt * a128
    row = lax.broadcasted_iota(jnp.int32, (L, L), 0)
    col = lax.broadcasted_iota(jnp.int32, (L, L), 1)
    causal = col <= row
    tri = causal.astype(F32)
    triT = (row <= col).astype(F32)
    tri = causal.astype(BF16)
    triT = (row <= col).astype(BF16)
    dA3 = _split3(dA)
    acum = _sum3(lambda part: _dot(tri, part), dA3)
    acumT = _sum3(lambda part: _dot_tn(part, triT), dA3)
    E = e_ref[...]
    acum_f = _spread(acum, E)
    dt_f = _spread(dt, E)
    e_f = jnp.exp(acum_f)
    w_f = jnp.exp(acum_f[L - 1:L, :] - acum_f)
    xt = xs * dt_f
    return dict(xs=xs, bm=bm, cm=cm, a128=a128, pre=pre, dt=dt, causal=causal, tri=tri, triT=triT, acum=acum,
                acumT=acumT, E=E, dt_f=dt_f, e_f=e_f, w_f=w_f, xt=xt, head=head)


def _split3(x):
    p1 = x.astype(BF16)
    r1 = x - p1.astype(F32)
    p2 = r1.astype(BF16)
    return p1, p2, (r1 - p2.astype(F32)).astype(BF16)


def _sum3(mm, parts):
    return (mm(parts[0]) + mm(parts[1])) + mm(parts[2])


def _spread(v, e):
    return _sum3(lambda part: _dot(part, e), _split3(v))


def _gather_heads(v, e):
    return _sum3(lambda part: _dot_nt(part, e), _split3(v))


def _head_mask(k):
    lane = lax.broadcasted_iota(jnp.int32, (CHUNK, LANES), 1)
    return (lane >= SSD_HEAD_DIM) if k == 1 else (lane < SSD_HEAD_DIM)


def _pair_decay(alast, h0):
    row = lax.broadcasted_iota(jnp.int32, (2 * SSD_HEAD_DIM, SSD_STATE), 0)
    return jnp.exp(jnp.where(row < SSD_HEAD_DIM, alast[:, h0:h0 + 1], alast[:, h0 + 1:h0 + 2]))


def _decay_matrix(q, h):
    seg = q["acum"][:, h:h + 1] - q["acumT"][h:h + 1, :]
    return jnp.exp(jnp.where(q["causal"], seg, -1e30))


def _gated_norm(y, zz, nw):
    sig = _sigmoid(zz)
    sil = zz * sig
    yg = y * sil
    half = D_SSD // SSD_GROUPS
    parts = []
    for g in range(SSD_GROUPS):
        xh, r = _rms_stats(yg[:, g * half:(g + 1) * half])
        parts.append((xh, r))
    return sig, sil, parts


def _ssd_fwd(xbc, dtk, z, dtb, alog, dsk, nw, expand):
    B, S, _ = xbc.shape
    L = CHUNK
    nc = S // L

    def body(xbc_ref, dtk_ref, z_ref, dtb_ref, alog_ref, dsk_ref, nw_ref, e_ref, y_ref, ys_ref, prev_ref, st_ref):
        @pl.when(pl.program_id(1) == 0)
        def _():
            st_ref[...] = jnp.zeros_like(st_ref)

        q = _ssd_common(xbc_ref, dtk_ref, dtb_ref, alog_ref, e_ref)
        xtb = q["xt"].astype(BF16)
        xwb = (q["xt"] * q["w_f"]).astype(BF16)
        alast = q["acum"][L - 1:L, :]
        ys = []
        for g in range(SSD_GROUPS):
            bg = q["bm"][:, g * 128:(g + 1) * 128]
            cg = q["cm"][:, g * 128:(g + 1) * 128]
            G = _dot_nt(cg, bg)
            for pr in range(SSD_HEADS // SSD_GROUPS // 2):
                h0 = g * 8 + 2 * pr
                lo = h0 * SSD_HEAD_DIM
                xt_p = xtb[:, lo:lo + 128]
                ydiag = jnp.zeros((L, LANES), F32)
                for k in range(2):
                    M = (G * _decay_matrix(q, h0 + k)).astype(BF16)
                    ydiag = ydiag + _dot(M, jnp.where(_head_mask(k), xt_p, jnp.zeros_like(xt_p)))
                hp = st_ref[lo:lo + 128, :]
                prev_ref[0, 0, lo:lo + 128, :] = hp.astype(BF16)
                zoff = _dot_nt(cg, hp.astype(BF16))
                ys.append(ydiag + zoff * q["e_f"][:, lo:lo + 128])
                st_ref[lo:lo + 128, :] = _pair_decay(alast, h0) * hp + _dot_tn(xwb[:, lo:lo + 128], bg)
        y = jnp.concatenate(ys, axis=1) + dsk_ref[...] * q["xs"]
        y_ref[0] = y.astype(BF16)
        _, _, parts = _gated_norm(y, z_ref[0].astype(F32), nw_ref[...])
        half = D_SSD // SSD_GROUPS
        ys_ref[0] = jnp.concatenate(
            [xh * nw_ref[:, g * half:(g + 1) * half] for g, (xh, _) in enumerate(parts)], axis=1).astype(BF16)

    chunk = lambda n: pl.BlockSpec((1, L, n), lambda b, c: (b, c, 0))
    vec = pl.BlockSpec((1, LANES), lambda b, c: (0, 0))
    return pl.pallas_call(
        body, grid=(B, nc), name="ssd_fwd",
        in_specs=[chunk(D_CONV), chunk(LANES), chunk(D_SSD), vec, vec, pl.BlockSpec((1, D_SSD), lambda b, c: (0, 0)),
                  pl.BlockSpec((1, D_SSD), lambda b, c: (0, 0)), pl.BlockSpec((LANES, D_SSD), lambda b, c: (0, 0))],
        out_specs=[chunk(D_SSD), chunk(D_SSD), pl.BlockSpec((1, 1, D_SSD, SSD_STATE), lambda b, c: (b, c, 0, 0))],
        out_shape=[jax.ShapeDtypeStruct((B, S, D_SSD), BF16), jax.ShapeDtypeStruct((B, S, D_SSD), BF16),
                   jax.ShapeDtypeStruct((B, nc, D_SSD, SSD_STATE), BF16)],
        scratch_shapes=[pltpu.VMEM((D_SSD, SSD_STATE), F32)],
        compiler_params=_cparams(("arbitrary", "arbitrary")),
    )(xbc, dtk, z, dtb, alog, dsk, nw, expand)


def _ssd_bwd(xbc, dtk, z, y, prev, dys, dtb, alog, dsk, nw, expand):
    B, S, _ = xbc.shape
    L = CHUNK
    nc = S // L
    half = D_SSD // SSD_GROUPS

    def body(xbc_ref, dtk_ref, z_ref, y_ref, prev_ref, dys_ref, dtb_ref, alog_ref, dsk_ref, nw_ref, e_ref,
             dxbc_ref, ddtk_ref, dz_ref, dnw_ref, dvec_ref, dh_ref, dskc_ref):
        @pl.when((pl.program_id(0) == 0) & (pl.program_id(1) == 0))
        def _():
            dnw_ref[...] = jnp.zeros_like(dnw_ref)
            dvec_ref[...] = jnp.zeros_like(dvec_ref)
            dskc_ref[...] = jnp.zeros_like(dskc_ref)

        @pl.when(pl.program_id(1) == 0)
        def _():
            dh_ref[...] = jnp.zeros_like(dh_ref)

        q = _ssd_common(xbc_ref, dtk_ref, dtb_ref, alog_ref, e_ref)
        E = q["E"]
        xs = q["xs"]
        yv = y_ref[0].astype(F32)
        zz = z_ref[0].astype(F32)
        sig, sil, parts = _gated_norm(yv, zz, nw_ref[...])
        dn = dys_ref[0].astype(F32)
        dyg, dnw_rows = [], []
        for g, (xh, r) in enumerate(parts):
            dpart, dw_rows = _rms_bwd(dn[:, g * half:(g + 1) * half], xh, r, nw_ref[:, g * half:(g + 1) * half])
            dyg.append(dpart)
            dnw_rows.append(dw_rows)
        dyg = jnp.concatenate(dyg, axis=1)
        dnw_ref[...] += _colsum(jnp.concatenate(dnw_rows, axis=1))
        dY = dyg * sil
        dz_ref[0] = (dyg * yv * (sig * (1.0 + zz * (1.0 - sig)))).astype(BF16)
        dsk_f = dsk_ref[...]
        dskc_ref[...] += _colsum(dY * xs)
        dYb = dY.astype(BF16)
        xtb = q["xt"].astype(BF16)
        xwb = (q["xt"] * q["w_f"]).astype(BF16)
        acum = q["acum"]
        alast = acum[L - 1:L, :]
        lane_id = lax.broadcasted_iota(jnp.int32, (L, LANES), 1)
        sub_id = lax.broadcasted_iota(jnp.int32, (LANES, L), 0)
        lane_row = lax.broadcasted_iota(jnp.int32, (1, LANES), 1)
        da_rows = jnp.zeros((L, LANES), F32)
        daT = jnp.zeros((LANES, L), F32)
        dxt, prod_off, prod_st, dbs, dcs = [], [], [], [], []
        hsum_row = jnp.zeros((1, LANES), F32)
        for g in range(SSD_GROUPS):
            bg = q["bm"][:, g * 128:(g + 1) * 128]
            cg = q["cm"][:, g * 128:(g + 1) * 128]
            G = _dot_nt(cg, bg)
            dG = jnp.zeros((L, L), F32)
            dcg = jnp.zeros((L, SSD_STATE), F32)
            dbg = jnp.zeros((L, SSD_STATE), F32)
            for pr in range(SSD_HEADS // SSD_GROUPS // 2):
                h0 = g * 8 + 2 * pr
                lo = h0 * SSD_HEAD_DIM
                cols = slice(lo, lo + 128)
                dY_p = dYb[:, cols]
                xt_p = xtb[:, cols]
                dxt_p = jnp.zeros((L, LANES), F32)
                for k in range(2):
                    h = h0 + k
                    Lm = _decay_matrix(q, h)
                    Mf = G * Lm
                    dYk = jnp.where(_head_mask(k), dY_p, jnp.zeros_like(dY_p))
                    dM = _dot_nt(dYk, xt_p)
                    dxt_p = dxt_p + _dot_tn(Mf.astype(BF16), dYk)
                    dG = dG + dM * Lm
                    Q = dM * Mf
                    da_rows = da_rows + jnp.where(lane_id == h, jnp.sum(Q, axis=1, keepdims=True), 0.0)
                    daT = daT + jnp.where(sub_id == h, jnp.sum(Q, axis=0, keepdims=True), 0.0)
                hpb = prev_ref[0, 0, lo:lo + 128, :]
                hp = hpb.astype(F32)
                zoff = _dot_nt(cg, hpb)
                e_p = q["e_f"][:, cols]
                dY_pf = dY[:, cols]
                dZb = (dY_pf * e_p).astype(BF16)
                dcg = dcg + _dot(dZb, hpb)
                dhp_off = _dot_tn(dZb, cg)
                prod_off.append(dY_pf * zoff * e_p)
                dS = dh_ref[lo:lo + 128, :]
                dSb = dS.astype(BF16)
                U = _dot_nt(bg, dSb)
                dxt_p = dxt_p + U * q["w_f"][:, cols]
                dbg = dbg + _dot(xwb[:, cols], dSb)
                prod_st.append(q["xt"][:, cols] * U)
                dh_ref[lo:lo + 128, :] = _pair_decay(alast, h0) * dS + dhp_off
                dsh = dS * hp
                for k in range(2):
                    total = jnp.sum(dsh[k * SSD_HEAD_DIM:(k + 1) * SSD_HEAD_DIM, :], axis=(0, 1), keepdims=True)
                    hsum_row = hsum_row + jnp.where(lane_row == h0 + k, total, 0.0)
                dxt.append(dxt_p)
            dGb = dG.astype(BF16)
            dcs.append(dcg + _dot(dGb, bg))
            dbs.append(dbg + _dot_tn(dGb, cg))
        dxt = jnp.concatenate(dxt, axis=1)
        da_rows = da_rows + _gather_heads(jnp.concatenate(prod_off, axis=1), E)
        dww = _gather_heads(jnp.concatenate(prod_st, axis=1), E) * jnp.exp(alast - acum)
        da_rows = da_rows - dww
        dlast = _colsum(dww) + jnp.exp(alast) * hsum_row
        triT = q["triT"]
        ddA = (_sum3(lambda part: _dot(triT, part), _split3(da_rows))
               - _sum3(lambda part: _dot_nt(triT, part), _split3(daT)) + dlast)
        ddA = jnp.where(q["head"], ddA, 0.0)
        ddt = ddA * q["a128"] + _gather_heads(dxt * xs, E)
        ddt_raw = jnp.where(q["head"], ddt * _sigmoid(q["pre"]), 0.0)
        ddtk_ref[0] = ddt_raw
        dxs = dxt * q["dt_f"] + dsk_f * dY
        dxbc_ref[0] = jnp.concatenate([dxs] + dbs + dcs, axis=1).astype(BF16)
        dvec_ref[0:1, :] += _colsum(ddt_raw)
        dvec_ref[1:2, :] += _colsum(ddA * q["dt"]) * q["a128"]

        @pl.when((pl.program_id(0) == B - 1) & (pl.program_id(1) == nc - 1))
        def _():
            dvec_ref[2:3, :] = _gather_heads(jnp.broadcast_to(dskc_ref[...], (8, D_SSD)), E)[0:1, :]

    rev = lambda n: pl.BlockSpec((1, L, n), lambda b, c: (b, nc - 1 - c, 0))
    vec = pl.BlockSpec((1, LANES), lambda b, c: (0, 0))
    sd = jax.ShapeDtypeStruct
    return pl.pallas_call(
        body, grid=(B, nc), name="ssd_bwd",
        in_specs=[rev(D_CONV), rev(LANES), rev(D_SSD), rev(D_SSD),
                  pl.BlockSpec((1, 1, D_SSD, SSD_STATE), lambda b, c: (b, nc - 1 - c, 0, 0)), rev(D_SSD), vec, vec,
                  pl.BlockSpec((1, D_SSD), lambda b, c: (0, 0)),
                  pl.BlockSpec((1, D_SSD), lambda b, c: (0, 0)), pl.BlockSpec((LANES, D_SSD), lambda b, c: (0, 0))],
        out_specs=[rev(D_CONV), rev(LANES), rev(D_SSD), pl.BlockSpec((1, D_SSD), lambda b, c: (0, 0)),
                   pl.BlockSpec((8, LANES), lambda b, c: (0, 0))],
        out_shape=[sd((B, S, D_CONV), BF16), sd((B, S, LANES), F32), sd((B, S, D_SSD), BF16), sd((1, D_SSD), F32),
                   sd((8, LANES), F32)],
        scratch_shapes=[pltpu.VMEM((D_SSD, SSD_STATE), F32), pltpu.VMEM((1, D_SSD), F32)],
        compiler_params=_cparams(("arbitrary", "arbitrary")),
    )(xbc, dtk, z, y, prev, dys, dtb, alog, dsk, nw, expand)


def _rope_tables(pos_ref, invf_ref, place_ref):
    ang = invf_ref[...] * pos_ref[0].astype(F32)
    place = place_ref[...]
    cosf = 1.0 + _sum3(lambda part: _dot_tn(part, place), _split3(jnp.cos(ang) - 1.0))
    sinf = _sum3(lambda part: _dot_tn(part, place), _split3(jnp.sin(ang)))
    return cosf, sinf


def _rot(u):
    lane = lax.broadcasted_iota(jnp.int32, u.shape, 1)
    first = (lane >= QK_NOPE) & (lane < QK_NOPE + QK_ROPE // 2)
    second = (lane >= QK_NOPE + QK_ROPE // 2) & (lane < QK_DIM)
    return jnp.where(first, -pltpu.roll(u, LANES - QK_ROPE // 2, 1), jnp.where(second, pltpu.roll(u, QK_ROPE // 2, 1), 0.0))


def _rope_lanes(shape):
    lane = lax.broadcasted_iota(jnp.int32, shape, 1)
    return (lane >= QK_NOPE) & (lane < QK_DIM)


def _mla_prep(cq, ckv, dtk, pos, qw, kvw, wuq, wukv, invf, place):
    T = cq.shape[0]
    tm = min(TOKEN_TILE, T)
    scale = 1.0 / math.sqrt(QK_DIM)
    HW = MLA_HEADS * HEAD_LANES

    def body(cq_ref, ckv_ref, dtk_ref, pos_ref, qw_ref, kvw_ref, wuq_ref, wukv_ref, invf_ref, place_ref, q_ref, k_ref, v_ref,
             cos_ref, sin_ref):
        xh, _ = _rms_stats(cq_ref[...])
        qv = _dot((xh * qw_ref[...]).astype(BF16), wuq_ref[...])
        xh, _ = _rms_stats(ckv_ref[...])
        kv = _dot((xh * kvw_ref[...]).astype(BF16), wukv_ref[...])
        cosf, sinf = _rope_tables(pos_ref, invf_ref, place_ref)
        cos_ref[...] = cosf
        sin_ref[...] = sinf
        rope = lambda u: u * cosf + _rot(u) * sinf
        dtkv = dtk_ref[...]
        kr = rope(jnp.where(_rope_lanes(dtkv.shape), dtkv, 0.0))
        for h in range(MLA_HEADS):
            cols = slice(h * HEAD_LANES, (h + 1) * HEAD_LANES)
            q_ref[:, cols] = (rope(qv[:, cols]) * scale).astype(BF16)
            k_ref[:, cols] = (kv[:, cols] + kr).astype(BF16)
        v_ref[...] = kv[:, HW:].astype(BF16)

    rows = lambda n: pl.BlockSpec((tm, n), lambda i: (i, 0))
    return pl.pallas_call(
        body, grid=(T // tm,), name="mla_prep",
        in_specs=[rows(Q_LORA), rows(KV_LORA), rows(LANES), pl.BlockSpec((1, 1, tm), lambda i: (i, 0, 0)),
                  _resident((1, Q_LORA)), _resident((1, KV_LORA)), _resident((Q_LORA, HW)), _resident((KV_LORA, 2 * HW)),
                  _resident((QK_ROPE // 2, 1)), _resident((QK_ROPE // 2, LANES))],
        out_specs=[rows(HW), rows(HW), rows(HW), rows(LANES), rows(LANES)],
        out_shape=[jax.ShapeDtypeStruct((T, HW), BF16)] * 3 + [jax.ShapeDtypeStruct((T, LANES), F32)] * 2,
        compiler_params=_cparams(("arbitrary",)),
    )(cq, ckv, dtk, pos.reshape(T // tm, 1, tm), qw, kvw, wuq, wukv, invf, place)


def _mla_prep_bwd(dq, dk, dv, cq, ckv, cos_t, sin_t, qw, kvw, wuq, wukv):
    T = cq.shape[0]
    tm = min(TOKEN_TILE, T)
    scale = 1.0 / math.sqrt(QK_DIM)
    HW = MLA_HEADS * HEAD_LANES

    def body(dq_ref, dk_ref, dv_ref, cq_ref, ckv_ref, cos_ref, sin_ref, qw_ref, kvw_ref, wuq_ref, wukv_ref,
             dcq_ref, dckv_ref, ddtk_ref, qn_ref, kvn_ref, dqo_ref, dkvo_ref, dqw_ref, dkvw_ref):
        @pl.when(pl.program_id(0) == 0)
        def _():
            dqw_ref[...] = jnp.zeros_like(dqw_ref)
            dkvw_ref[...] = jnp.zeros_like(dkvw_ref)

        cosf, sinf = cos_ref[...], sin_ref[...]
        unrope = lambda d: d * cosf - _rot(d * sinf)
        dkr = jnp.zeros((tm, LANES), F32)
        nope = lax.broadcasted_iota(jnp.int32, (tm, LANES), 1) < QK_NOPE
        for h in range(MLA_HEADS):
            cols = slice(h * HEAD_LANES, (h + 1) * HEAD_LANES)
            dqo_ref[:, cols] = unrope(dq_ref[:, cols].astype(F32) * scale).astype(BF16)
            dkh = dk_ref[:, cols].astype(F32)
            dkr = dkr + jnp.where(_rope_lanes(dkh.shape), dkh, 0.0)
            dkvo_ref[:, cols] = jnp.where(nope, dkh, 0.0).astype(BF16)
        dkvo_ref[:, HW:] = dv_ref[...].astype(BF16)
        ddtk_ref[...] = unrope(dkr)
        xh, r = _rms_stats(cq_ref[...])
        qn_ref[...] = (xh * qw_ref[...]).astype(BF16)
        dx, dw_rows = _rms_bwd(_dot_nt(dqo_ref[...], wuq_ref[...]), xh, r, qw_ref[...])
        dcq_ref[...] = dx
        dqw_ref[...] += _colsum(dw_rows)
        xh, r = _rms_stats(ckv_ref[...])
        kvn_ref[...] = (xh * kvw_ref[...]).astype(BF16)
        dx, dw_rows = _rms_bwd(_dot_nt(dkvo_ref[...], wukv_ref[...]), xh, r, kvw_ref[...])
        dckv_ref[...] = dx
        dkvw_ref[...] += _colsum(dw_rows)

    rows = lambda n: pl.BlockSpec((tm, n), lambda i: (i, 0))
    sd = jax.ShapeDtypeStruct
    return pl.pallas_call(
        body, grid=(T // tm,), name="mla_prep_bwd",
        in_specs=[rows(HW), rows(HW), rows(HW), rows(Q_LORA), rows(KV_LORA), rows(LANES), rows(LANES), _resident((1, Q_LORA)),
                  _resident((1, KV_LORA)), _resident((Q_LORA, HW)), _resident((KV_LORA, 2 * HW))],
        out_specs=[rows(Q_LORA), rows(KV_LORA), rows(LANES), rows(Q_LORA), rows(KV_LORA), rows(HW), rows(2 * HW),
                   pl.BlockSpec((1, Q_LORA), lambda i: (0, 0)), pl.BlockSpec((1, KV_LORA), lambda i: (0, 0))],
        out_shape=[sd((T, Q_LORA), F32), sd((T, KV_LORA), F32), sd((T, LANES), F32), sd((T, Q_LORA), BF16),
                   sd((T, KV_LORA), BF16), sd((T, HW), BF16), sd((T, 2 * HW), BF16), sd((1, Q_LORA), F32),
                   sd((1, KV_LORA), F32)],
        compiler_params=_cparams(("arbitrary",)),
    )(dq, dk, dv, cq, ckv, cos_t, sin_t, qw, kvw, wuq, wukv)


def _causal_mask(t):
    row = lax.broadcasted_iota(jnp.int32, (t, t), 0)
    col = lax.broadcasted_iota(jnp.int32, (t, t), 1)
    return col <= row


def _attn_fwd(q, k, v):
    B, S, HW = q.shape
    H = HW // HEAD_LANES
    t = min(ATTN_FWD_Q_TILE, S)
    tk = min(ATTN_FWD_KV_TILE, t)
    nq = S // t
    per = t // tk

    pair = 2
    pw = pair * HEAD_LANES

    def body(q_ref, k_ref, v_ref, o_ref, lse_ref):
        qi = pl.program_id(2)
        lanes = [slice(hh * HEAD_LANES, (hh + 1) * HEAD_LANES) for hh in range(pair)]
        qs = [q_ref[0, :, cols] for cols in lanes]

        def step(j, carry, diag):
            sl = pl.ds(pl.multiple_of(j * tk, tk), tk)
            out = []
            for qv, cols, (m, l, acc) in zip(qs, lanes, carry):
                s = _dot_nt(qv, k_ref[0, sl, cols])
                if diag is not None:
                    row = lax.broadcasted_iota(jnp.int32, (t, tk), 0)
                    col = lax.broadcasted_iota(jnp.int32, (t, tk), 1)
                    s = jnp.where(col + diag * tk <= row, s, -1e30)
                m_new = jnp.maximum(m, jnp.max(s, axis=-1, keepdims=True))
                alpha = jnp.exp(m - m_new)
                p = jnp.exp(s - m_new)
                l = alpha * l + jnp.sum(p, axis=-1, keepdims=True)
                acc = alpha * acc + _dot(p.astype(BF16), v_ref[0, sl, cols])
                out.append((m_new, l, acc))
            return tuple(out)

        init = tuple((jnp.full((t, 1), -1e30, F32), jnp.zeros((t, 1), F32), jnp.zeros((t, HEAD_LANES), F32))
                     for _ in range(pair))
        carry = lax.fori_loop(0, qi * per, lambda j, c: step(j, c, None), init)
        for d in range(per):
            carry = step(qi * per + d, carry, d)
        for hh, (m, l, acc) in enumerate(carry):
            o_ref[0, :, lanes[hh]] = (acc / l).astype(BF16)
            lse_ref[0, hh] = m + jnp.log(l)

    return pl.pallas_call(
        body, grid=(B, H // pair, nq), name="attn_fwd",
        in_specs=[pl.BlockSpec((1, t, pw), lambda b, h, i: (b, i, h)),
                  pl.BlockSpec((1, S, pw), lambda b, h, i: (b, 0, h)),
                  pl.BlockSpec((1, S, pw), lambda b, h, i: (b, 0, h))],
        out_specs=[pl.BlockSpec((1, t, pw), lambda b, h, i: (b, i, h)),
                   pl.BlockSpec((1, pair, t, 1), lambda b, h, i: (b, h, i, 0))],
        out_shape=[jax.ShapeDtypeStruct((B, S, HW), BF16), jax.ShapeDtypeStruct((B, H, S, 1), F32)],
        compiler_params=_cparams(("arbitrary", "arbitrary", "arbitrary")),
    )(q, k, v)


def _attn_bwd(q, k, v, o, do, lse):
    B, S, HW = q.shape
    H = HW // HEAD_LANES
    t = min(ATTN_BWD_TILE, S)
    nq = S // t

    def body(q_ref, k_ref, v_ref, o_ref, do_ref, lse_ref, dq_out_ref, dk_ref, dv_ref, dq_ref):
        j = pl.program_id(2)

        @pl.when(j == 0)
        def _():
            dq_ref[...] = jnp.zeros_like(dq_ref)

        kj = k_ref[0]
        vj = v_ref[0]

        def step(i, carry, masked):
            dk, dv = carry
            sl = pl.ds(pl.multiple_of(i * t, t), t)
            qi = q_ref[0, sl, :]
            doi = do_ref[0, sl, :]
            s = _dot_nt(qi, kj)
            if masked:
                s = jnp.where(_causal_mask(t), s, -1e30)
            p = jnp.exp(s - lse_ref[0, 0, sl, :])
            dv = dv + _dot_tn(p.astype(BF16), doi)
            dp = _dot_nt(doi, vj)
            delta = jnp.sum(doi.astype(F32) * o_ref[0, sl, :].astype(F32), axis=-1, keepdims=True)
            dsb = (p * (dp - delta)).astype(BF16)
            dk = dk + _dot_tn(dsb, qi)
            dq_ref[sl, :] += _dot(dsb, kj)
            return dk, dv

        zero = jnp.zeros((t, HEAD_LANES), F32)
        carry = step(j, (zero, zero), True)
        dk, dv = lax.fori_loop(j + 1, nq, lambda i, c: step(i, c, False), carry)
        dk_ref[0] = dk.astype(BF16)
        dv_ref[0] = dv.astype(BF16)

        @pl.when(j == nq - 1)
        def _():
            dq_out_ref[0] = dq_ref[...].astype(BF16)

    full = pl.BlockSpec((1, S, HEAD_LANES), lambda b, h, j: (b, 0, h))
    tile = pl.BlockSpec((1, t, HEAD_LANES), lambda b, h, j: (b, j, h))
    sd = jax.ShapeDtypeStruct
    return pl.pallas_call(
        body, grid=(B, H, nq), name="attn_bwd",
        in_specs=[full, tile, tile, full, full, pl.BlockSpec((1, 1, S, 1), lambda b, h, j: (b, h, 0, 0))],
        out_specs=[full, tile, tile],
        out_shape=[sd((B, S, HW), BF16), sd((B, S, HW), BF16), sd((B, S, HW), BF16)],
        scratch_shapes=[pltpu.VMEM((S, HEAD_LANES), F32)],
        compiler_params=_cparams(("arbitrary", "arbitrary", "arbitrary")),
    )(q, k, v, o, do, lse)


def _mix_out(x1, yssd, o, mw, wout, g, seq):
    T, D = x1.shape
    tm = min(TOKEN_TILE, seq)
    tps = seq // tm

    def body(x_ref, ys_ref, o_ref, mw_ref, w_ref, g_ref, xo_ref, m_ref, yc_ref):
        xh, _ = _rms_stats(o_ref[...].astype(F32))
        ycat = jnp.concatenate([ys_ref[...], (xh * mw_ref[...]).astype(BF16)], axis=1)
        m = _dot(ycat, w_ref[...])
        xo_ref[...] = x_ref[...] + g_ref[0] * m
        m_ref[...] = m.astype(BF16)
        yc_ref[...] = ycat

    rows = lambda n: pl.BlockSpec((tm, n), lambda i: (i, 0))
    perb = pl.BlockSpec((1, 1, D), lambda i: (i // tps, 0, 0))
    sd = jax.ShapeDtypeStruct
    return pl.pallas_call(
        body, grid=(T // tm,), name="mix_out",
        in_specs=[rows(D), rows(D_SSD), rows(D_MLA), _resident((1, D_MLA)), _resident((D_SSD + D_MLA, D)), perb],
        out_specs=[rows(D), rows(D), rows(D_SSD + D_MLA)],
        out_shape=[sd((T, D), F32), sd((T, D), BF16), sd((T, D_SSD + D_MLA), BF16)],
        compiler_params=_cparams(("arbitrary",)),
    )(x1, yssd, o, mw, wout, g)


def _mix_out_bwd(dx2, m, o, mw, wout, g, seq):
    T, D = dx2.shape
    B = T // seq
    tm = min(TOKEN_TILE, seq)
    tps = seq // tm

    def body(dx_ref, m_ref, o_ref, mw_ref, w_ref, g_ref, dys_ref, do_ref, dm_ref, dg_ref, dmw_ref):
        i = pl.program_id(0)

        @pl.when(i % tps == 0)
        def _():
            dg_ref[...] = jnp.zeros_like(dg_ref)

        @pl.when(i == 0)
        def _():
            dmw_ref[...] = jnp.zeros_like(dmw_ref)

        dxv = dx_ref[...]
        dg_ref[0] += _colsum(dxv * m_ref[...].astype(F32))
        dmb = (g_ref[0] * dxv).astype(BF16)
        dm_ref[...] = dmb
        dycat = _dot_nt(dmb, w_ref[...])
        dys_ref[...] = dycat[:, :D_SSD].astype(BF16)
        xh, r = _rms_stats(o_ref[...].astype(F32))
        dx, dw_rows = _rms_bwd(dycat[:, D_SSD:], xh, r, mw_ref[...])
        do_ref[...] = dx.astype(BF16)
        dmw_ref[...] += _colsum(dw_rows)

    rows = lambda n: pl.BlockSpec((tm, n), lambda i: (i, 0))
    perb = pl.BlockSpec((1, 1, D), lambda i: (i // tps, 0, 0))
    sd = jax.ShapeDtypeStruct
    return pl.pallas_call(
        body, grid=(T // tm,), name="mix_out_bwd",
        in_specs=[rows(D), rows(D), rows(D_MLA), _resident((1, D_MLA)), _resident((D_SSD + D_MLA, D)), perb],
        out_specs=[rows(D_SSD), rows(D_MLA), rows(D), perb, pl.BlockSpec((1, D_MLA), lambda i: (0, 0))],
        out_shape=[sd((T, D_SSD), BF16), sd((T, D_MLA), BF16), sd((T, D), BF16), sd((B, 1, D), F32), sd((1, D_MLA), F32)],
        compiler_params=_cparams(("arbitrary",)),
    )(dx2, m, o, mw, wout, g)


def _win_to_kernel(w):
    z0 = jnp.zeros((48, w.shape[1]), w.dtype)
    z1 = jnp.zeros((32, w.shape[1]), w.dtype)
    return jnp.concatenate([w[:2560], w[2576:3216], w[2560:2576], z0, w[3216:3248], z1], axis=0)


def _win_from_kernel(g):
    return jnp.concatenate([g[:2560], g[3200:3216], g[2560:3200], g[3264:3296]], axis=0)


def _wuq_to_kernel(w):
    w = w.reshape(Q_LORA, MLA_HEADS, QK_DIM)
    return jnp.pad(w, ((0, 0), (0, 0), (0, HEAD_LANES - QK_DIM))).reshape(Q_LORA, MLA_HEADS * HEAD_LANES)


def _wuq_from_kernel(g):
    return g.reshape(Q_LORA, MLA_HEADS, HEAD_LANES)[:, :, :QK_DIM].reshape(Q_LORA, MLA_HEADS * QK_DIM)


def _wukv_to_kernel(w):
    w = w.reshape(KV_LORA, MLA_HEADS, QK_NOPE + V_HEAD)
    kp = jnp.pad(w[:, :, :QK_NOPE], ((0, 0), (0, 0), (0, HEAD_LANES - QK_NOPE)))
    return jnp.concatenate([kp.reshape(KV_LORA, -1), w[:, :, QK_NOPE:].reshape(KV_LORA, -1)], axis=1)


def _wukv_from_kernel(g):
    hw = MLA_HEADS * HEAD_LANES
    kp = g[:, :hw].reshape(KV_LORA, MLA_HEADS, HEAD_LANES)[:, :, :QK_NOPE]
    vp = g[:, hw:].reshape(KV_LORA, MLA_HEADS, V_HEAD)
    return jnp.concatenate([kp, vp], axis=2).reshape(KV_LORA, MLA_HEADS * (QK_NOPE + V_HEAD))


def _lanes16(v):
    return jnp.pad(v.reshape(1, SSD_HEADS), ((0, 0), (0, LANES - SSD_HEADS)))


def _constants():
    e = np.zeros((LANES, D_SSD), np.float32)
    for h in range(SSD_HEADS):
        e[h, h * SSD_HEAD_DIM:(h + 1) * SSD_HEAD_DIM] = 1.0
    inv_freq = ROPE_THETA ** (-jnp.arange(0, QK_ROPE, 2, dtype=F32) / QK_ROPE)
    half = QK_ROPE // 2
    place = np.zeros((half, LANES), np.float32)
    for j in range(half):
        place[j, QK_NOPE + j] = place[j, QK_NOPE + half + j] = 1.0
    return jnp.asarray(e, BF16), inv_freq.reshape(half, 1), jnp.asarray(place, BF16)


def _local_step(x, positions, mod, w, later_weights, small, tgt, on_grads):
    B, S, D = x.shape
    T = B * S
    expand, invf, place = _constants()
    x0 = x.reshape(T, D)
    pos = positions.reshape(T)
    mods = [mod[:, i * D:(i + 1) * D].reshape(B, 1, D) for i in range(N_MOD)]
    sh1, sc1, g1, sh2, sc2, g2, sh3, sc3, g3 = mods
    dtb, alog = _lanes16(small["dt_bias"]), _lanes16(small["a_log"])
    dsk = jnp.repeat(small["d_skip"].reshape(1, SSD_HEADS), SSD_HEAD_DIM, axis=1)

    x1, a1, u1, f1 = _ffn_fwd(x0, small["norm_ffn1"], sh1, sc1, g1, w["ffn1_w_gate"], w["ffn1_w_up"], w["ffn1_w_down"], S, "ffn1_fwd")
    w = {**w, **later_weights(f1)}
    z, xraw, cq, ckv, dtk = _inproj_fwd(x1, small["norm_mix"], sh2, sc2, w["w_in"], S)
    xraw3 = xraw.reshape(B, S, D_CONV)
    xbc = _conv_fwd(xraw3, small["conv_w"], small["conv_b"])
    dtk3, z3 = dtk.reshape(B, S, LANES), z.reshape(B, S, D_SSD)
    y, yssd, prev = _ssd_fwd(xbc, dtk3, z3, dtb, alog, dsk, small["ssd_norm_w"], expand)
    q, k, v, cos_t, sin_t = _mla_prep(cq, ckv, dtk, pos, small["q_norm_w"], small["kv_norm_w"], w["w_uq"], w["w_ukv"], invf,
                                      place)
    hw = MLA_HEADS * HEAD_LANES
    q3, k3, v3 = q.reshape(B, S, hw), k.reshape(B, S, hw), v.reshape(B, S, hw)
    o3, lse = _attn_fwd(q3, k3, v3)
    o = o3.reshape(T, hw)
    x2, m, ycat = _mix_out(x1, yssd.reshape(T, D_SSD), o, small["mla_norm_w"], w["w_out"], g2, S)
    dx3, a2, u2, f2, loss, d_norm_final = _ffn_fwd(
        x2, small["norm_ffn2"], sh3, sc3, g3, w["ffn2_w_gate"], w["ffn2_w_up"], w["ffn2_w_down"], S, "ffn2_fwd",
        head=(small["norm_final"].reshape(1, D), tgt.reshape(T, D)))

    gw, gs = {}, {}
    dx2, h3, s3, df3, da3, du3, dsh3, dsc3, dg3, gs["norm_ffn2"] = _ffn_bwd(
        dx3, x2, small["norm_ffn2"], sh3, sc3, g3, a2, u2, f2, w["ffn2_w_gate"], w["ffn2_w_up"], w["ffn2_w_down"], S, "ffn2_bwd")
    gw["ffn2_w_gate"], gw["ffn2_w_up"], gw["ffn2_w_down"] = _ffn_wgrad(h3, s3, df3, da3, du3, "ffn2_wgrad")
    g2 = g2 + on_grads(("ffn2_w_gate", "ffn2_w_up", "ffn2_w_down"), gw)

    dys, do, dm, dg2, gs["mla_norm_w"] = _mix_out_bwd(dx2, m, o, small["mla_norm_w"], w["w_out"], g2, S)
    gw["w_out"] = _mm_tn(ycat, dm, 512, "dwout")

    dq3, dk3, dv3 = _attn_bwd(q3, k3, v3, o3, do.reshape(B, S, hw), lse)
    dcq, dckv, ddtk_b, qn, kvn, dqb, dkvb, gs["q_norm_w"], gs["kv_norm_w"] = _mla_prep_bwd(
        dq3.reshape(T, hw), dk3.reshape(T, hw), dv3.reshape(T, hw), cq, ckv, cos_t, sin_t, small["q_norm_w"],
        small["kv_norm_w"], w["w_uq"], w["w_ukv"])
    gw["w_uq"] = _mm_tn(qn, dqb, 512, "dwuq")
    gw["w_ukv"] = _mm_tn(kvn, dkvb, 1024, "dwukv")

    dxbc, ddtk_a, dz, gs["ssd_norm_w"], dvec = _ssd_bwd(
        xbc, dtk3, z3, y, prev, dys.reshape(B, S, D_SSD), dtb, alog, dsk, small["ssd_norm_w"], expand)
    gs["dt_bias"], gs["a_log"], gs["d_skip"] = dvec[0:1, :SSD_HEADS], dvec[1:2, :SSD_HEADS], dvec[2:3, :SSD_HEADS]
    dxraw, gs["conv_w"], gs["conv_b"] = _conv_bwd(dxbc, xraw3, small["conv_w"], small["conv_b"])
    dx1, h2, dproj, dsh2, dsc2, gs["norm_mix"] = _inproj_bwd(
        dx2, x1, small["norm_mix"], sh2, sc2, w["w_in"], dz.reshape(T, D_SSD), dxraw.reshape(T, D_CONV), dcq, dckv,
        ddtk_a.reshape(T, LANES), ddtk_b, S)
    gw["w_in"] = _mm_tn(dproj, h2, 512, "dwin")
    g1 = g1 + on_grads(("w_in", "w_uq", "w_ukv", "w_out"), gw)

    dx0, h1, s1, df1, da1, du1, dsh1, dsc1, dg1, gs["norm_ffn1"] = _ffn_bwd(
        dx1, x0, small["norm_ffn1"], sh1, sc1, g1, a1, u1, f1, w["ffn1_w_gate"], w["ffn1_w_up"], w["ffn1_w_down"], S, "ffn1_bwd")
    gw["ffn1_w_gate"], gw["ffn1_w_up"], gw["ffn1_w_down"] = _ffn_wgrad(h1, s1, df1, da1, du1, "ffn1_wgrad")
    gs["norm_final"] = d_norm_final
    dmod = jnp.concatenate([t.reshape(B, D) for t in (dsh1, dsc1, dg1, dsh2, dsc2, dg2, dsh3, dsc3, dg3)], axis=1)
    return loss, dx0.reshape(B, S, D), gw, dmod, gs


HBM_SPEC = pl.BlockSpec(memory_space=pltpu.HBM)
VMEM_SPEC = pl.BlockSpec(memory_space=pltpu.VMEM)


def _place():
    return lax.axis_index("x"), lax.axis_index("y"), lax.axis_index("c")


def _other_chips(mx, my):
    return [(1 - mx, my), (mx, 1 - my), (1 - mx, 1 - my)]


def _remote(src, dst, send_sem, recv_sem, to):
    return pltpu.make_async_remote_copy(src_ref=src, dst_ref=dst, send_sem=send_sem, recv_sem=recv_sem,
                                        device_id=to, device_id_type=MESH)


def _all_gather_small(xa, name):
    r, n = xa.shape

    def body(x_ref, o_ref, token, send_sems, recv_sems):
        mx, my, mc = _place()
        me = 4 * mx + 2 * my + mc
        token[...] = jnp.zeros_like(token)
        o_ref[pl.ds(me, 1)] = x_ref[...][None]
        sends = []
        for k in range(1, N_DEV):
            peer = (mx ^ (k >> 2), my ^ ((k >> 1) & 1), mc ^ (k & 1))
            cp = _remote(x_ref, o_ref.at[me], send_sems.at[k - 1], recv_sems.at[k - 1], peer)
            cp.start()
            sends.append(cp)
        for k in range(1, N_DEV):
            peer = (mx ^ (k >> 2), my ^ ((k >> 1) & 1), mc ^ (k & 1))
            slot = 4 * peer[0] + 2 * peer[1] + peer[2]
            _remote(x_ref, o_ref.at[slot], send_sems.at[k - 1], recv_sems.at[k - 1], peer).wait_recv()
        for cp in sends:
            cp.wait_send()

    return pl.pallas_call(
        body, name=name, in_specs=[VMEM_SPEC], out_specs=[VMEM_SPEC, VMEM_SPEC],
        out_shape=[jax.ShapeDtypeStruct((N_DEV, r, n), xa.dtype), jax.ShapeDtypeStruct((8, LANES), F32)],
        scratch_shapes=[pltpu.SemaphoreType.DMA((N_DEV - 1,)), pltpu.SemaphoreType.DMA((N_DEV - 1,))],
        compiler_params=pltpu.CompilerParams(vmem_limit_bytes=VMEM_LIMIT),
    )(xa)


def _halves_by_rows(shape):
    return (shape[-2] // 2) % 16 == 0


def _half_shape(shape):
    r, c = shape[-2:]
    return tuple(shape[:-2]) + ((r // 2, c) if _halves_by_rows(shape) else (r, c // 2))


def _half_index(shape, hc):
    r, c = shape[-2:]
    if _halves_by_rows(shape):
        return (pl.ds(pl.multiple_of(hc * (r // 2), 16), r // 2), slice(None))
    return (slice(None), pl.ds(pl.multiple_of(hc * (c // 2), LANES), c // 2))


def _half(ref, hc, lead=None):
    idx = _half_index(ref.shape, hc)
    return ref.at[idx] if lead is None else ref.at[(lead,) + idx]


def _gather_weights(shards):
    n = len(shards)

    def body(*refs):
        w_refs, o_refs, token = refs[:n], refs[n:2 * n], refs[2 * n]
        send_sems, recv_sems, stage_sems = refs[2 * n + 1:2 * n + 4]
        stages = refs[2 * n + 4:]
        mx, my, mc = _place()
        chip = 2 * mx + my
        others = _other_chips(mx, my)
        sibling = (mx, my, 1 - mc)
        token[...] = jnp.zeros_like(token)
        stage_in = [pltpu.make_async_copy(w, st, stage_sems.at[0, i]) for i, (w, st) in enumerate(zip(w_refs, stages))]
        for cp in stage_in:
            cp.start()
        first = []
        for i, (w, o) in enumerate(zip(w_refs, o_refs)):
            for k, (cx, cy) in enumerate(others):
                first.append(_remote(_half(w, mc), _half(o, mc, chip), send_sems.at[i, k],
                                     recv_sems.at[i, k], (cx, cy, mc)))
                first[-1].start()
        stage_out = []
        for i, (st, o) in enumerate(zip(stages, o_refs)):
            stage_in[i].wait()
            stage_out.append(pltpu.make_async_copy(st, o.at[chip], stage_sems.at[1, i]))
            stage_out[-1].start()
        passed = []
        for i, (w, o) in enumerate(zip(w_refs, o_refs)):
            for k, (cx, cy) in enumerate(others):
                landed = _half(o, mc, 2 * cx + cy)
                _remote(landed, landed, send_sems.at[i, k], recv_sems.at[i, k], (cx, cy, mc)).wait_recv()
                passed.append(_remote(landed, landed, send_sems.at[i, 3 + k], recv_sems.at[i, 3 + k], sibling))
                passed[-1].start()
        for i, (w, o) in enumerate(zip(w_refs, o_refs)):
            for k, (cx, cy) in enumerate(others):
                there = _half(o, 1 - mc, 2 * cx + cy)
                _remote(there, there, send_sems.at[i, 3 + k], recv_sems.at[i, 3 + k], sibling).wait_recv()
        for cp in first + passed:
            cp.wait_send()
        for cp in stage_out:
            cp.wait()

    out = pl.pallas_call(
        body, name="gather_weights", in_specs=[HBM_SPEC] * n, out_specs=[HBM_SPEC] * n + [VMEM_SPEC],
        out_shape=[jax.ShapeDtypeStruct((N_CHIPS,) + s.shape, s.dtype) for s in shards] + [jax.ShapeDtypeStruct((8, LANES), F32)],
        scratch_shapes=[pltpu.SemaphoreType.DMA((n, 6)), pltpu.SemaphoreType.DMA((n, 6)), pltpu.SemaphoreType.DMA((2, n))]
        + [pltpu.VMEM(s.shape, s.dtype) for s in shards],
        compiler_params=pltpu.CompilerParams(vmem_limit_bytes=VMEM_LIMIT),
    )(*shards)
    return out[:n], out[n]


SEM_SPEC = pl.BlockSpec(memory_space=pltpu.SEMAPHORE)
ANY_SPEC = pl.BlockSpec(memory_space=pl.ANY)
DATAFLOW = pltpu.SideEffectType.DATAFLOW_SIDE_EFFECTING


def _hbm(arr):
    return pltpu.with_memory_space_constraint(arr, pltpu.HBM)


def _gather_start(shards):
    n = len(shards)

    def body(*refs):
        w_refs, land_refs, send_sems, recv_sems, token = refs[:n], refs[n:2 * n], refs[2 * n], refs[2 * n + 1], refs[-1]
        mx, my, mc = _place()
        chip = 2 * mx + my
        for i, (w, land) in enumerate(zip(w_refs, land_refs)):
            for k, (cx, cy) in enumerate(_other_chips(mx, my)):
                _remote(_half(w, mc), _half(land, mc, chip), send_sems.at[3 * i + k],
                        recv_sems.at[3 * i + k], (cx, cy, mc)).start()
        token[...] = jnp.zeros_like(token)

    lands = [lax.empty((N_CHIPS,) + s.shape, s.dtype) for s in shards]
    out = pl.pallas_call(
        body, name="gather_start",
        out_shape=(pltpu.SemaphoreType.DMA((3 * n,)), pltpu.SemaphoreType.DMA((3 * n,)),
                   *[pltpu.HBM(s.shape, s.dtype) for s in shards], *[pltpu.HBM(l.shape, l.dtype) for l in lands],
                   jax.ShapeDtypeStruct((8, LANES), F32)),
        in_specs=[HBM_SPEC] * (2 * n), out_specs=(SEM_SPEC, SEM_SPEC, *[HBM_SPEC] * (2 * n), VMEM_SPEC),
        input_output_aliases={i: 2 + i for i in range(2 * n)},
        compiler_params=pltpu.CompilerParams(has_side_effects=DATAFLOW),
    )(*[_hbm(s) for s in shards], *[_hbm(l) for l in lands])
    return out[0], out[1], out[2:2 + n], out[2 + n:2 + 2 * n], out[-1]


def _gather_wait(send_sems, recv_sems, shards, lands, after):
    n = len(shards)

    def body(*refs):
        w_refs, land_refs, send_sems, recv_sems = refs[:n], refs[n:2 * n], refs[2 * n], refs[2 * n + 1]
        mx, my, mc = _place()
        for i, (w, land) in enumerate(zip(w_refs, land_refs)):
            for k, (cx, cy) in enumerate(_other_chips(mx, my)):
                cp = _remote(_half(w, mc), _half(land, mc, 2 * cx + cy), send_sems.at[3 * i + k],
                             recv_sems.at[3 * i + k], (cx, cy, mc))
                cp.wait_send()
                cp.wait_recv()

    out = pl.pallas_call(
        body, name="gather_wait",
        out_shape=(*[pltpu.HBM(s.shape, s.dtype) for s in shards], *[pltpu.HBM(l.shape, l.dtype) for l in lands]),
        in_specs=[HBM_SPEC] * (2 * n) + [SEM_SPEC, SEM_SPEC, ANY_SPEC], out_specs=tuple([HBM_SPEC] * (2 * n)),
        input_output_aliases={i: i for i in range(2 * n)},
        compiler_params=pltpu.CompilerParams(has_side_effects=DATAFLOW),
    )(*shards, *lands, send_sems, recv_sems, after)
    return out[n:]


def _gather_finish(shards, lands):
    n = len(shards)

    def body(*refs):
        w_refs, land_refs, o_refs = refs[:n], refs[n:2 * n], refs[2 * n:3 * n]
        send_sems, recv_sems, stage_sems = refs[3 * n:3 * n + 3]
        stages = refs[3 * n + 3:]
        mx, my, mc = _place()
        chip = 2 * mx + my
        others = _other_chips(mx, my)
        sibling = (mx, my, 1 - mc)
        stage_in = [pltpu.make_async_copy(w, st, stage_sems.at[0, i]) for i, (w, st) in enumerate(zip(w_refs, stages))]
        for cp in stage_in:
            cp.start()
        passed = []
        for i, (w, o) in enumerate(zip(w_refs, o_refs)):
            for k, (cx, cy) in enumerate(others):
                landed = _half(o, mc, 2 * cx + cy)
                passed.append(_remote(landed, landed, send_sems.at[i, k], recv_sems.at[i, k], sibling))
                passed[-1].start()
        stage_out = []
        for i, (st, o) in enumerate(zip(stages, o_refs)):
            stage_in[i].wait()
            stage_out.append(pltpu.make_async_copy(st, o.at[chip], stage_sems.at[1, i]))
            stage_out[-1].start()
        for i, (w, o) in enumerate(zip(w_refs, o_refs)):
            for k, (cx, cy) in enumerate(others):
                there = _half(o, 1 - mc, 2 * cx + cy)
                _remote(there, there, send_sems.at[i, k], recv_sems.at[i, k], sibling).wait_recv()
        for cp in passed:
            cp.wait_send()
        for cp in stage_out:
            cp.wait()

    return pl.pallas_call(
        body, name="gather_finish", in_specs=[HBM_SPEC] * (2 * n), out_specs=[HBM_SPEC] * n,
        out_shape=[jax.ShapeDtypeStruct(l.shape, l.dtype) for l in lands],
        input_output_aliases={n + i: i for i in range(n)},
        scratch_shapes=[pltpu.SemaphoreType.DMA((n, 3)), pltpu.SemaphoreType.DMA((n, 3)), pltpu.SemaphoreType.DMA((2, n))]
        + [pltpu.VMEM(s.shape, s.dtype) for s in shards],
        compiler_params=pltpu.CompilerParams(vmem_limit_bytes=VMEM_LIMIT),
    )(*shards, *lands)


def _scatter_start(ss, tag):
    n = len(ss)

    def body(*refs):
        s_refs, land_refs, send_sems, recv_sems, token = refs[:n], refs[n:2 * n], refs[2 * n], refs[2 * n + 1], refs[-1]
        mx, my, mc = _place()
        chip = 2 * mx + my
        for i, (s, land) in enumerate(zip(s_refs, land_refs)):
            for k, (cx, cy) in enumerate(_other_chips(mx, my)):
                _remote(s.at[2 * cx + cy], land.at[chip], send_sems.at[3 * i + k], recv_sems.at[3 * i + k],
                        (cx, cy, mc)).start()
        token[...] = jnp.zeros_like(token)

    lands = [lax.empty(s.shape, s.dtype) for s in ss]
    out = pl.pallas_call(
        body, name="scatter_start_" + tag,
        out_shape=(pltpu.SemaphoreType.DMA((3 * n,)), pltpu.SemaphoreType.DMA((3 * n,)),
                   *[pltpu.HBM(s.shape, s.dtype) for s in ss], *[pltpu.HBM(l.shape, l.dtype) for l in lands],
                   jax.ShapeDtypeStruct((8, LANES), F32)),
        in_specs=[HBM_SPEC] * (2 * n), out_specs=(SEM_SPEC, SEM_SPEC, *[HBM_SPEC] * (2 * n), VMEM_SPEC),
        input_output_aliases={i: 2 + i for i in range(2 * n)},
        compiler_params=pltpu.CompilerParams(has_side_effects=DATAFLOW),
    )(*[_hbm(s) for s in ss], *[_hbm(l) for l in lands])
    return out[0], out[1], out[2:2 + n], out[2 + n:2 + 2 * n], out[-1]


def _scatter_wait(send_sems, recv_sems, ss, lands, after, tag):
    n = len(ss)

    def body(*refs):
        s_refs, land_refs, send_sems, recv_sems = refs[:n], refs[n:2 * n], refs[2 * n], refs[2 * n + 1]
        mx, my, mc = _place()
        for i, (s, land) in enumerate(zip(s_refs, land_refs)):
            for k, (cx, cy) in enumerate(_other_chips(mx, my)):
                slot = land.at[2 * cx + cy]
                cp = _remote(s.at[2 * cx + cy], slot, send_sems.at[3 * i + k], recv_sems.at[3 * i + k], (cx, cy, mc))
                cp.wait_send()
                cp.wait_recv()

    out = pl.pallas_call(
        body, name="scatter_wait_" + tag,
        out_shape=(*[pltpu.HBM(s.shape, s.dtype) for s in ss], *[pltpu.HBM(l.shape, l.dtype) for l in lands]),
        in_specs=[HBM_SPEC] * (2 * n) + [SEM_SPEC, SEM_SPEC, ANY_SPEC], out_specs=tuple([HBM_SPEC] * (2 * n)),
        input_output_aliases={i: i for i in range(2 * n)},
        compiler_params=pltpu.CompilerParams(has_side_effects=DATAFLOW),
    )(*ss, *lands, send_sems, recv_sems, after)
    return out[:n], out[n:]


def _swap_halves(gs, after, name):
    n = len(gs)

    def body(*refs):
        g_refs, o_refs, send_sems, recv_sems = refs[:n], refs[n + 1:2 * n + 1], refs[2 * n + 1], refs[2 * n + 2]
        mx, my, mc = _place()
        copies = []
        for i, (g, o) in enumerate(zip(g_refs, o_refs)):
            src = g.at[(slice(None),) + _half_index(g.shape, 1 - mc)]
            copies.append(_remote(src, o, send_sems.at[i], recv_sems.at[i], (mx, my, 1 - mc)))
            copies[-1].start()
        for cp in copies:
            cp.wait()

    return pl.pallas_call(
        body, name=name, in_specs=[HBM_SPEC] * n + [ANY_SPEC], out_specs=[HBM_SPEC] * n,
        out_shape=[jax.ShapeDtypeStruct(_half_shape(g.shape), g.dtype) for g in gs],
        scratch_shapes=[pltpu.SemaphoreType.DMA((n,)), pltpu.SemaphoreType.DMA((n,))],
    )(*gs, after)


def _pair_sum(g, got, core, name):
    hr, hc = _half_shape(g.shape)[1:]
    by_rows = _halves_by_rows(g.shape)

    def body(core_ref, g_ref, got_ref, o_ref):
        o_ref[...] = (g_ref[...].astype(F32) + got_ref[...].astype(F32)).astype(BF16)

    return pl.pallas_call(
        body, name=name,
        grid_spec=pltpu.PrefetchScalarGridSpec(
            num_scalar_prefetch=1, grid=(N_CHIPS,),
            in_specs=[pl.BlockSpec((1, hr, hc), lambda j, core_ref: (j, core_ref[0], 0) if by_rows else (j, 0, core_ref[0])),
                      pl.BlockSpec((1, hr, hc), lambda j, core_ref: (j, 0, 0))],
            out_specs=pl.BlockSpec((1, hr, hc), lambda j, core_ref: (j, 0, 0))),
        out_shape=jax.ShapeDtypeStruct((N_CHIPS, hr, hc), BF16),
        compiler_params=_cparams(("arbitrary",)),
    )(core, g, got)


def _chip_sum(own, got, chip, name):
    _, h, c = own.shape

    def body(chip_ref, a_ref, b_ref, c_ref, d_ref, o_ref):
        o_ref[...] = ((a_ref[0].astype(F32) + b_ref[0].astype(F32)) + c_ref[0].astype(F32)) + d_ref[0].astype(F32)

    slot = lambda flip: pl.BlockSpec((1, h, c), lambda i, chip_ref: (chip_ref[0] ^ flip, 0, 0))
    return pl.pallas_call(
        body, name=name,
        grid_spec=pltpu.PrefetchScalarGridSpec(
            num_scalar_prefetch=1, grid=(1,), in_specs=[slot(0), slot(1), slot(2), slot(3)],
            out_specs=pl.BlockSpec((h, c), lambda i, chip_ref: (0, 0))),
        out_shape=jax.ShapeDtypeStruct((h, c), F32),
        compiler_params=_cparams(("arbitrary",)),
    )(chip, own, got, got, got)


def _join_halves(mine, name):
    n = len(mine)

    def body(*refs):
        m_refs, o_refs, send_sems, recv_sems = refs[:n], refs[n:2 * n], refs[2 * n], refs[2 * n + 1]
        mx, my, mc = _place()
        copies = []
        for i, (m, o) in enumerate(zip(m_refs, o_refs)):
            copies.append(_remote(m, o, send_sems.at[i], recv_sems.at[i], (mx, my, 1 - mc)))
            copies[-1].start()
        for cp in copies:
            cp.wait()

    return pl.pallas_call(
        body, name=name, in_specs=[HBM_SPEC] * n, out_specs=[HBM_SPEC] * n,
        out_shape=[jax.ShapeDtypeStruct(m.shape, m.dtype) for m in mine],
        scratch_shapes=[pltpu.SemaphoreType.DMA((n,)), pltpu.SemaphoreType.DMA((n,))],
    )(*mine)


def _adam_math(w, g, m, v):
    m2 = ADAM_B1 * m + (1.0 - ADAM_B1) * g
    v2 = ADAM_B2 * v + (1.0 - ADAM_B2) * (g * g)
    m_hat = m2 * (1.0 / (1.0 - ADAM_B1 ** ADAM_STEP))
    v_hat = v2 * (1.0 / (1.0 - ADAM_B2 ** ADAM_STEP))
    delta = -ADAM_LR * (m_hat / (jnp.sqrt(v_hat) + ADAM_EPS) + ADAM_WD * w)
    return delta, m2, v2


def _adam(w, g, m, v, name):
    def body(w_ref, g_ref, m_ref, v_ref, d_ref, m2_ref, v2_ref):
        d_ref[...], m2_ref[...], v2_ref[...] = _adam_math(w_ref[...], g_ref[...], m_ref[...], v_ref[...])

    return pl.pallas_call(body, name=name, out_shape=[jax.ShapeDtypeStruct(w.shape, F32)] * 3)(w, g, m, v)


def _adam_halves(w, m, v, mine, theirs, core, name):
    hr, hcols = _half_shape(w.shape)[1:]
    by_rows = _halves_by_rows(w.shape)

    def body(core_ref, w_ref, m_ref, v_ref, mine_ref, theirs_ref, g_ref, d_ref, m2_ref, v2_ref):
        g = jnp.where(pl.program_id(0) == core_ref[0], mine_ref[...], theirs_ref[...])
        g_ref[0] = g
        d_ref[0], m2_ref[0], v2_ref[0] = _adam_math(w_ref[0], g, m_ref[0], v_ref[0])

    half = pl.BlockSpec((1, hr, hcols), lambda hc, core_ref: (0, hc, 0) if by_rows else (0, 0, hc))
    whole = pl.BlockSpec((hr, hcols), lambda hc, core_ref: (0, 0))
    return pl.pallas_call(
        body, name=name,
        grid_spec=pltpu.PrefetchScalarGridSpec(
            num_scalar_prefetch=1, grid=(2,), in_specs=[half, half, half, whole, whole], out_specs=[half] * 4),
        out_shape=[jax.ShapeDtypeStruct(w.shape, F32)] * 4,
        compiler_params=_cparams(("arbitrary",)),
    )(core, w, m, v, mine, theirs)


ADA_COLS = N_MOD * D_MODEL // N_CHIPS


def _ada_fwd(c_all, w_ada, b_cols):
    def body(c_ref, w_ref, b_ref, o_ref):
        cv = c_ref[...]
        act = (cv * _sigmoid(cv)).astype(BF16)
        o_ref[...] = _dot(act, w_ref[...].astype(BF16)) + b_ref[...]

    return pl.pallas_call(
        body, name="ada_fwd", out_shape=jax.ShapeDtypeStruct((c_all.shape[0], ADA_COLS), F32),
        compiler_params=pltpu.CompilerParams(vmem_limit_bytes=VMEM_LIMIT),
    )(c_all, w_ada, b_cols)


def _ada_bwd(c_all, dmod_cols, w, m, v):
    nb = c_all.shape[0]
    tn = 384

    def body(c_ref, d_ref, w_ref, m_ref, v_ref, g_ref, dl_ref, m2_ref, v2_ref):
        cv = c_ref[...]
        act = (cv * _sigmoid(cv)).astype(BF16)
        g = _dot_tn(act, d_ref[...].astype(BF16))
        g_ref[...] = g
        dl_ref[...], m2_ref[...], v2_ref[...] = _adam_math(w_ref[...], g, m_ref[...], v_ref[...])

    blk = pl.BlockSpec((D_MODEL, tn), lambda j: (0, j))
    return pl.pallas_call(
        body, name="ada_bwd", grid=(ADA_COLS // tn,),
        in_specs=[pl.BlockSpec((nb, D_MODEL), lambda j: (0, 0)), pl.BlockSpec((nb, tn), lambda j: (0, j)), blk, blk, blk],
        out_specs=[blk] * 4, out_shape=[jax.ShapeDtypeStruct((D_MODEL, ADA_COLS), F32)] * 4,
        compiler_params=_cparams(("arbitrary",)),
    )(c_all, dmod_cols, w, m, v)


SMALL_NAMES = ("norm_ffn1", "norm_mix", "conv_w", "conv_b", "ssd_norm_w", "q_norm_w", "kv_norm_w", "mla_norm_w",
               "norm_ffn2", "norm_final", "dt_bias", "a_log", "d_skip")
SMALL_SIZES = (1024, 1024, CONV_WIDTH * D_CONV, D_CONV, 1024, Q_LORA, KV_LORA, 1024, 1024, 1024, 16, 16, 16)
SMALL_ROWS = 16
MOD_ROWS = 2 * N_MOD
SEND_ROWS = 40


def _pack_small(parts):
    flat = jnp.concatenate([parts[n].reshape(-1) for n in SMALL_NAMES])
    return jnp.pad(flat, (0, SMALL_ROWS * D_MODEL - flat.shape[0]))


def _unpack_small(flat):
    out, off = {}, 0
    for n, size in zip(SMALL_NAMES, SMALL_SIZES):
        out[n] = flat[off:off + size]
        off += size
    return out


def _small_sum(got):
    def body(g_ref, o_ref):
        bsum = jnp.zeros((N_MOD, D_MODEL), F32)
        ssum = jnp.zeros((SMALL_ROWS, D_MODEL), F32)
        for d in range(N_DEV):
            bsum = bsum + g_ref[d, 0:N_MOD, :] + g_ref[d, N_MOD:MOD_ROWS, :]
            ssum = ssum + g_ref[d, MOD_ROWS:MOD_ROWS + SMALL_ROWS, :]
        o_ref[...] = jnp.concatenate([bsum, ssum, jnp.zeros((32 - N_MOD - SMALL_ROWS, D_MODEL), F32)], axis=0)

    return pl.pallas_call(body, name="small_sum", out_shape=jax.ShapeDtypeStruct((32, D_MODEL), F32))(got)


BIG_NAMES = ("ffn1_w_gate", "ffn1_w_up", "ffn1_w_down", "w_in", "w_uq", "w_ukv", "w_out", "ffn2_w_gate", "ffn2_w_up",
             "ffn2_w_down")
_TO_KERNEL = {"w_in": _win_to_kernel, "w_uq": _wuq_to_kernel, "w_ukv": _wukv_to_kernel}
_FROM_KERNEL = {"w_in": _win_from_kernel, "w_uq": _wuq_from_kernel, "w_ukv": _wukv_from_kernel}


def _columns_joined(w4):
    n, r, c = w4.shape
    return w4.transpose(1, 0, 2).reshape(r, n * c)


def _columns_split(g):
    r, cols = g.shape
    return g.reshape(r, N_CHIPS, cols // N_CHIPS).transpose(1, 0, 2)


def kernel(x, c, positions, w_ada, b_ada, norm_ffn1, ffn1_w_gate, ffn1_w_up, ffn1_w_down, norm_mix, w_in, conv_w, conv_b, dt_bias, a_log, d_skip, ssd_norm_w, q_norm_w, w_uq, kv_norm_w, w_ukv, mla_norm_w, w_out, norm_ffn2, ffn2_w_gate, ffn2_w_up, ffn2_w_down, norm_final, loss_target, m_w_ada, m_b_ada, m_norm_ffn1, m_ffn1_w_gate, m_ffn1_w_up, m_ffn1_w_down, m_norm_mix, m_w_in, m_conv_w, m_conv_b, m_dt_bias, m_a_log, m_d_skip, m_ssd_norm_w, m_q_norm_w, m_w_uq, m_kv_norm_w, m_w_ukv, m_mla_norm_w, m_w_out, m_norm_ffn2, m_ffn2_w_gate, m_ffn2_w_up, m_ffn2_w_down, m_norm_final, v_w_ada, v_b_ada, v_norm_ffn1, v_ffn1_w_gate, v_ffn1_w_up, v_ffn1_w_down, v_norm_mix, v_w_in, v_conv_w, v_conv_b, v_dt_bias, v_a_log, v_d_skip, v_ssd_norm_w, v_q_norm_w, v_w_uq, v_kv_norm_w, v_w_ukv, v_mla_norm_w, v_w_out, v_norm_ffn2, v_ffn2_w_gate, v_ffn2_w_up, v_ffn2_w_down, v_norm_final):
    a = dict(locals())
    held_transposed = ("ffn1_w_gate", "ffn1_w_up", "ffn2_w_gate", "ffn2_w_up", "w_in")
    for n in held_transposed:
        for p in ("", "m_", "v_"):
            a[p + n] = a[p + n].transpose(0, 2, 1)
    B, S, D = x.shape
    mx, my, mc = _place()
    chip = 2 * mx + my
    dev = 2 * chip + mc
    core = mc.astype(jnp.int32).reshape(1)
    chip_id = chip.astype(jnp.int32).reshape(1)

    cw_rows = jnp.pad(conv_w[0], ((0, 0), (0, D - conv_w.shape[2])))
    got, _ = _all_gather_small(jnp.concatenate([c, cw_rows, jnp.zeros((8 - B - CONV_WIDTH, D), F32)], axis=0), "gather_c")
    c_all = got[:, :B, :].reshape(N_DEV * B, D)
    conv_full = got[::2, B:B + CONV_WIDTH, :conv_w.shape[2]].transpose(1, 0, 2).reshape(CONV_WIDTH, D_CONV)

    b_cols = lax.dynamic_slice(b_ada, (0, chip * ADA_COLS), (1, ADA_COLS))
    mod_all, mod_done = _all_gather_small(_ada_fwd(c_all, w_ada[0], b_cols), "gather_mod")
    mod = lax.dynamic_slice(mod_all, (0, B * dev, 0), (N_DEV, B, ADA_COLS))[::2].transpose(1, 0, 2).reshape(B, N_MOD * D)

    first = ("ffn1_w_gate", "ffn1_w_up", "ffn1_w_down")
    later = tuple(n for n in BIG_NAMES if n not in first)
    got_first, gathered = _gather_weights([(a[n][0] + mod_done[0, 0]).astype(BF16) for n in first])
    w = dict(zip(first, got_first))
    in_flight = _gather_start([(a[n][0] + gathered[0, 0]).astype(BF16) for n in later])

    def later_weights(after):
        send_sems, recv_sems, shards, lands, _ = in_flight
        lands = _gather_wait(send_sems, recv_sems, shards, lands, after)
        wl = dict(zip(later, _gather_finish([a[n][0].astype(BF16) for n in later], lands)))
        for n, to_kernel in _TO_KERNEL.items():
            wl[n] = to_kernel(wl[n].reshape(-1, D) if n in held_transposed else _columns_joined(wl[n]))
        wl["w_out"] = wl["w_out"].reshape(D_SSD + D_MLA, D)
        return wl

    small = {n: a[n].reshape(1, -1) for n in SMALL_NAMES if n not in ("conv_w", "norm_final")}
    small["conv_w"], small["norm_final"] = conv_full, norm_final

    def scatter_group(names, gw, after):
        g4 = []
        for n in names:
            g = gw[n]
            if n in _FROM_KERNEL:
                g = _FROM_KERNEL[n](g) if n in held_transposed else _columns_split(_FROM_KERNEL[n](g))
            g4.append(g.reshape(N_CHIPS, a[n].shape[1], a[n].shape[2]))
        swapped = _swap_halves(g4, g4[0] if after is None else after, "swap_" + names[0])
        pair = [_pair_sum(g, got, core, "pair_sum_" + n) for n, g, got in zip(names, g4, swapped)]
        return (names,) + tuple(_scatter_start(pair, names[0]))

    grads, deltas, new_m, new_v = {}, {}, {}, {}

    def finish_group(group, after):
        names, send_sems, recv_sems, pair, lands, _ = group
        pair, lands = _scatter_wait(send_sems, recv_sems, pair, lands, after, names[0])
        mine = [_chip_sum(own, got, chip_id, "chip_sum_" + n) for n, own, got in zip(names, pair, lands)]
        for n, own, other in zip(names, mine, _join_halves(mine, "join_" + names[0])):
            grads[n], deltas[n], new_m[n], new_v[n] = _adam_halves(a[n], a["m_" + n], a["v_" + n], own, other, core, "adam_" + n)
        return deltas[names[-1]]

    groups = []

    def on_grads(names, gw):
        groups.append(scatter_group(names, gw, None))
        return groups[-1][5][0, 0]

    loss_blk, grad_x, gw, dmod, gs = _local_step(x, positions, mod + in_flight[4][0, 0], w, later_weights, small, loss_target,
                                                 on_grads)

    small_flat = _pack_small(gs).at[-1].set(loss_blk[0, 0])
    send = jnp.concatenate([dmod.reshape(MOD_ROWS, D), small_flat.reshape(SMALL_ROWS, D),
                            jnp.zeros((SEND_ROWS - MOD_ROWS - SMALL_ROWS, D), F32)], axis=0)
    got, _ = _all_gather_small(send, "gather_small")
    summed = _small_sum(got)
    sums = summed[N_MOD:N_MOD + SMALL_ROWS].reshape(-1)
    loss = sums[-1]
    gsmall = _unpack_small(sums)
    gsmall["conv_w"] = lax.dynamic_slice(gsmall["conv_w"].reshape(CONV_WIDTH, D_CONV), (0, chip * conv_w.shape[2]),
                                         (CONV_WIDTH, conv_w.shape[2]))
    gsmall["b_ada"] = summed[:N_MOD]
    names = ("b_ada",) + SMALL_NAMES
    rows = 208

    def pack(parts):
        flat = jnp.concatenate([parts[n].reshape(-1) for n in names])
        return jnp.pad(flat, (0, rows * LANES - flat.shape[0])).reshape(rows, LANES)

    packed = [pack({n: a[p + n] for n in names}) for p in ("", "m_", "v_")]
    g_p = pack(gsmall)
    outs = (g_p,) + tuple(_adam(packed[0], g_p, packed[1], packed[2], "adam_small"))
    for dst, flat in zip((grads, deltas, new_m, new_v), outs):
        flat, off = flat.reshape(-1), 0
        for n in names:
            dst[n] = flat[off:off + a[n].size].reshape(a[n].shape)
            off += a[n].size

    dmod_all = got[:, :MOD_ROWS, :].reshape(N_DEV * B, N_MOD * D)
    dmod_cols = lax.dynamic_slice(dmod_all, (0, chip * ADA_COLS), (N_DEV * B, ADA_COLS))
    ada = _ada_bwd(c_all, dmod_cols, w_ada[0], m_w_ada[0], v_w_ada[0])
    for dst, t in zip((grads, deltas, new_m, new_v), ada):
        dst["w_ada"] = t[None]

    last = scatter_group(first, gw, summed)
    after = last[5]
    for group in groups:
        after = finish_group(group, after)
    finish_group(last, after)
    for dst in (grads, deltas, new_m, new_v):
        for n in held_transposed:
            dst[n] = dst[n].transpose(0, 2, 1)

    order = ("w_ada", "b_ada", "norm_ffn1", "ffn1_w_gate", "ffn1_w_up", "ffn1_w_down", "norm_mix", "w_in", "conv_w", "conv_b",
             "dt_bias", "a_log", "d_skip", "ssd_norm_w", "q_norm_w", "w_uq", "kv_norm_w", "w_ukv", "mla_norm_w", "w_out",
             "norm_ffn2", "ffn2_w_gate", "ffn2_w_up", "ffn2_w_down", "norm_final")
    return (loss, grad_x, *[grads[n] for n in order], *[deltas[n] for n in order], *[new_m[n] for n in order],
            *[new_v[n] for n in order])
```

```python
import functools
import math

import jax
import jax.numpy as jnp
import numpy as np
from jax import lax
from jax.experimental import pallas as pl
from jax.experimental.pallas import tpu as pltpu

F32 = jnp.float32
BF16 = jnp.bfloat16
HIGHEST = lax.Precision.HIGHEST

D_MODEL = 1024
D_FF = 2816
D_SSD = 1024
D_MLA = 1024
SSD_HEADS = 16
SSD_HEAD_DIM = 64
SSD_GROUPS = 2
SSD_STATE = 128
CONV_WIDTH = 4
CHUNK = 128
MLA_HEADS = 8
QK_NOPE = 64
QK_ROPE = 32
QK_DIM = QK_NOPE + QK_ROPE
V_HEAD = 128
Q_LORA = 384
KV_LORA = 256
ROPE_THETA = 10000.0
N_MOD = 9
EPS = 1e-6
D_CONV = D_SSD + 2 * SSD_GROUPS * SSD_STATE
D_PROJ = 3328
HEAD_LANES = 128
ADAM_LR = 0.001
ADAM_B1 = 0.9
ADAM_B2 = 0.999
ADAM_EPS = 1e-08
ADAM_WD = 0.01
ADAM_STEP = 10

LANES = 128
VMEM_LIMIT = 56 * 1024 * 1024
TOKEN_TILE = 512
WIDE_TOKEN_TILE = 1024
ATTN_FWD_Q_TILE = 1024
ATTN_FWD_KV_TILE = 1024
ATTN_BWD_TILE = 1024
N_CHIPS = 4
N_DEV = 8

MESH = pl.DeviceIdType.MESH


def _dot(a, b, precision=None):
    return jnp.dot(a, b, preferred_element_type=F32, precision=precision)


def _dot_nt(a, b, precision=None):
    return lax.dot_general(a, b, (((1,), (1,)), ((), ())), preferred_element_type=F32, precision=precision)


def _dot_tn(a, b, precision=None):
    return lax.dot_general(a, b, (((0,), (0,)), ((), ())), preferred_element_type=F32, precision=precision)


def _cparams(semantics):
    return pltpu.CompilerParams(dimension_semantics=semantics, vmem_limit_bytes=VMEM_LIMIT)


def _resident(shape):
    zeros = (0,) * len(shape)
    return pl.BlockSpec(shape, lambda *_: zeros, pipeline_mode=pl.Buffered(1))


def _sigmoid(x):
    return jax.nn.sigmoid(x)


def _rms_stats(x):
    r = lax.rsqrt(jnp.mean(x * x, axis=-1, keepdims=True) + EPS)
    return x * r, r


def _rms_bwd(dn, xh, r, w):
    dxh = dn * w
    dx = r * (dxh - xh * jnp.mean(dxh * xh, axis=-1, keepdims=True))
    return dx, dn * xh


def _colsum(v):
    return jnp.sum(v, axis=0, keepdims=True)


def _ffn_fwd(x, nw, sh, sc, g, wg, wu, wd, seq, name, head=None):
    T, D = x.shape
    fs = wg.shape[1]
    tm = min(TOKEN_TILE, seq)
    tps = seq // tm

    def body(x_ref, nw_ref, sh_ref, sc_ref, g_ref, wg_ref, wu_ref, wd_ref, *rest):
        if head is None:
            xo_ref, a_ref, u_ref, f_ref = rest
        else:
            nf_ref, t_ref, xo_ref, a_ref, u_ref, f_ref, loss_ref, dnf_ref = rest

            @pl.when(pl.program_id(0) == 0)
            def _():
                loss_ref[...] = jnp.zeros_like(loss_ref)
                dnf_ref[...] = jnp.zeros_like(dnf_ref)

        xv = x_ref[...]
        xh, _ = _rms_stats(xv)
        h = (xh * nw_ref[...]) * (1.0 + sc_ref[0]) + sh_ref[0]
        hb = h.astype(BF16)
        f = jnp.zeros((tm, D), F32)
        for j in range(N_CHIPS):
            a = _dot_nt(hb, wg_ref[j])
            u = _dot_nt(hb, wu_ref[j])
            a_ref[j] = a.astype(BF16)
            u_ref[j] = u.astype(BF16)
            f = f + _dot((a * _sigmoid(a) * u).astype(BF16), wd_ref[j])
        f_ref[...] = f.astype(BF16)
        xo = xv + 0.5 * g_ref[0] * f
        if head is None:
            xo_ref[...] = xo
        else:
            xh, r = _rms_stats(xo)
            nfv = nf_ref[...]
            err = xh * nfv - t_ref[...]
            loss_ref[...] += (0.5 / D) * jnp.sum(err * err)
            dxo, dw_rows = _rms_bwd(err * (1.0 / D), xh, r, nfv)
            xo_ref[...] = dxo
            dnf_ref[...] += _colsum(dw_rows)

    rows = lambda n: pl.BlockSpec((tm, n), lambda i: (i, 0))
    act = pl.BlockSpec((N_CHIPS, tm, fs), lambda i: (0, i, 0))
    perb = pl.BlockSpec((1, 1, D), lambda i: (i // tps, 0, 0))
    sd = jax.ShapeDtypeStruct
    in_specs = [rows(D), _resident((1, D)), perb, perb, perb, _resident((N_CHIPS, fs, D)), _resident((N_CHIPS, fs, D)),
                _resident((N_CHIPS, fs, D))]
    out_specs = [rows(D), act, act, rows(D)]
    out_shape = [sd((T, D), F32), sd((N_CHIPS, T, fs), BF16), sd((N_CHIPS, T, fs), BF16), sd((T, D), BF16)]
    if head is not None:
        in_specs += [_resident((1, D)), rows(D)]
        out_specs += [pl.BlockSpec((8, LANES), lambda i: (0, 0)), pl.BlockSpec((1, D), lambda i: (0, 0))]
        out_shape += [sd((8, LANES), F32), sd((1, D), F32)]
    return pl.pallas_call(
        body, grid=(T // tm,), name=name, in_specs=in_specs, out_specs=out_specs, out_shape=out_shape,
        compiler_params=_cparams(("arbitrary",)),
    )(x, nw, sh, sc, g, wg, wu, wd, *(head or ()))


def _ffn_bwd(dxo, x, nw, sh, sc, g, a, u, f, wg, wu, wd, seq, name):
    T, D = x.shape
    fs = wg.shape[1]
    B = T // seq
    tm = min(TOKEN_TILE // 2, seq)
    tps = seq // tm

    def body(dxo_ref, x_ref, nw_ref, sh_ref, sc_ref, g_ref, a_ref, u_ref, f_ref, wg_ref, wu_ref, wd_ref,
             dx_ref, h_ref, s_ref, df_ref, da_ref, du_ref, dsh_ref, dsc_ref, dg_ref, dnw_ref):
        i = pl.program_id(0)

        @pl.when(i % tps == 0)
        def _():
            dsh_ref[...] = jnp.zeros_like(dsh_ref)
            dsc_ref[...] = jnp.zeros_like(dsc_ref)
            dg_ref[...] = jnp.zeros_like(dg_ref)

        @pl.when(i == 0)
        def _():
            dnw_ref[...] = jnp.zeros_like(dnw_ref)

        dxo_v = dxo_ref[...]
        dfb = (0.5 * g_ref[0] * dxo_v).astype(BF16)
        dg_ref[0] += _colsum(0.5 * dxo_v * f_ref[...].astype(F32))
        dh = jnp.zeros((tm, D), F32)
        for j in range(N_CHIPS):
            ds = _dot_nt(dfb, wd_ref[j])
            av = a_ref[j].astype(F32)
            uv = u_ref[j].astype(F32)
            sig = _sigmoid(av)
            sil = av * sig
            dab = (ds * uv * (sig * (1.0 + av * (1.0 - sig)))).astype(BF16)
            dub = (ds * sil).astype(BF16)
            dh = dh + _dot(dab, wg_ref[j]) + _dot(dub, wu_ref[j])
            s_ref[j] = (sil * uv).astype(BF16)
            da_ref[j] = dab
            du_ref[j] = dub
        xv = x_ref[...]
        xh, r = _rms_stats(xv)
        nwv = nw_ref[...]
        n = xh * nwv
        scale1 = 1.0 + sc_ref[0]
        dsc_ref[0] += _colsum(dh * n)
        dsh_ref[0] += _colsum(dh)
        dx, dw_rows = _rms_bwd(dh * scale1, xh, r, nwv)
        dnw_ref[...] += _colsum(dw_rows)
        dx_ref[...] = dxo_v + dx
        h_ref[...] = (n * scale1 + sh_ref[0]).astype(BF16)
        df_ref[...] = dfb

    rows = lambda n: pl.BlockSpec((tm, n), lambda i: (i, 0))
    act = pl.BlockSpec((N_CHIPS, tm, fs), lambda i: (0, i, 0))
    perb = pl.BlockSpec((1, 1, D), lambda i: (i // tps, 0, 0))
    sd = jax.ShapeDtypeStruct
    return pl.pallas_call(
        body, grid=(T // tm,), name=name,
        in_specs=[rows(D), rows(D), _resident((1, D)), perb, perb, perb, act, act, rows(D),
                  _resident((N_CHIPS, fs, D)), _resident((N_CHIPS, fs, D)), _resident((N_CHIPS, fs, D))],
        out_specs=[rows(D), rows(D), act, rows(D), act, act, perb, perb, perb, pl.BlockSpec((1, D), lambda i: (0, 0))],
        out_shape=[sd((T, D), F32), sd((T, D), BF16), sd((N_CHIPS, T, fs), BF16), sd((T, D), BF16),
                   sd((N_CHIPS, T, fs), BF16), sd((N_CHIPS, T, fs), BF16), sd((B, 1, D), F32), sd((B, 1, D), F32),
                   sd((B, 1, D), F32), sd((1, D), F32)],
        compiler_params=_cparams(("arbitrary",)),
    )(dxo, x, nw, sh, sc, g, a, u, f, wg, wu, wd)


def _ffn_wgrad(h, s, df, da, du, name):
    T, D = h.shape
    fs = s.shape[2]
    tt = min(TOKEN_TILE, T)
    nt = T // tt

    def body(h_ref, s_ref, df_ref, da_ref, du_ref, dgate_ref, dup_ref, ddown_ref, gate_acc, up_acc, down_acc):
        @pl.when(pl.program_id(1) == 0)
        def _():
            gate_acc[...] = jnp.zeros_like(gate_acc)
            up_acc[...] = jnp.zeros_like(up_acc)
            down_acc[...] = jnp.zeros_like(down_acc)

        hv = h_ref[...]
        gate_acc[...] += _dot_tn(da_ref[0], hv)
        up_acc[...] += _dot_tn(du_ref[0], hv)
        down_acc[...] += _dot_tn(s_ref[0], df_ref[...])

        @pl.when(pl.program_id(1) == nt - 1)
        def _():
            dgate_ref[0] = gate_acc[...].astype(BF16)
            dup_ref[0] = up_acc[...].astype(BF16)
            ddown_ref[0] = down_acc[...].astype(BF16)

    rows = pl.BlockSpec((tt, D), lambda j, t: (t, 0))
    act = pl.BlockSpec((1, tt, fs), lambda j, t: (j, t, 0))
    shard = pl.BlockSpec((1, fs, D), lambda j, t: (j, 0, 0))
    return pl.pallas_call(
        body, grid=(N_CHIPS, nt), name=name,
        in_specs=[rows, act, rows, act, act],
        out_specs=[shard] * 3, out_shape=[jax.ShapeDtypeStruct((N_CHIPS, fs, D), BF16)] * 3,
        scratch_shapes=[pltpu.VMEM((fs, D), F32)] * 3,
        compiler_params=_cparams(("arbitrary", "arbitrary")),
    )(h, s, df, da, du)


def _mm_tn(xa, ya, tn, name):
    T, K = xa.shape
    N = ya.shape[1]
    tt = min(WIDE_TOKEN_TILE, T)

    def body(x_ref, y_ref, o_ref):
        @pl.when(pl.program_id(1) == 0)
        def _():
            o_ref[...] = jnp.zeros_like(o_ref)

        o_ref[...] += _dot_tn(x_ref[...], y_ref[...])

    return pl.pallas_call(
        body, grid=(N // tn, T // tt), name=name,
        in_specs=[pl.BlockSpec((tt, K), lambda j, t: (t, 0)), pl.BlockSpec((tt, tn), lambda j, t: (t, j))],
        out_specs=pl.BlockSpec((K, tn), lambda j, t: (0, j)),
        out_shape=jax.ShapeDtypeStruct((K, N), F32),
        compiler_params=_cparams(("arbitrary", "arbitrary")),
    )(xa, ya)


_PROJ_SPLITS = (0, 1024, 2560, 2944, 3200, 3328)


def _inproj_fwd(x, nw, sh, sc, win, seq):
    T, D = x.shape
    tm = min(WIDE_TOKEN_TILE, seq)
    tps = seq // tm
    widths = [b - a for a, b in zip(_PROJ_SPLITS[:-1], _PROJ_SPLITS[1:])]
    dtypes = [BF16, BF16, F32, F32, F32]

    def body(x_ref, nw_ref, sh_ref, sc_ref, w_ref, *outs):
        xh, _ = _rms_stats(x_ref[...])
        h = (xh * nw_ref[...]) * (1.0 + sc_ref[0]) + sh_ref[0]
        proj = _dot_nt(h.astype(BF16), w_ref[...])
        for o, lo, hi in zip(outs, _PROJ_SPLITS[:-1], _PROJ_SPLITS[1:]):
            o[...] = proj[:, lo:hi].astype(o.dtype)

    rows = lambda n: pl.BlockSpec((tm, n), lambda i: (i, 0))
    perb = pl.BlockSpec((1, 1, D), lambda i: (i // tps, 0, 0))
    return pl.pallas_call(
        body, grid=(T // tm,), name="inproj_fwd",
        in_specs=[rows(D), _resident((1, D)), perb, perb, _resident((D_PROJ, D))],
        out_specs=[rows(w) for w in widths],
        out_shape=[jax.ShapeDtypeStruct((T, w), dt) for w, dt in zip(widths, dtypes)],
        compiler_params=_cparams(("arbitrary",)),
    )(x, nw, sh, sc, win)


def _inproj_bwd(dx2, x, nw, sh, sc, win, dz, dxbc, dcq, dckv, ddtk_a, ddtk_b, seq):
    T, D = x.shape
    B = T // seq
    tm = min(TOKEN_TILE, seq)
    tps = seq // tm

    def body(dx2_ref, x_ref, nw_ref, sh_ref, sc_ref, w_ref, dz_ref, dxbc_ref, dcq_ref, dckv_ref, da_ref, db_ref,
             dx_ref, h_ref, dp_ref, dsh_ref, dsc_ref, dnw_ref):
        i = pl.program_id(0)

        @pl.when(i % tps == 0)
        def _():
            dsh_ref[...] = jnp.zeros_like(dsh_ref)
            dsc_ref[...] = jnp.zeros_like(dsc_ref)

        @pl.when(i == 0)
        def _():
            dnw_ref[...] = jnp.zeros_like(dnw_ref)

        dproj = jnp.concatenate(
            [dz_ref[...], dxbc_ref[...], dcq_ref[...].astype(BF16), dckv_ref[...].astype(BF16),
             (da_ref[...] + db_ref[...]).astype(BF16)], axis=1)
        dp_ref[...] = dproj
        dh = _dot(dproj, w_ref[...])
        xh, r = _rms_stats(x_ref[...])
        nwv = nw_ref[...]
        n = xh * nwv
        scale1 = 1.0 + sc_ref[0]
        dsc_ref[0] += _colsum(dh * n)
        dsh_ref[0] += _colsum(dh)
        dx, dw_rows = _rms_bwd(dh * scale1, xh, r, nwv)
        dnw_ref[...] += _colsum(dw_rows)
        dx_ref[...] = dx2_ref[...] + dx
        h_ref[...] = (n * scale1 + sh_ref[0]).astype(BF16)

    rows = lambda n: pl.BlockSpec((tm, n), lambda i: (i, 0))
    perb = pl.BlockSpec((1, 1, D), lambda i: (i // tps, 0, 0))
    sd = jax.ShapeDtypeStruct
    return pl.pallas_call(
        body, grid=(T // tm,), name="inproj_bwd",
        in_specs=[rows(D), rows(D), _resident((1, D)), perb, perb, _resident((D_PROJ, D)),
                  rows(1024), rows(D_CONV), rows(Q_LORA), rows(KV_LORA), rows(LANES), rows(LANES)],
        out_specs=[rows(D), rows(D), rows(D_PROJ), perb, perb, pl.BlockSpec((1, D), lambda i: (0, 0))],
        out_shape=[sd((T, D), F32), sd((T, D), BF16), sd((T, D_PROJ), BF16), sd((B, 1, D), F32), sd((B, 1, D), F32),
                   sd((1, D), F32)],
        compiler_params=_cparams(("arbitrary",)),
    )(dx2, x, nw, sh, sc, win, dz, dxbc, dcq, dckv, ddtk_a, ddtk_b)


def _shift_down(v, k, row):
    return jnp.where(row < k, 0.0, pltpu.roll(v, k, 0))


def _shift_up(v, k, row, n):
    return jnp.where(row >= n - k, 0.0, pltpu.roll(v, n - k, 0))


def _conv_pre(xv, w_ref, b_ref, row):
    pre = b_ref[...] + w_ref[CONV_WIDTH - 1:CONV_WIDTH, :] * xv
    for k in range(1, CONV_WIDTH):
        pre = pre + w_ref[CONV_WIDTH - 1 - k:CONV_WIDTH - k, :] * _shift_down(xv, k, row)
    return pre


def _conv_fwd(xraw, cw, cb):
    B, S, C = xraw.shape

    def body(x_ref, w_ref, b_ref, o_ref):
        xv = x_ref[0].astype(F32)
        row = lax.broadcasted_iota(jnp.int32, xv.shape, 0)
        pre = _conv_pre(xv, w_ref, b_ref, row)
        o_ref[0] = (pre * _sigmoid(pre)).astype(BF16)

    blk = pl.BlockSpec((1, S, LANES), lambda b, j: (b, 0, j))
    return pl.pallas_call(
        body, grid=(B, C // LANES), name="conv_fwd",
        in_specs=[blk, pl.BlockSpec((CONV_WIDTH, LANES), lambda b, j: (0, j)), pl.BlockSpec((1, LANES), lambda b, j: (0, j))],
        out_specs=blk, out_shape=jax.ShapeDtypeStruct((B, S, C), BF16),
        compiler_params=_cparams(("arbitrary", "arbitrary")),
    )(xraw, cw, cb)


def _conv_bwd(dout, xraw, cw, cb):
    B, S, C = xraw.shape

    def body(d_ref, x_ref, w_ref, b_ref, dx_ref, dw_ref, db_ref):
        @pl.when(pl.program_id(1) == 0)
        def _():
            dw_ref[...] = jnp.zeros_like(dw_ref)
            db_ref[...] = jnp.zeros_like(db_ref)

        xv = x_ref[0].astype(F32)
        row = lax.broadcasted_iota(jnp.int32, xv.shape, 0)
        pre = _conv_pre(xv, w_ref, b_ref, row)
        sig = _sigmoid(pre)
        dpre = d_ref[0].astype(F32) * (sig * (1.0 + pre * (1.0 - sig)))
        dx = w_ref[CONV_WIDTH - 1:CONV_WIDTH, :] * dpre
        for k in range(1, CONV_WIDTH):
            dx = dx + w_ref[CONV_WIDTH - 1 - k:CONV_WIDTH - k, :] * _shift_up(dpre, k, row, S)
        dx_ref[0] = dx.astype(BF16)
        db_ref[...] += _colsum(dpre)
        dws = [_colsum(dpre * (xv if k == 0 else _shift_down(xv, k, row))) for k in range(CONV_WIDTH - 1, -1, -1)]
        dw_ref[...] += jnp.concatenate(dws, axis=0)

    blk = pl.BlockSpec((1, S, LANES), lambda j, b: (b, 0, j))
    wspec = pl.BlockSpec((CONV_WIDTH, LANES), lambda j, b: (0, j))
    bspec = pl.BlockSpec((1, LANES), lambda j, b: (0, j))
    return pl.pallas_call(
        body, grid=(C // LANES, B), name="conv_bwd",
        in_specs=[blk, blk, wspec, bspec], out_specs=[blk, wspec, bspec],
        out_shape=[jax.ShapeDtypeStruct((B, S, C), BF16), jax.ShapeDtypeStruct((CONV_WIDTH, C), F32),
                   jax.ShapeDtypeStruct((1, C), F32)],
        compiler_params=_cparams(("arbitrary", "arbitrary")),
    )(dout, xraw, cw, cb)


def _softplus(x):
    return jnp.maximum(x, 0.0) + jnp.log(1.0 + jnp.exp(-jnp.abs(x)))


def _ssd_common(xbc_ref, dtk_ref, dtb_ref, alog_ref, e_ref):
    L = CHUNK
    xbc = xbc_ref[0]
    xs = xbc[:, :D_SSD].astype(F32)
    bm = xbc[:, D_SSD:D_SSD + 256]
    cm = xbc[:, D_SSD + 256:D_SSD + 512]
    head = lax.broadcasted_iota(jnp.int32, (1, LANES), 1) < SSD_HEADS
    a128 = jnp.where(head, -jnp.exp(alog_ref[...]), 0.0)
    pre = dtk_ref[0] + dtb_ref[...]
    dt = _softplus(pre)
    dA = dt * a128
    row = lax.broadcasted_iota(jnp.int32, (L, L), 0)
    col = lax.broadcasted_iota(jnp.int32, (L, L), 1)
    causal = col <= row
    tri = causal.astype(F32)
    triT = (row <= col).astype(F32)
    tri = causal.astype(BF16)
    triT = (row <= col).astype(BF16)
    dA3 = _split3(dA)
    acum = _sum3(lambda part: _dot(tri, part), dA3)
    acumT = _sum3(lambda part: _dot_tn(part, triT), dA3)
    E = e_ref[...]
    acum_f = _spread(acum, E)
    dt_f = _spread(dt, E)
    e_f = jnp.exp(acum_f)
    w_f = jnp.exp(acum_f[L - 1:L, :] - acum_f)
    xt = xs * dt_f
    return dict(xs=xs, bm=bm, cm=cm, a128=a128, pre=pre, dt=dt, causal=causal, tri=tri, triT=triT, acum=acum,
                acumT=acumT, E=E, dt_f=dt_f, e_f=e_f, w_f=w_f, xt=xt, head=head)


def _split3(x):
    p1 = x.astype(BF16)
    r1 = x - p1.astype(F32)
    p2 = r1.astype(BF16)
    return p1, p2, (r1 - p2.astype(F32)).astype(BF16)


def _sum3(mm, parts):
    return (mm(parts[0]) + mm(parts[1])) + mm(parts[2])


def _spread(v, e):
    return _sum3(lambda part: _dot(part, e), _split3(v))


def _gather_heads(v, e):
    return _sum3(lambda part: _dot_nt(part, e), _split3(v))


def _head_mask(k):
    lane = lax.broadcasted_iota(jnp.int32, (CHUNK, LANES), 1)
    return (lane >= SSD_HEAD_DIM) if k == 1 else (lane < SSD_HEAD_DIM)


def _pair_decay(alast, h0):
    row = lax.broadcasted_iota(jnp.int32, (2 * SSD_HEAD_DIM, SSD_STATE), 0)
    return jnp.exp(jnp.where(row < SSD_HEAD_DIM, alast[:, h0:h0 + 1], alast[:, h0 + 1:h0 + 2]))


def _decay_matrix(q, h):
    seg = q["acum"][:, h:h + 1] - q["acumT"][h:h + 1, :]
    return jnp.exp(jnp.where(q["causal"], seg, -1e30))


def _gated_norm(y, zz, nw):
    sig = _sigmoid(zz)
    sil = zz * sig
    yg = y * sil
    half = D_SSD // SSD_GROUPS
    parts = []
    for g in range(SSD_GROUPS):
        xh, r = _rms_stats(yg[:, g * half:(g + 1) * half])
        parts.append((xh, r))
    return sig, sil, parts


def _ssd_fwd(xbc, dtk, z, dtb, alog, dsk, nw, expand):
    B, S, _ = xbc.shape
    L = CHUNK
    nc = S // L

    def body(xbc_ref, dtk_ref, z_ref, dtb_ref, alog_ref, dsk_ref, nw_ref, e_ref, y_ref, ys_ref, prev_ref, st_ref):
        @pl.when(pl.program_id(1) == 0)
        def _():
            st_ref[...] = jnp.zeros_like(st_ref)

        q = _ssd_common(xbc_ref, dtk_ref, dtb_ref, alog_ref, e_ref)
        xtb = q["xt"].astype(BF16)
        xwb = (q["xt"] * q["w_f"]).astype(BF16)
        alast = q["acum"][L - 1:L, :]
        ys = []
        for g in range(SSD_GROUPS):
            bg = q["bm"][:, g * 128:(g + 1) * 128]
            cg = q["cm"][:, g * 128:(g + 1) * 128]
            G = _dot_nt(cg, bg)
            for pr in range(SSD_HEADS // SSD_GROUPS // 2):
                h0 = g * 8 + 2 * pr
                lo = h0 * SSD_HEAD_DIM
                xt_p = xtb[:, lo:lo + 128]
                ydiag = jnp.zeros((L, LANES), F32)
                for k in range(2):
                    M = (G * _decay_matrix(q, h0 + k)).astype(BF16)
                    ydiag = ydiag + _dot(M, jnp.where(_head_mask(k), xt_p, jnp.zeros_like(xt_p)))
                hp = st_ref[lo:lo + 128, :]
                prev_ref[0, 0, lo:lo + 128, :] = hp.astype(BF16)
                zoff = _dot_nt(cg, hp.astype(BF16))
                ys.append(ydiag + zoff * q["e_f"][:, lo:lo + 128])
                st_ref[lo:lo + 128, :] = _pair_decay(alast, h0) * hp + _dot_tn(xwb[:, lo:lo + 128], bg)
        y = jnp.concatenate(ys, axis=1) + dsk_ref[...] * q["xs"]
        y_ref[0] = y.astype(BF16)
        _, _, parts = _gated_norm(y, z_ref[0].astype(F32), nw_ref[...])
        half = D_SSD // SSD_GROUPS
        ys_ref[0] = jnp.concatenate(
            [xh * nw_ref[:, g * half:(g + 1) * half] for g, (xh, _) in enumerate(parts)], axis=1).astype(BF16)

    chunk = lambda n: pl.BlockSpec((1, L, n), lambda b, c: (b, c, 0))
    vec = pl.BlockSpec((1, LANES), lambda b, c: (0, 0))
    return pl.pallas_call(
        body, grid=(B, nc), name="ssd_fwd",
        in_specs=[chunk(D_CONV), chunk(LANES), chunk(D_SSD), vec, vec, pl.BlockSpec((1, D_SSD), lambda b, c: (0, 0)),
                  pl.BlockSpec((1, D_SSD), lambda b, c: (0, 0)), pl.BlockSpec((LANES, D_SSD), lambda b, c: (0, 0))],
        out_specs=[chunk(D_SSD), chunk(D_SSD), pl.BlockSpec((1, 1, D_SSD, SSD_STATE), lambda b, c: (b, c, 0, 0))],
        out_shape=[jax.ShapeDtypeStruct((B, S, D_SSD), BF16), jax.ShapeDtypeStruct((B, S, D_SSD), BF16),
                   jax.ShapeDtypeStruct((B, nc, D_SSD, SSD_STATE), BF16)],
        scratch_shapes=[pltpu.VMEM((D_SSD, SSD_STATE), F32)],
        compiler_params=_cparams(("arbitrary", "arbitrary")),
    )(xbc, dtk, z, dtb, alog, dsk, nw, expand)


def _ssd_bwd(xbc, dtk, z, y, prev, dys, dtb, alog, dsk, nw, expand):
    B, S, _ = xbc.shape
    L = CHUNK
    nc = S // L
    half = D_SSD // SSD_GROUPS

    def body(xbc_ref, dtk_ref, z_ref, y_ref, prev_ref, dys_ref, dtb_ref, alog_ref, dsk_ref, nw_ref, e_ref,
             dxbc_ref, ddtk_ref, dz_ref, dnw_ref, dvec_ref, dh_ref, dskc_ref):
        @pl.when((pl.program_id(0) == 0) & (pl.program_id(1) == 0))
        def _():
            dnw_ref[...] = jnp.zeros_like(dnw_ref)
            dvec_ref[...] = jnp.zeros_like(dvec_ref)
            dskc_ref[...] = jnp.zeros_like(dskc_ref)

        @pl.when(pl.program_id(1) == 0)
        def _():
            dh_ref[...] = jnp.zeros_like(dh_ref)

        q = _ssd_common(xbc_ref, dtk_ref, dtb_ref, alog_ref, e_ref)
        E = q["E"]
        xs = q["xs"]
        yv = y_ref[0].astype(F32)
        zz = z_ref[0].astype(F32)
        sig, sil, parts = _gated_norm(yv, zz, nw_ref[...])
        dn = dys_ref[0].astype(F32)
        dyg, dnw_rows = [], []
        for g, (xh, r) in enumerate(parts):
            dpart, dw_rows = _rms_bwd(dn[:, g * half:(g + 1) * half], xh, r, nw_ref[:, g * half:(g + 1) * half])
            dyg.append(dpart)
            dnw_rows.append(dw_rows)
        dyg = jnp.concatenate(dyg, axis=1)
        dnw_ref[...] += _colsum(jnp.concatenate(dnw_rows, axis=1))
        dY = dyg * sil
        dz_ref[0] = (dyg * yv * (sig * (1.0 + zz * (1.0 - sig)))).astype(BF16)
        dsk_f = dsk_ref[...]
        dskc_ref[...] += _colsum(dY * xs)
        dYb = dY.astype(BF16)
        xtb = q["xt"].astype(BF16)
        xwb = (q["xt"] * q["w_f"]).astype(BF16)
        acum = q["acum"]
        alast = acum[L - 1:L, :]
        lane_id = lax.broadcasted_iota(jnp.int32, (L, LANES), 1)
        sub_id = lax.broadcasted_iota(jnp.int32, (LANES, L), 0)
        lane_row = lax.broadcasted_iota(jnp.int32, (1, LANES), 1)
        da_rows = jnp.zeros((L, LANES), F32)
        daT = jnp.zeros((LANES, L), F32)
        dxt, prod_off, prod_st, dbs, dcs = [], [], [], [], []
        hsum_row = jnp.zeros((1, LANES), F32)
        for g in range(SSD_GROUPS):
            bg = q["bm"][:, g * 128:(g + 1) * 128]
            cg = q["cm"][:, g * 128:(g + 1) * 128]
            G = _dot_nt(cg, bg)
            dG = jnp.zeros((L, L), F32)
            dcg = jnp.zeros((L, SSD_STATE), F32)
            dbg = jnp.zeros((L, SSD_STATE), F32)
            for pr in range(SSD_HEADS // SSD_GROUPS // 2):
                h0 = g * 8 + 2 * pr
                lo = h0 * SSD_HEAD_DIM
                cols = slice(lo, lo + 128)
                dY_p = dYb[:, cols]
                xt_p = xtb[:, cols]
                dxt_p = jnp.zeros((L, LANES), F32)
                for k in range(2):
                    h = h0 + k
                    Lm = _decay_matrix(q, h)
                    Mf = G * Lm
                    dYk = jnp.where(_head_mask(k), dY_p, jnp.zeros_like(dY_p))
                    dM = _dot_nt(dYk, xt_p)
                    dxt_p = dxt_p + _dot_tn(Mf.astype(BF16), dYk)
                    dG = dG + dM * Lm
                    Q = dM * Mf
                    da_rows = da_rows + jnp.where(lane_id == h, jnp.sum(Q, axis=1, keepdims=True), 0.0)
                    daT = daT + jnp.where(sub_id == h, jnp.sum(Q, axis=0, keepdims=True), 0.0)
                hpb = prev_ref[0, 0, lo:lo + 128, :]
                hp = hpb.astype(F32)
                zoff = _dot_nt(cg, hpb)
                e_p = q["e_f"][:, cols]
                dY_pf = dY[:, cols]
                dZb = (dY_pf * e_p).astype(BF16)
                dcg = dcg + _dot(dZb, hpb)
                dhp_off = _dot_tn(dZb, cg)
                prod_off.append(dY_pf * zoff * e_p)
                dS = dh_ref[lo:lo + 128, :]
                dSb = dS.astype(BF16)
                U = _dot_nt(bg, dSb)
                dxt_p = dxt_p + U * q["w_f"][:, cols]
                dbg = dbg + _dot(xwb[:, cols], dSb)
                prod_st.append(q["xt"][:, cols] * U)
                dh_ref[lo:lo + 128, :] = _pair_decay(alast, h0) * dS + dhp_off
                dsh = dS * hp
                for k in range(2):
                    total = jnp.sum(dsh[k * SSD_HEAD_DIM:(k + 1) * SSD_HEAD_DIM, :], axis=(0, 1), keepdims=True)
                    hsum_row = hsum_row + jnp.where(lane_row == h0 + k, total, 0.0)
                dxt.append(dxt_p)
            dGb = dG.astype(BF16)
            dcs.append(dcg + _dot(dGb, bg))
            dbs.append(dbg + _dot_tn(dGb, cg))
        dxt = jnp.concatenate(dxt, axis=1)
        da_rows = da_rows + _gather_heads(jnp.concatenate(prod_off, axis=1), E)
        dww = _gather_heads(jnp.concatenate(prod_st, axis=1), E) * jnp.exp(alast - acum)
        da_rows = da_rows - dww
        dlast = _colsum(dww) + jnp.exp(alast) * hsum_row
        triT = q["triT"]
        ddA = (_sum3(lambda part: _dot(triT, part), _split3(da_rows))
               - _sum3(lambda part: _dot_nt(triT, part), _split3(daT)) + dlast)
        ddA = jnp.where(q["head"], ddA, 0.0)
        ddt = ddA * q["a128"] + _gather_heads(dxt * xs, E)
        ddt_raw = jnp.where(q["head"], ddt * _sigmoid(q["pre"]), 0.0)
        ddtk_ref[0] = ddt_raw
        dxs = dxt * q["dt_f"] + dsk_f * dY
        dxbc_ref[0] = jnp.concatenate([dxs] + dbs + dcs, axis=1).astype(BF16)
        dvec_ref[0:1, :] += _colsum(ddt_raw)
        dvec_ref[1:2, :] += _colsum(ddA * q["dt"]) * q["a128"]

        @pl.when((pl.program_id(0) == B - 1) & (pl.program_id(1) == nc - 1))
        def _():
            dvec_ref[2:3, :] = _gather_heads(jnp.broadcast_to(dskc_ref[...], (8, D_SSD)), E)[0:1, :]

    rev = lambda n: pl.BlockSpec((1, L, n), lambda b, c: (b, nc - 1 - c, 0))
    vec = pl.BlockSpec((1, LANES), lambda b, c: (0, 0))
    sd = jax.ShapeDtypeStruct
    return pl.pallas_call(
        body, grid=(B, nc), name="ssd_bwd",
        in_specs=[rev(D_CONV), rev(LANES), rev(D_SSD), rev(D_SSD),
                  pl.BlockSpec((1, 1, D_SSD, SSD_STATE), lambda b, c: (b, nc - 1 - c, 0, 0)), rev(D_SSD), vec, vec,
                  pl.BlockSpec((1, D_SSD), lambda b, c: (0, 0)),
                  pl.BlockSpec((1, D_SSD), lambda b, c: (0, 0)), pl.BlockSpec((LANES, D_SSD), lambda b, c: (0, 0))],
        out_specs=[rev(D_CONV), rev(LANES), rev(D_SSD), pl.BlockSpec((1, D_SSD), lambda b, c: (0, 0)),
                   pl.BlockSpec((8, LANES), lambda b, c: (0, 0))],
        out_shape=[sd((B, S, D_CONV), BF16), sd((B, S, LANES), F32), sd((B, S, D_SSD), BF16), sd((1, D_SSD), F32),
                   sd((8, LANES), F32)],
        scratch_shapes=[pltpu.VMEM((D_SSD, SSD_STATE), F32), pltpu.VMEM((1, D_SSD), F32)],
        compiler_params=_cparams(("arbitrary", "arbitrary")),
    )(xbc, dtk, z, y, prev, dys, dtb, alog, dsk, nw, expand)


def _rope_tables(pos_ref, invf_ref, place_ref):
    ang = invf_ref[...] * pos_ref[0].astype(F32)
    place = place_ref[...]
    cosf = 1.0 + _sum3(lambda part: _dot_tn(part, place), _split3(jnp.cos(ang) - 1.0))
    sinf = _sum3(lambda part: _dot_tn(part, place), _split3(jnp.sin(ang)))
    return cosf, sinf


def _rot(u):
    lane = lax.broadcasted_iota(jnp.int32, u.shape, 1)
    first = (lane >= QK_NOPE) & (lane < QK_NOPE + QK_ROPE // 2)
    second = (lane >= QK_NOPE + QK_ROPE // 2) & (lane < QK_DIM)
    return jnp.where(first, -pltpu.roll(u, LANES - QK_ROPE // 2, 1), jnp.where(second, pltpu.roll(u, QK_ROPE // 2, 1), 0.0))


def _rope_lanes(shape):
    lane = lax.broadcasted_iota(jnp.int32, shape, 1)
    return (lane >= QK_NOPE) & (lane < QK_DIM)


def _mla_prep(cq, ckv, dtk, pos, qw, kvw, wuq, wukv, invf, place):
    T = cq.shape[0]
    tm = min(WIDE_TOKEN_TILE, T)
    scale = 1.0 / math.sqrt(QK_DIM)
    HW = MLA_HEADS * HEAD_LANES

    def body(cq_ref, ckv_ref, dtk_ref, pos_ref, qw_ref, kvw_ref, wuq_ref, wukv_ref, invf_ref, place_ref, q_ref, k_ref, v_ref,
             cos_ref, sin_ref):
        xh, _ = _rms_stats(cq_ref[...])
        qv = _dot((xh * qw_ref[...]).astype(BF16), wuq_ref[...])
        xh, _ = _rms_stats(ckv_ref[...])
        kv = _dot((xh * kvw_ref[...]).astype(BF16), wukv_ref[...])
        cosf, sinf = _rope_tables(pos_ref, invf_ref, place_ref)
        cos_ref[...] = cosf
        sin_ref[...] = sinf
        rope = lambda u: u * cosf + _rot(u) * sinf
        dtkv = dtk_ref[...]
        kr = rope(jnp.where(_rope_lanes(dtkv.shape), dtkv, 0.0))
        for h in range(MLA_HEADS):
            cols = slice(h * HEAD_LANES, (h + 1) * HEAD_LANES)
            q_ref[:, cols] = (rope(qv[:, cols]) * scale).astype(BF16)
            k_ref[:, cols] = (kv[:, cols] + kr).astype(BF16)
        v_ref[...] = kv[:, HW:].astype(BF16)

    rows = lambda n: pl.BlockSpec((tm, n), lambda i: (i, 0))
    return pl.pallas_call(
        body, grid=(T // tm,), name="mla_prep",
        in_specs=[rows(Q_LORA), rows(KV_LORA), rows(LANES), pl.BlockSpec((1, 1, tm), lambda i: (i, 0, 0)),
                  _resident((1, Q_LORA)), _resident((1, KV_LORA)), _resident((Q_LORA, HW)), _resident((KV_LORA, 2 * HW)),
                  _resident((QK_ROPE // 2, 1)), _resident((QK_ROPE // 2, LANES))],
        out_specs=[rows(HW), rows(HW), rows(HW), rows(LANES), rows(LANES)],
        out_shape=[jax.ShapeDtypeStruct((T, HW), BF16)] * 3 + [jax.ShapeDtypeStruct((T, LANES), F32)] * 2,
        compiler_params=_cparams(("arbitrary",)),
    )(cq, ckv, dtk, pos.reshape(T // tm, 1, tm), qw, kvw, wuq, wukv, invf, place)


def _mla_prep_bwd(dq, dk, dv, cq, ckv, cos_t, sin_t, qw, kvw, wuq, wukv):
    T = cq.shape[0]
    tm = min(WIDE_TOKEN_TILE, T)
    scale = 1.0 / math.sqrt(QK_DIM)
    HW = MLA_HEADS * HEAD_LANES

    def body(dq_ref, dk_ref, dv_ref, cq_ref, ckv_ref, cos_ref, sin_ref, qw_ref, kvw_ref, wuq_ref, wukv_ref,
             dcq_ref, dckv_ref, ddtk_ref, qn_ref, kvn_ref, dqo_ref, dkvo_ref, dqw_ref, dkvw_ref):
        @pl.when(pl.program_id(0) == 0)
        def _():
            dqw_ref[...] = jnp.zeros_like(dqw_ref)
            dkvw_ref[...] = jnp.zeros_like(dkvw_ref)

        cosf, sinf = cos_ref[...], sin_ref[...]
        unrope = lambda d: d * cosf - _rot(d * sinf)
        dkr = jnp.zeros((tm, LANES), F32)
        nope = lax.broadcasted_iota(jnp.int32, (tm, LANES), 1) < QK_NOPE
        for h in range(MLA_HEADS):
            cols = slice(h * HEAD_LANES, (h + 1) * HEAD_LANES)
            dqo_ref[:, cols] = unrope(dq_ref[:, cols].astype(F32) * scale).astype(BF16)
            dkh = dk_ref[:, cols].astype(F32)
            dkr = dkr + jnp.where(_rope_lanes(dkh.shape), dkh, 0.0)
            dkvo_ref[:, cols] = jnp.where(nope, dkh, 0.0).astype(BF16)
        dkvo_ref[:, HW:] = dv_ref[...].astype(BF16)
        ddtk_ref[...] = unrope(dkr)
        xh, r = _rms_stats(cq_ref[...])
        qn_ref[...] = (xh * qw_ref[...]).astype(BF16)
        dx, dw_rows = _rms_bwd(_dot_nt(dqo_ref[...], wuq_ref[...]), xh, r, qw_ref[...])
        dcq_ref[...] = dx
        dqw_ref[...] += _colsum(dw_rows)
        xh, r = _rms_stats(ckv_ref[...])
        kvn_ref[...] = (xh * kvw_ref[...]).astype(BF16)
        dx, dw_rows = _rms_bwd(_dot_nt(dkvo_ref[...], wukv_ref[...]), xh, r, kvw_ref[...])
        dckv_ref[...] = dx
        dkvw_ref[...] += _colsum(dw_rows)

    rows = lambda n: pl.BlockSpec((tm, n), lambda i: (i, 0))
    sd = jax.ShapeDtypeStruct
    return pl.pallas_call(
        body, grid=(T // tm,), name="mla_prep_bwd",
        in_specs=[rows(HW), rows(HW), rows(HW), rows(Q_LORA), rows(KV_LORA), rows(LANES), rows(LANES), _resident((1, Q_LORA)),
                  _resident((1, KV_LORA)), _resident((Q_LORA, HW)), _resident((KV_LORA, 2 * HW))],
        out_specs=[rows(Q_LORA), rows(KV_LORA), rows(LANES), rows(Q_LORA), rows(KV_LORA), rows(HW), rows(2 * HW),
                   pl.BlockSpec((1, Q_LORA), lambda i: (0, 0)), pl.BlockSpec((1, KV_LORA), lambda i: (0, 0))],
        out_shape=[sd((T, Q_LORA), F32), sd((T, KV_LORA), F32), sd((T, LANES), F32), sd((T, Q_LORA), BF16),
                   sd((T, KV_LORA), BF16), sd((T, HW), BF16), sd((T, 2 * HW), BF16), sd((1, Q_LORA), F32),
                   sd((1, KV_LORA), F32)],
        compiler_params=_cparams(("arbitrary",)),
    )(dq, dk, dv, cq, ckv, cos_t, sin_t, qw, kvw, wuq, wukv)


def _causal_mask(t):
    row = lax.broadcasted_iota(jnp.int32, (t, t), 0)
    col = lax.broadcasted_iota(jnp.int32, (t, t), 1)
    return col <= row


def _attn_fwd(q, k, v):
    B, S, HW = q.shape
    H = HW // HEAD_LANES
    t = min(ATTN_FWD_Q_TILE, S)
    tk = min(ATTN_FWD_KV_TILE, t)
    nq = S // t
    per = t // tk

    pair = 2
    pw = pair * HEAD_LANES

    def body(q_ref, k_ref, v_ref, o_ref, lse_ref):
        qi = pl.program_id(2)
        lanes = [slice(hh * HEAD_LANES, (hh + 1) * HEAD_LANES) for hh in range(pair)]
        qs = [q_ref[0, :, cols] for cols in lanes]

        def step(j, carry, diag):
            sl = pl.ds(pl.multiple_of(j * tk, tk), tk)
            out = []
            for qv, cols, (m, l, acc) in zip(qs, lanes, carry):
                s = _dot_nt(qv, k_ref[0, sl, cols])
                if diag is not None:
                    row = lax.broadcasted_iota(jnp.int32, (t, tk), 0)
                    col = lax.broadcasted_iota(jnp.int32, (t, tk), 1)
                    s = jnp.where(col + diag * tk <= row, s, -1e30)
                m_new = jnp.maximum(m, jnp.max(s, axis=-1, keepdims=True))
                alpha = jnp.exp(m - m_new)
                p = jnp.exp(s - m_new)
                l = alpha * l + jnp.sum(p, axis=-1, keepdims=True)
                acc = alpha * acc + _dot(p.astype(BF16), v_ref[0, sl, cols])
                out.append((m_new, l, acc))
            return tuple(out)

        init = tuple((jnp.full((t, 1), -1e30, F32), jnp.zeros((t, 1), F32), jnp.zeros((t, HEAD_LANES), F32))
                     for _ in range(pair))
        carry = lax.fori_loop(0, qi * per, lambda j, c: step(j, c, None), init)
        for d in range(per):
            carry = step(qi * per + d, carry, d)
        for hh, (m, l, acc) in enumerate(carry):
            o_ref[0, :, lanes[hh]] = (acc / l).astype(BF16)
            lse_ref[0, hh] = m + jnp.log(l)

    return pl.pallas_call(
        body, grid=(B, H // pair, nq), name="attn_fwd",
        in_specs=[pl.BlockSpec((1, t, pw), lambda b, h, i: (b, i, h)),
                  pl.BlockSpec((1, S, pw), lambda b, h, i: (b, 0, h)),
                  pl.BlockSpec((1, S, pw), lambda b, h, i: (b, 0, h))],
        out_specs=[pl.BlockSpec((1, t, pw), lambda b, h, i: (b, i, h)),
                   pl.BlockSpec((1, pair, t, 1), lambda b, h, i: (b, h, i, 0))],
        out_shape=[jax.ShapeDtypeStruct((B, S, HW), BF16), jax.ShapeDtypeStruct((B, H, S, 1), F32)],
        compiler_params=_cparams(("arbitrary", "arbitrary", "arbitrary")),
    )(q, k, v)


def _attn_bwd(q, k, v, o, do, lse):
    B, S, HW = q.shape
    H = HW // HEAD_LANES
    t = min(ATTN_BWD_TILE, S)
    nq = S // t

    def body(q_ref, k_ref, v_ref, o_ref, do_ref, lse_ref, dq_out_ref, dk_ref, dv_ref, dq_ref):
        j = pl.program_id(2)

        @pl.when(j == 0)
        def _():
            dq_ref[...] = jnp.zeros_like(dq_ref)

        kj = k_ref[0]
        vj = v_ref[0]

        def step(i, carry, masked):
            dk, dv = carry
            sl = pl.ds(pl.multiple_of(i * t, t), t)
            qi = q_ref[0, sl, :]
            doi = do_ref[0, sl, :]
            s = _dot_nt(qi, kj)
            if masked:
                s = jnp.where(_causal_mask(t), s, -1e30)
            p = jnp.exp(s - lse_ref[0, 0, sl, :])
            dv = dv + _dot_tn(p.astype(BF16), doi)
            dp = _dot_nt(doi, vj)
            delta = jnp.sum(doi.astype(F32) * o_ref[0, sl, :].astype(F32), axis=-1, keepdims=True)
            dsb = (p * (dp - delta)).astype(BF16)
            dk = dk + _dot_tn(dsb, qi)
            dq_ref[sl, :] += _dot(dsb, kj)
            return dk, dv

        zero = jnp.zeros((t, HEAD_LANES), F32)
        carry = step(j, (zero, zero), True)
        dk, dv = lax.fori_loop(j + 1, nq, lambda i, c: step(i, c, False), carry)
        dk_ref[0] = dk.astype(BF16)
        dv_ref[0] = dv.astype(BF16)

        @pl.when(j == nq - 1)
        def _():
            dq_out_ref[0] = dq_ref[...].astype(BF16)

    full = pl.BlockSpec((1, S, HEAD_LANES), lambda b, h, j: (b, 0, h))
    tile = pl.BlockSpec((1, t, HEAD_LANES), lambda b, h, j: (b, j, h))
    sd = jax.ShapeDtypeStruct
    return pl.pallas_call(
        body, grid=(B, H, nq), name="attn_bwd",
        in_specs=[full, tile, tile, full, full, pl.BlockSpec((1, 1, S, 1), lambda b, h, j: (b, h, 0, 0))],
        out_specs=[full, tile, tile],
        out_shape=[sd((B, S, HW), BF16), sd((B, S, HW), BF16), sd((B, S, HW), BF16)],
        scratch_shapes=[pltpu.VMEM((S, HEAD_LANES), F32)],
        compiler_params=_cparams(("arbitrary", "arbitrary", "arbitrary")),
    )(q, k, v, o, do, lse)


def _mix_out(x1, yssd, o, mw, wout, g, seq):
    T, D = x1.shape
    tm = min(WIDE_TOKEN_TILE, seq)
    tps = seq // tm

    def body(x_ref, ys_ref, o_ref, mw_ref, w_ref, g_ref, xo_ref, m_ref, yc_ref):
        xh, _ = _rms_stats(o_ref[...].astype(F32))
        ycat = jnp.concatenate([ys_ref[...], (xh * mw_ref[...]).astype(BF16)], axis=1)
        m = _dot(ycat, w_ref[...])
        xo_ref[...] = x_ref[...] + g_ref[0] * m
        m_ref[...] = m.astype(BF16)
        yc_ref[...] = ycat

    rows = lambda n: pl.BlockSpec((tm, n), lambda i: (i, 0))
    perb = pl.BlockSpec((1, 1, D), lambda i: (i // tps, 0, 0))
    sd = jax.ShapeDtypeStruct
    return pl.pallas_call(
        body, grid=(T // tm,), name="mix_out",
        in_specs=[rows(D), rows(D_SSD), rows(D_MLA), _resident((1, D_MLA)), _resident((D_SSD + D_MLA, D)), perb],
        out_specs=[rows(D), rows(D), rows(D_SSD + D_MLA)],
        out_shape=[sd((T, D), F32), sd((T, D), BF16), sd((T, D_SSD + D_MLA), BF16)],
        compiler_params=_cparams(("arbitrary",)),
    )(x1, yssd, o, mw, wout, g)


def _mix_out_bwd(dx2, m, o, mw, wout, g, seq):
    T, D = dx2.shape
    B = T // seq
    tm = min(WIDE_TOKEN_TILE, seq)
    tps = seq // tm

    def body(dx_ref, m_ref, o_ref, mw_ref, w_ref, g_ref, dys_ref, do_ref, dm_ref, dg_ref, dmw_ref):
        i = pl.program_id(0)

        @pl.when(i % tps == 0)
        def _():
            dg_ref[...] = jnp.zeros_like(dg_ref)

        @pl.when(i == 0)
        def _():
            dmw_ref[...] = jnp.zeros_like(dmw_ref)

        dxv = dx_ref[...]
        dg_ref[0] += _colsum(dxv * m_ref[...].astype(F32))
        dmb = (g_ref[0] * dxv).astype(BF16)
        dm_ref[...] = dmb
        dycat = _dot_nt(dmb, w_ref[...])
        dys_ref[...] = dycat[:, :D_SSD].astype(BF16)
        xh, r = _rms_stats(o_ref[...].astype(F32))
        dx, dw_rows = _rms_bwd(dycat[:, D_SSD:], xh, r, mw_ref[...])
        do_ref[...] = dx.astype(BF16)
        dmw_ref[...] += _colsum(dw_rows)

    rows = lambda n: pl.BlockSpec((tm, n), lambda i: (i, 0))
    perb = pl.BlockSpec((1, 1, D), lambda i: (i // tps, 0, 0))
    sd = jax.ShapeDtypeStruct
    return pl.pallas_call(
        body, grid=(T // tm,), name="mix_out_bwd",
        in_specs=[rows(D), rows(D), rows(D_MLA), _resident((1, D_MLA)), _resident((D_SSD + D_MLA, D)), perb],
        out_specs=[rows(D_SSD), rows(D_MLA), rows(D), perb, pl.BlockSpec((1, D_MLA), lambda i: (0, 0))],
        out_shape=[sd((T, D_SSD), BF16), sd((T, D_MLA), BF16), sd((T, D), BF16), sd((B, 1, D), F32), sd((1, D_MLA), F32)],
        compiler_params=_cparams(("arbitrary",)),
    )(dx2, m, o, mw, wout, g)


def _win_to_kernel(w):
    z0 = jnp.zeros((48, w.shape[1]), w.dtype)
    z1 = jnp.zeros((32, w.shape[1]), w.dtype)
    return jnp.concatenate([w[:2560], w[2576:3216], w[2560:2576], z0, w[3216:3248], z1], axis=0)


def _win_from_kernel(g):
    return jnp.concatenate([g[:2560], g[3200:3216], g[2560:3200], g[3264:3296]], axis=0)


def _wuq_to_kernel(w):
    w = w.reshape(Q_LORA, MLA_HEADS, QK_DIM)
    return jnp.pad(w, ((0, 0), (0, 0), (0, HEAD_LANES - QK_DIM))).reshape(Q_LORA, MLA_HEADS * HEAD_LANES)


def _wuq_from_kernel(g):
    return g.reshape(Q_LORA, MLA_HEADS, HEAD_LANES)[:, :, :QK_DIM].reshape(Q_LORA, MLA_HEADS * QK_DIM)


def _wukv_to_kernel(w):
    w = w.reshape(KV_LORA, MLA_HEADS, QK_NOPE + V_HEAD)
    kp = jnp.pad(w[:, :, :QK_NOPE], ((0, 0), (0, 0), (0, HEAD_LANES - QK_NOPE)))
    return jnp.concatenate([kp.reshape(KV_LORA, -1), w[:, :, QK_NOPE:].reshape(KV_LORA, -1)], axis=1)


def _wukv_from_kernel(g):
    hw = MLA_HEADS * HEAD_LANES
    kp = g[:, :hw].reshape(KV_LORA, MLA_HEADS, HEAD_LANES)[:, :, :QK_NOPE]
    vp = g[:, hw:].reshape(KV_LORA, MLA_HEADS, V_HEAD)
    return jnp.concatenate([kp, vp], axis=2).reshape(KV_LORA, MLA_HEADS * (QK_NOPE + V_HEAD))


def _lanes16(v):
    return jnp.pad(v.reshape(1, SSD_HEADS), ((0, 0), (0, LANES - SSD_HEADS)))


def _constants():
    e = np.zeros((LANES, D_SSD), np.float32)
    for h in range(SSD_HEADS):
        e[h, h * SSD_HEAD_DIM:(h + 1) * SSD_HEAD_DIM] = 1.0
    inv_freq = ROPE_THETA ** (-jnp.arange(0, QK_ROPE, 2, dtype=F32) / QK_ROPE)
    half = QK_ROPE // 2
    place = np.zeros((half, LANES), np.float32)
    for j in range(half):
        place[j, QK_NOPE + j] = place[j, QK_NOPE + half + j] = 1.0
    return jnp.asarray(e, BF16), inv_freq.reshape(half, 1), jnp.asarray(place, BF16)


def _local_step(x, positions, mod, w, later_weights, small, tgt, on_grads):
    B, S, D = x.shape
    T = B * S
    expand, invf, place = _constants()
    x0 = x.reshape(T, D)
    pos = positions.reshape(T)
    mods = [mod[:, i * D:(i + 1) * D].reshape(B, 1, D) for i in range(N_MOD)]
    sh1, sc1, g1, sh2, sc2, g2, sh3, sc3, g3 = mods
    dtb, alog = _lanes16(small["dt_bias"]), _lanes16(small["a_log"])
    dsk = jnp.repeat(small["d_skip"].reshape(1, SSD_HEADS), SSD_HEAD_DIM, axis=1)

    x1, a1, u1, f1 = _ffn_fwd(x0, small["norm_ffn1"], sh1, sc1, g1, w["ffn1_w_gate"], w["ffn1_w_up"], w["ffn1_w_down"], S, "ffn1_fwd")
    w = {**w, **later_weights(f1)}
    z, xraw, cq, ckv, dtk = _inproj_fwd(x1, small["norm_mix"], sh2, sc2, w["w_in"], S)
    xraw3 = xraw.reshape(B, S, D_CONV)
    xbc = _conv_fwd(xraw3, small["conv_w"], small["conv_b"])
    dtk3, z3 = dtk.reshape(B, S, LANES), z.reshape(B, S, D_SSD)
    y, yssd, prev = _ssd_fwd(xbc, dtk3, z3, dtb, alog, dsk, small["ssd_norm_w"], expand)
    q, k, v, cos_t, sin_t = _mla_prep(cq, ckv, dtk, pos, small["q_norm_w"], small["kv_norm_w"], w["w_uq"], w["w_ukv"], invf,
                                      place)
    hw = MLA_HEADS * HEAD_LANES
    q3, k3, v3 = q.reshape(B, S, hw), k.reshape(B, S, hw), v.reshape(B, S, hw)
    o3, lse = _attn_fwd(q3, k3, v3)
    o = o3.reshape(T, hw)
    x2, m, ycat = _mix_out(x1, yssd.reshape(T, D_SSD), o, small["mla_norm_w"], w["w_out"], g2, S)
    dx3, a2, u2, f2, loss, d_norm_final = _ffn_fwd(
        x2, small["norm_ffn2"], sh3, sc3, g3, w["ffn2_w_gate"], w["ffn2_w_up"], w["ffn2_w_down"], S, "ffn2_fwd",
        head=(small["norm_final"].reshape(1, D), tgt.reshape(T, D)))

    gw, gs = {}, {}
    dx2, h3, s3, df3, da3, du3, dsh3, dsc3, dg3, gs["norm_ffn2"] = _ffn_bwd(
        dx3, x2, small["norm_ffn2"], sh3, sc3, g3, a2, u2, f2, w["ffn2_w_gate"], w["ffn2_w_up"], w["ffn2_w_down"], S, "ffn2_bwd")
    gw["ffn2_w_gate"], gw["ffn2_w_up"], gw["ffn2_w_down"] = _ffn_wgrad(h3, s3, df3, da3, du3, "ffn2_wgrad")
    g2 = g2 + on_grads(("ffn2_w_gate", "ffn2_w_up", "ffn2_w_down"), gw)

    dys, do, dm, dg2, gs["mla_norm_w"] = _mix_out_bwd(dx2, m, o, small["mla_norm_w"], w["w_out"], g2, S)
    gw["w_out"] = _mm_tn(ycat, dm, 512, "dwout")

    dq3, dk3, dv3 = _attn_bwd(q3, k3, v3, o3, do.reshape(B, S, hw), lse)
    dcq, dckv, ddtk_b, qn, kvn, dqb, dkvb, gs["q_norm_w"], gs["kv_norm_w"] = _mla_prep_bwd(
        dq3.reshape(T, hw), dk3.reshape(T, hw), dv3.reshape(T, hw), cq, ckv, cos_t, sin_t, small["q_norm_w"],
        small["kv_norm_w"], w["w_uq"], w["w_ukv"])
    gw["w_uq"] = _mm_tn(qn, dqb, 512, "dwuq")
    gw["w_ukv"] = _mm_tn(kvn, dkvb, 1024, "dwukv")

    dxbc, ddtk_a, dz, gs["ssd_norm_w"], dvec = _ssd_bwd(
        xbc, dtk3, z3, y, prev, dys.reshape(B, S, D_SSD), dtb, alog, dsk, small["ssd_norm_w"], expand)
    gs["dt_bias"], gs["a_log"], gs["d_skip"] = dvec[0:1, :SSD_HEADS], dvec[1:2, :SSD_HEADS], dvec[2:3, :SSD_HEADS]
    dxraw, gs["conv_w"], gs["conv_b"] = _conv_bwd(dxbc, xraw3, small["conv_w"], small["conv_b"])
    dx1, h2, dproj, dsh2, dsc2, gs["norm_mix"] = _inproj_bwd(
        dx2, x1, small["norm_mix"], sh2, sc2, w["w_in"], dz.reshape(T, D_SSD), dxraw.reshape(T, D_CONV), dcq, dckv,
        ddtk_a.reshape(T, LANES), ddtk_b, S)
    gw["w_in"] = _mm_tn(dproj, h2, 512, "dwin")
    g1 = g1 + on_grads(("w_in", "w_uq", "w_ukv", "w_out"), gw)

    dx0, h1, s1, df1, da1, du1, dsh1, dsc1, dg1, gs["norm_ffn1"] = _ffn_bwd(
        dx1, x0, small["norm_ffn1"], sh1, sc1, g1, a1, u1, f1, w["ffn1_w_gate"], w["ffn1_w_up"], w["ffn1_w_down"], S, "ffn1_bwd")
    gw["ffn1_w_gate"], gw["ffn1_w_up"], gw["ffn1_w_down"] = _ffn_wgrad(h1, s1, df1, da1, du1, "ffn1_wgrad")
    gs["norm_final"] = d_norm_final
    dmod = jnp.concatenate([t.reshape(B, D) for t in (dsh1, dsc1, dg1, dsh2, dsc2, dg2, dsh3, dsc3, dg3)], axis=1)
    return loss, dx0.reshape(B, S, D), gw, dmod, gs


HBM_SPEC = pl.BlockSpec(memory_space=pltpu.HBM)
VMEM_SPEC = pl.BlockSpec(memory_space=pltpu.VMEM)


def _place():
    return lax.axis_index("x"), lax.axis_index("y"), lax.axis_index("c")


def _other_chips(mx, my):
    return [(1 - mx, my), (mx, 1 - my), (1 - mx, 1 - my)]


def _remote(src, dst, send_sem, recv_sem, to):
    return pltpu.make_async_remote_copy(src_ref=src, dst_ref=dst, send_sem=send_sem, recv_sem=recv_sem,
                                        device_id=to, device_id_type=MESH)


def _all_gather_small(xa, name):
    r, n = xa.shape

    def body(x_ref, o_ref, token, send_sems, recv_sems):
        mx, my, mc = _place()
        me = 4 * mx + 2 * my + mc
        token[...] = jnp.zeros_like(token)
        o_ref[pl.ds(me, 1)] = x_ref[...][None]
        sends = []
        for k in range(1, N_DEV):
            peer = (mx ^ (k >> 2), my ^ ((k >> 1) & 1), mc ^ (k & 1))
            cp = _remote(x_ref, o_ref.at[me], send_sems.at[k - 1], recv_sems.at[k - 1], peer)
            cp.start()
            sends.append(cp)
        for k in range(1, N_DEV):
            peer = (mx ^ (k >> 2), my ^ ((k >> 1) & 1), mc ^ (k & 1))
            slot = 4 * peer[0] + 2 * peer[1] + peer[2]
            _remote(x_ref, o_ref.at[slot], send_sems.at[k - 1], recv_sems.at[k - 1], peer).wait_recv()
        for cp in sends:
            cp.wait_send()

    return pl.pallas_call(
        body, name=name, in_specs=[VMEM_SPEC], out_specs=[VMEM_SPEC, VMEM_SPEC],
        out_shape=[jax.ShapeDtypeStruct((N_DEV, r, n), xa.dtype), jax.ShapeDtypeStruct((8, LANES), F32)],
        scratch_shapes=[pltpu.SemaphoreType.DMA((N_DEV - 1,)), pltpu.SemaphoreType.DMA((N_DEV - 1,))],
        compiler_params=pltpu.CompilerParams(vmem_limit_bytes=VMEM_LIMIT),
    )(xa)


def _halves_by_rows(shape):
    return (shape[-2] // 2) % 16 == 0


def _half_shape(shape):
    r, c = shape[-2:]
    return tuple(shape[:-2]) + ((r // 2, c) if _halves_by_rows(shape) else (r, c // 2))


def _half_index(shape, hc):
    r, c = shape[-2:]
    if _halves_by_rows(shape):
        return (pl.ds(pl.multiple_of(hc * (r // 2), 16), r // 2), slice(None))
    return (slice(None), pl.ds(pl.multiple_of(hc * (c // 2), LANES), c // 2))


def _half(ref, hc, lead=None):
    idx = _half_index(ref.shape, hc)
    return ref.at[idx] if lead is None else ref.at[(lead,) + idx]


def _gather_weights(shards):
    n = len(shards)

    def body(*refs):
        w_refs, o_refs, token = refs[:n], refs[n:2 * n], refs[2 * n]
        send_sems, recv_sems, stage_sems = refs[2 * n + 1:2 * n + 4]
        stages = refs[2 * n + 4:]
        mx, my, mc = _place()
        chip = 2 * mx + my
        others = _other_chips(mx, my)
        sibling = (mx, my, 1 - mc)
        token[...] = jnp.zeros_like(token)
        stage_in = [pltpu.make_async_copy(w, st, stage_sems.at[0, i]) for i, (w, st) in enumerate(zip(w_refs, stages))]
        for cp in stage_in:
            cp.start()
        first = []
        for i, (w, o) in enumerate(zip(w_refs, o_refs)):
            for k, (cx, cy) in enumerate(others):
                first.append(_remote(_half(w, mc), _half(o, mc, chip), send_sems.at[i, k],
                                     recv_sems.at[i, k], (cx, cy, mc)))
                first[-1].start()
        stage_out = []
        for i, (st, o) in enumerate(zip(stages, o_refs)):
            stage_in[i].wait()
            stage_out.append(pltpu.make_async_copy(st, o.at[chip], stage_sems.at[1, i]))
            stage_out[-1].start()
        passed = []
        for i, (w, o) in enumerate(zip(w_refs, o_refs)):
            for k, (cx, cy) in enumerate(others):
                landed = _half(o, mc, 2 * cx + cy)
                _remote(landed, landed, send_sems.at[i, k], recv_sems.at[i, k], (cx, cy, mc)).wait_recv()
                passed.append(_remote(landed, landed, send_sems.at[i, 3 + k], recv_sems.at[i, 3 + k], sibling))
                passed[-1].start()
        for i, (w, o) in enumerate(zip(w_refs, o_refs)):
            for k, (cx, cy) in enumerate(others):
                there = _half(o, 1 - mc, 2 * cx + cy)
                _remote(there, there, send_sems.at[i, 3 + k], recv_sems.at[i, 3 + k], sibling).wait_recv()
        for cp in first + passed:
            cp.wait_send()
        for cp in stage_out:
            cp.wait()

    out = pl.pallas_call(
        body, name="gather_weights", in_specs=[HBM_SPEC] * n, out_specs=[HBM_SPEC] * n + [VMEM_SPEC],
        out_shape=[jax.ShapeDtypeStruct((N_CHIPS,) + s.shape, s.dtype) for s in shards] + [jax.ShapeDtypeStruct((8, LANES), F32)],
        scratch_shapes=[pltpu.SemaphoreType.DMA((n, 6)), pltpu.SemaphoreType.DMA((n, 6)), pltpu.SemaphoreType.DMA((2, n))]
        + [pltpu.VMEM(s.shape, s.dtype) for s in shards],
        compiler_params=pltpu.CompilerParams(vmem_limit_bytes=VMEM_LIMIT),
    )(*shards)
    return out[:n], out[n]


SEM_SPEC = pl.BlockSpec(memory_space=pltpu.SEMAPHORE)
ANY_SPEC = pl.BlockSpec(memory_space=pl.ANY)
DATAFLOW = pltpu.SideEffectType.DATAFLOW_SIDE_EFFECTING


def _hbm(arr):
    return pltpu.with_memory_space_constraint(arr, pltpu.HBM)


def _gather_start(shards):
    n = len(shards)

    def body(*refs):
        w_refs, land_refs, send_sems, recv_sems, token = refs[:n], refs[n:2 * n], refs[2 * n], refs[2 * n + 1], refs[-1]
        mx, my, mc = _place()
        chip = 2 * mx + my
        for i, (w, land) in enumerate(zip(w_refs, land_refs)):
            for k, (cx, cy) in enumerate(_other_chips(mx, my)):
                _remote(_half(w, mc), _half(land, mc, chip), send_sems.at[3 * i + k],
                        recv_sems.at[3 * i + k], (cx, cy, mc)).start()
        token[...] = jnp.zeros_like(token)

    lands = [lax.empty((N_CHIPS,) + s.shape, s.dtype) for s in shards]
    out = pl.pallas_call(
        body, name="gather_start",
        out_shape=(pltpu.SemaphoreType.DMA((3 * n,)), pltpu.SemaphoreType.DMA((3 * n,)),
                   *[pltpu.HBM(s.shape, s.dtype) for s in shards], *[pltpu.HBM(l.shape, l.dtype) for l in lands],
                   jax.ShapeDtypeStruct((8, LANES), F32)),
        in_specs=[HBM_SPEC] * (2 * n), out_specs=(SEM_SPEC, SEM_SPEC, *[HBM_SPEC] * (2 * n), VMEM_SPEC),
        input_output_aliases={i: 2 + i for i in range(2 * n)},
        compiler_params=pltpu.CompilerParams(has_side_effects=DATAFLOW),
    )(*[_hbm(s) for s in shards], *[_hbm(l) for l in lands])
    return out[0], out[1], out[2:2 + n], out[2 + n:2 + 2 * n], out[-1]


def _gather_wait(send_sems, recv_sems, shards, lands, after):
    n = len(shards)

    def body(*refs):
        w_refs, land_refs, send_sems, recv_sems = refs[:n], refs[n:2 * n], refs[2 * n], refs[2 * n + 1]
        mx, my, mc = _place()
        for i, (w, land) in enumerate(zip(w_refs, land_refs)):
            for k, (cx, cy) in enumerate(_other_chips(mx, my)):
                cp = _remote(_half(w, mc), _half(land, mc, 2 * cx + cy), send_sems.at[3 * i + k],
                             recv_sems.at[3 * i + k], (cx, cy, mc))
                cp.wait_send()
                cp.wait_recv()

    out = pl.pallas_call(
        body, name="gather_wait",
        out_shape=(*[pltpu.HBM(s.shape, s.dtype) for s in shards], *[pltpu.HBM(l.shape, l.dtype) for l in lands]),
        in_specs=[HBM_SPEC] * (2 * n) + [SEM_SPEC, SEM_SPEC, ANY_SPEC], out_specs=tuple([HBM_SPEC] * (2 * n)),
        input_output_aliases={i: i for i in range(2 * n)},
        compiler_params=pltpu.CompilerParams(has_side_effects=DATAFLOW),
    )(*shards, *lands, send_sems, recv_sems, after)
    return out[n:]


def _gather_finish(shards, lands):
    n = len(shards)

    def body(*refs):
        w_refs, land_refs, o_refs = refs[:n], refs[n:2 * n], refs[2 * n:3 * n]
        send_sems, recv_sems, stage_sems = refs[3 * n:3 * n + 3]
        stages = refs[3 * n + 3:]
        mx, my, mc = _place()
        chip = 2 * mx + my
        others = _other_chips(mx, my)
        sibling = (mx, my, 1 - mc)
        stage_in = [pltpu.make_async_copy(w, st, stage_sems.at[0, i]) for i, (w, st) in enumerate(zip(w_refs, stages))]
        for cp in stage_in:
            cp.start()
        passed = []
        for i, (w, o) in enumerate(zip(w_refs, o_refs)):
            for k, (cx, cy) in enumerate(others):
                landed = _half(o, mc, 2 * cx + cy)
                passed.append(_remote(landed, landed, send_sems.at[i, k], recv_sems.at[i, k], sibling))
                passed[-1].start()
        stage_out = []
        for i, (st, o) in enumerate(zip(stages, o_refs)):
            stage_in[i].wait()
            stage_out.append(pltpu.make_async_copy(st, o.at[chip], stage_sems.at[1, i]))
            stage_out[-1].start()
        for i, (w, o) in enumerate(zip(w_refs, o_refs)):
            for k, (cx, cy) in enumerate(others):
                there = _half(o, 1 - mc, 2 * cx + cy)
                _remote(there, there, send_sems.at[i, k], recv_sems.at[i, k], sibling).wait_recv()
        for cp in passed:
            cp.wait_send()
        for cp in stage_out:
            cp.wait()

    return pl.pallas_call(
        body, name="gather_finish", in_specs=[HBM_SPEC] * (2 * n), out_specs=[HBM_SPEC] * n,
        out_shape=[jax.ShapeDtypeStruct(l.shape, l.dtype) for l in lands],
        input_output_aliases={n + i: i for i in range(n)},
        scratch_shapes=[pltpu.SemaphoreType.DMA((n, 3)), pltpu.SemaphoreType.DMA((n, 3)), pltpu.SemaphoreType.DMA((2, n))]
        + [pltpu.VMEM(s.shape, s.dtype) for s in shards],
        compiler_params=pltpu.CompilerParams(vmem_limit_bytes=VMEM_LIMIT),
    )(*shards, *lands)


def _scatter_start(ss, tag):
    n = len(ss)

    def body(*refs):
        s_refs, land_refs, send_sems, recv_sems, token = refs[:n], refs[n:2 * n], refs[2 * n], refs[2 * n + 1], refs[-1]
        mx, my, mc = _place()
        chip = 2 * mx + my
        for i, (s, land) in enumerate(zip(s_refs, land_refs)):
            for k, (cx, cy) in enumerate(_other_chips(mx, my)):
                _remote(s.at[2 * cx + cy], land.at[chip], send_sems.at[3 * i + k], recv_sems.at[3 * i + k],
                        (cx, cy, mc)).start()
        token[...] = jnp.zeros_like(token)

    lands = [lax.empty(s.shape, s.dtype) for s in ss]
    out = pl.pallas_call(
        body, name="scatter_start_" + tag,
        out_shape=(pltpu.SemaphoreType.DMA((3 * n,)), pltpu.SemaphoreType.DMA((3 * n,)),
                   *[pltpu.HBM(s.shape, s.dtype) for s in ss], *[pltpu.HBM(l.shape, l.dtype) for l in lands],
                   jax.ShapeDtypeStruct((8, LANES), F32)),
        in_specs=[HBM_SPEC] * (2 * n), out_specs=(SEM_SPEC, SEM_SPEC, *[HBM_SPEC] * (2 * n), VMEM_SPEC),
        input_output_aliases={i: 2 + i for i in range(2 * n)},
        compiler_params=pltpu.CompilerParams(has_side_effects=DATAFLOW),
    )(*[_hbm(s) for s in ss], *[_hbm(l) for l in lands])
    return out[0], out[1], out[2:2 + n], out[2 + n:2 + 2 * n], out[-1]


def _scatter_wait(send_sems, recv_sems, ss, lands, after, tag):
    n = len(ss)

    def body(*refs):
        s_refs, land_refs, send_sems, recv_sems = refs[:n], refs[n:2 * n], refs[2 * n], refs[2 * n + 1]
        mx, my, mc = _place()
        for i, (s, land) in enumerate(zip(s_refs, land_refs)):
            for k, (cx, cy) in enumerate(_other_chips(mx, my)):
                slot = land.at[2 * cx + cy]
                cp = _remote(s.at[2 * cx + cy], slot, send_sems.at[3 * i + k], recv_sems.at[3 * i + k], (cx, cy, mc))
                cp.wait_send()
                cp.wait_recv()

    out = pl.pallas_call(
        body, name="scatter_wait_" + tag,
        out_shape=(*[pltpu.HBM(s.shape, s.dtype) for s in ss], *[pltpu.HBM(l.shape, l.dtype) for l in lands]),
        in_specs=[HBM_SPEC] * (2 * n) + [SEM_SPEC, SEM_SPEC, ANY_SPEC], out_specs=tuple([HBM_SPEC] * (2 * n)),
        input_output_aliases={i: i for i in range(2 * n)},
        compiler_params=pltpu.CompilerParams(has_side_effects=DATAFLOW),
    )(*ss, *lands, send_sems, recv_sems, after)
    return out[:n], out[n:]


def _swap_halves(gs, after, name):
    n = len(gs)

    def body(*refs):
        g_refs, o_refs, send_sems, recv_sems = refs[:n], refs[n + 1:2 * n + 1], refs[2 * n + 1], refs[2 * n + 2]
        mx, my, mc = _place()
        copies = []
        for i, (g, o) in enumerate(zip(g_refs, o_refs)):
            src = g.at[(slice(None),) + _half_index(g.shape, 1 - mc)]
            copies.append(_remote(src, o, send_sems.at[i], recv_sems.at[i], (mx, my, 1 - mc)))
            copies[-1].start()
        for cp in copies:
            cp.wait()

    return pl.pallas_call(
        body, name=name, in_specs=[HBM_SPEC] * n + [ANY_SPEC], out_specs=[HBM_SPEC] * n,
        out_shape=[jax.ShapeDtypeStruct(_half_shape(g.shape), g.dtype) for g in gs],
        scratch_shapes=[pltpu.SemaphoreType.DMA((n,)), pltpu.SemaphoreType.DMA((n,))],
    )(*gs, after)


def _pair_sum(g, got, core, name):
    hr, hc = _half_shape(g.shape)[1:]
    by_rows = _halves_by_rows(g.shape)

    def body(core_ref, g_ref, got_ref, o_ref):
        o_ref[...] = (g_ref[...].astype(F32) + got_ref[...].astype(F32)).astype(BF16)

    return pl.pallas_call(
        body, name=name,
        grid_spec=pltpu.PrefetchScalarGridSpec(
            num_scalar_prefetch=1, grid=(N_CHIPS,),
            in_specs=[pl.BlockSpec((1, hr, hc), lambda j, core_ref: (j, core_ref[0], 0) if by_rows else (j, 0, core_ref[0])),
                      pl.BlockSpec((1, hr, hc), lambda j, core_ref: (j, 0, 0))],
            out_specs=pl.BlockSpec((1, hr, hc), lambda j, core_ref: (j, 0, 0))),
        out_shape=jax.ShapeDtypeStruct((N_CHIPS, hr, hc), BF16),
        compiler_params=_cparams(("arbitrary",)),
    )(core, g, got)


def _chip_sum(own, got, chip, name):
    _, h, c = own.shape

    def body(chip_ref, a_ref, b_ref, c_ref, d_ref, o_ref):
        o_ref[...] = ((a_ref[0].astype(F32) + b_ref[0].astype(F32)) + c_ref[0].astype(F32)) + d_ref[0].astype(F32)

    slot = lambda flip: pl.BlockSpec((1, h, c), lambda i, chip_ref: (chip_ref[0] ^ flip, 0, 0))
    return pl.pallas_call(
        body, name=name,
        grid_spec=pltpu.PrefetchScalarGridSpec(
            num_scalar_prefetch=1, grid=(1,), in_specs=[slot(0), slot(1), slot(2), slot(3)],
            out_specs=pl.BlockSpec((h, c), lambda i, chip_ref: (0, 0))),
        out_shape=jax.ShapeDtypeStruct((h, c), F32),
        compiler_params=_cparams(("arbitrary",)),
    )(chip, own, got, got, got)


def _join_halves(mine, name):
    n = len(mine)

    def body(*refs):
        m_refs, o_refs, send_sems, recv_sems = refs[:n], refs[n:2 * n], refs[2 * n], refs[2 * n + 1]
        mx, my, mc = _place()
        copies = []
        for i, (m, o) in enumerate(zip(m_refs, o_refs)):
            copies.append(_remote(m, o, send_sems.at[i], recv_sems.at[i], (mx, my, 1 - mc)))
            copies[-1].start()
        for cp in copies:
            cp.wait()

    return pl.pallas_call(
        body, name=name, in_specs=[HBM_SPEC] * n, out_specs=[HBM_SPEC] * n,
        out_shape=[jax.ShapeDtypeStruct(m.shape, m.dtype) for m in mine],
        scratch_shapes=[pltpu.SemaphoreType.DMA((n,)), pltpu.SemaphoreType.DMA((n,))],
    )(*mine)


def _adam_math(w, g, m, v):
    m2 = ADAM_B1 * m + (1.0 - ADAM_B1) * g
    v2 = ADAM_B2 * v + (1.0 - ADAM_B2) * (g * g)
    m_hat = m2 * (1.0 / (1.0 - ADAM_B1 ** ADAM_STEP))
    v_hat = v2 * (1.0 / (1.0 - ADAM_B2 ** ADAM_STEP))
    delta = -ADAM_LR * (m_hat / (jnp.sqrt(v_hat) + ADAM_EPS) + ADAM_WD * w)
    return delta, m2, v2


def _adam(w, g, m, v, name):
    def body(w_ref, g_ref, m_ref, v_ref, d_ref, m2_ref, v2_ref):
        d_ref[...], m2_ref[...], v2_ref[...] = _adam_math(w_ref[...], g_ref[...], m_ref[...], v_ref[...])

    return pl.pallas_call(body, name=name, out_shape=[jax.ShapeDtypeStruct(w.shape, F32)] * 3)(w, g, m, v)


def _adam_halves(w, m, v, mine, theirs, core, name):
    hr, hcols = _half_shape(w.shape)[1:]
    by_rows = _halves_by_rows(w.shape)

    def body(core_ref, w_ref, m_ref, v_ref, mine_ref, theirs_ref, g_ref, d_ref, m2_ref, v2_ref):
        g = jnp.where(pl.program_id(0) == core_ref[0], mine_ref[...], theirs_ref[...])
        g_ref[0] = g
        d_ref[0], m2_ref[0], v2_ref[0] = _adam_math(w_ref[0], g, m_ref[0], v_ref[0])

    half = pl.BlockSpec((1, hr, hcols), lambda hc, core_ref: (0, hc, 0) if by_rows else (0, 0, hc))
    whole = pl.BlockSpec((hr, hcols), lambda hc, core_ref: (0, 0))
    return pl.pallas_call(
        body, name=name,
        grid_spec=pltpu.PrefetchScalarGridSpec(
            num_scalar_prefetch=1, grid=(2,), in_specs=[half, half, half, whole, whole], out_specs=[half] * 4),
        out_shape=[jax.ShapeDtypeStruct(w.shape, F32)] * 4,
        compiler_params=_cparams(("arbitrary",)),
    )(core, w, m, v, mine, theirs)


ADA_COLS = N_MOD * D_MODEL // N_CHIPS


def _ada_fwd(c_all, w_ada, b_cols):
    def body(c_ref, w_ref, b_ref, o_ref):
        cv = c_ref[...]
        act = (cv * _sigmoid(cv)).astype(BF16)
        o_ref[...] = _dot(act, w_ref[...].astype(BF16)) + b_ref[...]

    return pl.pallas_call(
        body, name="ada_fwd", out_shape=jax.ShapeDtypeStruct((c_all.shape[0], ADA_COLS), F32),
        compiler_params=pltpu.CompilerParams(vmem_limit_bytes=VMEM_LIMIT),
    )(c_all, w_ada, b_cols)


def _ada_bwd(c_all, dmod_cols, w, m, v):
    nb = c_all.shape[0]
    tn = 384

    def body(c_ref, d_ref, w_ref, m_ref, v_ref, g_ref, dl_ref, m2_ref, v2_ref):
        cv = c_ref[...]
        act = (cv * _sigmoid(cv)).astype(BF16)
        g = _dot_tn(act, d_ref[...].astype(BF16))
        g_ref[...] = g
        dl_ref[...], m2_ref[...], v2_ref[...] = _adam_math(w_ref[...], g, m_ref[...], v_ref[...])

    blk = pl.BlockSpec((D_MODEL, tn), lambda j: (0, j))
    return pl.pallas_call(
        body, name="ada_bwd", grid=(ADA_COLS // tn,),
        in_specs=[pl.BlockSpec((nb, D_MODEL), lambda j: (0, 0)), pl.BlockSpec((nb, tn), lambda j: (0, j)), blk, blk, blk],
        out_specs=[blk] * 4, out_shape=[jax.ShapeDtypeStruct((D_MODEL, ADA_COLS), F32)] * 4,
        compiler_params=_cparams(("arbitrary",)),
    )(c_all, dmod_cols, w, m, v)


SMALL_NAMES = ("norm_ffn1", "norm_mix", "conv_w", "conv_b", "ssd_norm_w", "q_norm_w", "kv_norm_w", "mla_norm_w",
               "norm_ffn2", "norm_final", "dt_bias", "a_log", "d_skip")
SMALL_SIZES = (1024, 1024, CONV_WIDTH * D_CONV, D_CONV, 1024, Q_LORA, KV_LORA, 1024, 1024, 1024, 16, 16, 16)
SMALL_ROWS = 16
MOD_ROWS = 2 * N_MOD
SEND_ROWS = 40


def _pack_small(parts):
    flat = jnp.concatenate([parts[n].reshape(-1) for n in SMALL_NAMES])
    return jnp.pad(flat, (0, SMALL_ROWS * D_MODEL - flat.shape[0]))


def _unpack_small(flat):
    out, off = {}, 0
    for n, size in zip(SMALL_NAMES, SMALL_SIZES):
        out[n] = flat[off:off + size]
        off += size
    return out


def _small_sum(got):
    def body(g_ref, o_ref):
        bsum = jnp.zeros((N_MOD, D_MODEL), F32)
        ssum = jnp.zeros((SMALL_ROWS, D_MODEL), F32)
        for d in range(N_DEV):
            bsum = bsum + g_ref[d, 0:N_MOD, :] + g_ref[d, N_MOD:MOD_ROWS, :]
            ssum = ssum + g_ref[d, MOD_ROWS:MOD_ROWS + SMALL_ROWS, :]
        o_ref[...] = jnp.concatenate([bsum, ssum, jnp.zeros((32 - N_MOD - SMALL_ROWS, D_MODEL), F32)], axis=0)

    return pl.pallas_call(body, name="small_sum", out_shape=jax.ShapeDtypeStruct((32, D_MODEL), F32))(got)


BIG_NAMES = ("ffn1_w_gate", "ffn1_w_up", "ffn1_w_down", "w_in", "w_uq", "w_ukv", "w_out", "ffn2_w_gate", "ffn2_w_up",
             "ffn2_w_down")
_TO_KERNEL = {"w_in": _win_to_kernel, "w_uq": _wuq_to_kernel, "w_ukv": _wukv_to_kernel}
_FROM_KERNEL = {"w_in": _win_from_kernel, "w_uq": _wuq_from_kernel, "w_ukv": _wukv_from_kernel}


def _columns_joined(w4):
    n, r, c = w4.shape
    return w4.transpose(1, 0, 2).reshape(r, n * c)


def _columns_split(g):
    r, cols = g.shape
    return g.reshape(r, N_CHIPS, cols // N_CHIPS).transpose(1, 0, 2)


def kernel(x, c, positions, w_ada, b_ada, norm_ffn1, ffn1_w_gate, ffn1_w_up, ffn1_w_down, norm_mix, w_in, conv_w, conv_b, dt_bias, a_log, d_skip, ssd_norm_w, q_norm_w, w_uq, kv_norm_w, w_ukv, mla_norm_w, w_out, norm_ffn2, ffn2_w_gate, ffn2_w_up, ffn2_w_down, norm_final, loss_target, m_w_ada, m_b_ada, m_norm_ffn1, m_ffn1_w_gate, m_ffn1_w_up, m_ffn1_w_down, m_norm_mix, m_w_in, m_conv_w, m_conv_b, m_dt_bias, m_a_log, m_d_skip, m_ssd_norm_w, m_q_norm_w, m_w_uq, m_kv_norm_w, m_w_ukv, m_mla_norm_w, m_w_out, m_norm_ffn2, m_ffn2_w_gate, m_ffn2_w_up, m_ffn2_w_down, m_norm_final, v_w_ada, v_b_ada, v_norm_ffn1, v_ffn1_w_gate, v_ffn1_w_up, v_ffn1_w_down, v_norm_mix, v_w_in, v_conv_w, v_conv_b, v_dt_bias, v_a_log, v_d_skip, v_ssd_norm_w, v_q_norm_w, v_w_uq, v_kv_norm_w, v_w_ukv, v_mla_norm_w, v_w_out, v_norm_ffn2, v_ffn2_w_gate, v_ffn2_w_up, v_ffn2_w_down, v_norm_final):
    a = dict(locals())
    held_transposed = ("ffn1_w_gate", "ffn1_w_up", "ffn2_w_gate", "ffn2_w_up", "w_in")
    for n in held_transposed:
        for p in ("", "m_", "v_"):
            a[p + n] = a[p + n].transpose(0, 2, 1)
    B, S, D = x.shape
    mx, my, mc = _place()
    chip = 2 * mx + my
    dev = 2 * chip + mc
    core = mc.astype(jnp.int32).reshape(1)
    chip_id = chip.astype(jnp.int32).reshape(1)

    cw_rows = jnp.pad(conv_w[0], ((0, 0), (0, D - conv_w.shape[2])))
    got, _ = _all_gather_small(jnp.concatenate([c, cw_rows, jnp.zeros((8 - B - CONV_WIDTH, D), F32)], axis=0), "gather_c")
    c_all = got[:, :B, :].reshape(N_DEV * B, D)
    conv_full = got[::2, B:B + CONV_WIDTH, :conv_w.shape[2]].transpose(1, 0, 2).reshape(CONV_WIDTH, D_CONV)

    b_cols = lax.dynamic_slice(b_ada, (0, chip * ADA_COLS), (1, ADA_COLS))
    mod_all, mod_done = _all_gather_small(_ada_fwd(c_all, w_ada[0], b_cols), "gather_mod")
    mod = lax.dynamic_slice(mod_all, (0, B * dev, 0), (N_DEV, B, ADA_COLS))[::2].transpose(1, 0, 2).reshape(B, N_MOD * D)

    first = ("ffn1_w_gate", "ffn1_w_up", "ffn1_w_down")
    later = tuple(n for n in BIG_NAMES if n not in first)
    got_first, gathered = _gather_weights([(a[n][0] + mod_done[0, 0]).astype(BF16) for n in first])
    w = dict(zip(first, got_first))
    in_flight = _gather_start([(a[n][0] + gathered[0, 0]).astype(BF16) for n in later])

    def later_weights(after):
        send_sems, recv_sems, shards, lands, _ = in_flight
        lands = _gather_wait(send_sems, recv_sems, shards, lands, after)
        wl = dict(zip(later, _gather_finish([a[n][0].astype(BF16) for n in later], lands)))
        for n, to_kernel in _TO_KERNEL.items():
            wl[n] = to_kernel(wl[n].reshape(-1, D) if n in held_transposed else _columns_joined(wl[n]))
        wl["w_out"] = wl["w_out"].reshape(D_SSD + D_MLA, D)
        return wl

    small = {n: a[n].reshape(1, -1) for n in SMALL_NAMES if n not in ("conv_w", "norm_final")}
    small["conv_w"], small["norm_final"] = conv_full, norm_final

    def scatter_group(names, gw, after):
        g4 = []
        for n in names:
            g = gw[n]
            if n in _FROM_KERNEL:
                g = _FROM_KERNEL[n](g) if n in held_transposed else _columns_split(_FROM_KERNEL[n](g))
            g4.append(g.reshape(N_CHIPS, a[n].shape[1], a[n].shape[2]))
        swapped = _swap_halves(g4, g4[0] if after is None else after, "swap_" + names[0])
        pair = [_pair_sum(g, got, core, "pair_sum_" + n) for n, g, got in zip(names, g4, swapped)]
        return (names,) + tuple(_scatter_start(pair, names[0]))

    grads, deltas, new_m, new_v = {}, {}, {}, {}

    def finish_group(group, after):
        names, send_sems, recv_sems, pair, lands, _ = group
        pair, lands = _scatter_wait(send_sems, recv_sems, pair, lands, after, names[0])
        mine = [_chip_sum(own, got, chip_id, "chip_sum_" + n) for n, own, got in zip(names, pair, lands)]
        for n, own, other in zip(names, mine, _join_halves(mine, "join_" + names[0])):
            grads[n], deltas[n], new_m[n], new_v[n] = _adam_halves(a[n], a["m_" + n], a["v_" + n], own, other, core, "adam_" + n)
        return deltas[names[-1]]

    groups = []

    def on_grads(names, gw):
        groups.append(scatter_group(names, gw, None))
        return groups[-1][5][0, 0]

    loss_blk, grad_x, gw, dmod, gs = _local_step(x, positions, mod + in_flight[4][0, 0], w, later_weights, small, loss_target,
                                                 on_grads)

    small_flat = _pack_small(gs).at[-1].set(loss_blk[0, 0])
    send = jnp.concatenate([dmod.reshape(MOD_ROWS, D), small_flat.reshape(SMALL_ROWS, D),
                            jnp.zeros((SEND_ROWS - MOD_ROWS - SMALL_ROWS, D), F32)], axis=0)
    got, _ = _all_gather_small(send, "gather_small")
    summed = _small_sum(got)
    sums = summed[N_MOD:N_MOD + SMALL_ROWS].reshape(-1)
    loss = sums[-1]
    gsmall = _unpack_small(sums)
    gsmall["conv_w"] = lax.dynamic_slice(gsmall["conv_w"].reshape(CONV_WIDTH, D_CONV), (0, chip * conv_w.shape[2]),
                                         (CONV_WIDTH, conv_w.shape[2]))
    gsmall["b_ada"] = summed[:N_MOD]
    names = ("b_ada",) + SMALL_NAMES
    rows = 208

    def pack(parts):
        flat = jnp.concatenate([parts[n].reshape(-1) for n in names])
        return jnp.pad(flat, (0, rows * LANES - flat.shape[0])).reshape(rows, LANES)

    packed = [pack({n: a[p + n] for n in names}) for p in ("", "m_", "v_")]
    g_p = pack(gsmall)
    outs = (g_p,) + tuple(_adam(packed[0], g_p, packed[1], packed[2], "adam_small"))
    for dst, flat in zip((grads, deltas, new_m, new_v), outs):
        flat, off = flat.reshape(-1), 0
        for n in names:
            dst[n] = flat[off:off + a[n].size].reshape(a[n].shape)
            off += a[n].size

    dmod_all = got[:, :MOD_ROWS, :].reshape(N_DEV * B, N_MOD * D)
    dmod_cols = lax.dynamic_slice(dmod_all, (0, chip * ADA_COLS), (N_DEV * B, ADA_COLS))
    ada = _ada_bwd(c_all, dmod_cols, w_ada[0], m_w_ada[0], v_w_ada[0])
    for dst, t in zip((grads, deltas, new_m, new_v), ada):
        dst["w_ada"] = t[None]

    last = scatter_group(first, gw, summed)
    after = last[5]
    for group in groups:
        after = finish_group(group, after)
    finish_group(last, after)
    for dst in (grads, deltas, new_m, new_v):
        for n in held_transposed:
            dst[n] = dst[n].transpose(0, 2, 1)

    order = ("w_ada", "b_ada", "norm_ffn1", "ffn1_w_gate", "ffn1_w_up", "ffn1_w_down", "norm_mix", "w_in", "conv_w", "conv_b",
             "dt_bias", "a_log", "d_skip", "ssd_norm_w", "q_norm_w", "w_uq", "kv_norm_w", "w_ukv", "mla_norm_w", "w_out",
             "norm_ffn2", "ffn2_w_gate", "ffn2_w_up", "ffn2_w_down", "norm_final")
    return (loss, grad_x, *[grads[n] for n in order], *[deltas[n] for n in order], *[new_m[n] for n in order],
            *[new_v[n] for n in order])
```

```python
import functools
import math

import jax
import jax.numpy as jnp
import numpy as np
from jax import lax
from jax.experimental import pallas as pl
from jax.experimental.pallas import tpu as pltpu

F32 = jnp.float32
BF16 = jnp.bfloat16
HIGHEST = lax.Precision.HIGHEST

D_MODEL = 1024
D_FF = 2816
D_SSD = 1024
D_MLA = 1024
SSD_HEADS = 16
SSD_HEAD_DIM = 64
SSD_GROUPS = 2
SSD_STATE = 128
CONV_WIDTH = 4
CHUNK = 128
MLA_HEADS = 8
QK_NOPE = 64
QK_ROPE = 32
QK_DIM = QK_NOPE + QK_ROPE
V_HEAD = 128
Q_LORA = 384
KV_LORA = 256
ROPE_THETA = 10000.0
N_MOD = 9
EPS = 1e-6
D_CONV = D_SSD + 2 * SSD_GROUPS * SSD_STATE
D_PROJ = 3328
HEAD_LANES = 128
ADAM_LR = 0.001
ADAM_B1 = 0.9
ADAM_B2 = 0.999
ADAM_EPS = 1e-08
ADAM_WD = 0.01
ADAM_STEP = 10

LANES = 128
VMEM_LIMIT = 56 * 1024 * 1024
TOKEN_TILE = 512
WIDE_TOKEN_TILE = 1024
ATTN_FWD_Q_TILE = 1024
ATTN_FWD_KV_TILE = 1024
ATTN_BWD_TILE = 1024
N_CHIPS = 4
N_DEV = 8

MESH = pl.DeviceIdType.MESH


def _dot(a, b, precision=None):
    return jnp.dot(a, b, preferred_element_type=F32, precision=precision)


def _dot_nt(a, b, precision=None):
    return lax.dot_general(a, b, (((1,), (1,)), ((), ())), preferred_element_type=F32, precision=precision)


def _dot_tn(a, b, precision=None):
    return lax.dot_general(a, b, (((0,), (0,)), ((), ())), preferred_element_type=F32, precision=precision)


def _cparams(semantics):
    return pltpu.CompilerParams(dimension_semantics=semantics, vmem_limit_bytes=VMEM_LIMIT)


def _resident(shape):
    zeros = (0,) * len(shape)
    return pl.BlockSpec(shape, lambda *_: zeros, pipeline_mode=pl.Buffered(1))


def _sigmoid(x):
    return jax.nn.sigmoid(x)


def _rms_stats(x):
    r = lax.rsqrt(jnp.mean(x * x, axis=-1, keepdims=True) + EPS)
    return x * r, r


def _rms_bwd(dn, xh, r, w):
    dxh = dn * w
    dx = r * (dxh - xh * jnp.mean(dxh * xh, axis=-1, keepdims=True))
    return dx, dn * xh


def _colsum(v):
    return jnp.sum(v, axis=0, keepdims=True)


def _ffn_fwd(x, nw, sh, sc, g, wg, wu, wd, seq, name, head=None):
    T, D = x.shape
    fs = wg.shape[1]
    tm = min(TOKEN_TILE, seq)
    tps = seq // tm

    def body(x_ref, nw_ref, sh_ref, sc_ref, g_ref, wg_ref, wu_ref, wd_ref, *rest):
        if head is None:
            xo_ref, a_ref, u_ref, f_ref = rest
        else:
            nf_ref, t_ref, xo_ref, a_ref, u_ref, f_ref, loss_ref, dnf_ref = rest

            @pl.when(pl.program_id(0) == 0)
            def _():
                loss_ref[...] = jnp.zeros_like(loss_ref)
                dnf_ref[...] = jnp.zeros_like(dnf_ref)

        xv = x_ref[...]
        xh, _ = _rms_stats(xv)
        h = (xh * nw_ref[...]) * (1.0 + sc_ref[0]) + sh_ref[0]
        hb = h.astype(BF16)
        f = jnp.zeros((tm, D), F32)
        for j in range(N_CHIPS):
            a = _dot_nt(hb, wg_ref[j])
            u = _dot_nt(hb, wu_ref[j])
            a_ref[j] = a.astype(BF16)
            u_ref[j] = u.astype(BF16)
            f = f + _dot((a * _sigmoid(a) * u).astype(BF16), wd_ref[j])
        f_ref[...] = f.astype(BF16)
        xo = xv + 0.5 * g_ref[0] * f
        if head is None:
            xo_ref[...] = xo
        else:
            xh, r = _rms_stats(xo)
            nfv = nf_ref[...]
            err = xh * nfv - t_ref[...]
            loss_ref[...] += (0.5 / D) * jnp.sum(err * err)
            dxo, dw_rows = _rms_bwd(err * (1.0 / D), xh, r, nfv)
            xo_ref[...] = dxo
            dnf_ref[...] += _colsum(dw_rows)

    rows = lambda n: pl.BlockSpec((tm, n), lambda i: (i, 0))
    act = pl.BlockSpec((N_CHIPS, tm, fs), lambda i: (0, i, 0))
    perb = pl.BlockSpec((1, 1, D), lambda i: (i // tps, 0, 0))
    sd = jax.ShapeDtypeStruct
    in_specs = [rows(D), _resident((1, D)), perb, perb, perb, _resident((N_CHIPS, fs, D)), _resident((N_CHIPS, fs, D)),
                _resident((N_CHIPS, fs, D))]
    out_specs = [rows(D), act, act, rows(D)]
    out_shape = [sd((T, D), F32), sd((N_CHIPS, T, fs), BF16), sd((N_CHIPS, T, fs), BF16), sd((T, D), BF16)]
    if head is not None:
        in_specs += [_resident((1, D)), rows(D)]
        out_specs += [pl.BlockSpec((8, LANES), lambda i: (0, 0)), pl.BlockSpec((1, D), lambda i: (0, 0))]
        out_shape += [sd((8, LANES), F32), sd((1, D), F32)]
    return pl.pallas_call(
        body, grid=(T // tm,), name=name, in_specs=in_specs, out_specs=out_specs, out_shape=out_shape,
        compiler_params=_cparams(("arbitrary",)),
    )(x, nw, sh, sc, g, wg, wu, wd, *(head or ()))


def _ffn_bwd(dxo, x, nw, sh, sc, g, a, u, f, wg, wu, wd, seq, name):
    T, D = x.shape
    fs = wg.shape[1]
    B = T // seq
    tm = min(TOKEN_TILE // 2, seq)
    tps = seq // tm

    def body(dxo_ref, x_ref, nw_ref, sh_ref, sc_ref, g_ref, a_ref, u_ref, f_ref, wg_ref, wu_ref, wd_ref,
             dx_ref, h_ref, s_ref, df_ref, da_ref, du_ref, dsh_ref, dsc_ref, dg_ref, dnw_ref):
        i = pl.program_id(0)

        @pl.when(i % tps == 0)
        def _():
            dsh_ref[...] = jnp.zeros_like(dsh_ref)
            dsc_ref[...] = jnp.zeros_like(dsc_ref)
            dg_ref[...] = jnp.zeros_like(dg_ref)

        @pl.when(i == 0)
        def _():
            dnw_ref[...] = jnp.zeros_like(dnw_ref)

        dxo_v = dxo_ref[...]
        dfb = (0.5 * g_ref[0] * dxo_v).astype(BF16)
        dg_ref[0] += _colsum(0.5 * dxo_v * f_ref[...].astype(F32))
        dh = jnp.zeros((tm, D), F32)
        for j in range(N_CHIPS):
            ds = _dot_nt(dfb, wd_ref[j])
            av = a_ref[j].astype(F32)
            uv = u_ref[j].astype(F32)
            sig = _sigmoid(av)
            sil = av * sig
            dab = (ds * uv * (sig * (1.0 + av * (1.0 - sig)))).astype(BF16)
            dub = (ds * sil).astype(BF16)
            dh = dh + _dot(dab, wg_ref[j]) + _dot(dub, wu_ref[j])
            s_ref[j] = (sil * uv).astype(BF16)
            da_ref[j] = dab
            du_ref[j] = dub
        xv = x_ref[...]
        xh, r = _rms_stats(xv)
        nwv = nw_ref[...]
        n = xh * nwv
        scale1 = 1.0 + sc_ref[0]
        dsc_ref[0] += _colsum(dh * n)
        dsh_ref[0] += _colsum(dh)
        dx, dw_rows = _rms_bwd(dh * scale1, xh, r, nwv)
        dnw_ref[...] += _colsum(dw_rows)
        dx_ref[...] = dxo_v + dx
        h_ref[...] = (n * scale1 + sh_ref[0]).astype(BF16)
        df_ref[...] = dfb

    rows = lambda n: pl.BlockSpec((tm, n), lambda i: (i, 0))
    act = pl.BlockSpec((N_CHIPS, tm, fs), lambda i: (0, i, 0))
    perb = pl.BlockSpec((1, 1, D), lambda i: (i // tps, 0, 0))
    sd = jax.ShapeDtypeStruct
    return pl.pallas_call(
        body, grid=(T // tm,), name=name,
        in_specs=[rows(D), rows(D), _resident((1, D)), perb, perb, perb, act, act, rows(D),
                  _resident((N_CHIPS, fs, D)), _resident((N_CHIPS, fs, D)), _resident((N_CHIPS, fs, D))],
        out_specs=[rows(D), rows(D), act, rows(D), act, act, perb, perb, perb, pl.BlockSpec((1, D), lambda i: (0, 0))],
        out_shape=[sd((T, D), F32), sd((T, D), BF16), sd((N_CHIPS, T, fs), BF16), sd((T, D), BF16),
                   sd((N_CHIPS, T, fs), BF16), sd((N_CHIPS, T, fs), BF16), sd((B, 1, D), F32), sd((B, 1, D), F32),
                   sd((B, 1, D), F32), sd((1, D), F32)],
        compiler_params=_cparams(("arbitrary",)),
    )(dxo, x, nw, sh, sc, g, a, u, f, wg, wu, wd)


def _ffn_wgrad(h, s, df, da, du, name):
    T, D = h.shape
    fs = s.shape[2]
    tt = min(TOKEN_TILE, T)
    nt = T // tt

    def body(h_ref, s_ref, df_ref, da_ref, du_ref, dgate_ref, dup_ref, ddown_ref, gate_acc, up_acc, down_acc):
        @pl.when(pl.program_id(1) == 0)
        def _():
            gate_acc[...] = jnp.zeros_like(gate_acc)
            up_acc[...] = jnp.zeros_like(up_acc)
            down_acc[...] = jnp.zeros_like(down_acc)

        hv = h_ref[...]
        gate_acc[...] += _dot_tn(da_ref[0], hv)
        up_acc[...] += _dot_tn(du_ref[0], hv)
        down_acc[...] += _dot_tn(s_ref[0], df_ref[...])

        @pl.when(pl.program_id(1) == nt - 1)
        def _():
            dgate_ref[0] = gate_acc[...].astype(BF16)
            dup_ref[0] = up_acc[...].astype(BF16)
            ddown_ref[0] = down_acc[...].astype(BF16)

    rows = pl.BlockSpec((tt, D), lambda j, t: (t, 0))
    act = pl.BlockSpec((1, tt, fs), lambda j, t: (j, t, 0))
    shard = pl.BlockSpec((1, fs, D), lambda j, t: (j, 0, 0))
    return pl.pallas_call(
        body, grid=(N_CHIPS, nt), name=name,
        in_specs=[rows, act, rows, act, act],
        out_specs=[shard] * 3, out_shape=[jax.ShapeDtypeStruct((N_CHIPS, fs, D), BF16)] * 3,
        scratch_shapes=[pltpu.VMEM((fs, D), F32)] * 3,
        compiler_params=_cparams(("arbitrary", "arbitrary")),
    )(h, s, df, da, du)


def _mm_tn(xa, ya, tn, name):
    T, K = xa.shape
    N = ya.shape[1]
    tt = min(WIDE_TOKEN_TILE, T)

    def body(x_ref, y_ref, o_ref):
        @pl.when(pl.program_id(1) == 0)
        def _():
            o_ref[...] = jnp.zeros_like(o_ref)

        o_ref[...] += _dot_tn(x_ref[...], y_ref[...])

    return pl.pallas_call(
        body, grid=(N // tn, T // tt), name=name,
        in_specs=[pl.BlockSpec((tt, K), lambda j, t: (t, 0)), pl.BlockSpec((tt, tn), lambda j, t: (t, j))],
        out_specs=pl.BlockSpec((K, tn), lambda j, t: (0, j)),
        out_shape=jax.ShapeDtypeStruct((K, N), F32),
        compiler_params=_cparams(("arbitrary", "arbitrary")),
    )(xa, ya)


_PROJ_SPLITS = (0, 1024, 2560, 2944, 3200, 3328)


def _inproj_fwd(x, nw, sh, sc, win, seq):
    T, D = x.shape
    tm = min(WIDE_TOKEN_TILE, seq)
    tps = seq // tm
    widths = [b - a for a, b in zip(_PROJ_SPLITS[:-1], _PROJ_SPLITS[1:])]
    dtypes = [BF16, BF16, F32, F32, F32]

    def body(x_ref, nw_ref, sh_ref, sc_ref, w_ref, *outs):
        xh, _ = _rms_stats(x_ref[...])
        h = (xh * nw_ref[...]) * (1.0 + sc_ref[0]) + sh_ref[0]
        proj = _dot_nt(h.astype(BF16), w_ref[...])
        for o, lo, hi in zip(outs, _PROJ_SPLITS[:-1], _PROJ_SPLITS[1:]):
            o[...] = proj[:, lo:hi].astype(o.dtype)

    rows = lambda n: pl.BlockSpec((tm, n), lambda i: (i, 0))
    perb = pl.BlockSpec((1, 1, D), lambda i: (i // tps, 0, 0))
    return pl.pallas_call(
        body, grid=(T // tm,), name="inproj_fwd",
        in_specs=[rows(D), _resident((1, D)), perb, perb, _resident((D_PROJ, D))],
        out_specs=[rows(w) for w in widths],
        out_shape=[jax.ShapeDtypeStruct((T, w), dt) for w, dt in zip(widths, dtypes)],
        compiler_params=_cparams(("arbitrary",)),
    )(x, nw, sh, sc, win)


def _inproj_bwd(dx2, x, nw, sh, sc, win, dz, dxbc, dcq, dckv, ddtk_a, ddtk_b, seq):
    T, D = x.shape
    B = T // seq
    tm = min(TOKEN_TILE, seq)
    tps = seq // tm

    def body(dx2_ref, x_ref, nw_ref, sh_ref, sc_ref, w_ref, dz_ref, dxbc_ref, dcq_ref, dckv_ref, da_ref, db_ref,
             dx_ref, h_ref, dp_ref, dsh_ref, dsc_ref, dnw_ref):
        i = pl.program_id(0)

        @pl.when(i % tps == 0)
        def _():
            dsh_ref[...] = jnp.zeros_like(dsh_ref)
            dsc_ref[...] = jnp.zeros_like(dsc_ref)

        @pl.when(i == 0)
        def _():
            dnw_ref[...] = jnp.zeros_like(dnw_ref)

        dproj = jnp.concatenate(
            [dz_ref[...], dxbc_ref[...], dcq_ref[...].astype(BF16), dckv_ref[...].astype(BF16),
             (da_ref[...] + db_ref[...]).astype(BF16)], axis=1)
        dp_ref[...] = dproj
        dh = _dot(dproj, w_ref[...])
        xh, r = _rms_stats(x_ref[...])
        nwv = nw_ref[...]
        n = xh * nwv
        scale1 = 1.0 + sc_ref[0]
        dsc_ref[0] += _colsum(dh * n)
        dsh_ref[0] += _colsum(dh)
        dx, dw_rows = _rms_bwd(dh * scale1, xh, r, nwv)
        dnw_ref[...] += _colsum(dw_rows)
        dx_ref[...] = dx2_ref[...] + dx
        h_ref[...] = (n * scale1 + sh_ref[0]).astype(BF16)

    rows = lambda n: pl.BlockSpec((tm, n), lambda i: (i, 0))
    perb = pl.BlockSpec((1, 1, D), lambda i: (i // tps, 0, 0))
    sd = jax.ShapeDtypeStruct
    return pl.pallas_call(
        body, grid=(T // tm,), name="inproj_bwd",
        in_specs=[rows(D), rows(D), _resident((1, D)), perb, perb, _resident((D_PROJ, D)),
                  rows(1024), rows(D_CONV), rows(Q_LORA), rows(KV_LORA), rows(LANES), rows(LANES)],
        out_specs=[rows(D), rows(D), rows(D_PROJ), perb, perb, pl.BlockSpec((1, D), lambda i: (0, 0))],
        out_shape=[sd((T, D), F32), sd((T, D), BF16), sd((T, D_PROJ), BF16), sd((B, 1, D), F32), sd((B, 1, D), F32),
                   sd((1, D), F32)],
        compiler_params=_cparams(("arbitrary",)),
    )(dx2, x, nw, sh, sc, win, dz, dxbc, dcq, dckv, ddtk_a, ddtk_b)


SUBLANES = 8


def _shift_down(v, k):
    r = pltpu.roll(v, k, 0)
    row = lax.broadcasted_iota(jnp.int32, (SUBLANES, v.shape[1]), 0)
    return jnp.concatenate([jnp.where(row < k, 0.0, r[:SUBLANES]), r[SUBLANES:]], axis=0)


def _shift_up(v, k):
    n = v.shape[0]
    r = pltpu.roll(v, n - k, 0)
    row = lax.broadcasted_iota(jnp.int32, (SUBLANES, v.shape[1]), 0)
    return jnp.concatenate([r[:n - SUBLANES], jnp.where(row >= SUBLANES - k, 0.0, r[n - SUBLANES:])], axis=0)


def _conv_pre(xv, w_ref, b_ref):
    pre = b_ref[...] + w_ref[CONV_WIDTH - 1:CONV_WIDTH, :] * xv
    for k in range(1, CONV_WIDTH):
        pre = pre + w_ref[CONV_WIDTH - 1 - k:CONV_WIDTH - k, :] * _shift_down(xv, k)
    return pre


def _conv_fwd(xraw, cw, cb):
    B, S, C = xraw.shape

    def body(x_ref, w_ref, b_ref, o_ref):
        pre = _conv_pre(x_ref[0].astype(F32), w_ref, b_ref)
        o_ref[0] = (pre * _sigmoid(pre)).astype(BF16)

    blk = pl.BlockSpec((1, S, LANES), lambda b, j: (b, 0, j))
    return pl.pallas_call(
        body, grid=(B, C // LANES), name="conv_fwd",
        in_specs=[blk, pl.BlockSpec((CONV_WIDTH, LANES), lambda b, j: (0, j)), pl.BlockSpec((1, LANES), lambda b, j: (0, j))],
        out_specs=blk, out_shape=jax.ShapeDtypeStruct((B, S, C), BF16),
        compiler_params=_cparams(("arbitrary", "arbitrary")),
    )(xraw, cw, cb)


def _conv_bwd(dout, xraw, cw, cb):
    B, S, C = xraw.shape

    def body(d_ref, x_ref, w_ref, b_ref, dx_ref, dw_ref, db_ref):
        @pl.when(pl.program_id(1) == 0)
        def _():
            dw_ref[...] = jnp.zeros_like(dw_ref)
            db_ref[...] = jnp.zeros_like(db_ref)

        xv = x_ref[0].astype(F32)
        pre = _conv_pre(xv, w_ref, b_ref)
        sig = _sigmoid(pre)
        dpre = d_ref[0].astype(F32) * (sig * (1.0 + pre * (1.0 - sig)))
        dx = w_ref[CONV_WIDTH - 1:CONV_WIDTH, :] * dpre
        for k in range(1, CONV_WIDTH):
            dx = dx + w_ref[CONV_WIDTH - 1 - k:CONV_WIDTH - k, :] * _shift_up(dpre, k)
        dx_ref[0] = dx.astype(BF16)
        db_ref[...] += _colsum(dpre)
        dws = [_colsum(dpre * (xv if k == 0 else _shift_down(xv, k))) for k in range(CONV_WIDTH - 1, -1, -1)]
        dw_ref[...] += jnp.concatenate(dws, axis=0)

    blk = pl.BlockSpec((1, S, LANES), lambda j, b: (b, 0, j))
    wspec = pl.BlockSpec((CONV_WIDTH, LANES), lambda j, b: (0, j))
    bspec = pl.BlockSpec((1, LANES), lambda j, b: (0, j))
    return pl.pallas_call(
        body, grid=(C // LANES, B), name="conv_bwd",
        in_specs=[blk, blk, wspec, bspec], out_specs=[blk, wspec, bspec],
        out_shape=[jax.ShapeDtypeStruct((B, S, C), BF16), jax.ShapeDtypeStruct((CONV_WIDTH, C), F32),
                   jax.ShapeDtypeStruct((1, C), F32)],
        compiler_params=_cparams(("arbitrary", "arbitrary")),
    )(dout, xraw, cw, cb)


def _softplus(x):
    return jnp.maximum(x, 0.0) + jnp.log(1.0 + jnp.exp(-jnp.abs(x)))


def _ssd_common(xbc_ref, dtk_ref, dtb_ref, alog_ref, e_ref):
    L = CHUNK
    xbc = xbc_ref[0]
    xs = xbc[:, :D_SSD].astype(F32)
    bm = xbc[:, D_SSD:D_SSD + 256]
    cm = xbc[:, D_SSD + 256:D_SSD + 512]
    head = lax.broadcasted_iota(jnp.int32, (1, LANES), 1) < SSD_HEADS
    a128 = jnp.where(head, -jnp.exp(alog_ref[...]), 0.0)
    pre = dtk_ref[0] + dtb_ref[...]
    dt = _softplus(pre)
    dA = dt * a128
    row = lax.broadcasted_iota(jnp.int32, (L, L), 0)
    col = lax.broadcasted_iota(jnp.int32, (L, L), 1)
    causal = col <= row
    tri = causal.astype(F32)
    triT = (row <= col).astype(F32)
    tri = causal.astype(BF16)
    triT = (row <= col).astype(BF16)
    dA3 = _split3(dA)
    acum = _sum3(lambda part: _dot(tri, part), dA3)
    acumT = _sum3(lambda part: _dot_tn(part, triT), dA3)
    E = e_ref[...]
    acum_f = _spread(acum, E)
    dt_f = _spread(dt, E)
    e_f = jnp.exp(acum_f)
    w_f = jnp.exp(acum_f[L - 1:L, :] - acum_f)
    xt = xs * dt_f
    return dict(xs=xs, bm=bm, cm=cm, a128=a128, pre=pre, dt=dt, causal=causal, tri=tri, triT=triT, acum=acum,
                acumT=acumT, E=E, dt_f=dt_f, e_f=e_f, w_f=w_f, xt=xt, head=head)


def _split3(x):
    p1 = x.astype(BF16)
    r1 = x - p1.astype(F32)
    p2 = r1.astype(BF16)
    return p1, p2, (r1 - p2.astype(F32)).astype(BF16)


def _sum3(mm, parts):
    return (mm(parts[0]) + mm(parts[1])) + mm(parts[2])


def _spread(v, e):
    return _sum3(lambda part: _dot(part, e), _split3(v))


def _gather_heads(v, e):
    return _sum3(lambda part: _dot_nt(part, e), _split3(v))


def _head_mask(k):
    lane = lax.broadcasted_iota(jnp.int32, (CHUNK, LANES), 1)
    return (lane >= SSD_HEAD_DIM) if k == 1 else (lane < SSD_HEAD_DIM)


def _pair_decay(alast, h0):
    row = lax.broadcasted_iota(jnp.int32, (2 * SSD_HEAD_DIM, SSD_STATE), 0)
    return jnp.exp(jnp.where(row < SSD_HEAD_DIM, alast[:, h0:h0 + 1], alast[:, h0 + 1:h0 + 2]))


def _decay_matrix(q, h):
    seg = q["acum"][:, h:h + 1] - q["acumT"][h:h + 1, :]
    return jnp.exp(jnp.where(q["causal"], seg, -1e30))


def _gated_norm(y, zz, nw):
    sig = _sigmoid(zz)
    sil = zz * sig
    yg = y * sil
    half = D_SSD // SSD_GROUPS
    parts = []
    for g in range(SSD_GROUPS):
        xh, r = _rms_stats(yg[:, g * half:(g + 1) * half])
        parts.append((xh, r))
    return sig, sil, parts


def _ssd_fwd(xbc, dtk, z, dtb, alog, dsk, nw, expand):
    B, S, _ = xbc.shape
    L = CHUNK
    nc = S // L

    def body(xbc_ref, dtk_ref, z_ref, dtb_ref, alog_ref, dsk_ref, nw_ref, e_ref, y_ref, ys_ref, prev_ref, st_ref):
        @pl.when(pl.program_id(1) == 0)
        def _():
            st_ref[...] = jnp.zeros_like(st_ref)

        q = _ssd_common(xbc_ref, dtk_ref, dtb_ref, alog_ref, e_ref)
        xtb = q["xt"].astype(BF16)
        xwb = (q["xt"] * q["w_f"]).astype(BF16)
        alast = q["acum"][L - 1:L, :]
        ys = []
        for g in range(SSD_GROUPS):
            bg = q["bm"][:, g * 128:(g + 1) * 128]
            cg = q["cm"][:, g * 128:(g + 1) * 128]
            G = _dot_nt(cg, bg)
            for pr in range(SSD_HEADS // SSD_GROUPS // 2):
                h0 = g * 8 + 2 * pr
                lo = h0 * SSD_HEAD_DIM
                xt_p = xtb[:, lo:lo + 128]
                ydiag = jnp.zeros((L, LANES), F32)
                for k in range(2):
                    M = (G * _decay_matrix(q, h0 + k)).astype(BF16)
                    ydiag = ydiag + _dot(M, jnp.where(_head_mask(k), xt_p, jnp.zeros_like(xt_p)))
                hp = st_ref[lo:lo + 128, :]
                prev_ref[0, 0, lo:lo + 128, :] = hp.astype(BF16)
                zoff = _dot_nt(cg, hp.astype(BF16))
                ys.append(ydiag + zoff * q["e_f"][:, lo:lo + 128])
                st_ref[lo:lo + 128, :] = _pair_decay(alast, h0) * hp + _dot_tn(xwb[:, lo:lo + 128], bg)
        y = jnp.concatenate(ys, axis=1) + dsk_ref[...] * q["xs"]
        y_ref[0] = y.astype(BF16)
        _, _, parts = _gated_norm(y, z_ref[0].astype(F32), nw_ref[...])
        half = D_SSD // SSD_GROUPS
        ys_ref[0] = jnp.concatenate(
            [xh * nw_ref[:, g * half:(g + 1) * half] for g, (xh, _) in enumerate(parts)], axis=1).astype(BF16)

    chunk = lambda n: pl.BlockSpec((1, L, n), lambda b, c: (b, c, 0))
    vec = pl.BlockSpec((1, LANES), lambda b, c: (0, 0))
    return pl.pallas_call(
        body, grid=(B, nc), name="ssd_fwd",
        in_specs=[chunk(D_CONV), chunk(LANES), chunk(D_SSD), vec, vec, pl.BlockSpec((1, D_SSD), lambda b, c: (0, 0)),
                  pl.BlockSpec((1, D_SSD), lambda b, c: (0, 0)), pl.BlockSpec((LANES, D_SSD), lambda b, c: (0, 0))],
        out_specs=[chunk(D_SSD), chunk(D_SSD), pl.BlockSpec((1, 1, D_SSD, SSD_STATE), lambda b, c: (b, c, 0, 0))],
        out_shape=[jax.ShapeDtypeStruct((B, S, D_SSD), BF16), jax.ShapeDtypeStruct((B, S, D_SSD), BF16),
                   jax.ShapeDtypeStruct((B, nc, D_SSD, SSD_STATE), BF16)],
        scratch_shapes=[pltpu.VMEM((D_SSD, SSD_STATE), F32)],
        compiler_params=_cparams(("arbitrary", "arbitrary")),
    )(xbc, dtk, z, dtb, alog, dsk, nw, expand)


def _ssd_bwd(xbc, dtk, z, y, prev, dys, dtb, alog, dsk, nw, expand):
    B, S, _ = xbc.shape
    L = CHUNK
    nc = S // L
    half = D_SSD // SSD_GROUPS

    def body(xbc_ref, dtk_ref, z_ref, y_ref, prev_ref, dys_ref, dtb_ref, alog_ref, dsk_ref, nw_ref, e_ref,
             dxbc_ref, ddtk_ref, dz_ref, dnw_ref, dvec_ref, dh_ref, dskc_ref):
        @pl.when((pl.program_id(0) == 0) & (pl.program_id(1) == 0))
        def _():
            dnw_ref[...] = jnp.zeros_like(dnw_ref)
            dvec_ref[...] = jnp.zeros_like(dvec_ref)
            dskc_ref[...] = jnp.zeros_like(dskc_ref)

        @pl.when(pl.program_id(1) == 0)
        def _():
            dh_ref[...] = jnp.zeros_like(dh_ref)

        q = _ssd_common(xbc_ref, dtk_ref, dtb_ref, alog_ref, e_ref)
        E = q["E"]
        xs = q["xs"]
        yv = y_ref[0].astype(F32)
        zz = z_ref[0].astype(F32)
        sig, sil, parts = _gated_norm(yv, zz, nw_ref[...])
        dn = dys_ref[0].astype(F32)
        dyg, dnw_rows = [], []
        for g, (xh, r) in enumerate(parts):
            dpart, dw_rows = _rms_bwd(dn[:, g * half:(g + 1) * half], xh, r, nw_ref[:, g * half:(g + 1) * half])
            dyg.append(dpart)
            dnw_rows.append(dw_rows)
        dyg = jnp.concatenate(dyg, axis=1)
        dnw_ref[...] += _colsum(jnp.concatenate(dnw_rows, axis=1))
        dY = dyg * sil
        dz_ref[0] = (dyg * yv * (sig * (1.0 + zz * (1.0 - sig)))).astype(BF16)
        dsk_f = dsk_ref[...]
        dskc_ref[...] += _colsum(dY * xs)
        dYb = dY.astype(BF16)
        xtb = q["xt"].astype(BF16)
        xwb = (q["xt"] * q["w_f"]).astype(BF16)
        acum = q["acum"]
        alast = acum[L - 1:L, :]
        lane_id = lax.broadcasted_iota(jnp.int32, (L, LANES), 1)
        sub_id = lax.broadcasted_iota(jnp.int32, (LANES, L), 0)
        lane_row = lax.broadcasted_iota(jnp.int32, (1, LANES), 1)
        da_rows = jnp.zeros((L, LANES), F32)
        daT = jnp.zeros((LANES, L), F32)
        dxt, prod_off, prod_st, dbs, dcs = [], [], [], [], []
        hsum_row = jnp.zeros((1, LANES), F32)
        for g in range(SSD_GROUPS):
            bg = q["bm"][:, g * 128:(g + 1) * 128]
            cg = q["cm"][:, g * 128:(g + 1) * 128]
            G = _dot_nt(cg, bg)
            dG = jnp.zeros((L, L), F32)
            dcg = jnp.zeros((L, SSD_STATE), F32)
            dbg = jnp.zeros((L, SSD_STATE), F32)
            for pr in range(SSD_HEADS // SSD_GROUPS // 2):
                h0 = g * 8 + 2 * pr
                lo = h0 * SSD_HEAD_DIM
                cols = slice(lo, lo + 128)
                dY_p = dYb[:, cols]
                xt_p = xtb[:, cols]
                dxt_p = jnp.zeros((L, LANES), F32)
                for k in range(2):
                    h = h0 + k
                    Lm = _decay_matrix(q, h)
                    Mf = G * Lm
                    dYk = jnp.where(_head_mask(k), dY_p, jnp.zeros_like(dY_p))
                    dM = _dot_nt(dYk, xt_p)
                    dxt_p = dxt_p + _dot_tn(Mf.astype(BF16), dYk)
                    dG = dG + dM * Lm
                    Q = dM * Mf
                    da_rows = da_rows + jnp.where(lane_id == h, jnp.sum(Q, axis=1, keepdims=True), 0.0)
                    daT = daT + jnp.where(sub_id == h, jnp.sum(Q, axis=0, keepdims=True), 0.0)
                hpb = prev_ref[0, 0, lo:lo + 128, :]
                hp = hpb.astype(F32)
                zoff = _dot_nt(cg, hpb)
                e_p = q["e_f"][:, cols]
                dY_pf = dY[:, cols]
                dZb = (dY_pf * e_p).astype(BF16)
                dcg = dcg + _dot(dZb, hpb)
                dhp_off = _dot_tn(dZb, cg)
                prod_off.append(dY_pf * zoff * e_p)
                dS = dh_ref[lo:lo + 128, :]
                dSb = dS.astype(BF16)
                U = _dot_nt(bg, dSb)
                dxt_p = dxt_p + U * q["w_f"][:, cols]
                dbg = dbg + _dot(xwb[:, cols], dSb)
                prod_st.append(q["xt"][:, cols] * U)
                dh_ref[lo:lo + 128, :] = _pair_decay(alast, h0) * dS + dhp_off
                dsh = dS * hp
                for k in range(2):
                    total = jnp.sum(dsh[k * SSD_HEAD_DIM:(k + 1) * SSD_HEAD_DIM, :], axis=(0, 1), keepdims=True)
                    hsum_row = hsum_row + jnp.where(lane_row == h0 + k, total, 0.0)
                dxt.append(dxt_p)
            dGb = dG.astype(BF16)
            dcs.append(dcg + _dot(dGb, bg))
            dbs.append(dbg + _dot_tn(dGb, cg))
        dxt = jnp.concatenate(dxt, axis=1)
        da_rows = da_rows + _gather_heads(jnp.concatenate(prod_off, axis=1), E)
        dww = _gather_heads(jnp.concatenate(prod_st, axis=1), E) * jnp.exp(alast - acum)
        da_rows = da_rows - dww
        dlast = _colsum(dww) + jnp.exp(alast) * hsum_row
        triT = q["triT"]
        ddA = (_sum3(lambda part: _dot(triT, part), _split3(da_rows))
               - _sum3(lambda part: _dot_nt(triT, part), _split3(daT)) + dlast)
        ddA = jnp.where(q["head"], ddA, 0.0)
        ddt = ddA * q["a128"] + _gather_heads(dxt * xs, E)
        ddt_raw = jnp.where(q["head"], ddt * _sigmoid(q["pre"]), 0.0)
        ddtk_ref[0] = ddt_raw
        dxs = dxt * q["dt_f"] + dsk_f * dY
        dxbc_ref[0] = jnp.concatenate([dxs] + dbs + dcs, axis=1).astype(BF16)
        dvec_ref[0:1, :] += _colsum(ddt_raw)
        dvec_ref[1:2, :] += _colsum(ddA * q["dt"]) * q["a128"]

        @pl.when((pl.program_id(0) == B - 1) & (pl.program_id(1) == nc - 1))
        def _():
            dvec_ref[2:3, :] = _gather_heads(jnp.broadcast_to(dskc_ref[...], (8, D_SSD)), E)[0:1, :]

    rev = lambda n: pl.BlockSpec((1, L, n), lambda b, c: (b, nc - 1 - c, 0))
    vec = pl.BlockSpec((1, LANES), lambda b, c: (0, 0))
    sd = jax.ShapeDtypeStruct
    return pl.pallas_call(
        body, grid=(B, nc), name="ssd_bwd",
        in_specs=[rev(D_CONV), rev(LANES), rev(D_SSD), rev(D_SSD),
                  pl.BlockSpec((1, 1, D_SSD, SSD_STATE), lambda b, c: (b, nc - 1 - c, 0, 0)), rev(D_SSD), vec, vec,
                  pl.BlockSpec((1, D_SSD), lambda b, c: (0, 0)),
                  pl.BlockSpec((1, D_SSD), lambda b, c: (0, 0)), pl.BlockSpec((LANES, D_SSD), lambda b, c: (0, 0))],
        out_specs=[rev(D_CONV), rev(LANES), rev(D_SSD), pl.BlockSpec((1, D_SSD), lambda b, c: (0, 0)),
                   pl.BlockSpec((8, LANES), lambda b, c: (0, 0))],
        out_shape=[sd((B, S, D_CONV), BF16), sd((B, S, LANES), F32), sd((B, S, D_SSD), BF16), sd((1, D_SSD), F32),
                   sd((8, LANES), F32)],
        scratch_shapes=[pltpu.VMEM((D_SSD, SSD_STATE), F32), pltpu.VMEM((1, D_SSD), F32)],
        compiler_params=_cparams(("arbitrary", "arbitrary")),
    )(xbc, dtk, z, y, prev, dys, dtb, alog, dsk, nw, expand)


def _rope_tables(pos_ref, invf_ref, place_ref):
    ang = invf_ref[...] * pos_ref[0].astype(F32)
    place = place_ref[...]
    cosf = 1.0 + _sum3(lambda part: _dot_tn(part, place), _split3(jnp.cos(ang) - 1.0))
    sinf = _sum3(lambda part: _dot_tn(part, place), _split3(jnp.sin(ang)))
    return cosf, sinf


def _rot(u):
    lane = lax.broadcasted_iota(jnp.int32, u.shape, 1)
    first = (lane >= QK_NOPE) & (lane < QK_NOPE + QK_ROPE // 2)
    second = (lane >= QK_NOPE + QK_ROPE // 2) & (lane < QK_DIM)
    return jnp.where(first, -pltpu.roll(u, LANES - QK_ROPE // 2, 1), jnp.where(second, pltpu.roll(u, QK_ROPE // 2, 1), 0.0))


def _rope_lanes(shape):
    lane = lax.broadcasted_iota(jnp.int32, shape, 1)
    return (lane >= QK_NOPE) & (lane < QK_DIM)


def _mla_prep(cq, ckv, dtk, pos, qw, kvw, wuq, wukv, invf, place):
    T = cq.shape[0]
    tm = min(WIDE_TOKEN_TILE, T)
    scale = 1.0 / math.sqrt(QK_DIM)
    HW = MLA_HEADS * HEAD_LANES

    def body(cq_ref, ckv_ref, dtk_ref, pos_ref, qw_ref, kvw_ref, wuq_ref, wukv_ref, invf_ref, place_ref, q_ref, k_ref, v_ref,
             cos_ref, sin_ref):
        xh, _ = _rms_stats(cq_ref[...])
        qv = _dot((xh * qw_ref[...]).astype(BF16), wuq_ref[...])
        xh, _ = _rms_stats(ckv_ref[...])
        kv = _dot((xh * kvw_ref[...]).astype(BF16), wukv_ref[...])
        cosf, sinf = _rope_tables(pos_ref, invf_ref, place_ref)
        cos_ref[...] = cosf
        sin_ref[...] = sinf
        rope = lambda u: u * cosf + _rot(u) * sinf
        dtkv = dtk_ref[...]
        kr = rope(jnp.where(_rope_lanes(dtkv.shape), dtkv, 0.0))
        for h in range(MLA_HEADS):
            cols = slice(h * HEAD_LANES, (h + 1) * HEAD_LANES)
            q_ref[:, cols] = (rope(qv[:, cols]) * scale).astype(BF16)
            k_ref[:, cols] = (kv[:, cols] + kr).astype(BF16)
        v_ref[...] = kv[:, HW:].astype(BF16)

    rows = lambda n: pl.BlockSpec((tm, n), lambda i: (i, 0))
    return pl.pallas_call(
        body, grid=(T // tm,), name="mla_prep",
        in_specs=[rows(Q_LORA), rows(KV_LORA), rows(LANES), pl.BlockSpec((1, 1, tm), lambda i: (i, 0, 0)),
                  _resident((1, Q_LORA)), _resident((1, KV_LORA)), _resident((Q_LORA, HW)), _resident((KV_LORA, 2 * HW)),
                  _resident((QK_ROPE // 2, 1)), _resident((QK_ROPE // 2, LANES))],
        out_specs=[rows(HW), rows(HW), rows(HW), rows(LANES), rows(LANES)],
        out_shape=[jax.ShapeDtypeStruct((T, HW), BF16)] * 3 + [jax.ShapeDtypeStruct((T, LANES), F32)] * 2,
        compiler_params=_cparams(("arbitrary",)),
    )(cq, ckv, dtk, pos.reshape(T // tm, 1, tm), qw, kvw, wuq, wukv, invf, place)


def _mla_prep_bwd(dq, dk, dv, cq, ckv, cos_t, sin_t, qw, kvw, wuq, wukv):
    T = cq.shape[0]
    tm = min(WIDE_TOKEN_TILE, T)
    scale = 1.0 / math.sqrt(QK_DIM)
    HW = MLA_HEADS * HEAD_LANES

    def body(dq_ref, dk_ref, dv_ref, cq_ref, ckv_ref, cos_ref, sin_ref, qw_ref, kvw_ref, wuq_ref, wukv_ref,
             dcq_ref, dckv_ref, ddtk_ref, qn_ref, kvn_ref, dqo_ref, dkvo_ref, dqw_ref, dkvw_ref):
        @pl.when(pl.program_id(0) == 0)
        def _():
            dqw_ref[...] = jnp.zeros_like(dqw_ref)
            dkvw_ref[...] = jnp.zeros_like(dkvw_ref)

        cosf, sinf = cos_ref[...], sin_ref[...]
        unrope = lambda d: d * cosf - _rot(d * sinf)
        dkr = jnp.zeros((tm, LANES), F32)
        nope = lax.broadcasted_iota(jnp.int32, (tm, LANES), 1) < QK_NOPE
        for h in range(MLA_HEADS):
            cols = slice(h * HEAD_LANES, (h + 1) * HEAD_LANES)
            dqo_ref[:, cols] = unrope(dq_ref[:, cols].astype(F32) * scale).astype(BF16)
            dkh = dk_ref[:, cols].astype(F32)
            dkr = dkr + jnp.where(_rope_lanes(dkh.shape), dkh, 0.0)
            dkvo_ref[:, cols] = jnp.where(nope, dkh, 0.0).astype(BF16)
        dkvo_ref[:, HW:] = dv_ref[...].astype(BF16)
        ddtk_ref[...] = unrope(dkr)
        xh, r = _rms_stats(cq_ref[...])
        qn_ref[...] = (xh * qw_ref[...]).astype(BF16)
        dx, dw_rows = _rms_bwd(_dot_nt(dqo_ref[...], wuq_ref[...]), xh, r, qw_ref[...])
        dcq_ref[...] = dx
        dqw_ref[...] += _colsum(dw_rows)
        xh, r = _rms_stats(ckv_ref[...])
        kvn_ref[...] = (xh * kvw_ref[...]).astype(BF16)
        dx, dw_rows = _rms_bwd(_dot_nt(dkvo_ref[...], wukv_ref[...]), xh, r, kvw_ref[...])
        dckv_ref[...] = dx
        dkvw_ref[...] += _colsum(dw_rows)

    rows = lambda n: pl.BlockSpec((tm, n), lambda i: (i, 0))
    sd = jax.ShapeDtypeStruct
    return pl.pallas_call(
        body, grid=(T // tm,), name="mla_prep_bwd",
        in_specs=[rows(HW), rows(HW), rows(HW), rows(Q_LORA), rows(KV_LORA), rows(LANES), rows(LANES), _resident((1, Q_LORA)),
                  _resident((1, KV_LORA)), _resident((Q_LORA, HW)), _resident((KV_LORA, 2 * HW))],
        out_specs=[rows(Q_LORA), rows(KV_LORA), rows(LANES), rows(Q_LORA), rows(KV_LORA), rows(HW), rows(2 * HW),
                   pl.BlockSpec((1, Q_LORA), lambda i: (0, 0)), pl.BlockSpec((1, KV_LORA), lambda i: (0, 0))],
        out_shape=[sd((T, Q_LORA), F32), sd((T, KV_LORA), F32), sd((T, LANES), F32), sd((T, Q_LORA), BF16),
                   sd((T, KV_LORA), BF16), sd((T, HW), BF16), sd((T, 2 * HW), BF16), sd((1, Q_LORA), F32),
                   sd((1, KV_LORA), F32)],
        compiler_params=_cparams(("arbitrary",)),
    )(dq, dk, dv, cq, ckv, cos_t, sin_t, qw, kvw, wuq, wukv)


def _causal_mask(t):
    row = lax.broadcasted_iota(jnp.int32, (t, t), 0)
    col = lax.broadcasted_iota(jnp.int32, (t, t), 1)
    return col <= row


def _attn_fwd(q, k, v):
    B, S, HW = q.shape
    H = HW // HEAD_LANES
    t = min(ATTN_FWD_Q_TILE, S)
    tk = min(ATTN_FWD_KV_TILE, t)
    nq = S // t
    per = t // tk

    pair = 2
    pw = pair * HEAD_LANES

    def body(q_ref, k_ref, v_ref, o_ref, lse_ref):
        qi = pl.program_id(2)
        lanes = [slice(hh * HEAD_LANES, (hh + 1) * HEAD_LANES) for hh in range(pair)]
        qs = [q_ref[0, :, cols] for cols in lanes]

        def step(j, carry, diag):
            sl = pl.ds(pl.multiple_of(j * tk, tk), tk)
            out = []
            for qv, cols, (m, l, acc) in zip(qs, lanes, carry):
                s = _dot_nt(qv, k_ref[0, sl, cols])
                if diag is not None:
                    row = lax.broadcasted_iota(jnp.int32, (t, tk), 0)
                    col = lax.broadcasted_iota(jnp.int32, (t, tk), 1)
                    s = jnp.where(col + diag * tk <= row, s, -1e30)
                m_new = jnp.maximum(m, jnp.max(s, axis=-1, keepdims=True))
                alpha = jnp.exp(m - m_new)
                p = jnp.exp(s - m_new)
                l = alpha * l + jnp.sum(p, axis=-1, keepdims=True)
                acc = alpha * acc + _dot(p.astype(BF16), v_ref[0, sl, cols])
                out.append((m_new, l, acc))
            return tuple(out)

        init = tuple((jnp.full((t, 1), -1e30, F32), jnp.zeros((t, 1), F32), jnp.zeros((t, HEAD_LANES), F32))
                     for _ in range(pair))
        carry = lax.fori_loop(0, qi * per, lambda j, c: step(j, c, None), init)
        for d in range(per):
            carry = step(qi * per + d, carry, d)
        for hh, (m, l, acc) in enumerate(carry):
            o_ref[0, :, lanes[hh]] = (acc / l).astype(BF16)
            lse_ref[0, hh] = m + jnp.log(l)

    return pl.pallas_call(
        body, grid=(B, H // pair, nq), name="attn_fwd",
        in_specs=[pl.BlockSpec((1, t, pw), lambda b, h, i: (b, i, h)),
                  pl.BlockSpec((1, S, pw), lambda b, h, i: (b, 0, h)),
                  pl.BlockSpec((1, S, pw), lambda b, h, i: (b, 0, h))],
        out_specs=[pl.BlockSpec((1, t, pw), lambda b, h, i: (b, i, h)),
                   pl.BlockSpec((1, pair, t, 1), lambda b, h, i: (b, h, i, 0))],
        out_shape=[jax.ShapeDtypeStruct((B, S, HW), BF16), jax.ShapeDtypeStruct((B, H, S, 1), F32)],
        compiler_params=_cparams(("arbitrary", "arbitrary", "arbitrary")),
    )(q, k, v)


def _attn_bwd(q, k, v, o, do, lse):
    B, S, HW = q.shape
    H = HW // HEAD_LANES
    t = min(ATTN_BWD_TILE, S)
    nq = S // t

    def body(q_ref, k_ref, v_ref, o_ref, do_ref, lse_ref, dq_out_ref, dk_ref, dv_ref, dq_ref):
        j = pl.program_id(2)

        @pl.when(j == 0)
        def _():
            dq_ref[...] = jnp.zeros_like(dq_ref)

        kj = k_ref[0]
        vj = v_ref[0]

        def step(i, carry, masked):
            dk, dv = carry
            sl = pl.ds(pl.multiple_of(i * t, t), t)
            qi = q_ref[0, sl, :]
            doi = do_ref[0, sl, :]
            s = _dot_nt(qi, kj)
            if masked:
                s = jnp.where(_causal_mask(t), s, -1e30)
            p = jnp.exp(s - lse_ref[0, 0, sl, :])
            dv = dv + _dot_tn(p.astype(BF16), doi)
            dp = _dot_nt(doi, vj)
            delta = jnp.sum(doi.astype(F32) * o_ref[0, sl, :].astype(F32), axis=-1, keepdims=True)
            dsb = (p * (dp - delta)).astype(BF16)
            dk = dk + _dot_tn(dsb, qi)
            dq_ref[sl, :] += _dot(dsb, kj)
            return dk, dv

        zero = jnp.zeros((t, HEAD_LANES), F32)
        carry = step(j, (zero, zero), True)
        dk, dv = lax.fori_loop(j + 1, nq, lambda i, c: step(i, c, False), carry)
        dk_ref[0] = dk.astype(BF16)
        dv_ref[0] = dv.astype(BF16)

        @pl.when(j == nq - 1)
        def _():
            dq_out_ref[0] = dq_ref[...].astype(BF16)

    full = pl.BlockSpec((1, S, HEAD_LANES), lambda b, h, j: (b, 0, h))
    tile = pl.BlockSpec((1, t, HEAD_LANES), lambda b, h, j: (b, j, h))
    sd = jax.ShapeDtypeStruct
    return pl.pallas_call(
        body, grid=(B, H, nq), name="attn_bwd",
        in_specs=[full, tile, tile, full, full, pl.BlockSpec((1, 1, S, 1), lambda b, h, j: (b, h, 0, 0))],
        out_specs=[full, tile, tile],
        out_shape=[sd((B, S, HW), BF16), sd((B, S, HW), BF16), sd((B, S, HW), BF16)],
        scratch_shapes=[pltpu.VMEM((S, HEAD_LANES), F32)],
        compiler_params=_cparams(("arbitrary", "arbitrary", "arbitrary")),
    )(q, k, v, o, do, lse)


def _mix_out(x1, yssd, o, mw, wout, g, seq):
    T, D = x1.shape
    tm = min(WIDE_TOKEN_TILE, seq)
    tps = seq // tm

    def body(x_ref, ys_ref, o_ref, mw_ref, w_ref, g_ref, xo_ref, m_ref, yc_ref):
        xh, _ = _rms_stats(o_ref[...].astype(F32))
        ycat = jnp.concatenate([ys_ref[...], (xh * mw_ref[...]).astype(BF16)], axis=1)
        m = _dot(ycat, w_ref[...])
        xo_ref[...] = x_ref[...] + g_ref[0] * m
        m_ref[...] = m.astype(BF16)
        yc_ref[...] = ycat

    rows = lambda n: pl.BlockSpec((tm, n), lambda i: (i, 0))
    perb = pl.BlockSpec((1, 1, D), lambda i: (i // tps, 0, 0))
    sd = jax.ShapeDtypeStruct
    return pl.pallas_call(
        body, grid=(T // tm,), name="mix_out",
        in_specs=[rows(D), rows(D_SSD), rows(D_MLA), _resident((1, D_MLA)), _resident((D_SSD + D_MLA, D)), perb],
        out_specs=[rows(D), rows(D), rows(D_SSD + D_MLA)],
        out_shape=[sd((T, D), F32), sd((T, D), BF16), sd((T, D_SSD + D_MLA), BF16)],
        compiler_params=_cparams(("arbitrary",)),
    )(x1, yssd, o, mw, wout, g)


def _mix_out_bwd(dx2, m, o, mw, wout, g, seq):
    T, D = dx2.shape
    B = T // seq
    tm = min(WIDE_TOKEN_TILE, seq)
    tps = seq // tm

    def body(dx_ref, m_ref, o_ref, mw_ref, w_ref, g_ref, dys_ref, do_ref, dm_ref, dg_ref, dmw_ref):
        i = pl.program_id(0)

        @pl.when(i % tps == 0)
        def _():
            dg_ref[...] = jnp.zeros_like(dg_ref)

        @pl.when(i == 0)
        def _():
            dmw_ref[...] = jnp.zeros_like(dmw_ref)

        dxv = dx_ref[...]
        dg_ref[0] += _colsum(dxv * m_ref[...].astype(F32))
        dmb = (g_ref[0] * dxv).astype(BF16)
        dm_ref[...] = dmb
        dycat = _dot_nt(dmb, w_ref[...])
        dys_ref[...] = dycat[:, :D_SSD].astype(BF16)
        xh, r = _rms_stats(o_ref[...].astype(F32))
        dx, dw_rows = _rms_bwd(dycat[:, D_SSD:], xh, r, mw_ref[...])
        do_ref[...] = dx.astype(BF16)
        dmw_ref[...] += _colsum(dw_rows)

    rows = lambda n: pl.BlockSpec((tm, n), lambda i: (i, 0))
    perb = pl.BlockSpec((1, 1, D), lambda i: (i // tps, 0, 0))
    sd = jax.ShapeDtypeStruct
    return pl.pallas_call(
        body, grid=(T // tm,), name="mix_out_bwd",
        in_specs=[rows(D), rows(D), rows(D_MLA), _resident((1, D_MLA)), _resident((D_SSD + D_MLA, D)), perb],
        out_specs=[rows(D_SSD), rows(D_MLA), rows(D), perb, pl.BlockSpec((1, D_MLA), lambda i: (0, 0))],
        out_shape=[sd((T, D_SSD), BF16), sd((T, D_MLA), BF16), sd((T, D), BF16), sd((B, 1, D), F32), sd((1, D_MLA), F32)],
        compiler_params=_cparams(("arbitrary",)),
    )(dx2, m, o, mw, wout, g)


def _win_to_kernel(w):
    z0 = jnp.zeros((48, w.shape[1]), w.dtype)
    z1 = jnp.zeros((32, w.shape[1]), w.dtype)
    return jnp.concatenate([w[:2560], w[2576:3216], w[2560:2576], z0, w[3216:3248], z1], axis=0)


def _win_from_kernel(g):
    return jnp.concatenate([g[:2560], g[3200:3216], g[2560:3200], g[3264:3296]], axis=0)


def _wuq_to_kernel(w):
    w = w.reshape(Q_LORA, MLA_HEADS, QK_DIM)
    return jnp.pad(w, ((0, 0), (0, 0), (0, HEAD_LANES - QK_DIM))).reshape(Q_LORA, MLA_HEADS * HEAD_LANES)


def _wuq_from_kernel(g):
    return g.reshape(Q_LORA, MLA_HEADS, HEAD_LANES)[:, :, :QK_DIM].reshape(Q_LORA, MLA_HEADS * QK_DIM)


def _wukv_to_kernel(w):
    w = w.reshape(KV_LORA, MLA_HEADS, QK_NOPE + V_HEAD)
    kp = jnp.pad(w[:, :, :QK_NOPE], ((0, 0), (0, 0), (0, HEAD_LANES - QK_NOPE)))
    return jnp.concatenate([kp.reshape(KV_LORA, -1), w[:, :, QK_NOPE:].reshape(KV_LORA, -1)], axis=1)


def _wukv_from_kernel(g):
    hw = MLA_HEADS * HEAD_LANES
    kp = g[:, :hw].reshape(KV_LORA, MLA_HEADS, HEAD_LANES)[:, :, :QK_NOPE]
    vp = g[:, hw:].reshape(KV_LORA, MLA_HEADS, V_HEAD)
    return jnp.concatenate([kp, vp], axis=2).reshape(KV_LORA, MLA_HEADS * (QK_NOPE + V_HEAD))


def _lanes16(v):
    return jnp.pad(v.reshape(1, SSD_HEADS), ((0, 0), (0, LANES - SSD_HEADS)))


def _constants():
    e = np.zeros((LANES, D_SSD), np.float32)
    for h in range(SSD_HEADS):
        e[h, h * SSD_HEAD_DIM:(h + 1) * SSD_HEAD_DIM] = 1.0
    inv_freq = ROPE_THETA ** (-jnp.arange(0, QK_ROPE, 2, dtype=F32) / QK_ROPE)
    half = QK_ROPE // 2
    place = np.zeros((half, LANES), np.float32)
    for j in range(half):
        place[j, QK_NOPE + j] = place[j, QK_NOPE + half + j] = 1.0
    return jnp.asarray(e, BF16), inv_freq.reshape(half, 1), jnp.asarray(place, BF16)


def _local_step(x, positions, mod, w, later_weights, small, tgt, on_grads):
    B, S, D = x.shape
    T = B * S
    expand, invf, place = _constants()
    x0 = x.reshape(T, D)
    pos = positions.reshape(T)
    mods = [mod[:, i * D:(i + 1) * D].reshape(B, 1, D) for i in range(N_MOD)]
    sh1, sc1, g1, sh2, sc2, g2, sh3, sc3, g3 = mods
    dtb, alog = _lanes16(small["dt_bias"]), _lanes16(small["a_log"])
    dsk = jnp.repeat(small["d_skip"].reshape(1, SSD_HEADS), SSD_HEAD_DIM, axis=1)

    x1, a1, u1, f1 = _ffn_fwd(x0, small["norm_ffn1"], sh1, sc1, g1, w["ffn1_w_gate"], w["ffn1_w_up"], w["ffn1_w_down"], S, "ffn1_fwd")
    w = {**w, **later_weights(f1)}
    z, xraw, cq, ckv, dtk = _inproj_fwd(x1, small["norm_mix"], sh2, sc2, w["w_in"], S)
    xraw3 = xraw.reshape(B, S, D_CONV)
    xbc = _conv_fwd(xraw3, small["conv_w"], small["conv_b"])
    dtk3, z3 = dtk.reshape(B, S, LANES), z.reshape(B, S, D_SSD)
    y, yssd, prev = _ssd_fwd(xbc, dtk3, z3, dtb, alog, dsk, small["ssd_norm_w"], expand)
    q, k, v, cos_t, sin_t = _mla_prep(cq, ckv, dtk, pos, small["q_norm_w"], small["kv_norm_w"], w["w_uq"], w["w_ukv"], invf,
                                      place)
    hw = MLA_HEADS * HEAD_LANES
    q3, k3, v3 = q.reshape(B, S, hw), k.reshape(B, S, hw), v.reshape(B, S, hw)
    o3, lse = _attn_fwd(q3, k3, v3)
    o = o3.reshape(T, hw)
    x2, m, ycat = _mix_out(x1, yssd.reshape(T, D_SSD), o, small["mla_norm_w"], w["w_out"], g2, S)
    dx3, a2, u2, f2, loss, d_norm_final = _ffn_fwd(
        x2, small["norm_ffn2"], sh3, sc3, g3, w["ffn2_w_gate"], w["ffn2_w_up"], w["ffn2_w_down"], S, "ffn2_fwd",
        head=(small["norm_final"].reshape(1, D), tgt.reshape(T, D)))

    gw, gs = {}, {}
    dx2, h3, s3, df3, da3, du3, dsh3, dsc3, dg3, gs["norm_ffn2"] = _ffn_bwd(
        dx3, x2, small["norm_ffn2"], sh3, sc3, g3, a2, u2, f2, w["ffn2_w_gate"], w["ffn2_w_up"], w["ffn2_w_down"], S, "ffn2_bwd")
    gw["ffn2_w_gate"], gw["ffn2_w_up"], gw["ffn2_w_down"] = _ffn_wgrad(h3, s3, df3, da3, du3, "ffn2_wgrad")
    g2 = g2 + on_grads(("ffn2_w_gate", "ffn2_w_up", "ffn2_w_down"), gw)

    dys, do, dm, dg2, gs["mla_norm_w"] = _mix_out_bwd(dx2, m, o, small["mla_norm_w"], w["w_out"], g2, S)
    gw["w_out"] = _mm_tn(ycat, dm, 512, "dwout")

    dq3, dk3, dv3 = _attn_bwd(q3, k3, v3, o3, do.reshape(B, S, hw), lse)
    dcq, dckv, ddtk_b, qn, kvn, dqb, dkvb, gs["q_norm_w"], gs["kv_norm_w"] = _mla_prep_bwd(
        dq3.reshape(T, hw), dk3.reshape(T, hw), dv3.reshape(T, hw), cq, ckv, cos_t, sin_t, small["q_norm_w"],
        small["kv_norm_w"], w["w_uq"], w["w_ukv"])
    gw["w_uq"] = _mm_tn(qn, dqb, 512, "dwuq")
    gw["w_ukv"] = _mm_tn(kvn, dkvb, 1024, "dwukv")

    dxbc, ddtk_a, dz, gs["ssd_norm_w"], dvec = _ssd_bwd(
        xbc, dtk3, z3, y, prev, dys.reshape(B, S, D_SSD), dtb, alog, dsk, small["ssd_norm_w"], expand)
    gs["dt_bias"], gs["a_log"], gs["d_skip"] = dvec[0:1, :SSD_HEADS], dvec[1:2, :SSD_HEADS], dvec[2:3, :SSD_HEADS]
    dxraw, gs["conv_w"], gs["conv_b"] = _conv_bwd(dxbc, xraw3, small["conv_w"], small["conv_b"])
    dx1, h2, dproj, dsh2, dsc2, gs["norm_mix"] = _inproj_bwd(
        dx2, x1, small["norm_mix"], sh2, sc2, w["w_in"], dz.reshape(T, D_SSD), dxraw.reshape(T, D_CONV), dcq, dckv,
        ddtk_a.reshape(T, LANES), ddtk_b, S)
    gw["w_in"] = _mm_tn(dproj, h2, 512, "dwin")
    g1 = g1 + on_grads(("w_in", "w_uq", "w_ukv", "w_out"), gw)

    dx0, h1, s1, df1, da1, du1, dsh1, dsc1, dg1, gs["norm_ffn1"] = _ffn_bwd(
        dx1, x0, small["norm_ffn1"], sh1, sc1, g1, a1, u1, f1, w["ffn1_w_gate"], w["ffn1_w_up"], w["ffn1_w_down"], S, "ffn1_bwd")
    gw["ffn1_w_gate"], gw["ffn1_w_up"], gw["ffn1_w_down"] = _ffn_wgrad(h1, s1, df1, da1, du1, "ffn1_wgrad")
    gs["norm_final"] = d_norm_final
    dmod = jnp.concatenate([t.reshape(B, D) for t in (dsh1, dsc1, dg1, dsh2, dsc2, dg2, dsh3, dsc3, dg3)], axis=1)
    return loss, dx0.reshape(B, S, D), gw, dmod, gs


HBM_SPEC = pl.BlockSpec(memory_space=pltpu.HBM)
VMEM_SPEC = pl.BlockSpec(memory_space=pltpu.VMEM)


def _place():
    return lax.axis_index("x"), lax.axis_index("y"), lax.axis_index("c")


def _other_chips(mx, my):
    return [(1 - mx, my), (mx, 1 - my), (1 - mx, 1 - my)]


def _remote(src, dst, send_sem, recv_sem, to):
    return pltpu.make_async_remote_copy(src_ref=src, dst_ref=dst, send_sem=send_sem, recv_sem=recv_sem,
                                        device_id=to, device_id_type=MESH)


def _all_gather_small(xa, name):
    r, n = xa.shape

    def body(x_ref, o_ref, token, send_sems, recv_sems):
        mx, my, mc = _place()
        me = 4 * mx + 2 * my + mc
        token[...] = jnp.zeros_like(token)
        o_ref[pl.ds(me, 1)] = x_ref[...][None]
        sends = []
        for k in range(1, N_DEV):
            peer = (mx ^ (k >> 2), my ^ ((k >> 1) & 1), mc ^ (k & 1))
            cp = _remote(x_ref, o_ref.at[me], send_sems.at[k - 1], recv_sems.at[k - 1], peer)
            cp.start()
            sends.append(cp)
        for k in range(1, N_DEV):
            peer = (mx ^ (k >> 2), my ^ ((k >> 1) & 1), mc ^ (k & 1))
            slot = 4 * peer[0] + 2 * peer[1] + peer[2]
            _remote(x_ref, o_ref.at[slot], send_sems.at[k - 1], recv_sems.at[k - 1], peer).wait_recv()
        for cp in sends:
            cp.wait_send()

    return pl.pallas_call(
        body, name=name, in_specs=[VMEM_SPEC], out_specs=[VMEM_SPEC, VMEM_SPEC],
        out_shape=[jax.ShapeDtypeStruct((N_DEV, r, n), xa.dtype), jax.ShapeDtypeStruct((8, LANES), F32)],
        scratch_shapes=[pltpu.SemaphoreType.DMA((N_DEV - 1,)), pltpu.SemaphoreType.DMA((N_DEV - 1,))],
        compiler_params=pltpu.CompilerParams(vmem_limit_bytes=VMEM_LIMIT),
    )(xa)


def _halves_by_rows(shape):
    return (shape[-2] // 2) % 16 == 0


def _half_shape(shape):
    r, c = shape[-2:]
    return tuple(shape[:-2]) + ((r // 2, c) if _halves_by_rows(shape) else (r, c // 2))


def _half_index(shape, hc):
    r, c = shape[-2:]
    if _halves_by_rows(shape):
        return (pl.ds(pl.multiple_of(hc * (r // 2), 16), r // 2), slice(None))
    return (slice(None), pl.ds(pl.multiple_of(hc * (c // 2), LANES), c // 2))


def _half(ref, hc, lead=None):
    idx = _half_index(ref.shape, hc)
    return ref.at[idx] if lead is None else ref.at[(lead,) + idx]


def _gather_weights(shards):
    n = len(shards)

    def body(*refs):
        w_refs, o_refs, token = refs[:n], refs[n:2 * n], refs[2 * n]
        send_sems, recv_sems, stage_sems = refs[2 * n + 1:2 * n + 4]
        stages = refs[2 * n + 4:]
        mx, my, mc = _place()
        chip = 2 * mx + my
        others = _other_chips(mx, my)
        sibling = (mx, my, 1 - mc)
        token[...] = jnp.zeros_like(token)
        stage_in = [pltpu.make_async_copy(w, st, stage_sems.at[0, i]) for i, (w, st) in enumerate(zip(w_refs, stages))]
        for cp in stage_in:
            cp.start()
        first = []
        for i, (w, o) in enumerate(zip(w_refs, o_refs)):
            for k, (cx, cy) in enumerate(others):
                first.append(_remote(_half(w, mc), _half(o, mc, chip), send_sems.at[i, k],
                                     recv_sems.at[i, k], (cx, cy, mc)))
                first[-1].start()
        stage_out = []
        for i, (st, o) in enumerate(zip(stages, o_refs)):
            stage_in[i].wait()
            stage_out.append(pltpu.make_async_copy(st, o.at[chip], stage_sems.at[1, i]))
            stage_out[-1].start()
        passed = []
        for i, (w, o) in enumerate(zip(w_refs, o_refs)):
            for k, (cx, cy) in enumerate(others):
                landed = _half(o, mc, 2 * cx + cy)
                _remote(landed, landed, send_sems.at[i, k], recv_sems.at[i, k], (cx, cy, mc)).wait_recv()
                passed.append(_remote(landed, landed, send_sems.at[i, 3 + k], recv_sems.at[i, 3 + k], sibling))
                passed[-1].start()
        for i, (w, o) in enumerate(zip(w_refs, o_refs)):
            for k, (cx, cy) in enumerate(others):
                there = _half(o, 1 - mc, 2 * cx + cy)
                _remote(there, there, send_sems.at[i, 3 + k], recv_sems.at[i, 3 + k], sibling).wait_recv()
        for cp in first + passed:
            cp.wait_send()
        for cp in stage_out:
            cp.wait()

    out = pl.pallas_call(
        body, name="gather_weights", in_specs=[HBM_SPEC] * n, out_specs=[HBM_SPEC] * n + [VMEM_SPEC],
        out_shape=[jax.ShapeDtypeStruct((N_CHIPS,) + s.shape, s.dtype) for s in shards] + [jax.ShapeDtypeStruct((8, LANES), F32)],
        scratch_shapes=[pltpu.SemaphoreType.DMA((n, 6)), pltpu.SemaphoreType.DMA((n, 6)), pltpu.SemaphoreType.DMA((2, n))]
        + [pltpu.VMEM(s.shape, s.dtype) for s in shards],
        compiler_params=pltpu.CompilerParams(vmem_limit_bytes=VMEM_LIMIT),
    )(*shards)
    return out[:n], out[n]


SEM_SPEC = pl.BlockSpec(memory_space=pltpu.SEMAPHORE)
ANY_SPEC = pl.BlockSpec(memory_space=pl.ANY)
DATAFLOW = pltpu.SideEffectType.DATAFLOW_SIDE_EFFECTING


def _hbm(arr):
    return pltpu.with_memory_space_constraint(arr, pltpu.HBM)


def _gather_start(shards):
    n = len(shards)

    def body(*refs):
        w_refs, land_refs, send_sems, recv_sems, token = refs[:n], refs[n:2 * n], refs[2 * n], refs[2 * n + 1], refs[-1]
        mx, my, mc = _place()
        chip = 2 * mx + my
        for i, (w, land) in enumerate(zip(w_refs, land_refs)):
            for k, (cx, cy) in enumerate(_other_chips(mx, my)):
                _remote(_half(w, mc), _half(land, mc, chip), send_sems.at[3 * i + k],
                        recv_sems.at[3 * i + k], (cx, cy, mc)).start()
        token[...] = jnp.zeros_like(token)

    lands = [lax.empty((N_CHIPS,) + s.shape, s.dtype) for s in shards]
    out = pl.pallas_call(
        body, name="gather_start",
        out_shape=(pltpu.SemaphoreType.DMA((3 * n,)), pltpu.SemaphoreType.DMA((3 * n,)),
                   *[pltpu.HBM(s.shape, s.dtype) for s in shards], *[pltpu.HBM(l.shape, l.dtype) for l in lands],
                   jax.ShapeDtypeStruct((8, LANES), F32)),
        in_specs=[HBM_SPEC] * (2 * n), out_specs=(SEM_SPEC, SEM_SPEC, *[HBM_SPEC] * (2 * n), VMEM_SPEC),
        input_output_aliases={i: 2 + i for i in range(2 * n)},
        compiler_params=pltpu.CompilerParams(has_side_effects=DATAFLOW),
    )(*[_hbm(s) for s in shards], *[_hbm(l) for l in lands])
    return out[0], out[1], out[2:2 + n], out[2 + n:2 + 2 * n], out[-1]


def _gather_wait(send_sems, recv_sems, shards, lands, after):
    n = len(shards)

    def body(*refs):
        w_refs, land_refs, send_sems, recv_sems = refs[:n], refs[n:2 * n], refs[2 * n], refs[2 * n + 1]
        mx, my, mc = _place()
        for i, (w, land) in enumerate(zip(w_refs, land_refs)):
            for k, (cx, cy) in enumerate(_other_chips(mx, my)):
                cp = _remote(_half(w, mc), _half(land, mc, 2 * cx + cy), send_sems.at[3 * i + k],
                             recv_sems.at[3 * i + k], (cx, cy, mc))
                cp.wait_send()
                cp.wait_recv()

    out = pl.pallas_call(
        body, name="gather_wait",
        out_shape=(*[pltpu.HBM(s.shape, s.dtype) for s in shards], *[pltpu.HBM(l.shape, l.dtype) for l in lands]),
        in_specs=[HBM_SPEC] * (2 * n) + [SEM_SPEC, SEM_SPEC, ANY_SPEC], out_specs=tuple([HBM_SPEC] * (2 * n)),
        input_output_aliases={i: i for i in range(2 * n)},
        compiler_params=pltpu.CompilerParams(has_side_effects=DATAFLOW),
    )(*shards, *lands, send_sems, recv_sems, after)
    return out[n:]


def _gather_finish(shards, lands):
    n = len(shards)

    def body(*refs):
        w_refs, land_refs, o_refs = refs[:n], refs[n:2 * n], refs[2 * n:3 * n]
        send_sems, recv_sems, stage_sems = refs[3 * n:3 * n + 3]
        stages = refs[3 * n + 3:]
        mx, my, mc = _place()
        chip = 2 * mx + my
        others = _other_chips(mx, my)
        sibling = (mx, my, 1 - mc)
        stage_in = [pltpu.make_async_copy(w, st, stage_sems.at[0, i]) for i, (w, st) in enumerate(zip(w_refs, stages))]
        for cp in stage_in:
            cp.start()
        passed = []
        for i, (w, o) in enumerate(zip(w_refs, o_refs)):
            for k, (cx, cy) in enumerate(others):
                landed = _half(o, mc, 2 * cx + cy)
                passed.append(_remote(landed, landed, send_sems.at[i, k], recv_sems.at[i, k], sibling))
                passed[-1].start()
        stage_out = []
        for i, (st, o) in enumerate(zip(stages, o_refs)):
            stage_in[i].wait()
            stage_out.append(pltpu.make_async_copy(st, o.at[chip], stage_sems.at[1, i]))
            stage_out[-1].start()
        for i, (w, o) in enumerate(zip(w_refs, o_refs)):
            for k, (cx, cy) in enumerate(others):
                there = _half(o, 1 - mc, 2 * cx + cy)
                _remote(there, there, send_sems.at[i, k], recv_sems.at[i, k], sibling).wait_recv()
        for cp in passed:
            cp.wait_send()
        for cp in stage_out:
            cp.wait()

    return pl.pallas_call(
        body, name="gather_finish", in_specs=[HBM_SPEC] * (2 * n), out_specs=[HBM_SPEC] * n,
        out_shape=[jax.ShapeDtypeStruct(l.shape, l.dtype) for l in lands],
        input_output_aliases={n + i: i for i in range(n)},
        scratch_shapes=[pltpu.SemaphoreType.DMA((n, 3)), pltpu.SemaphoreType.DMA((n, 3)), pltpu.SemaphoreType.DMA((2, n))]
        + [pltpu.VMEM(s.shape, s.dtype) for s in shards],
        compiler_params=pltpu.CompilerParams(vmem_limit_bytes=VMEM_LIMIT),
    )(*shards, *lands)


def _scatter_start(ss, tag):
    n = len(ss)

    def body(*refs):
        s_refs, land_refs, send_sems, recv_sems, token = refs[:n], refs[n:2 * n], refs[2 * n], refs[2 * n + 1], refs[-1]
        mx, my, mc = _place()
        chip = 2 * mx + my
        for i, (s, land) in enumerate(zip(s_refs, land_refs)):
            for k, (cx, cy) in enumerate(_other_chips(mx, my)):
                _remote(s.at[2 * cx + cy], land.at[chip], send_sems.at[3 * i + k], recv_sems.at[3 * i + k],
                        (cx, cy, mc)).start()
        token[...] = jnp.zeros_like(token)

    lands = [lax.empty(s.shape, s.dtype) for s in ss]
    out = pl.pallas_call(
        body, name="scatter_start_" + tag,
        out_shape=(pltpu.SemaphoreType.DMA((3 * n,)), pltpu.SemaphoreType.DMA((3 * n,)),
                   *[pltpu.HBM(s.shape, s.dtype) for s in ss], *[pltpu.HBM(l.shape, l.dtype) for l in lands],
                   jax.ShapeDtypeStruct((8, LANES), F32)),
        in_specs=[HBM_SPEC] * (2 * n), out_specs=(SEM_SPEC, SEM_SPEC, *[HBM_SPEC] * (2 * n), VMEM_SPEC),
        input_output_aliases={i: 2 + i for i in range(2 * n)},
        compiler_params=pltpu.CompilerParams(has_side_effects=DATAFLOW),
    )(*[_hbm(s) for s in ss], *[_hbm(l) for l in lands])
    return out[0], out[1], out[2:2 + n], out[2 + n:2 + 2 * n], out[-1]


def _scatter_wait(send_sems, recv_sems, ss, lands, after, tag):
    n = len(ss)

    def body(*refs):
        s_refs, land_refs, send_sems, recv_sems = refs[:n], refs[n:2 * n], refs[2 * n], refs[2 * n + 1]
        mx, my, mc = _place()
        for i, (s, land) in enumerate(zip(s_refs, land_refs)):
            for k, (cx, cy) in enumerate(_other_chips(mx, my)):
                slot = land.at[2 * cx + cy]
                cp = _remote(s.at[2 * cx + cy], slot, send_sems.at[3 * i + k], recv_sems.at[3 * i + k], (cx, cy, mc))
                cp.wait_send()
                cp.wait_recv()

    out = pl.pallas_call(
        body, name="scatter_wait_" + tag,
        out_shape=(*[pltpu.HBM(s.shape, s.dtype) for s in ss], *[pltpu.HBM(l.shape, l.dtype) for l in lands]),
        in_specs=[HBM_SPEC] * (2 * n) + [SEM_SPEC, SEM_SPEC, ANY_SPEC], out_specs=tuple([HBM_SPEC] * (2 * n)),
        input_output_aliases={i: i for i in range(2 * n)},
        compiler_params=pltpu.CompilerParams(has_side_effects=DATAFLOW),
    )(*ss, *lands, send_sems, recv_sems, after)
    return out[:n], out[n:]


def _swap_halves(gs, after, name):
    n = len(gs)

    def body(*refs):
        g_refs, o_refs, send_sems, recv_sems = refs[:n], refs[n + 1:2 * n + 1], refs[2 * n + 1], refs[2 * n + 2]
        mx, my, mc = _place()
        copies = []
        for i, (g, o) in enumerate(zip(g_refs, o_refs)):
            src = g.at[(slice(None),) + _half_index(g.shape, 1 - mc)]
            copies.append(_remote(src, o, send_sems.at[i], recv_sems.at[i], (mx, my, 1 - mc)))
            copies[-1].start()
        for cp in copies:
            cp.wait()

    return pl.pallas_call(
        body, name=name, in_specs=[HBM_SPEC] * n + [ANY_SPEC], out_specs=[HBM_SPEC] * n,
        out_shape=[jax.ShapeDtypeStruct(_half_shape(g.shape), g.dtype) for g in gs],
        scratch_shapes=[pltpu.SemaphoreType.DMA((n,)), pltpu.SemaphoreType.DMA((n,))],
    )(*gs, after)


def _pair_sum(g, got, core, name):
    hr, hc = _half_shape(g.shape)[1:]
    by_rows = _halves_by_rows(g.shape)

    def body(core_ref, g_ref, got_ref, o_ref):
        o_ref[...] = (g_ref[...].astype(F32) + got_ref[...].astype(F32)).astype(BF16)

    return pl.pallas_call(
        body, name=name,
        grid_spec=pltpu.PrefetchScalarGridSpec(
            num_scalar_prefetch=1, grid=(N_CHIPS,),
            in_specs=[pl.BlockSpec((1, hr, hc), lambda j, core_ref: (j, core_ref[0], 0) if by_rows else (j, 0, core_ref[0])),
                      pl.BlockSpec((1, hr, hc), lambda j, core_ref: (j, 0, 0))],
            out_specs=pl.BlockSpec((1, hr, hc), lambda j, core_ref: (j, 0, 0))),
        out_shape=jax.ShapeDtypeStruct((N_CHIPS, hr, hc), BF16),
        compiler_params=_cparams(("arbitrary",)),
    )(core, g, got)


def _chip_sum(own, got, chip, name):
    _, h, c = own.shape

    def body(chip_ref, a_ref, b_ref, c_ref, d_ref, o_ref):
        o_ref[...] = ((a_ref[0].astype(F32) + b_ref[0].astype(F32)) + c_ref[0].astype(F32)) + d_ref[0].astype(F32)

    slot = lambda flip: pl.BlockSpec((1, h, c), lambda i, chip_ref: (chip_ref[0] ^ flip, 0, 0))
    return pl.pallas_call(
        body, name=name,
        grid_spec=pltpu.PrefetchScalarGridSpec(
            num_scalar_prefetch=1, grid=(1,), in_specs=[slot(0), slot(1), slot(2), slot(3)],
            out_specs=pl.BlockSpec((h, c), lambda i, chip_ref: (0, 0))),
        out_shape=jax.ShapeDtypeStruct((h, c), F32),
        compiler_params=_cparams(("arbitrary",)),
    )(chip, own, got, got, got)


def _join_halves(mine, name):
    n = len(mine)

    def body(*refs):
        m_refs, o_refs, send_sems, recv_sems = refs[:n], refs[n:2 * n], refs[2 * n], refs[2 * n + 1]
        mx, my, mc = _place()
        copies = []
        for i, (m, o) in enumerate(zip(m_refs, o_refs)):
            copies.append(_remote(m, o, send_sems.at[i], recv_sems.at[i], (mx, my, 1 - mc)))
            copies[-1].start()
        for cp in copies:
            cp.wait()

    return pl.pallas_call(
        body, name=name, in_specs=[HBM_SPEC] * n, out_specs=[HBM_SPEC] * n,
        out_shape=[jax.ShapeDtypeStruct(m.shape, m.dtype) for m in mine],
        scratch_shapes=[pltpu.SemaphoreType.DMA((n,)), pltpu.SemaphoreType.DMA((n,))],
    )(*mine)


def _adam_math(w, g, m, v):
    m2 = ADAM_B1 * m + (1.0 - ADAM_B1) * g
    v2 = ADAM_B2 * v + (1.0 - ADAM_B2) * (g * g)
    m_hat = m2 * (1.0 / (1.0 - ADAM_B1 ** ADAM_STEP))
    v_hat = v2 * (1.0 / (1.0 - ADAM_B2 ** ADAM_STEP))
    delta = -ADAM_LR * (m_hat / (jnp.sqrt(v_hat) + ADAM_EPS) + ADAM_WD * w)
    return delta, m2, v2


def _adam(w, g, m, v, name):
    def body(w_ref, g_ref, m_ref, v_ref, d_ref, m2_ref, v2_ref):
        d_ref[...], m2_ref[...], v2_ref[...] = _adam_math(w_ref[...], g_ref[...], m_ref[...], v_ref[...])

    return pl.pallas_call(body, name=name, out_shape=[jax.ShapeDtypeStruct(w.shape, F32)] * 3)(w, g, m, v)


def _adam_halves(w, m, v, mine, theirs, core, name):
    hr, hcols = _half_shape(w.shape)[1:]
    by_rows = _halves_by_rows(w.shape)

    def body(core_ref, w_ref, m_ref, v_ref, mine_ref, theirs_ref, g_ref, d_ref, m2_ref, v2_ref):
        g = jnp.where(pl.program_id(0) == core_ref[0], mine_ref[...], theirs_ref[...])
        g_ref[0] = g
        d_ref[0], m2_ref[0], v2_ref[0] = _adam_math(w_ref[0], g, m_ref[0], v_ref[0])

    half = pl.BlockSpec((1, hr, hcols), lambda hc, core_ref: (0, hc, 0) if by_rows else (0, 0, hc))
    whole = pl.BlockSpec((hr, hcols), lambda hc, core_ref: (0, 0))
    return pl.pallas_call(
        body, name=name,
        grid_spec=pltpu.PrefetchScalarGridSpec(
            num_scalar_prefetch=1, grid=(2,), in_specs=[half, half, half, whole, whole], out_specs=[half] * 4),
        out_shape=[jax.ShapeDtypeStruct(w.shape, F32)] * 4,
        compiler_params=_cparams(("arbitrary",)),
    )(core, w, m, v, mine, theirs)


ADA_COLS = N_MOD * D_MODEL // N_CHIPS


def _ada_fwd(c_all, w_ada, b_cols):
    def body(c_ref, w_ref, b_ref, o_ref):
        cv = c_ref[...]
        act = (cv * _sigmoid(cv)).astype(BF16)
        o_ref[...] = _dot(act, w_ref[...].astype(BF16)) + b_ref[...]

    return pl.pallas_call(
        body, name="ada_fwd", out_shape=jax.ShapeDtypeStruct((c_all.shape[0], ADA_COLS), F32),
        compiler_params=pltpu.CompilerParams(vmem_limit_bytes=VMEM_LIMIT),
    )(c_all, w_ada, b_cols)


def _ada_bwd(c_all, dmod_cols, w, m, v):
    nb = c_all.shape[0]
    tn = 384

    def body(c_ref, d_ref, w_ref, m_ref, v_ref, g_ref, dl_ref, m2_ref, v2_ref):
        cv = c_ref[...]
        act = (cv * _sigmoid(cv)).astype(BF16)
        g = _dot_tn(act, d_ref[...].astype(BF16))
        g_ref[...] = g
        dl_ref[...], m2_ref[...], v2_ref[...] = _adam_math(w_ref[...], g, m_ref[...], v_ref[...])

    blk = pl.BlockSpec((D_MODEL, tn), lambda j: (0, j))
    return pl.pallas_call(
        body, name="ada_bwd", grid=(ADA_COLS // tn,),
        in_specs=[pl.BlockSpec((nb, D_MODEL), lambda j: (0, 0)), pl.BlockSpec((nb, tn), lambda j: (0, j)), blk, blk, blk],
        out_specs=[blk] * 4, out_shape=[jax.ShapeDtypeStruct((D_MODEL, ADA_COLS), F32)] * 4,
        compiler_params=_cparams(("arbitrary",)),
    )(c_all, dmod_cols, w, m, v)


SMALL_NAMES = ("norm_ffn1", "norm_mix", "conv_w", "conv_b", "ssd_norm_w", "q_norm_w", "kv_norm_w", "mla_norm_w",
               "norm_ffn2", "norm_final", "dt_bias", "a_log", "d_skip")
SMALL_SIZES = (1024, 1024, CONV_WIDTH * D_CONV, D_CONV, 1024, Q_LORA, KV_LORA, 1024, 1024, 1024, 16, 16, 16)
SMALL_ROWS = 16
MOD_ROWS = 2 * N_MOD
SEND_ROWS = 40


def _pack_small(parts):
    flat = jnp.concatenate([parts[n].reshape(-1) for n in SMALL_NAMES])
    return jnp.pad(flat, (0, SMALL_ROWS * D_MODEL - flat.shape[0]))


def _unpack_small(flat):
    out, off = {}, 0
    for n, size in zip(SMALL_NAMES, SMALL_SIZES):
        out[n] = flat[off:off + size]
        off += size
    return out


def _small_sum(got):
    def body(g_ref, o_ref):
        bsum = jnp.zeros((N_MOD, D_MODEL), F32)
        ssum = jnp.zeros((SMALL_ROWS, D_MODEL), F32)
        for d in range(N_DEV):
            bsum = bsum + g_ref[d, 0:N_MOD, :] + g_ref[d, N_MOD:MOD_ROWS, :]
            ssum = ssum + g_ref[d, MOD_ROWS:MOD_ROWS + SMALL_ROWS, :]
        o_ref[...] = jnp.concatenate([bsum, ssum, jnp.zeros((32 - N_MOD - SMALL_ROWS, D_MODEL), F32)], axis=0)

    return pl.pallas_call(body, name="small_sum", out_shape=jax.ShapeDtypeStruct((32, D_MODEL), F32))(got)


BIG_NAMES = ("ffn1_w_gate", "ffn1_w_up", "ffn1_w_down", "w_in", "w_uq", "w_ukv", "w_out", "ffn2_w_gate", "ffn2_w_up",
             "ffn2_w_down")
_TO_KERNEL = {"w_in": _win_to_kernel, "w_uq": _wuq_to_kernel, "w_ukv": _wukv_to_kernel}
_FROM_KERNEL = {"w_in": _win_from_kernel, "w_uq": _wuq_from_kernel, "w_ukv": _wukv_from_kernel}


def _columns_joined(w4):
    n, r, c = w4.shape
    return w4.transpose(1, 0, 2).reshape(r, n * c)


def _columns_split(g):
    r, cols = g.shape
    return g.reshape(r, N_CHIPS, cols // N_CHIPS).transpose(1, 0, 2)


def kernel(x, c, positions, w_ada, b_ada, norm_ffn1, ffn1_w_gate, ffn1_w_up, ffn1_w_down, norm_mix, w_in, conv_w, conv_b, dt_bias, a_log, d_skip, ssd_norm_w, q_norm_w, w_uq, kv_norm_w, w_ukv, mla_norm_w, w_out, norm_ffn2, ffn2_w_gate, ffn2_w_up, ffn2_w_down, norm_final, loss_target, m_w_ada, m_b_ada, m_norm_ffn1, m_ffn1_w_gate, m_ffn1_w_up, m_ffn1_w_down, m_norm_mix, m_w_in, m_conv_w, m_conv_b, m_dt_bias, m_a_log, m_d_skip, m_ssd_norm_w, m_q_norm_w, m_w_uq, m_kv_norm_w, m_w_ukv, m_mla_norm_w, m_w_out, m_norm_ffn2, m_ffn2_w_gate, m_ffn2_w_up, m_ffn2_w_down, m_norm_final, v_w_ada, v_b_ada, v_norm_ffn1, v_ffn1_w_gate, v_ffn1_w_up, v_ffn1_w_down, v_norm_mix, v_w_in, v_conv_w, v_conv_b, v_dt_bias, v_a_log, v_d_skip, v_ssd_norm_w, v_q_norm_w, v_w_uq, v_kv_norm_w, v_w_ukv, v_mla_norm_w, v_w_out, v_norm_ffn2, v_ffn2_w_gate, v_ffn2_w_up, v_ffn2_w_down, v_norm_final):
    a = dict(locals())
    held_transposed = ("ffn1_w_gate", "ffn1_w_up", "ffn2_w_gate", "ffn2_w_up", "w_in")
    for n in held_transposed:
        for p in ("", "m_", "v_"):
            a[p + n] = a[p + n].transpose(0, 2, 1)
    B, S, D = x.shape
    mx, my, mc = _place()
    chip = 2 * mx + my
    dev = 2 * chip + mc
    core = mc.astype(jnp.int32).reshape(1)
    chip_id = chip.astype(jnp.int32).reshape(1)

    cw_rows = jnp.pad(conv_w[0], ((0, 0), (0, D - conv_w.shape[2])))
    got, _ = _all_gather_small(jnp.concatenate([c, cw_rows, jnp.zeros((8 - B - CONV_WIDTH, D), F32)], axis=0), "gather_c")
    c_all = got[:, :B, :].reshape(N_DEV * B, D)
    conv_full = got[::2, B:B + CONV_WIDTH, :conv_w.shape[2]].transpose(1, 0, 2).reshape(CONV_WIDTH, D_CONV)

    b_cols = lax.dynamic_slice(b_ada, (0, chip * ADA_COLS), (1, ADA_COLS))
    mod_all, mod_done = _all_gather_small(_ada_fwd(c_all, w_ada[0], b_cols), "gather_mod")
    mod = lax.dynamic_slice(mod_all, (0, B * dev, 0), (N_DEV, B, ADA_COLS))[::2].transpose(1, 0, 2).reshape(B, N_MOD * D)

    first = ("ffn1_w_gate", "ffn1_w_up", "ffn1_w_down")
    later = tuple(n for n in BIG_NAMES if n not in first)
    got_first, gathered = _gather_weights([(a[n][0] + mod_done[0, 0]).astype(BF16) for n in first])
    w = dict(zip(first, got_first))
    in_flight = _gather_start([(a[n][0] + gathered[0, 0]).astype(BF16) for n in later])

    def later_weights(after):
        send_sems, recv_sems, shards, lands, _ = in_flight
        lands = _gather_wait(send_sems, recv_sems, shards, lands, after)
        wl = dict(zip(later, _gather_finish([a[n][0].astype(BF16) for n in later], lands)))
        for n, to_kernel in _TO_KERNEL.items():
            wl[n] = to_kernel(wl[n].reshape(-1, D) if n in held_transposed else _columns_joined(wl[n]))
        wl["w_out"] = wl["w_out"].reshape(D_SSD + D_MLA, D)
        return wl

    small = {n: a[n].reshape(1, -1) for n in SMALL_NAMES if n not in ("conv_w", "norm_final")}
    small["conv_w"], small["norm_final"] = conv_full, norm_final

    def scatter_group(names, gw, after):
        g4 = []
        for n in names:
            g = gw[n]
            if n in _FROM_KERNEL:
                g = _FROM_KERNEL[n](g) if n in held_transposed else _columns_split(_FROM_KERNEL[n](g))
            g4.append(g.reshape(N_CHIPS, a[n].shape[1], a[n].shape[2]))
        swapped = _swap_halves(g4, g4[0] if after is None else after, "swap_" + names[0])
        pair = [_pair_sum(g, got, core, "pair_sum_" + n) for n, g, got in zip(names, g4, swapped)]
        return (names,) + tuple(_scatter_start(pair, names[0]))

    grads, deltas, new_m, new_v = {}, {}, {}, {}

    def finish_groups(some, after):
        names, mine = [], []
        for group_names, send_sems, recv_sems, pair, lands, _ in some:
            pair, lands = _scatter_wait(send_sems, recv_sems, pair, lands, after, group_names[0])
            names += group_names
            mine += [_chip_sum(own, got, chip_id, "chip_sum_" + n) for n, own, got in zip(group_names, pair, lands)]
        for n, own, other in zip(names, mine, _join_halves(mine, "join_" + names[0])):
            grads[n], deltas[n], new_m[n], new_v[n] = _adam_halves(a[n], a["m_" + n], a["v_" + n], own, other, core, "adam_" + n)
        return deltas[names[-1]]

    groups = []

    def on_grads(names, gw):
        groups.append(scatter_group(names, gw, None))
        return groups[-1][5][0, 0]

    loss_blk, grad_x, gw, dmod, gs = _local_step(x, positions, mod + in_flight[4][0, 0], w, later_weights, small, loss_target,
                                                 on_grads)

    small_flat = _pack_small(gs).at[-1].set(loss_blk[0, 0])
    send = jnp.concatenate([dmod.reshape(MOD_ROWS, D), small_flat.reshape(SMALL_ROWS, D),
                            jnp.zeros((SEND_ROWS - MOD_ROWS - SMALL_ROWS, D), F32)], axis=0)
    got, _ = _all_gather_small(send, "gather_small")
    summed = _small_sum(got)
    sums = summed[N_MOD:N_MOD + SMALL_ROWS].reshape(-1)
    loss = sums[-1]
    gsmall = _unpack_small(sums)
    gsmall["conv_w"] = lax.dynamic_slice(gsmall["conv_w"].reshape(CONV_WIDTH, D_CONV), (0, chip * conv_w.shape[2]),
                                         (CONV_WIDTH, conv_w.shape[2]))
    gsmall["b_ada"] = summed[:N_MOD]
    names = ("b_ada",) + SMALL_NAMES
    rows = 208

    def pack(parts):
        flat = jnp.concatenate([parts[n].reshape(-1) for n in names])
        return jnp.pad(flat, (0, rows * LANES - flat.shape[0])).reshape(rows, LANES)

    packed = [pack({n: a[p + n] for n in names}) for p in ("", "m_", "v_")]
    g_p = pack(gsmall)
    outs = (g_p,) + tuple(_adam(packed[0], g_p, packed[1], packed[2], "adam_small"))
    for dst, flat in zip((grads, deltas, new_m, new_v), outs):
        flat, off = flat.reshape(-1), 0
        for n in names:
            dst[n] = flat[off:off + a[n].size].reshape(a[n].shape)
            off += a[n].size

    dmod_all = got[:, :MOD_ROWS, :].reshape(N_DEV * B, N_MOD * D)
    dmod_cols = lax.dynamic_slice(dmod_all, (0, chip * ADA_COLS), (N_DEV * B, ADA_COLS))
    ada = _ada_bwd(c_all, dmod_cols, w_ada[0], m_w_ada[0], v_w_ada[0])
    for dst, t in zip((grads, deltas, new_m, new_v), ada):
        dst["w_ada"] = t[None]

    last = scatter_group(first, gw, summed)
    finish_groups([last], finish_groups(groups, last[5]))
    for dst in (grads, deltas, new_m, new_v):
        for n in held_transposed:
            dst[n] = dst[n].transpose(0, 2, 1)

    order = ("w_ada", "b_ada", "norm_ffn1", "ffn1_w_gate", "ffn1_w_up", "ffn1_w_down", "norm_mix", "w_in", "conv_w", "conv_b",
             "dt_bias", "a_log", "d_skip", "ssd_norm_w", "q_norm_w", "w_uq", "kv_norm_w", "w_ukv", "mla_norm_w", "w_out",
             "norm_ffn2", "ffn2_w_gate", "ffn2_w_up", "ffn2_w_down", "norm_final")
    return (loss, grad_x, *[grads[n] for n in order], *[deltas[n] for n in order], *[new_m[n] for n in order],
            *[new_v[n] for n in order])
```

```python
import functools
import math

import jax
import jax.numpy as jnp
import numpy as np
from jax import lax
from jax.experimental import pallas as pl
from jax.experimental.pallas import tpu as pltpu

F32 = jnp.float32
BF16 = jnp.bfloat16
HIGHEST = lax.Precision.HIGHEST

D_MODEL = 1024
D_FF = 2816
D_SSD = 1024
D_MLA = 1024
SSD_HEADS = 16
SSD_HEAD_DIM = 64
SSD_GROUPS = 2
SSD_STATE = 128
CONV_WIDTH = 4
CHUNK = 128
MLA_HEADS = 8
QK_NOPE = 64
QK_ROPE = 32
QK_DIM = QK_NOPE + QK_ROPE
V_HEAD = 128
Q_LORA = 384
KV_LORA = 256
ROPE_THETA = 10000.0
N_MOD = 9
EPS = 1e-6
D_CONV = D_SSD + 2 * SSD_GROUPS * SSD_STATE
D_PROJ = 3328
HEAD_LANES = 128
ADAM_LR = 0.001
ADAM_B1 = 0.9
ADAM_B2 = 0.999
ADAM_EPS = 1e-08
ADAM_WD = 0.01
ADAM_STEP = 10

LANES = 128
VMEM_LIMIT = 56 * 1024 * 1024
TOKEN_TILE = 512
WIDE_TOKEN_TILE = 1024
ATTN_FWD_Q_TILE = 1024
ATTN_FWD_KV_TILE = 1024
ATTN_BWD_TILE = 1024
N_CHIPS = 4
N_DEV = 8

MESH = pl.DeviceIdType.MESH


def _dot(a, b, precision=None):
    return jnp.dot(a, b, preferred_element_type=F32, precision=precision)


def _dot_nt(a, b, precision=None):
    return lax.dot_general(a, b, (((1,), (1,)), ((), ())), preferred_element_type=F32, precision=precision)


def _dot_tn(a, b, precision=None):
    return lax.dot_general(a, b, (((0,), (0,)), ((), ())), preferred_element_type=F32, precision=precision)


def _cparams(semantics):
    return pltpu.CompilerParams(dimension_semantics=semantics, vmem_limit_bytes=VMEM_LIMIT)


def _resident(shape):
    zeros = (0,) * len(shape)
    return pl.BlockSpec(shape, lambda *_: zeros, pipeline_mode=pl.Buffered(1))


def _sigmoid(x):
    return jax.nn.sigmoid(x)


def _rms_stats(x):
    r = lax.rsqrt(jnp.mean(x * x, axis=-1, keepdims=True) + EPS)
    return x * r, r


def _rms_bwd(dn, xh, r, w):
    dxh = dn * w
    dx = r * (dxh - xh * jnp.mean(dxh * xh, axis=-1, keepdims=True))
    return dx, dn * xh


def _colsum(v):
    return jnp.sum(v, axis=0, keepdims=True)


def _ffn_fwd(x, nw, sh, sc, g, wg, wu, wd, seq, name, head=None):
    T, D = x.shape
    fs = wg.shape[1]
    tm = min(TOKEN_TILE, seq)
    tps = seq // tm

    def body(x_ref, nw_ref, sh_ref, sc_ref, g_ref, wg_ref, wu_ref, wd_ref, *rest):
        if head is None:
            xo_ref, a_ref, u_ref, f_ref = rest
        else:
            nf_ref, t_ref, xo_ref, a_ref, u_ref, f_ref, loss_ref, dnf_ref = rest

            @pl.when(pl.program_id(0) == 0)
            def _():
                loss_ref[...] = jnp.zeros_like(loss_ref)
                dnf_ref[...] = jnp.zeros_like(dnf_ref)

        xv = x_ref[...]
        xh, _ = _rms_stats(xv)
        h = (xh * nw_ref[...]) * (1.0 + sc_ref[0]) + sh_ref[0]
        hb = h.astype(BF16)
        f = jnp.zeros((tm, D), F32)
        for j in range(N_CHIPS):
            a = _dot_nt(hb, wg_ref[j])
            u = _dot_nt(hb, wu_ref[j])
            a_ref[j] = a.astype(BF16)
            u_ref[j] = u.astype(BF16)
            f = f + _dot((a * _sigmoid(a) * u).astype(BF16), wd_ref[j])
        f_ref[...] = f.astype(BF16)
        xo = xv + 0.5 * g_ref[0] * f
        if head is None:
            xo_ref[...] = xo
        else:
            xh, r = _rms_stats(xo)
            nfv = nf_ref[...]
            err = xh * nfv - t_ref[...]
            loss_ref[...] += (0.5 / D) * jnp.sum(err * err)
            dxo, dw_rows = _rms_bwd(err * (1.0 / D), xh, r, nfv)
            xo_ref[...] = dxo
            dnf_ref[...] += _colsum(dw_rows)

    rows = lambda n: pl.BlockSpec((tm, n), lambda i: (i, 0))
    act = pl.BlockSpec((N_CHIPS, tm, fs), lambda i: (0, i, 0))
    perb = pl.BlockSpec((1, 1, D), lambda i: (i // tps, 0, 0))
    sd = jax.ShapeDtypeStruct
    in_specs = [rows(D), _resident((1, D)), perb, perb, perb, _resident((N_CHIPS, fs, D)), _resident((N_CHIPS, fs, D)),
                _resident((N_CHIPS, fs, D))]
    out_specs = [rows(D), act, act, rows(D)]
    out_shape = [sd((T, D), F32), sd((N_CHIPS, T, fs), BF16), sd((N_CHIPS, T, fs), BF16), sd((T, D), BF16)]
    if head is not None:
        in_specs += [_resident((1, D)), rows(D)]
        out_specs += [pl.BlockSpec((8, LANES), lambda i: (0, 0)), pl.BlockSpec((1, D), lambda i: (0, 0))]
        out_shape += [sd((8, LANES), F32), sd((1, D), F32)]
    return pl.pallas_call(
        body, grid=(T // tm,), name=name, in_specs=in_specs, out_specs=out_specs, out_shape=out_shape,
        compiler_params=_cparams(("arbitrary",)),
    )(x, nw, sh, sc, g, wg, wu, wd, *(head or ()))


def _ffn_bwd(dxo, x, nw, sh, sc, g, a, u, f, wg, wu, wd, seq, name):
    T, D = x.shape
    fs = wg.shape[1]
    B = T // seq
    tm = min(TOKEN_TILE // 2, seq)
    tps = seq // tm

    def body(dxo_ref, x_ref, nw_ref, sh_ref, sc_ref, g_ref, a_ref, u_ref, f_ref, wg_ref, wu_ref, wd_ref,
             dx_ref, h_ref, s_ref, df_ref, da_ref, du_ref, dsh_ref, dsc_ref, dg_ref, dnw_ref):
        i = pl.program_id(0)

        @pl.when(i % tps == 0)
        def _():
            dsh_ref[...] = jnp.zeros_like(dsh_ref)
            dsc_ref[...] = jnp.zeros_like(dsc_ref)
            dg_ref[...] = jnp.zeros_like(dg_ref)

        @pl.when(i == 0)
        def _():
            dnw_ref[...] = jnp.zeros_like(dnw_ref)

        dxo_v = dxo_ref[...]
        dfb = (0.5 * g_ref[0] * dxo_v).astype(BF16)
        dg_ref[0] += _colsum(0.5 * dxo_v * f_ref[...].astype(F32))
        dh = jnp.zeros((tm, D), F32)
        for j in range(N_CHIPS):
            ds = _dot_nt(dfb, wd_ref[j])
            av = a_ref[j].astype(F32)
            uv = u_ref[j].astype(F32)
            sig = _sigmoid(av)
            sil = av * sig
            dab = (ds * uv * (sig * (1.0 + av * (1.0 - sig)))).astype(BF16)
            dub = (ds * sil).astype(BF16)
            dh = dh + _dot(dab, wg_ref[j]) + _dot(dub, wu_ref[j])
            s_ref[j] = (sil * uv).astype(BF16)
            da_ref[j] = dab
            du_ref[j] = dub
        xv = x_ref[...]
        xh, r = _rms_stats(xv)
        nwv = nw_ref[...]
        n = xh * nwv
        scale1 = 1.0 + sc_ref[0]
        dsc_ref[0] += _colsum(dh * n)
        dsh_ref[0] += _colsum(dh)
        dx, dw_rows = _rms_bwd(dh * scale1, xh, r, nwv)
        dnw_ref[...] += _colsum(dw_rows)
        dx_ref[...] = dxo_v + dx
        h_ref[...] = (n * scale1 + sh_ref[0]).astype(BF16)
        df_ref[...] = dfb

    rows = lambda n: pl.BlockSpec((tm, n), lambda i: (i, 0))
    act = pl.BlockSpec((N_CHIPS, tm, fs), lambda i: (0, i, 0))
    perb = pl.BlockSpec((1, 1, D), lambda i: (i // tps, 0, 0))
    sd = jax.ShapeDtypeStruct
    return pl.pallas_call(
        body, grid=(T // tm,), name=name,
        in_specs=[rows(D), rows(D), _resident((1, D)), perb, perb, perb, act, act, rows(D),
                  _resident((N_CHIPS, fs, D)), _resident((N_CHIPS, fs, D)), _resident((N_CHIPS, fs, D))],
        out_specs=[rows(D), rows(D), act, rows(D), act, act, perb, perb, perb, pl.BlockSpec((1, D), lambda i: (0, 0))],
        out_shape=[sd((T, D), F32), sd((T, D), BF16), sd((N_CHIPS, T, fs), BF16), sd((T, D), BF16),
                   sd((N_CHIPS, T, fs), BF16), sd((N_CHIPS, T, fs), BF16), sd((B, 1, D), F32), sd((B, 1, D), F32),
                   sd((B, 1, D), F32), sd((1, D), F32)],
        compiler_params=_cparams(("arbitrary",)),
    )(dxo, x, nw, sh, sc, g, a, u, f, wg, wu, wd)


def _ffn_wgrad(h, s, df, da, du, name):
    T, D = h.shape
    fs = s.shape[2]
    tt = min(TOKEN_TILE, T)
    nt = T // tt

    def body(h_ref, s_ref, df_ref, da_ref, du_ref, dgate_ref, dup_ref, ddown_ref, gate_acc, up_acc, down_acc):
        @pl.when(pl.program_id(1) == 0)
        def _():
            gate_acc[...] = jnp.zeros_like(gate_acc)
            up_acc[...] = jnp.zeros_like(up_acc)
            down_acc[...] = jnp.zeros_like(down_acc)

        hv = h_ref[...]
        gate_acc[...] += _dot_tn(da_ref[0], hv)
        up_acc[...] += _dot_tn(du_ref[0], hv)
        down_acc[...] += _dot_tn(s_ref[0], df_ref[...])

        @pl.when(pl.program_id(1) == nt - 1)
        def _():
            dgate_ref[0] = gate_acc[...].astype(BF16)
            dup_ref[0] = up_acc[...].astype(BF16)
            ddown_ref[0] = down_acc[...].astype(BF16)

    rows = pl.BlockSpec((tt, D), lambda j, t: (t, 0))
    act = pl.BlockSpec((1, tt, fs), lambda j, t: (j, t, 0))
    shard = pl.BlockSpec((1, fs, D), lambda j, t: (j, 0, 0))
    return pl.pallas_call(
        body, grid=(N_CHIPS, nt), name=name,
        in_specs=[rows, act, rows, act, act],
        out_specs=[shard] * 3, out_shape=[jax.ShapeDtypeStruct((N_CHIPS, fs, D), BF16)] * 3,
        scratch_shapes=[pltpu.VMEM((fs, D), F32)] * 3,
        compiler_params=_cparams(("arbitrary", "arbitrary")),
    )(h, s, df, da, du)


def _mm_tn(xa, ya, tn, name):
    T, K = xa.shape
    N = ya.shape[1]
    tt = min(WIDE_TOKEN_TILE, T)
    nt = T // tt

    def body(x_ref, y_ref, o_ref, acc_ref):
        @pl.when(pl.program_id(1) == 0)
        def _():
            acc_ref[...] = jnp.zeros_like(acc_ref)

        acc_ref[...] += _dot_tn(x_ref[...], y_ref[...])

        @pl.when(pl.program_id(1) == nt - 1)
        def _():
            o_ref[...] = acc_ref[...].astype(BF16)

    return pl.pallas_call(
        body, grid=(N // tn, nt), name=name,
        in_specs=[pl.BlockSpec((tt, K), lambda j, t: (t, 0)), pl.BlockSpec((tt, tn), lambda j, t: (t, j))],
        out_specs=pl.BlockSpec((K, tn), lambda j, t: (0, j)),
        out_shape=jax.ShapeDtypeStruct((K, N), BF16),
        scratch_shapes=[pltpu.VMEM((K, tn), F32)],
        compiler_params=_cparams(("arbitrary", "arbitrary")),
    )(xa, ya)


_PROJ_SPLITS = (0, 1024, 2560, 2944, 3200, 3328)


def _inproj_fwd(x, nw, sh, sc, win, seq):
    T, D = x.shape
    tm = min(WIDE_TOKEN_TILE, seq)
    tps = seq // tm
    widths = [b - a for a, b in zip(_PROJ_SPLITS[:-1], _PROJ_SPLITS[1:])]
    dtypes = [BF16, BF16, F32, F32, F32]

    def body(x_ref, nw_ref, sh_ref, sc_ref, w_ref, *outs):
        xh, _ = _rms_stats(x_ref[...])
        h = (xh * nw_ref[...]) * (1.0 + sc_ref[0]) + sh_ref[0]
        proj = _dot_nt(h.astype(BF16), w_ref[...])
        for o, lo, hi in zip(outs, _PROJ_SPLITS[:-1], _PROJ_SPLITS[1:]):
            o[...] = proj[:, lo:hi].astype(o.dtype)

    rows = lambda n: pl.BlockSpec((tm, n), lambda i: (i, 0))
    perb = pl.BlockSpec((1, 1, D), lambda i: (i // tps, 0, 0))
    return pl.pallas_call(
        body, grid=(T // tm,), name="inproj_fwd",
        in_specs=[rows(D), _resident((1, D)), perb, perb, _resident((D_PROJ, D))],
        out_specs=[rows(w) for w in widths],
        out_shape=[jax.ShapeDtypeStruct((T, w), dt) for w, dt in zip(widths, dtypes)],
        compiler_params=_cparams(("arbitrary",)),
    )(x, nw, sh, sc, win)


def _inproj_bwd(dx2, x, nw, sh, sc, win, dz, dxbc, dcq, dckv, ddtk_a, ddtk_b, seq):
    T, D = x.shape
    B = T // seq
    tm = min(TOKEN_TILE, seq)
    tps = seq // tm

    def body(dx2_ref, x_ref, nw_ref, sh_ref, sc_ref, w_ref, dz_ref, dxbc_ref, dcq_ref, dckv_ref, da_ref, db_ref,
             dx_ref, h_ref, dp_ref, dsh_ref, dsc_ref, dnw_ref):
        i = pl.program_id(0)

        @pl.when(i % tps == 0)
        def _():
            dsh_ref[...] = jnp.zeros_like(dsh_ref)
            dsc_ref[...] = jnp.zeros_like(dsc_ref)

        @pl.when(i == 0)
        def _():
            dnw_ref[...] = jnp.zeros_like(dnw_ref)

        dproj = jnp.concatenate(
            [dz_ref[...], dxbc_ref[...], dcq_ref[...].astype(BF16), dckv_ref[...].astype(BF16),
             (da_ref[...] + db_ref[...]).astype(BF16)], axis=1)
        dp_ref[...] = dproj
        dh = _dot(dproj, w_ref[...])
        xh, r = _rms_stats(x_ref[...])
        nwv = nw_ref[...]
        n = xh * nwv
        scale1 = 1.0 + sc_ref[0]
        dsc_ref[0] += _colsum(dh * n)
        dsh_ref[0] += _colsum(dh)
        dx, dw_rows = _rms_bwd(dh * scale1, xh, r, nwv)
        dnw_ref[...] += _colsum(dw_rows)
        dx_ref[...] = dx2_ref[...] + dx
        h_ref[...] = (n * scale1 + sh_ref[0]).astype(BF16)

    rows = lambda n: pl.BlockSpec((tm, n), lambda i: (i, 0))
    perb = pl.BlockSpec((1, 1, D), lambda i: (i // tps, 0, 0))
    sd = jax.ShapeDtypeStruct
    return pl.pallas_call(
        body, grid=(T // tm,), name="inproj_bwd",
        in_specs=[rows(D), rows(D), _resident((1, D)), perb, perb, _resident((D_PROJ, D)),
                  rows(1024), rows(D_CONV), rows(Q_LORA), rows(KV_LORA), rows(LANES), rows(LANES)],
        out_specs=[rows(D), rows(D), rows(D_PROJ), perb, perb, pl.BlockSpec((1, D), lambda i: (0, 0))],
        out_shape=[sd((T, D), F32), sd((T, D), BF16), sd((T, D_PROJ), BF16), sd((B, 1, D), F32), sd((B, 1, D), F32),
                   sd((1, D), F32)],
        compiler_params=_cparams(("arbitrary",)),
    )(dx2, x, nw, sh, sc, win, dz, dxbc, dcq, dckv, ddtk_a, ddtk_b)


SUBLANES = 8


def _shift_down(v, k):
    r = pltpu.roll(v, k, 0)
    row = lax.broadcasted_iota(jnp.int32, (SUBLANES, v.shape[1]), 0)
    return jnp.concatenate([jnp.where(row < k, 0.0, r[:SUBLANES]), r[SUBLANES:]], axis=0)


def _shift_up(v, k):
    n = v.shape[0]
    r = pltpu.roll(v, n - k, 0)
    row = lax.broadcasted_iota(jnp.int32, (SUBLANES, v.shape[1]), 0)
    return jnp.concatenate([r[:n - SUBLANES], jnp.where(row >= SUBLANES - k, 0.0, r[n - SUBLANES:])], axis=0)


def _conv_pre(xv, w_ref, b_ref):
    pre = b_ref[...] + w_ref[CONV_WIDTH - 1:CONV_WIDTH, :] * xv
    for k in range(1, CONV_WIDTH):
        pre = pre + w_ref[CONV_WIDTH - 1 - k:CONV_WIDTH - k, :] * _shift_down(xv, k)
    return pre


def _conv_fwd(xraw, cw, cb):
    B, S, C = xraw.shape

    def body(x_ref, w_ref, b_ref, o_ref):
        pre = _conv_pre(x_ref[0].astype(F32), w_ref, b_ref)
        o_ref[0] = (pre * _sigmoid(pre)).astype(BF16)

    blk = pl.BlockSpec((1, S, LANES), lambda b, j: (b, 0, j))
    return pl.pallas_call(
        body, grid=(B, C // LANES), name="conv_fwd",
        in_specs=[blk, pl.BlockSpec((CONV_WIDTH, LANES), lambda b, j: (0, j)), pl.BlockSpec((1, LANES), lambda b, j: (0, j))],
        out_specs=blk, out_shape=jax.ShapeDtypeStruct((B, S, C), BF16),
        compiler_params=_cparams(("arbitrary", "arbitrary")),
    )(xraw, cw, cb)


def _conv_bwd(dout, xraw, cw, cb):
    B, S, C = xraw.shape

    def body(d_ref, x_ref, w_ref, b_ref, dx_ref, dw_ref, db_ref):
        @pl.when(pl.program_id(1) == 0)
        def _():
            dw_ref[...] = jnp.zeros_like(dw_ref)
            db_ref[...] = jnp.zeros_like(db_ref)

        xv = x_ref[0].astype(F32)
        pre = _conv_pre(xv, w_ref, b_ref)
        sig = _sigmoid(pre)
        dpre = d_ref[0].astype(F32) * (sig * (1.0 + pre * (1.0 - sig)))
        dx = w_ref[CONV_WIDTH - 1:CONV_WIDTH, :] * dpre
        for k in range(1, CONV_WIDTH):
            dx = dx + w_ref[CONV_WIDTH - 1 - k:CONV_WIDTH - k, :] * _shift_up(dpre, k)
        dx_ref[0] = dx.astype(BF16)
        db_ref[...] += _colsum(dpre)
        dws = [_colsum(dpre * (xv if k == 0 else _shift_down(xv, k))) for k in range(CONV_WIDTH - 1, -1, -1)]
        dw_ref[...] += jnp.concatenate(dws, axis=0)

    blk = pl.BlockSpec((1, S, LANES), lambda j, b: (b, 0, j))
    wspec = pl.BlockSpec((CONV_WIDTH, LANES), lambda j, b: (0, j))
    bspec = pl.BlockSpec((1, LANES), lambda j, b: (0, j))
    return pl.pallas_call(
        body, grid=(C // LANES, B), name="conv_bwd",
        in_specs=[blk, blk, wspec, bspec], out_specs=[blk, wspec, bspec],
        out_shape=[jax.ShapeDtypeStruct((B, S, C), BF16), jax.ShapeDtypeStruct((CONV_WIDTH, C), F32),
                   jax.ShapeDtypeStruct((1, C), F32)],
        compiler_params=_cparams(("arbitrary", "arbitrary")),
    )(dout, xraw, cw, cb)


def _softplus(x):
    return jnp.maximum(x, 0.0) + jnp.log(1.0 + jnp.exp(-jnp.abs(x)))


def _ssd_common(xbc_ref, dtk_ref, dtb_ref, alog_ref, e_ref):
    L = CHUNK
    xbc = xbc_ref[0]
    xs = xbc[:, :D_SSD].astype(F32)
    bm = xbc[:, D_SSD:D_SSD + 256]
    cm = xbc[:, D_SSD + 256:D_SSD + 512]
    head = lax.broadcasted_iota(jnp.int32, (1, LANES), 1) < SSD_HEADS
    a128 = jnp.where(head, -jnp.exp(alog_ref[...]), 0.0)
    pre = dtk_ref[0] + dtb_ref[...]
    dt = _softplus(pre)
    dA = dt * a128
    row = lax.broadcasted_iota(jnp.int32, (L, L), 0)
    col = lax.broadcasted_iota(jnp.int32, (L, L), 1)
    causal = col <= row
    tri = causal.astype(F32)
    triT = (row <= col).astype(F32)
    tri = causal.astype(BF16)
    triT = (row <= col).astype(BF16)
    dA3 = _split3(dA)
    acum = _sum3(lambda part: _dot(tri, part), dA3)
    acumT = _sum3(lambda part: _dot_tn(part, triT), dA3)
    E = e_ref[...]
    acum_f = _spread(acum, E)
    dt_f = _spread(dt, E)
    e_f = jnp.exp(acum_f)
    w_f = jnp.exp(acum_f[L - 1:L, :] - acum_f)
    xt = xs * dt_f
    return dict(xs=xs, bm=bm, cm=cm, a128=a128, pre=pre, dt=dt, causal=causal, tri=tri, triT=triT, acum=acum,
                acumT=acumT, E=E, dt_f=dt_f, e_f=e_f, w_f=w_f, xt=xt, head=head)


def _split3(x):
    p1 = x.astype(BF16)
    r1 = x - p1.astype(F32)
    p2 = r1.astype(BF16)
    return p1, p2, (r1 - p2.astype(F32)).astype(BF16)


def _sum3(mm, parts):
    return (mm(parts[0]) + mm(parts[1])) + mm(parts[2])


def _spread(v, e):
    return _sum3(lambda part: _dot(part, e), _split3(v))


def _gather_heads(v, e):
    return _sum3(lambda part: _dot_nt(part, e), _split3(v))


def _head_mask(k):
    lane = lax.broadcasted_iota(jnp.int32, (CHUNK, LANES), 1)
    return (lane >= SSD_HEAD_DIM) if k == 1 else (lane < SSD_HEAD_DIM)


def _pair_decay(alast, h0):
    row = lax.broadcasted_iota(jnp.int32, (2 * SSD_HEAD_DIM, SSD_STATE), 0)
    return jnp.exp(jnp.where(row < SSD_HEAD_DIM, alast[:, h0:h0 + 1], alast[:, h0 + 1:h0 + 2]))


def _decay_matrix(q, h):
    seg = q["acum"][:, h:h + 1] - q["acumT"][h:h + 1, :]
    return jnp.exp(jnp.where(q["causal"], seg, -1e30))


def _gated_norm(y, zz, nw):
    sig = _sigmoid(zz)
    sil = zz * sig
    yg = y * sil
    half = D_SSD // SSD_GROUPS
    parts = []
    for g in range(SSD_GROUPS):
        xh, r = _rms_stats(yg[:, g * half:(g + 1) * half])
        parts.append((xh, r))
    return sig, sil, parts


def _ssd_fwd(xbc, dtk, z, dtb, alog, dsk, nw, expand):
    B, S, _ = xbc.shape
    L = CHUNK
    nc = S // L

    def body(xbc_ref, dtk_ref, z_ref, dtb_ref, alog_ref, dsk_ref, nw_ref, e_ref, y_ref, ys_ref, prev_ref, st_ref):
        @pl.when(pl.program_id(0) == 0)
        def _():
            st_ref[...] = jnp.zeros_like(st_ref)

        for b in range(B):
            one = lambda ref: ref.at[pl.ds(b, 1)]
            sequence_step(one(xbc_ref), one(dtk_ref), one(z_ref), dtb_ref, alog_ref, dsk_ref, nw_ref, e_ref, one(y_ref),
                          one(ys_ref), one(prev_ref), st_ref.at[b])

    def sequence_step(xbc_ref, dtk_ref, z_ref, dtb_ref, alog_ref, dsk_ref, nw_ref, e_ref, y_ref, ys_ref, prev_ref, st_ref):
        q = _ssd_common(xbc_ref, dtk_ref, dtb_ref, alog_ref, e_ref)
        xtb = q["xt"].astype(BF16)
        xwb = (q["xt"] * q["w_f"]).astype(BF16)
        alast = q["acum"][L - 1:L, :]
        ys = []
        for g in range(SSD_GROUPS):
            bg = q["bm"][:, g * 128:(g + 1) * 128]
            cg = q["cm"][:, g * 128:(g + 1) * 128]
            G = _dot_nt(cg, bg)
            for pr in range(SSD_HEADS // SSD_GROUPS // 2):
                h0 = g * 8 + 2 * pr
                lo = h0 * SSD_HEAD_DIM
                xt_p = xtb[:, lo:lo + 128]
                ydiag = jnp.zeros((L, LANES), F32)
                for k in range(2):
                    M = (G * _decay_matrix(q, h0 + k)).astype(BF16)
                    ydiag = ydiag + _dot(M, jnp.where(_head_mask(k), xt_p, jnp.zeros_like(xt_p)))
                hp = st_ref[lo:lo + 128, :]
                prev_ref[0, 0, lo:lo + 128, :] = hp.astype(BF16)
                zoff = _dot_nt(cg, hp.astype(BF16))
                ys.append(ydiag + zoff * q["e_f"][:, lo:lo + 128])
                st_ref[lo:lo + 128, :] = _pair_decay(alast, h0) * hp + _dot_tn(xwb[:, lo:lo + 128], bg)
        y = jnp.concatenate(ys, axis=1) + dsk_ref[...] * q["xs"]
        y_ref[0] = y.astype(BF16)
        _, _, parts = _gated_norm(y, z_ref[0].astype(F32), nw_ref[...])
        half = D_SSD // SSD_GROUPS
        ys_ref[0] = jnp.concatenate(
            [xh * nw_ref[:, g * half:(g + 1) * half] for g, (xh, _) in enumerate(parts)], axis=1).astype(BF16)

    chunk = lambda n: pl.BlockSpec((B, L, n), lambda c: (0, c, 0))
    vec = pl.BlockSpec((1, LANES), lambda c: (0, 0))
    return pl.pallas_call(
        body, grid=(nc,), name="ssd_fwd",
        in_specs=[chunk(D_CONV), chunk(LANES), chunk(D_SSD), vec, vec, pl.BlockSpec((1, D_SSD), lambda c: (0, 0)),
                  pl.BlockSpec((1, D_SSD), lambda c: (0, 0)), pl.BlockSpec((LANES, D_SSD), lambda c: (0, 0))],
        out_specs=[chunk(D_SSD), chunk(D_SSD), pl.BlockSpec((B, 1, D_SSD, SSD_STATE), lambda c: (0, c, 0, 0))],
        out_shape=[jax.ShapeDtypeStruct((B, S, D_SSD), BF16), jax.ShapeDtypeStruct((B, S, D_SSD), BF16),
                   jax.ShapeDtypeStruct((B, nc, D_SSD, SSD_STATE), BF16)],
        scratch_shapes=[pltpu.VMEM((B, D_SSD, SSD_STATE), F32)],
        compiler_params=_cparams(("arbitrary",)),
    )(xbc, dtk, z, dtb, alog, dsk, nw, expand)


def _ssd_bwd(xbc, dtk, z, y, prev, dys, dtb, alog, dsk, nw, expand):
    B, S, _ = xbc.shape
    L = CHUNK
    nc = S // L
    half = D_SSD // SSD_GROUPS

    def body(xbc_ref, dtk_ref, z_ref, y_ref, prev_ref, dys_ref, dtb_ref, alog_ref, dsk_ref, nw_ref, e_ref,
             dxbc_ref, ddtk_ref, dz_ref, dnw_ref, dvec_ref, dh_ref, dskc_ref):
        @pl.when(pl.program_id(0) == 0)
        def _():
            dnw_ref[...] = jnp.zeros_like(dnw_ref)
            dvec_ref[...] = jnp.zeros_like(dvec_ref)
            dskc_ref[...] = jnp.zeros_like(dskc_ref)
            dh_ref[...] = jnp.zeros_like(dh_ref)

        for b in range(B):
            one = lambda ref: ref.at[pl.ds(b, 1)]
            sequence_step(one(xbc_ref), one(dtk_ref), one(z_ref), one(y_ref), one(prev_ref), one(dys_ref), dtb_ref, alog_ref,
                          dsk_ref, nw_ref, e_ref, one(dxbc_ref), one(ddtk_ref), one(dz_ref), dnw_ref, dvec_ref, dh_ref.at[b],
                          dskc_ref)

        @pl.when(pl.program_id(0) == nc - 1)
        def _():
            dvec_ref[2:3, :] = _gather_heads(jnp.broadcast_to(dskc_ref[...], (8, D_SSD)), e_ref[...])[0:1, :]

    def sequence_step(xbc_ref, dtk_ref, z_ref, y_ref, prev_ref, dys_ref, dtb_ref, alog_ref, dsk_ref, nw_ref, e_ref,
                      dxbc_ref, ddtk_ref, dz_ref, dnw_ref, dvec_ref, dh_ref, dskc_ref):
        q = _ssd_common(xbc_ref, dtk_ref, dtb_ref, alog_ref, e_ref)
        E = q["E"]
        xs = q["xs"]
        yv = y_ref[0].astype(F32)
        zz = z_ref[0].astype(F32)
        sig, sil, parts = _gated_norm(yv, zz, nw_ref[...])
        dn = dys_ref[0].astype(F32)
        dyg, dnw_rows = [], []
        for g, (xh, r) in enumerate(parts):
            dpart, dw_rows = _rms_bwd(dn[:, g * half:(g + 1) * half], xh, r, nw_ref[:, g * half:(g + 1) * half])
            dyg.append(dpart)
            dnw_rows.append(dw_rows)
        dyg = jnp.concatenate(dyg, axis=1)
        dnw_ref[...] += _colsum(jnp.concatenate(dnw_rows, axis=1))
        dY = dyg * sil
        dz_ref[0] = (dyg * yv * (sig * (1.0 + zz * (1.0 - sig)))).astype(BF16)
        dsk_f = dsk_ref[...]
        dskc_ref[...] += _colsum(dY * xs)
        dYb = dY.astype(BF16)
        xtb = q["xt"].astype(BF16)
        xwb = (q["xt"] * q["w_f"]).astype(BF16)
        acum = q["acum"]
        alast = acum[L - 1:L, :]
        lane_id = lax.broadcasted_iota(jnp.int32, (L, LANES), 1)
        sub_id = lax.broadcasted_iota(jnp.int32, (LANES, L), 0)
        lane_row = lax.broadcasted_iota(jnp.int32, (1, LANES), 1)
        da_rows = jnp.zeros((L, LANES), F32)
        daT = jnp.zeros((LANES, L), F32)
        dxt, prod_off, prod_st, dbs, dcs = [], [], [], [], []
        hsum_row = jnp.zeros((1, LANES), F32)
        for g in range(SSD_GROUPS):
            bg = q["bm"][:, g * 128:(g + 1) * 128]
            cg = q["cm"][:, g * 128:(g + 1) * 128]
            G = _dot_nt(cg, bg)
            dG = jnp.zeros((L, L), F32)
            dcg = jnp.zeros((L, SSD_STATE), F32)
            dbg = jnp.zeros((L, SSD_STATE), F32)
            for pr in range(SSD_HEADS // SSD_GROUPS // 2):
                h0 = g * 8 + 2 * pr
                lo = h0 * SSD_HEAD_DIM
                cols = slice(lo, lo + 128)
                dY_p = dYb[:, cols]
                xt_p = xtb[:, cols]
                dxt_p = jnp.zeros((L, LANES), F32)
                for k in range(2):
                    h = h0 + k
                    Lm = _decay_matrix(q, h)
                    Mf = G * Lm
                    dYk = jnp.where(_head_mask(k), dY_p, jnp.zeros_like(dY_p))
                    dM = _dot_nt(dYk, xt_p)
                    dxt_p = dxt_p + _dot_tn(Mf.astype(BF16), dYk)
                    dG = dG + dM * Lm
                    Q = dM * Mf
                    da_rows = da_rows + jnp.where(lane_id == h, jnp.sum(Q, axis=1, keepdims=True), 0.0)
                    daT = daT + jnp.where(sub_id == h, jnp.sum(Q, axis=0, keepdims=True), 0.0)
                hpb = prev_ref[0, 0, lo:lo + 128, :]
                hp = hpb.astype(F32)
                zoff = _dot_nt(cg, hpb)
                e_p = q["e_f"][:, cols]
                dY_pf = dY[:, cols]
                dZb = (dY_pf * e_p).astype(BF16)
                dcg = dcg + _dot(dZb, hpb)
                dhp_off = _dot_tn(dZb, cg)
                prod_off.append(dY_pf * zoff * e_p)
                dS = dh_ref[lo:lo + 128, :]
                dSb = dS.astype(BF16)
                U = _dot_nt(bg, dSb)
                dxt_p = dxt_p + U * q["w_f"][:, cols]
                dbg = dbg + _dot(xwb[:, cols], dSb)
                prod_st.append(q["xt"][:, cols] * U)
                dh_ref[lo:lo + 128, :] = _pair_decay(alast, h0) * dS + dhp_off
                dsh = dS * hp
                for k in range(2):
                    total = jnp.sum(dsh[k * SSD_HEAD_DIM:(k + 1) * SSD_HEAD_DIM, :], axis=(0, 1), keepdims=True)
                    hsum_row = hsum_row + jnp.where(lane_row == h0 + k, total, 0.0)
                dxt.append(dxt_p)
            dGb = dG.astype(BF16)
            dcs.append(dcg + _dot(dGb, bg))
            dbs.append(dbg + _dot_tn(dGb, cg))
        dxt = jnp.concatenate(dxt, axis=1)
        da_rows = da_rows + _gather_heads(jnp.concatenate(prod_off, axis=1), E)
        dww = _gather_heads(jnp.concatenate(prod_st, axis=1), E) * jnp.exp(alast - acum)
        da_rows = da_rows - dww
        dlast = _colsum(dww) + jnp.exp(alast) * hsum_row
        triT = q["triT"]
        ddA = (_sum3(lambda part: _dot(triT, part), _split3(da_rows))
               - _sum3(lambda part: _dot_nt(triT, part), _split3(daT)) + dlast)
        ddA = jnp.where(q["head"], ddA, 0.0)
        ddt = ddA * q["a128"] + _gather_heads(dxt * xs, E)
        ddt_raw = jnp.where(q["head"], ddt * _sigmoid(q["pre"]), 0.0)
        ddtk_ref[0] = ddt_raw
        dxs = dxt * q["dt_f"] + dsk_f * dY
        dxbc_ref[0] = jnp.concatenate([dxs] + dbs + dcs, axis=1).astype(BF16)
        dvec_ref[0:1, :] += _colsum(ddt_raw)
        dvec_ref[1:2, :] += _colsum(ddA * q["dt"]) * q["a128"]

    rev = lambda n: pl.BlockSpec((B, L, n), lambda c: (0, nc - 1 - c, 0))
    vec = pl.BlockSpec((1, LANES), lambda c: (0, 0))
    sd = jax.ShapeDtypeStruct
    return pl.pallas_call(
        body, grid=(nc,), name="ssd_bwd",
        in_specs=[rev(D_CONV), rev(LANES), rev(D_SSD), rev(D_SSD),
                  pl.BlockSpec((B, 1, D_SSD, SSD_STATE), lambda c: (0, nc - 1 - c, 0, 0)), rev(D_SSD), vec, vec,
                  pl.BlockSpec((1, D_SSD), lambda c: (0, 0)),
                  pl.BlockSpec((1, D_SSD), lambda c: (0, 0)), pl.BlockSpec((LANES, D_SSD), lambda c: (0, 0))],
        out_specs=[rev(D_CONV), rev(LANES), rev(D_SSD), pl.BlockSpec((1, D_SSD), lambda c: (0, 0)),
                   pl.BlockSpec((8, LANES), lambda c: (0, 0))],
        out_shape=[sd((B, S, D_CONV), BF16), sd((B, S, LANES), F32), sd((B, S, D_SSD), BF16), sd((1, D_SSD), F32),
                   sd((8, LANES), F32)],
        scratch_shapes=[pltpu.VMEM((B, D_SSD, SSD_STATE), F32), pltpu.VMEM((1, D_SSD), F32)],
        compiler_params=_cparams(("arbitrary",)),
    )(xbc, dtk, z, y, prev, dys, dtb, alog, dsk, nw, expand)


def _rope_tables(pos_ref, invf_ref, place_ref):
    ang = invf_ref[...] * pos_ref[0].astype(F32)
    place = place_ref[...]
    cosf = 1.0 + _sum3(lambda part: _dot_tn(part, place), _split3(jnp.cos(ang) - 1.0))
    sinf = _sum3(lambda part: _dot_tn(part, place), _split3(jnp.sin(ang)))
    return cosf, sinf


def _rot(u):
    lane = lax.broadcasted_iota(jnp.int32, u.shape, 1)
    first = (lane >= QK_NOPE) & (lane < QK_NOPE + QK_ROPE // 2)
    second = (lane >= QK_NOPE + QK_ROPE // 2) & (lane < QK_DIM)
    return jnp.where(first, -pltpu.roll(u, LANES - QK_ROPE // 2, 1), jnp.where(second, pltpu.roll(u, QK_ROPE // 2, 1), 0.0))


def _rope_lanes(shape):
    lane = lax.broadcasted_iota(jnp.int32, shape, 1)
    return (lane >= QK_NOPE) & (lane < QK_DIM)


def _mla_prep(cq, ckv, dtk, pos, qw, kvw, wuq, wukv, invf, place):
    T = cq.shape[0]
    tm = min(WIDE_TOKEN_TILE, T)
    scale = 1.0 / math.sqrt(QK_DIM)
    HW = MLA_HEADS * HEAD_LANES

    def body(cq_ref, ckv_ref, dtk_ref, pos_ref, qw_ref, kvw_ref, wuq_ref, wukv_ref, invf_ref, place_ref, q_ref, k_ref, v_ref,
             cos_ref, sin_ref):
        xh, _ = _rms_stats(cq_ref[...])
        qv = _dot((xh * qw_ref[...]).astype(BF16), wuq_ref[...])
        xh, _ = _rms_stats(ckv_ref[...])
        kv = _dot((xh * kvw_ref[...]).astype(BF16), wukv_ref[...])
        cosf, sinf = _rope_tables(pos_ref, invf_ref, place_ref)
        cos_ref[...] = cosf
        sin_ref[...] = sinf
        rope = lambda u: u * cosf + _rot(u) * sinf
        dtkv = dtk_ref[...]
        kr = rope(jnp.where(_rope_lanes(dtkv.shape), dtkv, 0.0))
        for h in range(MLA_HEADS):
            cols = slice(h * HEAD_LANES, (h + 1) * HEAD_LANES)
            q_ref[:, cols] = (rope(qv[:, cols]) * scale).astype(BF16)
            k_ref[:, cols] = (kv[:, cols] + kr).astype(BF16)
        v_ref[...] = kv[:, HW:].astype(BF16)

    rows = lambda n: pl.BlockSpec((tm, n), lambda i: (i, 0))
    return pl.pallas_call(
        body, grid=(T // tm,), name="mla_prep",
        in_specs=[rows(Q_LORA), rows(KV_LORA), rows(LANES), pl.BlockSpec((1, 1, tm), lambda i: (i, 0, 0)),
                  _resident((1, Q_LORA)), _resident((1, KV_LORA)), _resident((Q_LORA, HW)), _resident((KV_LORA, 2 * HW)),
                  _resident((QK_ROPE // 2, 1)), _resident((QK_ROPE // 2, LANES))],
        out_specs=[rows(HW), rows(HW), rows(HW), rows(LANES), rows(LANES)],
        out_shape=[jax.ShapeDtypeStruct((T, HW), BF16)] * 3 + [jax.ShapeDtypeStruct((T, LANES), F32)] * 2,
        compiler_params=_cparams(("arbitrary",)),
    )(cq, ckv, dtk, pos.reshape(T // tm, 1, tm), qw, kvw, wuq, wukv, invf, place)


def _mla_prep_bwd(dq, dk, dv, cq, ckv, cos_t, sin_t, qw, kvw, wuq, wukv):
    T = cq.shape[0]
    tm = min(WIDE_TOKEN_TILE, T)
    scale = 1.0 / math.sqrt(QK_DIM)
    HW = MLA_HEADS * HEAD_LANES

    def body(dq_ref, dk_ref, dv_ref, cq_ref, ckv_ref, cos_ref, sin_ref, qw_ref, kvw_ref, wuq_ref, wukv_ref,
             dcq_ref, dckv_ref, ddtk_ref, qn_ref, kvn_ref, dqo_ref, dkvo_ref, dqw_ref, dkvw_ref):
        @pl.when(pl.program_id(0) == 0)
        def _():
            dqw_ref[...] = jnp.zeros_like(dqw_ref)
            dkvw_ref[...] = jnp.zeros_like(dkvw_ref)

        cosf, sinf = cos_ref[...], sin_ref[...]
        unrope = lambda d: d * cosf - _rot(d * sinf)
        dkr = jnp.zeros((tm, LANES), F32)
        nope = lax.broadcasted_iota(jnp.int32, (tm, LANES), 1) < QK_NOPE
        for h in range(MLA_HEADS):
            cols = slice(h * HEAD_LANES, (h + 1) * HEAD_LANES)
            dqo_ref[:, cols] = unrope(dq_ref[:, cols].astype(F32) * scale).astype(BF16)
            dkh = dk_ref[:, cols].astype(F32)
            dkr = dkr + jnp.where(_rope_lanes(dkh.shape), dkh, 0.0)
            dkvo_ref[:, cols] = jnp.where(nope, dkh, 0.0).astype(BF16)
        dkvo_ref[:, HW:] = dv_ref[...].astype(BF16)
        ddtk_ref[...] = unrope(dkr)
        xh, r = _rms_stats(cq_ref[...])
        qn_ref[...] = (xh * qw_ref[...]).astype(BF16)
        dx, dw_rows = _rms_bwd(_dot_nt(dqo_ref[...], wuq_ref[...]), xh, r, qw_ref[...])
        dcq_ref[...] = dx
        dqw_ref[...] += _colsum(dw_rows)
        xh, r = _rms_stats(ckv_ref[...])
        kvn_ref[...] = (xh * kvw_ref[...]).astype(BF16)
        dx, dw_rows = _rms_bwd(_dot_nt(dkvo_ref[...], wukv_ref[...]), xh, r, kvw_ref[...])
        dckv_ref[...] = dx
        dkvw_ref[...] += _colsum(dw_rows)

    rows = lambda n: pl.BlockSpec((tm, n), lambda i: (i, 0))
    sd = jax.ShapeDtypeStruct
    return pl.pallas_call(
        body, grid=(T // tm,), name="mla_prep_bwd",
        in_specs=[rows(HW), rows(HW), rows(HW), rows(Q_LORA), rows(KV_LORA), rows(LANES), rows(LANES), _resident((1, Q_LORA)),
                  _resident((1, KV_LORA)), _resident((Q_LORA, HW)), _resident((KV_LORA, 2 * HW))],
        out_specs=[rows(Q_LORA), rows(KV_LORA), rows(LANES), rows(Q_LORA), rows(KV_LORA), rows(HW), rows(2 * HW),
                   pl.BlockSpec((1, Q_LORA), lambda i: (0, 0)), pl.BlockSpec((1, KV_LORA), lambda i: (0, 0))],
        out_shape=[sd((T, Q_LORA), F32), sd((T, KV_LORA), F32), sd((T, LANES), F32), sd((T, Q_LORA), BF16),
                   sd((T, KV_LORA), BF16), sd((T, HW), BF16), sd((T, 2 * HW), BF16), sd((1, Q_LORA), F32),
                   sd((1, KV_LORA), F32)],
        compiler_params=_cparams(("arbitrary",)),
    )(dq, dk, dv, cq, ckv, cos_t, sin_t, qw, kvw, wuq, wukv)


def _causal_mask(t):
    row = lax.broadcasted_iota(jnp.int32, (t, t), 0)
    col = lax.broadcasted_iota(jnp.int32, (t, t), 1)
    return col <= row


def _attn_fwd(q, k, v):
    B, S, HW = q.shape
    H = HW // HEAD_LANES
    t = min(ATTN_FWD_Q_TILE, S)
    tk = min(ATTN_FWD_KV_TILE, t)
    nq = S // t
    per = t // tk

    pair = 2
    pw = pair * HEAD_LANES

    def body(q_ref, k_ref, v_ref, o_ref, lse_ref):
        qi = pl.program_id(2)
        lanes = [slice(hh * HEAD_LANES, (hh + 1) * HEAD_LANES) for hh in range(pair)]
        qs = [q_ref[0, :, cols] for cols in lanes]

        def step(j, carry, diag):
            sl = pl.ds(pl.multiple_of(j * tk, tk), tk)
            out = []
            for qv, cols, (m, l, acc) in zip(qs, lanes, carry):
                s = _dot_nt(qv, k_ref[0, sl, cols])
                if diag is not None:
                    row = lax.broadcasted_iota(jnp.int32, (t, tk), 0)
                    col = lax.broadcasted_iota(jnp.int32, (t, tk), 1)
                    s = jnp.where(col + diag * tk <= row, s, -1e30)
                m_new = jnp.maximum(m, jnp.max(s, axis=-1, keepdims=True))
                alpha = jnp.exp(m - m_new)
                p = jnp.exp(s - m_new)
                l = alpha * l + jnp.sum(p, axis=-1, keepdims=True)
                acc = alpha * acc + _dot(p.astype(BF16), v_ref[0, sl, cols])
                out.append((m_new, l, acc))
            return tuple(out)

        init = tuple((jnp.full((t, 1), -1e30, F32), jnp.zeros((t, 1), F32), jnp.zeros((t, HEAD_LANES), F32))
                     for _ in range(pair))
        carry = lax.fori_loop(0, qi * per, lambda j, c: step(j, c, None), init)
        for d in range(per):
            carry = step(qi * per + d, carry, d)
        for hh, (m, l, acc) in enumerate(carry):
            o_ref[0, :, lanes[hh]] = (acc / l).astype(BF16)
            lse_ref[0, hh] = m + jnp.log(l)

    return pl.pallas_call(
        body, grid=(B, H // pair, nq), name="attn_fwd",
        in_specs=[pl.BlockSpec((1, t, pw), lambda b, h, i: (b, i, h)),
                  pl.BlockSpec((1, S, pw), lambda b, h, i: (b, 0, h)),
                  pl.BlockSpec((1, S, pw), lambda b, h, i: (b, 0, h))],
        out_specs=[pl.BlockSpec((1, t, pw), lambda b, h, i: (b, i, h)),
                   pl.BlockSpec((1, pair, t, 1), lambda b, h, i: (b, h, i, 0))],
        out_shape=[jax.ShapeDtypeStruct((B, S, HW), BF16), jax.ShapeDtypeStruct((B, H, S, 1), F32)],
        compiler_params=_cparams(("arbitrary", "arbitrary", "arbitrary")),
    )(q, k, v)


def _attn_bwd(q, k, v, o, do, lse):
    B, S, HW = q.shape
    H = HW // HEAD_LANES
    t = min(ATTN_BWD_TILE, S)
    nq = S // t

    def body(q_ref, k_ref, v_ref, o_ref, do_ref, lse_ref, dq_out_ref, dk_ref, dv_ref, dq_ref):
        j = pl.program_id(2)

        @pl.when(j == 0)
        def _():
            dq_ref[...] = jnp.zeros_like(dq_ref)

        kj = k_ref[0]
        vj = v_ref[0]

        def step(i, carry, masked):
            dk, dv = carry
            sl = pl.ds(pl.multiple_of(i * t, t), t)
            qi = q_ref[0, sl, :]
            doi = do_ref[0, sl, :]
            s = _dot_nt(qi, kj)
            if masked:
                s = jnp.where(_causal_mask(t), s, -1e30)
            p = jnp.exp(s - lse_ref[0, 0, sl, :])
            dv = dv + _dot_tn(p.astype(BF16), doi)
            dp = _dot_nt(doi, vj)
            delta = jnp.sum(doi.astype(F32) * o_ref[0, sl, :].astype(F32), axis=-1, keepdims=True)
            dsb = (p * (dp - delta)).astype(BF16)
            dk = dk + _dot_tn(dsb, qi)
            dq_ref[sl, :] += _dot(dsb, kj)
            return dk, dv

        zero = jnp.zeros((t, HEAD_LANES), F32)
        carry = step(j, (zero, zero), True)
        dk, dv = lax.fori_loop(j + 1, nq, lambda i, c: step(i, c, False), carry)
        dk_ref[0] = dk.astype(BF16)
        dv_ref[0] = dv.astype(BF16)

        @pl.when(j == nq - 1)
        def _():
            dq_out_ref[0] = dq_ref[...].astype(BF16)

    full = pl.BlockSpec((1, S, HEAD_LANES), lambda b, h, j: (b, 0, h))
    tile = pl.BlockSpec((1, t, HEAD_LANES), lambda b, h, j: (b, j, h))
    sd = jax.ShapeDtypeStruct
    return pl.pallas_call(
        body, grid=(B, H, nq), name="attn_bwd",
        in_specs=[full, tile, tile, full, full, pl.BlockSpec((1, 1, S, 1), lambda b, h, j: (b, h, 0, 0))],
        out_specs=[full, tile, tile],
        out_shape=[sd((B, S, HW), BF16), sd((B, S, HW), BF16), sd((B, S, HW), BF16)],
        scratch_shapes=[pltpu.VMEM((S, HEAD_LANES), F32)],
        compiler_params=_cparams(("arbitrary", "arbitrary", "arbitrary")),
    )(q, k, v, o, do, lse)


def _mix_out(x1, yssd, o, mw, wout, g, seq):
    T, D = x1.shape
    tm = min(WIDE_TOKEN_TILE, seq)
    tps = seq // tm

    def body(x_ref, ys_ref, o_ref, mw_ref, w_ref, g_ref, xo_ref, m_ref, yc_ref):
        xh, _ = _rms_stats(o_ref[...].astype(F32))
        ycat = jnp.concatenate([ys_ref[...], (xh * mw_ref[...]).astype(BF16)], axis=1)
        m = _dot(ycat, w_ref[...])
        xo_ref[...] = x_ref[...] + g_ref[0] * m
        m_ref[...] = m.astype(BF16)
        yc_ref[...] = ycat

    rows = lambda n: pl.BlockSpec((tm, n), lambda i: (i, 0))
    perb = pl.BlockSpec((1, 1, D), lambda i: (i // tps, 0, 0))
    sd = jax.ShapeDtypeStruct
    return pl.pallas_call(
        body, grid=(T // tm,), name="mix_out",
        in_specs=[rows(D), rows(D_SSD), rows(D_MLA), _resident((1, D_MLA)), _resident((D_SSD + D_MLA, D)), perb],
        out_specs=[rows(D), rows(D), rows(D_SSD + D_MLA)],
        out_shape=[sd((T, D), F32), sd((T, D), BF16), sd((T, D_SSD + D_MLA), BF16)],
        compiler_params=_cparams(("arbitrary",)),
    )(x1, yssd, o, mw, wout, g)


def _mix_out_bwd(dx2, m, o, mw, wout, g, seq):
    T, D = dx2.shape
    B = T // seq
    tm = min(WIDE_TOKEN_TILE, seq)
    tps = seq // tm

    def body(dx_ref, m_ref, o_ref, mw_ref, w_ref, g_ref, dys_ref, do_ref, dm_ref, dg_ref, dmw_ref):
        i = pl.program_id(0)

        @pl.when(i % tps == 0)
        def _():
            dg_ref[...] = jnp.zeros_like(dg_ref)

        @pl.when(i == 0)
        def _():
            dmw_ref[...] = jnp.zeros_like(dmw_ref)

        dxv = dx_ref[...]
        dg_ref[0] += _colsum(dxv * m_ref[...].astype(F32))
        dmb = (g_ref[0] * dxv).astype(BF16)
        dm_ref[...] = dmb
        dycat = _dot_nt(dmb, w_ref[...])
        dys_ref[...] = dycat[:, :D_SSD].astype(BF16)
        xh, r = _rms_stats(o_ref[...].astype(F32))
        dx, dw_rows = _rms_bwd(dycat[:, D_SSD:], xh, r, mw_ref[...])
        do_ref[...] = dx.astype(BF16)
        dmw_ref[...] += _colsum(dw_rows)

    rows = lambda n: pl.BlockSpec((tm, n), lambda i: (i, 0))
    perb = pl.BlockSpec((1, 1, D), lambda i: (i // tps, 0, 0))
    sd = jax.ShapeDtypeStruct
    return pl.pallas_call(
        body, grid=(T // tm,), name="mix_out_bwd",
        in_specs=[rows(D), rows(D), rows(D_MLA), _resident((1, D_MLA)), _resident((D_SSD + D_MLA, D)), perb],
        out_specs=[rows(D_SSD), rows(D_MLA), rows(D), perb, pl.BlockSpec((1, D_MLA), lambda i: (0, 0))],
        out_shape=[sd((T, D_SSD), BF16), sd((T, D_MLA), BF16), sd((T, D), BF16), sd((B, 1, D), F32), sd((1, D_MLA), F32)],
        compiler_params=_cparams(("arbitrary",)),
    )(dx2, m, o, mw, wout, g)


def _win_to_kernel(w):
    z0 = jnp.zeros((48, w.shape[1]), w.dtype)
    z1 = jnp.zeros((32, w.shape[1]), w.dtype)
    return jnp.concatenate([w[:2560], w[2576:3216], w[2560:2576], z0, w[3216:3248], z1], axis=0)


def _win_from_kernel(g):
    return jnp.concatenate([g[:2560], g[3200:3216], g[2560:3200], g[3264:3296]], axis=0)


def _wuq_to_kernel(w):
    w = w.reshape(Q_LORA, MLA_HEADS, QK_DIM)
    return jnp.pad(w, ((0, 0), (0, 0), (0, HEAD_LANES - QK_DIM))).reshape(Q_LORA, MLA_HEADS * HEAD_LANES)


def _wuq_from_kernel(g):
    return g.reshape(Q_LORA, MLA_HEADS, HEAD_LANES)[:, :, :QK_DIM].reshape(Q_LORA, MLA_HEADS * QK_DIM)


def _wukv_to_kernel(w):
    w = w.reshape(KV_LORA, MLA_HEADS, QK_NOPE + V_HEAD)
    kp = jnp.pad(w[:, :, :QK_NOPE], ((0, 0), (0, 0), (0, HEAD_LANES - QK_NOPE)))
    return jnp.concatenate([kp.reshape(KV_LORA, -1), w[:, :, QK_NOPE:].reshape(KV_LORA, -1)], axis=1)


def _wukv_from_kernel(g):
    hw = MLA_HEADS * HEAD_LANES
    kp = g[:, :hw].reshape(KV_LORA, MLA_HEADS, HEAD_LANES)[:, :, :QK_NOPE]
    vp = g[:, hw:].reshape(KV_LORA, MLA_HEADS, V_HEAD)
    return jnp.concatenate([kp, vp], axis=2).reshape(KV_LORA, MLA_HEADS * (QK_NOPE + V_HEAD))


def _lanes16(v):
    return jnp.pad(v.reshape(1, SSD_HEADS), ((0, 0), (0, LANES - SSD_HEADS)))


def _constants():
    e = np.zeros((LANES, D_SSD), np.float32)
    for h in range(SSD_HEADS):
        e[h, h * SSD_HEAD_DIM:(h + 1) * SSD_HEAD_DIM] = 1.0
    inv_freq = ROPE_THETA ** (-jnp.arange(0, QK_ROPE, 2, dtype=F32) / QK_ROPE)
    half = QK_ROPE // 2
    place = np.zeros((half, LANES), np.float32)
    for j in range(half):
        place[j, QK_NOPE + j] = place[j, QK_NOPE + half + j] = 1.0
    return jnp.asarray(e, BF16), inv_freq.reshape(half, 1), jnp.asarray(place, BF16)


def _local_step(x, positions, mod, w, later_weights, small, tgt, on_grads):
    B, S, D = x.shape
    T = B * S
    expand, invf, place = _constants()
    x0 = x.reshape(T, D)
    pos = positions.reshape(T)
    mods = [mod[:, i * D:(i + 1) * D].reshape(B, 1, D) for i in range(N_MOD)]
    sh1, sc1, g1, sh2, sc2, g2, sh3, sc3, g3 = mods
    dtb, alog = _lanes16(small["dt_bias"]), _lanes16(small["a_log"])
    dsk = jnp.repeat(small["d_skip"].reshape(1, SSD_HEADS), SSD_HEAD_DIM, axis=1)

    x1, a1, u1, f1 = _ffn_fwd(x0, small["norm_ffn1"], sh1, sc1, g1, w["ffn1_w_gate"], w["ffn1_w_up"], w["ffn1_w_down"], S, "ffn1_fwd")
    w = {**w, **later_weights(f1)}
    z, xraw, cq, ckv, dtk = _inproj_fwd(x1, small["norm_mix"], sh2, sc2, w["w_in"], S)
    xraw3 = xraw.reshape(B, S, D_CONV)
    xbc = _conv_fwd(xraw3, small["conv_w"], small["conv_b"])
    dtk3, z3 = dtk.reshape(B, S, LANES), z.reshape(B, S, D_SSD)
    y, yssd, prev = _ssd_fwd(xbc, dtk3, z3, dtb, alog, dsk, small["ssd_norm_w"], expand)
    q, k, v, cos_t, sin_t = _mla_prep(cq, ckv, dtk, pos, small["q_norm_w"], small["kv_norm_w"], w["w_uq"], w["w_ukv"], invf,
                                      place)
    hw = MLA_HEADS * HEAD_LANES
    q3, k3, v3 = q.reshape(B, S, hw), k.reshape(B, S, hw), v.reshape(B, S, hw)
    o3, lse = _attn_fwd(q3, k3, v3)
    o = o3.reshape(T, hw)
    x2, m, ycat = _mix_out(x1, yssd.reshape(T, D_SSD), o, small["mla_norm_w"], w["w_out"], g2, S)
    dx3, a2, u2, f2, loss, d_norm_final = _ffn_fwd(
        x2, small["norm_ffn2"], sh3, sc3, g3, w["ffn2_w_gate"], w["ffn2_w_up"], w["ffn2_w_down"], S, "ffn2_fwd",
        head=(small["norm_final"].reshape(1, D), tgt.reshape(T, D)))

    gw, gs = {}, {}
    dx2, h3, s3, df3, da3, du3, dsh3, dsc3, dg3, gs["norm_ffn2"] = _ffn_bwd(
        dx3, x2, small["norm_ffn2"], sh3, sc3, g3, a2, u2, f2, w["ffn2_w_gate"], w["ffn2_w_up"], w["ffn2_w_down"], S, "ffn2_bwd")
    gw["ffn2_w_gate"], gw["ffn2_w_up"], gw["ffn2_w_down"] = _ffn_wgrad(h3, s3, df3, da3, du3, "ffn2_wgrad")
    g2 = g2 + on_grads(("ffn2_w_gate", "ffn2_w_up", "ffn2_w_down"), gw)

    dys, do, dm, dg2, gs["mla_norm_w"] = _mix_out_bwd(dx2, m, o, small["mla_norm_w"], w["w_out"], g2, S)
    gw["w_out"] = _mm_tn(ycat, dm, 512, "dwout")

    dq3, dk3, dv3 = _attn_bwd(q3, k3, v3, o3, do.reshape(B, S, hw), lse)
    dcq, dckv, ddtk_b, qn, kvn, dqb, dkvb, gs["q_norm_w"], gs["kv_norm_w"] = _mla_prep_bwd(
        dq3.reshape(T, hw), dk3.reshape(T, hw), dv3.reshape(T, hw), cq, ckv, cos_t, sin_t, small["q_norm_w"],
        small["kv_norm_w"], w["w_uq"], w["w_ukv"])
    gw["w_uq"] = _mm_tn(qn, dqb, 512, "dwuq")
    gw["w_ukv"] = _mm_tn(kvn, dkvb, 1024, "dwukv")

    dxbc, ddtk_a, dz, gs["ssd_norm_w"], dvec = _ssd_bwd(
        xbc, dtk3, z3, y, prev, dys.reshape(B, S, D_SSD), dtb, alog, dsk, small["ssd_norm_w"], expand)
    gs["dt_bias"], gs["a_log"], gs["d_skip"] = dvec[0:1, :SSD_HEADS], dvec[1:2, :SSD_HEADS], dvec[2:3, :SSD_HEADS]
    dxraw, gs["conv_w"], gs["conv_b"] = _conv_bwd(dxbc, xraw3, small["conv_w"], small["conv_b"])
    dx1, h2, dproj, dsh2, dsc2, gs["norm_mix"] = _inproj_bwd(
        dx2, x1, small["norm_mix"], sh2, sc2, w["w_in"], dz.reshape(T, D_SSD), dxraw.reshape(T, D_CONV), dcq, dckv,
        ddtk_a.reshape(T, LANES), ddtk_b, S)
    gw["w_in"] = _mm_tn(dproj, h2, 512, "dwin")
    g1 = g1 + on_grads(("w_in", "w_uq", "w_ukv", "w_out"), gw)

    dx0, h1, s1, df1, da1, du1, dsh1, dsc1, dg1, gs["norm_ffn1"] = _ffn_bwd(
        dx1, x0, small["norm_ffn1"], sh1, sc1, g1, a1, u1, f1, w["ffn1_w_gate"], w["ffn1_w_up"], w["ffn1_w_down"], S, "ffn1_bwd")
    gw["ffn1_w_gate"], gw["ffn1_w_up"], gw["ffn1_w_down"] = _ffn_wgrad(h1, s1, df1, da1, du1, "ffn1_wgrad")
    gs["norm_final"] = d_norm_final
    dmod = jnp.concatenate([t.reshape(B, D) for t in (dsh1, dsc1, dg1, dsh2, dsc2, dg2, dsh3, dsc3, dg3)], axis=1)
    return loss, dx0.reshape(B, S, D), gw, dmod, gs


HBM_SPEC = pl.BlockSpec(memory_space=pltpu.HBM)
VMEM_SPEC = pl.BlockSpec(memory_space=pltpu.VMEM)


def _place():
    return lax.axis_index("x"), lax.axis_index("y"), lax.axis_index("c")


def _other_chips(mx, my):
    return [(1 - mx, my), (mx, 1 - my), (1 - mx, 1 - my)]


def _remote(src, dst, send_sem, recv_sem, to):
    return pltpu.make_async_remote_copy(src_ref=src, dst_ref=dst, send_sem=send_sem, recv_sem=recv_sem,
                                        device_id=to, device_id_type=MESH)


def _all_gather_small(xa, name):
    r, n = xa.shape

    def body(x_ref, o_ref, token, send_sems, recv_sems):
        mx, my, mc = _place()
        me = 4 * mx + 2 * my + mc
        token[...] = jnp.zeros_like(token)
        o_ref[pl.ds(me, 1)] = x_ref[...][None]
        sends = []
        for k in range(1, N_DEV):
            peer = (mx ^ (k >> 2), my ^ ((k >> 1) & 1), mc ^ (k & 1))
            cp = _remote(x_ref, o_ref.at[me], send_sems.at[k - 1], recv_sems.at[k - 1], peer)
            cp.start()
            sends.append(cp)
        for k in range(1, N_DEV):
            peer = (mx ^ (k >> 2), my ^ ((k >> 1) & 1), mc ^ (k & 1))
            slot = 4 * peer[0] + 2 * peer[1] + peer[2]
            _remote(x_ref, o_ref.at[slot], send_sems.at[k - 1], recv_sems.at[k - 1], peer).wait_recv()
        for cp in sends:
            cp.wait_send()

    return pl.pallas_call(
        body, name=name, in_specs=[VMEM_SPEC], out_specs=[VMEM_SPEC, VMEM_SPEC],
        out_shape=[jax.ShapeDtypeStruct((N_DEV, r, n), xa.dtype), jax.ShapeDtypeStruct((8, LANES), F32)],
        scratch_shapes=[pltpu.SemaphoreType.DMA((N_DEV - 1,)), pltpu.SemaphoreType.DMA((N_DEV - 1,))],
        compiler_params=pltpu.CompilerParams(vmem_limit_bytes=VMEM_LIMIT),
    )(xa)


def _halves_by_rows(shape):
    return (shape[-2] // 2) % 16 == 0


def _half_shape(shape):
    r, c = shape[-2:]
    return tuple(shape[:-2]) + ((r // 2, c) if _halves_by_rows(shape) else (r, c // 2))


def _half_index(shape, hc):
    r, c = shape[-2:]
    if _halves_by_rows(shape):
        return (pl.ds(pl.multiple_of(hc * (r // 2), 16), r // 2), slice(None))
    return (slice(None), pl.ds(pl.multiple_of(hc * (c // 2), LANES), c // 2))


def _half(ref, hc, lead=None):
    idx = _half_index(ref.shape, hc)
    return ref.at[idx] if lead is None else ref.at[(lead,) + idx]


def _gather_weights(shards):
    n = len(shards)

    def body(*refs):
        w_refs, o_refs, token = refs[:n], refs[n:2 * n], refs[2 * n]
        send_sems, recv_sems, stage_sems = refs[2 * n + 1:2 * n + 4]
        stages = refs[2 * n + 4:]
        mx, my, mc = _place()
        chip = 2 * mx + my
        others = _other_chips(mx, my)
        sibling = (mx, my, 1 - mc)
        token[...] = jnp.zeros_like(token)
        stage_in = [pltpu.make_async_copy(w, st, stage_sems.at[0, i]) for i, (w, st) in enumerate(zip(w_refs, stages))]
        for cp in stage_in:
            cp.start()
        first = []
        for i, (w, o) in enumerate(zip(w_refs, o_refs)):
            for k, (cx, cy) in enumerate(others):
                first.append(_remote(_half(w, mc), _half(o, mc, chip), send_sems.at[i, k],
                                     recv_sems.at[i, k], (cx, cy, mc)))
                first[-1].start()
        stage_out = []
        for i, (st, o) in enumerate(zip(stages, o_refs)):
            stage_in[i].wait()
            stage_out.append(pltpu.make_async_copy(st, o.at[chip], stage_sems.at[1, i]))
            stage_out[-1].start()
        passed = []
        for i, (w, o) in enumerate(zip(w_refs, o_refs)):
            for k, (cx, cy) in enumerate(others):
                landed = _half(o, mc, 2 * cx + cy)
                _remote(landed, landed, send_sems.at[i, k], recv_sems.at[i, k], (cx, cy, mc)).wait_recv()
                passed.append(_remote(landed, landed, send_sems.at[i, 3 + k], recv_sems.at[i, 3 + k], sibling))
                passed[-1].start()
        for i, (w, o) in enumerate(zip(w_refs, o_refs)):
            for k, (cx, cy) in enumerate(others):
                there = _half(o, 1 - mc, 2 * cx + cy)
                _remote(there, there, send_sems.at[i, 3 + k], recv_sems.at[i, 3 + k], sibling).wait_recv()
        for cp in first + passed:
            cp.wait_send()
        for cp in stage_out:
            cp.wait()

    out = pl.pallas_call(
        body, name="gather_weights", in_specs=[HBM_SPEC] * n, out_specs=[HBM_SPEC] * n + [VMEM_SPEC],
        out_shape=[jax.ShapeDtypeStruct((N_CHIPS,) + s.shape, s.dtype) for s in shards] + [jax.ShapeDtypeStruct((8, LANES), F32)],
        scratch_shapes=[pltpu.SemaphoreType.DMA((n, 6)), pltpu.SemaphoreType.DMA((n, 6)), pltpu.SemaphoreType.DMA((2, n))]
        + [pltpu.VMEM(s.shape, s.dtype) for s in shards],
        compiler_params=pltpu.CompilerParams(vmem_limit_bytes=VMEM_LIMIT),
    )(*shards)
    return out[:n], out[n]


SEM_SPEC = pl.BlockSpec(memory_space=pltpu.SEMAPHORE)
ANY_SPEC = pl.BlockSpec(memory_space=pl.ANY)
DATAFLOW = pltpu.SideEffectType.DATAFLOW_SIDE_EFFECTING


def _hbm(arr):
    return pltpu.with_memory_space_constraint(arr, pltpu.HBM)


def _gather_start(shards):
    n = len(shards)

    def body(*refs):
        w_refs, land_refs, send_sems, recv_sems, token = refs[:n], refs[n:2 * n], refs[2 * n], refs[2 * n + 1], refs[-1]
        mx, my, mc = _place()
        chip = 2 * mx + my
        for i, (w, land) in enumerate(zip(w_refs, land_refs)):
            for k, (cx, cy) in enumerate(_other_chips(mx, my)):
                _remote(_half(w, mc), _half(land, mc, chip), send_sems.at[3 * i + k],
                        recv_sems.at[3 * i + k], (cx, cy, mc)).start()
        token[...] = jnp.zeros_like(token)

    lands = [lax.empty((N_CHIPS,) + s.shape, s.dtype) for s in shards]
    out = pl.pallas_call(
        body, name="gather_start",
        out_shape=(pltpu.SemaphoreType.DMA((3 * n,)), pltpu.SemaphoreType.DMA((3 * n,)),
                   *[pltpu.HBM(s.shape, s.dtype) for s in shards], *[pltpu.HBM(l.shape, l.dtype) for l in lands],
                   jax.ShapeDtypeStruct((8, LANES), F32)),
        in_specs=[HBM_SPEC] * (2 * n), out_specs=(SEM_SPEC, SEM_SPEC, *[HBM_SPEC] * (2 * n), VMEM_SPEC),
        input_output_aliases={i: 2 + i for i in range(2 * n)},
        compiler_params=pltpu.CompilerParams(has_side_effects=DATAFLOW),
    )(*[_hbm(s) for s in shards], *[_hbm(l) for l in lands])
    return out[0], out[1], out[2:2 + n], out[2 + n:2 + 2 * n], out[-1]


def _gather_wait(send_sems, recv_sems, shards, lands, after):
    n = len(shards)

    def body(*refs):
        w_refs, land_refs, send_sems, recv_sems = refs[:n], refs[n:2 * n], refs[2 * n], refs[2 * n + 1]
        mx, my, mc = _place()
        for i, (w, land) in enumerate(zip(w_refs, land_refs)):
            for k, (cx, cy) in enumerate(_other_chips(mx, my)):
                cp = _remote(_half(w, mc), _half(land, mc, 2 * cx + cy), send_sems.at[3 * i + k],
                             recv_sems.at[3 * i + k], (cx, cy, mc))
                cp.wait_send()
                cp.wait_recv()

    out = pl.pallas_call(
        body, name="gather_wait",
        out_shape=(*[pltpu.HBM(s.shape, s.dtype) for s in shards], *[pltpu.HBM(l.shape, l.dtype) for l in lands]),
        in_specs=[HBM_SPEC] * (2 * n) + [SEM_SPEC, SEM_SPEC, ANY_SPEC], out_specs=tuple([HBM_SPEC] * (2 * n)),
        input_output_aliases={i: i for i in range(2 * n)},
        compiler_params=pltpu.CompilerParams(has_side_effects=DATAFLOW),
    )(*shards, *lands, send_sems, recv_sems, after)
    return out[n:]


def _gather_finish(shards, lands):
    n = len(shards)

    def body(*refs):
        w_refs, land_refs, o_refs = refs[:n], refs[n:2 * n], refs[2 * n:3 * n]
        send_sems, recv_sems, stage_sems = refs[3 * n:3 * n + 3]
        stages = refs[3 * n + 3:]
        mx, my, mc = _place()
        chip = 2 * mx + my
        others = _other_chips(mx, my)
        sibling = (mx, my, 1 - mc)
        stage_in = [pltpu.make_async_copy(w, st, stage_sems.at[0, i]) for i, (w, st) in enumerate(zip(w_refs, stages))]
        for cp in stage_in:
            cp.start()
        passed = []
        for i, (w, o) in enumerate(zip(w_refs, o_refs)):
            for k, (cx, cy) in enumerate(others):
                landed = _half(o, mc, 2 * cx + cy)
                passed.append(_remote(landed, landed, send_sems.at[i, k], recv_sems.at[i, k], sibling))
                passed[-1].start()
        stage_out = []
        for i, (st, o) in enumerate(zip(stages, o_refs)):
            stage_in[i].wait()
            stage_out.append(pltpu.make_async_copy(st, o.at[chip], stage_sems.at[1, i]))
            stage_out[-1].start()
        for i, (w, o) in enumerate(zip(w_refs, o_refs)):
            for k, (cx, cy) in enumerate(others):
                there = _half(o, 1 - mc, 2 * cx + cy)
                _remote(there, there, send_sems.at[i, k], recv_sems.at[i, k], sibling).wait_recv()
        for cp in passed:
            cp.wait_send()
        for cp in stage_out:
            cp.wait()

    return pl.pallas_call(
        body, name="gather_finish", in_specs=[HBM_SPEC] * (2 * n), out_specs=[HBM_SPEC] * n,
        out_shape=[jax.ShapeDtypeStruct(l.shape, l.dtype) for l in lands],
        input_output_aliases={n + i: i for i in range(n)},
        scratch_shapes=[pltpu.SemaphoreType.DMA((n, 3)), pltpu.SemaphoreType.DMA((n, 3)), pltpu.SemaphoreType.DMA((2, n))]
        + [pltpu.VMEM(s.shape, s.dtype) for s in shards],
        compiler_params=pltpu.CompilerParams(vmem_limit_bytes=VMEM_LIMIT),
    )(*shards, *lands)


def _scatter_start(ss, tag):
    n = len(ss)

    def body(*refs):
        s_refs, land_refs, send_sems, recv_sems, token = refs[:n], refs[n:2 * n], refs[2 * n], refs[2 * n + 1], refs[-1]
        mx, my, mc = _place()
        chip = 2 * mx + my
        for i, (s, land) in enumerate(zip(s_refs, land_refs)):
            for k, (cx, cy) in enumerate(_other_chips(mx, my)):
                _remote(s.at[2 * cx + cy], land.at[chip], send_sems.at[3 * i + k], recv_sems.at[3 * i + k],
                        (cx, cy, mc)).start()
        token[...] = jnp.zeros_like(token)

    lands = [lax.empty(s.shape, s.dtype) for s in ss]
    out = pl.pallas_call(
        body, name="scatter_start_" + tag,
        out_shape=(pltpu.SemaphoreType.DMA((3 * n,)), pltpu.SemaphoreType.DMA((3 * n,)),
                   *[pltpu.HBM(s.shape, s.dtype) for s in ss], *[pltpu.HBM(l.shape, l.dtype) for l in lands],
                   jax.ShapeDtypeStruct((8, LANES), F32)),
        in_specs=[HBM_SPEC] * (2 * n), out_specs=(SEM_SPEC, SEM_SPEC, *[HBM_SPEC] * (2 * n), VMEM_SPEC),
        input_output_aliases={i: 2 + i for i in range(2 * n)},
        compiler_params=pltpu.CompilerParams(has_side_effects=DATAFLOW),
    )(*[_hbm(s) for s in ss], *[_hbm(l) for l in lands])
    return out[0], out[1], out[2:2 + n], out[2 + n:2 + 2 * n], out[-1]


def _scatter_wait(send_sems, recv_sems, ss, lands, after, tag):
    n = len(ss)

    def body(*refs):
        s_refs, land_refs, send_sems, recv_sems = refs[:n], refs[n:2 * n], refs[2 * n], refs[2 * n + 1]
        mx, my, mc = _place()
        for i, (s, land) in enumerate(zip(s_refs, land_refs)):
            for k, (cx, cy) in enumerate(_other_chips(mx, my)):
                slot = land.at[2 * cx + cy]
                cp = _remote(s.at[2 * cx + cy], slot, send_sems.at[3 * i + k], recv_sems.at[3 * i + k], (cx, cy, mc))
                cp.wait_send()
                cp.wait_recv()

    out = pl.pallas_call(
        body, name="scatter_wait_" + tag,
        out_shape=(*[pltpu.HBM(s.shape, s.dtype) for s in ss], *[pltpu.HBM(l.shape, l.dtype) for l in lands]),
        in_specs=[HBM_SPEC] * (2 * n) + [SEM_SPEC, SEM_SPEC, ANY_SPEC], out_specs=tuple([HBM_SPEC] * (2 * n)),
        input_output_aliases={i: i for i in range(2 * n)},
        compiler_params=pltpu.CompilerParams(has_side_effects=DATAFLOW),
    )(*ss, *lands, send_sems, recv_sems, after)
    return out[:n], out[n:]


def _swap_halves(gs, after, name):
    n = len(gs)

    def body(*refs):
        g_refs, o_refs, send_sems, recv_sems = refs[:n], refs[n + 1:2 * n + 1], refs[2 * n + 1], refs[2 * n + 2]
        mx, my, mc = _place()
        copies = []
        for i, (g, o) in enumerate(zip(g_refs, o_refs)):
            src = g.at[(slice(None),) + _half_index(g.shape, 1 - mc)]
            copies.append(_remote(src, o, send_sems.at[i], recv_sems.at[i], (mx, my, 1 - mc)))
            copies[-1].start()
        for cp in copies:
            cp.wait()

    return pl.pallas_call(
        body, name=name, in_specs=[HBM_SPEC] * n + [ANY_SPEC], out_specs=[HBM_SPEC] * n,
        out_shape=[jax.ShapeDtypeStruct(_half_shape(g.shape), g.dtype) for g in gs],
        scratch_shapes=[pltpu.SemaphoreType.DMA((n,)), pltpu.SemaphoreType.DMA((n,))],
    )(*gs, after)


def _pair_sum(g, got, core, name):
    hr, hc = _half_shape(g.shape)[1:]
    by_rows = _halves_by_rows(g.shape)

    def body(core_ref, g_ref, got_ref, o_ref):
        o_ref[...] = (g_ref[...].astype(F32) + got_ref[...].astype(F32)).astype(BF16)

    return pl.pallas_call(
        body, name=name,
        grid_spec=pltpu.PrefetchScalarGridSpec(
            num_scalar_prefetch=1, grid=(N_CHIPS,),
            in_specs=[pl.BlockSpec((1, hr, hc), lambda j, core_ref: (j, core_ref[0], 0) if by_rows else (j, 0, core_ref[0])),
                      pl.BlockSpec((1, hr, hc), lambda j, core_ref: (j, 0, 0))],
            out_specs=pl.BlockSpec((1, hr, hc), lambda j, core_ref: (j, 0, 0))),
        out_shape=jax.ShapeDtypeStruct((N_CHIPS, hr, hc), BF16),
        compiler_params=_cparams(("arbitrary",)),
    )(core, g, got)


def _chip_sum(own, got, chip, name):
    _, h, c = own.shape

    def body(chip_ref, a_ref, b_ref, c_ref, d_ref, o_ref):
        o_ref[...] = ((a_ref[0].astype(F32) + b_ref[0].astype(F32)) + c_ref[0].astype(F32)) + d_ref[0].astype(F32)

    slot = lambda flip: pl.BlockSpec((1, h, c), lambda i, chip_ref: (chip_ref[0] ^ flip, 0, 0))
    return pl.pallas_call(
        body, name=name,
        grid_spec=pltpu.PrefetchScalarGridSpec(
            num_scalar_prefetch=1, grid=(1,), in_specs=[slot(0), slot(1), slot(2), slot(3)],
            out_specs=pl.BlockSpec((h, c), lambda i, chip_ref: (0, 0))),
        out_shape=jax.ShapeDtypeStruct((h, c), F32),
        compiler_params=_cparams(("arbitrary",)),
    )(chip, own, got, got, got)


def _join_halves(mine, name):
    n = len(mine)

    def body(*refs):
        m_refs, o_refs, send_sems, recv_sems = refs[:n], refs[n:2 * n], refs[2 * n], refs[2 * n + 1]
        mx, my, mc = _place()
        copies = []
        for i, (m, o) in enumerate(zip(m_refs, o_refs)):
            copies.append(_remote(m, o, send_sems.at[i], recv_sems.at[i], (mx, my, 1 - mc)))
            copies[-1].start()
        for cp in copies:
            cp.wait()

    return pl.pallas_call(
        body, name=name, in_specs=[HBM_SPEC] * n, out_specs=[HBM_SPEC] * n,
        out_shape=[jax.ShapeDtypeStruct(m.shape, m.dtype) for m in mine],
        scratch_shapes=[pltpu.SemaphoreType.DMA((n,)), pltpu.SemaphoreType.DMA((n,))],
    )(*mine)


def _adam_math(w, g, m, v):
    m2 = ADAM_B1 * m + (1.0 - ADAM_B1) * g
    v2 = ADAM_B2 * v + (1.0 - ADAM_B2) * (g * g)
    m_hat = m2 * (1.0 / (1.0 - ADAM_B1 ** ADAM_STEP))
    v_hat = v2 * (1.0 / (1.0 - ADAM_B2 ** ADAM_STEP))
    delta = -ADAM_LR * (m_hat / (jnp.sqrt(v_hat) + ADAM_EPS) + ADAM_WD * w)
    return delta, m2, v2


def _adam(w, g, m, v, name):
    def body(w_ref, g_ref, m_ref, v_ref, d_ref, m2_ref, v2_ref):
        d_ref[...], m2_ref[...], v2_ref[...] = _adam_math(w_ref[...], g_ref[...], m_ref[...], v_ref[...])

    return pl.pallas_call(body, name=name, out_shape=[jax.ShapeDtypeStruct(w.shape, F32)] * 3)(w, g, m, v)


def _adam_halves(w, m, v, mine, theirs, core, name):
    hr, hcols = _half_shape(w.shape)[1:]
    by_rows = _halves_by_rows(w.shape)

    def body(core_ref, w_ref, m_ref, v_ref, mine_ref, theirs_ref, g_ref, d_ref, m2_ref, v2_ref):
        g = jnp.where(pl.program_id(0) == core_ref[0], mine_ref[...], theirs_ref[...])
        g_ref[0] = g
        d_ref[0], m2_ref[0], v2_ref[0] = _adam_math(w_ref[0], g, m_ref[0], v_ref[0])

    half = pl.BlockSpec((1, hr, hcols), lambda hc, core_ref: (0, hc, 0) if by_rows else (0, 0, hc))
    whole = pl.BlockSpec((hr, hcols), lambda hc, core_ref: (0, 0))
    return pl.pallas_call(
        body, name=name,
        grid_spec=pltpu.PrefetchScalarGridSpec(
            num_scalar_prefetch=1, grid=(2,), in_specs=[half, half, half, whole, whole], out_specs=[half] * 4),
        out_shape=[jax.ShapeDtypeStruct(w.shape, F32)] * 4,
        compiler_params=_cparams(("arbitrary",)),
    )(core, w, m, v, mine, theirs)


ADA_COLS = N_MOD * D_MODEL // N_CHIPS


def _ada_fwd(c_all, w_ada, b_cols):
    def body(c_ref, w_ref, b_ref, o_ref):
        cv = c_ref[...]
        act = (cv * _sigmoid(cv)).astype(BF16)
        o_ref[...] = _dot(act, w_ref[...].astype(BF16)) + b_ref[...]

    return pl.pallas_call(
        body, name="ada_fwd", out_shape=jax.ShapeDtypeStruct((c_all.shape[0], ADA_COLS), F32),
        compiler_params=pltpu.CompilerParams(vmem_limit_bytes=VMEM_LIMIT),
    )(c_all, w_ada, b_cols)


def _ada_bwd(c_all, dmod_cols, w, m, v):
    nb = c_all.shape[0]
    tn = 384

    def body(c_ref, d_ref, w_ref, m_ref, v_ref, g_ref, dl_ref, m2_ref, v2_ref):
        cv = c_ref[...]
        act = (cv * _sigmoid(cv)).astype(BF16)
        g = _dot_tn(act, d_ref[...].astype(BF16))
        g_ref[...] = g
        dl_ref[...], m2_ref[...], v2_ref[...] = _adam_math(w_ref[...], g, m_ref[...], v_ref[...])

    blk = pl.BlockSpec((D_MODEL, tn), lambda j: (0, j))
    return pl.pallas_call(
        body, name="ada_bwd", grid=(ADA_COLS // tn,),
        in_specs=[pl.BlockSpec((nb, D_MODEL), lambda j: (0, 0)), pl.BlockSpec((nb, tn), lambda j: (0, j)), blk, blk, blk],
        out_specs=[blk] * 4, out_shape=[jax.ShapeDtypeStruct((D_MODEL, ADA_COLS), F32)] * 4,
        compiler_params=_cparams(("arbitrary",)),
    )(c_all, dmod_cols, w, m, v)


SMALL_NAMES = ("norm_ffn1", "norm_mix", "conv_w", "conv_b", "ssd_norm_w", "q_norm_w", "kv_norm_w", "mla_norm_w",
               "norm_ffn2", "norm_final", "dt_bias", "a_log", "d_skip")
SMALL_SIZES = (1024, 1024, CONV_WIDTH * D_CONV, D_CONV, 1024, Q_LORA, KV_LORA, 1024, 1024, 1024, 16, 16, 16)
SMALL_ROWS = 16
MOD_ROWS = 2 * N_MOD
SEND_ROWS = 40


def _pack_small(parts):
    flat = jnp.concatenate([parts[n].reshape(-1) for n in SMALL_NAMES])
    return jnp.pad(flat, (0, SMALL_ROWS * D_MODEL - flat.shape[0]))


def _unpack_small(flat):
    out, off = {}, 0
    for n, size in zip(SMALL_NAMES, SMALL_SIZES):
        out[n] = flat[off:off + size]
        off += size
    return out


def _small_sum(got):
    def body(g_ref, o_ref):
        bsum = jnp.zeros((N_MOD, D_MODEL), F32)
        ssum = jnp.zeros((SMALL_ROWS, D_MODEL), F32)
        for d in range(N_DEV):
            bsum = bsum + g_ref[d, 0:N_MOD, :] + g_ref[d, N_MOD:MOD_ROWS, :]
            ssum = ssum + g_ref[d, MOD_ROWS:MOD_ROWS + SMALL_ROWS, :]
        o_ref[...] = jnp.concatenate([bsum, ssum, jnp.zeros((32 - N_MOD - SMALL_ROWS, D_MODEL), F32)], axis=0)

    return pl.pallas_call(body, name="small_sum", out_shape=jax.ShapeDtypeStruct((32, D_MODEL), F32))(got)


BIG_NAMES = ("ffn1_w_gate", "ffn1_w_up", "ffn1_w_down", "w_in", "w_uq", "w_ukv", "w_out", "ffn2_w_gate", "ffn2_w_up",
             "ffn2_w_down")
_TO_KERNEL = {"w_in": _win_to_kernel, "w_uq": _wuq_to_kernel, "w_ukv": _wukv_to_kernel}
_FROM_KERNEL = {"w_in": _win_from_kernel, "w_uq": _wuq_from_kernel, "w_ukv": _wukv_from_kernel}


def _columns_joined(w4):
    n, r, c = w4.shape
    return w4.transpose(1, 0, 2).reshape(r, n * c)


def _columns_split(g):
    r, cols = g.shape
    return g.reshape(r, N_CHIPS, cols // N_CHIPS).transpose(1, 0, 2)


def kernel(x, c, positions, w_ada, b_ada, norm_ffn1, ffn1_w_gate, ffn1_w_up, ffn1_w_down, norm_mix, w_in, conv_w, conv_b, dt_bias, a_log, d_skip, ssd_norm_w, q_norm_w, w_uq, kv_norm_w, w_ukv, mla_norm_w, w_out, norm_ffn2, ffn2_w_gate, ffn2_w_up, ffn2_w_down, norm_final, loss_target, m_w_ada, m_b_ada, m_norm_ffn1, m_ffn1_w_gate, m_ffn1_w_up, m_ffn1_w_down, m_norm_mix, m_w_in, m_conv_w, m_conv_b, m_dt_bias, m_a_log, m_d_skip, m_ssd_norm_w, m_q_norm_w, m_w_uq, m_kv_norm_w, m_w_ukv, m_mla_norm_w, m_w_out, m_norm_ffn2, m_ffn2_w_gate, m_ffn2_w_up, m_ffn2_w_down, m_norm_final, v_w_ada, v_b_ada, v_norm_ffn1, v_ffn1_w_gate, v_ffn1_w_up, v_ffn1_w_down, v_norm_mix, v_w_in, v_conv_w, v_conv_b, v_dt_bias, v_a_log, v_d_skip, v_ssd_norm_w, v_q_norm_w, v_w_uq, v_kv_norm_w, v_w_ukv, v_mla_norm_w, v_w_out, v_norm_ffn2, v_ffn2_w_gate, v_ffn2_w_up, v_ffn2_w_down, v_norm_final):
    a = dict(locals())
    held_transposed = ("ffn1_w_gate", "ffn1_w_up", "ffn2_w_gate", "ffn2_w_up", "w_in")
    for n in held_transposed:
        for p in ("", "m_", "v_"):
            a[p + n] = a[p + n].transpose(0, 2, 1)
    B, S, D = x.shape
    mx, my, mc = _place()
    chip = 2 * mx + my
    dev = 2 * chip + mc
    core = mc.astype(jnp.int32).reshape(1)
    chip_id = chip.astype(jnp.int32).reshape(1)

    cw_rows = jnp.pad(conv_w[0], ((0, 0), (0, D - conv_w.shape[2])))
    got, _ = _all_gather_small(jnp.concatenate([c, cw_rows, jnp.zeros((8 - B - CONV_WIDTH, D), F32)], axis=0), "gather_c")
    c_all = got[:, :B, :].reshape(N_DEV * B, D)
    conv_full = got[::2, B:B + CONV_WIDTH, :conv_w.shape[2]].transpose(1, 0, 2).reshape(CONV_WIDTH, D_CONV)

    b_cols = lax.dynamic_slice(b_ada, (0, chip * ADA_COLS), (1, ADA_COLS))
    mod_all, mod_done = _all_gather_small(_ada_fwd(c_all, w_ada[0], b_cols), "gather_mod")
    mod = lax.dynamic_slice(mod_all, (0, B * dev, 0), (N_DEV, B, ADA_COLS))[::2].transpose(1, 0, 2).reshape(B, N_MOD * D)

    first = ("ffn1_w_gate", "ffn1_w_up", "ffn1_w_down")
    later = tuple(n for n in BIG_NAMES if n not in first)
    got_first, gathered = _gather_weights([(a[n][0] + mod_done[0, 0]).astype(BF16) for n in first])
    w = dict(zip(first, got_first))
    in_flight = _gather_start([(a[n][0] + gathered[0, 0]).astype(BF16) for n in later])

    def later_weights(after):
        send_sems, recv_sems, shards, lands, _ = in_flight
        lands = _gather_wait(send_sems, recv_sems, shards, lands, after)
        wl = dict(zip(later, _gather_finish([a[n][0].astype(BF16) for n in later], lands)))
        for n, to_kernel in _TO_KERNEL.items():
            wl[n] = to_kernel(wl[n].reshape(-1, D) if n in held_transposed else _columns_joined(wl[n]))
        wl["w_out"] = wl["w_out"].reshape(D_SSD + D_MLA, D)
        return wl

    small = {n: a[n].reshape(1, -1) for n in SMALL_NAMES if n not in ("conv_w", "norm_final")}
    small["conv_w"], small["norm_final"] = conv_full, norm_final

    def scatter_group(names, gw, after):
        g4 = []
        for n in names:
            g = gw[n]
            if n in _FROM_KERNEL:
                g = _FROM_KERNEL[n](g) if n in held_transposed else _columns_split(_FROM_KERNEL[n](g))
            g4.append(g.reshape(N_CHIPS, a[n].shape[1], a[n].shape[2]))
        swapped = _swap_halves(g4, g4[0] if after is None else after, "swap_" + names[0])
        pair = [_pair_sum(g, got, core, "pair_sum_" + n) for n, g, got in zip(names, g4, swapped)]
        return (names,) + tuple(_scatter_start(pair, names[0]))

    grads, deltas, new_m, new_v = {}, {}, {}, {}

    def finish_groups(some, after):
        names, mine = [], []
        for group_names, send_sems, recv_sems, pair, lands, _ in some:
            pair, lands = _scatter_wait(send_sems, recv_sems, pair, lands, after, group_names[0])
            names += group_names
            mine += [_chip_sum(own, got, chip_id, "chip_sum_" + n) for n, own, got in zip(group_names, pair, lands)]
        for n, own, other in zip(names, mine, _join_halves(mine, "join_" + names[0])):
            grads[n], deltas[n], new_m[n], new_v[n] = _adam_halves(a[n], a["m_" + n], a["v_" + n], own, other, core, "adam_" + n)
        return deltas[names[-1]]

    groups = []

    def on_grads(names, gw):
        groups.append(scatter_group(names, gw, None))
        return groups[-1][5][0, 0]

    loss_blk, grad_x, gw, dmod, gs = _local_step(x, positions, mod + in_flight[4][0, 0], w, later_weights, small, loss_target,
                                                 on_grads)

    small_flat = _pack_small(gs).at[-1].set(loss_blk[0, 0])
    send = jnp.concatenate([dmod.reshape(MOD_ROWS, D), small_flat.reshape(SMALL_ROWS, D),
                            jnp.zeros((SEND_ROWS - MOD_ROWS - SMALL_ROWS, D), F32)], axis=0)
    got, _ = _all_gather_small(send, "gather_small")
    summed = _small_sum(got)
    sums = summed[N_MOD:N_MOD + SMALL_ROWS].reshape(-1)
    loss = sums[-1]
    gsmall = _unpack_small(sums)
    gsmall["conv_w"] = lax.dynamic_slice(gsmall["conv_w"].reshape(CONV_WIDTH, D_CONV), (0, chip * conv_w.shape[2]),
                                         (CONV_WIDTH, conv_w.shape[2]))
    gsmall["b_ada"] = summed[:N_MOD]
    names = ("b_ada",) + SMALL_NAMES
    rows = 208

    def pack(parts):
        flat = jnp.concatenate([parts[n].reshape(-1) for n in names])
        return jnp.pad(flat, (0, rows * LANES - flat.shape[0])).reshape(rows, LANES)

    packed = [pack({n: a[p + n] for n in names}) for p in ("", "m_", "v_")]
    g_p = pack(gsmall)
    outs = (g_p,) + tuple(_adam(packed[0], g_p, packed[1], packed[2], "adam_small"))
    for dst, flat in zip((grads, deltas, new_m, new_v), outs):
        flat, off = flat.reshape(-1), 0
        for n in names:
            dst[n] = flat[off:off + a[n].size].reshape(a[n].shape)
            off += a[n].size

    dmod_all = got[:, :MOD_ROWS, :].reshape(N_DEV * B, N_MOD * D)
    dmod_cols = lax.dynamic_slice(dmod_all, (0, chip * ADA_COLS), (N_DEV * B, ADA_COLS))
    ada = _ada_bwd(c_all, dmod_cols, w_ada[0], m_w_ada[0], v_w_ada[0])
    for dst, t in zip((grads, deltas, new_m, new_v), ada):
        dst["w_ada"] = t[None]

    last = scatter_group(first, gw, summed)
    finish_groups([last], finish_groups(groups, last[5]))
    for dst in (grads, deltas, new_m, new_v):
        for n in held_transposed:
            dst[n] = dst[n].transpose(0, 2, 1)

    order = ("w_ada", "b_ada", "norm_ffn1", "ffn1_w_gate", "ffn1_w_up", "ffn1_w_down", "norm_mix", "w_in", "conv_w", "conv_b",
             "dt_bias", "a_log", "d_skip", "ssd_norm_w", "q_norm_w", "w_uq", "kv_norm_w", "w_ukv", "mla_norm_w", "w_out",
             "norm_ffn2", "ffn2_w_gate", "ffn2_w_up", "ffn2_w_down", "norm_final")
    return (loss, grad_x, *[grads[n] for n in order], *[deltas[n] for n in order], *[new_m[n] for n in order],
            *[new_v[n] for n in order])
```

```python
import functools
import math

import jax
import jax.numpy as jnp
import numpy as np
from jax import lax
from jax.experimental import pallas as pl
from jax.experimental.pallas import tpu as pltpu

F32 = jnp.float32
BF16 = jnp.bfloat16
HIGHEST = lax.Precision.HIGHEST

D_MODEL = 1024
D_FF = 2816
D_SSD = 1024
D_MLA = 1024
SSD_HEADS = 16
SSD_HEAD_DIM = 64
SSD_GROUPS = 2
SSD_STATE = 128
CONV_WIDTH = 4
CHUNK = 128
MLA_HEADS = 8
QK_NOPE = 64
QK_ROPE = 32
QK_DIM = QK_NOPE + QK_ROPE
V_HEAD = 128
Q_LORA = 384
KV_LORA = 256
ROPE_THETA = 10000.0
N_MOD = 9
EPS = 1e-6
D_CONV = D_SSD + 2 * SSD_GROUPS * SSD_STATE
D_PROJ = 3328
HEAD_LANES = 128
ADAM_LR = 0.001
ADAM_B1 = 0.9
ADAM_B2 = 0.999
ADAM_EPS = 1e-08
ADAM_WD = 0.01
ADAM_STEP = 10

LANES = 128
VMEM_LIMIT = 56 * 1024 * 1024
TOKEN_TILE = 512
WIDE_TOKEN_TILE = 1024
ATTN_FWD_Q_TILE = 1024
ATTN_FWD_KV_TILE = 1024
ATTN_BWD_TILE = 1024
N_CHIPS = 4
N_DEV = 8

MESH = pl.DeviceIdType.MESH


def _dot(a, b, precision=None):
    return jnp.dot(a, b, preferred_element_type=F32, precision=precision)


def _dot_nt(a, b, precision=None):
    return lax.dot_general(a, b, (((1,), (1,)), ((), ())), preferred_element_type=F32, precision=precision)


def _dot_tn(a, b, precision=None):
    return lax.dot_general(a, b, (((0,), (0,)), ((), ())), preferred_element_type=F32, precision=precision)


def _cparams(semantics):
    return pltpu.CompilerParams(dimension_semantics=semantics, vmem_limit_bytes=VMEM_LIMIT)


def _resident(shape):
    zeros = (0,) * len(shape)
    return pl.BlockSpec(shape, lambda *_: zeros, pipeline_mode=pl.Buffered(1))


def _sigmoid(x):
    return jax.nn.sigmoid(x)


def _rms_stats(x):
    r = lax.rsqrt(jnp.mean(x * x, axis=-1, keepdims=True) + EPS)
    return x * r, r


def _rms_bwd(dn, xh, r, w):
    dxh = dn * w
    dx = r * (dxh - xh * jnp.mean(dxh * xh, axis=-1, keepdims=True))
    return dx, dn * xh


def _colsum(v):
    return jnp.sum(v, axis=0, keepdims=True)


def _ffn_fwd(x, nw, sh, sc, g, wg, wu, wd, seq, name, head=None):
    T, D = x.shape
    fs = wg.shape[1]
    tm = min(TOKEN_TILE, seq)
    tps = seq // tm

    def body(x_ref, nw_ref, sh_ref, sc_ref, g_ref, wg_ref, wu_ref, wd_ref, *rest):
        if head is None:
            xo_ref, a_ref, u_ref, f_ref = rest
        else:
            nf_ref, t_ref, xo_ref, a_ref, u_ref, f_ref, loss_ref, dnf_ref = rest

            @pl.when(pl.program_id(0) == 0)
            def _():
                loss_ref[...] = jnp.zeros_like(loss_ref)
                dnf_ref[...] = jnp.zeros_like(dnf_ref)

        xv = x_ref[...]
        xh, _ = _rms_stats(xv)
        h = (xh * nw_ref[...]) * (1.0 + sc_ref[0]) + sh_ref[0]
        hb = h.astype(BF16)
        f = jnp.zeros((tm, D), F32)
        for j in range(N_CHIPS):
            a = _dot_nt(hb, wg_ref[j])
            u = _dot_nt(hb, wu_ref[j])
            a_ref[j] = a.astype(BF16)
            u_ref[j] = u.astype(BF16)
            f = f + _dot((a * _sigmoid(a) * u).astype(BF16), wd_ref[j])
        f_ref[...] = f.astype(BF16)
        xo = xv + 0.5 * g_ref[0] * f
        if head is None:
            xo_ref[...] = xo
        else:
            xh, r = _rms_stats(xo)
            nfv = nf_ref[...]
            err = xh * nfv - t_ref[...]
            loss_ref[...] += (0.5 / D) * jnp.sum(err * err)
            dxo, dw_rows = _rms_bwd(err * (1.0 / D), xh, r, nfv)
            xo_ref[...] = dxo
            dnf_ref[...] += _colsum(dw_rows)

    rows = lambda n: pl.BlockSpec((tm, n), lambda i: (i, 0))
    act = pl.BlockSpec((N_CHIPS, tm, fs), lambda i: (0, i, 0))
    perb = pl.BlockSpec((1, 1, D), lambda i: (i // tps, 0, 0))
    sd = jax.ShapeDtypeStruct
    in_specs = [rows(D), _resident((1, D)), perb, perb, perb, _resident((N_CHIPS, fs, D)), _resident((N_CHIPS, fs, D)),
                _resident((N_CHIPS, fs, D))]
    out_specs = [rows(D), act, act, rows(D)]
    out_shape = [sd((T, D), F32), sd((N_CHIPS, T, fs), BF16), sd((N_CHIPS, T, fs), BF16), sd((T, D), BF16)]
    if head is not None:
        in_specs += [_resident((1, D)), rows(D)]
        out_specs += [pl.BlockSpec((8, LANES), lambda i: (0, 0)), pl.BlockSpec((1, D), lambda i: (0, 0))]
        out_shape += [sd((8, LANES), F32), sd((1, D), F32)]
    return pl.pallas_call(
        body, grid=(T // tm,), name=name, in_specs=in_specs, out_specs=out_specs, out_shape=out_shape,
        compiler_params=_cparams(("arbitrary",)),
    )(x, nw, sh, sc, g, wg, wu, wd, *(head or ()))


def _ffn_bwd(dxo, x, nw, sh, sc, g, a, u, f, wg, wu, wd, seq, name):
    T, D = x.shape
    fs = wg.shape[1]
    B = T // seq
    tm = min(TOKEN_TILE // 2, seq)
    tps = seq // tm

    def body(dxo_ref, x_ref, nw_ref, sh_ref, sc_ref, g_ref, a_ref, u_ref, f_ref, wg_ref, wu_ref, wd_ref,
             dx_ref, h_ref, s_ref, df_ref, da_ref, du_ref, dsh_ref, dsc_ref, dg_ref, dnw_ref):
        i = pl.program_id(0)

        @pl.when(i % tps == 0)
        def _():
            dsh_ref[...] = jnp.zeros_like(dsh_ref)
            dsc_ref[...] = jnp.zeros_like(dsc_ref)
            dg_ref[...] = jnp.zeros_like(dg_ref)

        @pl.when(i == 0)
        def _():
            dnw_ref[...] = jnp.zeros_like(dnw_ref)

        dxo_v = dxo_ref[...]
        dfb = (0.5 * g_ref[0] * dxo_v).astype(BF16)
        dg_ref[0] += _colsum(0.5 * dxo_v * f_ref[...].astype(F32))
        dh = jnp.zeros((tm, D), F32)
        for j in range(N_CHIPS):
            ds = _dot_nt(dfb, wd_ref[j])
            av = a_ref[j].astype(F32)
            uv = u_ref[j].astype(F32)
            sig = _sigmoid(av)
            sil = av * sig
            dab = (ds * uv * (sig * (1.0 + av * (1.0 - sig)))).astype(BF16)
            dub = (ds * sil).astype(BF16)
            dh = dh + _dot(dab, wg_ref[j]) + _dot(dub, wu_ref[j])
            s_ref[j] = (sil * uv).astype(BF16)
            da_ref[j] = dab
            du_ref[j] = dub
        xv = x_ref[...]
        xh, r = _rms_stats(xv)
        nwv = nw_ref[...]
        n = xh * nwv
        scale1 = 1.0 + sc_ref[0]
        dsc_ref[0] += _colsum(dh * n)
        dsh_ref[0] += _colsum(dh)
        dx, dw_rows = _rms_bwd(dh * scale1, xh, r, nwv)
        dnw_ref[...] += _colsum(dw_rows)
        dx_ref[...] = dxo_v + dx
        h_ref[...] = (n * scale1 + sh_ref[0]).astype(BF16)
        df_ref[...] = dfb

    rows = lambda n: pl.BlockSpec((tm, n), lambda i: (i, 0))
    act = pl.BlockSpec((N_CHIPS, tm, fs), lambda i: (0, i, 0))
    perb = pl.BlockSpec((1, 1, D), lambda i: (i // tps, 0, 0))
    sd = jax.ShapeDtypeStruct
    return pl.pallas_call(
        body, grid=(T // tm,), name=name,
        in_specs=[rows(D), rows(D), _resident((1, D)), perb, perb, perb, act, act, rows(D),
                  _resident((N_CHIPS, fs, D)), _resident((N_CHIPS, fs, D)), _resident((N_CHIPS, fs, D))],
        out_specs=[rows(D), rows(D), act, rows(D), act, act, perb, perb, perb, pl.BlockSpec((1, D), lambda i: (0, 0))],
        out_shape=[sd((T, D), F32), sd((T, D), BF16), sd((N_CHIPS, T, fs), BF16), sd((T, D), BF16),
                   sd((N_CHIPS, T, fs), BF16), sd((N_CHIPS, T, fs), BF16), sd((B, 1, D), F32), sd((B, 1, D), F32),
                   sd((B, 1, D), F32), sd((1, D), F32)],
        compiler_params=_cparams(("arbitrary",)),
    )(dxo, x, nw, sh, sc, g, a, u, f, wg, wu, wd)


def _ffn_wgrad(h, s, df, da, du, after, name):
    T, D = h.shape
    fs = s.shape[2]
    tt = min(TOKEN_TILE, T)
    nt = T // tt

    def body(h_ref, s_ref, df_ref, da_ref, du_ref, after_ref, dgate_ref, dup_ref, ddown_ref, gate_acc, up_acc, down_acc):
        @pl.when(pl.program_id(1) == 0)
        def _():
            gate_acc[...] = jnp.zeros_like(gate_acc)
            up_acc[...] = jnp.zeros_like(up_acc)
            down_acc[...] = jnp.zeros_like(down_acc)

        hv = h_ref[...]
        gate_acc[...] += _dot_tn(da_ref[0], hv)
        up_acc[...] += _dot_tn(du_ref[0], hv)
        down_acc[...] += _dot_tn(s_ref[0], df_ref[...])

        @pl.when(pl.program_id(1) == nt - 1)
        def _():
            dgate_ref[0] = gate_acc[...].astype(BF16)
            dup_ref[0] = up_acc[...].astype(BF16)
            ddown_ref[0] = down_acc[...].astype(BF16)

    rows = pl.BlockSpec((tt, D), lambda j, t: (t, 0))
    act = pl.BlockSpec((1, tt, fs), lambda j, t: (j, t, 0))
    shard = pl.BlockSpec((1, fs, D), lambda j, t: (j, 0, 0))
    return pl.pallas_call(
        body, grid=(N_CHIPS, nt), name=name,
        in_specs=[rows, act, rows, act, act, pl.BlockSpec(memory_space=pl.ANY)],
        out_specs=[shard] * 3, out_shape=[jax.ShapeDtypeStruct((N_CHIPS, fs, D), BF16)] * 3,
        scratch_shapes=[pltpu.VMEM((fs, D), F32)] * 3,
        compiler_params=_cparams(("arbitrary", "arbitrary")),
    )(h, s, df, da, du, after)


def _mm_tn(xa, ya, tn, name):
    T, K = xa.shape
    N = ya.shape[1]
    tt = min(WIDE_TOKEN_TILE, T)
    nt = T // tt

    def body(x_ref, y_ref, o_ref, acc_ref):
        @pl.when(pl.program_id(1) == 0)
        def _():
            acc_ref[...] = jnp.zeros_like(acc_ref)

        acc_ref[...] += _dot_tn(x_ref[...], y_ref[...])

        @pl.when(pl.program_id(1) == nt - 1)
        def _():
            o_ref[...] = acc_ref[...].astype(BF16)

    return pl.pallas_call(
        body, grid=(N // tn, nt), name=name,
        in_specs=[pl.BlockSpec((tt, K), lambda j, t: (t, 0)), pl.BlockSpec((tt, tn), lambda j, t: (t, j))],
        out_specs=pl.BlockSpec((K, tn), lambda j, t: (0, j)),
        out_shape=jax.ShapeDtypeStruct((K, N), BF16),
        scratch_shapes=[pltpu.VMEM((K, tn), F32)],
        compiler_params=_cparams(("arbitrary", "arbitrary")),
    )(xa, ya)


_PROJ_SPLITS = (0, 1024, 2560, 2944, 3200, 3328)


def _inproj_fwd(x, nw, sh, sc, win, seq):
    T, D = x.shape
    tm = min(WIDE_TOKEN_TILE, seq)
    tps = seq // tm
    widths = [b - a for a, b in zip(_PROJ_SPLITS[:-1], _PROJ_SPLITS[1:])]
    dtypes = [BF16, BF16, F32, F32, F32]

    def body(x_ref, nw_ref, sh_ref, sc_ref, w_ref, *outs):
        xh, _ = _rms_stats(x_ref[...])
        h = (xh * nw_ref[...]) * (1.0 + sc_ref[0]) + sh_ref[0]
        proj = _dot_nt(h.astype(BF16), w_ref[...])
        for o, lo, hi in zip(outs, _PROJ_SPLITS[:-1], _PROJ_SPLITS[1:]):
            o[...] = proj[:, lo:hi].astype(o.dtype)

    rows = lambda n: pl.BlockSpec((tm, n), lambda i: (i, 0))
    perb = pl.BlockSpec((1, 1, D), lambda i: (i // tps, 0, 0))
    return pl.pallas_call(
        body, grid=(T // tm,), name="inproj_fwd",
        in_specs=[rows(D), _resident((1, D)), perb, perb, _resident((D_PROJ, D))],
        out_specs=[rows(w) for w in widths],
        out_shape=[jax.ShapeDtypeStruct((T, w), dt) for w, dt in zip(widths, dtypes)],
        compiler_params=_cparams(("arbitrary",)),
    )(x, nw, sh, sc, win)


def _inproj_bwd(dx2, x, nw, sh, sc, win, dz, dxbc, dcq, dckv, ddtk_a, ddtk_b, seq):
    T, D = x.shape
    B = T // seq
    tm = min(TOKEN_TILE, seq)
    tps = seq // tm

    def body(dx2_ref, x_ref, nw_ref, sh_ref, sc_ref, w_ref, dz_ref, dxbc_ref, dcq_ref, dckv_ref, da_ref, db_ref,
             dx_ref, h_ref, dp_ref, dsh_ref, dsc_ref, dnw_ref):
        i = pl.program_id(0)

        @pl.when(i % tps == 0)
        def _():
            dsh_ref[...] = jnp.zeros_like(dsh_ref)
            dsc_ref[...] = jnp.zeros_like(dsc_ref)

        @pl.when(i == 0)
        def _():
            dnw_ref[...] = jnp.zeros_like(dnw_ref)

        dproj = jnp.concatenate(
            [dz_ref[...], dxbc_ref[...], dcq_ref[...].astype(BF16), dckv_ref[...].astype(BF16),
             (da_ref[...] + db_ref[...]).astype(BF16)], axis=1)
        dp_ref[...] = dproj
        dh = _dot(dproj, w_ref[...])
        xh, r = _rms_stats(x_ref[...])
        nwv = nw_ref[...]
        n = xh * nwv
        scale1 = 1.0 + sc_ref[0]
        dsc_ref[0] += _colsum(dh * n)
        dsh_ref[0] += _colsum(dh)
        dx, dw_rows = _rms_bwd(dh * scale1, xh, r, nwv)
        dnw_ref[...] += _colsum(dw_rows)
        dx_ref[...] = dx2_ref[...] + dx
        h_ref[...] = (n * scale1 + sh_ref[0]).astype(BF16)

    rows = lambda n: pl.BlockSpec((tm, n), lambda i: (i, 0))
    perb = pl.BlockSpec((1, 1, D), lambda i: (i // tps, 0, 0))
    sd = jax.ShapeDtypeStruct
    return pl.pallas_call(
        body, grid=(T // tm,), name="inproj_bwd",
        in_specs=[rows(D), rows(D), _resident((1, D)), perb, perb, _resident((D_PROJ, D)),
                  rows(1024), rows(D_CONV), rows(Q_LORA), rows(KV_LORA), rows(LANES), rows(LANES)],
        out_specs=[rows(D), rows(D), rows(D_PROJ), perb, perb, pl.BlockSpec((1, D), lambda i: (0, 0))],
        out_shape=[sd((T, D), F32), sd((T, D), BF16), sd((T, D_PROJ), BF16), sd((B, 1, D), F32), sd((B, 1, D), F32),
                   sd((1, D), F32)],
        compiler_params=_cparams(("arbitrary",)),
    )(dx2, x, nw, sh, sc, win, dz, dxbc, dcq, dckv, ddtk_a, ddtk_b)


SUBLANES = 8


def _shift_down(v, k):
    r = pltpu.roll(v, k, 0)
    row = lax.broadcasted_iota(jnp.int32, (SUBLANES, v.shape[1]), 0)
    return jnp.concatenate([jnp.where(row < k, 0.0, r[:SUBLANES]), r[SUBLANES:]], axis=0)


def _shift_up(v, k):
    n = v.shape[0]
    r = pltpu.roll(v, n - k, 0)
    row = lax.broadcasted_iota(jnp.int32, (SUBLANES, v.shape[1]), 0)
    return jnp.concatenate([r[:n - SUBLANES], jnp.where(row >= SUBLANES - k, 0.0, r[n - SUBLANES:])], axis=0)


def _conv_pre(xv, w_ref, b_ref):
    pre = b_ref[...] + w_ref[CONV_WIDTH - 1:CONV_WIDTH, :] * xv
    for k in range(1, CONV_WIDTH):
        pre = pre + w_ref[CONV_WIDTH - 1 - k:CONV_WIDTH - k, :] * _shift_down(xv, k)
    return pre


def _conv_fwd(xraw, cw, cb):
    B, S, C = xraw.shape

    def body(x_ref, w_ref, b_ref, o_ref):
        pre = _conv_pre(x_ref[0].astype(F32), w_ref, b_ref)
        o_ref[0] = (pre * _sigmoid(pre)).astype(BF16)

    blk = pl.BlockSpec((1, S, LANES), lambda b, j: (b, 0, j))
    return pl.pallas_call(
        body, grid=(B, C // LANES), name="conv_fwd",
        in_specs=[blk, pl.BlockSpec((CONV_WIDTH, LANES), lambda b, j: (0, j)), pl.BlockSpec((1, LANES), lambda b, j: (0, j))],
        out_specs=blk, out_shape=jax.ShapeDtypeStruct((B, S, C), BF16),
        compiler_params=_cparams(("arbitrary", "arbitrary")),
    )(xraw, cw, cb)


def _conv_bwd(dout, xraw, cw, cb):
    B, S, C = xraw.shape

    def body(d_ref, x_ref, w_ref, b_ref, dx_ref, dw_ref, db_ref):
        @pl.when(pl.program_id(1) == 0)
        def _():
            dw_ref[...] = jnp.zeros_like(dw_ref)
            db_ref[...] = jnp.zeros_like(db_ref)

        xv = x_ref[0].astype(F32)
        pre = _conv_pre(xv, w_ref, b_ref)
        sig = _sigmoid(pre)
        dpre = d_ref[0].astype(F32) * (sig * (1.0 + pre * (1.0 - sig)))
        dx = w_ref[CONV_WIDTH - 1:CONV_WIDTH, :] * dpre
        for k in range(1, CONV_WIDTH):
            dx = dx + w_ref[CONV_WIDTH - 1 - k:CONV_WIDTH - k, :] * _shift_up(dpre, k)
        dx_ref[0] = dx.astype(BF16)
        db_ref[...] += _colsum(dpre)
        dws = [_colsum(dpre * (xv if k == 0 else _shift_down(xv, k))) for k in range(CONV_WIDTH - 1, -1, -1)]
        dw_ref[...] += jnp.concatenate(dws, axis=0)

    blk = pl.BlockSpec((1, S, LANES), lambda j, b: (b, 0, j))
    wspec = pl.BlockSpec((CONV_WIDTH, LANES), lambda j, b: (0, j))
    bspec = pl.BlockSpec((1, LANES), lambda j, b: (0, j))
    return pl.pallas_call(
        body, grid=(C // LANES, B), name="conv_bwd",
        in_specs=[blk, blk, wspec, bspec], out_specs=[blk, wspec, bspec],
        out_shape=[jax.ShapeDtypeStruct((B, S, C), BF16), jax.ShapeDtypeStruct((CONV_WIDTH, C), F32),
                   jax.ShapeDtypeStruct((1, C), F32)],
        compiler_params=_cparams(("arbitrary", "arbitrary")),
    )(dout, xraw, cw, cb)


def _softplus(x):
    return jnp.maximum(x, 0.0) + jnp.log(1.0 + jnp.exp(-jnp.abs(x)))


def _ssd_common(xbc_ref, dtk_ref, dtb_ref, alog_ref, e_ref):
    L = CHUNK
    xbc = xbc_ref[0]
    xs = xbc[:, :D_SSD].astype(F32)
    bm = xbc[:, D_SSD:D_SSD + 256]
    cm = xbc[:, D_SSD + 256:D_SSD + 512]
    head = lax.broadcasted_iota(jnp.int32, (1, LANES), 1) < SSD_HEADS
    a128 = jnp.where(head, -jnp.exp(alog_ref[...]), 0.0)
    pre = dtk_ref[0] + dtb_ref[...]
    dt = _softplus(pre)
    dA = dt * a128
    row = lax.broadcasted_iota(jnp.int32, (L, L), 0)
    col = lax.broadcasted_iota(jnp.int32, (L, L), 1)
    causal = col <= row
    tri = causal.astype(F32)
    triT = (row <= col).astype(F32)
    tri = causal.astype(BF16)
    triT = (row <= col).astype(BF16)
    dA3 = _split3(dA)
    acum = _sum3(lambda part: _dot(tri, part), dA3)
    acumT = _sum3(lambda part: _dot_tn(part, triT), dA3)
    E = e_ref[...]
    acum_f = _spread(acum, E)
    dt_f = _spread(dt, E)
    e_f = jnp.exp(acum_f)
    w_f = jnp.exp(acum_f[L - 1:L, :] - acum_f)
    xt = xs * dt_f
    return dict(xs=xs, bm=bm, cm=cm, a128=a128, pre=pre, dt=dt, causal=causal, tri=tri, triT=triT, acum=acum,
                acumT=acumT, E=E, dt_f=dt_f, e_f=e_f, w_f=w_f, xt=xt, head=head)


def _split3(x):
    p1 = x.astype(BF16)
    r1 = x - p1.astype(F32)
    p2 = r1.astype(BF16)
    return p1, p2, (r1 - p2.astype(F32)).astype(BF16)


def _sum3(mm, parts):
    return (mm(parts[0]) + mm(parts[1])) + mm(parts[2])


def _spread(v, e):
    return _sum3(lambda part: _dot(part, e), _split3(v))


def _gather_heads(v, e):
    return _sum3(lambda part: _dot_nt(part, e), _split3(v))


def _head_mask(k):
    lane = lax.broadcasted_iota(jnp.int32, (CHUNK, LANES), 1)
    return (lane >= SSD_HEAD_DIM) if k == 1 else (lane < SSD_HEAD_DIM)


def _pair_decay(alast, h0):
    row = lax.broadcasted_iota(jnp.int32, (2 * SSD_HEAD_DIM, SSD_STATE), 0)
    return jnp.exp(jnp.where(row < SSD_HEAD_DIM, alast[:, h0:h0 + 1], alast[:, h0 + 1:h0 + 2]))


def _decay_matrix(q, h):
    seg = q["acum"][:, h:h + 1] - q["acumT"][h:h + 1, :]
    return jnp.exp(jnp.where(q["causal"], seg, -1e30))


def _gated_norm(y, zz, nw):
    sig = _sigmoid(zz)
    sil = zz * sig
    yg = y * sil
    half = D_SSD // SSD_GROUPS
    parts = []
    for g in range(SSD_GROUPS):
        xh, r = _rms_stats(yg[:, g * half:(g + 1) * half])
        parts.append((xh, r))
    return sig, sil, parts


def _ssd_fwd(xbc, dtk, z, dtb, alog, dsk, nw, expand):
    B, S, _ = xbc.shape
    L = CHUNK
    nc = S // L

    def body(xbc_ref, dtk_ref, z_ref, dtb_ref, alog_ref, dsk_ref, nw_ref, e_ref, y_ref, ys_ref, prev_ref, st_ref):
        @pl.when(pl.program_id(0) == 0)
        def _():
            st_ref[...] = jnp.zeros_like(st_ref)

        for b in range(B):
            one = lambda ref: ref.at[pl.ds(b, 1)]
            sequence_step(one(xbc_ref), one(dtk_ref), one(z_ref), dtb_ref, alog_ref, dsk_ref, nw_ref, e_ref, one(y_ref),
                          one(ys_ref), one(prev_ref), st_ref.at[b])

    def sequence_step(xbc_ref, dtk_ref, z_ref, dtb_ref, alog_ref, dsk_ref, nw_ref, e_ref, y_ref, ys_ref, prev_ref, st_ref):
        q = _ssd_common(xbc_ref, dtk_ref, dtb_ref, alog_ref, e_ref)
        xtb = q["xt"].astype(BF16)
        xwb = (q["xt"] * q["w_f"]).astype(BF16)
        alast = q["acum"][L - 1:L, :]
        ys = []
        for g in range(SSD_GROUPS):
            bg = q["bm"][:, g * 128:(g + 1) * 128]
            cg = q["cm"][:, g * 128:(g + 1) * 128]
            G = _dot_nt(cg, bg)
            for pr in range(SSD_HEADS // SSD_GROUPS // 2):
                h0 = g * 8 + 2 * pr
                lo = h0 * SSD_HEAD_DIM
                xt_p = xtb[:, lo:lo + 128]
                ydiag = jnp.zeros((L, LANES), F32)
                for k in range(2):
                    M = (G * _decay_matrix(q, h0 + k)).astype(BF16)
                    ydiag = ydiag + _dot(M, jnp.where(_head_mask(k), xt_p, jnp.zeros_like(xt_p)))
                hp = st_ref[lo:lo + 128, :]
                prev_ref[0, 0, lo:lo + 128, :] = hp.astype(BF16)
                zoff = _dot_nt(cg, hp.astype(BF16))
                ys.append(ydiag + zoff * q["e_f"][:, lo:lo + 128])
                st_ref[lo:lo + 128, :] = _pair_decay(alast, h0) * hp + _dot_tn(xwb[:, lo:lo + 128], bg)
        y = jnp.concatenate(ys, axis=1) + dsk_ref[...] * q["xs"]
        y_ref[0] = y.astype(BF16)
        _, _, parts = _gated_norm(y, z_ref[0].astype(F32), nw_ref[...])
        half = D_SSD // SSD_GROUPS
        ys_ref[0] = jnp.concatenate(
            [xh * nw_ref[:, g * half:(g + 1) * half] for g, (xh, _) in enumerate(parts)], axis=1).astype(BF16)

    chunk = lambda n: pl.BlockSpec((B, L, n), lambda c: (0, c, 0))
    vec = pl.BlockSpec((1, LANES), lambda c: (0, 0))
    return pl.pallas_call(
        body, grid=(nc,), name="ssd_fwd",
        in_specs=[chunk(D_CONV), chunk(LANES), chunk(D_SSD), vec, vec, pl.BlockSpec((1, D_SSD), lambda c: (0, 0)),
                  pl.BlockSpec((1, D_SSD), lambda c: (0, 0)), pl.BlockSpec((LANES, D_SSD), lambda c: (0, 0))],
        out_specs=[chunk(D_SSD), chunk(D_SSD), pl.BlockSpec((B, 1, D_SSD, SSD_STATE), lambda c: (0, c, 0, 0))],
        out_shape=[jax.ShapeDtypeStruct((B, S, D_SSD), BF16), jax.ShapeDtypeStruct((B, S, D_SSD), BF16),
                   jax.ShapeDtypeStruct((B, nc, D_SSD, SSD_STATE), BF16)],
        scratch_shapes=[pltpu.VMEM((B, D_SSD, SSD_STATE), F32)],
        compiler_params=_cparams(("arbitrary",)),
    )(xbc, dtk, z, dtb, alog, dsk, nw, expand)


def _ssd_bwd(xbc, dtk, z, y, prev, dys, dtb, alog, dsk, nw, expand):
    B, S, _ = xbc.shape
    L = CHUNK
    nc = S // L
    half = D_SSD // SSD_GROUPS

    def body(xbc_ref, dtk_ref, z_ref, y_ref, prev_ref, dys_ref, dtb_ref, alog_ref, dsk_ref, nw_ref, e_ref,
             dxbc_ref, ddtk_ref, dz_ref, dnw_ref, dvec_ref, dh_ref, dskc_ref):
        @pl.when(pl.program_id(0) == 0)
        def _():
            dnw_ref[...] = jnp.zeros_like(dnw_ref)
            dvec_ref[...] = jnp.zeros_like(dvec_ref)
            dskc_ref[...] = jnp.zeros_like(dskc_ref)
            dh_ref[...] = jnp.zeros_like(dh_ref)

        for b in range(B):
            one = lambda ref: ref.at[pl.ds(b, 1)]
            sequence_step(one(xbc_ref), one(dtk_ref), one(z_ref), one(y_ref), one(prev_ref), one(dys_ref), dtb_ref, alog_ref,
                          dsk_ref, nw_ref, e_ref, one(dxbc_ref), one(ddtk_ref), one(dz_ref), dnw_ref, dvec_ref, dh_ref.at[b],
                          dskc_ref)

        @pl.when(pl.program_id(0) == nc - 1)
        def _():
            dvec_ref[2:3, :] = _gather_heads(jnp.broadcast_to(dskc_ref[...], (8, D_SSD)), e_ref[...])[0:1, :]

    def sequence_step(xbc_ref, dtk_ref, z_ref, y_ref, prev_ref, dys_ref, dtb_ref, alog_ref, dsk_ref, nw_ref, e_ref,
                      dxbc_ref, ddtk_ref, dz_ref, dnw_ref, dvec_ref, dh_ref, dskc_ref):
        q = _ssd_common(xbc_ref, dtk_ref, dtb_ref, alog_ref, e_ref)
        E = q["E"]
        xs = q["xs"]
        yv = y_ref[0].astype(F32)
        zz = z_ref[0].astype(F32)
        sig, sil, parts = _gated_norm(yv, zz, nw_ref[...])
        dn = dys_ref[0].astype(F32)
        dyg, dnw_rows = [], []
        for g, (xh, r) in enumerate(parts):
            dpart, dw_rows = _rms_bwd(dn[:, g * half:(g + 1) * half], xh, r, nw_ref[:, g * half:(g + 1) * half])
            dyg.append(dpart)
            dnw_rows.append(dw_rows)
        dyg = jnp.concatenate(dyg, axis=1)
        dnw_ref[...] += _colsum(jnp.concatenate(dnw_rows, axis=1))
        dY = dyg * sil
        dz_ref[0] = (dyg * yv * (sig * (1.0 + zz * (1.0 - sig)))).astype(BF16)
        dsk_f = dsk_ref[...]
        dskc_ref[...] += _colsum(dY * xs)
        dYb = dY.astype(BF16)
        xtb = q["xt"].astype(BF16)
        xwb = (q["xt"] * q["w_f"]).astype(BF16)
        acum = q["acum"]
        alast = acum[L - 1:L, :]
        lane_id = lax.broadcasted_iota(jnp.int32, (L, LANES), 1)
        sub_id = lax.broadcasted_iota(jnp.int32, (LANES, L), 0)
        lane_row = lax.broadcasted_iota(jnp.int32, (1, LANES), 1)
        da_rows = jnp.zeros((L, LANES), F32)
        daT = jnp.zeros((LANES, L), F32)
        dxt, prod_off, prod_st, dbs, dcs = [], [], [], [], []
        hsum_row = jnp.zeros((1, LANES), F32)
        for g in range(SSD_GROUPS):
            bg = q["bm"][:, g * 128:(g + 1) * 128]
            cg = q["cm"][:, g * 128:(g + 1) * 128]
            G = _dot_nt(cg, bg)
            dG = jnp.zeros((L, L), F32)
            dcg = jnp.zeros((L, SSD_STATE), F32)
            dbg = jnp.zeros((L, SSD_STATE), F32)
            for pr in range(SSD_HEADS // SSD_GROUPS // 2):
                h0 = g * 8 + 2 * pr
                lo = h0 * SSD_HEAD_DIM
                cols = slice(lo, lo + 128)
                dY_p = dYb[:, cols]
                xt_p = xtb[:, cols]
                dxt_p = jnp.zeros((L, LANES), F32)
                for k in range(2):
                    h = h0 + k
                    Lm = _decay_matrix(q, h)
                    Mf = G * Lm
                    dYk = jnp.where(_head_mask(k), dY_p, jnp.zeros_like(dY_p))
                    dM = _dot_nt(dYk, xt_p)
                    dxt_p = dxt_p + _dot_tn(Mf.astype(BF16), dYk)
                    dG = dG + dM * Lm
                    Q = dM * Mf
                    da_rows = da_rows + jnp.where(lane_id == h, jnp.sum(Q, axis=1, keepdims=True), 0.0)
                    daT = daT + jnp.where(sub_id == h, jnp.sum(Q, axis=0, keepdims=True), 0.0)
                hpb = prev_ref[0, 0, lo:lo + 128, :]
                hp = hpb.astype(F32)
                zoff = _dot_nt(cg, hpb)
                e_p = q["e_f"][:, cols]
                dY_pf = dY[:, cols]
                dZb = (dY_pf * e_p).astype(BF16)
                dcg = dcg + _dot(dZb, hpb)
                dhp_off = _dot_tn(dZb, cg)
                prod_off.append(dY_pf * zoff * e_p)
                dS = dh_ref[lo:lo + 128, :]
                dSb = dS.astype(BF16)
                U = _dot_nt(bg, dSb)
                dxt_p = dxt_p + U * q["w_f"][:, cols]
                dbg = dbg + _dot(xwb[:, cols], dSb)
                prod_st.append(q["xt"][:, cols] * U)
                dh_ref[lo:lo + 128, :] = _pair_decay(alast, h0) * dS + dhp_off
                dsh = dS * hp
                for k in range(2):
                    total = jnp.sum(dsh[k * SSD_HEAD_DIM:(k + 1) * SSD_HEAD_DIM, :], axis=(0, 1), keepdims=True)
                    hsum_row = hsum_row + jnp.where(lane_row == h0 + k, total, 0.0)
                dxt.append(dxt_p)
            dGb = dG.astype(BF16)
            dcs.append(dcg + _dot(dGb, bg))
            dbs.append(dbg + _dot_tn(dGb, cg))
        dxt = jnp.concatenate(dxt, axis=1)
        da_rows = da_rows + _gather_heads(jnp.concatenate(prod_off, axis=1), E)
        dww = _gather_heads(jnp.concatenate(prod_st, axis=1), E) * jnp.exp(alast - acum)
        da_rows = da_rows - dww
        dlast = _colsum(dww) + jnp.exp(alast) * hsum_row
        triT = q["triT"]
        ddA = (_sum3(lambda part: _dot(triT, part), _split3(da_rows))
               - _sum3(lambda part: _dot_nt(triT, part), _split3(daT)) + dlast)
        ddA = jnp.where(q["head"], ddA, 0.0)
        ddt = ddA * q["a128"] + _gather_heads(dxt * xs, E)
        ddt_raw = jnp.where(q["head"], ddt * _sigmoid(q["pre"]), 0.0)
        ddtk_ref[0] = ddt_raw
        dxs = dxt * q["dt_f"] + dsk_f * dY
        dxbc_ref[0] = jnp.concatenate([dxs] + dbs + dcs, axis=1).astype(BF16)
        dvec_ref[0:1, :] += _colsum(ddt_raw)
        dvec_ref[1:2, :] += _colsum(ddA * q["dt"]) * q["a128"]

    rev = lambda n: pl.BlockSpec((B, L, n), lambda c: (0, nc - 1 - c, 0))
    vec = pl.BlockSpec((1, LANES), lambda c: (0, 0))
    sd = jax.ShapeDtypeStruct
    return pl.pallas_call(
        body, grid=(nc,), name="ssd_bwd",
        in_specs=[rev(D_CONV), rev(LANES), rev(D_SSD), rev(D_SSD),
                  pl.BlockSpec((B, 1, D_SSD, SSD_STATE), lambda c: (0, nc - 1 - c, 0, 0)), rev(D_SSD), vec, vec,
                  pl.BlockSpec((1, D_SSD), lambda c: (0, 0)),
                  pl.BlockSpec((1, D_SSD), lambda c: (0, 0)), pl.BlockSpec((LANES, D_SSD), lambda c: (0, 0))],
        out_specs=[rev(D_CONV), rev(LANES), rev(D_SSD), pl.BlockSpec((1, D_SSD), lambda c: (0, 0)),
                   pl.BlockSpec((8, LANES), lambda c: (0, 0))],
        out_shape=[sd((B, S, D_CONV), BF16), sd((B, S, LANES), F32), sd((B, S, D_SSD), BF16), sd((1, D_SSD), F32),
                   sd((8, LANES), F32)],
        scratch_shapes=[pltpu.VMEM((B, D_SSD, SSD_STATE), F32), pltpu.VMEM((1, D_SSD), F32)],
        compiler_params=_cparams(("arbitrary",)),
    )(xbc, dtk, z, y, prev, dys, dtb, alog, dsk, nw, expand)


def _rope_tables(pos_ref, invf_ref, place_ref):
    ang = invf_ref[...] * pos_ref[0].astype(F32)
    place = place_ref[...]
    cosf = 1.0 + _sum3(lambda part: _dot_tn(part, place), _split3(jnp.cos(ang) - 1.0))
    sinf = _sum3(lambda part: _dot_tn(part, place), _split3(jnp.sin(ang)))
    return cosf, sinf


def _rot(u):
    lane = lax.broadcasted_iota(jnp.int32, u.shape, 1)
    first = (lane >= QK_NOPE) & (lane < QK_NOPE + QK_ROPE // 2)
    second = (lane >= QK_NOPE + QK_ROPE // 2) & (lane < QK_DIM)
    return jnp.where(first, -pltpu.roll(u, LANES - QK_ROPE // 2, 1), jnp.where(second, pltpu.roll(u, QK_ROPE // 2, 1), 0.0))


def _rope_lanes(shape):
    lane = lax.broadcasted_iota(jnp.int32, shape, 1)
    return (lane >= QK_NOPE) & (lane < QK_DIM)


def _mla_prep(cq, ckv, dtk, pos, qw, kvw, wuq, wukv, invf, place):
    T = cq.shape[0]
    tm = min(WIDE_TOKEN_TILE, T)
    scale = 1.0 / math.sqrt(QK_DIM)
    HW = MLA_HEADS * HEAD_LANES

    def body(cq_ref, ckv_ref, dtk_ref, pos_ref, qw_ref, kvw_ref, wuq_ref, wukv_ref, invf_ref, place_ref, q_ref, k_ref, v_ref,
             cos_ref, sin_ref):
        xh, _ = _rms_stats(cq_ref[...])
        qv = _dot((xh * qw_ref[...]).astype(BF16), wuq_ref[...])
        xh, _ = _rms_stats(ckv_ref[...])
        kv = _dot((xh * kvw_ref[...]).astype(BF16), wukv_ref[...])
        cosf, sinf = _rope_tables(pos_ref, invf_ref, place_ref)
        cos_ref[...] = cosf
        sin_ref[...] = sinf
        rope = lambda u: u * cosf + _rot(u) * sinf
        dtkv = dtk_ref[...]
        kr = rope(jnp.where(_rope_lanes(dtkv.shape), dtkv, 0.0))
        for h in range(MLA_HEADS):
            cols = slice(h * HEAD_LANES, (h + 1) * HEAD_LANES)
            q_ref[:, cols] = (rope(qv[:, cols]) * scale).astype(BF16)
            k_ref[:, cols] = (kv[:, cols] + kr).astype(BF16)
        v_ref[...] = kv[:, HW:].astype(BF16)

    rows = lambda n: pl.BlockSpec((tm, n), lambda i: (i, 0))
    return pl.pallas_call(
        body, grid=(T // tm,), name="mla_prep",
        in_specs=[rows(Q_LORA), rows(KV_LORA), rows(LANES), pl.BlockSpec((1, 1, tm), lambda i: (i, 0, 0)),
                  _resident((1, Q_LORA)), _resident((1, KV_LORA)), _resident((Q_LORA, HW)), _resident((KV_LORA, 2 * HW)),
                  _resident((QK_ROPE // 2, 1)), _resident((QK_ROPE // 2, LANES))],
        out_specs=[rows(HW), rows(HW), rows(HW), rows(LANES), rows(LANES)],
        out_shape=[jax.ShapeDtypeStruct((T, HW), BF16)] * 3 + [jax.ShapeDtypeStruct((T, LANES), F32)] * 2,
        compiler_params=_cparams(("arbitrary",)),
    )(cq, ckv, dtk, pos.reshape(T // tm, 1, tm), qw, kvw, wuq, wukv, invf, place)


def _mla_prep_bwd(dq, dk, dv, cq, ckv, cos_t, sin_t, qw, kvw, wuq, wukv):
    T = cq.shape[0]
    tm = min(WIDE_TOKEN_TILE, T)
    scale = 1.0 / math.sqrt(QK_DIM)
    HW = MLA_HEADS * HEAD_LANES

    def body(dq_ref, dk_ref, dv_ref, cq_ref, ckv_ref, cos_ref, sin_ref, qw_ref, kvw_ref, wuq_ref, wukv_ref,
             dcq_ref, dckv_ref, ddtk_ref, qn_ref, kvn_ref, dqo_ref, dkvo_ref, dqw_ref, dkvw_ref):
        @pl.when(pl.program_id(0) == 0)
        def _():
            dqw_ref[...] = jnp.zeros_like(dqw_ref)
            dkvw_ref[...] = jnp.zeros_like(dkvw_ref)

        cosf, sinf = cos_ref[...], sin_ref[...]
        unrope = lambda d: d * cosf - _rot(d * sinf)
        dkr = jnp.zeros((tm, LANES), F32)
        nope = lax.broadcasted_iota(jnp.int32, (tm, LANES), 1) < QK_NOPE
        for h in range(MLA_HEADS):
            cols = slice(h * HEAD_LANES, (h + 1) * HEAD_LANES)
            dqo_ref[:, cols] = unrope(dq_ref[:, cols].astype(F32) * scale).astype(BF16)
            dkh = dk_ref[:, cols].astype(F32)
            dkr = dkr + jnp.where(_rope_lanes(dkh.shape), dkh, 0.0)
            dkvo_ref[:, cols] = jnp.where(nope, dkh, 0.0).astype(BF16)
        dkvo_ref[:, HW:] = dv_ref[...].astype(BF16)
        ddtk_ref[...] = unrope(dkr)
        xh, r = _rms_stats(cq_ref[...])
        qn_ref[...] = (xh * qw_ref[...]).astype(BF16)
        dx, dw_rows = _rms_bwd(_dot_nt(dqo_ref[...], wuq_ref[...]), xh, r, qw_ref[...])
        dcq_ref[...] = dx
        dqw_ref[...] += _colsum(dw_rows)
        xh, r = _rms_stats(ckv_ref[...])
        kvn_ref[...] = (xh * kvw_ref[...]).astype(BF16)
        dx, dw_rows = _rms_bwd(_dot_nt(dkvo_ref[...], wukv_ref[...]), xh, r, kvw_ref[...])
        dckv_ref[...] = dx
        dkvw_ref[...] += _colsum(dw_rows)

    rows = lambda n: pl.BlockSpec((tm, n), lambda i: (i, 0))
    sd = jax.ShapeDtypeStruct
    return pl.pallas_call(
        body, grid=(T // tm,), name="mla_prep_bwd",
        in_specs=[rows(HW), rows(HW), rows(HW), rows(Q_LORA), rows(KV_LORA), rows(LANES), rows(LANES), _resident((1, Q_LORA)),
                  _resident((1, KV_LORA)), _resident((Q_LORA, HW)), _resident((KV_LORA, 2 * HW))],
        out_specs=[rows(Q_LORA), rows(KV_LORA), rows(LANES), rows(Q_LORA), rows(KV_LORA), rows(HW), rows(2 * HW),
                   pl.BlockSpec((1, Q_LORA), lambda i: (0, 0)), pl.BlockSpec((1, KV_LORA), lambda i: (0, 0))],
        out_shape=[sd((T, Q_LORA), F32), sd((T, KV_LORA), F32), sd((T, LANES), F32), sd((T, Q_LORA), BF16),
                   sd((T, KV_LORA), BF16), sd((T, HW), BF16), sd((T, 2 * HW), BF16), sd((1, Q_LORA), F32),
                   sd((1, KV_LORA), F32)],
        compiler_params=_cparams(("arbitrary",)),
    )(dq, dk, dv, cq, ckv, cos_t, sin_t, qw, kvw, wuq, wukv)


def _causal_mask(t):
    row = lax.broadcasted_iota(jnp.int32, (t, t), 0)
    col = lax.broadcasted_iota(jnp.int32, (t, t), 1)
    return col <= row


def _attn_fwd(q, k, v):
    B, S, HW = q.shape
    H = HW // HEAD_LANES
    t = min(ATTN_FWD_Q_TILE, S)
    tk = min(ATTN_FWD_KV_TILE, t)
    nq = S // t
    per = t // tk

    pair = 2
    pw = pair * HEAD_LANES

    def body(q_ref, k_ref, v_ref, o_ref, lse_ref):
        qi = pl.program_id(2)
        lanes = [slice(hh * HEAD_LANES, (hh + 1) * HEAD_LANES) for hh in range(pair)]
        qs = [q_ref[0, :, cols] for cols in lanes]

        def step(j, carry, diag):
            sl = pl.ds(pl.multiple_of(j * tk, tk), tk)
            out = []
            for qv, cols, (m, l, acc) in zip(qs, lanes, carry):
                s = _dot_nt(qv, k_ref[0, sl, cols])
                if diag is not None:
                    row = lax.broadcasted_iota(jnp.int32, (t, tk), 0)
                    col = lax.broadcasted_iota(jnp.int32, (t, tk), 1)
                    s = jnp.where(col + diag * tk <= row, s, -1e30)
                m_new = jnp.maximum(m, jnp.max(s, axis=-1, keepdims=True))
                alpha = jnp.exp(m - m_new)
                p = jnp.exp(s - m_new)
                l = alpha * l + jnp.sum(p, axis=-1, keepdims=True)
                acc = alpha * acc + _dot(p.astype(BF16), v_ref[0, sl, cols])
                out.append((m_new, l, acc))
            return tuple(out)

        init = tuple((jnp.full((t, 1), -1e30, F32), jnp.zeros((t, 1), F32), jnp.zeros((t, HEAD_LANES), F32))
                     for _ in range(pair))
        carry = lax.fori_loop(0, qi * per, lambda j, c: step(j, c, None), init)
        for d in range(per):
            carry = step(qi * per + d, carry, d)
        for hh, (m, l, acc) in enumerate(carry):
            o_ref[0, :, lanes[hh]] = (acc / l).astype(BF16)
            lse_ref[0, hh] = m + jnp.log(l)

    return pl.pallas_call(
        body, grid=(B, H // pair, nq), name="attn_fwd",
        in_specs=[pl.BlockSpec((1, t, pw), lambda b, h, i: (b, i, h)),
                  pl.BlockSpec((1, S, pw), lambda b, h, i: (b, 0, h)),
                  pl.BlockSpec((1, S, pw), lambda b, h, i: (b, 0, h))],
        out_specs=[pl.BlockSpec((1, t, pw), lambda b, h, i: (b, i, h)),
                   pl.BlockSpec((1, pair, t, 1), lambda b, h, i: (b, h, i, 0))],
        out_shape=[jax.ShapeDtypeStruct((B, S, HW), BF16), jax.ShapeDtypeStruct((B, H, S, 1), F32)],
        compiler_params=_cparams(("arbitrary", "arbitrary", "arbitrary")),
    )(q, k, v)


def _attn_bwd(q, k, v, o, do, lse):
    B, S, HW = q.shape
    H = HW // HEAD_LANES
    t = min(ATTN_BWD_TILE, S)
    nq = S // t

    pair = 2
    pw = pair * HEAD_LANES

    def body(q_ref, k_ref, v_ref, o_ref, do_ref, lse_ref, dq_out_ref, dk_ref, dv_ref, dq_ref):
        j = pl.program_id(2)

        @pl.when(j == 0)
        def _():
            dq_ref[...] = jnp.zeros_like(dq_ref)

        lanes = [slice(hh * HEAD_LANES, (hh + 1) * HEAD_LANES) for hh in range(pair)]

        def step(i, carry, masked):
            sl = pl.ds(pl.multiple_of(i * t, t), t)
            out = []
            for hh, (cols, (dk, dv)) in enumerate(zip(lanes, carry)):
                kj = k_ref[0, :, cols]
                qi = q_ref[0, sl, cols]
                doi = do_ref[0, sl, cols]
                s = _dot_nt(qi, kj)
                if masked:
                    s = jnp.where(_causal_mask(t), s, -1e30)
                p = jnp.exp(s - lse_ref[0, hh, sl, :])
                dv = dv + _dot_tn(p.astype(BF16), doi)
                dp = _dot_nt(doi, v_ref[0, :, cols])
                delta = jnp.sum(doi.astype(F32) * o_ref[0, sl, cols].astype(F32), axis=-1, keepdims=True)
                dsb = (p * (dp - delta)).astype(BF16)
                dk = dk + _dot_tn(dsb, qi)
                dq_ref[sl, cols] += _dot(dsb, kj)
                out.append((dk, dv))
            return tuple(out)

        zero = jnp.zeros((t, HEAD_LANES), F32)
        carry = step(j, ((zero, zero),) * pair, True)
        carry = lax.fori_loop(j + 1, nq, lambda i, c: step(i, c, False), carry)
        for cols, (dk, dv) in zip(lanes, carry):
            dk_ref[0, :, cols] = dk.astype(BF16)
            dv_ref[0, :, cols] = dv.astype(BF16)

        @pl.when(j == nq - 1)
        def _():
            dq_out_ref[0] = dq_ref[...].astype(BF16)

    full = pl.BlockSpec((1, S, pw), lambda b, h, j: (b, 0, h))
    tile = pl.BlockSpec((1, t, pw), lambda b, h, j: (b, j, h))
    sd = jax.ShapeDtypeStruct
    return pl.pallas_call(
        body, grid=(B, H // pair, nq), name="attn_bwd",
        in_specs=[full, tile, tile, full, full, pl.BlockSpec((1, pair, S, 1), lambda b, h, j: (b, h, 0, 0))],
        out_specs=[full, tile, tile],
        out_shape=[sd((B, S, HW), BF16), sd((B, S, HW), BF16), sd((B, S, HW), BF16)],
        scratch_shapes=[pltpu.VMEM((S, pw), F32)],
        compiler_params=_cparams(("arbitrary", "arbitrary", "arbitrary")),
    )(q, k, v, o, do, lse)


def _mix_out(x1, yssd, o, mw, wout, g, seq):
    T, D = x1.shape
    tm = min(WIDE_TOKEN_TILE, seq)
    tps = seq // tm

    def body(x_ref, ys_ref, o_ref, mw_ref, w_ref, g_ref, xo_ref, m_ref, yc_ref):
        xh, _ = _rms_stats(o_ref[...].astype(F32))
        ycat = jnp.concatenate([ys_ref[...], (xh * mw_ref[...]).astype(BF16)], axis=1)
        m = _dot(ycat, w_ref[...])
        xo_ref[...] = x_ref[...] + g_ref[0] * m
        m_ref[...] = m.astype(BF16)
        yc_ref[...] = ycat

    rows = lambda n: pl.BlockSpec((tm, n), lambda i: (i, 0))
    perb = pl.BlockSpec((1, 1, D), lambda i: (i // tps, 0, 0))
    sd = jax.ShapeDtypeStruct
    return pl.pallas_call(
        body, grid=(T // tm,), name="mix_out",
        in_specs=[rows(D), rows(D_SSD), rows(D_MLA), _resident((1, D_MLA)), _resident((D_SSD + D_MLA, D)), perb],
        out_specs=[rows(D), rows(D), rows(D_SSD + D_MLA)],
        out_shape=[sd((T, D), F32), sd((T, D), BF16), sd((T, D_SSD + D_MLA), BF16)],
        compiler_params=_cparams(("arbitrary",)),
    )(x1, yssd, o, mw, wout, g)


def _mix_out_bwd(dx2, m, o, mw, wout, g, seq):
    T, D = dx2.shape
    B = T // seq
    tm = min(WIDE_TOKEN_TILE, seq)
    tps = seq // tm

    def body(dx_ref, m_ref, o_ref, mw_ref, w_ref, g_ref, dys_ref, do_ref, dm_ref, dg_ref, dmw_ref):
        i = pl.program_id(0)

        @pl.when(i % tps == 0)
        def _():
            dg_ref[...] = jnp.zeros_like(dg_ref)

        @pl.when(i == 0)
        def _():
            dmw_ref[...] = jnp.zeros_like(dmw_ref)

        dxv = dx_ref[...]
        dg_ref[0] += _colsum(dxv * m_ref[...].astype(F32))
        dmb = (g_ref[0] * dxv).astype(BF16)
        dm_ref[...] = dmb
        dycat = _dot_nt(dmb, w_ref[...])
        dys_ref[...] = dycat[:, :D_SSD].astype(BF16)
        xh, r = _rms_stats(o_ref[...].astype(F32))
        dx, dw_rows = _rms_bwd(dycat[:, D_SSD:], xh, r, mw_ref[...])
        do_ref[...] = dx.astype(BF16)
        dmw_ref[...] += _colsum(dw_rows)

    rows = lambda n: pl.BlockSpec((tm, n), lambda i: (i, 0))
    perb = pl.BlockSpec((1, 1, D), lambda i: (i // tps, 0, 0))
    sd = jax.ShapeDtypeStruct
    return pl.pallas_call(
        body, grid=(T // tm,), name="mix_out_bwd",
        in_specs=[rows(D), rows(D), rows(D_MLA), _resident((1, D_MLA)), _resident((D_SSD + D_MLA, D)), perb],
        out_specs=[rows(D_SSD), rows(D_MLA), rows(D), perb, pl.BlockSpec((1, D_MLA), lambda i: (0, 0))],
        out_shape=[sd((T, D_SSD), BF16), sd((T, D_MLA), BF16), sd((T, D), BF16), sd((B, 1, D), F32), sd((1, D_MLA), F32)],
        compiler_params=_cparams(("arbitrary",)),
    )(dx2, m, o, mw, wout, g)


def _win_to_kernel(w):
    z0 = jnp.zeros((48, w.shape[1]), w.dtype)
    z1 = jnp.zeros((32, w.shape[1]), w.dtype)
    return jnp.concatenate([w[:2560], w[2576:3216], w[2560:2576], z0, w[3216:3248], z1], axis=0)


def _win_from_kernel(g):
    return jnp.concatenate([g[:2560], g[3200:3216], g[2560:3200], g[3264:3296]], axis=0)


def _wuq_to_kernel(w):
    w = w.reshape(Q_LORA, MLA_HEADS, QK_DIM)
    return jnp.pad(w, ((0, 0), (0, 0), (0, HEAD_LANES - QK_DIM))).reshape(Q_LORA, MLA_HEADS * HEAD_LANES)


def _wuq_from_kernel(g):
    return g.reshape(Q_LORA, MLA_HEADS, HEAD_LANES)[:, :, :QK_DIM].reshape(Q_LORA, MLA_HEADS * QK_DIM)


def _wukv_to_kernel(w):
    w = w.reshape(KV_LORA, MLA_HEADS, QK_NOPE + V_HEAD)
    kp = jnp.pad(w[:, :, :QK_NOPE], ((0, 0), (0, 0), (0, HEAD_LANES - QK_NOPE)))
    return jnp.concatenate([kp.reshape(KV_LORA, -1), w[:, :, QK_NOPE:].reshape(KV_LORA, -1)], axis=1)


def _wukv_from_kernel(g):
    hw = MLA_HEADS * HEAD_LANES
    kp = g[:, :hw].reshape(KV_LORA, MLA_HEADS, HEAD_LANES)[:, :, :QK_NOPE]
    vp = g[:, hw:].reshape(KV_LORA, MLA_HEADS, V_HEAD)
    return jnp.concatenate([kp, vp], axis=2).reshape(KV_LORA, MLA_HEADS * (QK_NOPE + V_HEAD))


def _lanes16(v):
    return jnp.pad(v.reshape(1, SSD_HEADS), ((0, 0), (0, LANES - SSD_HEADS)))


def _constants():
    e = np.zeros((LANES, D_SSD), np.float32)
    for h in range(SSD_HEADS):
        e[h, h * SSD_HEAD_DIM:(h + 1) * SSD_HEAD_DIM] = 1.0
    inv_freq = ROPE_THETA ** (-jnp.arange(0, QK_ROPE, 2, dtype=F32) / QK_ROPE)
    half = QK_ROPE // 2
    place = np.zeros((half, LANES), np.float32)
    for j in range(half):
        place[j, QK_NOPE + j] = place[j, QK_NOPE + half + j] = 1.0
    return jnp.asarray(e, BF16), inv_freq.reshape(half, 1), jnp.asarray(place, BF16)


def _local_step(x, positions, mod, w, later_weights, small, tgt, on_grads, sync):
    B, S, D = x.shape
    T = B * S
    expand, invf, place = _constants()
    x0 = x.reshape(T, D)
    pos = positions.reshape(T)
    mods = [mod[:, i * D:(i + 1) * D].reshape(B, 1, D) for i in range(N_MOD)]
    sh1, sc1, g1, sh2, sc2, g2, sh3, sc3, g3 = mods
    dtb, alog = _lanes16(small["dt_bias"]), _lanes16(small["a_log"])
    dsk = jnp.repeat(small["d_skip"].reshape(1, SSD_HEADS), SSD_HEAD_DIM, axis=1)

    x1, a1, u1, f1 = _ffn_fwd(x0, small["norm_ffn1"], sh1, sc1, g1, w["ffn1_w_gate"], w["ffn1_w_up"], w["ffn1_w_down"], S, "ffn1_fwd")
    w = {**w, **later_weights(f1)}
    z, xraw, cq, ckv, dtk = _inproj_fwd(x1, small["norm_mix"], sh2, sc2, w["w_in"], S)
    xraw3 = xraw.reshape(B, S, D_CONV)
    xbc = _conv_fwd(xraw3, small["conv_w"], small["conv_b"])
    dtk3, z3 = dtk.reshape(B, S, LANES), z.reshape(B, S, D_SSD)
    y, yssd, prev = _ssd_fwd(xbc, dtk3, z3, dtb, alog, dsk, small["ssd_norm_w"], expand)
    q, k, v, cos_t, sin_t = _mla_prep(cq, ckv, dtk, pos, small["q_norm_w"], small["kv_norm_w"], w["w_uq"], w["w_ukv"], invf,
                                      place)
    hw = MLA_HEADS * HEAD_LANES
    q3, k3, v3 = q.reshape(B, S, hw), k.reshape(B, S, hw), v.reshape(B, S, hw)
    o3, lse = _attn_fwd(q3, k3, v3)
    o = o3.reshape(T, hw)
    x2, m, ycat = _mix_out(x1, yssd.reshape(T, D_SSD), o, small["mla_norm_w"], w["w_out"], g2, S)
    dx3, a2, u2, f2, loss, d_norm_final = _ffn_fwd(
        x2, small["norm_ffn2"], sh3, sc3, g3, w["ffn2_w_gate"], w["ffn2_w_up"], w["ffn2_w_down"], S, "ffn2_fwd",
        head=(small["norm_final"].reshape(1, D), tgt.reshape(T, D)))

    gw, gs = {}, {}
    dx2, h3, s3, df3, da3, du3, dsh3, dsc3, dg3, gs["norm_ffn2"] = _ffn_bwd(
        dx3, x2, small["norm_ffn2"], sh3, sc3, g3, a2, u2, f2, w["ffn2_w_gate"], w["ffn2_w_up"], w["ffn2_w_down"], S, "ffn2_bwd")
    gw["ffn2_w_gate"], gw["ffn2_w_up"], gw["ffn2_w_down"] = _ffn_wgrad(h3, s3, df3, da3, du3, dsh3, "ffn2_wgrad")
    g2 = g2 + on_grads(("ffn2_w_gate", "ffn2_w_up", "ffn2_w_down"), gw)

    dys, do, dm, dg2, gs["mla_norm_w"] = _mix_out_bwd(dx2, m, o, small["mla_norm_w"], w["w_out"], g2, S)
    gw["w_out"] = _mm_tn(ycat, dm, 512, "dwout")

    dq3, dk3, dv3 = _attn_bwd(q3, k3, v3, o3, do.reshape(B, S, hw), lse)
    dcq, dckv, ddtk_b, qn, kvn, dqb, dkvb, gs["q_norm_w"], gs["kv_norm_w"] = _mla_prep_bwd(
        dq3.reshape(T, hw), dk3.reshape(T, hw), dv3.reshape(T, hw), cq, ckv, cos_t, sin_t, small["q_norm_w"] + sync(dq3),
        small["kv_norm_w"], w["w_uq"], w["w_ukv"])
    gw["w_uq"] = _mm_tn(qn, dqb, 512, "dwuq")
    gw["w_ukv"] = _mm_tn(kvn, dkvb, 1024, "dwukv")

    dxbc, ddtk_a, dz, gs["ssd_norm_w"], dvec = _ssd_bwd(
        xbc, dtk3, z3, y, prev, dys.reshape(B, S, D_SSD), dtb, alog, dsk, small["ssd_norm_w"], expand)
    gs["dt_bias"], gs["a_log"], gs["d_skip"] = dvec[0:1, :SSD_HEADS], dvec[1:2, :SSD_HEADS], dvec[2:3, :SSD_HEADS]
    dxraw, gs["conv_w"], gs["conv_b"] = _conv_bwd(dxbc, xraw3, small["conv_w"], small["conv_b"])
    dx1, h2, dproj, dsh2, dsc2, gs["norm_mix"] = _inproj_bwd(
        dx2, x1, small["norm_mix"], sh2, sc2, w["w_in"], dz.reshape(T, D_SSD), dxraw.reshape(T, D_CONV), dcq, dckv,
        ddtk_a.reshape(T, LANES), ddtk_b, S)
    gw["w_in"] = _mm_tn(dproj, h2, 512, "dwin")
    g1 = g1 + on_grads(("w_in", "w_uq", "w_ukv", "w_out"), gw)

    dx0, h1, s1, df1, da1, du1, dsh1, dsc1, dg1, gs["norm_ffn1"] = _ffn_bwd(
        dx1, x0, small["norm_ffn1"], sh1, sc1, g1, a1, u1, f1, w["ffn1_w_gate"], w["ffn1_w_up"], w["ffn1_w_down"], S, "ffn1_bwd")
    gw["ffn1_w_gate"], gw["ffn1_w_up"], gw["ffn1_w_down"] = _ffn_wgrad(h1, s1, df1, da1, du1, dsh1 + sync(dx0), "ffn1_wgrad")
    gs["norm_final"] = d_norm_final
    dmod = jnp.concatenate([t.reshape(B, D) for t in (dsh1, dsc1, dg1, dsh2, dsc2, dg2, dsh3, dsc3, dg3)], axis=1)
    return loss, dx0.reshape(B, S, D), gw, dmod, gs


HBM_SPEC = pl.BlockSpec(memory_space=pltpu.HBM)
VMEM_SPEC = pl.BlockSpec(memory_space=pltpu.VMEM)


def _place():
    return lax.axis_index("x"), lax.axis_index("y"), lax.axis_index("c")


def _other_chips(mx, my):
    return [(1 - mx, my), (mx, 1 - my), (1 - mx, 1 - my)]


def _remote(src, dst, send_sem, recv_sem, to):
    return pltpu.make_async_remote_copy(src_ref=src, dst_ref=dst, send_sem=send_sem, recv_sem=recv_sem,
                                        device_id=to, device_id_type=MESH)


def _all_gather_small(xa, name):
    r, n = xa.shape

    def body(x_ref, o_ref, token, send_sems, recv_sems):
        mx, my, mc = _place()
        me = 4 * mx + 2 * my + mc
        token[...] = jnp.zeros_like(token)
        o_ref[pl.ds(me, 1)] = x_ref[...][None]
        sends = []
        for k in range(1, N_DEV):
            peer = (mx ^ (k >> 2), my ^ ((k >> 1) & 1), mc ^ (k & 1))
            cp = _remote(x_ref, o_ref.at[me], send_sems.at[k - 1], recv_sems.at[k - 1], peer)
            cp.start()
            sends.append(cp)
        for k in range(1, N_DEV):
            peer = (mx ^ (k >> 2), my ^ ((k >> 1) & 1), mc ^ (k & 1))
            slot = 4 * peer[0] + 2 * peer[1] + peer[2]
            _remote(x_ref, o_ref.at[slot], send_sems.at[k - 1], recv_sems.at[k - 1], peer).wait_recv()
        for cp in sends:
            cp.wait_send()

    return pl.pallas_call(
        body, name=name, in_specs=[VMEM_SPEC], out_specs=[VMEM_SPEC, VMEM_SPEC],
        out_shape=[jax.ShapeDtypeStruct((N_DEV, r, n), xa.dtype), jax.ShapeDtypeStruct((8, LANES), F32)],
        scratch_shapes=[pltpu.SemaphoreType.DMA((N_DEV - 1,)), pltpu.SemaphoreType.DMA((N_DEV - 1,))],
        compiler_params=pltpu.CompilerParams(vmem_limit_bytes=VMEM_LIMIT),
    )(xa)


def _halves_by_rows(shape):
    return (shape[-2] // 2) % 16 == 0


def _half_shape(shape):
    r, c = shape[-2:]
    return tuple(shape[:-2]) + ((r // 2, c) if _halves_by_rows(shape) else (r, c // 2))


def _half_index(shape, hc):
    r, c = shape[-2:]
    if _halves_by_rows(shape):
        return (pl.ds(pl.multiple_of(hc * (r // 2), 16), r // 2), slice(None))
    return (slice(None), pl.ds(pl.multiple_of(hc * (c // 2), LANES), c // 2))


def _half(ref, hc, lead=None):
    idx = _half_index(ref.shape, hc)
    return ref.at[idx] if lead is None else ref.at[(lead,) + idx]


def _gather_weights(shards):
    n = len(shards)

    def body(*refs):
        w_refs, o_refs, token = refs[:n], refs[n:2 * n], refs[2 * n]
        send_sems, recv_sems, stage_sems = refs[2 * n + 1:2 * n + 4]
        stages = refs[2 * n + 4:]
        mx, my, mc = _place()
        chip = 2 * mx + my
        others = _other_chips(mx, my)
        sibling = (mx, my, 1 - mc)
        token[...] = jnp.zeros_like(token)
        stage_in = [pltpu.make_async_copy(w, st, stage_sems.at[0, i]) for i, (w, st) in enumerate(zip(w_refs, stages))]
        for cp in stage_in:
            cp.start()
        first = []
        for i, (w, o) in enumerate(zip(w_refs, o_refs)):
            for k, (cx, cy) in enumerate(others):
                first.append(_remote(_half(w, mc), _half(o, mc, chip), send_sems.at[i, k],
                                     recv_sems.at[i, k], (cx, cy, mc)))
                first[-1].start()
        stage_out = []
        for i, (st, o) in enumerate(zip(stages, o_refs)):
            stage_in[i].wait()
            stage_out.append(pltpu.make_async_copy(st, o.at[chip], stage_sems.at[1, i]))
            stage_out[-1].start()
        passed = []
        for i, (w, o) in enumerate(zip(w_refs, o_refs)):
            for k, (cx, cy) in enumerate(others):
                landed = _half(o, mc, 2 * cx + cy)
                _remote(landed, landed, send_sems.at[i, k], recv_sems.at[i, k], (cx, cy, mc)).wait_recv()
                passed.append(_remote(landed, landed, send_sems.at[i, 3 + k], recv_sems.at[i, 3 + k], sibling))
                passed[-1].start()
        for i, (w, o) in enumerate(zip(w_refs, o_refs)):
            for k, (cx, cy) in enumerate(others):
                there = _half(o, 1 - mc, 2 * cx + cy)
                _remote(there, there, send_sems.at[i, 3 + k], recv_sems.at[i, 3 + k], sibling).wait_recv()
        for cp in first + passed:
            cp.wait_send()
        for cp in stage_out:
            cp.wait()

    out = pl.pallas_call(
        body, name="gather_weights", in_specs=[HBM_SPEC] * n, out_specs=[HBM_SPEC] * n + [VMEM_SPEC],
        out_shape=[jax.ShapeDtypeStruct((N_CHIPS,) + s.shape, s.dtype) for s in shards] + [jax.ShapeDtypeStruct((8, LANES), F32)],
        scratch_shapes=[pltpu.SemaphoreType.DMA((n, 6)), pltpu.SemaphoreType.DMA((n, 6)), pltpu.SemaphoreType.DMA((2, n))]
        + [pltpu.VMEM(s.shape, s.dtype) for s in shards],
        compiler_params=pltpu.CompilerParams(vmem_limit_bytes=VMEM_LIMIT),
    )(*shards)
    return out[:n], out[n]


SEM_SPEC = pl.BlockSpec(memory_space=pltpu.SEMAPHORE)
ANY_SPEC = pl.BlockSpec(memory_space=pl.ANY)
DATAFLOW = pltpu.SideEffectType.DATAFLOW_SIDE_EFFECTING


def _hbm(arr):
    return pltpu.with_memory_space_constraint(arr, pltpu.HBM)


def _gather_start(shards):
    n = len(shards)

    def body(*refs):
        w_refs, land_refs, send_sems, recv_sems, token = refs[:n], refs[n:2 * n], refs[2 * n], refs[2 * n + 1], refs[-1]
        mx, my, mc = _place()
        chip = 2 * mx + my
        for i, (w, land) in enumerate(zip(w_refs, land_refs)):
            for k, (cx, cy) in enumerate(_other_chips(mx, my)):
                _remote(_half(w, mc), _half(land, mc, chip), send_sems.at[3 * i + k],
                        recv_sems.at[3 * i + k], (cx, cy, mc)).start()
        token[...] = jnp.zeros_like(token)

    lands = [lax.empty((N_CHIPS,) + s.shape, s.dtype) for s in shards]
    out = pl.pallas_call(
        body, name="gather_start",
        out_shape=(pltpu.SemaphoreType.DMA((3 * n,)), pltpu.SemaphoreType.DMA((3 * n,)),
                   *[pltpu.HBM(s.shape, s.dtype) for s in shards], *[pltpu.HBM(l.shape, l.dtype) for l in lands],
                   jax.ShapeDtypeStruct((8, LANES), F32)),
        in_specs=[HBM_SPEC] * (2 * n), out_specs=(SEM_SPEC, SEM_SPEC, *[HBM_SPEC] * (2 * n), VMEM_SPEC),
        input_output_aliases={i: 2 + i for i in range(2 * n)},
        compiler_params=pltpu.CompilerParams(has_side_effects=DATAFLOW),
    )(*[_hbm(s) for s in shards], *[_hbm(l) for l in lands])
    return out[0], out[1], out[2:2 + n], out[2 + n:2 + 2 * n], out[-1]


def _gather_wait(send_sems, recv_sems, shards, lands, after):
    n = len(shards)

    def body(*refs):
        w_refs, land_refs, send_sems, recv_sems = refs[:n], refs[n:2 * n], refs[2 * n], refs[2 * n + 1]
        mx, my, mc = _place()
        for i, (w, land) in enumerate(zip(w_refs, land_refs)):
            for k, (cx, cy) in enumerate(_other_chips(mx, my)):
                cp = _remote(_half(w, mc), _half(land, mc, 2 * cx + cy), send_sems.at[3 * i + k],
                             recv_sems.at[3 * i + k], (cx, cy, mc))
                cp.wait_send()
                cp.wait_recv()

    out = pl.pallas_call(
        body, name="gather_wait",
        out_shape=(*[pltpu.HBM(s.shape, s.dtype) for s in shards], *[pltpu.HBM(l.shape, l.dtype) for l in lands]),
        in_specs=[HBM_SPEC] * (2 * n) + [SEM_SPEC, SEM_SPEC, ANY_SPEC], out_specs=tuple([HBM_SPEC] * (2 * n)),
        input_output_aliases={i: i for i in range(2 * n)},
        compiler_params=pltpu.CompilerParams(has_side_effects=DATAFLOW),
    )(*shards, *lands, send_sems, recv_sems, after)
    return out[n:]


def _gather_finish(shards, lands):
    n = len(shards)

    def body(*refs):
        w_refs, land_refs, o_refs = refs[:n], refs[n:2 * n], refs[2 * n:3 * n]
        send_sems, recv_sems, stage_sems = refs[3 * n:3 * n + 3]
        stages = refs[3 * n + 3:]
        mx, my, mc = _place()
        chip = 2 * mx + my
        others = _other_chips(mx, my)
        sibling = (mx, my, 1 - mc)
        stage_in = [pltpu.make_async_copy(w, st, stage_sems.at[0, i]) for i, (w, st) in enumerate(zip(w_refs, stages))]
        for cp in stage_in:
            cp.start()
        passed = []
        for i, (w, o) in enumerate(zip(w_refs, o_refs)):
            for k, (cx, cy) in enumerate(others):
                landed = _half(o, mc, 2 * cx + cy)
                passed.append(_remote(landed, landed, send_sems.at[i, k], recv_sems.at[i, k], sibling))
                passed[-1].start()
        stage_out = []
        for i, (st, o) in enumerate(zip(stages, o_refs)):
            stage_in[i].wait()
            stage_out.append(pltpu.make_async_copy(st, o.at[chip], stage_sems.at[1, i]))
            stage_out[-1].start()
        for i, (w, o) in enumerate(zip(w_refs, o_refs)):
            for k, (cx, cy) in enumerate(others):
                there = _half(o, 1 - mc, 2 * cx + cy)
                _remote(there, there, send_sems.at[i, k], recv_sems.at[i, k], sibling).wait_recv()
        for cp in passed:
            cp.wait_send()
        for cp in stage_out:
            cp.wait()

    return pl.pallas_call(
        body, name="gather_finish", in_specs=[HBM_SPEC] * (2 * n), out_specs=[HBM_SPEC] * n,
        out_shape=[jax.ShapeDtypeStruct(l.shape, l.dtype) for l in lands],
        input_output_aliases={n + i: i for i in range(n)},
        scratch_shapes=[pltpu.SemaphoreType.DMA((n, 3)), pltpu.SemaphoreType.DMA((n, 3)), pltpu.SemaphoreType.DMA((2, n))]
        + [pltpu.VMEM(s.shape, s.dtype) for s in shards],
        compiler_params=pltpu.CompilerParams(vmem_limit_bytes=VMEM_LIMIT),
    )(*shards, *lands)


def _scatter_start(ss, tag):
    n = len(ss)

    def body(*refs):
        s_refs, land_refs, send_sems, recv_sems, token = refs[:n], refs[n:2 * n], refs[2 * n], refs[2 * n + 1], refs[-1]
        mx, my, mc = _place()
        chip = 2 * mx + my
        for i, (s, land) in enumerate(zip(s_refs, land_refs)):
            for k, (cx, cy) in enumerate(_other_chips(mx, my)):
                _remote(s.at[2 * cx + cy], land.at[chip], send_sems.at[3 * i + k], recv_sems.at[3 * i + k],
                        (cx, cy, mc)).start()
        token[...] = jnp.zeros_like(token)

    lands = [lax.empty(s.shape, s.dtype) for s in ss]
    out = pl.pallas_call(
        body, name="scatter_start_" + tag,
        out_shape=(pltpu.SemaphoreType.DMA((3 * n,)), pltpu.SemaphoreType.DMA((3 * n,)),
                   *[pltpu.HBM(s.shape, s.dtype) for s in ss], *[pltpu.HBM(l.shape, l.dtype) for l in lands],
                   jax.ShapeDtypeStruct((8, LANES), F32)),
        in_specs=[HBM_SPEC] * (2 * n), out_specs=(SEM_SPEC, SEM_SPEC, *[HBM_SPEC] * (2 * n), VMEM_SPEC),
        input_output_aliases={i: 2 + i for i in range(2 * n)},
        compiler_params=pltpu.CompilerParams(has_side_effects=DATAFLOW),
    )(*[_hbm(s) for s in ss], *[_hbm(l) for l in lands])
    return out[0], out[1], out[2:2 + n], out[2 + n:2 + 2 * n], out[-1]


def _scatter_wait(send_sems, recv_sems, ss, lands, after, tag):
    n = len(ss)

    def body(*refs):
        s_refs, land_refs, send_sems, recv_sems = refs[:n], refs[n:2 * n], refs[2 * n], refs[2 * n + 1]
        mx, my, mc = _place()
        for i, (s, land) in enumerate(zip(s_refs, land_refs)):
            for k, (cx, cy) in enumerate(_other_chips(mx, my)):
                slot = land.at[2 * cx + cy]
                cp = _remote(s.at[2 * cx + cy], slot, send_sems.at[3 * i + k], recv_sems.at[3 * i + k], (cx, cy, mc))
                cp.wait_send()
                cp.wait_recv()

    out = pl.pallas_call(
        body, name="scatter_wait_" + tag,
        out_shape=(*[pltpu.HBM(s.shape, s.dtype) for s in ss], *[pltpu.HBM(l.shape, l.dtype) for l in lands]),
        in_specs=[HBM_SPEC] * (2 * n) + [SEM_SPEC, SEM_SPEC, ANY_SPEC], out_specs=tuple([HBM_SPEC] * (2 * n)),
        input_output_aliases={i: i for i in range(2 * n)},
        compiler_params=pltpu.CompilerParams(has_side_effects=DATAFLOW),
    )(*ss, *lands, send_sems, recv_sems, after)
    return out[:n], out[n:]


def _swap_halves(gs, after, name):
    n = len(gs)

    def body(*refs):
        g_refs, o_refs, send_sems, recv_sems = refs[:n], refs[n + 1:2 * n + 1], refs[2 * n + 1], refs[2 * n + 2]
        mx, my, mc = _place()
        copies = []
        for i, (g, o) in enumerate(zip(g_refs, o_refs)):
            src = g.at[(slice(None),) + _half_index(g.shape, 1 - mc)]
            copies.append(_remote(src, o, send_sems.at[i], recv_sems.at[i], (mx, my, 1 - mc)))
            copies[-1].start()
        for cp in copies:
            cp.wait()

    return pl.pallas_call(
        body, name=name, in_specs=[HBM_SPEC] * n + [ANY_SPEC], out_specs=[HBM_SPEC] * n,
        out_shape=[jax.ShapeDtypeStruct(_half_shape(g.shape), g.dtype) for g in gs],
        scratch_shapes=[pltpu.SemaphoreType.DMA((n,)), pltpu.SemaphoreType.DMA((n,))],
    )(*gs, after)


def _swap_start(gs, tag):
    n = len(gs)

    def body(*refs):
        g_refs, land_refs, send_sems, recv_sems, token = refs[:n], refs[n:2 * n], refs[2 * n], refs[2 * n + 1], refs[-1]
        mx, my, mc = _place()
        for i, (g, land) in enumerate(zip(g_refs, land_refs)):
            src = g.at[(slice(None),) + _half_index(g.shape, 1 - mc)]
            _remote(src, land, send_sems.at[i], recv_sems.at[i], (mx, my, 1 - mc)).start()
        token[...] = jnp.zeros_like(token)

    lands = [lax.empty(_half_shape(g.shape), g.dtype) for g in gs]
    out = pl.pallas_call(
        body, name="swap_start_" + tag,
        out_shape=(pltpu.SemaphoreType.DMA((n,)), pltpu.SemaphoreType.DMA((n,)), *[pltpu.HBM(g.shape, g.dtype) for g in gs],
                   *[pltpu.HBM(l.shape, l.dtype) for l in lands], jax.ShapeDtypeStruct((8, LANES), F32)),
        in_specs=[HBM_SPEC] * (2 * n), out_specs=(SEM_SPEC, SEM_SPEC, *[HBM_SPEC] * (2 * n), VMEM_SPEC),
        input_output_aliases={i: 2 + i for i in range(2 * n)},
        compiler_params=pltpu.CompilerParams(has_side_effects=DATAFLOW),
    )(*[_hbm(g) for g in gs], *[_hbm(l) for l in lands])
    return out[0], out[1], out[2:2 + n], out[2 + n:2 + 2 * n], out[-1]


def _swap_wait(send_sems, recv_sems, gs, lands, after, tag):
    n = len(gs)

    def body(*refs):
        g_refs, land_refs, send_sems, recv_sems = refs[:n], refs[n:2 * n], refs[2 * n], refs[2 * n + 1]
        mx, my, mc = _place()
        for i, (g, land) in enumerate(zip(g_refs, land_refs)):
            src = g.at[(slice(None),) + _half_index(g.shape, 1 - mc)]
            cp = _remote(src, land, send_sems.at[i], recv_sems.at[i], (mx, my, 1 - mc))
            cp.wait_send()
            cp.wait_recv()

    out = pl.pallas_call(
        body, name="swap_wait_" + tag,
        out_shape=(*[pltpu.HBM(g.shape, g.dtype) for g in gs], *[pltpu.HBM(l.shape, l.dtype) for l in lands]),
        in_specs=[HBM_SPEC] * (2 * n) + [SEM_SPEC, SEM_SPEC, ANY_SPEC], out_specs=tuple([HBM_SPEC] * (2 * n)),
        input_output_aliases={i: i for i in range(2 * n)},
        compiler_params=pltpu.CompilerParams(has_side_effects=DATAFLOW),
    )(*gs, *lands, send_sems, recv_sems, after)
    return out[:n], out[n:]


def _pair_sum(g, got, core, name):
    hr, hc = _half_shape(g.shape)[1:]
    by_rows = _halves_by_rows(g.shape)

    def body(core_ref, g_ref, got_ref, o_ref):
        o_ref[...] = (g_ref[...].astype(F32) + got_ref[...].astype(F32)).astype(BF16)

    return pl.pallas_call(
        body, name=name,
        grid_spec=pltpu.PrefetchScalarGridSpec(
            num_scalar_prefetch=1, grid=(N_CHIPS,),
            in_specs=[pl.BlockSpec((1, hr, hc), lambda j, core_ref: (j, core_ref[0], 0) if by_rows else (j, 0, core_ref[0])),
                      pl.BlockSpec((1, hr, hc), lambda j, core_ref: (j, 0, 0))],
            out_specs=pl.BlockSpec((1, hr, hc), lambda j, core_ref: (j, 0, 0))),
        out_shape=jax.ShapeDtypeStruct((N_CHIPS, hr, hc), BF16),
        compiler_params=_cparams(("arbitrary",)),
    )(core, g, got)


def _chip_sum(own, got, chip, name):
    _, h, c = own.shape

    def body(chip_ref, a_ref, b_ref, c_ref, d_ref, o_ref):
        o_ref[...] = ((a_ref[0].astype(F32) + b_ref[0].astype(F32)) + c_ref[0].astype(F32)) + d_ref[0].astype(F32)

    slot = lambda flip: pl.BlockSpec((1, h, c), lambda i, chip_ref: (chip_ref[0] ^ flip, 0, 0))
    return pl.pallas_call(
        body, name=name,
        grid_spec=pltpu.PrefetchScalarGridSpec(
            num_scalar_prefetch=1, grid=(1,), in_specs=[slot(0), slot(1), slot(2), slot(3)],
            out_specs=pl.BlockSpec((h, c), lambda i, chip_ref: (0, 0))),
        out_shape=jax.ShapeDtypeStruct((h, c), F32),
        compiler_params=_cparams(("arbitrary",)),
    )(chip, own, got, got, got)


def _join_halves(mine, name):
    n = len(mine)

    def body(*refs):
        m_refs, o_refs, send_sems, recv_sems = refs[:n], refs[n:2 * n], refs[2 * n], refs[2 * n + 1]
        mx, my, mc = _place()
        copies = []
        for i, (m, o) in enumerate(zip(m_refs, o_refs)):
            copies.append(_remote(m, o, send_sems.at[i], recv_sems.at[i], (mx, my, 1 - mc)))
            copies[-1].start()
        for cp in copies:
            cp.wait()

    return pl.pallas_call(
        body, name=name, in_specs=[HBM_SPEC] * n, out_specs=[HBM_SPEC] * n,
        out_shape=[jax.ShapeDtypeStruct(m.shape, m.dtype) for m in mine],
        scratch_shapes=[pltpu.SemaphoreType.DMA((n,)), pltpu.SemaphoreType.DMA((n,))],
    )(*mine)


def _adam_math(w, g, m, v):
    m2 = ADAM_B1 * m + (1.0 - ADAM_B1) * g
    v2 = ADAM_B2 * v + (1.0 - ADAM_B2) * (g * g)
    m_hat = m2 * (1.0 / (1.0 - ADAM_B1 ** ADAM_STEP))
    v_hat = v2 * (1.0 / (1.0 - ADAM_B2 ** ADAM_STEP))
    delta = -ADAM_LR * (m_hat / (jnp.sqrt(v_hat) + ADAM_EPS) + ADAM_WD * w)
    return delta, m2, v2


def _adam(w, g, m, v, name):
    def body(w_ref, g_ref, m_ref, v_ref, d_ref, m2_ref, v2_ref):
        d_ref[...], m2_ref[...], v2_ref[...] = _adam_math(w_ref[...], g_ref[...], m_ref[...], v_ref[...])

    return pl.pallas_call(body, name=name, out_shape=[jax.ShapeDtypeStruct(w.shape, F32)] * 3)(w, g, m, v)


def _adam_halves(w, m, v, mine, theirs, core, name):
    hr, hcols = _half_shape(w.shape)[1:]
    by_rows = _halves_by_rows(w.shape)

    def body(core_ref, w_ref, m_ref, v_ref, mine_ref, theirs_ref, g_ref, d_ref, m2_ref, v2_ref):
        g = jnp.where(pl.program_id(0) == core_ref[0], mine_ref[...], theirs_ref[...])
        g_ref[0] = g
        d_ref[0], m2_ref[0], v2_ref[0] = _adam_math(w_ref[0], g, m_ref[0], v_ref[0])

    half = pl.BlockSpec((1, hr, hcols), lambda hc, core_ref: (0, hc, 0) if by_rows else (0, 0, hc))
    whole = pl.BlockSpec((hr, hcols), lambda hc, core_ref: (0, 0))
    return pl.pallas_call(
        body, name=name,
        grid_spec=pltpu.PrefetchScalarGridSpec(
            num_scalar_prefetch=1, grid=(2,), in_specs=[half, half, half, whole, whole], out_specs=[half] * 4),
        out_shape=[jax.ShapeDtypeStruct(w.shape, F32)] * 4,
        compiler_params=_cparams(("arbitrary",)),
    )(core, w, m, v, mine, theirs)


ADA_COLS = N_MOD * D_MODEL // N_CHIPS


def _ada_fwd(c_all, w_ada, b_cols):
    def body(c_ref, w_ref, b_ref, o_ref):
        cv = c_ref[...]
        act = (cv * _sigmoid(cv)).astype(BF16)
        o_ref[...] = _dot(act, w_ref[...].astype(BF16)) + b_ref[...]

    return pl.pallas_call(
        body, name="ada_fwd", out_shape=jax.ShapeDtypeStruct((c_all.shape[0], ADA_COLS), F32),
        compiler_params=pltpu.CompilerParams(vmem_limit_bytes=VMEM_LIMIT),
    )(c_all, w_ada, b_cols)


def _ada_bwd(c_all, dmod_cols, w, m, v):
    nb = c_all.shape[0]
    tn = 384

    def body(c_ref, d_ref, w_ref, m_ref, v_ref, g_ref, dl_ref, m2_ref, v2_ref):
        cv = c_ref[...]
        act = (cv * _sigmoid(cv)).astype(BF16)
        g = _dot_tn(act, d_ref[...].astype(BF16))
        g_ref[...] = g
        dl_ref[...], m2_ref[...], v2_ref[...] = _adam_math(w_ref[...], g, m_ref[...], v_ref[...])

    blk = pl.BlockSpec((D_MODEL, tn), lambda j: (0, j))
    return pl.pallas_call(
        body, name="ada_bwd", grid=(ADA_COLS // tn,),
        in_specs=[pl.BlockSpec((nb, D_MODEL), lambda j: (0, 0)), pl.BlockSpec((nb, tn), lambda j: (0, j)), blk, blk, blk],
        out_specs=[blk] * 4, out_shape=[jax.ShapeDtypeStruct((D_MODEL, ADA_COLS), F32)] * 4,
        compiler_params=_cparams(("arbitrary",)),
    )(c_all, dmod_cols, w, m, v)


SMALL_NAMES = ("norm_ffn1", "norm_mix", "conv_w", "conv_b", "ssd_norm_w", "q_norm_w", "kv_norm_w", "mla_norm_w",
               "norm_ffn2", "norm_final", "dt_bias", "a_log", "d_skip")
SMALL_SIZES = (1024, 1024, CONV_WIDTH * D_CONV, D_CONV, 1024, Q_LORA, KV_LORA, 1024, 1024, 1024, 16, 16, 16)
SMALL_ROWS = 16
MOD_ROWS = 2 * N_MOD
SEND_ROWS = 40


def _pack_small(parts):
    flat = jnp.concatenate([parts[n].reshape(-1) for n in SMALL_NAMES])
    return jnp.pad(flat, (0, SMALL_ROWS * D_MODEL - flat.shape[0]))


def _unpack_small(flat):
    out, off = {}, 0
    for n, size in zip(SMALL_NAMES, SMALL_SIZES):
        out[n] = flat[off:off + size]
        off += size
    return out


def _small_sum(got):
    def body(g_ref, o_ref):
        bsum = jnp.zeros((N_MOD, D_MODEL), F32)
        ssum = jnp.zeros((SMALL_ROWS, D_MODEL), F32)
        for d in range(N_DEV):
            bsum = bsum + g_ref[d, 0:N_MOD, :] + g_ref[d, N_MOD:MOD_ROWS, :]
            ssum = ssum + g_ref[d, MOD_ROWS:MOD_ROWS + SMALL_ROWS, :]
        o_ref[...] = jnp.concatenate([bsum, ssum, jnp.zeros((32 - N_MOD - SMALL_ROWS, D_MODEL), F32)], axis=0)

    return pl.pallas_call(body, name="small_sum", out_shape=jax.ShapeDtypeStruct((32, D_MODEL), F32))(got)


BIG_NAMES = ("ffn1_w_gate", "ffn1_w_up", "ffn1_w_down", "w_in", "w_uq", "w_ukv", "w_out", "ffn2_w_gate", "ffn2_w_up",
             "ffn2_w_down")
_TO_KERNEL = {"w_in": _win_to_kernel, "w_uq": _wuq_to_kernel, "w_ukv": _wukv_to_kernel}
_FROM_KERNEL = {"w_in": _win_from_kernel, "w_uq": _wuq_from_kernel, "w_ukv": _wukv_from_kernel}


def _columns_joined(w4):
    n, r, c = w4.shape
    return w4.transpose(1, 0, 2).reshape(r, n * c)


def _columns_split(g):
    r, cols = g.shape
    return g.reshape(r, N_CHIPS, cols // N_CHIPS).transpose(1, 0, 2)


def kernel(x, c, positions, w_ada, b_ada, norm_ffn1, ffn1_w_gate, ffn1_w_up, ffn1_w_down, norm_mix, w_in, conv_w, conv_b, dt_bias, a_log, d_skip, ssd_norm_w, q_norm_w, w_uq, kv_norm_w, w_ukv, mla_norm_w, w_out, norm_ffn2, ffn2_w_gate, ffn2_w_up, ffn2_w_down, norm_final, loss_target, m_w_ada, m_b_ada, m_norm_ffn1, m_ffn1_w_gate, m_ffn1_w_up, m_ffn1_w_down, m_norm_mix, m_w_in, m_conv_w, m_conv_b, m_dt_bias, m_a_log, m_d_skip, m_ssd_norm_w, m_q_norm_w, m_w_uq, m_kv_norm_w, m_w_ukv, m_mla_norm_w, m_w_out, m_norm_ffn2, m_ffn2_w_gate, m_ffn2_w_up, m_ffn2_w_down, m_norm_final, v_w_ada, v_b_ada, v_norm_ffn1, v_ffn1_w_gate, v_ffn1_w_up, v_ffn1_w_down, v_norm_mix, v_w_in, v_conv_w, v_conv_b, v_dt_bias, v_a_log, v_d_skip, v_ssd_norm_w, v_q_norm_w, v_w_uq, v_kv_norm_w, v_w_ukv, v_mla_norm_w, v_w_out, v_norm_ffn2, v_ffn2_w_gate, v_ffn2_w_up, v_ffn2_w_down, v_norm_final):
    a = dict(locals())
    held_transposed = ("ffn1_w_gate", "ffn1_w_up", "ffn2_w_gate", "ffn2_w_up", "w_in")
    for n in held_transposed:
        for p in ("", "m_", "v_"):
            a[p + n] = a[p + n].transpose(0, 2, 1)
    B, S, D = x.shape
    mx, my, mc = _place()
    chip = 2 * mx + my
    dev = 2 * chip + mc
    core = mc.astype(jnp.int32).reshape(1)
    chip_id = chip.astype(jnp.int32).reshape(1)

    cw_rows = jnp.pad(conv_w[0], ((0, 0), (0, D - conv_w.shape[2])))
    got, _ = _all_gather_small(jnp.concatenate([c, cw_rows, jnp.zeros((8 - B - CONV_WIDTH, D), F32)], axis=0), "gather_c")
    c_all = got[:, :B, :].reshape(N_DEV * B, D)
    conv_full = got[::2, B:B + CONV_WIDTH, :conv_w.shape[2]].transpose(1, 0, 2).reshape(CONV_WIDTH, D_CONV)

    b_cols = lax.dynamic_slice(b_ada, (0, chip * ADA_COLS), (1, ADA_COLS))
    mod_all, mod_done = _all_gather_small(_ada_fwd(c_all, w_ada[0], b_cols), "gather_mod")
    mod = lax.dynamic_slice(mod_all, (0, B * dev, 0), (N_DEV, B, ADA_COLS))[::2].transpose(1, 0, 2).reshape(B, N_MOD * D)

    first = ("ffn1_w_gate", "ffn1_w_up", "ffn1_w_down")
    later = tuple(n for n in BIG_NAMES if n not in first)
    got_first, gathered = _gather_weights([(a[n][0] + mod_done[0, 0]).astype(BF16) for n in first])
    w = dict(zip(first, got_first))
    in_flight = _gather_start([(a[n][0] + gathered[0, 0]).astype(BF16) for n in later])

    def later_weights(after):
        send_sems, recv_sems, shards, lands, _ = in_flight
        lands = _gather_wait(send_sems, recv_sems, shards, lands, after)
        wl = dict(zip(later, _gather_finish([a[n][0].astype(BF16) for n in later], lands)))
        for n, to_kernel in _TO_KERNEL.items():
            wl[n] = to_kernel(wl[n].reshape(-1, D) if n in held_transposed else _columns_joined(wl[n]))
        wl["w_out"] = wl["w_out"].reshape(D_SSD + D_MLA, D)
        return wl

    small = {n: a[n].reshape(1, -1) for n in SMALL_NAMES if n not in ("conv_w", "norm_final")}
    small["conv_w"], small["norm_final"] = conv_full, norm_final

    def shards_of(names, gw):
        g4 = []
        for n in names:
            g = gw[n]
            if n in _FROM_KERNEL:
                g = _FROM_KERNEL[n](g) if n in held_transposed else _columns_split(_FROM_KERNEL[n](g))
            g4.append(g.reshape(N_CHIPS, a[n].shape[1], a[n].shape[2]))
        return g4

    def scatter_group(names, g4, swapped):
        pair = [_pair_sum(g, got, core, "pair_sum_" + n) for n, g, got in zip(names, g4, swapped)]
        return (names,) + tuple(_scatter_start(pair, names[0]))

    grads, deltas, new_m, new_v = {}, {}, {}, {}

    def finish_groups(some, after):
        names, mine = [], []
        for group_names, send_sems, recv_sems, pair, lands, _ in some:
            pair, lands = _scatter_wait(send_sems, recv_sems, pair, lands, after, group_names[0])
            names += group_names
            mine += [_chip_sum(own, got, chip_id, "chip_sum_" + n) for n, own, got in zip(group_names, pair, lands)]
        for n, own, other in zip(names, mine, _join_halves(mine, "join_" + names[0])):
            grads[n], deltas[n], new_m[n], new_v[n] = _adam_halves(a[n], a["m_" + n], a["v_" + n], own, other, core, "adam_" + n)
        return deltas[names[-1]]

    swapping, groups = [], []

    def on_grads(names, gw):
        send_sems, recv_sems, g4, lands, token = _swap_start(shards_of(names, gw), names[0])
        swapping.append((names, send_sems, recv_sems, g4, lands))
        return token[0, 0]

    def sync(after):
        token = 0.0
        while swapping:
            names, send_sems, recv_sems, g4, lands = swapping.pop(0)
            g4, swapped = _swap_wait(send_sems, recv_sems, g4, lands, after, names[0])
            groups.append(scatter_group(names, g4, swapped))
            token = groups[-1][5][0, 0]
        return token

    loss_blk, grad_x, gw, dmod, gs = _local_step(x, positions, mod + in_flight[4][0, 0], w, later_weights, small, loss_target,
                                                 on_grads, sync)

    small_flat = _pack_small(gs).at[-1].set(loss_blk[0, 0])
    send = jnp.concatenate([dmod.reshape(MOD_ROWS, D), small_flat.reshape(SMALL_ROWS, D),
                            jnp.zeros((SEND_ROWS - MOD_ROWS - SMALL_ROWS, D), F32)], axis=0)
    got, _ = _all_gather_small(send, "gather_small")
    summed = _small_sum(got)
    sums = summed[N_MOD:N_MOD + SMALL_ROWS].reshape(-1)
    loss = sums[-1]
    gsmall = _unpack_small(sums)
    gsmall["conv_w"] = lax.dynamic_slice(gsmall["conv_w"].reshape(CONV_WIDTH, D_CONV), (0, chip * conv_w.shape[2]),
                                         (CONV_WIDTH, conv_w.shape[2]))
    gsmall["b_ada"] = summed[:N_MOD]
    names = ("b_ada",) + SMALL_NAMES
    rows = 208

    def pack(parts):
        flat = jnp.concatenate([parts[n].reshape(-1) for n in names])
        return jnp.pad(flat, (0, rows * LANES - flat.shape[0])).reshape(rows, LANES)

    packed = [pack({n: a[p + n] for n in names}) for p in ("", "m_", "v_")]
    g_p = pack(gsmall)
    outs = (g_p,) + tuple(_adam(packed[0], g_p, packed[1], packed[2], "adam_small"))
    for dst, flat in zip((grads, deltas, new_m, new_v), outs):
        flat, off = flat.reshape(-1), 0
        for n in names:
            dst[n] = flat[off:off + a[n].size].reshape(a[n].shape)
            off += a[n].size

    dmod_all = got[:, :MOD_ROWS, :].reshape(N_DEV * B, N_MOD * D)
    dmod_cols = lax.dynamic_slice(dmod_all, (0, chip * ADA_COLS), (N_DEV * B, ADA_COLS))
    ada = _ada_bwd(c_all, dmod_cols, w_ada[0], m_w_ada[0], v_w_ada[0])
    for dst, t in zip((grads, deltas, new_m, new_v), ada):
        dst["w_ada"] = t[None]

    g4 = shards_of(first, gw)
    last = scatter_group(first, g4, _swap_halves(g4, summed, "swap_" + first[0]))
    finish_groups([last], finish_groups(groups, last[5]))
    for dst in (grads, deltas, new_m, new_v):
        for n in held_transposed:
            dst[n] = dst[n].transpose(0, 2, 1)

    order = ("w_ada", "b_ada", "norm_ffn1", "ffn1_w_gate", "ffn1_w_up", "ffn1_w_down", "norm_mix", "w_in", "conv_w", "conv_b",
             "dt_bias", "a_log", "d_skip", "ssd_norm_w", "q_norm_w", "w_uq", "kv_norm_w", "w_ukv", "mla_norm_w", "w_out",
             "norm_ffn2", "ffn2_w_gate", "ffn2_w_up", "ffn2_w_down", "norm_final")
    return (loss, grad_x, *[grads[n] for n in order], *[deltas[n] for n in order], *[new_m[n] for n in order],
            *[new_v[n] for n in order])
```

```python
import functools
import math

import jax
import jax.numpy as jnp
import numpy as np
from jax import lax
from jax.experimental import pallas as pl
from jax.experimental.pallas import tpu as pltpu

F32 = jnp.float32
BF16 = jnp.bfloat16
HIGHEST = lax.Precision.HIGHEST

D_MODEL = 1024
D_FF = 2816
D_SSD = 1024
D_MLA = 1024
SSD_HEADS = 16
SSD_HEAD_DIM = 64
SSD_GROUPS = 2
SSD_STATE = 128
CONV_WIDTH = 4
CHUNK = 128
MLA_HEADS = 8
QK_NOPE = 64
QK_ROPE = 32
QK_DIM = QK_NOPE + QK_ROPE
V_HEAD = 128
Q_LORA = 384
KV_LORA = 256
ROPE_THETA = 10000.0
N_MOD = 9
EPS = 1e-6
D_CONV = D_SSD + 2 * SSD_GROUPS * SSD_STATE
D_PROJ = 3328
HEAD_LANES = 128
ADAM_LR = 0.001
ADAM_B1 = 0.9
ADAM_B2 = 0.999
ADAM_EPS = 1e-08
ADAM_WD = 0.01
ADAM_STEP = 10

LANES = 128
VMEM_LIMIT = 56 * 1024 * 1024
TOKEN_TILE = 512
WIDE_TOKEN_TILE = 1024
ATTN_FWD_Q_TILE = 1024
ATTN_FWD_KV_TILE = 1024
ATTN_BWD_TILE = 1024
N_CHIPS = 4
N_DEV = 8

MESH = pl.DeviceIdType.MESH


def _dot(a, b, precision=None):
    return jnp.dot(a, b, preferred_element_type=F32, precision=precision)


def _dot_nt(a, b, precision=None):
    return lax.dot_general(a, b, (((1,), (1,)), ((), ())), preferred_element_type=F32, precision=precision)


def _dot_tn(a, b, precision=None):
    return lax.dot_general(a, b, (((0,), (0,)), ((), ())), preferred_element_type=F32, precision=precision)


def _cparams(semantics):
    return pltpu.CompilerParams(dimension_semantics=semantics, vmem_limit_bytes=VMEM_LIMIT)


def _resident(shape):
    zeros = (0,) * len(shape)
    return pl.BlockSpec(shape, lambda *_: zeros, pipeline_mode=pl.Buffered(1))


def _sigmoid(x):
    return jax.nn.sigmoid(x)


def _rms_stats(x):
    r = lax.rsqrt(jnp.mean(x * x, axis=-1, keepdims=True) + EPS)
    return x * r, r


def _rms_bwd(dn, xh, r, w):
    dxh = dn * w
    dx = r * (dxh - xh * jnp.mean(dxh * xh, axis=-1, keepdims=True))
    return dx, dn * xh


def _colsum(v):
    return jnp.sum(v, axis=0, keepdims=True)


def _ffn_fwd(x, nw, sh, sc, g, wg, wu, wd, seq, name, head=None):
    T, D = x.shape
    fs = wg.shape[1]
    tm = min(TOKEN_TILE, seq)
    tps = seq // tm

    def body(x_ref, nw_ref, sh_ref, sc_ref, g_ref, wg_ref, wu_ref, wd_ref, *rest):
        if head is None:
            xo_ref, p_ref, q_ref, s_ref, f_ref = rest
        else:
            nf_ref, t_ref, xo_ref, p_ref, q_ref, s_ref, f_ref, loss_ref, dnf_ref = rest

            @pl.when(pl.program_id(0) == 0)
            def _():
                loss_ref[...] = jnp.zeros_like(loss_ref)
                dnf_ref[...] = jnp.zeros_like(dnf_ref)

        xv = x_ref[...]
        xh, _ = _rms_stats(xv)
        h = (xh * nw_ref[...]) * (1.0 + sc_ref[0]) + sh_ref[0]
        hb = h.astype(BF16)
        f = jnp.zeros((tm, D), F32)
        for j in range(N_CHIPS):
            a = _dot_nt(hb, wg_ref[j])
            u = _dot_nt(hb, wu_ref[j])
            sig = _sigmoid(a)
            silu = a * sig
            sb = (silu * u).astype(BF16)
            p_ref[j] = (u * (sig * (1.0 + a * (1.0 - sig)))).astype(BF16)
            q_ref[j] = silu.astype(BF16)
            s_ref[j] = sb
            f = f + _dot(sb, wd_ref[j])
        f_ref[...] = f.astype(BF16)
        xo = xv + 0.5 * g_ref[0] * f
        if head is None:
            xo_ref[...] = xo
        else:
            xh, r = _rms_stats(xo)
            nfv = nf_ref[...]
            err = xh * nfv - t_ref[...]
            loss_ref[...] += (0.5 / D) * jnp.sum(err * err)
            dxo, dw_rows = _rms_bwd(err * (1.0 / D), xh, r, nfv)
            xo_ref[...] = dxo
            dnf_ref[...] += _colsum(dw_rows)

    rows = lambda n: pl.BlockSpec((tm, n), lambda i: (i, 0))
    act = pl.BlockSpec((N_CHIPS, tm, fs), lambda i: (0, i, 0))
    perb = pl.BlockSpec((1, 1, D), lambda i: (i // tps, 0, 0))
    sd = jax.ShapeDtypeStruct
    in_specs = [rows(D), _resident((1, D)), perb, perb, perb, _resident((N_CHIPS, fs, D)), _resident((N_CHIPS, fs, D)),
                _resident((N_CHIPS, fs, D))]
    out_specs = [rows(D), act, act, act, rows(D)]
    out_shape = [sd((T, D), F32)] + [sd((N_CHIPS, T, fs), BF16)] * 3 + [sd((T, D), BF16)]
    if head is not None:
        in_specs += [_resident((1, D)), rows(D)]
        out_specs += [pl.BlockSpec((8, LANES), lambda i: (0, 0)), pl.BlockSpec((1, D), lambda i: (0, 0))]
        out_shape += [sd((8, LANES), F32), sd((1, D), F32)]
    return pl.pallas_call(
        body, grid=(T // tm,), name=name, in_specs=in_specs, out_specs=out_specs, out_shape=out_shape,
        compiler_params=_cparams(("arbitrary",)),
    )(x, nw, sh, sc, g, wg, wu, wd, *(head or ()))


def _ffn_bwd(dxo, x, nw, sh, sc, g, p, q, f, wg, wu, wd, seq, name):
    T, D = x.shape
    fs = wg.shape[1]
    B = T // seq
    tm = min(TOKEN_TILE // 2, seq)
    tps = seq // tm

    def body(dxo_ref, x_ref, nw_ref, sh_ref, sc_ref, g_ref, p_ref, q_ref, f_ref, wg_ref, wu_ref, wd_ref,
             dx_ref, h_ref, df_ref, da_ref, du_ref, dsh_ref, dsc_ref, dg_ref, dnw_ref):
        i = pl.program_id(0)

        @pl.when(i % tps == 0)
        def _():
            dsh_ref[...] = jnp.zeros_like(dsh_ref)
            dsc_ref[...] = jnp.zeros_like(dsc_ref)
            dg_ref[...] = jnp.zeros_like(dg_ref)

        @pl.when(i == 0)
        def _():
            dnw_ref[...] = jnp.zeros_like(dnw_ref)

        dxo_v = dxo_ref[...]
        dfb = (0.5 * g_ref[0] * dxo_v).astype(BF16)
        dg_ref[0] += _colsum(0.5 * dxo_v * f_ref[...].astype(F32))
        dh = jnp.zeros((tm, D), F32)
        for j in range(N_CHIPS):
            ds = _dot_nt(dfb, wd_ref[j])
            dab = (ds * p_ref[j].astype(F32)).astype(BF16)
            dub = (ds * q_ref[j].astype(F32)).astype(BF16)
            dh = dh + _dot(dab, wg_ref[j]) + _dot(dub, wu_ref[j])
            da_ref[j] = dab
            du_ref[j] = dub
        xv = x_ref[...]
        xh, r = _rms_stats(xv)
        nwv = nw_ref[...]
        n = xh * nwv
        scale1 = 1.0 + sc_ref[0]
        dsc_ref[0] += _colsum(dh * n)
        dsh_ref[0] += _colsum(dh)
        dx, dw_rows = _rms_bwd(dh * scale1, xh, r, nwv)
        dnw_ref[...] += _colsum(dw_rows)
        dx_ref[...] = dxo_v + dx
        h_ref[...] = (n * scale1 + sh_ref[0]).astype(BF16)
        df_ref[...] = dfb

    rows = lambda n: pl.BlockSpec((tm, n), lambda i: (i, 0))
    act = pl.BlockSpec((N_CHIPS, tm, fs), lambda i: (0, i, 0))
    perb = pl.BlockSpec((1, 1, D), lambda i: (i // tps, 0, 0))
    sd = jax.ShapeDtypeStruct
    return pl.pallas_call(
        body, grid=(T // tm,), name=name,
        in_specs=[rows(D), rows(D), _resident((1, D)), perb, perb, perb, act, act, rows(D),
                  _resident((N_CHIPS, fs, D)), _resident((N_CHIPS, fs, D)), _resident((N_CHIPS, fs, D))],
        out_specs=[rows(D), rows(D), rows(D), act, act, perb, perb, perb, pl.BlockSpec((1, D), lambda i: (0, 0))],
        out_shape=[sd((T, D), F32), sd((T, D), BF16), sd((T, D), BF16),
                   sd((N_CHIPS, T, fs), BF16), sd((N_CHIPS, T, fs), BF16), sd((B, 1, D), F32), sd((B, 1, D), F32),
                   sd((B, 1, D), F32), sd((1, D), F32)],
        compiler_params=_cparams(("arbitrary",)),
    )(dxo, x, nw, sh, sc, g, p, q, f, wg, wu, wd)


def _ffn_wgrad(h, s, df, da, du, after, name):
    T, D = h.shape
    fs = s.shape[2]
    tt = min(TOKEN_TILE, T)
    nt = T // tt

    def body(h_ref, s_ref, df_ref, da_ref, du_ref, after_ref, dgate_ref, dup_ref, ddown_ref, gate_acc, up_acc, down_acc):
        @pl.when(pl.program_id(1) == 0)
        def _():
            gate_acc[...] = jnp.zeros_like(gate_acc)
            up_acc[...] = jnp.zeros_like(up_acc)
            down_acc[...] = jnp.zeros_like(down_acc)

        hv = h_ref[...]
        gate_acc[...] += _dot_tn(da_ref[0], hv)
        up_acc[...] += _dot_tn(du_ref[0], hv)
        down_acc[...] += _dot_tn(s_ref[0], df_ref[...])

        @pl.when(pl.program_id(1) == nt - 1)
        def _():
            dgate_ref[0] = gate_acc[...].astype(BF16)
            dup_ref[0] = up_acc[...].astype(BF16)
            ddown_ref[0] = down_acc[...].astype(BF16)

    rows = pl.BlockSpec((tt, D), lambda j, t: (t, 0))
    act = pl.BlockSpec((1, tt, fs), lambda j, t: (j, t, 0))
    shard = pl.BlockSpec((1, fs, D), lambda j, t: (j, 0, 0))
    return pl.pallas_call(
        body, grid=(N_CHIPS, nt), name=name,
        in_specs=[rows, act, rows, act, act, pl.BlockSpec(memory_space=pl.ANY)],
        out_specs=[shard] * 3, out_shape=[jax.ShapeDtypeStruct((N_CHIPS, fs, D), BF16)] * 3,
        scratch_shapes=[pltpu.VMEM((fs, D), F32)] * 3,
        compiler_params=_cparams(("arbitrary", "arbitrary")),
    )(h, s, df, da, du, after)


def _mm_tn(xa, ya, tn, name):
    T, K = xa.shape
    N = ya.shape[1]
    tt = min(WIDE_TOKEN_TILE, T)
    nt = T // tt

    def body(x_ref, y_ref, o_ref, acc_ref):
        @pl.when(pl.program_id(1) == 0)
        def _():
            acc_ref[...] = jnp.zeros_like(acc_ref)

        acc_ref[...] += _dot_tn(x_ref[...], y_ref[...])

        @pl.when(pl.program_id(1) == nt - 1)
        def _():
            o_ref[...] = acc_ref[...].astype(BF16)

    return pl.pallas_call(
        body, grid=(N // tn, nt), name=name,
        in_specs=[pl.BlockSpec((tt, K), lambda j, t: (t, 0)), pl.BlockSpec((tt, tn), lambda j, t: (t, j))],
        out_specs=pl.BlockSpec((K, tn), lambda j, t: (0, j)),
        out_shape=jax.ShapeDtypeStruct((K, N), BF16),
        scratch_shapes=[pltpu.VMEM((K, tn), F32)],
        compiler_params=_cparams(("arbitrary", "arbitrary")),
    )(xa, ya)


_PROJ_SPLITS = (0, 1024, 2560, 2944, 3200, 3328)


def _inproj_fwd(x, nw, sh, sc, win, seq):
    T, D = x.shape
    tm = min(WIDE_TOKEN_TILE, seq)
    tps = seq // tm
    widths = [b - a for a, b in zip(_PROJ_SPLITS[:-1], _PROJ_SPLITS[1:])]
    dtypes = [BF16, BF16, F32, F32, F32]

    def body(x_ref, nw_ref, sh_ref, sc_ref, w_ref, *outs):
        xh, _ = _rms_stats(x_ref[...])
        h = (xh * nw_ref[...]) * (1.0 + sc_ref[0]) + sh_ref[0]
        proj = _dot_nt(h.astype(BF16), w_ref[...])
        for o, lo, hi in zip(outs, _PROJ_SPLITS[:-1], _PROJ_SPLITS[1:]):
            o[...] = proj[:, lo:hi].astype(o.dtype)

    rows = lambda n: pl.BlockSpec((tm, n), lambda i: (i, 0))
    perb = pl.BlockSpec((1, 1, D), lambda i: (i // tps, 0, 0))
    return pl.pallas_call(
        body, grid=(T // tm,), name="inproj_fwd",
        in_specs=[rows(D), _resident((1, D)), perb, perb, _resident((D_PROJ, D))],
        out_specs=[rows(w) for w in widths],
        out_shape=[jax.ShapeDtypeStruct((T, w), dt) for w, dt in zip(widths, dtypes)],
        compiler_params=_cparams(("arbitrary",)),
    )(x, nw, sh, sc, win)


def _inproj_bwd(dx2, x, nw, sh, sc, win, dz, dxbc, dcq, dckv, ddtk_a, ddtk_b, seq):
    T, D = x.shape
    B = T // seq
    tm = min(TOKEN_TILE, seq)
    tps = seq // tm

    def body(dx2_ref, x_ref, nw_ref, sh_ref, sc_ref, w_ref, dz_ref, dxbc_ref, dcq_ref, dckv_ref, da_ref, db_ref,
             dx_ref, h_ref, dp_ref, dsh_ref, dsc_ref, dnw_ref):
        i = pl.program_id(0)

        @pl.when(i % tps == 0)
        def _():
            dsh_ref[...] = jnp.zeros_like(dsh_ref)
            dsc_ref[...] = jnp.zeros_like(dsc_ref)

        @pl.when(i == 0)
        def _():
            dnw_ref[...] = jnp.zeros_like(dnw_ref)

        dproj = jnp.concatenate(
            [dz_ref[...], dxbc_ref[...], dcq_ref[...].astype(BF16), dckv_ref[...].astype(BF16),
             (da_ref[...] + db_ref[...]).astype(BF16)], axis=1)
        dp_ref[...] = dproj
        dh = _dot(dproj, w_ref[...])
        xh, r = _rms_stats(x_ref[...])
        nwv = nw_ref[...]
        n = xh * nwv
        scale1 = 1.0 + sc_ref[0]
        dsc_ref[0] += _colsum(dh * n)
        dsh_ref[0] += _colsum(dh)
        dx, dw_rows = _rms_bwd(dh * scale1, xh, r, nwv)
        dnw_ref[...] += _colsum(dw_rows)
        dx_ref[...] = dx2_ref[...] + dx
        h_ref[...] = (n * scale1 + sh_ref[0]).astype(BF16)

    rows = lambda n: pl.BlockSpec((tm, n), lambda i: (i, 0))
    perb = pl.BlockSpec((1, 1, D), lambda i: (i // tps, 0, 0))
    sd = jax.ShapeDtypeStruct
    return pl.pallas_call(
        body, grid=(T // tm,), name="inproj_bwd",
        in_specs=[rows(D), rows(D), _resident((1, D)), perb, perb, _resident((D_PROJ, D)),
                  rows(1024), rows(D_CONV), rows(Q_LORA), rows(KV_LORA), rows(LANES), rows(LANES)],
        out_specs=[rows(D), rows(D), rows(D_PROJ), perb, perb, pl.BlockSpec((1, D), lambda i: (0, 0))],
        out_shape=[sd((T, D), F32), sd((T, D), BF16), sd((T, D_PROJ), BF16), sd((B, 1, D), F32), sd((B, 1, D), F32),
                   sd((1, D), F32)],
        compiler_params=_cparams(("arbitrary",)),
    )(dx2, x, nw, sh, sc, win, dz, dxbc, dcq, dckv, ddtk_a, ddtk_b)


SUBLANES = 8


def _shift_down(v, k):
    r = pltpu.roll(v, k, 0)
    row = lax.broadcasted_iota(jnp.int32, (SUBLANES, v.shape[1]), 0)
    return jnp.concatenate([jnp.where(row < k, 0.0, r[:SUBLANES]), r[SUBLANES:]], axis=0)


def _shift_up(v, k):
    n = v.shape[0]
    r = pltpu.roll(v, n - k, 0)
    row = lax.broadcasted_iota(jnp.int32, (SUBLANES, v.shape[1]), 0)
    return jnp.concatenate([r[:n - SUBLANES], jnp.where(row >= SUBLANES - k, 0.0, r[n - SUBLANES:])], axis=0)


def _conv_pre(xv, w_ref, b_ref):
    pre = b_ref[...] + w_ref[CONV_WIDTH - 1:CONV_WIDTH, :] * xv
    for k in range(1, CONV_WIDTH):
        pre = pre + w_ref[CONV_WIDTH - 1 - k:CONV_WIDTH - k, :] * _shift_down(xv, k)
    return pre


def _conv_fwd(xraw, cw, cb):
    B, S, C = xraw.shape

    def body(x_ref, w_ref, b_ref, o_ref):
        pre = _conv_pre(x_ref[0].astype(F32), w_ref, b_ref)
        o_ref[0] = (pre * _sigmoid(pre)).astype(BF16)

    blk = pl.BlockSpec((1, S, LANES), lambda b, j: (b, 0, j))
    return pl.pallas_call(
        body, grid=(B, C // LANES), name="conv_fwd",
        in_specs=[blk, pl.BlockSpec((CONV_WIDTH, LANES), lambda b, j: (0, j)), pl.BlockSpec((1, LANES), lambda b, j: (0, j))],
        out_specs=blk, out_shape=jax.ShapeDtypeStruct((B, S, C), BF16),
        compiler_params=_cparams(("arbitrary", "arbitrary")),
    )(xraw, cw, cb)


def _conv_bwd(dout, xraw, cw, cb):
    B, S, C = xraw.shape

    def body(d_ref, x_ref, w_ref, b_ref, dx_ref, dw_ref, db_ref):
        @pl.when(pl.program_id(1) == 0)
        def _():
            dw_ref[...] = jnp.zeros_like(dw_ref)
            db_ref[...] = jnp.zeros_like(db_ref)

        xv = x_ref[0].astype(F32)
        pre = _conv_pre(xv, w_ref, b_ref)
        sig = _sigmoid(pre)
        dpre = d_ref[0].astype(F32) * (sig * (1.0 + pre * (1.0 - sig)))
        dx = w_ref[CONV_WIDTH - 1:CONV_WIDTH, :] * dpre
        for k in range(1, CONV_WIDTH):
            dx = dx + w_ref[CONV_WIDTH - 1 - k:CONV_WIDTH - k, :] * _shift_up(dpre, k)
        dx_ref[0] = dx.astype(BF16)
        db_ref[...] += _colsum(dpre)
        dws = [_colsum(dpre * (xv if k == 0 else _shift_down(xv, k))) for k in range(CONV_WIDTH - 1, -1, -1)]
        dw_ref[...] += jnp.concatenate(dws, axis=0)

    blk = pl.BlockSpec((1, S, LANES), lambda j, b: (b, 0, j))
    wspec = pl.BlockSpec((CONV_WIDTH, LANES), lambda j, b: (0, j))
    bspec = pl.BlockSpec((1, LANES), lambda j, b: (0, j))
    return pl.pallas_call(
        body, grid=(C // LANES, B), name="conv_bwd",
        in_specs=[blk, blk, wspec, bspec], out_specs=[blk, wspec, bspec],
        out_shape=[jax.ShapeDtypeStruct((B, S, C), BF16), jax.ShapeDtypeStruct((CONV_WIDTH, C), F32),
                   jax.ShapeDtypeStruct((1, C), F32)],
        compiler_params=_cparams(("arbitrary", "arbitrary")),
    )(dout, xraw, cw, cb)


def _softplus(x):
    return jnp.maximum(x, 0.0) + jnp.log(1.0 + jnp.exp(-jnp.abs(x)))


def _ssd_common(xbc_ref, dtk_ref, dtb_ref, alog_ref, e_ref):
    L = CHUNK
    xbc = xbc_ref[0]
    xs = xbc[:, :D_SSD].astype(F32)
    bm = xbc[:, D_SSD:D_SSD + 256]
    cm = xbc[:, D_SSD + 256:D_SSD + 512]
    head = lax.broadcasted_iota(jnp.int32, (1, LANES), 1) < SSD_HEADS
    a128 = jnp.where(head, -jnp.exp(alog_ref[...]), 0.0)
    pre = dtk_ref[0] + dtb_ref[...]
    dt = _softplus(pre)
    dA = dt * a128
    row = lax.broadcasted_iota(jnp.int32, (L, L), 0)
    col = lax.broadcasted_iota(jnp.int32, (L, L), 1)
    causal = col <= row
    tri = causal.astype(F32)
    triT = (row <= col).astype(F32)
    tri = causal.astype(BF16)
    triT = (row <= col).astype(BF16)
    dA3 = _split3(dA)
    acum = _sum3(lambda part: _dot(tri, part), dA3)
    acumT = _sum3(lambda part: _dot_tn(part, triT), dA3)
    E = e_ref[...]
    acum_f = _spread(acum, E)
    dt_f = _spread(dt, E)
    e_f = jnp.exp(acum_f)
    w_f = jnp.exp(acum_f[L - 1:L, :] - acum_f)
    xt = xs * dt_f
    return dict(xs=xs, bm=bm, cm=cm, a128=a128, pre=pre, dt=dt, causal=causal, tri=tri, triT=triT, acum=acum,
                acumT=acumT, E=E, dt_f=dt_f, e_f=e_f, w_f=w_f, xt=xt, head=head)


def _split3(x):
    p1 = x.astype(BF16)
    r1 = x - p1.astype(F32)
    p2 = r1.astype(BF16)
    return p1, p2, (r1 - p2.astype(F32)).astype(BF16)


def _sum3(mm, parts):
    return (mm(parts[0]) + mm(parts[1])) + mm(parts[2])


def _spread(v, e):
    return _sum3(lambda part: _dot(part, e), _split3(v))


def _gather_heads(v, e):
    return _sum3(lambda part: _dot_nt(part, e), _split3(v))


def _head_mask(k):
    lane = lax.broadcasted_iota(jnp.int32, (CHUNK, LANES), 1)
    return (lane >= SSD_HEAD_DIM) if k == 1 else (lane < SSD_HEAD_DIM)


def _pair_decay(alast, h0):
    row = lax.broadcasted_iota(jnp.int32, (2 * SSD_HEAD_DIM, SSD_STATE), 0)
    return jnp.exp(jnp.where(row < SSD_HEAD_DIM, alast[:, h0:h0 + 1], alast[:, h0 + 1:h0 + 2]))


def _decay_matrix(q, h):
    seg = q["acum"][:, h:h + 1] - q["acumT"][h:h + 1, :]
    return jnp.exp(jnp.where(q["causal"], seg, -1e30))


def _gated_norm(y, zz, nw):
    sig = _sigmoid(zz)
    sil = zz * sig
    yg = y * sil
    half = D_SSD // SSD_GROUPS
    parts = []
    for g in range(SSD_GROUPS):
        xh, r = _rms_stats(yg[:, g * half:(g + 1) * half])
        parts.append((xh, r))
    return sig, sil, parts


def _ssd_fwd(xbc, dtk, z, dtb, alog, dsk, nw, expand):
    B, S, _ = xbc.shape
    L = CHUNK
    nc = S // L

    def body(xbc_ref, dtk_ref, z_ref, dtb_ref, alog_ref, dsk_ref, nw_ref, e_ref, y_ref, ys_ref, prev_ref, st_ref):
        @pl.when(pl.program_id(0) == 0)
        def _():
            st_ref[...] = jnp.zeros_like(st_ref)

        for b in range(B):
            one = lambda ref: ref.at[pl.ds(b, 1)]
            sequence_step(one(xbc_ref), one(dtk_ref), one(z_ref), dtb_ref, alog_ref, dsk_ref, nw_ref, e_ref, one(y_ref),
                          one(ys_ref), one(prev_ref), st_ref.at[b])

    def sequence_step(xbc_ref, dtk_ref, z_ref, dtb_ref, alog_ref, dsk_ref, nw_ref, e_ref, y_ref, ys_ref, prev_ref, st_ref):
        q = _ssd_common(xbc_ref, dtk_ref, dtb_ref, alog_ref, e_ref)
        xtb = q["xt"].astype(BF16)
        xwb = (q["xt"] * q["w_f"]).astype(BF16)
        alast = q["acum"][L - 1:L, :]
        ys = []
        for g in range(SSD_GROUPS):
            bg = q["bm"][:, g * 128:(g + 1) * 128]
            cg = q["cm"][:, g * 128:(g + 1) * 128]
            G = _dot_nt(cg, bg)
            for pr in range(SSD_HEADS // SSD_GROUPS // 2):
                h0 = g * 8 + 2 * pr
                lo = h0 * SSD_HEAD_DIM
                xt_p = xtb[:, lo:lo + 128]
                ydiag = jnp.zeros((L, LANES), F32)
                for k in range(2):
                    M = (G * _decay_matrix(q, h0 + k)).astype(BF16)
                    ydiag = ydiag + _dot(M, jnp.where(_head_mask(k), xt_p, jnp.zeros_like(xt_p)))
                hp = st_ref[lo:lo + 128, :]
                prev_ref[0, 0, lo:lo + 128, :] = hp.astype(BF16)
                zoff = _dot_nt(cg, hp.astype(BF16))
                ys.append(ydiag + zoff * q["e_f"][:, lo:lo + 128])
                st_ref[lo:lo + 128, :] = _pair_decay(alast, h0) * hp + _dot_tn(xwb[:, lo:lo + 128], bg)
        y = jnp.concatenate(ys, axis=1) + dsk_ref[...] * q["xs"]
        y_ref[0] = y.astype(BF16)
        _, _, parts = _gated_norm(y, z_ref[0].astype(F32), nw_ref[...])
        half = D_SSD // SSD_GROUPS
        ys_ref[0] = jnp.concatenate(
            [xh * nw_ref[:, g * half:(g + 1) * half] for g, (xh, _) in enumerate(parts)], axis=1).astype(BF16)

    chunk = lambda n: pl.BlockSpec((B, L, n), lambda c: (0, c, 0))
    vec = pl.BlockSpec((1, LANES), lambda c: (0, 0))
    return pl.pallas_call(
        body, grid=(nc,), name="ssd_fwd",
        in_specs=[chunk(D_CONV), chunk(LANES), chunk(D_SSD), vec, vec, pl.BlockSpec((1, D_SSD), lambda c: (0, 0)),
                  pl.BlockSpec((1, D_SSD), lambda c: (0, 0)), pl.BlockSpec((LANES, D_SSD), lambda c: (0, 0))],
        out_specs=[chunk(D_SSD), chunk(D_SSD), pl.BlockSpec((B, 1, D_SSD, SSD_STATE), lambda c: (0, c, 0, 0))],
        out_shape=[jax.ShapeDtypeStruct((B, S, D_SSD), BF16), jax.ShapeDtypeStruct((B, S, D_SSD), BF16),
                   jax.ShapeDtypeStruct((B, nc, D_SSD, SSD_STATE), BF16)],
        scratch_shapes=[pltpu.VMEM((B, D_SSD, SSD_STATE), F32)],
        compiler_params=_cparams(("arbitrary",)),
    )(xbc, dtk, z, dtb, alog, dsk, nw, expand)


def _ssd_bwd(xbc, dtk, z, y, prev, dys, dtb, alog, dsk, nw, expand):
    B, S, _ = xbc.shape
    L = CHUNK
    nc = S // L
    half = D_SSD // SSD_GROUPS

    def body(xbc_ref, dtk_ref, z_ref, y_ref, prev_ref, dys_ref, dtb_ref, alog_ref, dsk_ref, nw_ref, e_ref,
             dxbc_ref, ddtk_ref, dz_ref, dnw_ref, dvec_ref, dh_ref, dskc_ref):
        @pl.when(pl.program_id(0) == 0)
        def _():
            dnw_ref[...] = jnp.zeros_like(dnw_ref)
            dvec_ref[...] = jnp.zeros_like(dvec_ref)
            dskc_ref[...] = jnp.zeros_like(dskc_ref)
            dh_ref[...] = jnp.zeros_like(dh_ref)

        for b in range(B):
            one = lambda ref: ref.at[pl.ds(b, 1)]
            sequence_step(one(xbc_ref), one(dtk_ref), one(z_ref), one(y_ref), one(prev_ref), one(dys_ref), dtb_ref, alog_ref,
                          dsk_ref, nw_ref, e_ref, one(dxbc_ref), one(ddtk_ref), one(dz_ref), dnw_ref, dvec_ref, dh_ref.at[b],
                          dskc_ref)

        @pl.when(pl.program_id(0) == nc - 1)
        def _():
            dvec_ref[2:3, :] = _gather_heads(jnp.broadcast_to(dskc_ref[...], (8, D_SSD)), e_ref[...])[0:1, :]

    def sequence_step(xbc_ref, dtk_ref, z_ref, y_ref, prev_ref, dys_ref, dtb_ref, alog_ref, dsk_ref, nw_ref, e_ref,
                      dxbc_ref, ddtk_ref, dz_ref, dnw_ref, dvec_ref, dh_ref, dskc_ref):
        q = _ssd_common(xbc_ref, dtk_ref, dtb_ref, alog_ref, e_ref)
        E = q["E"]
        xs = q["xs"]
        yv = y_ref[0].astype(F32)
        zz = z_ref[0].astype(F32)
        sig, sil, parts = _gated_norm(yv, zz, nw_ref[...])
        dn = dys_ref[0].astype(F32)
        dyg, dnw_rows = [], []
        for g, (xh, r) in enumerate(parts):
            dpart, dw_rows = _rms_bwd(dn[:, g * half:(g + 1) * half], xh, r, nw_ref[:, g * half:(g + 1) * half])
            dyg.append(dpart)
            dnw_rows.append(dw_rows)
        dyg = jnp.concatenate(dyg, axis=1)
        dnw_ref[...] += _colsum(jnp.concatenate(dnw_rows, axis=1))
        dY = dyg * sil
        dz_ref[0] = (dyg * yv * (sig * (1.0 + zz * (1.0 - sig)))).astype(BF16)
        dsk_f = dsk_ref[...]
        dskc_ref[...] += _colsum(dY * xs)
        dYb = dY.astype(BF16)
        xtb = q["xt"].astype(BF16)
        xwb = (q["xt"] * q["w_f"]).astype(BF16)
        acum = q["acum"]
        alast = acum[L - 1:L, :]
        lane_id = lax.broadcasted_iota(jnp.int32, (L, LANES), 1)
        sub_id = lax.broadcasted_iota(jnp.int32, (LANES, L), 0)
        lane_row = lax.broadcasted_iota(jnp.int32, (1, LANES), 1)
        da_rows = jnp.zeros((L, LANES), F32)
        daT = jnp.zeros((LANES, L), F32)
        dxt, prod_off, prod_st, dbs, dcs = [], [], [], [], []
        hsum_row = jnp.zeros((1, LANES), F32)
        for g in range(SSD_GROUPS):
            bg = q["bm"][:, g * 128:(g + 1) * 128]
            cg = q["cm"][:, g * 128:(g + 1) * 128]
            G = _dot_nt(cg, bg)
            dG = jnp.zeros((L, L), F32)
            dcg = jnp.zeros((L, SSD_STATE), F32)
            dbg = jnp.zeros((L, SSD_STATE), F32)
            for pr in range(SSD_HEADS // SSD_GROUPS // 2):
                h0 = g * 8 + 2 * pr
                lo = h0 * SSD_HEAD_DIM
                cols = slice(lo, lo + 128)
                dY_p = dYb[:, cols]
                xt_p = xtb[:, cols]
                dxt_p = jnp.zeros((L, LANES), F32)
                for k in range(2):
                    h = h0 + k
                    Lm = _decay_matrix(q, h)
                    Mf = G * Lm
                    dYk = jnp.where(_head_mask(k), dY_p, jnp.zeros_like(dY_p))
                    dM = _dot_nt(dYk, xt_p)
                    dxt_p = dxt_p + _dot_tn(Mf.astype(BF16), dYk)
                    dG = dG + dM * Lm
                    Q = dM * Mf
                    da_rows = da_rows + jnp.where(lane_id == h, jnp.sum(Q, axis=1, keepdims=True), 0.0)
                    daT = daT + jnp.where(sub_id == h, jnp.sum(Q, axis=0, keepdims=True), 0.0)
                hpb = prev_ref[0, 0, lo:lo + 128, :]
                hp = hpb.astype(F32)
                zoff = _dot_nt(cg, hpb)
                e_p = q["e_f"][:, cols]
                dY_pf = dY[:, cols]
                dZb = (dY_pf * e_p).astype(BF16)
                dcg = dcg + _dot(dZb, hpb)
                dhp_off = _dot_tn(dZb, cg)
                prod_off.append(dY_pf * zoff * e_p)
                dS = dh_ref[lo:lo + 128, :]
                dSb = dS.astype(BF16)
                U = _dot_nt(bg, dSb)
                dxt_p = dxt_p + U * q["w_f"][:, cols]
                dbg = dbg + _dot(xwb[:, cols], dSb)
                prod_st.append(q["xt"][:, cols] * U)
                dh_ref[lo:lo + 128, :] = _pair_decay(alast, h0) * dS + dhp_off
                dsh = dS * hp
                for k in range(2):
                    total = jnp.sum(dsh[k * SSD_HEAD_DIM:(k + 1) * SSD_HEAD_DIM, :], axis=(0, 1), keepdims=True)
                    hsum_row = hsum_row + jnp.where(lane_row == h0 + k, total, 0.0)
                dxt.append(dxt_p)
            dGb = dG.astype(BF16)
            dcs.append(dcg + _dot(dGb, bg))
            dbs.append(dbg + _dot_tn(dGb, cg))
        dxt = jnp.concatenate(dxt, axis=1)
        da_rows = da_rows + _gather_heads(jnp.concatenate(prod_off, axis=1), E)
        dww = _gather_heads(jnp.concatenate(prod_st, axis=1), E) * jnp.exp(alast - acum)
        da_rows = da_rows - dww
        dlast = _colsum(dww) + jnp.exp(alast) * hsum_row
        triT = q["triT"]
        ddA = (_sum3(lambda part: _dot(triT, part), _split3(da_rows))
               - _sum3(lambda part: _dot_nt(triT, part), _split3(daT)) + dlast)
        ddA = jnp.where(q["head"], ddA, 0.0)
        ddt = ddA * q["a128"] + _gather_heads(dxt * xs, E)
        ddt_raw = jnp.where(q["head"], ddt * _sigmoid(q["pre"]), 0.0)
        ddtk_ref[0] = ddt_raw
        dxs = dxt * q["dt_f"] + dsk_f * dY
        dxbc_ref[0] = jnp.concatenate([dxs] + dbs + dcs, axis=1).astype(BF16)
        dvec_ref[0:1, :] += _colsum(ddt_raw)
        dvec_ref[1:2, :] += _colsum(ddA * q["dt"]) * q["a128"]

    rev = lambda n: pl.BlockSpec((B, L, n), lambda c: (0, nc - 1 - c, 0))
    vec = pl.BlockSpec((1, LANES), lambda c: (0, 0))
    sd = jax.ShapeDtypeStruct
    return pl.pallas_call(
        body, grid=(nc,), name="ssd_bwd",
        in_specs=[rev(D_CONV), rev(LANES), rev(D_SSD), rev(D_SSD),
                  pl.BlockSpec((B, 1, D_SSD, SSD_STATE), lambda c: (0, nc - 1 - c, 0, 0)), rev(D_SSD), vec, vec,
                  pl.BlockSpec((1, D_SSD), lambda c: (0, 0)),
                  pl.BlockSpec((1, D_SSD), lambda c: (0, 0)), pl.BlockSpec((LANES, D_SSD), lambda c: (0, 0))],
        out_specs=[rev(D_CONV), rev(LANES), rev(D_SSD), pl.BlockSpec((1, D_SSD), lambda c: (0, 0)),
                   pl.BlockSpec((8, LANES), lambda c: (0, 0))],
        out_shape=[sd((B, S, D_CONV), BF16), sd((B, S, LANES), F32), sd((B, S, D_SSD), BF16), sd((1, D_SSD), F32),
                   sd((8, LANES), F32)],
        scratch_shapes=[pltpu.VMEM((B, D_SSD, SSD_STATE), F32), pltpu.VMEM((1, D_SSD), F32)],
        compiler_params=_cparams(("arbitrary",)),
    )(xbc, dtk, z, y, prev, dys, dtb, alog, dsk, nw, expand)


def _rope_tables(pos_ref, invf_ref, place_ref):
    ang = invf_ref[...] * pos_ref[0].astype(F32)
    place = place_ref[...]
    cosf = 1.0 + _sum3(lambda part: _dot_tn(part, place), _split3(jnp.cos(ang) - 1.0))
    sinf = _sum3(lambda part: _dot_tn(part, place), _split3(jnp.sin(ang)))
    return cosf, sinf


def _rot(u):
    lane = lax.broadcasted_iota(jnp.int32, u.shape, 1)
    first = (lane >= QK_NOPE) & (lane < QK_NOPE + QK_ROPE // 2)
    second = (lane >= QK_NOPE + QK_ROPE // 2) & (lane < QK_DIM)
    return jnp.where(first, -pltpu.roll(u, LANES - QK_ROPE // 2, 1), jnp.where(second, pltpu.roll(u, QK_ROPE // 2, 1), 0.0))


def _rope_lanes(shape):
    lane = lax.broadcasted_iota(jnp.int32, shape, 1)
    return (lane >= QK_NOPE) & (lane < QK_DIM)


def _mla_prep(cq, ckv, dtk, pos, qw, kvw, wuq, wukv, invf, place):
    T = cq.shape[0]
    tm = min(WIDE_TOKEN_TILE, T)
    scale = 1.0 / math.sqrt(QK_DIM)
    HW = MLA_HEADS * HEAD_LANES

    def body(cq_ref, ckv_ref, dtk_ref, pos_ref, qw_ref, kvw_ref, wuq_ref, wukv_ref, invf_ref, place_ref, q_ref, k_ref, v_ref,
             cos_ref, sin_ref):
        xh, _ = _rms_stats(cq_ref[...])
        qv = _dot((xh * qw_ref[...]).astype(BF16), wuq_ref[...])
        xh, _ = _rms_stats(ckv_ref[...])
        kv = _dot((xh * kvw_ref[...]).astype(BF16), wukv_ref[...])
        cosf, sinf = _rope_tables(pos_ref, invf_ref, place_ref)
        cos_ref[...] = cosf
        sin_ref[...] = sinf
        rope = lambda u: u * cosf + _rot(u) * sinf
        dtkv = dtk_ref[...]
        kr = rope(jnp.where(_rope_lanes(dtkv.shape), dtkv, 0.0))
        for h in range(MLA_HEADS):
            cols = slice(h * HEAD_LANES, (h + 1) * HEAD_LANES)
            q_ref[:, cols] = (rope(qv[:, cols]) * scale).astype(BF16)
            k_ref[:, cols] = (kv[:, cols] + kr).astype(BF16)
        v_ref[...] = kv[:, HW:].astype(BF16)

    rows = lambda n: pl.BlockSpec((tm, n), lambda i: (i, 0))
    return pl.pallas_call(
        body, grid=(T // tm,), name="mla_prep",
        in_specs=[rows(Q_LORA), rows(KV_LORA), rows(LANES), pl.BlockSpec((1, 1, tm), lambda i: (i, 0, 0)),
                  _resident((1, Q_LORA)), _resident((1, KV_LORA)), _resident((Q_LORA, HW)), _resident((KV_LORA, 2 * HW)),
                  _resident((QK_ROPE // 2, 1)), _resident((QK_ROPE // 2, LANES))],
        out_specs=[rows(HW), rows(HW), rows(HW), rows(LANES), rows(LANES)],
        out_shape=[jax.ShapeDtypeStruct((T, HW), BF16)] * 3 + [jax.ShapeDtypeStruct((T, LANES), F32)] * 2,
        compiler_params=_cparams(("arbitrary",)),
    )(cq, ckv, dtk, pos.reshape(T // tm, 1, tm), qw, kvw, wuq, wukv, invf, place)


def _mla_prep_bwd(dq, dk, dv, cq, ckv, cos_t, sin_t, qw, kvw, wuq, wukv):
    T = cq.shape[0]
    tm = min(WIDE_TOKEN_TILE, T)
    scale = 1.0 / math.sqrt(QK_DIM)
    HW = MLA_HEADS * HEAD_LANES

    def body(dq_ref, dk_ref, dv_ref, cq_ref, ckv_ref, cos_ref, sin_ref, qw_ref, kvw_ref, wuq_ref, wukv_ref,
             dcq_ref, dckv_ref, ddtk_ref, qn_ref, kvn_ref, dqo_ref, dkvo_ref, dqw_ref, dkvw_ref):
        @pl.when(pl.program_id(0) == 0)
        def _():
            dqw_ref[...] = jnp.zeros_like(dqw_ref)
            dkvw_ref[...] = jnp.zeros_like(dkvw_ref)

        cosf, sinf = cos_ref[...], sin_ref[...]
        unrope = lambda d: d * cosf - _rot(d * sinf)
        dkr = jnp.zeros((tm, LANES), F32)
        nope = lax.broadcasted_iota(jnp.int32, (tm, LANES), 1) < QK_NOPE
        for h in range(MLA_HEADS):
            cols = slice(h * HEAD_LANES, (h + 1) * HEAD_LANES)
            dqo_ref[:, cols] = unrope(dq_ref[:, cols].astype(F32) * scale).astype(BF16)
            dkh = dk_ref[:, cols].astype(F32)
            dkr = dkr + jnp.where(_rope_lanes(dkh.shape), dkh, 0.0)
            dkvo_ref[:, cols] = jnp.where(nope, dkh, 0.0).astype(BF16)
        dkvo_ref[:, HW:] = dv_ref[...].astype(BF16)
        ddtk_ref[...] = unrope(dkr)
        xh, r = _rms_stats(cq_ref[...])
        qn_ref[...] = (xh * qw_ref[...]).astype(BF16)
        dx, dw_rows = _rms_bwd(_dot_nt(dqo_ref[...], wuq_ref[...]), xh, r, qw_ref[...])
        dcq_ref[...] = dx
        dqw_ref[...] += _colsum(dw_rows)
        xh, r = _rms_stats(ckv_ref[...])
        kvn_ref[...] = (xh * kvw_ref[...]).astype(BF16)
        dx, dw_rows = _rms_bwd(_dot_nt(dkvo_ref[...], wukv_ref[...]), xh, r, kvw_ref[...])
        dckv_ref[...] = dx
        dkvw_ref[...] += _colsum(dw_rows)

    rows = lambda n: pl.BlockSpec((tm, n), lambda i: (i, 0))
    sd = jax.ShapeDtypeStruct
    return pl.pallas_call(
        body, grid=(T // tm,), name="mla_prep_bwd",
        in_specs=[rows(HW), rows(HW), rows(HW), rows(Q_LORA), rows(KV_LORA), rows(LANES), rows(LANES), _resident((1, Q_LORA)),
                  _resident((1, KV_LORA)), _resident((Q_LORA, HW)), _resident((KV_LORA, 2 * HW))],
        out_specs=[rows(Q_LORA), rows(KV_LORA), rows(LANES), rows(Q_LORA), rows(KV_LORA), rows(HW), rows(2 * HW),
                   pl.BlockSpec((1, Q_LORA), lambda i: (0, 0)), pl.BlockSpec((1, KV_LORA), lambda i: (0, 0))],
        out_shape=[sd((T, Q_LORA), F32), sd((T, KV_LORA), F32), sd((T, LANES), F32), sd((T, Q_LORA), BF16),
                   sd((T, KV_LORA), BF16), sd((T, HW), BF16), sd((T, 2 * HW), BF16), sd((1, Q_LORA), F32),
                   sd((1, KV_LORA), F32)],
        compiler_params=_cparams(("arbitrary",)),
    )(dq, dk, dv, cq, ckv, cos_t, sin_t, qw, kvw, wuq, wukv)


def _causal_mask(t):
    row = lax.broadcasted_iota(jnp.int32, (t, t), 0)
    col = lax.broadcasted_iota(jnp.int32, (t, t), 1)
    return col <= row


def _attn_fwd(q, k, v):
    B, S, HW = q.shape
    H = HW // HEAD_LANES
    t = min(ATTN_FWD_Q_TILE, S)
    tk = min(ATTN_FWD_KV_TILE, t)
    nq = S // t
    per = t // tk

    pair = 2
    pw = pair * HEAD_LANES

    def body(q_ref, k_ref, v_ref, o_ref, lse_ref):
        qi = pl.program_id(2)
        lanes = [slice(hh * HEAD_LANES, (hh + 1) * HEAD_LANES) for hh in range(pair)]
        qs = [q_ref[0, :, cols] for cols in lanes]

        def step(j, carry, diag):
            sl = pl.ds(pl.multiple_of(j * tk, tk), tk)
            out = []
            for qv, cols, (m, l, acc) in zip(qs, lanes, carry):
                s = _dot_nt(qv, k_ref[0, sl, cols])
                if diag is not None:
                    row = lax.broadcasted_iota(jnp.int32, (t, tk), 0)
                    col = lax.broadcasted_iota(jnp.int32, (t, tk), 1)
                    s = jnp.where(col + diag * tk <= row, s, -1e30)
                m_new = jnp.maximum(m, jnp.max(s, axis=-1, keepdims=True))
                alpha = jnp.exp(m - m_new)
                p = jnp.exp(s - m_new)
                l = alpha * l + jnp.sum(p, axis=-1, keepdims=True)
                acc = alpha * acc + _dot(p.astype(BF16), v_ref[0, sl, cols])
                out.append((m_new, l, acc))
            return tuple(out)

        init = tuple((jnp.full((t, 1), -1e30, F32), jnp.zeros((t, 1), F32), jnp.zeros((t, HEAD_LANES), F32))
                     for _ in range(pair))
        carry = lax.fori_loop(0, qi * per, lambda j, c: step(j, c, None), init)
        for d in range(per):
            carry = step(qi * per + d, carry, d)
        for hh, (m, l, acc) in enumerate(carry):
            o_ref[0, :, lanes[hh]] = (acc / l).astype(BF16)
            lse_ref[0, hh] = m + jnp.log(l)

    return pl.pallas_call(
        body, grid=(B, H // pair, nq), name="attn_fwd",
        in_specs=[pl.BlockSpec((1, t, pw), lambda b, h, i: (b, i, h)),
                  pl.BlockSpec((1, S, pw), lambda b, h, i: (b, 0, h)),
                  pl.BlockSpec((1, S, pw), lambda b, h, i: (b, 0, h))],
        out_specs=[pl.BlockSpec((1, t, pw), lambda b, h, i: (b, i, h)),
                   pl.BlockSpec((1, pair, t, 1), lambda b, h, i: (b, h, i, 0))],
        out_shape=[jax.ShapeDtypeStruct((B, S, HW), BF16), jax.ShapeDtypeStruct((B, H, S, 1), F32)],
        compiler_params=_cparams(("arbitrary", "arbitrary", "arbitrary")),
    )(q, k, v)


def _attn_bwd(q, k, v, o, do, lse):
    B, S, HW = q.shape
    H = HW // HEAD_LANES
    t = min(ATTN_BWD_TILE, S)
    nq = S // t

    pair = 2
    pw = pair * HEAD_LANES

    def body(q_ref, k_ref, v_ref, o_ref, do_ref, lse_ref, dq_out_ref, dk_ref, dv_ref, dq_ref):
        j = pl.program_id(2)

        @pl.when(j == 0)
        def _():
            dq_ref[...] = jnp.zeros_like(dq_ref)

        lanes = [slice(hh * HEAD_LANES, (hh + 1) * HEAD_LANES) for hh in range(pair)]

        def step(i, carry, masked):
            sl = pl.ds(pl.multiple_of(i * t, t), t)
            out = []
            for hh, (cols, (dk, dv)) in enumerate(zip(lanes, carry)):
                kj = k_ref[0, :, cols]
                qi = q_ref[0, sl, cols]
                doi = do_ref[0, sl, cols]
                s = _dot_nt(qi, kj)
                if masked:
                    s = jnp.where(_causal_mask(t), s, -1e30)
                p = jnp.exp(s - lse_ref[0, hh, sl, :])
                dv = dv + _dot_tn(p.astype(BF16), doi)
                dp = _dot_nt(doi, v_ref[0, :, cols])
                delta = jnp.sum(doi.astype(F32) * o_ref[0, sl, cols].astype(F32), axis=-1, keepdims=True)
                dsb = (p * (dp - delta)).astype(BF16)
                dk = dk + _dot_tn(dsb, qi)
                dq_ref[sl, cols] += _dot(dsb, kj)
                out.append((dk, dv))
            return tuple(out)

        zero = jnp.zeros((t, HEAD_LANES), F32)
        carry = step(j, ((zero, zero),) * pair, True)
        carry = lax.fori_loop(j + 1, nq, lambda i, c: step(i, c, False), carry)
        for cols, (dk, dv) in zip(lanes, carry):
            dk_ref[0, :, cols] = dk.astype(BF16)
            dv_ref[0, :, cols] = dv.astype(BF16)

        @pl.when(j == nq - 1)
        def _():
            dq_out_ref[0] = dq_ref[...].astype(BF16)

    full = pl.BlockSpec((1, S, pw), lambda b, h, j: (b, 0, h))
    tile = pl.BlockSpec((1, t, pw), lambda b, h, j: (b, j, h))
    sd = jax.ShapeDtypeStruct
    return pl.pallas_call(
        body, grid=(B, H // pair, nq), name="attn_bwd",
        in_specs=[full, tile, tile, full, full, pl.BlockSpec((1, pair, S, 1), lambda b, h, j: (b, h, 0, 0))],
        out_specs=[full, tile, tile],
        out_shape=[sd((B, S, HW), BF16), sd((B, S, HW), BF16), sd((B, S, HW), BF16)],
        scratch_shapes=[pltpu.VMEM((S, pw), F32)],
        compiler_params=_cparams(("arbitrary", "arbitrary", "arbitrary")),
    )(q, k, v, o, do, lse)


def _mix_out(x1, yssd, o, mw, wout, g, seq):
    T, D = x1.shape
    tm = min(WIDE_TOKEN_TILE, seq)
    tps = seq // tm

    def body(x_ref, ys_ref, o_ref, mw_ref, w_ref, g_ref, xo_ref, m_ref, yc_ref):
        xh, _ = _rms_stats(o_ref[...].astype(F32))
        ycat = jnp.concatenate([ys_ref[...], (xh * mw_ref[...]).astype(BF16)], axis=1)
        m = _dot(ycat, w_ref[...])
        xo_ref[...] = x_ref[...] + g_ref[0] * m
        m_ref[...] = m.astype(BF16)
        yc_ref[...] = ycat

    rows = lambda n: pl.BlockSpec((tm, n), lambda i: (i, 0))
    perb = pl.BlockSpec((1, 1, D), lambda i: (i // tps, 0, 0))
    sd = jax.ShapeDtypeStruct
    return pl.pallas_call(
        body, grid=(T // tm,), name="mix_out",
        in_specs=[rows(D), rows(D_SSD), rows(D_MLA), _resident((1, D_MLA)), _resident((D_SSD + D_MLA, D)), perb],
        out_specs=[rows(D), rows(D), rows(D_SSD + D_MLA)],
        out_shape=[sd((T, D), F32), sd((T, D), BF16), sd((T, D_SSD + D_MLA), BF16)],
        compiler_params=_cparams(("arbitrary",)),
    )(x1, yssd, o, mw, wout, g)


def _mix_out_bwd(dx2, m, o, mw, wout, g, seq):
    T, D = dx2.shape
    B = T // seq
    tm = min(WIDE_TOKEN_TILE, seq)
    tps = seq // tm

    def body(dx_ref, m_ref, o_ref, mw_ref, w_ref, g_ref, dys_ref, do_ref, dm_ref, dg_ref, dmw_ref):
        i = pl.program_id(0)

        @pl.when(i % tps == 0)
        def _():
            dg_ref[...] = jnp.zeros_like(dg_ref)

        @pl.when(i == 0)
        def _():
            dmw_ref[...] = jnp.zeros_like(dmw_ref)

        dxv = dx_ref[...]
        dg_ref[0] += _colsum(dxv * m_ref[...].astype(F32))
        dmb = (g_ref[0] * dxv).astype(BF16)
        dm_ref[...] = dmb
        dycat = _dot_nt(dmb, w_ref[...])
        dys_ref[...] = dycat[:, :D_SSD].astype(BF16)
        xh, r = _rms_stats(o_ref[...].astype(F32))
        dx, dw_rows = _rms_bwd(dycat[:, D_SSD:], xh, r, mw_ref[...])
        do_ref[...] = dx.astype(BF16)
        dmw_ref[...] += _colsum(dw_rows)

    rows = lambda n: pl.BlockSpec((tm, n), lambda i: (i, 0))
    perb = pl.BlockSpec((1, 1, D), lambda i: (i // tps, 0, 0))
    sd = jax.ShapeDtypeStruct
    return pl.pallas_call(
        body, grid=(T // tm,), name="mix_out_bwd",
        in_specs=[rows(D), rows(D), rows(D_MLA), _resident((1, D_MLA)), _resident((D_SSD + D_MLA, D)), perb],
        out_specs=[rows(D_SSD), rows(D_MLA), rows(D), perb, pl.BlockSpec((1, D_MLA), lambda i: (0, 0))],
        out_shape=[sd((T, D_SSD), BF16), sd((T, D_MLA), BF16), sd((T, D), BF16), sd((B, 1, D), F32), sd((1, D_MLA), F32)],
        compiler_params=_cparams(("arbitrary",)),
    )(dx2, m, o, mw, wout, g)


def _win_to_kernel(w):
    z0 = jnp.zeros((48, w.shape[1]), w.dtype)
    z1 = jnp.zeros((32, w.shape[1]), w.dtype)
    return jnp.concatenate([w[:2560], w[2576:3216], w[2560:2576], z0, w[3216:3248], z1], axis=0)


def _win_from_kernel(g):
    return jnp.concatenate([g[:2560], g[3200:3216], g[2560:3200], g[3264:3296]], axis=0)


def _wuq_to_kernel(w):
    w = w.reshape(Q_LORA, MLA_HEADS, QK_DIM)
    return jnp.pad(w, ((0, 0), (0, 0), (0, HEAD_LANES - QK_DIM))).reshape(Q_LORA, MLA_HEADS * HEAD_LANES)


def _wuq_from_kernel(g):
    return g.reshape(Q_LORA, MLA_HEADS, HEAD_LANES)[:, :, :QK_DIM].reshape(Q_LORA, MLA_HEADS * QK_DIM)


def _wukv_to_kernel(w):
    w = w.reshape(KV_LORA, MLA_HEADS, QK_NOPE + V_HEAD)
    kp = jnp.pad(w[:, :, :QK_NOPE], ((0, 0), (0, 0), (0, HEAD_LANES - QK_NOPE)))
    return jnp.concatenate([kp.reshape(KV_LORA, -1), w[:, :, QK_NOPE:].reshape(KV_LORA, -1)], axis=1)


def _wukv_from_kernel(g):
    hw = MLA_HEADS * HEAD_LANES
    kp = g[:, :hw].reshape(KV_LORA, MLA_HEADS, HEAD_LANES)[:, :, :QK_NOPE]
    vp = g[:, hw:].reshape(KV_LORA, MLA_HEADS, V_HEAD)
    return jnp.concatenate([kp, vp], axis=2).reshape(KV_LORA, MLA_HEADS * (QK_NOPE + V_HEAD))


def _lanes16(v):
    return jnp.pad(v.reshape(1, SSD_HEADS), ((0, 0), (0, LANES - SSD_HEADS)))


def _constants():
    e = np.zeros((LANES, D_SSD), np.float32)
    for h in range(SSD_HEADS):
        e[h, h * SSD_HEAD_DIM:(h + 1) * SSD_HEAD_DIM] = 1.0
    inv_freq = ROPE_THETA ** (-jnp.arange(0, QK_ROPE, 2, dtype=F32) / QK_ROPE)
    half = QK_ROPE // 2
    place = np.zeros((half, LANES), np.float32)
    for j in range(half):
        place[j, QK_NOPE + j] = place[j, QK_NOPE + half + j] = 1.0
    return jnp.asarray(e, BF16), inv_freq.reshape(half, 1), jnp.asarray(place, BF16)


def _local_step(x, positions, mod, w, later_weights, small, tgt, on_grads, sync):
    B, S, D = x.shape
    T = B * S
    expand, invf, place = _constants()
    x0 = x.reshape(T, D)
    pos = positions.reshape(T)
    mods = [mod[:, i * D:(i + 1) * D].reshape(B, 1, D) for i in range(N_MOD)]
    sh1, sc1, g1, sh2, sc2, g2, sh3, sc3, g3 = mods
    dtb, alog = _lanes16(small["dt_bias"]), _lanes16(small["a_log"])
    dsk = jnp.repeat(small["d_skip"].reshape(1, SSD_HEADS), SSD_HEAD_DIM, axis=1)

    x1, p1, q1, s1, f1 = _ffn_fwd(x0, small["norm_ffn1"], sh1, sc1, g1, w["ffn1_w_gate"], w["ffn1_w_up"], w["ffn1_w_down"], S,
                                  "ffn1_fwd")
    w = {**w, **later_weights(f1)}
    z, xraw, cq, ckv, dtk = _inproj_fwd(x1, small["norm_mix"], sh2, sc2, w["w_in"], S)
    xraw3 = xraw.reshape(B, S, D_CONV)
    xbc = _conv_fwd(xraw3, small["conv_w"], small["conv_b"])
    dtk3, z3 = dtk.reshape(B, S, LANES), z.reshape(B, S, D_SSD)
    y, yssd, prev = _ssd_fwd(xbc, dtk3, z3, dtb, alog, dsk, small["ssd_norm_w"], expand)
    q, k, v, cos_t, sin_t = _mla_prep(cq, ckv, dtk, pos, small["q_norm_w"], small["kv_norm_w"], w["w_uq"], w["w_ukv"], invf,
                                      place)
    hw = MLA_HEADS * HEAD_LANES
    q3, k3, v3 = q.reshape(B, S, hw), k.reshape(B, S, hw), v.reshape(B, S, hw)
    o3, lse = _attn_fwd(q3, k3, v3)
    o = o3.reshape(T, hw)
    x2, m, ycat = _mix_out(x1, yssd.reshape(T, D_SSD), o, small["mla_norm_w"], w["w_out"], g2, S)
    dx3, p2, q2, s2, f2, loss, d_norm_final = _ffn_fwd(
        x2, small["norm_ffn2"], sh3, sc3, g3, w["ffn2_w_gate"], w["ffn2_w_up"], w["ffn2_w_down"], S, "ffn2_fwd",
        head=(small["norm_final"].reshape(1, D), tgt.reshape(T, D)))

    gw, gs = {}, {}
    dx2, h3, df3, da3, du3, dsh3, dsc3, dg3, gs["norm_ffn2"] = _ffn_bwd(
        dx3, x2, small["norm_ffn2"], sh3, sc3, g3, p2, q2, f2, w["ffn2_w_gate"], w["ffn2_w_up"], w["ffn2_w_down"], S, "ffn2_bwd")
    gw["ffn2_w_gate"], gw["ffn2_w_up"], gw["ffn2_w_down"] = _ffn_wgrad(h3, s2, df3, da3, du3, dsh3, "ffn2_wgrad")
    g2 = g2 + on_grads(("ffn2_w_gate", "ffn2_w_up", "ffn2_w_down"), gw)

    dys, do, dm, dg2, gs["mla_norm_w"] = _mix_out_bwd(dx2, m, o, small["mla_norm_w"], w["w_out"], g2, S)
    gw["w_out"] = _mm_tn(ycat, dm, 512, "dwout")

    dq3, dk3, dv3 = _attn_bwd(q3, k3, v3, o3, do.reshape(B, S, hw), lse)
    dcq, dckv, ddtk_b, qn, kvn, dqb, dkvb, gs["q_norm_w"], gs["kv_norm_w"] = _mla_prep_bwd(
        dq3.reshape(T, hw), dk3.reshape(T, hw), dv3.reshape(T, hw), cq, ckv, cos_t, sin_t, small["q_norm_w"] + sync(dq3),
        small["kv_norm_w"], w["w_uq"], w["w_ukv"])
    gw["w_uq"] = _mm_tn(qn, dqb, 512, "dwuq")
    gw["w_ukv"] = _mm_tn(kvn, dkvb, 1024, "dwukv")

    dxbc, ddtk_a, dz, gs["ssd_norm_w"], dvec = _ssd_bwd(
        xbc, dtk3, z3, y, prev, dys.reshape(B, S, D_SSD), dtb, alog, dsk, small["ssd_norm_w"], expand)
    gs["dt_bias"], gs["a_log"], gs["d_skip"] = dvec[0:1, :SSD_HEADS], dvec[1:2, :SSD_HEADS], dvec[2:3, :SSD_HEADS]
    dxraw, gs["conv_w"], gs["conv_b"] = _conv_bwd(dxbc, xraw3, small["conv_w"], small["conv_b"])
    dx1, h2, dproj, dsh2, dsc2, gs["norm_mix"] = _inproj_bwd(
        dx2, x1, small["norm_mix"], sh2, sc2, w["w_in"], dz.reshape(T, D_SSD), dxraw.reshape(T, D_CONV), dcq, dckv,
        ddtk_a.reshape(T, LANES), ddtk_b, S)
    gw["w_in"] = _mm_tn(dproj, h2, 512, "dwin")
    g1 = g1 + on_grads(("w_in", "w_uq", "w_ukv", "w_out"), gw)

    dx0, h1, df1, da1, du1, dsh1, dsc1, dg1, gs["norm_ffn1"] = _ffn_bwd(
        dx1, x0, small["norm_ffn1"], sh1, sc1, g1, p1, q1, f1, w["ffn1_w_gate"], w["ffn1_w_up"], w["ffn1_w_down"], S, "ffn1_bwd")
    gw["ffn1_w_gate"], gw["ffn1_w_up"], gw["ffn1_w_down"] = _ffn_wgrad(h1, s1, df1, da1, du1, dsh1 + sync(dx0), "ffn1_wgrad")
    gs["norm_final"] = d_norm_final
    dmod = jnp.concatenate([t.reshape(B, D) for t in (dsh1, dsc1, dg1, dsh2, dsc2, dg2, dsh3, dsc3, dg3)], axis=1)
    return loss, dx0.reshape(B, S, D), gw, dmod, gs


HBM_SPEC = pl.BlockSpec(memory_space=pltpu.HBM)
VMEM_SPEC = pl.BlockSpec(memory_space=pltpu.VMEM)


def _place():
    return lax.axis_index("x"), lax.axis_index("y"), lax.axis_index("c")


def _other_chips(mx, my):
    return [(1 - mx, my), (mx, 1 - my), (1 - mx, 1 - my)]


def _remote(src, dst, send_sem, recv_sem, to):
    return pltpu.make_async_remote_copy(src_ref=src, dst_ref=dst, send_sem=send_sem, recv_sem=recv_sem,
                                        device_id=to, device_id_type=MESH)


def _all_gather_small(xa, name):
    r, n = xa.shape

    def body(x_ref, o_ref, token, send_sems, recv_sems):
        mx, my, mc = _place()
        me = 4 * mx + 2 * my + mc
        token[...] = jnp.zeros_like(token)
        o_ref[pl.ds(me, 1)] = x_ref[...][None]
        sends = []
        for k in range(1, N_DEV):
            peer = (mx ^ (k >> 2), my ^ ((k >> 1) & 1), mc ^ (k & 1))
            cp = _remote(x_ref, o_ref.at[me], send_sems.at[k - 1], recv_sems.at[k - 1], peer)
            cp.start()
            sends.append(cp)
        for k in range(1, N_DEV):
            peer = (mx ^ (k >> 2), my ^ ((k >> 1) & 1), mc ^ (k & 1))
            slot = 4 * peer[0] + 2 * peer[1] + peer[2]
            _remote(x_ref, o_ref.at[slot], send_sems.at[k - 1], recv_sems.at[k - 1], peer).wait_recv()
        for cp in sends:
            cp.wait_send()

    return pl.pallas_call(
        body, name=name, in_specs=[VMEM_SPEC], out_specs=[VMEM_SPEC, VMEM_SPEC],
        out_shape=[jax.ShapeDtypeStruct((N_DEV, r, n), xa.dtype), jax.ShapeDtypeStruct((8, LANES), F32)],
        scratch_shapes=[pltpu.SemaphoreType.DMA((N_DEV - 1,)), pltpu.SemaphoreType.DMA((N_DEV - 1,))],
        compiler_params=pltpu.CompilerParams(vmem_limit_bytes=VMEM_LIMIT),
    )(xa)


def _halves_by_rows(shape):
    return (shape[-2] // 2) % 16 == 0


def _half_shape(shape):
    r, c = shape[-2:]
    return tuple(shape[:-2]) + ((r // 2, c) if _halves_by_rows(shape) else (r, c // 2))


def _half_index(shape, hc):
    r, c = shape[-2:]
    if _halves_by_rows(shape):
        return (pl.ds(pl.multiple_of(hc * (r // 2), 16), r // 2), slice(None))
    return (slice(None), pl.ds(pl.multiple_of(hc * (c // 2), LANES), c // 2))


def _half(ref, hc, lead=None):
    idx = _half_index(ref.shape, hc)
    return ref.at[idx] if lead is None else ref.at[(lead,) + idx]


def _gather_weights(shards):
    n = len(shards)

    def body(*refs):
        w_refs, o_refs, token = refs[:n], refs[n:2 * n], refs[2 * n]
        send_sems, recv_sems, stage_sems = refs[2 * n + 1:2 * n + 4]
        stages = refs[2 * n + 4:]
        mx, my, mc = _place()
        chip = 2 * mx + my
        others = _other_chips(mx, my)
        sibling = (mx, my, 1 - mc)
        token[...] = jnp.zeros_like(token)
        stage_in = [pltpu.make_async_copy(w, st, stage_sems.at[0, i]) for i, (w, st) in enumerate(zip(w_refs, stages))]
        for cp in stage_in:
            cp.start()
        first = []
        for i, (w, o) in enumerate(zip(w_refs, o_refs)):
            for k, (cx, cy) in enumerate(others):
                first.append(_remote(_half(w, mc), _half(o, mc, chip), send_sems.at[i, k],
                                     recv_sems.at[i, k], (cx, cy, mc)))
                first[-1].start()
        stage_out = []
        for i, (st, o) in enumerate(zip(stages, o_refs)):
            stage_in[i].wait()
            stage_out.append(pltpu.make_async_copy(st, o.at[chip], stage_sems.at[1, i]))
            stage_out[-1].start()
        passed = []
        for i, (w, o) in enumerate(zip(w_refs, o_refs)):
            for k, (cx, cy) in enumerate(others):
                landed = _half(o, mc, 2 * cx + cy)
                _remote(landed, landed, send_sems.at[i, k], recv_sems.at[i, k], (cx, cy, mc)).wait_recv()
                passed.append(_remote(landed, landed, send_sems.at[i, 3 + k], recv_sems.at[i, 3 + k], sibling))
                passed[-1].start()
        for i, (w, o) in enumerate(zip(w_refs, o_refs)):
            for k, (cx, cy) in enumerate(others):
                there = _half(o, 1 - mc, 2 * cx + cy)
                _remote(there, there, send_sems.at[i, 3 + k], recv_sems.at[i, 3 + k], sibling).wait_recv()
        for cp in first + passed:
            cp.wait_send()
        for cp in stage_out:
            cp.wait()

    out = pl.pallas_call(
        body, name="gather_weights", in_specs=[HBM_SPEC] * n, out_specs=[HBM_SPEC] * n + [VMEM_SPEC],
        out_shape=[jax.ShapeDtypeStruct((N_CHIPS,) + s.shape, s.dtype) for s in shards] + [jax.ShapeDtypeStruct((8, LANES), F32)],
        scratch_shapes=[pltpu.SemaphoreType.DMA((n, 6)), pltpu.SemaphoreType.DMA((n, 6)), pltpu.SemaphoreType.DMA((2, n))]
        + [pltpu.VMEM(s.shape, s.dtype) for s in shards],
        compiler_params=pltpu.CompilerParams(vmem_limit_bytes=VMEM_LIMIT),
    )(*shards)
    return out[:n], out[n]


SEM_SPEC = pl.BlockSpec(memory_space=pltpu.SEMAPHORE)
ANY_SPEC = pl.BlockSpec(memory_space=pl.ANY)
DATAFLOW = pltpu.SideEffectType.DATAFLOW_SIDE_EFFECTING


def _hbm(arr):
    return pltpu.with_memory_space_constraint(arr, pltpu.HBM)


def _gather_start(shards):
    n = len(shards)

    def body(*refs):
        w_refs, land_refs, send_sems, recv_sems, token = refs[:n], refs[n:2 * n], refs[2 * n], refs[2 * n + 1], refs[-1]
        mx, my, mc = _place()
        chip = 2 * mx + my
        for i, (w, land) in enumerate(zip(w_refs, land_refs)):
            for k, (cx, cy) in enumerate(_other_chips(mx, my)):
                _remote(_half(w, mc), _half(land, mc, chip), send_sems.at[3 * i + k],
                        recv_sems.at[3 * i + k], (cx, cy, mc)).start()
        token[...] = jnp.zeros_like(token)

    lands = [lax.empty((N_CHIPS,) + s.shape, s.dtype) for s in shards]
    out = pl.pallas_call(
        body, name="gather_start",
        out_shape=(pltpu.SemaphoreType.DMA((3 * n,)), pltpu.SemaphoreType.DMA((3 * n,)),
                   *[pltpu.HBM(s.shape, s.dtype) for s in shards], *[pltpu.HBM(l.shape, l.dtype) for l in lands],
                   jax.ShapeDtypeStruct((8, LANES), F32)),
        in_specs=[HBM_SPEC] * (2 * n), out_specs=(SEM_SPEC, SEM_SPEC, *[HBM_SPEC] * (2 * n), VMEM_SPEC),
        input_output_aliases={i: 2 + i for i in range(2 * n)},
        compiler_params=pltpu.CompilerParams(has_side_effects=DATAFLOW),
    )(*[_hbm(s) for s in shards], *[_hbm(l) for l in lands])
    return out[0], out[1], out[2:2 + n], out[2 + n:2 + 2 * n], out[-1]


def _gather_wait(send_sems, recv_sems, shards, lands, after):
    n = len(shards)

    def body(*refs):
        w_refs, land_refs, send_sems, recv_sems = refs[:n], refs[n:2 * n], refs[2 * n], refs[2 * n + 1]
        mx, my, mc = _place()
        for i, (w, land) in enumerate(zip(w_refs, land_refs)):
            for k, (cx, cy) in enumerate(_other_chips(mx, my)):
                cp = _remote(_half(w, mc), _half(land, mc, 2 * cx + cy), send_sems.at[3 * i + k],
                             recv_sems.at[3 * i + k], (cx, cy, mc))
                cp.wait_send()
                cp.wait_recv()

    out = pl.pallas_call(
        body, name="gather_wait",
        out_shape=(*[pltpu.HBM(s.shape, s.dtype) for s in shards], *[pltpu.HBM(l.shape, l.dtype) for l in lands]),
        in_specs=[HBM_SPEC] * (2 * n) + [SEM_SPEC, SEM_SPEC, ANY_SPEC], out_specs=tuple([HBM_SPEC] * (2 * n)),
        input_output_aliases={i: i for i in range(2 * n)},
        compiler_params=pltpu.CompilerParams(has_side_effects=DATAFLOW),
    )(*shards, *lands, send_sems, recv_sems, after)
    return out[n:]


def _gather_finish(shards, lands):
    n = len(shards)

    def body(*refs):
        w_refs, land_refs, o_refs = refs[:n], refs[n:2 * n], refs[2 * n:3 * n]
        send_sems, recv_sems, stage_sems = refs[3 * n:3 * n + 3]
        stages = refs[3 * n + 3:]
        mx, my, mc = _place()
        chip = 2 * mx + my
        others = _other_chips(mx, my)
        sibling = (mx, my, 1 - mc)
        stage_in = [pltpu.make_async_copy(w, st, stage_sems.at[0, i]) for i, (w, st) in enumerate(zip(w_refs, stages))]
        for cp in stage_in:
            cp.start()
        passed = []
        for i, (w, o) in enumerate(zip(w_refs, o_refs)):
            for k, (cx, cy) in enumerate(others):
                landed = _half(o, mc, 2 * cx + cy)
                passed.append(_remote(landed, landed, send_sems.at[i, k], recv_sems.at[i, k], sibling))
                passed[-1].start()
        stage_out = []
        for i, (st, o) in enumerate(zip(stages, o_refs)):
            stage_in[i].wait()
            stage_out.append(pltpu.make_async_copy(st, o.at[chip], stage_sems.at[1, i]))
            stage_out[-1].start()
        for i, (w, o) in enumerate(zip(w_refs, o_refs)):
            for k, (cx, cy) in enumerate(others):
                there = _half(o, 1 - mc, 2 * cx + cy)
                _remote(there, there, send_sems.at[i, k], recv_sems.at[i, k], sibling).wait_recv()
        for cp in passed:
            cp.wait_send()
        for cp in stage_out:
            cp.wait()

    return pl.pallas_call(
        body, name="gather_finish", in_specs=[HBM_SPEC] * (2 * n), out_specs=[HBM_SPEC] * n,
        out_shape=[jax.ShapeDtypeStruct(l.shape, l.dtype) for l in lands],
        input_output_aliases={n + i: i for i in range(n)},
        scratch_shapes=[pltpu.SemaphoreType.DMA((n, 3)), pltpu.SemaphoreType.DMA((n, 3)), pltpu.SemaphoreType.DMA((2, n))]
        + [pltpu.VMEM(s.shape, s.dtype) for s in shards],
        compiler_params=pltpu.CompilerParams(vmem_limit_bytes=VMEM_LIMIT),
    )(*shards, *lands)


def _scatter_start(ss, tag):
    n = len(ss)

    def body(*refs):
        s_refs, land_refs, send_sems, recv_sems, token = refs[:n], refs[n:2 * n], refs[2 * n], refs[2 * n + 1], refs[-1]
        mx, my, mc = _place()
        chip = 2 * mx + my
        for i, (s, land) in enumerate(zip(s_refs, land_refs)):
            for k, (cx, cy) in enumerate(_other_chips(mx, my)):
                _remote(s.at[2 * cx + cy], land.at[chip], send_sems.at[3 * i + k], recv_sems.at[3 * i + k],
                        (cx, cy, mc)).start()
        token[...] = jnp.zeros_like(token)

    lands = [lax.empty(s.shape, s.dtype) for s in ss]
    out = pl.pallas_call(
        body, name="scatter_start_" + tag,
        out_shape=(pltpu.SemaphoreType.DMA((3 * n,)), pltpu.SemaphoreType.DMA((3 * n,)),
                   *[pltpu.HBM(s.shape, s.dtype) for s in ss], *[pltpu.HBM(l.shape, l.dtype) for l in lands],
                   jax.ShapeDtypeStruct((8, LANES), F32)),
        in_specs=[HBM_SPEC] * (2 * n), out_specs=(SEM_SPEC, SEM_SPEC, *[HBM_SPEC] * (2 * n), VMEM_SPEC),
        input_output_aliases={i: 2 + i for i in range(2 * n)},
        compiler_params=pltpu.CompilerParams(has_side_effects=DATAFLOW),
    )(*[_hbm(s) for s in ss], *[_hbm(l) for l in lands])
    return out[0], out[1], out[2:2 + n], out[2 + n:2 + 2 * n], out[-1]


def _scatter_wait(send_sems, recv_sems, ss, lands, after, tag):
    n = len(ss)

    def body(*refs):
        s_refs, land_refs, send_sems, recv_sems = refs[:n], refs[n:2 * n], refs[2 * n], refs[2 * n + 1]
        mx, my, mc = _place()
        for i, (s, land) in enumerate(zip(s_refs, land_refs)):
            for k, (cx, cy) in enumerate(_other_chips(mx, my)):
                slot = land.at[2 * cx + cy]
                cp = _remote(s.at[2 * cx + cy], slot, send_sems.at[3 * i + k], recv_sems.at[3 * i + k], (cx, cy, mc))
                cp.wait_send()
                cp.wait_recv()

    out = pl.pallas_call(
        body, name="scatter_wait_" + tag,
        out_shape=(*[pltpu.HBM(s.shape, s.dtype) for s in ss], *[pltpu.HBM(l.shape, l.dtype) for l in lands]),
        in_specs=[HBM_SPEC] * (2 * n) + [SEM_SPEC, SEM_SPEC, ANY_SPEC], out_specs=tuple([HBM_SPEC] * (2 * n)),
        input_output_aliases={i: i for i in range(2 * n)},
        compiler_params=pltpu.CompilerParams(has_side_effects=DATAFLOW),
    )(*ss, *lands, send_sems, recv_sems, after)
    return out[:n], out[n:]


def _swap_halves(gs, after, name):
    n = len(gs)

    def body(*refs):
        g_refs, o_refs, send_sems, recv_sems = refs[:n], refs[n + 1:2 * n + 1], refs[2 * n + 1], refs[2 * n + 2]
        mx, my, mc = _place()
        copies = []
        for i, (g, o) in enumerate(zip(g_refs, o_refs)):
            src = g.at[(slice(None),) + _half_index(g.shape, 1 - mc)]
            copies.append(_remote(src, o, send_sems.at[i], recv_sems.at[i], (mx, my, 1 - mc)))
            copies[-1].start()
        for cp in copies:
            cp.wait()

    return pl.pallas_call(
        body, name=name, in_specs=[HBM_SPEC] * n + [ANY_SPEC], out_specs=[HBM_SPEC] * n,
        out_shape=[jax.ShapeDtypeStruct(_half_shape(g.shape), g.dtype) for g in gs],
        scratch_shapes=[pltpu.SemaphoreType.DMA((n,)), pltpu.SemaphoreType.DMA((n,))],
    )(*gs, after)


def _swap_start(gs, tag):
    n = len(gs)

    def body(*refs):
        g_refs, land_refs, send_sems, recv_sems, token = refs[:n], refs[n:2 * n], refs[2 * n], refs[2 * n + 1], refs[-1]
        mx, my, mc = _place()
        for i, (g, land) in enumerate(zip(g_refs, land_refs)):
            src = g.at[(slice(None),) + _half_index(g.shape, 1 - mc)]
            _remote(src, land, send_sems.at[i], recv_sems.at[i], (mx, my, 1 - mc)).start()
        token[...] = jnp.zeros_like(token)

    lands = [lax.empty(_half_shape(g.shape), g.dtype) for g in gs]
    out = pl.pallas_call(
        body, name="swap_start_" + tag,
        out_shape=(pltpu.SemaphoreType.DMA((n,)), pltpu.SemaphoreType.DMA((n,)), *[pltpu.HBM(g.shape, g.dtype) for g in gs],
                   *[pltpu.HBM(l.shape, l.dtype) for l in lands], jax.ShapeDtypeStruct((8, LANES), F32)),
        in_specs=[HBM_SPEC] * (2 * n), out_specs=(SEM_SPEC, SEM_SPEC, *[HBM_SPEC] * (2 * n), VMEM_SPEC),
        input_output_aliases={i: 2 + i for i in range(2 * n)},
        compiler_params=pltpu.CompilerParams(has_side_effects=DATAFLOW),
    )(*[_hbm(g) for g in gs], *[_hbm(l) for l in lands])
    return out[0], out[1], out[2:2 + n], out[2 + n:2 + 2 * n], out[-1]


def _swap_wait(send_sems, recv_sems, gs, lands, after, tag):
    n = len(gs)

    def body(*refs):
        g_refs, land_refs, send_sems, recv_sems = refs[:n], refs[n:2 * n], refs[2 * n], refs[2 * n + 1]
        mx, my, mc = _place()
        for i, (g, land) in enumerate(zip(g_refs, land_refs)):
            src = g.at[(slice(None),) + _half_index(g.shape, 1 - mc)]
            cp = _remote(src, land, send_sems.at[i], recv_sems.at[i], (mx, my, 1 - mc))
            cp.wait_send()
            cp.wait_recv()

    out = pl.pallas_call(
        body, name="swap_wait_" + tag,
        out_shape=(*[pltpu.HBM(g.shape, g.dtype) for g in gs], *[pltpu.HBM(l.shape, l.dtype) for l in lands]),
        in_specs=[HBM_SPEC] * (2 * n) + [SEM_SPEC, SEM_SPEC, ANY_SPEC], out_specs=tuple([HBM_SPEC] * (2 * n)),
        input_output_aliases={i: i for i in range(2 * n)},
        compiler_params=pltpu.CompilerParams(has_side_effects=DATAFLOW),
    )(*gs, *lands, send_sems, recv_sems, after)
    return out[:n], out[n:]


def _pair_sum(g, got, core, name):
    hr, hc = _half_shape(g.shape)[1:]
    by_rows = _halves_by_rows(g.shape)

    def body(core_ref, g_ref, got_ref, o_ref):
        o_ref[...] = (g_ref[...].astype(F32) + got_ref[...].astype(F32)).astype(BF16)

    return pl.pallas_call(
        body, name=name,
        grid_spec=pltpu.PrefetchScalarGridSpec(
            num_scalar_prefetch=1, grid=(N_CHIPS,),
            in_specs=[pl.BlockSpec((1, hr, hc), lambda j, core_ref: (j, core_ref[0], 0) if by_rows else (j, 0, core_ref[0])),
                      pl.BlockSpec((1, hr, hc), lambda j, core_ref: (j, 0, 0))],
            out_specs=pl.BlockSpec((1, hr, hc), lambda j, core_ref: (j, 0, 0))),
        out_shape=jax.ShapeDtypeStruct((N_CHIPS, hr, hc), BF16),
        compiler_params=_cparams(("arbitrary",)),
    )(core, g, got)


def _chip_sum(own, got, chip, name):
    _, h, c = own.shape

    def body(chip_ref, a_ref, b_ref, c_ref, d_ref, o_ref):
        o_ref[...] = ((a_ref[0].astype(F32) + b_ref[0].astype(F32)) + c_ref[0].astype(F32)) + d_ref[0].astype(F32)

    slot = lambda flip: pl.BlockSpec((1, h, c), lambda i, chip_ref: (chip_ref[0] ^ flip, 0, 0))
    return pl.pallas_call(
        body, name=name,
        grid_spec=pltpu.PrefetchScalarGridSpec(
            num_scalar_prefetch=1, grid=(1,), in_specs=[slot(0), slot(1), slot(2), slot(3)],
            out_specs=pl.BlockSpec((h, c), lambda i, chip_ref: (0, 0))),
        out_shape=jax.ShapeDtypeStruct((h, c), F32),
        compiler_params=_cparams(("arbitrary",)),
    )(chip, own, got, got, got)


def _join_halves(mine, name):
    n = len(mine)

    def body(*refs):
        m_refs, o_refs, send_sems, recv_sems = refs[:n], refs[n:2 * n], refs[2 * n], refs[2 * n + 1]
        mx, my, mc = _place()
        copies = []
        for i, (m, o) in enumerate(zip(m_refs, o_refs)):
            copies.append(_remote(m, o, send_sems.at[i], recv_sems.at[i], (mx, my, 1 - mc)))
            copies[-1].start()
        for cp in copies:
            cp.wait()

    return pl.pallas_call(
        body, name=name, in_specs=[HBM_SPEC] * n, out_specs=[HBM_SPEC] * n,
        out_shape=[jax.ShapeDtypeStruct(m.shape, m.dtype) for m in mine],
        scratch_shapes=[pltpu.SemaphoreType.DMA((n,)), pltpu.SemaphoreType.DMA((n,))],
    )(*mine)


def _adam_math(w, g, m, v):
    m2 = ADAM_B1 * m + (1.0 - ADAM_B1) * g
    v2 = ADAM_B2 * v + (1.0 - ADAM_B2) * (g * g)
    m_hat = m2 * (1.0 / (1.0 - ADAM_B1 ** ADAM_STEP))
    v_hat = v2 * (1.0 / (1.0 - ADAM_B2 ** ADAM_STEP))
    delta = -ADAM_LR * (m_hat / (jnp.sqrt(v_hat) + ADAM_EPS) + ADAM_WD * w)
    return delta, m2, v2


def _adam(w, g, m, v, name):
    def body(w_ref, g_ref, m_ref, v_ref, d_ref, m2_ref, v2_ref):
        d_ref[...], m2_ref[...], v2_ref[...] = _adam_math(w_ref[...], g_ref[...], m_ref[...], v_ref[...])

    return pl.pallas_call(body, name=name, out_shape=[jax.ShapeDtypeStruct(w.shape, F32)] * 3)(w, g, m, v)


def _adam_halves(w, m, v, mine, theirs, core, name):
    hr, hcols = _half_shape(w.shape)[1:]
    by_rows = _halves_by_rows(w.shape)

    def body(core_ref, w_ref, m_ref, v_ref, mine_ref, theirs_ref, g_ref, d_ref, m2_ref, v2_ref):
        g = jnp.where(pl.program_id(0) == core_ref[0], mine_ref[...], theirs_ref[...])
        g_ref[0] = g
        d_ref[0], m2_ref[0], v2_ref[0] = _adam_math(w_ref[0], g, m_ref[0], v_ref[0])

    half = pl.BlockSpec((1, hr, hcols), lambda hc, core_ref: (0, hc, 0) if by_rows else (0, 0, hc))
    whole = pl.BlockSpec((hr, hcols), lambda hc, core_ref: (0, 0))
    return pl.pallas_call(
        body, name=name,
        grid_spec=pltpu.PrefetchScalarGridSpec(
            num_scalar_prefetch=1, grid=(2,), in_specs=[half, half, half, whole, whole], out_specs=[half] * 4),
        out_shape=[jax.ShapeDtypeStruct(w.shape, F32)] * 4,
        compiler_params=_cparams(("arbitrary",)),
    )(core, w, m, v, mine, theirs)


ADA_COLS = N_MOD * D_MODEL // N_CHIPS


def _ada_fwd(c_all, w_ada, b_cols):
    def body(c_ref, w_ref, b_ref, o_ref):
        cv = c_ref[...]
        act = (cv * _sigmoid(cv)).astype(BF16)
        o_ref[...] = _dot(act, w_ref[...].astype(BF16)) + b_ref[...]

    return pl.pallas_call(
        body, name="ada_fwd", out_shape=jax.ShapeDtypeStruct((c_all.shape[0], ADA_COLS), F32),
        compiler_params=pltpu.CompilerParams(vmem_limit_bytes=VMEM_LIMIT),
    )(c_all, w_ada, b_cols)


def _ada_bwd(c_all, dmod_cols, w, m, v):
    nb = c_all.shape[0]
    tn = 384

    def body(c_ref, d_ref, w_ref, m_ref, v_ref, g_ref, dl_ref, m2_ref, v2_ref):
        cv = c_ref[...]
        act = (cv * _sigmoid(cv)).astype(BF16)
        g = _dot_tn(act, d_ref[...].astype(BF16))
        g_ref[...] = g
        dl_ref[...], m2_ref[...], v2_ref[...] = _adam_math(w_ref[...], g, m_ref[...], v_ref[...])

    blk = pl.BlockSpec((D_MODEL, tn), lambda j: (0, j))
    return pl.pallas_call(
        body, name="ada_bwd", grid=(ADA_COLS // tn,),
        in_specs=[pl.BlockSpec((nb, D_MODEL), lambda j: (0, 0)), pl.BlockSpec((nb, tn), lambda j: (0, j)), blk, blk, blk],
        out_specs=[blk] * 4, out_shape=[jax.ShapeDtypeStruct((D_MODEL, ADA_COLS), F32)] * 4,
        compiler_params=_cparams(("arbitrary",)),
    )(c_all, dmod_cols, w, m, v)


SMALL_NAMES = ("norm_ffn1", "norm_mix", "conv_w", "conv_b", "ssd_norm_w", "q_norm_w", "kv_norm_w", "mla_norm_w",
               "norm_ffn2", "norm_final", "dt_bias", "a_log", "d_skip")
SMALL_SIZES = (1024, 1024, CONV_WIDTH * D_CONV, D_CONV, 1024, Q_LORA, KV_LORA, 1024, 1024, 1024, 16, 16, 16)
SMALL_ROWS = 16
MOD_ROWS = 2 * N_MOD
SEND_ROWS = 40


def _pack_small(parts):
    flat = jnp.concatenate([parts[n].reshape(-1) for n in SMALL_NAMES])
    return jnp.pad(flat, (0, SMALL_ROWS * D_MODEL - flat.shape[0]))


def _unpack_small(flat):
    out, off = {}, 0
    for n, size in zip(SMALL_NAMES, SMALL_SIZES):
        out[n] = flat[off:off + size]
        off += size
    return out


def _small_sum(got):
    def body(g_ref, o_ref):
        bsum = jnp.zeros((N_MOD, D_MODEL), F32)
        ssum = jnp.zeros((SMALL_ROWS, D_MODEL), F32)
        for d in range(N_DEV):
            bsum = bsum + g_ref[d, 0:N_MOD, :] + g_ref[d, N_MOD:MOD_ROWS, :]
            ssum = ssum + g_ref[d, MOD_ROWS:MOD_ROWS + SMALL_ROWS, :]
        o_ref[...] = jnp.concatenate([bsum, ssum, jnp.zeros((32 - N_MOD - SMALL_ROWS, D_MODEL), F32)], axis=0)

    return pl.pallas_call(body, name="small_sum", out_shape=jax.ShapeDtypeStruct((32, D_MODEL), F32))(got)


BIG_NAMES = ("ffn1_w_gate", "ffn1_w_up", "ffn1_w_down", "w_in", "w_uq", "w_ukv", "w_out", "ffn2_w_gate", "ffn2_w_up",
             "ffn2_w_down")
_TO_KERNEL = {"w_in": _win_to_kernel, "w_uq": _wuq_to_kernel, "w_ukv": _wukv_to_kernel}
_FROM_KERNEL = {"w_in": _win_from_kernel, "w_uq": _wuq_from_kernel, "w_ukv": _wukv_from_kernel}


def _columns_joined(w4):
    n, r, c = w4.shape
    return w4.transpose(1, 0, 2).reshape(r, n * c)


def _columns_split(g):
    r, cols = g.shape
    return g.reshape(r, N_CHIPS, cols // N_CHIPS).transpose(1, 0, 2)


def kernel(x, c, positions, w_ada, b_ada, norm_ffn1, ffn1_w_gate, ffn1_w_up, ffn1_w_down, norm_mix, w_in, conv_w, conv_b, dt_bias, a_log, d_skip, ssd_norm_w, q_norm_w, w_uq, kv_norm_w, w_ukv, mla_norm_w, w_out, norm_ffn2, ffn2_w_gate, ffn2_w_up, ffn2_w_down, norm_final, loss_target, m_w_ada, m_b_ada, m_norm_ffn1, m_ffn1_w_gate, m_ffn1_w_up, m_ffn1_w_down, m_norm_mix, m_w_in, m_conv_w, m_conv_b, m_dt_bias, m_a_log, m_d_skip, m_ssd_norm_w, m_q_norm_w, m_w_uq, m_kv_norm_w, m_w_ukv, m_mla_norm_w, m_w_out, m_norm_ffn2, m_ffn2_w_gate, m_ffn2_w_up, m_ffn2_w_down, m_norm_final, v_w_ada, v_b_ada, v_norm_ffn1, v_ffn1_w_gate, v_ffn1_w_up, v_ffn1_w_down, v_norm_mix, v_w_in, v_conv_w, v_conv_b, v_dt_bias, v_a_log, v_d_skip, v_ssd_norm_w, v_q_norm_w, v_w_uq, v_kv_norm_w, v_w_ukv, v_mla_norm_w, v_w_out, v_norm_ffn2, v_ffn2_w_gate, v_ffn2_w_up, v_ffn2_w_down, v_norm_final):
    a = dict(locals())
    held_transposed = ("ffn1_w_gate", "ffn1_w_up", "ffn2_w_gate", "ffn2_w_up", "w_in")
    for n in held_transposed:
        for p in ("", "m_", "v_"):
            a[p + n] = a[p + n].transpose(0, 2, 1)
    B, S, D = x.shape
    mx, my, mc = _place()
    chip = 2 * mx + my
    dev = 2 * chip + mc
    core = mc.astype(jnp.int32).reshape(1)
    chip_id = chip.astype(jnp.int32).reshape(1)

    cw_rows = jnp.pad(conv_w[0], ((0, 0), (0, D - conv_w.shape[2])))
    got, _ = _all_gather_small(jnp.concatenate([c, cw_rows, jnp.zeros((8 - B - CONV_WIDTH, D), F32)], axis=0), "gather_c")
    c_all = got[:, :B, :].reshape(N_DEV * B, D)
    conv_full = got[::2, B:B + CONV_WIDTH, :conv_w.shape[2]].transpose(1, 0, 2).reshape(CONV_WIDTH, D_CONV)

    b_cols = lax.dynamic_slice(b_ada, (0, chip * ADA_COLS), (1, ADA_COLS))
    mod_all, mod_done = _all_gather_small(_ada_fwd(c_all, w_ada[0], b_cols), "gather_mod")
    mod = lax.dynamic_slice(mod_all, (0, B * dev, 0), (N_DEV, B, ADA_COLS))[::2].transpose(1, 0, 2).reshape(B, N_MOD * D)

    first = ("ffn1_w_gate", "ffn1_w_up", "ffn1_w_down")
    later = tuple(n for n in BIG_NAMES if n not in first)
    got_first, gathered = _gather_weights([(a[n][0] + mod_done[0, 0]).astype(BF16) for n in first])
    w = dict(zip(first, got_first))
    in_flight = _gather_start([(a[n][0] + gathered[0, 0]).astype(BF16) for n in later])

    def later_weights(after):
        send_sems, recv_sems, shards, lands, _ = in_flight
        lands = _gather_wait(send_sems, recv_sems, shards, lands, after)
        wl = dict(zip(later, _gather_finish([a[n][0].astype(BF16) for n in later], lands)))
        for n, to_kernel in _TO_KERNEL.items():
            wl[n] = to_kernel(wl[n].reshape(-1, D) if n in held_transposed else _columns_joined(wl[n]))
        wl["w_out"] = wl["w_out"].reshape(D_SSD + D_MLA, D)
        return wl

    small = {n: a[n].reshape(1, -1) for n in SMALL_NAMES if n not in ("conv_w", "norm_final")}
    small["conv_w"], small["norm_final"] = conv_full, norm_final

    def shards_of(names, gw):
        g4 = []
        for n in names:
            g = gw[n]
            if n in _FROM_KERNEL:
                g = _FROM_KERNEL[n](g) if n in held_transposed else _columns_split(_FROM_KERNEL[n](g))
            g4.append(g.reshape(N_CHIPS, a[n].shape[1], a[n].shape[2]))
        return g4

    def scatter_group(names, g4, swapped):
        pair = [_pair_sum(g, got, core, "pair_sum_" + n) for n, g, got in zip(names, g4, swapped)]
        return (names,) + tuple(_scatter_start(pair, names[0]))

    grads, deltas, new_m, new_v = {}, {}, {}, {}

    def finish_groups(some, after):
        names, mine = [], []
        for group_names, send_sems, recv_sems, pair, lands, _ in some:
            pair, lands = _scatter_wait(send_sems, recv_sems, pair, lands, after, group_names[0])
            names += group_names
            mine += [_chip_sum(own, got, chip_id, "chip_sum_" + n) for n, own, got in zip(group_names, pair, lands)]
        for n, own, other in zip(names, mine, _join_halves(mine, "join_" + names[0])):
            grads[n], deltas[n], new_m[n], new_v[n] = _adam_halves(a[n], a["m_" + n], a["v_" + n], own, other, core, "adam_" + n)
        return deltas[names[-1]]

    swapping, groups = [], []

    def on_grads(names, gw):
        send_sems, recv_sems, g4, lands, token = _swap_start(shards_of(names, gw), names[0])
        swapping.append((names, send_sems, recv_sems, g4, lands))
        return token[0, 0]

    def sync(after):
        token = 0.0
        while swapping:
            names, send_sems, recv_sems, g4, lands = swapping.pop(0)
            g4, swapped = _swap_wait(send_sems, recv_sems, g4, lands, after, names[0])
            groups.append(scatter_group(names, g4, swapped))
            token = groups[-1][5][0, 0]
        return token

    loss_blk, grad_x, gw, dmod, gs = _local_step(x, positions, mod + in_flight[4][0, 0], w, later_weights, small, loss_target,
                                                 on_grads, sync)

    small_flat = _pack_small(gs).at[-1].set(loss_blk[0, 0])
    send = jnp.concatenate([dmod.reshape(MOD_ROWS, D), small_flat.reshape(SMALL_ROWS, D),
                            jnp.zeros((SEND_ROWS - MOD_ROWS - SMALL_ROWS, D), F32)], axis=0)
    got, _ = _all_gather_small(send, "gather_small")
    summed = _small_sum(got)
    sums = summed[N_MOD:N_MOD + SMALL_ROWS].reshape(-1)
    loss = sums[-1]
    gsmall = _unpack_small(sums)
    gsmall["conv_w"] = lax.dynamic_slice(gsmall["conv_w"].reshape(CONV_WIDTH, D_CONV), (0, chip * conv_w.shape[2]),
                                         (CONV_WIDTH, conv_w.shape[2]))
    gsmall["b_ada"] = summed[:N_MOD]
    names = ("b_ada",) + SMALL_NAMES
    rows = 208

    def pack(parts):
        flat = jnp.concatenate([parts[n].reshape(-1) for n in names])
        return jnp.pad(flat, (0, rows * LANES - flat.shape[0])).reshape(rows, LANES)

    packed = [pack({n: a[p + n] for n in names}) for p in ("", "m_", "v_")]
    g_p = pack(gsmall)
    outs = (g_p,) + tuple(_adam(packed[0], g_p, packed[1], packed[2], "adam_small"))
    for dst, flat in zip((grads, deltas, new_m, new_v), outs):
        flat, off = flat.reshape(-1), 0
        for n in names:
            dst[n] = flat[off:off + a[n].size].reshape(a[n].shape)
            off += a[n].size

    dmod_all = got[:, :MOD_ROWS, :].reshape(N_DEV * B, N_MOD * D)
    dmod_cols = lax.dynamic_slice(dmod_all, (0, chip * ADA_COLS), (N_DEV * B, ADA_COLS))
    ada = _ada_bwd(c_all, dmod_cols, w_ada[0], m_w_ada[0], v_w_ada[0])
    for dst, t in zip((grads, deltas, new_m, new_v), ada):
        dst["w_ada"] = t[None]

    g4 = shards_of(first, gw)
    last = scatter_group(first, g4, _swap_halves(g4, summed, "swap_" + first[0]))
    finish_groups([last], finish_groups(groups, last[5]))
    for dst in (grads, deltas, new_m, new_v):
        for n in held_transposed:
            dst[n] = dst[n].transpose(0, 2, 1)

    order = ("w_ada", "b_ada", "norm_ffn1", "ffn1_w_gate", "ffn1_w_up", "ffn1_w_down", "norm_mix", "w_in", "conv_w", "conv_b",
             "dt_bias", "a_log", "d_skip", "ssd_norm_w", "q_norm_w", "w_uq", "kv_norm_w", "w_ukv", "mla_norm_w", "w_out",
             "norm_ffn2", "ffn2_w_gate", "ffn2_w_up", "ffn2_w_down", "norm_final")
    return (loss, grad_x, *[grads[n] for n in order], *[deltas[n] for n in order], *[new_m[n] for n in order],
            *[new_v[n] for n in order])
```

```python
import functools
import math

import jax
import jax.numpy as jnp
import numpy as np
from jax import lax
from jax.experimental import pallas as pl
from jax.experimental.pallas import tpu as pltpu

F32 = jnp.float32
BF16 = jnp.bfloat16

D_MODEL = 1024
D_FF = 2816
D_SSD = 1024
D_MLA = 1024
SSD_HEADS = 16
SSD_HEAD_DIM = 64
SSD_GROUPS = 2
SSD_STATE = 128
CONV_WIDTH = 4
CHUNK = 128
MLA_HEADS = 8
QK_NOPE = 64
QK_ROPE = 32
QK_DIM = QK_NOPE + QK_ROPE
V_HEAD = 128
Q_LORA = 384
KV_LORA = 256
ROPE_THETA = 10000.0
N_MOD = 9
EPS = 1e-6
D_CONV = D_SSD + 2 * SSD_GROUPS * SSD_STATE
D_PROJ = 3328
HEAD_LANES = 128
ADAM_LR = 0.001
ADAM_B1 = 0.9
ADAM_B2 = 0.999
ADAM_EPS = 1e-08
ADAM_WD = 0.01
ADAM_STEP = 10

LANES = 128
VMEM_LIMIT = 56 * 1024 * 1024
TOKEN_TILE = 512
WIDE_TOKEN_TILE = 1024
ATTN_FWD_Q_TILE = 1024
ATTN_FWD_KV_TILE = 1024
ATTN_BWD_TILE = 1024
N_CHIPS = 4
N_DEV = 8

MESH = pl.DeviceIdType.MESH


def _dot(a, b):
    return jnp.dot(a, b, preferred_element_type=F32)


def _dot_nt(a, b):
    return lax.dot_general(a, b, (((1,), (1,)), ((), ())), preferred_element_type=F32)


def _dot_tn(a, b):
    return lax.dot_general(a, b, (((0,), (0,)), ((), ())), preferred_element_type=F32)


def _cparams(semantics):
    return pltpu.CompilerParams(dimension_semantics=semantics, vmem_limit_bytes=VMEM_LIMIT)


def _resident(shape):
    zeros = (0,) * len(shape)
    return pl.BlockSpec(shape, lambda *_: zeros, pipeline_mode=pl.Buffered(1))


def _sigmoid(x):
    return jax.nn.sigmoid(x)


def _rms_stats(x):
    r = lax.rsqrt(jnp.mean(x * x, axis=-1, keepdims=True) + EPS)
    return x * r, r


def _rms_bwd(dn, xh, r, w):
    dxh = dn * w
    dx = r * (dxh - xh * jnp.mean(dxh * xh, axis=-1, keepdims=True))
    return dx, dn * xh


def _colsum(v):
    return jnp.sum(v, axis=0, keepdims=True)


def _ffn_fwd(x, nw, sh, sc, g, wg, wu, wd, seq, name, head=None):
    T, D = x.shape
    fs = wg.shape[1]
    tm = min(TOKEN_TILE, seq)
    tps = seq // tm

    def body(x_ref, nw_ref, sh_ref, sc_ref, g_ref, wg_ref, wu_ref, wd_ref, *rest):
        if head is None:
            xo_ref, a_ref, u_ref, f_ref = rest
        else:
            nf_ref, t_ref, xo_ref, a_ref, u_ref, f_ref, loss_ref, dnf_ref = rest

            @pl.when(pl.program_id(0) == 0)
            def _():
                loss_ref[...] = jnp.zeros_like(loss_ref)
                dnf_ref[...] = jnp.zeros_like(dnf_ref)

        xv = x_ref[...]
        xh, _ = _rms_stats(xv)
        h = (xh * nw_ref[...]) * (1.0 + sc_ref[0]) + sh_ref[0]
        hb = h.astype(BF16)
        f = jnp.zeros((tm, D), F32)
        for j in range(N_CHIPS):
            a = _dot_nt(hb, wg_ref[j])
            u = _dot_nt(hb, wu_ref[j])
            a_ref[j] = a.astype(BF16)
            u_ref[j] = u.astype(BF16)
            f = f + _dot((a * _sigmoid(a) * u).astype(BF16), wd_ref[j])
        f_ref[...] = f.astype(BF16)
        xo = xv + 0.5 * g_ref[0] * f
        if head is None:
            xo_ref[...] = xo
        else:
            xh, r = _rms_stats(xo)
            nfv = nf_ref[...]
            err = xh * nfv - t_ref[...]
            loss_ref[...] += (0.5 / D) * jnp.sum(err * err)
            dxo, dw_rows = _rms_bwd(err * (1.0 / D), xh, r, nfv)
            xo_ref[...] = dxo
            dnf_ref[...] += _colsum(dw_rows)

    rows = lambda n: pl.BlockSpec((tm, n), lambda i: (i, 0))
    act = pl.BlockSpec((N_CHIPS, tm, fs), lambda i: (0, i, 0))
    perb = pl.BlockSpec((1, 1, D), lambda i: (i // tps, 0, 0))
    sd = jax.ShapeDtypeStruct
    in_specs = [rows(D), _resident((1, D)), perb, perb, perb, _resident((N_CHIPS, fs, D)), _resident((N_CHIPS, fs, D)),
                _resident((N_CHIPS, fs, D))]
    out_specs = [rows(D), act, act, rows(D)]
    out_shape = [sd((T, D), F32), sd((N_CHIPS, T, fs), BF16), sd((N_CHIPS, T, fs), BF16), sd((T, D), BF16)]
    if head is not None:
        in_specs += [_resident((1, D)), rows(D)]
        out_specs += [pl.BlockSpec((8, LANES), lambda i: (0, 0)), pl.BlockSpec((1, D), lambda i: (0, 0))]
        out_shape += [sd((8, LANES), F32), sd((1, D), F32)]
    return pl.pallas_call(
        body, grid=(T // tm,), name=name, in_specs=in_specs, out_specs=out_specs, out_shape=out_shape,
        compiler_params=_cparams(("arbitrary",)),
    )(x, nw, sh, sc, g, wg, wu, wd, *(head or ()))


def _ffn_bwd(dxo, x, nw, sh, sc, g, a, u, f, wg, wu, wd, seq, name):
    T, D = x.shape
    fs = wg.shape[1]
    B = T // seq
    tm = min(TOKEN_TILE // 2, seq)
    tps = seq // tm

    def body(dxo_ref, x_ref, nw_ref, sh_ref, sc_ref, g_ref, a_ref, u_ref, f_ref, wg_ref, wu_ref, wd_ref,
             dx_ref, h_ref, s_ref, df_ref, da_ref, du_ref, dsh_ref, dsc_ref, dg_ref, dnw_ref):
        i = pl.program_id(0)

        @pl.when(i % tps == 0)
        def _():
            dsh_ref[...] = jnp.zeros_like(dsh_ref)
            dsc_ref[...] = jnp.zeros_like(dsc_ref)
            dg_ref[...] = jnp.zeros_like(dg_ref)

        @pl.when(i == 0)
        def _():
            dnw_ref[...] = jnp.zeros_like(dnw_ref)

        dxo_v = dxo_ref[...]
        dfb = (0.5 * g_ref[0] * dxo_v).astype(BF16)
        dg_ref[0] += _colsum(0.5 * dxo_v * f_ref[...].astype(F32))
        dh = jnp.zeros((tm, D), F32)
        for j in range(N_CHIPS):
            ds = _dot_nt(dfb, wd_ref[j])
            av = a_ref[j].astype(F32)
            uv = u_ref[j].astype(F32)
            sig = _sigmoid(av)
            sil = av * sig
            dab = (ds * uv * (sig * (1.0 + av * (1.0 - sig)))).astype(BF16)
            dub = (ds * sil).astype(BF16)
            dh = dh + _dot(dab, wg_ref[j]) + _dot(dub, wu_ref[j])
            s_ref[j] = (sil * uv).astype(BF16)
            da_ref[j] = dab
            du_ref[j] = dub
        xv = x_ref[...]
        xh, r = _rms_stats(xv)
        nwv = nw_ref[...]
        n = xh * nwv
        scale1 = 1.0 + sc_ref[0]
        dsc_ref[0] += _colsum(dh * n)
        dsh_ref[0] += _colsum(dh)
        dx, dw_rows = _rms_bwd(dh * scale1, xh, r, nwv)
        dnw_ref[...] += _colsum(dw_rows)
        dx_ref[...] = dxo_v + dx
        h_ref[...] = (n * scale1 + sh_ref[0]).astype(BF16)
        df_ref[...] = dfb

    rows = lambda n: pl.BlockSpec((tm, n), lambda i: (i, 0))
    act = pl.BlockSpec((N_CHIPS, tm, fs), lambda i: (0, i, 0))
    perb = pl.BlockSpec((1, 1, D), lambda i: (i // tps, 0, 0))
    sd = jax.ShapeDtypeStruct
    return pl.pallas_call(
        body, grid=(T // tm,), name=name,
        in_specs=[rows(D), rows(D), _resident((1, D)), perb, perb, perb, act, act, rows(D),
                  _resident((N_CHIPS, fs, D)), _resident((N_CHIPS, fs, D)), _resident((N_CHIPS, fs, D))],
        out_specs=[rows(D), rows(D), act, rows(D), act, act, perb, perb, perb, pl.BlockSpec((1, D), lambda i: (0, 0))],
        out_shape=[sd((T, D), F32), sd((T, D), BF16), sd((N_CHIPS, T, fs), BF16), sd((T, D), BF16),
                   sd((N_CHIPS, T, fs), BF16), sd((N_CHIPS, T, fs), BF16), sd((B, 1, D), F32), sd((B, 1, D), F32),
                   sd((B, 1, D), F32), sd((1, D), F32)],
        compiler_params=_cparams(("arbitrary",)),
    )(dxo, x, nw, sh, sc, g, a, u, f, wg, wu, wd)


def _ffn_wgrad(h, s, df, da, du, after, name):
    T, D = h.shape
    fs = s.shape[2]
    tt = min(TOKEN_TILE, T)
    nt = T // tt

    def body(h_ref, s_ref, df_ref, da_ref, du_ref, after_ref, dgate_ref, dup_ref, ddown_ref, gate_acc, up_acc, down_acc):
        @pl.when(pl.program_id(1) == 0)
        def _():
            gate_acc[...] = jnp.zeros_like(gate_acc)
            up_acc[...] = jnp.zeros_like(up_acc)
            down_acc[...] = jnp.zeros_like(down_acc)

        hv = h_ref[...]
        gate_acc[...] += _dot_tn(da_ref[0], hv)
        up_acc[...] += _dot_tn(du_ref[0], hv)
        down_acc[...] += _dot_tn(s_ref[0], df_ref[...])

        @pl.when(pl.program_id(1) == nt - 1)
        def _():
            dgate_ref[0] = gate_acc[...].astype(BF16)
            dup_ref[0] = up_acc[...].astype(BF16)
            ddown_ref[0] = down_acc[...].astype(BF16)

    rows = pl.BlockSpec((tt, D), lambda j, t: (t, 0))
    act = pl.BlockSpec((1, tt, fs), lambda j, t: (j, t, 0))
    shard = pl.BlockSpec((1, fs, D), lambda j, t: (j, 0, 0))
    return pl.pallas_call(
        body, grid=(N_CHIPS, nt), name=name,
        in_specs=[rows, act, rows, act, act, pl.BlockSpec(memory_space=pl.ANY)],
        out_specs=[shard] * 3, out_shape=[jax.ShapeDtypeStruct((N_CHIPS, fs, D), BF16)] * 3,
        scratch_shapes=[pltpu.VMEM((fs, D), F32)] * 3,
        compiler_params=_cparams(("arbitrary", "arbitrary")),
    )(h, s, df, da, du, after)


def _mm_tn(xa, ya, tn, name):
    T, K = xa.shape
    N = ya.shape[1]
    tt = min(WIDE_TOKEN_TILE, T)
    nt = T // tt

    def body(x_ref, y_ref, o_ref, acc_ref):
        @pl.when(pl.program_id(1) == 0)
        def _():
            acc_ref[...] = jnp.zeros_like(acc_ref)

        acc_ref[...] += _dot_tn(x_ref[...], y_ref[...])

        @pl.when(pl.program_id(1) == nt - 1)
        def _():
            o_ref[...] = acc_ref[...].astype(BF16)

    return pl.pallas_call(
        body, grid=(N // tn, nt), name=name,
        in_specs=[pl.BlockSpec((tt, K), lambda j, t: (t, 0)), pl.BlockSpec((tt, tn), lambda j, t: (t, j))],
        out_specs=pl.BlockSpec((K, tn), lambda j, t: (0, j)),
        out_shape=jax.ShapeDtypeStruct((K, N), BF16),
        scratch_shapes=[pltpu.VMEM((K, tn), F32)],
        compiler_params=_cparams(("arbitrary", "arbitrary")),
    )(xa, ya)


_PROJ_SPLITS = (0, 1024, 2560, 2944, 3200, 3328)


def _inproj_fwd(x, nw, sh, sc, win, seq):
    T, D = x.shape
    tm = min(WIDE_TOKEN_TILE, seq)
    tps = seq // tm
    widths = [b - a for a, b in zip(_PROJ_SPLITS[:-1], _PROJ_SPLITS[1:])]
    dtypes = [BF16, BF16, F32, F32, F32]

    def body(x_ref, nw_ref, sh_ref, sc_ref, w_ref, *outs):
        xh, _ = _rms_stats(x_ref[...])
        h = (xh * nw_ref[...]) * (1.0 + sc_ref[0]) + sh_ref[0]
        proj = _dot_nt(h.astype(BF16), w_ref[...])
        for o, lo, hi in zip(outs, _PROJ_SPLITS[:-1], _PROJ_SPLITS[1:]):
            o[...] = proj[:, lo:hi].astype(o.dtype)

    rows = lambda n: pl.BlockSpec((tm, n), lambda i: (i, 0))
    perb = pl.BlockSpec((1, 1, D), lambda i: (i // tps, 0, 0))
    return pl.pallas_call(
        body, grid=(T // tm,), name="inproj_fwd",
        in_specs=[rows(D), _resident((1, D)), perb, perb, _resident((D_PROJ, D))],
        out_specs=[rows(w) for w in widths],
        out_shape=[jax.ShapeDtypeStruct((T, w), dt) for w, dt in zip(widths, dtypes)],
        compiler_params=_cparams(("arbitrary",)),
    )(x, nw, sh, sc, win)


def _inproj_bwd(dx2, x, nw, sh, sc, win, dz, dxbc, dcq, dckv, ddtk_a, ddtk_b, seq):
    T, D = x.shape
    B = T // seq
    tm = min(TOKEN_TILE, seq)
    tps = seq // tm

    def body(dx2_ref, x_ref, nw_ref, sh_ref, sc_ref, w_ref, dz_ref, dxbc_ref, dcq_ref, dckv_ref, da_ref, db_ref,
             dx_ref, h_ref, dp_ref, dsh_ref, dsc_ref, dnw_ref):
        i = pl.program_id(0)

        @pl.when(i % tps == 0)
        def _():
            dsh_ref[...] = jnp.zeros_like(dsh_ref)
            dsc_ref[...] = jnp.zeros_like(dsc_ref)

        @pl.when(i == 0)
        def _():
            dnw_ref[...] = jnp.zeros_like(dnw_ref)

        dproj = jnp.concatenate(
            [dz_ref[...], dxbc_ref[...], dcq_ref[...].astype(BF16), dckv_ref[...].astype(BF16),
             (da_ref[...] + db_ref[...]).astype(BF16)], axis=1)
        dp_ref[...] = dproj
        dh = _dot(dproj, w_ref[...])
        xh, r = _rms_stats(x_ref[...])
        nwv = nw_ref[...]
        n = xh * nwv
        scale1 = 1.0 + sc_ref[0]
        dsc_ref[0] += _colsum(dh * n)
        dsh_ref[0] += _colsum(dh)
        dx, dw_rows = _rms_bwd(dh * scale1, xh, r, nwv)
        dnw_ref[...] += _colsum(dw_rows)
        dx_ref[...] = dx2_ref[...] + dx
        h_ref[...] = (n * scale1 + sh_ref[0]).astype(BF16)

    rows = lambda n: pl.BlockSpec((tm, n), lambda i: (i, 0))
    perb = pl.BlockSpec((1, 1, D), lambda i: (i // tps, 0, 0))
    sd = jax.ShapeDtypeStruct
    return pl.pallas_call(
        body, grid=(T // tm,), name="inproj_bwd",
        in_specs=[rows(D), rows(D), _resident((1, D)), perb, perb, _resident((D_PROJ, D)),
                  rows(1024), rows(D_CONV), rows(Q_LORA), rows(KV_LORA), rows(LANES), rows(LANES)],
        out_specs=[rows(D), rows(D), rows(D_PROJ), perb, perb, pl.BlockSpec((1, D), lambda i: (0, 0))],
        out_shape=[sd((T, D), F32), sd((T, D), BF16), sd((T, D_PROJ), BF16), sd((B, 1, D), F32), sd((B, 1, D), F32),
                   sd((1, D), F32)],
        compiler_params=_cparams(("arbitrary",)),
    )(dx2, x, nw, sh, sc, win, dz, dxbc, dcq, dckv, ddtk_a, ddtk_b)


SUBLANES = 8


def _shift_down(v, k):
    r = pltpu.roll(v, k, 0)
    row = lax.broadcasted_iota(jnp.int32, (SUBLANES, v.shape[1]), 0)
    return jnp.concatenate([jnp.where(row < k, 0.0, r[:SUBLANES]), r[SUBLANES:]], axis=0)


def _shift_up(v, k):
    n = v.shape[0]
    r = pltpu.roll(v, n - k, 0)
    row = lax.broadcasted_iota(jnp.int32, (SUBLANES, v.shape[1]), 0)
    return jnp.concatenate([r[:n - SUBLANES], jnp.where(row >= SUBLANES - k, 0.0, r[n - SUBLANES:])], axis=0)


def _conv_pre(xv, w_ref, b_ref):
    pre = b_ref[...] + w_ref[CONV_WIDTH - 1:CONV_WIDTH, :] * xv
    for k in range(1, CONV_WIDTH):
        pre = pre + w_ref[CONV_WIDTH - 1 - k:CONV_WIDTH - k, :] * _shift_down(xv, k)
    return pre


def _conv_fwd(xraw, cw, cb):
    B, S, C = xraw.shape

    def body(x_ref, w_ref, b_ref, o_ref):
        pre = _conv_pre(x_ref[0].astype(F32), w_ref, b_ref)
        o_ref[0] = (pre * _sigmoid(pre)).astype(BF16)

    blk = pl.BlockSpec((1, S, LANES), lambda b, j: (b, 0, j))
    return pl.pallas_call(
        body, grid=(B, C // LANES), name="conv_fwd",
        in_specs=[blk, pl.BlockSpec((CONV_WIDTH, LANES), lambda b, j: (0, j)), pl.BlockSpec((1, LANES), lambda b, j: (0, j))],
        out_specs=blk, out_shape=jax.ShapeDtypeStruct((B, S, C), BF16),
        compiler_params=_cparams(("arbitrary", "arbitrary")),
    )(xraw, cw, cb)


def _conv_bwd(dout, xraw, cw, cb):
    B, S, C = xraw.shape

    def body(d_ref, x_ref, w_ref, b_ref, dx_ref, dw_ref, db_ref):
        @pl.when(pl.program_id(1) == 0)
        def _():
            dw_ref[...] = jnp.zeros_like(dw_ref)
            db_ref[...] = jnp.zeros_like(db_ref)

        xv = x_ref[0].astype(F32)
        pre = _conv_pre(xv, w_ref, b_ref)
        sig = _sigmoid(pre)
        dpre = d_ref[0].astype(F32) * (sig * (1.0 + pre * (1.0 - sig)))
        dx = w_ref[CONV_WIDTH - 1:CONV_WIDTH, :] * dpre
        for k in range(1, CONV_WIDTH):
            dx = dx + w_ref[CONV_WIDTH - 1 - k:CONV_WIDTH - k, :] * _shift_up(dpre, k)
        dx_ref[0] = dx.astype(BF16)
        db_ref[...] += _colsum(dpre)
        dws = [_colsum(dpre * (xv if k == 0 else _shift_down(xv, k))) for k in range(CONV_WIDTH - 1, -1, -1)]
        dw_ref[...] += jnp.concatenate(dws, axis=0)

    blk = pl.BlockSpec((1, S, LANES), lambda j, b: (b, 0, j))
    wspec = pl.BlockSpec((CONV_WIDTH, LANES), lambda j, b: (0, j))
    bspec = pl.BlockSpec((1, LANES), lambda j, b: (0, j))
    return pl.pallas_call(
        body, grid=(C // LANES, B), name="conv_bwd",
        in_specs=[blk, blk, wspec, bspec], out_specs=[blk, wspec, bspec],
        out_shape=[jax.ShapeDtypeStruct((B, S, C), BF16), jax.ShapeDtypeStruct((CONV_WIDTH, C), F32),
                   jax.ShapeDtypeStruct((1, C), F32)],
        compiler_params=_cparams(("arbitrary", "arbitrary")),
    )(dout, xraw, cw, cb)


def _softplus(x):
    return jnp.maximum(x, 0.0) + jnp.log(1.0 + jnp.exp(-jnp.abs(x)))


def _ssd_common(xbc_ref, dtk_ref, dtb_ref, alog_ref, e_ref):
    L = CHUNK
    xbc = xbc_ref[0]
    xs = xbc[:, :D_SSD].astype(F32)
    bm = xbc[:, D_SSD:D_SSD + 256]
    cm = xbc[:, D_SSD + 256:D_SSD + 512]
    head = lax.broadcasted_iota(jnp.int32, (1, LANES), 1) < SSD_HEADS
    a128 = jnp.where(head, -jnp.exp(alog_ref[...]), 0.0)
    pre = dtk_ref[0] + dtb_ref[...]
    dt = _softplus(pre)
    dA = dt * a128
    row = lax.broadcasted_iota(jnp.int32, (L, L), 0)
    col = lax.broadcasted_iota(jnp.int32, (L, L), 1)
    causal = col <= row
    tri = causal.astype(F32)
    triT = (row <= col).astype(F32)
    tri = causal.astype(BF16)
    triT = (row <= col).astype(BF16)
    dA3 = _split3(dA)
    acum = _sum3(lambda part: _dot(tri, part), dA3)
    acumT = _sum3(lambda part: _dot_tn(part, triT), dA3)
    E = e_ref[...]
    acum_f = _spread(acum, E)
    dt_f = _spread(dt, E)
    e_f = jnp.exp(acum_f)
    w_f = jnp.exp(acum_f[L - 1:L, :] - acum_f)
    xt = xs * dt_f
    return dict(xs=xs, bm=bm, cm=cm, a128=a128, pre=pre, dt=dt, causal=causal, tri=tri, triT=triT, acum=acum,
                acumT=acumT, E=E, dt_f=dt_f, e_f=e_f, w_f=w_f, xt=xt, head=head)


def _split3(x):
    p1 = x.astype(BF16)
    r1 = x - p1.astype(F32)
    p2 = r1.astype(BF16)
    return p1, p2, (r1 - p2.astype(F32)).astype(BF16)


def _sum3(mm, parts):
    return (mm(parts[0]) + mm(parts[1])) + mm(parts[2])


def _spread(v, e):
    return _sum3(lambda part: _dot(part, e), _split3(v))


def _gather_heads(v, e):
    return _sum3(lambda part: _dot_nt(part, e), _split3(v))


def _head_mask(k):
    lane = lax.broadcasted_iota(jnp.int32, (CHUNK, LANES), 1)
    return (lane >= SSD_HEAD_DIM) if k == 1 else (lane < SSD_HEAD_DIM)


def _pair_decay(alast, h0):
    row = lax.broadcasted_iota(jnp.int32, (2 * SSD_HEAD_DIM, SSD_STATE), 0)
    return jnp.exp(jnp.where(row < SSD_HEAD_DIM, alast[:, h0:h0 + 1], alast[:, h0 + 1:h0 + 2]))


def _decay_matrix(q, h):
    seg = q["acum"][:, h:h + 1] - q["acumT"][h:h + 1, :]
    return jnp.exp(jnp.where(q["causal"], seg, -1e30))


def _gated_norm(y, zz, nw):
    sig = _sigmoid(zz)
    sil = zz * sig
    yg = y * sil
    half = D_SSD // SSD_GROUPS
    parts = []
    for g in range(SSD_GROUPS):
        xh, r = _rms_stats(yg[:, g * half:(g + 1) * half])
        parts.append((xh, r))
    return sig, sil, parts


def _ssd_fwd(xbc, dtk, z, dtb, alog, dsk, nw, expand):
    B, S, _ = xbc.shape
    L = CHUNK
    nc = S // L

    def body(xbc_ref, dtk_ref, z_ref, dtb_ref, alog_ref, dsk_ref, nw_ref, e_ref, y_ref, ys_ref, prev_ref, st_ref):
        @pl.when(pl.program_id(0) == 0)
        def _():
            st_ref[...] = jnp.zeros_like(st_ref)

        for b in range(B):
            one = lambda ref: ref.at[pl.ds(b, 1)]
            sequence_step(one(xbc_ref), one(dtk_ref), one(z_ref), dtb_ref, alog_ref, dsk_ref, nw_ref, e_ref, one(y_ref),
                          one(ys_ref), one(prev_ref), st_ref.at[b])

    def sequence_step(xbc_ref, dtk_ref, z_ref, dtb_ref, alog_ref, dsk_ref, nw_ref, e_ref, y_ref, ys_ref, prev_ref, st_ref):
        q = _ssd_common(xbc_ref, dtk_ref, dtb_ref, alog_ref, e_ref)
        xtb = q["xt"].astype(BF16)
        xwb = (q["xt"] * q["w_f"]).astype(BF16)
        alast = q["acum"][L - 1:L, :]
        ys = []
        for g in range(SSD_GROUPS):
            bg = q["bm"][:, g * 128:(g + 1) * 128]
            cg = q["cm"][:, g * 128:(g + 1) * 128]
            G = _dot_nt(cg, bg)
            for pr in range(SSD_HEADS // SSD_GROUPS // 2):
                h0 = g * 8 + 2 * pr
                lo = h0 * SSD_HEAD_DIM
                xt_p = xtb[:, lo:lo + 128]
                ydiag = jnp.zeros((L, LANES), F32)
                for k in range(2):
                    M = (G * _decay_matrix(q, h0 + k)).astype(BF16)
                    ydiag = ydiag + _dot(M, jnp.where(_head_mask(k), xt_p, jnp.zeros_like(xt_p)))
                hp = st_ref[lo:lo + 128, :]
                prev_ref[0, 0, lo:lo + 128, :] = hp.astype(BF16)
                zoff = _dot_nt(cg, hp.astype(BF16))
                ys.append(ydiag + zoff * q["e_f"][:, lo:lo + 128])
                st_ref[lo:lo + 128, :] = _pair_decay(alast, h0) * hp + _dot_tn(xwb[:, lo:lo + 128], bg)
        y = jnp.concatenate(ys, axis=1) + dsk_ref[...] * q["xs"]
        y_ref[0] = y.astype(BF16)
        _, _, parts = _gated_norm(y, z_ref[0].astype(F32), nw_ref[...])
        half = D_SSD // SSD_GROUPS
        ys_ref[0] = jnp.concatenate(
            [xh * nw_ref[:, g * half:(g + 1) * half] for g, (xh, _) in enumerate(parts)], axis=1).astype(BF16)

    chunk = lambda n: pl.BlockSpec((B, L, n), lambda c: (0, c, 0))
    vec = pl.BlockSpec((1, LANES), lambda c: (0, 0))
    return pl.pallas_call(
        body, grid=(nc,), name="ssd_fwd",
        in_specs=[chunk(D_CONV), chunk(LANES), chunk(D_SSD), vec, vec, pl.BlockSpec((1, D_SSD), lambda c: (0, 0)),
                  pl.BlockSpec((1, D_SSD), lambda c: (0, 0)), pl.BlockSpec((LANES, D_SSD), lambda c: (0, 0))],
        out_specs=[chunk(D_SSD), chunk(D_SSD), pl.BlockSpec((B, 1, D_SSD, SSD_STATE), lambda c: (0, c, 0, 0))],
        out_shape=[jax.ShapeDtypeStruct((B, S, D_SSD), BF16), jax.ShapeDtypeStruct((B, S, D_SSD), BF16),
                   jax.ShapeDtypeStruct((B, nc, D_SSD, SSD_STATE), BF16)],
        scratch_shapes=[pltpu.VMEM((B, D_SSD, SSD_STATE), F32)],
        compiler_params=_cparams(("arbitrary",)),
    )(xbc, dtk, z, dtb, alog, dsk, nw, expand)


def _ssd_bwd(xbc, dtk, z, y, prev, dys, dtb, alog, dsk, nw, expand):
    B, S, _ = xbc.shape
    L = CHUNK
    nc = S // L
    half = D_SSD // SSD_GROUPS

    def body(xbc_ref, dtk_ref, z_ref, y_ref, prev_ref, dys_ref, dtb_ref, alog_ref, dsk_ref, nw_ref, e_ref,
             dxbc_ref, ddtk_ref, dz_ref, dnw_ref, dvec_ref, dh_ref, dskc_ref):
        @pl.when(pl.program_id(0) == 0)
        def _():
            dnw_ref[...] = jnp.zeros_like(dnw_ref)
            dvec_ref[...] = jnp.zeros_like(dvec_ref)
            dskc_ref[...] = jnp.zeros_like(dskc_ref)
            dh_ref[...] = jnp.zeros_like(dh_ref)

        for b in range(B):
            one = lambda ref: ref.at[pl.ds(b, 1)]
            sequence_step(one(xbc_ref), one(dtk_ref), one(z_ref), one(y_ref), one(prev_ref), one(dys_ref), dtb_ref, alog_ref,
                          dsk_ref, nw_ref, e_ref, one(dxbc_ref), one(ddtk_ref), one(dz_ref), dnw_ref, dvec_ref, dh_ref.at[b],
                          dskc_ref)

        @pl.when(pl.program_id(0) == nc - 1)
        def _():
            dvec_ref[2:3, :] = _gather_heads(jnp.broadcast_to(dskc_ref[...], (8, D_SSD)), e_ref[...])[0:1, :]

    def sequence_step(xbc_ref, dtk_ref, z_ref, y_ref, prev_ref, dys_ref, dtb_ref, alog_ref, dsk_ref, nw_ref, e_ref,
                      dxbc_ref, ddtk_ref, dz_ref, dnw_ref, dvec_ref, dh_ref, dskc_ref):
        q = _ssd_common(xbc_ref, dtk_ref, dtb_ref, alog_ref, e_ref)
        E = q["E"]
        xs = q["xs"]
        yv = y_ref[0].astype(F32)
        zz = z_ref[0].astype(F32)
        sig, sil, parts = _gated_norm(yv, zz, nw_ref[...])
        dn = dys_ref[0].astype(F32)
        dyg, dnw_rows = [], []
        for g, (xh, r) in enumerate(parts):
            dpart, dw_rows = _rms_bwd(dn[:, g * half:(g + 1) * half], xh, r, nw_ref[:, g * half:(g + 1) * half])
            dyg.append(dpart)
            dnw_rows.append(dw_rows)
        dyg = jnp.concatenate(dyg, axis=1)
        dnw_ref[...] += _colsum(jnp.concatenate(dnw_rows, axis=1))
        dY = dyg * sil
        dz_ref[0] = (dyg * yv * (sig * (1.0 + zz * (1.0 - sig)))).astype(BF16)
        dsk_f = dsk_ref[...]
        dskc_ref[...] += _colsum(dY * xs)
        dYb = dY.astype(BF16)
        xtb = q["xt"].astype(BF16)
        xwb = (q["xt"] * q["w_f"]).astype(BF16)
        acum = q["acum"]
        alast = acum[L - 1:L, :]
        lane_id = lax.broadcasted_iota(jnp.int32, (L, LANES), 1)
        sub_id = lax.broadcasted_iota(jnp.int32, (LANES, L), 0)
        lane_row = lax.broadcasted_iota(jnp.int32, (1, LANES), 1)
        da_rows = jnp.zeros((L, LANES), F32)
        daT = jnp.zeros((LANES, L), F32)
        dxt, prod_off, prod_st, dbs, dcs = [], [], [], [], []
        hsum_row = jnp.zeros((1, LANES), F32)
        for g in range(SSD_GROUPS):
            bg = q["bm"][:, g * 128:(g + 1) * 128]
            cg = q["cm"][:, g * 128:(g + 1) * 128]
            G = _dot_nt(cg, bg)
            dG = jnp.zeros((L, L), F32)
            dcg = jnp.zeros((L, SSD_STATE), F32)
            dbg = jnp.zeros((L, SSD_STATE), F32)
            for pr in range(SSD_HEADS // SSD_GROUPS // 2):
                h0 = g * 8 + 2 * pr
                lo = h0 * SSD_HEAD_DIM
                cols = slice(lo, lo + 128)
                dY_p = dYb[:, cols]
                xt_p = xtb[:, cols]
                dxt_p = jnp.zeros((L, LANES), F32)
                for k in range(2):
                    h = h0 + k
                    Lm = _decay_matrix(q, h)
                    Mf = G * Lm
                    dYk = jnp.where(_head_mask(k), dY_p, jnp.zeros_like(dY_p))
                    dM = _dot_nt(dYk, xt_p)
                    dxt_p = dxt_p + _dot_tn(Mf.astype(BF16), dYk)
                    dG = dG + dM * Lm
                    Q = dM * Mf
                    da_rows = da_rows + jnp.where(lane_id == h, jnp.sum(Q, axis=1, keepdims=True), 0.0)
                    daT = daT + jnp.where(sub_id == h, jnp.sum(Q, axis=0, keepdims=True), 0.0)
                hpb = prev_ref[0, 0, lo:lo + 128, :]
                hp = hpb.astype(F32)
                zoff = _dot_nt(cg, hpb)
                e_p = q["e_f"][:, cols]
                dY_pf = dY[:, cols]
                dZb = (dY_pf * e_p).astype(BF16)
                dcg = dcg + _dot(dZb, hpb)
                dhp_off = _dot_tn(dZb, cg)
                prod_off.append(dY_pf * zoff * e_p)
                dS = dh_ref[lo:lo + 128, :]
                dSb = dS.astype(BF16)
                U = _dot_nt(bg, dSb)
                dxt_p = dxt_p + U * q["w_f"][:, cols]
                dbg = dbg + _dot(xwb[:, cols], dSb)
                prod_st.append(q["xt"][:, cols] * U)
                dh_ref[lo:lo + 128, :] = _pair_decay(alast, h0) * dS + dhp_off
                dsh = dS * hp
                for k in range(2):
                    total = jnp.sum(dsh[k * SSD_HEAD_DIM:(k + 1) * SSD_HEAD_DIM, :], axis=(0, 1), keepdims=True)
                    hsum_row = hsum_row + jnp.where(lane_row == h0 + k, total, 0.0)
                dxt.append(dxt_p)
            dGb = dG.astype(BF16)
            dcs.append(dcg + _dot(dGb, bg))
            dbs.append(dbg + _dot_tn(dGb, cg))
        dxt = jnp.concatenate(dxt, axis=1)
        da_rows = da_rows + _gather_heads(jnp.concatenate(prod_off, axis=1), E)
        dww = _gather_heads(jnp.concatenate(prod_st, axis=1), E) * jnp.exp(alast - acum)
        da_rows = da_rows - dww
        dlast = _colsum(dww) + jnp.exp(alast) * hsum_row
        triT = q["triT"]
        ddA = (_sum3(lambda part: _dot(triT, part), _split3(da_rows))
               - _sum3(lambda part: _dot_nt(triT, part), _split3(daT)) + dlast)
        ddA = jnp.where(q["head"], ddA, 0.0)
        ddt = ddA * q["a128"] + _gather_heads(dxt * xs, E)
        ddt_raw = jnp.where(q["head"], ddt * _sigmoid(q["pre"]), 0.0)
        ddtk_ref[0] = ddt_raw
        dxs = dxt * q["dt_f"] + dsk_f * dY
        dxbc_ref[0] = jnp.concatenate([dxs] + dbs + dcs, axis=1).astype(BF16)
        dvec_ref[0:1, :] += _colsum(ddt_raw)
        dvec_ref[1:2, :] += _colsum(ddA * q["dt"]) * q["a128"]

    rev = lambda n: pl.BlockSpec((B, L, n), lambda c: (0, nc - 1 - c, 0))
    vec = pl.BlockSpec((1, LANES), lambda c: (0, 0))
    sd = jax.ShapeDtypeStruct
    return pl.pallas_call(
        body, grid=(nc,), name="ssd_bwd",
        in_specs=[rev(D_CONV), rev(LANES), rev(D_SSD), rev(D_SSD),
                  pl.BlockSpec((B, 1, D_SSD, SSD_STATE), lambda c: (0, nc - 1 - c, 0, 0)), rev(D_SSD), vec, vec,
                  pl.BlockSpec((1, D_SSD), lambda c: (0, 0)),
                  pl.BlockSpec((1, D_SSD), lambda c: (0, 0)), pl.BlockSpec((LANES, D_SSD), lambda c: (0, 0))],
        out_specs=[rev(D_CONV), rev(LANES), rev(D_SSD), pl.BlockSpec((1, D_SSD), lambda c: (0, 0)),
                   pl.BlockSpec((8, LANES), lambda c: (0, 0))],
        out_shape=[sd((B, S, D_CONV), BF16), sd((B, S, LANES), F32), sd((B, S, D_SSD), BF16), sd((1, D_SSD), F32),
                   sd((8, LANES), F32)],
        scratch_shapes=[pltpu.VMEM((B, D_SSD, SSD_STATE), F32), pltpu.VMEM((1, D_SSD), F32)],
        compiler_params=_cparams(("arbitrary",)),
    )(xbc, dtk, z, y, prev, dys, dtb, alog, dsk, nw, expand)


def _rope_tables(pos_ref, invf_ref, place_ref):
    ang = invf_ref[...] * pos_ref[0].astype(F32)
    place = place_ref[...]
    cosf = 1.0 + _sum3(lambda part: _dot_tn(part, place), _split3(jnp.cos(ang) - 1.0))
    sinf = _sum3(lambda part: _dot_tn(part, place), _split3(jnp.sin(ang)))
    return cosf, sinf


def _rot(u):
    lane = lax.broadcasted_iota(jnp.int32, u.shape, 1)
    first = (lane >= QK_NOPE) & (lane < QK_NOPE + QK_ROPE // 2)
    second = (lane >= QK_NOPE + QK_ROPE // 2) & (lane < QK_DIM)
    return jnp.where(first, -pltpu.roll(u, LANES - QK_ROPE // 2, 1), jnp.where(second, pltpu.roll(u, QK_ROPE // 2, 1), 0.0))


def _rope_lanes(shape):
    lane = lax.broadcasted_iota(jnp.int32, shape, 1)
    return (lane >= QK_NOPE) & (lane < QK_DIM)


def _mla_prep(cq, ckv, dtk, pos, qw, kvw, wuq, wukv, invf, place):
    T = cq.shape[0]
    tm = min(WIDE_TOKEN_TILE, T)
    scale = 1.0 / math.sqrt(QK_DIM)
    HW = MLA_HEADS * HEAD_LANES

    def body(cq_ref, ckv_ref, dtk_ref, pos_ref, qw_ref, kvw_ref, wuq_ref, wukv_ref, invf_ref, place_ref, q_ref, k_ref, v_ref,
             cos_ref, sin_ref):
        xh, _ = _rms_stats(cq_ref[...])
        qv = _dot((xh * qw_ref[...]).astype(BF16), wuq_ref[...])
        xh, _ = _rms_stats(ckv_ref[...])
        kv = _dot((xh * kvw_ref[...]).astype(BF16), wukv_ref[...])
        cosf, sinf = _rope_tables(pos_ref, invf_ref, place_ref)
        cos_ref[...] = cosf
        sin_ref[...] = sinf
        rope = lambda u: u * cosf + _rot(u) * sinf
        dtkv = dtk_ref[...]
        kr = rope(jnp.where(_rope_lanes(dtkv.shape), dtkv, 0.0))
        for h in range(MLA_HEADS):
            cols = slice(h * HEAD_LANES, (h + 1) * HEAD_LANES)
            q_ref[:, cols] = (rope(qv[:, cols]) * scale).astype(BF16)
            k_ref[:, cols] = (kv[:, cols] + kr).astype(BF16)
        v_ref[...] = kv[:, HW:].astype(BF16)

    rows = lambda n: pl.BlockSpec((tm, n), lambda i: (i, 0))
    return pl.pallas_call(
        body, grid=(T // tm,), name="mla_prep",
        in_specs=[rows(Q_LORA), rows(KV_LORA), rows(LANES), pl.BlockSpec((1, 1, tm), lambda i: (i, 0, 0)),
                  _resident((1, Q_LORA)), _resident((1, KV_LORA)), _resident((Q_LORA, HW)), _resident((KV_LORA, 2 * HW)),
                  _resident((QK_ROPE // 2, 1)), _resident((QK_ROPE // 2, LANES))],
        out_specs=[rows(HW), rows(HW), rows(HW), rows(LANES), rows(LANES)],
        out_shape=[jax.ShapeDtypeStruct((T, HW), BF16)] * 3 + [jax.ShapeDtypeStruct((T, LANES), F32)] * 2,
        compiler_params=_cparams(("arbitrary",)),
    )(cq, ckv, dtk, pos.reshape(T // tm, 1, tm), qw, kvw, wuq, wukv, invf, place)


def _mla_prep_bwd(dq, dk, dv, cq, ckv, cos_t, sin_t, qw, kvw, wuq, wukv):
    T = cq.shape[0]
    tm = min(WIDE_TOKEN_TILE, T)
    scale = 1.0 / math.sqrt(QK_DIM)
    HW = MLA_HEADS * HEAD_LANES

    def body(dq_ref, dk_ref, dv_ref, cq_ref, ckv_ref, cos_ref, sin_ref, qw_ref, kvw_ref, wuq_ref, wukv_ref,
             dcq_ref, dckv_ref, ddtk_ref, qn_ref, kvn_ref, dqo_ref, dkvo_ref, dqw_ref, dkvw_ref):
        @pl.when(pl.program_id(0) == 0)
        def _():
            dqw_ref[...] = jnp.zeros_like(dqw_ref)
            dkvw_ref[...] = jnp.zeros_like(dkvw_ref)

        cosf, sinf = cos_ref[...], sin_ref[...]
        unrope = lambda d: d * cosf - _rot(d * sinf)
        dkr = jnp.zeros((tm, LANES), F32)
        nope = lax.broadcasted_iota(jnp.int32, (tm, LANES), 1) < QK_NOPE
        for h in range(MLA_HEADS):
            cols = slice(h * HEAD_LANES, (h + 1) * HEAD_LANES)
            dqo_ref[:, cols] = unrope(dq_ref[:, cols].astype(F32) * scale).astype(BF16)
            dkh = dk_ref[:, cols].astype(F32)
            dkr = dkr + jnp.where(_rope_lanes(dkh.shape), dkh, 0.0)
            dkvo_ref[:, cols] = jnp.where(nope, dkh, 0.0).astype(BF16)
        dkvo_ref[:, HW:] = dv_ref[...].astype(BF16)
        ddtk_ref[...] = unrope(dkr)
        xh, r = _rms_stats(cq_ref[...])
        qn_ref[...] = (xh * qw_ref[...]).astype(BF16)
        dx, dw_rows = _rms_bwd(_dot_nt(dqo_ref[...], wuq_ref[...]), xh, r, qw_ref[...])
        dcq_ref[...] = dx
        dqw_ref[...] += _colsum(dw_rows)
        xh, r = _rms_stats(ckv_ref[...])
        kvn_ref[...] = (xh * kvw_ref[...]).astype(BF16)
        dx, dw_rows = _rms_bwd(_dot_nt(dkvo_ref[...], wukv_ref[...]), xh, r, kvw_ref[...])
        dckv_ref[...] = dx
        dkvw_ref[...] += _colsum(dw_rows)

    rows = lambda n: pl.BlockSpec((tm, n), lambda i: (i, 0))
    sd = jax.ShapeDtypeStruct
    return pl.pallas_call(
        body, grid=(T // tm,), name="mla_prep_bwd",
        in_specs=[rows(HW), rows(HW), rows(HW), rows(Q_LORA), rows(KV_LORA), rows(LANES), rows(LANES), _resident((1, Q_LORA)),
                  _resident((1, KV_LORA)), _resident((Q_LORA, HW)), _resident((KV_LORA, 2 * HW))],
        out_specs=[rows(Q_LORA), rows(KV_LORA), rows(LANES), rows(Q_LORA), rows(KV_LORA), rows(HW), rows(2 * HW),
                   pl.BlockSpec((1, Q_LORA), lambda i: (0, 0)), pl.BlockSpec((1, KV_LORA), lambda i: (0, 0))],
        out_shape=[sd((T, Q_LORA), F32), sd((T, KV_LORA), F32), sd((T, LANES), F32), sd((T, Q_LORA), BF16),
                   sd((T, KV_LORA), BF16), sd((T, HW), BF16), sd((T, 2 * HW), BF16), sd((1, Q_LORA), F32),
                   sd((1, KV_LORA), F32)],
        compiler_params=_cparams(("arbitrary",)),
    )(dq, dk, dv, cq, ckv, cos_t, sin_t, qw, kvw, wuq, wukv)


def _causal_mask(t):
    row = lax.broadcasted_iota(jnp.int32, (t, t), 0)
    col = lax.broadcasted_iota(jnp.int32, (t, t), 1)
    return col <= row


def _attn_fwd(q, k, v):
    B, S, HW = q.shape
    H = HW // HEAD_LANES
    t = min(ATTN_FWD_Q_TILE, S)
    tk = min(ATTN_FWD_KV_TILE, t)
    nq = S // t
    per = t // tk

    pair = 4
    pw = pair * HEAD_LANES

    def body(q_ref, k_ref, v_ref, o_ref, lse_ref):
        qi = pl.program_id(2)
        lanes = [slice(hh * HEAD_LANES, (hh + 1) * HEAD_LANES) for hh in range(pair)]
        qs = [q_ref[0, :, cols] for cols in lanes]

        def step(j, carry, diag):
            sl = pl.ds(pl.multiple_of(j * tk, tk), tk)
            out = []
            for qv, cols, (m, l, acc) in zip(qs, lanes, carry):
                s = _dot_nt(qv, k_ref[0, sl, cols])
                if diag is not None:
                    row = lax.broadcasted_iota(jnp.int32, (t, tk), 0)
                    col = lax.broadcasted_iota(jnp.int32, (t, tk), 1)
                    s = jnp.where(col + diag * tk <= row, s, -1e30)
                m_new = jnp.maximum(m, jnp.max(s, axis=-1, keepdims=True))
                alpha = jnp.exp(m - m_new)
                p = jnp.exp(s - m_new)
                l = alpha * l + jnp.sum(p, axis=-1, keepdims=True)
                acc = alpha * acc + _dot(p.astype(BF16), v_ref[0, sl, cols])
                out.append((m_new, l, acc))
            return tuple(out)

        init = tuple((jnp.full((t, 1), -1e30, F32), jnp.zeros((t, 1), F32), jnp.zeros((t, HEAD_LANES), F32))
                     for _ in range(pair))
        carry = lax.fori_loop(0, qi * per, lambda j, c: step(j, c, None), init)
        for d in range(per):
            carry = step(qi * per + d, carry, d)
        for hh, (m, l, acc) in enumerate(carry):
            o_ref[0, :, lanes[hh]] = (acc / l).astype(BF16)
            lse_ref[0, hh] = m + jnp.log(l)

    return pl.pallas_call(
        body, grid=(B, H // pair, nq), name="attn_fwd",
        in_specs=[pl.BlockSpec((1, t, pw), lambda b, h, i: (b, i, h)),
                  pl.BlockSpec((1, S, pw), lambda b, h, i: (b, 0, h)),
                  pl.BlockSpec((1, S, pw), lambda b, h, i: (b, 0, h))],
        out_specs=[pl.BlockSpec((1, t, pw), lambda b, h, i: (b, i, h)),
                   pl.BlockSpec((1, pair, t, 1), lambda b, h, i: (b, h, i, 0))],
        out_shape=[jax.ShapeDtypeStruct((B, S, HW), BF16), jax.ShapeDtypeStruct((B, H, S, 1), F32)],
        compiler_params=_cparams(("arbitrary", "arbitrary", "arbitrary")),
    )(q, k, v)


def _attn_bwd(q, k, v, o, do, lse):
    B, S, HW = q.shape
    H = HW // HEAD_LANES
    t = min(ATTN_BWD_TILE, S)
    nq = S // t

    pair = 2
    pw = pair * HEAD_LANES

    def body(q_ref, k_ref, v_ref, o_ref, do_ref, lse_ref, dq_out_ref, dk_ref, dv_ref, dq_ref):
        j = pl.program_id(2)

        @pl.when(j == 0)
        def _():
            dq_ref[...] = jnp.zeros_like(dq_ref)

        lanes = [slice(hh * HEAD_LANES, (hh + 1) * HEAD_LANES) for hh in range(pair)]

        def step(i, carry, masked):
            sl = pl.ds(pl.multiple_of(i * t, t), t)
            out = []
            for hh, (cols, (dk, dv)) in enumerate(zip(lanes, carry)):
                kj = k_ref[0, :, cols]
                qi = q_ref[0, sl, cols]
                doi = do_ref[0, sl, cols]
                s = _dot_nt(qi, kj)
                if masked:
                    s = jnp.where(_causal_mask(t), s, -1e30)
                p = jnp.exp(s - lse_ref[0, hh, sl, :])
                dv = dv + _dot_tn(p.astype(BF16), doi)
                dp = _dot_nt(doi, v_ref[0, :, cols])
                delta = jnp.sum(doi.astype(F32) * o_ref[0, sl, cols].astype(F32), axis=-1, keepdims=True)
                dsb = (p * (dp - delta)).astype(BF16)
                dk = dk + _dot_tn(dsb, qi)
                dq_ref[sl, cols] += _dot(dsb, kj)
                out.append((dk, dv))
            return tuple(out)

        zero = jnp.zeros((t, HEAD_LANES), F32)
        carry = step(j, ((zero, zero),) * pair, True)
        carry = lax.fori_loop(j + 1, nq, lambda i, c: step(i, c, False), carry)
        for cols, (dk, dv) in zip(lanes, carry):
            dk_ref[0, :, cols] = dk.astype(BF16)
            dv_ref[0, :, cols] = dv.astype(BF16)

        @pl.when(j == nq - 1)
        def _():
            dq_out_ref[0] = dq_ref[...].astype(BF16)

    full = pl.BlockSpec((1, S, pw), lambda b, h, j: (b, 0, h))
    tile = pl.BlockSpec((1, t, pw), lambda b, h, j: (b, j, h))
    sd = jax.ShapeDtypeStruct
    return pl.pallas_call(
        body, grid=(B, H // pair, nq), name="attn_bwd",
        in_specs=[full, tile, tile, full, full, pl.BlockSpec((1, pair, S, 1), lambda b, h, j: (b, h, 0, 0))],
        out_specs=[full, tile, tile],
        out_shape=[sd((B, S, HW), BF16), sd((B, S, HW), BF16), sd((B, S, HW), BF16)],
        scratch_shapes=[pltpu.VMEM((S, pw), F32)],
        compiler_params=_cparams(("arbitrary", "arbitrary", "arbitrary")),
    )(q, k, v, o, do, lse)


def _mix_out(x1, yssd, o, mw, wout, g, seq):
    T, D = x1.shape
    tm = min(WIDE_TOKEN_TILE, seq)
    tps = seq // tm

    def body(x_ref, ys_ref, o_ref, mw_ref, w_ref, g_ref, xo_ref, m_ref, yc_ref):
        xh, _ = _rms_stats(o_ref[...].astype(F32))
        ycat = jnp.concatenate([ys_ref[...], (xh * mw_ref[...]).astype(BF16)], axis=1)
        m = _dot(ycat, w_ref[...])
        xo_ref[...] = x_ref[...] + g_ref[0] * m
        m_ref[...] = m.astype(BF16)
        yc_ref[...] = ycat

    rows = lambda n: pl.BlockSpec((tm, n), lambda i: (i, 0))
    perb = pl.BlockSpec((1, 1, D), lambda i: (i // tps, 0, 0))
    sd = jax.ShapeDtypeStruct
    return pl.pallas_call(
        body, grid=(T // tm,), name="mix_out",
        in_specs=[rows(D), rows(D_SSD), rows(D_MLA), _resident((1, D_MLA)), _resident((D_SSD + D_MLA, D)), perb],
        out_specs=[rows(D), rows(D), rows(D_SSD + D_MLA)],
        out_shape=[sd((T, D), F32), sd((T, D), BF16), sd((T, D_SSD + D_MLA), BF16)],
        compiler_params=_cparams(("arbitrary",)),
    )(x1, yssd, o, mw, wout, g)


def _mix_out_bwd(dx2, m, o, mw, wout, g, seq):
    T, D = dx2.shape
    B = T // seq
    tm = min(WIDE_TOKEN_TILE, seq)
    tps = seq // tm

    def body(dx_ref, m_ref, o_ref, mw_ref, w_ref, g_ref, dys_ref, do_ref, dm_ref, dg_ref, dmw_ref):
        i = pl.program_id(0)

        @pl.when(i % tps == 0)
        def _():
            dg_ref[...] = jnp.zeros_like(dg_ref)

        @pl.when(i == 0)
        def _():
            dmw_ref[...] = jnp.zeros_like(dmw_ref)

        dxv = dx_ref[...]
        dg_ref[0] += _colsum(dxv * m_ref[...].astype(F32))
        dmb = (g_ref[0] * dxv).astype(BF16)
        dm_ref[...] = dmb
        dycat = _dot_nt(dmb, w_ref[...])
        dys_ref[...] = dycat[:, :D_SSD].astype(BF16)
        xh, r = _rms_stats(o_ref[...].astype(F32))
        dx, dw_rows = _rms_bwd(dycat[:, D_SSD:], xh, r, mw_ref[...])
        do_ref[...] = dx.astype(BF16)
        dmw_ref[...] += _colsum(dw_rows)

    rows = lambda n: pl.BlockSpec((tm, n), lambda i: (i, 0))
    perb = pl.BlockSpec((1, 1, D), lambda i: (i // tps, 0, 0))
    sd = jax.ShapeDtypeStruct
    return pl.pallas_call(
        body, grid=(T // tm,), name="mix_out_bwd",
        in_specs=[rows(D), rows(D), rows(D_MLA), _resident((1, D_MLA)), _resident((D_SSD + D_MLA, D)), perb],
        out_specs=[rows(D_SSD), rows(D_MLA), rows(D), perb, pl.BlockSpec((1, D_MLA), lambda i: (0, 0))],
        out_shape=[sd((T, D_SSD), BF16), sd((T, D_MLA), BF16), sd((T, D), BF16), sd((B, 1, D), F32), sd((1, D_MLA), F32)],
        compiler_params=_cparams(("arbitrary",)),
    )(dx2, m, o, mw, wout, g)


def _win_to_kernel(w):
    z0 = jnp.zeros((48, w.shape[1]), w.dtype)
    z1 = jnp.zeros((32, w.shape[1]), w.dtype)
    return jnp.concatenate([w[:2560], w[2576:3216], w[2560:2576], z0, w[3216:3248], z1], axis=0)


def _win_from_kernel(g):
    return jnp.concatenate([g[:2560], g[3200:3216], g[2560:3200], g[3264:3296]], axis=0)


def _wuq_to_kernel(w):
    w = w.reshape(Q_LORA, MLA_HEADS, QK_DIM)
    return jnp.pad(w, ((0, 0), (0, 0), (0, HEAD_LANES - QK_DIM))).reshape(Q_LORA, MLA_HEADS * HEAD_LANES)


def _wuq_from_kernel(g):
    return g.reshape(Q_LORA, MLA_HEADS, HEAD_LANES)[:, :, :QK_DIM].reshape(Q_LORA, MLA_HEADS * QK_DIM)


def _wukv_to_kernel(w):
    w = w.reshape(KV_LORA, MLA_HEADS, QK_NOPE + V_HEAD)
    kp = jnp.pad(w[:, :, :QK_NOPE], ((0, 0), (0, 0), (0, HEAD_LANES - QK_NOPE)))
    return jnp.concatenate([kp.reshape(KV_LORA, -1), w[:, :, QK_NOPE:].reshape(KV_LORA, -1)], axis=1)


def _wukv_from_kernel(g):
    hw = MLA_HEADS * HEAD_LANES
    kp = g[:, :hw].reshape(KV_LORA, MLA_HEADS, HEAD_LANES)[:, :, :QK_NOPE]
    vp = g[:, hw:].reshape(KV_LORA, MLA_HEADS, V_HEAD)
    return jnp.concatenate([kp, vp], axis=2).reshape(KV_LORA, MLA_HEADS * (QK_NOPE + V_HEAD))


def _lanes16(v):
    return jnp.pad(v.reshape(1, SSD_HEADS), ((0, 0), (0, LANES - SSD_HEADS)))


def _constants():
    e = np.zeros((LANES, D_SSD), np.float32)
    for h in range(SSD_HEADS):
        e[h, h * SSD_HEAD_DIM:(h + 1) * SSD_HEAD_DIM] = 1.0
    inv_freq = ROPE_THETA ** (-jnp.arange(0, QK_ROPE, 2, dtype=F32) / QK_ROPE)
    half = QK_ROPE // 2
    place = np.zeros((half, LANES), np.float32)
    for j in range(half):
        place[j, QK_NOPE + j] = place[j, QK_NOPE + half + j] = 1.0
    return jnp.asarray(e, BF16), inv_freq.reshape(half, 1), jnp.asarray(place, BF16)


def _local_step(x, positions, mod, w, later_weights, small, tgt, on_grads, sync):
    B, S, D = x.shape
    T = B * S
    expand, invf, place = _constants()
    x0 = x.reshape(T, D)
    pos = positions.reshape(T)
    mods = [mod[:, i * D:(i + 1) * D].reshape(B, 1, D) for i in range(N_MOD)]
    sh1, sc1, g1, sh2, sc2, g2, sh3, sc3, g3 = mods
    dtb, alog = _lanes16(small["dt_bias"]), _lanes16(small["a_log"])
    dsk = jnp.repeat(small["d_skip"].reshape(1, SSD_HEADS), SSD_HEAD_DIM, axis=1)

    x1, a1, u1, f1 = _ffn_fwd(x0, small["norm_ffn1"], sh1, sc1, g1, w["ffn1_w_gate"], w["ffn1_w_up"], w["ffn1_w_down"], S, "ffn1_fwd")
    w = {**w, **later_weights(f1)}
    z, xraw, cq, ckv, dtk = _inproj_fwd(x1, small["norm_mix"], sh2, sc2, w["w_in"], S)
    xraw3 = xraw.reshape(B, S, D_CONV)
    xbc = _conv_fwd(xraw3, small["conv_w"], small["conv_b"])
    dtk3, z3 = dtk.reshape(B, S, LANES), z.reshape(B, S, D_SSD)
    y, yssd, prev = _ssd_fwd(xbc, dtk3, z3, dtb, alog, dsk, small["ssd_norm_w"], expand)
    q, k, v, cos_t, sin_t = _mla_prep(cq, ckv, dtk, pos, small["q_norm_w"], small["kv_norm_w"], w["w_uq"], w["w_ukv"], invf,
                                      place)
    hw = MLA_HEADS * HEAD_LANES
    q3, k3, v3 = q.reshape(B, S, hw), k.reshape(B, S, hw), v.reshape(B, S, hw)
    o3, lse = _attn_fwd(q3, k3, v3)
    o = o3.reshape(T, hw)
    x2, m, ycat = _mix_out(x1, yssd.reshape(T, D_SSD), o, small["mla_norm_w"], w["w_out"], g2, S)
    dx3, a2, u2, f2, loss, d_norm_final = _ffn_fwd(
        x2, small["norm_ffn2"], sh3, sc3, g3, w["ffn2_w_gate"], w["ffn2_w_up"], w["ffn2_w_down"], S, "ffn2_fwd",
        head=(small["norm_final"].reshape(1, D), tgt.reshape(T, D)))

    gw, gs = {}, {}
    dx2, h3, s3, df3, da3, du3, dsh3, dsc3, dg3, gs["norm_ffn2"] = _ffn_bwd(
        dx3, x2, small["norm_ffn2"], sh3, sc3, g3, a2, u2, f2, w["ffn2_w_gate"], w["ffn2_w_up"], w["ffn2_w_down"], S, "ffn2_bwd")
    gw["ffn2_w_gate"], gw["ffn2_w_up"], gw["ffn2_w_down"] = _ffn_wgrad(h3, s3, df3, da3, du3, dsh3, "ffn2_wgrad")
    g2 = g2 + on_grads(("ffn2_w_gate", "ffn2_w_up", "ffn2_w_down"), gw)

    dys, do, dm, dg2, gs["mla_norm_w"] = _mix_out_bwd(dx2, m, o, small["mla_norm_w"], w["w_out"], g2, S)
    gw["w_out"] = _mm_tn(ycat, dm, 512, "dwout")

    dq3, dk3, dv3 = _attn_bwd(q3, k3, v3, o3, do.reshape(B, S, hw), lse)
    dcq, dckv, ddtk_b, qn, kvn, dqb, dkvb, gs["q_norm_w"], gs["kv_norm_w"] = _mla_prep_bwd(
        dq3.reshape(T, hw), dk3.reshape(T, hw), dv3.reshape(T, hw), cq, ckv, cos_t, sin_t, small["q_norm_w"] + sync(dq3),
        small["kv_norm_w"], w["w_uq"], w["w_ukv"])
    gw["w_uq"] = _mm_tn(qn, dqb, 512, "dwuq")
    gw["w_ukv"] = _mm_tn(kvn, dkvb, 1024, "dwukv")

    dxbc, ddtk_a, dz, gs["ssd_norm_w"], dvec = _ssd_bwd(
        xbc, dtk3, z3, y, prev, dys.reshape(B, S, D_SSD), dtb, alog, dsk, small["ssd_norm_w"], expand)
    gs["dt_bias"], gs["a_log"], gs["d_skip"] = dvec[0:1, :SSD_HEADS], dvec[1:2, :SSD_HEADS], dvec[2:3, :SSD_HEADS]
    dxraw, gs["conv_w"], gs["conv_b"] = _conv_bwd(dxbc, xraw3, small["conv_w"], small["conv_b"])
    dx1, h2, dproj, dsh2, dsc2, gs["norm_mix"] = _inproj_bwd(
        dx2, x1, small["norm_mix"], sh2, sc2, w["w_in"], dz.reshape(T, D_SSD), dxraw.reshape(T, D_CONV), dcq, dckv,
        ddtk_a.reshape(T, LANES), ddtk_b, S)
    gw["w_in"] = _mm_tn(dproj, h2, 512, "dwin")
    g1 = g1 + on_grads(("w_in", "w_uq", "w_ukv", "w_out"), gw)

    dx0, h1, s1, df1, da1, du1, dsh1, dsc1, dg1, gs["norm_ffn1"] = _ffn_bwd(
        dx1, x0, small["norm_ffn1"], sh1, sc1, g1, a1, u1, f1, w["ffn1_w_gate"], w["ffn1_w_up"], w["ffn1_w_down"], S, "ffn1_bwd")
    gw["ffn1_w_gate"], gw["ffn1_w_up"], gw["ffn1_w_down"] = _ffn_wgrad(h1, s1, df1, da1, du1, dsh1 + sync(dx0), "ffn1_wgrad")
    gs["norm_final"] = d_norm_final
    dmod = jnp.concatenate([t.reshape(B, D) for t in (dsh1, dsc1, dg1, dsh2, dsc2, dg2, dsh3, dsc3, dg3)], axis=1)
    return loss, dx0.reshape(B, S, D), gw, dmod, gs


HBM_SPEC = pl.BlockSpec(memory_space=pltpu.HBM)
VMEM_SPEC = pl.BlockSpec(memory_space=pltpu.VMEM)


def _place():
    return lax.axis_index("x"), lax.axis_index("y"), lax.axis_index("c")


def _other_chips(mx, my):
    return [(1 - mx, my), (mx, 1 - my), (1 - mx, 1 - my)]


def _remote(src, dst, send_sem, recv_sem, to):
    return pltpu.make_async_remote_copy(src_ref=src, dst_ref=dst, send_sem=send_sem, recv_sem=recv_sem,
                                        device_id=to, device_id_type=MESH)


def _all_gather_small(xa, name):
    r, n = xa.shape

    def body(x_ref, o_ref, token, send_sems, recv_sems):
        mx, my, mc = _place()
        me = 4 * mx + 2 * my + mc
        token[...] = jnp.zeros_like(token)
        o_ref[pl.ds(me, 1)] = x_ref[...][None]
        sends = []
        for k in range(1, N_DEV):
            peer = (mx ^ (k >> 2), my ^ ((k >> 1) & 1), mc ^ (k & 1))
            cp = _remote(x_ref, o_ref.at[me], send_sems.at[k - 1], recv_sems.at[k - 1], peer)
            cp.start()
            sends.append(cp)
        for k in range(1, N_DEV):
            peer = (mx ^ (k >> 2), my ^ ((k >> 1) & 1), mc ^ (k & 1))
            slot = 4 * peer[0] + 2 * peer[1] + peer[2]
            _remote(x_ref, o_ref.at[slot], send_sems.at[k - 1], recv_sems.at[k - 1], peer).wait_recv()
        for cp in sends:
            cp.wait_send()

    return pl.pallas_call(
        body, name=name, in_specs=[VMEM_SPEC], out_specs=[VMEM_SPEC, VMEM_SPEC],
        out_shape=[jax.ShapeDtypeStruct((N_DEV, r, n), xa.dtype), jax.ShapeDtypeStruct((8, LANES), F32)],
        scratch_shapes=[pltpu.SemaphoreType.DMA((N_DEV - 1,)), pltpu.SemaphoreType.DMA((N_DEV - 1,))],
        compiler_params=pltpu.CompilerParams(vmem_limit_bytes=VMEM_LIMIT),
    )(xa)


def _halves_by_rows(shape):
    return (shape[-2] // 2) % 16 == 0


def _half_shape(shape):
    r, c = shape[-2:]
    return tuple(shape[:-2]) + ((r // 2, c) if _halves_by_rows(shape) else (r, c // 2))


def _half_index(shape, hc):
    r, c = shape[-2:]
    if _halves_by_rows(shape):
        return (pl.ds(pl.multiple_of(hc * (r // 2), 16), r // 2), slice(None))
    return (slice(None), pl.ds(pl.multiple_of(hc * (c // 2), LANES), c // 2))


def _half(ref, hc, lead=None):
    idx = _half_index(ref.shape, hc)
    return ref.at[idx] if lead is None else ref.at[(lead,) + idx]


def _gather_weights(shards):
    n = len(shards)

    def body(*refs):
        w_refs, o_refs, token = refs[:n], refs[n:2 * n], refs[2 * n]
        send_sems, recv_sems, stage_sems = refs[2 * n + 1:2 * n + 4]
        stages = refs[2 * n + 4:]
        mx, my, mc = _place()
        chip = 2 * mx + my
        others = _other_chips(mx, my)
        sibling = (mx, my, 1 - mc)
        token[...] = jnp.zeros_like(token)
        stage_in = [pltpu.make_async_copy(w, st, stage_sems.at[0, i]) for i, (w, st) in enumerate(zip(w_refs, stages))]
        for cp in stage_in:
            cp.start()
        first = []
        for i, (w, o) in enumerate(zip(w_refs, o_refs)):
            for k, (cx, cy) in enumerate(others):
                first.append(_remote(_half(w, mc), _half(o, mc, chip), send_sems.at[i, k],
                                     recv_sems.at[i, k], (cx, cy, mc)))
                first[-1].start()
        stage_out = []
        for i, (st, o) in enumerate(zip(stages, o_refs)):
            stage_in[i].wait()
            stage_out.append(pltpu.make_async_copy(st, o.at[chip], stage_sems.at[1, i]))
            stage_out[-1].start()
        passed = []
        for i, (w, o) in enumerate(zip(w_refs, o_refs)):
            for k, (cx, cy) in enumerate(others):
                landed = _half(o, mc, 2 * cx + cy)
                _remote(landed, landed, send_sems.at[i, k], recv_sems.at[i, k], (cx, cy, mc)).wait_recv()
                passed.append(_remote(landed, landed, send_sems.at[i, 3 + k], recv_sems.at[i, 3 + k], sibling))
                passed[-1].start()
        for i, (w, o) in enumerate(zip(w_refs, o_refs)):
            for k, (cx, cy) in enumerate(others):
                there = _half(o, 1 - mc, 2 * cx + cy)
                _remote(there, there, send_sems.at[i, 3 + k], recv_sems.at[i, 3 + k], sibling).wait_recv()
        for cp in first + passed:
            cp.wait_send()
        for cp in stage_out:
            cp.wait()

    out = pl.pallas_call(
        body, name="gather_weights", in_specs=[HBM_SPEC] * n, out_specs=[HBM_SPEC] * n + [VMEM_SPEC],
        out_shape=[jax.ShapeDtypeStruct((N_CHIPS,) + s.shape, s.dtype) for s in shards] + [jax.ShapeDtypeStruct((8, LANES), F32)],
        scratch_shapes=[pltpu.SemaphoreType.DMA((n, 6)), pltpu.SemaphoreType.DMA((n, 6)), pltpu.SemaphoreType.DMA((2, n))]
        + [pltpu.VMEM(s.shape, s.dtype) for s in shards],
        compiler_params=pltpu.CompilerParams(vmem_limit_bytes=VMEM_LIMIT),
    )(*shards)
    return out[:n], out[n]


SEM_SPEC = pl.BlockSpec(memory_space=pltpu.SEMAPHORE)
ANY_SPEC = pl.BlockSpec(memory_space=pl.ANY)
DATAFLOW = pltpu.SideEffectType.DATAFLOW_SIDE_EFFECTING


def _hbm(arr):
    return pltpu.with_memory_space_constraint(arr, pltpu.HBM)


def _gather_start(shards):
    n = len(shards)

    def body(*refs):
        w_refs, land_refs, send_sems, recv_sems, token = refs[:n], refs[n:2 * n], refs[2 * n], refs[2 * n + 1], refs[-1]
        mx, my, mc = _place()
        chip = 2 * mx + my
        for i, (w, land) in enumerate(zip(w_refs, land_refs)):
            for k, (cx, cy) in enumerate(_other_chips(mx, my)):
                _remote(_half(w, mc), _half(land, mc, chip), send_sems.at[3 * i + k],
                        recv_sems.at[3 * i + k], (cx, cy, mc)).start()
        token[...] = jnp.zeros_like(token)

    lands = [lax.empty((N_CHIPS,) + s.shape, s.dtype) for s in shards]
    out = pl.pallas_call(
        body, name="gather_start",
        out_shape=(pltpu.SemaphoreType.DMA((3 * n,)), pltpu.SemaphoreType.DMA((3 * n,)),
                   *[pltpu.HBM(s.shape, s.dtype) for s in shards], *[pltpu.HBM(l.shape, l.dtype) for l in lands],
                   jax.ShapeDtypeStruct((8, LANES), F32)),
        in_specs=[HBM_SPEC] * (2 * n), out_specs=(SEM_SPEC, SEM_SPEC, *[HBM_SPEC] * (2 * n), VMEM_SPEC),
        input_output_aliases={i: 2 + i for i in range(2 * n)},
        compiler_params=pltpu.CompilerParams(has_side_effects=DATAFLOW),
    )(*[_hbm(s) for s in shards], *[_hbm(l) for l in lands])
    return out[0], out[1], out[2:2 + n], out[2 + n:2 + 2 * n], out[-1]


def _gather_wait(send_sems, recv_sems, shards, lands, after):
    n = len(shards)

    def body(*refs):
        w_refs, land_refs, send_sems, recv_sems = refs[:n], refs[n:2 * n], refs[2 * n], refs[2 * n + 1]
        mx, my, mc = _place()
        for i, (w, land) in enumerate(zip(w_refs, land_refs)):
            for k, (cx, cy) in enumerate(_other_chips(mx, my)):
                cp = _remote(_half(w, mc), _half(land, mc, 2 * cx + cy), send_sems.at[3 * i + k],
                             recv_sems.at[3 * i + k], (cx, cy, mc))
                cp.wait_send()
                cp.wait_recv()

    out = pl.pallas_call(
        body, name="gather_wait",
        out_shape=(*[pltpu.HBM(s.shape, s.dtype) for s in shards], *[pltpu.HBM(l.shape, l.dtype) for l in lands]),
        in_specs=[HBM_SPEC] * (2 * n) + [SEM_SPEC, SEM_SPEC, ANY_SPEC], out_specs=tuple([HBM_SPEC] * (2 * n)),
        input_output_aliases={i: i for i in range(2 * n)},
        compiler_params=pltpu.CompilerParams(has_side_effects=DATAFLOW),
    )(*shards, *lands, send_sems, recv_sems, after)
    return out[n:]


def _gather_finish(shards, lands):
    n = len(shards)

    def body(*refs):
        w_refs, land_refs, o_refs = refs[:n], refs[n:2 * n], refs[2 * n:3 * n]
        send_sems, recv_sems, stage_sems = refs[3 * n:3 * n + 3]
        stages = refs[3 * n + 3:]
        mx, my, mc = _place()
        chip = 2 * mx + my
        others = _other_chips(mx, my)
        sibling = (mx, my, 1 - mc)
        stage_in = [pltpu.make_async_copy(w, st, stage_sems.at[0, i]) for i, (w, st) in enumerate(zip(w_refs, stages))]
        for cp in stage_in:
            cp.start()
        passed = []
        for i, (w, o) in enumerate(zip(w_refs, o_refs)):
            for k, (cx, cy) in enumerate(others):
                landed = _half(o, mc, 2 * cx + cy)
                passed.append(_remote(landed, landed, send_sems.at[i, k], recv_sems.at[i, k], sibling))
                passed[-1].start()
        stage_out = []
        for i, (st, o) in enumerate(zip(stages, o_refs)):
            stage_in[i].wait()
            stage_out.append(pltpu.make_async_copy(st, o.at[chip], stage_sems.at[1, i]))
            stage_out[-1].start()
        for i, (w, o) in enumerate(zip(w_refs, o_refs)):
            for k, (cx, cy) in enumerate(others):
                there = _half(o, 1 - mc, 2 * cx + cy)
                _remote(there, there, send_sems.at[i, k], recv_sems.at[i, k], sibling).wait_recv()
        for cp in passed:
            cp.wait_send()
        for cp in stage_out:
            cp.wait()

    return pl.pallas_call(
        body, name="gather_finish", in_specs=[HBM_SPEC] * (2 * n), out_specs=[HBM_SPEC] * n,
        out_shape=[jax.ShapeDtypeStruct(l.shape, l.dtype) for l in lands],
        input_output_aliases={n + i: i for i in range(n)},
        scratch_shapes=[pltpu.SemaphoreType.DMA((n, 3)), pltpu.SemaphoreType.DMA((n, 3)), pltpu.SemaphoreType.DMA((2, n))]
        + [pltpu.VMEM(s.shape, s.dtype) for s in shards],
        compiler_params=pltpu.CompilerParams(vmem_limit_bytes=VMEM_LIMIT),
    )(*shards, *lands)


def _scatter_start(ss, tag):
    n = len(ss)

    def body(*refs):
        s_refs, land_refs, send_sems, recv_sems, token = refs[:n], refs[n:2 * n], refs[2 * n], refs[2 * n + 1], refs[-1]
        mx, my, mc = _place()
        chip = 2 * mx + my
        for i, (s, land) in enumerate(zip(s_refs, land_refs)):
            for k, (cx, cy) in enumerate(_other_chips(mx, my)):
                _remote(s.at[2 * cx + cy], land.at[chip], send_sems.at[3 * i + k], recv_sems.at[3 * i + k],
                        (cx, cy, mc)).start()
        token[...] = jnp.zeros_like(token)

    lands = [lax.empty(s.shape, s.dtype) for s in ss]
    out = pl.pallas_call(
        body, name="scatter_start_" + tag,
        out_shape=(pltpu.SemaphoreType.DMA((3 * n,)), pltpu.SemaphoreType.DMA((3 * n,)),
                   *[pltpu.HBM(s.shape, s.dtype) for s in ss], *[pltpu.HBM(l.shape, l.dtype) for l in lands],
                   jax.ShapeDtypeStruct((8, LANES), F32)),
        in_specs=[HBM_SPEC] * (2 * n), out_specs=(SEM_SPEC, SEM_SPEC, *[HBM_SPEC] * (2 * n), VMEM_SPEC),
        input_output_aliases={i: 2 + i for i in range(2 * n)},
        compiler_params=pltpu.CompilerParams(has_side_effects=DATAFLOW),
    )(*[_hbm(s) for s in ss], *[_hbm(l) for l in lands])
    return out[0], out[1], out[2:2 + n], out[2 + n:2 + 2 * n], out[-1]


def _scatter_wait(send_sems, recv_sems, ss, lands, after, tag):
    n = len(ss)

    def body(*refs):
        s_refs, land_refs, send_sems, recv_sems = refs[:n], refs[n:2 * n], refs[2 * n], refs[2 * n + 1]
        mx, my, mc = _place()
        for i, (s, land) in enumerate(zip(s_refs, land_refs)):
            for k, (cx, cy) in enumerate(_other_chips(mx, my)):
                slot = land.at[2 * cx + cy]
                cp = _remote(s.at[2 * cx + cy], slot, send_sems.at[3 * i + k], recv_sems.at[3 * i + k], (cx, cy, mc))
                cp.wait_send()
                cp.wait_recv()

    out = pl.pallas_call(
        body, name="scatter_wait_" + tag,
        out_shape=(*[pltpu.HBM(s.shape, s.dtype) for s in ss], *[pltpu.HBM(l.shape, l.dtype) for l in lands]),
        in_specs=[HBM_SPEC] * (2 * n) + [SEM_SPEC, SEM_SPEC, ANY_SPEC], out_specs=tuple([HBM_SPEC] * (2 * n)),
        input_output_aliases={i: i for i in range(2 * n)},
        compiler_params=pltpu.CompilerParams(has_side_effects=DATAFLOW),
    )(*ss, *lands, send_sems, recv_sems, after)
    return out[:n], out[n:]


def _swap_halves(gs, after, name):
    n = len(gs)

    def body(*refs):
        g_refs, o_refs, send_sems, recv_sems = refs[:n], refs[n + 1:2 * n + 1], refs[2 * n + 1], refs[2 * n + 2]
        mx, my, mc = _place()
        copies = []
        for i, (g, o) in enumerate(zip(g_refs, o_refs)):
            src = g.at[(slice(None),) + _half_index(g.shape, 1 - mc)]
            copies.append(_remote(src, o, send_sems.at[i], recv_sems.at[i], (mx, my, 1 - mc)))
            copies[-1].start()
        for cp in copies:
            cp.wait()

    return pl.pallas_call(
        body, name=name, in_specs=[HBM_SPEC] * n + [ANY_SPEC], out_specs=[HBM_SPEC] * n,
        out_shape=[jax.ShapeDtypeStruct(_half_shape(g.shape), g.dtype) for g in gs],
        scratch_shapes=[pltpu.SemaphoreType.DMA((n,)), pltpu.SemaphoreType.DMA((n,))],
    )(*gs, after)


def _swap_start(gs, tag):
    n = len(gs)

    def body(*refs):
        g_refs, land_refs, send_sems, recv_sems, token = refs[:n], refs[n:2 * n], refs[2 * n], refs[2 * n + 1], refs[-1]
        mx, my, mc = _place()
        for i, (g, land) in enumerate(zip(g_refs, land_refs)):
            src = g.at[(slice(None),) + _half_index(g.shape, 1 - mc)]
            _remote(src, land, send_sems.at[i], recv_sems.at[i], (mx, my, 1 - mc)).start()
        token[...] = jnp.zeros_like(token)

    lands = [lax.empty(_half_shape(g.shape), g.dtype) for g in gs]
    out = pl.pallas_call(
        body, name="swap_start_" + tag,
        out_shape=(pltpu.SemaphoreType.DMA((n,)), pltpu.SemaphoreType.DMA((n,)), *[pltpu.HBM(g.shape, g.dtype) for g in gs],
                   *[pltpu.HBM(l.shape, l.dtype) for l in lands], jax.ShapeDtypeStruct((8, LANES), F32)),
        in_specs=[HBM_SPEC] * (2 * n), out_specs=(SEM_SPEC, SEM_SPEC, *[HBM_SPEC] * (2 * n), VMEM_SPEC),
        input_output_aliases={i: 2 + i for i in range(2 * n)},
        compiler_params=pltpu.CompilerParams(has_side_effects=DATAFLOW),
    )(*[_hbm(g) for g in gs], *[_hbm(l) for l in lands])
    return out[0], out[1], out[2:2 + n], out[2 + n:2 + 2 * n], out[-1]


def _swap_wait(send_sems, recv_sems, gs, lands, after, tag):
    n = len(gs)

    def body(*refs):
        g_refs, land_refs, send_sems, recv_sems = refs[:n], refs[n:2 * n], refs[2 * n], refs[2 * n + 1]
        mx, my, mc = _place()
        for i, (g, land) in enumerate(zip(g_refs, land_refs)):
            src = g.at[(slice(None),) + _half_index(g.shape, 1 - mc)]
            cp = _remote(src, land, send_sems.at[i], recv_sems.at[i], (mx, my, 1 - mc))
            cp.wait_send()
            cp.wait_recv()

    out = pl.pallas_call(
        body, name="swap_wait_" + tag,
        out_shape=(*[pltpu.HBM(g.shape, g.dtype) for g in gs], *[pltpu.HBM(l.shape, l.dtype) for l in lands]),
        in_specs=[HBM_SPEC] * (2 * n) + [SEM_SPEC, SEM_SPEC, ANY_SPEC], out_specs=tuple([HBM_SPEC] * (2 * n)),
        input_output_aliases={i: i for i in range(2 * n)},
        compiler_params=pltpu.CompilerParams(has_side_effects=DATAFLOW),
    )(*gs, *lands, send_sems, recv_sems, after)
    return out[:n], out[n:]


def _pair_sum(g, got, core, name):
    hr, hc = _half_shape(g.shape)[1:]
    by_rows = _halves_by_rows(g.shape)

    def body(core_ref, g_ref, got_ref, o_ref):
        o_ref[...] = (g_ref[...].astype(F32) + got_ref[...].astype(F32)).astype(BF16)

    return pl.pallas_call(
        body, name=name,
        grid_spec=pltpu.PrefetchScalarGridSpec(
            num_scalar_prefetch=1, grid=(N_CHIPS,),
            in_specs=[pl.BlockSpec((1, hr, hc), lambda j, core_ref: (j, core_ref[0], 0) if by_rows else (j, 0, core_ref[0])),
                      pl.BlockSpec((1, hr, hc), lambda j, core_ref: (j, 0, 0))],
            out_specs=pl.BlockSpec((1, hr, hc), lambda j, core_ref: (j, 0, 0))),
        out_shape=jax.ShapeDtypeStruct((N_CHIPS, hr, hc), BF16),
        compiler_params=_cparams(("arbitrary",)),
    )(core, g, got)


def _chip_sum(own, got, chip, name):
    _, h, c = own.shape

    def body(chip_ref, a_ref, b_ref, c_ref, d_ref, o_ref):
        o_ref[...] = ((a_ref[0].astype(F32) + b_ref[0].astype(F32)) + c_ref[0].astype(F32)) + d_ref[0].astype(F32)

    slot = lambda flip: pl.BlockSpec((1, h, c), lambda i, chip_ref: (chip_ref[0] ^ flip, 0, 0))
    return pl.pallas_call(
        body, name=name,
        grid_spec=pltpu.PrefetchScalarGridSpec(
            num_scalar_prefetch=1, grid=(1,), in_specs=[slot(0), slot(1), slot(2), slot(3)],
            out_specs=pl.BlockSpec((h, c), lambda i, chip_ref: (0, 0))),
        out_shape=jax.ShapeDtypeStruct((h, c), F32),
        compiler_params=_cparams(("arbitrary",)),
    )(chip, own, got, got, got)


def _join_halves(mine, name):
    n = len(mine)

    def body(*refs):
        m_refs, o_refs, send_sems, recv_sems = refs[:n], refs[n:2 * n], refs[2 * n], refs[2 * n + 1]
        mx, my, mc = _place()
        copies = []
        for i, (m, o) in enumerate(zip(m_refs, o_refs)):
            copies.append(_remote(m, o, send_sems.at[i], recv_sems.at[i], (mx, my, 1 - mc)))
            copies[-1].start()
        for cp in copies:
            cp.wait()

    return pl.pallas_call(
        body, name=name, in_specs=[HBM_SPEC] * n, out_specs=[HBM_SPEC] * n,
        out_shape=[jax.ShapeDtypeStruct(m.shape, m.dtype) for m in mine],
        scratch_shapes=[pltpu.SemaphoreType.DMA((n,)), pltpu.SemaphoreType.DMA((n,))],
    )(*mine)


def _adam_math(w, g, m, v):
    m2 = ADAM_B1 * m + (1.0 - ADAM_B1) * g
    v2 = ADAM_B2 * v + (1.0 - ADAM_B2) * (g * g)
    m_hat = m2 * (1.0 / (1.0 - ADAM_B1 ** ADAM_STEP))
    v_hat = v2 * (1.0 / (1.0 - ADAM_B2 ** ADAM_STEP))
    delta = -ADAM_LR * (m_hat / (jnp.sqrt(v_hat) + ADAM_EPS) + ADAM_WD * w)
    return delta, m2, v2


def _adam(w, g, m, v, name):
    def body(w_ref, g_ref, m_ref, v_ref, d_ref, m2_ref, v2_ref):
        d_ref[...], m2_ref[...], v2_ref[...] = _adam_math(w_ref[...], g_ref[...], m_ref[...], v_ref[...])

    return pl.pallas_call(body, name=name, out_shape=[jax.ShapeDtypeStruct(w.shape, F32)] * 3)(w, g, m, v)


def _adam_halves(w, m, v, mine, theirs, core, name):
    hr, hcols = _half_shape(w.shape)[1:]
    by_rows = _halves_by_rows(w.shape)

    def body(core_ref, w_ref, m_ref, v_ref, mine_ref, theirs_ref, g_ref, d_ref, m2_ref, v2_ref):
        g = jnp.where(pl.program_id(0) == core_ref[0], mine_ref[...], theirs_ref[...])
        g_ref[0] = g
        d_ref[0], m2_ref[0], v2_ref[0] = _adam_math(w_ref[0], g, m_ref[0], v_ref[0])

    half = pl.BlockSpec((1, hr, hcols), lambda hc, core_ref: (0, hc, 0) if by_rows else (0, 0, hc))
    whole = pl.BlockSpec((hr, hcols), lambda hc, core_ref: (0, 0))
    return pl.pallas_call(
        body, name=name,
        grid_spec=pltpu.PrefetchScalarGridSpec(
            num_scalar_prefetch=1, grid=(2,), in_specs=[half, half, half, whole, whole], out_specs=[half] * 4),
        out_shape=[jax.ShapeDtypeStruct(w.shape, F32)] * 4,
        compiler_params=_cparams(("arbitrary",)),
    )(core, w, m, v, mine, theirs)


ADA_COLS = N_MOD * D_MODEL // N_CHIPS


def _ada_fwd(c_all, w_ada, b_cols):
    def body(c_ref, w_ref, b_ref, o_ref):
        cv = c_ref[...]
        act = (cv * _sigmoid(cv)).astype(BF16)
        o_ref[...] = _dot(act, w_ref[...].astype(BF16)) + b_ref[...]

    return pl.pallas_call(
        body, name="ada_fwd", out_shape=jax.ShapeDtypeStruct((c_all.shape[0], ADA_COLS), F32),
        compiler_params=pltpu.CompilerParams(vmem_limit_bytes=VMEM_LIMIT),
    )(c_all, w_ada, b_cols)


def _ada_bwd(c_all, dmod_cols, w, m, v):
    nb = c_all.shape[0]
    tn = 384

    def body(c_ref, d_ref, w_ref, m_ref, v_ref, g_ref, dl_ref, m2_ref, v2_ref):
        cv = c_ref[...]
        act = (cv * _sigmoid(cv)).astype(BF16)
        g = _dot_tn(act, d_ref[...].astype(BF16))
        g_ref[...] = g
        dl_ref[...], m2_ref[...], v2_ref[...] = _adam_math(w_ref[...], g, m_ref[...], v_ref[...])

    blk = pl.BlockSpec((D_MODEL, tn), lambda j: (0, j))
    return pl.pallas_call(
        body, name="ada_bwd", grid=(ADA_COLS // tn,),
        in_specs=[pl.BlockSpec((nb, D_MODEL), lambda j: (0, 0)), pl.BlockSpec((nb, tn), lambda j: (0, j)), blk, blk, blk],
        out_specs=[blk] * 4, out_shape=[jax.ShapeDtypeStruct((D_MODEL, ADA_COLS), F32)] * 4,
        compiler_params=_cparams(("arbitrary",)),
    )(c_all, dmod_cols, w, m, v)


SMALL_NAMES = ("norm_ffn1", "norm_mix", "conv_w", "conv_b", "ssd_norm_w", "q_norm_w", "kv_norm_w", "mla_norm_w",
               "norm_ffn2", "norm_final", "dt_bias", "a_log", "d_skip")
SMALL_SIZES = (1024, 1024, CONV_WIDTH * D_CONV, D_CONV, 1024, Q_LORA, KV_LORA, 1024, 1024, 1024, 16, 16, 16)
SMALL_ROWS = 16
MOD_ROWS = 2 * N_MOD
SEND_ROWS = 40


def _pack_small(parts):
    flat = jnp.concatenate([parts[n].reshape(-1) for n in SMALL_NAMES])
    return jnp.pad(flat, (0, SMALL_ROWS * D_MODEL - flat.shape[0]))


def _unpack_small(flat):
    out, off = {}, 0
    for n, size in zip(SMALL_NAMES, SMALL_SIZES):
        out[n] = flat[off:off + size]
        off += size
    return out


def _small_sum(got):
    def body(g_ref, o_ref):
        bsum = jnp.zeros((N_MOD, D_MODEL), F32)
        ssum = jnp.zeros((SMALL_ROWS, D_MODEL), F32)
        for d in range(N_DEV):
            bsum = bsum + g_ref[d, 0:N_MOD, :] + g_ref[d, N_MOD:MOD_ROWS, :]
            ssum = ssum + g_ref[d, MOD_ROWS:MOD_ROWS + SMALL_ROWS, :]
        o_ref[...] = jnp.concatenate([bsum, ssum, jnp.zeros((32 - N_MOD - SMALL_ROWS, D_MODEL), F32)], axis=0)

    return pl.pallas_call(body, name="small_sum", out_shape=jax.ShapeDtypeStruct((32, D_MODEL), F32))(got)


BIG_NAMES = ("ffn1_w_gate", "ffn1_w_up", "ffn1_w_down", "w_in", "w_uq", "w_ukv", "w_out", "ffn2_w_gate", "ffn2_w_up",
             "ffn2_w_down")
_TO_KERNEL = {"w_in": _win_to_kernel, "w_uq": _wuq_to_kernel, "w_ukv": _wukv_to_kernel}
_FROM_KERNEL = {"w_in": _win_from_kernel, "w_uq": _wuq_from_kernel, "w_ukv": _wukv_from_kernel}


def _columns_joined(w4):
    n, r, c = w4.shape
    return w4.transpose(1, 0, 2).reshape(r, n * c)


def _columns_split(g):
    r, cols = g.shape
    return g.reshape(r, N_CHIPS, cols // N_CHIPS).transpose(1, 0, 2)


def kernel(x, c, positions, w_ada, b_ada, norm_ffn1, ffn1_w_gate, ffn1_w_up, ffn1_w_down, norm_mix, w_in, conv_w, conv_b, dt_bias, a_log, d_skip, ssd_norm_w, q_norm_w, w_uq, kv_norm_w, w_ukv, mla_norm_w, w_out, norm_ffn2, ffn2_w_gate, ffn2_w_up, ffn2_w_down, norm_final, loss_target, m_w_ada, m_b_ada, m_norm_ffn1, m_ffn1_w_gate, m_ffn1_w_up, m_ffn1_w_down, m_norm_mix, m_w_in, m_conv_w, m_conv_b, m_dt_bias, m_a_log, m_d_skip, m_ssd_norm_w, m_q_norm_w, m_w_uq, m_kv_norm_w, m_w_ukv, m_mla_norm_w, m_w_out, m_norm_ffn2, m_ffn2_w_gate, m_ffn2_w_up, m_ffn2_w_down, m_norm_final, v_w_ada, v_b_ada, v_norm_ffn1, v_ffn1_w_gate, v_ffn1_w_up, v_ffn1_w_down, v_norm_mix, v_w_in, v_conv_w, v_conv_b, v_dt_bias, v_a_log, v_d_skip, v_ssd_norm_w, v_q_norm_w, v_w_uq, v_kv_norm_w, v_w_ukv, v_mla_norm_w, v_w_out, v_norm_ffn2, v_ffn2_w_gate, v_ffn2_w_up, v_ffn2_w_down, v_norm_final):
    a = dict(locals())
    held_transposed = ("ffn1_w_gate", "ffn1_w_up", "ffn2_w_gate", "ffn2_w_up", "w_in")
    for n in held_transposed:
        for p in ("", "m_", "v_"):
            a[p + n] = a[p + n].transpose(0, 2, 1)
    B, S, D = x.shape
    mx, my, mc = _place()
    chip = 2 * mx + my
    dev = 2 * chip + mc
    core = mc.astype(jnp.int32).reshape(1)
    chip_id = chip.astype(jnp.int32).reshape(1)

    cw_rows = jnp.pad(conv_w[0], ((0, 0), (0, D - conv_w.shape[2])))
    got, _ = _all_gather_small(jnp.concatenate([c, cw_rows, jnp.zeros((8 - B - CONV_WIDTH, D), F32)], axis=0), "gather_c")
    c_all = got[:, :B, :].reshape(N_DEV * B, D)
    conv_full = got[::2, B:B + CONV_WIDTH, :conv_w.shape[2]].transpose(1, 0, 2).reshape(CONV_WIDTH, D_CONV)

    b_cols = lax.dynamic_slice(b_ada, (0, chip * ADA_COLS), (1, ADA_COLS))
    mod_all, mod_done = _all_gather_small(_ada_fwd(c_all, w_ada[0], b_cols), "gather_mod")
    mod = lax.dynamic_slice(mod_all, (0, B * dev, 0), (N_DEV, B, ADA_COLS))[::2].transpose(1, 0, 2).reshape(B, N_MOD * D)

    first = ("ffn1_w_gate", "ffn1_w_up", "ffn1_w_down")
    later = tuple(n for n in BIG_NAMES if n not in first)
    got_first, gathered = _gather_weights([(a[n][0] + mod_done[0, 0]).astype(BF16) for n in first])
    w = dict(zip(first, got_first))
    in_flight = _gather_start([(a[n][0] + gathered[0, 0]).astype(BF16) for n in later])

    def later_weights(after):
        send_sems, recv_sems, shards, lands, _ = in_flight
        lands = _gather_wait(send_sems, recv_sems, shards, lands, after)
        wl = dict(zip(later, _gather_finish([a[n][0].astype(BF16) for n in later], lands)))
        for n, to_kernel in _TO_KERNEL.items():
            wl[n] = to_kernel(wl[n].reshape(-1, D) if n in held_transposed else _columns_joined(wl[n]))
        wl["w_out"] = wl["w_out"].reshape(D_SSD + D_MLA, D)
        return wl

    small = {n: a[n].reshape(1, -1) for n in SMALL_NAMES if n not in ("conv_w", "norm_final")}
    small["conv_w"], small["norm_final"] = conv_full, norm_final

    def shards_of(names, gw):
        g4 = []
        for n in names:
            g = gw[n]
            if n in _FROM_KERNEL:
                g = _FROM_KERNEL[n](g) if n in held_transposed else _columns_split(_FROM_KERNEL[n](g))
            g4.append(g.reshape(N_CHIPS, a[n].shape[1], a[n].shape[2]))
        return g4

    def scatter_group(names, g4, swapped):
        pair = [_pair_sum(g, got, core, "pair_sum_" + n) for n, g, got in zip(names, g4, swapped)]
        return (names,) + tuple(_scatter_start(pair, names[0]))

    grads, deltas, new_m, new_v = {}, {}, {}, {}

    def finish_groups(some, after):
        names, mine = [], []
        for group_names, send_sems, recv_sems, pair, lands, _ in some:
            pair, lands = _scatter_wait(send_sems, recv_sems, pair, lands, after, group_names[0])
            names += group_names
            mine += [_chip_sum(own, got, chip_id, "chip_sum_" + n) for n, own, got in zip(group_names, pair, lands)]
        for n, own, other in zip(names, mine, _join_halves(mine, "join_" + names[0])):
            grads[n], deltas[n], new_m[n], new_v[n] = _adam_halves(a[n], a["m_" + n], a["v_" + n], own, other, core, "adam_" + n)
        return deltas[names[-1]]

    swapping, groups = [], []

    def on_grads(names, gw):
        send_sems, recv_sems, g4, lands, token = _swap_start(shards_of(names, gw), names[0])
        swapping.append((names, send_sems, recv_sems, g4, lands))
        return token[0, 0]

    def sync(after):
        token = 0.0
        while swapping:
            names, send_sems, recv_sems, g4, lands = swapping.pop(0)
            g4, swapped = _swap_wait(send_sems, recv_sems, g4, lands, after, names[0])
            groups.append(scatter_group(names, g4, swapped))
            token = groups[-1][5][0, 0]
        return token

    loss_blk, grad_x, gw, dmod, gs = _local_step(x, positions, mod + in_flight[4][0, 0], w, later_weights, small, loss_target,
                                                 on_grads, sync)

    small_flat = _pack_small(gs).at[-1].set(loss_blk[0, 0])
    send = jnp.concatenate([dmod.reshape(MOD_ROWS, D), small_flat.reshape(SMALL_ROWS, D),
                            jnp.zeros((SEND_ROWS - MOD_ROWS - SMALL_ROWS, D), F32)], axis=0)
    got, _ = _all_gather_small(send, "gather_small")
    summed = _small_sum(got)
    sums = summed[N_MOD:N_MOD + SMALL_ROWS].reshape(-1)
    loss = sums[-1]
    gsmall = _unpack_small(sums)
    gsmall["conv_w"] = lax.dynamic_slice(gsmall["conv_w"].reshape(CONV_WIDTH, D_CONV), (0, chip * conv_w.shape[2]),
                                         (CONV_WIDTH, conv_w.shape[2]))
    gsmall["b_ada"] = summed[:N_MOD]
    names = ("b_ada",) + SMALL_NAMES
    rows = 208

    def pack(parts):
        flat = jnp.concatenate([parts[n].reshape(-1) for n in names])
        return jnp.pad(flat, (0, rows * LANES - flat.shape[0])).reshape(rows, LANES)

    packed = [pack({n: a[p + n] for n in names}) for p in ("", "m_", "v_")]
    g_p = pack(gsmall)
    outs = (g_p,) + tuple(_adam(packed[0], g_p, packed[1], packed[2], "adam_small"))
    for dst, flat in zip((grads, deltas, new_m, new_v), outs):
        flat, off = flat.reshape(-1), 0
        for n in names:
            dst[n] = flat[off:off + a[n].size].reshape(a[n].shape)
            off += a[n].size

    dmod_all = got[:, :MOD_ROWS, :].reshape(N_DEV * B, N_MOD * D)
    dmod_cols = lax.dynamic_slice(dmod_all, (0, chip * ADA_COLS), (N_DEV * B, ADA_COLS))
    ada = _ada_bwd(c_all, dmod_cols, w_ada[0], m_w_ada[0], v_w_ada[0])
    for dst, t in zip((grads, deltas, new_m, new_v), ada):
        dst["w_ada"] = t[None]

    g4 = shards_of(first, gw)
    last = scatter_group(first, g4, _swap_halves(g4, summed, "swap_" + first[0]))
    finish_groups([last], finish_groups(groups, last[5]))
    for dst in (grads, deltas, new_m, new_v):
        for n in held_transposed:
            dst[n] = dst[n].transpose(0, 2, 1)

    order = ("w_ada", "b_ada", "norm_ffn1", "ffn1_w_gate", "ffn1_w_up", "ffn1_w_down", "norm_mix", "w_in", "conv_w", "conv_b",
             "dt_bias", "a_log", "d_skip", "ssd_norm_w", "q_norm_w", "w_uq", "kv_norm_w", "w_ukv", "mla_norm_w", "w_out",
             "norm_ffn2", "ffn2_w_gate", "ffn2_w_up", "ffn2_w_down", "norm_final")
    return (loss, grad_x, *[grads[n] for n in order], *[deltas[n] for n in order], *[new_m[n] for n in order],
            *[new_v[n] for n in order])
```

```python
import functools
import math

import jax
import jax.numpy as jnp
import numpy as np
from jax import lax
from jax.experimental import pallas as pl
from jax.experimental.pallas import tpu as pltpu

F32 = jnp.float32
BF16 = jnp.bfloat16

D_MODEL = 1024
D_FF = 2816
D_SSD = 1024
D_MLA = 1024
SSD_HEADS = 16
SSD_HEAD_DIM = 64
SSD_GROUPS = 2
SSD_STATE = 128
CONV_WIDTH = 4
CHUNK = 128
MLA_HEADS = 8
QK_NOPE = 64
QK_ROPE = 32
QK_DIM = QK_NOPE + QK_ROPE
V_HEAD = 128
Q_LORA = 384
KV_LORA = 256
ROPE_THETA = 10000.0
N_MOD = 9
EPS = 1e-6
D_CONV = D_SSD + 2 * SSD_GROUPS * SSD_STATE
D_PROJ = 3328
HEAD_LANES = 128
ADAM_LR = 0.001
ADAM_B1 = 0.9
ADAM_B2 = 0.999
ADAM_EPS = 1e-08
ADAM_WD = 0.01
ADAM_STEP = 10

LANES = 128
VMEM_LIMIT = 56 * 1024 * 1024
TOKEN_TILE = 512
WIDE_TOKEN_TILE = 1024
ATTN_FWD_Q_TILE = 1024
ATTN_FWD_KV_TILE = 1024
ATTN_BWD_TILE = 1024
N_CHIPS = 4
N_DEV = 8

MESH = pl.DeviceIdType.MESH


def _dot(a, b):
    return jnp.dot(a, b, preferred_element_type=F32)


def _dot_nt(a, b):
    return lax.dot_general(a, b, (((1,), (1,)), ((), ())), preferred_element_type=F32)


def _dot_tn(a, b):
    return lax.dot_general(a, b, (((0,), (0,)), ((), ())), preferred_element_type=F32)


def _cparams(semantics):
    return pltpu.CompilerParams(dimension_semantics=semantics, vmem_limit_bytes=VMEM_LIMIT)


def _resident(shape):
    zeros = (0,) * len(shape)
    return pl.BlockSpec(shape, lambda *_: zeros, pipeline_mode=pl.Buffered(1))


def _sigmoid(x):
    return jax.nn.sigmoid(x)


def _rms_stats(x):
    r = lax.rsqrt(jnp.mean(x * x, axis=-1, keepdims=True) + EPS)
    return x * r, r


def _rms_bwd(dn, xh, r, w):
    dxh = dn * w
    dx = r * (dxh - xh * jnp.mean(dxh * xh, axis=-1, keepdims=True))
    return dx, dn * xh


def _colsum(v):
    return jnp.sum(v, axis=0, keepdims=True)


def _ffn_fwd(x, nw, sh, sc, g, wg, wu, wd, seq, name, head=None):
    T, D = x.shape
    fs = wg.shape[1]
    tm = min(TOKEN_TILE, seq)
    tps = seq // tm

    def body(x_ref, nw_ref, sh_ref, sc_ref, g_ref, wg_ref, wu_ref, wd_ref, *rest):
        if head is None:
            xo_ref, a_ref, u_ref, f_ref = rest
        else:
            nf_ref, t_ref, xo_ref, a_ref, u_ref, f_ref, loss_ref, dnf_ref = rest

            @pl.when(pl.program_id(0) == 0)
            def _():
                loss_ref[...] = jnp.zeros_like(loss_ref)
                dnf_ref[...] = jnp.zeros_like(dnf_ref)

        xv = x_ref[...]
        xh, _ = _rms_stats(xv)
        h = (xh * nw_ref[...]) * (1.0 + sc_ref[0]) + sh_ref[0]
        hb = h.astype(BF16)
        f = jnp.zeros((tm, D), F32)
        for j in range(N_CHIPS):
            a = _dot_nt(hb, wg_ref[j])
            u = _dot_nt(hb, wu_ref[j])
            a_ref[j] = a.astype(BF16)
            u_ref[j] = u.astype(BF16)
            f = f + _dot((a * _sigmoid(a) * u).astype(BF16), wd_ref[j])
        f_ref[...] = f.astype(BF16)
        xo = xv + 0.5 * g_ref[0] * f
        if head is None:
            xo_ref[...] = xo
        else:
            xh, r = _rms_stats(xo)
            nfv = nf_ref[...]
            err = xh * nfv - t_ref[...]
            loss_ref[...] += (0.5 / D) * jnp.sum(err * err)
            dxo, dw_rows = _rms_bwd(err * (1.0 / D), xh, r, nfv)
            xo_ref[...] = dxo
            dnf_ref[...] += _colsum(dw_rows)

    rows = lambda n: pl.BlockSpec((tm, n), lambda i: (i, 0))
    act = pl.BlockSpec((N_CHIPS, tm, fs), lambda i: (0, i, 0))
    perb = pl.BlockSpec((1, 1, D), lambda i: (i // tps, 0, 0))
    sd = jax.ShapeDtypeStruct
    in_specs = [rows(D), _resident((1, D)), perb, perb, perb, _resident((N_CHIPS, fs, D)), _resident((N_CHIPS, fs, D)),
                _resident((N_CHIPS, fs, D))]
    out_specs = [rows(D), act, act, rows(D)]
    out_shape = [sd((T, D), F32), sd((N_CHIPS, T, fs), BF16), sd((N_CHIPS, T, fs), BF16), sd((T, D), BF16)]
    if head is not None:
        in_specs += [_resident((1, D)), rows(D)]
        out_specs += [pl.BlockSpec((8, LANES), lambda i: (0, 0)), pl.BlockSpec((1, D), lambda i: (0, 0))]
        out_shape += [sd((8, LANES), F32), sd((1, D), F32)]
    return pl.pallas_call(
        body, grid=(T // tm,), name=name, in_specs=in_specs, out_specs=out_specs, out_shape=out_shape,
        compiler_params=_cparams(("arbitrary",)),
    )(x, nw, sh, sc, g, wg, wu, wd, *(head or ()))


def _ffn_bwd(dxo, x, nw, sh, sc, g, a, u, f, wg, wu, wd, seq, name):
    T, D = x.shape
    fs = wg.shape[1]
    B = T // seq
    tm = min(TOKEN_TILE // 2, seq)
    tps = seq // tm

    def body(dxo_ref, x_ref, nw_ref, sh_ref, sc_ref, g_ref, a_ref, u_ref, f_ref, wg_ref, wu_ref, wd_ref,
             dx_ref, h_ref, s_ref, df_ref, da_ref, du_ref, dsh_ref, dsc_ref, dg_ref, dnw_ref):
        i = pl.program_id(0)

        @pl.when(i % tps == 0)
        def _():
            dsh_ref[...] = jnp.zeros_like(dsh_ref)
            dsc_ref[...] = jnp.zeros_like(dsc_ref)
            dg_ref[...] = jnp.zeros_like(dg_ref)

        @pl.when(i == 0)
        def _():
            dnw_ref[...] = jnp.zeros_like(dnw_ref)

        dxo_v = dxo_ref[...]
        dfb = (0.5 * g_ref[0] * dxo_v).astype(BF16)
        dg_ref[0] += _colsum(0.5 * dxo_v * f_ref[...].astype(F32))
        dh = jnp.zeros((tm, D), F32)
        for j in range(N_CHIPS):
            ds = _dot_nt(dfb, wd_ref[j])
            av = a_ref[j].astype(F32)
            uv = u_ref[j].astype(F32)
            sig = _sigmoid(av)
            sil = av * sig
            dab = (ds * uv * (sig * (1.0 + av * (1.0 - sig)))).astype(BF16)
            dub = (ds * sil).astype(BF16)
            dh = dh + _dot(dab, wg_ref[j]) + _dot(dub, wu_ref[j])
            s_ref[j] = (sil * uv).astype(BF16)
            da_ref[j] = dab
            du_ref[j] = dub
        xv = x_ref[...]
        xh, r = _rms_stats(xv)
        nwv = nw_ref[...]
        n = xh * nwv
        scale1 = 1.0 + sc_ref[0]
        dsc_ref[0] += _colsum(dh * n)
        dsh_ref[0] += _colsum(dh)
        dx, dw_rows = _rms_bwd(dh * scale1, xh, r, nwv)
        dnw_ref[...] += _colsum(dw_rows)
        dx_ref[...] = dxo_v + dx
        h_ref[...] = (n * scale1 + sh_ref[0]).astype(BF16)
        df_ref[...] = dfb

    rows = lambda n: pl.BlockSpec((tm, n), lambda i: (i, 0))
    act = pl.BlockSpec((N_CHIPS, tm, fs), lambda i: (0, i, 0))
    perb = pl.BlockSpec((1, 1, D), lambda i: (i // tps, 0, 0))
    sd = jax.ShapeDtypeStruct
    return pl.pallas_call(
        body, grid=(T // tm,), name=name,
        in_specs=[rows(D), rows(D), _resident((1, D)), perb, perb, perb, act, act, rows(D),
                  _resident((N_CHIPS, fs, D)), _resident((N_CHIPS, fs, D)), _resident((N_CHIPS, fs, D))],
        out_specs=[rows(D), rows(D), act, rows(D), act, act, perb, perb, perb, pl.BlockSpec((1, D), lambda i: (0, 0))],
        out_shape=[sd((T, D), F32), sd((T, D), BF16), sd((N_CHIPS, T, fs), BF16), sd((T, D), BF16),
                   sd((N_CHIPS, T, fs), BF16), sd((N_CHIPS, T, fs), BF16), sd((B, 1, D), F32), sd((B, 1, D), F32),
                   sd((B, 1, D), F32), sd((1, D), F32)],
        compiler_params=_cparams(("arbitrary",)),
    )(dxo, x, nw, sh, sc, g, a, u, f, wg, wu, wd)


def _ffn_wgrad(h, s, df, da, du, after, name):
    T, D = h.shape
    fs = s.shape[2]
    tt = min(TOKEN_TILE, T)
    nt = T // tt

    def body(h_ref, s_ref, df_ref, da_ref, du_ref, after_ref, dgate_ref, dup_ref, ddown_ref, gate_acc, up_acc, down_acc):
        @pl.when(pl.program_id(1) == 0)
        def _():
            gate_acc[...] = jnp.zeros_like(gate_acc)
            up_acc[...] = jnp.zeros_like(up_acc)
            down_acc[...] = jnp.zeros_like(down_acc)

        hv = h_ref[...]
        gate_acc[...] += _dot_tn(da_ref[0], hv)
        up_acc[...] += _dot_tn(du_ref[0], hv)
        down_acc[...] += _dot_tn(s_ref[0], df_ref[...])

        @pl.when(pl.program_id(1) == nt - 1)
        def _():
            dgate_ref[0] = gate_acc[...].astype(BF16)
            dup_ref[0] = up_acc[...].astype(BF16)
            ddown_ref[0] = down_acc[...].astype(BF16)

    rows = pl.BlockSpec((tt, D), lambda j, t: (t, 0))
    act = pl.BlockSpec((1, tt, fs), lambda j, t: (j, t, 0))
    shard = pl.BlockSpec((1, fs, D), lambda j, t: (j, 0, 0))
    return pl.pallas_call(
        body, grid=(N_CHIPS, nt), name=name,
        in_specs=[rows, act, rows, act, act, pl.BlockSpec(memory_space=pl.ANY)],
        out_specs=[shard] * 3, out_shape=[jax.ShapeDtypeStruct((N_CHIPS, fs, D), BF16)] * 3,
        scratch_shapes=[pltpu.VMEM((fs, D), F32)] * 3,
        compiler_params=_cparams(("arbitrary", "arbitrary")),
    )(h, s, df, da, du, after)


def _mm_tn(xa, ya, tn, name):
    T, K = xa.shape
    N = ya.shape[1]
    tt = min(WIDE_TOKEN_TILE, T)
    nt = T // tt

    def body(x_ref, y_ref, o_ref, acc_ref):
        @pl.when(pl.program_id(1) == 0)
        def _():
            acc_ref[...] = jnp.zeros_like(acc_ref)

        acc_ref[...] += _dot_tn(x_ref[...], y_ref[...])

        @pl.when(pl.program_id(1) == nt - 1)
        def _():
            o_ref[...] = acc_ref[...].astype(BF16)

    return pl.pallas_call(
        body, grid=(N // tn, nt), name=name,
        in_specs=[pl.BlockSpec((tt, K), lambda j, t: (t, 0)), pl.BlockSpec((tt, tn), lambda j, t: (t, j))],
        out_specs=pl.BlockSpec((K, tn), lambda j, t: (0, j)),
        out_shape=jax.ShapeDtypeStruct((K, N), BF16),
        scratch_shapes=[pltpu.VMEM((K, tn), F32)],
        compiler_params=_cparams(("arbitrary", "arbitrary")),
    )(xa, ya)


_PROJ_SPLITS = (0, 1024, 2560, 2944, 3200, 3328)


def _inproj_fwd(x, nw, sh, sc, win, seq):
    T, D = x.shape
    tm = min(WIDE_TOKEN_TILE, seq)
    tps = seq // tm
    widths = [b - a for a, b in zip(_PROJ_SPLITS[:-1], _PROJ_SPLITS[1:])]
    dtypes = [BF16, BF16, F32, F32, F32]

    def body(x_ref, nw_ref, sh_ref, sc_ref, w_ref, *outs):
        xh, _ = _rms_stats(x_ref[...])
        h = (xh * nw_ref[...]) * (1.0 + sc_ref[0]) + sh_ref[0]
        proj = _dot_nt(h.astype(BF16), w_ref[...])
        for o, lo, hi in zip(outs, _PROJ_SPLITS[:-1], _PROJ_SPLITS[1:]):
            o[...] = proj[:, lo:hi].astype(o.dtype)

    rows = lambda n: pl.BlockSpec((tm, n), lambda i: (i, 0))
    perb = pl.BlockSpec((1, 1, D), lambda i: (i // tps, 0, 0))
    return pl.pallas_call(
        body, grid=(T // tm,), name="inproj_fwd",
        in_specs=[rows(D), _resident((1, D)), perb, perb, _resident((D_PROJ, D))],
        out_specs=[rows(w) for w in widths],
        out_shape=[jax.ShapeDtypeStruct((T, w), dt) for w, dt in zip(widths, dtypes)],
        compiler_params=_cparams(("arbitrary",)),
    )(x, nw, sh, sc, win)


def _inproj_bwd(dx2, x, nw, sh, sc, win, dz, dxbc, dcq, dckv, ddtk_a, ddtk_b, seq):
    T, D = x.shape
    B = T // seq
    tm = min(TOKEN_TILE, seq)
    tps = seq // tm

    def body(dx2_ref, x_ref, nw_ref, sh_ref, sc_ref, w_ref, dz_ref, dxbc_ref, dcq_ref, dckv_ref, da_ref, db_ref,
             dx_ref, h_ref, dp_ref, dsh_ref, dsc_ref, dnw_ref):
        i = pl.program_id(0)

        @pl.when(i % tps == 0)
        def _():
            dsh_ref[...] = jnp.zeros_like(dsh_ref)
            dsc_ref[...] = jnp.zeros_like(dsc_ref)

        @pl.when(i == 0)
        def _():
            dnw_ref[...] = jnp.zeros_like(dnw_ref)

        dproj = jnp.concatenate(
            [dz_ref[...], dxbc_ref[...], dcq_ref[...].astype(BF16), dckv_ref[...].astype(BF16),
             (da_ref[...] + db_ref[...]).astype(BF16)], axis=1)
        dp_ref[...] = dproj
        dh = _dot(dproj, w_ref[...])
        xh, r = _rms_stats(x_ref[...])
        nwv = nw_ref[...]
        n = xh * nwv
        scale1 = 1.0 + sc_ref[0]
        dsc_ref[0] += _colsum(dh * n)
        dsh_ref[0] += _colsum(dh)
        dx, dw_rows = _rms_bwd(dh * scale1, xh, r, nwv)
        dnw_ref[...] += _colsum(dw_rows)
        dx_ref[...] = dx2_ref[...] + dx
        h_ref[...] = (n * scale1 + sh_ref[0]).astype(BF16)

    rows = lambda n: pl.BlockSpec((tm, n), lambda i: (i, 0))
    perb = pl.BlockSpec((1, 1, D), lambda i: (i // tps, 0, 0))
    sd = jax.ShapeDtypeStruct
    return pl.pallas_call(
        body, grid=(T // tm,), name="inproj_bwd",
        in_specs=[rows(D), rows(D), _resident((1, D)), perb, perb, _resident((D_PROJ, D)),
                  rows(1024), rows(D_CONV), rows(Q_LORA), rows(KV_LORA), rows(LANES), rows(LANES)],
        out_specs=[rows(D), rows(D), rows(D_PROJ), perb, perb, pl.BlockSpec((1, D), lambda i: (0, 0))],
        out_shape=[sd((T, D), F32), sd((T, D), BF16), sd((T, D_PROJ), BF16), sd((B, 1, D), F32), sd((B, 1, D), F32),
                   sd((1, D), F32)],
        compiler_params=_cparams(("arbitrary",)),
    )(dx2, x, nw, sh, sc, win, dz, dxbc, dcq, dckv, ddtk_a, ddtk_b)


SUBLANES = 8


def _shift_down(v, k):
    r = pltpu.roll(v, k, 0)
    row = lax.broadcasted_iota(jnp.int32, (SUBLANES, v.shape[1]), 0)
    return jnp.concatenate([jnp.where(row < k, 0.0, r[:SUBLANES]), r[SUBLANES:]], axis=0)


def _shift_up(v, k):
    n = v.shape[0]
    r = pltpu.roll(v, n - k, 0)
    row = lax.broadcasted_iota(jnp.int32, (SUBLANES, v.shape[1]), 0)
    return jnp.concatenate([r[:n - SUBLANES], jnp.where(row >= SUBLANES - k, 0.0, r[n - SUBLANES:])], axis=0)


def _conv_pre(xv, w_ref, b_ref):
    pre = b_ref[...] + w_ref[CONV_WIDTH - 1:CONV_WIDTH, :] * xv
    for k in range(1, CONV_WIDTH):
        pre = pre + w_ref[CONV_WIDTH - 1 - k:CONV_WIDTH - k, :] * _shift_down(xv, k)
    return pre


def _conv_fwd(xraw, cw, cb):
    B, S, C = xraw.shape

    def body(x_ref, w_ref, b_ref, o_ref):
        pre = _conv_pre(x_ref[0].astype(F32), w_ref, b_ref)
        o_ref[0] = (pre * _sigmoid(pre)).astype(BF16)

    blk = pl.BlockSpec((1, S, LANES), lambda b, j: (b, 0, j))
    return pl.pallas_call(
        body, grid=(B, C // LANES), name="conv_fwd",
        in_specs=[blk, pl.BlockSpec((CONV_WIDTH, LANES), lambda b, j: (0, j)), pl.BlockSpec((1, LANES), lambda b, j: (0, j))],
        out_specs=blk, out_shape=jax.ShapeDtypeStruct((B, S, C), BF16),
        compiler_params=_cparams(("arbitrary", "arbitrary")),
    )(xraw, cw, cb)


def _conv_bwd(dout, xraw, cw, cb):
    B, S, C = xraw.shape

    def body(d_ref, x_ref, w_ref, b_ref, dx_ref, dw_ref, db_ref):
        @pl.when(pl.program_id(1) == 0)
        def _():
            dw_ref[...] = jnp.zeros_like(dw_ref)
            db_ref[...] = jnp.zeros_like(db_ref)

        xv = x_ref[0].astype(F32)
        pre = _conv_pre(xv, w_ref, b_ref)
        sig = _sigmoid(pre)
        dpre = d_ref[0].astype(F32) * (sig * (1.0 + pre * (1.0 - sig)))
        dx = w_ref[CONV_WIDTH - 1:CONV_WIDTH, :] * dpre
        for k in range(1, CONV_WIDTH):
            dx = dx + w_ref[CONV_WIDTH - 1 - k:CONV_WIDTH - k, :] * _shift_up(dpre, k)
        dx_ref[0] = dx.astype(BF16)
        db_ref[...] += _colsum(dpre)
        dws = [_colsum(dpre * (xv if k == 0 else _shift_down(xv, k))) for k in range(CONV_WIDTH - 1, -1, -1)]
        dw_ref[...] += jnp.concatenate(dws, axis=0)

    blk = pl.BlockSpec((1, S, LANES), lambda j, b: (b, 0, j))
    wspec = pl.BlockSpec((CONV_WIDTH, LANES), lambda j, b: (0, j))
    bspec = pl.BlockSpec((1, LANES), lambda j, b: (0, j))
    return pl.pallas_call(
        body, grid=(C // LANES, B), name="conv_bwd",
        in_specs=[blk, blk, wspec, bspec], out_specs=[blk, wspec, bspec],
        out_shape=[jax.ShapeDtypeStruct((B, S, C), BF16), jax.ShapeDtypeStruct((CONV_WIDTH, C), F32),
                   jax.ShapeDtypeStruct((1, C), F32)],
        compiler_params=_cparams(("arbitrary", "arbitrary")),
    )(dout, xraw, cw, cb)


def _softplus(x):
    return jnp.maximum(x, 0.0) + jnp.log(1.0 + jnp.exp(-jnp.abs(x)))


def _ssd_common(xbc_ref, dtk_ref, dtb_ref, alog_ref, e_ref):
    L = CHUNK
    xbc = xbc_ref[0]
    xs = xbc[:, :D_SSD].astype(F32)
    bm = xbc[:, D_SSD:D_SSD + 256]
    cm = xbc[:, D_SSD + 256:D_SSD + 512]
    head = lax.broadcasted_iota(jnp.int32, (1, LANES), 1) < SSD_HEADS
    a128 = jnp.where(head, -jnp.exp(alog_ref[...]), 0.0)
    pre = dtk_ref[0] + dtb_ref[...]
    dt = _softplus(pre)
    dA = dt * a128
    row = lax.broadcasted_iota(jnp.int32, (L, L), 0)
    col = lax.broadcasted_iota(jnp.int32, (L, L), 1)
    causal = col <= row
    tri = causal.astype(F32)
    triT = (row <= col).astype(F32)
    tri = causal.astype(BF16)
    triT = (row <= col).astype(BF16)
    dA3 = _split3(dA)
    acum = _sum3(lambda part: _dot(tri, part), dA3)
    acumT = _sum3(lambda part: _dot_tn(part, triT), dA3)
    E = e_ref[...]
    acum_f = _spread(acum, E)
    dt_f = _spread(dt, E)
    e_f = jnp.exp(acum_f)
    w_f = jnp.exp(acum_f[L - 1:L, :] - acum_f)
    xt = xs * dt_f
    return dict(xs=xs, bm=bm, cm=cm, a128=a128, pre=pre, dt=dt, causal=causal, tri=tri, triT=triT, acum=acum,
                acumT=acumT, E=E, dt_f=dt_f, e_f=e_f, w_f=w_f, xt=xt, head=head)


def _split3(x):
    p1 = x.astype(BF16)
    r1 = x - p1.astype(F32)
    p2 = r1.astype(BF16)
    return p1, p2, (r1 - p2.astype(F32)).astype(BF16)


def _sum3(mm, parts):
    return (mm(parts[0]) + mm(parts[1])) + mm(parts[2])


def _spread(v, e):
    return _sum3(lambda part: _dot(part, e), _split3(v))


def _gather_heads(v, e):
    return _sum3(lambda part: _dot_nt(part, e), _split3(v))


def _head_mask(k):
    lane = lax.broadcasted_iota(jnp.int32, (CHUNK, LANES), 1)
    return (lane >= SSD_HEAD_DIM) if k == 1 else (lane < SSD_HEAD_DIM)


def _pair_decay(alast, h0):
    row = lax.broadcasted_iota(jnp.int32, (2 * SSD_HEAD_DIM, SSD_STATE), 0)
    return jnp.exp(jnp.where(row < SSD_HEAD_DIM, alast[:, h0:h0 + 1], alast[:, h0 + 1:h0 + 2]))


def _decay_matrix(q, h):
    seg = q["acum"][:, h:h + 1] - q["acumT"][h:h + 1, :]
    return jnp.exp(jnp.where(q["causal"], seg, -1e30))


def _gated_norm(y, zz, nw):
    sig = _sigmoid(zz)
    sil = zz * sig
    yg = y * sil
    half = D_SSD // SSD_GROUPS
    parts = []
    for g in range(SSD_GROUPS):
        xh, r = _rms_stats(yg[:, g * half:(g + 1) * half])
        parts.append((xh, r))
    return sig, sil, parts


def _ssd_fwd(xbc, dtk, z, dtb, alog, dsk, nw, expand):
    B, S, _ = xbc.shape
    L = CHUNK
    nc = S // L

    def body(xbc_ref, dtk_ref, z_ref, dtb_ref, alog_ref, dsk_ref, nw_ref, e_ref, y_ref, ys_ref, prev_ref, st_ref):
        @pl.when(pl.program_id(0) == 0)
        def _():
            st_ref[...] = jnp.zeros_like(st_ref)

        for b in range(B):
            one = lambda ref: ref.at[pl.ds(b, 1)]
            sequence_step(one(xbc_ref), one(dtk_ref), one(z_ref), dtb_ref, alog_ref, dsk_ref, nw_ref, e_ref, one(y_ref),
                          one(ys_ref), one(prev_ref), st_ref.at[b])

    def sequence_step(xbc_ref, dtk_ref, z_ref, dtb_ref, alog_ref, dsk_ref, nw_ref, e_ref, y_ref, ys_ref, prev_ref, st_ref):
        q = _ssd_common(xbc_ref, dtk_ref, dtb_ref, alog_ref, e_ref)
        xtb = q["xt"].astype(BF16)
        xwb = (q["xt"] * q["w_f"]).astype(BF16)
        alast = q["acum"][L - 1:L, :]
        ys = []
        for g in range(SSD_GROUPS):
            bg = q["bm"][:, g * 128:(g + 1) * 128]
            cg = q["cm"][:, g * 128:(g + 1) * 128]
            G = _dot_nt(cg, bg)
            for pr in range(SSD_HEADS // SSD_GROUPS // 2):
                h0 = g * 8 + 2 * pr
                lo = h0 * SSD_HEAD_DIM
                xt_p = xtb[:, lo:lo + 128]
                ydiag = jnp.zeros((L, LANES), F32)
                for k in range(2):
                    M = (G * _decay_matrix(q, h0 + k)).astype(BF16)
                    ydiag = ydiag + _dot(M, jnp.where(_head_mask(k), xt_p, jnp.zeros_like(xt_p)))
                hp = st_ref[lo:lo + 128, :]
                prev_ref[0, 0, lo:lo + 128, :] = hp.astype(BF16)
                zoff = _dot_nt(cg, hp.astype(BF16))
                ys.append(ydiag + zoff * q["e_f"][:, lo:lo + 128])
                st_ref[lo:lo + 128, :] = _pair_decay(alast, h0) * hp + _dot_tn(xwb[:, lo:lo + 128], bg)
        y = jnp.concatenate(ys, axis=1) + dsk_ref[...] * q["xs"]
        y_ref[0] = y.astype(BF16)
        _, _, parts = _gated_norm(y, z_ref[0].astype(F32), nw_ref[...])
        half = D_SSD // SSD_GROUPS
        ys_ref[0] = jnp.concatenate(
            [xh * nw_ref[:, g * half:(g + 1) * half] for g, (xh, _) in enumerate(parts)], axis=1).astype(BF16)

    chunk = lambda n: pl.BlockSpec((B, L, n), lambda c: (0, c, 0))
    vec = pl.BlockSpec((1, LANES), lambda c: (0, 0))
    return pl.pallas_call(
        body, grid=(nc,), name="ssd_fwd",
        in_specs=[chunk(D_CONV), chunk(LANES), chunk(D_SSD), vec, vec, pl.BlockSpec((1, D_SSD), lambda c: (0, 0)),
                  pl.BlockSpec((1, D_SSD), lambda c: (0, 0)), pl.BlockSpec((LANES, D_SSD), lambda c: (0, 0))],
        out_specs=[chunk(D_SSD), chunk(D_SSD), pl.BlockSpec((B, 1, D_SSD, SSD_STATE), lambda c: (0, c, 0, 0))],
        out_shape=[jax.ShapeDtypeStruct((B, S, D_SSD), BF16), jax.ShapeDtypeStruct((B, S, D_SSD), BF16),
                   jax.ShapeDtypeStruct((B, nc, D_SSD, SSD_STATE), BF16)],
        scratch_shapes=[pltpu.VMEM((B, D_SSD, SSD_STATE), F32)],
        compiler_params=_cparams(("arbitrary",)),
    )(xbc, dtk, z, dtb, alog, dsk, nw, expand)


def _ssd_bwd(xbc, dtk, z, y, prev, dys, dtb, alog, dsk, nw, expand):
    B, S, _ = xbc.shape
    L = CHUNK
    nc = S // L
    half = D_SSD // SSD_GROUPS

    def body(xbc_ref, dtk_ref, z_ref, y_ref, prev_ref, dys_ref, dtb_ref, alog_ref, dsk_ref, nw_ref, e_ref,
             dxbc_ref, ddtk_ref, dz_ref, dnw_ref, dvec_ref, dh_ref, dskc_ref):
        @pl.when(pl.program_id(0) == 0)
        def _():
            dnw_ref[...] = jnp.zeros_like(dnw_ref)
            dvec_ref[...] = jnp.zeros_like(dvec_ref)
            dskc_ref[...] = jnp.zeros_like(dskc_ref)
            dh_ref[...] = jnp.zeros_like(dh_ref)

        for b in range(B):
            one = lambda ref: ref.at[pl.ds(b, 1)]
            sequence_step(one(xbc_ref), one(dtk_ref), one(z_ref), one(y_ref), one(prev_ref), one(dys_ref), dtb_ref, alog_ref,
                          dsk_ref, nw_ref, e_ref, one(dxbc_ref), one(ddtk_ref), one(dz_ref), dnw_ref, dvec_ref, dh_ref.at[b],
                          dskc_ref)

        @pl.when(pl.program_id(0) == nc - 1)
        def _():
            dvec_ref[2:3, :] = _gather_heads(jnp.broadcast_to(dskc_ref[...], (8, D_SSD)), e_ref[...])[0:1, :]

    def sequence_step(xbc_ref, dtk_ref, z_ref, y_ref, prev_ref, dys_ref, dtb_ref, alog_ref, dsk_ref, nw_ref, e_ref,
                      dxbc_ref, ddtk_ref, dz_ref, dnw_ref, dvec_ref, dh_ref, dskc_ref):
        q = _ssd_common(xbc_ref, dtk_ref, dtb_ref, alog_ref, e_ref)
        E = q["E"]
        xs = q["xs"]
        yv = y_ref[0].astype(F32)
        zz = z_ref[0].astype(F32)
        sig, sil, parts = _gated_norm(yv, zz, nw_ref[...])
        dn = dys_ref[0].astype(F32)
        dyg, dnw_rows = [], []
        for g, (xh, r) in enumerate(parts):
            dpart, dw_rows = _rms_bwd(dn[:, g * half:(g + 1) * half], xh, r, nw_ref[:, g * half:(g + 1) * half])
            dyg.append(dpart)
            dnw_rows.append(dw_rows)
        dyg = jnp.concatenate(dyg, axis=1)
        dnw_ref[...] += _colsum(jnp.concatenate(dnw_rows, axis=1))
        dY = dyg * sil
        dz_ref[0] = (dyg * yv * (sig * (1.0 + zz * (1.0 - sig)))).astype(BF16)
        dsk_f = dsk_ref[...]
        dskc_ref[...] += _colsum(dY * xs)
        dYb = dY.astype(BF16)
        xtb = q["xt"].astype(BF16)
        xwb = (q["xt"] * q["w_f"]).astype(BF16)
        acum = q["acum"]
        alast = acum[L - 1:L, :]
        lane_id = lax.broadcasted_iota(jnp.int32, (L, LANES), 1)
        sub_id = lax.broadcasted_iota(jnp.int32, (LANES, L), 0)
        lane_row = lax.broadcasted_iota(jnp.int32, (1, LANES), 1)
        da_rows = jnp.zeros((L, LANES), F32)
        daT = jnp.zeros((LANES, L), F32)
        dxt, prod_off, prod_st, dbs, dcs = [], [], [], [], []
        hsum_row = jnp.zeros((1, LANES), F32)
        for g in range(SSD_GROUPS):
            bg = q["bm"][:, g * 128:(g + 1) * 128]
            cg = q["cm"][:, g * 128:(g + 1) * 128]
            G = _dot_nt(cg, bg)
            dG = jnp.zeros((L, L), F32)
            dcg = jnp.zeros((L, SSD_STATE), F32)
            dbg = jnp.zeros((L, SSD_STATE), F32)
            for pr in range(SSD_HEADS // SSD_GROUPS // 2):
                h0 = g * 8 + 2 * pr
                lo = h0 * SSD_HEAD_DIM
                cols = slice(lo, lo + 128)
                dY_p = dYb[:, cols]
                xt_p = xtb[:, cols]
                dxt_p = jnp.zeros((L, LANES), F32)
                for k in range(2):
                    h = h0 + k
                    Lm = _decay_matrix(q, h)
                    Mf = G * Lm
                    dYk = jnp.where(_head_mask(k), dY_p, jnp.zeros_like(dY_p))
                    dM = _dot_nt(dYk, xt_p)
                    dxt_p = dxt_p + _dot_tn(Mf.astype(BF16), dYk)
                    dG = dG + dM * Lm
                    Q = dM * Mf
                    da_rows = da_rows + jnp.where(lane_id == h, jnp.sum(Q, axis=1, keepdims=True), 0.0)
                    daT = daT + jnp.where(sub_id == h, jnp.sum(Q, axis=0, keepdims=True), 0.0)
                hpb = prev_ref[0, 0, lo:lo + 128, :]
                hp = hpb.astype(F32)
                zoff = _dot_nt(cg, hpb)
                e_p = q["e_f"][:, cols]
                dY_pf = dY[:, cols]
                dZb = (dY_pf * e_p).astype(BF16)
                dcg = dcg + _dot(dZb, hpb)
                dhp_off = _dot_tn(dZb, cg)
                prod_off.append(dY_pf * zoff * e_p)
                dS = dh_ref[lo:lo + 128, :]
                dSb = dS.astype(BF16)
                U = _dot_nt(bg, dSb)
                dxt_p = dxt_p + U * q["w_f"][:, cols]
                dbg = dbg + _dot(xwb[:, cols], dSb)
                prod_st.append(q["xt"][:, cols] * U)
                dh_ref[lo:lo + 128, :] = _pair_decay(alast, h0) * dS + dhp_off
                dsh = dS * hp
                for k in range(2):
                    total = jnp.sum(dsh[k * SSD_HEAD_DIM:(k + 1) * SSD_HEAD_DIM, :], axis=(0, 1), keepdims=True)
                    hsum_row = hsum_row + jnp.where(lane_row == h0 + k, total, 0.0)
                dxt.append(dxt_p)
            dGb = dG.astype(BF16)
            dcs.append(dcg + _dot(dGb, bg))
            dbs.append(dbg + _dot_tn(dGb, cg))
        dxt = jnp.concatenate(dxt, axis=1)
        da_rows = da_rows + _gather_heads(jnp.concatenate(prod_off, axis=1), E)
        dww = _gather_heads(jnp.concatenate(prod_st, axis=1), E) * jnp.exp(alast - acum)
        da_rows = da_rows - dww
        dlast = _colsum(dww) + jnp.exp(alast) * hsum_row
        triT = q["triT"]
        ddA = (_sum3(lambda part: _dot(triT, part), _split3(da_rows))
               - _sum3(lambda part: _dot_nt(triT, part), _split3(daT)) + dlast)
        ddA = jnp.where(q["head"], ddA, 0.0)
        ddt = ddA * q["a128"] + _gather_heads(dxt * xs, E)
        ddt_raw = jnp.where(q["head"], ddt * _sigmoid(q["pre"]), 0.0)
        ddtk_ref[0] = ddt_raw
        dxs = dxt * q["dt_f"] + dsk_f * dY
        dxbc_ref[0] = jnp.concatenate([dxs] + dbs + dcs, axis=1).astype(BF16)
        dvec_ref[0:1, :] += _colsum(ddt_raw)
        dvec_ref[1:2, :] += _colsum(ddA * q["dt"]) * q["a128"]

    rev = lambda n: pl.BlockSpec((B, L, n), lambda c: (0, nc - 1 - c, 0))
    vec = pl.BlockSpec((1, LANES), lambda c: (0, 0))
    sd = jax.ShapeDtypeStruct
    return pl.pallas_call(
        body, grid=(nc,), name="ssd_bwd",
        in_specs=[rev(D_CONV), rev(LANES), rev(D_SSD), rev(D_SSD),
                  pl.BlockSpec((B, 1, D_SSD, SSD_STATE), lambda c: (0, nc - 1 - c, 0, 0)), rev(D_SSD), vec, vec,
                  pl.BlockSpec((1, D_SSD), lambda c: (0, 0)),
                  pl.BlockSpec((1, D_SSD), lambda c: (0, 0)), pl.BlockSpec((LANES, D_SSD), lambda c: (0, 0))],
        out_specs=[rev(D_CONV), rev(LANES), rev(D_SSD), pl.BlockSpec((1, D_SSD), lambda c: (0, 0)),
                   pl.BlockSpec((8, LANES), lambda c: (0, 0))],
        out_shape=[sd((B, S, D_CONV), BF16), sd((B, S, LANES), F32), sd((B, S, D_SSD), BF16), sd((1, D_SSD), F32),
                   sd((8, LANES), F32)],
        scratch_shapes=[pltpu.VMEM((B, D_SSD, SSD_STATE), F32), pltpu.VMEM((1, D_SSD), F32)],
        compiler_params=_cparams(("arbitrary",)),
    )(xbc, dtk, z, y, prev, dys, dtb, alog, dsk, nw, expand)


def _rope_tables(pos_ref, invf_ref, place_ref):
    ang = invf_ref[...] * pos_ref[0].astype(F32)
    place = place_ref[...]
    cosf = 1.0 + _sum3(lambda part: _dot_tn(part, place), _split3(jnp.cos(ang) - 1.0))
    sinf = _sum3(lambda part: _dot_tn(part, place), _split3(jnp.sin(ang)))
    return cosf, sinf


def _rot(u):
    lane = lax.broadcasted_iota(jnp.int32, u.shape, 1)
    first = (lane >= QK_NOPE) & (lane < QK_NOPE + QK_ROPE // 2)
    second = (lane >= QK_NOPE + QK_ROPE // 2) & (lane < QK_DIM)
    return jnp.where(first, -pltpu.roll(u, LANES - QK_ROPE // 2, 1), jnp.where(second, pltpu.roll(u, QK_ROPE // 2, 1), 0.0))


def _rope_lanes(shape):
    lane = lax.broadcasted_iota(jnp.int32, shape, 1)
    return (lane >= QK_NOPE) & (lane < QK_DIM)


def _mla_prep(cq, ckv, dtk, pos, qw, kvw, wuq, wukv, invf, place):
    T = cq.shape[0]
    tm = min(WIDE_TOKEN_TILE, T)
    scale = 1.0 / math.sqrt(QK_DIM)
    HW = MLA_HEADS * HEAD_LANES

    def body(cq_ref, ckv_ref, dtk_ref, pos_ref, qw_ref, kvw_ref, wuq_ref, wukv_ref, invf_ref, place_ref, q_ref, k_ref, v_ref,
             cos_ref, sin_ref):
        xh, _ = _rms_stats(cq_ref[...])
        qv = _dot((xh * qw_ref[...]).astype(BF16), wuq_ref[...])
        xh, _ = _rms_stats(ckv_ref[...])
        kv = _dot((xh * kvw_ref[...]).astype(BF16), wukv_ref[...])
        cosf, sinf = _rope_tables(pos_ref, invf_ref, place_ref)
        cos_ref[...] = cosf
        sin_ref[...] = sinf
        rope = lambda u: u * cosf + _rot(u) * sinf
        dtkv = dtk_ref[...]
        kr = rope(jnp.where(_rope_lanes(dtkv.shape), dtkv, 0.0))
        for h in range(MLA_HEADS):
            cols = slice(h * HEAD_LANES, (h + 1) * HEAD_LANES)
            q_ref[:, cols] = (rope(qv[:, cols]) * scale).astype(BF16)
            k_ref[:, cols] = (kv[:, cols] + kr).astype(BF16)
        v_ref[...] = kv[:, HW:].astype(BF16)

    rows = lambda n: pl.BlockSpec((tm, n), lambda i: (i, 0))
    return pl.pallas_call(
        body, grid=(T // tm,), name="mla_prep",
        in_specs=[rows(Q_LORA), rows(KV_LORA), rows(LANES), pl.BlockSpec((1, 1, tm), lambda i: (i, 0, 0)),
                  _resident((1, Q_LORA)), _resident((1, KV_LORA)), _resident((Q_LORA, HW)), _resident((KV_LORA, 2 * HW)),
                  _resident((QK_ROPE // 2, 1)), _resident((QK_ROPE // 2, LANES))],
        out_specs=[rows(HW), rows(HW), rows(HW), rows(LANES), rows(LANES)],
        out_shape=[jax.ShapeDtypeStruct((T, HW), BF16)] * 3 + [jax.ShapeDtypeStruct((T, LANES), F32)] * 2,
        compiler_params=_cparams(("arbitrary",)),
    )(cq, ckv, dtk, pos.reshape(T // tm, 1, tm), qw, kvw, wuq, wukv, invf, place)


def _mla_prep_bwd(dq, dk, dv, cq, ckv, cos_t, sin_t, qw, kvw, wuq, wukv):
    T = cq.shape[0]
    tm = min(WIDE_TOKEN_TILE, T)
    scale = 1.0 / math.sqrt(QK_DIM)
    HW = MLA_HEADS * HEAD_LANES

    def body(dq_ref, dk_ref, dv_ref, cq_ref, ckv_ref, cos_ref, sin_ref, qw_ref, kvw_ref, wuq_ref, wukv_ref,
             dcq_ref, dckv_ref, ddtk_ref, qn_ref, kvn_ref, dqo_ref, dkvo_ref, dqw_ref, dkvw_ref):
        @pl.when(pl.program_id(0) == 0)
        def _():
            dqw_ref[...] = jnp.zeros_like(dqw_ref)
            dkvw_ref[...] = jnp.zeros_like(dkvw_ref)

        cosf, sinf = cos_ref[...], sin_ref[...]
        unrope = lambda d: d * cosf - _rot(d * sinf)
        dkr = jnp.zeros((tm, LANES), F32)
        nope = lax.broadcasted_iota(jnp.int32, (tm, LANES), 1) < QK_NOPE
        for h in range(MLA_HEADS):
            cols = slice(h * HEAD_LANES, (h + 1) * HEAD_LANES)
            dqo_ref[:, cols] = unrope(dq_ref[:, cols].astype(F32) * scale).astype(BF16)
            dkh = dk_ref[:, cols].astype(F32)
            dkr = dkr + jnp.where(_rope_lanes(dkh.shape), dkh, 0.0)
            dkvo_ref[:, cols] = jnp.where(nope, dkh, 0.0).astype(BF16)
        dkvo_ref[:, HW:] = dv_ref[...].astype(BF16)
        ddtk_ref[...] = unrope(dkr)
        xh, r = _rms_stats(cq_ref[...])
        qn_ref[...] = (xh * qw_ref[...]).astype(BF16)
        dx, dw_rows = _rms_bwd(_dot_nt(dqo_ref[...], wuq_ref[...]), xh, r, qw_ref[...])
        dcq_ref[...] = dx
        dqw_ref[...] += _colsum(dw_rows)
        xh, r = _rms_stats(ckv_ref[...])
        kvn_ref[...] = (xh * kvw_ref[...]).astype(BF16)
        dx, dw_rows = _rms_bwd(_dot_nt(dkvo_ref[...], wukv_ref[...]), xh, r, kvw_ref[...])
        dckv_ref[...] = dx
        dkvw_ref[...] += _colsum(dw_rows)

    rows = lambda n: pl.BlockSpec((tm, n), lambda i: (i, 0))
    sd = jax.ShapeDtypeStruct
    return pl.pallas_call(
        body, grid=(T // tm,), name="mla_prep_bwd",
        in_specs=[rows(HW), rows(HW), rows(HW), rows(Q_LORA), rows(KV_LORA), rows(LANES), rows(LANES), _resident((1, Q_LORA)),
                  _resident((1, KV_LORA)), _resident((Q_LORA, HW)), _resident((KV_LORA, 2 * HW))],
        out_specs=[rows(Q_LORA), rows(KV_LORA), rows(LANES), rows(Q_LORA), rows(KV_LORA), rows(HW), rows(2 * HW),
                   pl.BlockSpec((1, Q_LORA), lambda i: (0, 0)), pl.BlockSpec((1, KV_LORA), lambda i: (0, 0))],
        out_shape=[sd((T, Q_LORA), F32), sd((T, KV_LORA), F32), sd((T, LANES), F32), sd((T, Q_LORA), BF16),
                   sd((T, KV_LORA), BF16), sd((T, HW), BF16), sd((T, 2 * HW), BF16), sd((1, Q_LORA), F32),
                   sd((1, KV_LORA), F32)],
        compiler_params=_cparams(("arbitrary",)),
    )(dq, dk, dv, cq, ckv, cos_t, sin_t, qw, kvw, wuq, wukv)


def _causal_mask(t):
    row = lax.broadcasted_iota(jnp.int32, (t, t), 0)
    col = lax.broadcasted_iota(jnp.int32, (t, t), 1)
    return col <= row


def _attn_fwd(q, k, v):
    B, S, HW = q.shape
    H = HW // HEAD_LANES
    t = min(ATTN_FWD_Q_TILE, S)
    tk = min(ATTN_FWD_KV_TILE, t)
    nq = S // t
    per = t // tk

    pair = 4
    pw = pair * HEAD_LANES

    def body(q_ref, k_ref, v_ref, o_ref, lse_ref):
        qi = pl.program_id(2)
        lanes = [slice(hh * HEAD_LANES, (hh + 1) * HEAD_LANES) for hh in range(pair)]
        qs = [q_ref[0, :, cols] for cols in lanes]

        def step(j, carry, diag):
            sl = pl.ds(pl.multiple_of(j * tk, tk), tk)
            out = []
            for qv, cols, (m, l, acc) in zip(qs, lanes, carry):
                s = _dot_nt(qv, k_ref[0, sl, cols])
                if diag is not None:
                    row = lax.broadcasted_iota(jnp.int32, (t, tk), 0)
                    col = lax.broadcasted_iota(jnp.int32, (t, tk), 1)
                    s = jnp.where(col + diag * tk <= row, s, -1e30)
                m_new = jnp.maximum(m, jnp.max(s, axis=-1, keepdims=True))
                alpha = jnp.exp(m - m_new)
                p = jnp.exp(s - m_new)
                l = alpha * l + jnp.sum(p, axis=-1, keepdims=True)
                acc = alpha * acc + _dot(p.astype(BF16), v_ref[0, sl, cols])
                out.append((m_new, l, acc))
            return tuple(out)

        init = tuple((jnp.full((t, 1), -1e30, F32), jnp.zeros((t, 1), F32), jnp.zeros((t, HEAD_LANES), F32))
                     for _ in range(pair))
        carry = lax.fori_loop(0, qi * per, lambda j, c: step(j, c, None), init)
        for d in range(per):
            carry = step(qi * per + d, carry, d)
        for hh, (m, l, acc) in enumerate(carry):
            o_ref[0, :, lanes[hh]] = (acc / l).astype(BF16)
            lse_ref[0, hh] = m + jnp.log(l)

    return pl.pallas_call(
        body, grid=(B, H // pair, nq), name="attn_fwd",
        in_specs=[pl.BlockSpec((1, t, pw), lambda b, h, i: (b, i, h)),
                  pl.BlockSpec((1, S, pw), lambda b, h, i: (b, 0, h)),
                  pl.BlockSpec((1, S, pw), lambda b, h, i: (b, 0, h))],
        out_specs=[pl.BlockSpec((1, t, pw), lambda b, h, i: (b, i, h)),
                   pl.BlockSpec((1, pair, t, 1), lambda b, h, i: (b, h, i, 0))],
        out_shape=[jax.ShapeDtypeStruct((B, S, HW), BF16), jax.ShapeDtypeStruct((B, H, S, 1), F32)],
        compiler_params=_cparams(("arbitrary", "arbitrary", "arbitrary")),
    )(q, k, v)


def _attn_bwd(q, k, v, o, do, lse):
    B, S, HW = q.shape
    H = HW // HEAD_LANES
    t = min(ATTN_BWD_TILE, S)
    nq = S // t

    pair = 2
    pw = pair * HEAD_LANES

    def body(q_ref, k_ref, v_ref, o_ref, do_ref, lse_ref, dq_out_ref, dk_ref, dv_ref, dq_ref):
        j = pl.program_id(2)

        @pl.when(j == 0)
        def _():
            dq_ref[...] = jnp.zeros_like(dq_ref)

        lanes = [slice(hh * HEAD_LANES, (hh + 1) * HEAD_LANES) for hh in range(pair)]

        def step(i, carry, masked):
            sl = pl.ds(pl.multiple_of(i * t, t), t)
            out = []
            for hh, (cols, (dk, dv)) in enumerate(zip(lanes, carry)):
                kj = k_ref[0, :, cols]
                qi = q_ref[0, sl, cols]
                doi = do_ref[0, sl, cols]
                s = _dot_nt(qi, kj)
                if masked:
                    s = jnp.where(_causal_mask(t), s, -1e30)
                p = jnp.exp(s - lse_ref[0, hh, sl, :])
                dv = dv + _dot_tn(p.astype(BF16), doi)
                dp = _dot_nt(doi, v_ref[0, :, cols])
                delta = jnp.sum(doi.astype(F32) * o_ref[0, sl, cols].astype(F32), axis=-1, keepdims=True)
                dsb = (p * (dp - delta)).astype(BF16)
                dk = dk + _dot_tn(dsb, qi)
                dq_ref[sl, cols] += _dot(dsb, kj)
                out.append((dk, dv))
            return tuple(out)

        zero = jnp.zeros((t, HEAD_LANES), F32)
        carry = step(j, ((zero, zero),) * pair, True)
        carry = lax.fori_loop(j + 1, nq, lambda i, c: step(i, c, False), carry)
        for cols, (dk, dv) in zip(lanes, carry):
            dk_ref[0, :, cols] = dk.astype(BF16)
            dv_ref[0, :, cols] = dv.astype(BF16)

        @pl.when(j == nq - 1)
        def _():
            dq_out_ref[0] = dq_ref[...].astype(BF16)

    full = pl.BlockSpec((1, S, pw), lambda b, h, j: (b, 0, h))
    tile = pl.BlockSpec((1, t, pw), lambda b, h, j: (b, j, h))
    sd = jax.ShapeDtypeStruct
    return pl.pallas_call(
        body, grid=(B, H // pair, nq), name="attn_bwd",
        in_specs=[full, tile, tile, full, full, pl.BlockSpec((1, pair, S, 1), lambda b, h, j: (b, h, 0, 0))],
        out_specs=[full, tile, tile],
        out_shape=[sd((B, S, HW), BF16), sd((B, S, HW), BF16), sd((B, S, HW), BF16)],
        scratch_shapes=[pltpu.VMEM((S, pw), F32)],
        compiler_params=_cparams(("arbitrary", "arbitrary", "arbitrary")),
    )(q, k, v, o, do, lse)


def _mix_out(x1, yssd, o, mw, wout, g, seq):
    T, D = x1.shape
    tm = min(WIDE_TOKEN_TILE, seq)
    tps = seq // tm

    def body(x_ref, ys_ref, o_ref, mw_ref, w_ref, g_ref, xo_ref, m_ref, yc_ref):
        xh, _ = _rms_stats(o_ref[...].astype(F32))
        ycat = jnp.concatenate([ys_ref[...], (xh * mw_ref[...]).astype(BF16)], axis=1)
        m = _dot(ycat, w_ref[...])
        xo_ref[...] = x_ref[...] + g_ref[0] * m
        m_ref[...] = m.astype(BF16)
        yc_ref[...] = ycat

    rows = lambda n: pl.BlockSpec((tm, n), lambda i: (i, 0))
    perb = pl.BlockSpec((1, 1, D), lambda i: (i // tps, 0, 0))
    sd = jax.ShapeDtypeStruct
    return pl.pallas_call(
        body, grid=(T // tm,), name="mix_out",
        in_specs=[rows(D), rows(D_SSD), rows(D_MLA), _resident((1, D_MLA)), _resident((D_SSD + D_MLA, D)), perb],
        out_specs=[rows(D), rows(D), rows(D_SSD + D_MLA)],
        out_shape=[sd((T, D), F32), sd((T, D), BF16), sd((T, D_SSD + D_MLA), BF16)],
        compiler_params=_cparams(("arbitrary",)),
    )(x1, yssd, o, mw, wout, g)


def _mix_out_bwd(dx2, m, o, mw, wout, g, seq):
    T, D = dx2.shape
    B = T // seq
    tm = min(WIDE_TOKEN_TILE, seq)
    tps = seq // tm

    def body(dx_ref, m_ref, o_ref, mw_ref, w_ref, g_ref, dys_ref, do_ref, dm_ref, dg_ref, dmw_ref):
        i = pl.program_id(0)

        @pl.when(i % tps == 0)
        def _():
            dg_ref[...] = jnp.zeros_like(dg_ref)

        @pl.when(i == 0)
        def _():
            dmw_ref[...] = jnp.zeros_like(dmw_ref)

        dxv = dx_ref[...]
        dg_ref[0] += _colsum(dxv * m_ref[...].astype(F32))
        dmb = (g_ref[0] * dxv).astype(BF16)
        dm_ref[...] = dmb
        dycat = _dot_nt(dmb, w_ref[...])
        dys_ref[...] = dycat[:, :D_SSD].astype(BF16)
        xh, r = _rms_stats(o_ref[...].astype(F32))
        dx, dw_rows = _rms_bwd(dycat[:, D_SSD:], xh, r, mw_ref[...])
        do_ref[...] = dx.astype(BF16)
        dmw_ref[...] += _colsum(dw_rows)

    rows = lambda n: pl.BlockSpec((tm, n), lambda i: (i, 0))
    perb = pl.BlockSpec((1, 1, D), lambda i: (i // tps, 0, 0))
    sd = jax.ShapeDtypeStruct
    return pl.pallas_call(
        body, grid=(T // tm,), name="mix_out_bwd",
        in_specs=[rows(D), rows(D), rows(D_MLA), _resident((1, D_MLA)), _resident((D_SSD + D_MLA, D)), perb],
        out_specs=[rows(D_SSD), rows(D_MLA), rows(D), perb, pl.BlockSpec((1, D_MLA), lambda i: (0, 0))],
        out_shape=[sd((T, D_SSD), BF16), sd((T, D_MLA), BF16), sd((T, D), BF16), sd((B, 1, D), F32), sd((1, D_MLA), F32)],
        compiler_params=_cparams(("arbitrary",)),
    )(dx2, m, o, mw, wout, g)


def _win_to_kernel(w):
    z0 = jnp.zeros((48, w.shape[1]), w.dtype)
    z1 = jnp.zeros((32, w.shape[1]), w.dtype)
    return jnp.concatenate([w[:2560], w[2576:3216], w[2560:2576], z0, w[3216:3248], z1], axis=0)


def _win_from_kernel(g):
    return jnp.concatenate([g[:2560], g[3200:3216], g[2560:3200], g[3264:3296]], axis=0)


def _wuq_to_kernel(w):
    w = w.reshape(Q_LORA, MLA_HEADS, QK_DIM)
    return jnp.pad(w, ((0, 0), (0, 0), (0, HEAD_LANES - QK_DIM))).reshape(Q_LORA, MLA_HEADS * HEAD_LANES)


def _wuq_from_kernel(g):
    return g.reshape(Q_LORA, MLA_HEADS, HEAD_LANES)[:, :, :QK_DIM].reshape(Q_LORA, MLA_HEADS * QK_DIM)


def _wukv_to_kernel(w):
    w = w.reshape(KV_LORA, MLA_HEADS, QK_NOPE + V_HEAD)
    kp = jnp.pad(w[:, :, :QK_NOPE], ((0, 0), (0, 0), (0, HEAD_LANES - QK_NOPE)))
    return jnp.concatenate([kp.reshape(KV_LORA, -1), w[:, :, QK_NOPE:].reshape(KV_LORA, -1)], axis=1)


def _wukv_from_kernel(g):
    hw = MLA_HEADS * HEAD_LANES
    kp = g[:, :hw].reshape(KV_LORA, MLA_HEADS, HEAD_LANES)[:, :, :QK_NOPE]
    vp = g[:, hw:].reshape(KV_LORA, MLA_HEADS, V_HEAD)
    return jnp.concatenate([kp, vp], axis=2).reshape(KV_LORA, MLA_HEADS * (QK_NOPE + V_HEAD))


def _lanes16(v):
    return jnp.pad(v.reshape(1, SSD_HEADS), ((0, 0), (0, LANES - SSD_HEADS)))


def _constants():
    e = np.zeros((LANES, D_SSD), np.float32)
    for h in range(SSD_HEADS):
        e[h, h * SSD_HEAD_DIM:(h + 1) * SSD_HEAD_DIM] = 1.0
    inv_freq = ROPE_THETA ** (-jnp.arange(0, QK_ROPE, 2, dtype=F32) / QK_ROPE)
    half = QK_ROPE // 2
    place = np.zeros((half, LANES), np.float32)
    for j in range(half):
        place[j, QK_NOPE + j] = place[j, QK_NOPE + half + j] = 1.0
    return jnp.asarray(e, BF16), inv_freq.reshape(half, 1), jnp.asarray(place, BF16)


def _local_step(x, positions, mod, w, later_weights, small, tgt, on_grads, sync):
    B, S, D = x.shape
    T = B * S
    expand, invf, place = _constants()
    x0 = x.reshape(T, D)
    pos = positions.reshape(T)
    mods = [mod[:, i * D:(i + 1) * D].reshape(B, 1, D) for i in range(N_MOD)]
    sh1, sc1, g1, sh2, sc2, g2, sh3, sc3, g3 = mods
    dtb, alog = _lanes16(small["dt_bias"]), _lanes16(small["a_log"])
    dsk = jnp.repeat(small["d_skip"].reshape(1, SSD_HEADS), SSD_HEAD_DIM, axis=1)

    x1, a1, u1, f1 = _ffn_fwd(x0, small["norm_ffn1"], sh1, sc1, g1, w["ffn1_w_gate"], w["ffn1_w_up"], w["ffn1_w_down"], S, "ffn1_fwd")
    w = {**w, **later_weights(f1)}
    z, xraw, cq, ckv, dtk = _inproj_fwd(x1, small["norm_mix"], sh2, sc2, w["w_in"], S)
    xraw3 = xraw.reshape(B, S, D_CONV)
    xbc = _conv_fwd(xraw3, small["conv_w"], small["conv_b"])
    dtk3, z3 = dtk.reshape(B, S, LANES), z.reshape(B, S, D_SSD)
    y, yssd, prev = _ssd_fwd(xbc, dtk3, z3, dtb, alog, dsk, small["ssd_norm_w"], expand)
    q, k, v, cos_t, sin_t = _mla_prep(cq, ckv, dtk, pos, small["q_norm_w"], small["kv_norm_w"], w["w_uq"], w["w_ukv"], invf,
                                      place)
    hw = MLA_HEADS * HEAD_LANES
    q3, k3, v3 = q.reshape(B, S, hw), k.reshape(B, S, hw), v.reshape(B, S, hw)
    o3, lse = _attn_fwd(q3, k3, v3)
    o = o3.reshape(T, hw)
    x2, m, ycat = _mix_out(x1, yssd.reshape(T, D_SSD), o, small["mla_norm_w"], w["w_out"], g2, S)
    dx3, a2, u2, f2, loss, d_norm_final = _ffn_fwd(
        x2, small["norm_ffn2"], sh3, sc3, g3, w["ffn2_w_gate"], w["ffn2_w_up"], w["ffn2_w_down"], S, "ffn2_fwd",
        head=(small["norm_final"].reshape(1, D), tgt.reshape(T, D)))

    gw, gs = {}, {}
    dx2, h3, s3, df3, da3, du3, dsh3, dsc3, dg3, gs["norm_ffn2"] = _ffn_bwd(
        dx3, x2, small["norm_ffn2"], sh3, sc3, g3, a2, u2, f2, w["ffn2_w_gate"], w["ffn2_w_up"], w["ffn2_w_down"], S, "ffn2_bwd")
    gw["ffn2_w_gate"], gw["ffn2_w_up"], gw["ffn2_w_down"] = _ffn_wgrad(h3, s3, df3, da3, du3, dsh3, "ffn2_wgrad")
    g2 = g2 + on_grads(("ffn2_w_gate", "ffn2_w_up", "ffn2_w_down"), gw)

    dys, do, dm, dg2, gs["mla_norm_w"] = _mix_out_bwd(dx2, m, o, small["mla_norm_w"], w["w_out"], g2, S)
    gw["w_out"] = _mm_tn(ycat, dm, 512, "dwout")

    dq3, dk3, dv3 = _attn_bwd(q3, k3, v3, o3, do.reshape(B, S, hw), lse)
    dcq, dckv, ddtk_b, qn, kvn, dqb, dkvb, gs["q_norm_w"], gs["kv_norm_w"] = _mla_prep_bwd(
        dq3.reshape(T, hw), dk3.reshape(T, hw), dv3.reshape(T, hw), cq, ckv, cos_t, sin_t, small["q_norm_w"] + sync(dq3),
        small["kv_norm_w"], w["w_uq"], w["w_ukv"])
    gw["w_uq"] = _mm_tn(qn, dqb, 512, "dwuq")
    gw["w_ukv"] = _mm_tn(kvn, dkvb, 1024, "dwukv")

    dxbc, ddtk_a, dz, gs["ssd_norm_w"], dvec = _ssd_bwd(
        xbc, dtk3, z3, y, prev, dys.reshape(B, S, D_SSD), dtb, alog, dsk, small["ssd_norm_w"], expand)
    gs["dt_bias"], gs["a_log"], gs["d_skip"] = dvec[0:1, :SSD_HEADS], dvec[1:2, :SSD_HEADS], dvec[2:3, :SSD_HEADS]
    dxraw, gs["conv_w"], gs["conv_b"] = _conv_bwd(dxbc, xraw3, small["conv_w"], small["conv_b"])
    dx1, h2, dproj, dsh2, dsc2, gs["norm_mix"] = _inproj_bwd(
        dx2, x1, small["norm_mix"], sh2, sc2, w["w_in"], dz.reshape(T, D_SSD), dxraw.reshape(T, D_CONV), dcq, dckv,
        ddtk_a.reshape(T, LANES), ddtk_b, S)
    gw["w_in"] = _mm_tn(dproj, h2, 512, "dwin")
    g1 = g1 + on_grads(("w_in", "w_uq", "w_ukv", "w_out"), gw)

    dx0, h1, s1, df1, da1, du1, dsh1, dsc1, dg1, gs["norm_ffn1"] = _ffn_bwd(
        dx1, x0, small["norm_ffn1"], sh1, sc1, g1, a1, u1, f1, w["ffn1_w_gate"], w["ffn1_w_up"], w["ffn1_w_down"], S, "ffn1_bwd")
    gw["ffn1_w_gate"], gw["ffn1_w_up"], gw["ffn1_w_down"] = _ffn_wgrad(h1, s1, df1, da1, du1, dsh1 + sync(dx0), "ffn1_wgrad")
    gs["norm_final"] = d_norm_final
    dmod = jnp.concatenate([t.reshape(B, D) for t in (dsh1, dsc1, dg1, dsh2, dsc2, dg2, dsh3, dsc3, dg3)], axis=1)
    return loss, dx0.reshape(B, S, D), gw, dmod, gs


HBM_SPEC = pl.BlockSpec(memory_space=pltpu.HBM)
VMEM_SPEC = pl.BlockSpec(memory_space=pltpu.VMEM)


def _place():
    return lax.axis_index("x"), lax.axis_index("y"), lax.axis_index("c")


def _other_chips(mx, my):
    return [(1 - mx, my), (mx, 1 - my), (1 - mx, 1 - my)]


def _remote(src, dst, send_sem, recv_sem, to):
    return pltpu.make_async_remote_copy(src_ref=src, dst_ref=dst, send_sem=send_sem, recv_sem=recv_sem,
                                        device_id=to, device_id_type=MESH)


def _all_gather_small(xa, name):
    r, n = xa.shape

    def body(x_ref, o_ref, token, send_sems, recv_sems):
        mx, my, mc = _place()
        me = 4 * mx + 2 * my + mc
        token[...] = jnp.zeros_like(token)
        o_ref[pl.ds(me, 1)] = x_ref[...][None]
        sends = []
        for k in range(1, N_DEV):
            peer = (mx ^ (k >> 2), my ^ ((k >> 1) & 1), mc ^ (k & 1))
            cp = _remote(x_ref, o_ref.at[me], send_sems.at[k - 1], recv_sems.at[k - 1], peer)
            cp.start()
            sends.append(cp)
        for k in range(1, N_DEV):
            peer = (mx ^ (k >> 2), my ^ ((k >> 1) & 1), mc ^ (k & 1))
            slot = 4 * peer[0] + 2 * peer[1] + peer[2]
            _remote(x_ref, o_ref.at[slot], send_sems.at[k - 1], recv_sems.at[k - 1], peer).wait_recv()
        for cp in sends:
            cp.wait_send()

    return pl.pallas_call(
        body, name=name, in_specs=[VMEM_SPEC], out_specs=[VMEM_SPEC, VMEM_SPEC],
        out_shape=[jax.ShapeDtypeStruct((N_DEV, r, n), xa.dtype), jax.ShapeDtypeStruct((8, LANES), F32)],
        scratch_shapes=[pltpu.SemaphoreType.DMA((N_DEV - 1,)), pltpu.SemaphoreType.DMA((N_DEV - 1,))],
        compiler_params=pltpu.CompilerParams(vmem_limit_bytes=VMEM_LIMIT),
    )(xa)


def _halves_by_rows(shape):
    return (shape[-2] // 2) % 16 == 0


def _half_shape(shape):
    r, c = shape[-2:]
    return tuple(shape[:-2]) + ((r // 2, c) if _halves_by_rows(shape) else (r, c // 2))


def _half_index(shape, hc):
    r, c = shape[-2:]
    if _halves_by_rows(shape):
        return (pl.ds(pl.multiple_of(hc * (r // 2), 16), r // 2), slice(None))
    return (slice(None), pl.ds(pl.multiple_of(hc * (c // 2), LANES), c // 2))


def _half(ref, hc, lead=None):
    idx = _half_index(ref.shape, hc)
    return ref.at[idx] if lead is None else ref.at[(lead,) + idx]


def _gather_weights(shards):
    n = len(shards)

    def body(*refs):
        w_refs, o_refs, token = refs[:n], refs[n:2 * n], refs[2 * n]
        send_sems, recv_sems, stage_sems = refs[2 * n + 1:2 * n + 4]
        stages = refs[2 * n + 4:]
        mx, my, mc = _place()
        chip = 2 * mx + my
        others = _other_chips(mx, my)
        sibling = (mx, my, 1 - mc)
        token[...] = jnp.zeros_like(token)
        stage_in = [pltpu.make_async_copy(w, st, stage_sems.at[0, i]) for i, (w, st) in enumerate(zip(w_refs, stages))]
        for cp in stage_in:
            cp.start()
        first = []
        for i, (w, o) in enumerate(zip(w_refs, o_refs)):
            for k, (cx, cy) in enumerate(others):
                first.append(_remote(_half(w, mc), _half(o, mc, chip), send_sems.at[i, k],
                                     recv_sems.at[i, k], (cx, cy, mc)))
                first[-1].start()
        stage_out = []
        for i, (st, o) in enumerate(zip(stages, o_refs)):
            stage_in[i].wait()
            stage_out.append(pltpu.make_async_copy(st, o.at[chip], stage_sems.at[1, i]))
            stage_out[-1].start()
        passed = []
        for i, (w, o) in enumerate(zip(w_refs, o_refs)):
            for k, (cx, cy) in enumerate(others):
                landed = _half(o, mc, 2 * cx + cy)
                _remote(landed, landed, send_sems.at[i, k], recv_sems.at[i, k], (cx, cy, mc)).wait_recv()
                passed.append(_remote(landed, landed, send_sems.at[i, 3 + k], recv_sems.at[i, 3 + k], sibling))
                passed[-1].start()
        for i, (w, o) in enumerate(zip(w_refs, o_refs)):
            for k, (cx, cy) in enumerate(others):
                there = _half(o, 1 - mc, 2 * cx + cy)
                _remote(there, there, send_sems.at[i, 3 + k], recv_sems.at[i, 3 + k], sibling).wait_recv()
        for cp in first + passed:
            cp.wait_send()
        for cp in stage_out:
            cp.wait()

    out = pl.pallas_call(
        body, name="gather_weights", in_specs=[HBM_SPEC] * n, out_specs=[HBM_SPEC] * n + [VMEM_SPEC],
        out_shape=[jax.ShapeDtypeStruct((N_CHIPS,) + s.shape, s.dtype) for s in shards] + [jax.ShapeDtypeStruct((8, LANES), F32)],
        scratch_shapes=[pltpu.SemaphoreType.DMA((n, 6)), pltpu.SemaphoreType.DMA((n, 6)), pltpu.SemaphoreType.DMA((2, n))]
        + [pltpu.VMEM(s.shape, s.dtype) for s in shards],
        compiler_params=pltpu.CompilerParams(vmem_limit_bytes=VMEM_LIMIT),
    )(*shards)
    return out[:n], out[n]


SEM_SPEC = pl.BlockSpec(memory_space=pltpu.SEMAPHORE)
ANY_SPEC = pl.BlockSpec(memory_space=pl.ANY)
DATAFLOW = pltpu.SideEffectType.DATAFLOW_SIDE_EFFECTING


def _hbm(arr):
    return pltpu.with_memory_space_constraint(arr, pltpu.HBM)


def _gather_start(shards):
    n = len(shards)

    def body(*refs):
        w_refs, land_refs, send_sems, recv_sems, token = refs[:n], refs[n:2 * n], refs[2 * n], refs[2 * n + 1], refs[-1]
        mx, my, mc = _place()
        chip = 2 * mx + my
        for i, (w, land) in enumerate(zip(w_refs, land_refs)):
            for k, (cx, cy) in enumerate(_other_chips(mx, my)):
                _remote(_half(w, mc), _half(land, mc, chip), send_sems.at[3 * i + k],
                        recv_sems.at[3 * i + k], (cx, cy, mc)).start()
        token[...] = jnp.zeros_like(token)

    lands = [lax.empty((N_CHIPS,) + s.shape, s.dtype) for s in shards]
    out = pl.pallas_call(
        body, name="gather_start",
        out_shape=(pltpu.SemaphoreType.DMA((3 * n,)), pltpu.SemaphoreType.DMA((3 * n,)),
                   *[pltpu.HBM(s.shape, s.dtype) for s in shards], *[pltpu.HBM(l.shape, l.dtype) for l in lands],
                   jax.ShapeDtypeStruct((8, LANES), F32)),
        in_specs=[HBM_SPEC] * (2 * n), out_specs=(SEM_SPEC, SEM_SPEC, *[HBM_SPEC] * (2 * n), VMEM_SPEC),
        input_output_aliases={i: 2 + i for i in range(2 * n)},
        compiler_params=pltpu.CompilerParams(has_side_effects=DATAFLOW),
    )(*[_hbm(s) for s in shards], *[_hbm(l) for l in lands])
    return out[0], out[1], out[2:2 + n], out[2 + n:2 + 2 * n], out[-1]


def _gather_wait(send_sems, recv_sems, shards, lands, after):
    n = len(shards)

    def body(*refs):
        w_refs, land_refs, send_sems, recv_sems = refs[:n], refs[n:2 * n], refs[2 * n], refs[2 * n + 1]
        mx, my, mc = _place()
        for i, (w, land) in enumerate(zip(w_refs, land_refs)):
            for k, (cx, cy) in enumerate(_other_chips(mx, my)):
                cp = _remote(_half(w, mc), _half(land, mc, 2 * cx + cy), send_sems.at[3 * i + k],
                             recv_sems.at[3 * i + k], (cx, cy, mc))
                cp.wait_send()
                cp.wait_recv()

    out = pl.pallas_call(
        body, name="gather_wait",
        out_shape=(*[pltpu.HBM(s.shape, s.dtype) for s in shards], *[pltpu.HBM(l.shape, l.dtype) for l in lands]),
        in_specs=[HBM_SPEC] * (2 * n) + [SEM_SPEC, SEM_SPEC, ANY_SPEC], out_specs=tuple([HBM_SPEC] * (2 * n)),
        input_output_aliases={i: i for i in range(2 * n)},
        compiler_params=pltpu.CompilerParams(has_side_effects=DATAFLOW),
    )(*shards, *lands, send_sems, recv_sems, after)
    return out[n:]


def _gather_finish(shards, lands):
    n = len(shards)

    def body(*refs):
        w_refs, land_refs, o_refs = refs[:n], refs[n:2 * n], refs[2 * n:3 * n]
        send_sems, recv_sems, stage_sems = refs[3 * n:3 * n + 3]
        stages = refs[3 * n + 3:]
        mx, my, mc = _place()
        chip = 2 * mx + my
        others = _other_chips(mx, my)
        sibling = (mx, my, 1 - mc)
        stage_in = [pltpu.make_async_copy(w, st, stage_sems.at[0, i]) for i, (w, st) in enumerate(zip(w_refs, stages))]
        for cp in stage_in:
            cp.start()
        passed = []
        for i, (w, o) in enumerate(zip(w_refs, o_refs)):
            for k, (cx, cy) in enumerate(others):
                landed = _half(o, mc, 2 * cx + cy)
                passed.append(_remote(landed, landed, send_sems.at[i, k], recv_sems.at[i, k], sibling))
                passed[-1].start()
        stage_out = []
        for i, (st, o) in enumerate(zip(stages, o_refs)):
            stage_in[i].wait()
            stage_out.append(pltpu.make_async_copy(st, o.at[chip], stage_sems.at[1, i]))
            stage_out[-1].start()
        for i, (w, o) in enumerate(zip(w_refs, o_refs)):
            for k, (cx, cy) in enumerate(others):
                there = _half(o, 1 - mc, 2 * cx + cy)
                _remote(there, there, send_sems.at[i, k], recv_sems.at[i, k], sibling).wait_recv()
        for cp in passed:
            cp.wait_send()
        for cp in stage_out:
            cp.wait()

    return pl.pallas_call(
        body, name="gather_finish", in_specs=[HBM_SPEC] * (2 * n), out_specs=[HBM_SPEC] * n,
        out_shape=[jax.ShapeDtypeStruct(l.shape, l.dtype) for l in lands],
        input_output_aliases={n + i: i for i in range(n)},
        scratch_shapes=[pltpu.SemaphoreType.DMA((n, 3)), pltpu.SemaphoreType.DMA((n, 3)), pltpu.SemaphoreType.DMA((2, n))]
        + [pltpu.VMEM(s.shape, s.dtype) for s in shards],
        compiler_params=pltpu.CompilerParams(vmem_limit_bytes=VMEM_LIMIT),
    )(*shards, *lands)


def _scatter_start(ss, tag):
    n = len(ss)

    def body(*refs):
        s_refs, land_refs, send_sems, recv_sems, token = refs[:n], refs[n:2 * n], refs[2 * n], refs[2 * n + 1], refs[-1]
        mx, my, mc = _place()
        chip = 2 * mx + my
        for i, (s, land) in enumerate(zip(s_refs, land_refs)):
            for k, (cx, cy) in enumerate(_other_chips(mx, my)):
                _remote(s.at[2 * cx + cy], land.at[chip], send_sems.at[3 * i + k], recv_sems.at[3 * i + k],
                        (cx, cy, mc)).start()
        token[...] = jnp.zeros_like(token)

    lands = [lax.empty(s.shape, s.dtype) for s in ss]
    out = pl.pallas_call(
        body, name="scatter_start_" + tag,
        out_shape=(pltpu.SemaphoreType.DMA((3 * n,)), pltpu.SemaphoreType.DMA((3 * n,)),
                   *[pltpu.HBM(s.shape, s.dtype) for s in ss], *[pltpu.HBM(l.shape, l.dtype) for l in lands],
                   jax.ShapeDtypeStruct((8, LANES), F32)),
        in_specs=[HBM_SPEC] * (2 * n), out_specs=(SEM_SPEC, SEM_SPEC, *[HBM_SPEC] * (2 * n), VMEM_SPEC),
        input_output_aliases={i: 2 + i for i in range(2 * n)},
        compiler_params=pltpu.CompilerParams(has_side_effects=DATAFLOW),
    )(*[_hbm(s) for s in ss], *[_hbm(l) for l in lands])
    return out[0], out[1], out[2:2 + n], out[2 + n:2 + 2 * n], out[-1]


def _scatter_wait(send_sems, recv_sems, ss, lands, after, tag):
    n = len(ss)

    def body(*refs):
        s_refs, land_refs, send_sems, recv_sems = refs[:n], refs[n:2 * n], refs[2 * n], refs[2 * n + 1]
        mx, my, mc = _place()
        for i, (s, land) in enumerate(zip(s_refs, land_refs)):
            for k, (cx, cy) in enumerate(_other_chips(mx, my)):
                slot = land.at[2 * cx + cy]
                cp = _remote(s.at[2 * cx + cy], slot, send_sems.at[3 * i + k], recv_sems.at[3 * i + k], (cx, cy, mc))
                cp.wait_send()
                cp.wait_recv()

    out = pl.pallas_call(
        body, name="scatter_wait_" + tag,
        out_shape=(*[pltpu.HBM(s.shape, s.dtype) for s in ss], *[pltpu.HBM(l.shape, l.dtype) for l in lands]),
        in_specs=[HBM_SPEC] * (2 * n) + [SEM_SPEC, SEM_SPEC, ANY_SPEC], out_specs=tuple([HBM_SPEC] * (2 * n)),
        input_output_aliases={i: i for i in range(2 * n)},
        compiler_params=pltpu.CompilerParams(has_side_effects=DATAFLOW),
    )(*ss, *lands, send_sems, recv_sems, after)
    return out[:n], out[n:]


def _swap_halves(gs, after, name):
    n = len(gs)

    def body(*refs):
        g_refs, o_refs, send_sems, recv_sems = refs[:n], refs[n + 1:2 * n + 1], refs[2 * n + 1], refs[2 * n + 2]
        mx, my, mc = _place()
        copies = []
        for i, (g, o) in enumerate(zip(g_refs, o_refs)):
            src = g.at[(slice(None),) + _half_index(g.shape, 1 - mc)]
            copies.append(_remote(src, o, send_sems.at[i], recv_sems.at[i], (mx, my, 1 - mc)))
            copies[-1].start()
        for cp in copies:
            cp.wait()

    return pl.pallas_call(
        body, name=name, in_specs=[HBM_SPEC] * n + [ANY_SPEC], out_specs=[HBM_SPEC] * n,
        out_shape=[jax.ShapeDtypeStruct(_half_shape(g.shape), g.dtype) for g in gs],
        scratch_shapes=[pltpu.SemaphoreType.DMA((n,)), pltpu.SemaphoreType.DMA((n,))],
    )(*gs, after)


def _swap_start(gs, tag):
    n = len(gs)

    def body(*refs):
        g_refs, land_refs, send_sems, recv_sems, token = refs[:n], refs[n:2 * n], refs[2 * n], refs[2 * n + 1], refs[-1]
        mx, my, mc = _place()
        for i, (g, land) in enumerate(zip(g_refs, land_refs)):
            src = g.at[(slice(None),) + _half_index(g.shape, 1 - mc)]
            _remote(src, land, send_sems.at[i], recv_sems.at[i], (mx, my, 1 - mc)).start()
        token[...] = jnp.zeros_like(token)

    lands = [lax.empty(_half_shape(g.shape), g.dtype) for g in gs]
    out = pl.pallas_call(
        body, name="swap_start_" + tag,
        out_shape=(pltpu.SemaphoreType.DMA((n,)), pltpu.SemaphoreType.DMA((n,)), *[pltpu.HBM(g.shape, g.dtype) for g in gs],
                   *[pltpu.HBM(l.shape, l.dtype) for l in lands], jax.ShapeDtypeStruct((8, LANES), F32)),
        in_specs=[HBM_SPEC] * (2 * n), out_specs=(SEM_SPEC, SEM_SPEC, *[HBM_SPEC] * (2 * n), VMEM_SPEC),
        input_output_aliases={i: 2 + i for i in range(2 * n)},
        compiler_params=pltpu.CompilerParams(has_side_effects=DATAFLOW),
    )(*[_hbm(g) for g in gs], *[_hbm(l) for l in lands])
    return out[0], out[1], out[2:2 + n], out[2 + n:2 + 2 * n], out[-1]


def _swap_wait(send_sems, recv_sems, gs, lands, after, tag):
    n = len(gs)

    def body(*refs):
        g_refs, land_refs, send_sems, recv_sems = refs[:n], refs[n:2 * n], refs[2 * n], refs[2 * n + 1]
        mx, my, mc = _place()
        for i, (g, land) in enumerate(zip(g_refs, land_refs)):
            src = g.at[(slice(None),) + _half_index(g.shape, 1 - mc)]
            cp = _remote(src, land, send_sems.at[i], recv_sems.at[i], (mx, my, 1 - mc))
            cp.wait_send()
            cp.wait_recv()

    out = pl.pallas_call(
        body, name="swap_wait_" + tag,
        out_shape=(*[pltpu.HBM(g.shape, g.dtype) for g in gs], *[pltpu.HBM(l.shape, l.dtype) for l in lands]),
        in_specs=[HBM_SPEC] * (2 * n) + [SEM_SPEC, SEM_SPEC, ANY_SPEC], out_specs=tuple([HBM_SPEC] * (2 * n)),
        input_output_aliases={i: i for i in range(2 * n)},
        compiler_params=pltpu.CompilerParams(has_side_effects=DATAFLOW),
    )(*gs, *lands, send_sems, recv_sems, after)
    return out[:n], out[n:]


def _pair_sum(g, got, core, name):
    hr, hc = _half_shape(g.shape)[1:]
    by_rows = _halves_by_rows(g.shape)

    def body(core_ref, g_ref, got_ref, o_ref):
        o_ref[...] = (g_ref[...].astype(F32) + got_ref[...].astype(F32)).astype(BF16)

    return pl.pallas_call(
        body, name=name,
        grid_spec=pltpu.PrefetchScalarGridSpec(
            num_scalar_prefetch=1, grid=(N_CHIPS,),
            in_specs=[pl.BlockSpec((1, hr, hc), lambda j, core_ref: (j, core_ref[0], 0) if by_rows else (j, 0, core_ref[0])),
                      pl.BlockSpec((1, hr, hc), lambda j, core_ref: (j, 0, 0))],
            out_specs=pl.BlockSpec((1, hr, hc), lambda j, core_ref: (j, 0, 0))),
        out_shape=jax.ShapeDtypeStruct((N_CHIPS, hr, hc), BF16),
        compiler_params=_cparams(("arbitrary",)),
    )(core, g, got)


def _reduce_join(owns, gots, name):
    n = len(owns)

    def body(*refs):
        own_refs, got_refs, mine_refs, theirs_refs = refs[:n], refs[n:2 * n], refs[2 * n:3 * n], refs[3 * n:4 * n]
        send_sems, recv_sems, load_sems, store_sems = refs[4 * n:4 * n + 4]
        parts, sums = refs[4 * n + 4:5 * n + 4], refs[5 * n + 4:]
        mx, my, mc = _place()
        chip = 2 * mx + my
        loads = []
        for i, (own, got, part) in enumerate(zip(own_refs, got_refs, parts)):
            loads.append([pltpu.make_async_copy((own if k == 0 else got).at[chip ^ k], part.at[k], load_sems.at[i, k])
                          for k in range(N_CHIPS)])
            for cp in loads[-1]:
                cp.start()
        out = []
        for i, (part, total, mine, theirs) in enumerate(zip(parts, sums, mine_refs, theirs_refs)):
            for cp in loads[i]:
                cp.wait()
            total[...] = ((part[0].astype(F32) + part[1].astype(F32)) + part[2].astype(F32)) + part[3].astype(F32)
            out.append(pltpu.make_async_copy(total, mine, store_sems.at[i]))
            out.append(_remote(total, theirs, send_sems.at[i], recv_sems.at[i], (mx, my, 1 - mc)))
            out[-2].start()
            out[-1].start()
        for cp in out:
            cp.wait()

    halves = [jax.ShapeDtypeStruct(o.shape[1:], F32) for o in owns]
    out = pl.pallas_call(
        body, name=name, in_specs=[HBM_SPEC] * (2 * n), out_specs=[HBM_SPEC] * (2 * n), out_shape=halves + halves,
        scratch_shapes=[pltpu.SemaphoreType.DMA((n,)), pltpu.SemaphoreType.DMA((n,)), pltpu.SemaphoreType.DMA((n, N_CHIPS)),
                        pltpu.SemaphoreType.DMA((n,))]
        + [pltpu.VMEM(o.shape, o.dtype) for o in owns] + [pltpu.VMEM(o.shape[1:], F32) for o in owns],
        compiler_params=pltpu.CompilerParams(vmem_limit_bytes=VMEM_LIMIT),
    )(*owns, *gots)
    return out[:n], out[n:]


def _adam_math(w, g, m, v):
    m2 = ADAM_B1 * m + (1.0 - ADAM_B1) * g
    v2 = ADAM_B2 * v + (1.0 - ADAM_B2) * (g * g)
    m_hat = m2 * (1.0 / (1.0 - ADAM_B1 ** ADAM_STEP))
    v_hat = v2 * (1.0 / (1.0 - ADAM_B2 ** ADAM_STEP))
    delta = -ADAM_LR * (m_hat / (jnp.sqrt(v_hat) + ADAM_EPS) + ADAM_WD * w)
    return delta, m2, v2


def _adam(w, g, m, v, name):
    def body(w_ref, g_ref, m_ref, v_ref, d_ref, m2_ref, v2_ref):
        d_ref[...], m2_ref[...], v2_ref[...] = _adam_math(w_ref[...], g_ref[...], m_ref[...], v_ref[...])

    return pl.pallas_call(body, name=name, out_shape=[jax.ShapeDtypeStruct(w.shape, F32)] * 3)(w, g, m, v)


def _adam_halves(w, m, v, mine, theirs, core, name):
    hr, hcols = _half_shape(w.shape)[1:]
    by_rows = _halves_by_rows(w.shape)

    def body(core_ref, w_ref, m_ref, v_ref, mine_ref, theirs_ref, g_ref, d_ref, m2_ref, v2_ref):
        g = jnp.where(pl.program_id(0) == core_ref[0], mine_ref[...], theirs_ref[...])
        g_ref[0] = g
        d_ref[0], m2_ref[0], v2_ref[0] = _adam_math(w_ref[0], g, m_ref[0], v_ref[0])

    half = pl.BlockSpec((1, hr, hcols), lambda hc, core_ref: (0, hc, 0) if by_rows else (0, 0, hc))
    whole = pl.BlockSpec((hr, hcols), lambda hc, core_ref: (0, 0))
    return pl.pallas_call(
        body, name=name,
        grid_spec=pltpu.PrefetchScalarGridSpec(
            num_scalar_prefetch=1, grid=(2,), in_specs=[half, half, half, whole, whole], out_specs=[half] * 4),
        out_shape=[jax.ShapeDtypeStruct(w.shape, F32)] * 4,
        compiler_params=_cparams(("arbitrary",)),
    )(core, w, m, v, mine, theirs)


ADA_COLS = N_MOD * D_MODEL // N_CHIPS


def _ada_fwd(c_all, w_ada, b_cols):
    def body(c_ref, w_ref, b_ref, o_ref):
        cv = c_ref[...]
        act = (cv * _sigmoid(cv)).astype(BF16)
        o_ref[...] = _dot(act, w_ref[...].astype(BF16)) + b_ref[...]

    return pl.pallas_call(
        body, name="ada_fwd", out_shape=jax.ShapeDtypeStruct((c_all.shape[0], ADA_COLS), F32),
        compiler_params=pltpu.CompilerParams(vmem_limit_bytes=VMEM_LIMIT),
    )(c_all, w_ada, b_cols)


def _ada_bwd(c_all, dmod_cols, w, m, v):
    nb = c_all.shape[0]
    tn = 384

    def body(c_ref, d_ref, w_ref, m_ref, v_ref, g_ref, dl_ref, m2_ref, v2_ref):
        cv = c_ref[...]
        act = (cv * _sigmoid(cv)).astype(BF16)
        g = _dot_tn(act, d_ref[...].astype(BF16))
        g_ref[...] = g
        dl_ref[...], m2_ref[...], v2_ref[...] = _adam_math(w_ref[...], g, m_ref[...], v_ref[...])

    blk = pl.BlockSpec((D_MODEL, tn), lambda j: (0, j))
    return pl.pallas_call(
        body, name="ada_bwd", grid=(ADA_COLS // tn,),
        in_specs=[pl.BlockSpec((nb, D_MODEL), lambda j: (0, 0)), pl.BlockSpec((nb, tn), lambda j: (0, j)), blk, blk, blk],
        out_specs=[blk] * 4, out_shape=[jax.ShapeDtypeStruct((D_MODEL, ADA_COLS), F32)] * 4,
        compiler_params=_cparams(("arbitrary",)),
    )(c_all, dmod_cols, w, m, v)


SMALL_NAMES = ("norm_ffn1", "norm_mix", "conv_w", "conv_b", "ssd_norm_w", "q_norm_w", "kv_norm_w", "mla_norm_w",
               "norm_ffn2", "norm_final", "dt_bias", "a_log", "d_skip")
SMALL_SIZES = (1024, 1024, CONV_WIDTH * D_CONV, D_CONV, 1024, Q_LORA, KV_LORA, 1024, 1024, 1024, 16, 16, 16)
SMALL_ROWS = 16
MOD_ROWS = 2 * N_MOD
SEND_ROWS = 40


def _pack_small(parts):
    flat = jnp.concatenate([parts[n].reshape(-1) for n in SMALL_NAMES])
    return jnp.pad(flat, (0, SMALL_ROWS * D_MODEL - flat.shape[0]))


def _unpack_small(flat):
    out, off = {}, 0
    for n, size in zip(SMALL_NAMES, SMALL_SIZES):
        out[n] = flat[off:off + size]
        off += size
    return out


def _small_sum(got):
    def body(g_ref, o_ref):
        bsum = jnp.zeros((N_MOD, D_MODEL), F32)
        ssum = jnp.zeros((SMALL_ROWS, D_MODEL), F32)
        for d in range(N_DEV):
            bsum = bsum + g_ref[d, 0:N_MOD, :] + g_ref[d, N_MOD:MOD_ROWS, :]
            ssum = ssum + g_ref[d, MOD_ROWS:MOD_ROWS + SMALL_ROWS, :]
        o_ref[...] = jnp.concatenate([bsum, ssum, jnp.zeros((32 - N_MOD - SMALL_ROWS, D_MODEL), F32)], axis=0)

    return pl.pallas_call(body, name="small_sum", out_shape=jax.ShapeDtypeStruct((32, D_MODEL), F32))(got)


BIG_NAMES = ("ffn1_w_gate", "ffn1_w_up", "ffn1_w_down", "w_in", "w_uq", "w_ukv", "w_out", "ffn2_w_gate", "ffn2_w_up",
             "ffn2_w_down")
_TO_KERNEL = {"w_in": _win_to_kernel, "w_uq": _wuq_to_kernel, "w_ukv": _wukv_to_kernel}
_FROM_KERNEL = {"w_in": _win_from_kernel, "w_uq": _wuq_from_kernel, "w_ukv": _wukv_from_kernel}


def _columns_joined(w4):
    n, r, c = w4.shape
    return w4.transpose(1, 0, 2).reshape(r, n * c)


def _columns_split(g):
    r, cols = g.shape
    return g.reshape(r, N_CHIPS, cols // N_CHIPS).transpose(1, 0, 2)


def kernel(x, c, positions, w_ada, b_ada, norm_ffn1, ffn1_w_gate, ffn1_w_up, ffn1_w_down, norm_mix, w_in, conv_w, conv_b, dt_bias, a_log, d_skip, ssd_norm_w, q_norm_w, w_uq, kv_norm_w, w_ukv, mla_norm_w, w_out, norm_ffn2, ffn2_w_gate, ffn2_w_up, ffn2_w_down, norm_final, loss_target, m_w_ada, m_b_ada, m_norm_ffn1, m_ffn1_w_gate, m_ffn1_w_up, m_ffn1_w_down, m_norm_mix, m_w_in, m_conv_w, m_conv_b, m_dt_bias, m_a_log, m_d_skip, m_ssd_norm_w, m_q_norm_w, m_w_uq, m_kv_norm_w, m_w_ukv, m_mla_norm_w, m_w_out, m_norm_ffn2, m_ffn2_w_gate, m_ffn2_w_up, m_ffn2_w_down, m_norm_final, v_w_ada, v_b_ada, v_norm_ffn1, v_ffn1_w_gate, v_ffn1_w_up, v_ffn1_w_down, v_norm_mix, v_w_in, v_conv_w, v_conv_b, v_dt_bias, v_a_log, v_d_skip, v_ssd_norm_w, v_q_norm_w, v_w_uq, v_kv_norm_w, v_w_ukv, v_mla_norm_w, v_w_out, v_norm_ffn2, v_ffn2_w_gate, v_ffn2_w_up, v_ffn2_w_down, v_norm_final):
    a = dict(locals())
    held_transposed = ("ffn1_w_gate", "ffn1_w_up", "ffn2_w_gate", "ffn2_w_up", "w_in")
    for n in held_transposed:
        for p in ("", "m_", "v_"):
            a[p + n] = a[p + n].transpose(0, 2, 1)
    B, S, D = x.shape
    mx, my, mc = _place()
    chip = 2 * mx + my
    dev = 2 * chip + mc
    core = mc.astype(jnp.int32).reshape(1)

    cw_rows = jnp.pad(conv_w[0], ((0, 0), (0, D - conv_w.shape[2])))
    got, _ = _all_gather_small(jnp.concatenate([c, cw_rows, jnp.zeros((8 - B - CONV_WIDTH, D), F32)], axis=0), "gather_c")
    c_all = got[:, :B, :].reshape(N_DEV * B, D)
    conv_full = got[::2, B:B + CONV_WIDTH, :conv_w.shape[2]].transpose(1, 0, 2).reshape(CONV_WIDTH, D_CONV)

    b_cols = lax.dynamic_slice(b_ada, (0, chip * ADA_COLS), (1, ADA_COLS))
    mod_all, mod_done = _all_gather_small(_ada_fwd(c_all, w_ada[0], b_cols), "gather_mod")
    mod = lax.dynamic_slice(mod_all, (0, B * dev, 0), (N_DEV, B, ADA_COLS))[::2].transpose(1, 0, 2).reshape(B, N_MOD * D)

    first = ("ffn1_w_gate", "ffn1_w_up", "ffn1_w_down")
    later = tuple(n for n in BIG_NAMES if n not in first)
    got_first, gathered = _gather_weights([(a[n][0] + mod_done[0, 0]).astype(BF16) for n in first])
    w = dict(zip(first, got_first))
    in_flight = _gather_start([(a[n][0] + gathered[0, 0]).astype(BF16) for n in later])

    def later_weights(after):
        send_sems, recv_sems, shards, lands, _ = in_flight
        lands = _gather_wait(send_sems, recv_sems, shards, lands, after)
        wl = dict(zip(later, _gather_finish([a[n][0].astype(BF16) for n in later], lands)))
        for n, to_kernel in _TO_KERNEL.items():
            wl[n] = to_kernel(wl[n].reshape(-1, D) if n in held_transposed else _columns_joined(wl[n]))
        wl["w_out"] = wl["w_out"].reshape(D_SSD + D_MLA, D)
        return wl

    small = {n: a[n].reshape(1, -1) for n in SMALL_NAMES if n not in ("conv_w", "norm_final")}
    small["conv_w"], small["norm_final"] = conv_full, norm_final

    def shards_of(names, gw):
        g4 = []
        for n in names:
            g = gw[n]
            if n in _FROM_KERNEL:
                g = _FROM_KERNEL[n](g) if n in held_transposed else _columns_split(_FROM_KERNEL[n](g))
            g4.append(g.reshape(N_CHIPS, a[n].shape[1], a[n].shape[2]))
        return g4

    def scatter_group(names, g4, swapped):
        pair = [_pair_sum(g, got, core, "pair_sum_" + n) for n, g, got in zip(names, g4, swapped)]
        return (names,) + tuple(_scatter_start(pair, names[0]))

    grads, deltas, new_m, new_v = {}, {}, {}, {}

    def finish_groups(some, after):
        names, owns, gots = [], [], []
        for group_names, send_sems, recv_sems, pair, lands, _ in some:
            pair, lands = _scatter_wait(send_sems, recv_sems, pair, lands, after, group_names[0])
            names += group_names
            owns += pair
            gots += lands
        mine, theirs = _reduce_join(owns, gots, "reduce_join_" + names[0])
        for n, own, other in zip(names, mine, theirs):
            grads[n], deltas[n], new_m[n], new_v[n] = _adam_halves(a[n], a["m_" + n], a["v_" + n], own, other, core, "adam_" + n)
        return deltas[names[-1]]

    swapping, groups = [], []

    def on_grads(names, gw):
        send_sems, recv_sems, g4, lands, token = _swap_start(shards_of(names, gw), names[0])
        swapping.append((names, send_sems, recv_sems, g4, lands))
        return token[0, 0]

    def sync(after):
        token = 0.0
        while swapping:
            names, send_sems, recv_sems, g4, lands = swapping.pop(0)
            g4, swapped = _swap_wait(send_sems, recv_sems, g4, lands, after, names[0])
            groups.append(scatter_group(names, g4, swapped))
            token = groups[-1][5][0, 0]
        return token

    loss_blk, grad_x, gw, dmod, gs = _local_step(x, positions, mod + in_flight[4][0, 0], w, later_weights, small, loss_target,
                                                 on_grads, sync)

    small_flat = _pack_small(gs).at[-1].set(loss_blk[0, 0])
    send = jnp.concatenate([dmod.reshape(MOD_ROWS, D), small_flat.reshape(SMALL_ROWS, D),
                            jnp.zeros((SEND_ROWS - MOD_ROWS - SMALL_ROWS, D), F32)], axis=0)
    got, _ = _all_gather_small(send, "gather_small")
    summed = _small_sum(got)
    sums = summed[N_MOD:N_MOD + SMALL_ROWS].reshape(-1)
    loss = sums[-1]
    gsmall = _unpack_small(sums)
    gsmall["conv_w"] = lax.dynamic_slice(gsmall["conv_w"].reshape(CONV_WIDTH, D_CONV), (0, chip * conv_w.shape[2]),
                                         (CONV_WIDTH, conv_w.shape[2]))
    gsmall["b_ada"] = summed[:N_MOD]
    names = ("b_ada",) + SMALL_NAMES
    rows = 208

    def pack(parts):
        flat = jnp.concatenate([parts[n].reshape(-1) for n in names])
        return jnp.pad(flat, (0, rows * LANES - flat.shape[0])).reshape(rows, LANES)

    packed = [pack({n: a[p + n] for n in names}) for p in ("", "m_", "v_")]
    g_p = pack(gsmall)
    outs = (g_p,) + tuple(_adam(packed[0], g_p, packed[1], packed[2], "adam_small"))
    for dst, flat in zip((grads, deltas, new_m, new_v), outs):
        flat, off = flat.reshape(-1), 0
        for n in names:
            dst[n] = flat[off:off + a[n].size].reshape(a[n].shape)
            off += a[n].size

    dmod_all = got[:, :MOD_ROWS, :].reshape(N_DEV * B, N_MOD * D)
    dmod_cols = lax.dynamic_slice(dmod_all, (0, chip * ADA_COLS), (N_DEV * B, ADA_COLS))
    ada = _ada_bwd(c_all, dmod_cols, w_ada[0], m_w_ada[0], v_w_ada[0])
    for dst, t in zip((grads, deltas, new_m, new_v), ada):
        dst["w_ada"] = t[None]

    g4 = shards_of(first, gw)
    last = scatter_group(first, g4, _swap_halves(g4, summed, "swap_" + first[0]))
    finish_groups([last], finish_groups(groups, last[5]))
    for dst in (grads, deltas, new_m, new_v):
        for n in held_transposed:
            dst[n] = dst[n].transpose(0, 2, 1)

    order = ("w_ada", "b_ada", "norm_ffn1", "ffn1_w_gate", "ffn1_w_up", "ffn1_w_down", "norm_mix", "w_in", "conv_w", "conv_b",
             "dt_bias", "a_log", "d_skip", "ssd_norm_w", "q_norm_w", "w_uq", "kv_norm_w", "w_ukv", "mla_norm_w", "w_out",
             "norm_ffn2", "ffn2_w_gate", "ffn2_w_up", "ffn2_w_down", "norm_final")
    return (loss, grad_x, *[grads[n] for n in order], *[deltas[n] for n in order], *[new_m[n] for n in order],
            *[new_v[n] for n in order])
```

```python
import functools
import math

import jax
import jax.numpy as jnp
import numpy as np
from jax import lax
from jax.experimental import pallas as pl
from jax.experimental.pallas import tpu as pltpu

F32 = jnp.float32
BF16 = jnp.bfloat16

D_MODEL = 1024
D_FF = 2816
D_SSD = 1024
D_MLA = 1024
SSD_HEADS = 16
SSD_HEAD_DIM = 64
SSD_GROUPS = 2
SSD_STATE = 128
CONV_WIDTH = 4
CHUNK = 128
MLA_HEADS = 8
QK_NOPE = 64
QK_ROPE = 32
QK_DIM = QK_NOPE + QK_ROPE
V_HEAD = 128
Q_LORA = 384
KV_LORA = 256
ROPE_THETA = 10000.0
N_MOD = 9
EPS = 1e-6
D_CONV = D_SSD + 2 * SSD_GROUPS * SSD_STATE
D_PROJ = 3328
HEAD_LANES = 128
ADAM_LR = 0.001
ADAM_B1 = 0.9
ADAM_B2 = 0.999
ADAM_EPS = 1e-08
ADAM_WD = 0.01
ADAM_STEP = 10

LANES = 128
VMEM_LIMIT = 56 * 1024 * 1024
TOKEN_TILE = 512
WIDE_TOKEN_TILE = 1024
ATTN_FWD_Q_TILE = 1024
ATTN_FWD_KV_TILE = 1024
ATTN_BWD_TILE = 1024
N_CHIPS = 4
N_DEV = 8

MESH = pl.DeviceIdType.MESH


def _dot(a, b):
    return jnp.dot(a, b, preferred_element_type=F32)


def _dot_nt(a, b):
    return lax.dot_general(a, b, (((1,), (1,)), ((), ())), preferred_element_type=F32)


def _dot_tn(a, b):
    return lax.dot_general(a, b, (((0,), (0,)), ((), ())), preferred_element_type=F32)


def _cparams(semantics):
    return pltpu.CompilerParams(dimension_semantics=semantics, vmem_limit_bytes=VMEM_LIMIT)


def _resident(shape):
    zeros = (0,) * len(shape)
    return pl.BlockSpec(shape, lambda *_: zeros, pipeline_mode=pl.Buffered(1))


def _sigmoid(x):
    return jax.nn.sigmoid(x)


def _rms_stats(x):
    r = lax.rsqrt(jnp.mean(x * x, axis=-1, keepdims=True) + EPS)
    return x * r, r


def _rms_bwd(dn, xh, r, w):
    dxh = dn * w
    dx = r * (dxh - xh * jnp.mean(dxh * xh, axis=-1, keepdims=True))
    return dx, dn * xh


def _colsum(v):
    return jnp.sum(v, axis=0, keepdims=True)


def _ffn_fwd(x, nw, sh, sc, g, wg, wu, wd, seq, name, head=None):
    T, D = x.shape
    fs = wg.shape[1]
    tm = min(TOKEN_TILE, seq)
    tps = seq // tm

    def body(x_ref, nw_ref, sh_ref, sc_ref, g_ref, wg_ref, wu_ref, wd_ref, *rest):
        if head is None:
            xo_ref, a_ref, u_ref, f_ref = rest
        else:
            nf_ref, t_ref, xo_ref, a_ref, u_ref, f_ref, loss_ref, dnf_ref = rest

            @pl.when(pl.program_id(0) == 0)
            def _():
                loss_ref[...] = jnp.zeros_like(loss_ref)
                dnf_ref[...] = jnp.zeros_like(dnf_ref)

        xv = x_ref[...]
        xh, _ = _rms_stats(xv)
        h = (xh * nw_ref[...]) * (1.0 + sc_ref[0]) + sh_ref[0]
        hb = h.astype(BF16)
        f = jnp.zeros((tm, D), F32)
        for j in range(N_CHIPS):
            a = _dot_nt(hb, wg_ref[j])
            u = _dot_nt(hb, wu_ref[j])
            a_ref[j] = a.astype(BF16)
            u_ref[j] = u.astype(BF16)
            f = f + _dot((a * _sigmoid(a) * u).astype(BF16), wd_ref[j])
        f_ref[...] = f.astype(BF16)
        xo = xv + 0.5 * g_ref[0] * f
        if head is None:
            xo_ref[...] = xo
        else:
            xh, r = _rms_stats(xo)
            nfv = nf_ref[...]
            err = xh * nfv - t_ref[...]
            loss_ref[...] += (0.5 / D) * jnp.sum(err * err)
            dxo, dw_rows = _rms_bwd(err * (1.0 / D), xh, r, nfv)
            xo_ref[...] = dxo
            dnf_ref[...] += _colsum(dw_rows)

    rows = lambda n: pl.BlockSpec((tm, n), lambda i: (i, 0))
    act = pl.BlockSpec((N_CHIPS, tm, fs), lambda i: (0, i, 0))
    perb = pl.BlockSpec((1, 1, D), lambda i: (i // tps, 0, 0))
    sd = jax.ShapeDtypeStruct
    in_specs = [rows(D), _resident((1, D)), perb, perb, perb, _resident((N_CHIPS, fs, D)), _resident((N_CHIPS, fs, D)),
                _resident((N_CHIPS, fs, D))]
    out_specs = [rows(D), act, act, rows(D)]
    out_shape = [sd((T, D), F32), sd((N_CHIPS, T, fs), BF16), sd((N_CHIPS, T, fs), BF16), sd((T, D), BF16)]
    if head is not None:
        in_specs += [_resident((1, D)), rows(D)]
        out_specs += [pl.BlockSpec((8, LANES), lambda i: (0, 0)), pl.BlockSpec((1, D), lambda i: (0, 0))]
        out_shape += [sd((8, LANES), F32), sd((1, D), F32)]
    return pl.pallas_call(
        body, grid=(T // tm,), name=name, in_specs=in_specs, out_specs=out_specs, out_shape=out_shape,
        compiler_params=_cparams(("arbitrary",)),
    )(x, nw, sh, sc, g, wg, wu, wd, *(head or ()))


def _ffn_bwd(dxo, x, nw, sh, sc, g, a, u, f, wg, wu, wd, seq, name):
    T, D = x.shape
    fs = wg.shape[1]
    B = T // seq
    tm = min(TOKEN_TILE // 2, seq)
    tps = seq // tm

    def body(dxo_ref, x_ref, nw_ref, sh_ref, sc_ref, g_ref, a_ref, u_ref, f_ref, wg_ref, wu_ref, wd_ref,
             dx_ref, h_ref, s_ref, df_ref, da_ref, du_ref, dsh_ref, dsc_ref, dg_ref, dnw_ref):
        i = pl.program_id(0)

        @pl.when(i % tps == 0)
        def _():
            dsh_ref[...] = jnp.zeros_like(dsh_ref)
            dsc_ref[...] = jnp.zeros_like(dsc_ref)
            dg_ref[...] = jnp.zeros_like(dg_ref)

        @pl.when(i == 0)
        def _():
            dnw_ref[...] = jnp.zeros_like(dnw_ref)

        dxo_v = dxo_ref[...]
        dfb = (0.5 * g_ref[0] * dxo_v).astype(BF16)
        dg_ref[0] += _colsum(0.5 * dxo_v * f_ref[...].astype(F32))
        dh = jnp.zeros((tm, D), F32)
        for j in range(N_CHIPS):
            ds = _dot_nt(dfb, wd_ref[j])
            av = a_ref[j].astype(F32)
            uv = u_ref[j].astype(F32)
            sig = _sigmoid(av)
            sil = av * sig
            dab = (ds * uv * (sig * (1.0 + av * (1.0 - sig)))).astype(BF16)
            dub = (ds * sil).astype(BF16)
            dh = dh + _dot(dab, wg_ref[j]) + _dot(dub, wu_ref[j])
            s_ref[j] = (sil * uv).astype(BF16)
            da_ref[j] = dab
            du_ref[j] = dub
        xv = x_ref[...]
        xh, r = _rms_stats(xv)
        nwv = nw_ref[...]
        n = xh * nwv
        scale1 = 1.0 + sc_ref[0]
        dsc_ref[0] += _colsum(dh * n)
        dsh_ref[0] += _colsum(dh)
        dx, dw_rows = _rms_bwd(dh * scale1, xh, r, nwv)
        dnw_ref[...] += _colsum(dw_rows)
        dx_ref[...] = dxo_v + dx
        h_ref[...] = (n * scale1 + sh_ref[0]).astype(BF16)
        df_ref[...] = dfb

    rows = lambda n: pl.BlockSpec((tm, n), lambda i: (i, 0))
    act = pl.BlockSpec((N_CHIPS, tm, fs), lambda i: (0, i, 0))
    perb = pl.BlockSpec((1, 1, D), lambda i: (i // tps, 0, 0))
    sd = jax.ShapeDtypeStruct
    return pl.pallas_call(
        body, grid=(T // tm,), name=name,
        in_specs=[rows(D), rows(D), _resident((1, D)), perb, perb, perb, act, act, rows(D),
                  _resident((N_CHIPS, fs, D)), _resident((N_CHIPS, fs, D)), _resident((N_CHIPS, fs, D))],
        out_specs=[rows(D), rows(D), act, rows(D), act, act, perb, perb, perb, pl.BlockSpec((1, D), lambda i: (0, 0))],
        out_shape=[sd((T, D), F32), sd((T, D), BF16), sd((N_CHIPS, T, fs), BF16), sd((T, D), BF16),
                   sd((N_CHIPS, T, fs), BF16), sd((N_CHIPS, T, fs), BF16), sd((B, 1, D), F32), sd((B, 1, D), F32),
                   sd((B, 1, D), F32), sd((1, D), F32)],
        compiler_params=_cparams(("arbitrary",)),
    )(dxo, x, nw, sh, sc, g, a, u, f, wg, wu, wd)


def _ffn_wgrad(h, s, df, da, du, after, name):
    T, D = h.shape
    fs = s.shape[2]
    tt = min(TOKEN_TILE, T)
    nt = T // tt

    def body(h_ref, s_ref, df_ref, da_ref, du_ref, after_ref, dgate_ref, dup_ref, ddown_ref, gate_acc, up_acc, down_acc):
        @pl.when(pl.program_id(1) == 0)
        def _():
            gate_acc[...] = jnp.zeros_like(gate_acc)
            up_acc[...] = jnp.zeros_like(up_acc)
            down_acc[...] = jnp.zeros_like(down_acc)

        hv = h_ref[...]
        gate_acc[...] += _dot_tn(da_ref[0], hv)
        up_acc[...] += _dot_tn(du_ref[0], hv)
        down_acc[...] += _dot_tn(s_ref[0], df_ref[...])

        @pl.when(pl.program_id(1) == nt - 1)
        def _():
            dgate_ref[0] = gate_acc[...].astype(BF16)
            dup_ref[0] = up_acc[...].astype(BF16)
            ddown_ref[0] = down_acc[...].astype(BF16)

    rows = pl.BlockSpec((tt, D), lambda j, t: (t, 0))
    act = pl.BlockSpec((1, tt, fs), lambda j, t: (j, t, 0))
    shard = pl.BlockSpec((1, fs, D), lambda j, t: (j, 0, 0))
    return pl.pallas_call(
        body, grid=(N_CHIPS, nt), name=name,
        in_specs=[rows, act, rows, act, act, pl.BlockSpec(memory_space=pl.ANY)],
        out_specs=[shard] * 3, out_shape=[jax.ShapeDtypeStruct((N_CHIPS, fs, D), BF16)] * 3,
        scratch_shapes=[pltpu.VMEM((fs, D), F32)] * 3,
        compiler_params=_cparams(("arbitrary", "arbitrary")),
    )(h, s, df, da, du, after)


def _mm_tn(xa, ya, tn, name):
    T, K = xa.shape
    N = ya.shape[1]
    tt = min(WIDE_TOKEN_TILE, T)
    nt = T // tt

    def body(x_ref, y_ref, o_ref, acc_ref):
        @pl.when(pl.program_id(1) == 0)
        def _():
            acc_ref[...] = jnp.zeros_like(acc_ref)

        acc_ref[...] += _dot_tn(x_ref[...], y_ref[...])

        @pl.when(pl.program_id(1) == nt - 1)
        def _():
            o_ref[...] = acc_ref[...].astype(BF16)

    return pl.pallas_call(
        body, grid=(N // tn, nt), name=name,
        in_specs=[pl.BlockSpec((tt, K), lambda j, t: (t, 0)), pl.BlockSpec((tt, tn), lambda j, t: (t, j))],
        out_specs=pl.BlockSpec((K, tn), lambda j, t: (0, j)),
        out_shape=jax.ShapeDtypeStruct((K, N), BF16),
        scratch_shapes=[pltpu.VMEM((K, tn), F32)],
        compiler_params=_cparams(("arbitrary", "arbitrary")),
    )(xa, ya)


_PROJ_SPLITS = (0, 1024, 2560, 2944, 3200, 3328)


def _inproj_fwd(x, nw, sh, sc, win, seq):
    T, D = x.shape
    tm = min(WIDE_TOKEN_TILE, seq)
    tps = seq // tm
    widths = [b - a for a, b in zip(_PROJ_SPLITS[:-1], _PROJ_SPLITS[1:])]
    dtypes = [BF16, BF16, F32, F32, F32]

    def body(x_ref, nw_ref, sh_ref, sc_ref, w_ref, *outs):
        xh, _ = _rms_stats(x_ref[...])
        h = (xh * nw_ref[...]) * (1.0 + sc_ref[0]) + sh_ref[0]
        proj = _dot_nt(h.astype(BF16), w_ref[...])
        for o, lo, hi in zip(outs, _PROJ_SPLITS[:-1], _PROJ_SPLITS[1:]):
            o[...] = proj[:, lo:hi].astype(o.dtype)

    rows = lambda n: pl.BlockSpec((tm, n), lambda i: (i, 0))
    perb = pl.BlockSpec((1, 1, D), lambda i: (i // tps, 0, 0))
    return pl.pallas_call(
        body, grid=(T // tm,), name="inproj_fwd",
        in_specs=[rows(D), _resident((1, D)), perb, perb, _resident((D_PROJ, D))],
        out_specs=[rows(w) for w in widths],
        out_shape=[jax.ShapeDtypeStruct((T, w), dt) for w, dt in zip(widths, dtypes)],
        compiler_params=_cparams(("arbitrary",)),
    )(x, nw, sh, sc, win)


def _inproj_bwd(dx2, x, nw, sh, sc, win, dz, dxbc, dcq, dckv, ddtk_a, ddtk_b, seq):
    T, D = x.shape
    B = T // seq
    tm = min(TOKEN_TILE, seq)
    tps = seq // tm

    def body(dx2_ref, x_ref, nw_ref, sh_ref, sc_ref, w_ref, dz_ref, dxbc_ref, dcq_ref, dckv_ref, da_ref, db_ref,
             dx_ref, h_ref, dp_ref, dsh_ref, dsc_ref, dnw_ref):
        i = pl.program_id(0)

        @pl.when(i % tps == 0)
        def _():
            dsh_ref[...] = jnp.zeros_like(dsh_ref)
            dsc_ref[...] = jnp.zeros_like(dsc_ref)

        @pl.when(i == 0)
        def _():
            dnw_ref[...] = jnp.zeros_like(dnw_ref)

        dproj = jnp.concatenate(
            [dz_ref[...], dxbc_ref[...], dcq_ref[...].astype(BF16), dckv_ref[...].astype(BF16),
             (da_ref[...] + db_ref[...]).astype(BF16)], axis=1)
        dp_ref[...] = dproj
        dh = _dot(dproj, w_ref[...])
        xh, r = _rms_stats(x_ref[...])
        nwv = nw_ref[...]
        n = xh * nwv
        scale1 = 1.0 + sc_ref[0]
        dsc_ref[0] += _colsum(dh * n)
        dsh_ref[0] += _colsum(dh)
        dx, dw_rows = _rms_bwd(dh * scale1, xh, r, nwv)
        dnw_ref[...] += _colsum(dw_rows)
        dx_ref[...] = dx2_ref[...] + dx
        h_ref[...] = (n * scale1 + sh_ref[0]).astype(BF16)

    rows = lambda n: pl.BlockSpec((tm, n), lambda i: (i, 0))
    perb = pl.BlockSpec((1, 1, D), lambda i: (i // tps, 0, 0))
    sd = jax.ShapeDtypeStruct
    return pl.pallas_call(
        body, grid=(T // tm,), name="inproj_bwd",
        in_specs=[rows(D), rows(D), _resident((1, D)), perb, perb, _resident((D_PROJ, D)),
                  rows(1024), rows(D_CONV), rows(Q_LORA), rows(KV_LORA), rows(LANES), rows(LANES)],
        out_specs=[rows(D), rows(D), rows(D_PROJ), perb, perb, pl.BlockSpec((1, D), lambda i: (0, 0))],
        out_shape=[sd((T, D), F32), sd((T, D), BF16), sd((T, D_PROJ), BF16), sd((B, 1, D), F32), sd((B, 1, D), F32),
                   sd((1, D), F32)],
        compiler_params=_cparams(("arbitrary",)),
    )(dx2, x, nw, sh, sc, win, dz, dxbc, dcq, dckv, ddtk_a, ddtk_b)


SUBLANES = 8


def _shift_down(v, k):
    r = pltpu.roll(v, k, 0)
    row = lax.broadcasted_iota(jnp.int32, (SUBLANES, v.shape[1]), 0)
    return jnp.concatenate([jnp.where(row < k, 0.0, r[:SUBLANES]), r[SUBLANES:]], axis=0)


def _shift_up(v, k):
    n = v.shape[0]
    r = pltpu.roll(v, n - k, 0)
    row = lax.broadcasted_iota(jnp.int32, (SUBLANES, v.shape[1]), 0)
    return jnp.concatenate([r[:n - SUBLANES], jnp.where(row >= SUBLANES - k, 0.0, r[n - SUBLANES:])], axis=0)


def _conv_pre(xv, w_ref, b_ref):
    pre = b_ref[...] + w_ref[CONV_WIDTH - 1:CONV_WIDTH, :] * xv
    for k in range(1, CONV_WIDTH):
        pre = pre + w_ref[CONV_WIDTH - 1 - k:CONV_WIDTH - k, :] * _shift_down(xv, k)
    return pre


def _conv_fwd(xraw, cw, cb):
    B, S, C = xraw.shape

    def body(x_ref, w_ref, b_ref, o_ref):
        pre = _conv_pre(x_ref[0].astype(F32), w_ref, b_ref)
        o_ref[0] = (pre * _sigmoid(pre)).astype(BF16)

    blk = pl.BlockSpec((1, S, LANES), lambda b, j: (b, 0, j))
    return pl.pallas_call(
        body, grid=(B, C // LANES), name="conv_fwd",
        in_specs=[blk, pl.BlockSpec((CONV_WIDTH, LANES), lambda b, j: (0, j)), pl.BlockSpec((1, LANES), lambda b, j: (0, j))],
        out_specs=blk, out_shape=jax.ShapeDtypeStruct((B, S, C), BF16),
        compiler_params=_cparams(("arbitrary", "arbitrary")),
    )(xraw, cw, cb)


def _conv_bwd(dout, xraw, cw, cb):
    B, S, C = xraw.shape

    def body(d_ref, x_ref, w_ref, b_ref, dx_ref, dw_ref, db_ref):
        @pl.when(pl.program_id(1) == 0)
        def _():
            dw_ref[...] = jnp.zeros_like(dw_ref)
            db_ref[...] = jnp.zeros_like(db_ref)

        xv = x_ref[0].astype(F32)
        pre = _conv_pre(xv, w_ref, b_ref)
        sig = _sigmoid(pre)
        dpre = d_ref[0].astype(F32) * (sig * (1.0 + pre * (1.0 - sig)))
        dx = w_ref[CONV_WIDTH - 1:CONV_WIDTH, :] * dpre
        for k in range(1, CONV_WIDTH):
            dx = dx + w_ref[CONV_WIDTH - 1 - k:CONV_WIDTH - k, :] * _shift_up(dpre, k)
        dx_ref[0] = dx.astype(BF16)
        db_ref[...] += _colsum(dpre)
        dws = [_colsum(dpre * (xv if k == 0 else _shift_down(xv, k))) for k in range(CONV_WIDTH - 1, -1, -1)]
        dw_ref[...] += jnp.concatenate(dws, axis=0)

    blk = pl.BlockSpec((1, S, LANES), lambda j, b: (b, 0, j))
    wspec = pl.BlockSpec((CONV_WIDTH, LANES), lambda j, b: (0, j))
    bspec = pl.BlockSpec((1, LANES), lambda j, b: (0, j))
    return pl.pallas_call(
        body, grid=(C // LANES, B), name="conv_bwd",
        in_specs=[blk, blk, wspec, bspec], out_specs=[blk, wspec, bspec],
        out_shape=[jax.ShapeDtypeStruct((B, S, C), BF16), jax.ShapeDtypeStruct((CONV_WIDTH, C), F32),
                   jax.ShapeDtypeStruct((1, C), F32)],
        compiler_params=_cparams(("arbitrary", "arbitrary")),
    )(dout, xraw, cw, cb)


def _softplus(x):
    return jnp.maximum(x, 0.0) + jnp.log(1.0 + jnp.exp(-jnp.abs(x)))


def _ssd_common(xbc_ref, dtk_ref, dtb_ref, alog_ref, e_ref):
    L = CHUNK
    xbc = xbc_ref[0]
    xs = xbc[:, :D_SSD].astype(F32)
    bm = xbc[:, D_SSD:D_SSD + 256]
    cm = xbc[:, D_SSD + 256:D_SSD + 512]
    head = lax.broadcasted_iota(jnp.int32, (1, LANES), 1) < SSD_HEADS
    a128 = jnp.where(head, -jnp.exp(alog_ref[...]), 0.0)
    pre = dtk_ref[0] + dtb_ref[...]
    dt = _softplus(pre)
    dA = dt * a128
    row = lax.broadcasted_iota(jnp.int32, (L, L), 0)
    col = lax.broadcasted_iota(jnp.int32, (L, L), 1)
    causal = col <= row
    tri = causal.astype(F32)
    triT = (row <= col).astype(F32)
    tri = causal.astype(BF16)
    triT = (row <= col).astype(BF16)
    dA3 = _split3(dA)
    acum = _sum3(lambda part: _dot(tri, part), dA3)
    acumT = _sum3(lambda part: _dot_tn(part, triT), dA3)
    E = e_ref[...]
    acum_f = _spread(acum, E)
    dt_f = _spread(dt, E)
    e_f = jnp.exp(acum_f)
    w_f = jnp.exp(acum_f[L - 1:L, :] - acum_f)
    xt = xs * dt_f
    return dict(xs=xs, bm=bm, cm=cm, a128=a128, pre=pre, dt=dt, causal=causal, tri=tri, triT=triT, acum=acum,
                acumT=acumT, E=E, dt_f=dt_f, e_f=e_f, w_f=w_f, xt=xt, head=head)


def _split3(x):
    p1 = x.astype(BF16)
    r1 = x - p1.astype(F32)
    p2 = r1.astype(BF16)
    return p1, p2, (r1 - p2.astype(F32)).astype(BF16)


def _sum3(mm, parts):
    return (mm(parts[0]) + mm(parts[1])) + mm(parts[2])


def _spread(v, e):
    return _sum3(lambda part: _dot(part, e), _split3(v))


def _gather_heads(v, e):
    return _sum3(lambda part: _dot_nt(part, e), _split3(v))


def _head_mask(k):
    lane = lax.broadcasted_iota(jnp.int32, (CHUNK, LANES), 1)
    return (lane >= SSD_HEAD_DIM) if k == 1 else (lane < SSD_HEAD_DIM)


def _pair_decay(alast, h0):
    row = lax.broadcasted_iota(jnp.int32, (2 * SSD_HEAD_DIM, SSD_STATE), 0)
    return jnp.exp(jnp.where(row < SSD_HEAD_DIM, alast[:, h0:h0 + 1], alast[:, h0 + 1:h0 + 2]))


def _decay_matrix(q, h):
    seg = q["acum"][:, h:h + 1] - q["acumT"][h:h + 1, :]
    return jnp.exp(jnp.where(q["causal"], seg, -1e30))


def _gated_norm(y, zz, nw):
    sig = _sigmoid(zz)
    sil = zz * sig
    yg = y * sil
    half = D_SSD // SSD_GROUPS
    parts = []
    for g in range(SSD_GROUPS):
        xh, r = _rms_stats(yg[:, g * half:(g + 1) * half])
        parts.append((xh, r))
    return sig, sil, parts


def _ssd_fwd(xbc, dtk, z, dtb, alog, dsk, nw, expand):
    B, S, _ = xbc.shape
    L = CHUNK
    nc = S // L

    def body(xbc_ref, dtk_ref, z_ref, dtb_ref, alog_ref, dsk_ref, nw_ref, e_ref, y_ref, ys_ref, prev_ref, st_ref):
        @pl.when(pl.program_id(0) == 0)
        def _():
            st_ref[...] = jnp.zeros_like(st_ref)

        for b in range(B):
            one = lambda ref: ref.at[pl.ds(b, 1)]
            sequence_step(one(xbc_ref), one(dtk_ref), one(z_ref), dtb_ref, alog_ref, dsk_ref, nw_ref, e_ref, one(y_ref),
                          one(ys_ref), one(prev_ref), st_ref.at[b])

    def sequence_step(xbc_ref, dtk_ref, z_ref, dtb_ref, alog_ref, dsk_ref, nw_ref, e_ref, y_ref, ys_ref, prev_ref, st_ref):
        q = _ssd_common(xbc_ref, dtk_ref, dtb_ref, alog_ref, e_ref)
        xtb = q["xt"].astype(BF16)
        xwb = (q["xt"] * q["w_f"]).astype(BF16)
        alast = q["acum"][L - 1:L, :]
        ys = []
        for g in range(SSD_GROUPS):
            bg = q["bm"][:, g * 128:(g + 1) * 128]
            cg = q["cm"][:, g * 128:(g + 1) * 128]
            G = _dot_nt(cg, bg)
            for pr in range(SSD_HEADS // SSD_GROUPS // 2):
                h0 = g * 8 + 2 * pr
                lo = h0 * SSD_HEAD_DIM
                xt_p = xtb[:, lo:lo + 128]
                ydiag = jnp.zeros((L, LANES), F32)
                for k in range(2):
                    M = (G * _decay_matrix(q, h0 + k)).astype(BF16)
                    ydiag = ydiag + _dot(M, jnp.where(_head_mask(k), xt_p, jnp.zeros_like(xt_p)))
                hp = st_ref[lo:lo + 128, :]
                prev_ref[0, 0, lo:lo + 128, :] = hp.astype(BF16)
                zoff = _dot_nt(cg, hp.astype(BF16))
                ys.append(ydiag + zoff * q["e_f"][:, lo:lo + 128])
                st_ref[lo:lo + 128, :] = _pair_decay(alast, h0) * hp + _dot_tn(xwb[:, lo:lo + 128], bg)
        y = jnp.concatenate(ys, axis=1) + dsk_ref[...] * q["xs"]
        y_ref[0] = y.astype(BF16)
        _, _, parts = _gated_norm(y, z_ref[0].astype(F32), nw_ref[...])
        half = D_SSD // SSD_GROUPS
        ys_ref[0] = jnp.concatenate(
            [xh * nw_ref[:, g * half:(g + 1) * half] for g, (xh, _) in enumerate(parts)], axis=1).astype(BF16)

    chunk = lambda n: pl.BlockSpec((B, L, n), lambda c: (0, c, 0))
    vec = pl.BlockSpec((1, LANES), lambda c: (0, 0))
    return pl.pallas_call(
        body, grid=(nc,), name="ssd_fwd",
        in_specs=[chunk(D_CONV), chunk(LANES), chunk(D_SSD), vec, vec, pl.BlockSpec((1, D_SSD), lambda c: (0, 0)),
                  pl.BlockSpec((1, D_SSD), lambda c: (0, 0)), pl.BlockSpec((LANES, D_SSD), lambda c: (0, 0))],
        out_specs=[chunk(D_SSD), chunk(D_SSD), pl.BlockSpec((B, 1, D_SSD, SSD_STATE), lambda c: (0, c, 0, 0))],
        out_shape=[jax.ShapeDtypeStruct((B, S, D_SSD), BF16), jax.ShapeDtypeStruct((B, S, D_SSD), BF16),
                   jax.ShapeDtypeStruct((B, nc, D_SSD, SSD_STATE), BF16)],
        scratch_shapes=[pltpu.VMEM((B, D_SSD, SSD_STATE), F32)],
        compiler_params=_cparams(("arbitrary",)),
    )(xbc, dtk, z, dtb, alog, dsk, nw, expand)


def _ssd_bwd(xbc, dtk, z, y, prev, dys, dtb, alog, dsk, nw, expand):
    B, S, _ = xbc.shape
    L = CHUNK
    nc = S // L
    half = D_SSD // SSD_GROUPS

    def body(xbc_ref, dtk_ref, z_ref, y_ref, prev_ref, dys_ref, dtb_ref, alog_ref, dsk_ref, nw_ref, e_ref,
             dxbc_ref, ddtk_ref, dz_ref, dnw_ref, dvec_ref, dh_ref, dskc_ref):
        @pl.when(pl.program_id(0) == 0)
        def _():
            dnw_ref[...] = jnp.zeros_like(dnw_ref)
            dvec_ref[...] = jnp.zeros_like(dvec_ref)
            dskc_ref[...] = jnp.zeros_like(dskc_ref)
            dh_ref[...] = jnp.zeros_like(dh_ref)

        for b in range(B):
            one = lambda ref: ref.at[pl.ds(b, 1)]
            sequence_step(one(xbc_ref), one(dtk_ref), one(z_ref), one(y_ref), one(prev_ref), one(dys_ref), dtb_ref, alog_ref,
                          dsk_ref, nw_ref, e_ref, one(dxbc_ref), one(ddtk_ref), one(dz_ref), dnw_ref, dvec_ref, dh_ref.at[b],
                          dskc_ref)

        @pl.when(pl.program_id(0) == nc - 1)
        def _():
            dvec_ref[2:3, :] = _gather_heads(jnp.broadcast_to(dskc_ref[...], (8, D_SSD)), e_ref[...])[0:1, :]

    def sequence_step(xbc_ref, dtk_ref, z_ref, y_ref, prev_ref, dys_ref, dtb_ref, alog_ref, dsk_ref, nw_ref, e_ref,
                      dxbc_ref, ddtk_ref, dz_ref, dnw_ref, dvec_ref, dh_ref, dskc_ref):
        q = _ssd_common(xbc_ref, dtk_ref, dtb_ref, alog_ref, e_ref)
        E = q["E"]
        xs = q["xs"]
        yv = y_ref[0].astype(F32)
        zz = z_ref[0].astype(F32)
        sig, sil, parts = _gated_norm(yv, zz, nw_ref[...])
        dn = dys_ref[0].astype(F32)
        dyg, dnw_rows = [], []
        for g, (xh, r) in enumerate(parts):
            dpart, dw_rows = _rms_bwd(dn[:, g * half:(g + 1) * half], xh, r, nw_ref[:, g * half:(g + 1) * half])
            dyg.append(dpart)
            dnw_rows.append(dw_rows)
        dyg = jnp.concatenate(dyg, axis=1)
        dnw_ref[...] += _colsum(jnp.concatenate(dnw_rows, axis=1))
        dY = dyg * sil
        dz_ref[0] = (dyg * yv * (sig * (1.0 + zz * (1.0 - sig)))).astype(BF16)
        dsk_f = dsk_ref[...]
        dskc_ref[...] += _colsum(dY * xs)
        dYb = dY.astype(BF16)
        xtb = q["xt"].astype(BF16)
        xwb = (q["xt"] * q["w_f"]).astype(BF16)
        acum = q["acum"]
        alast = acum[L - 1:L, :]
        lane_id = lax.broadcasted_iota(jnp.int32, (L, LANES), 1)
        sub_id = lax.broadcasted_iota(jnp.int32, (LANES, L), 0)
        lane_row = lax.broadcasted_iota(jnp.int32, (1, LANES), 1)
        da_rows = jnp.zeros((L, LANES), F32)
        daT = jnp.zeros((LANES, L), F32)
        dxt, prod_off, prod_st, dbs, dcs = [], [], [], [], []
        hsum_row = jnp.zeros((1, LANES), F32)
        for g in range(SSD_GROUPS):
            bg = q["bm"][:, g * 128:(g + 1) * 128]
            cg = q["cm"][:, g * 128:(g + 1) * 128]
            G = _dot_nt(cg, bg)
            dG = jnp.zeros((L, L), F32)
            dcg = jnp.zeros((L, SSD_STATE), F32)
            dbg = jnp.zeros((L, SSD_STATE), F32)
            for pr in range(SSD_HEADS // SSD_GROUPS // 2):
                h0 = g * 8 + 2 * pr
                lo = h0 * SSD_HEAD_DIM
                cols = slice(lo, lo + 128)
                dY_p = dYb[:, cols]
                xt_p = xtb[:, cols]
                dxt_p = jnp.zeros((L, LANES), F32)
                for k in range(2):
                    h = h0 + k
                    Lm = _decay_matrix(q, h)
                    Mf = G * Lm
                    dYk = jnp.where(_head_mask(k), dY_p, jnp.zeros_like(dY_p))
                    dM = _dot_nt(dYk, xt_p)
                    dxt_p = dxt_p + _dot_tn(Mf.astype(BF16), dYk)
                    dG = dG + dM * Lm
                    Q = dM * Mf
                    da_rows = da_rows + jnp.where(lane_id == h, jnp.sum(Q, axis=1, keepdims=True), 0.0)
                    daT = daT + jnp.where(sub_id == h, jnp.sum(Q, axis=0, keepdims=True), 0.0)
                hpb = prev_ref[0, 0, lo:lo + 128, :]
                hp = hpb.astype(F32)
                zoff = _dot_nt(cg, hpb)
                e_p = q["e_f"][:, cols]
                dY_pf = dY[:, cols]
                dZb = (dY_pf * e_p).astype(BF16)
                dcg = dcg + _dot(dZb, hpb)
                dhp_off = _dot_tn(dZb, cg)
                prod_off.append(dY_pf * zoff * e_p)
                dS = dh_ref[lo:lo + 128, :]
                dSb = dS.astype(BF16)
                U = _dot_nt(bg, dSb)
                dxt_p = dxt_p + U * q["w_f"][:, cols]
                dbg = dbg + _dot(xwb[:, cols], dSb)
                prod_st.append(q["xt"][:, cols] * U)
                dh_ref[lo:lo + 128, :] = _pair_decay(alast, h0) * dS + dhp_off
                dsh = dS * hp
                for k in range(2):
                    total = jnp.sum(dsh[k * SSD_HEAD_DIM:(k + 1) * SSD_HEAD_DIM, :], axis=(0, 1), keepdims=True)
                    hsum_row = hsum_row + jnp.where(lane_row == h0 + k, total, 0.0)
                dxt.append(dxt_p)
            dGb = dG.astype(BF16)
            dcs.append(dcg + _dot(dGb, bg))
            dbs.append(dbg + _dot_tn(dGb, cg))
        dxt = jnp.concatenate(dxt, axis=1)
        da_rows = da_rows + _gather_heads(jnp.concatenate(prod_off, axis=1), E)
        dww = _gather_heads(jnp.concatenate(prod_st, axis=1), E) * jnp.exp(alast - acum)
        da_rows = da_rows - dww
        dlast = _colsum(dww) + jnp.exp(alast) * hsum_row
        triT = q["triT"]
        ddA = (_sum3(lambda part: _dot(triT, part), _split3(da_rows))
               - _sum3(lambda part: _dot_nt(triT, part), _split3(daT)) + dlast)
        ddA = jnp.where(q["head"], ddA, 0.0)
        ddt = ddA * q["a128"] + _gather_heads(dxt * xs, E)
        ddt_raw = jnp.where(q["head"], ddt * _sigmoid(q["pre"]), 0.0)
        ddtk_ref[0] = ddt_raw
        dxs = dxt * q["dt_f"] + dsk_f * dY
        dxbc_ref[0] = jnp.concatenate([dxs] + dbs + dcs, axis=1).astype(BF16)
        dvec_ref[0:1, :] += _colsum(ddt_raw)
        dvec_ref[1:2, :] += _colsum(ddA * q["dt"]) * q["a128"]

    rev = lambda n: pl.BlockSpec((B, L, n), lambda c: (0, nc - 1 - c, 0))
    vec = pl.BlockSpec((1, LANES), lambda c: (0, 0))
    sd = jax.ShapeDtypeStruct
    return pl.pallas_call(
        body, grid=(nc,), name="ssd_bwd",
        in_specs=[rev(D_CONV), rev(LANES), rev(D_SSD), rev(D_SSD),
                  pl.BlockSpec((B, 1, D_SSD, SSD_STATE), lambda c: (0, nc - 1 - c, 0, 0)), rev(D_SSD), vec, vec,
                  pl.BlockSpec((1, D_SSD), lambda c: (0, 0)),
                  pl.BlockSpec((1, D_SSD), lambda c: (0, 0)), pl.BlockSpec((LANES, D_SSD), lambda c: (0, 0))],
        out_specs=[rev(D_CONV), rev(LANES), rev(D_SSD), pl.BlockSpec((1, D_SSD), lambda c: (0, 0)),
                   pl.BlockSpec((8, LANES), lambda c: (0, 0))],
        out_shape=[sd((B, S, D_CONV), BF16), sd((B, S, LANES), F32), sd((B, S, D_SSD), BF16), sd((1, D_SSD), F32),
                   sd((8, LANES), F32)],
        scratch_shapes=[pltpu.VMEM((B, D_SSD, SSD_STATE), F32), pltpu.VMEM((1, D_SSD), F32)],
        compiler_params=_cparams(("arbitrary",)),
    )(xbc, dtk, z, y, prev, dys, dtb, alog, dsk, nw, expand)


def _rope_tables(pos_ref, invf_ref, place_ref):
    ang = invf_ref[...] * pos_ref[0].astype(F32)
    place = place_ref[...]
    cosf = 1.0 + _sum3(lambda part: _dot_tn(part, place), _split3(jnp.cos(ang) - 1.0))
    sinf = _sum3(lambda part: _dot_tn(part, place), _split3(jnp.sin(ang)))
    return cosf, sinf


def _rot(u):
    lane = lax.broadcasted_iota(jnp.int32, u.shape, 1)
    first = (lane >= QK_NOPE) & (lane < QK_NOPE + QK_ROPE // 2)
    second = (lane >= QK_NOPE + QK_ROPE // 2) & (lane < QK_DIM)
    return jnp.where(first, -pltpu.roll(u, LANES - QK_ROPE // 2, 1), jnp.where(second, pltpu.roll(u, QK_ROPE // 2, 1), 0.0))


def _rope_lanes(shape):
    lane = lax.broadcasted_iota(jnp.int32, shape, 1)
    return (lane >= QK_NOPE) & (lane < QK_DIM)


def _mla_prep(cq, ckv, dtk, pos, qw, kvw, wuq, wukv, invf, place):
    T = cq.shape[0]
    tm = min(WIDE_TOKEN_TILE, T)
    scale = 1.0 / math.sqrt(QK_DIM)
    HW = MLA_HEADS * HEAD_LANES

    def body(cq_ref, ckv_ref, dtk_ref, pos_ref, qw_ref, kvw_ref, wuq_ref, wukv_ref, invf_ref, place_ref, q_ref, k_ref, v_ref,
             cos_ref, sin_ref):
        xh, _ = _rms_stats(cq_ref[...])
        qv = _dot((xh * qw_ref[...]).astype(BF16), wuq_ref[...])
        xh, _ = _rms_stats(ckv_ref[...])
        kv = _dot((xh * kvw_ref[...]).astype(BF16), wukv_ref[...])
        cosf, sinf = _rope_tables(pos_ref, invf_ref, place_ref)
        cos_ref[...] = cosf
        sin_ref[...] = sinf
        rope = lambda u: u * cosf + _rot(u) * sinf
        dtkv = dtk_ref[...]
        kr = rope(jnp.where(_rope_lanes(dtkv.shape), dtkv, 0.0))
        for h in range(MLA_HEADS):
            cols = slice(h * HEAD_LANES, (h + 1) * HEAD_LANES)
            q_ref[:, cols] = (rope(qv[:, cols]) * scale).astype(BF16)
            k_ref[:, cols] = (kv[:, cols] + kr).astype(BF16)
        v_ref[...] = kv[:, HW:].astype(BF16)

    rows = lambda n: pl.BlockSpec((tm, n), lambda i: (i, 0))
    return pl.pallas_call(
        body, grid=(T // tm,), name="mla_prep",
        in_specs=[rows(Q_LORA), rows(KV_LORA), rows(LANES), pl.BlockSpec((1, 1, tm), lambda i: (i, 0, 0)),
                  _resident((1, Q_LORA)), _resident((1, KV_LORA)), _resident((Q_LORA, HW)), _resident((KV_LORA, 2 * HW)),
                  _resident((QK_ROPE // 2, 1)), _resident((QK_ROPE // 2, LANES))],
        out_specs=[rows(HW), rows(HW), rows(HW), rows(LANES), rows(LANES)],
        out_shape=[jax.ShapeDtypeStruct((T, HW), BF16)] * 3 + [jax.ShapeDtypeStruct((T, LANES), F32)] * 2,
        compiler_params=_cparams(("arbitrary",)),
    )(cq, ckv, dtk, pos.reshape(T // tm, 1, tm), qw, kvw, wuq, wukv, invf, place)


def _mla_prep_bwd(dq, dk, dv, cq, ckv, cos_t, sin_t, qw, kvw, wuq, wukv):
    T = cq.shape[0]
    tm = min(WIDE_TOKEN_TILE, T)
    scale = 1.0 / math.sqrt(QK_DIM)
    HW = MLA_HEADS * HEAD_LANES

    def body(dq_ref, dk_ref, dv_ref, cq_ref, ckv_ref, cos_ref, sin_ref, qw_ref, kvw_ref, wuq_ref, wukv_ref,
             dcq_ref, dckv_ref, ddtk_ref, qn_ref, kvn_ref, dqo_ref, dkvo_ref, dqw_ref, dkvw_ref):
        @pl.when(pl.program_id(0) == 0)
        def _():
            dqw_ref[...] = jnp.zeros_like(dqw_ref)
            dkvw_ref[...] = jnp.zeros_like(dkvw_ref)

        cosf, sinf = cos_ref[...], sin_ref[...]
        unrope = lambda d: d * cosf - _rot(d * sinf)
        dkr = jnp.zeros((tm, LANES), F32)
        nope = lax.broadcasted_iota(jnp.int32, (tm, LANES), 1) < QK_NOPE
        for h in range(MLA_HEADS):
            cols = slice(h * HEAD_LANES, (h + 1) * HEAD_LANES)
            dqo_ref[:, cols] = unrope(dq_ref[:, cols].astype(F32) * scale).astype(BF16)
            dkh = dk_ref[:, cols].astype(F32)
            dkr = dkr + jnp.where(_rope_lanes(dkh.shape), dkh, 0.0)
            dkvo_ref[:, cols] = jnp.where(nope, dkh, 0.0).astype(BF16)
        dkvo_ref[:, HW:] = dv_ref[...].astype(BF16)
        ddtk_ref[...] = unrope(dkr)
        xh, r = _rms_stats(cq_ref[...])
        qn_ref[...] = (xh * qw_ref[...]).astype(BF16)
        dx, dw_rows = _rms_bwd(_dot_nt(dqo_ref[...], wuq_ref[...]), xh, r, qw_ref[...])
        dcq_ref[...] = dx
        dqw_ref[...] += _colsum(dw_rows)
        xh, r = _rms_stats(ckv_ref[...])
        kvn_ref[...] = (xh * kvw_ref[...]).astype(BF16)
        dx, dw_rows = _rms_bwd(_dot_nt(dkvo_ref[...], wukv_ref[...]), xh, r, kvw_ref[...])
        dckv_ref[...] = dx
        dkvw_ref[...] += _colsum(dw_rows)

    rows = lambda n: pl.BlockSpec((tm, n), lambda i: (i, 0))
    sd = jax.ShapeDtypeStruct
    return pl.pallas_call(
        body, grid=(T // tm,), name="mla_prep_bwd",
        in_specs=[rows(HW), rows(HW), rows(HW), rows(Q_LORA), rows(KV_LORA), rows(LANES), rows(LANES), _resident((1, Q_LORA)),
                  _resident((1, KV_LORA)), _resident((Q_LORA, HW)), _resident((KV_LORA, 2 * HW))],
        out_specs=[rows(Q_LORA), rows(KV_LORA), rows(LANES), rows(Q_LORA), rows(KV_LORA), rows(HW), rows(2 * HW),
                   pl.BlockSpec((1, Q_LORA), lambda i: (0, 0)), pl.BlockSpec((1, KV_LORA), lambda i: (0, 0))],
        out_shape=[sd((T, Q_LORA), F32), sd((T, KV_LORA), F32), sd((T, LANES), F32), sd((T, Q_LORA), BF16),
                   sd((T, KV_LORA), BF16), sd((T, HW), BF16), sd((T, 2 * HW), BF16), sd((1, Q_LORA), F32),
                   sd((1, KV_LORA), F32)],
        compiler_params=_cparams(("arbitrary",)),
    )(dq, dk, dv, cq, ckv, cos_t, sin_t, qw, kvw, wuq, wukv)


def _causal_mask(t):
    row = lax.broadcasted_iota(jnp.int32, (t, t), 0)
    col = lax.broadcasted_iota(jnp.int32, (t, t), 1)
    return col <= row


def _attn_fwd(q, k, v):
    B, S, HW = q.shape
    H = HW // HEAD_LANES
    t = min(ATTN_FWD_Q_TILE, S)
    tk = min(ATTN_FWD_KV_TILE, t)
    nq = S // t
    per = t // tk

    pair = 4
    pw = pair * HEAD_LANES

    def body(q_ref, k_ref, v_ref, o_ref, lse_ref):
        qi = pl.program_id(2)
        lanes = [slice(hh * HEAD_LANES, (hh + 1) * HEAD_LANES) for hh in range(pair)]
        qs = [q_ref[0, :, cols] for cols in lanes]

        def step(j, carry, diag):
            sl = pl.ds(pl.multiple_of(j * tk, tk), tk)
            out = []
            for qv, cols, (m, l, acc) in zip(qs, lanes, carry):
                s = _dot_nt(qv, k_ref[0, sl, cols])
                if diag is not None:
                    row = lax.broadcasted_iota(jnp.int32, (t, tk), 0)
                    col = lax.broadcasted_iota(jnp.int32, (t, tk), 1)
                    s = jnp.where(col + diag * tk <= row, s, -1e30)
                m_new = jnp.maximum(m, jnp.max(s, axis=-1, keepdims=True))
                alpha = jnp.exp(m - m_new)
                p = jnp.exp(s - m_new)
                l = alpha * l + jnp.sum(p, axis=-1, keepdims=True)
                acc = alpha * acc + _dot(p.astype(BF16), v_ref[0, sl, cols])
                out.append((m_new, l, acc))
            return tuple(out)

        init = tuple((jnp.full((t, 1), -1e30, F32), jnp.zeros((t, 1), F32), jnp.zeros((t, HEAD_LANES), F32))
                     for _ in range(pair))
        carry = lax.fori_loop(0, qi * per, lambda j, c: step(j, c, None), init)
        for d in range(per):
            carry = step(qi * per + d, carry, d)
        for hh, (m, l, acc) in enumerate(carry):
            o_ref[0, :, lanes[hh]] = (acc / l).astype(BF16)
            lse_ref[0, hh] = m + jnp.log(l)

    return pl.pallas_call(
        body, grid=(B, H // pair, nq), name="attn_fwd",
        in_specs=[pl.BlockSpec((1, t, pw), lambda b, h, i: (b, i, h)),
                  pl.BlockSpec((1, S, pw), lambda b, h, i: (b, 0, h)),
                  pl.BlockSpec((1, S, pw), lambda b, h, i: (b, 0, h))],
        out_specs=[pl.BlockSpec((1, t, pw), lambda b, h, i: (b, i, h)),
                   pl.BlockSpec((1, pair, t, 1), lambda b, h, i: (b, h, i, 0))],
        out_shape=[jax.ShapeDtypeStruct((B, S, HW), BF16), jax.ShapeDtypeStruct((B, H, S, 1), F32)],
        compiler_params=_cparams(("arbitrary", "arbitrary", "arbitrary")),
    )(q, k, v)


def _attn_bwd(q, k, v, o, do, lse):
    B, S, HW = q.shape
    H = HW // HEAD_LANES
    t = min(ATTN_BWD_TILE, S)
    nq = S // t

    pair = 2
    pw = pair * HEAD_LANES

    def body(q_ref, k_ref, v_ref, o_ref, do_ref, lse_ref, dq_out_ref, dk_ref, dv_ref, dq_ref):
        j = pl.program_id(2)

        @pl.when(j == 0)
        def _():
            dq_ref[...] = jnp.zeros_like(dq_ref)

        lanes = [slice(hh * HEAD_LANES, (hh + 1) * HEAD_LANES) for hh in range(pair)]

        def step(i, carry, masked):
            sl = pl.ds(pl.multiple_of(i * t, t), t)
            out = []
            for hh, (cols, (dk, dv)) in enumerate(zip(lanes, carry)):
                kj = k_ref[0, :, cols]
                qi = q_ref[0, sl, cols]
                doi = do_ref[0, sl, cols]
                s = _dot_nt(qi, kj)
                if masked:
                    s = jnp.where(_causal_mask(t), s, -1e30)
                p = jnp.exp(s - lse_ref[0, hh, sl, :])
                dv = dv + _dot_tn(p.astype(BF16), doi)
                dp = _dot_nt(doi, v_ref[0, :, cols])
                delta = jnp.sum(doi.astype(F32) * o_ref[0, sl, cols].astype(F32), axis=-1, keepdims=True)
                dsb = (p * (dp - delta)).astype(BF16)
                dk = dk + _dot_tn(dsb, qi)
                dq_ref[sl, cols] += _dot(dsb, kj)
                out.append((dk, dv))
            return tuple(out)

        zero = jnp.zeros((t, HEAD_LANES), F32)
        carry = step(j, ((zero, zero),) * pair, True)
        carry = lax.fori_loop(j + 1, nq, lambda i, c: step(i, c, False), carry)
        for cols, (dk, dv) in zip(lanes, carry):
            dk_ref[0, :, cols] = dk.astype(BF16)
            dv_ref[0, :, cols] = dv.astype(BF16)

        @pl.when(j == nq - 1)
        def _():
            dq_out_ref[0] = dq_ref[...].astype(BF16)

    full = pl.BlockSpec((1, S, pw), lambda b, h, j: (b, 0, h))
    tile = pl.BlockSpec((1, t, pw), lambda b, h, j: (b, j, h))
    sd = jax.ShapeDtypeStruct
    return pl.pallas_call(
        body, grid=(B, H // pair, nq), name="attn_bwd",
        in_specs=[full, tile, tile, full, full, pl.BlockSpec((1, pair, S, 1), lambda b, h, j: (b, h, 0, 0))],
        out_specs=[full, tile, tile],
        out_shape=[sd((B, S, HW), BF16), sd((B, S, HW), BF16), sd((B, S, HW), BF16)],
        scratch_shapes=[pltpu.VMEM((S, pw), F32)],
        compiler_params=_cparams(("arbitrary", "arbitrary", "arbitrary")),
    )(q, k, v, o, do, lse)


def _mix_out(x1, yssd, o, mw, wout, g, seq):
    T, D = x1.shape
    tm = min(WIDE_TOKEN_TILE, seq)
    tps = seq // tm

    def body(x_ref, ys_ref, o_ref, mw_ref, w_ref, g_ref, xo_ref, m_ref, yc_ref):
        xh, _ = _rms_stats(o_ref[...].astype(F32))
        ycat = jnp.concatenate([ys_ref[...], (xh * mw_ref[...]).astype(BF16)], axis=1)
        m = _dot(ycat, w_ref[...])
        xo_ref[...] = x_ref[...] + g_ref[0] * m
        m_ref[...] = m.astype(BF16)
        yc_ref[...] = ycat

    rows = lambda n: pl.BlockSpec((tm, n), lambda i: (i, 0))
    perb = pl.BlockSpec((1, 1, D), lambda i: (i // tps, 0, 0))
    sd = jax.ShapeDtypeStruct
    return pl.pallas_call(
        body, grid=(T // tm,), name="mix_out",
        in_specs=[rows(D), rows(D_SSD), rows(D_MLA), _resident((1, D_MLA)), _resident((D_SSD + D_MLA, D)), perb],
        out_specs=[rows(D), rows(D), rows(D_SSD + D_MLA)],
        out_shape=[sd((T, D), F32), sd((T, D), BF16), sd((T, D_SSD + D_MLA), BF16)],
        compiler_params=_cparams(("arbitrary",)),
    )(x1, yssd, o, mw, wout, g)


def _mix_out_bwd(dx2, m, o, mw, wout, g, seq):
    T, D = dx2.shape
    B = T // seq
    tm = min(WIDE_TOKEN_TILE, seq)
    tps = seq // tm

    def body(dx_ref, m_ref, o_ref, mw_ref, w_ref, g_ref, dys_ref, do_ref, dm_ref, dg_ref, dmw_ref):
        i = pl.program_id(0)

        @pl.when(i % tps == 0)
        def _():
            dg_ref[...] = jnp.zeros_like(dg_ref)

        @pl.when(i == 0)
        def _():
            dmw_ref[...] = jnp.zeros_like(dmw_ref)

        dxv = dx_ref[...]
        dg_ref[0] += _colsum(dxv * m_ref[...].astype(F32))
        dmb = (g_ref[0] * dxv).astype(BF16)
        dm_ref[...] = dmb
        dycat = _dot_nt(dmb, w_ref[...])
        dys_ref[...] = dycat[:, :D_SSD].astype(BF16)
        xh, r = _rms_stats(o_ref[...].astype(F32))
        dx, dw_rows = _rms_bwd(dycat[:, D_SSD:], xh, r, mw_ref[...])
        do_ref[...] = dx.astype(BF16)
        dmw_ref[...] += _colsum(dw_rows)

    rows = lambda n: pl.BlockSpec((tm, n), lambda i: (i, 0))
    perb = pl.BlockSpec((1, 1, D), lambda i: (i // tps, 0, 0))
    sd = jax.ShapeDtypeStruct
    return pl.pallas_call(
        body, grid=(T // tm,), name="mix_out_bwd",
        in_specs=[rows(D), rows(D), rows(D_MLA), _resident((1, D_MLA)), _resident((D_SSD + D_MLA, D)), perb],
        out_specs=[rows(D_SSD), rows(D_MLA), rows(D), perb, pl.BlockSpec((1, D_MLA), lambda i: (0, 0))],
        out_shape=[sd((T, D_SSD), BF16), sd((T, D_MLA), BF16), sd((T, D), BF16), sd((B, 1, D), F32), sd((1, D_MLA), F32)],
        compiler_params=_cparams(("arbitrary",)),
    )(dx2, m, o, mw, wout, g)


def _win_to_kernel(w):
    z0 = jnp.zeros((48, w.shape[1]), w.dtype)
    z1 = jnp.zeros((32, w.shape[1]), w.dtype)
    return jnp.concatenate([w[:2560], w[2576:3216], w[2560:2576], z0, w[3216:3248], z1], axis=0)


def _win_from_kernel(g):
    return jnp.concatenate([g[:2560], g[3200:3216], g[2560:3200], g[3264:3296]], axis=0)


def _wuq_to_kernel(w):
    w = w.reshape(Q_LORA, MLA_HEADS, QK_DIM)
    return jnp.pad(w, ((0, 0), (0, 0), (0, HEAD_LANES - QK_DIM))).reshape(Q_LORA, MLA_HEADS * HEAD_LANES)


def _wuq_from_kernel(g):
    return g.reshape(Q_LORA, MLA_HEADS, HEAD_LANES)[:, :, :QK_DIM].reshape(Q_LORA, MLA_HEADS * QK_DIM)


def _wukv_to_kernel(w):
    w = w.reshape(KV_LORA, MLA_HEADS, QK_NOPE + V_HEAD)
    kp = jnp.pad(w[:, :, :QK_NOPE], ((0, 0), (0, 0), (0, HEAD_LANES - QK_NOPE)))
    return jnp.concatenate([kp.reshape(KV_LORA, -1), w[:, :, QK_NOPE:].reshape(KV_LORA, -1)], axis=1)


def _wukv_from_kernel(g):
    hw = MLA_HEADS * HEAD_LANES
    kp = g[:, :hw].reshape(KV_LORA, MLA_HEADS, HEAD_LANES)[:, :, :QK_NOPE]
    vp = g[:, hw:].reshape(KV_LORA, MLA_HEADS, V_HEAD)
    return jnp.concatenate([kp, vp], axis=2).reshape(KV_LORA, MLA_HEADS * (QK_NOPE + V_HEAD))


def _lanes16(v):
    return jnp.pad(v.reshape(1, SSD_HEADS), ((0, 0), (0, LANES - SSD_HEADS)))


def _constants():
    e = np.zeros((LANES, D_SSD), np.float32)
    for h in range(SSD_HEADS):
        e[h, h * SSD_HEAD_DIM:(h + 1) * SSD_HEAD_DIM] = 1.0
    inv_freq = ROPE_THETA ** (-jnp.arange(0, QK_ROPE, 2, dtype=F32) / QK_ROPE)
    half = QK_ROPE // 2
    place = np.zeros((half, LANES), np.float32)
    for j in range(half):
        place[j, QK_NOPE + j] = place[j, QK_NOPE + half + j] = 1.0
    return jnp.asarray(e, BF16), inv_freq.reshape(half, 1), jnp.asarray(place, BF16)


def _local_step(x, positions, mod, w, later_weights, small, tgt, on_grads, sync):
    B, S, D = x.shape
    T = B * S
    expand, invf, place = _constants()
    x0 = x.reshape(T, D)
    pos = positions.reshape(T)
    mods = [mod[:, i * D:(i + 1) * D].reshape(B, 1, D) for i in range(N_MOD)]
    sh1, sc1, g1, sh2, sc2, g2, sh3, sc3, g3 = mods
    dtb, alog = _lanes16(small["dt_bias"]), _lanes16(small["a_log"])
    dsk = jnp.repeat(small["d_skip"].reshape(1, SSD_HEADS), SSD_HEAD_DIM, axis=1)

    x1, a1, u1, f1 = _ffn_fwd(x0, small["norm_ffn1"], sh1, sc1, g1, w["ffn1_w_gate"], w["ffn1_w_up"], w["ffn1_w_down"], S, "ffn1_fwd")
    w = {**w, **later_weights(f1)}
    z, xraw, cq, ckv, dtk = _inproj_fwd(x1, small["norm_mix"], sh2, sc2, w["w_in"], S)
    xraw3 = xraw.reshape(B, S, D_CONV)
    xbc = _conv_fwd(xraw3, small["conv_w"], small["conv_b"])
    dtk3, z3 = dtk.reshape(B, S, LANES), z.reshape(B, S, D_SSD)
    y, yssd, prev = _ssd_fwd(xbc, dtk3, z3, dtb, alog, dsk, small["ssd_norm_w"], expand)
    q, k, v, cos_t, sin_t = _mla_prep(cq, ckv, dtk, pos, small["q_norm_w"], small["kv_norm_w"], w["w_uq"], w["w_ukv"], invf,
                                      place)
    hw = MLA_HEADS * HEAD_LANES
    q3, k3, v3 = q.reshape(B, S, hw), k.reshape(B, S, hw), v.reshape(B, S, hw)
    o3, lse = _attn_fwd(q3, k3, v3)
    o = o3.reshape(T, hw)
    x2, m, ycat = _mix_out(x1, yssd.reshape(T, D_SSD), o, small["mla_norm_w"], w["w_out"], g2, S)
    dx3, a2, u2, f2, loss, d_norm_final = _ffn_fwd(
        x2, small["norm_ffn2"], sh3, sc3, g3, w["ffn2_w_gate"], w["ffn2_w_up"], w["ffn2_w_down"], S, "ffn2_fwd",
        head=(small["norm_final"].reshape(1, D), tgt.reshape(T, D)))

    gw, gs = {}, {}
    dx2, h3, s3, df3, da3, du3, dsh3, dsc3, dg3, gs["norm_ffn2"] = _ffn_bwd(
        dx3, x2, small["norm_ffn2"], sh3, sc3, g3, a2, u2, f2, w["ffn2_w_gate"], w["ffn2_w_up"], w["ffn2_w_down"], S, "ffn2_bwd")
    gw["ffn2_w_gate"], gw["ffn2_w_up"], gw["ffn2_w_down"] = _ffn_wgrad(h3, s3, df3, da3, du3, dsh3, "ffn2_wgrad")
    g2 = g2 + on_grads(("ffn2_w_gate", "ffn2_w_up", "ffn2_w_down"), gw)

    dys, do, dm, dg2, gs["mla_norm_w"] = _mix_out_bwd(dx2, m, o, small["mla_norm_w"], w["w_out"], g2, S)
    gw["w_out"] = _mm_tn(ycat, dm, 512, "dwout")

    dq3, dk3, dv3 = _attn_bwd(q3, k3, v3, o3, do.reshape(B, S, hw), lse)
    dcq, dckv, ddtk_b, qn, kvn, dqb, dkvb, gs["q_norm_w"], gs["kv_norm_w"] = _mla_prep_bwd(
        dq3.reshape(T, hw), dk3.reshape(T, hw), dv3.reshape(T, hw), cq, ckv, cos_t, sin_t, small["q_norm_w"] + sync(dq3),
        small["kv_norm_w"], w["w_uq"], w["w_ukv"])
    gw["w_uq"] = _mm_tn(qn, dqb, 512, "dwuq")
    gw["w_ukv"] = _mm_tn(kvn, dkvb, 1024, "dwukv")

    dxbc, ddtk_a, dz, gs["ssd_norm_w"], dvec = _ssd_bwd(
        xbc, dtk3, z3, y, prev, dys.reshape(B, S, D_SSD), dtb, alog, dsk, small["ssd_norm_w"], expand)
    gs["dt_bias"], gs["a_log"], gs["d_skip"] = dvec[0:1, :SSD_HEADS], dvec[1:2, :SSD_HEADS], dvec[2:3, :SSD_HEADS]
    dxraw, gs["conv_w"], gs["conv_b"] = _conv_bwd(dxbc, xraw3, small["conv_w"], small["conv_b"])
    dx1, h2, dproj, dsh2, dsc2, gs["norm_mix"] = _inproj_bwd(
        dx2, x1, small["norm_mix"], sh2, sc2, w["w_in"], dz.reshape(T, D_SSD), dxraw.reshape(T, D_CONV), dcq, dckv,
        ddtk_a.reshape(T, LANES), ddtk_b, S)
    gw["w_in"] = _mm_tn(dproj, h2, 512, "dwin")
    g1 = g1 + on_grads(("w_in", "w_uq", "w_ukv", "w_out"), gw)

    dx0, h1, s1, df1, da1, du1, dsh1, dsc1, dg1, gs["norm_ffn1"] = _ffn_bwd(
        dx1, x0, small["norm_ffn1"], sh1, sc1, g1, a1, u1, f1, w["ffn1_w_gate"], w["ffn1_w_up"], w["ffn1_w_down"], S, "ffn1_bwd")
    gw["ffn1_w_gate"], gw["ffn1_w_up"], gw["ffn1_w_down"] = _ffn_wgrad(h1, s1, df1, da1, du1, dsh1 + sync(dx0), "ffn1_wgrad")
    gs["norm_final"] = d_norm_final
    dmod = jnp.concatenate([t.reshape(B, D) for t in (dsh1, dsc1, dg1, dsh2, dsc2, dg2, dsh3, dsc3, dg3)], axis=1)
    return loss, dx0.reshape(B, S, D), gw, dmod, gs


HBM_SPEC = pl.BlockSpec(memory_space=pltpu.HBM)
VMEM_SPEC = pl.BlockSpec(memory_space=pltpu.VMEM)


def _place():
    return lax.axis_index("x"), lax.axis_index("y"), lax.axis_index("c")


def _other_chips(mx, my):
    return [(1 - mx, my), (mx, 1 - my), (1 - mx, 1 - my)]


def _remote(src, dst, send_sem, recv_sem, to):
    return pltpu.make_async_remote_copy(src_ref=src, dst_ref=dst, send_sem=send_sem, recv_sem=recv_sem,
                                        device_id=to, device_id_type=MESH)


def _all_gather_small(xa, name):
    r, n = xa.shape

    def body(x_ref, o_ref, token, send_sems, recv_sems):
        mx, my, mc = _place()
        me = 4 * mx + 2 * my + mc
        token[...] = jnp.zeros_like(token)
        o_ref[pl.ds(me, 1)] = x_ref[...][None]
        sends = []
        for k in range(1, N_DEV):
            peer = (mx ^ (k >> 2), my ^ ((k >> 1) & 1), mc ^ (k & 1))
            cp = _remote(x_ref, o_ref.at[me], send_sems.at[k - 1], recv_sems.at[k - 1], peer)
            cp.start()
            sends.append(cp)
        for k in range(1, N_DEV):
            peer = (mx ^ (k >> 2), my ^ ((k >> 1) & 1), mc ^ (k & 1))
            slot = 4 * peer[0] + 2 * peer[1] + peer[2]
            _remote(x_ref, o_ref.at[slot], send_sems.at[k - 1], recv_sems.at[k - 1], peer).wait_recv()
        for cp in sends:
            cp.wait_send()

    return pl.pallas_call(
        body, name=name, in_specs=[VMEM_SPEC], out_specs=[VMEM_SPEC, VMEM_SPEC],
        out_shape=[jax.ShapeDtypeStruct((N_DEV, r, n), xa.dtype), jax.ShapeDtypeStruct((8, LANES), F32)],
        scratch_shapes=[pltpu.SemaphoreType.DMA((N_DEV - 1,)), pltpu.SemaphoreType.DMA((N_DEV - 1,))],
        compiler_params=pltpu.CompilerParams(vmem_limit_bytes=VMEM_LIMIT),
    )(xa)


def _halves_by_rows(shape):
    return (shape[-2] // 2) % 16 == 0


def _half_shape(shape):
    r, c = shape[-2:]
    return tuple(shape[:-2]) + ((r // 2, c) if _halves_by_rows(shape) else (r, c // 2))


def _half_index(shape, hc):
    r, c = shape[-2:]
    if _halves_by_rows(shape):
        return (pl.ds(pl.multiple_of(hc * (r // 2), 16), r // 2), slice(None))
    return (slice(None), pl.ds(pl.multiple_of(hc * (c // 2), LANES), c // 2))


def _half(ref, hc, lead=None):
    idx = _half_index(ref.shape, hc)
    return ref.at[idx] if lead is None else ref.at[(lead,) + idx]


def _gather_weights(shards):
    n = len(shards)

    def body(*refs):
        w_refs, o_refs, token = refs[:n], refs[n:2 * n], refs[2 * n]
        send_sems, recv_sems, stage_sems = refs[2 * n + 1:2 * n + 4]
        stages = refs[2 * n + 4:]
        mx, my, mc = _place()
        chip = 2 * mx + my
        others = _other_chips(mx, my)
        sibling = (mx, my, 1 - mc)
        token[...] = jnp.zeros_like(token)
        stage_in = [pltpu.make_async_copy(w, st, stage_sems.at[0, i]) for i, (w, st) in enumerate(zip(w_refs, stages))]
        for cp in stage_in:
            cp.start()
        first = []
        for i, (w, o) in enumerate(zip(w_refs, o_refs)):
            for k, (cx, cy) in enumerate(others):
                first.append(_remote(_half(w, mc), _half(o, mc, chip), send_sems.at[i, k],
                                     recv_sems.at[i, k], (cx, cy, mc)))
                first[-1].start()
        stage_out = []
        for i, (st, o) in enumerate(zip(stages, o_refs)):
            stage_in[i].wait()
            stage_out.append(pltpu.make_async_copy(st, o.at[chip], stage_sems.at[1, i]))
            stage_out[-1].start()
        passed = []
        for i, (w, o) in enumerate(zip(w_refs, o_refs)):
            for k, (cx, cy) in enumerate(others):
                landed = _half(o, mc, 2 * cx + cy)
                _remote(landed, landed, send_sems.at[i, k], recv_sems.at[i, k], (cx, cy, mc)).wait_recv()
                passed.append(_remote(landed, landed, send_sems.at[i, 3 + k], recv_sems.at[i, 3 + k], sibling))
                passed[-1].start()
        for i, (w, o) in enumerate(zip(w_refs, o_refs)):
            for k, (cx, cy) in enumerate(others):
                there = _half(o, 1 - mc, 2 * cx + cy)
                _remote(there, there, send_sems.at[i, 3 + k], recv_sems.at[i, 3 + k], sibling).wait_recv()
        for cp in first + passed:
            cp.wait_send()
        for cp in stage_out:
            cp.wait()

    out = pl.pallas_call(
        body, name="gather_weights", in_specs=[HBM_SPEC] * n, out_specs=[HBM_SPEC] * n + [VMEM_SPEC],
        out_shape=[jax.ShapeDtypeStruct((N_CHIPS,) + s.shape, s.dtype) for s in shards] + [jax.ShapeDtypeStruct((8, LANES), F32)],
        scratch_shapes=[pltpu.SemaphoreType.DMA((n, 6)), pltpu.SemaphoreType.DMA((n, 6)), pltpu.SemaphoreType.DMA((2, n))]
        + [pltpu.VMEM(s.shape, s.dtype) for s in shards],
        compiler_params=pltpu.CompilerParams(vmem_limit_bytes=VMEM_LIMIT),
    )(*shards)
    return out[:n], out[n]


SEM_SPEC = pl.BlockSpec(memory_space=pltpu.SEMAPHORE)
ANY_SPEC = pl.BlockSpec(memory_space=pl.ANY)
DATAFLOW = pltpu.SideEffectType.DATAFLOW_SIDE_EFFECTING


def _hbm(arr):
    return pltpu.with_memory_space_constraint(arr, pltpu.HBM)


def _gather_start(shards):
    n = len(shards)

    def body(*refs):
        w_refs, land_refs, send_sems, recv_sems, token = refs[:n], refs[n:2 * n], refs[2 * n], refs[2 * n + 1], refs[-1]
        mx, my, mc = _place()
        chip = 2 * mx + my
        for i, (w, land) in enumerate(zip(w_refs, land_refs)):
            for k, (cx, cy) in enumerate(_other_chips(mx, my)):
                _remote(_half(w, mc), _half(land, mc, chip), send_sems.at[3 * i + k],
                        recv_sems.at[3 * i + k], (cx, cy, mc)).start()
        token[...] = jnp.zeros_like(token)

    lands = [lax.empty((N_CHIPS,) + s.shape, s.dtype) for s in shards]
    out = pl.pallas_call(
        body, name="gather_start",
        out_shape=(pltpu.SemaphoreType.DMA((3 * n,)), pltpu.SemaphoreType.DMA((3 * n,)),
                   *[pltpu.HBM(s.shape, s.dtype) for s in shards], *[pltpu.HBM(l.shape, l.dtype) for l in lands],
                   jax.ShapeDtypeStruct((8, LANES), F32)),
        in_specs=[HBM_SPEC] * (2 * n), out_specs=(SEM_SPEC, SEM_SPEC, *[HBM_SPEC] * (2 * n), VMEM_SPEC),
        input_output_aliases={i: 2 + i for i in range(2 * n)},
        compiler_params=pltpu.CompilerParams(has_side_effects=DATAFLOW),
    )(*[_hbm(s) for s in shards], *[_hbm(l) for l in lands])
    return out[0], out[1], out[2:2 + n], out[2 + n:2 + 2 * n], out[-1]


def _gather_wait(send_sems, recv_sems, shards, lands, after):
    n = len(shards)

    def body(*refs):
        w_refs, land_refs, send_sems, recv_sems = refs[:n], refs[n:2 * n], refs[2 * n], refs[2 * n + 1]
        mx, my, mc = _place()
        for i, (w, land) in enumerate(zip(w_refs, land_refs)):
            for k, (cx, cy) in enumerate(_other_chips(mx, my)):
                cp = _remote(_half(w, mc), _half(land, mc, 2 * cx + cy), send_sems.at[3 * i + k],
                             recv_sems.at[3 * i + k], (cx, cy, mc))
                cp.wait_send()
                cp.wait_recv()

    out = pl.pallas_call(
        body, name="gather_wait",
        out_shape=(*[pltpu.HBM(s.shape, s.dtype) for s in shards], *[pltpu.HBM(l.shape, l.dtype) for l in lands]),
        in_specs=[HBM_SPEC] * (2 * n) + [SEM_SPEC, SEM_SPEC, ANY_SPEC], out_specs=tuple([HBM_SPEC] * (2 * n)),
        input_output_aliases={i: i for i in range(2 * n)},
        compiler_params=pltpu.CompilerParams(has_side_effects=DATAFLOW),
    )(*shards, *lands, send_sems, recv_sems, after)
    return out[n:]


def _gather_finish(shards, lands):
    n = len(shards)

    def body(*refs):
        w_refs, land_refs, o_refs = refs[:n], refs[n:2 * n], refs[2 * n:3 * n]
        send_sems, recv_sems, stage_sems = refs[3 * n:3 * n + 3]
        stages = refs[3 * n + 3:]
        mx, my, mc = _place()
        chip = 2 * mx + my
        others = _other_chips(mx, my)
        sibling = (mx, my, 1 - mc)
        stage_in = [pltpu.make_async_copy(w, st, stage_sems.at[0, i]) for i, (w, st) in enumerate(zip(w_refs, stages))]
        for cp in stage_in:
            cp.start()
        passed = []
        for i, (w, o) in enumerate(zip(w_refs, o_refs)):
            for k, (cx, cy) in enumerate(others):
                landed = _half(o, mc, 2 * cx + cy)
                passed.append(_remote(landed, landed, send_sems.at[i, k], recv_sems.at[i, k], sibling))
                passed[-1].start()
        stage_out = []
        for i, (st, o) in enumerate(zip(stages, o_refs)):
            stage_in[i].wait()
            stage_out.append(pltpu.make_async_copy(st, o.at[chip], stage_sems.at[1, i]))
            stage_out[-1].start()
        for i, (w, o) in enumerate(zip(w_refs, o_refs)):
            for k, (cx, cy) in enumerate(others):
                there = _half(o, 1 - mc, 2 * cx + cy)
                _remote(there, there, send_sems.at[i, k], recv_sems.at[i, k], sibling).wait_recv()
        for cp in passed:
            cp.wait_send()
        for cp in stage_out:
            cp.wait()

    return pl.pallas_call(
        body, name="gather_finish", in_specs=[HBM_SPEC] * (2 * n), out_specs=[HBM_SPEC] * n,
        out_shape=[jax.ShapeDtypeStruct(l.shape, l.dtype) for l in lands],
        input_output_aliases={n + i: i for i in range(n)},
        scratch_shapes=[pltpu.SemaphoreType.DMA((n, 3)), pltpu.SemaphoreType.DMA((n, 3)), pltpu.SemaphoreType.DMA((2, n))]
        + [pltpu.VMEM(s.shape, s.dtype) for s in shards],
        compiler_params=pltpu.CompilerParams(vmem_limit_bytes=VMEM_LIMIT),
    )(*shards, *lands)


def _scatter_start(ss, tag):
    n = len(ss)

    def body(*refs):
        s_refs, land_refs, send_sems, recv_sems, token = refs[:n], refs[n:2 * n], refs[2 * n], refs[2 * n + 1], refs[-1]
        mx, my, mc = _place()
        chip = 2 * mx + my
        for i, (s, land) in enumerate(zip(s_refs, land_refs)):
            for k, (cx, cy) in enumerate(_other_chips(mx, my)):
                _remote(s.at[2 * cx + cy], land.at[chip], send_sems.at[3 * i + k], recv_sems.at[3 * i + k],
                        (cx, cy, mc)).start()
        token[...] = jnp.zeros_like(token)

    lands = [lax.empty(s.shape, s.dtype) for s in ss]
    out = pl.pallas_call(
        body, name="scatter_start_" + tag,
        out_shape=(pltpu.SemaphoreType.DMA((3 * n,)), pltpu.SemaphoreType.DMA((3 * n,)),
                   *[pltpu.HBM(s.shape, s.dtype) for s in ss], *[pltpu.HBM(l.shape, l.dtype) for l in lands],
                   jax.ShapeDtypeStruct((8, LANES), F32)),
        in_specs=[HBM_SPEC] * (2 * n), out_specs=(SEM_SPEC, SEM_SPEC, *[HBM_SPEC] * (2 * n), VMEM_SPEC),
        input_output_aliases={i: 2 + i for i in range(2 * n)},
        compiler_params=pltpu.CompilerParams(has_side_effects=DATAFLOW),
    )(*[_hbm(s) for s in ss], *[_hbm(l) for l in lands])
    return out[0], out[1], out[2:2 + n], out[2 + n:2 + 2 * n], out[-1]


def _scatter_wait(send_sems, recv_sems, ss, lands, after, tag):
    n = len(ss)

    def body(*refs):
        s_refs, land_refs, send_sems, recv_sems = refs[:n], refs[n:2 * n], refs[2 * n], refs[2 * n + 1]
        mx, my, mc = _place()
        for i, (s, land) in enumerate(zip(s_refs, land_refs)):
            for k, (cx, cy) in enumerate(_other_chips(mx, my)):
                slot = land.at[2 * cx + cy]
                cp = _remote(s.at[2 * cx + cy], slot, send_sems.at[3 * i + k], recv_sems.at[3 * i + k], (cx, cy, mc))
                cp.wait_send()
                cp.wait_recv()

    out = pl.pallas_call(
        body, name="scatter_wait_" + tag,
        out_shape=(*[pltpu.HBM(s.shape, s.dtype) for s in ss], *[pltpu.HBM(l.shape, l.dtype) for l in lands]),
        in_specs=[HBM_SPEC] * (2 * n) + [SEM_SPEC, SEM_SPEC, ANY_SPEC], out_specs=tuple([HBM_SPEC] * (2 * n)),
        input_output_aliases={i: i for i in range(2 * n)},
        compiler_params=pltpu.CompilerParams(has_side_effects=DATAFLOW),
    )(*ss, *lands, send_sems, recv_sems, after)
    return out[:n], out[n:]


def _swap_halves(gs, after, name):
    n = len(gs)

    def body(*refs):
        g_refs, o_refs, send_sems, recv_sems = refs[:n], refs[n + 1:2 * n + 1], refs[2 * n + 1], refs[2 * n + 2]
        mx, my, mc = _place()
        copies = []
        for i, (g, o) in enumerate(zip(g_refs, o_refs)):
            src = g.at[(slice(None),) + _half_index(g.shape, 1 - mc)]
            copies.append(_remote(src, o, send_sems.at[i], recv_sems.at[i], (mx, my, 1 - mc)))
            copies[-1].start()
        for cp in copies:
            cp.wait()

    return pl.pallas_call(
        body, name=name, in_specs=[HBM_SPEC] * n + [ANY_SPEC], out_specs=[HBM_SPEC] * n,
        out_shape=[jax.ShapeDtypeStruct(_half_shape(g.shape), g.dtype) for g in gs],
        scratch_shapes=[pltpu.SemaphoreType.DMA((n,)), pltpu.SemaphoreType.DMA((n,))],
    )(*gs, after)


def _swap_start(gs, tag):
    n = len(gs)

    def body(*refs):
        g_refs, land_refs, send_sems, recv_sems, token = refs[:n], refs[n:2 * n], refs[2 * n], refs[2 * n + 1], refs[-1]
        mx, my, mc = _place()
        for i, (g, land) in enumerate(zip(g_refs, land_refs)):
            src = g.at[(slice(None),) + _half_index(g.shape, 1 - mc)]
            _remote(src, land, send_sems.at[i], recv_sems.at[i], (mx, my, 1 - mc)).start()
        token[...] = jnp.zeros_like(token)

    lands = [lax.empty(_half_shape(g.shape), g.dtype) for g in gs]
    out = pl.pallas_call(
        body, name="swap_start_" + tag,
        out_shape=(pltpu.SemaphoreType.DMA((n,)), pltpu.SemaphoreType.DMA((n,)), *[pltpu.HBM(g.shape, g.dtype) for g in gs],
                   *[pltpu.HBM(l.shape, l.dtype) for l in lands], jax.ShapeDtypeStruct((8, LANES), F32)),
        in_specs=[HBM_SPEC] * (2 * n), out_specs=(SEM_SPEC, SEM_SPEC, *[HBM_SPEC] * (2 * n), VMEM_SPEC),
        input_output_aliases={i: 2 + i for i in range(2 * n)},
        compiler_params=pltpu.CompilerParams(has_side_effects=DATAFLOW),
    )(*[_hbm(g) for g in gs], *[_hbm(l) for l in lands])
    return out[0], out[1], out[2:2 + n], out[2 + n:2 + 2 * n], out[-1]


def _swap_wait(send_sems, recv_sems, gs, lands, after, tag):
    n = len(gs)

    def body(*refs):
        g_refs, land_refs, send_sems, recv_sems = refs[:n], refs[n:2 * n], refs[2 * n], refs[2 * n + 1]
        mx, my, mc = _place()
        for i, (g, land) in enumerate(zip(g_refs, land_refs)):
            src = g.at[(slice(None),) + _half_index(g.shape, 1 - mc)]
            cp = _remote(src, land, send_sems.at[i], recv_sems.at[i], (mx, my, 1 - mc))
            cp.wait_send()
            cp.wait_recv()

    out = pl.pallas_call(
        body, name="swap_wait_" + tag,
        out_shape=(*[pltpu.HBM(g.shape, g.dtype) for g in gs], *[pltpu.HBM(l.shape, l.dtype) for l in lands]),
        in_specs=[HBM_SPEC] * (2 * n) + [SEM_SPEC, SEM_SPEC, ANY_SPEC], out_specs=tuple([HBM_SPEC] * (2 * n)),
        input_output_aliases={i: i for i in range(2 * n)},
        compiler_params=pltpu.CompilerParams(has_side_effects=DATAFLOW),
    )(*gs, *lands, send_sems, recv_sems, after)
    return out[:n], out[n:]


def _pair_sums(gs, gots, name):
    n = len(gs)

    def body(*refs):
        g_refs, got_refs, o_refs, load_sems, store_sems = refs[:n], refs[n:2 * n], refs[2 * n:3 * n], refs[3 * n], refs[3 * n + 1]
        mine, theirs, sums = refs[3 * n + 2:4 * n + 2], refs[4 * n + 2:5 * n + 2], refs[5 * n + 2:]
        mc = lax.axis_index("c")
        loads = []
        for i, (g, got) in enumerate(zip(g_refs, got_refs)):
            loads.append((pltpu.make_async_copy(g.at[(slice(None),) + _half_index(g.shape, mc)], mine[i], load_sems.at[i, 0]),
                          pltpu.make_async_copy(got, theirs[i], load_sems.at[i, 1])))
            for cp in loads[-1]:
                cp.start()
        stores = []
        for i in range(n):
            for cp in loads[i]:
                cp.wait()
            sums[i][...] = (mine[i][...].astype(F32) + theirs[i][...].astype(F32)).astype(BF16)
            stores.append(pltpu.make_async_copy(sums[i], o_refs[i], store_sems.at[i]))
            stores[-1].start()
        for cp in stores:
            cp.wait()

    halves = [jax.ShapeDtypeStruct(got.shape, BF16) for got in gots]
    return pl.pallas_call(
        body, name=name, in_specs=[HBM_SPEC] * (2 * n), out_specs=[HBM_SPEC] * n, out_shape=halves,
        scratch_shapes=[pltpu.SemaphoreType.DMA((n, 2)), pltpu.SemaphoreType.DMA((n,))]
        + [pltpu.VMEM(got.shape, g.dtype) for g, got in zip(gs, gots)] + [pltpu.VMEM(got.shape, got.dtype) for got in gots]
        + [pltpu.VMEM(got.shape, BF16) for got in gots],
        compiler_params=pltpu.CompilerParams(vmem_limit_bytes=VMEM_LIMIT),
    )(*gs, *gots)


def _reduce_join(owns, gots, name):
    n = len(owns)

    def body(*refs):
        own_refs, got_refs, mine_refs, theirs_refs = refs[:n], refs[n:2 * n], refs[2 * n:3 * n], refs[3 * n:4 * n]
        send_sems, recv_sems, load_sems, store_sems = refs[4 * n:4 * n + 4]
        parts, sums = refs[4 * n + 4:5 * n + 4], refs[5 * n + 4:]
        mx, my, mc = _place()
        chip = 2 * mx + my
        loads = []
        for i, (own, got, part) in enumerate(zip(own_refs, got_refs, parts)):
            loads.append([pltpu.make_async_copy((own if k == 0 else got).at[chip ^ k], part.at[k], load_sems.at[i, k])
                          for k in range(N_CHIPS)])
            for cp in loads[-1]:
                cp.start()
        out = []
        for i, (part, total, mine, theirs) in enumerate(zip(parts, sums, mine_refs, theirs_refs)):
            for cp in loads[i]:
                cp.wait()
            total[...] = ((part[0].astype(F32) + part[1].astype(F32)) + part[2].astype(F32)) + part[3].astype(F32)
            out.append(pltpu.make_async_copy(total, mine, store_sems.at[i]))
            out.append(_remote(total, theirs, send_sems.at[i], recv_sems.at[i], (mx, my, 1 - mc)))
            out[-2].start()
            out[-1].start()
        for cp in out:
            cp.wait()

    halves = [jax.ShapeDtypeStruct(o.shape[1:], F32) for o in owns]
    out = pl.pallas_call(
        body, name=name, in_specs=[HBM_SPEC] * (2 * n), out_specs=[HBM_SPEC] * (2 * n), out_shape=halves + halves,
        scratch_shapes=[pltpu.SemaphoreType.DMA((n,)), pltpu.SemaphoreType.DMA((n,)), pltpu.SemaphoreType.DMA((n, N_CHIPS)),
                        pltpu.SemaphoreType.DMA((n,))]
        + [pltpu.VMEM(o.shape, o.dtype) for o in owns] + [pltpu.VMEM(o.shape[1:], F32) for o in owns],
        compiler_params=pltpu.CompilerParams(vmem_limit_bytes=VMEM_LIMIT),
    )(*owns, *gots)
    return out[:n], out[n:]


def _adam_math(w, g, m, v):
    m2 = ADAM_B1 * m + (1.0 - ADAM_B1) * g
    v2 = ADAM_B2 * v + (1.0 - ADAM_B2) * (g * g)
    m_hat = m2 * (1.0 / (1.0 - ADAM_B1 ** ADAM_STEP))
    v_hat = v2 * (1.0 / (1.0 - ADAM_B2 ** ADAM_STEP))
    delta = -ADAM_LR * (m_hat / (jnp.sqrt(v_hat) + ADAM_EPS) + ADAM_WD * w)
    return delta, m2, v2


def _adam(w, g, m, v, name):
    def body(w_ref, g_ref, m_ref, v_ref, d_ref, m2_ref, v2_ref):
        d_ref[...], m2_ref[...], v2_ref[...] = _adam_math(w_ref[...], g_ref[...], m_ref[...], v_ref[...])

    return pl.pallas_call(body, name=name, out_shape=[jax.ShapeDtypeStruct(w.shape, F32)] * 3)(w, g, m, v)


def _adam_halves(w, m, v, mine, theirs, core, name):
    hr, hcols = _half_shape(w.shape)[1:]
    by_rows = _halves_by_rows(w.shape)

    def body(core_ref, w_ref, m_ref, v_ref, mine_ref, theirs_ref, g_ref, d_ref, m2_ref, v2_ref):
        g = jnp.where(pl.program_id(0) == core_ref[0], mine_ref[...], theirs_ref[...])
        g_ref[0] = g
        d_ref[0], m2_ref[0], v2_ref[0] = _adam_math(w_ref[0], g, m_ref[0], v_ref[0])

    half = pl.BlockSpec((1, hr, hcols), lambda hc, core_ref: (0, hc, 0) if by_rows else (0, 0, hc))
    whole = pl.BlockSpec((hr, hcols), lambda hc, core_ref: (0, 0))
    return pl.pallas_call(
        body, name=name,
        grid_spec=pltpu.PrefetchScalarGridSpec(
            num_scalar_prefetch=1, grid=(2,), in_specs=[half, half, half, whole, whole], out_specs=[half] * 4),
        out_shape=[jax.ShapeDtypeStruct(w.shape, F32)] * 4,
        compiler_params=_cparams(("arbitrary",)),
    )(core, w, m, v, mine, theirs)


ADA_COLS = N_MOD * D_MODEL // N_CHIPS


def _ada_fwd(c_all, w_ada, b_cols):
    def body(c_ref, w_ref, b_ref, o_ref):
        cv = c_ref[...]
        act = (cv * _sigmoid(cv)).astype(BF16)
        o_ref[...] = _dot(act, w_ref[...].astype(BF16)) + b_ref[...]

    return pl.pallas_call(
        body, name="ada_fwd", out_shape=jax.ShapeDtypeStruct((c_all.shape[0], ADA_COLS), F32),
        compiler_params=pltpu.CompilerParams(vmem_limit_bytes=VMEM_LIMIT),
    )(c_all, w_ada, b_cols)


def _ada_bwd(c_all, dmod_cols, w, m, v):
    nb = c_all.shape[0]
    tn = 384

    def body(c_ref, d_ref, w_ref, m_ref, v_ref, g_ref, dl_ref, m2_ref, v2_ref):
        cv = c_ref[...]
        act = (cv * _sigmoid(cv)).astype(BF16)
        g = _dot_tn(act, d_ref[...].astype(BF16))
        g_ref[...] = g
        dl_ref[...], m2_ref[...], v2_ref[...] = _adam_math(w_ref[...], g, m_ref[...], v_ref[...])

    blk = pl.BlockSpec((D_MODEL, tn), lambda j: (0, j))
    return pl.pallas_call(
        body, name="ada_bwd", grid=(ADA_COLS // tn,),
        in_specs=[pl.BlockSpec((nb, D_MODEL), lambda j: (0, 0)), pl.BlockSpec((nb, tn), lambda j: (0, j)), blk, blk, blk],
        out_specs=[blk] * 4, out_shape=[jax.ShapeDtypeStruct((D_MODEL, ADA_COLS), F32)] * 4,
        compiler_params=_cparams(("arbitrary",)),
    )(c_all, dmod_cols, w, m, v)


SMALL_NAMES = ("norm_ffn1", "norm_mix", "conv_w", "conv_b", "ssd_norm_w", "q_norm_w", "kv_norm_w", "mla_norm_w",
               "norm_ffn2", "norm_final", "dt_bias", "a_log", "d_skip")
SMALL_SIZES = (1024, 1024, CONV_WIDTH * D_CONV, D_CONV, 1024, Q_LORA, KV_LORA, 1024, 1024, 1024, 16, 16, 16)
SMALL_ROWS = 16
MOD_ROWS = 2 * N_MOD
SEND_ROWS = 40


def _pack_small(parts):
    flat = jnp.concatenate([parts[n].reshape(-1) for n in SMALL_NAMES])
    return jnp.pad(flat, (0, SMALL_ROWS * D_MODEL - flat.shape[0]))


def _unpack_small(flat):
    out, off = {}, 0
    for n, size in zip(SMALL_NAMES, SMALL_SIZES):
        out[n] = flat[off:off + size]
        off += size
    return out


def _small_sum(got):
    def body(g_ref, o_ref):
        bsum = jnp.zeros((N_MOD, D_MODEL), F32)
        ssum = jnp.zeros((SMALL_ROWS, D_MODEL), F32)
        for d in range(N_DEV):
            bsum = bsum + g_ref[d, 0:N_MOD, :] + g_ref[d, N_MOD:MOD_ROWS, :]
            ssum = ssum + g_ref[d, MOD_ROWS:MOD_ROWS + SMALL_ROWS, :]
        o_ref[...] = jnp.concatenate([bsum, ssum, jnp.zeros((32 - N_MOD - SMALL_ROWS, D_MODEL), F32)], axis=0)

    return pl.pallas_call(body, name="small_sum", out_shape=jax.ShapeDtypeStruct((32, D_MODEL), F32))(got)


BIG_NAMES = ("ffn1_w_gate", "ffn1_w_up", "ffn1_w_down", "w_in", "w_uq", "w_ukv", "w_out", "ffn2_w_gate", "ffn2_w_up",
             "ffn2_w_down")
_TO_KERNEL = {"w_in": _win_to_kernel, "w_uq": _wuq_to_kernel, "w_ukv": _wukv_to_kernel}
_FROM_KERNEL = {"w_in": _win_from_kernel, "w_uq": _wuq_from_kernel, "w_ukv": _wukv_from_kernel}


def _columns_joined(w4):
    n, r, c = w4.shape
    return w4.transpose(1, 0, 2).reshape(r, n * c)


def _columns_split(g):
    r, cols = g.shape
    return g.reshape(r, N_CHIPS, cols // N_CHIPS).transpose(1, 0, 2)


def kernel(x, c, positions, w_ada, b_ada, norm_ffn1, ffn1_w_gate, ffn1_w_up, ffn1_w_down, norm_mix, w_in, conv_w, conv_b, dt_bias, a_log, d_skip, ssd_norm_w, q_norm_w, w_uq, kv_norm_w, w_ukv, mla_norm_w, w_out, norm_ffn2, ffn2_w_gate, ffn2_w_up, ffn2_w_down, norm_final, loss_target, m_w_ada, m_b_ada, m_norm_ffn1, m_ffn1_w_gate, m_ffn1_w_up, m_ffn1_w_down, m_norm_mix, m_w_in, m_conv_w, m_conv_b, m_dt_bias, m_a_log, m_d_skip, m_ssd_norm_w, m_q_norm_w, m_w_uq, m_kv_norm_w, m_w_ukv, m_mla_norm_w, m_w_out, m_norm_ffn2, m_ffn2_w_gate, m_ffn2_w_up, m_ffn2_w_down, m_norm_final, v_w_ada, v_b_ada, v_norm_ffn1, v_ffn1_w_gate, v_ffn1_w_up, v_ffn1_w_down, v_norm_mix, v_w_in, v_conv_w, v_conv_b, v_dt_bias, v_a_log, v_d_skip, v_ssd_norm_w, v_q_norm_w, v_w_uq, v_kv_norm_w, v_w_ukv, v_mla_norm_w, v_w_out, v_norm_ffn2, v_ffn2_w_gate, v_ffn2_w_up, v_ffn2_w_down, v_norm_final):
    a = dict(locals())
    held_transposed = ("ffn1_w_gate", "ffn1_w_up", "ffn2_w_gate", "ffn2_w_up", "w_in")
    for n in held_transposed:
        for p in ("", "m_", "v_"):
            a[p + n] = a[p + n].transpose(0, 2, 1)
    B, S, D = x.shape
    mx, my, mc = _place()
    chip = 2 * mx + my
    dev = 2 * chip + mc
    core = mc.astype(jnp.int32).reshape(1)

    cw_rows = jnp.pad(conv_w[0], ((0, 0), (0, D - conv_w.shape[2])))
    got, _ = _all_gather_small(jnp.concatenate([c, cw_rows, jnp.zeros((8 - B - CONV_WIDTH, D), F32)], axis=0), "gather_c")
    c_all = got[:, :B, :].reshape(N_DEV * B, D)
    conv_full = got[::2, B:B + CONV_WIDTH, :conv_w.shape[2]].transpose(1, 0, 2).reshape(CONV_WIDTH, D_CONV)

    b_cols = lax.dynamic_slice(b_ada, (0, chip * ADA_COLS), (1, ADA_COLS))
    mod_all, mod_done = _all_gather_small(_ada_fwd(c_all, w_ada[0], b_cols), "gather_mod")
    mod = lax.dynamic_slice(mod_all, (0, B * dev, 0), (N_DEV, B, ADA_COLS))[::2].transpose(1, 0, 2).reshape(B, N_MOD * D)

    first = ("ffn1_w_gate", "ffn1_w_up", "ffn1_w_down")
    later = tuple(n for n in BIG_NAMES if n not in first)
    got_first, gathered = _gather_weights([(a[n][0] + mod_done[0, 0]).astype(BF16) for n in first])
    w = dict(zip(first, got_first))
    in_flight = _gather_start([(a[n][0] + gathered[0, 0]).astype(BF16) for n in later])

    def later_weights(after):
        send_sems, recv_sems, shards, lands, _ = in_flight
        lands = _gather_wait(send_sems, recv_sems, shards, lands, after)
        wl = dict(zip(later, _gather_finish([a[n][0].astype(BF16) for n in later], lands)))
        for n, to_kernel in _TO_KERNEL.items():
            wl[n] = to_kernel(wl[n].reshape(-1, D) if n in held_transposed else _columns_joined(wl[n]))
        wl["w_out"] = wl["w_out"].reshape(D_SSD + D_MLA, D)
        return wl

    small = {n: a[n].reshape(1, -1) for n in SMALL_NAMES if n not in ("conv_w", "norm_final")}
    small["conv_w"], small["norm_final"] = conv_full, norm_final

    def shards_of(names, gw):
        g4 = []
        for n in names:
            g = gw[n]
            if n in _FROM_KERNEL:
                g = _FROM_KERNEL[n](g) if n in held_transposed else _columns_split(_FROM_KERNEL[n](g))
            g4.append(g.reshape(N_CHIPS, a[n].shape[1], a[n].shape[2]))
        return g4

    def scatter_group(names, g4, swapped):
        pair = _pair_sums(g4, swapped, "pair_sums_" + names[0])
        return (names,) + tuple(_scatter_start(pair, names[0]))

    grads, deltas, new_m, new_v = {}, {}, {}, {}

    def finish_groups(some, after):
        names, owns, gots = [], [], []
        for group_names, send_sems, recv_sems, pair, lands, _ in some:
            pair, lands = _scatter_wait(send_sems, recv_sems, pair, lands, after, group_names[0])
            names += group_names
            owns += pair
            gots += lands
        mine, theirs = _reduce_join(owns, gots, "reduce_join_" + names[0])
        for n, own, other in zip(names, mine, theirs):
            grads[n], deltas[n], new_m[n], new_v[n] = _adam_halves(a[n], a["m_" + n], a["v_" + n], own, other, core, "adam_" + n)
        return deltas[names[-1]]

    swapping, groups = [], []

    def on_grads(names, gw):
        send_sems, recv_sems, g4, lands, token = _swap_start(shards_of(names, gw), names[0])
        swapping.append((names, send_sems, recv_sems, g4, lands))
        return token[0, 0]

    def sync(after):
        token = 0.0
        while swapping:
            names, send_sems, recv_sems, g4, lands = swapping.pop(0)
            g4, swapped = _swap_wait(send_sems, recv_sems, g4, lands, after, names[0])
            groups.append(scatter_group(names, g4, swapped))
            token = groups[-1][5][0, 0]
        return token

    loss_blk, grad_x, gw, dmod, gs = _local_step(x, positions, mod + in_flight[4][0, 0], w, later_weights, small, loss_target,
                                                 on_grads, sync)

    small_flat = _pack_small(gs).at[-1].set(loss_blk[0, 0])
    send = jnp.concatenate([dmod.reshape(MOD_ROWS, D), small_flat.reshape(SMALL_ROWS, D),
                            jnp.zeros((SEND_ROWS - MOD_ROWS - SMALL_ROWS, D), F32)], axis=0)
    got, _ = _all_gather_small(send, "gather_small")
    summed = _small_sum(got)
    sums = summed[N_MOD:N_MOD + SMALL_ROWS].reshape(-1)
    loss = sums[-1]
    gsmall = _unpack_small(sums)
    gsmall["conv_w"] = lax.dynamic_slice(gsmall["conv_w"].reshape(CONV_WIDTH, D_CONV), (0, chip * conv_w.shape[2]),
                                         (CONV_WIDTH, conv_w.shape[2]))
    gsmall["b_ada"] = summed[:N_MOD]
    names = ("b_ada",) + SMALL_NAMES
    rows = 208

    def pack(parts):
        flat = jnp.concatenate([parts[n].reshape(-1) for n in names])
        return jnp.pad(flat, (0, rows * LANES - flat.shape[0])).reshape(rows, LANES)

    packed = [pack({n: a[p + n] for n in names}) for p in ("", "m_", "v_")]
    g_p = pack(gsmall)
    outs = (g_p,) + tuple(_adam(packed[0], g_p, packed[1], packed[2], "adam_small"))
    for dst, flat in zip((grads, deltas, new_m, new_v), outs):
        flat, off = flat.reshape(-1), 0
        for n in names:
            dst[n] = flat[off:off + a[n].size].reshape(a[n].shape)
            off += a[n].size

    dmod_all = got[:, :MOD_ROWS, :].reshape(N_DEV * B, N_MOD * D)
    dmod_cols = lax.dynamic_slice(dmod_all, (0, chip * ADA_COLS), (N_DEV * B, ADA_COLS))
    ada = _ada_bwd(c_all, dmod_cols, w_ada[0], m_w_ada[0], v_w_ada[0])
    for dst, t in zip((grads, deltas, new_m, new_v), ada):
        dst["w_ada"] = t[None]

    g4 = shards_of(first, gw)
    last = scatter_group(first, g4, _swap_halves(g4, summed, "swap_" + first[0]))
    finish_groups([last], finish_groups(groups, last[5]))
    for dst in (grads, deltas, new_m, new_v):
        for n in held_transposed:
            dst[n] = dst[n].transpose(0, 2, 1)

    order = ("w_ada", "b_ada", "norm_ffn1", "ffn1_w_gate", "ffn1_w_up", "ffn1_w_down", "norm_mix", "w_in", "conv_w", "conv_b",
             "dt_bias", "a_log", "d_skip", "ssd_norm_w", "q_norm_w", "w_uq", "kv_norm_w", "w_ukv", "mla_norm_w", "w_out",
             "norm_ffn2", "ffn2_w_gate", "ffn2_w_up", "ffn2_w_down", "norm_final")
    return (loss, grad_x, *[grads[n] for n in order], *[deltas[n] for n in order], *[new_m[n] for n in order],
            *[new_v[n] for n in order])
```

```python
import functools
import math

import jax
import jax.numpy as jnp
import numpy as np
from jax import lax
from jax.experimental import pallas as pl
from jax.experimental.pallas import tpu as pltpu

F32 = jnp.float32
BF16 = jnp.bfloat16

D_MODEL = 1024
D_FF = 2816
D_SSD = 1024
D_MLA = 1024
SSD_HEADS = 16
SSD_HEAD_DIM = 64
SSD_GROUPS = 2
SSD_STATE = 128
CONV_WIDTH = 4
CHUNK = 128
MLA_HEADS = 8
QK_NOPE = 64
QK_ROPE = 32
QK_DIM = QK_NOPE + QK_ROPE
V_HEAD = 128
Q_LORA = 384
KV_LORA = 256
ROPE_THETA = 10000.0
N_MOD = 9
EPS = 1e-6
D_CONV = D_SSD + 2 * SSD_GROUPS * SSD_STATE
D_PROJ = 3328
HEAD_LANES = 128
ADAM_LR = 0.001
ADAM_B1 = 0.9
ADAM_B2 = 0.999
ADAM_EPS = 1e-08
ADAM_WD = 0.01
ADAM_STEP = 10

LANES = 128
VMEM_LIMIT = 56 * 1024 * 1024
TOKEN_TILE = 512
WIDE_TOKEN_TILE = 1024
ATTN_FWD_Q_TILE = 1024
ATTN_FWD_KV_TILE = 1024
ATTN_BWD_TILE = 1024
N_CHIPS = 4
N_DEV = 8

MESH = pl.DeviceIdType.MESH


def _dot(a, b):
    return jnp.dot(a, b, preferred_element_type=F32)


def _dot_nt(a, b):
    return lax.dot_general(a, b, (((1,), (1,)), ((), ())), preferred_element_type=F32)


def _dot_tn(a, b):
    return lax.dot_general(a, b, (((0,), (0,)), ((), ())), preferred_element_type=F32)


def _cparams(semantics):
    return pltpu.CompilerParams(dimension_semantics=semantics, vmem_limit_bytes=VMEM_LIMIT)


def _resident(shape):
    zeros = (0,) * len(shape)
    return pl.BlockSpec(shape, lambda *_: zeros, pipeline_mode=pl.Buffered(1))


def _sigmoid(x):
    return jax.nn.sigmoid(x)


def _rms_stats(x):
    r = lax.rsqrt(jnp.mean(x * x, axis=-1, keepdims=True) + EPS)
    return x * r, r


def _rms_bwd(dn, xh, r, w):
    dxh = dn * w
    dx = r * (dxh - xh * jnp.mean(dxh * xh, axis=-1, keepdims=True))
    return dx, dn * xh


def _colsum(v):
    return jnp.sum(v, axis=0, keepdims=True)


def _ffn_fwd(x, nw, sh, sc, g, wg, wu, wd, seq, name, head=None):
    T, D = x.shape
    fs = wg.shape[1]
    tm = min(TOKEN_TILE, seq)
    tps = seq // tm

    def body(x_ref, nw_ref, sh_ref, sc_ref, g_ref, wg_ref, wu_ref, wd_ref, *rest):
        if head is None:
            xo_ref, a_ref, u_ref, f_ref = rest
        else:
            nf_ref, t_ref, xo_ref, a_ref, u_ref, f_ref, loss_ref, dnf_ref = rest

            @pl.when(pl.program_id(0) == 0)
            def _():
                loss_ref[...] = jnp.zeros_like(loss_ref)
                dnf_ref[...] = jnp.zeros_like(dnf_ref)

        xv = x_ref[...]
        xh, _ = _rms_stats(xv)
        h = (xh * nw_ref[...]) * (1.0 + sc_ref[0]) + sh_ref[0]
        hb = h.astype(BF16)
        f = jnp.zeros((tm, D), F32)
        for j in range(N_CHIPS):
            a = _dot_nt(hb, wg_ref[j])
            u = _dot_nt(hb, wu_ref[j])
            a_ref[j] = a.astype(BF16)
            u_ref[j] = u.astype(BF16)
            f = f + _dot((a * _sigmoid(a) * u).astype(BF16), wd_ref[j])
        f_ref[...] = f.astype(BF16)
        xo = xv + 0.5 * g_ref[0] * f
        if head is None:
            xo_ref[...] = xo
        else:
            xh, r = _rms_stats(xo)
            nfv = nf_ref[...]
            err = xh * nfv - t_ref[...]
            loss_ref[...] += (0.5 / D) * jnp.sum(err * err)
            dxo, dw_rows = _rms_bwd(err * (1.0 / D), xh, r, nfv)
            xo_ref[...] = dxo
            dnf_ref[...] += _colsum(dw_rows)

    rows = lambda n: pl.BlockSpec((tm, n), lambda i: (i, 0))
    act = pl.BlockSpec((N_CHIPS, tm, fs), lambda i: (0, i, 0))
    perb = pl.BlockSpec((1, 1, D), lambda i: (i // tps, 0, 0))
    sd = jax.ShapeDtypeStruct
    in_specs = [rows(D), _resident((1, D)), perb, perb, perb, _resident((N_CHIPS, fs, D)), _resident((N_CHIPS, fs, D)),
                _resident((N_CHIPS, fs, D))]
    out_specs = [rows(D), act, act, rows(D)]
    out_shape = [sd((T, D), F32), sd((N_CHIPS, T, fs), BF16), sd((N_CHIPS, T, fs), BF16), sd((T, D), BF16)]
    if head is not None:
        in_specs += [_resident((1, D)), rows(D)]
        out_specs += [pl.BlockSpec((8, LANES), lambda i: (0, 0)), pl.BlockSpec((1, D), lambda i: (0, 0))]
        out_shape += [sd((8, LANES), F32), sd((1, D), F32)]
    return pl.pallas_call(
        body, grid=(T // tm,), name=name, in_specs=in_specs, out_specs=out_specs, out_shape=out_shape,
        compiler_params=_cparams(("arbitrary",)),
    )(x, nw, sh, sc, g, wg, wu, wd, *(head or ()))


def _ffn_bwd(dxo, x, nw, sh, sc, g, a, u, f, wg, wu, wd, seq, name):
    T, D = x.shape
    fs = wg.shape[1]
    B = T // seq
    tm = min(TOKEN_TILE // 2, seq)
    tps = seq // tm

    def body(dxo_ref, x_ref, nw_ref, sh_ref, sc_ref, g_ref, a_ref, u_ref, f_ref, wg_ref, wu_ref, wd_ref,
             dx_ref, h_ref, s_ref, df_ref, da_ref, du_ref, dsh_ref, dsc_ref, dg_ref, dnw_ref):
        i = pl.program_id(0)

        @pl.when(i % tps == 0)
        def _():
            dsh_ref[...] = jnp.zeros_like(dsh_ref)
            dsc_ref[...] = jnp.zeros_like(dsc_ref)
            dg_ref[...] = jnp.zeros_like(dg_ref)

        @pl.when(i == 0)
        def _():
            dnw_ref[...] = jnp.zeros_like(dnw_ref)

        dxo_v = dxo_ref[...]
        dfb = (0.5 * g_ref[0] * dxo_v).astype(BF16)
        dg_ref[0] += _colsum(0.5 * dxo_v * f_ref[...].astype(F32))
        dh = jnp.zeros((tm, D), F32)
        for j in range(N_CHIPS):
            ds = _dot_nt(dfb, wd_ref[j])
            av = a_ref[j].astype(F32)
            uv = u_ref[j].astype(F32)
            sig = _sigmoid(av)
            sil = av * sig
            dab = (ds * uv * (sig * (1.0 + av * (1.0 - sig)))).astype(BF16)
            dub = (ds * sil).astype(BF16)
            dh = dh + _dot(dab, wg_ref[j]) + _dot(dub, wu_ref[j])
            s_ref[j] = (sil * uv).astype(BF16)
            da_ref[j] = dab
            du_ref[j] = dub
        xv = x_ref[...]
        xh, r = _rms_stats(xv)
        nwv = nw_ref[...]
        n = xh * nwv
        scale1 = 1.0 + sc_ref[0]
        dsc_ref[0] += _colsum(dh * n)
        dsh_ref[0] += _colsum(dh)
        dx, dw_rows = _rms_bwd(dh * scale1, xh, r, nwv)
        dnw_ref[...] += _colsum(dw_rows)
        dx_ref[...] = dxo_v + dx
        h_ref[...] = (n * scale1 + sh_ref[0]).astype(BF16)
        df_ref[...] = dfb

    rows = lambda n: pl.BlockSpec((tm, n), lambda i: (i, 0))
    act = pl.BlockSpec((N_CHIPS, tm, fs), lambda i: (0, i, 0))
    perb = pl.BlockSpec((1, 1, D), lambda i: (i // tps, 0, 0))
    sd = jax.ShapeDtypeStruct
    return pl.pallas_call(
        body, grid=(T // tm,), name=name,
        in_specs=[rows(D), rows(D), _resident((1, D)), perb, perb, perb, act, act, rows(D),
                  _resident((N_CHIPS, fs, D)), _resident((N_CHIPS, fs, D)), _resident((N_CHIPS, fs, D))],
        out_specs=[rows(D), rows(D), act, rows(D), act, act, perb, perb, perb, pl.BlockSpec((1, D), lambda i: (0, 0))],
        out_shape=[sd((T, D), F32), sd((T, D), BF16), sd((N_CHIPS, T, fs), BF16), sd((T, D), BF16),
                   sd((N_CHIPS, T, fs), BF16), sd((N_CHIPS, T, fs), BF16), sd((B, 1, D), F32), sd((B, 1, D), F32),
                   sd((B, 1, D), F32), sd((1, D), F32)],
        compiler_params=_cparams(("arbitrary",)),
    )(dxo, x, nw, sh, sc, g, a, u, f, wg, wu, wd)


def _ffn_wgrad(h, s, df, da, du, after, name):
    T, D = h.shape
    fs = s.shape[2]
    tt = min(WIDE_TOKEN_TILE, T)
    nt = T // tt

    def body(h_ref, s_ref, df_ref, da_ref, du_ref, after_ref, dgate_ref, dup_ref, ddown_ref, gate_acc, up_acc, down_acc):
        @pl.when(pl.program_id(1) == 0)
        def _():
            gate_acc[...] = jnp.zeros_like(gate_acc)
            up_acc[...] = jnp.zeros_like(up_acc)
            down_acc[...] = jnp.zeros_like(down_acc)

        hv = h_ref[...]
        gate_acc[...] += _dot_tn(da_ref[0], hv)
        up_acc[...] += _dot_tn(du_ref[0], hv)
        down_acc[...] += _dot_tn(s_ref[0], df_ref[...])

        @pl.when(pl.program_id(1) == nt - 1)
        def _():
            dgate_ref[0] = gate_acc[...].astype(BF16)
            dup_ref[0] = up_acc[...].astype(BF16)
            ddown_ref[0] = down_acc[...].astype(BF16)

    rows = pl.BlockSpec((tt, D), lambda j, t: (t, 0))
    act = pl.BlockSpec((1, tt, fs), lambda j, t: (j, t, 0))
    shard = pl.BlockSpec((1, fs, D), lambda j, t: (j, 0, 0))
    return pl.pallas_call(
        body, grid=(N_CHIPS, nt), name=name,
        in_specs=[rows, act, rows, act, act, pl.BlockSpec(memory_space=pl.ANY)],
        out_specs=[shard] * 3, out_shape=[jax.ShapeDtypeStruct((N_CHIPS, fs, D), BF16)] * 3,
        scratch_shapes=[pltpu.VMEM((fs, D), F32)] * 3,
        compiler_params=_cparams(("arbitrary", "arbitrary")),
    )(h, s, df, da, du, after)


def _mm_tn(xa, ya, tn, name):
    T, K = xa.shape
    N = ya.shape[1]
    tt = min(WIDE_TOKEN_TILE, T)
    nt = T // tt

    def body(x_ref, y_ref, o_ref, acc_ref):
        @pl.when(pl.program_id(1) == 0)
        def _():
            acc_ref[...] = jnp.zeros_like(acc_ref)

        acc_ref[...] += _dot_tn(x_ref[...], y_ref[...])

        @pl.when(pl.program_id(1) == nt - 1)
        def _():
            o_ref[...] = acc_ref[...].astype(BF16)

    return pl.pallas_call(
        body, grid=(N // tn, nt), name=name,
        in_specs=[pl.BlockSpec((tt, K), lambda j, t: (t, 0)), pl.BlockSpec((tt, tn), lambda j, t: (t, j))],
        out_specs=pl.BlockSpec((K, tn), lambda j, t: (0, j)),
        out_shape=jax.ShapeDtypeStruct((K, N), BF16),
        scratch_shapes=[pltpu.VMEM((K, tn), F32)],
        compiler_params=_cparams(("arbitrary", "arbitrary")),
    )(xa, ya)


_PROJ_SPLITS = (0, 1024, 2560, 2944, 3200, 3328)


def _inproj_fwd(x, nw, sh, sc, win, seq):
    T, D = x.shape
    tm = min(WIDE_TOKEN_TILE, seq)
    tps = seq // tm
    widths = [b - a for a, b in zip(_PROJ_SPLITS[:-1], _PROJ_SPLITS[1:])]
    dtypes = [BF16, BF16, F32, F32, F32]

    def body(x_ref, nw_ref, sh_ref, sc_ref, w_ref, *outs):
        xh, _ = _rms_stats(x_ref[...])
        h = (xh * nw_ref[...]) * (1.0 + sc_ref[0]) + sh_ref[0]
        proj = _dot_nt(h.astype(BF16), w_ref[...])
        for o, lo, hi in zip(outs, _PROJ_SPLITS[:-1], _PROJ_SPLITS[1:]):
            o[...] = proj[:, lo:hi].astype(o.dtype)

    rows = lambda n: pl.BlockSpec((tm, n), lambda i: (i, 0))
    perb = pl.BlockSpec((1, 1, D), lambda i: (i // tps, 0, 0))
    return pl.pallas_call(
        body, grid=(T // tm,), name="inproj_fwd",
        in_specs=[rows(D), _resident((1, D)), perb, perb, _resident((D_PROJ, D))],
        out_specs=[rows(w) for w in widths],
        out_shape=[jax.ShapeDtypeStruct((T, w), dt) for w, dt in zip(widths, dtypes)],
        compiler_params=_cparams(("arbitrary",)),
    )(x, nw, sh, sc, win)


def _inproj_bwd(dx2, x, nw, sh, sc, win, dz, dxbc, dcq, dckv, ddtk_a, ddtk_b, seq):
    T, D = x.shape
    B = T // seq
    tm = min(TOKEN_TILE, seq)
    tps = seq // tm

    def body(dx2_ref, x_ref, nw_ref, sh_ref, sc_ref, w_ref, dz_ref, dxbc_ref, dcq_ref, dckv_ref, da_ref, db_ref,
             dx_ref, h_ref, dp_ref, dsh_ref, dsc_ref, dnw_ref):
        i = pl.program_id(0)

        @pl.when(i % tps == 0)
        def _():
            dsh_ref[...] = jnp.zeros_like(dsh_ref)
            dsc_ref[...] = jnp.zeros_like(dsc_ref)

        @pl.when(i == 0)
        def _():
            dnw_ref[...] = jnp.zeros_like(dnw_ref)

        dproj = jnp.concatenate(
            [dz_ref[...], dxbc_ref[...], dcq_ref[...].astype(BF16), dckv_ref[...].astype(BF16),
             (da_ref[...] + db_ref[...]).astype(BF16)], axis=1)
        dp_ref[...] = dproj
        dh = _dot(dproj, w_ref[...])
        xh, r = _rms_stats(x_ref[...])
        nwv = nw_ref[...]
        n = xh * nwv
        scale1 = 1.0 + sc_ref[0]
        dsc_ref[0] += _colsum(dh * n)
        dsh_ref[0] += _colsum(dh)
        dx, dw_rows = _rms_bwd(dh * scale1, xh, r, nwv)
        dnw_ref[...] += _colsum(dw_rows)
        dx_ref[...] = dx2_ref[...] + dx
        h_ref[...] = (n * scale1 + sh_ref[0]).astype(BF16)

    rows = lambda n: pl.BlockSpec((tm, n), lambda i: (i, 0))
    perb = pl.BlockSpec((1, 1, D), lambda i: (i // tps, 0, 0))
    sd = jax.ShapeDtypeStruct
    return pl.pallas_call(
        body, grid=(T // tm,), name="inproj_bwd",
        in_specs=[rows(D), rows(D), _resident((1, D)), perb, perb, _resident((D_PROJ, D)),
                  rows(1024), rows(D_CONV), rows(Q_LORA), rows(KV_LORA), rows(LANES), rows(LANES)],
        out_specs=[rows(D), rows(D), rows(D_PROJ), perb, perb, pl.BlockSpec((1, D), lambda i: (0, 0))],
        out_shape=[sd((T, D), F32), sd((T, D), BF16), sd((T, D_PROJ), BF16), sd((B, 1, D), F32), sd((B, 1, D), F32),
                   sd((1, D), F32)],
        compiler_params=_cparams(("arbitrary",)),
    )(dx2, x, nw, sh, sc, win, dz, dxbc, dcq, dckv, ddtk_a, ddtk_b)


SUBLANES = 8


def _shift_down(v, k):
    r = pltpu.roll(v, k, 0)
    row = lax.broadcasted_iota(jnp.int32, (SUBLANES, v.shape[1]), 0)
    return jnp.concatenate([jnp.where(row < k, 0.0, r[:SUBLANES]), r[SUBLANES:]], axis=0)


def _shift_up(v, k):
    n = v.shape[0]
    r = pltpu.roll(v, n - k, 0)
    row = lax.broadcasted_iota(jnp.int32, (SUBLANES, v.shape[1]), 0)
    return jnp.concatenate([r[:n - SUBLANES], jnp.where(row >= SUBLANES - k, 0.0, r[n - SUBLANES:])], axis=0)


def _conv_pre(xv, w_ref, b_ref):
    pre = b_ref[...] + w_ref[CONV_WIDTH - 1:CONV_WIDTH, :] * xv
    for k in range(1, CONV_WIDTH):
        pre = pre + w_ref[CONV_WIDTH - 1 - k:CONV_WIDTH - k, :] * _shift_down(xv, k)
    return pre


def _conv_fwd(xraw, cw, cb):
    B, S, C = xraw.shape

    def body(x_ref, w_ref, b_ref, o_ref):
        pre = _conv_pre(x_ref[0].astype(F32), w_ref, b_ref)
        o_ref[0] = (pre * _sigmoid(pre)).astype(BF16)

    blk = pl.BlockSpec((1, S, LANES), lambda b, j: (b, 0, j))
    return pl.pallas_call(
        body, grid=(B, C // LANES), name="conv_fwd",
        in_specs=[blk, pl.BlockSpec((CONV_WIDTH, LANES), lambda b, j: (0, j)), pl.BlockSpec((1, LANES), lambda b, j: (0, j))],
        out_specs=blk, out_shape=jax.ShapeDtypeStruct((B, S, C), BF16),
        compiler_params=_cparams(("arbitrary", "arbitrary")),
    )(xraw, cw, cb)


def _conv_bwd(dout, xraw, cw, cb):
    B, S, C = xraw.shape

    def body(d_ref, x_ref, w_ref, b_ref, dx_ref, dw_ref, db_ref):
        @pl.when(pl.program_id(1) == 0)
        def _():
            dw_ref[...] = jnp.zeros_like(dw_ref)
            db_ref[...] = jnp.zeros_like(db_ref)

        xv = x_ref[0].astype(F32)
        pre = _conv_pre(xv, w_ref, b_ref)
        sig = _sigmoid(pre)
        dpre = d_ref[0].astype(F32) * (sig * (1.0 + pre * (1.0 - sig)))
        dx = w_ref[CONV_WIDTH - 1:CONV_WIDTH, :] * dpre
        for k in range(1, CONV_WIDTH):
            dx = dx + w_ref[CONV_WIDTH - 1 - k:CONV_WIDTH - k, :] * _shift_up(dpre, k)
        dx_ref[0] = dx.astype(BF16)
        db_ref[...] += _colsum(dpre)
        dws = [_colsum(dpre * (xv if k == 0 else _shift_down(xv, k))) for k in range(CONV_WIDTH - 1, -1, -1)]
        dw_ref[...] += jnp.concatenate(dws, axis=0)

    blk = pl.BlockSpec((1, S, LANES), lambda j, b: (b, 0, j))
    wspec = pl.BlockSpec((CONV_WIDTH, LANES), lambda j, b: (0, j))
    bspec = pl.BlockSpec((1, LANES), lambda j, b: (0, j))
    return pl.pallas_call(
        body, grid=(C // LANES, B), name="conv_bwd",
        in_specs=[blk, blk, wspec, bspec], out_specs=[blk, wspec, bspec],
        out_shape=[jax.ShapeDtypeStruct((B, S, C), BF16), jax.ShapeDtypeStruct((CONV_WIDTH, C), F32),
                   jax.ShapeDtypeStruct((1, C), F32)],
        compiler_params=_cparams(("arbitrary", "arbitrary")),
    )(dout, xraw, cw, cb)


def _softplus(x):
    return jnp.maximum(x, 0.0) + jnp.log(1.0 + jnp.exp(-jnp.abs(x)))


def _ssd_common(xbc_ref, dtk_ref, dtb_ref, alog_ref, e_ref):
    L = CHUNK
    xbc = xbc_ref[0]
    xs = xbc[:, :D_SSD].astype(F32)
    bm = xbc[:, D_SSD:D_SSD + 256]
    cm = xbc[:, D_SSD + 256:D_SSD + 512]
    head = lax.broadcasted_iota(jnp.int32, (1, LANES), 1) < SSD_HEADS
    a128 = jnp.where(head, -jnp.exp(alog_ref[...]), 0.0)
    pre = dtk_ref[0] + dtb_ref[...]
    dt = _softplus(pre)
    dA = dt * a128
    row = lax.broadcasted_iota(jnp.int32, (L, L), 0)
    col = lax.broadcasted_iota(jnp.int32, (L, L), 1)
    causal = col <= row
    tri = causal.astype(F32)
    triT = (row <= col).astype(F32)
    tri = causal.astype(BF16)
    triT = (row <= col).astype(BF16)
    dA3 = _split3(dA)
    acum = _sum3(lambda part: _dot(tri, part), dA3)
    acumT = _sum3(lambda part: _dot_tn(part, triT), dA3)
    E = e_ref[...]
    acum_f = _spread(acum, E)
    dt_f = _spread(dt, E)
    e_f = jnp.exp(acum_f)
    w_f = jnp.exp(acum_f[L - 1:L, :] - acum_f)
    xt = xs * dt_f
    return dict(xs=xs, bm=bm, cm=cm, a128=a128, pre=pre, dt=dt, causal=causal, tri=tri, triT=triT, acum=acum,
                acumT=acumT, E=E, dt_f=dt_f, e_f=e_f, w_f=w_f, xt=xt, head=head)


def _split3(x):
    p1 = x.astype(BF16)
    r1 = x - p1.astype(F32)
    p2 = r1.astype(BF16)
    return p1, p2, (r1 - p2.astype(F32)).astype(BF16)


def _sum3(mm, parts):
    return (mm(parts[0]) + mm(parts[1])) + mm(parts[2])


def _spread(v, e):
    return _sum3(lambda part: _dot(part, e), _split3(v))


def _gather_heads(v, e):
    return _sum3(lambda part: _dot_nt(part, e), _split3(v))


def _head_mask(k):
    lane = lax.broadcasted_iota(jnp.int32, (CHUNK, LANES), 1)
    return (lane >= SSD_HEAD_DIM) if k == 1 else (lane < SSD_HEAD_DIM)


def _pair_decay(alast, h0):
    row = lax.broadcasted_iota(jnp.int32, (2 * SSD_HEAD_DIM, SSD_STATE), 0)
    return jnp.exp(jnp.where(row < SSD_HEAD_DIM, alast[:, h0:h0 + 1], alast[:, h0 + 1:h0 + 2]))


def _decay_matrix(q, h):
    seg = q["acum"][:, h:h + 1] - q["acumT"][h:h + 1, :]
    return jnp.exp(jnp.where(q["causal"], seg, -1e30))


def _gated_norm(y, zz, nw):
    sig = _sigmoid(zz)
    sil = zz * sig
    yg = y * sil
    half = D_SSD // SSD_GROUPS
    parts = []
    for g in range(SSD_GROUPS):
        xh, r = _rms_stats(yg[:, g * half:(g + 1) * half])
        parts.append((xh, r))
    return sig, sil, parts


def _ssd_fwd(xbc, dtk, z, dtb, alog, dsk, nw, expand):
    B, S, _ = xbc.shape
    L = CHUNK
    nc = S // L

    def body(xbc_ref, dtk_ref, z_ref, dtb_ref, alog_ref, dsk_ref, nw_ref, e_ref, y_ref, ys_ref, prev_ref, st_ref):
        @pl.when(pl.program_id(0) == 0)
        def _():
            st_ref[...] = jnp.zeros_like(st_ref)

        for b in range(B):
            one = lambda ref: ref.at[pl.ds(b, 1)]
            sequence_step(one(xbc_ref), one(dtk_ref), one(z_ref), dtb_ref, alog_ref, dsk_ref, nw_ref, e_ref, one(y_ref),
                          one(ys_ref), one(prev_ref), st_ref.at[b])

    def sequence_step(xbc_ref, dtk_ref, z_ref, dtb_ref, alog_ref, dsk_ref, nw_ref, e_ref, y_ref, ys_ref, prev_ref, st_ref):
        q = _ssd_common(xbc_ref, dtk_ref, dtb_ref, alog_ref, e_ref)
        xtb = q["xt"].astype(BF16)
        xwb = (q["xt"] * q["w_f"]).astype(BF16)
        alast = q["acum"][L - 1:L, :]
        ys = []
        for g in range(SSD_GROUPS):
            bg = q["bm"][:, g * 128:(g + 1) * 128]
            cg = q["cm"][:, g * 128:(g + 1) * 128]
            G = _dot_nt(cg, bg)
            for pr in range(SSD_HEADS // SSD_GROUPS // 2):
                h0 = g * 8 + 2 * pr
                lo = h0 * SSD_HEAD_DIM
                xt_p = xtb[:, lo:lo + 128]
                ydiag = jnp.zeros((L, LANES), F32)
                for k in range(2):
                    M = (G * _decay_matrix(q, h0 + k)).astype(BF16)
                    ydiag = ydiag + _dot(M, jnp.where(_head_mask(k), xt_p, jnp.zeros_like(xt_p)))
                hp = st_ref[lo:lo + 128, :]
                prev_ref[0, 0, lo:lo + 128, :] = hp.astype(BF16)
                zoff = _dot_nt(cg, hp.astype(BF16))
                ys.append(ydiag + zoff * q["e_f"][:, lo:lo + 128])
                st_ref[lo:lo + 128, :] = _pair_decay(alast, h0) * hp + _dot_tn(xwb[:, lo:lo + 128], bg)
        y = jnp.concatenate(ys, axis=1) + dsk_ref[...] * q["xs"]
        y_ref[0] = y.astype(BF16)
        _, _, parts = _gated_norm(y, z_ref[0].astype(F32), nw_ref[...])
        half = D_SSD // SSD_GROUPS
        ys_ref[0] = jnp.concatenate(
            [xh * nw_ref[:, g * half:(g + 1) * half] for g, (xh, _) in enumerate(parts)], axis=1).astype(BF16)

    chunk = lambda n: pl.BlockSpec((B, L, n), lambda c: (0, c, 0))
    vec = pl.BlockSpec((1, LANES), lambda c: (0, 0))
    return pl.pallas_call(
        body, grid=(nc,), name="ssd_fwd",
        in_specs=[chunk(D_CONV), chunk(LANES), chunk(D_SSD), vec, vec, pl.BlockSpec((1, D_SSD), lambda c: (0, 0)),
                  pl.BlockSpec((1, D_SSD), lambda c: (0, 0)), pl.BlockSpec((LANES, D_SSD), lambda c: (0, 0))],
        out_specs=[chunk(D_SSD), chunk(D_SSD), pl.BlockSpec((B, 1, D_SSD, SSD_STATE), lambda c: (0, c, 0, 0))],
        out_shape=[jax.ShapeDtypeStruct((B, S, D_SSD), BF16), jax.ShapeDtypeStruct((B, S, D_SSD), BF16),
                   jax.ShapeDtypeStruct((B, nc, D_SSD, SSD_STATE), BF16)],
        scratch_shapes=[pltpu.VMEM((B, D_SSD, SSD_STATE), F32)],
        compiler_params=_cparams(("arbitrary",)),
    )(xbc, dtk, z, dtb, alog, dsk, nw, expand)


def _ssd_bwd(xbc, dtk, z, y, prev, dys, dtb, alog, dsk, nw, expand):
    B, S, _ = xbc.shape
    L = CHUNK
    nc = S // L
    half = D_SSD // SSD_GROUPS

    def body(xbc_ref, dtk_ref, z_ref, y_ref, prev_ref, dys_ref, dtb_ref, alog_ref, dsk_ref, nw_ref, e_ref,
             dxbc_ref, ddtk_ref, dz_ref, dnw_ref, dvec_ref, dh_ref, dskc_ref):
        @pl.when(pl.program_id(0) == 0)
        def _():
            dnw_ref[...] = jnp.zeros_like(dnw_ref)
            dvec_ref[...] = jnp.zeros_like(dvec_ref)
            dskc_ref[...] = jnp.zeros_like(dskc_ref)
            dh_ref[...] = jnp.zeros_like(dh_ref)

        for b in range(B):
            one = lambda ref: ref.at[pl.ds(b, 1)]
            sequence_step(one(xbc_ref), one(dtk_ref), one(z_ref), one(y_ref), one(prev_ref), one(dys_ref), dtb_ref, alog_ref,
                          dsk_ref, nw_ref, e_ref, one(dxbc_ref), one(ddtk_ref), one(dz_ref), dnw_ref, dvec_ref, dh_ref.at[b],
                          dskc_ref)

        @pl.when(pl.program_id(0) == nc - 1)
        def _():
            dvec_ref[2:3, :] = _gather_heads(jnp.broadcast_to(dskc_ref[...], (8, D_SSD)), e_ref[...])[0:1, :]

    def sequence_step(xbc_ref, dtk_ref, z_ref, y_ref, prev_ref, dys_ref, dtb_ref, alog_ref, dsk_ref, nw_ref, e_ref,
                      dxbc_ref, ddtk_ref, dz_ref, dnw_ref, dvec_ref, dh_ref, dskc_ref):
        q = _ssd_common(xbc_ref, dtk_ref, dtb_ref, alog_ref, e_ref)
        E = q["E"]
        xs = q["xs"]
        yv = y_ref[0].astype(F32)
        zz = z_ref[0].astype(F32)
        sig, sil, parts = _gated_norm(yv, zz, nw_ref[...])
        dn = dys_ref[0].astype(F32)
        dyg, dnw_rows = [], []
        for g, (xh, r) in enumerate(parts):
            dpart, dw_rows = _rms_bwd(dn[:, g * half:(g + 1) * half], xh, r, nw_ref[:, g * half:(g + 1) * half])
            dyg.append(dpart)
            dnw_rows.append(dw_rows)
        dyg = jnp.concatenate(dyg, axis=1)
        dnw_ref[...] += _colsum(jnp.concatenate(dnw_rows, axis=1))
        dY = dyg * sil
        dz_ref[0] = (dyg * yv * (sig * (1.0 + zz * (1.0 - sig)))).astype(BF16)
        dsk_f = dsk_ref[...]
        dskc_ref[...] += _colsum(dY * xs)
        dYb = dY.astype(BF16)
        xtb = q["xt"].astype(BF16)
        xwb = (q["xt"] * q["w_f"]).astype(BF16)
        acum = q["acum"]
        alast = acum[L - 1:L, :]
        lane_id = lax.broadcasted_iota(jnp.int32, (L, LANES), 1)
        sub_id = lax.broadcasted_iota(jnp.int32, (LANES, L), 0)
        lane_row = lax.broadcasted_iota(jnp.int32, (1, LANES), 1)
        da_rows = jnp.zeros((L, LANES), F32)
        daT = jnp.zeros((LANES, L), F32)
        dxt, prod_off, prod_st, dbs, dcs = [], [], [], [], []
        hsum_row = jnp.zeros((1, LANES), F32)
        for g in range(SSD_GROUPS):
            bg = q["bm"][:, g * 128:(g + 1) * 128]
            cg = q["cm"][:, g * 128:(g + 1) * 128]
            G = _dot_nt(cg, bg)
            dG = jnp.zeros((L, L), F32)
            dcg = jnp.zeros((L, SSD_STATE), F32)
            dbg = jnp.zeros((L, SSD_STATE), F32)
            for pr in range(SSD_HEADS // SSD_GROUPS // 2):
                h0 = g * 8 + 2 * pr
                lo = h0 * SSD_HEAD_DIM
                cols = slice(lo, lo + 128)
                dY_p = dYb[:, cols]
                xt_p = xtb[:, cols]
                dxt_p = jnp.zeros((L, LANES), F32)
                for k in range(2):
                    h = h0 + k
                    Lm = _decay_matrix(q, h)
                    Mf = G * Lm
                    dYk = jnp.where(_head_mask(k), dY_p, jnp.zeros_like(dY_p))
                    dM = _dot_nt(dYk, xt_p)
                    dxt_p = dxt_p + _dot_tn(Mf.astype(BF16), dYk)
                    dG = dG + dM * Lm
                    Q = dM * Mf
                    da_rows = da_rows + jnp.where(lane_id == h, jnp.sum(Q, axis=1, keepdims=True), 0.0)
                    daT = daT + jnp.where(sub_id == h, jnp.sum(Q, axis=0, keepdims=True), 0.0)
                hpb = prev_ref[0, 0, lo:lo + 128, :]
                hp = hpb.astype(F32)
                zoff = _dot_nt(cg, hpb)
                e_p = q["e_f"][:, cols]
                dY_pf = dY[:, cols]
                dZb = (dY_pf * e_p).astype(BF16)
                dcg = dcg + _dot(dZb, hpb)
                dhp_off = _dot_tn(dZb, cg)
                prod_off.append(dY_pf * zoff * e_p)
                dS = dh_ref[lo:lo + 128, :]
                dSb = dS.astype(BF16)
                U = _dot_nt(bg, dSb)
                dxt_p = dxt_p + U * q["w_f"][:, cols]
                dbg = dbg + _dot(xwb[:, cols], dSb)
                prod_st.append(q["xt"][:, cols] * U)
                dh_ref[lo:lo + 128, :] = _pair_decay(alast, h0) * dS + dhp_off
                dsh = dS * hp
                for k in range(2):
                    total = jnp.sum(dsh[k * SSD_HEAD_DIM:(k + 1) * SSD_HEAD_DIM, :], axis=(0, 1), keepdims=True)
                    hsum_row = hsum_row + jnp.where(lane_row == h0 + k, total, 0.0)
                dxt.append(dxt_p)
            dGb = dG.astype(BF16)
            dcs.append(dcg + _dot(dGb, bg))
            dbs.append(dbg + _dot_tn(dGb, cg))
        dxt = jnp.concatenate(dxt, axis=1)
        da_rows = da_rows + _gather_heads(jnp.concatenate(prod_off, axis=1), E)
        dww = _gather_heads(jnp.concatenate(prod_st, axis=1), E) * jnp.exp(alast - acum)
        da_rows = da_rows - dww
        dlast = _colsum(dww) + jnp.exp(alast) * hsum_row
        triT = q["triT"]
        ddA = (_sum3(lambda part: _dot(triT, part), _split3(da_rows))
               - _sum3(lambda part: _dot_nt(triT, part), _split3(daT)) + dlast)
        ddA = jnp.where(q["head"], ddA, 0.0)
        ddt = ddA * q["a128"] + _gather_heads(dxt * xs, E)
        ddt_raw = jnp.where(q["head"], ddt * _sigmoid(q["pre"]), 0.0)
        ddtk_ref[0] = ddt_raw
        dxs = dxt * q["dt_f"] + dsk_f * dY
        dxbc_ref[0] = jnp.concatenate([dxs] + dbs + dcs, axis=1).astype(BF16)
        dvec_ref[0:1, :] += _colsum(ddt_raw)
        dvec_ref[1:2, :] += _colsum(ddA * q["dt"]) * q["a128"]

    rev = lambda n: pl.BlockSpec((B, L, n), lambda c: (0, nc - 1 - c, 0))
    vec = pl.BlockSpec((1, LANES), lambda c: (0, 0))
    sd = jax.ShapeDtypeStruct
    return pl.pallas_call(
        body, grid=(nc,), name="ssd_bwd",
        in_specs=[rev(D_CONV), rev(LANES), rev(D_SSD), rev(D_SSD),
                  pl.BlockSpec((B, 1, D_SSD, SSD_STATE), lambda c: (0, nc - 1 - c, 0, 0)), rev(D_SSD), vec, vec,
                  pl.BlockSpec((1, D_SSD), lambda c: (0, 0)),
                  pl.BlockSpec((1, D_SSD), lambda c: (0, 0)), pl.BlockSpec((LANES, D_SSD), lambda c: (0, 0))],
        out_specs=[rev(D_CONV), rev(LANES), rev(D_SSD), pl.BlockSpec((1, D_SSD), lambda c: (0, 0)),
                   pl.BlockSpec((8, LANES), lambda c: (0, 0))],
        out_shape=[sd((B, S, D_CONV), BF16), sd((B, S, LANES), F32), sd((B, S, D_SSD), BF16), sd((1, D_SSD), F32),
                   sd((8, LANES), F32)],
        scratch_shapes=[pltpu.VMEM((B, D_SSD, SSD_STATE), F32), pltpu.VMEM((1, D_SSD), F32)],
        compiler_params=_cparams(("arbitrary",)),
    )(xbc, dtk, z, y, prev, dys, dtb, alog, dsk, nw, expand)


def _rope_tables(pos_ref, invf_ref, place_ref):
    ang = invf_ref[...] * pos_ref[0].astype(F32)
    place = place_ref[...]
    cosf = 1.0 + _sum3(lambda part: _dot_tn(part, place), _split3(jnp.cos(ang) - 1.0))
    sinf = _sum3(lambda part: _dot_tn(part, place), _split3(jnp.sin(ang)))
    return cosf, sinf


def _rot(u):
    lane = lax.broadcasted_iota(jnp.int32, u.shape, 1)
    first = (lane >= QK_NOPE) & (lane < QK_NOPE + QK_ROPE // 2)
    second = (lane >= QK_NOPE + QK_ROPE // 2) & (lane < QK_DIM)
    return jnp.where(first, -pltpu.roll(u, LANES - QK_ROPE // 2, 1), jnp.where(second, pltpu.roll(u, QK_ROPE // 2, 1), 0.0))


def _rope_lanes(shape):
    lane = lax.broadcasted_iota(jnp.int32, shape, 1)
    return (lane >= QK_NOPE) & (lane < QK_DIM)


def _mla_prep(cq, ckv, dtk, pos, qw, kvw, wuq, wukv, invf, place):
    T = cq.shape[0]
    tm = min(WIDE_TOKEN_TILE, T)
    scale = 1.0 / math.sqrt(QK_DIM)
    HW = MLA_HEADS * HEAD_LANES

    def body(cq_ref, ckv_ref, dtk_ref, pos_ref, qw_ref, kvw_ref, wuq_ref, wukv_ref, invf_ref, place_ref, q_ref, k_ref, v_ref,
             cos_ref, sin_ref):
        xh, _ = _rms_stats(cq_ref[...])
        qv = _dot((xh * qw_ref[...]).astype(BF16), wuq_ref[...])
        xh, _ = _rms_stats(ckv_ref[...])
        kv = _dot((xh * kvw_ref[...]).astype(BF16), wukv_ref[...])
        cosf, sinf = _rope_tables(pos_ref, invf_ref, place_ref)
        cos_ref[...] = cosf
        sin_ref[...] = sinf
        rope = lambda u: u * cosf + _rot(u) * sinf
        dtkv = dtk_ref[...]
        kr = rope(jnp.where(_rope_lanes(dtkv.shape), dtkv, 0.0))
        for h in range(MLA_HEADS):
            cols = slice(h * HEAD_LANES, (h + 1) * HEAD_LANES)
            q_ref[:, cols] = (rope(qv[:, cols]) * scale).astype(BF16)
            k_ref[:, cols] = (kv[:, cols] + kr).astype(BF16)
        v_ref[...] = kv[:, HW:].astype(BF16)

    rows = lambda n: pl.BlockSpec((tm, n), lambda i: (i, 0))
    return pl.pallas_call(
        body, grid=(T // tm,), name="mla_prep",
        in_specs=[rows(Q_LORA), rows(KV_LORA), rows(LANES), pl.BlockSpec((1, 1, tm), lambda i: (i, 0, 0)),
                  _resident((1, Q_LORA)), _resident((1, KV_LORA)), _resident((Q_LORA, HW)), _resident((KV_LORA, 2 * HW)),
                  _resident((QK_ROPE // 2, 1)), _resident((QK_ROPE // 2, LANES))],
        out_specs=[rows(HW), rows(HW), rows(HW), rows(LANES), rows(LANES)],
        out_shape=[jax.ShapeDtypeStruct((T, HW), BF16)] * 3 + [jax.ShapeDtypeStruct((T, LANES), F32)] * 2,
        compiler_params=_cparams(("arbitrary",)),
    )(cq, ckv, dtk, pos.reshape(T // tm, 1, tm), qw, kvw, wuq, wukv, invf, place)


def _mla_prep_bwd(dq, dk, dv, cq, ckv, cos_t, sin_t, qw, kvw, wuq, wukv):
    T = cq.shape[0]
    tm = min(WIDE_TOKEN_TILE, T)
    scale = 1.0 / math.sqrt(QK_DIM)
    HW = MLA_HEADS * HEAD_LANES

    def body(dq_ref, dk_ref, dv_ref, cq_ref, ckv_ref, cos_ref, sin_ref, qw_ref, kvw_ref, wuq_ref, wukv_ref,
             dcq_ref, dckv_ref, ddtk_ref, qn_ref, kvn_ref, dqo_ref, dkvo_ref, dqw_ref, dkvw_ref):
        @pl.when(pl.program_id(0) == 0)
        def _():
            dqw_ref[...] = jnp.zeros_like(dqw_ref)
            dkvw_ref[...] = jnp.zeros_like(dkvw_ref)

        cosf, sinf = cos_ref[...], sin_ref[...]
        unrope = lambda d: d * cosf - _rot(d * sinf)
        dkr = jnp.zeros((tm, LANES), F32)
        nope = lax.broadcasted_iota(jnp.int32, (tm, LANES), 1) < QK_NOPE
        for h in range(MLA_HEADS):
            cols = slice(h * HEAD_LANES, (h + 1) * HEAD_LANES)
            dqo_ref[:, cols] = unrope(dq_ref[:, cols].astype(F32) * scale).astype(BF16)
            dkh = dk_ref[:, cols].astype(F32)
            dkr = dkr + jnp.where(_rope_lanes(dkh.shape), dkh, 0.0)
            dkvo_ref[:, cols] = jnp.where(nope, dkh, 0.0).astype(BF16)
        dkvo_ref[:, HW:] = dv_ref[...].astype(BF16)
        ddtk_ref[...] = unrope(dkr)
        xh, r = _rms_stats(cq_ref[...])
        qn_ref[...] = (xh * qw_ref[...]).astype(BF16)
        dx, dw_rows = _rms_bwd(_dot_nt(dqo_ref[...], wuq_ref[...]), xh, r, qw_ref[...])
        dcq_ref[...] = dx
        dqw_ref[...] += _colsum(dw_rows)
        xh, r = _rms_stats(ckv_ref[...])
        kvn_ref[...] = (xh * kvw_ref[...]).astype(BF16)
        dx, dw_rows = _rms_bwd(_dot_nt(dkvo_ref[...], wukv_ref[...]), xh, r, kvw_ref[...])
        dckv_ref[...] = dx
        dkvw_ref[...] += _colsum(dw_rows)

    rows = lambda n: pl.BlockSpec((tm, n), lambda i: (i, 0))
    sd = jax.ShapeDtypeStruct
    return pl.pallas_call(
        body, grid=(T // tm,), name="mla_prep_bwd",
        in_specs=[rows(HW), rows(HW), rows(HW), rows(Q_LORA), rows(KV_LORA), rows(LANES), rows(LANES), _resident((1, Q_LORA)),
                  _resident((1, KV_LORA)), _resident((Q_LORA, HW)), _resident((KV_LORA, 2 * HW))],
        out_specs=[rows(Q_LORA), rows(KV_LORA), rows(LANES), rows(Q_LORA), rows(KV_LORA), rows(HW), rows(2 * HW),
                   pl.BlockSpec((1, Q_LORA), lambda i: (0, 0)), pl.BlockSpec((1, KV_LORA), lambda i: (0, 0))],
        out_shape=[sd((T, Q_LORA), F32), sd((T, KV_LORA), F32), sd((T, LANES), F32), sd((T, Q_LORA), BF16),
                   sd((T, KV_LORA), BF16), sd((T, HW), BF16), sd((T, 2 * HW), BF16), sd((1, Q_LORA), F32),
                   sd((1, KV_LORA), F32)],
        compiler_params=_cparams(("arbitrary",)),
    )(dq, dk, dv, cq, ckv, cos_t, sin_t, qw, kvw, wuq, wukv)


def _causal_mask(t):
    row = lax.broadcasted_iota(jnp.int32, (t, t), 0)
    col = lax.broadcasted_iota(jnp.int32, (t, t), 1)
    return col <= row


def _attn_fwd(q, k, v):
    B, S, HW = q.shape
    H = HW // HEAD_LANES
    t = min(ATTN_FWD_Q_TILE, S)
    tk = min(ATTN_FWD_KV_TILE, t)
    nq = S // t
    per = t // tk

    pair = 4
    pw = pair * HEAD_LANES

    def body(q_ref, k_ref, v_ref, o_ref, lse_ref):
        qi = pl.program_id(2)
        lanes = [slice(hh * HEAD_LANES, (hh + 1) * HEAD_LANES) for hh in range(pair)]
        qs = [q_ref[0, :, cols] for cols in lanes]

        def step(j, carry, diag):
            sl = pl.ds(pl.multiple_of(j * tk, tk), tk)
            out = []
            for qv, cols, (m, l, acc) in zip(qs, lanes, carry):
                s = _dot_nt(qv, k_ref[0, sl, cols])
                if diag is not None:
                    row = lax.broadcasted_iota(jnp.int32, (t, tk), 0)
                    col = lax.broadcasted_iota(jnp.int32, (t, tk), 1)
                    s = jnp.where(col + diag * tk <= row, s, -1e30)
                m_new = jnp.maximum(m, jnp.max(s, axis=-1, keepdims=True))
                alpha = jnp.exp(m - m_new)
                p = jnp.exp(s - m_new)
                l = alpha * l + jnp.sum(p, axis=-1, keepdims=True)
                acc = alpha * acc + _dot(p.astype(BF16), v_ref[0, sl, cols])
                out.append((m_new, l, acc))
            return tuple(out)

        init = tuple((jnp.full((t, 1), -1e30, F32), jnp.zeros((t, 1), F32), jnp.zeros((t, HEAD_LANES), F32))
                     for _ in range(pair))
        carry = lax.fori_loop(0, qi * per, lambda j, c: step(j, c, None), init)
        for d in range(per):
            carry = step(qi * per + d, carry, d)
        for hh, (m, l, acc) in enumerate(carry):
            o_ref[0, :, lanes[hh]] = (acc / l).astype(BF16)
            lse_ref[0, hh] = m + jnp.log(l)

    return pl.pallas_call(
        body, grid=(B, H // pair, nq), name="attn_fwd",
        in_specs=[pl.BlockSpec((1, t, pw), lambda b, h, i: (b, i, h)),
                  pl.BlockSpec((1, S, pw), lambda b, h, i: (b, 0, h)),
                  pl.BlockSpec((1, S, pw), lambda b, h, i: (b, 0, h))],
        out_specs=[pl.BlockSpec((1, t, pw), lambda b, h, i: (b, i, h)),
                   pl.BlockSpec((1, pair, t, 1), lambda b, h, i: (b, h, i, 0))],
        out_shape=[jax.ShapeDtypeStruct((B, S, HW), BF16), jax.ShapeDtypeStruct((B, H, S, 1), F32)],
        compiler_params=_cparams(("arbitrary", "arbitrary", "arbitrary")),
    )(q, k, v)


def _attn_bwd(q, k, v, o, do, lse):
    B, S, HW = q.shape
    H = HW // HEAD_LANES
    t = min(ATTN_BWD_TILE, S)
    nq = S // t

    pair = 2
    pw = pair * HEAD_LANES

    def body(q_ref, k_ref, v_ref, o_ref, do_ref, lse_ref, dq_out_ref, dk_ref, dv_ref, dq_ref):
        j = pl.program_id(2)

        @pl.when(j == 0)
        def _():
            dq_ref[...] = jnp.zeros_like(dq_ref)

        lanes = [slice(hh * HEAD_LANES, (hh + 1) * HEAD_LANES) for hh in range(pair)]

        def step(i, carry, masked):
            sl = pl.ds(pl.multiple_of(i * t, t), t)
            out = []
            for hh, (cols, (dk, dv)) in enumerate(zip(lanes, carry)):
                kj = k_ref[0, :, cols]
                qi = q_ref[0, sl, cols]
                doi = do_ref[0, sl, cols]
                s = _dot_nt(qi, kj)
                if masked:
                    s = jnp.where(_causal_mask(t), s, -1e30)
                p = jnp.exp(s - lse_ref[0, hh, sl, :])
                dv = dv + _dot_tn(p.astype(BF16), doi)
                dp = _dot_nt(doi, v_ref[0, :, cols])
                delta = jnp.sum(doi.astype(F32) * o_ref[0, sl, cols].astype(F32), axis=-1, keepdims=True)
                dsb = (p * (dp - delta)).astype(BF16)
                dk = dk + _dot_tn(dsb, qi)
                dq_ref[sl, cols] += _dot(dsb, kj)
                out.append((dk, dv))
            return tuple(out)

        zero = jnp.zeros((t, HEAD_LANES), F32)
        carry = step(j, ((zero, zero),) * pair, True)
        carry = lax.fori_loop(j + 1, nq, lambda i, c: step(i, c, False), carry)
        for cols, (dk, dv) in zip(lanes, carry):
            dk_ref[0, :, cols] = dk.astype(BF16)
            dv_ref[0, :, cols] = dv.astype(BF16)

        @pl.when(j == nq - 1)
        def _():
            dq_out_ref[0] = dq_ref[...].astype(BF16)

    full = pl.BlockSpec((1, S, pw), lambda b, h, j: (b, 0, h))
    tile = pl.BlockSpec((1, t, pw), lambda b, h, j: (b, j, h))
    sd = jax.ShapeDtypeStruct
    return pl.pallas_call(
        body, grid=(B, H // pair, nq), name="attn_bwd",
        in_specs=[full, tile, tile, full, full, pl.BlockSpec((1, pair, S, 1), lambda b, h, j: (b, h, 0, 0))],
        out_specs=[full, tile, tile],
        out_shape=[sd((B, S, HW), BF16), sd((B, S, HW), BF16), sd((B, S, HW), BF16)],
        scratch_shapes=[pltpu.VMEM((S, pw), F32)],
        compiler_params=_cparams(("arbitrary", "arbitrary", "arbitrary")),
    )(q, k, v, o, do, lse)


def _mix_out(x1, yssd, o, mw, wout, g, seq):
    T, D = x1.shape
    tm = min(WIDE_TOKEN_TILE, seq)
    tps = seq // tm

    def body(x_ref, ys_ref, o_ref, mw_ref, w_ref, g_ref, xo_ref, m_ref, yc_ref):
        xh, _ = _rms_stats(o_ref[...].astype(F32))
        ycat = jnp.concatenate([ys_ref[...], (xh * mw_ref[...]).astype(BF16)], axis=1)
        m = _dot(ycat, w_ref[...])
        xo_ref[...] = x_ref[...] + g_ref[0] * m
        m_ref[...] = m.astype(BF16)
        yc_ref[...] = ycat

    rows = lambda n: pl.BlockSpec((tm, n), lambda i: (i, 0))
    perb = pl.BlockSpec((1, 1, D), lambda i: (i // tps, 0, 0))
    sd = jax.ShapeDtypeStruct
    return pl.pallas_call(
        body, grid=(T // tm,), name="mix_out",
        in_specs=[rows(D), rows(D_SSD), rows(D_MLA), _resident((1, D_MLA)), _resident((D_SSD + D_MLA, D)), perb],
        out_specs=[rows(D), rows(D), rows(D_SSD + D_MLA)],
        out_shape=[sd((T, D), F32), sd((T, D), BF16), sd((T, D_SSD + D_MLA), BF16)],
        compiler_params=_cparams(("arbitrary",)),
    )(x1, yssd, o, mw, wout, g)


def _mix_out_bwd(dx2, m, o, mw, wout, g, seq):
    T, D = dx2.shape
    B = T // seq
    tm = min(WIDE_TOKEN_TILE, seq)
    tps = seq // tm

    def body(dx_ref, m_ref, o_ref, mw_ref, w_ref, g_ref, dys_ref, do_ref, dm_ref, dg_ref, dmw_ref):
        i = pl.program_id(0)

        @pl.when(i % tps == 0)
        def _():
            dg_ref[...] = jnp.zeros_like(dg_ref)

        @pl.when(i == 0)
        def _():
            dmw_ref[...] = jnp.zeros_like(dmw_ref)

        dxv = dx_ref[...]
        dg_ref[0] += _colsum(dxv * m_ref[...].astype(F32))
        dmb = (g_ref[0] * dxv).astype(BF16)
        dm_ref[...] = dmb
        dycat = _dot_nt(dmb, w_ref[...])
        dys_ref[...] = dycat[:, :D_SSD].astype(BF16)
        xh, r = _rms_stats(o_ref[...].astype(F32))
        dx, dw_rows = _rms_bwd(dycat[:, D_SSD:], xh, r, mw_ref[...])
        do_ref[...] = dx.astype(BF16)
        dmw_ref[...] += _colsum(dw_rows)

    rows = lambda n: pl.BlockSpec((tm, n), lambda i: (i, 0))
    perb = pl.BlockSpec((1, 1, D), lambda i: (i // tps, 0, 0))
    sd = jax.ShapeDtypeStruct
    return pl.pallas_call(
        body, grid=(T // tm,), name="mix_out_bwd",
        in_specs=[rows(D), rows(D), rows(D_MLA), _resident((1, D_MLA)), _resident((D_SSD + D_MLA, D)), perb],
        out_specs=[rows(D_SSD), rows(D_MLA), rows(D), perb, pl.BlockSpec((1, D_MLA), lambda i: (0, 0))],
        out_shape=[sd((T, D_SSD), BF16), sd((T, D_MLA), BF16), sd((T, D), BF16), sd((B, 1, D), F32), sd((1, D_MLA), F32)],
        compiler_params=_cparams(("arbitrary",)),
    )(dx2, m, o, mw, wout, g)


def _win_to_kernel(w):
    z0 = jnp.zeros((48, w.shape[1]), w.dtype)
    z1 = jnp.zeros((32, w.shape[1]), w.dtype)
    return jnp.concatenate([w[:2560], w[2576:3216], w[2560:2576], z0, w[3216:3248], z1], axis=0)


def _win_from_kernel(g):
    return jnp.concatenate([g[:2560], g[3200:3216], g[2560:3200], g[3264:3296]], axis=0)


def _wuq_to_kernel(w):
    w = w.reshape(Q_LORA, MLA_HEADS, QK_DIM)
    return jnp.pad(w, ((0, 0), (0, 0), (0, HEAD_LANES - QK_DIM))).reshape(Q_LORA, MLA_HEADS * HEAD_LANES)


def _wuq_from_kernel(g):
    return g.reshape(Q_LORA, MLA_HEADS, HEAD_LANES)[:, :, :QK_DIM].reshape(Q_LORA, MLA_HEADS * QK_DIM)


def _wukv_to_kernel(w):
    w = w.reshape(KV_LORA, MLA_HEADS, QK_NOPE + V_HEAD)
    kp = jnp.pad(w[:, :, :QK_NOPE], ((0, 0), (0, 0), (0, HEAD_LANES - QK_NOPE)))
    return jnp.concatenate([kp.reshape(KV_LORA, -1), w[:, :, QK_NOPE:].reshape(KV_LORA, -1)], axis=1)


def _wukv_from_kernel(g):
    hw = MLA_HEADS * HEAD_LANES
    kp = g[:, :hw].reshape(KV_LORA, MLA_HEADS, HEAD_LANES)[:, :, :QK_NOPE]
    vp = g[:, hw:].reshape(KV_LORA, MLA_HEADS, V_HEAD)
    return jnp.concatenate([kp, vp], axis=2).reshape(KV_LORA, MLA_HEADS * (QK_NOPE + V_HEAD))


def _lanes16(v):
    return jnp.pad(v.reshape(1, SSD_HEADS), ((0, 0), (0, LANES - SSD_HEADS)))


def _constants():
    e = np.zeros((LANES, D_SSD), np.float32)
    for h in range(SSD_HEADS):
        e[h, h * SSD_HEAD_DIM:(h + 1) * SSD_HEAD_DIM] = 1.0
    inv_freq = ROPE_THETA ** (-jnp.arange(0, QK_ROPE, 2, dtype=F32) / QK_ROPE)
    half = QK_ROPE // 2
    place = np.zeros((half, LANES), np.float32)
    for j in range(half):
        place[j, QK_NOPE + j] = place[j, QK_NOPE + half + j] = 1.0
    return jnp.asarray(e, BF16), inv_freq.reshape(half, 1), jnp.asarray(place, BF16)


def _local_step(x, positions, mod, w, later_weights, small, tgt, on_grads, sync):
    B, S, D = x.shape
    T = B * S
    expand, invf, place = _constants()
    x0 = x.reshape(T, D)
    pos = positions.reshape(T)
    mods = [mod[:, i * D:(i + 1) * D].reshape(B, 1, D) for i in range(N_MOD)]
    sh1, sc1, g1, sh2, sc2, g2, sh3, sc3, g3 = mods
    dtb, alog = _lanes16(small["dt_bias"]), _lanes16(small["a_log"])
    dsk = jnp.repeat(small["d_skip"].reshape(1, SSD_HEADS), SSD_HEAD_DIM, axis=1)

    x1, a1, u1, f1 = _ffn_fwd(x0, small["norm_ffn1"], sh1, sc1, g1, w["ffn1_w_gate"], w["ffn1_w_up"], w["ffn1_w_down"], S, "ffn1_fwd")
    w = {**w, **later_weights(f1)}
    z, xraw, cq, ckv, dtk = _inproj_fwd(x1, small["norm_mix"], sh2, sc2, w["w_in"], S)
    xraw3 = xraw.reshape(B, S, D_CONV)
    xbc = _conv_fwd(xraw3, small["conv_w"], small["conv_b"])
    dtk3, z3 = dtk.reshape(B, S, LANES), z.reshape(B, S, D_SSD)
    y, yssd, prev = _ssd_fwd(xbc, dtk3, z3, dtb, alog, dsk, small["ssd_norm_w"], expand)
    q, k, v, cos_t, sin_t = _mla_prep(cq, ckv, dtk, pos, small["q_norm_w"], small["kv_norm_w"], w["w_uq"], w["w_ukv"], invf,
                                      place)
    hw = MLA_HEADS * HEAD_LANES
    q3, k3, v3 = q.reshape(B, S, hw), k.reshape(B, S, hw), v.reshape(B, S, hw)
    o3, lse = _attn_fwd(q3, k3, v3)
    o = o3.reshape(T, hw)
    x2, m, ycat = _mix_out(x1, yssd.reshape(T, D_SSD), o, small["mla_norm_w"], w["w_out"], g2, S)
    dx3, a2, u2, f2, loss, d_norm_final = _ffn_fwd(
        x2, small["norm_ffn2"], sh3, sc3, g3, w["ffn2_w_gate"], w["ffn2_w_up"], w["ffn2_w_down"], S, "ffn2_fwd",
        head=(small["norm_final"].reshape(1, D), tgt.reshape(T, D)))

    gw, gs = {}, {}
    dx2, h3, s3, df3, da3, du3, dsh3, dsc3, dg3, gs["norm_ffn2"] = _ffn_bwd(
        dx3, x2, small["norm_ffn2"], sh3, sc3, g3, a2, u2, f2, w["ffn2_w_gate"], w["ffn2_w_up"], w["ffn2_w_down"], S, "ffn2_bwd")
    gw["ffn2_w_gate"], gw["ffn2_w_up"], gw["ffn2_w_down"] = _ffn_wgrad(h3, s3, df3, da3, du3, dsh3, "ffn2_wgrad")
    g2 = g2 + on_grads(("ffn2_w_gate", "ffn2_w_up", "ffn2_w_down"), gw)

    dys, do, dm, dg2, gs["mla_norm_w"] = _mix_out_bwd(dx2, m, o, small["mla_norm_w"], w["w_out"], g2, S)
    gw["w_out"] = _mm_tn(ycat, dm, 512, "dwout")

    dq3, dk3, dv3 = _attn_bwd(q3, k3, v3, o3, do.reshape(B, S, hw), lse)
    dcq, dckv, ddtk_b, qn, kvn, dqb, dkvb, gs["q_norm_w"], gs["kv_norm_w"] = _mla_prep_bwd(
        dq3.reshape(T, hw), dk3.reshape(T, hw), dv3.reshape(T, hw), cq, ckv, cos_t, sin_t, small["q_norm_w"] + sync(dq3),
        small["kv_norm_w"], w["w_uq"], w["w_ukv"])
    gw["w_uq"] = _mm_tn(qn, dqb, 512, "dwuq")
    gw["w_ukv"] = _mm_tn(kvn, dkvb, 1024, "dwukv")

    dxbc, ddtk_a, dz, gs["ssd_norm_w"], dvec = _ssd_bwd(
        xbc, dtk3, z3, y, prev, dys.reshape(B, S, D_SSD), dtb, alog, dsk, small["ssd_norm_w"], expand)
    gs["dt_bias"], gs["a_log"], gs["d_skip"] = dvec[0:1, :SSD_HEADS], dvec[1:2, :SSD_HEADS], dvec[2:3, :SSD_HEADS]
    dxraw, gs["conv_w"], gs["conv_b"] = _conv_bwd(dxbc, xraw3, small["conv_w"], small["conv_b"])
    dx1, h2, dproj, dsh2, dsc2, gs["norm_mix"] = _inproj_bwd(
        dx2, x1, small["norm_mix"], sh2, sc2, w["w_in"], dz.reshape(T, D_SSD), dxraw.reshape(T, D_CONV), dcq, dckv,
        ddtk_a.reshape(T, LANES), ddtk_b, S)
    gw["w_in"] = _mm_tn(dproj, h2, 512, "dwin")
    g1 = g1 + on_grads(("w_in", "w_uq", "w_ukv", "w_out"), gw)

    dx0, h1, s1, df1, da1, du1, dsh1, dsc1, dg1, gs["norm_ffn1"] = _ffn_bwd(
        dx1, x0, small["norm_ffn1"], sh1, sc1, g1, a1, u1, f1, w["ffn1_w_gate"], w["ffn1_w_up"], w["ffn1_w_down"], S, "ffn1_bwd")
    gw["ffn1_w_gate"], gw["ffn1_w_up"], gw["ffn1_w_down"] = _ffn_wgrad(h1, s1, df1, da1, du1, dsh1 + sync(dx0), "ffn1_wgrad")
    gs["norm_final"] = d_norm_final
    dmod = jnp.concatenate([t.reshape(B, D) for t in (dsh1, dsc1, dg1, dsh2, dsc2, dg2, dsh3, dsc3, dg3)], axis=1)
    return loss, dx0.reshape(B, S, D), gw, dmod, gs


HBM_SPEC = pl.BlockSpec(memory_space=pltpu.HBM)
VMEM_SPEC = pl.BlockSpec(memory_space=pltpu.VMEM)


def _place():
    return lax.axis_index("x"), lax.axis_index("y"), lax.axis_index("c")


def _other_chips(mx, my):
    return [(1 - mx, my), (mx, 1 - my), (1 - mx, 1 - my)]


def _remote(src, dst, send_sem, recv_sem, to):
    return pltpu.make_async_remote_copy(src_ref=src, dst_ref=dst, send_sem=send_sem, recv_sem=recv_sem,
                                        device_id=to, device_id_type=MESH)


def _all_gather_small(xa, name):
    r, n = xa.shape

    def body(x_ref, o_ref, token, send_sems, recv_sems):
        mx, my, mc = _place()
        me = 4 * mx + 2 * my + mc
        token[...] = jnp.zeros_like(token)
        o_ref[pl.ds(me, 1)] = x_ref[...][None]
        sends = []
        for k in range(1, N_DEV):
            peer = (mx ^ (k >> 2), my ^ ((k >> 1) & 1), mc ^ (k & 1))
            cp = _remote(x_ref, o_ref.at[me], send_sems.at[k - 1], recv_sems.at[k - 1], peer)
            cp.start()
            sends.append(cp)
        for k in range(1, N_DEV):
            peer = (mx ^ (k >> 2), my ^ ((k >> 1) & 1), mc ^ (k & 1))
            slot = 4 * peer[0] + 2 * peer[1] + peer[2]
            _remote(x_ref, o_ref.at[slot], send_sems.at[k - 1], recv_sems.at[k - 1], peer).wait_recv()
        for cp in sends:
            cp.wait_send()

    return pl.pallas_call(
        body, name=name, in_specs=[VMEM_SPEC], out_specs=[VMEM_SPEC, VMEM_SPEC],
        out_shape=[jax.ShapeDtypeStruct((N_DEV, r, n), xa.dtype), jax.ShapeDtypeStruct((8, LANES), F32)],
        scratch_shapes=[pltpu.SemaphoreType.DMA((N_DEV - 1,)), pltpu.SemaphoreType.DMA((N_DEV - 1,))],
        compiler_params=pltpu.CompilerParams(vmem_limit_bytes=VMEM_LIMIT),
    )(xa)


def _halves_by_rows(shape):
    return (shape[-2] // 2) % 16 == 0


def _half_shape(shape):
    r, c = shape[-2:]
    return tuple(shape[:-2]) + ((r // 2, c) if _halves_by_rows(shape) else (r, c // 2))


def _half_index(shape, hc):
    r, c = shape[-2:]
    if _halves_by_rows(shape):
        return (pl.ds(pl.multiple_of(hc * (r // 2), 16), r // 2), slice(None))
    return (slice(None), pl.ds(pl.multiple_of(hc * (c // 2), LANES), c // 2))


def _half(ref, hc, lead=None):
    idx = _half_index(ref.shape, hc)
    return ref.at[idx] if lead is None else ref.at[(lead,) + idx]


def _gather_weights(shards):
    n = len(shards)

    def body(*refs):
        w_refs, o_refs, token = refs[:n], refs[n:2 * n], refs[2 * n]
        send_sems, recv_sems, stage_sems = refs[2 * n + 1:2 * n + 4]
        stages = refs[2 * n + 4:]
        mx, my, mc = _place()
        chip = 2 * mx + my
        others = _other_chips(mx, my)
        sibling = (mx, my, 1 - mc)
        token[...] = jnp.zeros_like(token)
        stage_in = [pltpu.make_async_copy(w, st, stage_sems.at[0, i]) for i, (w, st) in enumerate(zip(w_refs, stages))]
        for cp in stage_in:
            cp.start()
        first = []
        for i, (w, o) in enumerate(zip(w_refs, o_refs)):
            for k, (cx, cy) in enumerate(others):
                first.append(_remote(_half(w, mc), _half(o, mc, chip), send_sems.at[i, k],
                                     recv_sems.at[i, k], (cx, cy, mc)))
                first[-1].start()
        stage_out = []
        for i, (st, o) in enumerate(zip(stages, o_refs)):
            stage_in[i].wait()
            stage_out.append(pltpu.make_async_copy(st, o.at[chip], stage_sems.at[1, i]))
            stage_out[-1].start()
        passed = []
        for i, (w, o) in enumerate(zip(w_refs, o_refs)):
            for k, (cx, cy) in enumerate(others):
                landed = _half(o, mc, 2 * cx + cy)
                _remote(landed, landed, send_sems.at[i, k], recv_sems.at[i, k], (cx, cy, mc)).wait_recv()
                passed.append(_remote(landed, landed, send_sems.at[i, 3 + k], recv_sems.at[i, 3 + k], sibling))
                passed[-1].start()
        for i, (w, o) in enumerate(zip(w_refs, o_refs)):
            for k, (cx, cy) in enumerate(others):
                there = _half(o, 1 - mc, 2 * cx + cy)
                _remote(there, there, send_sems.at[i, 3 + k], recv_sems.at[i, 3 + k], sibling).wait_recv()
        for cp in first + passed:
            cp.wait_send()
        for cp in stage_out:
            cp.wait()

    out = pl.pallas_call(
        body, name="gather_weights", in_specs=[HBM_SPEC] * n, out_specs=[HBM_SPEC] * n + [VMEM_SPEC],
        out_shape=[jax.ShapeDtypeStruct((N_CHIPS,) + s.shape, s.dtype) for s in shards] + [jax.ShapeDtypeStruct((8, LANES), F32)],
        scratch_shapes=[pltpu.SemaphoreType.DMA((n, 6)), pltpu.SemaphoreType.DMA((n, 6)), pltpu.SemaphoreType.DMA((2, n))]
        + [pltpu.VMEM(s.shape, s.dtype) for s in shards],
        compiler_params=pltpu.CompilerParams(vmem_limit_bytes=VMEM_LIMIT),
    )(*shards)
    return out[:n], out[n]


SEM_SPEC = pl.BlockSpec(memory_space=pltpu.SEMAPHORE)
ANY_SPEC = pl.BlockSpec(memory_space=pl.ANY)
DATAFLOW = pltpu.SideEffectType.DATAFLOW_SIDE_EFFECTING


def _hbm(arr):
    return pltpu.with_memory_space_constraint(arr, pltpu.HBM)


def _gather_start(shards):
    n = len(shards)

    def body(*refs):
        w_refs, land_refs, send_sems, recv_sems, token = refs[:n], refs[n:2 * n], refs[2 * n], refs[2 * n + 1], refs[-1]
        mx, my, mc = _place()
        chip = 2 * mx + my
        for i, (w, land) in enumerate(zip(w_refs, land_refs)):
            for k, (cx, cy) in enumerate(_other_chips(mx, my)):
                _remote(_half(w, mc), _half(land, mc, chip), send_sems.at[3 * i + k],
                        recv_sems.at[3 * i + k], (cx, cy, mc)).start()
        token[...] = jnp.zeros_like(token)

    lands = [lax.empty((N_CHIPS,) + s.shape, s.dtype) for s in shards]
    out = pl.pallas_call(
        body, name="gather_start",
        out_shape=(pltpu.SemaphoreType.DMA((3 * n,)), pltpu.SemaphoreType.DMA((3 * n,)),
                   *[pltpu.HBM(s.shape, s.dtype) for s in shards], *[pltpu.HBM(l.shape, l.dtype) for l in lands],
                   jax.ShapeDtypeStruct((8, LANES), F32)),
        in_specs=[HBM_SPEC] * (2 * n), out_specs=(SEM_SPEC, SEM_SPEC, *[HBM_SPEC] * (2 * n), VMEM_SPEC),
        input_output_aliases={i: 2 + i for i in range(2 * n)},
        compiler_params=pltpu.CompilerParams(has_side_effects=DATAFLOW),
    )(*[_hbm(s) for s in shards], *[_hbm(l) for l in lands])
    return out[0], out[1], out[2:2 + n], out[2 + n:2 + 2 * n], out[-1]


def _gather_wait(send_sems, recv_sems, shards, lands, after):
    n = len(shards)

    def body(*refs):
        w_refs, land_refs, send_sems, recv_sems = refs[:n], refs[n:2 * n], refs[2 * n], refs[2 * n + 1]
        mx, my, mc = _place()
        for i, (w, land) in enumerate(zip(w_refs, land_refs)):
            for k, (cx, cy) in enumerate(_other_chips(mx, my)):
                cp = _remote(_half(w, mc), _half(land, mc, 2 * cx + cy), send_sems.at[3 * i + k],
                             recv_sems.at[3 * i + k], (cx, cy, mc))
                cp.wait_send()
                cp.wait_recv()

    out = pl.pallas_call(
        body, name="gather_wait",
        out_shape=(*[pltpu.HBM(s.shape, s.dtype) for s in shards], *[pltpu.HBM(l.shape, l.dtype) for l in lands]),
        in_specs=[HBM_SPEC] * (2 * n) + [SEM_SPEC, SEM_SPEC, ANY_SPEC], out_specs=tuple([HBM_SPEC] * (2 * n)),
        input_output_aliases={i: i for i in range(2 * n)},
        compiler_params=pltpu.CompilerParams(has_side_effects=DATAFLOW),
    )(*shards, *lands, send_sems, recv_sems, after)
    return out[n:]


def _gather_finish(shards, lands):
    n = len(shards)

    def body(*refs):
        w_refs, land_refs, o_refs = refs[:n], refs[n:2 * n], refs[2 * n:3 * n]
        send_sems, recv_sems, stage_sems = refs[3 * n:3 * n + 3]
        stages = refs[3 * n + 3:]
        mx, my, mc = _place()
        chip = 2 * mx + my
        others = _other_chips(mx, my)
        sibling = (mx, my, 1 - mc)
        stage_in = [pltpu.make_async_copy(w, st, stage_sems.at[0, i]) for i, (w, st) in enumerate(zip(w_refs, stages))]
        for cp in stage_in:
            cp.start()
        passed = []
        for i, (w, o) in enumerate(zip(w_refs, o_refs)):
            for k, (cx, cy) in enumerate(others):
                landed = _half(o, mc, 2 * cx + cy)
                passed.append(_remote(landed, landed, send_sems.at[i, k], recv_sems.at[i, k], sibling))
                passed[-1].start()
        stage_out = []
        for i, (st, o) in enumerate(zip(stages, o_refs)):
            stage_in[i].wait()
            stage_out.append(pltpu.make_async_copy(st, o.at[chip], stage_sems.at[1, i]))
            stage_out[-1].start()
        for i, (w, o) in enumerate(zip(w_refs, o_refs)):
            for k, (cx, cy) in enumerate(others):
                there = _half(o, 1 - mc, 2 * cx + cy)
                _remote(there, there, send_sems.at[i, k], recv_sems.at[i, k], sibling).wait_recv()
        for cp in passed:
            cp.wait_send()
        for cp in stage_out:
            cp.wait()

    return pl.pallas_call(
        body, name="gather_finish", in_specs=[HBM_SPEC] * (2 * n), out_specs=[HBM_SPEC] * n,
        out_shape=[jax.ShapeDtypeStruct(l.shape, l.dtype) for l in lands],
        input_output_aliases={n + i: i for i in range(n)},
        scratch_shapes=[pltpu.SemaphoreType.DMA((n, 3)), pltpu.SemaphoreType.DMA((n, 3)), pltpu.SemaphoreType.DMA((2, n))]
        + [pltpu.VMEM(s.shape, s.dtype) for s in shards],
        compiler_params=pltpu.CompilerParams(vmem_limit_bytes=VMEM_LIMIT),
    )(*shards, *lands)


def _scatter_start(ss, tag):
    n = len(ss)

    def body(*refs):
        s_refs, land_refs, send_sems, recv_sems, token = refs[:n], refs[n:2 * n], refs[2 * n], refs[2 * n + 1], refs[-1]
        mx, my, mc = _place()
        chip = 2 * mx + my
        for i, (s, land) in enumerate(zip(s_refs, land_refs)):
            for k, (cx, cy) in enumerate(_other_chips(mx, my)):
                _remote(s.at[2 * cx + cy], land.at[chip], send_sems.at[3 * i + k], recv_sems.at[3 * i + k],
                        (cx, cy, mc)).start()
        token[...] = jnp.zeros_like(token)

    lands = [lax.empty(s.shape, s.dtype) for s in ss]
    out = pl.pallas_call(
        body, name="scatter_start_" + tag,
        out_shape=(pltpu.SemaphoreType.DMA((3 * n,)), pltpu.SemaphoreType.DMA((3 * n,)),
                   *[pltpu.HBM(s.shape, s.dtype) for s in ss], *[pltpu.HBM(l.shape, l.dtype) for l in lands],
                   jax.ShapeDtypeStruct((8, LANES), F32)),
        in_specs=[HBM_SPEC] * (2 * n), out_specs=(SEM_SPEC, SEM_SPEC, *[HBM_SPEC] * (2 * n), VMEM_SPEC),
        input_output_aliases={i: 2 + i for i in range(2 * n)},
        compiler_params=pltpu.CompilerParams(has_side_effects=DATAFLOW),
    )(*[_hbm(s) for s in ss], *[_hbm(l) for l in lands])
    return out[0], out[1], out[2:2 + n], out[2 + n:2 + 2 * n], out[-1]


def _scatter_wait(send_sems, recv_sems, ss, lands, after, tag):
    n = len(ss)

    def body(*refs):
        s_refs, land_refs, send_sems, recv_sems = refs[:n], refs[n:2 * n], refs[2 * n], refs[2 * n + 1]
        mx, my, mc = _place()
        for i, (s, land) in enumerate(zip(s_refs, land_refs)):
            for k, (cx, cy) in enumerate(_other_chips(mx, my)):
                slot = land.at[2 * cx + cy]
                cp = _remote(s.at[2 * cx + cy], slot, send_sems.at[3 * i + k], recv_sems.at[3 * i + k], (cx, cy, mc))
                cp.wait_send()
                cp.wait_recv()

    out = pl.pallas_call(
        body, name="scatter_wait_" + tag,
        out_shape=(*[pltpu.HBM(s.shape, s.dtype) for s in ss], *[pltpu.HBM(l.shape, l.dtype) for l in lands]),
        in_specs=[HBM_SPEC] * (2 * n) + [SEM_SPEC, SEM_SPEC, ANY_SPEC], out_specs=tuple([HBM_SPEC] * (2 * n)),
        input_output_aliases={i: i for i in range(2 * n)},
        compiler_params=pltpu.CompilerParams(has_side_effects=DATAFLOW),
    )(*ss, *lands, send_sems, recv_sems, after)
    return out[:n], out[n:]


def _swap_halves(gs, after, name):
    n = len(gs)

    def body(*refs):
        g_refs, o_refs, send_sems, recv_sems = refs[:n], refs[n + 1:2 * n + 1], refs[2 * n + 1], refs[2 * n + 2]
        mx, my, mc = _place()
        copies = []
        for i, (g, o) in enumerate(zip(g_refs, o_refs)):
            src = g.at[(slice(None),) + _half_index(g.shape, 1 - mc)]
            copies.append(_remote(src, o, send_sems.at[i], recv_sems.at[i], (mx, my, 1 - mc)))
            copies[-1].start()
        for cp in copies:
            cp.wait()

    return pl.pallas_call(
        body, name=name, in_specs=[HBM_SPEC] * n + [ANY_SPEC], out_specs=[HBM_SPEC] * n,
        out_shape=[jax.ShapeDtypeStruct(_half_shape(g.shape), g.dtype) for g in gs],
        scratch_shapes=[pltpu.SemaphoreType.DMA((n,)), pltpu.SemaphoreType.DMA((n,))],
    )(*gs, after)


def _swap_start(gs, tag):
    n = len(gs)

    def body(*refs):
        g_refs, land_refs, send_sems, recv_sems, token = refs[:n], refs[n:2 * n], refs[2 * n], refs[2 * n + 1], refs[-1]
        mx, my, mc = _place()
        for i, (g, land) in enumerate(zip(g_refs, land_refs)):
            src = g.at[(slice(None),) + _half_index(g.shape, 1 - mc)]
            _remote(src, land, send_sems.at[i], recv_sems.at[i], (mx, my, 1 - mc)).start()
        token[...] = jnp.zeros_like(token)

    lands = [lax.empty(_half_shape(g.shape), g.dtype) for g in gs]
    out = pl.pallas_call(
        body, name="swap_start_" + tag,
        out_shape=(pltpu.SemaphoreType.DMA((n,)), pltpu.SemaphoreType.DMA((n,)), *[pltpu.HBM(g.shape, g.dtype) for g in gs],
                   *[pltpu.HBM(l.shape, l.dtype) for l in lands], jax.ShapeDtypeStruct((8, LANES), F32)),
        in_specs=[HBM_SPEC] * (2 * n), out_specs=(SEM_SPEC, SEM_SPEC, *[HBM_SPEC] * (2 * n), VMEM_SPEC),
        input_output_aliases={i: 2 + i for i in range(2 * n)},
        compiler_params=pltpu.CompilerParams(has_side_effects=DATAFLOW),
    )(*[_hbm(g) for g in gs], *[_hbm(l) for l in lands])
    return out[0], out[1], out[2:2 + n], out[2 + n:2 + 2 * n], out[-1]


def _swap_wait(send_sems, recv_sems, gs, lands, after, tag):
    n = len(gs)

    def body(*refs):
        g_refs, land_refs, send_sems, recv_sems = refs[:n], refs[n:2 * n], refs[2 * n], refs[2 * n + 1]
        mx, my, mc = _place()
        for i, (g, land) in enumerate(zip(g_refs, land_refs)):
            src = g.at[(slice(None),) + _half_index(g.shape, 1 - mc)]
            cp = _remote(src, land, send_sems.at[i], recv_sems.at[i], (mx, my, 1 - mc))
            cp.wait_send()
            cp.wait_recv()

    out = pl.pallas_call(
        body, name="swap_wait_" + tag,
        out_shape=(*[pltpu.HBM(g.shape, g.dtype) for g in gs], *[pltpu.HBM(l.shape, l.dtype) for l in lands]),
        in_specs=[HBM_SPEC] * (2 * n) + [SEM_SPEC, SEM_SPEC, ANY_SPEC], out_specs=tuple([HBM_SPEC] * (2 * n)),
        input_output_aliases={i: i for i in range(2 * n)},
        compiler_params=pltpu.CompilerParams(has_side_effects=DATAFLOW),
    )(*gs, *lands, send_sems, recv_sems, after)
    return out[:n], out[n:]


def _pair_sums(gs, gots, name):
    n = len(gs)

    def body(*refs):
        g_refs, got_refs, o_refs, load_sems, store_sems = refs[:n], refs[n:2 * n], refs[2 * n:3 * n], refs[3 * n], refs[3 * n + 1]
        mine, theirs, sums = refs[3 * n + 2:4 * n + 2], refs[4 * n + 2:5 * n + 2], refs[5 * n + 2:]
        mc = lax.axis_index("c")
        loads = []
        for i, (g, got) in enumerate(zip(g_refs, got_refs)):
            loads.append((pltpu.make_async_copy(g.at[(slice(None),) + _half_index(g.shape, mc)], mine[i], load_sems.at[i, 0]),
                          pltpu.make_async_copy(got, theirs[i], load_sems.at[i, 1])))
            for cp in loads[-1]:
                cp.start()
        stores = []
        for i in range(n):
            for cp in loads[i]:
                cp.wait()
            sums[i][...] = (mine[i][...].astype(F32) + theirs[i][...].astype(F32)).astype(BF16)
            stores.append(pltpu.make_async_copy(sums[i], o_refs[i], store_sems.at[i]))
            stores[-1].start()
        for cp in stores:
            cp.wait()

    halves = [jax.ShapeDtypeStruct(got.shape, BF16) for got in gots]
    return pl.pallas_call(
        body, name=name, in_specs=[HBM_SPEC] * (2 * n), out_specs=[HBM_SPEC] * n, out_shape=halves,
        scratch_shapes=[pltpu.SemaphoreType.DMA((n, 2)), pltpu.SemaphoreType.DMA((n,))]
        + [pltpu.VMEM(got.shape, g.dtype) for g, got in zip(gs, gots)] + [pltpu.VMEM(got.shape, got.dtype) for got in gots]
        + [pltpu.VMEM(got.shape, BF16) for got in gots],
        compiler_params=pltpu.CompilerParams(vmem_limit_bytes=VMEM_LIMIT),
    )(*gs, *gots)


def _reduce_join(owns, gots, name):
    n = len(owns)

    def body(*refs):
        own_refs, got_refs, mine_refs, theirs_refs = refs[:n], refs[n:2 * n], refs[2 * n:3 * n], refs[3 * n:4 * n]
        send_sems, recv_sems, load_sems, store_sems = refs[4 * n:4 * n + 4]
        parts, sums = refs[4 * n + 4:5 * n + 4], refs[5 * n + 4:]
        mx, my, mc = _place()
        chip = 2 * mx + my
        loads = []
        for i, (own, got, part) in enumerate(zip(own_refs, got_refs, parts)):
            loads.append([pltpu.make_async_copy((own if k == 0 else got).at[chip ^ k], part.at[k], load_sems.at[i, k])
                          for k in range(N_CHIPS)])
            for cp in loads[-1]:
                cp.start()
        out = []
        for i, (part, total, mine, theirs) in enumerate(zip(parts, sums, mine_refs, theirs_refs)):
            for cp in loads[i]:
                cp.wait()
            total[...] = ((part[0].astype(F32) + part[1].astype(F32)) + part[2].astype(F32)) + part[3].astype(F32)
            out.append(pltpu.make_async_copy(total, mine, store_sems.at[i]))
            out.append(_remote(total, theirs, send_sems.at[i], recv_sems.at[i], (mx, my, 1 - mc)))
            out[-2].start()
            out[-1].start()
        for cp in out:
            cp.wait()

    halves = [jax.ShapeDtypeStruct(o.shape[1:], F32) for o in owns]
    out = pl.pallas_call(
        body, name=name, in_specs=[HBM_SPEC] * (2 * n), out_specs=[HBM_SPEC] * (2 * n), out_shape=halves + halves,
        scratch_shapes=[pltpu.SemaphoreType.DMA((n,)), pltpu.SemaphoreType.DMA((n,)), pltpu.SemaphoreType.DMA((n, N_CHIPS)),
                        pltpu.SemaphoreType.DMA((n,))]
        + [pltpu.VMEM(o.shape, o.dtype) for o in owns] + [pltpu.VMEM(o.shape[1:], F32) for o in owns],
        compiler_params=pltpu.CompilerParams(vmem_limit_bytes=VMEM_LIMIT),
    )(*owns, *gots)
    return out[:n], out[n:]


def _adam_math(w, g, m, v):
    m2 = ADAM_B1 * m + (1.0 - ADAM_B1) * g
    v2 = ADAM_B2 * v + (1.0 - ADAM_B2) * (g * g)
    m_hat = m2 * (1.0 / (1.0 - ADAM_B1 ** ADAM_STEP))
    v_hat = v2 * (1.0 / (1.0 - ADAM_B2 ** ADAM_STEP))
    delta = -ADAM_LR * (m_hat / (jnp.sqrt(v_hat) + ADAM_EPS) + ADAM_WD * w)
    return delta, m2, v2


def _adam(w, g, m, v, name):
    def body(w_ref, g_ref, m_ref, v_ref, d_ref, m2_ref, v2_ref):
        d_ref[...], m2_ref[...], v2_ref[...] = _adam_math(w_ref[...], g_ref[...], m_ref[...], v_ref[...])

    return pl.pallas_call(body, name=name, out_shape=[jax.ShapeDtypeStruct(w.shape, F32)] * 3)(w, g, m, v)


def _adam_halves(w, m, v, mine, theirs, core, name):
    hr, hcols = _half_shape(w.shape)[1:]
    by_rows = _halves_by_rows(w.shape)

    def body(core_ref, w_ref, m_ref, v_ref, mine_ref, theirs_ref, g_ref, d_ref, m2_ref, v2_ref):
        g = jnp.where(pl.program_id(0) == core_ref[0], mine_ref[...], theirs_ref[...])
        g_ref[0] = g
        d_ref[0], m2_ref[0], v2_ref[0] = _adam_math(w_ref[0], g, m_ref[0], v_ref[0])

    half = pl.BlockSpec((1, hr, hcols), lambda hc, core_ref: (0, hc, 0) if by_rows else (0, 0, hc))
    whole = pl.BlockSpec((hr, hcols), lambda hc, core_ref: (0, 0))
    return pl.pallas_call(
        body, name=name,
        grid_spec=pltpu.PrefetchScalarGridSpec(
            num_scalar_prefetch=1, grid=(2,), in_specs=[half, half, half, whole, whole], out_specs=[half] * 4),
        out_shape=[jax.ShapeDtypeStruct(w.shape, F32)] * 4,
        compiler_params=_cparams(("arbitrary",)),
    )(core, w, m, v, mine, theirs)


ADA_COLS = N_MOD * D_MODEL // N_CHIPS


def _ada_fwd(c_all, w_ada, b_cols):
    def body(c_ref, w_ref, b_ref, o_ref):
        cv = c_ref[...]
        act = (cv * _sigmoid(cv)).astype(BF16)
        o_ref[...] = _dot(act, w_ref[...].astype(BF16)) + b_ref[...]

    return pl.pallas_call(
        body, name="ada_fwd", out_shape=jax.ShapeDtypeStruct((c_all.shape[0], ADA_COLS), F32),
        compiler_params=pltpu.CompilerParams(vmem_limit_bytes=VMEM_LIMIT),
    )(c_all, w_ada, b_cols)


def _ada_bwd(c_all, dmod_cols, w, m, v):
    nb = c_all.shape[0]
    tn = 384

    def body(c_ref, d_ref, w_ref, m_ref, v_ref, g_ref, dl_ref, m2_ref, v2_ref):
        cv = c_ref[...]
        act = (cv * _sigmoid(cv)).astype(BF16)
        g = _dot_tn(act, d_ref[...].astype(BF16))
        g_ref[...] = g
        dl_ref[...], m2_ref[...], v2_ref[...] = _adam_math(w_ref[...], g, m_ref[...], v_ref[...])

    blk = pl.BlockSpec((D_MODEL, tn), lambda j: (0, j))
    return pl.pallas_call(
        body, name="ada_bwd", grid=(ADA_COLS // tn,),
        in_specs=[pl.BlockSpec((nb, D_MODEL), lambda j: (0, 0)), pl.BlockSpec((nb, tn), lambda j: (0, j)), blk, blk, blk],
        out_specs=[blk] * 4, out_shape=[jax.ShapeDtypeStruct((D_MODEL, ADA_COLS), F32)] * 4,
        compiler_params=_cparams(("arbitrary",)),
    )(c_all, dmod_cols, w, m, v)


SMALL_NAMES = ("norm_ffn1", "norm_mix", "conv_w", "conv_b", "ssd_norm_w", "q_norm_w", "kv_norm_w", "mla_norm_w",
               "norm_ffn2", "norm_final", "dt_bias", "a_log", "d_skip")
SMALL_SIZES = (1024, 1024, CONV_WIDTH * D_CONV, D_CONV, 1024, Q_LORA, KV_LORA, 1024, 1024, 1024, 16, 16, 16)
SMALL_ROWS = 16
MOD_ROWS = 2 * N_MOD
SEND_ROWS = 40


def _pack_small(parts):
    flat = jnp.concatenate([parts[n].reshape(-1) for n in SMALL_NAMES])
    return jnp.pad(flat, (0, SMALL_ROWS * D_MODEL - flat.shape[0]))


def _unpack_small(flat):
    out, off = {}, 0
    for n, size in zip(SMALL_NAMES, SMALL_SIZES):
        out[n] = flat[off:off + size]
        off += size
    return out


def _small_sum(got):
    def body(g_ref, o_ref):
        bsum = jnp.zeros((N_MOD, D_MODEL), F32)
        ssum = jnp.zeros((SMALL_ROWS, D_MODEL), F32)
        for d in range(N_DEV):
            bsum = bsum + g_ref[d, 0:N_MOD, :] + g_ref[d, N_MOD:MOD_ROWS, :]
            ssum = ssum + g_ref[d, MOD_ROWS:MOD_ROWS + SMALL_ROWS, :]
        o_ref[...] = jnp.concatenate([bsum, ssum, jnp.zeros((32 - N_MOD - SMALL_ROWS, D_MODEL), F32)], axis=0)

    return pl.pallas_call(body, name="small_sum", out_shape=jax.ShapeDtypeStruct((32, D_MODEL), F32))(got)


BIG_NAMES = ("ffn1_w_gate", "ffn1_w_up", "ffn1_w_down", "w_in", "w_uq", "w_ukv", "w_out", "ffn2_w_gate", "ffn2_w_up",
             "ffn2_w_down")
_TO_KERNEL = {"w_in": _win_to_kernel, "w_uq": _wuq_to_kernel, "w_ukv": _wukv_to_kernel}
_FROM_KERNEL = {"w_in": _win_from_kernel, "w_uq": _wuq_from_kernel, "w_ukv": _wukv_from_kernel}


def _columns_joined(w4):
    n, r, c = w4.shape
    return w4.transpose(1, 0, 2).reshape(r, n * c)


def _columns_split(g):
    r, cols = g.shape
    return g.reshape(r, N_CHIPS, cols // N_CHIPS).transpose(1, 0, 2)


def kernel(x, c, positions, w_ada, b_ada, norm_ffn1, ffn1_w_gate, ffn1_w_up, ffn1_w_down, norm_mix, w_in, conv_w, conv_b, dt_bias, a_log, d_skip, ssd_norm_w, q_norm_w, w_uq, kv_norm_w, w_ukv, mla_norm_w, w_out, norm_ffn2, ffn2_w_gate, ffn2_w_up, ffn2_w_down, norm_final, loss_target, m_w_ada, m_b_ada, m_norm_ffn1, m_ffn1_w_gate, m_ffn1_w_up, m_ffn1_w_down, m_norm_mix, m_w_in, m_conv_w, m_conv_b, m_dt_bias, m_a_log, m_d_skip, m_ssd_norm_w, m_q_norm_w, m_w_uq, m_kv_norm_w, m_w_ukv, m_mla_norm_w, m_w_out, m_norm_ffn2, m_ffn2_w_gate, m_ffn2_w_up, m_ffn2_w_down, m_norm_final, v_w_ada, v_b_ada, v_norm_ffn1, v_ffn1_w_gate, v_ffn1_w_up, v_ffn1_w_down, v_norm_mix, v_w_in, v_conv_w, v_conv_b, v_dt_bias, v_a_log, v_d_skip, v_ssd_norm_w, v_q_norm_w, v_w_uq, v_kv_norm_w, v_w_ukv, v_mla_norm_w, v_w_out, v_norm_ffn2, v_ffn2_w_gate, v_ffn2_w_up, v_ffn2_w_down, v_norm_final):
    a = dict(locals())
    held_transposed = ("ffn1_w_gate", "ffn1_w_up", "ffn2_w_gate", "ffn2_w_up", "w_in")
    for n in held_transposed:
        for p in ("", "m_", "v_"):
            a[p + n] = a[p + n].transpose(0, 2, 1)
    B, S, D = x.shape
    mx, my, mc = _place()
    chip = 2 * mx + my
    dev = 2 * chip + mc
    core = mc.astype(jnp.int32).reshape(1)

    cw_rows = jnp.pad(conv_w[0], ((0, 0), (0, D - conv_w.shape[2])))
    got, _ = _all_gather_small(jnp.concatenate([c, cw_rows, jnp.zeros((8 - B - CONV_WIDTH, D), F32)], axis=0), "gather_c")
    c_all = got[:, :B, :].reshape(N_DEV * B, D)
    conv_full = got[::2, B:B + CONV_WIDTH, :conv_w.shape[2]].transpose(1, 0, 2).reshape(CONV_WIDTH, D_CONV)

    b_cols = lax.dynamic_slice(b_ada, (0, chip * ADA_COLS), (1, ADA_COLS))
    mod_all, mod_done = _all_gather_small(_ada_fwd(c_all, w_ada[0], b_cols), "gather_mod")
    mod = lax.dynamic_slice(mod_all, (0, B * dev, 0), (N_DEV, B, ADA_COLS))[::2].transpose(1, 0, 2).reshape(B, N_MOD * D)

    first = ("ffn1_w_gate", "ffn1_w_up", "ffn1_w_down")
    later = tuple(n for n in BIG_NAMES if n not in first)
    got_first, gathered = _gather_weights([(a[n][0] + mod_done[0, 0]).astype(BF16) for n in first])
    w = dict(zip(first, got_first))
    in_flight = _gather_start([(a[n][0] + gathered[0, 0]).astype(BF16) for n in later])

    def later_weights(after):
        send_sems, recv_sems, shards, lands, _ = in_flight
        lands = _gather_wait(send_sems, recv_sems, shards, lands, after)
        wl = dict(zip(later, _gather_finish([a[n][0].astype(BF16) for n in later], lands)))
        for n, to_kernel in _TO_KERNEL.items():
            wl[n] = to_kernel(wl[n].reshape(-1, D) if n in held_transposed else _columns_joined(wl[n]))
        wl["w_out"] = wl["w_out"].reshape(D_SSD + D_MLA, D)
        return wl

    small = {n: a[n].reshape(1, -1) for n in SMALL_NAMES if n not in ("conv_w", "norm_final")}
    small["conv_w"], small["norm_final"] = conv_full, norm_final

    def shards_of(names, gw):
        g4 = []
        for n in names:
            g = gw[n]
            if n in _FROM_KERNEL:
                g = _FROM_KERNEL[n](g) if n in held_transposed else _columns_split(_FROM_KERNEL[n](g))
            g4.append(g.reshape(N_CHIPS, a[n].shape[1], a[n].shape[2]))
        return g4

    def scatter_group(names, g4, swapped):
        pair = _pair_sums(g4, swapped, "pair_sums_" + names[0])
        return (names,) + tuple(_scatter_start(pair, names[0]))

    grads, deltas, new_m, new_v = {}, {}, {}, {}

    def finish_groups(some, after):
        names, owns, gots = [], [], []
        for group_names, send_sems, recv_sems, pair, lands, _ in some:
            pair, lands = _scatter_wait(send_sems, recv_sems, pair, lands, after, group_names[0])
            names += group_names
            owns += pair
            gots += lands
        mine, theirs = _reduce_join(owns, gots, "reduce_join_" + names[0])
        for n, own, other in zip(names, mine, theirs):
            grads[n], deltas[n], new_m[n], new_v[n] = _adam_halves(a[n], a["m_" + n], a["v_" + n], own, other, core, "adam_" + n)
        return deltas[names[-1]]

    swapping, groups = [], []

    def on_grads(names, gw):
        send_sems, recv_sems, g4, lands, token = _swap_start(shards_of(names, gw), names[0])
        swapping.append((names, send_sems, recv_sems, g4, lands))
        return token[0, 0]

    def sync(after):
        token = 0.0
        while swapping:
            names, send_sems, recv_sems, g4, lands = swapping.pop(0)
            g4, swapped = _swap_wait(send_sems, recv_sems, g4, lands, after, names[0])
            groups.append(scatter_group(names, g4, swapped))
            token = groups[-1][5][0, 0]
        return token

    loss_blk, grad_x, gw, dmod, gs = _local_step(x, positions, mod + in_flight[4][0, 0], w, later_weights, small, loss_target,
                                                 on_grads, sync)

    small_flat = _pack_small(gs).at[-1].set(loss_blk[0, 0])
    send = jnp.concatenate([dmod.reshape(MOD_ROWS, D), small_flat.reshape(SMALL_ROWS, D),
                            jnp.zeros((SEND_ROWS - MOD_ROWS - SMALL_ROWS, D), F32)], axis=0)
    got, _ = _all_gather_small(send, "gather_small")
    summed = _small_sum(got)
    sums = summed[N_MOD:N_MOD + SMALL_ROWS].reshape(-1)
    loss = sums[-1]
    gsmall = _unpack_small(sums)
    gsmall["conv_w"] = lax.dynamic_slice(gsmall["conv_w"].reshape(CONV_WIDTH, D_CONV), (0, chip * conv_w.shape[2]),
                                         (CONV_WIDTH, conv_w.shape[2]))
    gsmall["b_ada"] = summed[:N_MOD]
    names = ("b_ada",) + SMALL_NAMES
    rows = 208

    def pack(parts):
        flat = jnp.concatenate([parts[n].reshape(-1) for n in names])
        return jnp.pad(flat, (0, rows * LANES - flat.shape[0])).reshape(rows, LANES)

    packed = [pack({n: a[p + n] for n in names}) for p in ("", "m_", "v_")]
    g_p = pack(gsmall)
    outs = (g_p,) + tuple(_adam(packed[0], g_p, packed[1], packed[2], "adam_small"))
    for dst, flat in zip((grads, deltas, new_m, new_v), outs):
        flat, off = flat.reshape(-1), 0
        for n in names:
            dst[n] = flat[off:off + a[n].size].reshape(a[n].shape)
            off += a[n].size

    dmod_all = got[:, :MOD_ROWS, :].reshape(N_DEV * B, N_MOD * D)
    dmod_cols = lax.dynamic_slice(dmod_all, (0, chip * ADA_COLS), (N_DEV * B, ADA_COLS))
    ada = _ada_bwd(c_all, dmod_cols, w_ada[0], m_w_ada[0], v_w_ada[0])
    for dst, t in zip((grads, deltas, new_m, new_v), ada):
        dst["w_ada"] = t[None]

    g4 = shards_of(first, gw)
    last = scatter_group(first, g4, _swap_halves(g4, summed, "swap_" + first[0]))
    finish_groups([last], finish_groups(groups, last[5]))
    for dst in (grads, deltas, new_m, new_v):
        for n in held_transposed:
            dst[n] = dst[n].transpose(0, 2, 1)

    order = ("w_ada", "b_ada", "norm_ffn1", "ffn1_w_gate", "ffn1_w_up", "ffn1_w_down", "norm_mix", "w_in", "conv_w", "conv_b",
             "dt_bias", "a_log", "d_skip", "ssd_norm_w", "q_norm_w", "w_uq", "kv_norm_w", "w_ukv", "mla_norm_w", "w_out",
             "norm_ffn2", "ffn2_w_gate", "ffn2_w_up", "ffn2_w_down", "norm_final")
    return (loss, grad_x, *[grads[n] for n in order], *[deltas[n] for n in order], *[new_m[n] for n in order],
            *[new_v[n] for n in order])
```

```python
import functools
import math

import jax
import jax.numpy as jnp
import numpy as np
from jax import lax
from jax.experimental import pallas as pl
from jax.experimental.pallas import tpu as pltpu

F32 = jnp.float32
BF16 = jnp.bfloat16

D_MODEL = 1024
D_FF = 2816
D_SSD = 1024
D_MLA = 1024
SSD_HEADS = 16
SSD_HEAD_DIM = 64
SSD_GROUPS = 2
SSD_STATE = 128
CONV_WIDTH = 4
CHUNK = 128
MLA_HEADS = 8
QK_NOPE = 64
QK_ROPE = 32
QK_DIM = QK_NOPE + QK_ROPE
V_HEAD = 128
Q_LORA = 384
KV_LORA = 256
ROPE_THETA = 10000.0
N_MOD = 9
EPS = 1e-6
D_CONV = D_SSD + 2 * SSD_GROUPS * SSD_STATE
D_PROJ = 3328
HEAD_LANES = 128
ADAM_LR = 0.001
ADAM_B1 = 0.9
ADAM_B2 = 0.999
ADAM_EPS = 1e-08
ADAM_WD = 0.01
ADAM_STEP = 10

LANES = 128
VMEM_LIMIT = 56 * 1024 * 1024
TOKEN_TILE = 512
WIDE_TOKEN_TILE = 1024
WGRAD_TOKEN_TILE = 2048
ATTN_FWD_Q_TILE = 1024
ATTN_FWD_KV_TILE = 1024
ATTN_BWD_TILE = 1024
N_CHIPS = 4
N_DEV = 8

MESH = pl.DeviceIdType.MESH


def _dot(a, b):
    return jnp.dot(a, b, preferred_element_type=F32)


def _dot_nt(a, b):
    return lax.dot_general(a, b, (((1,), (1,)), ((), ())), preferred_element_type=F32)


def _dot_tn(a, b):
    return lax.dot_general(a, b, (((0,), (0,)), ((), ())), preferred_element_type=F32)


def _cparams(semantics):
    return pltpu.CompilerParams(dimension_semantics=semantics, vmem_limit_bytes=VMEM_LIMIT)


def _resident(shape):
    zeros = (0,) * len(shape)
    return pl.BlockSpec(shape, lambda *_: zeros, pipeline_mode=pl.Buffered(1))


def _sigmoid(x):
    return jax.nn.sigmoid(x)


def _rms_stats(x):
    r = lax.rsqrt(jnp.mean(x * x, axis=-1, keepdims=True) + EPS)
    return x * r, r


def _rms_bwd(dn, xh, r, w):
    dxh = dn * w
    dx = r * (dxh - xh * jnp.mean(dxh * xh, axis=-1, keepdims=True))
    return dx, dn * xh


def _colsum(v):
    return jnp.sum(v, axis=0, keepdims=True)


def _ffn_fwd(x, nw, sh, sc, g, wg, wu, wd, seq, name, head=None):
    T, D = x.shape
    fs = wg.shape[1]
    tm = min(TOKEN_TILE, seq)
    tps = seq // tm

    def body(x_ref, nw_ref, sh_ref, sc_ref, g_ref, wg_ref, wu_ref, wd_ref, *rest):
        if head is None:
            xo_ref, a_ref, u_ref, f_ref = rest
        else:
            nf_ref, t_ref, xo_ref, a_ref, u_ref, f_ref, loss_ref, dnf_ref = rest

            @pl.when(pl.program_id(0) == 0)
            def _():
                loss_ref[...] = jnp.zeros_like(loss_ref)
                dnf_ref[...] = jnp.zeros_like(dnf_ref)

        xv = x_ref[...]
        xh, _ = _rms_stats(xv)
        h = (xh * nw_ref[...]) * (1.0 + sc_ref[0]) + sh_ref[0]
        hb = h.astype(BF16)
        f = jnp.zeros((tm, D), F32)
        for j in range(N_CHIPS):
            a = _dot_nt(hb, wg_ref[j])
            u = _dot_nt(hb, wu_ref[j])
            a_ref[j] = a.astype(BF16)
            u_ref[j] = u.astype(BF16)
            f = f + _dot((a * _sigmoid(a) * u).astype(BF16), wd_ref[j])
        f_ref[...] = f.astype(BF16)
        xo = xv + 0.5 * g_ref[0] * f
        if head is None:
            xo_ref[...] = xo
        else:
            xh, r = _rms_stats(xo)
            nfv = nf_ref[...]
            err = xh * nfv - t_ref[...]
            loss_ref[...] += (0.5 / D) * jnp.sum(err * err)
            dxo, dw_rows = _rms_bwd(err * (1.0 / D), xh, r, nfv)
            xo_ref[...] = dxo
            dnf_ref[...] += _colsum(dw_rows)

    rows = lambda n: pl.BlockSpec((tm, n), lambda i: (i, 0))
    act = pl.BlockSpec((N_CHIPS, tm, fs), lambda i: (0, i, 0))
    perb = pl.BlockSpec((1, 1, D), lambda i: (i // tps, 0, 0))
    sd = jax.ShapeDtypeStruct
    in_specs = [rows(D), _resident((1, D)), perb, perb, perb, _resident((N_CHIPS, fs, D)), _resident((N_CHIPS, fs, D)),
                _resident((N_CHIPS, fs, D))]
    out_specs = [rows(D), act, act, rows(D)]
    out_shape = [sd((T, D), F32), sd((N_CHIPS, T, fs), BF16), sd((N_CHIPS, T, fs), BF16), sd((T, D), BF16)]
    if head is not None:
        in_specs += [_resident((1, D)), rows(D)]
        out_specs += [pl.BlockSpec((8, LANES), lambda i: (0, 0)), pl.BlockSpec((1, D), lambda i: (0, 0))]
        out_shape += [sd((8, LANES), F32), sd((1, D), F32)]
    return pl.pallas_call(
        body, grid=(T // tm,), name=name, in_specs=in_specs, out_specs=out_specs, out_shape=out_shape,
        compiler_params=_cparams(("arbitrary",)),
    )(x, nw, sh, sc, g, wg, wu, wd, *(head or ()))


def _ffn_bwd(dxo, x, nw, sh, sc, g, a, u, f, wg, wu, wd, seq, name):
    T, D = x.shape
    fs = wg.shape[1]
    B = T // seq
    tm = min(TOKEN_TILE // 2, seq)
    tps = seq // tm

    def body(dxo_ref, x_ref, nw_ref, sh_ref, sc_ref, g_ref, a_ref, u_ref, f_ref, wg_ref, wu_ref, wd_ref,
             dx_ref, h_ref, s_ref, df_ref, da_ref, du_ref, dsh_ref, dsc_ref, dg_ref, dnw_ref):
        i = pl.program_id(0)

        @pl.when(i % tps == 0)
        def _():
            dsh_ref[...] = jnp.zeros_like(dsh_ref)
            dsc_ref[...] = jnp.zeros_like(dsc_ref)
            dg_ref[...] = jnp.zeros_like(dg_ref)

        @pl.when(i == 0)
        def _():
            dnw_ref[...] = jnp.zeros_like(dnw_ref)

        dxo_v = dxo_ref[...]
        dfb = (0.5 * g_ref[0] * dxo_v).astype(BF16)
        dg_ref[0] += _colsum(0.5 * dxo_v * f_ref[...].astype(F32))
        dh = jnp.zeros((tm, D), F32)
        for j in range(N_CHIPS):
            ds = _dot_nt(dfb, wd_ref[j])
            av = a_ref[j].astype(F32)
            uv = u_ref[j].astype(F32)
            sig = _sigmoid(av)
            sil = av * sig
            dab = (ds * uv * (sig * (1.0 + av * (1.0 - sig)))).astype(BF16)
            dub = (ds * sil).astype(BF16)
            dh = dh + _dot(dab, wg_ref[j]) + _dot(dub, wu_ref[j])
            s_ref[j] = (sil * uv).astype(BF16)
            da_ref[j] = dab
            du_ref[j] = dub
        xv = x_ref[...]
        xh, r = _rms_stats(xv)
        nwv = nw_ref[...]
        n = xh * nwv
        scale1 = 1.0 + sc_ref[0]
        dsc_ref[0] += _colsum(dh * n)
        dsh_ref[0] += _colsum(dh)
        dx, dw_rows = _rms_bwd(dh * scale1, xh, r, nwv)
        dnw_ref[...] += _colsum(dw_rows)
        dx_ref[...] = dxo_v + dx
        h_ref[...] = (n * scale1 + sh_ref[0]).astype(BF16)
        df_ref[...] = dfb

    rows = lambda n: pl.BlockSpec((tm, n), lambda i: (i, 0))
    act = pl.BlockSpec((N_CHIPS, tm, fs), lambda i: (0, i, 0))
    perb = pl.BlockSpec((1, 1, D), lambda i: (i // tps, 0, 0))
    sd = jax.ShapeDtypeStruct
    return pl.pallas_call(
        body, grid=(T // tm,), name=name,
        in_specs=[rows(D), rows(D), _resident((1, D)), perb, perb, perb, act, act, rows(D),
                  _resident((N_CHIPS, fs, D)), _resident((N_CHIPS, fs, D)), _resident((N_CHIPS, fs, D))],
        out_specs=[rows(D), rows(D), act, rows(D), act, act, perb, perb, perb, pl.BlockSpec((1, D), lambda i: (0, 0))],
        out_shape=[sd((T, D), F32), sd((T, D), BF16), sd((N_CHIPS, T, fs), BF16), sd((T, D), BF16),
                   sd((N_CHIPS, T, fs), BF16), sd((N_CHIPS, T, fs), BF16), sd((B, 1, D), F32), sd((B, 1, D), F32),
                   sd((B, 1, D), F32), sd((1, D), F32)],
        compiler_params=_cparams(("arbitrary",)),
    )(dxo, x, nw, sh, sc, g, a, u, f, wg, wu, wd)


def _ffn_wgrad(h, s, df, da, du, after, name):
    T, D = h.shape
    fs = s.shape[2]
    tt = min(WGRAD_TOKEN_TILE, T)
    nt = T // tt

    def body(h_ref, s_ref, df_ref, da_ref, du_ref, after_ref, dgate_ref, dup_ref, ddown_ref, gate_acc, up_acc, down_acc):
        @pl.when(pl.program_id(1) == 0)
        def _():
            gate_acc[...] = jnp.zeros_like(gate_acc)
            up_acc[...] = jnp.zeros_like(up_acc)
            down_acc[...] = jnp.zeros_like(down_acc)

        hv = h_ref[...]
        gate_acc[...] += _dot_tn(da_ref[0], hv)
        up_acc[...] += _dot_tn(du_ref[0], hv)
        down_acc[...] += _dot_tn(s_ref[0], df_ref[...])

        @pl.when(pl.program_id(1) == nt - 1)
        def _():
            dgate_ref[0] = gate_acc[...].astype(BF16)
            dup_ref[0] = up_acc[...].astype(BF16)
            ddown_ref[0] = down_acc[...].astype(BF16)

    rows = pl.BlockSpec((tt, D), lambda j, t: (t, 0))
    act = pl.BlockSpec((1, tt, fs), lambda j, t: (j, t, 0))
    shard = pl.BlockSpec((1, fs, D), lambda j, t: (j, 0, 0))
    return pl.pallas_call(
        body, grid=(N_CHIPS, nt), name=name,
        in_specs=[rows, act, rows, act, act, pl.BlockSpec(memory_space=pl.ANY)],
        out_specs=[shard] * 3, out_shape=[jax.ShapeDtypeStruct((N_CHIPS, fs, D), BF16)] * 3,
        scratch_shapes=[pltpu.VMEM((fs, D), F32)] * 3,
        compiler_params=_cparams(("arbitrary", "arbitrary")),
    )(h, s, df, da, du, after)


def _mm_tn(xa, ya, tn, name):
    T, K = xa.shape
    N = ya.shape[1]
    tt = min(WGRAD_TOKEN_TILE, T)
    nt = T // tt

    def body(x_ref, y_ref, o_ref, acc_ref):
        @pl.when(pl.program_id(1) == 0)
        def _():
            acc_ref[...] = jnp.zeros_like(acc_ref)

        acc_ref[...] += _dot_tn(x_ref[...], y_ref[...])

        @pl.when(pl.program_id(1) == nt - 1)
        def _():
            o_ref[...] = acc_ref[...].astype(BF16)

    return pl.pallas_call(
        body, grid=(N // tn, nt), name=name,
        in_specs=[pl.BlockSpec((tt, K), lambda j, t: (t, 0)), pl.BlockSpec((tt, tn), lambda j, t: (t, j))],
        out_specs=pl.BlockSpec((K, tn), lambda j, t: (0, j)),
        out_shape=jax.ShapeDtypeStruct((K, N), BF16),
        scratch_shapes=[pltpu.VMEM((K, tn), F32)],
        compiler_params=_cparams(("arbitrary", "arbitrary")),
    )(xa, ya)


_PROJ_SPLITS = (0, 1024, 2560, 2944, 3200, 3328)


def _inproj_fwd(x, nw, sh, sc, win, seq):
    T, D = x.shape
    tm = min(WIDE_TOKEN_TILE, seq)
    tps = seq // tm
    widths = [b - a for a, b in zip(_PROJ_SPLITS[:-1], _PROJ_SPLITS[1:])]
    dtypes = [BF16, BF16, F32, F32, F32]

    def body(x_ref, nw_ref, sh_ref, sc_ref, w_ref, *outs):
        xh, _ = _rms_stats(x_ref[...])
        h = (xh * nw_ref[...]) * (1.0 + sc_ref[0]) + sh_ref[0]
        proj = _dot_nt(h.astype(BF16), w_ref[...])
        for o, lo, hi in zip(outs, _PROJ_SPLITS[:-1], _PROJ_SPLITS[1:]):
            o[...] = proj[:, lo:hi].astype(o.dtype)

    rows = lambda n: pl.BlockSpec((tm, n), lambda i: (i, 0))
    perb = pl.BlockSpec((1, 1, D), lambda i: (i // tps, 0, 0))
    return pl.pallas_call(
        body, grid=(T // tm,), name="inproj_fwd",
        in_specs=[rows(D), _resident((1, D)), perb, perb, _resident((D_PROJ, D))],
        out_specs=[rows(w) for w in widths],
        out_shape=[jax.ShapeDtypeStruct((T, w), dt) for w, dt in zip(widths, dtypes)],
        compiler_params=_cparams(("arbitrary",)),
    )(x, nw, sh, sc, win)


def _inproj_bwd(dx2, x, nw, sh, sc, win, dz, dxbc, dcq, dckv, ddtk_a, ddtk_b, seq):
    T, D = x.shape
    B = T // seq
    tm = min(TOKEN_TILE, seq)
    tps = seq // tm

    def body(dx2_ref, x_ref, nw_ref, sh_ref, sc_ref, w_ref, dz_ref, dxbc_ref, dcq_ref, dckv_ref, da_ref, db_ref,
             dx_ref, h_ref, dp_ref, dsh_ref, dsc_ref, dnw_ref):
        i = pl.program_id(0)

        @pl.when(i % tps == 0)
        def _():
            dsh_ref[...] = jnp.zeros_like(dsh_ref)
            dsc_ref[...] = jnp.zeros_like(dsc_ref)

        @pl.when(i == 0)
        def _():
            dnw_ref[...] = jnp.zeros_like(dnw_ref)

        dproj = jnp.concatenate(
            [dz_ref[...], dxbc_ref[...], dcq_ref[...].astype(BF16), dckv_ref[...].astype(BF16),
             (da_ref[...] + db_ref[...]).astype(BF16)], axis=1)
        dp_ref[...] = dproj
        dh = _dot(dproj, w_ref[...])
        xh, r = _rms_stats(x_ref[...])
        nwv = nw_ref[...]
        n = xh * nwv
        scale1 = 1.0 + sc_ref[0]
        dsc_ref[0] += _colsum(dh * n)
        dsh_ref[0] += _colsum(dh)
        dx, dw_rows = _rms_bwd(dh * scale1, xh, r, nwv)
        dnw_ref[...] += _colsum(dw_rows)
        dx_ref[...] = dx2_ref[...] + dx
        h_ref[...] = (n * scale1 + sh_ref[0]).astype(BF16)

    rows = lambda n: pl.BlockSpec((tm, n), lambda i: (i, 0))
    perb = pl.BlockSpec((1, 1, D), lambda i: (i // tps, 0, 0))
    sd = jax.ShapeDtypeStruct
    return pl.pallas_call(
        body, grid=(T // tm,), name="inproj_bwd",
        in_specs=[rows(D), rows(D), _resident((1, D)), perb, perb, _resident((D_PROJ, D)),
                  rows(1024), rows(D_CONV), rows(Q_LORA), rows(KV_LORA), rows(LANES), rows(LANES)],
        out_specs=[rows(D), rows(D), rows(D_PROJ), perb, perb, pl.BlockSpec((1, D), lambda i: (0, 0))],
        out_shape=[sd((T, D), F32), sd((T, D), BF16), sd((T, D_PROJ), BF16), sd((B, 1, D), F32), sd((B, 1, D), F32),
                   sd((1, D), F32)],
        compiler_params=_cparams(("arbitrary",)),
    )(dx2, x, nw, sh, sc, win, dz, dxbc, dcq, dckv, ddtk_a, ddtk_b)


SUBLANES = 8


def _shift_down(v, k):
    r = pltpu.roll(v, k, 0)
    row = lax.broadcasted_iota(jnp.int32, (SUBLANES, v.shape[1]), 0)
    return jnp.concatenate([jnp.where(row < k, 0.0, r[:SUBLANES]), r[SUBLANES:]], axis=0)


def _shift_up(v, k):
    n = v.shape[0]
    r = pltpu.roll(v, n - k, 0)
    row = lax.broadcasted_iota(jnp.int32, (SUBLANES, v.shape[1]), 0)
    return jnp.concatenate([r[:n - SUBLANES], jnp.where(row >= SUBLANES - k, 0.0, r[n - SUBLANES:])], axis=0)


def _conv_pre(xv, w_ref, b_ref):
    pre = b_ref[...] + w_ref[CONV_WIDTH - 1:CONV_WIDTH, :] * xv
    for k in range(1, CONV_WIDTH):
        pre = pre + w_ref[CONV_WIDTH - 1 - k:CONV_WIDTH - k, :] * _shift_down(xv, k)
    return pre


def _conv_fwd(xraw, cw, cb):
    B, S, C = xraw.shape

    def body(x_ref, w_ref, b_ref, o_ref):
        pre = _conv_pre(x_ref[0].astype(F32), w_ref, b_ref)
        o_ref[0] = (pre * _sigmoid(pre)).astype(BF16)

    blk = pl.BlockSpec((1, S, LANES), lambda b, j: (b, 0, j))
    return pl.pallas_call(
        body, grid=(B, C // LANES), name="conv_fwd",
        in_specs=[blk, pl.BlockSpec((CONV_WIDTH, LANES), lambda b, j: (0, j)), pl.BlockSpec((1, LANES), lambda b, j: (0, j))],
        out_specs=blk, out_shape=jax.ShapeDtypeStruct((B, S, C), BF16),
        compiler_params=_cparams(("arbitrary", "arbitrary")),
    )(xraw, cw, cb)


def _conv_bwd(dout, xraw, cw, cb):
    B, S, C = xraw.shape

    def body(d_ref, x_ref, w_ref, b_ref, dx_ref, dw_ref, db_ref):
        @pl.when(pl.program_id(1) == 0)
        def _():
            dw_ref[...] = jnp.zeros_like(dw_ref)
            db_ref[...] = jnp.zeros_like(db_ref)

        xv = x_ref[0].astype(F32)
        pre = _conv_pre(xv, w_ref, b_ref)
        sig = _sigmoid(pre)
        dpre = d_ref[0].astype(F32) * (sig * (1.0 + pre * (1.0 - sig)))
        dx = w_ref[CONV_WIDTH - 1:CONV_WIDTH, :] * dpre
        for k in range(1, CONV_WIDTH):
            dx = dx + w_ref[CONV_WIDTH - 1 - k:CONV_WIDTH - k, :] * _shift_up(dpre, k)
        dx_ref[0] = dx.astype(BF16)
        db_ref[...] += _colsum(dpre)
        dws = [_colsum(dpre * (xv if k == 0 else _shift_down(xv, k))) for k in range(CONV_WIDTH - 1, -1, -1)]
        dw_ref[...] += jnp.concatenate(dws, axis=0)

    blk = pl.BlockSpec((1, S, LANES), lambda j, b: (b, 0, j))
    wspec = pl.BlockSpec((CONV_WIDTH, LANES), lambda j, b: (0, j))
    bspec = pl.BlockSpec((1, LANES), lambda j, b: (0, j))
    return pl.pallas_call(
        body, grid=(C // LANES, B), name="conv_bwd",
        in_specs=[blk, blk, wspec, bspec], out_specs=[blk, wspec, bspec],
        out_shape=[jax.ShapeDtypeStruct((B, S, C), BF16), jax.ShapeDtypeStruct((CONV_WIDTH, C), F32),
                   jax.ShapeDtypeStruct((1, C), F32)],
        compiler_params=_cparams(("arbitrary", "arbitrary")),
    )(dout, xraw, cw, cb)


def _softplus(x):
    return jnp.maximum(x, 0.0) + jnp.log(1.0 + jnp.exp(-jnp.abs(x)))


def _ssd_common(xbc_ref, dtk_ref, dtb_ref, alog_ref, e_ref):
    L = CHUNK
    xbc = xbc_ref[0]
    xs = xbc[:, :D_SSD].astype(F32)
    bm = xbc[:, D_SSD:D_SSD + 256]
    cm = xbc[:, D_SSD + 256:D_SSD + 512]
    head = lax.broadcasted_iota(jnp.int32, (1, LANES), 1) < SSD_HEADS
    a128 = jnp.where(head, -jnp.exp(alog_ref[...]), 0.0)
    pre = dtk_ref[0] + dtb_ref[...]
    dt = _softplus(pre)
    dA = dt * a128
    row = lax.broadcasted_iota(jnp.int32, (L, L), 0)
    col = lax.broadcasted_iota(jnp.int32, (L, L), 1)
    causal = col <= row
    tri = causal.astype(F32)
    triT = (row <= col).astype(F32)
    tri = causal.astype(BF16)
    triT = (row <= col).astype(BF16)
    dA3 = _split3(dA)
    acum = _sum3(lambda part: _dot(tri, part), dA3)
    acumT = _sum3(lambda part: _dot_tn(part, triT), dA3)
    E = e_ref[...]
    acum_f = _spread(acum, E)
    dt_f = _spread(dt, E)
    e_f = jnp.exp(acum_f)
    w_f = jnp.exp(acum_f[L - 1:L, :] - acum_f)
    xt = xs * dt_f
    return dict(xs=xs, bm=bm, cm=cm, a128=a128, pre=pre, dt=dt, causal=causal, tri=tri, triT=triT, acum=acum,
                acumT=acumT, E=E, dt_f=dt_f, e_f=e_f, w_f=w_f, xt=xt, head=head)


def _split3(x):
    p1 = x.astype(BF16)
    r1 = x - p1.astype(F32)
    p2 = r1.astype(BF16)
    return p1, p2, (r1 - p2.astype(F32)).astype(BF16)


def _sum3(mm, parts):
    return (mm(parts[0]) + mm(parts[1])) + mm(parts[2])


def _spread(v, e):
    return _sum3(lambda part: _dot(part, e), _split3(v))


def _gather_heads(v, e):
    return _sum3(lambda part: _dot_nt(part, e), _split3(v))


def _head_mask(k):
    lane = lax.broadcasted_iota(jnp.int32, (CHUNK, LANES), 1)
    return (lane >= SSD_HEAD_DIM) if k == 1 else (lane < SSD_HEAD_DIM)


def _pair_decay(alast, h0):
    row = lax.broadcasted_iota(jnp.int32, (2 * SSD_HEAD_DIM, SSD_STATE), 0)
    return jnp.exp(jnp.where(row < SSD_HEAD_DIM, alast[:, h0:h0 + 1], alast[:, h0 + 1:h0 + 2]))


def _decay_matrix(q, h):
    seg = q["acum"][:, h:h + 1] - q["acumT"][h:h + 1, :]
    return jnp.exp(jnp.where(q["causal"], seg, -1e30))


def _gated_norm(y, zz, nw):
    sig = _sigmoid(zz)
    sil = zz * sig
    yg = y * sil
    half = D_SSD // SSD_GROUPS
    parts = []
    for g in range(SSD_GROUPS):
        xh, r = _rms_stats(yg[:, g * half:(g + 1) * half])
        parts.append((xh, r))
    return sig, sil, parts


def _ssd_fwd(xbc, dtk, z, dtb, alog, dsk, nw, expand):
    B, S, _ = xbc.shape
    L = CHUNK
    nc = S // L

    def body(xbc_ref, dtk_ref, z_ref, dtb_ref, alog_ref, dsk_ref, nw_ref, e_ref, y_ref, ys_ref, prev_ref, st_ref):
        @pl.when(pl.program_id(0) == 0)
        def _():
            st_ref[...] = jnp.zeros_like(st_ref)

        for b in range(B):
            one = lambda ref: ref.at[pl.ds(b, 1)]
            sequence_step(one(xbc_ref), one(dtk_ref), one(z_ref), dtb_ref, alog_ref, dsk_ref, nw_ref, e_ref, one(y_ref),
                          one(ys_ref), one(prev_ref), st_ref.at[b])

    def sequence_step(xbc_ref, dtk_ref, z_ref, dtb_ref, alog_ref, dsk_ref, nw_ref, e_ref, y_ref, ys_ref, prev_ref, st_ref):
        q = _ssd_common(xbc_ref, dtk_ref, dtb_ref, alog_ref, e_ref)
        xtb = q["xt"].astype(BF16)
        xwb = (q["xt"] * q["w_f"]).astype(BF16)
        alast = q["acum"][L - 1:L, :]
        ys = []
        for g in range(SSD_GROUPS):
            bg = q["bm"][:, g * 128:(g + 1) * 128]
            cg = q["cm"][:, g * 128:(g + 1) * 128]
            G = _dot_nt(cg, bg)
            for pr in range(SSD_HEADS // SSD_GROUPS // 2):
                h0 = g * 8 + 2 * pr
                lo = h0 * SSD_HEAD_DIM
                xt_p = xtb[:, lo:lo + 128]
                ydiag = jnp.zeros((L, LANES), F32)
                for k in range(2):
                    M = (G * _decay_matrix(q, h0 + k)).astype(BF16)
                    ydiag = ydiag + _dot(M, jnp.where(_head_mask(k), xt_p, jnp.zeros_like(xt_p)))
                hp = st_ref[lo:lo + 128, :]
                prev_ref[0, 0, lo:lo + 128, :] = hp.astype(BF16)
                zoff = _dot_nt(cg, hp.astype(BF16))
                ys.append(ydiag + zoff * q["e_f"][:, lo:lo + 128])
                st_ref[lo:lo + 128, :] = _pair_decay(alast, h0) * hp + _dot_tn(xwb[:, lo:lo + 128], bg)
        y = jnp.concatenate(ys, axis=1) + dsk_ref[...] * q["xs"]
        y_ref[0] = y.astype(BF16)
        _, _, parts = _gated_norm(y, z_ref[0].astype(F32), nw_ref[...])
        half = D_SSD // SSD_GROUPS
        ys_ref[0] = jnp.concatenate(
            [xh * nw_ref[:, g * half:(g + 1) * half] for g, (xh, _) in enumerate(parts)], axis=1).astype(BF16)

    chunk = lambda n: pl.BlockSpec((B, L, n), lambda c: (0, c, 0))
    vec = pl.BlockSpec((1, LANES), lambda c: (0, 0))
    return pl.pallas_call(
        body, grid=(nc,), name="ssd_fwd",
        in_specs=[chunk(D_CONV), chunk(LANES), chunk(D_SSD), vec, vec, pl.BlockSpec((1, D_SSD), lambda c: (0, 0)),
                  pl.BlockSpec((1, D_SSD), lambda c: (0, 0)), pl.BlockSpec((LANES, D_SSD), lambda c: (0, 0))],
        out_specs=[chunk(D_SSD), chunk(D_SSD), pl.BlockSpec((B, 1, D_SSD, SSD_STATE), lambda c: (0, c, 0, 0))],
        out_shape=[jax.ShapeDtypeStruct((B, S, D_SSD), BF16), jax.ShapeDtypeStruct((B, S, D_SSD), BF16),
                   jax.ShapeDtypeStruct((B, nc, D_SSD, SSD_STATE), BF16)],
        scratch_shapes=[pltpu.VMEM((B, D_SSD, SSD_STATE), F32)],
        compiler_params=_cparams(("arbitrary",)),
    )(xbc, dtk, z, dtb, alog, dsk, nw, expand)


def _ssd_bwd(xbc, dtk, z, y, prev, dys, dtb, alog, dsk, nw, expand):
    B, S, _ = xbc.shape
    L = CHUNK
    nc = S // L
    half = D_SSD // SSD_GROUPS

    def body(xbc_ref, dtk_ref, z_ref, y_ref, prev_ref, dys_ref, dtb_ref, alog_ref, dsk_ref, nw_ref, e_ref,
             dxbc_ref, ddtk_ref, dz_ref, dnw_ref, dvec_ref, dh_ref, dskc_ref):
        @pl.when(pl.program_id(0) == 0)
        def _():
            dnw_ref[...] = jnp.zeros_like(dnw_ref)
            dvec_ref[...] = jnp.zeros_like(dvec_ref)
            dskc_ref[...] = jnp.zeros_like(dskc_ref)
            dh_ref[...] = jnp.zeros_like(dh_ref)

        for b in range(B):
            one = lambda ref: ref.at[pl.ds(b, 1)]
            sequence_step(one(xbc_ref), one(dtk_ref), one(z_ref), one(y_ref), one(prev_ref), one(dys_ref), dtb_ref, alog_ref,
                          dsk_ref, nw_ref, e_ref, one(dxbc_ref), one(ddtk_ref), one(dz_ref), dnw_ref, dvec_ref, dh_ref.at[b],
                          dskc_ref)

        @pl.when(pl.program_id(0) == nc - 1)
        def _():
            dvec_ref[2:3, :] = _gather_heads(jnp.broadcast_to(dskc_ref[...], (8, D_SSD)), e_ref[...])[0:1, :]

    def sequence_step(xbc_ref, dtk_ref, z_ref, y_ref, prev_ref, dys_ref, dtb_ref, alog_ref, dsk_ref, nw_ref, e_ref,
                      dxbc_ref, ddtk_ref, dz_ref, dnw_ref, dvec_ref, dh_ref, dskc_ref):
        q = _ssd_common(xbc_ref, dtk_ref, dtb_ref, alog_ref, e_ref)
        E = q["E"]
        xs = q["xs"]
        yv = y_ref[0].astype(F32)
        zz = z_ref[0].astype(F32)
        sig, sil, parts = _gated_norm(yv, zz, nw_ref[...])
        dn = dys_ref[0].astype(F32)
        dyg, dnw_rows = [], []
        for g, (xh, r) in enumerate(parts):
            dpart, dw_rows = _rms_bwd(dn[:, g * half:(g + 1) * half], xh, r, nw_ref[:, g * half:(g + 1) * half])
            dyg.append(dpart)
            dnw_rows.append(dw_rows)
        dyg = jnp.concatenate(dyg, axis=1)
        dnw_ref[...] += _colsum(jnp.concatenate(dnw_rows, axis=1))
        dY = dyg * sil
        dz_ref[0] = (dyg * yv * (sig * (1.0 + zz * (1.0 - sig)))).astype(BF16)
        dsk_f = dsk_ref[...]
        dskc_ref[...] += _colsum(dY * xs)
        dYb = dY.astype(BF16)
        xtb = q["xt"].astype(BF16)
        xwb = (q["xt"] * q["w_f"]).astype(BF16)
        acum = q["acum"]
        alast = acum[L - 1:L, :]
        lane_id = lax.broadcasted_iota(jnp.int32, (L, LANES), 1)
        sub_id = lax.broadcasted_iota(jnp.int32, (LANES, L), 0)
        lane_row = lax.broadcasted_iota(jnp.int32, (1, LANES), 1)
        da_rows = jnp.zeros((L, LANES), F32)
        daT = jnp.zeros((LANES, L), F32)
        dxt, prod_off, prod_st, dbs, dcs = [], [], [], [], []
        hsum_row = jnp.zeros((1, LANES), F32)
        for g in range(SSD_GROUPS):
            bg = q["bm"][:, g * 128:(g + 1) * 128]
            cg = q["cm"][:, g * 128:(g + 1) * 128]
            G = _dot_nt(cg, bg)
            dG = jnp.zeros((L, L), F32)
            dcg = jnp.zeros((L, SSD_STATE), F32)
            dbg = jnp.zeros((L, SSD_STATE), F32)
            for pr in range(SSD_HEADS // SSD_GROUPS // 2):
                h0 = g * 8 + 2 * pr
                lo = h0 * SSD_HEAD_DIM
                cols = slice(lo, lo + 128)
                dY_p = dYb[:, cols]
                xt_p = xtb[:, cols]
                dxt_p = jnp.zeros((L, LANES), F32)
                for k in range(2):
                    h = h0 + k
                    Lm = _decay_matrix(q, h)
                    Mf = G * Lm
                    dYk = jnp.where(_head_mask(k), dY_p, jnp.zeros_like(dY_p))
                    dM = _dot_nt(dYk, xt_p)
                    dxt_p = dxt_p + _dot_tn(Mf.astype(BF16), dYk)
                    dG = dG + dM * Lm
                    Q = dM * Mf
                    da_rows = da_rows + jnp.where(lane_id == h, jnp.sum(Q, axis=1, keepdims=True), 0.0)
                    daT = daT + jnp.where(sub_id == h, jnp.sum(Q, axis=0, keepdims=True), 0.0)
                hpb = prev_ref[0, 0, lo:lo + 128, :]
                hp = hpb.astype(F32)
                zoff = _dot_nt(cg, hpb)
                e_p = q["e_f"][:, cols]
                dY_pf = dY[:, cols]
                dZb = (dY_pf * e_p).astype(BF16)
                dcg = dcg + _dot(dZb, hpb)
                dhp_off = _dot_tn(dZb, cg)
                prod_off.append(dY_pf * zoff * e_p)
                dS = dh_ref[lo:lo + 128, :]
                dSb = dS.astype(BF16)
                U = _dot_nt(bg, dSb)
                dxt_p = dxt_p + U * q["w_f"][:, cols]
                dbg = dbg + _dot(xwb[:, cols], dSb)
                prod_st.append(q["xt"][:, cols] * U)
                dh_ref[lo:lo + 128, :] = _pair_decay(alast, h0) * dS + dhp_off
                dsh = dS * hp
                for k in range(2):
                    total = jnp.sum(dsh[k * SSD_HEAD_DIM:(k + 1) * SSD_HEAD_DIM, :], axis=(0, 1), keepdims=True)
                    hsum_row = hsum_row + jnp.where(lane_row == h0 + k, total, 0.0)
                dxt.append(dxt_p)
            dGb = dG.astype(BF16)
            dcs.append(dcg + _dot(dGb, bg))
            dbs.append(dbg + _dot_tn(dGb, cg))
        dxt = jnp.concatenate(dxt, axis=1)
        da_rows = da_rows + _gather_heads(jnp.concatenate(prod_off, axis=1), E)
        dww = _gather_heads(jnp.concatenate(prod_st, axis=1), E) * jnp.exp(alast - acum)
        da_rows = da_rows - dww
        dlast = _colsum(dww) + jnp.exp(alast) * hsum_row
        triT = q["triT"]
        ddA = (_sum3(lambda part: _dot(triT, part), _split3(da_rows))
               - _sum3(lambda part: _dot_nt(triT, part), _split3(daT)) + dlast)
        ddA = jnp.where(q["head"], ddA, 0.0)
        ddt = ddA * q["a128"] + _gather_heads(dxt * xs, E)
        ddt_raw = jnp.where(q["head"], ddt * _sigmoid(q["pre"]), 0.0)
        ddtk_ref[0] = ddt_raw
        dxs = dxt * q["dt_f"] + dsk_f * dY
        dxbc_ref[0] = jnp.concatenate([dxs] + dbs + dcs, axis=1).astype(BF16)
        dvec_ref[0:1, :] += _colsum(ddt_raw)
        dvec_ref[1:2, :] += _colsum(ddA * q["dt"]) * q["a128"]

    rev = lambda n: pl.BlockSpec((B, L, n), lambda c: (0, nc - 1 - c, 0))
    vec = pl.BlockSpec((1, LANES), lambda c: (0, 0))
    sd = jax.ShapeDtypeStruct
    return pl.pallas_call(
        body, grid=(nc,), name="ssd_bwd",
        in_specs=[rev(D_CONV), rev(LANES), rev(D_SSD), rev(D_SSD),
                  pl.BlockSpec((B, 1, D_SSD, SSD_STATE), lambda c: (0, nc - 1 - c, 0, 0)), rev(D_SSD), vec, vec,
                  pl.BlockSpec((1, D_SSD), lambda c: (0, 0)),
                  pl.BlockSpec((1, D_SSD), lambda c: (0, 0)), pl.BlockSpec((LANES, D_SSD), lambda c: (0, 0))],
        out_specs=[rev(D_CONV), rev(LANES), rev(D_SSD), pl.BlockSpec((1, D_SSD), lambda c: (0, 0)),
                   pl.BlockSpec((8, LANES), lambda c: (0, 0))],
        out_shape=[sd((B, S, D_CONV), BF16), sd((B, S, LANES), F32), sd((B, S, D_SSD), BF16), sd((1, D_SSD), F32),
                   sd((8, LANES), F32)],
        scratch_shapes=[pltpu.VMEM((B, D_SSD, SSD_STATE), F32), pltpu.VMEM((1, D_SSD), F32)],
        compiler_params=_cparams(("arbitrary",)),
    )(xbc, dtk, z, y, prev, dys, dtb, alog, dsk, nw, expand)


def _rope_tables(pos_ref, invf_ref, place_ref):
    ang = invf_ref[...] * pos_ref[0].astype(F32)
    place = place_ref[...]
    cosf = 1.0 + _sum3(lambda part: _dot_tn(part, place), _split3(jnp.cos(ang) - 1.0))
    sinf = _sum3(lambda part: _dot_tn(part, place), _split3(jnp.sin(ang)))
    return cosf, sinf


def _rot(u):
    lane = lax.broadcasted_iota(jnp.int32, u.shape, 1)
    first = (lane >= QK_NOPE) & (lane < QK_NOPE + QK_ROPE // 2)
    second = (lane >= QK_NOPE + QK_ROPE // 2) & (lane < QK_DIM)
    return jnp.where(first, -pltpu.roll(u, LANES - QK_ROPE // 2, 1), jnp.where(second, pltpu.roll(u, QK_ROPE // 2, 1), 0.0))


def _rope_lanes(shape):
    lane = lax.broadcasted_iota(jnp.int32, shape, 1)
    return (lane >= QK_NOPE) & (lane < QK_DIM)


def _mla_prep(cq, ckv, dtk, pos, qw, kvw, wuq, wukv, invf, place):
    T = cq.shape[0]
    tm = min(WIDE_TOKEN_TILE, T)
    scale = 1.0 / math.sqrt(QK_DIM)
    HW = MLA_HEADS * HEAD_LANES

    def body(cq_ref, ckv_ref, dtk_ref, pos_ref, qw_ref, kvw_ref, wuq_ref, wukv_ref, invf_ref, place_ref, q_ref, k_ref, v_ref,
             cos_ref, sin_ref):
        xh, _ = _rms_stats(cq_ref[...])
        qv = _dot((xh * qw_ref[...]).astype(BF16), wuq_ref[...])
        xh, _ = _rms_stats(ckv_ref[...])
        kv = _dot((xh * kvw_ref[...]).astype(BF16), wukv_ref[...])
        cosf, sinf = _rope_tables(pos_ref, invf_ref, place_ref)
        cos_ref[...] = cosf
        sin_ref[...] = sinf
        rope = lambda u: u * cosf + _rot(u) * sinf
        dtkv = dtk_ref[...]
        kr = rope(jnp.where(_rope_lanes(dtkv.shape), dtkv, 0.0))
        for h in range(MLA_HEADS):
            cols = slice(h * HEAD_LANES, (h + 1) * HEAD_LANES)
            q_ref[:, cols] = (rope(qv[:, cols]) * scale).astype(BF16)
            k_ref[:, cols] = (kv[:, cols] + kr).astype(BF16)
        v_ref[...] = kv[:, HW:].astype(BF16)

    rows = lambda n: pl.BlockSpec((tm, n), lambda i: (i, 0))
    return pl.pallas_call(
        body, grid=(T // tm,), name="mla_prep",
        in_specs=[rows(Q_LORA), rows(KV_LORA), rows(LANES), pl.BlockSpec((1, 1, tm), lambda i: (i, 0, 0)),
                  _resident((1, Q_LORA)), _resident((1, KV_LORA)), _resident((Q_LORA, HW)), _resident((KV_LORA, 2 * HW)),
                  _resident((QK_ROPE // 2, 1)), _resident((QK_ROPE // 2, LANES))],
        out_specs=[rows(HW), rows(HW), rows(HW), rows(LANES), rows(LANES)],
        out_shape=[jax.ShapeDtypeStruct((T, HW), BF16)] * 3 + [jax.ShapeDtypeStruct((T, LANES), F32)] * 2,
        compiler_params=_cparams(("arbitrary",)),
    )(cq, ckv, dtk, pos.reshape(T // tm, 1, tm), qw, kvw, wuq, wukv, invf, place)


def _mla_prep_bwd(dq, dk, dv, cq, ckv, cos_t, sin_t, qw, kvw, wuq, wukv):
    T = cq.shape[0]
    tm = min(WIDE_TOKEN_TILE, T)
    scale = 1.0 / math.sqrt(QK_DIM)
    HW = MLA_HEADS * HEAD_LANES

    def body(dq_ref, dk_ref, dv_ref, cq_ref, ckv_ref, cos_ref, sin_ref, qw_ref, kvw_ref, wuq_ref, wukv_ref,
             dcq_ref, dckv_ref, ddtk_ref, qn_ref, kvn_ref, dqo_ref, dkvo_ref, dqw_ref, dkvw_ref):
        @pl.when(pl.program_id(0) == 0)
        def _():
            dqw_ref[...] = jnp.zeros_like(dqw_ref)
            dkvw_ref[...] = jnp.zeros_like(dkvw_ref)

        cosf, sinf = cos_ref[...], sin_ref[...]
        unrope = lambda d: d * cosf - _rot(d * sinf)
        dkr = jnp.zeros((tm, LANES), F32)
        nope = lax.broadcasted_iota(jnp.int32, (tm, LANES), 1) < QK_NOPE
        for h in range(MLA_HEADS):
            cols = slice(h * HEAD_LANES, (h + 1) * HEAD_LANES)
            dqo_ref[:, cols] = unrope(dq_ref[:, cols].astype(F32) * scale).astype(BF16)
            dkh = dk_ref[:, cols].astype(F32)
            dkr = dkr + jnp.where(_rope_lanes(dkh.shape), dkh, 0.0)
            dkvo_ref[:, cols] = jnp.where(nope, dkh, 0.0).astype(BF16)
        dkvo_ref[:, HW:] = dv_ref[...].astype(BF16)
        ddtk_ref[...] = unrope(dkr)
        xh, r = _rms_stats(cq_ref[...])
        qn_ref[...] = (xh * qw_ref[...]).astype(BF16)
        dx, dw_rows = _rms_bwd(_dot_nt(dqo_ref[...], wuq_ref[...]), xh, r, qw_ref[...])
        dcq_ref[...] = dx
        dqw_ref[...] += _colsum(dw_rows)
        xh, r = _rms_stats(ckv_ref[...])
        kvn_ref[...] = (xh * kvw_ref[...]).astype(BF16)
        dx, dw_rows = _rms_bwd(_dot_nt(dkvo_ref[...], wukv_ref[...]), xh, r, kvw_ref[...])
        dckv_ref[...] = dx
        dkvw_ref[...] += _colsum(dw_rows)

    rows = lambda n: pl.BlockSpec((tm, n), lambda i: (i, 0))
    sd = jax.ShapeDtypeStruct
    return pl.pallas_call(
        body, grid=(T // tm,), name="mla_prep_bwd",
        in_specs=[rows(HW), rows(HW), rows(HW), rows(Q_LORA), rows(KV_LORA), rows(LANES), rows(LANES), _resident((1, Q_LORA)),
                  _resident((1, KV_LORA)), _resident((Q_LORA, HW)), _resident((KV_LORA, 2 * HW))],
        out_specs=[rows(Q_LORA), rows(KV_LORA), rows(LANES), rows(Q_LORA), rows(KV_LORA), rows(HW), rows(2 * HW),
                   pl.BlockSpec((1, Q_LORA), lambda i: (0, 0)), pl.BlockSpec((1, KV_LORA), lambda i: (0, 0))],
        out_shape=[sd((T, Q_LORA), F32), sd((T, KV_LORA), F32), sd((T, LANES), F32), sd((T, Q_LORA), BF16),
                   sd((T, KV_LORA), BF16), sd((T, HW), BF16), sd((T, 2 * HW), BF16), sd((1, Q_LORA), F32),
                   sd((1, KV_LORA), F32)],
        compiler_params=_cparams(("arbitrary",)),
    )(dq, dk, dv, cq, ckv, cos_t, sin_t, qw, kvw, wuq, wukv)


def _causal_mask(t):
    row = lax.broadcasted_iota(jnp.int32, (t, t), 0)
    col = lax.broadcasted_iota(jnp.int32, (t, t), 1)
    return col <= row


def _attn_fwd(q, k, v):
    B, S, HW = q.shape
    H = HW // HEAD_LANES
    t = min(ATTN_FWD_Q_TILE, S)
    tk = min(ATTN_FWD_KV_TILE, t)
    nq = S // t
    per = t // tk

    pair = 4
    pw = pair * HEAD_LANES

    def body(q_ref, k_ref, v_ref, o_ref, lse_ref):
        qi = pl.program_id(2)
        lanes = [slice(hh * HEAD_LANES, (hh + 1) * HEAD_LANES) for hh in range(pair)]
        qs = [q_ref[0, :, cols] for cols in lanes]

        def step(j, carry, diag):
            sl = pl.ds(pl.multiple_of(j * tk, tk), tk)
            out = []
            for qv, cols, (m, l, acc) in zip(qs, lanes, carry):
                s = _dot_nt(qv, k_ref[0, sl, cols])
                if diag is not None:
                    row = lax.broadcasted_iota(jnp.int32, (t, tk), 0)
                    col = lax.broadcasted_iota(jnp.int32, (t, tk), 1)
                    s = jnp.where(col + diag * tk <= row, s, -1e30)
                m_new = jnp.maximum(m, jnp.max(s, axis=-1, keepdims=True))
                alpha = jnp.exp(m - m_new)
                p = jnp.exp(s - m_new)
                l = alpha * l + jnp.sum(p, axis=-1, keepdims=True)
                acc = alpha * acc + _dot(p.astype(BF16), v_ref[0, sl, cols])
                out.append((m_new, l, acc))
            return tuple(out)

        init = tuple((jnp.full((t, 1), -1e30, F32), jnp.zeros((t, 1), F32), jnp.zeros((t, HEAD_LANES), F32))
                     for _ in range(pair))
        carry = lax.fori_loop(0, qi * per, lambda j, c: step(j, c, None), init)
        for d in range(per):
            carry = step(qi * per + d, carry, d)
        for hh, (m, l, acc) in enumerate(carry):
            o_ref[0, :, lanes[hh]] = (acc / l).astype(BF16)
            lse_ref[0, hh] = m + jnp.log(l)

    return pl.pallas_call(
        body, grid=(B, H // pair, nq), name="attn_fwd",
        in_specs=[pl.BlockSpec((1, t, pw), lambda b, h, i: (b, i, h)),
                  pl.BlockSpec((1, S, pw), lambda b, h, i: (b, 0, h)),
                  pl.BlockSpec((1, S, pw), lambda b, h, i: (b, 0, h))],
        out_specs=[pl.BlockSpec((1, t, pw), lambda b, h, i: (b, i, h)),
                   pl.BlockSpec((1, pair, t, 1), lambda b, h, i: (b, h, i, 0))],
        out_shape=[jax.ShapeDtypeStruct((B, S, HW), BF16), jax.ShapeDtypeStruct((B, H, S, 1), F32)],
        compiler_params=_cparams(("arbitrary", "arbitrary", "arbitrary")),
    )(q, k, v)


def _attn_bwd(q, k, v, o, do, lse):
    B, S, HW = q.shape
    H = HW // HEAD_LANES
    t = min(ATTN_BWD_TILE, S)
    nq = S // t

    pair = 2
    pw = pair * HEAD_LANES

    def body(q_ref, k_ref, v_ref, o_ref, do_ref, lse_ref, dq_out_ref, dk_ref, dv_ref, dq_ref):
        j = pl.program_id(2)

        @pl.when(j == 0)
        def _():
            dq_ref[...] = jnp.zeros_like(dq_ref)

        lanes = [slice(hh * HEAD_LANES, (hh + 1) * HEAD_LANES) for hh in range(pair)]

        def step(i, carry, masked):
            sl = pl.ds(pl.multiple_of(i * t, t), t)
            out = []
            for hh, (cols, (dk, dv)) in enumerate(zip(lanes, carry)):
                kj = k_ref[0, :, cols]
                qi = q_ref[0, sl, cols]
                doi = do_ref[0, sl, cols]
                s = _dot_nt(qi, kj)
                if masked:
                    s = jnp.where(_causal_mask(t), s, -1e30)
                p = jnp.exp(s - lse_ref[0, hh, sl, :])
                dv = dv + _dot_tn(p.astype(BF16), doi)
                dp = _dot_nt(doi, v_ref[0, :, cols])
                delta = jnp.sum(doi.astype(F32) * o_ref[0, sl, cols].astype(F32), axis=-1, keepdims=True)
                dsb = (p * (dp - delta)).astype(BF16)
                dk = dk + _dot_tn(dsb, qi)
                dq_ref[sl, cols] += _dot(dsb, kj)
                out.append((dk, dv))
            return tuple(out)

        zero = jnp.zeros((t, HEAD_LANES), F32)
        carry = step(j, ((zero, zero),) * pair, True)
        carry = lax.fori_loop(j + 1, nq, lambda i, c: step(i, c, False), carry)
        for cols, (dk, dv) in zip(lanes, carry):
            dk_ref[0, :, cols] = dk.astype(BF16)
            dv_ref[0, :, cols] = dv.astype(BF16)

        @pl.when(j == nq - 1)
        def _():
            dq_out_ref[0] = dq_ref[...].astype(BF16)

    full = pl.BlockSpec((1, S, pw), lambda b, h, j: (b, 0, h))
    tile = pl.BlockSpec((1, t, pw), lambda b, h, j: (b, j, h))
    sd = jax.ShapeDtypeStruct
    return pl.pallas_call(
        body, grid=(B, H // pair, nq), name="attn_bwd",
        in_specs=[full, tile, tile, full, full, pl.BlockSpec((1, pair, S, 1), lambda b, h, j: (b, h, 0, 0))],
        out_specs=[full, tile, tile],
        out_shape=[sd((B, S, HW), BF16), sd((B, S, HW), BF16), sd((B, S, HW), BF16)],
        scratch_shapes=[pltpu.VMEM((S, pw), F32)],
        compiler_params=_cparams(("arbitrary", "arbitrary", "arbitrary")),
    )(q, k, v, o, do, lse)


def _mix_out(x1, yssd, o, mw, wout, g, seq):
    T, D = x1.shape
    tm = min(WIDE_TOKEN_TILE, seq)
    tps = seq // tm

    def body(x_ref, ys_ref, o_ref, mw_ref, w_ref, g_ref, xo_ref, m_ref, yc_ref):
        xh, _ = _rms_stats(o_ref[...].astype(F32))
        ycat = jnp.concatenate([ys_ref[...], (xh * mw_ref[...]).astype(BF16)], axis=1)
        m = _dot(ycat, w_ref[...])
        xo_ref[...] = x_ref[...] + g_ref[0] * m
        m_ref[...] = m.astype(BF16)
        yc_ref[...] = ycat

    rows = lambda n: pl.BlockSpec((tm, n), lambda i: (i, 0))
    perb = pl.BlockSpec((1, 1, D), lambda i: (i // tps, 0, 0))
    sd = jax.ShapeDtypeStruct
    return pl.pallas_call(
        body, grid=(T // tm,), name="mix_out",
        in_specs=[rows(D), rows(D_SSD), rows(D_MLA), _resident((1, D_MLA)), _resident((D_SSD + D_MLA, D)), perb],
        out_specs=[rows(D), rows(D), rows(D_SSD + D_MLA)],
        out_shape=[sd((T, D), F32), sd((T, D), BF16), sd((T, D_SSD + D_MLA), BF16)],
        compiler_params=_cparams(("arbitrary",)),
    )(x1, yssd, o, mw, wout, g)


def _mix_out_bwd(dx2, m, o, mw, wout, g, seq):
    T, D = dx2.shape
    B = T // seq
    tm = min(WIDE_TOKEN_TILE, seq)
    tps = seq // tm

    def body(dx_ref, m_ref, o_ref, mw_ref, w_ref, g_ref, dys_ref, do_ref, dm_ref, dg_ref, dmw_ref):
        i = pl.program_id(0)

        @pl.when(i % tps == 0)
        def _():
            dg_ref[...] = jnp.zeros_like(dg_ref)

        @pl.when(i == 0)
        def _():
            dmw_ref[...] = jnp.zeros_like(dmw_ref)

        dxv = dx_ref[...]
        dg_ref[0] += _colsum(dxv * m_ref[...].astype(F32))
        dmb = (g_ref[0] * dxv).astype(BF16)
        dm_ref[...] = dmb
        dycat = _dot_nt(dmb, w_ref[...])
        dys_ref[...] = dycat[:, :D_SSD].astype(BF16)
        xh, r = _rms_stats(o_ref[...].astype(F32))
        dx, dw_rows = _rms_bwd(dycat[:, D_SSD:], xh, r, mw_ref[...])
        do_ref[...] = dx.astype(BF16)
        dmw_ref[...] += _colsum(dw_rows)

    rows = lambda n: pl.BlockSpec((tm, n), lambda i: (i, 0))
    perb = pl.BlockSpec((1, 1, D), lambda i: (i // tps, 0, 0))
    sd = jax.ShapeDtypeStruct
    return pl.pallas_call(
        body, grid=(T // tm,), name="mix_out_bwd",
        in_specs=[rows(D), rows(D), rows(D_MLA), _resident((1, D_MLA)), _resident((D_SSD + D_MLA, D)), perb],
        out_specs=[rows(D_SSD), rows(D_MLA), rows(D), perb, pl.BlockSpec((1, D_MLA), lambda i: (0, 0))],
        out_shape=[sd((T, D_SSD), BF16), sd((T, D_MLA), BF16), sd((T, D), BF16), sd((B, 1, D), F32), sd((1, D_MLA), F32)],
        compiler_params=_cparams(("arbitrary",)),
    )(dx2, m, o, mw, wout, g)


def _win_to_kernel(w):
    z0 = jnp.zeros((48, w.shape[1]), w.dtype)
    z1 = jnp.zeros((32, w.shape[1]), w.dtype)
    return jnp.concatenate([w[:2560], w[2576:3216], w[2560:2576], z0, w[3216:3248], z1], axis=0)


def _win_from_kernel(g):
    return jnp.concatenate([g[:2560], g[3200:3216], g[2560:3200], g[3264:3296]], axis=0)


def _wuq_to_kernel(w):
    w = w.reshape(Q_LORA, MLA_HEADS, QK_DIM)
    return jnp.pad(w, ((0, 0), (0, 0), (0, HEAD_LANES - QK_DIM))).reshape(Q_LORA, MLA_HEADS * HEAD_LANES)


def _wuq_from_kernel(g):
    return g.reshape(Q_LORA, MLA_HEADS, HEAD_LANES)[:, :, :QK_DIM].reshape(Q_LORA, MLA_HEADS * QK_DIM)


def _wukv_to_kernel(w):
    w = w.reshape(KV_LORA, MLA_HEADS, QK_NOPE + V_HEAD)
    kp = jnp.pad(w[:, :, :QK_NOPE], ((0, 0), (0, 0), (0, HEAD_LANES - QK_NOPE)))
    return jnp.concatenate([kp.reshape(KV_LORA, -1), w[:, :, QK_NOPE:].reshape(KV_LORA, -1)], axis=1)


def _wukv_from_kernel(g):
    hw = MLA_HEADS * HEAD_LANES
    kp = g[:, :hw].reshape(KV_LORA, MLA_HEADS, HEAD_LANES)[:, :, :QK_NOPE]
    vp = g[:, hw:].reshape(KV_LORA, MLA_HEADS, V_HEAD)
    return jnp.concatenate([kp, vp], axis=2).reshape(KV_LORA, MLA_HEADS * (QK_NOPE + V_HEAD))


def _lanes16(v):
    return jnp.pad(v.reshape(1, SSD_HEADS), ((0, 0), (0, LANES - SSD_HEADS)))


def _constants():
    e = np.zeros((LANES, D_SSD), np.float32)
    for h in range(SSD_HEADS):
        e[h, h * SSD_HEAD_DIM:(h + 1) * SSD_HEAD_DIM] = 1.0
    inv_freq = ROPE_THETA ** (-jnp.arange(0, QK_ROPE, 2, dtype=F32) / QK_ROPE)
    half = QK_ROPE // 2
    place = np.zeros((half, LANES), np.float32)
    for j in range(half):
        place[j, QK_NOPE + j] = place[j, QK_NOPE + half + j] = 1.0
    return jnp.asarray(e, BF16), inv_freq.reshape(half, 1), jnp.asarray(place, BF16)


def _local_step(x, positions, mod, w, later_weights, small, tgt, on_grads, sync):
    B, S, D = x.shape
    T = B * S
    expand, invf, place = _constants()
    x0 = x.reshape(T, D)
    pos = positions.reshape(T)
    mods = [mod[:, i * D:(i + 1) * D].reshape(B, 1, D) for i in range(N_MOD)]
    sh1, sc1, g1, sh2, sc2, g2, sh3, sc3, g3 = mods
    dtb, alog = _lanes16(small["dt_bias"]), _lanes16(small["a_log"])
    dsk = jnp.repeat(small["d_skip"].reshape(1, SSD_HEADS), SSD_HEAD_DIM, axis=1)

    x1, a1, u1, f1 = _ffn_fwd(x0, small["norm_ffn1"], sh1, sc1, g1, w["ffn1_w_gate"], w["ffn1_w_up"], w["ffn1_w_down"], S, "ffn1_fwd")
    w = {**w, **later_weights(f1)}
    z, xraw, cq, ckv, dtk = _inproj_fwd(x1, small["norm_mix"], sh2, sc2, w["w_in"], S)
    xraw3 = xraw.reshape(B, S, D_CONV)
    xbc = _conv_fwd(xraw3, small["conv_w"], small["conv_b"])
    dtk3, z3 = dtk.reshape(B, S, LANES), z.reshape(B, S, D_SSD)
    y, yssd, prev = _ssd_fwd(xbc, dtk3, z3, dtb, alog, dsk, small["ssd_norm_w"], expand)
    q, k, v, cos_t, sin_t = _mla_prep(cq, ckv, dtk, pos, small["q_norm_w"], small["kv_norm_w"], w["w_uq"], w["w_ukv"], invf,
                                      place)
    hw = MLA_HEADS * HEAD_LANES
    q3, k3, v3 = q.reshape(B, S, hw), k.reshape(B, S, hw), v.reshape(B, S, hw)
    o3, lse = _attn_fwd(q3, k3, v3)
    o = o3.reshape(T, hw)
    x2, m, ycat = _mix_out(x1, yssd.reshape(T, D_SSD), o, small["mla_norm_w"], w["w_out"], g2, S)
    dx3, a2, u2, f2, loss, d_norm_final = _ffn_fwd(
        x2, small["norm_ffn2"], sh3, sc3, g3, w["ffn2_w_gate"], w["ffn2_w_up"], w["ffn2_w_down"], S, "ffn2_fwd",
        head=(small["norm_final"].reshape(1, D), tgt.reshape(T, D)))

    gw, gs = {}, {}
    dx2, h3, s3, df3, da3, du3, dsh3, dsc3, dg3, gs["norm_ffn2"] = _ffn_bwd(
        dx3, x2, small["norm_ffn2"], sh3, sc3, g3, a2, u2, f2, w["ffn2_w_gate"], w["ffn2_w_up"], w["ffn2_w_down"], S, "ffn2_bwd")
    gw["ffn2_w_gate"], gw["ffn2_w_up"], gw["ffn2_w_down"] = _ffn_wgrad(h3, s3, df3, da3, du3, dsh3, "ffn2_wgrad")
    g2 = g2 + on_grads(("ffn2_w_gate", "ffn2_w_up", "ffn2_w_down"), gw)

    dys, do, dm, dg2, gs["mla_norm_w"] = _mix_out_bwd(dx2, m, o, small["mla_norm_w"], w["w_out"], g2, S)
    gw["w_out"] = _mm_tn(ycat, dm, 512, "dwout")

    dq3, dk3, dv3 = _attn_bwd(q3, k3, v3, o3, do.reshape(B, S, hw), lse)
    dcq, dckv, ddtk_b, qn, kvn, dqb, dkvb, gs["q_norm_w"], gs["kv_norm_w"] = _mla_prep_bwd(
        dq3.reshape(T, hw), dk3.reshape(T, hw), dv3.reshape(T, hw), cq, ckv, cos_t, sin_t, small["q_norm_w"] + sync(dq3),
        small["kv_norm_w"], w["w_uq"], w["w_ukv"])
    gw["w_uq"] = _mm_tn(qn, dqb, 512, "dwuq")
    gw["w_ukv"] = _mm_tn(kvn, dkvb, 1024, "dwukv")

    dxbc, ddtk_a, dz, gs["ssd_norm_w"], dvec = _ssd_bwd(
        xbc, dtk3, z3, y, prev, dys.reshape(B, S, D_SSD), dtb, alog, dsk, small["ssd_norm_w"], expand)
    gs["dt_bias"], gs["a_log"], gs["d_skip"] = dvec[0:1, :SSD_HEADS], dvec[1:2, :SSD_HEADS], dvec[2:3, :SSD_HEADS]
    dxraw, gs["conv_w"], gs["conv_b"] = _conv_bwd(dxbc, xraw3, small["conv_w"], small["conv_b"])
    dx1, h2, dproj, dsh2, dsc2, gs["norm_mix"] = _inproj_bwd(
        dx2, x1, small["norm_mix"], sh2, sc2, w["w_in"], dz.reshape(T, D_SSD), dxraw.reshape(T, D_CONV), dcq, dckv,
        ddtk_a.reshape(T, LANES), ddtk_b, S)
    gw["w_in"] = _mm_tn(dproj, h2, 512, "dwin")
    g1 = g1 + on_grads(("w_in", "w_uq", "w_ukv", "w_out"), gw)

    dx0, h1, s1, df1, da1, du1, dsh1, dsc1, dg1, gs["norm_ffn1"] = _ffn_bwd(
        dx1, x0, small["norm_ffn1"], sh1, sc1, g1, a1, u1, f1, w["ffn1_w_gate"], w["ffn1_w_up"], w["ffn1_w_down"], S, "ffn1_bwd")
    gw["ffn1_w_gate"], gw["ffn1_w_up"], gw["ffn1_w_down"] = _ffn_wgrad(h1, s1, df1, da1, du1, dsh1 + sync(dx0), "ffn1_wgrad")
    gs["norm_final"] = d_norm_final
    dmod = jnp.concatenate([t.reshape(B, D) for t in (dsh1, dsc1, dg1, dsh2, dsc2, dg2, dsh3, dsc3, dg3)], axis=1)
    return loss, dx0.reshape(B, S, D), gw, dmod, gs


HBM_SPEC = pl.BlockSpec(memory_space=pltpu.HBM)
VMEM_SPEC = pl.BlockSpec(memory_space=pltpu.VMEM)


def _place():
    return lax.axis_index("x"), lax.axis_index("y"), lax.axis_index("c")


def _other_chips(mx, my):
    return [(1 - mx, my), (mx, 1 - my), (1 - mx, 1 - my)]


def _remote(src, dst, send_sem, recv_sem, to):
    return pltpu.make_async_remote_copy(src_ref=src, dst_ref=dst, send_sem=send_sem, recv_sem=recv_sem,
                                        device_id=to, device_id_type=MESH)


def _all_gather_small(xa, name):
    r, n = xa.shape

    def body(x_ref, o_ref, token, send_sems, recv_sems):
        mx, my, mc = _place()
        me = 4 * mx + 2 * my + mc
        token[...] = jnp.zeros_like(token)
        o_ref[pl.ds(me, 1)] = x_ref[...][None]
        sends = []
        for k in range(1, N_DEV):
            peer = (mx ^ (k >> 2), my ^ ((k >> 1) & 1), mc ^ (k & 1))
            cp = _remote(x_ref, o_ref.at[me], send_sems.at[k - 1], recv_sems.at[k - 1], peer)
            cp.start()
            sends.append(cp)
        for k in range(1, N_DEV):
            peer = (mx ^ (k >> 2), my ^ ((k >> 1) & 1), mc ^ (k & 1))
            slot = 4 * peer[0] + 2 * peer[1] + peer[2]
            _remote(x_ref, o_ref.at[slot], send_sems.at[k - 1], recv_sems.at[k - 1], peer).wait_recv()
        for cp in sends:
            cp.wait_send()

    return pl.pallas_call(
        body, name=name, in_specs=[VMEM_SPEC], out_specs=[VMEM_SPEC, VMEM_SPEC],
        out_shape=[jax.ShapeDtypeStruct((N_DEV, r, n), xa.dtype), jax.ShapeDtypeStruct((8, LANES), F32)],
        scratch_shapes=[pltpu.SemaphoreType.DMA((N_DEV - 1,)), pltpu.SemaphoreType.DMA((N_DEV - 1,))],
        compiler_params=pltpu.CompilerParams(vmem_limit_bytes=VMEM_LIMIT),
    )(xa)


def _halves_by_rows(shape):
    return (shape[-2] // 2) % 16 == 0


def _half_shape(shape):
    r, c = shape[-2:]
    return tuple(shape[:-2]) + ((r // 2, c) if _halves_by_rows(shape) else (r, c // 2))


def _half_index(shape, hc):
    r, c = shape[-2:]
    if _halves_by_rows(shape):
        return (pl.ds(pl.multiple_of(hc * (r // 2), 16), r // 2), slice(None))
    return (slice(None), pl.ds(pl.multiple_of(hc * (c // 2), LANES), c // 2))


def _half(ref, hc, lead=None):
    idx = _half_index(ref.shape, hc)
    return ref.at[idx] if lead is None else ref.at[(lead,) + idx]


def _gather_weights(shards):
    n = len(shards)

    def body(*refs):
        w_refs, o_refs, token = refs[:n], refs[n:2 * n], refs[2 * n]
        send_sems, recv_sems, stage_sems = refs[2 * n + 1:2 * n + 4]
        stages = refs[2 * n + 4:]
        mx, my, mc = _place()
        chip = 2 * mx + my
        others = _other_chips(mx, my)
        sibling = (mx, my, 1 - mc)
        token[...] = jnp.zeros_like(token)
        stage_in = [pltpu.make_async_copy(w, st, stage_sems.at[0, i]) for i, (w, st) in enumerate(zip(w_refs, stages))]
        for cp in stage_in:
            cp.start()
        first = []
        for i, (w, o) in enumerate(zip(w_refs, o_refs)):
            for k, (cx, cy) in enumerate(others):
                first.append(_remote(_half(w, mc), _half(o, mc, chip), send_sems.at[i, k],
                                     recv_sems.at[i, k], (cx, cy, mc)))
                first[-1].start()
        stage_out = []
        for i, (st, o) in enumerate(zip(stages, o_refs)):
            stage_in[i].wait()
            stage_out.append(pltpu.make_async_copy(st, o.at[chip], stage_sems.at[1, i]))
            stage_out[-1].start()
        passed = []
        for i, (w, o) in enumerate(zip(w_refs, o_refs)):
            for k, (cx, cy) in enumerate(others):
                landed = _half(o, mc, 2 * cx + cy)
                _remote(landed, landed, send_sems.at[i, k], recv_sems.at[i, k], (cx, cy, mc)).wait_recv()
                passed.append(_remote(landed, landed, send_sems.at[i, 3 + k], recv_sems.at[i, 3 + k], sibling))
                passed[-1].start()
        for i, (w, o) in enumerate(zip(w_refs, o_refs)):
            for k, (cx, cy) in enumerate(others):
                there = _half(o, 1 - mc, 2 * cx + cy)
                _remote(there, there, send_sems.at[i, 3 + k], recv_sems.at[i, 3 + k], sibling).wait_recv()
        for cp in first + passed:
            cp.wait_send()
        for cp in stage_out:
            cp.wait()

    out = pl.pallas_call(
        body, name="gather_weights", in_specs=[HBM_SPEC] * n, out_specs=[HBM_SPEC] * n + [VMEM_SPEC],
        out_shape=[jax.ShapeDtypeStruct((N_CHIPS,) + s.shape, s.dtype) for s in shards] + [jax.ShapeDtypeStruct((8, LANES), F32)],
        scratch_shapes=[pltpu.SemaphoreType.DMA((n, 6)), pltpu.SemaphoreType.DMA((n, 6)), pltpu.SemaphoreType.DMA((2, n))]
        + [pltpu.VMEM(s.shape, s.dtype) for s in shards],
        compiler_params=pltpu.CompilerParams(vmem_limit_bytes=VMEM_LIMIT),
    )(*shards)
    return out[:n], out[n]


SEM_SPEC = pl.BlockSpec(memory_space=pltpu.SEMAPHORE)
ANY_SPEC = pl.BlockSpec(memory_space=pl.ANY)
DATAFLOW = pltpu.SideEffectType.DATAFLOW_SIDE_EFFECTING


def _hbm(arr):
    return pltpu.with_memory_space_constraint(arr, pltpu.HBM)


def _gather_start(shards):
    n = len(shards)

    def body(*refs):
        w_refs, land_refs, send_sems, recv_sems, token = refs[:n], refs[n:2 * n], refs[2 * n], refs[2 * n + 1], refs[-1]
        mx, my, mc = _place()
        chip = 2 * mx + my
        for i, (w, land) in enumerate(zip(w_refs, land_refs)):
            for k, (cx, cy) in enumerate(_other_chips(mx, my)):
                _remote(_half(w, mc), _half(land, mc, chip), send_sems.at[3 * i + k],
                        recv_sems.at[3 * i + k], (cx, cy, mc)).start()
        token[...] = jnp.zeros_like(token)

    lands = [lax.empty((N_CHIPS,) + s.shape, s.dtype) for s in shards]
    out = pl.pallas_call(
        body, name="gather_start",
        out_shape=(pltpu.SemaphoreType.DMA((3 * n,)), pltpu.SemaphoreType.DMA((3 * n,)),
                   *[pltpu.HBM(s.shape, s.dtype) for s in shards], *[pltpu.HBM(l.shape, l.dtype) for l in lands],
                   jax.ShapeDtypeStruct((8, LANES), F32)),
        in_specs=[HBM_SPEC] * (2 * n), out_specs=(SEM_SPEC, SEM_SPEC, *[HBM_SPEC] * (2 * n), VMEM_SPEC),
        input_output_aliases={i: 2 + i for i in range(2 * n)},
        compiler_params=pltpu.CompilerParams(has_side_effects=DATAFLOW),
    )(*[_hbm(s) for s in shards], *[_hbm(l) for l in lands])
    return out[0], out[1], out[2:2 + n], out[2 + n:2 + 2 * n], out[-1]


def _gather_wait(send_sems, recv_sems, shards, lands, after):
    n = len(shards)

    def body(*refs):
        w_refs, land_refs, send_sems, recv_sems = refs[:n], refs[n:2 * n], refs[2 * n], refs[2 * n + 1]
        mx, my, mc = _place()
        for i, (w, land) in enumerate(zip(w_refs, land_refs)):
            for k, (cx, cy) in enumerate(_other_chips(mx, my)):
                cp = _remote(_half(w, mc), _half(land, mc, 2 * cx + cy), send_sems.at[3 * i + k],
                             recv_sems.at[3 * i + k], (cx, cy, mc))
                cp.wait_send()
                cp.wait_recv()

    out = pl.pallas_call(
        body, name="gather_wait",
        out_shape=(*[pltpu.HBM(s.shape, s.dtype) for s in shards], *[pltpu.HBM(l.shape, l.dtype) for l in lands]),
        in_specs=[HBM_SPEC] * (2 * n) + [SEM_SPEC, SEM_SPEC, ANY_SPEC], out_specs=tuple([HBM_SPEC] * (2 * n)),
        input_output_aliases={i: i for i in range(2 * n)},
        compiler_params=pltpu.CompilerParams(has_side_effects=DATAFLOW),
    )(*shards, *lands, send_sems, recv_sems, after)
    return out[n:]


def _gather_finish(shards, lands):
    n = len(shards)

    def body(*refs):
        w_refs, land_refs, o_refs = refs[:n], refs[n:2 * n], refs[2 * n:3 * n]
        send_sems, recv_sems, stage_sems = refs[3 * n:3 * n + 3]
        stages = refs[3 * n + 3:]
        mx, my, mc = _place()
        chip = 2 * mx + my
        others = _other_chips(mx, my)
        sibling = (mx, my, 1 - mc)
        stage_in = [pltpu.make_async_copy(w, st, stage_sems.at[0, i]) for i, (w, st) in enumerate(zip(w_refs, stages))]
        for cp in stage_in:
            cp.start()
        passed = []
        for i, (w, o) in enumerate(zip(w_refs, o_refs)):
            for k, (cx, cy) in enumerate(others):
                landed = _half(o, mc, 2 * cx + cy)
                passed.append(_remote(landed, landed, send_sems.at[i, k], recv_sems.at[i, k], sibling))
                passed[-1].start()
        stage_out = []
        for i, (st, o) in enumerate(zip(stages, o_refs)):
            stage_in[i].wait()
            stage_out.append(pltpu.make_async_copy(st, o.at[chip], stage_sems.at[1, i]))
            stage_out[-1].start()
        for i, (w, o) in enumerate(zip(w_refs, o_refs)):
            for k, (cx, cy) in enumerate(others):
                there = _half(o, 1 - mc, 2 * cx + cy)
                _remote(there, there, send_sems.at[i, k], recv_sems.at[i, k], sibling).wait_recv()
        for cp in passed:
            cp.wait_send()
        for cp in stage_out:
            cp.wait()

    return pl.pallas_call(
        body, name="gather_finish", in_specs=[HBM_SPEC] * (2 * n), out_specs=[HBM_SPEC] * n,
        out_shape=[jax.ShapeDtypeStruct(l.shape, l.dtype) for l in lands],
        input_output_aliases={n + i: i for i in range(n)},
        scratch_shapes=[pltpu.SemaphoreType.DMA((n, 3)), pltpu.SemaphoreType.DMA((n, 3)), pltpu.SemaphoreType.DMA((2, n))]
        + [pltpu.VMEM(s.shape, s.dtype) for s in shards],
        compiler_params=pltpu.CompilerParams(vmem_limit_bytes=VMEM_LIMIT),
    )(*shards, *lands)


def _scatter_start(ss, tag):
    n = len(ss)

    def body(*refs):
        s_refs, land_refs, send_sems, recv_sems, token = refs[:n], refs[n:2 * n], refs[2 * n], refs[2 * n + 1], refs[-1]
        mx, my, mc = _place()
        chip = 2 * mx + my
        for i, (s, land) in enumerate(zip(s_refs, land_refs)):
            for k, (cx, cy) in enumerate(_other_chips(mx, my)):
                _remote(s.at[2 * cx + cy], land.at[chip], send_sems.at[3 * i + k], recv_sems.at[3 * i + k],
                        (cx, cy, mc)).start()
        token[...] = jnp.zeros_like(token)

    lands = [lax.empty(s.shape, s.dtype) for s in ss]
    out = pl.pallas_call(
        body, name="scatter_start_" + tag,
        out_shape=(pltpu.SemaphoreType.DMA((3 * n,)), pltpu.SemaphoreType.DMA((3 * n,)),
                   *[pltpu.HBM(s.shape, s.dtype) for s in ss], *[pltpu.HBM(l.shape, l.dtype) for l in lands],
                   jax.ShapeDtypeStruct((8, LANES), F32)),
        in_specs=[HBM_SPEC] * (2 * n), out_specs=(SEM_SPEC, SEM_SPEC, *[HBM_SPEC] * (2 * n), VMEM_SPEC),
        input_output_aliases={i: 2 + i for i in range(2 * n)},
        compiler_params=pltpu.CompilerParams(has_side_effects=DATAFLOW),
    )(*[_hbm(s) for s in ss], *[_hbm(l) for l in lands])
    return out[0], out[1], out[2:2 + n], out[2 + n:2 + 2 * n], out[-1]


def _scatter_wait(send_sems, recv_sems, ss, lands, after, tag):
    n = len(ss)

    def body(*refs):
        s_refs, land_refs, send_sems, recv_sems = refs[:n], refs[n:2 * n], refs[2 * n], refs[2 * n + 1]
        mx, my, mc = _place()
        for i, (s, land) in enumerate(zip(s_refs, land_refs)):
            for k, (cx, cy) in enumerate(_other_chips(mx, my)):
                slot = land.at[2 * cx + cy]
                cp = _remote(s.at[2 * cx + cy], slot, send_sems.at[3 * i + k], recv_sems.at[3 * i + k], (cx, cy, mc))
                cp.wait_send()
                cp.wait_recv()

    out = pl.pallas_call(
        body, name="scatter_wait_" + tag,
        out_shape=(*[pltpu.HBM(s.shape, s.dtype) for s in ss], *[pltpu.HBM(l.shape, l.dtype) for l in lands]),
        in_specs=[HBM_SPEC] * (2 * n) + [SEM_SPEC, SEM_SPEC, ANY_SPEC], out_specs=tuple([HBM_SPEC] * (2 * n)),
        input_output_aliases={i: i for i in range(2 * n)},
        compiler_params=pltpu.CompilerParams(has_side_effects=DATAFLOW),
    )(*ss, *lands, send_sems, recv_sems, after)
    return out[:n], out[n:]


def _swap_halves(gs, after, name):
    n = len(gs)

    def body(*refs):
        g_refs, o_refs, send_sems, recv_sems = refs[:n], refs[n + 1:2 * n + 1], refs[2 * n + 1], refs[2 * n + 2]
        mx, my, mc = _place()
        copies = []
        for i, (g, o) in enumerate(zip(g_refs, o_refs)):
            src = g.at[(slice(None),) + _half_index(g.shape, 1 - mc)]
            copies.append(_remote(src, o, send_sems.at[i], recv_sems.at[i], (mx, my, 1 - mc)))
            copies[-1].start()
        for cp in copies:
            cp.wait()

    return pl.pallas_call(
        body, name=name, in_specs=[HBM_SPEC] * n + [ANY_SPEC], out_specs=[HBM_SPEC] * n,
        out_shape=[jax.ShapeDtypeStruct(_half_shape(g.shape), g.dtype) for g in gs],
        scratch_shapes=[pltpu.SemaphoreType.DMA((n,)), pltpu.SemaphoreType.DMA((n,))],
    )(*gs, after)


def _swap_start(gs, tag):
    n = len(gs)

    def body(*refs):
        g_refs, land_refs, send_sems, recv_sems, token = refs[:n], refs[n:2 * n], refs[2 * n], refs[2 * n + 1], refs[-1]
        mx, my, mc = _place()
        for i, (g, land) in enumerate(zip(g_refs, land_refs)):
            src = g.at[(slice(None),) + _half_index(g.shape, 1 - mc)]
            _remote(src, land, send_sems.at[i], recv_sems.at[i], (mx, my, 1 - mc)).start()
        token[...] = jnp.zeros_like(token)

    lands = [lax.empty(_half_shape(g.shape), g.dtype) for g in gs]
    out = pl.pallas_call(
        body, name="swap_start_" + tag,
        out_shape=(pltpu.SemaphoreType.DMA((n,)), pltpu.SemaphoreType.DMA((n,)), *[pltpu.HBM(g.shape, g.dtype) for g in gs],
                   *[pltpu.HBM(l.shape, l.dtype) for l in lands], jax.ShapeDtypeStruct((8, LANES), F32)),
        in_specs=[HBM_SPEC] * (2 * n), out_specs=(SEM_SPEC, SEM_SPEC, *[HBM_SPEC] * (2 * n), VMEM_SPEC),
        input_output_aliases={i: 2 + i for i in range(2 * n)},
        compiler_params=pltpu.CompilerParams(has_side_effects=DATAFLOW),
    )(*[_hbm(g) for g in gs], *[_hbm(l) for l in lands])
    return out[0], out[1], out[2:2 + n], out[2 + n:2 + 2 * n], out[-1]


def _swap_wait(send_sems, recv_sems, gs, lands, after, tag):
    n = len(gs)

    def body(*refs):
        g_refs, land_refs, send_sems, recv_sems = refs[:n], refs[n:2 * n], refs[2 * n], refs[2 * n + 1]
        mx, my, mc = _place()
        for i, (g, land) in enumerate(zip(g_refs, land_refs)):
            src = g.at[(slice(None),) + _half_index(g.shape, 1 - mc)]
            cp = _remote(src, land, send_sems.at[i], recv_sems.at[i], (mx, my, 1 - mc))
            cp.wait_send()
            cp.wait_recv()

    out = pl.pallas_call(
        body, name="swap_wait_" + tag,
        out_shape=(*[pltpu.HBM(g.shape, g.dtype) for g in gs], *[pltpu.HBM(l.shape, l.dtype) for l in lands]),
        in_specs=[HBM_SPEC] * (2 * n) + [SEM_SPEC, SEM_SPEC, ANY_SPEC], out_specs=tuple([HBM_SPEC] * (2 * n)),
        input_output_aliases={i: i for i in range(2 * n)},
        compiler_params=pltpu.CompilerParams(has_side_effects=DATAFLOW),
    )(*gs, *lands, send_sems, recv_sems, after)
    return out[:n], out[n:]


def _pair_sums(gs, gots, name):
    n = len(gs)

    def body(*refs):
        g_refs, got_refs, o_refs, load_sems, store_sems = refs[:n], refs[n:2 * n], refs[2 * n:3 * n], refs[3 * n], refs[3 * n + 1]
        mine, theirs, sums = refs[3 * n + 2:4 * n + 2], refs[4 * n + 2:5 * n + 2], refs[5 * n + 2:]
        mc = lax.axis_index("c")
        loads = []
        for i, (g, got) in enumerate(zip(g_refs, got_refs)):
            loads.append((pltpu.make_async_copy(g.at[(slice(None),) + _half_index(g.shape, mc)], mine[i], load_sems.at[i, 0]),
                          pltpu.make_async_copy(got, theirs[i], load_sems.at[i, 1])))
            for cp in loads[-1]:
                cp.start()
        stores = []
        for i in range(n):
            for cp in loads[i]:
                cp.wait()
            sums[i][...] = (mine[i][...].astype(F32) + theirs[i][...].astype(F32)).astype(BF16)
            stores.append(pltpu.make_async_copy(sums[i], o_refs[i], store_sems.at[i]))
            stores[-1].start()
        for cp in stores:
            cp.wait()

    halves = [jax.ShapeDtypeStruct(got.shape, BF16) for got in gots]
    return pl.pallas_call(
        body, name=name, in_specs=[HBM_SPEC] * (2 * n), out_specs=[HBM_SPEC] * n, out_shape=halves,
        scratch_shapes=[pltpu.SemaphoreType.DMA((n, 2)), pltpu.SemaphoreType.DMA((n,))]
        + [pltpu.VMEM(got.shape, g.dtype) for g, got in zip(gs, gots)] + [pltpu.VMEM(got.shape, got.dtype) for got in gots]
        + [pltpu.VMEM(got.shape, BF16) for got in gots],
        compiler_params=pltpu.CompilerParams(vmem_limit_bytes=VMEM_LIMIT),
    )(*gs, *gots)


def _reduce_join(owns, gots, name):
    n = len(owns)

    def body(*refs):
        own_refs, got_refs, mine_refs, theirs_refs = refs[:n], refs[n:2 * n], refs[2 * n:3 * n], refs[3 * n:4 * n]
        send_sems, recv_sems, load_sems, store_sems = refs[4 * n:4 * n + 4]
        parts, sums = refs[4 * n + 4:5 * n + 4], refs[5 * n + 4:]
        mx, my, mc = _place()
        chip = 2 * mx + my
        loads = []
        for i, (own, got, part) in enumerate(zip(own_refs, got_refs, parts)):
            loads.append([pltpu.make_async_copy((own if k == 0 else got).at[chip ^ k], part.at[k], load_sems.at[i, k])
                          for k in range(N_CHIPS)])
            for cp in loads[-1]:
                cp.start()
        out = []
        for i, (part, total, mine, theirs) in enumerate(zip(parts, sums, mine_refs, theirs_refs)):
            for cp in loads[i]:
                cp.wait()
            total[...] = ((part[0].astype(F32) + part[1].astype(F32)) + part[2].astype(F32)) + part[3].astype(F32)
            out.append(pltpu.make_async_copy(total, mine, store_sems.at[i]))
            out.append(_remote(total, theirs, send_sems.at[i], recv_sems.at[i], (mx, my, 1 - mc)))
            out[-2].start()
            out[-1].start()
        for cp in out:
            cp.wait()

    halves = [jax.ShapeDtypeStruct(o.shape[1:], F32) for o in owns]
    out = pl.pallas_call(
        body, name=name, in_specs=[HBM_SPEC] * (2 * n), out_specs=[HBM_SPEC] * (2 * n), out_shape=halves + halves,
        scratch_shapes=[pltpu.SemaphoreType.DMA((n,)), pltpu.SemaphoreType.DMA((n,)), pltpu.SemaphoreType.DMA((n, N_CHIPS)),
                        pltpu.SemaphoreType.DMA((n,))]
        + [pltpu.VMEM(o.shape, o.dtype) for o in owns] + [pltpu.VMEM(o.shape[1:], F32) for o in owns],
        compiler_params=pltpu.CompilerParams(vmem_limit_bytes=VMEM_LIMIT),
    )(*owns, *gots)
    return out[:n], out[n:]


def _adam_math(w, g, m, v):
    m2 = ADAM_B1 * m + (1.0 - ADAM_B1) * g
    v2 = ADAM_B2 * v + (1.0 - ADAM_B2) * (g * g)
    m_hat = m2 * (1.0 / (1.0 - ADAM_B1 ** ADAM_STEP))
    v_hat = v2 * (1.0 / (1.0 - ADAM_B2 ** ADAM_STEP))
    delta = -ADAM_LR * (m_hat / (jnp.sqrt(v_hat) + ADAM_EPS) + ADAM_WD * w)
    return delta, m2, v2


def _adam(w, g, m, v, name):
    def body(w_ref, g_ref, m_ref, v_ref, d_ref, m2_ref, v2_ref):
        d_ref[...], m2_ref[...], v2_ref[...] = _adam_math(w_ref[...], g_ref[...], m_ref[...], v_ref[...])

    return pl.pallas_call(body, name=name, out_shape=[jax.ShapeDtypeStruct(w.shape, F32)] * 3)(w, g, m, v)


def _adam_halves(w, m, v, mine, theirs, core, name):
    hr, hcols = _half_shape(w.shape)[1:]
    by_rows = _halves_by_rows(w.shape)

    def body(core_ref, w_ref, m_ref, v_ref, mine_ref, theirs_ref, g_ref, d_ref, m2_ref, v2_ref):
        g = jnp.where(pl.program_id(0) == core_ref[0], mine_ref[...], theirs_ref[...])
        g_ref[0] = g
        d_ref[0], m2_ref[0], v2_ref[0] = _adam_math(w_ref[0], g, m_ref[0], v_ref[0])

    half = pl.BlockSpec((1, hr, hcols), lambda hc, core_ref: (0, hc, 0) if by_rows else (0, 0, hc))
    whole = pl.BlockSpec((hr, hcols), lambda hc, core_ref: (0, 0))
    return pl.pallas_call(
        body, name=name,
        grid_spec=pltpu.PrefetchScalarGridSpec(
            num_scalar_prefetch=1, grid=(2,), in_specs=[half, half, half, whole, whole], out_specs=[half] * 4),
        out_shape=[jax.ShapeDtypeStruct(w.shape, F32)] * 4,
        compiler_params=_cparams(("arbitrary",)),
    )(core, w, m, v, mine, theirs)


ADA_COLS = N_MOD * D_MODEL // N_CHIPS


def _ada_fwd(c_all, w_ada, b_cols):
    def body(c_ref, w_ref, b_ref, o_ref):
        cv = c_ref[...]
        act = (cv * _sigmoid(cv)).astype(BF16)
        o_ref[...] = _dot(act, w_ref[...].astype(BF16)) + b_ref[...]

    return pl.pallas_call(
        body, name="ada_fwd", out_shape=jax.ShapeDtypeStruct((c_all.shape[0], ADA_COLS), F32),
        compiler_params=pltpu.CompilerParams(vmem_limit_bytes=VMEM_LIMIT),
    )(c_all, w_ada, b_cols)


def _ada_bwd(c_all, dmod_cols, w, m, v):
    nb = c_all.shape[0]
    tn = 384

    def body(c_ref, d_ref, w_ref, m_ref, v_ref, g_ref, dl_ref, m2_ref, v2_ref):
        cv = c_ref[...]
        act = (cv * _sigmoid(cv)).astype(BF16)
        g = _dot_tn(act, d_ref[...].astype(BF16))
        g_ref[...] = g
        dl_ref[...], m2_ref[...], v2_ref[...] = _adam_math(w_ref[...], g, m_ref[...], v_ref[...])

    blk = pl.BlockSpec((D_MODEL, tn), lambda j: (0, j))
    return pl.pallas_call(
        body, name="ada_bwd", grid=(ADA_COLS // tn,),
        in_specs=[pl.BlockSpec((nb, D_MODEL), lambda j: (0, 0)), pl.BlockSpec((nb, tn), lambda j: (0, j)), blk, blk, blk],
        out_specs=[blk] * 4, out_shape=[jax.ShapeDtypeStruct((D_MODEL, ADA_COLS), F32)] * 4,
        compiler_params=_cparams(("arbitrary",)),
    )(c_all, dmod_cols, w, m, v)


SMALL_NAMES = ("norm_ffn1", "norm_mix", "conv_w", "conv_b", "ssd_norm_w", "q_norm_w", "kv_norm_w", "mla_norm_w",
               "norm_ffn2", "norm_final", "dt_bias", "a_log", "d_skip")
SMALL_SIZES = (1024, 1024, CONV_WIDTH * D_CONV, D_CONV, 1024, Q_LORA, KV_LORA, 1024, 1024, 1024, 16, 16, 16)
SMALL_ROWS = 16
MOD_ROWS = 2 * N_MOD
SEND_ROWS = 40


def _pack_small(parts):
    flat = jnp.concatenate([parts[n].reshape(-1) for n in SMALL_NAMES])
    return jnp.pad(flat, (0, SMALL_ROWS * D_MODEL - flat.shape[0]))


def _unpack_small(flat):
    out, off = {}, 0
    for n, size in zip(SMALL_NAMES, SMALL_SIZES):
        out[n] = flat[off:off + size]
        off += size
    return out


def _small_sum(got):
    def body(g_ref, o_ref):
        bsum = jnp.zeros((N_MOD, D_MODEL), F32)
        ssum = jnp.zeros((SMALL_ROWS, D_MODEL), F32)
        for d in range(N_DEV):
            bsum = bsum + g_ref[d, 0:N_MOD, :] + g_ref[d, N_MOD:MOD_ROWS, :]
            ssum = ssum + g_ref[d, MOD_ROWS:MOD_ROWS + SMALL_ROWS, :]
        o_ref[...] = jnp.concatenate([bsum, ssum, jnp.zeros((32 - N_MOD - SMALL_ROWS, D_MODEL), F32)], axis=0)

    return pl.pallas_call(body, name="small_sum", out_shape=jax.ShapeDtypeStruct((32, D_MODEL), F32))(got)


BIG_NAMES = ("ffn1_w_gate", "ffn1_w_up", "ffn1_w_down", "w_in", "w_uq", "w_ukv", "w_out", "ffn2_w_gate", "ffn2_w_up",
             "ffn2_w_down")
_TO_KERNEL = {"w_in": _win_to_kernel, "w_uq": _wuq_to_kernel, "w_ukv": _wukv_to_kernel}
_FROM_KERNEL = {"w_in": _win_from_kernel, "w_uq": _wuq_from_kernel, "w_ukv": _wukv_from_kernel}


def _columns_joined(w4):
    n, r, c = w4.shape
    return w4.transpose(1, 0, 2).reshape(r, n * c)


def _columns_split(g):
    r, cols = g.shape
    return g.reshape(r, N_CHIPS, cols // N_CHIPS).transpose(1, 0, 2)


def kernel(x, c, positions, w_ada, b_ada, norm_ffn1, ffn1_w_gate, ffn1_w_up, ffn1_w_down, norm_mix, w_in, conv_w, conv_b, dt_bias, a_log, d_skip, ssd_norm_w, q_norm_w, w_uq, kv_norm_w, w_ukv, mla_norm_w, w_out, norm_ffn2, ffn2_w_gate, ffn2_w_up, ffn2_w_down, norm_final, loss_target, m_w_ada, m_b_ada, m_norm_ffn1, m_ffn1_w_gate, m_ffn1_w_up, m_ffn1_w_down, m_norm_mix, m_w_in, m_conv_w, m_conv_b, m_dt_bias, m_a_log, m_d_skip, m_ssd_norm_w, m_q_norm_w, m_w_uq, m_kv_norm_w, m_w_ukv, m_mla_norm_w, m_w_out, m_norm_ffn2, m_ffn2_w_gate, m_ffn2_w_up, m_ffn2_w_down, m_norm_final, v_w_ada, v_b_ada, v_norm_ffn1, v_ffn1_w_gate, v_ffn1_w_up, v_ffn1_w_down, v_norm_mix, v_w_in, v_conv_w, v_conv_b, v_dt_bias, v_a_log, v_d_skip, v_ssd_norm_w, v_q_norm_w, v_w_uq, v_kv_norm_w, v_w_ukv, v_mla_norm_w, v_w_out, v_norm_ffn2, v_ffn2_w_gate, v_ffn2_w_up, v_ffn2_w_down, v_norm_final):
    a = dict(locals())
    held_transposed = ("ffn1_w_gate", "ffn1_w_up", "ffn2_w_gate", "ffn2_w_up", "w_in")
    for n in held_transposed:
        for p in ("", "m_", "v_"):
            a[p + n] = a[p + n].transpose(0, 2, 1)
    B, S, D = x.shape
    mx, my, mc = _place()
    chip = 2 * mx + my
    dev = 2 * chip + mc
    core = mc.astype(jnp.int32).reshape(1)

    cw_rows = jnp.pad(conv_w[0], ((0, 0), (0, D - conv_w.shape[2])))
    got, _ = _all_gather_small(jnp.concatenate([c, cw_rows, jnp.zeros((8 - B - CONV_WIDTH, D), F32)], axis=0), "gather_c")
    c_all = got[:, :B, :].reshape(N_DEV * B, D)
    conv_full = got[::2, B:B + CONV_WIDTH, :conv_w.shape[2]].transpose(1, 0, 2).reshape(CONV_WIDTH, D_CONV)

    b_cols = lax.dynamic_slice(b_ada, (0, chip * ADA_COLS), (1, ADA_COLS))
    mod_all, mod_done = _all_gather_small(_ada_fwd(c_all, w_ada[0], b_cols), "gather_mod")
    mod = lax.dynamic_slice(mod_all, (0, B * dev, 0), (N_DEV, B, ADA_COLS))[::2].transpose(1, 0, 2).reshape(B, N_MOD * D)

    first = ("ffn1_w_gate", "ffn1_w_up", "ffn1_w_down")
    later = tuple(n for n in BIG_NAMES if n not in first)
    got_first, gathered = _gather_weights([(a[n][0] + mod_done[0, 0]).astype(BF16) for n in first])
    w = dict(zip(first, got_first))
    in_flight = _gather_start([(a[n][0] + gathered[0, 0]).astype(BF16) for n in later])

    def later_weights(after):
        send_sems, recv_sems, shards, lands, _ = in_flight
        lands = _gather_wait(send_sems, recv_sems, shards, lands, after)
        wl = dict(zip(later, _gather_finish([a[n][0].astype(BF16) for n in later], lands)))
        for n, to_kernel in _TO_KERNEL.items():
            wl[n] = to_kernel(wl[n].reshape(-1, D) if n in held_transposed else _columns_joined(wl[n]))
        wl["w_out"] = wl["w_out"].reshape(D_SSD + D_MLA, D)
        return wl

    small = {n: a[n].reshape(1, -1) for n in SMALL_NAMES if n not in ("conv_w", "norm_final")}
    small["conv_w"], small["norm_final"] = conv_full, norm_final

    def shards_of(names, gw):
        g4 = []
        for n in names:
            g = gw[n]
            if n in _FROM_KERNEL:
                g = _FROM_KERNEL[n](g) if n in held_transposed else _columns_split(_FROM_KERNEL[n](g))
            g4.append(g.reshape(N_CHIPS, a[n].shape[1], a[n].shape[2]))
        return g4

    def scatter_group(names, g4, swapped):
        pair = _pair_sums(g4, swapped, "pair_sums_" + names[0])
        return (names,) + tuple(_scatter_start(pair, names[0]))

    grads, deltas, new_m, new_v = {}, {}, {}, {}

    def finish_groups(some, after):
        names, owns, gots = [], [], []
        for group_names, send_sems, recv_sems, pair, lands, _ in some:
            pair, lands = _scatter_wait(send_sems, recv_sems, pair, lands, after, group_names[0])
            names += group_names
            owns += pair
            gots += lands
        mine, theirs = _reduce_join(owns, gots, "reduce_join_" + names[0])
        for n, own, other in zip(names, mine, theirs):
            grads[n], deltas[n], new_m[n], new_v[n] = _adam_halves(a[n], a["m_" + n], a["v_" + n], own, other, core, "adam_" + n)
        return deltas[names[-1]]

    swapping, groups = [], []

    def on_grads(names, gw):
        send_sems, recv_sems, g4, lands, token = _swap_start(shards_of(names, gw), names[0])
        swapping.append((names, send_sems, recv_sems, g4, lands))
        return token[0, 0]

    def sync(after):
        token = 0.0
        while swapping:
            names, send_sems, recv_sems, g4, lands = swapping.pop(0)
            g4, swapped = _swap_wait(send_sems, recv_sems, g4, lands, after, names[0])
            groups.append(scatter_group(names, g4, swapped))
            token = groups[-1][5][0, 0]
        return token

    loss_blk, grad_x, gw, dmod, gs = _local_step(x, positions, mod + in_flight[4][0, 0], w, later_weights, small, loss_target,
                                                 on_grads, sync)

    small_flat = _pack_small(gs).at[-1].set(loss_blk[0, 0])
    send = jnp.concatenate([dmod.reshape(MOD_ROWS, D), small_flat.reshape(SMALL_ROWS, D),
                            jnp.zeros((SEND_ROWS - MOD_ROWS - SMALL_ROWS, D), F32)], axis=0)
    got, _ = _all_gather_small(send, "gather_small")
    summed = _small_sum(got)
    sums = summed[N_MOD:N_MOD + SMALL_ROWS].reshape(-1)
    loss = sums[-1]
    gsmall = _unpack_small(sums)
    gsmall["conv_w"] = lax.dynamic_slice(gsmall["conv_w"].reshape(CONV_WIDTH, D_CONV), (0, chip * conv_w.shape[2]),
                                         (CONV_WIDTH, conv_w.shape[2]))
    gsmall["b_ada"] = summed[:N_MOD]
    names = ("b_ada",) + SMALL_NAMES
    rows = 208

    def pack(parts):
        flat = jnp.concatenate([parts[n].reshape(-1) for n in names])
        return jnp.pad(flat, (0, rows * LANES - flat.shape[0])).reshape(rows, LANES)

    packed = [pack({n: a[p + n] for n in names}) for p in ("", "m_", "v_")]
    g_p = pack(gsmall)
    outs = (g_p,) + tuple(_adam(packed[0], g_p, packed[1], packed[2], "adam_small"))
    for dst, flat in zip((grads, deltas, new_m, new_v), outs):
        flat, off = flat.reshape(-1), 0
        for n in names:
            dst[n] = flat[off:off + a[n].size].reshape(a[n].shape)
            off += a[n].size

    dmod_all = got[:, :MOD_ROWS, :].reshape(N_DEV * B, N_MOD * D)
    dmod_cols = lax.dynamic_slice(dmod_all, (0, chip * ADA_COLS), (N_DEV * B, ADA_COLS))
    ada = _ada_bwd(c_all, dmod_cols, w_ada[0], m_w_ada[0], v_w_ada[0])
    for dst, t in zip((grads, deltas, new_m, new_v), ada):
        dst["w_ada"] = t[None]

    g4 = shards_of(first, gw)
    last = scatter_group(first, g4, _swap_halves(g4, summed, "swap_" + first[0]))
    finish_groups([last], finish_groups(groups, last[5]))
    for dst in (grads, deltas, new_m, new_v):
        for n in held_transposed:
            dst[n] = dst[n].transpose(0, 2, 1)

    order = ("w_ada", "b_ada", "norm_ffn1", "ffn1_w_gate", "ffn1_w_up", "ffn1_w_down", "norm_mix", "w_in", "conv_w", "conv_b",
             "dt_bias", "a_log", "d_skip", "ssd_norm_w", "q_norm_w", "w_uq", "kv_norm_w", "w_ukv", "mla_norm_w", "w_out",
             "norm_ffn2", "ffn2_w_gate", "ffn2_w_up", "ffn2_w_down", "norm_final")
    return (loss, grad_x, *[grads[n] for n in order], *[deltas[n] for n in order], *[new_m[n] for n in order],
            *[new_v[n] for n in order])
```

```python
import math

import jax
import jax.numpy as jnp
import numpy as np
from jax import lax
from jax.experimental import pallas as pl
from jax.experimental.pallas import tpu as pltpu

F32 = jnp.float32
BF16 = jnp.bfloat16

D_MODEL = 1024
D_SSD = 1024
D_MLA = 1024
SSD_HEADS = 16
SSD_HEAD_DIM = 64
SSD_GROUPS = 2
SSD_STATE = 128
CONV_WIDTH = 4
CHUNK = 128
MLA_HEADS = 8
QK_NOPE = 64
QK_ROPE = 32
QK_DIM = QK_NOPE + QK_ROPE
V_HEAD = 128
Q_LORA = 384
KV_LORA = 256
ROPE_THETA = 10000.0
N_MOD = 9
EPS = 1e-6
D_CONV = D_SSD + 2 * SSD_GROUPS * SSD_STATE
D_PROJ = 3328
HEAD_LANES = 128
ADAM_LR = 0.001
ADAM_B1 = 0.9
ADAM_B2 = 0.999
ADAM_EPS = 1e-08
ADAM_WD = 0.01
ADAM_STEP = 10

LANES = 128
VMEM_LIMIT = 56 * 1024 * 1024
TOKEN_TILE = 512
WIDE_TOKEN_TILE = 1024
WGRAD_TOKEN_TILE = 2048
ATTN_FWD_Q_TILE = 1024
ATTN_FWD_KV_TILE = 1024
ATTN_BWD_TILE = 1024
N_CHIPS = 4
N_DEV = 8

MESH = pl.DeviceIdType.MESH


def _dot(a, b):
    return jnp.dot(a, b, preferred_element_type=F32)


def _dot_nt(a, b):
    return lax.dot_general(a, b, (((1,), (1,)), ((), ())), preferred_element_type=F32)


def _dot_tn(a, b):
    return lax.dot_general(a, b, (((0,), (0,)), ((), ())), preferred_element_type=F32)


def _cparams(semantics):
    return pltpu.CompilerParams(dimension_semantics=semantics, vmem_limit_bytes=VMEM_LIMIT)


def _resident(shape):
    zeros = (0,) * len(shape)
    return pl.BlockSpec(shape, lambda *_: zeros, pipeline_mode=pl.Buffered(1))


def _sigmoid(x):
    return jax.nn.sigmoid(x)


def _rms_stats(x):
    r = lax.rsqrt(jnp.mean(x * x, axis=-1, keepdims=True) + EPS)
    return x * r, r


def _rms_bwd(dn, xh, r, w):
    dxh = dn * w
    dx = r * (dxh - xh * jnp.mean(dxh * xh, axis=-1, keepdims=True))
    return dx, dn * xh


def _colsum(v):
    return jnp.sum(v, axis=0, keepdims=True)


def _ffn_fwd(x, nw, sh, sc, g, wg, wu, wd, seq, name, head=None):
    T, D = x.shape
    fs = wg.shape[1]
    tm = min(TOKEN_TILE, seq)
    tps = seq // tm

    def body(x_ref, nw_ref, sh_ref, sc_ref, g_ref, wg_ref, wu_ref, wd_ref, *rest):
        if head is None:
            xo_ref, a_ref, u_ref, f_ref = rest
        else:
            nf_ref, t_ref, xo_ref, a_ref, u_ref, f_ref, loss_ref, dnf_ref = rest

            @pl.when(pl.program_id(0) == 0)
            def _():
                loss_ref[...] = jnp.zeros_like(loss_ref)
                dnf_ref[...] = jnp.zeros_like(dnf_ref)

        xv = x_ref[...]
        xh, _ = _rms_stats(xv)
        h = (xh * nw_ref[...]) * (1.0 + sc_ref[0]) + sh_ref[0]
        hb = h.astype(BF16)
        f = jnp.zeros((tm, D), F32)
        for j in range(N_CHIPS):
            a = _dot_nt(hb, wg_ref[j])
            u = _dot_nt(hb, wu_ref[j])
            a_ref[j] = a.astype(BF16)
            u_ref[j] = u.astype(BF16)
            f = f + _dot((a * _sigmoid(a) * u).astype(BF16), wd_ref[j])
        f_ref[...] = f.astype(BF16)
        xo = xv + 0.5 * g_ref[0] * f
        if head is None:
            xo_ref[...] = xo
        else:
            xh, r = _rms_stats(xo)
            nfv = nf_ref[...]
            err = xh * nfv - t_ref[...]
            loss_ref[...] += (0.5 / D) * jnp.sum(err * err)
            dxo, dw_rows = _rms_bwd(err * (1.0 / D), xh, r, nfv)
            xo_ref[...] = dxo
            dnf_ref[...] += _colsum(dw_rows)

    rows = lambda n: pl.BlockSpec((tm, n), lambda i: (i, 0))
    act = pl.BlockSpec((N_CHIPS, tm, fs), lambda i: (0, i, 0))
    perb = pl.BlockSpec((1, 1, D), lambda i: (i // tps, 0, 0))
    sd = jax.ShapeDtypeStruct
    in_specs = [rows(D), _resident((1, D)), perb, perb, perb, _resident((N_CHIPS, fs, D)), _resident((N_CHIPS, fs, D)),
                _resident((N_CHIPS, fs, D))]
    out_specs = [rows(D), act, act, rows(D)]
    out_shape = [sd((T, D), F32), sd((N_CHIPS, T, fs), BF16), sd((N_CHIPS, T, fs), BF16), sd((T, D), BF16)]
    if head is not None:
        in_specs += [_resident((1, D)), rows(D)]
        out_specs += [pl.BlockSpec((8, LANES), lambda i: (0, 0)), pl.BlockSpec((1, D), lambda i: (0, 0))]
        out_shape += [sd((8, LANES), F32), sd((1, D), F32)]
    return pl.pallas_call(
        body, grid=(T // tm,), name=name, in_specs=in_specs, out_specs=out_specs, out_shape=out_shape,
        compiler_params=_cparams(("arbitrary",)),
    )(x, nw, sh, sc, g, wg, wu, wd, *(head or ()))


def _ffn_bwd(dxo, x, nw, sh, sc, g, a, u, f, wg, wu, wd, seq, name):
    T, D = x.shape
    fs = wg.shape[1]
    B = T // seq
    tm = min(TOKEN_TILE // 2, seq)
    tps = seq // tm

    def body(dxo_ref, x_ref, nw_ref, sh_ref, sc_ref, g_ref, a_ref, u_ref, f_ref, wg_ref, wu_ref, wd_ref,
             dx_ref, h_ref, s_ref, df_ref, da_ref, du_ref, dsh_ref, dsc_ref, dg_ref, dnw_ref):
        i = pl.program_id(0)

        @pl.when(i % tps == 0)
        def _():
            dsh_ref[...] = jnp.zeros_like(dsh_ref)
            dsc_ref[...] = jnp.zeros_like(dsc_ref)
            dg_ref[...] = jnp.zeros_like(dg_ref)

        @pl.when(i == 0)
        def _():
            dnw_ref[...] = jnp.zeros_like(dnw_ref)

        dxo_v = dxo_ref[...]
        dfb = (0.5 * g_ref[0] * dxo_v).astype(BF16)
        dg_ref[0] += _colsum(0.5 * dxo_v * f_ref[...].astype(F32))
        dh = jnp.zeros((tm, D), F32)
        for j in range(N_CHIPS):
            ds = _dot_nt(dfb, wd_ref[j])
            av = a_ref[j].astype(F32)
            uv = u_ref[j].astype(F32)
            sig = _sigmoid(av)
            sil = av * sig
            dab = (ds * uv * (sig * (1.0 + av * (1.0 - sig)))).astype(BF16)
            dub = (ds * sil).astype(BF16)
            dh = dh + _dot(dab, wg_ref[j]) + _dot(dub, wu_ref[j])
            s_ref[j] = (sil * uv).astype(BF16)
            da_ref[j] = dab
            du_ref[j] = dub
        xv = x_ref[...]
        xh, r = _rms_stats(xv)
        nwv = nw_ref[...]
        n = xh * nwv
        scale1 = 1.0 + sc_ref[0]
        dsc_ref[0] += _colsum(dh * n)
        dsh_ref[0] += _colsum(dh)
        dx, dw_rows = _rms_bwd(dh * scale1, xh, r, nwv)
        dnw_ref[...] += _colsum(dw_rows)
        dx_ref[...] = dxo_v + dx
        h_ref[...] = (n * scale1 + sh_ref[0]).astype(BF16)
        df_ref[...] = dfb

    rows = lambda n: pl.BlockSpec((tm, n), lambda i: (i, 0))
    act = pl.BlockSpec((N_CHIPS, tm, fs), lambda i: (0, i, 0))
    perb = pl.BlockSpec((1, 1, D), lambda i: (i // tps, 0, 0))
    sd = jax.ShapeDtypeStruct
    return pl.pallas_call(
        body, grid=(T // tm,), name=name,
        in_specs=[rows(D), rows(D), _resident((1, D)), perb, perb, perb, act, act, rows(D),
                  _resident((N_CHIPS, fs, D)), _resident((N_CHIPS, fs, D)), _resident((N_CHIPS, fs, D))],
        out_specs=[rows(D), rows(D), act, rows(D), act, act, perb, perb, perb, pl.BlockSpec((1, D), lambda i: (0, 0))],
        out_shape=[sd((T, D), F32), sd((T, D), BF16), sd((N_CHIPS, T, fs), BF16), sd((T, D), BF16),
                   sd((N_CHIPS, T, fs), BF16), sd((N_CHIPS, T, fs), BF16), sd((B, 1, D), F32), sd((B, 1, D), F32),
                   sd((B, 1, D), F32), sd((1, D), F32)],
        compiler_params=_cparams(("arbitrary",)),
    )(dxo, x, nw, sh, sc, g, a, u, f, wg, wu, wd)


def _ffn_wgrad(h, s, df, da, du, after, name):
    T, D = h.shape
    fs = s.shape[2]
    tt = min(WGRAD_TOKEN_TILE, T)
    nt = T // tt

    def body(h_ref, s_ref, df_ref, da_ref, du_ref, after_ref, dgate_ref, dup_ref, ddown_ref, gate_acc, up_acc, down_acc):
        @pl.when(pl.program_id(1) == 0)
        def _():
            gate_acc[...] = jnp.zeros_like(gate_acc)
            up_acc[...] = jnp.zeros_like(up_acc)
            down_acc[...] = jnp.zeros_like(down_acc)

        hv = h_ref[...]
        gate_acc[...] += _dot_tn(da_ref[0], hv)
        up_acc[...] += _dot_tn(du_ref[0], hv)
        down_acc[...] += _dot_tn(s_ref[0], df_ref[...])

        @pl.when(pl.program_id(1) == nt - 1)
        def _():
            dgate_ref[0] = gate_acc[...].astype(BF16)
            dup_ref[0] = up_acc[...].astype(BF16)
            ddown_ref[0] = down_acc[...].astype(BF16)

    rows = pl.BlockSpec((tt, D), lambda j, t: (t, 0))
    act = pl.BlockSpec((1, tt, fs), lambda j, t: (j, t, 0))
    shard = pl.BlockSpec((1, fs, D), lambda j, t: (j, 0, 0))
    return pl.pallas_call(
        body, grid=(N_CHIPS, nt), name=name,
        in_specs=[rows, act, rows, act, act, pl.BlockSpec(memory_space=pl.ANY)],
        out_specs=[shard] * 3, out_shape=[jax.ShapeDtypeStruct((N_CHIPS, fs, D), BF16)] * 3,
        scratch_shapes=[pltpu.VMEM((fs, D), F32)] * 3,
        compiler_params=_cparams(("arbitrary", "arbitrary")),
    )(h, s, df, da, du, after)


def _mm_tn(xa, ya, tn, name):
    T, K = xa.shape
    N = ya.shape[1]
    tt = min(WGRAD_TOKEN_TILE, T)
    nt = T // tt

    def body(x_ref, y_ref, o_ref, acc_ref):
        @pl.when(pl.program_id(1) == 0)
        def _():
            acc_ref[...] = jnp.zeros_like(acc_ref)

        acc_ref[...] += _dot_tn(x_ref[...], y_ref[...])

        @pl.when(pl.program_id(1) == nt - 1)
        def _():
            o_ref[...] = acc_ref[...].astype(BF16)

    return pl.pallas_call(
        body, grid=(N // tn, nt), name=name,
        in_specs=[pl.BlockSpec((tt, K), lambda j, t: (t, 0)), pl.BlockSpec((tt, tn), lambda j, t: (t, j))],
        out_specs=pl.BlockSpec((K, tn), lambda j, t: (0, j)),
        out_shape=jax.ShapeDtypeStruct((K, N), BF16),
        scratch_shapes=[pltpu.VMEM((K, tn), F32)],
        compiler_params=_cparams(("arbitrary", "arbitrary")),
    )(xa, ya)


_PROJ_SPLITS = (0, 1024, 2560, 2944, 3200, 3328)


def _inproj_fwd(x, nw, sh, sc, win, seq):
    T, D = x.shape
    tm = min(WIDE_TOKEN_TILE, seq)
    tps = seq // tm
    widths = [b - a for a, b in zip(_PROJ_SPLITS[:-1], _PROJ_SPLITS[1:])]
    dtypes = [BF16, BF16, F32, F32, F32]

    def body(x_ref, nw_ref, sh_ref, sc_ref, w_ref, *outs):
        xh, _ = _rms_stats(x_ref[...])
        h = (xh * nw_ref[...]) * (1.0 + sc_ref[0]) + sh_ref[0]
        proj = _dot_nt(h.astype(BF16), w_ref[...])
        for o, lo, hi in zip(outs, _PROJ_SPLITS[:-1], _PROJ_SPLITS[1:]):
            o[...] = proj[:, lo:hi].astype(o.dtype)

    rows = lambda n: pl.BlockSpec((tm, n), lambda i: (i, 0))
    perb = pl.BlockSpec((1, 1, D), lambda i: (i // tps, 0, 0))
    return pl.pallas_call(
        body, grid=(T // tm,), name="inproj_fwd",
        in_specs=[rows(D), _resident((1, D)), perb, perb, _resident((D_PROJ, D))],
        out_specs=[rows(w) for w in widths],
        out_shape=[jax.ShapeDtypeStruct((T, w), dt) for w, dt in zip(widths, dtypes)],
        compiler_params=_cparams(("arbitrary",)),
    )(x, nw, sh, sc, win)


def _inproj_bwd(dx2, x, nw, sh, sc, win, dz, dxbc, dcq, dckv, ddtk_a, ddtk_b, seq):
    T, D = x.shape
    B = T // seq
    tm = min(TOKEN_TILE, seq)
    tps = seq // tm

    def body(dx2_ref, x_ref, nw_ref, sh_ref, sc_ref, w_ref, dz_ref, dxbc_ref, dcq_ref, dckv_ref, da_ref, db_ref,
             dx_ref, h_ref, dp_ref, dsh_ref, dsc_ref, dnw_ref):
        i = pl.program_id(0)

        @pl.when(i % tps == 0)
        def _():
            dsh_ref[...] = jnp.zeros_like(dsh_ref)
            dsc_ref[...] = jnp.zeros_like(dsc_ref)

        @pl.when(i == 0)
        def _():
            dnw_ref[...] = jnp.zeros_like(dnw_ref)

        dproj = jnp.concatenate(
            [dz_ref[...], dxbc_ref[...], dcq_ref[...].astype(BF16), dckv_ref[...].astype(BF16),
             (da_ref[...] + db_ref[...]).astype(BF16)], axis=1)
        dp_ref[...] = dproj
        dh = _dot(dproj, w_ref[...])
        xh, r = _rms_stats(x_ref[...])
        nwv = nw_ref[...]
        n = xh * nwv
        scale1 = 1.0 + sc_ref[0]
        dsc_ref[0] += _colsum(dh * n)
        dsh_ref[0] += _colsum(dh)
        dx, dw_rows = _rms_bwd(dh * scale1, xh, r, nwv)
        dnw_ref[...] += _colsum(dw_rows)
        dx_ref[...] = dx2_ref[...] + dx
        h_ref[...] = (n * scale1 + sh_ref[0]).astype(BF16)

    rows = lambda n: pl.BlockSpec((tm, n), lambda i: (i, 0))
    perb = pl.BlockSpec((1, 1, D), lambda i: (i // tps, 0, 0))
    sd = jax.ShapeDtypeStruct
    return pl.pallas_call(
        body, grid=(T // tm,), name="inproj_bwd",
        in_specs=[rows(D), rows(D), _resident((1, D)), perb, perb, _resident((D_PROJ, D)),
                  rows(1024), rows(D_CONV), rows(Q_LORA), rows(KV_LORA), rows(LANES), rows(LANES)],
        out_specs=[rows(D), rows(D), rows(D_PROJ), perb, perb, pl.BlockSpec((1, D), lambda i: (0, 0))],
        out_shape=[sd((T, D), F32), sd((T, D), BF16), sd((T, D_PROJ), BF16), sd((B, 1, D), F32), sd((B, 1, D), F32),
                   sd((1, D), F32)],
        compiler_params=_cparams(("arbitrary",)),
    )(dx2, x, nw, sh, sc, win, dz, dxbc, dcq, dckv, ddtk_a, ddtk_b)


SUBLANES = 8


def _shift_down(v, k):
    r = pltpu.roll(v, k, 0)
    row = lax.broadcasted_iota(jnp.int32, (SUBLANES, v.shape[1]), 0)
    return jnp.concatenate([jnp.where(row < k, 0.0, r[:SUBLANES]), r[SUBLANES:]], axis=0)


def _shift_up(v, k):
    n = v.shape[0]
    r = pltpu.roll(v, n - k, 0)
    row = lax.broadcasted_iota(jnp.int32, (SUBLANES, v.shape[1]), 0)
    return jnp.concatenate([r[:n - SUBLANES], jnp.where(row >= SUBLANES - k, 0.0, r[n - SUBLANES:])], axis=0)


def _conv_pre(xv, w_ref, b_ref):
    pre = b_ref[...] + w_ref[CONV_WIDTH - 1:CONV_WIDTH, :] * xv
    for k in range(1, CONV_WIDTH):
        pre = pre + w_ref[CONV_WIDTH - 1 - k:CONV_WIDTH - k, :] * _shift_down(xv, k)
    return pre


def _conv_fwd(xraw, cw, cb):
    B, S, C = xraw.shape

    def body(x_ref, w_ref, b_ref, o_ref):
        pre = _conv_pre(x_ref[0].astype(F32), w_ref, b_ref)
        o_ref[0] = (pre * _sigmoid(pre)).astype(BF16)

    blk = pl.BlockSpec((1, S, LANES), lambda b, j: (b, 0, j))
    return pl.pallas_call(
        body, grid=(B, C // LANES), name="conv_fwd",
        in_specs=[blk, pl.BlockSpec((CONV_WIDTH, LANES), lambda b, j: (0, j)), pl.BlockSpec((1, LANES), lambda b, j: (0, j))],
        out_specs=blk, out_shape=jax.ShapeDtypeStruct((B, S, C), BF16),
        compiler_params=_cparams(("arbitrary", "arbitrary")),
    )(xraw, cw, cb)


def _conv_bwd(dout, xraw, cw, cb):
    B, S, C = xraw.shape

    def body(d_ref, x_ref, w_ref, b_ref, dx_ref, dw_ref, db_ref):
        @pl.when(pl.program_id(1) == 0)
        def _():
            dw_ref[...] = jnp.zeros_like(dw_ref)
            db_ref[...] = jnp.zeros_like(db_ref)

        xv = x_ref[0].astype(F32)
        pre = _conv_pre(xv, w_ref, b_ref)
        sig = _sigmoid(pre)
        dpre = d_ref[0].astype(F32) * (sig * (1.0 + pre * (1.0 - sig)))
        dx = w_ref[CONV_WIDTH - 1:CONV_WIDTH, :] * dpre
        for k in range(1, CONV_WIDTH):
            dx = dx + w_ref[CONV_WIDTH - 1 - k:CONV_WIDTH - k, :] * _shift_up(dpre, k)
        dx_ref[0] = dx.astype(BF16)
        db_ref[...] += _colsum(dpre)
        dws = [_colsum(dpre * (xv if k == 0 else _shift_down(xv, k))) for k in range(CONV_WIDTH - 1, -1, -1)]
        dw_ref[...] += jnp.concatenate(dws, axis=0)

    blk = pl.BlockSpec((1, S, LANES), lambda j, b: (b, 0, j))
    wspec = pl.BlockSpec((CONV_WIDTH, LANES), lambda j, b: (0, j))
    bspec = pl.BlockSpec((1, LANES), lambda j, b: (0, j))
    return pl.pallas_call(
        body, grid=(C // LANES, B), name="conv_bwd",
        in_specs=[blk, blk, wspec, bspec], out_specs=[blk, wspec, bspec],
        out_shape=[jax.ShapeDtypeStruct((B, S, C), BF16), jax.ShapeDtypeStruct((CONV_WIDTH, C), F32),
                   jax.ShapeDtypeStruct((1, C), F32)],
        compiler_params=_cparams(("arbitrary", "arbitrary")),
    )(dout, xraw, cw, cb)


def _softplus(x):
    return jnp.maximum(x, 0.0) + jnp.log(1.0 + jnp.exp(-jnp.abs(x)))


def _ssd_common(xbc_ref, dtk_ref, dtb_ref, alog_ref, e_ref):
    L = CHUNK
    xbc = xbc_ref[0]
    xs = xbc[:, :D_SSD].astype(F32)
    bm = xbc[:, D_SSD:D_SSD + 256]
    cm = xbc[:, D_SSD + 256:D_SSD + 512]
    head = lax.broadcasted_iota(jnp.int32, (1, LANES), 1) < SSD_HEADS
    a128 = jnp.where(head, -jnp.exp(alog_ref[...]), 0.0)
    pre = dtk_ref[0] + dtb_ref[...]
    dt = _softplus(pre)
    dA = dt * a128
    row = lax.broadcasted_iota(jnp.int32, (L, L), 0)
    col = lax.broadcasted_iota(jnp.int32, (L, L), 1)
    causal = col <= row
    tri = causal.astype(F32)
    triT = (row <= col).astype(F32)
    tri = causal.astype(BF16)
    triT = (row <= col).astype(BF16)
    dA3 = _split3(dA)
    acum = _sum3(lambda part: _dot(tri, part), dA3)
    acumT = _sum3(lambda part: _dot_tn(part, triT), dA3)
    E = e_ref[...]
    acum_f = _spread(acum, E)
    dt_f = _spread(dt, E)
    e_f = jnp.exp(acum_f)
    w_f = jnp.exp(acum_f[L - 1:L, :] - acum_f)
    xt = xs * dt_f
    return dict(xs=xs, bm=bm, cm=cm, a128=a128, pre=pre, dt=dt, causal=causal, tri=tri, triT=triT, acum=acum,
                acumT=acumT, E=E, dt_f=dt_f, e_f=e_f, w_f=w_f, xt=xt, head=head)


def _split3(x):
    p1 = x.astype(BF16)
    r1 = x - p1.astype(F32)
    p2 = r1.astype(BF16)
    return p1, p2, (r1 - p2.astype(F32)).astype(BF16)


def _sum3(mm, parts):
    return (mm(parts[0]) + mm(parts[1])) + mm(parts[2])


def _spread(v, e):
    return _sum3(lambda part: _dot(part, e), _split3(v))


def _gather_heads(v, e):
    return _sum3(lambda part: _dot_nt(part, e), _split3(v))


def _head_mask(k):
    lane = lax.broadcasted_iota(jnp.int32, (CHUNK, LANES), 1)
    return (lane >= SSD_HEAD_DIM) if k == 1 else (lane < SSD_HEAD_DIM)


def _pair_decay(alast, h0):
    row = lax.broadcasted_iota(jnp.int32, (2 * SSD_HEAD_DIM, SSD_STATE), 0)
    return jnp.exp(jnp.where(row < SSD_HEAD_DIM, alast[:, h0:h0 + 1], alast[:, h0 + 1:h0 + 2]))


def _decay_matrix(q, h):
    seg = q["acum"][:, h:h + 1] - q["acumT"][h:h + 1, :]
    return jnp.exp(jnp.where(q["causal"], seg, -1e30))


def _gated_norm(y, zz, nw):
    sig = _sigmoid(zz)
    sil = zz * sig
    yg = y * sil
    half = D_SSD // SSD_GROUPS
    parts = []
    for g in range(SSD_GROUPS):
        xh, r = _rms_stats(yg[:, g * half:(g + 1) * half])
        parts.append((xh, r))
    return sig, sil, parts


def _ssd_fwd(xbc, dtk, z, dtb, alog, dsk, nw, expand):
    B, S, _ = xbc.shape
    L = CHUNK
    nc = S // L

    def body(xbc_ref, dtk_ref, z_ref, dtb_ref, alog_ref, dsk_ref, nw_ref, e_ref, y_ref, ys_ref, prev_ref, st_ref):
        @pl.when(pl.program_id(0) == 0)
        def _():
            st_ref[...] = jnp.zeros_like(st_ref)

        for b in range(B):
            one = lambda ref: ref.at[pl.ds(b, 1)]
            sequence_step(one(xbc_ref), one(dtk_ref), one(z_ref), dtb_ref, alog_ref, dsk_ref, nw_ref, e_ref, one(y_ref),
                          one(ys_ref), one(prev_ref), st_ref.at[b])

    def sequence_step(xbc_ref, dtk_ref, z_ref, dtb_ref, alog_ref, dsk_ref, nw_ref, e_ref, y_ref, ys_ref, prev_ref, st_ref):
        q = _ssd_common(xbc_ref, dtk_ref, dtb_ref, alog_ref, e_ref)
        xtb = q["xt"].astype(BF16)
        xwb = (q["xt"] * q["w_f"]).astype(BF16)
        alast = q["acum"][L - 1:L, :]
        ys = []
        for g in range(SSD_GROUPS):
            bg = q["bm"][:, g * 128:(g + 1) * 128]
            cg = q["cm"][:, g * 128:(g + 1) * 128]
            G = _dot_nt(cg, bg)
            for pr in range(SSD_HEADS // SSD_GROUPS // 2):
                h0 = g * 8 + 2 * pr
                lo = h0 * SSD_HEAD_DIM
                xt_p = xtb[:, lo:lo + 128]
                ydiag = jnp.zeros((L, LANES), F32)
                for k in range(2):
                    M = (G * _decay_matrix(q, h0 + k)).astype(BF16)
                    ydiag = ydiag + _dot(M, jnp.where(_head_mask(k), xt_p, jnp.zeros_like(xt_p)))
                hp = st_ref[lo:lo + 128, :]
                prev_ref[0, 0, lo:lo + 128, :] = hp.astype(BF16)
                zoff = _dot_nt(cg, hp.astype(BF16))
                ys.append(ydiag + zoff * q["e_f"][:, lo:lo + 128])
                st_ref[lo:lo + 128, :] = _pair_decay(alast, h0) * hp + _dot_tn(xwb[:, lo:lo + 128], bg)
        y = jnp.concatenate(ys, axis=1) + dsk_ref[...] * q["xs"]
        y_ref[0] = y.astype(BF16)
        _, _, parts = _gated_norm(y, z_ref[0].astype(F32), nw_ref[...])
        half = D_SSD // SSD_GROUPS
        ys_ref[0] = jnp.concatenate(
            [xh * nw_ref[:, g * half:(g + 1) * half] for g, (xh, _) in enumerate(parts)], axis=1).astype(BF16)

    chunk = lambda n: pl.BlockSpec((B, L, n), lambda c: (0, c, 0))
    vec = pl.BlockSpec((1, LANES), lambda c: (0, 0))
    return pl.pallas_call(
        body, grid=(nc,), name="ssd_fwd",
        in_specs=[chunk(D_CONV), chunk(LANES), chunk(D_SSD), vec, vec, pl.BlockSpec((1, D_SSD), lambda c: (0, 0)),
                  pl.BlockSpec((1, D_SSD), lambda c: (0, 0)), pl.BlockSpec((LANES, D_SSD), lambda c: (0, 0))],
        out_specs=[chunk(D_SSD), chunk(D_SSD), pl.BlockSpec((B, 1, D_SSD, SSD_STATE), lambda c: (0, c, 0, 0))],
        out_shape=[jax.ShapeDtypeStruct((B, S, D_SSD), BF16), jax.ShapeDtypeStruct((B, S, D_SSD), BF16),
                   jax.ShapeDtypeStruct((B, nc, D_SSD, SSD_STATE), BF16)],
        scratch_shapes=[pltpu.VMEM((B, D_SSD, SSD_STATE), F32)],
        compiler_params=_cparams(("arbitrary",)),
    )(xbc, dtk, z, dtb, alog, dsk, nw, expand)


def _ssd_bwd(xbc, dtk, z, y, prev, dys, dtb, alog, dsk, nw, expand):
    B, S, _ = xbc.shape
    L = CHUNK
    nc = S // L
    half = D_SSD // SSD_GROUPS

    def body(xbc_ref, dtk_ref, z_ref, y_ref, prev_ref, dys_ref, dtb_ref, alog_ref, dsk_ref, nw_ref, e_ref,
             dxbc_ref, ddtk_ref, dz_ref, dnw_ref, dvec_ref, dh_ref, dskc_ref):
        @pl.when(pl.program_id(0) == 0)
        def _():
            dnw_ref[...] = jnp.zeros_like(dnw_ref)
            dvec_ref[...] = jnp.zeros_like(dvec_ref)
            dskc_ref[...] = jnp.zeros_like(dskc_ref)
            dh_ref[...] = jnp.zeros_like(dh_ref)

        for b in range(B):
            one = lambda ref: ref.at[pl.ds(b, 1)]
            sequence_step(one(xbc_ref), one(dtk_ref), one(z_ref), one(y_ref), one(prev_ref), one(dys_ref), dtb_ref, alog_ref,
                          dsk_ref, nw_ref, e_ref, one(dxbc_ref), one(ddtk_ref), one(dz_ref), dnw_ref, dvec_ref, dh_ref.at[b],
                          dskc_ref)

        @pl.when(pl.program_id(0) == nc - 1)
        def _():
            dvec_ref[2:3, :] = _gather_heads(jnp.broadcast_to(dskc_ref[...], (8, D_SSD)), e_ref[...])[0:1, :]

    def sequence_step(xbc_ref, dtk_ref, z_ref, y_ref, prev_ref, dys_ref, dtb_ref, alog_ref, dsk_ref, nw_ref, e_ref,
                      dxbc_ref, ddtk_ref, dz_ref, dnw_ref, dvec_ref, dh_ref, dskc_ref):
        q = _ssd_common(xbc_ref, dtk_ref, dtb_ref, alog_ref, e_ref)
        E = q["E"]
        xs = q["xs"]
        yv = y_ref[0].astype(F32)
        zz = z_ref[0].astype(F32)
        sig, sil, parts = _gated_norm(yv, zz, nw_ref[...])
        dn = dys_ref[0].astype(F32)
        dyg, dnw_rows = [], []
        for g, (xh, r) in enumerate(parts):
            dpart, dw_rows = _rms_bwd(dn[:, g * half:(g + 1) * half], xh, r, nw_ref[:, g * half:(g + 1) * half])
            dyg.append(dpart)
            dnw_rows.append(dw_rows)
        dyg = jnp.concatenate(dyg, axis=1)
        dnw_ref[...] += _colsum(jnp.concatenate(dnw_rows, axis=1))
        dY = dyg * sil
        dz_ref[0] = (dyg * yv * (sig * (1.0 + zz * (1.0 - sig)))).astype(BF16)
        dsk_f = dsk_ref[...]
        dskc_ref[...] += _colsum(dY * xs)
        dYb = dY.astype(BF16)
        xtb = q["xt"].astype(BF16)
        xwb = (q["xt"] * q["w_f"]).astype(BF16)
        acum = q["acum"]
        alast = acum[L - 1:L, :]
        lane_id = lax.broadcasted_iota(jnp.int32, (L, LANES), 1)
        sub_id = lax.broadcasted_iota(jnp.int32, (LANES, L), 0)
        lane_row = lax.broadcasted_iota(jnp.int32, (1, LANES), 1)
        da_rows = jnp.zeros((L, LANES), F32)
        daT = jnp.zeros((LANES, L), F32)
        dxt, prod_off, prod_st, dbs, dcs = [], [], [], [], []
        hsum_row = jnp.zeros((1, LANES), F32)
        for g in range(SSD_GROUPS):
            bg = q["bm"][:, g * 128:(g + 1) * 128]
            cg = q["cm"][:, g * 128:(g + 1) * 128]
            G = _dot_nt(cg, bg)
            dG = jnp.zeros((L, L), F32)
            dcg = jnp.zeros((L, SSD_STATE), F32)
            dbg = jnp.zeros((L, SSD_STATE), F32)
            for pr in range(SSD_HEADS // SSD_GROUPS // 2):
                h0 = g * 8 + 2 * pr
                lo = h0 * SSD_HEAD_DIM
                cols = slice(lo, lo + 128)
                dY_p = dYb[:, cols]
                xt_p = xtb[:, cols]
                dxt_p = jnp.zeros((L, LANES), F32)
                for k in range(2):
                    h = h0 + k
                    Lm = _decay_matrix(q, h)
                    Mf = G * Lm
                    dYk = jnp.where(_head_mask(k), dY_p, jnp.zeros_like(dY_p))
                    dM = _dot_nt(dYk, xt_p)
                    dxt_p = dxt_p + _dot_tn(Mf.astype(BF16), dYk)
                    dG = dG + dM * Lm
                    Q = dM * Mf
                    da_rows = da_rows + jnp.where(lane_id == h, jnp.sum(Q, axis=1, keepdims=True), 0.0)
                    daT = daT + jnp.where(sub_id == h, jnp.sum(Q, axis=0, keepdims=True), 0.0)
                hpb = prev_ref[0, 0, lo:lo + 128, :]
                hp = hpb.astype(F32)
                zoff = _dot_nt(cg, hpb)
                e_p = q["e_f"][:, cols]
                dY_pf = dY[:, cols]
                dZb = (dY_pf * e_p).astype(BF16)
                dcg = dcg + _dot(dZb, hpb)
                dhp_off = _dot_tn(dZb, cg)
                prod_off.append(dY_pf * zoff * e_p)
                dS = dh_ref[lo:lo + 128, :]
                dSb = dS.astype(BF16)
                U = _dot_nt(bg, dSb)
                dxt_p = dxt_p + U * q["w_f"][:, cols]
                dbg = dbg + _dot(xwb[:, cols], dSb)
                prod_st.append(q["xt"][:, cols] * U)
                dh_ref[lo:lo + 128, :] = _pair_decay(alast, h0) * dS + dhp_off
                dsh = dS * hp
                for k in range(2):
                    total = jnp.sum(dsh[k * SSD_HEAD_DIM:(k + 1) * SSD_HEAD_DIM, :], axis=(0, 1), keepdims=True)
                    hsum_row = hsum_row + jnp.where(lane_row == h0 + k, total, 0.0)
                dxt.append(dxt_p)
            dGb = dG.astype(BF16)
            dcs.append(dcg + _dot(dGb, bg))
            dbs.append(dbg + _dot_tn(dGb, cg))
        dxt = jnp.concatenate(dxt, axis=1)
        da_rows = da_rows + _gather_heads(jnp.concatenate(prod_off, axis=1), E)
        dww = _gather_heads(jnp.concatenate(prod_st, axis=1), E) * jnp.exp(alast - acum)
        da_rows = da_rows - dww
        dlast = _colsum(dww) + jnp.exp(alast) * hsum_row
        triT = q["triT"]
        ddA = (_sum3(lambda part: _dot(triT, part), _split3(da_rows))
               - _sum3(lambda part: _dot_nt(triT, part), _split3(daT)) + dlast)
        ddA = jnp.where(q["head"], ddA, 0.0)
        ddt = ddA * q["a128"] + _gather_heads(dxt * xs, E)
        ddt_raw = jnp.where(q["head"], ddt * _sigmoid(q["pre"]), 0.0)
        ddtk_ref[0] = ddt_raw
        dxs = dxt * q["dt_f"] + dsk_f * dY
        dxbc_ref[0] = jnp.concatenate([dxs] + dbs + dcs, axis=1).astype(BF16)
        dvec_ref[0:1, :] += _colsum(ddt_raw)
        dvec_ref[1:2, :] += _colsum(ddA * q["dt"]) * q["a128"]

    rev = lambda n: pl.BlockSpec((B, L, n), lambda c: (0, nc - 1 - c, 0))
    vec = pl.BlockSpec((1, LANES), lambda c: (0, 0))
    sd = jax.ShapeDtypeStruct
    return pl.pallas_call(
        body, grid=(nc,), name="ssd_bwd",
        in_specs=[rev(D_CONV), rev(LANES), rev(D_SSD), rev(D_SSD),
                  pl.BlockSpec((B, 1, D_SSD, SSD_STATE), lambda c: (0, nc - 1 - c, 0, 0)), rev(D_SSD), vec, vec,
                  pl.BlockSpec((1, D_SSD), lambda c: (0, 0)),
                  pl.BlockSpec((1, D_SSD), lambda c: (0, 0)), pl.BlockSpec((LANES, D_SSD), lambda c: (0, 0))],
        out_specs=[rev(D_CONV), rev(LANES), rev(D_SSD), pl.BlockSpec((1, D_SSD), lambda c: (0, 0)),
                   pl.BlockSpec((8, LANES), lambda c: (0, 0))],
        out_shape=[sd((B, S, D_CONV), BF16), sd((B, S, LANES), F32), sd((B, S, D_SSD), BF16), sd((1, D_SSD), F32),
                   sd((8, LANES), F32)],
        scratch_shapes=[pltpu.VMEM((B, D_SSD, SSD_STATE), F32), pltpu.VMEM((1, D_SSD), F32)],
        compiler_params=_cparams(("arbitrary",)),
    )(xbc, dtk, z, y, prev, dys, dtb, alog, dsk, nw, expand)


def _rope_tables(pos_ref, invf_ref, place_ref):
    ang = invf_ref[...] * pos_ref[0].astype(F32)
    place = place_ref[...]
    cosf = 1.0 + _sum3(lambda part: _dot_tn(part, place), _split3(jnp.cos(ang) - 1.0))
    sinf = _sum3(lambda part: _dot_tn(part, place), _split3(jnp.sin(ang)))
    return cosf, sinf


def _rot(u):
    lane = lax.broadcasted_iota(jnp.int32, u.shape, 1)
    first = (lane >= QK_NOPE) & (lane < QK_NOPE + QK_ROPE // 2)
    second = (lane >= QK_NOPE + QK_ROPE // 2) & (lane < QK_DIM)
    return jnp.where(first, -pltpu.roll(u, LANES - QK_ROPE // 2, 1), jnp.where(second, pltpu.roll(u, QK_ROPE // 2, 1), 0.0))


def _rope_lanes(shape):
    lane = lax.broadcasted_iota(jnp.int32, shape, 1)
    return (lane >= QK_NOPE) & (lane < QK_DIM)


def _mla_prep(cq, ckv, dtk, pos, qw, kvw, wuq, wukv, invf, place):
    T = cq.shape[0]
    tm = min(WIDE_TOKEN_TILE, T)
    scale = 1.0 / math.sqrt(QK_DIM)
    HW = MLA_HEADS * HEAD_LANES

    def body(cq_ref, ckv_ref, dtk_ref, pos_ref, qw_ref, kvw_ref, wuq_ref, wukv_ref, invf_ref, place_ref, q_ref, k_ref, v_ref,
             cos_ref, sin_ref):
        xh, _ = _rms_stats(cq_ref[...])
        qv = _dot((xh * qw_ref[...]).astype(BF16), wuq_ref[...])
        xh, _ = _rms_stats(ckv_ref[...])
        kv = _dot((xh * kvw_ref[...]).astype(BF16), wukv_ref[...])
        cosf, sinf = _rope_tables(pos_ref, invf_ref, place_ref)
        cos_ref[...] = cosf
        sin_ref[...] = sinf
        rope = lambda u: u * cosf + _rot(u) * sinf
        dtkv = dtk_ref[...]
        kr = rope(jnp.where(_rope_lanes(dtkv.shape), dtkv, 0.0))
        for h in range(MLA_HEADS):
            cols = slice(h * HEAD_LANES, (h + 1) * HEAD_LANES)
            q_ref[:, cols] = (rope(qv[:, cols]) * scale).astype(BF16)
            k_ref[:, cols] = (kv[:, cols] + kr).astype(BF16)
        v_ref[...] = kv[:, HW:].astype(BF16)

    rows = lambda n: pl.BlockSpec((tm, n), lambda i: (i, 0))
    return pl.pallas_call(
        body, grid=(T // tm,), name="mla_prep",
        in_specs=[rows(Q_LORA), rows(KV_LORA), rows(LANES), pl.BlockSpec((1, 1, tm), lambda i: (i, 0, 0)),
                  _resident((1, Q_LORA)), _resident((1, KV_LORA)), _resident((Q_LORA, HW)), _resident((KV_LORA, 2 * HW)),
                  _resident((QK_ROPE // 2, 1)), _resident((QK_ROPE // 2, LANES))],
        out_specs=[rows(HW), rows(HW), rows(HW), rows(LANES), rows(LANES)],
        out_shape=[jax.ShapeDtypeStruct((T, HW), BF16)] * 3 + [jax.ShapeDtypeStruct((T, LANES), F32)] * 2,
        compiler_params=_cparams(("arbitrary",)),
    )(cq, ckv, dtk, pos.reshape(T // tm, 1, tm), qw, kvw, wuq, wukv, invf, place)


def _mla_prep_bwd(dq, dk, dv, cq, ckv, cos_t, sin_t, qw, kvw, wuq, wukv):
    T = cq.shape[0]
    tm = min(WIDE_TOKEN_TILE, T)
    scale = 1.0 / math.sqrt(QK_DIM)
    HW = MLA_HEADS * HEAD_LANES

    def body(dq_ref, dk_ref, dv_ref, cq_ref, ckv_ref, cos_ref, sin_ref, qw_ref, kvw_ref, wuq_ref, wukv_ref,
             dcq_ref, dckv_ref, ddtk_ref, qn_ref, kvn_ref, dqo_ref, dkvo_ref, dqw_ref, dkvw_ref):
        @pl.when(pl.program_id(0) == 0)
        def _():
            dqw_ref[...] = jnp.zeros_like(dqw_ref)
            dkvw_ref[...] = jnp.zeros_like(dkvw_ref)

        cosf, sinf = cos_ref[...], sin_ref[...]
        unrope = lambda d: d * cosf - _rot(d * sinf)
        dkr = jnp.zeros((tm, LANES), F32)
        nope = lax.broadcasted_iota(jnp.int32, (tm, LANES), 1) < QK_NOPE
        for h in range(MLA_HEADS):
            cols = slice(h * HEAD_LANES, (h + 1) * HEAD_LANES)
            dqo_ref[:, cols] = unrope(dq_ref[:, cols].astype(F32) * scale).astype(BF16)
            dkh = dk_ref[:, cols].astype(F32)
            dkr = dkr + jnp.where(_rope_lanes(dkh.shape), dkh, 0.0)
            dkvo_ref[:, cols] = jnp.where(nope, dkh, 0.0).astype(BF16)
        dkvo_ref[:, HW:] = dv_ref[...].astype(BF16)
        ddtk_ref[...] = unrope(dkr)
        xh, r = _rms_stats(cq_ref[...])
        qn_ref[...] = (xh * qw_ref[...]).astype(BF16)
        dx, dw_rows = _rms_bwd(_dot_nt(dqo_ref[...], wuq_ref[...]), xh, r, qw_ref[...])
        dcq_ref[...] = dx
        dqw_ref[...] += _colsum(dw_rows)
        xh, r = _rms_stats(ckv_ref[...])
        kvn_ref[...] = (xh * kvw_ref[...]).astype(BF16)
        dx, dw_rows = _rms_bwd(_dot_nt(dkvo_ref[...], wukv_ref[...]), xh, r, kvw_ref[...])
        dckv_ref[...] = dx
        dkvw_ref[...] += _colsum(dw_rows)

    rows = lambda n: pl.BlockSpec((tm, n), lambda i: (i, 0))
    sd = jax.ShapeDtypeStruct
    return pl.pallas_call(
        body, grid=(T // tm,), name="mla_prep_bwd",
        in_specs=[rows(HW), rows(HW), rows(HW), rows(Q_LORA), rows(KV_LORA), rows(LANES), rows(LANES), _resident((1, Q_LORA)),
                  _resident((1, KV_LORA)), _resident((Q_LORA, HW)), _resident((KV_LORA, 2 * HW))],
        out_specs=[rows(Q_LORA), rows(KV_LORA), rows(LANES), rows(Q_LORA), rows(KV_LORA), rows(HW), rows(2 * HW),
                   pl.BlockSpec((1, Q_LORA), lambda i: (0, 0)), pl.BlockSpec((1, KV_LORA), lambda i: (0, 0))],
        out_shape=[sd((T, Q_LORA), F32), sd((T, KV_LORA), F32), sd((T, LANES), F32), sd((T, Q_LORA), BF16),
                   sd((T, KV_LORA), BF16), sd((T, HW), BF16), sd((T, 2 * HW), BF16), sd((1, Q_LORA), F32),
                   sd((1, KV_LORA), F32)],
        compiler_params=_cparams(("arbitrary",)),
    )(dq, dk, dv, cq, ckv, cos_t, sin_t, qw, kvw, wuq, wukv)


def _causal_mask(t):
    row = lax.broadcasted_iota(jnp.int32, (t, t), 0)
    col = lax.broadcasted_iota(jnp.int32, (t, t), 1)
    return col <= row


def _attn_fwd(q, k, v):
    B, S, HW = q.shape
    H = HW // HEAD_LANES
    t = min(ATTN_FWD_Q_TILE, S)
    tk = min(ATTN_FWD_KV_TILE, t)
    nq = S // t
    per = t // tk

    pair = 4
    pw = pair * HEAD_LANES

    def body(q_ref, k_ref, v_ref, o_ref, lse_ref):
        qi = pl.program_id(2)
        lanes = [slice(hh * HEAD_LANES, (hh + 1) * HEAD_LANES) for hh in range(pair)]
        qs = [q_ref[0, :, cols] for cols in lanes]

        def step(j, carry, diag):
            sl = pl.ds(pl.multiple_of(j * tk, tk), tk)
            out = []
            for qv, cols, (m, l, acc) in zip(qs, lanes, carry):
                s = _dot_nt(qv, k_ref[0, sl, cols])
                if diag is not None:
                    row = lax.broadcasted_iota(jnp.int32, (t, tk), 0)
                    col = lax.broadcasted_iota(jnp.int32, (t, tk), 1)
                    s = jnp.where(col + diag * tk <= row, s, -1e30)
                m_new = jnp.maximum(m, jnp.max(s, axis=-1, keepdims=True))
                alpha = jnp.exp(m - m_new)
                p = jnp.exp(s - m_new)
                l = alpha * l + jnp.sum(p, axis=-1, keepdims=True)
                acc = alpha * acc + _dot(p.astype(BF16), v_ref[0, sl, cols])
                out.append((m_new, l, acc))
            return tuple(out)

        init = tuple((jnp.full((t, 1), -1e30, F32), jnp.zeros((t, 1), F32), jnp.zeros((t, HEAD_LANES), F32))
                     for _ in range(pair))
        carry = lax.fori_loop(0, qi * per, lambda j, c: step(j, c, None), init)
        for d in range(per):
            carry = step(qi * per + d, carry, d)
        for hh, (m, l, acc) in enumerate(carry):
            o_ref[0, :, lanes[hh]] = (acc / l).astype(BF16)
            lse_ref[0, hh] = m + jnp.log(l)

    return pl.pallas_call(
        body, grid=(B, H // pair, nq), name="attn_fwd",
        in_specs=[pl.BlockSpec((1, t, pw), lambda b, h, i: (b, i, h)),
                  pl.BlockSpec((1, S, pw), lambda b, h, i: (b, 0, h)),
                  pl.BlockSpec((1, S, pw), lambda b, h, i: (b, 0, h))],
        out_specs=[pl.BlockSpec((1, t, pw), lambda b, h, i: (b, i, h)),
                   pl.BlockSpec((1, pair, t, 1), lambda b, h, i: (b, h, i, 0))],
        out_shape=[jax.ShapeDtypeStruct((B, S, HW), BF16), jax.ShapeDtypeStruct((B, H, S, 1), F32)],
        compiler_params=_cparams(("arbitrary", "arbitrary", "arbitrary")),
    )(q, k, v)


def _attn_bwd(q, k, v, o, do, lse):
    B, S, HW = q.shape
    H = HW // HEAD_LANES
    t = min(ATTN_BWD_TILE, S)
    nq = S // t

    pair = 2
    pw = pair * HEAD_LANES

    def body(q_ref, k_ref, v_ref, o_ref, do_ref, lse_ref, dq_out_ref, dk_ref, dv_ref, dq_ref):
        j = pl.program_id(2)

        @pl.when(j == 0)
        def _():
            dq_ref[...] = jnp.zeros_like(dq_ref)

        lanes = [slice(hh * HEAD_LANES, (hh + 1) * HEAD_LANES) for hh in range(pair)]

        def step(i, carry, masked):
            sl = pl.ds(pl.multiple_of(i * t, t), t)
            out = []
            for hh, (cols, (dk, dv)) in enumerate(zip(lanes, carry)):
                kj = k_ref[0, :, cols]
                qi = q_ref[0, sl, cols]
                doi = do_ref[0, sl, cols]
                s = _dot_nt(qi, kj)
                if masked:
                    s = jnp.where(_causal_mask(t), s, -1e30)
                p = jnp.exp(s - lse_ref[0, hh, sl, :])
                dv = dv + _dot_tn(p.astype(BF16), doi)
                dp = _dot_nt(doi, v_ref[0, :, cols])
                delta = jnp.sum(doi.astype(F32) * o_ref[0, sl, cols].astype(F32), axis=-1, keepdims=True)
                dsb = (p * (dp - delta)).astype(BF16)
                dk = dk + _dot_tn(dsb, qi)
                dq_ref[sl, cols] += _dot(dsb, kj)
                out.append((dk, dv))
            return tuple(out)

        zero = jnp.zeros((t, HEAD_LANES), F32)
        carry = step(j, ((zero, zero),) * pair, True)
        carry = lax.fori_loop(j + 1, nq, lambda i, c: step(i, c, False), carry)
        for cols, (dk, dv) in zip(lanes, carry):
            dk_ref[0, :, cols] = dk.astype(BF16)
            dv_ref[0, :, cols] = dv.astype(BF16)

        @pl.when(j == nq - 1)
        def _():
            dq_out_ref[0] = dq_ref[...].astype(BF16)

    full = pl.BlockSpec((1, S, pw), lambda b, h, j: (b, 0, h))
    tile = pl.BlockSpec((1, t, pw), lambda b, h, j: (b, j, h))
    sd = jax.ShapeDtypeStruct
    return pl.pallas_call(
        body, grid=(B, H // pair, nq), name="attn_bwd",
        in_specs=[full, tile, tile, full, full, pl.BlockSpec((1, pair, S, 1), lambda b, h, j: (b, h, 0, 0))],
        out_specs=[full, tile, tile],
        out_shape=[sd((B, S, HW), BF16), sd((B, S, HW), BF16), sd((B, S, HW), BF16)],
        scratch_shapes=[pltpu.VMEM((S, pw), F32)],
        compiler_params=_cparams(("arbitrary", "arbitrary", "arbitrary")),
    )(q, k, v, o, do, lse)


def _mix_out(x1, yssd, o, mw, wout, g, seq):
    T, D = x1.shape
    tm = min(WIDE_TOKEN_TILE, seq)
    tps = seq // tm

    def body(x_ref, ys_ref, o_ref, mw_ref, w_ref, g_ref, xo_ref, m_ref, yc_ref):
        xh, _ = _rms_stats(o_ref[...].astype(F32))
        ycat = jnp.concatenate([ys_ref[...], (xh * mw_ref[...]).astype(BF16)], axis=1)
        m = _dot(ycat, w_ref[...])
        xo_ref[...] = x_ref[...] + g_ref[0] * m
        m_ref[...] = m.astype(BF16)
        yc_ref[...] = ycat

    rows = lambda n: pl.BlockSpec((tm, n), lambda i: (i, 0))
    perb = pl.BlockSpec((1, 1, D), lambda i: (i // tps, 0, 0))
    sd = jax.ShapeDtypeStruct
    return pl.pallas_call(
        body, grid=(T // tm,), name="mix_out",
        in_specs=[rows(D), rows(D_SSD), rows(D_MLA), _resident((1, D_MLA)), _resident((D_SSD + D_MLA, D)), perb],
        out_specs=[rows(D), rows(D), rows(D_SSD + D_MLA)],
        out_shape=[sd((T, D), F32), sd((T, D), BF16), sd((T, D_SSD + D_MLA), BF16)],
        compiler_params=_cparams(("arbitrary",)),
    )(x1, yssd, o, mw, wout, g)


def _mix_out_bwd(dx2, m, o, mw, wout, g, seq):
    T, D = dx2.shape
    B = T // seq
    tm = min(WIDE_TOKEN_TILE, seq)
    tps = seq // tm

    def body(dx_ref, m_ref, o_ref, mw_ref, w_ref, g_ref, dys_ref, do_ref, dm_ref, dg_ref, dmw_ref):
        i = pl.program_id(0)

        @pl.when(i % tps == 0)
        def _():
            dg_ref[...] = jnp.zeros_like(dg_ref)

        @pl.when(i == 0)
        def _():
            dmw_ref[...] = jnp.zeros_like(dmw_ref)

        dxv = dx_ref[...]
        dg_ref[0] += _colsum(dxv * m_ref[...].astype(F32))
        dmb = (g_ref[0] * dxv).astype(BF16)
        dm_ref[...] = dmb
        dycat = _dot_nt(dmb, w_ref[...])
        dys_ref[...] = dycat[:, :D_SSD].astype(BF16)
        xh, r = _rms_stats(o_ref[...].astype(F32))
        dx, dw_rows = _rms_bwd(dycat[:, D_SSD:], xh, r, mw_ref[...])
        do_ref[...] = dx.astype(BF16)
        dmw_ref[...] += _colsum(dw_rows)

    rows = lambda n: pl.BlockSpec((tm, n), lambda i: (i, 0))
    perb = pl.BlockSpec((1, 1, D), lambda i: (i // tps, 0, 0))
    sd = jax.ShapeDtypeStruct
    return pl.pallas_call(
        body, grid=(T // tm,), name="mix_out_bwd",
        in_specs=[rows(D), rows(D), rows(D_MLA), _resident((1, D_MLA)), _resident((D_SSD + D_MLA, D)), perb],
        out_specs=[rows(D_SSD), rows(D_MLA), rows(D), perb, pl.BlockSpec((1, D_MLA), lambda i: (0, 0))],
        out_shape=[sd((T, D_SSD), BF16), sd((T, D_MLA), BF16), sd((T, D), BF16), sd((B, 1, D), F32), sd((1, D_MLA), F32)],
        compiler_params=_cparams(("arbitrary",)),
    )(dx2, m, o, mw, wout, g)


def _win_to_kernel(w):
    z0 = jnp.zeros((48, w.shape[1]), w.dtype)
    z1 = jnp.zeros((32, w.shape[1]), w.dtype)
    return jnp.concatenate([w[:2560], w[2576:3216], w[2560:2576], z0, w[3216:3248], z1], axis=0)


def _win_from_kernel(g):
    return jnp.concatenate([g[:2560], g[3200:3216], g[2560:3200], g[3264:3296]], axis=0)


def _wuq_to_kernel(w):
    w = w.reshape(Q_LORA, MLA_HEADS, QK_DIM)
    return jnp.pad(w, ((0, 0), (0, 0), (0, HEAD_LANES - QK_DIM))).reshape(Q_LORA, MLA_HEADS * HEAD_LANES)


def _wuq_from_kernel(g):
    return g.reshape(Q_LORA, MLA_HEADS, HEAD_LANES)[:, :, :QK_DIM].reshape(Q_LORA, MLA_HEADS * QK_DIM)


def _wukv_to_kernel(w):
    w = w.reshape(KV_LORA, MLA_HEADS, QK_NOPE + V_HEAD)
    kp = jnp.pad(w[:, :, :QK_NOPE], ((0, 0), (0, 0), (0, HEAD_LANES - QK_NOPE)))
    return jnp.concatenate([kp.reshape(KV_LORA, -1), w[:, :, QK_NOPE:].reshape(KV_LORA, -1)], axis=1)


def _wukv_from_kernel(g):
    hw = MLA_HEADS * HEAD_LANES
    kp = g[:, :hw].reshape(KV_LORA, MLA_HEADS, HEAD_LANES)[:, :, :QK_NOPE]
    vp = g[:, hw:].reshape(KV_LORA, MLA_HEADS, V_HEAD)
    return jnp.concatenate([kp, vp], axis=2).reshape(KV_LORA, MLA_HEADS * (QK_NOPE + V_HEAD))


def _lanes16(v):
    return jnp.pad(v.reshape(1, SSD_HEADS), ((0, 0), (0, LANES - SSD_HEADS)))


def _constants():
    e = np.zeros((LANES, D_SSD), np.float32)
    for h in range(SSD_HEADS):
        e[h, h * SSD_HEAD_DIM:(h + 1) * SSD_HEAD_DIM] = 1.0
    inv_freq = ROPE_THETA ** (-jnp.arange(0, QK_ROPE, 2, dtype=F32) / QK_ROPE)
    half = QK_ROPE // 2
    place = np.zeros((half, LANES), np.float32)
    for j in range(half):
        place[j, QK_NOPE + j] = place[j, QK_NOPE + half + j] = 1.0
    return jnp.asarray(e, BF16), inv_freq.reshape(half, 1), jnp.asarray(place, BF16)


def _local_step(x, positions, mod, w, later_weights, small, tgt, on_grads, sync):
    B, S, D = x.shape
    T = B * S
    expand, invf, place = _constants()
    x0 = x.reshape(T, D)
    pos = positions.reshape(T)
    mods = [mod[:, i * D:(i + 1) * D].reshape(B, 1, D) for i in range(N_MOD)]
    sh1, sc1, g1, sh2, sc2, g2, sh3, sc3, g3 = mods
    dtb, alog = _lanes16(small["dt_bias"]), _lanes16(small["a_log"])
    dsk = jnp.repeat(small["d_skip"].reshape(1, SSD_HEADS), SSD_HEAD_DIM, axis=1)

    x1, a1, u1, f1 = _ffn_fwd(x0, small["norm_ffn1"], sh1, sc1, g1, w["ffn1_w_gate"], w["ffn1_w_up"], w["ffn1_w_down"], S, "ffn1_fwd")
    w = {**w, **later_weights(f1)}
    z, xraw, cq, ckv, dtk = _inproj_fwd(x1, small["norm_mix"], sh2, sc2, w["w_in"], S)
    xraw3 = xraw.reshape(B, S, D_CONV)
    xbc = _conv_fwd(xraw3, small["conv_w"], small["conv_b"])
    dtk3, z3 = dtk.reshape(B, S, LANES), z.reshape(B, S, D_SSD)
    y, yssd, prev = _ssd_fwd(xbc, dtk3, z3, dtb, alog, dsk, small["ssd_norm_w"], expand)
    q, k, v, cos_t, sin_t = _mla_prep(cq, ckv, dtk, pos, small["q_norm_w"], small["kv_norm_w"], w["w_uq"], w["w_ukv"], invf,
                                      place)
    hw = MLA_HEADS * HEAD_LANES
    q3, k3, v3 = q.reshape(B, S, hw), k.reshape(B, S, hw), v.reshape(B, S, hw)
    o3, lse = _attn_fwd(q3, k3, v3)
    o = o3.reshape(T, hw)
    x2, m, ycat = _mix_out(x1, yssd.reshape(T, D_SSD), o, small["mla_norm_w"], w["w_out"], g2, S)
    dx3, a2, u2, f2, loss, d_norm_final = _ffn_fwd(
        x2, small["norm_ffn2"], sh3, sc3, g3, w["ffn2_w_gate"], w["ffn2_w_up"], w["ffn2_w_down"], S, "ffn2_fwd",
        head=(small["norm_final"].reshape(1, D), tgt.reshape(T, D)))

    gw, gs = {}, {}
    dx2, h3, s3, df3, da3, du3, dsh3, dsc3, dg3, gs["norm_ffn2"] = _ffn_bwd(
        dx3, x2, small["norm_ffn2"], sh3, sc3, g3, a2, u2, f2, w["ffn2_w_gate"], w["ffn2_w_up"], w["ffn2_w_down"], S, "ffn2_bwd")
    gw["ffn2_w_gate"], gw["ffn2_w_up"], gw["ffn2_w_down"] = _ffn_wgrad(h3, s3, df3, da3, du3, dsh3, "ffn2_wgrad")
    g2 = g2 + on_grads(("ffn2_w_gate", "ffn2_w_up", "ffn2_w_down"), gw)

    dys, do, dm, dg2, gs["mla_norm_w"] = _mix_out_bwd(dx2, m, o, small["mla_norm_w"], w["w_out"], g2, S)
    gw["w_out"] = _mm_tn(ycat, dm, 512, "dwout")

    dq3, dk3, dv3 = _attn_bwd(q3, k3, v3, o3, do.reshape(B, S, hw), lse)
    dcq, dckv, ddtk_b, qn, kvn, dqb, dkvb, gs["q_norm_w"], gs["kv_norm_w"] = _mla_prep_bwd(
        dq3.reshape(T, hw), dk3.reshape(T, hw), dv3.reshape(T, hw), cq, ckv, cos_t, sin_t, small["q_norm_w"] + sync(dq3),
        small["kv_norm_w"], w["w_uq"], w["w_ukv"])
    gw["w_uq"] = _mm_tn(qn, dqb, 512, "dwuq")
    gw["w_ukv"] = _mm_tn(kvn, dkvb, 1024, "dwukv")

    dxbc, ddtk_a, dz, gs["ssd_norm_w"], dvec = _ssd_bwd(
        xbc, dtk3, z3, y, prev, dys.reshape(B, S, D_SSD), dtb, alog, dsk, small["ssd_norm_w"], expand)
    gs["dt_bias"], gs["a_log"], gs["d_skip"] = dvec[0:1, :SSD_HEADS], dvec[1:2, :SSD_HEADS], dvec[2:3, :SSD_HEADS]
    dxraw, gs["conv_w"], gs["conv_b"] = _conv_bwd(dxbc, xraw3, small["conv_w"], small["conv_b"])
    dx1, h2, dproj, dsh2, dsc2, gs["norm_mix"] = _inproj_bwd(
        dx2, x1, small["norm_mix"], sh2, sc2, w["w_in"], dz.reshape(T, D_SSD), dxraw.reshape(T, D_CONV), dcq, dckv,
        ddtk_a.reshape(T, LANES), ddtk_b, S)
    gw["w_in"] = _mm_tn(dproj, h2, 512, "dwin")
    g1 = g1 + on_grads(("w_in", "w_uq", "w_ukv", "w_out"), gw)

    dx0, h1, s1, df1, da1, du1, dsh1, dsc1, dg1, gs["norm_ffn1"] = _ffn_bwd(
        dx1, x0, small["norm_ffn1"], sh1, sc1, g1, a1, u1, f1, w["ffn1_w_gate"], w["ffn1_w_up"], w["ffn1_w_down"], S, "ffn1_bwd")
    gw["ffn1_w_gate"], gw["ffn1_w_up"], gw["ffn1_w_down"] = _ffn_wgrad(h1, s1, df1, da1, du1, dsh1 + sync(dx0), "ffn1_wgrad")
    gs["norm_final"] = d_norm_final
    dmod = jnp.concatenate([t.reshape(B, D) for t in (dsh1, dsc1, dg1, dsh2, dsc2, dg2, dsh3, dsc3, dg3)], axis=1)
    return loss, dx0.reshape(B, S, D), gw, dmod, gs


HBM_SPEC = pl.BlockSpec(memory_space=pltpu.HBM)
VMEM_SPEC = pl.BlockSpec(memory_space=pltpu.VMEM)


def _place():
    return lax.axis_index("x"), lax.axis_index("y"), lax.axis_index("c")


def _other_chips(mx, my):
    return [(1 - mx, my), (mx, 1 - my), (1 - mx, 1 - my)]


def _remote(src, dst, send_sem, recv_sem, to):
    return pltpu.make_async_remote_copy(src_ref=src, dst_ref=dst, send_sem=send_sem, recv_sem=recv_sem,
                                        device_id=to, device_id_type=MESH)


def _all_gather_small(xa, name):
    r, n = xa.shape

    def body(x_ref, o_ref, token, send_sems, recv_sems):
        mx, my, mc = _place()
        me = 4 * mx + 2 * my + mc
        token[...] = jnp.zeros_like(token)
        o_ref[pl.ds(me, 1)] = x_ref[...][None]
        sends = []
        for k in range(1, N_DEV):
            peer = (mx ^ (k >> 2), my ^ ((k >> 1) & 1), mc ^ (k & 1))
            cp = _remote(x_ref, o_ref.at[me], send_sems.at[k - 1], recv_sems.at[k - 1], peer)
            cp.start()
            sends.append(cp)
        for k in range(1, N_DEV):
            peer = (mx ^ (k >> 2), my ^ ((k >> 1) & 1), mc ^ (k & 1))
            slot = 4 * peer[0] + 2 * peer[1] + peer[2]
            _remote(x_ref, o_ref.at[slot], send_sems.at[k - 1], recv_sems.at[k - 1], peer).wait_recv()
        for cp in sends:
            cp.wait_send()

    return pl.pallas_call(
        body, name=name, in_specs=[VMEM_SPEC], out_specs=[VMEM_SPEC, VMEM_SPEC],
        out_shape=[jax.ShapeDtypeStruct((N_DEV, r, n), xa.dtype), jax.ShapeDtypeStruct((8, LANES), F32)],
        scratch_shapes=[pltpu.SemaphoreType.DMA((N_DEV - 1,)), pltpu.SemaphoreType.DMA((N_DEV - 1,))],
        compiler_params=pltpu.CompilerParams(vmem_limit_bytes=VMEM_LIMIT),
    )(xa)


def _halves_by_rows(shape):
    return (shape[-2] // 2) % 16 == 0


def _half_shape(shape):
    r, c = shape[-2:]
    return tuple(shape[:-2]) + ((r // 2, c) if _halves_by_rows(shape) else (r, c // 2))


def _half_index(shape, hc):
    r, c = shape[-2:]
    if _halves_by_rows(shape):
        return (pl.ds(pl.multiple_of(hc * (r // 2), 16), r // 2), slice(None))
    return (slice(None), pl.ds(pl.multiple_of(hc * (c // 2), LANES), c // 2))


def _half(ref, hc, lead=None):
    idx = _half_index(ref.shape, hc)
    return ref.at[idx] if lead is None else ref.at[(lead,) + idx]


SEM_SPEC = pl.BlockSpec(memory_space=pltpu.SEMAPHORE)
ANY_SPEC = pl.BlockSpec(memory_space=pl.ANY)
DATAFLOW = pltpu.SideEffectType.DATAFLOW_SIDE_EFFECTING


def _hbm(arr):
    return pltpu.with_memory_space_constraint(arr, pltpu.HBM)


def _gather_start(shards, tag):
    n = len(shards)

    def body(*refs):
        w_refs, land_refs, send_sems, recv_sems, token = refs[:n], refs[n:2 * n], refs[2 * n], refs[2 * n + 1], refs[-1]
        mx, my, mc = _place()
        chip = 2 * mx + my
        for i, (w, land) in enumerate(zip(w_refs, land_refs)):
            for k, (cx, cy) in enumerate(_other_chips(mx, my)):
                _remote(_half(w, mc), _half(land, mc, chip), send_sems.at[3 * i + k],
                        recv_sems.at[3 * i + k], (cx, cy, mc)).start()
        token[...] = jnp.zeros_like(token)

    lands = [lax.empty((N_CHIPS,) + s.shape, s.dtype) for s in shards]
    out = pl.pallas_call(
        body, name="gather_start_" + tag,
        out_shape=(pltpu.SemaphoreType.DMA((3 * n,)), pltpu.SemaphoreType.DMA((3 * n,)),
                   *[pltpu.HBM(s.shape, s.dtype) for s in shards], *[pltpu.HBM(l.shape, l.dtype) for l in lands],
                   jax.ShapeDtypeStruct((8, LANES), F32)),
        in_specs=[HBM_SPEC] * (2 * n), out_specs=(SEM_SPEC, SEM_SPEC, *[HBM_SPEC] * (2 * n), VMEM_SPEC),
        input_output_aliases={i: 2 + i for i in range(2 * n)},
        compiler_params=pltpu.CompilerParams(has_side_effects=DATAFLOW),
    )(*[_hbm(s) for s in shards], *[_hbm(l) for l in lands])
    return out[0], out[1], out[2:2 + n], out[2 + n:2 + 2 * n], out[-1]


def _gather_wait(send_sems, recv_sems, shards, lands, after, tag):
    n = len(shards)

    def body(*refs):
        w_refs, land_refs, send_sems, recv_sems = refs[:n], refs[n:2 * n], refs[2 * n], refs[2 * n + 1]
        mx, my, mc = _place()
        for i, (w, land) in enumerate(zip(w_refs, land_refs)):
            for k, (cx, cy) in enumerate(_other_chips(mx, my)):
                cp = _remote(_half(w, mc), _half(land, mc, 2 * cx + cy), send_sems.at[3 * i + k],
                             recv_sems.at[3 * i + k], (cx, cy, mc))
                cp.wait_send()
                cp.wait_recv()

    out = pl.pallas_call(
        body, name="gather_wait_" + tag,
        out_shape=(*[pltpu.HBM(s.shape, s.dtype) for s in shards], *[pltpu.HBM(l.shape, l.dtype) for l in lands]),
        in_specs=[HBM_SPEC] * (2 * n) + [SEM_SPEC, SEM_SPEC, ANY_SPEC], out_specs=tuple([HBM_SPEC] * (2 * n)),
        input_output_aliases={i: i for i in range(2 * n)},
        compiler_params=pltpu.CompilerParams(has_side_effects=DATAFLOW),
    )(*shards, *lands, send_sems, recv_sems, after)
    return out[n:]


def _gather_finish(shards, lands, tag):
    n = len(shards)

    def body(*refs):
        w_refs, land_refs, o_refs, token = refs[:n], refs[n:2 * n], refs[2 * n:3 * n], refs[3 * n]
        send_sems, recv_sems, stage_sems = refs[3 * n + 1:3 * n + 4]
        stages = refs[3 * n + 4:]
        token[...] = jnp.zeros_like(token)
        mx, my, mc = _place()
        chip = 2 * mx + my
        others = _other_chips(mx, my)
        sibling = (mx, my, 1 - mc)
        stage_in = [pltpu.make_async_copy(w, st, stage_sems.at[0, i]) for i, (w, st) in enumerate(zip(w_refs, stages))]
        for cp in stage_in:
            cp.start()
        passed = []
        for i, (w, o) in enumerate(zip(w_refs, o_refs)):
            for k, (cx, cy) in enumerate(others):
                landed = _half(o, mc, 2 * cx + cy)
                passed.append(_remote(landed, landed, send_sems.at[i, k], recv_sems.at[i, k], sibling))
                passed[-1].start()
        stage_out = []
        for i, (st, o) in enumerate(zip(stages, o_refs)):
            stage_in[i].wait()
            stage_out.append(pltpu.make_async_copy(st, o.at[chip], stage_sems.at[1, i]))
            stage_out[-1].start()
        for i, (w, o) in enumerate(zip(w_refs, o_refs)):
            for k, (cx, cy) in enumerate(others):
                there = _half(o, 1 - mc, 2 * cx + cy)
                _remote(there, there, send_sems.at[i, k], recv_sems.at[i, k], sibling).wait_recv()
        for cp in passed:
            cp.wait_send()
        for cp in stage_out:
            cp.wait()

    out = pl.pallas_call(
        body, name="gather_finish_" + tag, in_specs=[HBM_SPEC] * (2 * n), out_specs=[HBM_SPEC] * n + [VMEM_SPEC],
        out_shape=[jax.ShapeDtypeStruct(l.shape, l.dtype) for l in lands] + [jax.ShapeDtypeStruct((8, LANES), F32)],
        input_output_aliases={n + i: i for i in range(n)},
        scratch_shapes=[pltpu.SemaphoreType.DMA((n, 3)), pltpu.SemaphoreType.DMA((n, 3)), pltpu.SemaphoreType.DMA((2, n))]
        + [pltpu.VMEM(s.shape, s.dtype) for s in shards],
        compiler_params=pltpu.CompilerParams(vmem_limit_bytes=VMEM_LIMIT),
    )(*shards, *lands)
    return out[:n], out[n]


def _scatter_start(ss, tag):
    n = len(ss)

    def body(*refs):
        s_refs, land_refs, send_sems, recv_sems, token = refs[:n], refs[n:2 * n], refs[2 * n], refs[2 * n + 1], refs[-1]
        mx, my, mc = _place()
        chip = 2 * mx + my
        for i, (s, land) in enumerate(zip(s_refs, land_refs)):
            for k, (cx, cy) in enumerate(_other_chips(mx, my)):
                _remote(s.at[2 * cx + cy], land.at[chip], send_sems.at[3 * i + k], recv_sems.at[3 * i + k],
                        (cx, cy, mc)).start()
        token[...] = jnp.zeros_like(token)

    lands = [lax.empty(s.shape, s.dtype) for s in ss]
    out = pl.pallas_call(
        body, name="scatter_start_" + tag,
        out_shape=(pltpu.SemaphoreType.DMA((3 * n,)), pltpu.SemaphoreType.DMA((3 * n,)),
                   *[pltpu.HBM(s.shape, s.dtype) for s in ss], *[pltpu.HBM(l.shape, l.dtype) for l in lands],
                   jax.ShapeDtypeStruct((8, LANES), F32)),
        in_specs=[HBM_SPEC] * (2 * n), out_specs=(SEM_SPEC, SEM_SPEC, *[HBM_SPEC] * (2 * n), VMEM_SPEC),
        input_output_aliases={i: 2 + i for i in range(2 * n)},
        compiler_params=pltpu.CompilerParams(has_side_effects=DATAFLOW),
    )(*[_hbm(s) for s in ss], *[_hbm(l) for l in lands])
    return out[0], out[1], out[2:2 + n], out[2 + n:2 + 2 * n], out[-1]


def _scatter_wait(send_sems, recv_sems, ss, lands, after, tag):
    n = len(ss)

    def body(*refs):
        s_refs, land_refs, send_sems, recv_sems = refs[:n], refs[n:2 * n], refs[2 * n], refs[2 * n + 1]
        mx, my, mc = _place()
        for i, (s, land) in enumerate(zip(s_refs, land_refs)):
            for k, (cx, cy) in enumerate(_other_chips(mx, my)):
                slot = land.at[2 * cx + cy]
                cp = _remote(s.at[2 * cx + cy], slot, send_sems.at[3 * i + k], recv_sems.at[3 * i + k], (cx, cy, mc))
                cp.wait_send()
                cp.wait_recv()

    out = pl.pallas_call(
        body, name="scatter_wait_" + tag,
        out_shape=(*[pltpu.HBM(s.shape, s.dtype) for s in ss], *[pltpu.HBM(l.shape, l.dtype) for l in lands]),
        in_specs=[HBM_SPEC] * (2 * n) + [SEM_SPEC, SEM_SPEC, ANY_SPEC], out_specs=tuple([HBM_SPEC] * (2 * n)),
        input_output_aliases={i: i for i in range(2 * n)},
        compiler_params=pltpu.CompilerParams(has_side_effects=DATAFLOW),
    )(*ss, *lands, send_sems, recv_sems, after)
    return out[:n], out[n:]


def _swap_halves(gs, after, name):
    n = len(gs)

    def body(*refs):
        g_refs, o_refs, send_sems, recv_sems = refs[:n], refs[n + 1:2 * n + 1], refs[2 * n + 1], refs[2 * n + 2]
        mx, my, mc = _place()
        copies = []
        for i, (g, o) in enumerate(zip(g_refs, o_refs)):
            src = g.at[(slice(None),) + _half_index(g.shape, 1 - mc)]
            copies.append(_remote(src, o, send_sems.at[i], recv_sems.at[i], (mx, my, 1 - mc)))
            copies[-1].start()
        for cp in copies:
            cp.wait()

    return pl.pallas_call(
        body, name=name, in_specs=[HBM_SPEC] * n + [ANY_SPEC], out_specs=[HBM_SPEC] * n,
        out_shape=[jax.ShapeDtypeStruct(_half_shape(g.shape), g.dtype) for g in gs],
        scratch_shapes=[pltpu.SemaphoreType.DMA((n,)), pltpu.SemaphoreType.DMA((n,))],
    )(*gs, after)


def _swap_start(gs, tag):
    n = len(gs)

    def body(*refs):
        g_refs, land_refs, send_sems, recv_sems, token = refs[:n], refs[n:2 * n], refs[2 * n], refs[2 * n + 1], refs[-1]
        mx, my, mc = _place()
        for i, (g, land) in enumerate(zip(g_refs, land_refs)):
            src = g.at[(slice(None),) + _half_index(g.shape, 1 - mc)]
            _remote(src, land, send_sems.at[i], recv_sems.at[i], (mx, my, 1 - mc)).start()
        token[...] = jnp.zeros_like(token)

    lands = [lax.empty(_half_shape(g.shape), g.dtype) for g in gs]
    out = pl.pallas_call(
        body, name="swap_start_" + tag,
        out_shape=(pltpu.SemaphoreType.DMA((n,)), pltpu.SemaphoreType.DMA((n,)), *[pltpu.HBM(g.shape, g.dtype) for g in gs],
                   *[pltpu.HBM(l.shape, l.dtype) for l in lands], jax.ShapeDtypeStruct((8, LANES), F32)),
        in_specs=[HBM_SPEC] * (2 * n), out_specs=(SEM_SPEC, SEM_SPEC, *[HBM_SPEC] * (2 * n), VMEM_SPEC),
        input_output_aliases={i: 2 + i for i in range(2 * n)},
        compiler_params=pltpu.CompilerParams(has_side_effects=DATAFLOW),
    )(*[_hbm(g) for g in gs], *[_hbm(l) for l in lands])
    return out[0], out[1], out[2:2 + n], out[2 + n:2 + 2 * n], out[-1]


def _swap_wait(send_sems, recv_sems, gs, lands, after, tag):
    n = len(gs)

    def body(*refs):
        g_refs, land_refs, send_sems, recv_sems = refs[:n], refs[n:2 * n], refs[2 * n], refs[2 * n + 1]
        mx, my, mc = _place()
        for i, (g, land) in enumerate(zip(g_refs, land_refs)):
            src = g.at[(slice(None),) + _half_index(g.shape, 1 - mc)]
            cp = _remote(src, land, send_sems.at[i], recv_sems.at[i], (mx, my, 1 - mc))
            cp.wait_send()
            cp.wait_recv()

    out = pl.pallas_call(
        body, name="swap_wait_" + tag,
        out_shape=(*[pltpu.HBM(g.shape, g.dtype) for g in gs], *[pltpu.HBM(l.shape, l.dtype) for l in lands]),
        in_specs=[HBM_SPEC] * (2 * n) + [SEM_SPEC, SEM_SPEC, ANY_SPEC], out_specs=tuple([HBM_SPEC] * (2 * n)),
        input_output_aliases={i: i for i in range(2 * n)},
        compiler_params=pltpu.CompilerParams(has_side_effects=DATAFLOW),
    )(*gs, *lands, send_sems, recv_sems, after)
    return out[:n], out[n:]


def _pair_sums(gs, gots, name):
    n = len(gs)

    def body(*refs):
        g_refs, got_refs, o_refs, load_sems, store_sems = refs[:n], refs[n:2 * n], refs[2 * n:3 * n], refs[3 * n], refs[3 * n + 1]
        mine, theirs, sums = refs[3 * n + 2:4 * n + 2], refs[4 * n + 2:5 * n + 2], refs[5 * n + 2:]
        mc = lax.axis_index("c")
        loads = []
        for i, (g, got) in enumerate(zip(g_refs, got_refs)):
            loads.append((pltpu.make_async_copy(g.at[(slice(None),) + _half_index(g.shape, mc)], mine[i], load_sems.at[i, 0]),
                          pltpu.make_async_copy(got, theirs[i], load_sems.at[i, 1])))
            for cp in loads[-1]:
                cp.start()
        stores = []
        for i in range(n):
            for cp in loads[i]:
                cp.wait()
            sums[i][...] = (mine[i][...].astype(F32) + theirs[i][...].astype(F32)).astype(BF16)
            stores.append(pltpu.make_async_copy(sums[i], o_refs[i], store_sems.at[i]))
            stores[-1].start()
        for cp in stores:
            cp.wait()

    halves = [jax.ShapeDtypeStruct(got.shape, BF16) for got in gots]
    return pl.pallas_call(
        body, name=name, in_specs=[HBM_SPEC] * (2 * n), out_specs=[HBM_SPEC] * n, out_shape=halves,
        scratch_shapes=[pltpu.SemaphoreType.DMA((n, 2)), pltpu.SemaphoreType.DMA((n,))]
        + [pltpu.VMEM(got.shape, g.dtype) for g, got in zip(gs, gots)] + [pltpu.VMEM(got.shape, got.dtype) for got in gots]
        + [pltpu.VMEM(got.shape, BF16) for got in gots],
        compiler_params=pltpu.CompilerParams(vmem_limit_bytes=VMEM_LIMIT),
    )(*gs, *gots)


def _reduce_join(owns, gots, name):
    n = len(owns)

    def body(*refs):
        own_refs, got_refs, mine_refs, theirs_refs = refs[:n], refs[n:2 * n], refs[2 * n:3 * n], refs[3 * n:4 * n]
        send_sems, recv_sems, load_sems, store_sems = refs[4 * n:4 * n + 4]
        parts, sums = refs[4 * n + 4:5 * n + 4], refs[5 * n + 4:]
        mx, my, mc = _place()
        chip = 2 * mx + my
        loads = []
        for i, (own, got, part) in enumerate(zip(own_refs, got_refs, parts)):
            loads.append([pltpu.make_async_copy((own if k == 0 else got).at[chip ^ k], part.at[k], load_sems.at[i, k])
                          for k in range(N_CHIPS)])
            for cp in loads[-1]:
                cp.start()
        out = []
        for i, (part, total, mine, theirs) in enumerate(zip(parts, sums, mine_refs, theirs_refs)):
            for cp in loads[i]:
                cp.wait()
            total[...] = ((part[0].astype(F32) + part[1].astype(F32)) + part[2].astype(F32)) + part[3].astype(F32)
            out.append(pltpu.make_async_copy(total, mine, store_sems.at[i]))
            out.append(_remote(total, theirs, send_sems.at[i], recv_sems.at[i], (mx, my, 1 - mc)))
            out[-2].start()
            out[-1].start()
        for cp in out:
            cp.wait()

    halves = [jax.ShapeDtypeStruct(o.shape[1:], F32) for o in owns]
    out = pl.pallas_call(
        body, name=name, in_specs=[HBM_SPEC] * (2 * n), out_specs=[HBM_SPEC] * (2 * n), out_shape=halves + halves,
        scratch_shapes=[pltpu.SemaphoreType.DMA((n,)), pltpu.SemaphoreType.DMA((n,)), pltpu.SemaphoreType.DMA((n, N_CHIPS)),
                        pltpu.SemaphoreType.DMA((n,))]
        + [pltpu.VMEM(o.shape, o.dtype) for o in owns] + [pltpu.VMEM(o.shape[1:], F32) for o in owns],
        compiler_params=pltpu.CompilerParams(vmem_limit_bytes=VMEM_LIMIT),
    )(*owns, *gots)
    return out[:n], out[n:]


def _adam_math(w, g, m, v):
    m2 = ADAM_B1 * m + (1.0 - ADAM_B1) * g
    v2 = ADAM_B2 * v + (1.0 - ADAM_B2) * (g * g)
    m_hat = m2 * (1.0 / (1.0 - ADAM_B1 ** ADAM_STEP))
    v_hat = v2 * (1.0 / (1.0 - ADAM_B2 ** ADAM_STEP))
    delta = -ADAM_LR * (m_hat / (jnp.sqrt(v_hat) + ADAM_EPS) + ADAM_WD * w)
    return delta, m2, v2


def _adam(w, g, m, v, name):
    def body(w_ref, g_ref, m_ref, v_ref, d_ref, m2_ref, v2_ref):
        d_ref[...], m2_ref[...], v2_ref[...] = _adam_math(w_ref[...], g_ref[...], m_ref[...], v_ref[...])

    return pl.pallas_call(body, name=name, out_shape=[jax.ShapeDtypeStruct(w.shape, F32)] * 3)(w, g, m, v)


def _adam_halves(w, m, v, mine, theirs, core, name):
    hr, hcols = _half_shape(w.shape)[1:]
    by_rows = _halves_by_rows(w.shape)

    def body(core_ref, w_ref, m_ref, v_ref, mine_ref, theirs_ref, g_ref, d_ref, m2_ref, v2_ref):
        g = jnp.where(pl.program_id(0) == core_ref[0], mine_ref[...], theirs_ref[...])
        g_ref[0] = g
        d_ref[0], m2_ref[0], v2_ref[0] = _adam_math(w_ref[0], g, m_ref[0], v_ref[0])

    half = pl.BlockSpec((1, hr, hcols), lambda hc, core_ref: (0, hc, 0) if by_rows else (0, 0, hc))
    whole = pl.BlockSpec((hr, hcols), lambda hc, core_ref: (0, 0))
    return pl.pallas_call(
        body, name=name,
        grid_spec=pltpu.PrefetchScalarGridSpec(
            num_scalar_prefetch=1, grid=(2,), in_specs=[half, half, half, whole, whole], out_specs=[half] * 4),
        out_shape=[jax.ShapeDtypeStruct(w.shape, F32)] * 4,
        compiler_params=_cparams(("arbitrary",)),
    )(core, w, m, v, mine, theirs)


ADA_COLS = N_MOD * D_MODEL // N_CHIPS


def _ada_fwd(c_all, w_ada, b_cols):
    def body(c_ref, w_ref, b_ref, o_ref):
        cv = c_ref[...]
        act = (cv * _sigmoid(cv)).astype(BF16)
        o_ref[...] = _dot(act, w_ref[...].astype(BF16)) + b_ref[...]

    return pl.pallas_call(
        body, name="ada_fwd", out_shape=jax.ShapeDtypeStruct((c_all.shape[0], ADA_COLS), F32),
        compiler_params=pltpu.CompilerParams(vmem_limit_bytes=VMEM_LIMIT),
    )(c_all, w_ada, b_cols)


def _ada_bwd(c_all, dmod_cols, w, m, v):
    nb = c_all.shape[0]
    tn = 384

    def body(c_ref, d_ref, w_ref, m_ref, v_ref, g_ref, dl_ref, m2_ref, v2_ref):
        cv = c_ref[...]
        act = (cv * _sigmoid(cv)).astype(BF16)
        g = _dot_tn(act, d_ref[...].astype(BF16))
        g_ref[...] = g
        dl_ref[...], m2_ref[...], v2_ref[...] = _adam_math(w_ref[...], g, m_ref[...], v_ref[...])

    blk = pl.BlockSpec((D_MODEL, tn), lambda j: (0, j))
    return pl.pallas_call(
        body, name="ada_bwd", grid=(ADA_COLS // tn,),
        in_specs=[pl.BlockSpec((nb, D_MODEL), lambda j: (0, 0)), pl.BlockSpec((nb, tn), lambda j: (0, j)), blk, blk, blk],
        out_specs=[blk] * 4, out_shape=[jax.ShapeDtypeStruct((D_MODEL, ADA_COLS), F32)] * 4,
        compiler_params=_cparams(("arbitrary",)),
    )(c_all, dmod_cols, w, m, v)


SMALL_NAMES = ("norm_ffn1", "norm_mix", "conv_w", "conv_b", "ssd_norm_w", "q_norm_w", "kv_norm_w", "mla_norm_w",
               "norm_ffn2", "norm_final", "dt_bias", "a_log", "d_skip")
SMALL_SIZES = (1024, 1024, CONV_WIDTH * D_CONV, D_CONV, 1024, Q_LORA, KV_LORA, 1024, 1024, 1024, 16, 16, 16)
SMALL_ROWS = 16
MOD_ROWS = 2 * N_MOD
SEND_ROWS = 40


def _pack_small(parts):
    flat = jnp.concatenate([parts[n].reshape(-1) for n in SMALL_NAMES])
    return jnp.pad(flat, (0, SMALL_ROWS * D_MODEL - flat.shape[0]))


def _unpack_small(flat):
    out, off = {}, 0
    for n, size in zip(SMALL_NAMES, SMALL_SIZES):
        out[n] = flat[off:off + size]
        off += size
    return out


def _small_sum(got):
    def body(g_ref, o_ref):
        bsum = jnp.zeros((N_MOD, D_MODEL), F32)
        ssum = jnp.zeros((SMALL_ROWS, D_MODEL), F32)
        for d in range(N_DEV):
            bsum = bsum + g_ref[d, 0:N_MOD, :] + g_ref[d, N_MOD:MOD_ROWS, :]
            ssum = ssum + g_ref[d, MOD_ROWS:MOD_ROWS + SMALL_ROWS, :]
        o_ref[...] = jnp.concatenate([bsum, ssum, jnp.zeros((32 - N_MOD - SMALL_ROWS, D_MODEL), F32)], axis=0)

    return pl.pallas_call(body, name="small_sum", out_shape=jax.ShapeDtypeStruct((32, D_MODEL), F32))(got)


BIG_NAMES = ("ffn1_w_gate", "ffn1_w_up", "ffn1_w_down", "w_in", "w_uq", "w_ukv", "w_out", "ffn2_w_gate", "ffn2_w_up",
             "ffn2_w_down")
_TO_KERNEL = {"w_in": _win_to_kernel, "w_uq": _wuq_to_kernel, "w_ukv": _wukv_to_kernel}
_FROM_KERNEL = {"w_in": _win_from_kernel, "w_uq": _wuq_from_kernel, "w_ukv": _wukv_from_kernel}


def _columns_joined(w4):
    n, r, c = w4.shape
    return w4.transpose(1, 0, 2).reshape(r, n * c)


def _columns_split(g):
    r, cols = g.shape
    return g.reshape(r, N_CHIPS, cols // N_CHIPS).transpose(1, 0, 2)


def kernel(x, c, positions, w_ada, b_ada, norm_ffn1, ffn1_w_gate, ffn1_w_up, ffn1_w_down, norm_mix, w_in, conv_w, conv_b, dt_bias, a_log, d_skip, ssd_norm_w, q_norm_w, w_uq, kv_norm_w, w_ukv, mla_norm_w, w_out, norm_ffn2, ffn2_w_gate, ffn2_w_up, ffn2_w_down, norm_final, loss_target, m_w_ada, m_b_ada, m_norm_ffn1, m_ffn1_w_gate, m_ffn1_w_up, m_ffn1_w_down, m_norm_mix, m_w_in, m_conv_w, m_conv_b, m_dt_bias, m_a_log, m_d_skip, m_ssd_norm_w, m_q_norm_w, m_w_uq, m_kv_norm_w, m_w_ukv, m_mla_norm_w, m_w_out, m_norm_ffn2, m_ffn2_w_gate, m_ffn2_w_up, m_ffn2_w_down, m_norm_final, v_w_ada, v_b_ada, v_norm_ffn1, v_ffn1_w_gate, v_ffn1_w_up, v_ffn1_w_down, v_norm_mix, v_w_in, v_conv_w, v_conv_b, v_dt_bias, v_a_log, v_d_skip, v_ssd_norm_w, v_q_norm_w, v_w_uq, v_kv_norm_w, v_w_ukv, v_mla_norm_w, v_w_out, v_norm_ffn2, v_ffn2_w_gate, v_ffn2_w_up, v_ffn2_w_down, v_norm_final):
    a = dict(locals())
    held_transposed = ("ffn1_w_gate", "ffn1_w_up", "ffn2_w_gate", "ffn2_w_up", "w_in")
    for n in held_transposed:
        for p in ("", "m_", "v_"):
            a[p + n] = a[p + n].transpose(0, 2, 1)
    B, S, D = x.shape
    mx, my, mc = _place()
    chip = 2 * mx + my
    dev = 2 * chip + mc
    core = mc.astype(jnp.int32).reshape(1)

    first = ("ffn1_w_gate", "ffn1_w_up", "ffn1_w_down")
    later = tuple(n for n in BIG_NAMES if n not in first)
    first_flight = _gather_start([a[n][0].astype(BF16) for n in first], "first")

    cw_rows = jnp.pad(conv_w[0], ((0, 0), (0, D - conv_w.shape[2])))
    got, _ = _all_gather_small(
        jnp.concatenate([c + first_flight[4][0, 0], cw_rows, jnp.zeros((8 - B - CONV_WIDTH, D), F32)], axis=0), "gather_c")
    c_all = got[:, :B, :].reshape(N_DEV * B, D)
    conv_full = got[::2, B:B + CONV_WIDTH, :conv_w.shape[2]].transpose(1, 0, 2).reshape(CONV_WIDTH, D_CONV)

    b_cols = lax.dynamic_slice(b_ada, (0, chip * ADA_COLS), (1, ADA_COLS))
    mod_all, mod_done = _all_gather_small(_ada_fwd(c_all, w_ada[0], b_cols), "gather_mod")
    mod = lax.dynamic_slice(mod_all, (0, B * dev, 0), (N_DEV, B, ADA_COLS))[::2].transpose(1, 0, 2).reshape(B, N_MOD * D)

    send_sems, recv_sems, shards, lands, _ = first_flight
    lands = _gather_wait(send_sems, recv_sems, shards, lands, mod_done, "first")
    got_first, gathered = _gather_finish([a[n][0].astype(BF16) for n in first], lands, "first")
    w = dict(zip(first, got_first))
    in_flight = _gather_start([(a[n][0] + gathered[0, 0]).astype(BF16) for n in later], "later")

    def later_weights(after):
        send_sems, recv_sems, shards, lands, _ = in_flight
        lands = _gather_wait(send_sems, recv_sems, shards, lands, after, "later")
        wl = dict(zip(later, _gather_finish([a[n][0].astype(BF16) for n in later], lands, "later")[0]))
        for n, to_kernel in _TO_KERNEL.items():
            wl[n] = to_kernel(wl[n].reshape(-1, D) if n in held_transposed else _columns_joined(wl[n]))
        wl["w_out"] = wl["w_out"].reshape(D_SSD + D_MLA, D)
        return wl

    small = {n: a[n].reshape(1, -1) for n in SMALL_NAMES if n not in ("conv_w", "norm_final")}
    small["conv_w"], small["norm_final"] = conv_full, norm_final

    def shards_of(names, gw):
        g4 = []
        for n in names:
            g = gw[n]
            if n in _FROM_KERNEL:
                g = _FROM_KERNEL[n](g) if n in held_transposed else _columns_split(_FROM_KERNEL[n](g))
            g4.append(g.reshape(N_CHIPS, a[n].shape[1], a[n].shape[2]))
        return g4

    def scatter_group(names, g4, swapped):
        pair = _pair_sums(g4, swapped, "pair_sums_" + names[0])
        return (names,) + tuple(_scatter_start(pair, names[0]))

    grads, deltas, new_m, new_v = {}, {}, {}, {}

    def finish_groups(some, after):
        names, owns, gots = [], [], []
        for group_names, send_sems, recv_sems, pair, lands, _ in some:
            pair, lands = _scatter_wait(send_sems, recv_sems, pair, lands, after, group_names[0])
            names += group_names
            owns += pair
            gots += lands
        mine, theirs = _reduce_join(owns, gots, "reduce_join_" + names[0])
        for n, own, other in zip(names, mine, theirs):
            grads[n], deltas[n], new_m[n], new_v[n] = _adam_halves(a[n], a["m_" + n], a["v_" + n], own, other, core, "adam_" + n)
        return deltas[names[-1]]

    swapping, groups = [], []

    def on_grads(names, gw):
        send_sems, recv_sems, g4, lands, token = _swap_start(shards_of(names, gw), names[0])
        swapping.append((names, send_sems, recv_sems, g4, lands))
        return token[0, 0]

    def sync(after):
        token = 0.0
        while swapping:
            names, send_sems, recv_sems, g4, lands = swapping.pop(0)
            g4, swapped = _swap_wait(send_sems, recv_sems, g4, lands, after, names[0])
            groups.append(scatter_group(names, g4, swapped))
            token = groups[-1][5][0, 0]
        return token

    loss_blk, grad_x, gw, dmod, gs = _local_step(x, positions, mod + in_flight[4][0, 0], w, later_weights, small, loss_target,
                                                 on_grads, sync)

    small_flat = _pack_small(gs).at[-1].set(loss_blk[0, 0])
    send = jnp.concatenate([dmod.reshape(MOD_ROWS, D), small_flat.reshape(SMALL_ROWS, D),
                            jnp.zeros((SEND_ROWS - MOD_ROWS - SMALL_ROWS, D), F32)], axis=0)
    got, _ = _all_gather_small(send, "gather_small")
    summed = _small_sum(got)
    sums = summed[N_MOD:N_MOD + SMALL_ROWS].reshape(-1)
    loss = sums[-1]
    gsmall = _unpack_small(sums)
    gsmall["conv_w"] = lax.dynamic_slice(gsmall["conv_w"].reshape(CONV_WIDTH, D_CONV), (0, chip * conv_w.shape[2]),
                                         (CONV_WIDTH, conv_w.shape[2]))
    gsmall["b_ada"] = summed[:N_MOD]
    names = ("b_ada",) + SMALL_NAMES
    rows = 208

    def pack(parts):
        flat = jnp.concatenate([parts[n].reshape(-1) for n in names])
        return jnp.pad(flat, (0, rows * LANES - flat.shape[0])).reshape(rows, LANES)

    packed = [pack({n: a[p + n] for n in names}) for p in ("", "m_", "v_")]
    g_p = pack(gsmall)
    outs = (g_p,) + tuple(_adam(packed[0], g_p, packed[1], packed[2], "adam_small"))
    for dst, flat in zip((grads, deltas, new_m, new_v), outs):
        flat, off = flat.reshape(-1), 0
        for n in names:
            dst[n] = flat[off:off + a[n].size].reshape(a[n].shape)
            off += a[n].size

    dmod_all = got[:, :MOD_ROWS, :].reshape(N_DEV * B, N_MOD * D)
    dmod_cols = lax.dynamic_slice(dmod_all, (0, chip * ADA_COLS), (N_DEV * B, ADA_COLS))
    ada = _ada_bwd(c_all, dmod_cols, w_ada[0], m_w_ada[0], v_w_ada[0])
    for dst, t in zip((grads, deltas, new_m, new_v), ada):
        dst["w_ada"] = t[None]

    g4 = shards_of(first, gw)
    last = scatter_group(first, g4, _swap_halves(g4, summed, "swap_" + first[0]))
    finish_groups([last], finish_groups(groups, last[5]))
    for dst in (grads, deltas, new_m, new_v):
        for n in held_transposed:
            dst[n] = dst[n].transpose(0, 2, 1)

    order = ("w_ada", "b_ada", "norm_ffn1", "ffn1_w_gate", "ffn1_w_up", "ffn1_w_down", "norm_mix", "w_in", "conv_w", "conv_b",
             "dt_bias", "a_log", "d_skip", "ssd_norm_w", "q_norm_w", "w_uq", "kv_norm_w", "w_ukv", "mla_norm_w", "w_out",
             "norm_ffn2", "ffn2_w_gate", "ffn2_w_up", "ffn2_w_down", "norm_final")
    return (loss, grad_x, *[grads[n] for n in order], *[deltas[n] for n in order], *[new_m[n] for n in order],
            *[new_v[n] for n in order])
```

```python
import functools
import math

import jax
import jax.numpy as jnp
import numpy as np
from jax import lax
from jax.experimental import pallas as pl
from jax.experimental.pallas import tpu as pltpu

F32 = jnp.float32
BF16 = jnp.bfloat16

D_MODEL = 1024
D_FF = 2816
D_SSD = 1024
D_MLA = 1024
SSD_HEADS = 16
SSD_HEAD_DIM = 64
SSD_GROUPS = 2
SSD_STATE = 128
CONV_WIDTH = 4
CHUNK = 128
MLA_HEADS = 8
QK_NOPE = 64
QK_ROPE = 32
QK_DIM = QK_NOPE + QK_ROPE
V_HEAD = 128
Q_LORA = 384
KV_LORA = 256
ROPE_THETA = 10000.0
N_MOD = 9
EPS = 1e-6
D_CONV = D_SSD + 2 * SSD_GROUPS * SSD_STATE
D_PROJ = 3328
HEAD_LANES = 128
ADAM_LR = 0.001
ADAM_B1 = 0.9
ADAM_B2 = 0.999
ADAM_EPS = 1e-08
ADAM_WD = 0.01
ADAM_STEP = 10

LANES = 128
VMEM_LIMIT = 56 * 1024 * 1024
TOKEN_TILE = 512
WIDE_TOKEN_TILE = 1024
WGRAD_TOKEN_TILE = 2048
ATTN_FWD_Q_TILE = 1024
ATTN_FWD_KV_TILE = 1024
ATTN_BWD_TILE = 1024
N_CHIPS = 4
N_DEV = 8

MESH = pl.DeviceIdType.MESH


def _dot(a, b):
    return jnp.dot(a, b, preferred_element_type=F32)


def _dot_nt(a, b):
    return lax.dot_general(a, b, (((1,), (1,)), ((), ())), preferred_element_type=F32)


def _dot_tn(a, b):
    return lax.dot_general(a, b, (((0,), (0,)), ((), ())), preferred_element_type=F32)


def _cparams(semantics):
    return pltpu.CompilerParams(dimension_semantics=semantics, vmem_limit_bytes=VMEM_LIMIT)


def _resident(shape):
    zeros = (0,) * len(shape)
    return pl.BlockSpec(shape, lambda *_: zeros, pipeline_mode=pl.Buffered(1))


def _sigmoid(x):
    return jax.nn.sigmoid(x)


def _rms_stats(x):
    r = lax.rsqrt(jnp.mean(x * x, axis=-1, keepdims=True) + EPS)
    return x * r, r


def _rms_bwd(dn, xh, r, w):
    dxh = dn * w
    dx = r * (dxh - xh * jnp.mean(dxh * xh, axis=-1, keepdims=True))
    return dx, dn * xh


def _colsum(v):
    return jnp.sum(v, axis=0, keepdims=True)


def _ffn_fwd(x, nw, sh, sc, g, wg, wu, wd, seq, name, head=None):
    T, D = x.shape
    fs = wg.shape[1]
    tm = min(TOKEN_TILE, seq)
    tps = seq // tm

    def body(x_ref, nw_ref, sh_ref, sc_ref, g_ref, wg_ref, wu_ref, wd_ref, *rest):
        if head is None:
            xo_ref, a_ref, u_ref, f_ref = rest
        else:
            nf_ref, t_ref, xo_ref, a_ref, u_ref, f_ref, loss_ref, dnf_ref = rest

            @pl.when(pl.program_id(0) == 0)
            def _():
                loss_ref[...] = jnp.zeros_like(loss_ref)
                dnf_ref[...] = jnp.zeros_like(dnf_ref)

        xv = x_ref[...]
        xh, _ = _rms_stats(xv)
        h = (xh * nw_ref[...]) * (1.0 + sc_ref[0]) + sh_ref[0]
        hb = h.astype(BF16)
        f = jnp.zeros((tm, D), F32)
        for j in range(N_CHIPS):
            a = _dot_nt(hb, wg_ref[j])
            u = _dot_nt(hb, wu_ref[j])
            a_ref[j] = a.astype(BF16)
            u_ref[j] = u.astype(BF16)
            f = f + _dot((a * _sigmoid(a) * u).astype(BF16), wd_ref[j])
        f_ref[...] = f.astype(BF16)
        xo = xv + 0.5 * g_ref[0] * f
        if head is None:
            xo_ref[...] = xo
        else:
            xh, r = _rms_stats(xo)
            nfv = nf_ref[...]
            err = xh * nfv - t_ref[...]
            loss_ref[...] += (0.5 / D) * jnp.sum(err * err)
            dxo, dw_rows = _rms_bwd(err * (1.0 / D), xh, r, nfv)
            xo_ref[...] = dxo
            dnf_ref[...] += _colsum(dw_rows)

    rows = lambda n: pl.BlockSpec((tm, n), lambda i: (i, 0))
    act = pl.BlockSpec((N_CHIPS, tm, fs), lambda i: (0, i, 0))
    perb = pl.BlockSpec((1, 1, D), lambda i: (i // tps, 0, 0))
    sd = jax.ShapeDtypeStruct
    in_specs = [rows(D), _resident((1, D)), perb, perb, perb, _resident((N_CHIPS, fs, D)), _resident((N_CHIPS, fs, D)),
                _resident((N_CHIPS, fs, D))]
    out_specs = [rows(D), act, act, rows(D)]
    out_shape = [sd((T, D), F32), sd((N_CHIPS, T, fs), BF16), sd((N_CHIPS, T, fs), BF16), sd((T, D), BF16)]
    if head is not None:
        in_specs += [_resident((1, D)), rows(D)]
        out_specs += [pl.BlockSpec((8, LANES), lambda i: (0, 0)), pl.BlockSpec((1, D), lambda i: (0, 0))]
        out_shape += [sd((8, LANES), F32), sd((1, D), F32)]
    return pl.pallas_call(
        body, grid=(T // tm,), name=name, in_specs=in_specs, out_specs=out_specs, out_shape=out_shape,
        compiler_params=_cparams(("arbitrary",)),
    )(x, nw, sh, sc, g, wg, wu, wd, *(head or ()))


def _ffn_bwd(dxo, x, nw, sh, sc, g, a, u, f, wg, wu, wd, seq, name):
    T, D = x.shape
    fs = wg.shape[1]
    B = T // seq
    tm = min(TOKEN_TILE // 2, seq)
    tps = seq // tm

    def body(dxo_ref, x_ref, nw_ref, sh_ref, sc_ref, g_ref, a_ref, u_ref, f_ref, wg_ref, wu_ref, wd_ref,
             dx_ref, h_ref, s_ref, df_ref, da_ref, du_ref, dsh_ref, dsc_ref, dg_ref, dnw_ref):
        i = pl.program_id(0)

        @pl.when(i % tps == 0)
        def _():
            dsh_ref[...] = jnp.zeros_like(dsh_ref)
            dsc_ref[...] = jnp.zeros_like(dsc_ref)
            dg_ref[...] = jnp.zeros_like(dg_ref)

        @pl.when(i == 0)
        def _():
            dnw_ref[...] = jnp.zeros_like(dnw_ref)

        dxo_v = dxo_ref[...]
        dfb = (0.5 * g_ref[0] * dxo_v).astype(BF16)
        dg_ref[0] += _colsum(0.5 * dxo_v * f_ref[...].astype(F32))
        dh = jnp.zeros((tm, D), F32)
        for j in range(N_CHIPS):
            ds = _dot_nt(dfb, wd_ref[j])
            av = a_ref[j].astype(F32)
            uv = u_ref[j].astype(F32)
            sig = _sigmoid(av)
            sil = av * sig
            dab = (ds * uv * (sig * (1.0 + av * (1.0 - sig)))).astype(BF16)
            dub = (ds * sil).astype(BF16)
            dh = dh + _dot(dab, wg_ref[j]) + _dot(dub, wu_ref[j])
            s_ref[j] = (sil * uv).astype(BF16)
            da_ref[j] = dab
            du_ref[j] = dub
        xv = x_ref[...]
        xh, r = _rms_stats(xv)
        nwv = nw_ref[...]
        n = xh * nwv
        scale1 = 1.0 + sc_ref[0]
        dsc_ref[0] += _colsum(dh * n)
        dsh_ref[0] += _colsum(dh)
        dx, dw_rows = _rms_bwd(dh * scale1, xh, r, nwv)
        dnw_ref[...] += _colsum(dw_rows)
        dx_ref[...] = dxo_v + dx
        h_ref[...] = (n * scale1 + sh_ref[0]).astype(BF16)
        df_ref[...] = dfb

    rows = lambda n: pl.BlockSpec((tm, n), lambda i: (i, 0))
    act = pl.BlockSpec((N_CHIPS, tm, fs), lambda i: (0, i, 0))
    perb = pl.BlockSpec((1, 1, D), lambda i: (i // tps, 0, 0))
    sd = jax.ShapeDtypeStruct
    return pl.pallas_call(
        body, grid=(T // tm,), name=name,
        in_specs=[rows(D), rows(D), _resident((1, D)), perb, perb, perb, act, act, rows(D),
                  _resident((N_CHIPS, fs, D)), _resident((N_CHIPS, fs, D)), _resident((N_CHIPS, fs, D))],
        out_specs=[rows(D), rows(D), act, rows(D), act, act, perb, perb, perb, pl.BlockSpec((1, D), lambda i: (0, 0))],
        out_shape=[sd((T, D), F32), sd((T, D), BF16), sd((N_CHIPS, T, fs), BF16), sd((T, D), BF16),
                   sd((N_CHIPS, T, fs), BF16), sd((N_CHIPS, T, fs), BF16), sd((B, 1, D), F32), sd((B, 1, D), F32),
                   sd((B, 1, D), F32), sd((1, D), F32)],
        compiler_params=_cparams(("arbitrary",)),
    )(dxo, x, nw, sh, sc, g, a, u, f, wg, wu, wd)


def _ffn_wgrad(h, s, df, da, du, after, name):
    T, D = h.shape
    fs = s.shape[2]
    tt = min(WGRAD_TOKEN_TILE, T)
    nt = T // tt

    def body(h_ref, s_ref, df_ref, da_ref, du_ref, after_ref, dgate_ref, dup_ref, ddown_ref, gate_acc, up_acc, down_acc):
        @pl.when(pl.program_id(1) == 0)
        def _():
            gate_acc[...] = jnp.zeros_like(gate_acc)
            up_acc[...] = jnp.zeros_like(up_acc)
            down_acc[...] = jnp.zeros_like(down_acc)

        hv = h_ref[...]
        gate_acc[...] += _dot_tn(da_ref[0], hv)
        up_acc[...] += _dot_tn(du_ref[0], hv)
        down_acc[...] += _dot_tn(s_ref[0], df_ref[...])

        @pl.when(pl.program_id(1) == nt - 1)
        def _():
            dgate_ref[0] = gate_acc[...].astype(BF16)
            dup_ref[0] = up_acc[...].astype(BF16)
            ddown_ref[0] = down_acc[...].astype(BF16)

    rows = pl.BlockSpec((tt, D), lambda j, t: (t, 0))
    act = pl.BlockSpec((1, tt, fs), lambda j, t: (j, t, 0))
    shard = pl.BlockSpec((1, fs, D), lambda j, t: (j, 0, 0))
    return pl.pallas_call(
        body, grid=(N_CHIPS, nt), name=name,
        in_specs=[rows, act, rows, act, act, pl.BlockSpec(memory_space=pl.ANY)],
        out_specs=[shard] * 3, out_shape=[jax.ShapeDtypeStruct((N_CHIPS, fs, D), BF16)] * 3,
        scratch_shapes=[pltpu.VMEM((fs, D), F32)] * 3,
        compiler_params=_cparams(("arbitrary", "arbitrary")),
    )(h, s, df, da, du, after)


def _mm_tn(xa, ya, tn, name):
    T, K = xa.shape
    N = ya.shape[1]
    tt = min(WGRAD_TOKEN_TILE, T)
    nt = T // tt

    def body(x_ref, y_ref, o_ref, acc_ref):
        @pl.when(pl.program_id(1) == 0)
        def _():
            acc_ref[...] = jnp.zeros_like(acc_ref)

        acc_ref[...] += _dot_tn(x_ref[...], y_ref[...])

        @pl.when(pl.program_id(1) == nt - 1)
        def _():
            o_ref[...] = acc_ref[...].astype(BF16)

    return pl.pallas_call(
        body, grid=(N // tn, nt), name=name,
        in_specs=[pl.BlockSpec((tt, K), lambda j, t: (t, 0)), pl.BlockSpec((tt, tn), lambda j, t: (t, j))],
        out_specs=pl.BlockSpec((K, tn), lambda j, t: (0, j)),
        out_shape=jax.ShapeDtypeStruct((K, N), BF16),
        scratch_shapes=[pltpu.VMEM((K, tn), F32)],
        compiler_params=_cparams(("arbitrary", "arbitrary")),
    )(xa, ya)


_PROJ_SPLITS = (0, 1024, 2560, 2944, 3200, 3328)


def _inproj_fwd(x, nw, sh, sc, win, seq):
    T, D = x.shape
    tm = min(WIDE_TOKEN_TILE, seq)
    tps = seq // tm
    widths = [b - a for a, b in zip(_PROJ_SPLITS[:-1], _PROJ_SPLITS[1:])]
    dtypes = [BF16, BF16, F32, F32, F32]

    def body(x_ref, nw_ref, sh_ref, sc_ref, w_ref, *outs):
        xh, _ = _rms_stats(x_ref[...])
        h = (xh * nw_ref[...]) * (1.0 + sc_ref[0]) + sh_ref[0]
        proj = _dot_nt(h.astype(BF16), w_ref[...])
        for o, lo, hi in zip(outs, _PROJ_SPLITS[:-1], _PROJ_SPLITS[1:]):
            o[...] = proj[:, lo:hi].astype(o.dtype)

    rows = lambda n: pl.BlockSpec((tm, n), lambda i: (i, 0))
    perb = pl.BlockSpec((1, 1, D), lambda i: (i // tps, 0, 0))
    return pl.pallas_call(
        body, grid=(T // tm,), name="inproj_fwd",
        in_specs=[rows(D), _resident((1, D)), perb, perb, _resident((D_PROJ, D))],
        out_specs=[rows(w) for w in widths],
        out_shape=[jax.ShapeDtypeStruct((T, w), dt) for w, dt in zip(widths, dtypes)],
        compiler_params=_cparams(("arbitrary",)),
    )(x, nw, sh, sc, win)


def _inproj_bwd(dx2, x, nw, sh, sc, win, dz, dxbc, dcq, dckv, ddtk_a, ddtk_b, seq):
    T, D = x.shape
    B = T // seq
    tm = min(TOKEN_TILE, seq)
    tps = seq // tm

    def body(dx2_ref, x_ref, nw_ref, sh_ref, sc_ref, w_ref, dz_ref, dxbc_ref, dcq_ref, dckv_ref, da_ref, db_ref,
             dx_ref, h_ref, dp_ref, dsh_ref, dsc_ref, dnw_ref):
        i = pl.program_id(0)

        @pl.when(i % tps == 0)
        def _():
            dsh_ref[...] = jnp.zeros_like(dsh_ref)
            dsc_ref[...] = jnp.zeros_like(dsc_ref)

        @pl.when(i == 0)
        def _():
            dnw_ref[...] = jnp.zeros_like(dnw_ref)

        dproj = jnp.concatenate(
            [dz_ref[...], dxbc_ref[...], dcq_ref[...].astype(BF16), dckv_ref[...].astype(BF16),
             (da_ref[...] + db_ref[...]).astype(BF16)], axis=1)
        dp_ref[...] = dproj
        dh = _dot(dproj, w_ref[...])
        xh, r = _rms_stats(x_ref[...])
        nwv = nw_ref[...]
        n = xh * nwv
        scale1 = 1.0 + sc_ref[0]
        dsc_ref[0] += _colsum(dh * n)
        dsh_ref[0] += _colsum(dh)
        dx, dw_rows = _rms_bwd(dh * scale1, xh, r, nwv)
        dnw_ref[...] += _colsum(dw_rows)
        dx_ref[...] = dx2_ref[...] + dx
        h_ref[...] = (n * scale1 + sh_ref[0]).astype(BF16)

    rows = lambda n: pl.BlockSpec((tm, n), lambda i: (i, 0))
    perb = pl.BlockSpec((1, 1, D), lambda i: (i // tps, 0, 0))
    sd = jax.ShapeDtypeStruct
    return pl.pallas_call(
        body, grid=(T // tm,), name="inproj_bwd",
        in_specs=[rows(D), rows(D), _resident((1, D)), perb, perb, _resident((D_PROJ, D)),
                  rows(1024), rows(D_CONV), rows(Q_LORA), rows(KV_LORA), rows(LANES), rows(LANES)],
        out_specs=[rows(D), rows(D), rows(D_PROJ), perb, perb, pl.BlockSpec((1, D), lambda i: (0, 0))],
        out_shape=[sd((T, D), F32), sd((T, D), BF16), sd((T, D_PROJ), BF16), sd((B, 1, D), F32), sd((B, 1, D), F32),
                   sd((1, D), F32)],
        compiler_params=_cparams(("arbitrary",)),
    )(dx2, x, nw, sh, sc, win, dz, dxbc, dcq, dckv, ddtk_a, ddtk_b)


SUBLANES = 8


def _shift_down(v, k):
    r = pltpu.roll(v, k, 0)
    row = lax.broadcasted_iota(jnp.int32, (SUBLANES, v.shape[1]), 0)
    return jnp.concatenate([jnp.where(row < k, 0.0, r[:SUBLANES]), r[SUBLANES:]], axis=0)


def _shift_up(v, k):
    n = v.shape[0]
    r = pltpu.roll(v, n - k, 0)
    row = lax.broadcasted_iota(jnp.int32, (SUBLANES, v.shape[1]), 0)
    return jnp.concatenate([r[:n - SUBLANES], jnp.where(row >= SUBLANES - k, 0.0, r[n - SUBLANES:])], axis=0)


def _conv_pre(xv, w_ref, b_ref):
    pre = b_ref[...] + w_ref[CONV_WIDTH - 1:CONV_WIDTH, :] * xv
    for k in range(1, CONV_WIDTH):
        pre = pre + w_ref[CONV_WIDTH - 1 - k:CONV_WIDTH - k, :] * _shift_down(xv, k)
    return pre


def _conv_fwd(xraw, cw, cb):
    B, S, C = xraw.shape

    def body(x_ref, w_ref, b_ref, o_ref):
        pre = _conv_pre(x_ref[0].astype(F32), w_ref, b_ref)
        o_ref[0] = (pre * _sigmoid(pre)).astype(BF16)

    blk = pl.BlockSpec((1, S, LANES), lambda b, j: (b, 0, j))
    return pl.pallas_call(
        body, grid=(B, C // LANES), name="conv_fwd",
        in_specs=[blk, pl.BlockSpec((CONV_WIDTH, LANES), lambda b, j: (0, j)), pl.BlockSpec((1, LANES), lambda b, j: (0, j))],
        out_specs=blk, out_shape=jax.ShapeDtypeStruct((B, S, C), BF16),
        compiler_params=_cparams(("arbitrary", "arbitrary")),
    )(xraw, cw, cb)


def _conv_bwd(dout, xraw, cw, cb):
    B, S, C = xraw.shape

    def body(d_ref, x_ref, w_ref, b_ref, dx_ref, dw_ref, db_ref):
        @pl.when(pl.program_id(1) == 0)
        def _():
            dw_ref[...] = jnp.zeros_like(dw_ref)
            db_ref[...] = jnp.zeros_like(db_ref)

        xv = x_ref[0].astype(F32)
        pre = _conv_pre(xv, w_ref, b_ref)
        sig = _sigmoid(pre)
        dpre = d_ref[0].astype(F32) * (sig * (1.0 + pre * (1.0 - sig)))
        dx = w_ref[CONV_WIDTH - 1:CONV_WIDTH, :] * dpre
        for k in range(1, CONV_WIDTH):
            dx = dx + w_ref[CONV_WIDTH - 1 - k:CONV_WIDTH - k, :] * _shift_up(dpre, k)
        dx_ref[0] = dx.astype(BF16)
        db_ref[...] += _colsum(dpre)
        dws = [_colsum(dpre * (xv if k == 0 else _shift_down(xv, k))) for k in range(CONV_WIDTH - 1, -1, -1)]
        dw_ref[...] += jnp.concatenate(dws, axis=0)

    blk = pl.BlockSpec((1, S, LANES), lambda j, b: (b, 0, j))
    wspec = pl.BlockSpec((CONV_WIDTH, LANES), lambda j, b: (0, j))
    bspec = pl.BlockSpec((1, LANES), lambda j, b: (0, j))
    return pl.pallas_call(
        body, grid=(C // LANES, B), name="conv_bwd",
        in_specs=[blk, blk, wspec, bspec], out_specs=[blk, wspec, bspec],
        out_shape=[jax.ShapeDtypeStruct((B, S, C), BF16), jax.ShapeDtypeStruct((CONV_WIDTH, C), F32),
                   jax.ShapeDtypeStruct((1, C), F32)],
        compiler_params=_cparams(("arbitrary", "arbitrary")),
    )(dout, xraw, cw, cb)


def _softplus(x):
    return jnp.maximum(x, 0.0) + jnp.log(1.0 + jnp.exp(-jnp.abs(x)))


def _ssd_common(xbc_ref, dtk_ref, dtb_ref, alog_ref, e_ref):
    L = CHUNK
    xbc = xbc_ref[0]
    xs = xbc[:, :D_SSD].astype(F32)
    bm = xbc[:, D_SSD:D_SSD + 256]
    cm = xbc[:, D_SSD + 256:D_SSD + 512]
    head = lax.broadcasted_iota(jnp.int32, (1, LANES), 1) < SSD_HEADS
    a128 = jnp.where(head, -jnp.exp(alog_ref[...]), 0.0)
    pre = dtk_ref[0] + dtb_ref[...]
    dt = _softplus(pre)
    dA = dt * a128
    row = lax.broadcasted_iota(jnp.int32, (L, L), 0)
    col = lax.broadcasted_iota(jnp.int32, (L, L), 1)
    causal = col <= row
    tri = causal.astype(F32)
    triT = (row <= col).astype(F32)
    tri = causal.astype(BF16)
    triT = (row <= col).astype(BF16)
    dA3 = _split3(dA)
    acum = _sum3(lambda part: _dot(tri, part), dA3)
    acumT = _sum3(lambda part: _dot_tn(part, triT), dA3)
    E = e_ref[...]
    acum_f = _spread(acum, E)
    dt_f = _spread(dt, E)
    e_f = jnp.exp(acum_f)
    w_f = jnp.exp(acum_f[L - 1:L, :] - acum_f)
    xt = xs * dt_f
    return dict(xs=xs, bm=bm, cm=cm, a128=a128, pre=pre, dt=dt, causal=causal, tri=tri, triT=triT, acum=acum,
                acumT=acumT, E=E, dt_f=dt_f, e_f=e_f, w_f=w_f, xt=xt, head=head)


def _split3(x):
    p1 = x.astype(BF16)
    r1 = x - p1.astype(F32)
    p2 = r1.astype(BF16)
    return p1, p2, (r1 - p2.astype(F32)).astype(BF16)


def _sum3(mm, parts):
    return (mm(parts[0]) + mm(parts[1])) + mm(parts[2])


def _spread(v, e):
    return _sum3(lambda part: _dot(part, e), _split3(v))


def _gather_heads(v, e):
    return _sum3(lambda part: _dot_nt(part, e), _split3(v))


def _head_mask(k):
    lane = lax.broadcasted_iota(jnp.int32, (CHUNK, LANES), 1)
    return (lane >= SSD_HEAD_DIM) if k == 1 else (lane < SSD_HEAD_DIM)


def _pair_decay(alast, h0):
    row = lax.broadcasted_iota(jnp.int32, (2 * SSD_HEAD_DIM, SSD_STATE), 0)
    return jnp.exp(jnp.where(row < SSD_HEAD_DIM, alast[:, h0:h0 + 1], alast[:, h0 + 1:h0 + 2]))


def _decay_matrix(q, h):
    seg = q["acum"][:, h:h + 1] - q["acumT"][h:h + 1, :]
    return jnp.exp(jnp.where(q["causal"], seg, -1e30))


def _gated_norm(y, zz, nw):
    sig = _sigmoid(zz)
    sil = zz * sig
    yg = y * sil
    half = D_SSD // SSD_GROUPS
    parts = []
    for g in range(SSD_GROUPS):
        xh, r = _rms_stats(yg[:, g * half:(g + 1) * half])
        parts.append((xh, r))
    return sig, sil, parts


def _ssd_fwd(xbc, dtk, z, dtb, alog, dsk, nw, expand):
    B, S, _ = xbc.shape
    L = CHUNK
    nc = S // L

    def body(xbc_ref, dtk_ref, z_ref, dtb_ref, alog_ref, dsk_ref, nw_ref, e_ref, y_ref, ys_ref, prev_ref, st_ref):
        @pl.when(pl.program_id(0) == 0)
        def _():
            st_ref[...] = jnp.zeros_like(st_ref)

        for b in range(B):
            one = lambda ref: ref.at[pl.ds(b, 1)]
            sequence_step(one(xbc_ref), one(dtk_ref), one(z_ref), dtb_ref, alog_ref, dsk_ref, nw_ref, e_ref, one(y_ref),
                          one(ys_ref), one(prev_ref), st_ref.at[b])

    def sequence_step(xbc_ref, dtk_ref, z_ref, dtb_ref, alog_ref, dsk_ref, nw_ref, e_ref, y_ref, ys_ref, prev_ref, st_ref):
        q = _ssd_common(xbc_ref, dtk_ref, dtb_ref, alog_ref, e_ref)
        xtb = q["xt"].astype(BF16)
        xwb = (q["xt"] * q["w_f"]).astype(BF16)
        alast = q["acum"][L - 1:L, :]
        ys = []
        for g in range(SSD_GROUPS):
            bg = q["bm"][:, g * 128:(g + 1) * 128]
            cg = q["cm"][:, g * 128:(g + 1) * 128]
            G = _dot_nt(cg, bg)
            for pr in range(SSD_HEADS // SSD_GROUPS // 2):
                h0 = g * 8 + 2 * pr
                lo = h0 * SSD_HEAD_DIM
                xt_p = xtb[:, lo:lo + 128]
                ydiag = jnp.zeros((L, LANES), F32)
                for k in range(2):
                    M = (G * _decay_matrix(q, h0 + k)).astype(BF16)
                    ydiag = ydiag + _dot(M, jnp.where(_head_mask(k), xt_p, jnp.zeros_like(xt_p)))
                hp = st_ref[lo:lo + 128, :]
                prev_ref[0, 0, lo:lo + 128, :] = hp.astype(BF16)
                zoff = _dot_nt(cg, hp.astype(BF16))
                ys.append(ydiag + zoff * q["e_f"][:, lo:lo + 128])
                st_ref[lo:lo + 128, :] = _pair_decay(alast, h0) * hp + _dot_tn(xwb[:, lo:lo + 128], bg)
        y = jnp.concatenate(ys, axis=1) + dsk_ref[...] * q["xs"]
        y_ref[0] = y.astype(BF16)
        _, _, parts = _gated_norm(y, z_ref[0].astype(F32), nw_ref[...])
        half = D_SSD // SSD_GROUPS
        ys_ref[0] = jnp.concatenate(
            [xh * nw_ref[:, g * half:(g + 1) * half] for g, (xh, _) in enumerate(parts)], axis=1).astype(BF16)

    chunk = lambda n: pl.BlockSpec((B, L, n), lambda c: (0, c, 0))
    vec = pl.BlockSpec((1, LANES), lambda c: (0, 0))
    return pl.pallas_call(
        body, grid=(nc,), name="ssd_fwd",
        in_specs=[chunk(D_CONV), chunk(LANES), chunk(D_SSD), vec, vec, pl.BlockSpec((1, D_SSD), lambda c: (0, 0)),
                  pl.BlockSpec((1, D_SSD), lambda c: (0, 0)), pl.BlockSpec((LANES, D_SSD), lambda c: (0, 0))],
        out_specs=[chunk(D_SSD), chunk(D_SSD), pl.BlockSpec((B, 1, D_SSD, SSD_STATE), lambda c: (0, c, 0, 0))],
        out_shape=[jax.ShapeDtypeStruct((B, S, D_SSD), BF16), jax.ShapeDtypeStruct((B, S, D_SSD), BF16),
                   jax.ShapeDtypeStruct((B, nc, D_SSD, SSD_STATE), BF16)],
        scratch_shapes=[pltpu.VMEM((B, D_SSD, SSD_STATE), F32)],
        compiler_params=_cparams(("arbitrary",)),
    )(xbc, dtk, z, dtb, alog, dsk, nw, expand)


def _ssd_bwd(xbc, dtk, z, y, prev, dys, dtb, alog, dsk, nw, expand):
    B, S, _ = xbc.shape
    L = CHUNK
    nc = S // L
    half = D_SSD // SSD_GROUPS

    def body(xbc_ref, dtk_ref, z_ref, y_ref, prev_ref, dys_ref, dtb_ref, alog_ref, dsk_ref, nw_ref, e_ref,
             dxbc_ref, ddtk_ref, dz_ref, dnw_ref, dvec_ref, dh_ref, dskc_ref):
        @pl.when(pl.program_id(0) == 0)
        def _():
            dnw_ref[...] = jnp.zeros_like(dnw_ref)
            dvec_ref[...] = jnp.zeros_like(dvec_ref)
            dskc_ref[...] = jnp.zeros_like(dskc_ref)
            dh_ref[...] = jnp.zeros_like(dh_ref)

        for b in range(B):
            one = lambda ref: ref.at[pl.ds(b, 1)]
            sequence_step(one(xbc_ref), one(dtk_ref), one(z_ref), one(y_ref), one(prev_ref), one(dys_ref), dtb_ref, alog_ref,
                          dsk_ref, nw_ref, e_ref, one(dxbc_ref), one(ddtk_ref), one(dz_ref), dnw_ref, dvec_ref, dh_ref.at[b],
                          dskc_ref)

        @pl.when(pl.program_id(0) == nc - 1)
        def _():
            dvec_ref[2:3, :] = _gather_heads(jnp.broadcast_to(dskc_ref[...], (8, D_SSD)), e_ref[...])[0:1, :]

    def sequence_step(xbc_ref, dtk_ref, z_ref, y_ref, prev_ref, dys_ref, dtb_ref, alog_ref, dsk_ref, nw_ref, e_ref,
                      dxbc_ref, ddtk_ref, dz_ref, dnw_ref, dvec_ref, dh_ref, dskc_ref):
        q = _ssd_common(xbc_ref, dtk_ref, dtb_ref, alog_ref, e_ref)
        E = q["E"]
        xs = q["xs"]
        yv = y_ref[0].astype(F32)
        zz = z_ref[0].astype(F32)
        sig, sil, parts = _gated_norm(yv, zz, nw_ref[...])
        dn = dys_ref[0].astype(F32)
        dyg, dnw_rows = [], []
        for g, (xh, r) in enumerate(parts):
            dpart, dw_rows = _rms_bwd(dn[:, g * half:(g + 1) * half], xh, r, nw_ref[:, g * half:(g + 1) * half])
            dyg.append(dpart)
            dnw_rows.append(dw_rows)
        dyg = jnp.concatenate(dyg, axis=1)
        dnw_ref[...] += _colsum(jnp.concatenate(dnw_rows, axis=1))
        dY = dyg * sil
        dz_ref[0] = (dyg * yv * (sig * (1.0 + zz * (1.0 - sig)))).astype(BF16)
        dsk_f = dsk_ref[...]
        dskc_ref[...] += _colsum(dY * xs)
        dYb = dY.astype(BF16)
        xtb = q["xt"].astype(BF16)
        xwb = (q["xt"] * q["w_f"]).astype(BF16)
        acum = q["acum"]
        alast = acum[L - 1:L, :]
        lane_id = lax.broadcasted_iota(jnp.int32, (L, LANES), 1)
        sub_id = lax.broadcasted_iota(jnp.int32, (LANES, L), 0)
        lane_row = lax.broadcasted_iota(jnp.int32, (1, LANES), 1)
        da_rows = jnp.zeros((L, LANES), F32)
        daT = jnp.zeros((LANES, L), F32)
        dxt, prod_off, prod_st, dbs, dcs = [], [], [], [], []
        hsum_row = jnp.zeros((1, LANES), F32)
        for g in range(SSD_GROUPS):
            bg = q["bm"][:, g * 128:(g + 1) * 128]
            cg = q["cm"][:, g * 128:(g + 1) * 128]
            G = _dot_nt(cg, bg)
            dG = jnp.zeros((L, L), F32)
            dcg = jnp.zeros((L, SSD_STATE), F32)
            dbg = jnp.zeros((L, SSD_STATE), F32)
            for pr in range(SSD_HEADS // SSD_GROUPS // 2):
                h0 = g * 8 + 2 * pr
                lo = h0 * SSD_HEAD_DIM
                cols = slice(lo, lo + 128)
                dY_p = dYb[:, cols]
                xt_p = xtb[:, cols]
                dxt_p = jnp.zeros((L, LANES), F32)
                for k in range(2):
                    h = h0 + k
                    Lm = _decay_matrix(q, h)
                    Mf = G * Lm
                    dYk = jnp.where(_head_mask(k), dY_p, jnp.zeros_like(dY_p))
                    dM = _dot_nt(dYk, xt_p)
                    dxt_p = dxt_p + _dot_tn(Mf.astype(BF16), dYk)
                    dG = dG + dM * Lm
                    Q = dM * Mf
                    da_rows = da_rows + jnp.where(lane_id == h, jnp.sum(Q, axis=1, keepdims=True), 0.0)
                    daT = daT + jnp.where(sub_id == h, jnp.sum(Q, axis=0, keepdims=True), 0.0)
                hpb = prev_ref[0, 0, lo:lo + 128, :]
                hp = hpb.astype(F32)
                zoff = _dot_nt(cg, hpb)
                e_p = q["e_f"][:, cols]
                dY_pf = dY[:, cols]
                dZb = (dY_pf * e_p).astype(BF16)
                dcg = dcg + _dot(dZb, hpb)
                dhp_off = _dot_tn(dZb, cg)
                prod_off.append(dY_pf * zoff * e_p)
                dS = dh_ref[lo:lo + 128, :]
                dSb = dS.astype(BF16)
                U = _dot_nt(bg, dSb)
                dxt_p = dxt_p + U * q["w_f"][:, cols]
                dbg = dbg + _dot(xwb[:, cols], dSb)
                prod_st.append(q["xt"][:, cols] * U)
                dh_ref[lo:lo + 128, :] = _pair_decay(alast, h0) * dS + dhp_off
                dsh = dS * hp
                for k in range(2):
                    total = jnp.sum(dsh[k * SSD_HEAD_DIM:(k + 1) * SSD_HEAD_DIM, :], axis=(0, 1), keepdims=True)
                    hsum_row = hsum_row + jnp.where(lane_row == h0 + k, total, 0.0)
                dxt.append(dxt_p)
            dGb = dG.astype(BF16)
            dcs.append(dcg + _dot(dGb, bg))
            dbs.append(dbg + _dot_tn(dGb, cg))
        dxt = jnp.concatenate(dxt, axis=1)
        da_rows = da_rows + _gather_heads(jnp.concatenate(prod_off, axis=1), E)
        dww = _gather_heads(jnp.concatenate(prod_st, axis=1), E) * jnp.exp(alast - acum)
        da_rows = da_rows - dww
        dlast = _colsum(dww) + jnp.exp(alast) * hsum_row
        triT = q["triT"]
        ddA = (_sum3(lambda part: _dot(triT, part), _split3(da_rows))
               - _sum3(lambda part: _dot_nt(triT, part), _split3(daT)) + dlast)
        ddA = jnp.where(q["head"], ddA, 0.0)
        ddt = ddA * q["a128"] + _gather_heads(dxt * xs, E)
        ddt_raw = jnp.where(q["head"], ddt * _sigmoid(q["pre"]), 0.0)
        ddtk_ref[0] = ddt_raw
        dxs = dxt * q["dt_f"] + dsk_f * dY
        dxbc_ref[0] = jnp.concatenate([dxs] + dbs + dcs, axis=1).astype(BF16)
        dvec_ref[0:1, :] += _colsum(ddt_raw)
        dvec_ref[1:2, :] += _colsum(ddA * q["dt"]) * q["a128"]

    rev = lambda n: pl.BlockSpec((B, L, n), lambda c: (0, nc - 1 - c, 0))
    vec = pl.BlockSpec((1, LANES), lambda c: (0, 0))
    sd = jax.ShapeDtypeStruct
    return pl.pallas_call(
        body, grid=(nc,), name="ssd_bwd",
        in_specs=[rev(D_CONV), rev(LANES), rev(D_SSD), rev(D_SSD),
                  pl.BlockSpec((B, 1, D_SSD, SSD_STATE), lambda c: (0, nc - 1 - c, 0, 0)), rev(D_SSD), vec, vec,
                  pl.BlockSpec((1, D_SSD), lambda c: (0, 0)),
                  pl.BlockSpec((1, D_SSD), lambda c: (0, 0)), pl.BlockSpec((LANES, D_SSD), lambda c: (0, 0))],
        out_specs=[rev(D_CONV), rev(LANES), rev(D_SSD), pl.BlockSpec((1, D_SSD), lambda c: (0, 0)),
                   pl.BlockSpec((8, LANES), lambda c: (0, 0))],
        out_shape=[sd((B, S, D_CONV), BF16), sd((B, S, LANES), F32), sd((B, S, D_SSD), BF16), sd((1, D_SSD), F32),
                   sd((8, LANES), F32)],
        scratch_shapes=[pltpu.VMEM((B, D_SSD, SSD_STATE), F32), pltpu.VMEM((1, D_SSD), F32)],
        compiler_params=_cparams(("arbitrary",)),
    )(xbc, dtk, z, y, prev, dys, dtb, alog, dsk, nw, expand)


def _rope_tables(pos_ref, invf_ref, place_ref):
    ang = invf_ref[...] * pos_ref[0].astype(F32)
    place = place_ref[...]
    cosf = 1.0 + _sum3(lambda part: _dot_tn(part, place), _split3(jnp.cos(ang) - 1.0))
    sinf = _sum3(lambda part: _dot_tn(part, place), _split3(jnp.sin(ang)))
    return cosf, sinf


def _rot(u):
    lane = lax.broadcasted_iota(jnp.int32, u.shape, 1)
    first = (lane >= QK_NOPE) & (lane < QK_NOPE + QK_ROPE // 2)
    second = (lane >= QK_NOPE + QK_ROPE // 2) & (lane < QK_DIM)
    return jnp.where(first, -pltpu.roll(u, LANES - QK_ROPE // 2, 1), jnp.where(second, pltpu.roll(u, QK_ROPE // 2, 1), 0.0))


def _rope_lanes(shape):
    lane = lax.broadcasted_iota(jnp.int32, shape, 1)
    return (lane >= QK_NOPE) & (lane < QK_DIM)


def _mla_prep(cq, ckv, dtk, pos, qw, kvw, wuq, wukv, invf, place):
    T = cq.shape[0]
    tm = min(WIDE_TOKEN_TILE, T)
    scale = 1.0 / math.sqrt(QK_DIM)
    HW = MLA_HEADS * HEAD_LANES

    def body(cq_ref, ckv_ref, dtk_ref, pos_ref, qw_ref, kvw_ref, wuq_ref, wukv_ref, invf_ref, place_ref, q_ref, k_ref, v_ref,
             cos_ref, sin_ref):
        xh, _ = _rms_stats(cq_ref[...])
        qv = _dot((xh * qw_ref[...]).astype(BF16), wuq_ref[...])
        xh, _ = _rms_stats(ckv_ref[...])
        kv = _dot((xh * kvw_ref[...]).astype(BF16), wukv_ref[...])
        cosf, sinf = _rope_tables(pos_ref, invf_ref, place_ref)
        cos_ref[...] = cosf
        sin_ref[...] = sinf
        rope = lambda u: u * cosf + _rot(u) * sinf
        dtkv = dtk_ref[...]
        kr = rope(jnp.where(_rope_lanes(dtkv.shape), dtkv, 0.0))
        for h in range(MLA_HEADS):
            cols = slice(h * HEAD_LANES, (h + 1) * HEAD_LANES)
            q_ref[:, cols] = (rope(qv[:, cols]) * scale).astype(BF16)
            k_ref[:, cols] = (kv[:, cols] + kr).astype(BF16)
        v_ref[...] = kv[:, HW:].astype(BF16)

    rows = lambda n: pl.BlockSpec((tm, n), lambda i: (i, 0))
    return pl.pallas_call(
        body, grid=(T // tm,), name="mla_prep",
        in_specs=[rows(Q_LORA), rows(KV_LORA), rows(LANES), pl.BlockSpec((1, 1, tm), lambda i: (i, 0, 0)),
                  _resident((1, Q_LORA)), _resident((1, KV_LORA)), _resident((Q_LORA, HW)), _resident((KV_LORA, 2 * HW)),
                  _resident((QK_ROPE // 2, 1)), _resident((QK_ROPE // 2, LANES))],
        out_specs=[rows(HW), rows(HW), rows(HW), rows(LANES), rows(LANES)],
        out_shape=[jax.ShapeDtypeStruct((T, HW), BF16)] * 3 + [jax.ShapeDtypeStruct((T, LANES), F32)] * 2,
        compiler_params=_cparams(("arbitrary",)),
    )(cq, ckv, dtk, pos.reshape(T // tm, 1, tm), qw, kvw, wuq, wukv, invf, place)


def _mla_prep_bwd(dq, dk, dv, cq, ckv, cos_t, sin_t, qw, kvw, wuq, wukv):
    T = cq.shape[0]
    tm = min(WIDE_TOKEN_TILE, T)
    scale = 1.0 / math.sqrt(QK_DIM)
    HW = MLA_HEADS * HEAD_LANES

    def body(dq_ref, dk_ref, dv_ref, cq_ref, ckv_ref, cos_ref, sin_ref, qw_ref, kvw_ref, wuq_ref, wukv_ref,
             dcq_ref, dckv_ref, ddtk_ref, qn_ref, kvn_ref, dqo_ref, dkvo_ref, dqw_ref, dkvw_ref):
        @pl.when(pl.program_id(0) == 0)
        def _():
            dqw_ref[...] = jnp.zeros_like(dqw_ref)
            dkvw_ref[...] = jnp.zeros_like(dkvw_ref)

        cosf, sinf = cos_ref[...], sin_ref[...]
        unrope = lambda d: d * cosf - _rot(d * sinf)
        dkr = jnp.zeros((tm, LANES), F32)
        nope = lax.broadcasted_iota(jnp.int32, (tm, LANES), 1) < QK_NOPE
        for h in range(MLA_HEADS):
            cols = slice(h * HEAD_LANES, (h + 1) * HEAD_LANES)
            dqo_ref[:, cols] = unrope(dq_ref[:, cols].astype(F32) * scale).astype(BF16)
            dkh = dk_ref[:, cols].astype(F32)
            dkr = dkr + jnp.where(_rope_lanes(dkh.shape), dkh, 0.0)
            dkvo_ref[:, cols] = jnp.where(nope, dkh, 0.0).astype(BF16)
        dkvo_ref[:, HW:] = dv_ref[...].astype(BF16)
        ddtk_ref[...] = unrope(dkr)
        xh, r = _rms_stats(cq_ref[...])
        qn_ref[...] = (xh * qw_ref[...]).astype(BF16)
        dx, dw_rows = _rms_bwd(_dot_nt(dqo_ref[...], wuq_ref[...]), xh, r, qw_ref[...])
        dcq_ref[...] = dx
        dqw_ref[...] += _colsum(dw_rows)
        xh, r = _rms_stats(ckv_ref[...])
        kvn_ref[...] = (xh * kvw_ref[...]).astype(BF16)
        dx, dw_rows = _rms_bwd(_dot_nt(dkvo_ref[...], wukv_ref[...]), xh, r, kvw_ref[...])
        dckv_ref[...] = dx
        dkvw_ref[...] += _colsum(dw_rows)

    rows = lambda n: pl.BlockSpec((tm, n), lambda i: (i, 0))
    sd = jax.ShapeDtypeStruct
    return pl.pallas_call(
        body, grid=(T // tm,), name="mla_prep_bwd",
        in_specs=[rows(HW), rows(HW), rows(HW), rows(Q_LORA), rows(KV_LORA), rows(LANES), rows(LANES), _resident((1, Q_LORA)),
                  _resident((1, KV_LORA)), _resident((Q_LORA, HW)), _resident((KV_LORA, 2 * HW))],
        out_specs=[rows(Q_LORA), rows(KV_LORA), rows(LANES), rows(Q_LORA), rows(KV_LORA), rows(HW), rows(2 * HW),
                   pl.BlockSpec((1, Q_LORA), lambda i: (0, 0)), pl.BlockSpec((1, KV_LORA), lambda i: (0, 0))],
        out_shape=[sd((T, Q_LORA), F32), sd((T, KV_LORA), F32), sd((T, LANES), F32), sd((T, Q_LORA), BF16),
                   sd((T, KV_LORA), BF16), sd((T, HW), BF16), sd((T, 2 * HW), BF16), sd((1, Q_LORA), F32),
                   sd((1, KV_LORA), F32)],
        compiler_params=_cparams(("arbitrary",)),
    )(dq, dk, dv, cq, ckv, cos_t, sin_t, qw, kvw, wuq, wukv)


def _causal_mask(t):
    row = lax.broadcasted_iota(jnp.int32, (t, t), 0)
    col = lax.broadcasted_iota(jnp.int32, (t, t), 1)
    return col <= row


def _attn_fwd(q, k, v):
    B, S, HW = q.shape
    H = HW // HEAD_LANES
    t = min(ATTN_FWD_Q_TILE, S)
    tk = min(ATTN_FWD_KV_TILE, t)
    nq = S // t
    per = t // tk

    pair = 4
    pw = pair * HEAD_LANES

    def body(q_ref, k_ref, v_ref, o_ref, lse_ref):
        qi = pl.program_id(2)
        lanes = [slice(hh * HEAD_LANES, (hh + 1) * HEAD_LANES) for hh in range(pair)]
        qs = [q_ref[0, :, cols] for cols in lanes]

        def step(j, carry, diag):
            sl = pl.ds(pl.multiple_of(j * tk, tk), tk)
            out = []
            for qv, cols, (m, l, acc) in zip(qs, lanes, carry):
                s = _dot_nt(qv, k_ref[0, sl, cols])
                if diag is not None:
                    row = lax.broadcasted_iota(jnp.int32, (t, tk), 0)
                    col = lax.broadcasted_iota(jnp.int32, (t, tk), 1)
                    s = jnp.where(col + diag * tk <= row, s, -1e30)
                m_new = jnp.maximum(m, jnp.max(s, axis=-1, keepdims=True))
                alpha = jnp.exp(m - m_new)
                p = jnp.exp(s - m_new)
                l = alpha * l + jnp.sum(p, axis=-1, keepdims=True)
                acc = alpha * acc + _dot(p.astype(BF16), v_ref[0, sl, cols])
                out.append((m_new, l, acc))
            return tuple(out)

        init = tuple((jnp.full((t, 1), -1e30, F32), jnp.zeros((t, 1), F32), jnp.zeros((t, HEAD_LANES), F32))
                     for _ in range(pair))
        carry = lax.fori_loop(0, qi * per, lambda j, c: step(j, c, None), init)
        for d in range(per):
            carry = step(qi * per + d, carry, d)
        for hh, (m, l, acc) in enumerate(carry):
            o_ref[0, :, lanes[hh]] = (acc / l).astype(BF16)
            lse_ref[0, hh] = m + jnp.log(l)

    return pl.pallas_call(
        body, grid=(B, H // pair, nq), name="attn_fwd",
        in_specs=[pl.BlockSpec((1, t, pw), lambda b, h, i: (b, i, h)),
                  pl.BlockSpec((1, S, pw), lambda b, h, i: (b, 0, h)),
                  pl.BlockSpec((1, S, pw), lambda b, h, i: (b, 0, h))],
        out_specs=[pl.BlockSpec((1, t, pw), lambda b, h, i: (b, i, h)),
                   pl.BlockSpec((1, pair, t, 1), lambda b, h, i: (b, h, i, 0))],
        out_shape=[jax.ShapeDtypeStruct((B, S, HW), BF16), jax.ShapeDtypeStruct((B, H, S, 1), F32)],
        compiler_params=_cparams(("arbitrary", "arbitrary", "arbitrary")),
    )(q, k, v)


def _attn_bwd(q, k, v, o, do, lse):
    B, S, HW = q.shape
    H = HW // HEAD_LANES
    t = min(ATTN_BWD_TILE, S)
    nq = S // t

    pair = 2
    pw = pair * HEAD_LANES

    def body(q_ref, k_ref, v_ref, o_ref, do_ref, lse_ref, dq_out_ref, dk_ref, dv_ref, dq_ref):
        j = pl.program_id(2)

        @pl.when(j == 0)
        def _():
            dq_ref[...] = jnp.zeros_like(dq_ref)

        lanes = [slice(hh * HEAD_LANES, (hh + 1) * HEAD_LANES) for hh in range(pair)]

        def step(i, carry, masked):
            sl = pl.ds(pl.multiple_of(i * t, t), t)
            out = []
            for hh, (cols, (dk, dv)) in enumerate(zip(lanes, carry)):
                kj = k_ref[0, :, cols]
                qi = q_ref[0, sl, cols]
                doi = do_ref[0, sl, cols]
                s = _dot_nt(qi, kj)
                if masked:
                    s = jnp.where(_causal_mask(t), s, -1e30)
                p = jnp.exp(s - lse_ref[0, hh, sl, :])
                dv = dv + _dot_tn(p.astype(BF16), doi)
                dp = _dot_nt(doi, v_ref[0, :, cols])
                delta = jnp.sum(doi.astype(F32) * o_ref[0, sl, cols].astype(F32), axis=-1, keepdims=True)
                dsb = (p * (dp - delta)).astype(BF16)
                dk = dk + _dot_tn(dsb, qi)
                dq_ref[sl, cols] += _dot(dsb, kj)
                out.append((dk, dv))
            return tuple(out)

        zero = jnp.zeros((t, HEAD_LANES), F32)
        carry = step(j, ((zero, zero),) * pair, True)
        carry = lax.fori_loop(j + 1, nq, lambda i, c: step(i, c, False), carry)
        for cols, (dk, dv) in zip(lanes, carry):
            dk_ref[0, :, cols] = dk.astype(BF16)
            dv_ref[0, :, cols] = dv.astype(BF16)

        @pl.when(j == nq - 1)
        def _():
            dq_out_ref[0] = dq_ref[...].astype(BF16)

    full = pl.BlockSpec((1, S, pw), lambda b, h, j: (b, 0, h))
    tile = pl.BlockSpec((1, t, pw), lambda b, h, j: (b, j, h))
    sd = jax.ShapeDtypeStruct
    return pl.pallas_call(
        body, grid=(B, H // pair, nq), name="attn_bwd",
        in_specs=[full, tile, tile, full, full, pl.BlockSpec((1, pair, S, 1), lambda b, h, j: (b, h, 0, 0))],
        out_specs=[full, tile, tile],
        out_shape=[sd((B, S, HW), BF16), sd((B, S, HW), BF16), sd((B, S, HW), BF16)],
        scratch_shapes=[pltpu.VMEM((S, pw), F32)],
        compiler_params=_cparams(("arbitrary", "arbitrary", "arbitrary")),
    )(q, k, v, o, do, lse)


def _mix_out(x1, yssd, o, mw, wout, g, seq):
    T, D = x1.shape
    tm = min(WIDE_TOKEN_TILE, seq)
    tps = seq // tm

    def body(x_ref, ys_ref, o_ref, mw_ref, w_ref, g_ref, xo_ref, m_ref, yc_ref):
        xh, _ = _rms_stats(o_ref[...].astype(F32))
        ycat = jnp.concatenate([ys_ref[...], (xh * mw_ref[...]).astype(BF16)], axis=1)
        m = _dot(ycat, w_ref[...])
        xo_ref[...] = x_ref[...] + g_ref[0] * m
        m_ref[...] = m.astype(BF16)
        yc_ref[...] = ycat

    rows = lambda n: pl.BlockSpec((tm, n), lambda i: (i, 0))
    perb = pl.BlockSpec((1, 1, D), lambda i: (i // tps, 0, 0))
    sd = jax.ShapeDtypeStruct
    return pl.pallas_call(
        body, grid=(T // tm,), name="mix_out",
        in_specs=[rows(D), rows(D_SSD), rows(D_MLA), _resident((1, D_MLA)), _resident((D_SSD + D_MLA, D)), perb],
        out_specs=[rows(D), rows(D), rows(D_SSD + D_MLA)],
        out_shape=[sd((T, D), F32), sd((T, D), BF16), sd((T, D_SSD + D_MLA), BF16)],
        compiler_params=_cparams(("arbitrary",)),
    )(x1, yssd, o, mw, wout, g)


def _mix_out_bwd(dx2, m, o, mw, wout, g, seq):
    T, D = dx2.shape
    B = T // seq
    tm = min(WIDE_TOKEN_TILE, seq)
    tps = seq // tm

    def body(dx_ref, m_ref, o_ref, mw_ref, w_ref, g_ref, dys_ref, do_ref, dm_ref, dg_ref, dmw_ref):
        i = pl.program_id(0)

        @pl.when(i % tps == 0)
        def _():
            dg_ref[...] = jnp.zeros_like(dg_ref)

        @pl.when(i == 0)
        def _():
            dmw_ref[...] = jnp.zeros_like(dmw_ref)

        dxv = dx_ref[...]
        dg_ref[0] += _colsum(dxv * m_ref[...].astype(F32))
        dmb = (g_ref[0] * dxv).astype(BF16)
        dm_ref[...] = dmb
        dycat = _dot_nt(dmb, w_ref[...])
        dys_ref[...] = dycat[:, :D_SSD].astype(BF16)
        xh, r = _rms_stats(o_ref[...].astype(F32))
        dx, dw_rows = _rms_bwd(dycat[:, D_SSD:], xh, r, mw_ref[...])
        do_ref[...] = dx.astype(BF16)
        dmw_ref[...] += _colsum(dw_rows)

    rows = lambda n: pl.BlockSpec((tm, n), lambda i: (i, 0))
    perb = pl.BlockSpec((1, 1, D), lambda i: (i // tps, 0, 0))
    sd = jax.ShapeDtypeStruct
    return pl.pallas_call(
        body, grid=(T // tm,), name="mix_out_bwd",
        in_specs=[rows(D), rows(D), rows(D_MLA), _resident((1, D_MLA)), _resident((D_SSD + D_MLA, D)), perb],
        out_specs=[rows(D_SSD), rows(D_MLA), rows(D), perb, pl.BlockSpec((1, D_MLA), lambda i: (0, 0))],
        out_shape=[sd((T, D_SSD), BF16), sd((T, D_MLA), BF16), sd((T, D), BF16), sd((B, 1, D), F32), sd((1, D_MLA), F32)],
        compiler_params=_cparams(("arbitrary",)),
    )(dx2, m, o, mw, wout, g)


def _win_to_kernel(w):
    z0 = jnp.zeros((48, w.shape[1]), w.dtype)
    z1 = jnp.zeros((32, w.shape[1]), w.dtype)
    return jnp.concatenate([w[:2560], w[2576:3216], w[2560:2576], z0, w[3216:3248], z1], axis=0)


def _win_from_kernel(g):
    return jnp.concatenate([g[:2560], g[3200:3216], g[2560:3200], g[3264:3296]], axis=0)


def _wuq_to_kernel(w):
    w = w.reshape(Q_LORA, MLA_HEADS, QK_DIM)
    return jnp.pad(w, ((0, 0), (0, 0), (0, HEAD_LANES - QK_DIM))).reshape(Q_LORA, MLA_HEADS * HEAD_LANES)


def _wuq_from_kernel(g):
    return g.reshape(Q_LORA, MLA_HEADS, HEAD_LANES)[:, :, :QK_DIM].reshape(Q_LORA, MLA_HEADS * QK_DIM)


def _wukv_to_kernel(w):
    w = w.reshape(KV_LORA, MLA_HEADS, QK_NOPE + V_HEAD)
    kp = jnp.pad(w[:, :, :QK_NOPE], ((0, 0), (0, 0), (0, HEAD_LANES - QK_NOPE)))
    return jnp.concatenate([kp.reshape(KV_LORA, -1), w[:, :, QK_NOPE:].reshape(KV_LORA, -1)], axis=1)


def _wukv_from_kernel(g):
    hw = MLA_HEADS * HEAD_LANES
    kp = g[:, :hw].reshape(KV_LORA, MLA_HEADS, HEAD_LANES)[:, :, :QK_NOPE]
    vp = g[:, hw:].reshape(KV_LORA, MLA_HEADS, V_HEAD)
    return jnp.concatenate([kp, vp], axis=2).reshape(KV_LORA, MLA_HEADS * (QK_NOPE + V_HEAD))


def _lanes16(v):
    return jnp.pad(v.reshape(1, SSD_HEADS), ((0, 0), (0, LANES - SSD_HEADS)))


def _constants():
    e = np.zeros((LANES, D_SSD), np.float32)
    for h in range(SSD_HEADS):
        e[h, h * SSD_HEAD_DIM:(h + 1) * SSD_HEAD_DIM] = 1.0
    inv_freq = ROPE_THETA ** (-jnp.arange(0, QK_ROPE, 2, dtype=F32) / QK_ROPE)
    half = QK_ROPE // 2
    place = np.zeros((half, LANES), np.float32)
    for j in range(half):
        place[j, QK_NOPE + j] = place[j, QK_NOPE + half + j] = 1.0
    return jnp.asarray(e, BF16), inv_freq.reshape(half, 1), jnp.asarray(place, BF16)


def _local_step(x, positions, mod, w, later_weights, small, tgt, on_grads, sync):
    B, S, D = x.shape
    T = B * S
    expand, invf, place = _constants()
    x0 = x.reshape(T, D)
    pos = positions.reshape(T)
    mods = [mod[:, i * D:(i + 1) * D].reshape(B, 1, D) for i in range(N_MOD)]
    sh1, sc1, g1, sh2, sc2, g2, sh3, sc3, g3 = mods
    dtb, alog = _lanes16(small["dt_bias"]), _lanes16(small["a_log"])
    dsk = jnp.repeat(small["d_skip"].reshape(1, SSD_HEADS), SSD_HEAD_DIM, axis=1)

    x1, a1, u1, f1 = _ffn_fwd(x0, small["norm_ffn1"], sh1, sc1, g1, w["ffn1_w_gate"], w["ffn1_w_up"], w["ffn1_w_down"], S, "ffn1_fwd")
    w = {**w, **later_weights(f1)}
    z, xraw, cq, ckv, dtk = _inproj_fwd(x1, small["norm_mix"], sh2, sc2, w["w_in"], S)
    xraw3 = xraw.reshape(B, S, D_CONV)
    xbc = _conv_fwd(xraw3, small["conv_w"], small["conv_b"])
    dtk3, z3 = dtk.reshape(B, S, LANES), z.reshape(B, S, D_SSD)
    y, yssd, prev = _ssd_fwd(xbc, dtk3, z3, dtb, alog, dsk, small["ssd_norm_w"], expand)
    q, k, v, cos_t, sin_t = _mla_prep(cq, ckv, dtk, pos, small["q_norm_w"], small["kv_norm_w"], w["w_uq"], w["w_ukv"], invf,
                                      place)
    hw = MLA_HEADS * HEAD_LANES
    q3, k3, v3 = q.reshape(B, S, hw), k.reshape(B, S, hw), v.reshape(B, S, hw)
    o3, lse = _attn_fwd(q3, k3, v3)
    o = o3.reshape(T, hw)
    x2, m, ycat = _mix_out(x1, yssd.reshape(T, D_SSD), o, small["mla_norm_w"], w["w_out"], g2, S)
    dx3, a2, u2, f2, loss, d_norm_final = _ffn_fwd(
        x2, small["norm_ffn2"], sh3, sc3, g3, w["ffn2_w_gate"], w["ffn2_w_up"], w["ffn2_w_down"], S, "ffn2_fwd",
        head=(small["norm_final"].reshape(1, D), tgt.reshape(T, D)))

    gw, gs = {}, {}
    dx2, h3, s3, df3, da3, du3, dsh3, dsc3, dg3, gs["norm_ffn2"] = _ffn_bwd(
        dx3, x2, small["norm_ffn2"], sh3, sc3, g3, a2, u2, f2, w["ffn2_w_gate"], w["ffn2_w_up"], w["ffn2_w_down"], S, "ffn2_bwd")
    gw["ffn2_w_gate"], gw["ffn2_w_up"], gw["ffn2_w_down"] = _ffn_wgrad(h3, s3, df3, da3, du3, dsh3, "ffn2_wgrad")
    g2 = g2 + on_grads(("ffn2_w_gate", "ffn2_w_up", "ffn2_w_down"), gw)

    dys, do, dm, dg2, gs["mla_norm_w"] = _mix_out_bwd(dx2, m, o, small["mla_norm_w"], w["w_out"], g2, S)
    gw["w_out"] = _mm_tn(ycat, dm, 512, "dwout")

    dq3, dk3, dv3 = _attn_bwd(q3, k3, v3, o3, do.reshape(B, S, hw), lse)
    dcq, dckv, ddtk_b, qn, kvn, dqb, dkvb, gs["q_norm_w"], gs["kv_norm_w"] = _mla_prep_bwd(
        dq3.reshape(T, hw), dk3.reshape(T, hw), dv3.reshape(T, hw), cq, ckv, cos_t, sin_t, small["q_norm_w"] + sync(dq3),
        small["kv_norm_w"], w["w_uq"], w["w_ukv"])
    gw["w_uq"] = _mm_tn(qn, dqb, 512, "dwuq")
    gw["w_ukv"] = _mm_tn(kvn, dkvb, 1024, "dwukv")

    dxbc, ddtk_a, dz, gs["ssd_norm_w"], dvec = _ssd_bwd(
        xbc, dtk3, z3, y, prev, dys.reshape(B, S, D_SSD), dtb, alog, dsk, small["ssd_norm_w"], expand)
    gs["dt_bias"], gs["a_log"], gs["d_skip"] = dvec[0:1, :SSD_HEADS], dvec[1:2, :SSD_HEADS], dvec[2:3, :SSD_HEADS]
    dxraw, gs["conv_w"], gs["conv_b"] = _conv_bwd(dxbc, xraw3, small["conv_w"], small["conv_b"])
    dx1, h2, dproj, dsh2, dsc2, gs["norm_mix"] = _inproj_bwd(
        dx2, x1, small["norm_mix"], sh2, sc2, w["w_in"], dz.reshape(T, D_SSD), dxraw.reshape(T, D_CONV), dcq, dckv,
        ddtk_a.reshape(T, LANES), ddtk_b, S)
    gw["w_in"] = _mm_tn(dproj, h2, 512, "dwin")
    g1 = g1 + on_grads(("w_in", "w_uq", "w_ukv", "w_out"), gw)

    dx0, h1, s1, df1, da1, du1, dsh1, dsc1, dg1, gs["norm_ffn1"] = _ffn_bwd(
        dx1, x0, small["norm_ffn1"], sh1, sc1, g1, a1, u1, f1, w["ffn1_w_gate"], w["ffn1_w_up"], w["ffn1_w_down"], S, "ffn1_bwd")
    gw["ffn1_w_gate"], gw["ffn1_w_up"], gw["ffn1_w_down"] = _ffn_wgrad(h1, s1, df1, da1, du1, dsh1 + sync(dx0), "ffn1_wgrad")
    gs["norm_final"] = d_norm_final
    dmod = jnp.concatenate([t.reshape(B, D) for t in (dsh1, dsc1, dg1, dsh2, dsc2, dg2, dsh3, dsc3, dg3)], axis=1)
    return loss, dx0.reshape(B, S, D), gw, dmod, gs


HBM_SPEC = pl.BlockSpec(memory_space=pltpu.HBM)
VMEM_SPEC = pl.BlockSpec(memory_space=pltpu.VMEM)


def _place():
    return lax.axis_index("x"), lax.axis_index("y"), lax.axis_index("c")


def _other_chips(mx, my):
    return [(1 - mx, my), (mx, 1 - my), (1 - mx, 1 - my)]


def _remote(src, dst, send_sem, recv_sem, to):
    return pltpu.make_async_remote_copy(src_ref=src, dst_ref=dst, send_sem=send_sem, recv_sem=recv_sem,
                                        device_id=to, device_id_type=MESH)


def _all_gather_small(xa, name):
    r, n = xa.shape

    def body(x_ref, o_ref, token, send_sems, recv_sems):
        mx, my, mc = _place()
        me = 4 * mx + 2 * my + mc
        token[...] = jnp.zeros_like(token)
        o_ref[pl.ds(me, 1)] = x_ref[...][None]
        sends = []
        for k in range(1, N_DEV):
            peer = (mx ^ (k >> 2), my ^ ((k >> 1) & 1), mc ^ (k & 1))
            cp = _remote(x_ref, o_ref.at[me], send_sems.at[k - 1], recv_sems.at[k - 1], peer)
            cp.start()
            sends.append(cp)
        for k in range(1, N_DEV):
            peer = (mx ^ (k >> 2), my ^ ((k >> 1) & 1), mc ^ (k & 1))
            slot = 4 * peer[0] + 2 * peer[1] + peer[2]
            _remote(x_ref, o_ref.at[slot], send_sems.at[k - 1], recv_sems.at[k - 1], peer).wait_recv()
        for cp in sends:
            cp.wait_send()

    return pl.pallas_call(
        body, name=name, in_specs=[VMEM_SPEC], out_specs=[VMEM_SPEC, VMEM_SPEC],
        out_shape=[jax.ShapeDtypeStruct((N_DEV, r, n), xa.dtype), jax.ShapeDtypeStruct((8, LANES), F32)],
        scratch_shapes=[pltpu.SemaphoreType.DMA((N_DEV - 1,)), pltpu.SemaphoreType.DMA((N_DEV - 1,))],
        compiler_params=pltpu.CompilerParams(vmem_limit_bytes=VMEM_LIMIT),
    )(xa)


def _halves_by_rows(shape):
    return (shape[-2] // 2) % 16 == 0


def _half_shape(shape):
    r, c = shape[-2:]
    return tuple(shape[:-2]) + ((r // 2, c) if _halves_by_rows(shape) else (r, c // 2))


def _half_index(shape, hc):
    r, c = shape[-2:]
    if _halves_by_rows(shape):
        return (pl.ds(pl.multiple_of(hc * (r // 2), 16), r // 2), slice(None))
    return (slice(None), pl.ds(pl.multiple_of(hc * (c // 2), LANES), c // 2))


def _half(ref, hc, lead=None):
    idx = _half_index(ref.shape, hc)
    return ref.at[idx] if lead is None else ref.at[(lead,) + idx]


def _gather_weights(shards):
    n = len(shards)

    def body(*refs):
        w_refs, o_refs, token = refs[:n], refs[n:2 * n], refs[2 * n]
        send_sems, recv_sems, stage_sems = refs[2 * n + 1:2 * n + 4]
        stages = refs[2 * n + 4:]
        mx, my, mc = _place()
        chip = 2 * mx + my
        others = _other_chips(mx, my)
        sibling = (mx, my, 1 - mc)
        token[...] = jnp.zeros_like(token)
        stage_in = [pltpu.make_async_copy(w, st, stage_sems.at[0, i]) for i, (w, st) in enumerate(zip(w_refs, stages))]
        for cp in stage_in:
            cp.start()
        first = []
        for i, (w, o) in enumerate(zip(w_refs, o_refs)):
            for k, (cx, cy) in enumerate(others):
                first.append(_remote(_half(w, mc), _half(o, mc, chip), send_sems.at[i, k],
                                     recv_sems.at[i, k], (cx, cy, mc)))
                first[-1].start()
        stage_out = []
        for i, (st, o) in enumerate(zip(stages, o_refs)):
            stage_in[i].wait()
            stage_out.append(pltpu.make_async_copy(st, o.at[chip], stage_sems.at[1, i]))
            stage_out[-1].start()
        passed = []
        for i, (w, o) in enumerate(zip(w_refs, o_refs)):
            for k, (cx, cy) in enumerate(others):
                landed = _half(o, mc, 2 * cx + cy)
                _remote(landed, landed, send_sems.at[i, k], recv_sems.at[i, k], (cx, cy, mc)).wait_recv()
                passed.append(_remote(landed, landed, send_sems.at[i, 3 + k], recv_sems.at[i, 3 + k], sibling))
                passed[-1].start()
        for i, (w, o) in enumerate(zip(w_refs, o_refs)):
            for k, (cx, cy) in enumerate(others):
                there = _half(o, 1 - mc, 2 * cx + cy)
                _remote(there, there, send_sems.at[i, 3 + k], recv_sems.at[i, 3 + k], sibling).wait_recv()
        for cp in first + passed:
            cp.wait_send()
        for cp in stage_out:
            cp.wait()

    out = pl.pallas_call(
        body, name="gather_weights", in_specs=[HBM_SPEC] * n, out_specs=[HBM_SPEC] * n + [VMEM_SPEC],
        out_shape=[jax.ShapeDtypeStruct((N_CHIPS,) + s.shape, s.dtype) for s in shards] + [jax.ShapeDtypeStruct((8, LANES), F32)],
        scratch_shapes=[pltpu.SemaphoreType.DMA((n, 6)), pltpu.SemaphoreType.DMA((n, 6)), pltpu.SemaphoreType.DMA((2, n))]
        + [pltpu.VMEM(s.shape, s.dtype) for s in shards],
        compiler_params=pltpu.CompilerParams(vmem_limit_bytes=VMEM_LIMIT),
    )(*shards)
    return out[:n], out[n]


SEM_SPEC = pl.BlockSpec(memory_space=pltpu.SEMAPHORE)
ANY_SPEC = pl.BlockSpec(memory_space=pl.ANY)
DATAFLOW = pltpu.SideEffectType.DATAFLOW_SIDE_EFFECTING


def _hbm(arr):
    return pltpu.with_memory_space_constraint(arr, pltpu.HBM)


def _gather_start(shards):
    n = len(shards)

    def body(*refs):
        w_refs, land_refs, send_sems, recv_sems, token = refs[:n], refs[n:2 * n], refs[2 * n], refs[2 * n + 1], refs[-1]
        mx, my, mc = _place()
        chip = 2 * mx + my
        for i, (w, land) in enumerate(zip(w_refs, land_refs)):
            for k, (cx, cy) in enumerate(_other_chips(mx, my)):
                _remote(_half(w, mc), _half(land, mc, chip), send_sems.at[3 * i + k],
                        recv_sems.at[3 * i + k], (cx, cy, mc)).start()
        token[...] = jnp.zeros_like(token)

    lands = [lax.empty((N_CHIPS,) + s.shape, s.dtype) for s in shards]
    out = pl.pallas_call(
        body, name="gather_start",
        out_shape=(pltpu.SemaphoreType.DMA((3 * n,)), pltpu.SemaphoreType.DMA((3 * n,)),
                   *[pltpu.HBM(s.shape, s.dtype) for s in shards], *[pltpu.HBM(l.shape, l.dtype) for l in lands],
                   jax.ShapeDtypeStruct((8, LANES), F32)),
        in_specs=[HBM_SPEC] * (2 * n), out_specs=(SEM_SPEC, SEM_SPEC, *[HBM_SPEC] * (2 * n), VMEM_SPEC),
        input_output_aliases={i: 2 + i for i in range(2 * n)},
        compiler_params=pltpu.CompilerParams(has_side_effects=DATAFLOW),
    )(*[_hbm(s) for s in shards], *[_hbm(l) for l in lands])
    return out[0], out[1], out[2:2 + n], out[2 + n:2 + 2 * n], out[-1]


def _gather_wait(send_sems, recv_sems, shards, lands, after):
    n = len(shards)

    def body(*refs):
        w_refs, land_refs, send_sems, recv_sems = refs[:n], refs[n:2 * n], refs[2 * n], refs[2 * n + 1]
        mx, my, mc = _place()
        for i, (w, land) in enumerate(zip(w_refs, land_refs)):
            for k, (cx, cy) in enumerate(_other_chips(mx, my)):
                cp = _remote(_half(w, mc), _half(land, mc, 2 * cx + cy), send_sems.at[3 * i + k],
                             recv_sems.at[3 * i + k], (cx, cy, mc))
                cp.wait_send()
                cp.wait_recv()

    out = pl.pallas_call(
        body, name="gather_wait",
        out_shape=(*[pltpu.HBM(s.shape, s.dtype) for s in shards], *[pltpu.HBM(l.shape, l.dtype) for l in lands]),
        in_specs=[HBM_SPEC] * (2 * n) + [SEM_SPEC, SEM_SPEC, ANY_SPEC], out_specs=tuple([HBM_SPEC] * (2 * n)),
        input_output_aliases={i: i for i in range(2 * n)},
        compiler_params=pltpu.CompilerParams(has_side_effects=DATAFLOW),
    )(*shards, *lands, send_sems, recv_sems, after)
    return out[n:]


def _gather_finish(shards, lands):
    n = len(shards)

    def body(*refs):
        w_refs, land_refs, o_refs = refs[:n], refs[n:2 * n], refs[2 * n:3 * n]
        send_sems, recv_sems, stage_sems = refs[3 * n:3 * n + 3]
        stages = refs[3 * n + 3:]
        mx, my, mc = _place()
        chip = 2 * mx + my
        others = _other_chips(mx, my)
        sibling = (mx, my, 1 - mc)
        stage_in = [pltpu.make_async_copy(w, st, stage_sems.at[0, i]) for i, (w, st) in enumerate(zip(w_refs, stages))]
        for cp in stage_in:
            cp.start()
        passed = []
        for i, (w, o) in enumerate(zip(w_refs, o_refs)):
            for k, (cx, cy) in enumerate(others):
                landed = _half(o, mc, 2 * cx + cy)
                passed.append(_remote(landed, landed, send_sems.at[i, k], recv_sems.at[i, k], sibling))
                passed[-1].start()
        stage_out = []
        for i, (st, o) in enumerate(zip(stages, o_refs)):
            stage_in[i].wait()
            stage_out.append(pltpu.make_async_copy(st, o.at[chip], stage_sems.at[1, i]))
            stage_out[-1].start()
        for i, (w, o) in enumerate(zip(w_refs, o_refs)):
            for k, (cx, cy) in enumerate(others):
                there = _half(o, 1 - mc, 2 * cx + cy)
                _remote(there, there, send_sems.at[i, k], recv_sems.at[i, k], sibling).wait_recv()
        for cp in passed:
            cp.wait_send()
        for cp in stage_out:
            cp.wait()

    return pl.pallas_call(
        body, name="gather_finish", in_specs=[HBM_SPEC] * (2 * n), out_specs=[HBM_SPEC] * n,
        out_shape=[jax.ShapeDtypeStruct(l.shape, l.dtype) for l in lands],
        input_output_aliases={n + i: i for i in range(n)},
        scratch_shapes=[pltpu.SemaphoreType.DMA((n, 3)), pltpu.SemaphoreType.DMA((n, 3)), pltpu.SemaphoreType.DMA((2, n))]
        + [pltpu.VMEM(s.shape, s.dtype) for s in shards],
        compiler_params=pltpu.CompilerParams(vmem_limit_bytes=VMEM_LIMIT),
    )(*shards, *lands)


def _scatter_start(ss, tag):
    n = len(ss)

    def body(*refs):
        s_refs, land_refs, send_sems, recv_sems, token = refs[:n], refs[n:2 * n], refs[2 * n], refs[2 * n + 1], refs[-1]
        mx, my, mc = _place()
        chip = 2 * mx + my
        for i, (s, land) in enumerate(zip(s_refs, land_refs)):
            for k, (cx, cy) in enumerate(_other_chips(mx, my)):
                _remote(s.at[2 * cx + cy], land.at[chip], send_sems.at[3 * i + k], recv_sems.at[3 * i + k],
                        (cx, cy, mc)).start()
        token[...] = jnp.zeros_like(token)

    lands = [lax.empty(s.shape, s.dtype) for s in ss]
    out = pl.pallas_call(
        body, name="scatter_start_" + tag,
        out_shape=(pltpu.SemaphoreType.DMA((3 * n,)), pltpu.SemaphoreType.DMA((3 * n,)),
                   *[pltpu.HBM(s.shape, s.dtype) for s in ss], *[pltpu.HBM(l.shape, l.dtype) for l in lands],
                   jax.ShapeDtypeStruct((8, LANES), F32)),
        in_specs=[HBM_SPEC] * (2 * n), out_specs=(SEM_SPEC, SEM_SPEC, *[HBM_SPEC] * (2 * n), VMEM_SPEC),
        input_output_aliases={i: 2 + i for i in range(2 * n)},
        compiler_params=pltpu.CompilerParams(has_side_effects=DATAFLOW),
    )(*[_hbm(s) for s in ss], *[_hbm(l) for l in lands])
    return out[0], out[1], out[2:2 + n], out[2 + n:2 + 2 * n], out[-1]


def _scatter_wait(send_sems, recv_sems, ss, lands, after, tag):
    n = len(ss)

    def body(*refs):
        s_refs, land_refs, send_sems, recv_sems = refs[:n], refs[n:2 * n], refs[2 * n], refs[2 * n + 1]
        mx, my, mc = _place()
        for i, (s, land) in enumerate(zip(s_refs, land_refs)):
            for k, (cx, cy) in enumerate(_other_chips(mx, my)):
                slot = land.at[2 * cx + cy]
                cp = _remote(s.at[2 * cx + cy], slot, send_sems.at[3 * i + k], recv_sems.at[3 * i + k], (cx, cy, mc))
                cp.wait_send()
                cp.wait_recv()

    out = pl.pallas_call(
        body, name="scatter_wait_" + tag,
        out_shape=(*[pltpu.HBM(s.shape, s.dtype) for s in ss], *[pltpu.HBM(l.shape, l.dtype) for l in lands]),
        in_specs=[HBM_SPEC] * (2 * n) + [SEM_SPEC, SEM_SPEC, ANY_SPEC], out_specs=tuple([HBM_SPEC] * (2 * n)),
        input_output_aliases={i: i for i in range(2 * n)},
        compiler_params=pltpu.CompilerParams(has_side_effects=DATAFLOW),
    )(*ss, *lands, send_sems, recv_sems, after)
    return out[:n], out[n:]


def _swap_start(gs, tag):
    n = len(gs)

    def body(*refs):
        g_refs, land_refs, send_sems, recv_sems, token = refs[:n], refs[n:2 * n], refs[2 * n], refs[2 * n + 1], refs[-1]
        mx, my, mc = _place()
        for i, (g, land) in enumerate(zip(g_refs, land_refs)):
            src = g.at[(slice(None),) + _half_index(g.shape, 1 - mc)]
            _remote(src, land, send_sems.at[i], recv_sems.at[i], (mx, my, 1 - mc)).start()
        token[...] = jnp.zeros_like(token)

    lands = [lax.empty(_half_shape(g.shape), g.dtype) for g in gs]
    out = pl.pallas_call(
        body, name="swap_start_" + tag,
        out_shape=(pltpu.SemaphoreType.DMA((n,)), pltpu.SemaphoreType.DMA((n,)), *[pltpu.HBM(g.shape, g.dtype) for g in gs],
                   *[pltpu.HBM(l.shape, l.dtype) for l in lands], jax.ShapeDtypeStruct((8, LANES), F32)),
        in_specs=[HBM_SPEC] * (2 * n), out_specs=(SEM_SPEC, SEM_SPEC, *[HBM_SPEC] * (2 * n), VMEM_SPEC),
        input_output_aliases={i: 2 + i for i in range(2 * n)},
        compiler_params=pltpu.CompilerParams(has_side_effects=DATAFLOW),
    )(*[_hbm(g) for g in gs], *[_hbm(l) for l in lands])
    return out[0], out[1], out[2:2 + n], out[2 + n:2 + 2 * n], out[-1]


def _swap_wait(send_sems, recv_sems, gs, lands, after, tag):
    n = len(gs)

    def body(*refs):
        g_refs, land_refs, send_sems, recv_sems = refs[:n], refs[n:2 * n], refs[2 * n], refs[2 * n + 1]
        mx, my, mc = _place()
        for i, (g, land) in enumerate(zip(g_refs, land_refs)):
            src = g.at[(slice(None),) + _half_index(g.shape, 1 - mc)]
            cp = _remote(src, land, send_sems.at[i], recv_sems.at[i], (mx, my, 1 - mc))
            cp.wait_send()
            cp.wait_recv()

    out = pl.pallas_call(
        body, name="swap_wait_" + tag,
        out_shape=(*[pltpu.HBM(g.shape, g.dtype) for g in gs], *[pltpu.HBM(l.shape, l.dtype) for l in lands]),
        in_specs=[HBM_SPEC] * (2 * n) + [SEM_SPEC, SEM_SPEC, ANY_SPEC], out_specs=tuple([HBM_SPEC] * (2 * n)),
        input_output_aliases={i: i for i in range(2 * n)},
        compiler_params=pltpu.CompilerParams(has_side_effects=DATAFLOW),
    )(*gs, *lands, send_sems, recv_sems, after)
    return out[:n], out[n:]


def _pair_sums(gs, gots, name):
    n = len(gs)

    def body(*refs):
        g_refs, got_refs, o_refs, load_sems, store_sems = refs[:n], refs[n:2 * n], refs[2 * n:3 * n], refs[3 * n], refs[3 * n + 1]
        mine, theirs, sums = refs[3 * n + 2:4 * n + 2], refs[4 * n + 2:5 * n + 2], refs[5 * n + 2:]
        mc = lax.axis_index("c")
        loads = []
        for i, (g, got) in enumerate(zip(g_refs, got_refs)):
            loads.append((pltpu.make_async_copy(g.at[(slice(None),) + _half_index(g.shape, mc)], mine[i], load_sems.at[i, 0]),
                          pltpu.make_async_copy(got, theirs[i], load_sems.at[i, 1])))
            for cp in loads[-1]:
                cp.start()
        stores = []
        for i in range(n):
            for cp in loads[i]:
                cp.wait()
            sums[i][...] = (mine[i][...].astype(F32) + theirs[i][...].astype(F32)).astype(BF16)
            stores.append(pltpu.make_async_copy(sums[i], o_refs[i], store_sems.at[i]))
            stores[-1].start()
        for cp in stores:
            cp.wait()

    halves = [jax.ShapeDtypeStruct(got.shape, BF16) for got in gots]
    return pl.pallas_call(
        body, name=name, in_specs=[HBM_SPEC] * (2 * n), out_specs=[HBM_SPEC] * n, out_shape=halves,
        scratch_shapes=[pltpu.SemaphoreType.DMA((n, 2)), pltpu.SemaphoreType.DMA((n,))]
        + [pltpu.VMEM(got.shape, g.dtype) for g, got in zip(gs, gots)] + [pltpu.VMEM(got.shape, got.dtype) for got in gots]
        + [pltpu.VMEM(got.shape, BF16) for got in gots],
        compiler_params=pltpu.CompilerParams(vmem_limit_bytes=VMEM_LIMIT),
    )(*gs, *gots)


def _reduce_join(owns, gots, name):
    n = len(owns)

    def body(*refs):
        own_refs, got_refs, mine_refs, theirs_refs = refs[:n], refs[n:2 * n], refs[2 * n:3 * n], refs[3 * n:4 * n]
        send_sems, recv_sems, load_sems, store_sems = refs[4 * n:4 * n + 4]
        parts, sums = refs[4 * n + 4:5 * n + 4], refs[5 * n + 4:]
        mx, my, mc = _place()
        chip = 2 * mx + my
        loads = []
        for i, (own, got, part) in enumerate(zip(own_refs, got_refs, parts)):
            loads.append([pltpu.make_async_copy((own if k == 0 else got).at[chip ^ k], part.at[k], load_sems.at[i, k])
                          for k in range(N_CHIPS)])
            for cp in loads[-1]:
                cp.start()
        out = []
        for i, (part, total, mine, theirs) in enumerate(zip(parts, sums, mine_refs, theirs_refs)):
            for cp in loads[i]:
                cp.wait()
            total[...] = ((part[0].astype(F32) + part[1].astype(F32)) + part[2].astype(F32)) + part[3].astype(F32)
            out.append(pltpu.make_async_copy(total, mine, store_sems.at[i]))
            out.append(_remote(total, theirs, send_sems.at[i], recv_sems.at[i], (mx, my, 1 - mc)))
            out[-2].start()
            out[-1].start()
        for cp in out:
            cp.wait()

    halves = [jax.ShapeDtypeStruct(o.shape[1:], F32) for o in owns]
    out = pl.pallas_call(
        body, name=name, in_specs=[HBM_SPEC] * (2 * n), out_specs=[HBM_SPEC] * (2 * n), out_shape=halves + halves,
        scratch_shapes=[pltpu.SemaphoreType.DMA((n,)), pltpu.SemaphoreType.DMA((n,)), pltpu.SemaphoreType.DMA((n, N_CHIPS)),
                        pltpu.SemaphoreType.DMA((n,))]
        + [pltpu.VMEM(o.shape, o.dtype) for o in owns] + [pltpu.VMEM(o.shape[1:], F32) for o in owns],
        compiler_params=pltpu.CompilerParams(vmem_limit_bytes=VMEM_LIMIT),
    )(*owns, *gots)
    return out[:n], out[n:]


def _adam_math(w, g, m, v):
    m2 = ADAM_B1 * m + (1.0 - ADAM_B1) * g
    v2 = ADAM_B2 * v + (1.0 - ADAM_B2) * (g * g)
    m_hat = m2 * (1.0 / (1.0 - ADAM_B1 ** ADAM_STEP))
    v_hat = v2 * (1.0 / (1.0 - ADAM_B2 ** ADAM_STEP))
    delta = -ADAM_LR * (m_hat / (jnp.sqrt(v_hat) + ADAM_EPS) + ADAM_WD * w)
    return delta, m2, v2


def _adam(w, g, m, v, name):
    def body(w_ref, g_ref, m_ref, v_ref, d_ref, m2_ref, v2_ref):
        d_ref[...], m2_ref[...], v2_ref[...] = _adam_math(w_ref[...], g_ref[...], m_ref[...], v_ref[...])

    return pl.pallas_call(body, name=name, out_shape=[jax.ShapeDtypeStruct(w.shape, F32)] * 3)(w, g, m, v)


def _adam_halves(w, m, v, mine, theirs, core, name):
    hr, hcols = _half_shape(w.shape)[1:]
    by_rows = _halves_by_rows(w.shape)

    def body(core_ref, w_ref, m_ref, v_ref, mine_ref, theirs_ref, g_ref, d_ref, m2_ref, v2_ref):
        g = jnp.where(pl.program_id(0) == core_ref[0], mine_ref[...], theirs_ref[...])
        g_ref[0] = g
        d_ref[0], m2_ref[0], v2_ref[0] = _adam_math(w_ref[0], g, m_ref[0], v_ref[0])

    half = pl.BlockSpec((1, hr, hcols), lambda hc, core_ref: (0, hc, 0) if by_rows else (0, 0, hc))
    whole = pl.BlockSpec((hr, hcols), lambda hc, core_ref: (0, 0))
    return pl.pallas_call(
        body, name=name,
        grid_spec=pltpu.PrefetchScalarGridSpec(
            num_scalar_prefetch=1, grid=(2,), in_specs=[half, half, half, whole, whole], out_specs=[half] * 4),
        out_shape=[jax.ShapeDtypeStruct(w.shape, F32)] * 4,
        compiler_params=_cparams(("arbitrary",)),
    )(core, w, m, v, mine, theirs)


ADA_COLS = N_MOD * D_MODEL // N_CHIPS


def _ada_fwd(c_all, w_ada, b_cols):
    def body(c_ref, w_ref, b_ref, o_ref):
        cv = c_ref[...]
        act = (cv * _sigmoid(cv)).astype(BF16)
        o_ref[...] = _dot(act, w_ref[...].astype(BF16)) + b_ref[...]

    return pl.pallas_call(
        body, name="ada_fwd", out_shape=jax.ShapeDtypeStruct((c_all.shape[0], ADA_COLS), F32),
        compiler_params=pltpu.CompilerParams(vmem_limit_bytes=VMEM_LIMIT),
    )(c_all, w_ada, b_cols)


def _ada_bwd(c_all, dmod_cols, w, m, v):
    nb = c_all.shape[0]
    tn = 384

    def body(c_ref, d_ref, w_ref, m_ref, v_ref, g_ref, dl_ref, m2_ref, v2_ref):
        cv = c_ref[...]
        act = (cv * _sigmoid(cv)).astype(BF16)
        g = _dot_tn(act, d_ref[...].astype(BF16))
        g_ref[...] = g
        dl_ref[...], m2_ref[...], v2_ref[...] = _adam_math(w_ref[...], g, m_ref[...], v_ref[...])

    blk = pl.BlockSpec((D_MODEL, tn), lambda j: (0, j))
    return pl.pallas_call(
        body, name="ada_bwd", grid=(ADA_COLS // tn,),
        in_specs=[pl.BlockSpec((nb, D_MODEL), lambda j: (0, 0)), pl.BlockSpec((nb, tn), lambda j: (0, j)), blk, blk, blk],
        out_specs=[blk] * 4, out_shape=[jax.ShapeDtypeStruct((D_MODEL, ADA_COLS), F32)] * 4,
        compiler_params=_cparams(("arbitrary",)),
    )(c_all, dmod_cols, w, m, v)


SMALL_NAMES = ("norm_ffn1", "norm_mix", "conv_w", "conv_b", "ssd_norm_w", "q_norm_w", "kv_norm_w", "mla_norm_w",
               "norm_ffn2", "norm_final", "dt_bias", "a_log", "d_skip")
SMALL_SIZES = (1024, 1024, CONV_WIDTH * D_CONV, D_CONV, 1024, Q_LORA, KV_LORA, 1024, 1024, 1024, 16, 16, 16)
SMALL_ROWS = 16
MOD_ROWS = 2 * N_MOD
SEND_ROWS = 40


def _pack_small(parts):
    flat = jnp.concatenate([parts[n].reshape(-1) for n in SMALL_NAMES])
    return jnp.pad(flat, (0, SMALL_ROWS * D_MODEL - flat.shape[0]))


def _unpack_small(flat):
    out, off = {}, 0
    for n, size in zip(SMALL_NAMES, SMALL_SIZES):
        out[n] = flat[off:off + size]
        off += size
    return out


def _small_sum(got):
    def body(g_ref, o_ref):
        bsum = jnp.zeros((N_MOD, D_MODEL), F32)
        ssum = jnp.zeros((SMALL_ROWS, D_MODEL), F32)
        for d in range(N_DEV):
            bsum = bsum + g_ref[d, 0:N_MOD, :] + g_ref[d, N_MOD:MOD_ROWS, :]
            ssum = ssum + g_ref[d, MOD_ROWS:MOD_ROWS + SMALL_ROWS, :]
        o_ref[...] = jnp.concatenate([bsum, ssum, jnp.zeros((32 - N_MOD - SMALL_ROWS, D_MODEL), F32)], axis=0)

    return pl.pallas_call(body, name="small_sum", out_shape=jax.ShapeDtypeStruct((32, D_MODEL), F32))(got)


BIG_NAMES = ("ffn1_w_gate", "ffn1_w_up", "ffn1_w_down", "w_in", "w_uq", "w_ukv", "w_out", "ffn2_w_gate", "ffn2_w_up",
             "ffn2_w_down")
_TO_KERNEL = {"w_in": _win_to_kernel, "w_uq": _wuq_to_kernel, "w_ukv": _wukv_to_kernel}
_FROM_KERNEL = {"w_in": _win_from_kernel, "w_uq": _wuq_from_kernel, "w_ukv": _wukv_from_kernel}


def _columns_joined(w4):
    n, r, c = w4.shape
    return w4.transpose(1, 0, 2).reshape(r, n * c)


def _columns_split(g):
    r, cols = g.shape
    return g.reshape(r, N_CHIPS, cols // N_CHIPS).transpose(1, 0, 2)


def kernel(x, c, positions, w_ada, b_ada, norm_ffn1, ffn1_w_gate, ffn1_w_up, ffn1_w_down, norm_mix, w_in, conv_w, conv_b, dt_bias, a_log, d_skip, ssd_norm_w, q_norm_w, w_uq, kv_norm_w, w_ukv, mla_norm_w, w_out, norm_ffn2, ffn2_w_gate, ffn2_w_up, ffn2_w_down, norm_final, loss_target, m_w_ada, m_b_ada, m_norm_ffn1, m_ffn1_w_gate, m_ffn1_w_up, m_ffn1_w_down, m_norm_mix, m_w_in, m_conv_w, m_conv_b, m_dt_bias, m_a_log, m_d_skip, m_ssd_norm_w, m_q_norm_w, m_w_uq, m_kv_norm_w, m_w_ukv, m_mla_norm_w, m_w_out, m_norm_ffn2, m_ffn2_w_gate, m_ffn2_w_up, m_ffn2_w_down, m_norm_final, v_w_ada, v_b_ada, v_norm_ffn1, v_ffn1_w_gate, v_ffn1_w_up, v_ffn1_w_down, v_norm_mix, v_w_in, v_conv_w, v_conv_b, v_dt_bias, v_a_log, v_d_skip, v_ssd_norm_w, v_q_norm_w, v_w_uq, v_kv_norm_w, v_w_ukv, v_mla_norm_w, v_w_out, v_norm_ffn2, v_ffn2_w_gate, v_ffn2_w_up, v_ffn2_w_down, v_norm_final):
    a = dict(locals())
    held_transposed = ("ffn1_w_gate", "ffn1_w_up", "ffn2_w_gate", "ffn2_w_up", "w_in")
    for n in held_transposed:
        for p in ("", "m_", "v_"):
            a[p + n] = a[p + n].transpose(0, 2, 1)
    B, S, D = x.shape
    mx, my, mc = _place()
    chip = 2 * mx + my
    dev = 2 * chip + mc
    core = mc.astype(jnp.int32).reshape(1)

    cw_rows = jnp.pad(conv_w[0], ((0, 0), (0, D - conv_w.shape[2])))
    got, _ = _all_gather_small(jnp.concatenate([c, cw_rows, jnp.zeros((8 - B - CONV_WIDTH, D), F32)], axis=0), "gather_c")
    c_all = got[:, :B, :].reshape(N_DEV * B, D)
    conv_full = got[::2, B:B + CONV_WIDTH, :conv_w.shape[2]].transpose(1, 0, 2).reshape(CONV_WIDTH, D_CONV)

    b_cols = lax.dynamic_slice(b_ada, (0, chip * ADA_COLS), (1, ADA_COLS))
    mod_all, mod_done = _all_gather_small(_ada_fwd(c_all, w_ada[0], b_cols), "gather_mod")
    mod = lax.dynamic_slice(mod_all, (0, B * dev, 0), (N_DEV, B, ADA_COLS))[::2].transpose(1, 0, 2).reshape(B, N_MOD * D)

    first = ("ffn1_w_gate", "ffn1_w_up", "ffn1_w_down")
    later = tuple(n for n in BIG_NAMES if n not in first)
    got_first, gathered = _gather_weights([(a[n][0] + mod_done[0, 0]).astype(BF16) for n in first])
    w = dict(zip(first, got_first))
    in_flight = _gather_start([(a[n][0] + gathered[0, 0]).astype(BF16) for n in later])

    def later_weights(after):
        send_sems, recv_sems, shards, lands, _ = in_flight
        lands = _gather_wait(send_sems, recv_sems, shards, lands, after)
        wl = dict(zip(later, _gather_finish([a[n][0].astype(BF16) for n in later], lands)))
        for n, to_kernel in _TO_KERNEL.items():
            wl[n] = to_kernel(wl[n].reshape(-1, D) if n in held_transposed else _columns_joined(wl[n]))
        wl["w_out"] = wl["w_out"].reshape(D_SSD + D_MLA, D)
        return wl

    small = {n: a[n].reshape(1, -1) for n in SMALL_NAMES if n not in ("conv_w", "norm_final")}
    small["conv_w"], small["norm_final"] = conv_full, norm_final

    def shards_of(names, gw):
        g4 = []
        for n in names:
            g = gw[n]
            if n in _FROM_KERNEL:
                g = _FROM_KERNEL[n](g) if n in held_transposed else _columns_split(_FROM_KERNEL[n](g))
            g4.append(g.reshape(N_CHIPS, a[n].shape[1], a[n].shape[2]))
        return g4

    def scatter_group(names, g4, swapped):
        pair = _pair_sums(g4, swapped, "pair_sums_" + names[0])
        return (names,) + tuple(_scatter_start(pair, names[0]))

    grads, deltas, new_m, new_v = {}, {}, {}, {}

    def finish_groups(some, after):
        names, owns, gots = [], [], []
        for group_names, send_sems, recv_sems, pair, lands, _ in some:
            pair, lands = _scatter_wait(send_sems, recv_sems, pair, lands, after, group_names[0])
            names += group_names
            owns += pair
            gots += lands
        mine, theirs = _reduce_join(owns, gots, "reduce_join_" + names[0])
        for n, own, other in zip(names, mine, theirs):
            grads[n], deltas[n], new_m[n], new_v[n] = _adam_halves(a[n], a["m_" + n], a["v_" + n], own, other, core, "adam_" + n)
        return deltas[names[-1]]

    swapping, groups = [], []

    def on_grads(names, gw):
        send_sems, recv_sems, g4, lands, token = _swap_start(shards_of(names, gw), names[0])
        swapping.append((names, send_sems, recv_sems, g4, lands))
        return token[0, 0]

    def sync(after):
        token = 0.0
        while swapping:
            names, send_sems, recv_sems, g4, lands = swapping.pop(0)
            g4, swapped = _swap_wait(send_sems, recv_sems, g4, lands, after, names[0])
            groups.append(scatter_group(names, g4, swapped))
            token = groups[-1][5][0, 0]
        return token

    loss_blk, grad_x, gw, dmod, gs = _local_step(x, positions, mod + in_flight[4][0, 0], w, later_weights, small, loss_target,
                                                 on_grads, sync)

    swap_started = on_grads(first, gw)

    small_flat = _pack_small(gs).at[-1].set(loss_blk[0, 0])
    send = jnp.concatenate([dmod.reshape(MOD_ROWS, D) + swap_started, small_flat.reshape(SMALL_ROWS, D),
                            jnp.zeros((SEND_ROWS - MOD_ROWS - SMALL_ROWS, D), F32)], axis=0)
    got, _ = _all_gather_small(send, "gather_small")
    summed = _small_sum(got)
    sums = summed[N_MOD:N_MOD + SMALL_ROWS].reshape(-1)
    loss = sums[-1]
    gsmall = _unpack_small(sums)
    gsmall["conv_w"] = lax.dynamic_slice(gsmall["conv_w"].reshape(CONV_WIDTH, D_CONV), (0, chip * conv_w.shape[2]),
                                         (CONV_WIDTH, conv_w.shape[2]))
    gsmall["b_ada"] = summed[:N_MOD]
    names = ("b_ada",) + SMALL_NAMES
    rows = 208

    def pack(parts):
        flat = jnp.concatenate([parts[n].reshape(-1) for n in names])
        return jnp.pad(flat, (0, rows * LANES - flat.shape[0])).reshape(rows, LANES)

    packed = [pack({n: a[p + n] for n in names}) for p in ("", "m_", "v_")]
    g_p = pack(gsmall)
    outs = (g_p,) + tuple(_adam(packed[0], g_p, packed[1], packed[2], "adam_small"))
    for dst, flat in zip((grads, deltas, new_m, new_v), outs):
        flat, off = flat.reshape(-1), 0
        for n in names:
            dst[n] = flat[off:off + a[n].size].reshape(a[n].shape)
            off += a[n].size

    dmod_all = got[:, :MOD_ROWS, :].reshape(N_DEV * B, N_MOD * D)
    dmod_cols = lax.dynamic_slice(dmod_all, (0, chip * ADA_COLS), (N_DEV * B, ADA_COLS))
    ada = _ada_bwd(c_all, dmod_cols, w_ada[0], m_w_ada[0], v_w_ada[0])
    for dst, t in zip((grads, deltas, new_m, new_v), ada):
        dst["w_ada"] = t[None]

    sync(summed)
    last = groups.pop()
    finish_groups([last], finish_groups(groups, last[5]))
    for dst in (grads, deltas, new_m, new_v):
        for n in held_transposed:
            dst[n] = dst[n].transpose(0, 2, 1)

    order = ("w_ada", "b_ada", "norm_ffn1", "ffn1_w_gate", "ffn1_w_up", "ffn1_w_down", "norm_mix", "w_in", "conv_w", "conv_b",
             "dt_bias", "a_log", "d_skip", "ssd_norm_w", "q_norm_w", "w_uq", "kv_norm_w", "w_ukv", "mla_norm_w", "w_out",
             "norm_ffn2", "ffn2_w_gate", "ffn2_w_up", "ffn2_w_down", "norm_final")
    return (loss, grad_x, *[grads[n] for n in order], *[deltas[n] for n in order], *[new_m[n] for n in order],
            *[new_v[n] for n in order])
```

```python
import functools
import math

import jax
import jax.numpy as jnp
import numpy as np
from jax import lax
from jax.experimental import pallas as pl
from jax.experimental.pallas import tpu as pltpu

F32 = jnp.float32
BF16 = jnp.bfloat16

D_MODEL = 1024
D_FF = 2816
D_SSD = 1024
D_MLA = 1024
SSD_HEADS = 16
SSD_HEAD_DIM = 64
SSD_GROUPS = 2
SSD_STATE = 128
CONV_WIDTH = 4
CHUNK = 128
MLA_HEADS = 8
QK_NOPE = 64
QK_ROPE = 32
QK_DIM = QK_NOPE + QK_ROPE
V_HEAD = 128
Q_LORA = 384
KV_LORA = 256
ROPE_THETA = 10000.0
N_MOD = 9
EPS = 1e-6
D_CONV = D_SSD + 2 * SSD_GROUPS * SSD_STATE
D_PROJ = 3328
HEAD_LANES = 128
ADAM_LR = 0.001
ADAM_B1 = 0.9
ADAM_B2 = 0.999
ADAM_EPS = 1e-08
ADAM_WD = 0.01
ADAM_STEP = 10

LANES = 128
VMEM_LIMIT = 56 * 1024 * 1024
TOKEN_TILE = 512
WIDE_TOKEN_TILE = 1024
WGRAD_TOKEN_TILE = 2048
ATTN_FWD_Q_TILE = 1024
ATTN_FWD_KV_TILE = 1024
ATTN_BWD_TILE = 1024
N_CHIPS = 4
N_DEV = 8

MESH = pl.DeviceIdType.MESH


def _dot(a, b):
    return jnp.dot(a, b, preferred_element_type=F32)


def _dot_nt(a, b):
    return lax.dot_general(a, b, (((1,), (1,)), ((), ())), preferred_element_type=F32)


def _dot_tn(a, b):
    return lax.dot_general(a, b, (((0,), (0,)), ((), ())), preferred_element_type=F32)


def _cparams(semantics):
    return pltpu.CompilerParams(dimension_semantics=semantics, vmem_limit_bytes=VMEM_LIMIT)


def _resident(shape):
    zeros = (0,) * len(shape)
    return pl.BlockSpec(shape, lambda *_: zeros, pipeline_mode=pl.Buffered(1))


def _sigmoid(x):
    return jax.nn.sigmoid(x)


def _rms_stats(x):
    r = lax.rsqrt(jnp.mean(x * x, axis=-1, keepdims=True) + EPS)
    return x * r, r


def _rms_bwd(dn, xh, r, w):
    dxh = dn * w
    dx = r * (dxh - xh * jnp.mean(dxh * xh, axis=-1, keepdims=True))
    return dx, dn * xh


def _colsum(v):
    return jnp.sum(v, axis=0, keepdims=True)


def _ffn_fwd(x, nw, sh, sc, g, wg, wu, wd, seq, name, head=None):
    T, D = x.shape
    fs = wg.shape[1]
    tm = min(TOKEN_TILE, seq)
    tps = seq // tm

    def body(x_ref, nw_ref, sh_ref, sc_ref, g_ref, wg_ref, wu_ref, wd_ref, *rest):
        if head is None:
            xo_ref, a_ref, u_ref, f_ref = rest
        else:
            nf_ref, t_ref, xo_ref, a_ref, u_ref, f_ref, loss_ref, dnf_ref = rest

            @pl.when(pl.program_id(0) == 0)
            def _():
                loss_ref[...] = jnp.zeros_like(loss_ref)
                dnf_ref[...] = jnp.zeros_like(dnf_ref)

        xv = x_ref[...]
        xh, _ = _rms_stats(xv)
        h = (xh * nw_ref[...]) * (1.0 + sc_ref[0]) + sh_ref[0]
        hb = h.astype(BF16)
        f = jnp.zeros((tm, D), F32)
        for j in range(N_CHIPS):
            a = _dot_nt(hb, wg_ref[j])
            u = _dot_nt(hb, wu_ref[j])
            a_ref[j] = a.astype(BF16)
            u_ref[j] = u.astype(BF16)
            f = f + _dot((a * _sigmoid(a) * u).astype(BF16), wd_ref[j])
        f_ref[...] = f.astype(BF16)
        xo = xv + 0.5 * g_ref[0] * f
        if head is None:
            xo_ref[...] = xo
        else:
            xh, r = _rms_stats(xo)
            nfv = nf_ref[...]
            err = xh * nfv - t_ref[...]
            loss_ref[...] += (0.5 / D) * jnp.sum(err * err)
            dxo, dw_rows = _rms_bwd(err * (1.0 / D), xh, r, nfv)
            xo_ref[...] = dxo
            dnf_ref[...] += _colsum(dw_rows)

    rows = lambda n: pl.BlockSpec((tm, n), lambda i: (i, 0))
    act = pl.BlockSpec((N_CHIPS, tm, fs), lambda i: (0, i, 0))
    perb = pl.BlockSpec((1, 1, D), lambda i: (i // tps, 0, 0))
    sd = jax.ShapeDtypeStruct
    in_specs = [rows(D), _resident((1, D)), perb, perb, perb, _resident((N_CHIPS, fs, D)), _resident((N_CHIPS, fs, D)),
                _resident((N_CHIPS, fs, D))]
    out_specs = [rows(D), act, act, rows(D)]
    out_shape = [sd((T, D), F32), sd((N_CHIPS, T, fs), BF16), sd((N_CHIPS, T, fs), BF16), sd((T, D), BF16)]
    if head is not None:
        in_specs += [_resident((1, D)), rows(D)]
        out_specs += [pl.BlockSpec((8, LANES), lambda i: (0, 0)), pl.BlockSpec((1, D), lambda i: (0, 0))]
        out_shape += [sd((8, LANES), F32), sd((1, D), F32)]
    return pl.pallas_call(
        body, grid=(T // tm,), name=name, in_specs=in_specs, out_specs=out_specs, out_shape=out_shape,
        compiler_params=_cparams(("arbitrary",)),
    )(x, nw, sh, sc, g, wg, wu, wd, *(head or ()))


def _ffn_bwd(dxo, x, nw, sh, sc, g, a, u, f, wg, wu, wd, seq, name):
    T, D = x.shape
    fs = wg.shape[1]
    B = T // seq
    tm = min(TOKEN_TILE // 2, seq)
    tps = seq // tm

    def body(dxo_ref, x_ref, nw_ref, sh_ref, sc_ref, g_ref, a_ref, u_ref, f_ref, wg_ref, wu_ref, wd_ref,
             dx_ref, h_ref, s_ref, df_ref, da_ref, du_ref, dsh_ref, dsc_ref, dg_ref, dnw_ref):
        i = pl.program_id(0)

        @pl.when(i % tps == 0)
        def _():
            dsh_ref[...] = jnp.zeros_like(dsh_ref)
            dsc_ref[...] = jnp.zeros_like(dsc_ref)
            dg_ref[...] = jnp.zeros_like(dg_ref)

        @pl.when(i == 0)
        def _():
            dnw_ref[...] = jnp.zeros_like(dnw_ref)

        dxo_v = dxo_ref[...]
        dfb = (0.5 * g_ref[0] * dxo_v).astype(BF16)
        dg_ref[0] += _colsum(0.5 * dxo_v * f_ref[...].astype(F32))
        dh = jnp.zeros((tm, D), F32)
        for j in range(N_CHIPS):
            ds = _dot_nt(dfb, wd_ref[j])
            av = a_ref[j].astype(F32)
            uv = u_ref[j].astype(F32)
            sig = _sigmoid(av)
            sil = av * sig
            dab = (ds * uv * (sig * (1.0 + av * (1.0 - sig)))).astype(BF16)
            dub = (ds * sil).astype(BF16)
            dh = dh + _dot(dab, wg_ref[j]) + _dot(dub, wu_ref[j])
            s_ref[j] = (sil * uv).astype(BF16)
            da_ref[j] = dab
            du_ref[j] = dub
        xv = x_ref[...]
        xh, r = _rms_stats(xv)
        nwv = nw_ref[...]
        n = xh * nwv
        scale1 = 1.0 + sc_ref[0]
        dsc_ref[0] += _colsum(dh * n)
        dsh_ref[0] += _colsum(dh)
        dx, dw_rows = _rms_bwd(dh * scale1, xh, r, nwv)
        dnw_ref[...] += _colsum(dw_rows)
        dx_ref[...] = dxo_v + dx
        h_ref[...] = (n * scale1 + sh_ref[0]).astype(BF16)
        df_ref[...] = dfb

    rows = lambda n: pl.BlockSpec((tm, n), lambda i: (i, 0))
    act = pl.BlockSpec((N_CHIPS, tm, fs), lambda i: (0, i, 0))
    perb = pl.BlockSpec((1, 1, D), lambda i: (i // tps, 0, 0))
    sd = jax.ShapeDtypeStruct
    return pl.pallas_call(
        body, grid=(T // tm,), name=name,
        in_specs=[rows(D), rows(D), _resident((1, D)), perb, perb, perb, act, act, rows(D),
                  _resident((N_CHIPS, fs, D)), _resident((N_CHIPS, fs, D)), _resident((N_CHIPS, fs, D))],
        out_specs=[rows(D), rows(D), act, rows(D), act, act, perb, perb, perb, pl.BlockSpec((1, D), lambda i: (0, 0))],
        out_shape=[sd((T, D), F32), sd((T, D), BF16), sd((N_CHIPS, T, fs), BF16), sd((T, D), BF16),
                   sd((N_CHIPS, T, fs), BF16), sd((N_CHIPS, T, fs), BF16), sd((B, 1, D), F32), sd((B, 1, D), F32),
                   sd((B, 1, D), F32), sd((1, D), F32)],
        compiler_params=_cparams(("arbitrary",)),
    )(dxo, x, nw, sh, sc, g, a, u, f, wg, wu, wd)


def _ffn_wgrad(h, s, df, da, du, after, name):
    T, D = h.shape
    fs = s.shape[2]
    tt = min(WGRAD_TOKEN_TILE, T)
    nt = T // tt

    def body(h_ref, s_ref, df_ref, da_ref, du_ref, after_ref, dgate_ref, dup_ref, ddown_ref, gate_acc, up_acc, down_acc):
        @pl.when(pl.program_id(1) == 0)
        def _():
            gate_acc[...] = jnp.zeros_like(gate_acc)
            up_acc[...] = jnp.zeros_like(up_acc)
            down_acc[...] = jnp.zeros_like(down_acc)

        hv = h_ref[...]
        gate_acc[...] += _dot_tn(da_ref[0], hv)
        up_acc[...] += _dot_tn(du_ref[0], hv)
        down_acc[...] += _dot_tn(s_ref[0], df_ref[...])

        @pl.when(pl.program_id(1) == nt - 1)
        def _():
            dgate_ref[0] = gate_acc[...].astype(BF16)
            dup_ref[0] = up_acc[...].astype(BF16)
            ddown_ref[0] = down_acc[...].astype(BF16)

    rows = pl.BlockSpec((tt, D), lambda j, t: (t, 0))
    act = pl.BlockSpec((1, tt, fs), lambda j, t: (j, t, 0))
    shard = pl.BlockSpec((1, fs, D), lambda j, t: (j, 0, 0))
    return pl.pallas_call(
        body, grid=(N_CHIPS, nt), name=name,
        in_specs=[rows, act, rows, act, act, pl.BlockSpec(memory_space=pl.ANY)],
        out_specs=[shard] * 3, out_shape=[jax.ShapeDtypeStruct((N_CHIPS, fs, D), BF16)] * 3,
        scratch_shapes=[pltpu.VMEM((fs, D), F32)] * 3,
        compiler_params=_cparams(("arbitrary", "arbitrary")),
    )(h, s, df, da, du, after)


def _mm_tn(xa, ya, tn, name):
    T, K = xa.shape
    N = ya.shape[1]
    tt = min(WGRAD_TOKEN_TILE, T)
    nt = T // tt

    def body(x_ref, y_ref, o_ref, acc_ref):
        @pl.when(pl.program_id(1) == 0)
        def _():
            acc_ref[...] = jnp.zeros_like(acc_ref)

        acc_ref[...] += _dot_tn(x_ref[...], y_ref[...])

        @pl.when(pl.program_id(1) == nt - 1)
        def _():
            o_ref[...] = acc_ref[...].astype(BF16)

    return pl.pallas_call(
        body, grid=(N // tn, nt), name=name,
        in_specs=[pl.BlockSpec((tt, K), lambda j, t: (t, 0)), pl.BlockSpec((tt, tn), lambda j, t: (t, j))],
        out_specs=pl.BlockSpec((K, tn), lambda j, t: (0, j)),
        out_shape=jax.ShapeDtypeStruct((K, N), BF16),
        scratch_shapes=[pltpu.VMEM((K, tn), F32)],
        compiler_params=_cparams(("arbitrary", "arbitrary")),
    )(xa, ya)


_PROJ_SPLITS = (0, 1024, 2560, 2944, 3200, 3328)


def _inproj_fwd(x, nw, sh, sc, win, seq):
    T, D = x.shape
    tm = min(WIDE_TOKEN_TILE, seq)
    tps = seq // tm
    widths = [b - a for a, b in zip(_PROJ_SPLITS[:-1], _PROJ_SPLITS[1:])]
    dtypes = [BF16, BF16, F32, F32, F32]

    def body(x_ref, nw_ref, sh_ref, sc_ref, w_ref, *outs):
        xh, _ = _rms_stats(x_ref[...])
        h = (xh * nw_ref[...]) * (1.0 + sc_ref[0]) + sh_ref[0]
        proj = _dot_nt(h.astype(BF16), w_ref[...])
        for o, lo, hi in zip(outs, _PROJ_SPLITS[:-1], _PROJ_SPLITS[1:]):
            o[...] = proj[:, lo:hi].astype(o.dtype)

    rows = lambda n: pl.BlockSpec((tm, n), lambda i: (i, 0))
    perb = pl.BlockSpec((1, 1, D), lambda i: (i // tps, 0, 0))
    return pl.pallas_call(
        body, grid=(T // tm,), name="inproj_fwd",
        in_specs=[rows(D), _resident((1, D)), perb, perb, _resident((D_PROJ, D))],
        out_specs=[rows(w) for w in widths],
        out_shape=[jax.ShapeDtypeStruct((T, w), dt) for w, dt in zip(widths, dtypes)],
        compiler_params=_cparams(("arbitrary",)),
    )(x, nw, sh, sc, win)


def _inproj_bwd(dx2, x, nw, sh, sc, win, dz, dxbc, dcq, dckv, ddtk_a, ddtk_b, seq):
    T, D = x.shape
    B = T // seq
    tm = min(TOKEN_TILE, seq)
    tps = seq // tm

    def body(dx2_ref, x_ref, nw_ref, sh_ref, sc_ref, w_ref, dz_ref, dxbc_ref, dcq_ref, dckv_ref, da_ref, db_ref,
             dx_ref, h_ref, dp_ref, dsh_ref, dsc_ref, dnw_ref):
        i = pl.program_id(0)

        @pl.when(i % tps == 0)
        def _():
            dsh_ref[...] = jnp.zeros_like(dsh_ref)
            dsc_ref[...] = jnp.zeros_like(dsc_ref)

        @pl.when(i == 0)
        def _():
            dnw_ref[...] = jnp.zeros_like(dnw_ref)

        dproj = jnp.concatenate(
            [dz_ref[...], dxbc_ref[...], dcq_ref[...].astype(BF16), dckv_ref[...].astype(BF16),
             (da_ref[...] + db_ref[...]).astype(BF16)], axis=1)
        dp_ref[...] = dproj
        dh = _dot(dproj, w_ref[...])
        xh, r = _rms_stats(x_ref[...])
        nwv = nw_ref[...]
        n = xh * nwv
        scale1 = 1.0 + sc_ref[0]
        dsc_ref[0] += _colsum(dh * n)
        dsh_ref[0] += _colsum(dh)
        dx, dw_rows = _rms_bwd(dh * scale1, xh, r, nwv)
        dnw_ref[...] += _colsum(dw_rows)
        dx_ref[...] = dx2_ref[...] + dx
        h_ref[...] = (n * scale1 + sh_ref[0]).astype(BF16)

    rows = lambda n: pl.BlockSpec((tm, n), lambda i: (i, 0))
    perb = pl.BlockSpec((1, 1, D), lambda i: (i // tps, 0, 0))
    sd = jax.ShapeDtypeStruct
    return pl.pallas_call(
        body, grid=(T // tm,), name="inproj_bwd",
        in_specs=[rows(D), rows(D), _resident((1, D)), perb, perb, _resident((D_PROJ, D)),
                  rows(1024), rows(D_CONV), rows(Q_LORA), rows(KV_LORA), rows(LANES), rows(LANES)],
        out_specs=[rows(D), rows(D), rows(D_PROJ), perb, perb, pl.BlockSpec((1, D), lambda i: (0, 0))],
        out_shape=[sd((T, D), F32), sd((T, D), BF16), sd((T, D_PROJ), BF16), sd((B, 1, D), F32), sd((B, 1, D), F32),
                   sd((1, D), F32)],
        compiler_params=_cparams(("arbitrary",)),
    )(dx2, x, nw, sh, sc, win, dz, dxbc, dcq, dckv, ddtk_a, ddtk_b)


SUBLANES = 8


def _shift_down(v, k):
    r = pltpu.roll(v, k, 0)
    row = lax.broadcasted_iota(jnp.int32, (SUBLANES, v.shape[1]), 0)
    return jnp.concatenate([jnp.where(row < k, 0.0, r[:SUBLANES]), r[SUBLANES:]], axis=0)


def _shift_up(v, k):
    n = v.shape[0]
    r = pltpu.roll(v, n - k, 0)
    row = lax.broadcasted_iota(jnp.int32, (SUBLANES, v.shape[1]), 0)
    return jnp.concatenate([r[:n - SUBLANES], jnp.where(row >= SUBLANES - k, 0.0, r[n - SUBLANES:])], axis=0)


def _conv_pre(xv, w_ref, b_ref):
    pre = b_ref[...] + w_ref[CONV_WIDTH - 1:CONV_WIDTH, :] * xv
    for k in range(1, CONV_WIDTH):
        pre = pre + w_ref[CONV_WIDTH - 1 - k:CONV_WIDTH - k, :] * _shift_down(xv, k)
    return pre


def _conv_fwd(xraw, cw, cb):
    B, S, C = xraw.shape

    def body(x_ref, w_ref, b_ref, o_ref):
        pre = _conv_pre(x_ref[0].astype(F32), w_ref, b_ref)
        o_ref[0] = (pre * _sigmoid(pre)).astype(BF16)

    blk = pl.BlockSpec((1, S, LANES), lambda b, j: (b, 0, j))
    return pl.pallas_call(
        body, grid=(B, C // LANES), name="conv_fwd",
        in_specs=[blk, pl.BlockSpec((CONV_WIDTH, LANES), lambda b, j: (0, j)), pl.BlockSpec((1, LANES), lambda b, j: (0, j))],
        out_specs=blk, out_shape=jax.ShapeDtypeStruct((B, S, C), BF16),
        compiler_params=_cparams(("arbitrary", "arbitrary")),
    )(xraw, cw, cb)


def _conv_bwd(dout, xraw, cw, cb):
    B, S, C = xraw.shape

    def body(d_ref, x_ref, w_ref, b_ref, dx_ref, dw_ref, db_ref):
        @pl.when(pl.program_id(1) == 0)
        def _():
            dw_ref[...] = jnp.zeros_like(dw_ref)
            db_ref[...] = jnp.zeros_like(db_ref)

        xv = x_ref[0].astype(F32)
        pre = _conv_pre(xv, w_ref, b_ref)
        sig = _sigmoid(pre)
        dpre = d_ref[0].astype(F32) * (sig * (1.0 + pre * (1.0 - sig)))
        dx = w_ref[CONV_WIDTH - 1:CONV_WIDTH, :] * dpre
        for k in range(1, CONV_WIDTH):
            dx = dx + w_ref[CONV_WIDTH - 1 - k:CONV_WIDTH - k, :] * _shift_up(dpre, k)
        dx_ref[0] = dx.astype(BF16)
        db_ref[...] += _colsum(dpre)
        dws = [_colsum(dpre * (xv if k == 0 else _shift_down(xv, k))) for k in range(CONV_WIDTH - 1, -1, -1)]
        dw_ref[...] += jnp.concatenate(dws, axis=0)

    blk = pl.BlockSpec((1, S, LANES), lambda j, b: (b, 0, j))
    wspec = pl.BlockSpec((CONV_WIDTH, LANES), lambda j, b: (0, j))
    bspec = pl.BlockSpec((1, LANES), lambda j, b: (0, j))
    return pl.pallas_call(
        body, grid=(C // LANES, B), name="conv_bwd",
        in_specs=[blk, blk, wspec, bspec], out_specs=[blk, wspec, bspec],
        out_shape=[jax.ShapeDtypeStruct((B, S, C), BF16), jax.ShapeDtypeStruct((CONV_WIDTH, C), F32),
                   jax.ShapeDtypeStruct((1, C), F32)],
        compiler_params=_cparams(("arbitrary", "arbitrary")),
    )(dout, xraw, cw, cb)


def _softplus(x):
    return jnp.maximum(x, 0.0) + jnp.log(1.0 + jnp.exp(-jnp.abs(x)))


def _ssd_common(xbc_ref, dtk_ref, dtb_ref, alog_ref, e_ref):
    L = CHUNK
    xbc = xbc_ref[0]
    xs = xbc[:, :D_SSD].astype(F32)
    bm = xbc[:, D_SSD:D_SSD + 256]
    cm = xbc[:, D_SSD + 256:D_SSD + 512]
    head = lax.broadcasted_iota(jnp.int32, (1, LANES), 1) < SSD_HEADS
    a128 = jnp.where(head, -jnp.exp(alog_ref[...]), 0.0)
    pre = dtk_ref[0] + dtb_ref[...]
    dt = _softplus(pre)
    dA = dt * a128
    row = lax.broadcasted_iota(jnp.int32, (L, L), 0)
    col = lax.broadcasted_iota(jnp.int32, (L, L), 1)
    causal = col <= row
    tri = causal.astype(F32)
    triT = (row <= col).astype(F32)
    tri = causal.astype(BF16)
    triT = (row <= col).astype(BF16)
    dA3 = _split3(dA)
    acum = _sum3(lambda part: _dot(tri, part), dA3)
    acumT = _sum3(lambda part: _dot_tn(part, triT), dA3)
    E = e_ref[...]
    acum_f = _spread(acum, E)
    dt_f = _spread(dt, E)
    e_f = jnp.exp(acum_f)
    w_f = jnp.exp(acum_f[L - 1:L, :] - acum_f)
    xt = xs * dt_f
    return dict(xs=xs, bm=bm, cm=cm, a128=a128, pre=pre, dt=dt, causal=causal, tri=tri, triT=triT, acum=acum,
                acumT=acumT, E=E, dt_f=dt_f, e_f=e_f, w_f=w_f, xt=xt, head=head)


def _split3(x):
    p1 = x.astype(BF16)
    r1 = x - p1.astype(F32)
    p2 = r1.astype(BF16)
    return p1, p2, (r1 - p2.astype(F32)).astype(BF16)


def _sum3(mm, parts):
    return (mm(parts[0]) + mm(parts[1])) + mm(parts[2])


def _spread(v, e):
    return _sum3(lambda part: _dot(part, e), _split3(v))


def _gather_heads(v, e):
    return _sum3(lambda part: _dot_nt(part, e), _split3(v))


def _head_mask(k):
    lane = lax.broadcasted_iota(jnp.int32, (CHUNK, LANES), 1)
    return (lane >= SSD_HEAD_DIM) if k == 1 else (lane < SSD_HEAD_DIM)


def _pair_decay(alast, h0):
    row = lax.broadcasted_iota(jnp.int32, (2 * SSD_HEAD_DIM, SSD_STATE), 0)
    return jnp.exp(jnp.where(row < SSD_HEAD_DIM, alast[:, h0:h0 + 1], alast[:, h0 + 1:h0 + 2]))


def _decay_matrix(q, h):
    seg = q["acum"][:, h:h + 1] - q["acumT"][h:h + 1, :]
    return jnp.exp(jnp.where(q["causal"], seg, -1e30))


def _gated_norm(y, zz, nw):
    sig = _sigmoid(zz)
    sil = zz * sig
    yg = y * sil
    half = D_SSD // SSD_GROUPS
    parts = []
    for g in range(SSD_GROUPS):
        xh, r = _rms_stats(yg[:, g * half:(g + 1) * half])
        parts.append((xh, r))
    return sig, sil, parts


def _ssd_fwd(xbc, dtk, z, dtb, alog, dsk, nw, expand):
    B, S, _ = xbc.shape
    L = CHUNK
    nc = S // L

    def body(xbc_ref, dtk_ref, z_ref, dtb_ref, alog_ref, dsk_ref, nw_ref, e_ref, y_ref, ys_ref, prev_ref, st_ref):
        @pl.when(pl.program_id(0) == 0)
        def _():
            st_ref[...] = jnp.zeros_like(st_ref)

        for b in range(B):
            one = lambda ref: ref.at[pl.ds(b, 1)]
            sequence_step(one(xbc_ref), one(dtk_ref), one(z_ref), dtb_ref, alog_ref, dsk_ref, nw_ref, e_ref, one(y_ref),
                          one(ys_ref), one(prev_ref), st_ref.at[b])

    def sequence_step(xbc_ref, dtk_ref, z_ref, dtb_ref, alog_ref, dsk_ref, nw_ref, e_ref, y_ref, ys_ref, prev_ref, st_ref):
        q = _ssd_common(xbc_ref, dtk_ref, dtb_ref, alog_ref, e_ref)
        xtb = q["xt"].astype(BF16)
        xwb = (q["xt"] * q["w_f"]).astype(BF16)
        alast = q["acum"][L - 1:L, :]
        ys = []
        for g in range(SSD_GROUPS):
            bg = q["bm"][:, g * 128:(g + 1) * 128]
            cg = q["cm"][:, g * 128:(g + 1) * 128]
            G = _dot_nt(cg, bg)
            for pr in range(SSD_HEADS // SSD_GROUPS // 2):
                h0 = g * 8 + 2 * pr
                lo = h0 * SSD_HEAD_DIM
                xt_p = xtb[:, lo:lo + 128]
                ydiag = jnp.zeros((L, LANES), F32)
                for k in range(2):
                    M = (G * _decay_matrix(q, h0 + k)).astype(BF16)
                    ydiag = ydiag + _dot(M, jnp.where(_head_mask(k), xt_p, jnp.zeros_like(xt_p)))
                hp = st_ref[lo:lo + 128, :]
                prev_ref[0, 0, lo:lo + 128, :] = hp.astype(BF16)
                zoff = _dot_nt(cg, hp.astype(BF16))
                ys.append(ydiag + zoff * q["e_f"][:, lo:lo + 128])
                st_ref[lo:lo + 128, :] = _pair_decay(alast, h0) * hp + _dot_tn(xwb[:, lo:lo + 128], bg)
        y = jnp.concatenate(ys, axis=1) + dsk_ref[...] * q["xs"]
        y_ref[0] = y.astype(BF16)
        _, _, parts = _gated_norm(y, z_ref[0].astype(F32), nw_ref[...])
        half = D_SSD // SSD_GROUPS
        ys_ref[0] = jnp.concatenate(
            [xh * nw_ref[:, g * half:(g + 1) * half] for g, (xh, _) in enumerate(parts)], axis=1).astype(BF16)

    chunk = lambda n: pl.BlockSpec((B, L, n), lambda c: (0, c, 0))
    vec = pl.BlockSpec((1, LANES), lambda c: (0, 0))
    return pl.pallas_call(
        body, grid=(nc,), name="ssd_fwd",
        in_specs=[chunk(D_CONV), chunk(LANES), chunk(D_SSD), vec, vec, pl.BlockSpec((1, D_SSD), lambda c: (0, 0)),
                  pl.BlockSpec((1, D_SSD), lambda c: (0, 0)), pl.BlockSpec((LANES, D_SSD), lambda c: (0, 0))],
        out_specs=[chunk(D_SSD), chunk(D_SSD), pl.BlockSpec((B, 1, D_SSD, SSD_STATE), lambda c: (0, c, 0, 0))],
        out_shape=[jax.ShapeDtypeStruct((B, S, D_SSD), BF16), jax.ShapeDtypeStruct((B, S, D_SSD), BF16),
                   jax.ShapeDtypeStruct((B, nc, D_SSD, SSD_STATE), BF16)],
        scratch_shapes=[pltpu.VMEM((B, D_SSD, SSD_STATE), F32)],
        compiler_params=_cparams(("arbitrary",)),
    )(xbc, dtk, z, dtb, alog, dsk, nw, expand)


def _ssd_bwd(xbc, dtk, z, y, prev, dys, dtb, alog, dsk, nw, expand):
    B, S, _ = xbc.shape
    L = CHUNK
    nc = S // L
    half = D_SSD // SSD_GROUPS

    def body(xbc_ref, dtk_ref, z_ref, y_ref, prev_ref, dys_ref, dtb_ref, alog_ref, dsk_ref, nw_ref, e_ref,
             dxbc_ref, ddtk_ref, dz_ref, dnw_ref, dvec_ref, dh_ref, dskc_ref):
        @pl.when(pl.program_id(0) == 0)
        def _():
            dnw_ref[...] = jnp.zeros_like(dnw_ref)
            dvec_ref[...] = jnp.zeros_like(dvec_ref)
            dskc_ref[...] = jnp.zeros_like(dskc_ref)
            dh_ref[...] = jnp.zeros_like(dh_ref)

        for b in range(B):
            one = lambda ref: ref.at[pl.ds(b, 1)]
            sequence_step(one(xbc_ref), one(dtk_ref), one(z_ref), one(y_ref), one(prev_ref), one(dys_ref), dtb_ref, alog_ref,
                          dsk_ref, nw_ref, e_ref, one(dxbc_ref), one(ddtk_ref), one(dz_ref), dnw_ref, dvec_ref, dh_ref.at[b],
                          dskc_ref)

        @pl.when(pl.program_id(0) == nc - 1)
        def _():
            dvec_ref[2:3, :] = _gather_heads(jnp.broadcast_to(dskc_ref[...], (8, D_SSD)), e_ref[...])[0:1, :]

    def sequence_step(xbc_ref, dtk_ref, z_ref, y_ref, prev_ref, dys_ref, dtb_ref, alog_ref, dsk_ref, nw_ref, e_ref,
                      dxbc_ref, ddtk_ref, dz_ref, dnw_ref, dvec_ref, dh_ref, dskc_ref):
        q = _ssd_common(xbc_ref, dtk_ref, dtb_ref, alog_ref, e_ref)
        E = q["E"]
        xs = q["xs"]
        yv = y_ref[0].astype(F32)
        zz = z_ref[0].astype(F32)
        sig, sil, parts = _gated_norm(yv, zz, nw_ref[...])
        dn = dys_ref[0].astype(F32)
        dyg, dnw_rows = [], []
        for g, (xh, r) in enumerate(parts):
            dpart, dw_rows = _rms_bwd(dn[:, g * half:(g + 1) * half], xh, r, nw_ref[:, g * half:(g + 1) * half])
            dyg.append(dpart)
            dnw_rows.append(dw_rows)
        dyg = jnp.concatenate(dyg, axis=1)
        dnw_ref[...] += _colsum(jnp.concatenate(dnw_rows, axis=1))
        dY = dyg * sil
        dz_ref[0] = (dyg * yv * (sig * (1.0 + zz * (1.0 - sig)))).astype(BF16)
        dsk_f = dsk_ref[...]
        dskc_ref[...] += _colsum(dY * xs)
        dYb = dY.astype(BF16)
        xtb = q["xt"].astype(BF16)
        xwb = (q["xt"] * q["w_f"]).astype(BF16)
        acum = q["acum"]
        alast = acum[L - 1:L, :]
        lane_id = lax.broadcasted_iota(jnp.int32, (L, LANES), 1)
        sub_id = lax.broadcasted_iota(jnp.int32, (LANES, L), 0)
        lane_row = lax.broadcasted_iota(jnp.int32, (1, LANES), 1)
        da_rows = jnp.zeros((L, LANES), F32)
        daT = jnp.zeros((LANES, L), F32)
        dxt, prod_off, prod_st, dbs, dcs = [], [], [], [], []
        hsum_row = jnp.zeros((1, LANES), F32)
        for g in range(SSD_GROUPS):
            bg = q["bm"][:, g * 128:(g + 1) * 128]
            cg = q["cm"][:, g * 128:(g + 1) * 128]
            G = _dot_nt(cg, bg)
            dG = jnp.zeros((L, L), F32)
            dcg = jnp.zeros((L, SSD_STATE), F32)
            dbg = jnp.zeros((L, SSD_STATE), F32)
            for pr in range(SSD_HEADS // SSD_GROUPS // 2):
                h0 = g * 8 + 2 * pr
                lo = h0 * SSD_HEAD_DIM
                cols = slice(lo, lo + 128)
                dY_p = dYb[:, cols]
                xt_p = xtb[:, cols]
                dxt_p = jnp.zeros((L, LANES), F32)
                for k in range(2):
                    h = h0 + k
                    Lm = _decay_matrix(q, h)
                    Mf = G * Lm
                    dYk = jnp.where(_head_mask(k), dY_p, jnp.zeros_like(dY_p))
                    dM = _dot_nt(dYk, xt_p)
                    dxt_p = dxt_p + _dot_tn(Mf.astype(BF16), dYk)
                    dG = dG + dM * Lm
                    Q = dM * Mf
                    da_rows = da_rows + jnp.where(lane_id == h, jnp.sum(Q, axis=1, keepdims=True), 0.0)
                    daT = daT + jnp.where(sub_id == h, jnp.sum(Q, axis=0, keepdims=True), 0.0)
                hpb = prev_ref[0, 0, lo:lo + 128, :]
                hp = hpb.astype(F32)
                zoff = _dot_nt(cg, hpb)
                e_p = q["e_f"][:, cols]
                dY_pf = dY[:, cols]
                dZb = (dY_pf * e_p).astype(BF16)
                dcg = dcg + _dot(dZb, hpb)
                dhp_off = _dot_tn(dZb, cg)
                prod_off.append(dY_pf * zoff * e_p)
                dS = dh_ref[lo:lo + 128, :]
                dSb = dS.astype(BF16)
                U = _dot_nt(bg, dSb)
                dxt_p = dxt_p + U * q["w_f"][:, cols]
                dbg = dbg + _dot(xwb[:, cols], dSb)
                prod_st.append(q["xt"][:, cols] * U)
                dh_ref[lo:lo + 128, :] = _pair_decay(alast, h0) * dS + dhp_off
                dsh = dS * hp
                for k in range(2):
                    total = jnp.sum(dsh[k * SSD_HEAD_DIM:(k + 1) * SSD_HEAD_DIM, :], axis=(0, 1), keepdims=True)
                    hsum_row = hsum_row + jnp.where(lane_row == h0 + k, total, 0.0)
                dxt.append(dxt_p)
            dGb = dG.astype(BF16)
            dcs.append(dcg + _dot(dGb, bg))
            dbs.append(dbg + _dot_tn(dGb, cg))
        dxt = jnp.concatenate(dxt, axis=1)
        da_rows = da_rows + _gather_heads(jnp.concatenate(prod_off, axis=1), E)
        dww = _gather_heads(jnp.concatenate(prod_st, axis=1), E) * jnp.exp(alast - acum)
        da_rows = da_rows - dww
        dlast = _colsum(dww) + jnp.exp(alast) * hsum_row
        triT = q["triT"]
        ddA = (_sum3(lambda part: _dot(triT, part), _split3(da_rows))
               - _sum3(lambda part: _dot_nt(triT, part), _split3(daT)) + dlast)
        ddA = jnp.where(q["head"], ddA, 0.0)
        ddt = ddA * q["a128"] + _gather_heads(dxt * xs, E)
        ddt_raw = jnp.where(q["head"], ddt * _sigmoid(q["pre"]), 0.0)
        ddtk_ref[0] = ddt_raw
        dxs = dxt * q["dt_f"] + dsk_f * dY
        dxbc_ref[0] = jnp.concatenate([dxs] + dbs + dcs, axis=1).astype(BF16)
        dvec_ref[0:1, :] += _colsum(ddt_raw)
        dvec_ref[1:2, :] += _colsum(ddA * q["dt"]) * q["a128"]

    rev = lambda n: pl.BlockSpec((B, L, n), lambda c: (0, nc - 1 - c, 0))
    vec = pl.BlockSpec((1, LANES), lambda c: (0, 0))
    sd = jax.ShapeDtypeStruct
    return pl.pallas_call(
        body, grid=(nc,), name="ssd_bwd",
        in_specs=[rev(D_CONV), rev(LANES), rev(D_SSD), rev(D_SSD),
                  pl.BlockSpec((B, 1, D_SSD, SSD_STATE), lambda c: (0, nc - 1 - c, 0, 0)), rev(D_SSD), vec, vec,
                  pl.BlockSpec((1, D_SSD), lambda c: (0, 0)),
                  pl.BlockSpec((1, D_SSD), lambda c: (0, 0)), pl.BlockSpec((LANES, D_SSD), lambda c: (0, 0))],
        out_specs=[rev(D_CONV), rev(LANES), rev(D_SSD), pl.BlockSpec((1, D_SSD), lambda c: (0, 0)),
                   pl.BlockSpec((8, LANES), lambda c: (0, 0))],
        out_shape=[sd((B, S, D_CONV), BF16), sd((B, S, LANES), F32), sd((B, S, D_SSD), BF16), sd((1, D_SSD), F32),
                   sd((8, LANES), F32)],
        scratch_shapes=[pltpu.VMEM((B, D_SSD, SSD_STATE), F32), pltpu.VMEM((1, D_SSD), F32)],
        compiler_params=_cparams(("arbitrary",)),
    )(xbc, dtk, z, y, prev, dys, dtb, alog, dsk, nw, expand)


def _rope_tables(pos_ref, invf_ref, place_ref):
    ang = invf_ref[...] * pos_ref[0].astype(F32)
    place = place_ref[...]
    cosf = 1.0 + _sum3(lambda part: _dot_tn(part, place), _split3(jnp.cos(ang) - 1.0))
    sinf = _sum3(lambda part: _dot_tn(part, place), _split3(jnp.sin(ang)))
    return cosf, sinf


def _rot(u):
    lane = lax.broadcasted_iota(jnp.int32, u.shape, 1)
    first = (lane >= QK_NOPE) & (lane < QK_NOPE + QK_ROPE // 2)
    second = (lane >= QK_NOPE + QK_ROPE // 2) & (lane < QK_DIM)
    return jnp.where(first, -pltpu.roll(u, LANES - QK_ROPE // 2, 1), jnp.where(second, pltpu.roll(u, QK_ROPE // 2, 1), 0.0))


def _rope_lanes(shape):
    lane = lax.broadcasted_iota(jnp.int32, shape, 1)
    return (lane >= QK_NOPE) & (lane < QK_DIM)


def _mla_prep(cq, ckv, dtk, pos, qw, kvw, wuq, wukv, invf, place):
    T = cq.shape[0]
    tm = min(WIDE_TOKEN_TILE, T)
    scale = 1.0 / math.sqrt(QK_DIM)
    HW = MLA_HEADS * HEAD_LANES

    def body(cq_ref, ckv_ref, dtk_ref, pos_ref, qw_ref, kvw_ref, wuq_ref, wukv_ref, invf_ref, place_ref, q_ref, k_ref, v_ref,
             cos_ref, sin_ref):
        xh, _ = _rms_stats(cq_ref[...])
        qv = _dot((xh * qw_ref[...]).astype(BF16), wuq_ref[...])
        xh, _ = _rms_stats(ckv_ref[...])
        kv = _dot((xh * kvw_ref[...]).astype(BF16), wukv_ref[...])
        cosf, sinf = _rope_tables(pos_ref, invf_ref, place_ref)
        cos_ref[...] = cosf
        sin_ref[...] = sinf
        rope = lambda u: u * cosf + _rot(u) * sinf
        dtkv = dtk_ref[...]
        kr = rope(jnp.where(_rope_lanes(dtkv.shape), dtkv, 0.0))
        for h in range(MLA_HEADS):
            cols = slice(h * HEAD_LANES, (h + 1) * HEAD_LANES)
            q_ref[:, cols] = (rope(qv[:, cols]) * scale).astype(BF16)
            k_ref[:, cols] = (kv[:, cols] + kr).astype(BF16)
        v_ref[...] = kv[:, HW:].astype(BF16)

    rows = lambda n: pl.BlockSpec((tm, n), lambda i: (i, 0))
    return pl.pallas_call(
        body, grid=(T // tm,), name="mla_prep",
        in_specs=[rows(Q_LORA), rows(KV_LORA), rows(LANES), pl.BlockSpec((1, 1, tm), lambda i: (i, 0, 0)),
                  _resident((1, Q_LORA)), _resident((1, KV_LORA)), _resident((Q_LORA, HW)), _resident((KV_LORA, 2 * HW)),
                  _resident((QK_ROPE // 2, 1)), _resident((QK_ROPE // 2, LANES))],
        out_specs=[rows(HW), rows(HW), rows(HW), rows(LANES), rows(LANES)],
        out_shape=[jax.ShapeDtypeStruct((T, HW), BF16)] * 3 + [jax.ShapeDtypeStruct((T, LANES), F32)] * 2,
        compiler_params=_cparams(("arbitrary",)),
    )(cq, ckv, dtk, pos.reshape(T // tm, 1, tm), qw, kvw, wuq, wukv, invf, place)


def _mla_prep_bwd(dq, dk, dv, cq, ckv, cos_t, sin_t, qw, kvw, wuq, wukv):
    T = cq.shape[0]
    tm = min(WIDE_TOKEN_TILE, T)
    scale = 1.0 / math.sqrt(QK_DIM)
    HW = MLA_HEADS * HEAD_LANES

    def body(dq_ref, dk_ref, dv_ref, cq_ref, ckv_ref, cos_ref, sin_ref, qw_ref, kvw_ref, wuq_ref, wukv_ref,
             dcq_ref, dckv_ref, ddtk_ref, qn_ref, kvn_ref, dqo_ref, dkvo_ref, dqw_ref, dkvw_ref):
        @pl.when(pl.program_id(0) == 0)
        def _():
            dqw_ref[...] = jnp.zeros_like(dqw_ref)
            dkvw_ref[...] = jnp.zeros_like(dkvw_ref)

        cosf, sinf = cos_ref[...], sin_ref[...]
        unrope = lambda d: d * cosf - _rot(d * sinf)
        dkr = jnp.zeros((tm, LANES), F32)
        nope = lax.broadcasted_iota(jnp.int32, (tm, LANES), 1) < QK_NOPE
        for h in range(MLA_HEADS):
            cols = slice(h * HEAD_LANES, (h + 1) * HEAD_LANES)
            dqo_ref[:, cols] = unrope(dq_ref[:, cols].astype(F32) * scale).astype(BF16)
            dkh = dk_ref[:, cols].astype(F32)
            dkr = dkr + jnp.where(_rope_lanes(dkh.shape), dkh, 0.0)
            dkvo_ref[:, cols] = jnp.where(nope, dkh, 0.0).astype(BF16)
        dkvo_ref[:, HW:] = dv_ref[...].astype(BF16)
        ddtk_ref[...] = unrope(dkr)
        xh, r = _rms_stats(cq_ref[...])
        qn_ref[...] = (xh * qw_ref[...]).astype(BF16)
        dx, dw_rows = _rms_bwd(_dot_nt(dqo_ref[...], wuq_ref[...]), xh, r, qw_ref[...])
        dcq_ref[...] = dx
        dqw_ref[...] += _colsum(dw_rows)
        xh, r = _rms_stats(ckv_ref[...])
        kvn_ref[...] = (xh * kvw_ref[...]).astype(BF16)
        dx, dw_rows = _rms_bwd(_dot_nt(dkvo_ref[...], wukv_ref[...]), xh, r, kvw_ref[...])
        dckv_ref[...] = dx
        dkvw_ref[...] += _colsum(dw_rows)

    rows = lambda n: pl.BlockSpec((tm, n), lambda i: (i, 0))
    sd = jax.ShapeDtypeStruct
    return pl.pallas_call(
        body, grid=(T // tm,), name="mla_prep_bwd",
        in_specs=[rows(HW), rows(HW), rows(HW), rows(Q_LORA), rows(KV_LORA), rows(LANES), rows(LANES), _resident((1, Q_LORA)),
                  _resident((1, KV_LORA)), _resident((Q_LORA, HW)), _resident((KV_LORA, 2 * HW))],
        out_specs=[rows(Q_LORA), rows(KV_LORA), rows(LANES), rows(Q_LORA), rows(KV_LORA), rows(HW), rows(2 * HW),
                   pl.BlockSpec((1, Q_LORA), lambda i: (0, 0)), pl.BlockSpec((1, KV_LORA), lambda i: (0, 0))],
        out_shape=[sd((T, Q_LORA), F32), sd((T, KV_LORA), F32), sd((T, LANES), F32), sd((T, Q_LORA), BF16),
                   sd((T, KV_LORA), BF16), sd((T, HW), BF16), sd((T, 2 * HW), BF16), sd((1, Q_LORA), F32),
                   sd((1, KV_LORA), F32)],
        compiler_params=_cparams(("arbitrary",)),
    )(dq, dk, dv, cq, ckv, cos_t, sin_t, qw, kvw, wuq, wukv)


def _causal_mask(t):
    row = lax.broadcasted_iota(jnp.int32, (t, t), 0)
    col = lax.broadcasted_iota(jnp.int32, (t, t), 1)
    return col <= row


def _attn_fwd(q, k, v):
    B, S, HW = q.shape
    H = HW // HEAD_LANES
    t = min(ATTN_FWD_Q_TILE, S)
    tk = min(ATTN_FWD_KV_TILE, t)
    nq = S // t
    per = t // tk

    pair = 4
    pw = pair * HEAD_LANES

    def body(q_ref, k_ref, v_ref, o_ref, lse_ref):
        qi = pl.program_id(2)
        lanes = [slice(hh * HEAD_LANES, (hh + 1) * HEAD_LANES) for hh in range(pair)]
        qs = [q_ref[0, :, cols] for cols in lanes]

        def step(j, carry, diag):
            sl = pl.ds(pl.multiple_of(j * tk, tk), tk)
            out = []
            for qv, cols, (m, l, acc) in zip(qs, lanes, carry):
                s = _dot_nt(qv, k_ref[0, sl, cols])
                if diag is not None:
                    row = lax.broadcasted_iota(jnp.int32, (t, tk), 0)
                    col = lax.broadcasted_iota(jnp.int32, (t, tk), 1)
                    s = jnp.where(col + diag * tk <= row, s, -1e30)
                m_new = jnp.maximum(m, jnp.max(s, axis=-1, keepdims=True))
                alpha = jnp.exp(m - m_new)
                p = jnp.exp(s - m_new)
                l = alpha * l + jnp.sum(p, axis=-1, keepdims=True)
                acc = alpha * acc + _dot(p.astype(BF16), v_ref[0, sl, cols])
                out.append((m_new, l, acc))
            return tuple(out)

        init = tuple((jnp.full((t, 1), -1e30, F32), jnp.zeros((t, 1), F32), jnp.zeros((t, HEAD_LANES), F32))
                     for _ in range(pair))
        carry = lax.fori_loop(0, qi * per, lambda j, c: step(j, c, None), init)
        for d in range(per):
            carry = step(qi * per + d, carry, d)
        for hh, (m, l, acc) in enumerate(carry):
            o_ref[0, :, lanes[hh]] = (acc / l).astype(BF16)
            lse_ref[0, hh] = m + jnp.log(l)

    return pl.pallas_call(
        body, grid=(B, H // pair, nq), name="attn_fwd",
        in_specs=[pl.BlockSpec((1, t, pw), lambda b, h, i: (b, i, h)),
                  pl.BlockSpec((1, S, pw), lambda b, h, i: (b, 0, h)),
                  pl.BlockSpec((1, S, pw), lambda b, h, i: (b, 0, h))],
        out_specs=[pl.BlockSpec((1, t, pw), lambda b, h, i: (b, i, h)),
                   pl.BlockSpec((1, pair, t, 1), lambda b, h, i: (b, h, i, 0))],
        out_shape=[jax.ShapeDtypeStruct((B, S, HW), BF16), jax.ShapeDtypeStruct((B, H, S, 1), F32)],
        compiler_params=_cparams(("arbitrary", "arbitrary", "arbitrary")),
    )(q, k, v)


def _attn_bwd(q, k, v, o, do, lse):
    B, S, HW = q.shape
    H = HW // HEAD_LANES
    t = min(ATTN_BWD_TILE, S)
    nq = S // t

    pair = 2
    pw = pair * HEAD_LANES

    def body(q_ref, k_ref, v_ref, o_ref, do_ref, lse_ref, dq_out_ref, dk_ref, dv_ref, dq_ref):
        j = pl.program_id(2)

        @pl.when(j == 0)
        def _():
            dq_ref[...] = jnp.zeros_like(dq_ref)

        lanes = [slice(hh * HEAD_LANES, (hh + 1) * HEAD_LANES) for hh in range(pair)]

        def step(i, carry, masked):
            sl = pl.ds(pl.multiple_of(i * t, t), t)
            out = []
            for hh, (cols, (dk, dv)) in enumerate(zip(lanes, carry)):
                kj = k_ref[0, :, cols]
                qi = q_ref[0, sl, cols]
                doi = do_ref[0, sl, cols]
                s = _dot_nt(qi, kj)
                if masked:
                    s = jnp.where(_causal_mask(t), s, -1e30)
                p = jnp.exp(s - lse_ref[0, hh, sl, :])
                dv = dv + _dot_tn(p.astype(BF16), doi)
                dp = _dot_nt(doi, v_ref[0, :, cols])
                delta = jnp.sum(doi.astype(F32) * o_ref[0, sl, cols].astype(F32), axis=-1, keepdims=True)
                dsb = (p * (dp - delta)).astype(BF16)
                dk = dk + _dot_tn(dsb, qi)
                dq_ref[sl, cols] += _dot(dsb, kj)
                out.append((dk, dv))
            return tuple(out)

        zero = jnp.zeros((t, HEAD_LANES), F32)
        carry = step(j, ((zero, zero),) * pair, True)
        carry = lax.fori_loop(j + 1, nq, lambda i, c: step(i, c, False), carry)
        for cols, (dk, dv) in zip(lanes, carry):
            dk_ref[0, :, cols] = dk.astype(BF16)
            dv_ref[0, :, cols] = dv.astype(BF16)

        @pl.when(j == nq - 1)
        def _():
            dq_out_ref[0] = dq_ref[...].astype(BF16)

    full = pl.BlockSpec((1, S, pw), lambda b, h, j: (b, 0, h))
    tile = pl.BlockSpec((1, t, pw), lambda b, h, j: (b, j, h))
    sd = jax.ShapeDtypeStruct
    return pl.pallas_call(
        body, grid=(B, H // pair, nq), name="attn_bwd",
        in_specs=[full, tile, tile, full, full, pl.BlockSpec((1, pair, S, 1), lambda b, h, j: (b, h, 0, 0))],
        out_specs=[full, tile, tile],
        out_shape=[sd((B, S, HW), BF16), sd((B, S, HW), BF16), sd((B, S, HW), BF16)],
        scratch_shapes=[pltpu.VMEM((S, pw), F32)],
        compiler_params=_cparams(("arbitrary", "arbitrary", "arbitrary")),
    )(q, k, v, o, do, lse)


def _mix_out(x1, yssd, o, mw, wout, g, seq):
    T, D = x1.shape
    tm = min(WIDE_TOKEN_TILE, seq)
    tps = seq // tm

    def body(x_ref, ys_ref, o_ref, mw_ref, w_ref, g_ref, xo_ref, m_ref, yc_ref):
        xh, _ = _rms_stats(o_ref[...].astype(F32))
        ycat = jnp.concatenate([ys_ref[...], (xh * mw_ref[...]).astype(BF16)], axis=1)
        m = _dot(ycat, w_ref[...])
        xo_ref[...] = x_ref[...] + g_ref[0] * m
        m_ref[...] = m.astype(BF16)
        yc_ref[...] = ycat

    rows = lambda n: pl.BlockSpec((tm, n), lambda i: (i, 0))
    perb = pl.BlockSpec((1, 1, D), lambda i: (i // tps, 0, 0))
    sd = jax.ShapeDtypeStruct
    return pl.pallas_call(
        body, grid=(T // tm,), name="mix_out",
        in_specs=[rows(D), rows(D_SSD), rows(D_MLA), _resident((1, D_MLA)), _resident((D_SSD + D_MLA, D)), perb],
        out_specs=[rows(D), rows(D), rows(D_SSD + D_MLA)],
        out_shape=[sd((T, D), F32), sd((T, D), BF16), sd((T, D_SSD + D_MLA), BF16)],
        compiler_params=_cparams(("arbitrary",)),
    )(x1, yssd, o, mw, wout, g)


def _mix_out_bwd(dx2, m, o, mw, wout, g, seq):
    T, D = dx2.shape
    B = T // seq
    tm = min(WIDE_TOKEN_TILE, seq)
    tps = seq // tm

    def body(dx_ref, m_ref, o_ref, mw_ref, w_ref, g_ref, dys_ref, do_ref, dm_ref, dg_ref, dmw_ref):
        i = pl.program_id(0)

        @pl.when(i % tps == 0)
        def _():
            dg_ref[...] = jnp.zeros_like(dg_ref)

        @pl.when(i == 0)
        def _():
            dmw_ref[...] = jnp.zeros_like(dmw_ref)

        dxv = dx_ref[...]
        dg_ref[0] += _colsum(dxv * m_ref[...].astype(F32))
        dmb = (g_ref[0] * dxv).astype(BF16)
        dm_ref[...] = dmb
        dycat = _dot_nt(dmb, w_ref[...])
        dys_ref[...] = dycat[:, :D_SSD].astype(BF16)
        xh, r = _rms_stats(o_ref[...].astype(F32))
        dx, dw_rows = _rms_bwd(dycat[:, D_SSD:], xh, r, mw_ref[...])
        do_ref[...] = dx.astype(BF16)
        dmw_ref[...] += _colsum(dw_rows)

    rows = lambda n: pl.BlockSpec((tm, n), lambda i: (i, 0))
    perb = pl.BlockSpec((1, 1, D), lambda i: (i // tps, 0, 0))
    sd = jax.ShapeDtypeStruct
    return pl.pallas_call(
        body, grid=(T // tm,), name="mix_out_bwd",
        in_specs=[rows(D), rows(D), rows(D_MLA), _resident((1, D_MLA)), _resident((D_SSD + D_MLA, D)), perb],
        out_specs=[rows(D_SSD), rows(D_MLA), rows(D), perb, pl.BlockSpec((1, D_MLA), lambda i: (0, 0))],
        out_shape=[sd((T, D_SSD), BF16), sd((T, D_MLA), BF16), sd((T, D), BF16), sd((B, 1, D), F32), sd((1, D_MLA), F32)],
        compiler_params=_cparams(("arbitrary",)),
    )(dx2, m, o, mw, wout, g)


def _win_to_kernel(w):
    z0 = jnp.zeros((48, w.shape[1]), w.dtype)
    z1 = jnp.zeros((32, w.shape[1]), w.dtype)
    return jnp.concatenate([w[:2560], w[2576:3216], w[2560:2576], z0, w[3216:3248], z1], axis=0)


def _win_from_kernel(g):
    return jnp.concatenate([g[:2560], g[3200:3216], g[2560:3200], g[3264:3296]], axis=0)


def _wuq_to_kernel(w):
    w = w.reshape(Q_LORA, MLA_HEADS, QK_DIM)
    return jnp.pad(w, ((0, 0), (0, 0), (0, HEAD_LANES - QK_DIM))).reshape(Q_LORA, MLA_HEADS * HEAD_LANES)


def _wuq_from_kernel(g):
    return g.reshape(Q_LORA, MLA_HEADS, HEAD_LANES)[:, :, :QK_DIM].reshape(Q_LORA, MLA_HEADS * QK_DIM)


def _wukv_to_kernel(w):
    w = w.reshape(KV_LORA, MLA_HEADS, QK_NOPE + V_HEAD)
    kp = jnp.pad(w[:, :, :QK_NOPE], ((0, 0), (0, 0), (0, HEAD_LANES - QK_NOPE)))
    return jnp.concatenate([kp.reshape(KV_LORA, -1), w[:, :, QK_NOPE:].reshape(KV_LORA, -1)], axis=1)


def _wukv_from_kernel(g):
    hw = MLA_HEADS * HEAD_LANES
    kp = g[:, :hw].reshape(KV_LORA, MLA_HEADS, HEAD_LANES)[:, :, :QK_NOPE]
    vp = g[:, hw:].reshape(KV_LORA, MLA_HEADS, V_HEAD)
    return jnp.concatenate([kp, vp], axis=2).reshape(KV_LORA, MLA_HEADS * (QK_NOPE + V_HEAD))


def _lanes16(v):
    return jnp.pad(v.reshape(1, SSD_HEADS), ((0, 0), (0, LANES - SSD_HEADS)))


def _constants():
    e = np.zeros((LANES, D_SSD), np.float32)
    for h in range(SSD_HEADS):
        e[h, h * SSD_HEAD_DIM:(h + 1) * SSD_HEAD_DIM] = 1.0
    inv_freq = ROPE_THETA ** (-jnp.arange(0, QK_ROPE, 2, dtype=F32) / QK_ROPE)
    half = QK_ROPE // 2
    place = np.zeros((half, LANES), np.float32)
    for j in range(half):
        place[j, QK_NOPE + j] = place[j, QK_NOPE + half + j] = 1.0
    return jnp.asarray(e, BF16), inv_freq.reshape(half, 1), jnp.asarray(place, BF16)


def _local_step(x, positions, mod, w, later_weights, small, tgt, on_grads, sync, on_small):
    B, S, D = x.shape
    T = B * S
    expand, invf, place = _constants()
    x0 = x.reshape(T, D)
    pos = positions.reshape(T)
    mods = [mod[:, i * D:(i + 1) * D].reshape(B, 1, D) for i in range(N_MOD)]
    sh1, sc1, g1, sh2, sc2, g2, sh3, sc3, g3 = mods
    dtb, alog = _lanes16(small["dt_bias"]), _lanes16(small["a_log"])
    dsk = jnp.repeat(small["d_skip"].reshape(1, SSD_HEADS), SSD_HEAD_DIM, axis=1)

    x1, a1, u1, f1 = _ffn_fwd(x0, small["norm_ffn1"], sh1, sc1, g1, w["ffn1_w_gate"], w["ffn1_w_up"], w["ffn1_w_down"], S, "ffn1_fwd")
    w = {**w, **later_weights(f1)}
    z, xraw, cq, ckv, dtk = _inproj_fwd(x1, small["norm_mix"], sh2, sc2, w["w_in"], S)
    xraw3 = xraw.reshape(B, S, D_CONV)
    xbc = _conv_fwd(xraw3, small["conv_w"], small["conv_b"])
    dtk3, z3 = dtk.reshape(B, S, LANES), z.reshape(B, S, D_SSD)
    y, yssd, prev = _ssd_fwd(xbc, dtk3, z3, dtb, alog, dsk, small["ssd_norm_w"], expand)
    q, k, v, cos_t, sin_t = _mla_prep(cq, ckv, dtk, pos, small["q_norm_w"], small["kv_norm_w"], w["w_uq"], w["w_ukv"], invf,
                                      place)
    hw = MLA_HEADS * HEAD_LANES
    q3, k3, v3 = q.reshape(B, S, hw), k.reshape(B, S, hw), v.reshape(B, S, hw)
    o3, lse = _attn_fwd(q3, k3, v3)
    o = o3.reshape(T, hw)
    x2, m, ycat = _mix_out(x1, yssd.reshape(T, D_SSD), o, small["mla_norm_w"], w["w_out"], g2, S)
    dx3, a2, u2, f2, loss, d_norm_final = _ffn_fwd(
        x2, small["norm_ffn2"], sh3, sc3, g3, w["ffn2_w_gate"], w["ffn2_w_up"], w["ffn2_w_down"], S, "ffn2_fwd",
        head=(small["norm_final"].reshape(1, D), tgt.reshape(T, D)))

    gw, gs = {}, {}
    dx2, h3, s3, df3, da3, du3, dsh3, dsc3, dg3, gs["norm_ffn2"] = _ffn_bwd(
        dx3, x2, small["norm_ffn2"], sh3, sc3, g3, a2, u2, f2, w["ffn2_w_gate"], w["ffn2_w_up"], w["ffn2_w_down"], S, "ffn2_bwd")
    gw["ffn2_w_gate"], gw["ffn2_w_up"], gw["ffn2_w_down"] = _ffn_wgrad(h3, s3, df3, da3, du3, dsh3, "ffn2_wgrad")
    g2 = g2 + on_grads(("ffn2_w_gate", "ffn2_w_up", "ffn2_w_down"), gw)

    dys, do, dm, dg2, gs["mla_norm_w"] = _mix_out_bwd(dx2, m, o, small["mla_norm_w"], w["w_out"], g2, S)
    gw["w_out"] = _mm_tn(ycat, dm, 512, "dwout")

    dq3, dk3, dv3 = _attn_bwd(q3, k3, v3, o3, do.reshape(B, S, hw), lse)
    dcq, dckv, ddtk_b, qn, kvn, dqb, dkvb, gs["q_norm_w"], gs["kv_norm_w"] = _mla_prep_bwd(
        dq3.reshape(T, hw), dk3.reshape(T, hw), dv3.reshape(T, hw), cq, ckv, cos_t, sin_t, small["q_norm_w"] + sync(dq3),
        small["kv_norm_w"], w["w_uq"], w["w_ukv"])
    gw["w_uq"] = _mm_tn(qn, dqb, 512, "dwuq")
    gw["w_ukv"] = _mm_tn(kvn, dkvb, 1024, "dwukv")

    dxbc, ddtk_a, dz, gs["ssd_norm_w"], dvec = _ssd_bwd(
        xbc, dtk3, z3, y, prev, dys.reshape(B, S, D_SSD), dtb, alog, dsk, small["ssd_norm_w"], expand)
    gs["dt_bias"], gs["a_log"], gs["d_skip"] = dvec[0:1, :SSD_HEADS], dvec[1:2, :SSD_HEADS], dvec[2:3, :SSD_HEADS]
    dxraw, gs["conv_w"], gs["conv_b"] = _conv_bwd(dxbc, xraw3, small["conv_w"], small["conv_b"])
    dx1, h2, dproj, dsh2, dsc2, gs["norm_mix"] = _inproj_bwd(
        dx2, x1, small["norm_mix"], sh2, sc2, w["w_in"], dz.reshape(T, D_SSD), dxraw.reshape(T, D_CONV), dcq, dckv,
        ddtk_a.reshape(T, LANES), ddtk_b, S)
    gw["w_in"] = _mm_tn(dproj, h2, 512, "dwin")
    g1 = g1 + on_grads(("w_in", "w_uq", "w_ukv", "w_out"), gw)

    dx0, h1, s1, df1, da1, du1, dsh1, dsc1, dg1, gs["norm_ffn1"] = _ffn_bwd(
        dx1, x0, small["norm_ffn1"], sh1, sc1, g1, a1, u1, f1, w["ffn1_w_gate"], w["ffn1_w_up"], w["ffn1_w_down"], S, "ffn1_bwd")
    gs["norm_final"] = d_norm_final
    dmod = jnp.concatenate([t.reshape(B, D) for t in (dsh1, dsc1, dg1, dsh2, dsc2, dg2, dsh3, dsc3, dg3)], axis=1)
    small_started = on_small(loss, dmod, gs)
    gw["ffn1_w_gate"], gw["ffn1_w_up"], gw["ffn1_w_down"] = _ffn_wgrad(h1, s1, df1, da1, du1, dsh1 + sync(small_started),
                                                                       "ffn1_wgrad")
    return dx0.reshape(B, S, D), gw


HBM_SPEC = pl.BlockSpec(memory_space=pltpu.HBM)
VMEM_SPEC = pl.BlockSpec(memory_space=pltpu.VMEM)


def _place():
    return lax.axis_index("x"), lax.axis_index("y"), lax.axis_index("c")


def _other_chips(mx, my):
    return [(1 - mx, my), (mx, 1 - my), (1 - mx, 1 - my)]


def _remote(src, dst, send_sem, recv_sem, to):
    return pltpu.make_async_remote_copy(src_ref=src, dst_ref=dst, send_sem=send_sem, recv_sem=recv_sem,
                                        device_id=to, device_id_type=MESH)


def _all_gather_small(xa, name):
    r, n = xa.shape

    def body(x_ref, o_ref, token, send_sems, recv_sems):
        mx, my, mc = _place()
        me = 4 * mx + 2 * my + mc
        token[...] = jnp.zeros_like(token)
        o_ref[pl.ds(me, 1)] = x_ref[...][None]
        sends = []
        for k in range(1, N_DEV):
            peer = (mx ^ (k >> 2), my ^ ((k >> 1) & 1), mc ^ (k & 1))
            cp = _remote(x_ref, o_ref.at[me], send_sems.at[k - 1], recv_sems.at[k - 1], peer)
            cp.start()
            sends.append(cp)
        for k in range(1, N_DEV):
            peer = (mx ^ (k >> 2), my ^ ((k >> 1) & 1), mc ^ (k & 1))
            slot = 4 * peer[0] + 2 * peer[1] + peer[2]
            _remote(x_ref, o_ref.at[slot], send_sems.at[k - 1], recv_sems.at[k - 1], peer).wait_recv()
        for cp in sends:
            cp.wait_send()

    return pl.pallas_call(
        body, name=name, in_specs=[VMEM_SPEC], out_specs=[VMEM_SPEC, VMEM_SPEC],
        out_shape=[jax.ShapeDtypeStruct((N_DEV, r, n), xa.dtype), jax.ShapeDtypeStruct((8, LANES), F32)],
        scratch_shapes=[pltpu.SemaphoreType.DMA((N_DEV - 1,)), pltpu.SemaphoreType.DMA((N_DEV - 1,))],
        compiler_params=pltpu.CompilerParams(vmem_limit_bytes=VMEM_LIMIT),
    )(xa)


def _halves_by_rows(shape):
    return (shape[-2] // 2) % 16 == 0


def _half_shape(shape):
    r, c = shape[-2:]
    return tuple(shape[:-2]) + ((r // 2, c) if _halves_by_rows(shape) else (r, c // 2))


def _half_index(shape, hc):
    r, c = shape[-2:]
    if _halves_by_rows(shape):
        return (pl.ds(pl.multiple_of(hc * (r // 2), 16), r // 2), slice(None))
    return (slice(None), pl.ds(pl.multiple_of(hc * (c // 2), LANES), c // 2))


def _half(ref, hc, lead=None):
    idx = _half_index(ref.shape, hc)
    return ref.at[idx] if lead is None else ref.at[(lead,) + idx]


def _gather_weights(shards):
    n = len(shards)

    def body(*refs):
        w_refs, o_refs, token = refs[:n], refs[n:2 * n], refs[2 * n]
        send_sems, recv_sems, stage_sems = refs[2 * n + 1:2 * n + 4]
        stages = refs[2 * n + 4:]
        mx, my, mc = _place()
        chip = 2 * mx + my
        others = _other_chips(mx, my)
        sibling = (mx, my, 1 - mc)
        token[...] = jnp.zeros_like(token)
        stage_in = [pltpu.make_async_copy(w, st, stage_sems.at[0, i]) for i, (w, st) in enumerate(zip(w_refs, stages))]
        for cp in stage_in:
            cp.start()
        first = []
        for i, (w, o) in enumerate(zip(w_refs, o_refs)):
            for k, (cx, cy) in enumerate(others):
                first.append(_remote(_half(w, mc), _half(o, mc, chip), send_sems.at[i, k],
                                     recv_sems.at[i, k], (cx, cy, mc)))
                first[-1].start()
        stage_out = []
        for i, (st, o) in enumerate(zip(stages, o_refs)):
            stage_in[i].wait()
            stage_out.append(pltpu.make_async_copy(st, o.at[chip], stage_sems.at[1, i]))
            stage_out[-1].start()
        passed = []
        for i, (w, o) in enumerate(zip(w_refs, o_refs)):
            for k, (cx, cy) in enumerate(others):
                landed = _half(o, mc, 2 * cx + cy)
                _remote(landed, landed, send_sems.at[i, k], recv_sems.at[i, k], (cx, cy, mc)).wait_recv()
                passed.append(_remote(landed, landed, send_sems.at[i, 3 + k], recv_sems.at[i, 3 + k], sibling))
                passed[-1].start()
        for i, (w, o) in enumerate(zip(w_refs, o_refs)):
            for k, (cx, cy) in enumerate(others):
                there = _half(o, 1 - mc, 2 * cx + cy)
                _remote(there, there, send_sems.at[i, 3 + k], recv_sems.at[i, 3 + k], sibling).wait_recv()
        for cp in first + passed:
            cp.wait_send()
        for cp in stage_out:
            cp.wait()

    out = pl.pallas_call(
        body, name="gather_weights", in_specs=[HBM_SPEC] * n, out_specs=[HBM_SPEC] * n + [VMEM_SPEC],
        out_shape=[jax.ShapeDtypeStruct((N_CHIPS,) + s.shape, s.dtype) for s in shards] + [jax.ShapeDtypeStruct((8, LANES), F32)],
        scratch_shapes=[pltpu.SemaphoreType.DMA((n, 6)), pltpu.SemaphoreType.DMA((n, 6)), pltpu.SemaphoreType.DMA((2, n))]
        + [pltpu.VMEM(s.shape, s.dtype) for s in shards],
        compiler_params=pltpu.CompilerParams(vmem_limit_bytes=VMEM_LIMIT),
    )(*shards)
    return out[:n], out[n]


SEM_SPEC = pl.BlockSpec(memory_space=pltpu.SEMAPHORE)
ANY_SPEC = pl.BlockSpec(memory_space=pl.ANY)
DATAFLOW = pltpu.SideEffectType.DATAFLOW_SIDE_EFFECTING


def _hbm(arr):
    return pltpu.with_memory_space_constraint(arr, pltpu.HBM)


def _gather_start(shards):
    n = len(shards)

    def body(*refs):
        w_refs, land_refs, send_sems, recv_sems, token = refs[:n], refs[n:2 * n], refs[2 * n], refs[2 * n + 1], refs[-1]
        mx, my, mc = _place()
        chip = 2 * mx + my
        for i, (w, land) in enumerate(zip(w_refs, land_refs)):
            for k, (cx, cy) in enumerate(_other_chips(mx, my)):
                _remote(_half(w, mc), _half(land, mc, chip), send_sems.at[3 * i + k],
                        recv_sems.at[3 * i + k], (cx, cy, mc)).start()
        token[...] = jnp.zeros_like(token)

    lands = [lax.empty((N_CHIPS,) + s.shape, s.dtype) for s in shards]
    out = pl.pallas_call(
        body, name="gather_start",
        out_shape=(pltpu.SemaphoreType.DMA((3 * n,)), pltpu.SemaphoreType.DMA((3 * n,)),
                   *[pltpu.HBM(s.shape, s.dtype) for s in shards], *[pltpu.HBM(l.shape, l.dtype) for l in lands],
                   jax.ShapeDtypeStruct((8, LANES), F32)),
        in_specs=[HBM_SPEC] * (2 * n), out_specs=(SEM_SPEC, SEM_SPEC, *[HBM_SPEC] * (2 * n), VMEM_SPEC),
        input_output_aliases={i: 2 + i for i in range(2 * n)},
        compiler_params=pltpu.CompilerParams(has_side_effects=DATAFLOW),
    )(*[_hbm(s) for s in shards], *[_hbm(l) for l in lands])
    return out[0], out[1], out[2:2 + n], out[2 + n:2 + 2 * n], out[-1]


def _gather_wait(send_sems, recv_sems, shards, lands, after):
    n = len(shards)

    def body(*refs):
        w_refs, land_refs, send_sems, recv_sems = refs[:n], refs[n:2 * n], refs[2 * n], refs[2 * n + 1]
        mx, my, mc = _place()
        for i, (w, land) in enumerate(zip(w_refs, land_refs)):
            for k, (cx, cy) in enumerate(_other_chips(mx, my)):
                cp = _remote(_half(w, mc), _half(land, mc, 2 * cx + cy), send_sems.at[3 * i + k],
                             recv_sems.at[3 * i + k], (cx, cy, mc))
                cp.wait_send()
                cp.wait_recv()

    out = pl.pallas_call(
        body, name="gather_wait",
        out_shape=(*[pltpu.HBM(s.shape, s.dtype) for s in shards], *[pltpu.HBM(l.shape, l.dtype) for l in lands]),
        in_specs=[HBM_SPEC] * (2 * n) + [SEM_SPEC, SEM_SPEC, ANY_SPEC], out_specs=tuple([HBM_SPEC] * (2 * n)),
        input_output_aliases={i: i for i in range(2 * n)},
        compiler_params=pltpu.CompilerParams(has_side_effects=DATAFLOW),
    )(*shards, *lands, send_sems, recv_sems, after)
    return out[n:]


def _gather_finish(shards, lands):
    n = len(shards)

    def body(*refs):
        w_refs, land_refs, o_refs = refs[:n], refs[n:2 * n], refs[2 * n:3 * n]
        send_sems, recv_sems, stage_sems = refs[3 * n:3 * n + 3]
        stages = refs[3 * n + 3:]
        mx, my, mc = _place()
        chip = 2 * mx + my
        others = _other_chips(mx, my)
        sibling = (mx, my, 1 - mc)
        stage_in = [pltpu.make_async_copy(w, st, stage_sems.at[0, i]) for i, (w, st) in enumerate(zip(w_refs, stages))]
        for cp in stage_in:
            cp.start()
        passed = []
        for i, (w, o) in enumerate(zip(w_refs, o_refs)):
            for k, (cx, cy) in enumerate(others):
                landed = _half(o, mc, 2 * cx + cy)
                passed.append(_remote(landed, landed, send_sems.at[i, k], recv_sems.at[i, k], sibling))
                passed[-1].start()
        stage_out = []
        for i, (st, o) in enumerate(zip(stages, o_refs)):
            stage_in[i].wait()
            stage_out.append(pltpu.make_async_copy(st, o.at[chip], stage_sems.at[1, i]))
            stage_out[-1].start()
        for i, (w, o) in enumerate(zip(w_refs, o_refs)):
            for k, (cx, cy) in enumerate(others):
                there = _half(o, 1 - mc, 2 * cx + cy)
                _remote(there, there, send_sems.at[i, k], recv_sems.at[i, k], sibling).wait_recv()
        for cp in passed:
            cp.wait_send()
        for cp in stage_out:
            cp.wait()

    return pl.pallas_call(
        body, name="gather_finish", in_specs=[HBM_SPEC] * (2 * n), out_specs=[HBM_SPEC] * n,
        out_shape=[jax.ShapeDtypeStruct(l.shape, l.dtype) for l in lands],
        input_output_aliases={n + i: i for i in range(n)},
        scratch_shapes=[pltpu.SemaphoreType.DMA((n, 3)), pltpu.SemaphoreType.DMA((n, 3)), pltpu.SemaphoreType.DMA((2, n))]
        + [pltpu.VMEM(s.shape, s.dtype) for s in shards],
        compiler_params=pltpu.CompilerParams(vmem_limit_bytes=VMEM_LIMIT),
    )(*shards, *lands)


def _scatter_start(ss, tag):
    n = len(ss)

    def body(*refs):
        s_refs, land_refs, send_sems, recv_sems, token = refs[:n], refs[n:2 * n], refs[2 * n], refs[2 * n + 1], refs[-1]
        mx, my, mc = _place()
        chip = 2 * mx + my
        for i, (s, land) in enumerate(zip(s_refs, land_refs)):
            for k, (cx, cy) in enumerate(_other_chips(mx, my)):
                _remote(s.at[2 * cx + cy], land.at[chip], send_sems.at[3 * i + k], recv_sems.at[3 * i + k],
                        (cx, cy, mc)).start()
        token[...] = jnp.zeros_like(token)

    lands = [lax.empty(s.shape, s.dtype) for s in ss]
    out = pl.pallas_call(
        body, name="scatter_start_" + tag,
        out_shape=(pltpu.SemaphoreType.DMA((3 * n,)), pltpu.SemaphoreType.DMA((3 * n,)),
                   *[pltpu.HBM(s.shape, s.dtype) for s in ss], *[pltpu.HBM(l.shape, l.dtype) for l in lands],
                   jax.ShapeDtypeStruct((8, LANES), F32)),
        in_specs=[HBM_SPEC] * (2 * n), out_specs=(SEM_SPEC, SEM_SPEC, *[HBM_SPEC] * (2 * n), VMEM_SPEC),
        input_output_aliases={i: 2 + i for i in range(2 * n)},
        compiler_params=pltpu.CompilerParams(has_side_effects=DATAFLOW),
    )(*[_hbm(s) for s in ss], *[_hbm(l) for l in lands])
    return out[0], out[1], out[2:2 + n], out[2 + n:2 + 2 * n], out[-1]


def _scatter_wait(send_sems, recv_sems, ss, lands, after, tag):
    n = len(ss)

    def body(*refs):
        s_refs, land_refs, send_sems, recv_sems = refs[:n], refs[n:2 * n], refs[2 * n], refs[2 * n + 1]
        mx, my, mc = _place()
        for i, (s, land) in enumerate(zip(s_refs, land_refs)):
            for k, (cx, cy) in enumerate(_other_chips(mx, my)):
                slot = land.at[2 * cx + cy]
                cp = _remote(s.at[2 * cx + cy], slot, send_sems.at[3 * i + k], recv_sems.at[3 * i + k], (cx, cy, mc))
                cp.wait_send()
                cp.wait_recv()

    out = pl.pallas_call(
        body, name="scatter_wait_" + tag,
        out_shape=(*[pltpu.HBM(s.shape, s.dtype) for s in ss], *[pltpu.HBM(l.shape, l.dtype) for l in lands]),
        in_specs=[HBM_SPEC] * (2 * n) + [SEM_SPEC, SEM_SPEC, ANY_SPEC], out_specs=tuple([HBM_SPEC] * (2 * n)),
        input_output_aliases={i: i for i in range(2 * n)},
        compiler_params=pltpu.CompilerParams(has_side_effects=DATAFLOW),
    )(*ss, *lands, send_sems, recv_sems, after)
    return out[:n], out[n:]


def _swap_start(gs, tag):
    n = len(gs)

    def body(*refs):
        g_refs, land_refs, send_sems, recv_sems, token = refs[:n], refs[n:2 * n], refs[2 * n], refs[2 * n + 1], refs[-1]
        mx, my, mc = _place()
        for i, (g, land) in enumerate(zip(g_refs, land_refs)):
            src = g.at[(slice(None),) + _half_index(g.shape, 1 - mc)]
            _remote(src, land, send_sems.at[i], recv_sems.at[i], (mx, my, 1 - mc)).start()
        token[...] = jnp.zeros_like(token)

    lands = [lax.empty(_half_shape(g.shape), g.dtype) for g in gs]
    out = pl.pallas_call(
        body, name="swap_start_" + tag,
        out_shape=(pltpu.SemaphoreType.DMA((n,)), pltpu.SemaphoreType.DMA((n,)), *[pltpu.HBM(g.shape, g.dtype) for g in gs],
                   *[pltpu.HBM(l.shape, l.dtype) for l in lands], jax.ShapeDtypeStruct((8, LANES), F32)),
        in_specs=[HBM_SPEC] * (2 * n), out_specs=(SEM_SPEC, SEM_SPEC, *[HBM_SPEC] * (2 * n), VMEM_SPEC),
        input_output_aliases={i: 2 + i for i in range(2 * n)},
        compiler_params=pltpu.CompilerParams(has_side_effects=DATAFLOW),
    )(*[_hbm(g) for g in gs], *[_hbm(l) for l in lands])
    return out[0], out[1], out[2:2 + n], out[2 + n:2 + 2 * n], out[-1]


def _swap_wait(send_sems, recv_sems, gs, lands, after, tag):
    n = len(gs)

    def body(*refs):
        g_refs, land_refs, send_sems, recv_sems = refs[:n], refs[n:2 * n], refs[2 * n], refs[2 * n + 1]
        mx, my, mc = _place()
        for i, (g, land) in enumerate(zip(g_refs, land_refs)):
            src = g.at[(slice(None),) + _half_index(g.shape, 1 - mc)]
            cp = _remote(src, land, send_sems.at[i], recv_sems.at[i], (mx, my, 1 - mc))
            cp.wait_send()
            cp.wait_recv()

    out = pl.pallas_call(
        body, name="swap_wait_" + tag,
        out_shape=(*[pltpu.HBM(g.shape, g.dtype) for g in gs], *[pltpu.HBM(l.shape, l.dtype) for l in lands]),
        in_specs=[HBM_SPEC] * (2 * n) + [SEM_SPEC, SEM_SPEC, ANY_SPEC], out_specs=tuple([HBM_SPEC] * (2 * n)),
        input_output_aliases={i: i for i in range(2 * n)},
        compiler_params=pltpu.CompilerParams(has_side_effects=DATAFLOW),
    )(*gs, *lands, send_sems, recv_sems, after)
    return out[:n], out[n:]


def _small_peers(mx, my, mc):
    peers = [(mx ^ (k >> 2), my ^ ((k >> 1) & 1), mc ^ (k & 1)) for k in range(1, N_DEV)]
    return [(peer, 4 * peer[0] + 2 * peer[1] + peer[2]) for peer in peers]


def _gather_small_start(xa):
    def body(x_ref, land_ref, send_sems, recv_sems, x_out, land_out, token):
        mx, my, mc = _place()
        me = 4 * mx + 2 * my + mc
        pltpu.make_async_copy(x_ref, land_ref.at[me], send_sems.at[N_DEV - 1]).start()
        for k, (peer, _) in enumerate(_small_peers(mx, my, mc)):
            _remote(x_ref, land_ref.at[me], send_sems.at[k], recv_sems.at[k], peer).start()
        token[...] = jnp.zeros_like(token)

    land = lax.empty((N_DEV,) + xa.shape, xa.dtype)
    return pl.pallas_call(
        body, name="gather_small_start",
        out_shape=(pltpu.SemaphoreType.DMA((N_DEV,)), pltpu.SemaphoreType.DMA((N_DEV,)), pltpu.HBM(xa.shape, xa.dtype),
                   pltpu.HBM(land.shape, land.dtype), jax.ShapeDtypeStruct((8, LANES), F32)),
        in_specs=[HBM_SPEC] * 2, out_specs=(SEM_SPEC, SEM_SPEC, HBM_SPEC, HBM_SPEC, VMEM_SPEC),
        input_output_aliases={0: 2, 1: 3},
        compiler_params=pltpu.CompilerParams(has_side_effects=DATAFLOW),
    )(_hbm(xa), _hbm(land))


def _gather_small_wait(send_sems, recv_sems, xa, land, after):
    def body(x_ref, land_ref, send_sems, recv_sems, after_ref, x_out, land_out):
        mx, my, mc = _place()
        me = 4 * mx + 2 * my + mc
        pltpu.make_async_copy(x_ref, land_ref.at[me], send_sems.at[N_DEV - 1]).wait()
        for k, (peer, slot) in enumerate(_small_peers(mx, my, mc)):
            cp = _remote(x_ref, land_ref.at[slot], send_sems.at[k], recv_sems.at[k], peer)
            cp.wait_send()
            cp.wait_recv()

    out = pl.pallas_call(
        body, name="gather_small_wait",
        out_shape=(pltpu.HBM(xa.shape, xa.dtype), pltpu.HBM(land.shape, land.dtype)),
        in_specs=[HBM_SPEC] * 2 + [SEM_SPEC, SEM_SPEC, ANY_SPEC], out_specs=(HBM_SPEC, HBM_SPEC),
        input_output_aliases={0: 0, 1: 1},
        compiler_params=pltpu.CompilerParams(has_side_effects=DATAFLOW),
    )(xa, land, send_sems, recv_sems, after)
    return out[1]


def _pair_sums(gs, gots, name):
    n = len(gs)

    def body(*refs):
        g_refs, got_refs, o_refs, load_sems, store_sems = refs[:n], refs[n:2 * n], refs[2 * n:3 * n], refs[3 * n], refs[3 * n + 1]
        mine, theirs, sums = refs[3 * n + 2:4 * n + 2], refs[4 * n + 2:5 * n + 2], refs[5 * n + 2:]
        mc = lax.axis_index("c")
        loads = []
        for i, (g, got) in enumerate(zip(g_refs, got_refs)):
            loads.append((pltpu.make_async_copy(g.at[(slice(None),) + _half_index(g.shape, mc)], mine[i], load_sems.at[i, 0]),
                          pltpu.make_async_copy(got, theirs[i], load_sems.at[i, 1])))
            for cp in loads[-1]:
                cp.start()
        stores = []
        for i in range(n):
            for cp in loads[i]:
                cp.wait()
            sums[i][...] = (mine[i][...].astype(F32) + theirs[i][...].astype(F32)).astype(BF16)
            stores.append(pltpu.make_async_copy(sums[i], o_refs[i], store_sems.at[i]))
            stores[-1].start()
        for cp in stores:
            cp.wait()

    halves = [jax.ShapeDtypeStruct(got.shape, BF16) for got in gots]
    return pl.pallas_call(
        body, name=name, in_specs=[HBM_SPEC] * (2 * n), out_specs=[HBM_SPEC] * n, out_shape=halves,
        scratch_shapes=[pltpu.SemaphoreType.DMA((n, 2)), pltpu.SemaphoreType.DMA((n,))]
        + [pltpu.VMEM(got.shape, g.dtype) for g, got in zip(gs, gots)] + [pltpu.VMEM(got.shape, got.dtype) for got in gots]
        + [pltpu.VMEM(got.shape, BF16) for got in gots],
        compiler_params=pltpu.CompilerParams(vmem_limit_bytes=VMEM_LIMIT),
    )(*gs, *gots)


def _reduce_join(owns, gots, name):
    n = len(owns)

    def body(*refs):
        own_refs, got_refs, mine_refs, theirs_refs = refs[:n], refs[n:2 * n], refs[2 * n:3 * n], refs[3 * n:4 * n]
        send_sems, recv_sems, load_sems, store_sems = refs[4 * n:4 * n + 4]
        parts, sums = refs[4 * n + 4:5 * n + 4], refs[5 * n + 4:]
        mx, my, mc = _place()
        chip = 2 * mx + my
        loads = []
        for i, (own, got, part) in enumerate(zip(own_refs, got_refs, parts)):
            loads.append([pltpu.make_async_copy((own if k == 0 else got).at[chip ^ k], part.at[k], load_sems.at[i, k])
                          for k in range(N_CHIPS)])
            for cp in loads[-1]:
                cp.start()
        out = []
        for i, (part, total, mine, theirs) in enumerate(zip(parts, sums, mine_refs, theirs_refs)):
            for cp in loads[i]:
                cp.wait()
            total[...] = ((part[0].astype(F32) + part[1].astype(F32)) + part[2].astype(F32)) + part[3].astype(F32)
            out.append(pltpu.make_async_copy(total, mine, store_sems.at[i]))
            out.append(_remote(total, theirs, send_sems.at[i], recv_sems.at[i], (mx, my, 1 - mc)))
            out[-2].start()
            out[-1].start()
        for cp in out:
            cp.wait()

    halves = [jax.ShapeDtypeStruct(o.shape[1:], F32) for o in owns]
    out = pl.pallas_call(
        body, name=name, in_specs=[HBM_SPEC] * (2 * n), out_specs=[HBM_SPEC] * (2 * n), out_shape=halves + halves,
        scratch_shapes=[pltpu.SemaphoreType.DMA((n,)), pltpu.SemaphoreType.DMA((n,)), pltpu.SemaphoreType.DMA((n, N_CHIPS)),
                        pltpu.SemaphoreType.DMA((n,))]
        + [pltpu.VMEM(o.shape, o.dtype) for o in owns] + [pltpu.VMEM(o.shape[1:], F32) for o in owns],
        compiler_params=pltpu.CompilerParams(vmem_limit_bytes=VMEM_LIMIT),
    )(*owns, *gots)
    return out[:n], out[n:]


def _adam_math(w, g, m, v):
    m2 = ADAM_B1 * m + (1.0 - ADAM_B1) * g
    v2 = ADAM_B2 * v + (1.0 - ADAM_B2) * (g * g)
    m_hat = m2 * (1.0 / (1.0 - ADAM_B1 ** ADAM_STEP))
    v_hat = v2 * (1.0 / (1.0 - ADAM_B2 ** ADAM_STEP))
    delta = -ADAM_LR * (m_hat / (jnp.sqrt(v_hat) + ADAM_EPS) + ADAM_WD * w)
    return delta, m2, v2


def _adam(w, g, m, v, name):
    def body(w_ref, g_ref, m_ref, v_ref, d_ref, m2_ref, v2_ref):
        d_ref[...], m2_ref[...], v2_ref[...] = _adam_math(w_ref[...], g_ref[...], m_ref[...], v_ref[...])

    return pl.pallas_call(body, name=name, out_shape=[jax.ShapeDtypeStruct(w.shape, F32)] * 3)(w, g, m, v)


def _adam_halves(w, m, v, mine, theirs, core, name):
    hr, hcols = _half_shape(w.shape)[1:]
    by_rows = _halves_by_rows(w.shape)

    def body(core_ref, w_ref, m_ref, v_ref, mine_ref, theirs_ref, g_ref, d_ref, m2_ref, v2_ref):
        g = jnp.where(pl.program_id(0) == core_ref[0], mine_ref[...], theirs_ref[...])
        g_ref[0] = g
        d_ref[0], m2_ref[0], v2_ref[0] = _adam_math(w_ref[0], g, m_ref[0], v_ref[0])

    half = pl.BlockSpec((1, hr, hcols), lambda hc, core_ref: (0, hc, 0) if by_rows else (0, 0, hc))
    whole = pl.BlockSpec((hr, hcols), lambda hc, core_ref: (0, 0))
    return pl.pallas_call(
        body, name=name,
        grid_spec=pltpu.PrefetchScalarGridSpec(
            num_scalar_prefetch=1, grid=(2,), in_specs=[half, half, half, whole, whole], out_specs=[half] * 4),
        out_shape=[jax.ShapeDtypeStruct(w.shape, F32)] * 4,
        compiler_params=_cparams(("arbitrary",)),
    )(core, w, m, v, mine, theirs)


ADA_COLS = N_MOD * D_MODEL // N_CHIPS


def _ada_fwd(c_all, w_ada, b_cols):
    def body(c_ref, w_ref, b_ref, o_ref):
        cv = c_ref[...]
        act = (cv * _sigmoid(cv)).astype(BF16)
        o_ref[...] = _dot(act, w_ref[...].astype(BF16)) + b_ref[...]

    return pl.pallas_call(
        body, name="ada_fwd", out_shape=jax.ShapeDtypeStruct((c_all.shape[0], ADA_COLS), F32),
        compiler_params=pltpu.CompilerParams(vmem_limit_bytes=VMEM_LIMIT),
    )(c_all, w_ada, b_cols)


def _ada_bwd(c_all, dmod_cols, w, m, v):
    nb = c_all.shape[0]
    tn = 384

    def body(c_ref, d_ref, w_ref, m_ref, v_ref, g_ref, dl_ref, m2_ref, v2_ref):
        cv = c_ref[...]
        act = (cv * _sigmoid(cv)).astype(BF16)
        g = _dot_tn(act, d_ref[...].astype(BF16))
        g_ref[...] = g
        dl_ref[...], m2_ref[...], v2_ref[...] = _adam_math(w_ref[...], g, m_ref[...], v_ref[...])

    blk = pl.BlockSpec((D_MODEL, tn), lambda j: (0, j))
    return pl.pallas_call(
        body, name="ada_bwd", grid=(ADA_COLS // tn,),
        in_specs=[pl.BlockSpec((nb, D_MODEL), lambda j: (0, 0)), pl.BlockSpec((nb, tn), lambda j: (0, j)), blk, blk, blk],
        out_specs=[blk] * 4, out_shape=[jax.ShapeDtypeStruct((D_MODEL, ADA_COLS), F32)] * 4,
        compiler_params=_cparams(("arbitrary",)),
    )(c_all, dmod_cols, w, m, v)


SMALL_NAMES = ("norm_ffn1", "norm_mix", "conv_w", "conv_b", "ssd_norm_w", "q_norm_w", "kv_norm_w", "mla_norm_w",
               "norm_ffn2", "norm_final", "dt_bias", "a_log", "d_skip")
SMALL_SIZES = (1024, 1024, CONV_WIDTH * D_CONV, D_CONV, 1024, Q_LORA, KV_LORA, 1024, 1024, 1024, 16, 16, 16)
SMALL_ROWS = 16
MOD_ROWS = 2 * N_MOD
SEND_ROWS = 40


def _pack_small(parts):
    flat = jnp.concatenate([parts[n].reshape(-1) for n in SMALL_NAMES])
    return jnp.pad(flat, (0, SMALL_ROWS * D_MODEL - flat.shape[0]))


def _unpack_small(flat):
    out, off = {}, 0
    for n, size in zip(SMALL_NAMES, SMALL_SIZES):
        out[n] = flat[off:off + size]
        off += size
    return out


def _small_sum(got):
    def body(g_ref, o_ref):
        bsum = jnp.zeros((N_MOD, D_MODEL), F32)
        ssum = jnp.zeros((SMALL_ROWS, D_MODEL), F32)
        for d in range(N_DEV):
            bsum = bsum + g_ref[d, 0:N_MOD, :] + g_ref[d, N_MOD:MOD_ROWS, :]
            ssum = ssum + g_ref[d, MOD_ROWS:MOD_ROWS + SMALL_ROWS, :]
        o_ref[...] = jnp.concatenate([bsum, ssum, jnp.zeros((32 - N_MOD - SMALL_ROWS, D_MODEL), F32)], axis=0)

    return pl.pallas_call(body, name="small_sum", out_shape=jax.ShapeDtypeStruct((32, D_MODEL), F32))(got)


BIG_NAMES = ("ffn1_w_gate", "ffn1_w_up", "ffn1_w_down", "w_in", "w_uq", "w_ukv", "w_out", "ffn2_w_gate", "ffn2_w_up",
             "ffn2_w_down")
_TO_KERNEL = {"w_in": _win_to_kernel, "w_uq": _wuq_to_kernel, "w_ukv": _wukv_to_kernel}
_FROM_KERNEL = {"w_in": _win_from_kernel, "w_uq": _wuq_from_kernel, "w_ukv": _wukv_from_kernel}


def _columns_joined(w4):
    n, r, c = w4.shape
    return w4.transpose(1, 0, 2).reshape(r, n * c)


def _columns_split(g):
    r, cols = g.shape
    return g.reshape(r, N_CHIPS, cols // N_CHIPS).transpose(1, 0, 2)


def kernel(x, c, positions, w_ada, b_ada, norm_ffn1, ffn1_w_gate, ffn1_w_up, ffn1_w_down, norm_mix, w_in, conv_w, conv_b, dt_bias, a_log, d_skip, ssd_norm_w, q_norm_w, w_uq, kv_norm_w, w_ukv, mla_norm_w, w_out, norm_ffn2, ffn2_w_gate, ffn2_w_up, ffn2_w_down, norm_final, loss_target, m_w_ada, m_b_ada, m_norm_ffn1, m_ffn1_w_gate, m_ffn1_w_up, m_ffn1_w_down, m_norm_mix, m_w_in, m_conv_w, m_conv_b, m_dt_bias, m_a_log, m_d_skip, m_ssd_norm_w, m_q_norm_w, m_w_uq, m_kv_norm_w, m_w_ukv, m_mla_norm_w, m_w_out, m_norm_ffn2, m_ffn2_w_gate, m_ffn2_w_up, m_ffn2_w_down, m_norm_final, v_w_ada, v_b_ada, v_norm_ffn1, v_ffn1_w_gate, v_ffn1_w_up, v_ffn1_w_down, v_norm_mix, v_w_in, v_conv_w, v_conv_b, v_dt_bias, v_a_log, v_d_skip, v_ssd_norm_w, v_q_norm_w, v_w_uq, v_kv_norm_w, v_w_ukv, v_mla_norm_w, v_w_out, v_norm_ffn2, v_ffn2_w_gate, v_ffn2_w_up, v_ffn2_w_down, v_norm_final):
    a = dict(locals())
    held_transposed = ("ffn1_w_gate", "ffn1_w_up", "ffn2_w_gate", "ffn2_w_up", "w_in")
    for n in held_transposed:
        for p in ("", "m_", "v_"):
            a[p + n] = a[p + n].transpose(0, 2, 1)
    B, S, D = x.shape
    mx, my, mc = _place()
    chip = 2 * mx + my
    dev = 2 * chip + mc
    core = mc.astype(jnp.int32).reshape(1)

    cw_rows = jnp.pad(conv_w[0], ((0, 0), (0, D - conv_w.shape[2])))
    got, _ = _all_gather_small(jnp.concatenate([c, cw_rows, jnp.zeros((8 - B - CONV_WIDTH, D), F32)], axis=0), "gather_c")
    c_all = got[:, :B, :].reshape(N_DEV * B, D)
    conv_full = got[::2, B:B + CONV_WIDTH, :conv_w.shape[2]].transpose(1, 0, 2).reshape(CONV_WIDTH, D_CONV)

    b_cols = lax.dynamic_slice(b_ada, (0, chip * ADA_COLS), (1, ADA_COLS))
    mod_all, mod_done = _all_gather_small(_ada_fwd(c_all, w_ada[0], b_cols), "gather_mod")
    mod = lax.dynamic_slice(mod_all, (0, B * dev, 0), (N_DEV, B, ADA_COLS))[::2].transpose(1, 0, 2).reshape(B, N_MOD * D)

    first = ("ffn1_w_gate", "ffn1_w_up", "ffn1_w_down")
    later = tuple(n for n in BIG_NAMES if n not in first)
    got_first, gathered = _gather_weights([(a[n][0] + mod_done[0, 0]).astype(BF16) for n in first])
    w = dict(zip(first, got_first))
    in_flight = _gather_start([(a[n][0] + gathered[0, 0]).astype(BF16) for n in later])

    def later_weights(after):
        send_sems, recv_sems, shards, lands, _ = in_flight
        lands = _gather_wait(send_sems, recv_sems, shards, lands, after)
        wl = dict(zip(later, _gather_finish([a[n][0].astype(BF16) for n in later], lands)))
        for n, to_kernel in _TO_KERNEL.items():
            wl[n] = to_kernel(wl[n].reshape(-1, D) if n in held_transposed else _columns_joined(wl[n]))
        wl["w_out"] = wl["w_out"].reshape(D_SSD + D_MLA, D)
        return wl

    small = {n: a[n].reshape(1, -1) for n in SMALL_NAMES if n not in ("conv_w", "norm_final")}
    small["conv_w"], small["norm_final"] = conv_full, norm_final

    def shards_of(names, gw):
        g4 = []
        for n in names:
            g = gw[n]
            if n in _FROM_KERNEL:
                g = _FROM_KERNEL[n](g) if n in held_transposed else _columns_split(_FROM_KERNEL[n](g))
            g4.append(g.reshape(N_CHIPS, a[n].shape[1], a[n].shape[2]))
        return g4

    def scatter_group(names, g4, swapped):
        pair = _pair_sums(g4, swapped, "pair_sums_" + names[0])
        return (names,) + tuple(_scatter_start(pair, names[0]))

    grads, deltas, new_m, new_v = {}, {}, {}, {}

    def finish_groups(some, after):
        names, owns, gots = [], [], []
        for group_names, send_sems, recv_sems, pair, lands, _ in some:
            pair, lands = _scatter_wait(send_sems, recv_sems, pair, lands, after, group_names[0])
            names += group_names
            owns += pair
            gots += lands
        mine, theirs = _reduce_join(owns, gots, "reduce_join_" + names[0])
        for n, own, other in zip(names, mine, theirs):
            grads[n], deltas[n], new_m[n], new_v[n] = _adam_halves(a[n], a["m_" + n], a["v_" + n], own, other, core, "adam_" + n)
        return deltas[names[-1]]

    swapping, groups = [], []

    def on_grads(names, gw):
        send_sems, recv_sems, g4, lands, token = _swap_start(shards_of(names, gw), names[0])
        swapping.append((names, send_sems, recv_sems, g4, lands))
        return token[0, 0]

    def sync(after):
        token = 0.0
        while swapping:
            names, send_sems, recv_sems, g4, lands = swapping.pop(0)
            g4, swapped = _swap_wait(send_sems, recv_sems, g4, lands, after, names[0])
            groups.append(scatter_group(names, g4, swapped))
            token = groups[-1][5][0, 0]
        return token

    small_flight = []

    def on_small(loss_blk, dmod, gs):
        small_flat = _pack_small(gs).at[-1].set(loss_blk[0, 0])
        send = jnp.concatenate([dmod.reshape(MOD_ROWS, D), small_flat.reshape(SMALL_ROWS, D),
                                jnp.zeros((SEND_ROWS - MOD_ROWS - SMALL_ROWS, D), F32)], axis=0)
        small_flight.extend(_gather_small_start(send))
        return small_flight[4]

    grad_x, gw = _local_step(x, positions, mod + in_flight[4][0, 0], w, later_weights, small, loss_target, on_grads, sync,
                             on_small)

    swap_started = on_grads(first, gw)

    got = _gather_small_wait(*small_flight[:4], jnp.full((8, LANES), swap_started))
    summed = _small_sum(got)
    sums = summed[N_MOD:N_MOD + SMALL_ROWS].reshape(-1)
    loss = sums[-1]
    gsmall = _unpack_small(sums)
    gsmall["conv_w"] = lax.dynamic_slice(gsmall["conv_w"].reshape(CONV_WIDTH, D_CONV), (0, chip * conv_w.shape[2]),
                                         (CONV_WIDTH, conv_w.shape[2]))
    gsmall["b_ada"] = summed[:N_MOD]
    names = ("b_ada",) + SMALL_NAMES
    rows = 208

    def pack(parts):
        flat = jnp.concatenate([parts[n].reshape(-1) for n in names])
        return jnp.pad(flat, (0, rows * LANES - flat.shape[0])).reshape(rows, LANES)

    packed = [pack({n: a[p + n] for n in names}) for p in ("", "m_", "v_")]
    g_p = pack(gsmall)
    outs = (g_p,) + tuple(_adam(packed[0], g_p, packed[1], packed[2], "adam_small"))
    for dst, flat in zip((grads, deltas, new_m, new_v), outs):
        flat, off = flat.reshape(-1), 0
        for n in names:
            dst[n] = flat[off:off + a[n].size].reshape(a[n].shape)
            off += a[n].size

    dmod_all = got[:, :MOD_ROWS, :].reshape(N_DEV * B, N_MOD * D)
    dmod_cols = lax.dynamic_slice(dmod_all, (0, chip * ADA_COLS), (N_DEV * B, ADA_COLS))
    ada = _ada_bwd(c_all, dmod_cols, w_ada[0], m_w_ada[0], v_w_ada[0])
    for dst, t in zip((grads, deltas, new_m, new_v), ada):
        dst["w_ada"] = t[None]

    sync(ada[0])
    last = groups.pop()
    finish_groups([last], finish_groups(groups, last[5]))
    for dst in (grads, deltas, new_m, new_v):
        for n in held_transposed:
            dst[n] = dst[n].transpose(0, 2, 1)

    order = ("w_ada", "b_ada", "norm_ffn1", "ffn1_w_gate", "ffn1_w_up", "ffn1_w_down", "norm_mix", "w_in", "conv_w", "conv_b",
             "dt_bias", "a_log", "d_skip", "ssd_norm_w", "q_norm_w", "w_uq", "kv_norm_w", "w_ukv", "mla_norm_w", "w_out",
             "norm_ffn2", "ffn2_w_gate", "ffn2_w_up", "ffn2_w_down", "norm_final")
    return (loss, grad_x, *[grads[n] for n in order], *[deltas[n] for n in order], *[new_m[n] for n in order],
            *[new_v[n] for n in order])
```

```python
import functools
import math

import jax
import jax.numpy as jnp
import numpy as np
from jax import lax
from jax.experimental import pallas as pl
from jax.experimental.pallas import tpu as pltpu

F32 = jnp.float32
BF16 = jnp.bfloat16

D_MODEL = 1024
D_FF = 2816
D_SSD = 1024
D_MLA = 1024
SSD_HEADS = 16
SSD_HEAD_DIM = 64
SSD_GROUPS = 2
SSD_STATE = 128
CONV_WIDTH = 4
CHUNK = 128
MLA_HEADS = 8
QK_NOPE = 64
QK_ROPE = 32
QK_DIM = QK_NOPE + QK_ROPE
V_HEAD = 128
Q_LORA = 384
KV_LORA = 256
ROPE_THETA = 10000.0
N_MOD = 9
EPS = 1e-6
D_CONV = D_SSD + 2 * SSD_GROUPS * SSD_STATE
D_PROJ = 3328
HEAD_LANES = 128
ADAM_LR = 0.001
ADAM_B1 = 0.9
ADAM_B2 = 0.999
ADAM_EPS = 1e-08
ADAM_WD = 0.01
ADAM_STEP = 10

LANES = 128
VMEM_LIMIT = 56 * 1024 * 1024
TOKEN_TILE = 512
WIDE_TOKEN_TILE = 1024
WGRAD_TOKEN_TILE = 2048
ATTN_FWD_Q_TILE = 1024
ATTN_FWD_KV_TILE = 1024
ATTN_BWD_TILE = 1024
N_CHIPS = 4
N_DEV = 8

MESH = pl.DeviceIdType.MESH


def _dot(a, b):
    return jnp.dot(a, b, preferred_element_type=F32)


def _dot_nt(a, b):
    return lax.dot_general(a, b, (((1,), (1,)), ((), ())), preferred_element_type=F32)


def _dot_tn(a, b):
    return lax.dot_general(a, b, (((0,), (0,)), ((), ())), preferred_element_type=F32)


def _cparams(semantics):
    return pltpu.CompilerParams(dimension_semantics=semantics, vmem_limit_bytes=VMEM_LIMIT)


def _resident(shape):
    zeros = (0,) * len(shape)
    return pl.BlockSpec(shape, lambda *_: zeros, pipeline_mode=pl.Buffered(1))


def _sigmoid(x):
    return jax.nn.sigmoid(x)


def _rms_stats(x):
    r = lax.rsqrt(jnp.mean(x * x, axis=-1, keepdims=True) + EPS)
    return x * r, r


def _rms_bwd(dn, xh, r, w):
    dxh = dn * w
    dx = r * (dxh - xh * jnp.mean(dxh * xh, axis=-1, keepdims=True))
    return dx, dn * xh


def _colsum(v):
    return jnp.sum(v, axis=0, keepdims=True)


def _ffn_fwd(x, nw, sh, sc, g, wg, wu, wd, seq, name, head=None):
    T, D = x.shape
    fs = wg.shape[1]
    tm = min(TOKEN_TILE, seq)
    tps = seq // tm

    def body(x_ref, nw_ref, sh_ref, sc_ref, g_ref, wg_ref, wu_ref, wd_ref, *rest):
        if head is None:
            xo_ref, a_ref, u_ref, f_ref = rest
        else:
            nf_ref, t_ref, xo_ref, a_ref, u_ref, f_ref, loss_ref, dnf_ref = rest

            @pl.when(pl.program_id(0) == 0)
            def _():
                loss_ref[...] = jnp.zeros_like(loss_ref)
                dnf_ref[...] = jnp.zeros_like(dnf_ref)

        xv = x_ref[...]
        xh, _ = _rms_stats(xv)
        h = (xh * nw_ref[...]) * (1.0 + sc_ref[0]) + sh_ref[0]
        hb = h.astype(BF16)
        f = jnp.zeros((tm, D), F32)
        for j in range(N_CHIPS):
            a = _dot_nt(hb, wg_ref[j])
            u = _dot_nt(hb, wu_ref[j])
            a_ref[j] = a.astype(BF16)
            u_ref[j] = u.astype(BF16)
            f = f + _dot((a * _sigmoid(a) * u).astype(BF16), wd_ref[j])
        f_ref[...] = f.astype(BF16)
        xo = xv + 0.5 * g_ref[0] * f
        if head is None:
            xo_ref[...] = xo
        else:
            xh, r = _rms_stats(xo)
            nfv = nf_ref[...]
            err = xh * nfv - t_ref[...]
            loss_ref[...] += (0.5 / D) * jnp.sum(err * err)
            dxo, dw_rows = _rms_bwd(err * (1.0 / D), xh, r, nfv)
            xo_ref[...] = dxo
            dnf_ref[...] += _colsum(dw_rows)

    rows = lambda n: pl.BlockSpec((tm, n), lambda i: (i, 0))
    act = pl.BlockSpec((N_CHIPS, tm, fs), lambda i: (0, i, 0))
    perb = pl.BlockSpec((1, 1, D), lambda i: (i // tps, 0, 0))
    sd = jax.ShapeDtypeStruct
    in_specs = [rows(D), _resident((1, D)), perb, perb, perb, _resident((N_CHIPS, fs, D)), _resident((N_CHIPS, fs, D)),
                _resident((N_CHIPS, fs, D))]
    out_specs = [rows(D), act, act, rows(D)]
    out_shape = [sd((T, D), F32), sd((N_CHIPS, T, fs), BF16), sd((N_CHIPS, T, fs), BF16), sd((T, D), BF16)]
    if head is not None:
        in_specs += [_resident((1, D)), rows(D)]
        out_specs += [pl.BlockSpec((8, LANES), lambda i: (0, 0)), pl.BlockSpec((1, D), lambda i: (0, 0))]
        out_shape += [sd((8, LANES), F32), sd((1, D), F32)]
    return pl.pallas_call(
        body, grid=(T // tm,), name=name, in_specs=in_specs, out_specs=out_specs, out_shape=out_shape,
        compiler_params=_cparams(("arbitrary",)),
    )(x, nw, sh, sc, g, wg, wu, wd, *(head or ()))


def _ffn_bwd(dxo, x, nw, sh, sc, g, a, u, f, wg, wu, wd, seq, name):
    T, D = x.shape
    fs = wg.shape[1]
    B = T // seq
    tm = min(TOKEN_TILE // 2, seq)
    tps = seq // tm

    def body(dxo_ref, x_ref, nw_ref, sh_ref, sc_ref, g_ref, a_ref, u_ref, f_ref, wg_ref, wu_ref, wd_ref,
             dx_ref, h_ref, s_ref, df_ref, da_ref, du_ref, dsh_ref, dsc_ref, dg_ref, dnw_ref):
        i = pl.program_id(0)

        @pl.when(i % tps == 0)
        def _():
            dsh_ref[...] = jnp.zeros_like(dsh_ref)
            dsc_ref[...] = jnp.zeros_like(dsc_ref)
            dg_ref[...] = jnp.zeros_like(dg_ref)

        @pl.when(i == 0)
        def _():
            dnw_ref[...] = jnp.zeros_like(dnw_ref)

        dxo_v = dxo_ref[...]
        dfb = (0.5 * g_ref[0] * dxo_v).astype(BF16)
        dg_ref[0] += _colsum(0.5 * dxo_v * f_ref[...].astype(F32))
        dh = jnp.zeros((tm, D), F32)
        for j in range(N_CHIPS):
            ds = _dot_nt(dfb, wd_ref[j])
            av = a_ref[j].astype(F32)
            uv = u_ref[j].astype(F32)
            sig = _sigmoid(av)
            sil = av * sig
            dab = (ds * uv * (sig * (1.0 + av * (1.0 - sig)))).astype(BF16)
            dub = (ds * sil).astype(BF16)
            dh = dh + _dot(dab, wg_ref[j]) + _dot(dub, wu_ref[j])
            s_ref[j] = (sil * uv).astype(BF16)
            da_ref[j] = dab
            du_ref[j] = dub
        xv = x_ref[...]
        xh, r = _rms_stats(xv)
        nwv = nw_ref[...]
        n = xh * nwv
        scale1 = 1.0 + sc_ref[0]
        dsc_ref[0] += _colsum(dh * n)
        dsh_ref[0] += _colsum(dh)
        dx, dw_rows = _rms_bwd(dh * scale1, xh, r, nwv)
        dnw_ref[...] += _colsum(dw_rows)
        dx_ref[...] = dxo_v + dx
        h_ref[...] = (n * scale1 + sh_ref[0]).astype(BF16)
        df_ref[...] = dfb

    rows = lambda n: pl.BlockSpec((tm, n), lambda i: (i, 0))
    act = pl.BlockSpec((N_CHIPS, tm, fs), lambda i: (0, i, 0))
    perb = pl.BlockSpec((1, 1, D), lambda i: (i // tps, 0, 0))
    sd = jax.ShapeDtypeStruct
    return pl.pallas_call(
        body, grid=(T // tm,), name=name,
        in_specs=[rows(D), rows(D), _resident((1, D)), perb, perb, perb, act, act, rows(D),
                  _resident((N_CHIPS, fs, D)), _resident((N_CHIPS, fs, D)), _resident((N_CHIPS, fs, D))],
        out_specs=[rows(D), rows(D), act, rows(D), act, act, perb, perb, perb, pl.BlockSpec((1, D), lambda i: (0, 0))],
        out_shape=[sd((T, D), F32), sd((T, D), BF16), sd((N_CHIPS, T, fs), BF16), sd((T, D), BF16),
                   sd((N_CHIPS, T, fs), BF16), sd((N_CHIPS, T, fs), BF16), sd((B, 1, D), F32), sd((B, 1, D), F32),
                   sd((B, 1, D), F32), sd((1, D), F32)],
        compiler_params=_cparams(("arbitrary",)),
    )(dxo, x, nw, sh, sc, g, a, u, f, wg, wu, wd)


def _ffn_wgrad(h, s, df, da, du, after, name):
    T, D = h.shape
    fs = s.shape[2]
    tt = min(WGRAD_TOKEN_TILE, T)
    nt = T // tt

    def body(h_ref, s_ref, df_ref, da_ref, du_ref, after_ref, dgate_ref, dup_ref, ddown_ref, gate_acc, up_acc, down_acc):
        @pl.when(pl.program_id(1) == 0)
        def _():
            gate_acc[...] = jnp.zeros_like(gate_acc)
            up_acc[...] = jnp.zeros_like(up_acc)
            down_acc[...] = jnp.zeros_like(down_acc)

        hv = h_ref[...]
        gate_acc[...] += _dot_tn(da_ref[0], hv)
        up_acc[...] += _dot_tn(du_ref[0], hv)
        down_acc[...] += _dot_tn(s_ref[0], df_ref[...])

        @pl.when(pl.program_id(1) == nt - 1)
        def _():
            dgate_ref[0] = gate_acc[...].astype(BF16)
            dup_ref[0] = up_acc[...].astype(BF16)
            ddown_ref[0] = down_acc[...].astype(BF16)

    rows = pl.BlockSpec((tt, D), lambda j, t: (t, 0))
    act = pl.BlockSpec((1, tt, fs), lambda j, t: (j, t, 0))
    shard = pl.BlockSpec((1, fs, D), lambda j, t: (j, 0, 0))
    return pl.pallas_call(
        body, grid=(N_CHIPS, nt), name=name,
        in_specs=[rows, act, rows, act, act, pl.BlockSpec(memory_space=pl.ANY)],
        out_specs=[shard] * 3, out_shape=[jax.ShapeDtypeStruct((N_CHIPS, fs, D), BF16)] * 3,
        scratch_shapes=[pltpu.VMEM((fs, D), F32)] * 3,
        compiler_params=_cparams(("arbitrary", "arbitrary")),
    )(h, s, df, da, du, after)


def _mm_tn(xa, ya, tn, name):
    T, K = xa.shape
    N = ya.shape[1]
    tt = min(WGRAD_TOKEN_TILE, T)
    nt = T // tt

    def body(x_ref, y_ref, o_ref, acc_ref):
        @pl.when(pl.program_id(1) == 0)
        def _():
            acc_ref[...] = jnp.zeros_like(acc_ref)

        acc_ref[...] += _dot_tn(x_ref[...], y_ref[...])

        @pl.when(pl.program_id(1) == nt - 1)
        def _():
            o_ref[...] = acc_ref[...].astype(BF16)

    return pl.pallas_call(
        body, grid=(N // tn, nt), name=name,
        in_specs=[pl.BlockSpec((tt, K), lambda j, t: (t, 0)), pl.BlockSpec((tt, tn), lambda j, t: (t, j))],
        out_specs=pl.BlockSpec((K, tn), lambda j, t: (0, j)),
        out_shape=jax.ShapeDtypeStruct((K, N), BF16),
        scratch_shapes=[pltpu.VMEM((K, tn), F32)],
        compiler_params=_cparams(("arbitrary", "arbitrary")),
    )(xa, ya)


_PROJ_SPLITS = (0, 1024, 2560, 2944, 3200, 3328)


def _inproj_fwd(x, nw, sh, sc, win, seq):
    T, D = x.shape
    tm = min(WIDE_TOKEN_TILE, seq)
    tps = seq // tm
    widths = [b - a for a, b in zip(_PROJ_SPLITS[:-1], _PROJ_SPLITS[1:])]
    dtypes = [BF16, BF16, F32, F32, F32]

    def body(x_ref, nw_ref, sh_ref, sc_ref, w_ref, *outs):
        xh, _ = _rms_stats(x_ref[...])
        h = (xh * nw_ref[...]) * (1.0 + sc_ref[0]) + sh_ref[0]
        proj = _dot_nt(h.astype(BF16), w_ref[...])
        for o, lo, hi in zip(outs, _PROJ_SPLITS[:-1], _PROJ_SPLITS[1:]):
            o[...] = proj[:, lo:hi].astype(o.dtype)

    rows = lambda n: pl.BlockSpec((tm, n), lambda i: (i, 0))
    perb = pl.BlockSpec((1, 1, D), lambda i: (i // tps, 0, 0))
    return pl.pallas_call(
        body, grid=(T // tm,), name="inproj_fwd",
        in_specs=[rows(D), _resident((1, D)), perb, perb, _resident((D_PROJ, D))],
        out_specs=[rows(w) for w in widths],
        out_shape=[jax.ShapeDtypeStruct((T, w), dt) for w, dt in zip(widths, dtypes)],
        compiler_params=_cparams(("arbitrary",)),
    )(x, nw, sh, sc, win)


def _inproj_bwd(dx2, x, nw, sh, sc, win, dz, dxbc, dcq, dckv, ddtk_a, ddtk_b, seq):
    T, D = x.shape
    B = T // seq
    tm = min(TOKEN_TILE, seq)
    tps = seq // tm

    def body(dx2_ref, x_ref, nw_ref, sh_ref, sc_ref, w_ref, dz_ref, dxbc_ref, dcq_ref, dckv_ref, da_ref, db_ref,
             dx_ref, h_ref, dp_ref, dsh_ref, dsc_ref, dnw_ref):
        i = pl.program_id(0)

        @pl.when(i % tps == 0)
        def _():
            dsh_ref[...] = jnp.zeros_like(dsh_ref)
            dsc_ref[...] = jnp.zeros_like(dsc_ref)

        @pl.when(i == 0)
        def _():
            dnw_ref[...] = jnp.zeros_like(dnw_ref)

        dproj = jnp.concatenate(
            [dz_ref[...], dxbc_ref[...], dcq_ref[...].astype(BF16), dckv_ref[...].astype(BF16),
             (da_ref[...] + db_ref[...]).astype(BF16)], axis=1)
        dp_ref[...] = dproj
        dh = _dot(dproj, w_ref[...])
        xh, r = _rms_stats(x_ref[...])
        nwv = nw_ref[...]
        n = xh * nwv
        scale1 = 1.0 + sc_ref[0]
        dsc_ref[0] += _colsum(dh * n)
        dsh_ref[0] += _colsum(dh)
        dx, dw_rows = _rms_bwd(dh * scale1, xh, r, nwv)
        dnw_ref[...] += _colsum(dw_rows)
        dx_ref[...] = dx2_ref[...] + dx
        h_ref[...] = (n * scale1 + sh_ref[0]).astype(BF16)

    rows = lambda n: pl.BlockSpec((tm, n), lambda i: (i, 0))
    perb = pl.BlockSpec((1, 1, D), lambda i: (i // tps, 0, 0))
    sd = jax.ShapeDtypeStruct
    return pl.pallas_call(
        body, grid=(T // tm,), name="inproj_bwd",
        in_specs=[rows(D), rows(D), _resident((1, D)), perb, perb, _resident((D_PROJ, D)),
                  rows(1024), rows(D_CONV), rows(Q_LORA), rows(KV_LORA), rows(LANES), rows(LANES)],
        out_specs=[rows(D), rows(D), rows(D_PROJ), perb, perb, pl.BlockSpec((1, D), lambda i: (0, 0))],
        out_shape=[sd((T, D), F32), sd((T, D), BF16), sd((T, D_PROJ), BF16), sd((B, 1, D), F32), sd((B, 1, D), F32),
                   sd((1, D), F32)],
        compiler_params=_cparams(("arbitrary",)),
    )(dx2, x, nw, sh, sc, win, dz, dxbc, dcq, dckv, ddtk_a, ddtk_b)


SUBLANES = 8


def _shift_down(v, k):
    r = pltpu.roll(v, k, 0)
    row = lax.broadcasted_iota(jnp.int32, (SUBLANES, v.shape[1]), 0)
    return jnp.concatenate([jnp.where(row < k, 0.0, r[:SUBLANES]), r[SUBLANES:]], axis=0)


def _shift_up(v, k):
    n = v.shape[0]
    r = pltpu.roll(v, n - k, 0)
    row = lax.broadcasted_iota(jnp.int32, (SUBLANES, v.shape[1]), 0)
    return jnp.concatenate([r[:n - SUBLANES], jnp.where(row >= SUBLANES - k, 0.0, r[n - SUBLANES:])], axis=0)


def _conv_pre(xv, w_ref, b_ref):
    pre = b_ref[...] + w_ref[CONV_WIDTH - 1:CONV_WIDTH, :] * xv
    for k in range(1, CONV_WIDTH):
        pre = pre + w_ref[CONV_WIDTH - 1 - k:CONV_WIDTH - k, :] * _shift_down(xv, k)
    return pre


def _conv_fwd(xraw, cw, cb):
    B, S, C = xraw.shape

    def body(x_ref, w_ref, b_ref, o_ref):
        pre = _conv_pre(x_ref[0].astype(F32), w_ref, b_ref)
        o_ref[0] = (pre * _sigmoid(pre)).astype(BF16)

    blk = pl.BlockSpec((1, S, LANES), lambda b, j: (b, 0, j))
    return pl.pallas_call(
        body, grid=(B, C // LANES), name="conv_fwd",
        in_specs=[blk, pl.BlockSpec((CONV_WIDTH, LANES), lambda b, j: (0, j)), pl.BlockSpec((1, LANES), lambda b, j: (0, j))],
        out_specs=blk, out_shape=jax.ShapeDtypeStruct((B, S, C), BF16),
        compiler_params=_cparams(("arbitrary", "arbitrary")),
    )(xraw, cw, cb)


def _conv_bwd(dout, xraw, cw, cb):
    B, S, C = xraw.shape

    def body(d_ref, x_ref, w_ref, b_ref, dx_ref, dw_ref, db_ref):
        @pl.when(pl.program_id(1) == 0)
        def _():
            dw_ref[...] = jnp.zeros_like(dw_ref)
            db_ref[...] = jnp.zeros_like(db_ref)

        xv = x_ref[0].astype(F32)
        pre = _conv_pre(xv, w_ref, b_ref)
        sig = _sigmoid(pre)
        dpre = d_ref[0].astype(F32) * (sig * (1.0 + pre * (1.0 - sig)))
        dx = w_ref[CONV_WIDTH - 1:CONV_WIDTH, :] * dpre
        for k in range(1, CONV_WIDTH):
            dx = dx + w_ref[CONV_WIDTH - 1 - k:CONV_WIDTH - k, :] * _shift_up(dpre, k)
        dx_ref[0] = dx.astype(BF16)
        db_ref[...] += _colsum(dpre)
        dws = [_colsum(dpre * (xv if k == 0 else _shift_down(xv, k))) for k in range(CONV_WIDTH - 1, -1, -1)]
        dw_ref[...] += jnp.concatenate(dws, axis=0)

    blk = pl.BlockSpec((1, S, LANES), lambda j, b: (b, 0, j))
    wspec = pl.BlockSpec((CONV_WIDTH, LANES), lambda j, b: (0, j))
    bspec = pl.BlockSpec((1, LANES), lambda j, b: (0, j))
    return pl.pallas_call(
        body, grid=(C // LANES, B), name="conv_bwd",
        in_specs=[blk, blk, wspec, bspec], out_specs=[blk, wspec, bspec],
        out_shape=[jax.ShapeDtypeStruct((B, S, C), BF16), jax.ShapeDtypeStruct((CONV_WIDTH, C), F32),
                   jax.ShapeDtypeStruct((1, C), F32)],
        compiler_params=_cparams(("arbitrary", "arbitrary")),
    )(dout, xraw, cw, cb)


def _softplus(x):
    return jnp.maximum(x, 0.0) + jnp.log(1.0 + jnp.exp(-jnp.abs(x)))


def _ssd_common(xbc_ref, dtk_ref, dtb_ref, alog_ref, e_ref):
    L = CHUNK
    xbc = xbc_ref[0]
    xs = xbc[:, :D_SSD].astype(F32)
    bm = xbc[:, D_SSD:D_SSD + 256]
    cm = xbc[:, D_SSD + 256:D_SSD + 512]
    head = lax.broadcasted_iota(jnp.int32, (1, LANES), 1) < SSD_HEADS
    a128 = jnp.where(head, -jnp.exp(alog_ref[...]), 0.0)
    pre = dtk_ref[0] + dtb_ref[...]
    dt = _softplus(pre)
    dA = dt * a128
    row = lax.broadcasted_iota(jnp.int32, (L, L), 0)
    col = lax.broadcasted_iota(jnp.int32, (L, L), 1)
    causal = col <= row
    tri = causal.astype(F32)
    triT = (row <= col).astype(F32)
    tri = causal.astype(BF16)
    triT = (row <= col).astype(BF16)
    dA3 = _split3(dA)
    acum = _sum3(lambda part: _dot(tri, part), dA3)
    acumT = _sum3(lambda part: _dot_tn(part, triT), dA3)
    E = e_ref[...]
    acum_f = _spread(acum, E)
    dt_f = _spread(dt, E)
    e_f = jnp.exp(acum_f)
    w_f = jnp.exp(acum_f[L - 1:L, :] - acum_f)
    xt = xs * dt_f
    return dict(xs=xs, bm=bm, cm=cm, a128=a128, pre=pre, dt=dt, causal=causal, tri=tri, triT=triT, acum=acum,
                acumT=acumT, E=E, dt_f=dt_f, e_f=e_f, w_f=w_f, xt=xt, head=head)


def _split3(x):
    p1 = x.astype(BF16)
    r1 = x - p1.astype(F32)
    p2 = r1.astype(BF16)
    return p1, p2, (r1 - p2.astype(F32)).astype(BF16)


def _sum3(mm, parts):
    return (mm(parts[0]) + mm(parts[1])) + mm(parts[2])


def _spread(v, e):
    return _sum3(lambda part: _dot(part, e), _split3(v))


def _gather_heads(v, e):
    return _sum3(lambda part: _dot_nt(part, e), _split3(v))


def _head_mask(k):
    lane = lax.broadcasted_iota(jnp.int32, (CHUNK, LANES), 1)
    return (lane >= SSD_HEAD_DIM) if k == 1 else (lane < SSD_HEAD_DIM)


def _pair_decay(alast, h0):
    row = lax.broadcasted_iota(jnp.int32, (2 * SSD_HEAD_DIM, SSD_STATE), 0)
    return jnp.exp(jnp.where(row < SSD_HEAD_DIM, alast[:, h0:h0 + 1], alast[:, h0 + 1:h0 + 2]))


def _decay_matrix(q, h):
    seg = q["acum"][:, h:h + 1] - q["acumT"][h:h + 1, :]
    return jnp.exp(jnp.where(q["causal"], seg, -1e30))


def _gated_norm(y, zz, nw):
    sig = _sigmoid(zz)
    sil = zz * sig
    yg = y * sil
    half = D_SSD // SSD_GROUPS
    parts = []
    for g in range(SSD_GROUPS):
        xh, r = _rms_stats(yg[:, g * half:(g + 1) * half])
        parts.append((xh, r))
    return sig, sil, parts


def _ssd_fwd(xbc, dtk, z, dtb, alog, dsk, nw, expand):
    B, S, _ = xbc.shape
    L = CHUNK
    nc = S // L

    def body(xbc_ref, dtk_ref, z_ref, dtb_ref, alog_ref, dsk_ref, nw_ref, e_ref, y_ref, ys_ref, prev_ref, st_ref):
        @pl.when(pl.program_id(0) == 0)
        def _():
            st_ref[...] = jnp.zeros_like(st_ref)

        for b in range(B):
            one = lambda ref: ref.at[pl.ds(b, 1)]
            sequence_step(one(xbc_ref), one(dtk_ref), one(z_ref), dtb_ref, alog_ref, dsk_ref, nw_ref, e_ref, one(y_ref),
                          one(ys_ref), one(prev_ref), st_ref.at[b])

    def sequence_step(xbc_ref, dtk_ref, z_ref, dtb_ref, alog_ref, dsk_ref, nw_ref, e_ref, y_ref, ys_ref, prev_ref, st_ref):
        q = _ssd_common(xbc_ref, dtk_ref, dtb_ref, alog_ref, e_ref)
        xtb = q["xt"].astype(BF16)
        xwb = (q["xt"] * q["w_f"]).astype(BF16)
        alast = q["acum"][L - 1:L, :]
        ys = []
        for g in range(SSD_GROUPS):
            bg = q["bm"][:, g * 128:(g + 1) * 128]
            cg = q["cm"][:, g * 128:(g + 1) * 128]
            G = _dot_nt(cg, bg)
            for pr in range(SSD_HEADS // SSD_GROUPS // 2):
                h0 = g * 8 + 2 * pr
                lo = h0 * SSD_HEAD_DIM
                xt_p = xtb[:, lo:lo + 128]
                ydiag = jnp.zeros((L, LANES), F32)
                for k in range(2):
                    M = (G * _decay_matrix(q, h0 + k)).astype(BF16)
                    ydiag = ydiag + _dot(M, jnp.where(_head_mask(k), xt_p, jnp.zeros_like(xt_p)))
                hp = st_ref[lo:lo + 128, :]
                prev_ref[0, 0, lo:lo + 128, :] = hp.astype(BF16)
                zoff = _dot_nt(cg, hp.astype(BF16))
                ys.append(ydiag + zoff * q["e_f"][:, lo:lo + 128])
                st_ref[lo:lo + 128, :] = _pair_decay(alast, h0) * hp + _dot_tn(xwb[:, lo:lo + 128], bg)
        y = jnp.concatenate(ys, axis=1) + dsk_ref[...] * q["xs"]
        y_ref[0] = y.astype(BF16)
        _, _, parts = _gated_norm(y, z_ref[0].astype(F32), nw_ref[...])
        half = D_SSD // SSD_GROUPS
        ys_ref[0] = jnp.concatenate(
            [xh * nw_ref[:, g * half:(g + 1) * half] for g, (xh, _) in enumerate(parts)], axis=1).astype(BF16)

    chunk = lambda n: pl.BlockSpec((B, L, n), lambda c: (0, c, 0))
    vec = pl.BlockSpec((1, LANES), lambda c: (0, 0))
    return pl.pallas_call(
        body, grid=(nc,), name="ssd_fwd",
        in_specs=[chunk(D_CONV), chunk(LANES), chunk(D_SSD), vec, vec, pl.BlockSpec((1, D_SSD), lambda c: (0, 0)),
                  pl.BlockSpec((1, D_SSD), lambda c: (0, 0)), pl.BlockSpec((LANES, D_SSD), lambda c: (0, 0))],
        out_specs=[chunk(D_SSD), chunk(D_SSD), pl.BlockSpec((B, 1, D_SSD, SSD_STATE), lambda c: (0, c, 0, 0))],
        out_shape=[jax.ShapeDtypeStruct((B, S, D_SSD), BF16), jax.ShapeDtypeStruct((B, S, D_SSD), BF16),
                   jax.ShapeDtypeStruct((B, nc, D_SSD, SSD_STATE), BF16)],
        scratch_shapes=[pltpu.VMEM((B, D_SSD, SSD_STATE), F32)],
        compiler_params=_cparams(("arbitrary",)),
    )(xbc, dtk, z, dtb, alog, dsk, nw, expand)


def _ssd_bwd(xbc, dtk, z, y, prev, dys, dtb, alog, dsk, nw, expand):
    B, S, _ = xbc.shape
    L = CHUNK
    nc = S // L
    half = D_SSD // SSD_GROUPS

    def body(xbc_ref, dtk_ref, z_ref, y_ref, prev_ref, dys_ref, dtb_ref, alog_ref, dsk_ref, nw_ref, e_ref,
             dxbc_ref, ddtk_ref, dz_ref, dnw_ref, dvec_ref, dh_ref, dskc_ref):
        @pl.when(pl.program_id(0) == 0)
        def _():
            dnw_ref[...] = jnp.zeros_like(dnw_ref)
            dvec_ref[...] = jnp.zeros_like(dvec_ref)
            dskc_ref[...] = jnp.zeros_like(dskc_ref)
            dh_ref[...] = jnp.zeros_like(dh_ref)

        for b in range(B):
            one = lambda ref: ref.at[pl.ds(b, 1)]
            sequence_step(one(xbc_ref), one(dtk_ref), one(z_ref), one(y_ref), one(prev_ref), one(dys_ref), dtb_ref, alog_ref,
                          dsk_ref, nw_ref, e_ref, one(dxbc_ref), one(ddtk_ref), one(dz_ref), dnw_ref, dvec_ref, dh_ref.at[b],
                          dskc_ref)

        @pl.when(pl.program_id(0) == nc - 1)
        def _():
            dvec_ref[2:3, :] = _gather_heads(jnp.broadcast_to(dskc_ref[...], (8, D_SSD)), e_ref[...])[0:1, :]

    def sequence_step(xbc_ref, dtk_ref, z_ref, y_ref, prev_ref, dys_ref, dtb_ref, alog_ref, dsk_ref, nw_ref, e_ref,
                      dxbc_ref, ddtk_ref, dz_ref, dnw_ref, dvec_ref, dh_ref, dskc_ref):
        q = _ssd_common(xbc_ref, dtk_ref, dtb_ref, alog_ref, e_ref)
        E = q["E"]
        xs = q["xs"]
        yv = y_ref[0].astype(F32)
        zz = z_ref[0].astype(F32)
        sig, sil, parts = _gated_norm(yv, zz, nw_ref[...])
        dn = dys_ref[0].astype(F32)
        dyg, dnw_rows = [], []
        for g, (xh, r) in enumerate(parts):
            dpart, dw_rows = _rms_bwd(dn[:, g * half:(g + 1) * half], xh, r, nw_ref[:, g * half:(g + 1) * half])
            dyg.append(dpart)
            dnw_rows.append(dw_rows)
        dyg = jnp.concatenate(dyg, axis=1)
        dnw_ref[...] += _colsum(jnp.concatenate(dnw_rows, axis=1))
        dY = dyg * sil
        dz_ref[0] = (dyg * yv * (sig * (1.0 + zz * (1.0 - sig)))).astype(BF16)
        dsk_f = dsk_ref[...]
        dskc_ref[...] += _colsum(dY * xs)
        dYb = dY.astype(BF16)
        xtb = q["xt"].astype(BF16)
        xwb = (q["xt"] * q["w_f"]).astype(BF16)
        acum = q["acum"]
        alast = acum[L - 1:L, :]
        lane_id = lax.broadcasted_iota(jnp.int32, (L, LANES), 1)
        sub_id = lax.broadcasted_iota(jnp.int32, (LANES, L), 0)
        lane_row = lax.broadcasted_iota(jnp.int32, (1, LANES), 1)
        da_rows = jnp.zeros((L, LANES), F32)
        daT = jnp.zeros((LANES, L), F32)
        dxt, prod_off, prod_st, dbs, dcs = [], [], [], [], []
        hsum_row = jnp.zeros((1, LANES), F32)
        for g in range(SSD_GROUPS):
            bg = q["bm"][:, g * 128:(g + 1) * 128]
            cg = q["cm"][:, g * 128:(g + 1) * 128]
            G = _dot_nt(cg, bg)
            dG = jnp.zeros((L, L), F32)
            dcg = jnp.zeros((L, SSD_STATE), F32)
            dbg = jnp.zeros((L, SSD_STATE), F32)
            for pr in range(SSD_HEADS // SSD_GROUPS // 2):
                h0 = g * 8 + 2 * pr
                lo = h0 * SSD_HEAD_DIM
                cols = slice(lo, lo + 128)
                dY_p = dYb[:, cols]
                xt_p = xtb[:, cols]
                dxt_p = jnp.zeros((L, LANES), F32)
                for k in range(2):
                    h = h0 + k
                    Lm = _decay_matrix(q, h)
                    Mf = G * Lm
                    dYk = jnp.where(_head_mask(k), dY_p, jnp.zeros_like(dY_p))
                    dM = _dot_nt(dYk, xt_p)
                    dxt_p = dxt_p + _dot_tn(Mf.astype(BF16), dYk)
                    dG = dG + dM * Lm
                    Q = dM * Mf
                    da_rows = da_rows + jnp.where(lane_id == h, jnp.sum(Q, axis=1, keepdims=True), 0.0)
                    daT = daT + jnp.where(sub_id == h, jnp.sum(Q, axis=0, keepdims=True), 0.0)
                hpb = prev_ref[0, 0, lo:lo + 128, :]
                hp = hpb.astype(F32)
                zoff = _dot_nt(cg, hpb)
                e_p = q["e_f"][:, cols]
                dY_pf = dY[:, cols]
                dZb = (dY_pf * e_p).astype(BF16)
                dcg = dcg + _dot(dZb, hpb)
                dhp_off = _dot_tn(dZb, cg)
                prod_off.append(dY_pf * zoff * e_p)
                dS = dh_ref[lo:lo + 128, :]
                dSb = dS.astype(BF16)
                U = _dot_nt(bg, dSb)
                dxt_p = dxt_p + U * q["w_f"][:, cols]
                dbg = dbg + _dot(xwb[:, cols], dSb)
                prod_st.append(q["xt"][:, cols] * U)
                dh_ref[lo:lo + 128, :] = _pair_decay(alast, h0) * dS + dhp_off
                dsh = dS * hp
                for k in range(2):
                    total = jnp.sum(dsh[k * SSD_HEAD_DIM:(k + 1) * SSD_HEAD_DIM, :], axis=(0, 1), keepdims=True)
                    hsum_row = hsum_row + jnp.where(lane_row == h0 + k, total, 0.0)
                dxt.append(dxt_p)
            dGb = dG.astype(BF16)
            dcs.append(dcg + _dot(dGb, bg))
            dbs.append(dbg + _dot_tn(dGb, cg))
        dxt = jnp.concatenate(dxt, axis=1)
        da_rows = da_rows + _gather_heads(jnp.concatenate(prod_off, axis=1), E)
        dww = _gather_heads(jnp.concatenate(prod_st, axis=1), E) * jnp.exp(alast - acum)
        da_rows = da_rows - dww
        dlast = _colsum(dww) + jnp.exp(alast) * hsum_row
        triT = q["triT"]
        ddA = (_sum3(lambda part: _dot(triT, part), _split3(da_rows))
               - _sum3(lambda part: _dot_nt(triT, part), _split3(daT)) + dlast)
        ddA = jnp.where(q["head"], ddA, 0.0)
        ddt = ddA * q["a128"] + _gather_heads(dxt * xs, E)
        ddt_raw = jnp.where(q["head"], ddt * _sigmoid(q["pre"]), 0.0)
        ddtk_ref[0] = ddt_raw
        dxs = dxt * q["dt_f"] + dsk_f * dY
        dxbc_ref[0] = jnp.concatenate([dxs] + dbs + dcs, axis=1).astype(BF16)
        dvec_ref[0:1, :] += _colsum(ddt_raw)
        dvec_ref[1:2, :] += _colsum(ddA * q["dt"]) * q["a128"]

    rev = lambda n: pl.BlockSpec((B, L, n), lambda c: (0, nc - 1 - c, 0))
    vec = pl.BlockSpec((1, LANES), lambda c: (0, 0))
    sd = jax.ShapeDtypeStruct
    return pl.pallas_call(
        body, grid=(nc,), name="ssd_bwd",
        in_specs=[rev(D_CONV), rev(LANES), rev(D_SSD), rev(D_SSD),
                  pl.BlockSpec((B, 1, D_SSD, SSD_STATE), lambda c: (0, nc - 1 - c, 0, 0)), rev(D_SSD), vec, vec,
                  pl.BlockSpec((1, D_SSD), lambda c: (0, 0)),
                  pl.BlockSpec((1, D_SSD), lambda c: (0, 0)), pl.BlockSpec((LANES, D_SSD), lambda c: (0, 0))],
        out_specs=[rev(D_CONV), rev(LANES), rev(D_SSD), pl.BlockSpec((1, D_SSD), lambda c: (0, 0)),
                   pl.BlockSpec((8, LANES), lambda c: (0, 0))],
        out_shape=[sd((B, S, D_CONV), BF16), sd((B, S, LANES), F32), sd((B, S, D_SSD), BF16), sd((1, D_SSD), F32),
                   sd((8, LANES), F32)],
        scratch_shapes=[pltpu.VMEM((B, D_SSD, SSD_STATE), F32), pltpu.VMEM((1, D_SSD), F32)],
        compiler_params=_cparams(("arbitrary",)),
    )(xbc, dtk, z, y, prev, dys, dtb, alog, dsk, nw, expand)


def _rope_tables(pos_ref, invf_ref, place_ref):
    ang = invf_ref[...] * pos_ref[0].astype(F32)
    place = place_ref[...]
    cosf = 1.0 + _sum3(lambda part: _dot_tn(part, place), _split3(jnp.cos(ang) - 1.0))
    sinf = _sum3(lambda part: _dot_tn(part, place), _split3(jnp.sin(ang)))
    return cosf, sinf


def _rot(u):
    lane = lax.broadcasted_iota(jnp.int32, u.shape, 1)
    first = (lane >= QK_NOPE) & (lane < QK_NOPE + QK_ROPE // 2)
    second = (lane >= QK_NOPE + QK_ROPE // 2) & (lane < QK_DIM)
    return jnp.where(first, -pltpu.roll(u, LANES - QK_ROPE // 2, 1), jnp.where(second, pltpu.roll(u, QK_ROPE // 2, 1), 0.0))


def _rope_lanes(shape):
    lane = lax.broadcasted_iota(jnp.int32, shape, 1)
    return (lane >= QK_NOPE) & (lane < QK_DIM)


def _mla_prep(cq, ckv, dtk, pos, qw, kvw, wuq, wukv, invf, place):
    T = cq.shape[0]
    tm = min(WIDE_TOKEN_TILE, T)
    scale = 1.0 / math.sqrt(QK_DIM)
    HW = MLA_HEADS * HEAD_LANES

    def body(cq_ref, ckv_ref, dtk_ref, pos_ref, qw_ref, kvw_ref, wuq_ref, wukv_ref, invf_ref, place_ref, q_ref, k_ref, v_ref,
             cos_ref, sin_ref):
        xh, _ = _rms_stats(cq_ref[...])
        qv = _dot((xh * qw_ref[...]).astype(BF16), wuq_ref[...])
        xh, _ = _rms_stats(ckv_ref[...])
        kv = _dot((xh * kvw_ref[...]).astype(BF16), wukv_ref[...])
        cosf, sinf = _rope_tables(pos_ref, invf_ref, place_ref)
        cos_ref[...] = cosf
        sin_ref[...] = sinf
        rope = lambda u: u * cosf + _rot(u) * sinf
        dtkv = dtk_ref[...]
        kr = rope(jnp.where(_rope_lanes(dtkv.shape), dtkv, 0.0))
        for h in range(MLA_HEADS):
            cols = slice(h * HEAD_LANES, (h + 1) * HEAD_LANES)
            q_ref[:, cols] = (rope(qv[:, cols]) * scale).astype(BF16)
            k_ref[:, cols] = (kv[:, cols] + kr).astype(BF16)
        v_ref[...] = kv[:, HW:].astype(BF16)

    rows = lambda n: pl.BlockSpec((tm, n), lambda i: (i, 0))
    return pl.pallas_call(
        body, grid=(T // tm,), name="mla_prep",
        in_specs=[rows(Q_LORA), rows(KV_LORA), rows(LANES), pl.BlockSpec((1, 1, tm), lambda i: (i, 0, 0)),
                  _resident((1, Q_LORA)), _resident((1, KV_LORA)), _resident((Q_LORA, HW)), _resident((KV_LORA, 2 * HW)),
                  _resident((QK_ROPE // 2, 1)), _resident((QK_ROPE // 2, LANES))],
        out_specs=[rows(HW), rows(HW), rows(HW), rows(LANES), rows(LANES)],
        out_shape=[jax.ShapeDtypeStruct((T, HW), BF16)] * 3 + [jax.ShapeDtypeStruct((T, LANES), F32)] * 2,
        compiler_params=_cparams(("arbitrary",)),
    )(cq, ckv, dtk, pos.reshape(T // tm, 1, tm), qw, kvw, wuq, wukv, invf, place)


def _mla_prep_bwd(dq, dk, dv, cq, ckv, cos_t, sin_t, qw, kvw, wuq, wukv):
    T = cq.shape[0]
    tm = min(WIDE_TOKEN_TILE, T)
    scale = 1.0 / math.sqrt(QK_DIM)
    HW = MLA_HEADS * HEAD_LANES

    def body(dq_ref, dk_ref, dv_ref, cq_ref, ckv_ref, cos_ref, sin_ref, qw_ref, kvw_ref, wuq_ref, wukv_ref,
             dcq_ref, dckv_ref, ddtk_ref, qn_ref, kvn_ref, dqo_ref, dkvo_ref, dqw_ref, dkvw_ref):
        @pl.when(pl.program_id(0) == 0)
        def _():
            dqw_ref[...] = jnp.zeros_like(dqw_ref)
            dkvw_ref[...] = jnp.zeros_like(dkvw_ref)

        cosf, sinf = cos_ref[...], sin_ref[...]
        unrope = lambda d: d * cosf - _rot(d * sinf)
        dkr = jnp.zeros((tm, LANES), F32)
        nope = lax.broadcasted_iota(jnp.int32, (tm, LANES), 1) < QK_NOPE
        for h in range(MLA_HEADS):
            cols = slice(h * HEAD_LANES, (h + 1) * HEAD_LANES)
            dqo_ref[:, cols] = unrope(dq_ref[:, cols].astype(F32) * scale).astype(BF16)
            dkh = dk_ref[:, cols].astype(F32)
            dkr = dkr + jnp.where(_rope_lanes(dkh.shape), dkh, 0.0)
            dkvo_ref[:, cols] = jnp.where(nope, dkh, 0.0).astype(BF16)
        dkvo_ref[:, HW:] = dv_ref[...].astype(BF16)
        ddtk_ref[...] = unrope(dkr)
        xh, r = _rms_stats(cq_ref[...])
        qn_ref[...] = (xh * qw_ref[...]).astype(BF16)
        dx, dw_rows = _rms_bwd(_dot_nt(dqo_ref[...], wuq_ref[...]), xh, r, qw_ref[...])
        dcq_ref[...] = dx
        dqw_ref[...] += _colsum(dw_rows)
        xh, r = _rms_stats(ckv_ref[...])
        kvn_ref[...] = (xh * kvw_ref[...]).astype(BF16)
        dx, dw_rows = _rms_bwd(_dot_nt(dkvo_ref[...], wukv_ref[...]), xh, r, kvw_ref[...])
        dckv_ref[...] = dx
        dkvw_ref[...] += _colsum(dw_rows)

    rows = lambda n: pl.BlockSpec((tm, n), lambda i: (i, 0))
    sd = jax.ShapeDtypeStruct
    return pl.pallas_call(
        body, grid=(T // tm,), name="mla_prep_bwd",
        in_specs=[rows(HW), rows(HW), rows(HW), rows(Q_LORA), rows(KV_LORA), rows(LANES), rows(LANES), _resident((1, Q_LORA)),
                  _resident((1, KV_LORA)), _resident((Q_LORA, HW)), _resident((KV_LORA, 2 * HW))],
        out_specs=[rows(Q_LORA), rows(KV_LORA), rows(LANES), rows(Q_LORA), rows(KV_LORA), rows(HW), rows(2 * HW),
                   pl.BlockSpec((1, Q_LORA), lambda i: (0, 0)), pl.BlockSpec((1, KV_LORA), lambda i: (0, 0))],
        out_shape=[sd((T, Q_LORA), F32), sd((T, KV_LORA), F32), sd((T, LANES), F32), sd((T, Q_LORA), BF16),
                   sd((T, KV_LORA), BF16), sd((T, HW), BF16), sd((T, 2 * HW), BF16), sd((1, Q_LORA), F32),
                   sd((1, KV_LORA), F32)],
        compiler_params=_cparams(("arbitrary",)),
    )(dq, dk, dv, cq, ckv, cos_t, sin_t, qw, kvw, wuq, wukv)


def _causal_mask(t):
    row = lax.broadcasted_iota(jnp.int32, (t, t), 0)
    col = lax.broadcasted_iota(jnp.int32, (t, t), 1)
    return col <= row


def _attn_fwd(q, k, v):
    B, S, HW = q.shape
    H = HW // HEAD_LANES
    t = min(ATTN_FWD_Q_TILE, S)
    tk = min(ATTN_FWD_KV_TILE, t)
    nq = S // t
    per = t // tk

    pair = 4
    pw = pair * HEAD_LANES

    def body(q_ref, k_ref, v_ref, o_ref, lse_ref):
        qi = pl.program_id(2)
        lanes = [slice(hh * HEAD_LANES, (hh + 1) * HEAD_LANES) for hh in range(pair)]
        qs = [q_ref[0, :, cols] for cols in lanes]

        def step(j, carry, diag):
            sl = pl.ds(pl.multiple_of(j * tk, tk), tk)
            out = []
            for qv, cols, (m, l, acc) in zip(qs, lanes, carry):
                s = _dot_nt(qv, k_ref[0, sl, cols])
                if diag is not None:
                    row = lax.broadcasted_iota(jnp.int32, (t, tk), 0)
                    col = lax.broadcasted_iota(jnp.int32, (t, tk), 1)
                    s = jnp.where(col + diag * tk <= row, s, -1e30)
                m_new = jnp.maximum(m, jnp.max(s, axis=-1, keepdims=True))
                alpha = jnp.exp(m - m_new)
                p = jnp.exp(s - m_new)
                l = alpha * l + jnp.sum(p, axis=-1, keepdims=True)
                acc = alpha * acc + _dot(p.astype(BF16), v_ref[0, sl, cols])
                out.append((m_new, l, acc))
            return tuple(out)

        init = tuple((jnp.full((t, 1), -1e30, F32), jnp.zeros((t, 1), F32), jnp.zeros((t, HEAD_LANES), F32))
                     for _ in range(pair))
        carry = lax.fori_loop(0, qi * per, lambda j, c: step(j, c, None), init)
        for d in range(per):
            carry = step(qi * per + d, carry, d)
        for hh, (m, l, acc) in enumerate(carry):
            o_ref[0, :, lanes[hh]] = (acc / l).astype(BF16)
            lse_ref[0, hh] = m + jnp.log(l)

    return pl.pallas_call(
        body, grid=(B, H // pair, nq), name="attn_fwd",
        in_specs=[pl.BlockSpec((1, t, pw), lambda b, h, i: (b, i, h)),
                  pl.BlockSpec((1, S, pw), lambda b, h, i: (b, 0, h)),
                  pl.BlockSpec((1, S, pw), lambda b, h, i: (b, 0, h))],
        out_specs=[pl.BlockSpec((1, t, pw), lambda b, h, i: (b, i, h)),
                   pl.BlockSpec((1, pair, t, 1), lambda b, h, i: (b, h, i, 0))],
        out_shape=[jax.ShapeDtypeStruct((B, S, HW), BF16), jax.ShapeDtypeStruct((B, H, S, 1), F32)],
        compiler_params=_cparams(("arbitrary", "arbitrary", "arbitrary")),
    )(q, k, v)


def _attn_bwd(q, k, v, o, do, lse):
    B, S, HW = q.shape
    H = HW // HEAD_LANES
    t = min(ATTN_BWD_TILE, S)
    nq = S // t

    pair = 2
    pw = pair * HEAD_LANES

    def body(q_ref, k_ref, v_ref, o_ref, do_ref, lse_ref, dq_out_ref, dk_ref, dv_ref, dq_ref):
        j = pl.program_id(2)

        @pl.when(j == 0)
        def _():
            dq_ref[...] = jnp.zeros_like(dq_ref)

        lanes = [slice(hh * HEAD_LANES, (hh + 1) * HEAD_LANES) for hh in range(pair)]

        def step(i, carry, masked):
            sl = pl.ds(pl.multiple_of(i * t, t), t)
            out = []
            for hh, (cols, (dk, dv)) in enumerate(zip(lanes, carry)):
                kj = k_ref[0, :, cols]
                qi = q_ref[0, sl, cols]
                doi = do_ref[0, sl, cols]
                s = _dot_nt(qi, kj)
                if masked:
                    s = jnp.where(_causal_mask(t), s, -1e30)
                p = jnp.exp(s - lse_ref[0, hh, sl, :])
                dv = dv + _dot_tn(p.astype(BF16), doi)
                dp = _dot_nt(doi, v_ref[0, :, cols])
                delta = jnp.sum(doi.astype(F32) * o_ref[0, sl, cols].astype(F32), axis=-1, keepdims=True)
                dsb = (p * (dp - delta)).astype(BF16)
                dk = dk + _dot_tn(dsb, qi)
                dq_ref[sl, cols] += _dot(dsb, kj)
                out.append((dk, dv))
            return tuple(out)

        zero = jnp.zeros((t, HEAD_LANES), F32)
        carry = step(j, ((zero, zero),) * pair, True)
        carry = lax.fori_loop(j + 1, nq, lambda i, c: step(i, c, False), carry)
        for cols, (dk, dv) in zip(lanes, carry):
            dk_ref[0, :, cols] = dk.astype(BF16)
            dv_ref[0, :, cols] = dv.astype(BF16)

        @pl.when(j == nq - 1)
        def _():
            dq_out_ref[0] = dq_ref[...].astype(BF16)

    full = pl.BlockSpec((1, S, pw), lambda b, h, j: (b, 0, h))
    tile = pl.BlockSpec((1, t, pw), lambda b, h, j: (b, j, h))
    sd = jax.ShapeDtypeStruct
    return pl.pallas_call(
        body, grid=(B, H // pair, nq), name="attn_bwd",
        in_specs=[full, tile, tile, full, full, pl.BlockSpec((1, pair, S, 1), lambda b, h, j: (b, h, 0, 0))],
        out_specs=[full, tile, tile],
        out_shape=[sd((B, S, HW), BF16), sd((B, S, HW), BF16), sd((B, S, HW), BF16)],
        scratch_shapes=[pltpu.VMEM((S, pw), F32)],
        compiler_params=_cparams(("arbitrary", "arbitrary", "arbitrary")),
    )(q, k, v, o, do, lse)


def _mix_out(x1, yssd, o, mw, wout, g, seq):
    T, D = x1.shape
    tm = min(WIDE_TOKEN_TILE, seq)
    tps = seq // tm

    def body(x_ref, ys_ref, o_ref, mw_ref, w_ref, g_ref, xo_ref, m_ref, yc_ref):
        xh, _ = _rms_stats(o_ref[...].astype(F32))
        ycat = jnp.concatenate([ys_ref[...], (xh * mw_ref[...]).astype(BF16)], axis=1)
        m = _dot(ycat, w_ref[...])
        xo_ref[...] = x_ref[...] + g_ref[0] * m
        m_ref[...] = m.astype(BF16)
        yc_ref[...] = ycat

    rows = lambda n: pl.BlockSpec((tm, n), lambda i: (i, 0))
    perb = pl.BlockSpec((1, 1, D), lambda i: (i // tps, 0, 0))
    sd = jax.ShapeDtypeStruct
    return pl.pallas_call(
        body, grid=(T // tm,), name="mix_out",
        in_specs=[rows(D), rows(D_SSD), rows(D_MLA), _resident((1, D_MLA)), _resident((D_SSD + D_MLA, D)), perb],
        out_specs=[rows(D), rows(D), rows(D_SSD + D_MLA)],
        out_shape=[sd((T, D), F32), sd((T, D), BF16), sd((T, D_SSD + D_MLA), BF16)],
        compiler_params=_cparams(("arbitrary",)),
    )(x1, yssd, o, mw, wout, g)


def _mix_out_bwd(dx2, m, o, mw, wout, g, seq):
    T, D = dx2.shape
    B = T // seq
    tm = min(WIDE_TOKEN_TILE, seq)
    tps = seq // tm

    def body(dx_ref, m_ref, o_ref, mw_ref, w_ref, g_ref, dys_ref, do_ref, dm_ref, dg_ref, dmw_ref):
        i = pl.program_id(0)

        @pl.when(i % tps == 0)
        def _():
            dg_ref[...] = jnp.zeros_like(dg_ref)

        @pl.when(i == 0)
        def _():
            dmw_ref[...] = jnp.zeros_like(dmw_ref)

        dxv = dx_ref[...]
        dg_ref[0] += _colsum(dxv * m_ref[...].astype(F32))
        dmb = (g_ref[0] * dxv).astype(BF16)
        dm_ref[...] = dmb
        dycat = _dot_nt(dmb, w_ref[...])
        dys_ref[...] = dycat[:, :D_SSD].astype(BF16)
        xh, r = _rms_stats(o_ref[...].astype(F32))
        dx, dw_rows = _rms_bwd(dycat[:, D_SSD:], xh, r, mw_ref[...])
        do_ref[...] = dx.astype(BF16)
        dmw_ref[...] += _colsum(dw_rows)

    rows = lambda n: pl.BlockSpec((tm, n), lambda i: (i, 0))
    perb = pl.BlockSpec((1, 1, D), lambda i: (i // tps, 0, 0))
    sd = jax.ShapeDtypeStruct
    return pl.pallas_call(
        body, grid=(T // tm,), name="mix_out_bwd",
        in_specs=[rows(D), rows(D), rows(D_MLA), _resident((1, D_MLA)), _resident((D_SSD + D_MLA, D)), perb],
        out_specs=[rows(D_SSD), rows(D_MLA), rows(D), perb, pl.BlockSpec((1, D_MLA), lambda i: (0, 0))],
        out_shape=[sd((T, D_SSD), BF16), sd((T, D_MLA), BF16), sd((T, D), BF16), sd((B, 1, D), F32), sd((1, D_MLA), F32)],
        compiler_params=_cparams(("arbitrary",)),
    )(dx2, m, o, mw, wout, g)


def _win_to_kernel(w):
    z0 = jnp.zeros((48, w.shape[1]), w.dtype)
    z1 = jnp.zeros((32, w.shape[1]), w.dtype)
    return jnp.concatenate([w[:2560], w[2576:3216], w[2560:2576], z0, w[3216:3248], z1], axis=0)


def _win_from_kernel(g):
    return jnp.concatenate([g[:2560], g[3200:3216], g[2560:3200], g[3264:3296]], axis=0)


def _wuq_to_kernel(w):
    w = w.reshape(Q_LORA, MLA_HEADS, QK_DIM)
    return jnp.pad(w, ((0, 0), (0, 0), (0, HEAD_LANES - QK_DIM))).reshape(Q_LORA, MLA_HEADS * HEAD_LANES)


def _wuq_from_kernel(g):
    return g.reshape(Q_LORA, MLA_HEADS, HEAD_LANES)[:, :, :QK_DIM].reshape(Q_LORA, MLA_HEADS * QK_DIM)


def _wukv_to_kernel(w):
    w = w.reshape(KV_LORA, MLA_HEADS, QK_NOPE + V_HEAD)
    kp = jnp.pad(w[:, :, :QK_NOPE], ((0, 0), (0, 0), (0, HEAD_LANES - QK_NOPE)))
    return jnp.concatenate([kp.reshape(KV_LORA, -1), w[:, :, QK_NOPE:].reshape(KV_LORA, -1)], axis=1)


def _wukv_from_kernel(g):
    hw = MLA_HEADS * HEAD_LANES
    kp = g[:, :hw].reshape(KV_LORA, MLA_HEADS, HEAD_LANES)[:, :, :QK_NOPE]
    vp = g[:, hw:].reshape(KV_LORA, MLA_HEADS, V_HEAD)
    return jnp.concatenate([kp, vp], axis=2).reshape(KV_LORA, MLA_HEADS * (QK_NOPE + V_HEAD))


def _lanes16(v):
    return jnp.pad(v.reshape(1, SSD_HEADS), ((0, 0), (0, LANES - SSD_HEADS)))


def _constants():
    e = np.zeros((LANES, D_SSD), np.float32)
    for h in range(SSD_HEADS):
        e[h, h * SSD_HEAD_DIM:(h + 1) * SSD_HEAD_DIM] = 1.0
    inv_freq = ROPE_THETA ** (-jnp.arange(0, QK_ROPE, 2, dtype=F32) / QK_ROPE)
    half = QK_ROPE // 2
    place = np.zeros((half, LANES), np.float32)
    for j in range(half):
        place[j, QK_NOPE + j] = place[j, QK_NOPE + half + j] = 1.0
    return jnp.asarray(e, BF16), inv_freq.reshape(half, 1), jnp.asarray(place, BF16)


def _local_step(x, positions, mod, w, later_weights, small, tgt, on_grads, sync, on_small):
    B, S, D = x.shape
    T = B * S
    expand, invf, place = _constants()
    x0 = x.reshape(T, D)
    pos = positions.reshape(T)
    mods = [mod[:, i * D:(i + 1) * D].reshape(B, 1, D) for i in range(N_MOD)]
    sh1, sc1, g1, sh2, sc2, g2, sh3, sc3, g3 = mods
    dtb, alog = _lanes16(small["dt_bias"]), _lanes16(small["a_log"])
    dsk = jnp.repeat(small["d_skip"].reshape(1, SSD_HEADS), SSD_HEAD_DIM, axis=1)

    x1, a1, u1, f1 = _ffn_fwd(x0, small["norm_ffn1"], sh1, sc1, g1, w["ffn1_w_gate"], w["ffn1_w_up"], w["ffn1_w_down"], S, "ffn1_fwd")
    w = {**w, **later_weights(f1)}
    z, xraw, cq, ckv, dtk = _inproj_fwd(x1, small["norm_mix"], sh2, sc2, w["w_in"], S)
    xraw3 = xraw.reshape(B, S, D_CONV)
    xbc = _conv_fwd(xraw3, small["conv_w"], small["conv_b"])
    dtk3, z3 = dtk.reshape(B, S, LANES), z.reshape(B, S, D_SSD)
    y, yssd, prev = _ssd_fwd(xbc, dtk3, z3, dtb, alog, dsk, small["ssd_norm_w"], expand)
    q, k, v, cos_t, sin_t = _mla_prep(cq, ckv, dtk, pos, small["q_norm_w"], small["kv_norm_w"], w["w_uq"], w["w_ukv"], invf,
                                      place)
    hw = MLA_HEADS * HEAD_LANES
    q3, k3, v3 = q.reshape(B, S, hw), k.reshape(B, S, hw), v.reshape(B, S, hw)
    o3, lse = _attn_fwd(q3, k3, v3)
    o = o3.reshape(T, hw)
    x2, m, ycat = _mix_out(x1, yssd.reshape(T, D_SSD), o, small["mla_norm_w"], w["w_out"], g2, S)
    dx3, a2, u2, f2, loss, d_norm_final = _ffn_fwd(
        x2, small["norm_ffn2"], sh3, sc3, g3, w["ffn2_w_gate"], w["ffn2_w_up"], w["ffn2_w_down"], S, "ffn2_fwd",
        head=(small["norm_final"].reshape(1, D), tgt.reshape(T, D)))

    gw, gs = {}, {}
    dx2, h3, s3, df3, da3, du3, dsh3, dsc3, dg3, gs["norm_ffn2"] = _ffn_bwd(
        dx3, x2, small["norm_ffn2"], sh3, sc3, g3, a2, u2, f2, w["ffn2_w_gate"], w["ffn2_w_up"], w["ffn2_w_down"], S, "ffn2_bwd")
    gw["ffn2_w_gate"], gw["ffn2_w_up"], gw["ffn2_w_down"] = _ffn_wgrad(h3, s3, df3, da3, du3, dsh3, "ffn2_wgrad")
    g2 = g2 + on_grads(("ffn2_w_gate", "ffn2_w_up", "ffn2_w_down"), gw)

    dys, do, dm, dg2, gs["mla_norm_w"] = _mix_out_bwd(dx2, m, o, small["mla_norm_w"], w["w_out"], g2, S)
    gw["w_out"] = _mm_tn(ycat, dm, 512, "dwout")

    dq3, dk3, dv3 = _attn_bwd(q3, k3, v3, o3, do.reshape(B, S, hw), lse)
    dcq, dckv, ddtk_b, qn, kvn, dqb, dkvb, gs["q_norm_w"], gs["kv_norm_w"] = _mla_prep_bwd(
        dq3.reshape(T, hw), dk3.reshape(T, hw), dv3.reshape(T, hw), cq, ckv, cos_t, sin_t, small["q_norm_w"] + sync(dq3),
        small["kv_norm_w"], w["w_uq"], w["w_ukv"])
    gw["w_uq"] = _mm_tn(qn, dqb, 512, "dwuq")
    gw["w_ukv"] = _mm_tn(kvn, dkvb, 1024, "dwukv")

    dxbc, ddtk_a, dz, gs["ssd_norm_w"], dvec = _ssd_bwd(
        xbc, dtk3, z3, y, prev, dys.reshape(B, S, D_SSD), dtb, alog, dsk, small["ssd_norm_w"], expand)
    gs["dt_bias"], gs["a_log"], gs["d_skip"] = dvec[0:1, :SSD_HEADS], dvec[1:2, :SSD_HEADS], dvec[2:3, :SSD_HEADS]
    dxraw, gs["conv_w"], gs["conv_b"] = _conv_bwd(dxbc, xraw3, small["conv_w"], small["conv_b"])
    dx1, h2, dproj, dsh2, dsc2, gs["norm_mix"] = _inproj_bwd(
        dx2, x1, small["norm_mix"], sh2, sc2, w["w_in"], dz.reshape(T, D_SSD), dxraw.reshape(T, D_CONV), dcq, dckv,
        ddtk_a.reshape(T, LANES), ddtk_b, S)
    gw["w_in"] = _mm_tn(dproj, h2, 512, "dwin")
    g1 = g1 + on_grads(("w_in", "w_uq", "w_ukv", "w_out"), gw)

    dx0, h1, s1, df1, da1, du1, dsh1, dsc1, dg1, gs["norm_ffn1"] = _ffn_bwd(
        dx1, x0, small["norm_ffn1"], sh1, sc1, g1, a1, u1, f1, w["ffn1_w_gate"], w["ffn1_w_up"], w["ffn1_w_down"], S, "ffn1_bwd")
    gs["norm_final"] = d_norm_final
    dmod = jnp.concatenate([t.reshape(B, D) for t in (dsh1, dsc1, dg1, dsh2, dsc2, dg2, dsh3, dsc3, dg3)], axis=1)
    small_started = on_small(loss, dmod, gs)
    gw["ffn1_w_gate"], gw["ffn1_w_up"], gw["ffn1_w_down"] = _ffn_wgrad(h1, s1, df1, da1, du1, dsh1 + sync(small_started),
                                                                       "ffn1_wgrad")
    return dx0.reshape(B, S, D), gw


HBM_SPEC = pl.BlockSpec(memory_space=pltpu.HBM)
VMEM_SPEC = pl.BlockSpec(memory_space=pltpu.VMEM)


def _place():
    return lax.axis_index("x"), lax.axis_index("y"), lax.axis_index("c")


def _other_chips(mx, my):
    return [(1 - mx, my), (mx, 1 - my), (1 - mx, 1 - my)]


def _remote(src, dst, send_sem, recv_sem, to):
    return pltpu.make_async_remote_copy(src_ref=src, dst_ref=dst, send_sem=send_sem, recv_sem=recv_sem,
                                        device_id=to, device_id_type=MESH)


def _all_gather_small(xa, name):
    r, n = xa.shape

    def body(x_ref, o_ref, token, send_sems, recv_sems):
        mx, my, mc = _place()
        me = 4 * mx + 2 * my + mc
        token[...] = jnp.zeros_like(token)
        o_ref[pl.ds(me, 1)] = x_ref[...][None]
        sends = []
        for k in range(1, N_DEV):
            peer = (mx ^ (k >> 2), my ^ ((k >> 1) & 1), mc ^ (k & 1))
            cp = _remote(x_ref, o_ref.at[me], send_sems.at[k - 1], recv_sems.at[k - 1], peer)
            cp.start()
            sends.append(cp)
        for k in range(1, N_DEV):
            peer = (mx ^ (k >> 2), my ^ ((k >> 1) & 1), mc ^ (k & 1))
            slot = 4 * peer[0] + 2 * peer[1] + peer[2]
            _remote(x_ref, o_ref.at[slot], send_sems.at[k - 1], recv_sems.at[k - 1], peer).wait_recv()
        for cp in sends:
            cp.wait_send()

    return pl.pallas_call(
        body, name=name, in_specs=[VMEM_SPEC], out_specs=[VMEM_SPEC, VMEM_SPEC],
        out_shape=[jax.ShapeDtypeStruct((N_DEV, r, n), xa.dtype), jax.ShapeDtypeStruct((8, LANES), F32)],
        scratch_shapes=[pltpu.SemaphoreType.DMA((N_DEV - 1,)), pltpu.SemaphoreType.DMA((N_DEV - 1,))],
        compiler_params=pltpu.CompilerParams(vmem_limit_bytes=VMEM_LIMIT),
    )(xa)


def _halves_by_rows(shape):
    return (shape[-2] // 2) % 16 == 0


def _half_shape(shape):
    r, c = shape[-2:]
    return tuple(shape[:-2]) + ((r // 2, c) if _halves_by_rows(shape) else (r, c // 2))


def _half_index(shape, hc):
    r, c = shape[-2:]
    if _halves_by_rows(shape):
        return (pl.ds(pl.multiple_of(hc * (r // 2), 16), r // 2), slice(None))
    return (slice(None), pl.ds(pl.multiple_of(hc * (c // 2), LANES), c // 2))


def _half(ref, hc, lead=None):
    idx = _half_index(ref.shape, hc)
    return ref.at[idx] if lead is None else ref.at[(lead,) + idx]


SEM_SPEC = pl.BlockSpec(memory_space=pltpu.SEMAPHORE)
ANY_SPEC = pl.BlockSpec(memory_space=pl.ANY)
DATAFLOW = pltpu.SideEffectType.DATAFLOW_SIDE_EFFECTING


def _hbm(arr):
    return pltpu.with_memory_space_constraint(arr, pltpu.HBM)


def _gather_start(shards, tag):
    n = len(shards)

    def body(*refs):
        w_refs, land_refs, send_sems, recv_sems, token = refs[:n], refs[n:2 * n], refs[2 * n], refs[2 * n + 1], refs[-1]
        mx, my, mc = _place()
        chip = 2 * mx + my
        for i, (w, land) in enumerate(zip(w_refs, land_refs)):
            for k, (cx, cy) in enumerate(_other_chips(mx, my)):
                _remote(_half(w, mc), _half(land, mc, chip), send_sems.at[3 * i + k],
                        recv_sems.at[3 * i + k], (cx, cy, mc)).start()
        token[...] = jnp.zeros_like(token)

    lands = [lax.empty((N_CHIPS,) + s.shape, s.dtype) for s in shards]
    out = pl.pallas_call(
        body, name="gather_start_" + tag,
        out_shape=(pltpu.SemaphoreType.DMA((3 * n,)), pltpu.SemaphoreType.DMA((3 * n,)),
                   *[pltpu.HBM(s.shape, s.dtype) for s in shards], *[pltpu.HBM(l.shape, l.dtype) for l in lands],
                   jax.ShapeDtypeStruct((8, LANES), F32)),
        in_specs=[HBM_SPEC] * (2 * n), out_specs=(SEM_SPEC, SEM_SPEC, *[HBM_SPEC] * (2 * n), VMEM_SPEC),
        input_output_aliases={i: 2 + i for i in range(2 * n)},
        compiler_params=pltpu.CompilerParams(has_side_effects=DATAFLOW),
    )(*[_hbm(s) for s in shards], *[_hbm(l) for l in lands])
    return out[0], out[1], out[2:2 + n], out[2 + n:2 + 2 * n], out[-1]


def _gather_wait(send_sems, recv_sems, shards, lands, after, tag):
    n = len(shards)

    def body(*refs):
        w_refs, land_refs, send_sems, recv_sems = refs[:n], refs[n:2 * n], refs[2 * n], refs[2 * n + 1]
        mx, my, mc = _place()
        for i, (w, land) in enumerate(zip(w_refs, land_refs)):
            for k, (cx, cy) in enumerate(_other_chips(mx, my)):
                cp = _remote(_half(w, mc), _half(land, mc, 2 * cx + cy), send_sems.at[3 * i + k],
                             recv_sems.at[3 * i + k], (cx, cy, mc))
                cp.wait_send()
                cp.wait_recv()

    out = pl.pallas_call(
        body, name="gather_wait_" + tag,
        out_shape=(*[pltpu.HBM(s.shape, s.dtype) for s in shards], *[pltpu.HBM(l.shape, l.dtype) for l in lands]),
        in_specs=[HBM_SPEC] * (2 * n) + [SEM_SPEC, SEM_SPEC, ANY_SPEC], out_specs=tuple([HBM_SPEC] * (2 * n)),
        input_output_aliases={i: i for i in range(2 * n)},
        compiler_params=pltpu.CompilerParams(has_side_effects=DATAFLOW),
    )(*shards, *lands, send_sems, recv_sems, after)
    return out[n:]


def _gather_finish(shards, lands, tag):
    n = len(shards)

    def body(*refs):
        w_refs, land_refs, o_refs, token = refs[:n], refs[n:2 * n], refs[2 * n:3 * n], refs[3 * n]
        send_sems, recv_sems, stage_sems = refs[3 * n + 1:3 * n + 4]
        stages = refs[3 * n + 4:]
        token[...] = jnp.zeros_like(token)
        mx, my, mc = _place()
        chip = 2 * mx + my
        others = _other_chips(mx, my)
        sibling = (mx, my, 1 - mc)
        stage_in = [pltpu.make_async_copy(w, st, stage_sems.at[0, i]) for i, (w, st) in enumerate(zip(w_refs, stages))]
        for cp in stage_in:
            cp.start()
        passed = []
        for i, (w, o) in enumerate(zip(w_refs, o_refs)):
            for k, (cx, cy) in enumerate(others):
                landed = _half(o, mc, 2 * cx + cy)
                passed.append(_remote(landed, landed, send_sems.at[i, k], recv_sems.at[i, k], sibling))
                passed[-1].start()
        stage_out = []
        for i, (st, o) in enumerate(zip(stages, o_refs)):
            stage_in[i].wait()
            stage_out.append(pltpu.make_async_copy(st, o.at[chip], stage_sems.at[1, i]))
            stage_out[-1].start()
        for i, (w, o) in enumerate(zip(w_refs, o_refs)):
            for k, (cx, cy) in enumerate(others):
                there = _half(o, 1 - mc, 2 * cx + cy)
                _remote(there, there, send_sems.at[i, k], recv_sems.at[i, k], sibling).wait_recv()
        for cp in passed:
            cp.wait_send()
        for cp in stage_out:
            cp.wait()

    out = pl.pallas_call(
        body, name="gather_finish_" + tag, in_specs=[HBM_SPEC] * (2 * n), out_specs=[HBM_SPEC] * n + [VMEM_SPEC],
        out_shape=[jax.ShapeDtypeStruct(l.shape, l.dtype) for l in lands] + [jax.ShapeDtypeStruct((8, LANES), F32)],
        input_output_aliases={n + i: i for i in range(n)},
        scratch_shapes=[pltpu.SemaphoreType.DMA((n, 3)), pltpu.SemaphoreType.DMA((n, 3)), pltpu.SemaphoreType.DMA((2, n))]
        + [pltpu.VMEM(s.shape, s.dtype) for s in shards],
        compiler_params=pltpu.CompilerParams(vmem_limit_bytes=VMEM_LIMIT),
    )(*shards, *lands)
    return out[:n], out[n]


def _scatter_start(ss, tag):
    n = len(ss)

    def body(*refs):
        s_refs, land_refs, send_sems, recv_sems, token = refs[:n], refs[n:2 * n], refs[2 * n], refs[2 * n + 1], refs[-1]
        mx, my, mc = _place()
        chip = 2 * mx + my
        for i, (s, land) in enumerate(zip(s_refs, land_refs)):
            for k, (cx, cy) in enumerate(_other_chips(mx, my)):
                _remote(s.at[2 * cx + cy], land.at[chip], send_sems.at[3 * i + k], recv_sems.at[3 * i + k],
                        (cx, cy, mc)).start()
        token[...] = jnp.zeros_like(token)

    lands = [lax.empty(s.shape, s.dtype) for s in ss]
    out = pl.pallas_call(
        body, name="scatter_start_" + tag,
        out_shape=(pltpu.SemaphoreType.DMA((3 * n,)), pltpu.SemaphoreType.DMA((3 * n,)),
                   *[pltpu.HBM(s.shape, s.dtype) for s in ss], *[pltpu.HBM(l.shape, l.dtype) for l in lands],
                   jax.ShapeDtypeStruct((8, LANES), F32)),
        in_specs=[HBM_SPEC] * (2 * n), out_specs=(SEM_SPEC, SEM_SPEC, *[HBM_SPEC] * (2 * n), VMEM_SPEC),
        input_output_aliases={i: 2 + i for i in range(2 * n)},
        compiler_params=pltpu.CompilerParams(has_side_effects=DATAFLOW),
    )(*[_hbm(s) for s in ss], *[_hbm(l) for l in lands])
    return out[0], out[1], out[2:2 + n], out[2 + n:2 + 2 * n], out[-1]


def _scatter_wait(send_sems, recv_sems, ss, lands, after, tag):
    n = len(ss)

    def body(*refs):
        s_refs, land_refs, send_sems, recv_sems = refs[:n], refs[n:2 * n], refs[2 * n], refs[2 * n + 1]
        mx, my, mc = _place()
        for i, (s, land) in enumerate(zip(s_refs, land_refs)):
            for k, (cx, cy) in enumerate(_other_chips(mx, my)):
                slot = land.at[2 * cx + cy]
                cp = _remote(s.at[2 * cx + cy], slot, send_sems.at[3 * i + k], recv_sems.at[3 * i + k], (cx, cy, mc))
                cp.wait_send()
                cp.wait_recv()

    out = pl.pallas_call(
        body, name="scatter_wait_" + tag,
        out_shape=(*[pltpu.HBM(s.shape, s.dtype) for s in ss], *[pltpu.HBM(l.shape, l.dtype) for l in lands]),
        in_specs=[HBM_SPEC] * (2 * n) + [SEM_SPEC, SEM_SPEC, ANY_SPEC], out_specs=tuple([HBM_SPEC] * (2 * n)),
        input_output_aliases={i: i for i in range(2 * n)},
        compiler_params=pltpu.CompilerParams(has_side_effects=DATAFLOW),
    )(*ss, *lands, send_sems, recv_sems, after)
    return out[:n], out[n:]


def _swap_start(gs, tag):
    n = len(gs)

    def body(*refs):
        g_refs, land_refs, send_sems, recv_sems, token = refs[:n], refs[n:2 * n], refs[2 * n], refs[2 * n + 1], refs[-1]
        mx, my, mc = _place()
        for i, (g, land) in enumerate(zip(g_refs, land_refs)):
            src = g.at[(slice(None),) + _half_index(g.shape, 1 - mc)]
            _remote(src, land, send_sems.at[i], recv_sems.at[i], (mx, my, 1 - mc)).start()
        token[...] = jnp.zeros_like(token)

    lands = [lax.empty(_half_shape(g.shape), g.dtype) for g in gs]
    out = pl.pallas_call(
        body, name="swap_start_" + tag,
        out_shape=(pltpu.SemaphoreType.DMA((n,)), pltpu.SemaphoreType.DMA((n,)), *[pltpu.HBM(g.shape, g.dtype) for g in gs],
                   *[pltpu.HBM(l.shape, l.dtype) for l in lands], jax.ShapeDtypeStruct((8, LANES), F32)),
        in_specs=[HBM_SPEC] * (2 * n), out_specs=(SEM_SPEC, SEM_SPEC, *[HBM_SPEC] * (2 * n), VMEM_SPEC),
        input_output_aliases={i: 2 + i for i in range(2 * n)},
        compiler_params=pltpu.CompilerParams(has_side_effects=DATAFLOW),
    )(*[_hbm(g) for g in gs], *[_hbm(l) for l in lands])
    return out[0], out[1], out[2:2 + n], out[2 + n:2 + 2 * n], out[-1]


def _swap_wait(send_sems, recv_sems, gs, lands, after, tag):
    n = len(gs)

    def body(*refs):
        g_refs, land_refs, send_sems, recv_sems = refs[:n], refs[n:2 * n], refs[2 * n], refs[2 * n + 1]
        mx, my, mc = _place()
        for i, (g, land) in enumerate(zip(g_refs, land_refs)):
            src = g.at[(slice(None),) + _half_index(g.shape, 1 - mc)]
            cp = _remote(src, land, send_sems.at[i], recv_sems.at[i], (mx, my, 1 - mc))
            cp.wait_send()
            cp.wait_recv()

    out = pl.pallas_call(
        body, name="swap_wait_" + tag,
        out_shape=(*[pltpu.HBM(g.shape, g.dtype) for g in gs], *[pltpu.HBM(l.shape, l.dtype) for l in lands]),
        in_specs=[HBM_SPEC] * (2 * n) + [SEM_SPEC, SEM_SPEC, ANY_SPEC], out_specs=tuple([HBM_SPEC] * (2 * n)),
        input_output_aliases={i: i for i in range(2 * n)},
        compiler_params=pltpu.CompilerParams(has_side_effects=DATAFLOW),
    )(*gs, *lands, send_sems, recv_sems, after)
    return out[:n], out[n:]


def _small_peers(mx, my, mc):
    peers = [(mx ^ (k >> 2), my ^ ((k >> 1) & 1), mc ^ (k & 1)) for k in range(1, N_DEV)]
    return [(peer, 4 * peer[0] + 2 * peer[1] + peer[2]) for peer in peers]


def _gather_small_start(xa):
    def body(x_ref, land_ref, send_sems, recv_sems, x_out, land_out, token):
        mx, my, mc = _place()
        me = 4 * mx + 2 * my + mc
        pltpu.make_async_copy(x_ref, land_ref.at[me], send_sems.at[N_DEV - 1]).start()
        for k, (peer, _) in enumerate(_small_peers(mx, my, mc)):
            _remote(x_ref, land_ref.at[me], send_sems.at[k], recv_sems.at[k], peer).start()
        token[...] = jnp.zeros_like(token)

    land = lax.empty((N_DEV,) + xa.shape, xa.dtype)
    return pl.pallas_call(
        body, name="gather_small_start",
        out_shape=(pltpu.SemaphoreType.DMA((N_DEV,)), pltpu.SemaphoreType.DMA((N_DEV,)), pltpu.HBM(xa.shape, xa.dtype),
                   pltpu.HBM(land.shape, land.dtype), jax.ShapeDtypeStruct((8, LANES), F32)),
        in_specs=[HBM_SPEC] * 2, out_specs=(SEM_SPEC, SEM_SPEC, HBM_SPEC, HBM_SPEC, VMEM_SPEC),
        input_output_aliases={0: 2, 1: 3},
        compiler_params=pltpu.CompilerParams(has_side_effects=DATAFLOW),
    )(_hbm(xa), _hbm(land))


def _gather_small_wait(send_sems, recv_sems, xa, land, after):
    def body(x_ref, land_ref, send_sems, recv_sems, after_ref, x_out, land_out):
        mx, my, mc = _place()
        me = 4 * mx + 2 * my + mc
        pltpu.make_async_copy(x_ref, land_ref.at[me], send_sems.at[N_DEV - 1]).wait()
        for k, (peer, slot) in enumerate(_small_peers(mx, my, mc)):
            cp = _remote(x_ref, land_ref.at[slot], send_sems.at[k], recv_sems.at[k], peer)
            cp.wait_send()
            cp.wait_recv()

    out = pl.pallas_call(
        body, name="gather_small_wait",
        out_shape=(pltpu.HBM(xa.shape, xa.dtype), pltpu.HBM(land.shape, land.dtype)),
        in_specs=[HBM_SPEC] * 2 + [SEM_SPEC, SEM_SPEC, ANY_SPEC], out_specs=(HBM_SPEC, HBM_SPEC),
        input_output_aliases={0: 0, 1: 1},
        compiler_params=pltpu.CompilerParams(has_side_effects=DATAFLOW),
    )(xa, land, send_sems, recv_sems, after)
    return out[1]


def _pair_sums(gs, gots, name):
    n = len(gs)

    def body(*refs):
        g_refs, got_refs, o_refs, load_sems, store_sems = refs[:n], refs[n:2 * n], refs[2 * n:3 * n], refs[3 * n], refs[3 * n + 1]
        mine, theirs, sums = refs[3 * n + 2:4 * n + 2], refs[4 * n + 2:5 * n + 2], refs[5 * n + 2:]
        mc = lax.axis_index("c")
        loads = []
        for i, (g, got) in enumerate(zip(g_refs, got_refs)):
            loads.append((pltpu.make_async_copy(g.at[(slice(None),) + _half_index(g.shape, mc)], mine[i], load_sems.at[i, 0]),
                          pltpu.make_async_copy(got, theirs[i], load_sems.at[i, 1])))
            for cp in loads[-1]:
                cp.start()
        stores = []
        for i in range(n):
            for cp in loads[i]:
                cp.wait()
            sums[i][...] = (mine[i][...].astype(F32) + theirs[i][...].astype(F32)).astype(BF16)
            stores.append(pltpu.make_async_copy(sums[i], o_refs[i], store_sems.at[i]))
            stores[-1].start()
        for cp in stores:
            cp.wait()

    halves = [jax.ShapeDtypeStruct(got.shape, BF16) for got in gots]
    return pl.pallas_call(
        body, name=name, in_specs=[HBM_SPEC] * (2 * n), out_specs=[HBM_SPEC] * n, out_shape=halves,
        scratch_shapes=[pltpu.SemaphoreType.DMA((n, 2)), pltpu.SemaphoreType.DMA((n,))]
        + [pltpu.VMEM(got.shape, g.dtype) for g, got in zip(gs, gots)] + [pltpu.VMEM(got.shape, got.dtype) for got in gots]
        + [pltpu.VMEM(got.shape, BF16) for got in gots],
        compiler_params=pltpu.CompilerParams(vmem_limit_bytes=VMEM_LIMIT),
    )(*gs, *gots)


def _reduce_join(owns, gots, name):
    n = len(owns)

    def body(*refs):
        own_refs, got_refs, mine_refs, theirs_refs = refs[:n], refs[n:2 * n], refs[2 * n:3 * n], refs[3 * n:4 * n]
        send_sems, recv_sems, load_sems, store_sems = refs[4 * n:4 * n + 4]
        parts, sums = refs[4 * n + 4:5 * n + 4], refs[5 * n + 4:]
        mx, my, mc = _place()
        chip = 2 * mx + my
        loads = []
        for i, (own, got, part) in enumerate(zip(own_refs, got_refs, parts)):
            loads.append([pltpu.make_async_copy((own if k == 0 else got).at[chip ^ k], part.at[k], load_sems.at[i, k])
                          for k in range(N_CHIPS)])
            for cp in loads[-1]:
                cp.start()
        out = []
        for i, (part, total, mine, theirs) in enumerate(zip(parts, sums, mine_refs, theirs_refs)):
            for cp in loads[i]:
                cp.wait()
            total[...] = ((part[0].astype(F32) + part[1].astype(F32)) + part[2].astype(F32)) + part[3].astype(F32)
            out.append(pltpu.make_async_copy(total, mine, store_sems.at[i]))
            out.append(_remote(total, theirs, send_sems.at[i], recv_sems.at[i], (mx, my, 1 - mc)))
            out[-2].start()
            out[-1].start()
        for cp in out:
            cp.wait()

    halves = [jax.ShapeDtypeStruct(o.shape[1:], F32) for o in owns]
    out = pl.pallas_call(
        body, name=name, in_specs=[HBM_SPEC] * (2 * n), out_specs=[HBM_SPEC] * (2 * n), out_shape=halves + halves,
        scratch_shapes=[pltpu.SemaphoreType.DMA((n,)), pltpu.SemaphoreType.DMA((n,)), pltpu.SemaphoreType.DMA((n, N_CHIPS)),
                        pltpu.SemaphoreType.DMA((n,))]
        + [pltpu.VMEM(o.shape, o.dtype) for o in owns] + [pltpu.VMEM(o.shape[1:], F32) for o in owns],
        compiler_params=pltpu.CompilerParams(vmem_limit_bytes=VMEM_LIMIT),
    )(*owns, *gots)
    return out[:n], out[n:]


def _adam_math(w, g, m, v):
    m2 = ADAM_B1 * m + (1.0 - ADAM_B1) * g
    v2 = ADAM_B2 * v + (1.0 - ADAM_B2) * (g * g)
    m_hat = m2 * (1.0 / (1.0 - ADAM_B1 ** ADAM_STEP))
    v_hat = v2 * (1.0 / (1.0 - ADAM_B2 ** ADAM_STEP))
    delta = -ADAM_LR * (m_hat / (jnp.sqrt(v_hat) + ADAM_EPS) + ADAM_WD * w)
    return delta, m2, v2


def _adam(w, g, m, v, name):
    def body(w_ref, g_ref, m_ref, v_ref, d_ref, m2_ref, v2_ref):
        d_ref[...], m2_ref[...], v2_ref[...] = _adam_math(w_ref[...], g_ref[...], m_ref[...], v_ref[...])

    return pl.pallas_call(body, name=name, out_shape=[jax.ShapeDtypeStruct(w.shape, F32)] * 3)(w, g, m, v)


def _adam_halves(w, m, v, mine, theirs, core, name):
    hr, hcols = _half_shape(w.shape)[1:]
    by_rows = _halves_by_rows(w.shape)

    def body(core_ref, w_ref, m_ref, v_ref, mine_ref, theirs_ref, g_ref, d_ref, m2_ref, v2_ref):
        g = jnp.where(pl.program_id(0) == core_ref[0], mine_ref[...], theirs_ref[...])
        g_ref[0] = g
        d_ref[0], m2_ref[0], v2_ref[0] = _adam_math(w_ref[0], g, m_ref[0], v_ref[0])

    half = pl.BlockSpec((1, hr, hcols), lambda hc, core_ref: (0, hc, 0) if by_rows else (0, 0, hc))
    whole = pl.BlockSpec((hr, hcols), lambda hc, core_ref: (0, 0))
    return pl.pallas_call(
        body, name=name,
        grid_spec=pltpu.PrefetchScalarGridSpec(
            num_scalar_prefetch=1, grid=(2,), in_specs=[half, half, half, whole, whole], out_specs=[half] * 4),
        out_shape=[jax.ShapeDtypeStruct(w.shape, F32)] * 4,
        compiler_params=_cparams(("arbitrary",)),
    )(core, w, m, v, mine, theirs)


ADA_COLS = N_MOD * D_MODEL // N_CHIPS


def _ada_fwd(c_all, w_ada, b_cols):
    def body(c_ref, w_ref, b_ref, o_ref):
        cv = c_ref[...]
        act = (cv * _sigmoid(cv)).astype(BF16)
        o_ref[...] = _dot(act, w_ref[...].astype(BF16)) + b_ref[...]

    return pl.pallas_call(
        body, name="ada_fwd", out_shape=jax.ShapeDtypeStruct((c_all.shape[0], ADA_COLS), F32),
        compiler_params=pltpu.CompilerParams(vmem_limit_bytes=VMEM_LIMIT),
    )(c_all, w_ada, b_cols)


def _ada_bwd(c_all, dmod_cols, w, m, v):
    nb = c_all.shape[0]
    tn = 384

    def body(c_ref, d_ref, w_ref, m_ref, v_ref, g_ref, dl_ref, m2_ref, v2_ref):
        cv = c_ref[...]
        act = (cv * _sigmoid(cv)).astype(BF16)
        g = _dot_tn(act, d_ref[...].astype(BF16))
        g_ref[...] = g
        dl_ref[...], m2_ref[...], v2_ref[...] = _adam_math(w_ref[...], g, m_ref[...], v_ref[...])

    blk = pl.BlockSpec((D_MODEL, tn), lambda j: (0, j))
    return pl.pallas_call(
        body, name="ada_bwd", grid=(ADA_COLS // tn,),
        in_specs=[pl.BlockSpec((nb, D_MODEL), lambda j: (0, 0)), pl.BlockSpec((nb, tn), lambda j: (0, j)), blk, blk, blk],
        out_specs=[blk] * 4, out_shape=[jax.ShapeDtypeStruct((D_MODEL, ADA_COLS), F32)] * 4,
        compiler_params=_cparams(("arbitrary",)),
    )(c_all, dmod_cols, w, m, v)


SMALL_NAMES = ("norm_ffn1", "norm_mix", "conv_w", "conv_b", "ssd_norm_w", "q_norm_w", "kv_norm_w", "mla_norm_w",
               "norm_ffn2", "norm_final", "dt_bias", "a_log", "d_skip")
SMALL_SIZES = (1024, 1024, CONV_WIDTH * D_CONV, D_CONV, 1024, Q_LORA, KV_LORA, 1024, 1024, 1024, 16, 16, 16)
SMALL_ROWS = 16
MOD_ROWS = 2 * N_MOD
SEND_ROWS = 40


def _pack_small(parts):
    flat = jnp.concatenate([parts[n].reshape(-1) for n in SMALL_NAMES])
    return jnp.pad(flat, (0, SMALL_ROWS * D_MODEL - flat.shape[0]))


def _unpack_small(flat):
    out, off = {}, 0
    for n, size in zip(SMALL_NAMES, SMALL_SIZES):
        out[n] = flat[off:off + size]
        off += size
    return out


def _small_sum(got):
    def body(g_ref, o_ref):
        bsum = jnp.zeros((N_MOD, D_MODEL), F32)
        ssum = jnp.zeros((SMALL_ROWS, D_MODEL), F32)
        for d in range(N_DEV):
            bsum = bsum + g_ref[d, 0:N_MOD, :] + g_ref[d, N_MOD:MOD_ROWS, :]
            ssum = ssum + g_ref[d, MOD_ROWS:MOD_ROWS + SMALL_ROWS, :]
        o_ref[...] = jnp.concatenate([bsum, ssum, jnp.zeros((32 - N_MOD - SMALL_ROWS, D_MODEL), F32)], axis=0)

    return pl.pallas_call(body, name="small_sum", out_shape=jax.ShapeDtypeStruct((32, D_MODEL), F32))(got)


BIG_NAMES = ("ffn1_w_gate", "ffn1_w_up", "ffn1_w_down", "w_in", "w_uq", "w_ukv", "w_out", "ffn2_w_gate", "ffn2_w_up",
             "ffn2_w_down")
_TO_KERNEL = {"w_in": _win_to_kernel, "w_uq": _wuq_to_kernel, "w_ukv": _wukv_to_kernel}
_FROM_KERNEL = {"w_in": _win_from_kernel, "w_uq": _wuq_from_kernel, "w_ukv": _wukv_from_kernel}


def _columns_joined(w4):
    n, r, c = w4.shape
    return w4.transpose(1, 0, 2).reshape(r, n * c)


def _columns_split(g):
    r, cols = g.shape
    return g.reshape(r, N_CHIPS, cols // N_CHIPS).transpose(1, 0, 2)


def kernel(x, c, positions, w_ada, b_ada, norm_ffn1, ffn1_w_gate, ffn1_w_up, ffn1_w_down, norm_mix, w_in, conv_w, conv_b, dt_bias, a_log, d_skip, ssd_norm_w, q_norm_w, w_uq, kv_norm_w, w_ukv, mla_norm_w, w_out, norm_ffn2, ffn2_w_gate, ffn2_w_up, ffn2_w_down, norm_final, loss_target, m_w_ada, m_b_ada, m_norm_ffn1, m_ffn1_w_gate, m_ffn1_w_up, m_ffn1_w_down, m_norm_mix, m_w_in, m_conv_w, m_conv_b, m_dt_bias, m_a_log, m_d_skip, m_ssd_norm_w, m_q_norm_w, m_w_uq, m_kv_norm_w, m_w_ukv, m_mla_norm_w, m_w_out, m_norm_ffn2, m_ffn2_w_gate, m_ffn2_w_up, m_ffn2_w_down, m_norm_final, v_w_ada, v_b_ada, v_norm_ffn1, v_ffn1_w_gate, v_ffn1_w_up, v_ffn1_w_down, v_norm_mix, v_w_in, v_conv_w, v_conv_b, v_dt_bias, v_a_log, v_d_skip, v_ssd_norm_w, v_q_norm_w, v_w_uq, v_kv_norm_w, v_w_ukv, v_mla_norm_w, v_w_out, v_norm_ffn2, v_ffn2_w_gate, v_ffn2_w_up, v_ffn2_w_down, v_norm_final):
    a = dict(locals())
    held_transposed = ("ffn1_w_gate", "ffn1_w_up", "ffn2_w_gate", "ffn2_w_up", "w_in")
    for n in held_transposed:
        for p in ("", "m_", "v_"):
            a[p + n] = a[p + n].transpose(0, 2, 1)
    B, S, D = x.shape
    mx, my, mc = _place()
    chip = 2 * mx + my
    dev = 2 * chip + mc
    core = mc.astype(jnp.int32).reshape(1)

    cw_rows = jnp.pad(conv_w[0], ((0, 0), (0, D - conv_w.shape[2])))
    got, c_done = _all_gather_small(jnp.concatenate([c, cw_rows, jnp.zeros((8 - B - CONV_WIDTH, D), F32)], axis=0), "gather_c")
    c_all = got[:, :B, :].reshape(N_DEV * B, D)

    first = ("ffn1_w_gate", "ffn1_w_up", "ffn1_w_down")
    later = tuple(n for n in BIG_NAMES if n not in first)
    first_flight = _gather_start([(a[n][0] + c_done[0, 0]).astype(BF16) for n in first], "first")
    conv_full = got[::2, B:B + CONV_WIDTH, :conv_w.shape[2]].transpose(1, 0, 2).reshape(CONV_WIDTH, D_CONV)

    b_cols = lax.dynamic_slice(b_ada, (0, chip * ADA_COLS), (1, ADA_COLS))
    mod_all, mod_done = _all_gather_small(_ada_fwd(c_all, w_ada[0], b_cols) + first_flight[4][0, 0], "gather_mod")
    mod = lax.dynamic_slice(mod_all, (0, B * dev, 0), (N_DEV, B, ADA_COLS))[::2].transpose(1, 0, 2).reshape(B, N_MOD * D)

    send_sems, recv_sems, shards, lands, _ = first_flight
    lands = _gather_wait(send_sems, recv_sems, shards, lands, mod_done, "first")
    got_first, gathered = _gather_finish([a[n][0].astype(BF16) for n in first], lands, "first")
    w = dict(zip(first, got_first))
    in_flight = _gather_start([(a[n][0] + gathered[0, 0]).astype(BF16) for n in later], "later")

    def later_weights(after):
        send_sems, recv_sems, shards, lands, _ = in_flight
        lands = _gather_wait(send_sems, recv_sems, shards, lands, after, "later")
        wl = dict(zip(later, _gather_finish([a[n][0].astype(BF16) for n in later], lands, "later")[0]))
        for n, to_kernel in _TO_KERNEL.items():
            wl[n] = to_kernel(wl[n].reshape(-1, D) if n in held_transposed else _columns_joined(wl[n]))
        wl["w_out"] = wl["w_out"].reshape(D_SSD + D_MLA, D)
        return wl

    small = {n: a[n].reshape(1, -1) for n in SMALL_NAMES if n not in ("conv_w", "norm_final")}
    small["conv_w"], small["norm_final"] = conv_full, norm_final

    def shards_of(names, gw):
        g4 = []
        for n in names:
            g = gw[n]
            if n in _FROM_KERNEL:
                g = _FROM_KERNEL[n](g) if n in held_transposed else _columns_split(_FROM_KERNEL[n](g))
            g4.append(g.reshape(N_CHIPS, a[n].shape[1], a[n].shape[2]))
        return g4

    def scatter_group(names, g4, swapped):
        pair = _pair_sums(g4, swapped, "pair_sums_" + names[0])
        return (names,) + tuple(_scatter_start(pair, names[0]))

    grads, deltas, new_m, new_v = {}, {}, {}, {}

    def finish_groups(some, after):
        names, owns, gots = [], [], []
        for group_names, send_sems, recv_sems, pair, lands, _ in some:
            pair, lands = _scatter_wait(send_sems, recv_sems, pair, lands, after, group_names[0])
            names += group_names
            owns += pair
            gots += lands
        mine, theirs = _reduce_join(owns, gots, "reduce_join_" + names[0])
        for n, own, other in zip(names, mine, theirs):
            grads[n], deltas[n], new_m[n], new_v[n] = _adam_halves(a[n], a["m_" + n], a["v_" + n], own, other, core, "adam_" + n)
        return deltas[names[-1]]

    swapping, groups = [], []

    def on_grads(names, gw):
        send_sems, recv_sems, g4, lands, token = _swap_start(shards_of(names, gw), names[0])
        swapping.append((names, send_sems, recv_sems, g4, lands))
        return token[0, 0]

    def sync(after):
        token = 0.0
        while swapping:
            names, send_sems, recv_sems, g4, lands = swapping.pop(0)
            g4, swapped = _swap_wait(send_sems, recv_sems, g4, lands, after, names[0])
            groups.append(scatter_group(names, g4, swapped))
            token = groups[-1][5][0, 0]
        return token

    small_flight = []

    def on_small(loss_blk, dmod, gs):
        small_flat = _pack_small(gs).at[-1].set(loss_blk[0, 0])
        send = jnp.concatenate([dmod.reshape(MOD_ROWS, D), small_flat.reshape(SMALL_ROWS, D),
                                jnp.zeros((SEND_ROWS - MOD_ROWS - SMALL_ROWS, D), F32)], axis=0)
        small_flight.extend(_gather_small_start(send))
        return small_flight[4]

    grad_x, gw = _local_step(x, positions, mod + in_flight[4][0, 0], w, later_weights, small, loss_target, on_grads, sync,
                             on_small)

    swap_started = on_grads(first, gw)

    got = _gather_small_wait(*small_flight[:4], jnp.full((8, LANES), swap_started))
    summed = _small_sum(got)
    sums = summed[N_MOD:N_MOD + SMALL_ROWS].reshape(-1)
    loss = sums[-1]
    gsmall = _unpack_small(sums)
    gsmall["conv_w"] = lax.dynamic_slice(gsmall["conv_w"].reshape(CONV_WIDTH, D_CONV), (0, chip * conv_w.shape[2]),
                                         (CONV_WIDTH, conv_w.shape[2]))
    gsmall["b_ada"] = summed[:N_MOD]
    names = ("b_ada",) + SMALL_NAMES
    rows = 208

    def pack(parts):
        flat = jnp.concatenate([parts[n].reshape(-1) for n in names])
        return jnp.pad(flat, (0, rows * LANES - flat.shape[0])).reshape(rows, LANES)

    packed = [pack({n: a[p + n] for n in names}) for p in ("", "m_", "v_")]
    g_p = pack(gsmall)
    outs = (g_p,) + tuple(_adam(packed[0], g_p, packed[1], packed[2], "adam_small"))
    for dst, flat in zip((grads, deltas, new_m, new_v), outs):
        flat, off = flat.reshape(-1), 0
        for n in names:
            dst[n] = flat[off:off + a[n].size].reshape(a[n].shape)
            off += a[n].size

    dmod_all = got[:, :MOD_ROWS, :].reshape(N_DEV * B, N_MOD * D)
    dmod_cols = lax.dynamic_slice(dmod_all, (0, chip * ADA_COLS), (N_DEV * B, ADA_COLS))
    ada = _ada_bwd(c_all, dmod_cols, w_ada[0], m_w_ada[0], v_w_ada[0])
    for dst, t in zip((grads, deltas, new_m, new_v), ada):
        dst["w_ada"] = t[None]

    sync(ada[0])
    last = groups.pop()
    finish_groups([last], finish_groups(groups, last[5]))
    for dst in (grads, deltas, new_m, new_v):
        for n in held_transposed:
            dst[n] = dst[n].transpose(0, 2, 1)

    order = ("w_ada", "b_ada", "norm_ffn1", "ffn1_w_gate", "ffn1_w_up", "ffn1_w_down", "norm_mix", "w_in", "conv_w", "conv_b",
             "dt_bias", "a_log", "d_skip", "ssd_norm_w", "q_norm_w", "w_uq", "kv_norm_w", "w_ukv", "mla_norm_w", "w_out",
             "norm_ffn2", "ffn2_w_gate", "ffn2_w_up", "ffn2_w_down", "norm_final")
    return (loss, grad_x, *[grads[n] for n in order], *[deltas[n] for n in order], *[new_m[n] for n in order],
            *[new_v[n] for n in order])
```
